```python
import jax, jax.numpy as jnp
from jax import lax
import numpy as np

D_MODEL = 1024
BATCH = 8
SEQ = 4096
DEPTH = 1

MIX_WIDTH = D_MODEL
POOL_WIDTH = MIX_WIDTH // 2
POOL_WINDOWS = (2, 4, 8, 16)
N_POOL_GROUPS = len(POOL_WINDOWS)
POOL_GROUP_DIM = POOL_WIDTH // N_POOL_GROUPS
ATTN_WIDTH = MIX_WIDTH - POOL_WIDTH
HEAD_DIM = 64
N_Q_HEADS = ATTN_WIDTH // HEAD_DIM
N_KV_HEADS = 2
GQA_GROUP = N_Q_HEADS // N_KV_HEADS
WINDOW = 128
BLOCK = 128
N_BUCKETS = 32
MAX_EXACT = N_BUCKETS // 2
MAX_DISTANCE = 128
IN_WIDTH = POOL_WIDTH + N_Q_HEADS * HEAD_DIM + 2 * N_KV_HEADS * HEAD_DIM
D_FF = -(-(8 * D_MODEL) // (3 * 256)) * 256
EPS = 1e-6
NEG_INF = -1e30

kernel_name = "hybrid_pool_swa_sink_t5bias_block"


def rmsnorm(x, g):
    xf = x.astype(jnp.float32)
    xf = xf * lax.rsqrt(jnp.mean(xf * xf, axis=-1, keepdims=True) + EPS)
    return xf.astype(x.dtype) * g


def multiscale_pool(u, w_pool, pool_scale):
    b, s, _ = u.shape
    ug = u.reshape(b, s, N_POOL_GROUPS, POOL_GROUP_DIM)
    csum = jnp.cumsum(ug.astype(jnp.float32), axis=1)
    pos = jnp.arange(1, s + 1, dtype=jnp.float32)
    pooled = []
    for g, w in enumerate(POOL_WINDOWS):
        c = csum[:, :, g]
        lagged = jnp.pad(c, ((0, 0), (w, 0), (0, 0)))[:, :s]
        count = jnp.minimum(pos, float(w))[None, :, None]
        pooled.append((c - lagged) / count)
    pooled = jnp.stack(pooled, axis=2).astype(u.dtype) - ug
    mixed = jnp.einsum('bsgc,gcd->bsgd', pooled, w_pool)
    return mixed.reshape(b, s, POOL_WIDTH) * pool_scale


def relative_bias_band(rel_bias):
    qi = jnp.arange(BLOCK)[:, None]
    kj = jnp.arange(2 * BLOCK)[None, :]
    dist = qi + BLOCK - kj
    n = jnp.maximum(dist, 0)
    nf = jnp.maximum(n, 1).astype(jnp.float32)
    large = MAX_EXACT + (jnp.log(nf / MAX_EXACT) / np.float32(np.log(MAX_DISTANCE / MAX_EXACT))
                         * (N_BUCKETS - MAX_EXACT)).astype(jnp.int32)
    large = jnp.minimum(large, N_BUCKETS - 1)
    bucket = jnp.where(n < MAX_EXACT, n, large)
    bias = rel_bias.astype(jnp.float32)[bucket]
    bias = jnp.transpose(bias, (2, 0, 1)).reshape(N_KV_HEADS, GQA_GROUP, BLOCK, 2 * BLOCK)
    in_window = (dist >= 0) & (dist < WINDOW)
    return bias, in_window


def sliding_window_attention(q, k, v, rel_bias, sinks):
    b, s = q.shape[:2]
    nb = s // BLOCK
    qb = q.reshape(b, nb, BLOCK, N_KV_HEADS, GQA_GROUP, HEAD_DIM)

    def band(t):
        prev = jnp.pad(t, ((0, 0), (BLOCK, 0), (0, 0), (0, 0)))[:, :s]
        prev = prev.reshape(b, nb, BLOCK, N_KV_HEADS, HEAD_DIM)
        cur = t.reshape(b, nb, BLOCK, N_KV_HEADS, HEAD_DIM)
        return jnp.concatenate([prev, cur], axis=2)

    kb, vb = band(k), band(v)
    bias, in_window = relative_bias_band(rel_bias)
    key_exists = (jnp.arange(nb)[:, None] > 0) | (jnp.arange(2 * BLOCK)[None, :] >= BLOCK)
    mask = in_window[None] & key_exists[:, None, :]
    scale = 1.0 / np.sqrt(HEAD_DIM).astype(np.float32)
    logits = jnp.einsum('bnqhgd,bnkhd->bnhgqk', qb, kb).astype(jnp.float32) * scale
    logits = logits + bias[None, None]
    logits = jnp.where(mask[None, :, None, None], logits, NEG_INF)
    sink = sinks.astype(jnp.float32).reshape(N_KV_HEADS, GQA_GROUP)[None, None, :, :, None, None]
    m = jnp.maximum(jnp.max(logits, axis=-1, keepdims=True), sink)
    p = jnp.exp(logits - m)
    denom = jnp.sum(p, axis=-1, keepdims=True) + jnp.exp(sink - m)
    probs = (p / denom).astype(v.dtype)
    out = jnp.einsum('bnhgqk,bnkhd->bnqhgd', probs, vb)
    return out.reshape(b, s, ATTN_WIDTH)


def _fwd_setup_inputs(seed: int = 0) -> dict:
    key = jax.random.key(seed)
    ks = jax.random.split(key, 16)
    f32 = jnp.float32

    def w(k, shape, fan_in):
        return jax.random.normal(k, shape, f32) * (fan_in ** -0.5)

    def gain(k, shape):
        return 1.0 + 0.05 * jax.random.normal(k, shape, f32)

    return {
        "x": jax.random.normal(ks[0], (BATCH, SEQ, D_MODEL), f32),
        "g_pre_mix": gain(ks[1], (DEPTH, D_MODEL)),
        "w_in": w(ks[2], (DEPTH, D_MODEL, IN_WIDTH), D_MODEL),
        "w_pool": w(ks[3], (DEPTH, N_POOL_GROUPS, POOL_GROUP_DIM, POOL_GROUP_DIM), POOL_GROUP_DIM),
        "pool_scale": 1.0 + 0.1 * jax.random.normal(ks[4], (DEPTH, POOL_WIDTH), f32),
        "rel_bias": 0.5 * jax.random.normal(ks[5], (N_BUCKETS, N_Q_HEADS), f32),
        "sinks": 0.5 * jax.random.normal(ks[6], (DEPTH, N_Q_HEADS), f32),
        "w_out": w(ks[7], (DEPTH, MIX_WIDTH, D_MODEL), MIX_WIDTH),
        "g_post_mix": gain(ks[8], (DEPTH, D_MODEL)),
        "g_pre_ffn": gain(ks[9], (DEPTH, D_MODEL)),
        "w_gate": w(ks[10], (DEPTH, D_MODEL, D_FF), D_MODEL),
        "w_up": w(ks[11], (DEPTH, D_MODEL, D_FF), D_MODEL),
        "w_down": w(ks[12], (DEPTH, D_FF, D_MODEL), D_FF),
        "g_post_ffn": gain(ks[13], (DEPTH, D_MODEL)),
    }


def _fwd_reference(x, g_pre_mix, w_in, w_pool, pool_scale, rel_bias, sinks, w_out,
              g_post_mix, g_pre_ffn, w_gate, w_up, w_down, g_post_ffn):
    b, s, _ = x.shape
    q_end = POOL_WIDTH + N_Q_HEADS * HEAD_DIM
    k_end = q_end + N_KV_HEADS * HEAD_DIM
    for l in range(DEPTH):
        h = rmsnorm(x, g_pre_mix[l])
        proj = h @ w_in[l]
        u = proj[..., :POOL_WIDTH]
        q = proj[..., POOL_WIDTH:q_end].reshape(b, s, N_Q_HEADS, HEAD_DIM)
        k = proj[..., q_end:k_end].reshape(b, s, N_KV_HEADS, HEAD_DIM)
        v = proj[..., k_end:].reshape(b, s, N_KV_HEADS, HEAD_DIM)
        pool_out = multiscale_pool(u, w_pool[l], pool_scale[l])
        attn_out = sliding_window_attention(q, k, v, rel_bias, sinks[l])
        mix = jnp.concatenate([pool_out, attn_out], axis=-1) @ w_out[l]
        x = x + rmsnorm(mix, g_post_mix[l])
        h = rmsnorm(x, g_pre_ffn[l])
        f = (jax.nn.silu(h @ w_gate[l]) * (h @ w_up[l])) @ w_down[l]
        x = x + rmsnorm(f, g_post_ffn[l])
    return x


import jax as _jax
import jax.numpy as _jnp

TWIN_FORMAT = 'train_step'
FWD_PARAMS = ['x', 'g_pre_mix', 'w_in', 'w_pool', 'pool_scale', 'rel_bias', 'sinks', 'w_out', 'g_post_mix', 'g_pre_ffn', 'w_gate', 'w_up', 'w_down', 'g_post_ffn']
TWIN_WEIGHTS = ['g_pre_mix', 'w_in', 'w_pool', 'pool_scale', 'rel_bias', 'sinks', 'w_out', 'g_post_mix', 'g_pre_ffn', 'w_gate', 'w_up', 'w_down', 'g_post_ffn']
TWIN_DIFF_INPUT = 'x'
TWIN_INPUTS = ['x', 'g_pre_mix', 'w_in', 'w_pool', 'pool_scale', 'rel_bias', 'sinks', 'w_out', 'g_post_mix', 'g_pre_ffn', 'w_gate', 'w_up', 'w_down', 'g_post_ffn', 'loss_target', 'm_g_pre_mix', 'm_w_in', 'm_w_pool', 'm_pool_scale', 'm_rel_bias', 'm_sinks', 'm_w_out', 'm_g_post_mix', 'm_g_pre_ffn', 'm_w_gate', 'm_w_up', 'm_w_down', 'm_g_post_ffn', 'v_g_pre_mix', 'v_w_in', 'v_w_pool', 'v_pool_scale', 'v_rel_bias', 'v_sinks', 'v_w_out', 'v_g_post_mix', 'v_g_pre_ffn', 'v_w_gate', 'v_w_up', 'v_w_down', 'v_g_post_ffn']
TWIN_OUTPUTS = ['loss', 'grad_x', 'grad_g_pre_mix', 'grad_w_in', 'grad_w_pool', 'grad_pool_scale', 'grad_rel_bias', 'grad_sinks', 'grad_w_out', 'grad_g_post_mix', 'grad_g_pre_ffn', 'grad_w_gate', 'grad_w_up', 'grad_w_down', 'grad_g_post_ffn', 'delta_g_pre_mix', 'delta_w_in', 'delta_w_pool', 'delta_pool_scale', 'delta_rel_bias', 'delta_sinks', 'delta_w_out', 'delta_g_post_mix', 'delta_g_pre_ffn', 'delta_w_gate', 'delta_w_up', 'delta_w_down', 'delta_g_post_ffn', 'new_m_g_pre_mix', 'new_m_w_in', 'new_m_w_pool', 'new_m_pool_scale', 'new_m_rel_bias', 'new_m_sinks', 'new_m_w_out', 'new_m_g_post_mix', 'new_m_g_pre_ffn', 'new_m_w_gate', 'new_m_w_up', 'new_m_w_down', 'new_m_g_post_ffn', 'new_v_g_pre_mix', 'new_v_w_in', 'new_v_w_pool', 'new_v_pool_scale', 'new_v_rel_bias', 'new_v_sinks', 'new_v_w_out', 'new_v_g_post_mix', 'new_v_g_pre_ffn', 'new_v_w_gate', 'new_v_w_up', 'new_v_w_down', 'new_v_g_post_ffn']
TWIN_LEAF_KINDS = {'loss': 'loss', 'grad_x': 'grad_x', 'grad_g_pre_mix': 'grad_w', 'grad_w_in': 'grad_w', 'grad_w_pool': 'grad_w', 'grad_pool_scale': 'grad_w', 'grad_rel_bias': 'grad_w', 'grad_sinks': 'grad_w', 'grad_w_out': 'grad_w', 'grad_g_post_mix': 'grad_w', 'grad_g_pre_ffn': 'grad_w', 'grad_w_gate': 'grad_w', 'grad_w_up': 'grad_w', 'grad_w_down': 'grad_w', 'grad_g_post_ffn': 'grad_w', 'delta_g_pre_mix': 'delta_w', 'delta_w_in': 'delta_w', 'delta_w_pool': 'delta_w', 'delta_pool_scale': 'delta_w', 'delta_rel_bias': 'delta_w', 'delta_sinks': 'delta_w', 'delta_w_out': 'delta_w', 'delta_g_post_mix': 'delta_w', 'delta_g_pre_ffn': 'delta_w', 'delta_w_gate': 'delta_w', 'delta_w_up': 'delta_w', 'delta_w_down': 'delta_w', 'delta_g_post_ffn': 'delta_w', 'new_m_g_pre_mix': 'new_m', 'new_m_w_in': 'new_m', 'new_m_w_pool': 'new_m', 'new_m_pool_scale': 'new_m', 'new_m_rel_bias': 'new_m', 'new_m_sinks': 'new_m', 'new_m_w_out': 'new_m', 'new_m_g_post_mix': 'new_m', 'new_m_g_pre_ffn': 'new_m', 'new_m_w_gate': 'new_m', 'new_m_w_up': 'new_m', 'new_m_w_down': 'new_m', 'new_m_g_post_ffn': 'new_m', 'new_v_g_pre_mix': 'new_v', 'new_v_w_in': 'new_v', 'new_v_w_pool': 'new_v', 'new_v_pool_scale': 'new_v', 'new_v_rel_bias': 'new_v', 'new_v_sinks': 'new_v', 'new_v_w_out': 'new_v', 'new_v_g_post_mix': 'new_v', 'new_v_g_pre_ffn': 'new_v', 'new_v_w_gate': 'new_v', 'new_v_w_up': 'new_v', 'new_v_w_down': 'new_v', 'new_v_g_post_ffn': 'new_v'}


def _forward(args):
    return _fwd_reference(*[args[k] for k in FWD_PARAMS])


def _output_shape():
    out = _jax.eval_shape(lambda: _forward(_fwd_setup_inputs(0)))
    return out.shape, out.dtype

N_MICROBATCH = 1
ADAM_LR = 0.001
ADAM_B1 = 0.9
ADAM_B2 = 0.999
ADAM_EPS = 1e-08
ADAM_WD = 0.01
ADAM_STEP = 10
PER_EXAMPLE_BATCH_AXIS = {'x': 0, 'loss_target': 0}
SHARED_INPUTS = []
_WEIGHT_DTYPES = {'g_pre_mix': _jnp.float32, 'w_in': _jnp.float32, 'w_pool': _jnp.float32, 'pool_scale': _jnp.float32, 'rel_bias': _jnp.float32, 'sinks': _jnp.float32, 'w_out': _jnp.float32, 'g_post_mix': _jnp.float32, 'g_pre_ffn': _jnp.float32, 'w_gate': _jnp.float32, 'w_up': _jnp.float32, 'w_down': _jnp.float32, 'g_post_ffn': _jnp.float32}
MOMENT_SCALE = {'g_pre_mix': 7.595392e-01, 'w_in': 6.490750e-01, 'w_pool': 1.140898e+00, 'pool_scale': 1.467805e+00, 'rel_bias': 1.675682e-01, 'sinks': 8.593109e-02, 'w_out': 8.728253e-01, 'g_post_mix': 3.212627e+01, 'g_pre_ffn': 6.164312e-01, 'w_gate': 1.948190e-01, 'w_up': 3.419367e-01, 'w_down': 5.700159e-01, 'g_post_ffn': 3.209296e+01}


def _to_microbatches(a, axis):
    t = _jnp.moveaxis(a, axis, 0)
    t = t.reshape((N_MICROBATCH, t.shape[0] // N_MICROBATCH) + t.shape[1:])
    return _jnp.moveaxis(t, 1, axis + 1)


def setup_inputs(seed: int = 0) -> dict:
    inp = _fwd_setup_inputs(seed)
    key = _jax.random.fold_in(_jax.random.key(seed), 7919)
    shape, _ = _output_shape()
    out = dict(inp)
    out["loss_target"] = _jax.random.normal(_jax.random.fold_in(key, 0), shape, _jnp.float32)
    for i, name in enumerate(TWIN_WEIGHTS):
        w = inp[name].astype(_jnp.float32)
        if MOMENT_SCALE is None:
            s = _jnp.sqrt(_jnp.mean(_jnp.square(w)) + 1e-30)
        else:
            s = MOMENT_SCALE[name]
        km, kv = _jax.random.split(_jax.random.fold_in(key, i + 1))
        out[name] = w
        out["m_" + name] = s * _jax.random.normal(km, w.shape, _jnp.float32)
        out["v_" + name] = (s * s) * _jax.random.uniform(kv, w.shape, _jnp.float32, 0.5, 1.5)
    if N_MICROBATCH > 1:
        for name, axis in PER_EXAMPLE_BATCH_AXIS.items():
            out[name] = _to_microbatches(out[name], axis)
    return {'x': out['x'], 'g_pre_mix': out['g_pre_mix'], 'w_in': out['w_in'], 'w_pool': out['w_pool'], 'pool_scale': out['pool_scale'], 'rel_bias': out['rel_bias'], 'sinks': out['sinks'], 'w_out': out['w_out'], 'g_post_mix': out['g_post_mix'], 'g_pre_ffn': out['g_pre_ffn'], 'w_gate': out['w_gate'], 'w_up': out['w_up'], 'w_down': out['w_down'], 'g_post_ffn': out['g_post_ffn'], 'loss_target': out['loss_target'], 'm_g_pre_mix': out['m_g_pre_mix'], 'm_w_in': out['m_w_in'], 'm_w_pool': out['m_w_pool'], 'm_pool_scale': out['m_pool_scale'], 'm_rel_bias': out['m_rel_bias'], 'm_sinks': out['m_sinks'], 'm_w_out': out['m_w_out'], 'm_g_post_mix': out['m_g_post_mix'], 'm_g_pre_ffn': out['m_g_pre_ffn'], 'm_w_gate': out['m_w_gate'], 'm_w_up': out['m_w_up'], 'm_w_down': out['m_w_down'], 'm_g_post_ffn': out['m_g_post_ffn'], 'v_g_pre_mix': out['v_g_pre_mix'], 'v_w_in': out['v_w_in'], 'v_w_pool': out['v_w_pool'], 'v_pool_scale': out['v_pool_scale'], 'v_rel_bias': out['v_rel_bias'], 'v_sinks': out['v_sinks'], 'v_w_out': out['v_w_out'], 'v_g_post_mix': out['v_g_post_mix'], 'v_g_pre_ffn': out['v_g_pre_ffn'], 'v_w_gate': out['v_w_gate'], 'v_w_up': out['v_w_up'], 'v_w_down': out['v_w_down'], 'v_g_post_ffn': out['v_g_post_ffn']}


def _loss(weights, diff, rest, loss_target):
    with _jax.named_scope("forward"):
        args = {**rest, TWIN_DIFF_INPUT: diff, **{k: w.astype(_WEIGHT_DTYPES[k]) for k, w in weights.items()}}
        y = _forward(args)
    with _jax.named_scope("loss_head"):
        err = _jnp.square(y.astype(_jnp.float32) - loss_target)
        return 0.5 * _jnp.sum(_jnp.mean(err, axis=-1)) if err.ndim else 0.5 * err


def _adamw(w, g, m, v):
    m = ADAM_B1 * m + (1.0 - ADAM_B1) * g
    v = ADAM_B2 * v + (1.0 - ADAM_B2) * _jnp.square(g)
    m_hat = m / (1.0 - ADAM_B1 ** ADAM_STEP)
    v_hat = v / (1.0 - ADAM_B2 ** ADAM_STEP)
    delta = -ADAM_LR * (m_hat / (_jnp.sqrt(v_hat) + ADAM_EPS) + ADAM_WD * w)
    return delta, m, v


def reference(x, g_pre_mix, w_in, w_pool, pool_scale, rel_bias, sinks, w_out, g_post_mix, g_pre_ffn, w_gate, w_up, w_down, g_post_ffn, loss_target, m_g_pre_mix, m_w_in, m_w_pool, m_pool_scale, m_rel_bias, m_sinks, m_w_out, m_g_post_mix, m_g_pre_ffn, m_w_gate, m_w_up, m_w_down, m_g_post_ffn, v_g_pre_mix, v_w_in, v_w_pool, v_pool_scale, v_rel_bias, v_sinks, v_w_out, v_g_post_mix, v_g_pre_ffn, v_w_gate, v_w_up, v_w_down, v_g_post_ffn):
    given = dict(x=x, g_pre_mix=g_pre_mix, w_in=w_in, w_pool=w_pool, pool_scale=pool_scale, rel_bias=rel_bias, sinks=sinks, w_out=w_out, g_post_mix=g_post_mix, g_pre_ffn=g_pre_ffn, w_gate=w_gate, w_up=w_up, w_down=w_down, g_post_ffn=g_post_ffn, loss_target=loss_target, m_g_pre_mix=m_g_pre_mix, m_w_in=m_w_in, m_w_pool=m_w_pool, m_pool_scale=m_pool_scale, m_rel_bias=m_rel_bias, m_sinks=m_sinks, m_w_out=m_w_out, m_g_post_mix=m_g_post_mix, m_g_pre_ffn=m_g_pre_ffn, m_w_gate=m_w_gate, m_w_up=m_w_up, m_w_down=m_w_down, m_g_post_ffn=m_g_post_ffn, v_g_pre_mix=v_g_pre_mix, v_w_in=v_w_in, v_w_pool=v_w_pool, v_pool_scale=v_pool_scale, v_rel_bias=v_rel_bias, v_sinks=v_sinks, v_w_out=v_w_out, v_g_post_mix=v_g_post_mix, v_g_pre_ffn=v_g_pre_ffn, v_w_gate=v_w_gate, v_w_up=v_w_up, v_w_down=v_w_down, v_g_post_ffn=v_g_post_ffn)
    weights = {n: given[n] for n in TWIN_WEIGHTS}
    shared = {n: given[n] for n in SHARED_INPUTS}
    per_example = {n: given[n] for n in ['x']}
    grad_fn = _jax.value_and_grad(_loss, argnums=(0, 1))

    def one_microbatch(ex, loss_target):
        ex = dict(ex)
        diff = ex.pop(TWIN_DIFF_INPUT)
        return grad_fn(weights, diff, {**shared, **ex}, loss_target)

    if N_MICROBATCH == 1:
        loss, (grad_w, grad_x) = one_microbatch(per_example, given["loss_target"])
    else:
        def body(carry, xs):
            loss_sum, grad_sum = carry
            l_k, (gw_k, gx_k) = one_microbatch(xs[0], xs[1])
            with _jax.named_scope("update"):
                return (loss_sum + l_k, _jax.tree.map(_jnp.add, grad_sum, gw_k)), gx_k

        init = (_jnp.zeros((), _jnp.float32), _jax.tree.map(_jnp.zeros_like, weights))
        (loss, grad_w), grad_x = _jax.lax.scan(body, init, (per_example, given["loss_target"]))
    with _jax.named_scope("update"):
        delta_w, new_m, new_v = {}, {}, {}
        for n in TWIN_WEIGHTS:
            delta_w[n], new_m[n], new_v[n] = _adamw(weights[n], grad_w[n], given["m_" + n], given["v_" + n])
    return (loss, grad_x, *[grad_w[n] for n in TWIN_WEIGHTS], *[delta_w[n] for n in TWIN_WEIGHTS],
            *[new_m[n] for n in TWIN_WEIGHTS], *[new_v[n] for n in TWIN_WEIGHTS])
```

```python
import functools

import numpy as np
import jax
import jax.numpy as jnp
from jax import lax
from jax.experimental import pallas as pl
from jax.experimental.pallas import tpu as pltpu

F32 = jnp.float32
BF16 = jnp.bfloat16

D = 1024
POOL_W = 512
GROUP = 128
WINDOWS = (2, 4, 8, 16)
HALO = 128
NQ = 8
BLK = 128
NBUCKET = 32
IN_W = 1280
DFF = 2816
NSH = 4
FS = DFF // NSH
WIN_S = IN_W // NSH
WOUT_S = D // NSH
EPS = 1e-6
NEG = -1e30
SCALE = 0.125

ADAM_LR = 0.001
ADAM_B1 = 0.9
ADAM_B2 = 0.999
ADAM_EPS = 1e-08
ADAM_WD = 0.01
ADAM_STEP = 10

TM = 512
TM_POOL = 256
TM_FFN = 512
VMEM_LIMIT = 56 * 1024 * 1024

MESH_T = pl.DeviceIdType.MESH
ANY = pl.BlockSpec(memory_space=pl.ANY)
SHARD_SHAPES = ((D, WIN_S), (WOUT_S, D), (D, FS), (D, FS), (FS, D))


def _cp(n_grid=0, **kw):
    sem = ("arbitrary",) * n_grid if n_grid else None
    return pltpu.CompilerParams(dimension_semantics=sem, vmem_limit_bytes=VMEM_LIMIT, **kw)


def _dot(a, b):
    return jnp.dot(a, b, preferred_element_type=F32)


def _dot_nt(a, b):
    return lax.dot_general(a, b, (((1,), (1,)), ((), ())), preferred_element_type=F32)


def _dot_tn(a, b):
    return lax.dot_general(a, b, (((0,), (0,)), ((), ())), preferred_element_type=F32)


def _rms(x):
    r = lax.rsqrt(jnp.mean(x * x, axis=-1, keepdims=True) + EPS)
    return r, x * r


def _rms_bwd(r, n, g, dout):
    dn = dout * g
    dx = r * (dn - n * jnp.mean(dn * n, axis=-1, keepdims=True))
    dg = jnp.sum(dout * n, axis=0, keepdims=True)
    return dx, dg


def _split(x, n):
    parts = []
    r = x
    for _ in range(n):
        p = r.astype(BF16)
        parts.append(p)
        r = r - p.astype(F32)
    return parts


def _band_dot(a, x, n):
    acc = None
    for p in _split(x, n):
        t = _dot(a, p)
        acc = t if acc is None else acc + t
    return acc


def _bucket_table():
    qi = np.arange(BLK)[:, None]
    kj = np.arange(2 * BLK)[None, :]
    dist = qi + BLK - kj
    n = np.maximum(dist, 0)
    nf = np.maximum(n, 1).astype(np.float32)
    large = 16 + (np.log(nf / np.float32(16)) / np.float32(np.log(128 / 16)) * np.float32(16)).astype(np.int32)
    large = np.minimum(large, NBUCKET - 1)
    return np.where(n < 16, n, large).astype(np.int32)


def _inproj_fwd(x, g1, w_in):
    s = x.shape[0]
    tm = min(TM, s)

    def body(x_ref, g_ref, w_ref, u_ref, q_ref, k_ref, v_ref):
        _, n = _rms(x_ref[...])
        h = (n * g_ref[...]).astype(BF16)
        proj = _dot(h, w_ref[...])
        u_ref[...] = proj[:, :512]
        q_ref[...] = proj[:, 512:1024].astype(BF16)
        k_ref[...] = proj[:, 1024:1152].astype(BF16)
        v_ref[...] = proj[:, 1152:1280].astype(BF16)

    row = lambda w: pl.BlockSpec((tm, w), lambda i: (i, 0))
    return pl.pallas_call(
        body, name="inproj_fwd", grid=(s // tm,),
        in_specs=[row(D), pl.BlockSpec((1, D), lambda i: (0, 0)), pl.BlockSpec((D, IN_W), lambda i: (0, 0))],
        out_specs=[row(512), row(512), row(128), row(128)],
        out_shape=[jax.ShapeDtypeStruct((s, 512), F32), jax.ShapeDtypeStruct((s, 512), BF16),
                   jax.ShapeDtypeStruct((s, 128), BF16), jax.ShapeDtypeStruct((s, 128), BF16)],
        compiler_params=_cp(1))(x, g1, w_in)


def _pooled(ext, cur, t0, tm):
    rows = lax.broadcasted_iota(jnp.int32, (tm, tm + HALO), 0)
    cols = lax.broadcasted_iota(jnp.int32, (tm, tm + HALO), 1)
    d = rows + HALO - cols
    t = t0 + lax.broadcasted_iota(jnp.int32, (tm, GROUP), 0)
    out = []
    for g, w in enumerate(WINDOWS):
        a = jnp.where((d >= 0) & (d < w), 1.0, 0.0).astype(BF16)
        ws = _band_dot(a, ext[:, g * GROUP:(g + 1) * GROUP], 3)
        cnt = jnp.minimum(t + 1, w).astype(F32)
        out.append(ws / cnt - cur[:, g * GROUP:(g + 1) * GROUP])
    return out


def _pool_fwd(u, w_pool, pool_scale):
    s = u.shape[0]
    tm = min(TM_POOL, s)
    hb = tm // HALO

    def body(uc_ref, uh_ref, wp_ref, sc_ref, o_ref):
        i = pl.program_id(0)
        cur = uc_ref[...]
        halo = jnp.where(i > 0, uh_ref[...], 0.0)
        ext = jnp.concatenate([halo, cur], axis=0)
        pooled = _pooled(ext, cur, i * tm, tm)
        for g in range(4):
            sl = slice(g * GROUP, (g + 1) * GROUP)
            mixed = _dot(pooled[g].astype(BF16), wp_ref[g])
            o_ref[:, sl] = (mixed * sc_ref[:, sl]).astype(BF16)

    return pl.pallas_call(
        body, name="pool_fwd", grid=(s // tm,),
        in_specs=[pl.BlockSpec((tm, 512), lambda i: (i, 0)),
                  pl.BlockSpec((HALO, 512), lambda i: (jnp.maximum(i * hb - 1, 0), 0)),
                  pl.BlockSpec((4, GROUP, GROUP), lambda i: (0, 0, 0)),
                  pl.BlockSpec((1, 512), lambda i: (0, 0))],
        out_specs=pl.BlockSpec((tm, 512), lambda i: (i, 0)),
        out_shape=jax.ShapeDtypeStruct((s, 512), BF16),
        compiler_params=_cp(1))(u, u, w_pool, pool_scale)


def _bias_table(bucket, rel_bias):
    def body(b_ref, rb_ref, o_ref):
        bucket_v = b_ref[...]
        rows = lax.broadcasted_iota(jnp.int32, (BLK, 2 * BLK), 0)
        cols = lax.broadcasted_iota(jnp.int32, (BLK, 2 * BLK), 1)
        dist = rows + BLK - cols
        inwin = (dist >= 0) & (dist < BLK)
        for h in range(NQ):
            acc = jnp.zeros((BLK, 2 * BLK), F32)
            for b in range(NBUCKET):
                acc = jnp.where(bucket_v == b, rb_ref[b, h], acc)
            rest = jnp.where(inwin, acc, NEG)
            o_ref[1, h] = rest
            o_ref[0, h] = jnp.where(cols >= BLK, rest, NEG)

    return pl.pallas_call(
        body, name="bias_table",
        in_specs=[pl.BlockSpec(memory_space=pltpu.VMEM), pl.BlockSpec(memory_space=pltpu.SMEM)],
        out_specs=pl.BlockSpec(memory_space=pltpu.VMEM),
        out_shape=jax.ShapeDtypeStruct((2, NQ, BLK, 2 * BLK), F32),
        compiler_params=_cp())(bucket, rel_bias)


def _kv_bands(k_ref, v_ref, n):
    pstart = pl.multiple_of(jnp.maximum(n - 1, 0) * BLK, BLK)
    cstart = pl.multiple_of(n * BLK, BLK)
    lo = lax.broadcasted_iota(jnp.int32, (2 * BLK, 128), 1) < 64
    out = []
    for ref in (k_ref, v_ref):
        band = jnp.concatenate([ref[pl.ds(pstart, BLK), :], ref[pl.ds(cstart, BLK), :]], axis=0).astype(F32)
        rot = pltpu.roll(band, 64, axis=1)
        out.append((jnp.where(lo, band, rot).astype(BF16), jnp.where(lo, rot, band).astype(BF16)))
    return out[0], out[1], lo, pstart, cstart


def _attn_probs(qm, kk, bias, sink):
    s = _dot_nt(qm, kk) + bias
    m = jnp.maximum(jnp.max(s, axis=-1, keepdims=True), sink)
    p = jnp.exp(s - m)
    es = jnp.exp(sink - m)
    inv = 1.0 / (jnp.sum(p, axis=-1, keepdims=True) + es)
    return p * inv, es * inv


def _attn_fwd(q, k, v, bias, sinks):
    s = q.shape[0]

    def body(q_ref, k_ref, v_ref, b_ref, sk_ref, o_ref):
        n = pl.program_id(0)
        kk, vv, lo, _, _ = _kv_bands(k_ref, v_ref, n)
        bidx = jnp.minimum(n, 1)
        lo_q = lax.broadcasted_iota(jnp.int32, (BLK, 128), 1) < 64
        for p in range(4):
            h = p // 2
            qt = q_ref[:, p * 128:(p + 1) * 128].astype(F32) * SCALE
            acc = jnp.zeros((BLK, 128), F32)
            for e in range(2):
                head = 2 * p + e
                sel_q = lo_q if e == 0 else jnp.logical_not(lo_q)
                sel_kv = lo if e == 0 else jnp.logical_not(lo)
                qm = jnp.where(sel_q, qt, 0.0).astype(BF16)
                prob, _ = _attn_probs(qm, kk[h], b_ref[bidx, head], sk_ref[0, head])
                vm = jnp.where(sel_kv, vv[h], jnp.zeros_like(vv[h]))
                acc = acc + _dot(prob.astype(BF16), vm)
            o_ref[:, p * 128:(p + 1) * 128] = acc.astype(BF16)

    return pl.pallas_call(
        body, name="attn_fwd", grid=(s // BLK,),
        in_specs=[pl.BlockSpec((BLK, 512), lambda i: (i, 0)),
                  pl.BlockSpec((s, 128), lambda i: (0, 0)),
                  pl.BlockSpec((s, 128), lambda i: (0, 0)),
                  pl.BlockSpec((2, NQ, BLK, 2 * BLK), lambda i: (0, 0, 0, 0)),
                  pl.BlockSpec(memory_space=pltpu.SMEM)],
        out_specs=pl.BlockSpec((BLK, 512), lambda i: (i, 0)),
        out_shape=jax.ShapeDtypeStruct((s, 512), BF16),
        compiler_params=_cp(1))(q, k, v, bias, sinks)


def _outproj_fwd(pool_o, attn_o, w_out, x, g2, g3):
    s = x.shape[0]
    tm = min(TM, s)

    def body(p_ref, a_ref, w_ref, x_ref, g2_ref, g3_ref, mix_ref, x1_ref, h2_ref):
        mix = _dot(p_ref[...], w_ref[0:512, :]) + _dot(a_ref[...], w_ref[512:1024, :])
        mix_ref[...] = mix
        _, n2 = _rms(mix)
        x1 = x_ref[...] + n2 * g2_ref[...]
        x1_ref[...] = x1
        _, n3 = _rms(x1)
        h2_ref[...] = (n3 * g3_ref[...]).astype(BF16)

    row = lambda w: pl.BlockSpec((tm, w), lambda i: (i, 0))
    vec = pl.BlockSpec((1, D), lambda i: (0, 0))
    return pl.pallas_call(
        body, name="outproj_fwd", grid=(s // tm,),
        in_specs=[row(512), row(512), pl.BlockSpec((D, D), lambda i: (0, 0)), row(D), vec, vec],
        out_specs=[row(D), row(D), row(D)],
        out_shape=[jax.ShapeDtypeStruct((s, D), F32), jax.ShapeDtypeStruct((s, D), F32),
                   jax.ShapeDtypeStruct((s, D), BF16)],
        compiler_params=_cp(1))(pool_o, attn_o, w_out, x, g2, g3)


def _silu_parts(g):
    sg = jax.nn.sigmoid(g)
    return sg, g * sg


def _ffn_fwd(h2, wg, wu, wd, x1, target, g4):
    s = h2.shape[0]
    tm = min(TM_FFN, s)

    def body(h_ref, wg_ref, wu_ref, wd_ref, x1_ref, t_ref, g4_ref,
             gate_ref, up_ref, df_ref, dy_ref, loss_ref, gg4_ref, f_acc):
        i = pl.program_id(0)
        j = pl.program_id(1)
        h = h_ref[...]
        gate = _dot(h, wg_ref[...])
        up = _dot(h, wu_ref[...])
        gate_ref[...] = gate.astype(BF16)
        up_ref[...] = up.astype(BF16)
        _, sl = _silu_parts(gate)
        contrib = _dot((sl * up).astype(BF16), wd_ref[...])

        @pl.when(j == 0)
        def _():
            f_acc[...] = contrib

        @pl.when(j > 0)
        def _():
            f_acc[...] += contrib

        @pl.when((i == 0) & (j == 0))
        def _():
            loss_ref[...] = jnp.zeros_like(loss_ref)
            gg4_ref[...] = jnp.zeros_like(gg4_ref)

        @pl.when(j == NSH - 1)
        def _():
            r4, n4 = _rms(f_acc[...])
            g4v = g4_ref[...]
            err = x1_ref[...] + n4 * g4v - t_ref[...]
            loss_ref[...] += (0.5 / D) * jnp.sum(err * err).reshape(1, 1)
            dy = err * (1.0 / D)
            dy_ref[...] = dy
            df, dg = _rms_bwd(r4, n4, g4v, dy)
            gg4_ref[...] += dg
            df_ref[...] = df.astype(BF16)

    row = lambda w: pl.BlockSpec((tm, w), lambda i, j: (i, 0))
    sh = lambda r, c: pl.BlockSpec((None, r, c), lambda i, j: (j, 0, 0))
    act = pl.BlockSpec((None, tm, FS), lambda i, j: (j, i, 0))
    vec = pl.BlockSpec((1, D), lambda i, j: (0, 0))
    return pl.pallas_call(
        body, name="ffn_fwd", grid=(s // tm, NSH),
        in_specs=[row(D), sh(D, FS), sh(D, FS), sh(FS, D), row(D), row(D), vec],
        out_specs=[act, act, row(D), row(D), pl.BlockSpec((1, 1), lambda i, j: (0, 0)), vec],
        out_shape=[jax.ShapeDtypeStruct((NSH, s, FS), BF16), jax.ShapeDtypeStruct((NSH, s, FS), BF16),
                   jax.ShapeDtypeStruct((s, D), BF16), jax.ShapeDtypeStruct((s, D), F32),
                   jax.ShapeDtypeStruct((1, 1), F32), jax.ShapeDtypeStruct((1, D), F32)],
        scratch_shapes=[pltpu.VMEM((tm, D), F32)],
        compiler_params=_cp(2))(h2, wg, wu, wd, x1, target, g4)


def _ffn_bwd_act(df, gate, up, wg, wu, wd, dy, x1, mix, g3, g2):
    s = df.shape[0]
    tm = min(TM_FFN, s)

    def body(df_ref, gate_ref, up_ref, wg_ref, wu_ref, wd_ref, dy_ref, x1_ref, mix_ref, g3_ref, g2_ref,
             dgate_ref, dup_ref, dx1_ref, dmix_ref, gg3_ref, gg2_ref, acc):
        i = pl.program_id(0)
        j = pl.program_id(1)
        da = _dot_nt(df_ref[...], wd_ref[...])
        g = gate_ref[...].astype(F32)
        u = up_ref[...].astype(F32)
        sg, sl = _silu_parts(g)
        dgate = (da * u * (sg * (1.0 + g * (1.0 - sg)))).astype(BF16)
        dup = (da * sl).astype(BF16)
        dgate_ref[...] = dgate
        dup_ref[...] = dup
        contrib = _dot_nt(dgate, wg_ref[...]) + _dot_nt(dup, wu_ref[...])

        @pl.when(j == 0)
        def _():
            acc[...] = contrib

        @pl.when(j > 0)
        def _():
            acc[...] += contrib

        @pl.when((i == 0) & (j == 0))
        def _():
            gg3_ref[...] = jnp.zeros_like(gg3_ref)
            gg2_ref[...] = jnp.zeros_like(gg2_ref)

        @pl.when(j == NSH - 1)
        def _():
            r3, n3 = _rms(x1_ref[...])
            dx1n, dg3 = _rms_bwd(r3, n3, g3_ref[...], acc[...])
            dx1 = dy_ref[...] + dx1n
            dx1_ref[...] = dx1
            gg3_ref[...] += dg3
            r2, n2 = _rms(mix_ref[...])
            dmix, dg2 = _rms_bwd(r2, n2, g2_ref[...], dx1)
            gg2_ref[...] += dg2
            dmix_ref[...] = dmix.astype(BF16)

    row = lambda w: pl.BlockSpec((tm, w), lambda i, j: (i, 0))
    sh = lambda r, c: pl.BlockSpec((None, r, c), lambda i, j: (j, 0, 0))
    act = pl.BlockSpec((None, tm, FS), lambda i, j: (j, i, 0))
    vec = pl.BlockSpec((1, D), lambda i, j: (0, 0))
    return pl.pallas_call(
        body, name="ffn_bwd_act", grid=(s // tm, NSH),
        in_specs=[row(D), act, act, sh(D, FS), sh(D, FS), sh(FS, D), row(D), row(D), row(D), vec, vec],
        out_specs=[act, act, row(D), row(D), vec, vec],
        out_shape=[jax.ShapeDtypeStruct((NSH, s, FS), BF16), jax.ShapeDtypeStruct((NSH, s, FS), BF16),
                   jax.ShapeDtypeStruct((s, D), F32), jax.ShapeDtypeStruct((s, D), BF16),
                   jax.ShapeDtypeStruct((1, D), F32), jax.ShapeDtypeStruct((1, D), F32)],
        scratch_shapes=[pltpu.VMEM((tm, D), F32)],
        compiler_params=_cp(2))(df, gate, up, wg, wu, wd, dy, x1, mix, g3, g2)


def _ffn_bwd_w(h2, gate, up, dgate, dup, df):
    s = h2.shape[0]
    tm = min(TM_FFN, s)

    def body(h_ref, gate_ref, up_ref, dgate_ref, dup_ref, df_ref, gwg_ref, gwu_ref, gwd_ref):
        i = pl.program_id(1)

        @pl.when(i == 0)
        def _():
            gwg_ref[...] = jnp.zeros_like(gwg_ref)
            gwu_ref[...] = jnp.zeros_like(gwu_ref)
            gwd_ref[...] = jnp.zeros_like(gwd_ref)

        h = h_ref[...]
        gwg_ref[...] += _dot_tn(h, dgate_ref[...])
        gwu_ref[...] += _dot_tn(h, dup_ref[...])
        _, sl = _silu_parts(gate_ref[...].astype(F32))
        a = (sl * up_ref[...].astype(F32)).astype(BF16)
        gwd_ref[...] += _dot_tn(a, df_ref[...])

    row = lambda w: pl.BlockSpec((tm, w), lambda j, i: (i, 0))
    act = pl.BlockSpec((None, tm, FS), lambda j, i: (j, i, 0))
    sh = lambda r, c: pl.BlockSpec((None, r, c), lambda j, i: (j, 0, 0))
    return pl.pallas_call(
        body, name="ffn_bwd_w", grid=(NSH, s // tm),
        in_specs=[row(D), act, act, act, act, row(D)],
        out_specs=[sh(D, FS), sh(D, FS), sh(FS, D)],
        out_shape=[jax.ShapeDtypeStruct((NSH, D, FS), F32), jax.ShapeDtypeStruct((NSH, D, FS), F32),
                   jax.ShapeDtypeStruct((NSH, FS, D), F32)],
        compiler_params=_cp(2))(h2, gate, up, dgate, dup, df)


def _outproj_bwd(dmix, w_out, pool_o, attn_o):
    s = dmix.shape[0]
    tm = min(TM, s)

    def body(dm_ref, w_ref, p_ref, a_ref, dpool_ref, dattn_ref, gw_ref):
        i = pl.program_id(0)

        @pl.when(i == 0)
        def _():
            gw_ref[...] = jnp.zeros_like(gw_ref)

        dm = dm_ref[...]
        dpool_ref[...] = _dot_nt(dm, w_ref[0:512, :])
        dattn_ref[...] = _dot_nt(dm, w_ref[512:1024, :]).astype(BF16)
        gw_ref[0:512, :] += _dot_tn(p_ref[...], dm)
        gw_ref[512:1024, :] += _dot_tn(a_ref[...], dm)

    row = lambda w: pl.BlockSpec((tm, w), lambda i: (i, 0))
    full = pl.BlockSpec((D, D), lambda i: (0, 0))
    return pl.pallas_call(
        body, name="outproj_bwd", grid=(s // tm,),
        in_specs=[row(D), full, row(512), row(512)],
        out_specs=[row(512), row(512), full],
        out_shape=[jax.ShapeDtypeStruct((s, 512), F32), jax.ShapeDtypeStruct((s, 512), BF16),
                   jax.ShapeDtypeStruct((D, D), F32)],
        compiler_params=_cp(1))(dmix, w_out, pool_o, attn_o)


def _pool_bwd(u, dpool, w_pool, pool_scale):
    s = u.shape[0]
    tm = min(TM_POOL, s)
    hb = tm // HALO
    nt = s // tm
    last_halo = s // HALO - 1

    def body(uc_ref, uh_ref, dc_ref, dn_ref, wp_ref, sc_ref, du_ref, gwp_ref, gsc_ref):
        i = pl.program_id(0)

        @pl.when(i == 0)
        def _():
            gwp_ref[...] = jnp.zeros_like(gwp_ref)
            gsc_ref[...] = jnp.zeros_like(gsc_ref)

        cur = uc_ref[...]
        halo = jnp.where(i > 0, uh_ref[...], 0.0)
        pooled = _pooled(jnp.concatenate([halo, cur], axis=0), cur, i * tm, tm)
        dcur = dc_ref[...]
        dnext = jnp.where(i < nt - 1, dn_ref[...], 0.0)
        dext = jnp.concatenate([dcur, dnext], axis=0)
        rows = lax.broadcasted_iota(jnp.int32, (tm, tm + HALO), 0)
        cols = lax.broadcasted_iota(jnp.int32, (tm, tm + HALO), 1)
        d = cols - rows
        t_ext = i * tm + lax.broadcasted_iota(jnp.int32, (tm + HALO, GROUP), 0)
        for g, w in enumerate(WINDOWS):
            sl = slice(g * GROUP, (g + 1) * GROUP)
            pb = pooled[g].astype(BF16)
            wp = wp_ref[g]
            mixed = _dot(pb, wp)
            gsc_ref[:, sl] += jnp.sum(dcur[:, sl] * mixed, axis=0, keepdims=True)
            dmixed = (dext[:, sl] * sc_ref[:, sl]).astype(BF16)
            gwp_ref[g] += _dot_tn(pb, dmixed[0:tm])
            dpooled = _dot_nt(dmixed, wp)
            z = dpooled / jnp.minimum(t_ext + 1, w).astype(F32)
            b = jnp.where((d >= 0) & (d < w), 1.0, 0.0).astype(BF16)
            du_ref[:, sl] = (_band_dot(b, z, 2) - dpooled[0:tm]).astype(BF16)

    return pl.pallas_call(
        body, name="pool_bwd", grid=(nt,),
        in_specs=[pl.BlockSpec((tm, 512), lambda i: (i, 0)),
                  pl.BlockSpec((HALO, 512), lambda i: (jnp.maximum(i * hb - 1, 0), 0)),
                  pl.BlockSpec((tm, 512), lambda i: (i, 0)),
                  pl.BlockSpec((HALO, 512), lambda i: (jnp.minimum((i + 1) * hb, last_halo), 0)),
                  pl.BlockSpec((4, GROUP, GROUP), lambda i: (0, 0, 0)),
                  pl.BlockSpec((1, 512), lambda i: (0, 0))],
        out_specs=[pl.BlockSpec((tm, 512), lambda i: (i, 0)),
                   pl.BlockSpec((4, GROUP, GROUP), lambda i: (0, 0, 0)),
                   pl.BlockSpec((1, 512), lambda i: (0, 0))],
        out_shape=[jax.ShapeDtypeStruct((s, 512), BF16), jax.ShapeDtypeStruct((4, GROUP, GROUP), F32),
                   jax.ShapeDtypeStruct((1, 512), F32)],
        compiler_params=_cp(1))(u, u, dpool, dpool, w_pool, pool_scale)


def _attn_bwd(q, k, v, do, bias, sinks):
    s = q.shape[0]

    def body(q_ref, do_ref, k_ref, v_ref, b_ref, sk_ref, dq_ref, dk_ref, dv_ref, dss_ref, gsk_ref):
        n = pl.program_id(0)

        @pl.when(n == 0)
        def _():
            dk_ref[...] = jnp.zeros_like(dk_ref)
            dv_ref[...] = jnp.zeros_like(dv_ref)
            dss_ref[...] = jnp.zeros_like(dss_ref)
            gsk_ref[...] = jnp.zeros_like(gsk_ref)

        kk, vv, lo, pstart, cstart = _kv_bands(k_ref, v_ref, n)
        bidx = jnp.minimum(n, 1)
        lo_q = lax.broadcasted_iota(jnp.int32, (BLK, 128), 1) < 64
        dkk = [jnp.zeros((2 * BLK, 128), F32), jnp.zeros((2 * BLK, 128), F32)]
        dvv = [jnp.zeros((2 * BLK, 128), F32), jnp.zeros((2 * BLK, 128), F32)]
        for p in range(4):
            h = p // 2
            qt = q_ref[:, p * 128:(p + 1) * 128].astype(F32) * SCALE
            dot = do_ref[:, p * 128:(p + 1) * 128]
            dq = jnp.zeros((BLK, 128), F32)
            for e in range(2):
                head = 2 * p + e
                sel_q = lo_q if e == 0 else jnp.logical_not(lo_q)
                sel_kv = lo if e == 0 else jnp.logical_not(lo)
                qm = jnp.where(sel_q, qt, 0.0).astype(BF16)
                prob, ps = _attn_probs(qm, kk[h], b_ref[bidx, head], sk_ref[0, head])
                dom = jnp.where(sel_q, dot, jnp.zeros_like(dot))
                dp = _dot_nt(dom, vv[h])
                delta = jnp.sum(prob * dp, axis=-1, keepdims=True)
                ds = prob * (dp - delta)
                gsk_ref[pl.ds(head, 1), :] += jnp.broadcast_to(-jnp.sum(ps * delta).reshape(1, 1), (1, 128))
                dss_ref[head] += ds
                dsb = ds.astype(BF16)
                km = jnp.where(sel_kv, kk[h], jnp.zeros_like(kk[h]))
                dq = dq + _dot(dsb, km)
                dkk[h] = dkk[h] + _dot_tn(dsb, qm)
                dvv[h] = dvv[h] + _dot_tn(prob.astype(BF16), dom)
            dq_ref[:, p * 128:(p + 1) * 128] = (dq * SCALE).astype(BF16)
        for acc, ref in ((dkk, dk_ref), (dvv, dv_ref)):
            f0 = acc[0] + pltpu.roll(acc[0], 64, axis=1)
            f1 = acc[1] + pltpu.roll(acc[1], 64, axis=1)
            band = jnp.where(lo, f0, f1)
            ref[pl.ds(pstart, BLK), :] += band[0:BLK]
            ref[pl.ds(cstart, BLK), :] += band[BLK:2 * BLK]

    blk = pl.BlockSpec((BLK, 512), lambda i: (i, 0))
    kv = pl.BlockSpec((s, 128), lambda i: (0, 0))
    return pl.pallas_call(
        body, name="attn_bwd", grid=(s // BLK,),
        in_specs=[blk, blk, kv, kv, pl.BlockSpec((2, NQ, BLK, 2 * BLK), lambda i: (0, 0, 0, 0)),
                  pl.BlockSpec(memory_space=pltpu.SMEM)],
        out_specs=[blk, kv, kv, pl.BlockSpec((NQ, BLK, 2 * BLK), lambda i: (0, 0, 0)),
                   pl.BlockSpec((NQ, 128), lambda i: (0, 0))],
        out_shape=[jax.ShapeDtypeStruct((s, 512), BF16), jax.ShapeDtypeStruct((s, 128), F32),
                   jax.ShapeDtypeStruct((s, 128), F32), jax.ShapeDtypeStruct((NQ, BLK, 2 * BLK), F32),
                   jax.ShapeDtypeStruct((NQ, 128), F32)],
        compiler_params=_cp(1))(q, do, k, v, bias, sinks)


def _relbias_grad(dss, bucket):
    def body(ds_ref, b_ref, o_ref):
        bucket_v = b_ref[...]
        r = lax.broadcasted_iota(jnp.int32, (NBUCKET, 128), 0)
        c = lax.broadcasted_iota(jnp.int32, (NBUCKET, 128), 1)
        acc = jnp.zeros((NBUCKET, 128), F32)
        for h in range(NQ):
            ds = ds_ref[h]

            def step(b, a):
                sv = jnp.sum(jnp.where(bucket_v == b, ds, 0.0))
                return a + jnp.where((r == b) & (c == h), sv, 0.0)

            acc = lax.fori_loop(0, NBUCKET, step, acc)
        o_ref[...] = acc

    return pl.pallas_call(
        body, name="relbias_grad",
        in_specs=[pl.BlockSpec(memory_space=pltpu.VMEM), pl.BlockSpec(memory_space=pltpu.VMEM)],
        out_specs=pl.BlockSpec(memory_space=pltpu.VMEM),
        out_shape=jax.ShapeDtypeStruct((NBUCKET, 128), F32),
        compiler_params=_cp())(dss, bucket)


def _inproj_bwd(x, g1, du, dq, dk, dv, w_in, dx1):
    s = x.shape[0]
    tm = min(TM, s)
    cols = ((0, 512), (512, 1024), (1024, 1152), (1152, 1280))

    def body(x_ref, g_ref, du_ref, dq_ref, dk_ref, dv_ref, w_ref, dx1_ref, gx_ref, gw_ref, gg_ref):
        i = pl.program_id(0)

        @pl.when(i == 0)
        def _():
            gw_ref[...] = jnp.zeros_like(gw_ref)
            gg_ref[...] = jnp.zeros_like(gg_ref)

        gv = g_ref[...]
        r1, n1 = _rms(x_ref[...])
        h = (n1 * gv).astype(BF16)
        parts = (du_ref[...], dq_ref[...], dk_ref[...].astype(BF16), dv_ref[...].astype(BF16))
        dh = None
        for (a, b), dpart in zip(cols, parts):
            t = _dot_nt(dpart, w_ref[:, a:b])
            dh = t if dh is None else dh + t
            gw_ref[:, a:b] += _dot_tn(h, dpart)
        dx, dg = _rms_bwd(r1, n1, gv, dh)
        gg_ref[...] += dg
        gx_ref[...] = dx1_ref[...] + dx

    row = lambda w: pl.BlockSpec((tm, w), lambda i: (i, 0))
    vec = pl.BlockSpec((1, D), lambda i: (0, 0))
    full = pl.BlockSpec((D, IN_W), lambda i: (0, 0))
    return pl.pallas_call(
        body, name="inproj_bwd", grid=(s // tm,),
        in_specs=[row(D), vec, row(512), row(512), row(128), row(128), full, row(D)],
        out_specs=[row(D), full, vec],
        out_shape=[jax.ShapeDtypeStruct((s, D), F32), jax.ShapeDtypeStruct((D, IN_W), F32),
                   jax.ShapeDtypeStruct((1, D), F32)],
        compiler_params=_cp(1))(x, g1, du, dq, dk, dv, w_in, dx1)


def _local_step(x, target, small, w_in, w_out, wg, wu, wd):
    g1, g2, g3, g4, w_pool, pool_scale, rel_bias, sinks = small
    bucket = jnp.asarray(_bucket_table())
    wp_bf = w_pool.astype(BF16)
    u, q, k, v = _inproj_fwd(x, g1, w_in)
    pool_o = _pool_fwd(u, wp_bf, pool_scale)
    bias = _bias_table(bucket, rel_bias)
    attn_o = _attn_fwd(q, k, v, bias, sinks)
    mix, x1, h2 = _outproj_fwd(pool_o, attn_o, w_out, x, g2, g3)
    gate, up, df, dy, loss, gg4 = _ffn_fwd(h2, wg, wu, wd, x1, target, g4)
    dgate, dup, dx1, dmix, gg3, gg2 = _ffn_bwd_act(df, gate, up, wg, wu, wd, dy, x1, mix, g3, g2)
    gwg, gwu, gwd = _ffn_bwd_w(h2, gate, up, dgate, dup, df)
    dpool, dattn, gwout = _outproj_bwd(dmix, w_out, pool_o, attn_o)
    du, gwp, gsc = _pool_bwd(u, dpool, wp_bf, pool_scale)
    dq, dk, dv, dss, gsk = _attn_bwd(q, k, v, dattn, bias, sinks)
    grb = _relbias_grad(dss, bucket)
    gx, gwin, gg1 = _inproj_bwd(x, g1, du, dq, dk, dv, w_in, dx1)
    small_grads = (gg1, gg2, gg3, gg4, gwp, gsc, grb[:, :NQ], gsk[:, 0].reshape(1, NQ))
    return loss, gx, small_grads, (gwin, gwout, gwg, gwu, gwd)


def _place():
    x, y, c = lax.axis_index("x"), lax.axis_index("y"), lax.axis_index("c")
    chips = ((1 - x, y), (x, 1 - y), (1 - x, 1 - y))
    return x, y, c, chips


def _remote(src, dst, ssem, rsem, dev):
    return pltpu.make_async_remote_copy(src_ref=src, dst_ref=dst, send_sem=ssem, recv_sem=rsem,
                                        device_id=dev, device_id_type=MESH_T)


def _all_gather_weights(shards):
    nt = len(shards)

    def body(*refs):
        srcs, dsts = refs[:nt], refs[nt:2 * nt]
        loc, isend, irecv, dsend, drecv = refs[2 * nt:]
        x, y, c, chips = _place()
        me = 2 * x + y
        sib = (x, y, 1 - c)
        locals_, sends = [], []
        for t in range(nt):
            half = srcs[t].shape[0] // 2
            mine = pl.ds(pl.multiple_of(c * half, 16), half)
            cp = pltpu.make_async_copy(srcs[t], dsts[t].at[me], loc.at[t])
            cp.start()
            locals_.append(cp)
            for j, (px, py) in enumerate(chips):
                cp = _remote(srcs[t].at[mine], dsts[t].at[me, mine], isend.at[t, j], irecv.at[t, j], (px, py, c))
                cp.start()
                sends.append(cp)
        for t in range(nt):
            half = srcs[t].shape[0] // 2
            mine = pl.ds(pl.multiple_of(c * half, 16), half)
            for j, (px, py) in enumerate(chips):
                blk = dsts[t].at[2 * px + py, mine]
                _remote(blk, blk, isend.at[t, j], irecv.at[t, j], (px, py, c)).wait_recv()
                cp = _remote(blk, blk, dsend.at[t, j], drecv.at[t, j], sib)
                cp.start()
                sends.append(cp)
        for t in range(nt):
            half = srcs[t].shape[0] // 2
            other = pl.ds(pl.multiple_of((1 - c) * half, 16), half)
            for j, (px, py) in enumerate(chips):
                blk = dsts[t].at[2 * px + py, other]
                _remote(blk, blk, dsend.at[t, j], drecv.at[t, j], sib).wait_recv()
        for cp in sends:
            cp.wait_send()
        for cp in locals_:
            cp.wait()

    return pl.pallas_call(
        body, name="allgather_weights",
        in_specs=[ANY] * nt, out_specs=[ANY] * nt,
        out_shape=[jax.ShapeDtypeStruct((NSH,) + a.shape, a.dtype) for a in shards],
        scratch_shapes=[pltpu.SemaphoreType.DMA((nt,)), pltpu.SemaphoreType.DMA((nt, 3)),
                        pltpu.SemaphoreType.DMA((nt, 3)), pltpu.SemaphoreType.DMA((nt, 3)),
                        pltpu.SemaphoreType.DMA((nt, 3))],
        compiler_params=_cp())(*shards)


def _to_sibling(arrs, name):
    nt = len(arrs)

    def body(*refs):
        srcs, dsts = refs[:nt], refs[nt:2 * nt]
        ssem, rsem = refs[2 * nt:]
        x, y, c, _ = _place()
        cps = [_remote(srcs[t], dsts[t], ssem.at[t], rsem.at[t], (x, y, 1 - c)) for t in range(nt)]
        for cp in cps:
            cp.start()
        for cp in cps:
            cp.wait()

    return pl.pallas_call(
        body, name=name, in_specs=[ANY] * nt, out_specs=[ANY] * nt,
        out_shape=[jax.ShapeDtypeStruct(a.shape, a.dtype) for a in arrs],
        scratch_shapes=[pltpu.SemaphoreType.DMA((nt,)), pltpu.SemaphoreType.DMA((nt,))],
        compiler_params=_cp())(*arrs)


def _scatter_to_chips(arrs):
    nt = len(arrs)

    def body(*refs):
        srcs, dsts = refs[:nt], refs[nt:2 * nt]
        ssem, rsem = refs[2 * nt:]
        x, y, c, chips = _place()
        cps = []
        for t in range(nt):
            for j, (px, py) in enumerate(chips):
                cps.append(_remote(srcs[t].at[2 * px + py], dsts[t].at[j], ssem.at[t, j], rsem.at[t, j], (px, py, c)))
        for cp in cps:
            cp.start()
        for cp in cps:
            cp.wait()

    return pl.pallas_call(
        body, name="scatter_to_chips", in_specs=[ANY] * nt, out_specs=[ANY] * nt,
        out_shape=[jax.ShapeDtypeStruct((3,) + a.shape[1:], a.dtype) for a in arrs],
        scratch_shapes=[pltpu.SemaphoreType.DMA((nt, 3)), pltpu.SemaphoreType.DMA((nt, 3))],
        compiler_params=_cp())(*arrs)


def _halves(g):
    return g.reshape(NSH, 2, g.shape[1] // 2, g.shape[2])


def _cast_other_half(grads, oc):
    nt = len(grads)

    def body(oc_ref, *refs):
        for t in range(nt):
            refs[nt + t][...] = refs[t][...].astype(BF16)

    ins = [_halves(g) for g in grads]
    in_specs = [pl.BlockSpec((None, None) + a.shape[2:], lambda k, oc_ref: (k, oc_ref[0], 0, 0)) for a in ins]
    out_specs = [pl.BlockSpec((None,) + a.shape[2:], lambda k, oc_ref: (k, 0, 0)) for a in ins]
    return pl.pallas_call(
        body, name="rs_cast_other_half",
        grid_spec=pltpu.PrefetchScalarGridSpec(num_scalar_prefetch=1, grid=(NSH,), in_specs=in_specs,
                                               out_specs=out_specs),
        out_shape=[jax.ShapeDtypeStruct((NSH,) + a.shape[2:], BF16) for a in ins],
        compiler_params=_cp(1))(oc, *ins)


def _chip_partial(grads, recv, cs):
    nt = len(grads)

    def body(cs_ref, *refs):
        k = pl.program_id(0)
        for t in range(nt):
            p = refs[t][...] + refs[nt + t][...].astype(F32)
            refs[2 * nt + t][...] = p.astype(BF16)

            @pl.when(k == cs_ref[1])
            def _():
                refs[3 * nt + t][...] = p

    ins = [_halves(g) for g in grads]
    g_specs = [pl.BlockSpec((None, None) + a.shape[2:], lambda k, cs_ref: (k, cs_ref[0], 0, 0)) for a in ins]
    r_specs = [pl.BlockSpec((None,) + a.shape[2:], lambda k, cs_ref: (k, 0, 0)) for a in ins]
    own_specs = [pl.BlockSpec(a.shape[2:], lambda k, cs_ref: (0, 0)) for a in ins]
    return pl.pallas_call(
        body, name="rs_chip_partial",
        grid_spec=pltpu.PrefetchScalarGridSpec(num_scalar_prefetch=1, grid=(NSH,), in_specs=g_specs + r_specs,
                                               out_specs=r_specs + own_specs),
        out_shape=[jax.ShapeDtypeStruct((NSH,) + a.shape[2:], BF16) for a in ins]
        + [jax.ShapeDtypeStruct(a.shape[2:], F32) for a in ins],
        compiler_params=_cp(1))(cs, *ins, *recv)


def _sum_chips(own, recv):
    nt = len(own)

    def body(*refs):
        for t in range(nt):
            r = refs[nt + t]
            refs[2 * nt + t][...] = ((refs[t][...] + r[0].astype(F32)) + r[1].astype(F32)) + r[2].astype(F32)

    vm = pl.BlockSpec(memory_space=pltpu.VMEM)
    return pl.pallas_call(
        body, name="rs_sum_chips", in_specs=[vm] * (2 * nt), out_specs=[vm] * nt,
        out_shape=[jax.ShapeDtypeStruct(a.shape, F32) for a in own],
        compiler_params=_cp())(*own, *recv)


def _adamw_math(w, g, m, v):
    m = ADAM_B1 * m + (1.0 - ADAM_B1) * g
    v = ADAM_B2 * v + (1.0 - ADAM_B2) * (g * g)
    m_hat = m / (1.0 - ADAM_B1 ** ADAM_STEP)
    v_hat = v / (1.0 - ADAM_B2 ** ADAM_STEP)
    delta = -ADAM_LR * (m_hat / (jnp.sqrt(v_hat) + ADAM_EPS) + ADAM_WD * w)
    return delta, m, v


ADAM_SPLIT = 4


def _adamw_shards(own, sib, ws, ms, vs, c_arr):
    nt = len(own)

    def body(c_ref, *refs):
        hh = pl.program_id(0)
        mine = hh == c_ref[0]
        for t in range(nt):
            g = jnp.where(mine, refs[t][...], refs[nt + t][...])
            delta, m, v = _adamw_math(refs[2 * nt + t][...], g, refs[3 * nt + t][...], refs[4 * nt + t][...])
            refs[5 * nt + t][...] = g
            refs[6 * nt + t][...] = delta
            refs[7 * nt + t][...] = m
            refs[8 * nt + t][...] = v

    def tile(a):
        return (a.shape[0] // ADAM_SPLIT, a.shape[1])

    half_specs = [pl.BlockSpec(tile(a), lambda hh, q, c_ref: (q, 0)) for a in own]
    full_specs = [pl.BlockSpec(tile(a), lambda hh, q, c_ref: (hh * ADAM_SPLIT + q, 0)) for a in own]
    return pl.pallas_call(
        body, name="adamw_shards",
        grid_spec=pltpu.PrefetchScalarGridSpec(num_scalar_prefetch=1, grid=(2, ADAM_SPLIT),
                                               in_specs=half_specs * 2 + full_specs * 3, out_specs=full_specs * 4),
        out_shape=[jax.ShapeDtypeStruct(w.shape, F32) for w in ws] * 4,
        compiler_params=_cp(2))(c_arr, *own, *sib, *ws, *ms, *vs)


def _small_allreduce_adamw(gp, wp, mp, vp):
    rows = gp.shape[0]

    def body(g_ref, w_ref, m_ref, v_ref, go_ref, d_ref, mo_ref, vo_ref, gather, ssem, rsem):
        x, y, c, _ = _place()
        me = 4 * x + 2 * y + c
        gather[me] = g_ref[...]
        cps = []
        for kk in range(1, 8):
            px, py, pc = x ^ (kk >> 2), y ^ ((kk >> 1) & 1), c ^ (kk & 1)
            cp = _remote(g_ref, gather.at[me], ssem.at[kk], rsem.at[kk], (px, py, pc))
            cp.start()
            cps.append(cp)
        for kk in range(1, 8):
            px, py, pc = x ^ (kk >> 2), y ^ ((kk >> 1) & 1), c ^ (kk & 1)
            _remote(g_ref, gather.at[4 * px + 2 * py + pc], ssem.at[kk], rsem.at[kk], (px, py, pc)).wait_recv()
        for cp in cps:
            cp.wait_send()
        g = gather[0]
        for d in range(1, 8):
            g = g + gather[d]
        delta, m, v = _adamw_math(w_ref[...], g, m_ref[...], v_ref[...])
        go_ref[...] = g
        d_ref[...] = delta
        mo_ref[...] = m
        vo_ref[...] = v

    vm = pl.BlockSpec(memory_space=pltpu.VMEM)
    return pl.pallas_call(
        body, name="small_allreduce_adamw", in_specs=[vm] * 4, out_specs=[vm] * 4,
        out_shape=[jax.ShapeDtypeStruct(gp.shape, F32)] * 4,
        scratch_shapes=[pltpu.VMEM((8, rows, 128), F32), pltpu.SemaphoreType.DMA((8,)),
                        pltpu.SemaphoreType.DMA((8,))],
        compiler_params=_cp())(gp, wp, mp, vp)


SMALL_PARTS = (("g1", (1, D)), ("g2", (1, D)), ("g3", (1, D)), ("g4", (1, D)), ("w_pool", (1, 4, GROUP, GROUP)),
               ("pool_scale", (1, POOL_W)), ("rel_bias", (NBUCKET, NQ)), ("sinks", (1, NQ)))


def _pack_small(parts):
    rows = []
    for a in parts:
        flat = a.reshape(-1)
        n = flat.shape[0]
        padded = -(-n // 1024) * 1024
        rows.append(jnp.pad(flat, (0, padded - n)).reshape(-1, 128))
    return jnp.concatenate(rows, axis=0)


def _unpack_small(packed):
    out, r = [], 0
    for _, shape in SMALL_PARTS:
        n = int(np.prod(shape))
        nr = -(-n // 1024) * 8
        out.append(packed[r:r + nr].reshape(-1)[:n].reshape(shape))
        r += nr
    return out


def kernel(x, g_pre_mix, w_in, w_pool, pool_scale, rel_bias, sinks, w_out, g_post_mix, g_pre_ffn, w_gate, w_up, w_down, g_post_ffn, loss_target, m_g_pre_mix, m_w_in, m_w_pool, m_pool_scale, m_rel_bias, m_sinks, m_w_out, m_g_post_mix, m_g_pre_ffn, m_w_gate, m_w_up, m_w_down, m_g_post_ffn, v_g_pre_mix, v_w_in, v_w_pool, v_pool_scale, v_rel_bias, v_sinks, v_w_out, v_g_post_mix, v_g_pre_ffn, v_w_gate, v_w_up, v_w_down, v_g_post_ffn):
    c = lax.axis_index("c")
    chip = 2 * lax.axis_index("x") + lax.axis_index("y")

    big_w = (w_in[0], w_out[0], w_gate[0], w_up[0], w_down[0])
    win_all, wout_all, wg_all, wu_all, wd_all = _all_gather_weights([w.astype(BF16) for w in big_w])
    win_full = jnp.transpose(win_all, (1, 0, 2)).reshape(D, IN_W)
    wout_full = wout_all.reshape(D, D)

    small = (g_pre_mix, g_post_mix, g_pre_ffn, g_post_ffn, w_pool[0], pool_scale, rel_bias, sinks)
    loss, gx, small_grads, (gwin, gwout, gwg, gwu, gwd) = _local_step(
        x[0], loss_target[0], small, win_full, wout_full, wg_all, wu_all, wd_all)

    grads = [jnp.transpose(gwin.reshape(D, NSH, WIN_S), (1, 0, 2)), gwout.reshape(NSH, WOUT_S, D), gwg, gwu, gwd]
    send_a = _cast_other_half(grads, (1 - c).reshape(1).astype(jnp.int32))
    recv_a = _to_sibling(send_a, "rs_sibling_exchange")
    outs = _chip_partial(grads, recv_a, jnp.stack([c, chip]).astype(jnp.int32))
    send_b, own = outs[:5], outs[5:]
    recv_b = _scatter_to_chips(send_b)
    final_half = _sum_chips(own, recv_b)
    sib_half = _to_sibling(final_half, "rs_sibling_share")
    adam = _adamw_shards(final_half, sib_half, big_w, (m_w_in[0], m_w_out[0], m_w_gate[0], m_w_up[0], m_w_down[0]),
                         (v_w_in[0], v_w_out[0], v_w_gate[0], v_w_up[0], v_w_down[0]),
                         c.reshape(1).astype(jnp.int32))
    big_g, big_d, big_m, big_v = adam[0:5], adam[5:10], adam[10:15], adam[15:20]

    gp = _pack_small(small_grads)
    wp = _pack_small((g_pre_mix, g_post_mix, g_pre_ffn, g_post_ffn, w_pool, pool_scale, rel_bias, sinks))
    mp = _pack_small((m_g_pre_mix, m_g_post_mix, m_g_pre_ffn, m_g_post_ffn, m_w_pool, m_pool_scale, m_rel_bias,
                      m_sinks))
    vp = _pack_small((v_g_pre_mix, v_g_post_mix, v_g_pre_ffn, v_g_post_ffn, v_w_pool, v_pool_scale, v_rel_bias,
                      v_sinks))
    small_out = [_unpack_small(p) for p in _small_allreduce_adamw(gp, wp, mp, vp)]

    total_loss = lax.psum(loss[0, 0], ("x", "y", "c"))

    def ordered(small4, big4):
        sg1, sg2, sg3, sg4, swp, ssc, srb, ssk = small4
        bwin, bwout, bwg, bwu, bwd = [a[None] for a in big4]
        return [sg1, bwin, swp, ssc, srb, ssk, bwout, sg2, sg3, bwg, bwu, bwd, sg4]

    res = [total_loss, gx[None]]
    for sm, bg in zip(small_out, (big_g, big_d, big_m, big_v)):
        res += ordered(sm, bg)
    return tuple(res)
```

```python
import functools

import numpy as np
import jax
import jax.numpy as jnp
from jax import lax
from jax.experimental import pallas as pl
from jax.experimental.pallas import tpu as pltpu

F32 = jnp.float32
BF16 = jnp.bfloat16

D = 1024
POOL_W = 512
GROUP = 128
WINDOWS = (2, 4, 8, 16)
HALO = 128
NQ = 8
BLK = 128
NBUCKET = 32
IN_W = 1280
DFF = 2816
NSH = 4
FS = DFF // NSH
WIN_S = IN_W // NSH
WOUT_S = D // NSH
EPS = 1e-6
NEG = -1e30
SCALE = 0.125

ADAM_LR = 0.001
ADAM_B1 = 0.9
ADAM_B2 = 0.999
ADAM_EPS = 1e-08
ADAM_WD = 0.01
ADAM_STEP = 10

TM = 512
TM_POOL = 256
TM_FFN = 512
VMEM_LIMIT = 56 * 1024 * 1024

MESH_T = pl.DeviceIdType.MESH
ANY = pl.BlockSpec(memory_space=pl.ANY)


def _cp(n_grid=0, **kw):
    sem = ("arbitrary",) * n_grid if n_grid else None
    return pltpu.CompilerParams(dimension_semantics=sem, vmem_limit_bytes=VMEM_LIMIT, **kw)


def _dot(a, b):
    return jnp.dot(a, b, preferred_element_type=F32)


def _dot_nt(a, b):
    return lax.dot_general(a, b, (((1,), (1,)), ((), ())), preferred_element_type=F32)


def _dot_tn(a, b):
    return lax.dot_general(a, b, (((0,), (0,)), ((), ())), preferred_element_type=F32)


def _rms(x):
    r = lax.rsqrt(jnp.mean(x * x, axis=-1, keepdims=True) + EPS)
    return r, x * r


def _rms_bwd(r, n, g, dout):
    dn = dout * g
    dx = r * (dn - n * jnp.mean(dn * n, axis=-1, keepdims=True))
    dg = jnp.sum(dout * n, axis=0, keepdims=True)
    return dx, dg


def _split(x, n):
    parts = []
    r = x
    for _ in range(n):
        p = r.astype(BF16)
        parts.append(p)
        r = r - p.astype(F32)
    return parts


def _band_dot(a, x, n):
    acc = None
    for p in _split(x, n):
        t = _dot(a, p)
        acc = t if acc is None else acc + t
    return acc


def _bucket_table():
    qi = np.arange(BLK)[:, None]
    kj = np.arange(2 * BLK)[None, :]
    dist = qi + BLK - kj
    n = np.maximum(dist, 0)
    nf = np.maximum(n, 1).astype(np.float32)
    large = 16 + (np.log(nf / np.float32(16)) / np.float32(np.log(128 / 16)) * np.float32(16)).astype(np.int32)
    large = np.minimum(large, NBUCKET - 1)
    return np.where(n < 16, n, large).astype(np.int32)


def _inproj_fwd(x, g1, w_in):
    s = x.shape[0]
    tm = min(TM, s)

    def body(x_ref, g_ref, w_ref, u_ref, q_ref, k_ref, v_ref):
        _, n = _rms(x_ref[...])
        h = (n * g_ref[...]).astype(BF16)
        proj = _dot_nt(h, w_ref[...])
        u_ref[...] = proj[:, :512]
        q_ref[...] = proj[:, 512:1024].astype(BF16)
        k_ref[...] = proj[:, 1024:1152].astype(BF16)
        v_ref[...] = proj[:, 1152:1280].astype(BF16)

    row = lambda w: pl.BlockSpec((tm, w), lambda i: (i, 0))
    return pl.pallas_call(
        body, name="inproj_fwd", grid=(s // tm,),
        in_specs=[row(D), pl.BlockSpec((1, D), lambda i: (0, 0)), pl.BlockSpec((IN_W, D), lambda i: (0, 0))],
        out_specs=[row(512), row(512), row(128), row(128)],
        out_shape=[jax.ShapeDtypeStruct((s, 512), F32), jax.ShapeDtypeStruct((s, 512), BF16),
                   jax.ShapeDtypeStruct((s, 128), BF16), jax.ShapeDtypeStruct((s, 128), BF16)],
        compiler_params=_cp(1))(x, g1, w_in)


def _pooled(ext, cur, t0, tm):
    rows = lax.broadcasted_iota(jnp.int32, (tm, tm + HALO), 0)
    cols = lax.broadcasted_iota(jnp.int32, (tm, tm + HALO), 1)
    d = rows + HALO - cols
    t = t0 + lax.broadcasted_iota(jnp.int32, (tm, GROUP), 0)
    out = []
    for g, w in enumerate(WINDOWS):
        a = jnp.where((d >= 0) & (d < w), 1.0, 0.0).astype(BF16)
        ws = _band_dot(a, ext[:, g * GROUP:(g + 1) * GROUP], 3)
        cnt = jnp.minimum(t + 1, w).astype(F32)
        out.append(ws / cnt - cur[:, g * GROUP:(g + 1) * GROUP])
    return out


def _pool_fwd(u, w_pool, pool_scale):
    s = u.shape[0]
    tm = min(TM_POOL, s)
    hb = tm // HALO

    def body(uc_ref, uh_ref, wp_ref, sc_ref, o_ref):
        i = pl.program_id(0)
        cur = uc_ref[...]
        halo = jnp.where(i > 0, uh_ref[...], 0.0)
        ext = jnp.concatenate([halo, cur], axis=0)
        pooled = _pooled(ext, cur, i * tm, tm)
        for g in range(4):
            sl = slice(g * GROUP, (g + 1) * GROUP)
            mixed = _dot(pooled[g].astype(BF16), wp_ref[g])
            o_ref[:, sl] = (mixed * sc_ref[:, sl]).astype(BF16)

    return pl.pallas_call(
        body, name="pool_fwd", grid=(s // tm,),
        in_specs=[pl.BlockSpec((tm, 512), lambda i: (i, 0)),
                  pl.BlockSpec((HALO, 512), lambda i: (jnp.maximum(i * hb - 1, 0), 0)),
                  pl.BlockSpec((4, GROUP, GROUP), lambda i: (0, 0, 0)),
                  pl.BlockSpec((1, 512), lambda i: (0, 0))],
        out_specs=pl.BlockSpec((tm, 512), lambda i: (i, 0)),
        out_shape=jax.ShapeDtypeStruct((s, 512), BF16),
        compiler_params=_cp(1))(u, u, w_pool, pool_scale)


def _bias_table(bucket, rel_bias):
    def body(b_ref, rb_ref, o_ref):
        bucket_v = b_ref[...]
        rows = lax.broadcasted_iota(jnp.int32, (BLK, 2 * BLK), 0)
        cols = lax.broadcasted_iota(jnp.int32, (BLK, 2 * BLK), 1)
        dist = rows + BLK - cols
        inwin = (dist >= 0) & (dist < BLK)
        for h in range(NQ):
            acc = jnp.zeros((BLK, 2 * BLK), F32)
            for b in range(NBUCKET):
                acc = jnp.where(bucket_v == b, rb_ref[b, h], acc)
            rest = jnp.where(inwin, acc, NEG)
            o_ref[1, h] = rest
            o_ref[0, h] = jnp.where(cols >= BLK, rest, NEG)

    return pl.pallas_call(
        body, name="bias_table",
        in_specs=[pl.BlockSpec(memory_space=pltpu.VMEM), pl.BlockSpec(memory_space=pltpu.SMEM)],
        out_specs=pl.BlockSpec(memory_space=pltpu.VMEM),
        out_shape=jax.ShapeDtypeStruct((2, NQ, BLK, 2 * BLK), F32),
        compiler_params=_cp())(bucket, rel_bias)


def _kv_bands(k_ref, v_ref, n):
    pstart = pl.multiple_of(jnp.maximum(n - 1, 0) * BLK, BLK)
    cstart = pl.multiple_of(n * BLK, BLK)
    lo = lax.broadcasted_iota(jnp.int32, (2 * BLK, 128), 1) < 64
    out = []
    for ref in (k_ref, v_ref):
        band = jnp.concatenate([ref[pl.ds(pstart, BLK), :], ref[pl.ds(cstart, BLK), :]], axis=0).astype(F32)
        rot = pltpu.roll(band, 64, axis=1)
        out.append((jnp.where(lo, band, rot).astype(BF16), jnp.where(lo, rot, band).astype(BF16)))
    return out[0], out[1], lo, pstart, cstart


def _attn_probs(qm, kk, bias, sink):
    s = _dot_nt(qm, kk) + bias
    m = jnp.maximum(jnp.max(s, axis=-1, keepdims=True), sink)
    p = jnp.exp(s - m)
    es = jnp.exp(sink - m)
    inv = 1.0 / (jnp.sum(p, axis=-1, keepdims=True) + es)
    return p * inv, es * inv


def _attn_fwd(q, k, v, bias, sinks):
    s = q.shape[0]

    def body(q_ref, k_ref, v_ref, b_ref, sk_ref, o_ref):
        n = pl.program_id(0)
        kk, vv, lo, _, _ = _kv_bands(k_ref, v_ref, n)
        bidx = jnp.minimum(n, 1)
        lo_q = lax.broadcasted_iota(jnp.int32, (BLK, 128), 1) < 64
        for p in range(4):
            h = p // 2
            qt = q_ref[:, p * 128:(p + 1) * 128].astype(F32) * SCALE
            acc = jnp.zeros((BLK, 128), F32)
            for e in range(2):
                head = 2 * p + e
                sel_q = lo_q if e == 0 else jnp.logical_not(lo_q)
                sel_kv = lo if e == 0 else jnp.logical_not(lo)
                qm = jnp.where(sel_q, qt, 0.0).astype(BF16)
                prob, _ = _attn_probs(qm, kk[h], b_ref[bidx, head], sk_ref[0, head])
                vm = jnp.where(sel_kv, vv[h], jnp.zeros_like(vv[h]))
                acc = acc + _dot(prob.astype(BF16), vm)
            o_ref[:, p * 128:(p + 1) * 128] = acc.astype(BF16)

    return pl.pallas_call(
        body, name="attn_fwd", grid=(s // BLK,),
        in_specs=[pl.BlockSpec((BLK, 512), lambda i: (i, 0)),
                  pl.BlockSpec((s, 128), lambda i: (0, 0)),
                  pl.BlockSpec((s, 128), lambda i: (0, 0)),
                  pl.BlockSpec((2, NQ, BLK, 2 * BLK), lambda i: (0, 0, 0, 0)),
                  pl.BlockSpec(memory_space=pltpu.SMEM)],
        out_specs=pl.BlockSpec((BLK, 512), lambda i: (i, 0)),
        out_shape=jax.ShapeDtypeStruct((s, 512), BF16),
        compiler_params=_cp(1))(q, k, v, bias, sinks)


def _outproj_fwd(pool_o, attn_o, w_out, x, g2, g3):
    s = x.shape[0]
    tm = min(TM, s)

    def body(p_ref, a_ref, w_ref, x_ref, g2_ref, g3_ref, mix_ref, x1_ref, h2_ref):
        mix = _dot(p_ref[...], w_ref[0:512, :]) + _dot(a_ref[...], w_ref[512:1024, :])
        mix_ref[...] = mix
        _, n2 = _rms(mix)
        x1 = x_ref[...] + n2 * g2_ref[...]
        x1_ref[...] = x1
        _, n3 = _rms(x1)
        h2_ref[...] = (n3 * g3_ref[...]).astype(BF16)

    row = lambda w: pl.BlockSpec((tm, w), lambda i: (i, 0))
    vec = pl.BlockSpec((1, D), lambda i: (0, 0))
    return pl.pallas_call(
        body, name="outproj_fwd", grid=(s // tm,),
        in_specs=[row(512), row(512), pl.BlockSpec((D, D), lambda i: (0, 0)), row(D), vec, vec],
        out_specs=[row(D), row(D), row(D)],
        out_shape=[jax.ShapeDtypeStruct((s, D), F32), jax.ShapeDtypeStruct((s, D), F32),
                   jax.ShapeDtypeStruct((s, D), BF16)],
        compiler_params=_cp(1))(pool_o, attn_o, w_out, x, g2, g3)


def _silu_parts(g):
    sg = jax.nn.sigmoid(g)
    return sg, g * sg


def _ffn_fwd(h2, wg, wu, wd, x1, target, g4):
    s = h2.shape[0]
    tm = min(TM_FFN, s)

    def body(h_ref, wg_ref, wu_ref, wd_ref, x1_ref, t_ref, g4_ref,
             gate_ref, up_ref, df_ref, dy_ref, loss_ref, gg4_ref, f_acc):
        i = pl.program_id(0)
        j = pl.program_id(1)
        h = h_ref[...]
        gate = _dot_nt(h, wg_ref[...])
        up = _dot_nt(h, wu_ref[...])
        gate_ref[...] = gate.astype(BF16)
        up_ref[...] = up.astype(BF16)
        _, sl = _silu_parts(gate)
        contrib = _dot((sl * up).astype(BF16), wd_ref[...])

        @pl.when(j == 0)
        def _():
            f_acc[...] = contrib

        @pl.when(j > 0)
        def _():
            f_acc[...] += contrib

        @pl.when((i == 0) & (j == 0))
        def _():
            loss_ref[...] = jnp.zeros_like(loss_ref)
            gg4_ref[...] = jnp.zeros_like(gg4_ref)

        @pl.when(j == NSH - 1)
        def _():
            r4, n4 = _rms(f_acc[...])
            g4v = g4_ref[...]
            err = x1_ref[...] + n4 * g4v - t_ref[...]
            loss_ref[...] += (0.5 / D) * jnp.sum(err * err).reshape(1, 1)
            dy = err * (1.0 / D)
            dy_ref[...] = dy
            df, dg = _rms_bwd(r4, n4, g4v, dy)
            gg4_ref[...] += dg
            df_ref[...] = df.astype(BF16)

    row = lambda w: pl.BlockSpec((tm, w), lambda i, j: (i, 0))
    sh = lambda r, c: pl.BlockSpec((None, r, c), lambda i, j: (j, 0, 0))
    act = pl.BlockSpec((None, tm, FS), lambda i, j: (j, i, 0))
    vec = pl.BlockSpec((1, D), lambda i, j: (0, 0))
    return pl.pallas_call(
        body, name="ffn_fwd", grid=(s // tm, NSH),
        in_specs=[row(D), sh(FS, D), sh(FS, D), sh(FS, D), row(D), row(D), vec],
        out_specs=[act, act, row(D), row(D), pl.BlockSpec((1, 1), lambda i, j: (0, 0)), vec],
        out_shape=[jax.ShapeDtypeStruct((NSH, s, FS), BF16), jax.ShapeDtypeStruct((NSH, s, FS), BF16),
                   jax.ShapeDtypeStruct((s, D), BF16), jax.ShapeDtypeStruct((s, D), F32),
                   jax.ShapeDtypeStruct((1, 1), F32), jax.ShapeDtypeStruct((1, D), F32)],
        scratch_shapes=[pltpu.VMEM((tm, D), F32)],
        compiler_params=_cp(2))(h2, wg, wu, wd, x1, target, g4)


def _ffn_bwd_act(df, gate, up, wg, wu, wd, dy, x1, mix, g3, g2):
    s = df.shape[0]
    tm = min(TM_FFN, s)

    def body(df_ref, gate_ref, up_ref, wg_ref, wu_ref, wd_ref, dy_ref, x1_ref, mix_ref, g3_ref, g2_ref,
             dgate_ref, dup_ref, dx1_ref, dmix_ref, gg3_ref, gg2_ref, acc):
        i = pl.program_id(0)
        j = pl.program_id(1)
        da = _dot_nt(df_ref[...], wd_ref[...])
        g = gate_ref[...].astype(F32)
        u = up_ref[...].astype(F32)
        sg, sl = _silu_parts(g)
        dgate = (da * u * (sg * (1.0 + g * (1.0 - sg)))).astype(BF16)
        dup = (da * sl).astype(BF16)
        dgate_ref[...] = dgate
        dup_ref[...] = dup
        contrib = _dot(dgate, wg_ref[...]) + _dot(dup, wu_ref[...])

        @pl.when(j == 0)
        def _():
            acc[...] = contrib

        @pl.when(j > 0)
        def _():
            acc[...] += contrib

        @pl.when((i == 0) & (j == 0))
        def _():
            gg3_ref[...] = jnp.zeros_like(gg3_ref)
            gg2_ref[...] = jnp.zeros_like(gg2_ref)

        @pl.when(j == NSH - 1)
        def _():
            r3, n3 = _rms(x1_ref[...])
            dx1n, dg3 = _rms_bwd(r3, n3, g3_ref[...], acc[...])
            dx1 = dy_ref[...] + dx1n
            dx1_ref[...] = dx1
            gg3_ref[...] += dg3
            r2, n2 = _rms(mix_ref[...])
            dmix, dg2 = _rms_bwd(r2, n2, g2_ref[...], dx1)
            gg2_ref[...] += dg2
            dmix_ref[...] = dmix.astype(BF16)

    row = lambda w: pl.BlockSpec((tm, w), lambda i, j: (i, 0))
    sh = lambda r, c: pl.BlockSpec((None, r, c), lambda i, j: (j, 0, 0))
    act = pl.BlockSpec((None, tm, FS), lambda i, j: (j, i, 0))
    vec = pl.BlockSpec((1, D), lambda i, j: (0, 0))
    return pl.pallas_call(
        body, name="ffn_bwd_act", grid=(s // tm, NSH),
        in_specs=[row(D), act, act, sh(FS, D), sh(FS, D), sh(FS, D), row(D), row(D), row(D), vec, vec],
        out_specs=[act, act, row(D), row(D), vec, vec],
        out_shape=[jax.ShapeDtypeStruct((NSH, s, FS), BF16), jax.ShapeDtypeStruct((NSH, s, FS), BF16),
                   jax.ShapeDtypeStruct((s, D), F32), jax.ShapeDtypeStruct((s, D), BF16),
                   jax.ShapeDtypeStruct((1, D), F32), jax.ShapeDtypeStruct((1, D), F32)],
        scratch_shapes=[pltpu.VMEM((tm, D), F32)],
        compiler_params=_cp(2))(df, gate, up, wg, wu, wd, dy, x1, mix, g3, g2)


def _ffn_bwd_w(h2, gate, up, dgate, dup, df):
    s = h2.shape[0]
    tm = min(TM_FFN, s)

    def body(h_ref, gate_ref, up_ref, dgate_ref, dup_ref, df_ref, gwg_ref, gwu_ref, gwd_ref):
        i = pl.program_id(1)

        @pl.when(i == 0)
        def _():
            gwg_ref[...] = jnp.zeros_like(gwg_ref)
            gwu_ref[...] = jnp.zeros_like(gwu_ref)
            gwd_ref[...] = jnp.zeros_like(gwd_ref)

        h = h_ref[...]
        gwg_ref[...] += _dot_tn(dgate_ref[...], h)
        gwu_ref[...] += _dot_tn(dup_ref[...], h)
        _, sl = _silu_parts(gate_ref[...].astype(F32))
        a = (sl * up_ref[...].astype(F32)).astype(BF16)
        gwd_ref[...] += _dot_tn(a, df_ref[...])

    row = lambda w: pl.BlockSpec((tm, w), lambda j, i: (i, 0))
    act = pl.BlockSpec((None, tm, FS), lambda j, i: (j, i, 0))
    sh = lambda r, c: pl.BlockSpec((None, r, c), lambda j, i: (j, 0, 0))
    return pl.pallas_call(
        body, name="ffn_bwd_w", grid=(NSH, s // tm),
        in_specs=[row(D), act, act, act, act, row(D)],
        out_specs=[sh(FS, D), sh(FS, D), sh(FS, D)],
        out_shape=[jax.ShapeDtypeStruct((NSH, FS, D), F32)] * 3,
        compiler_params=_cp(2))(h2, gate, up, dgate, dup, df)


def _outproj_bwd(dmix, w_out, pool_o, attn_o):
    s = dmix.shape[0]
    tm = min(TM, s)

    def body(dm_ref, w_ref, p_ref, a_ref, dpool_ref, dattn_ref, gw_ref):
        i = pl.program_id(0)

        @pl.when(i == 0)
        def _():
            gw_ref[...] = jnp.zeros_like(gw_ref)

        dm = dm_ref[...]
        dpool_ref[...] = _dot_nt(dm, w_ref[0:512, :])
        dattn_ref[...] = _dot_nt(dm, w_ref[512:1024, :]).astype(BF16)
        gw_ref[0:512, :] += _dot_tn(p_ref[...], dm)
        gw_ref[512:1024, :] += _dot_tn(a_ref[...], dm)

    row = lambda w: pl.BlockSpec((tm, w), lambda i: (i, 0))
    full = pl.BlockSpec((D, D), lambda i: (0, 0))
    return pl.pallas_call(
        body, name="outproj_bwd", grid=(s // tm,),
        in_specs=[row(D), full, row(512), row(512)],
        out_specs=[row(512), row(512), full],
        out_shape=[jax.ShapeDtypeStruct((s, 512), F32), jax.ShapeDtypeStruct((s, 512), BF16),
                   jax.ShapeDtypeStruct((D, D), F32)],
        compiler_params=_cp(1))(dmix, w_out, pool_o, attn_o)


def _pool_bwd(u, dpool, w_pool, pool_scale):
    s = u.shape[0]
    tm = min(TM_POOL, s)
    hb = tm // HALO
    nt = s // tm
    last_halo = s // HALO - 1

    def body(uc_ref, uh_ref, dc_ref, dn_ref, wp_ref, sc_ref, du_ref, gwp_ref, gsc_ref):
        i = pl.program_id(0)

        @pl.when(i == 0)
        def _():
            gwp_ref[...] = jnp.zeros_like(gwp_ref)
            gsc_ref[...] = jnp.zeros_like(gsc_ref)

        cur = uc_ref[...]
        halo = jnp.where(i > 0, uh_ref[...], 0.0)
        pooled = _pooled(jnp.concatenate([halo, cur], axis=0), cur, i * tm, tm)
        dcur = dc_ref[...]
        dnext = jnp.where(i < nt - 1, dn_ref[...], 0.0)
        dext = jnp.concatenate([dcur, dnext], axis=0)
        rows = lax.broadcasted_iota(jnp.int32, (tm, tm + HALO), 0)
        cols = lax.broadcasted_iota(jnp.int32, (tm, tm + HALO), 1)
        d = cols - rows
        t_ext = i * tm + lax.broadcasted_iota(jnp.int32, (tm + HALO, GROUP), 0)
        for g, w in enumerate(WINDOWS):
            sl = slice(g * GROUP, (g + 1) * GROUP)
            pb = pooled[g].astype(BF16)
            wp = wp_ref[g]
            mixed = _dot(pb, wp)
            gsc_ref[:, sl] += jnp.sum(dcur[:, sl] * mixed, axis=0, keepdims=True)
            dmixed = (dext[:, sl] * sc_ref[:, sl]).astype(BF16)
            gwp_ref[g] += _dot_tn(pb, dmixed[0:tm])
            dpooled = _dot_nt(dmixed, wp)
            z = dpooled / jnp.minimum(t_ext + 1, w).astype(F32)
            b = jnp.where((d >= 0) & (d < w), 1.0, 0.0).astype(BF16)
            du_ref[:, sl] = (_band_dot(b, z, 2) - dpooled[0:tm]).astype(BF16)

    return pl.pallas_call(
        body, name="pool_bwd", grid=(nt,),
        in_specs=[pl.BlockSpec((tm, 512), lambda i: (i, 0)),
                  pl.BlockSpec((HALO, 512), lambda i: (jnp.maximum(i * hb - 1, 0), 0)),
                  pl.BlockSpec((tm, 512), lambda i: (i, 0)),
                  pl.BlockSpec((HALO, 512), lambda i: (jnp.minimum((i + 1) * hb, last_halo), 0)),
                  pl.BlockSpec((4, GROUP, GROUP), lambda i: (0, 0, 0)),
                  pl.BlockSpec((1, 512), lambda i: (0, 0))],
        out_specs=[pl.BlockSpec((tm, 512), lambda i: (i, 0)),
                   pl.BlockSpec((4, GROUP, GROUP), lambda i: (0, 0, 0)),
                   pl.BlockSpec((1, 512), lambda i: (0, 0))],
        out_shape=[jax.ShapeDtypeStruct((s, 512), BF16), jax.ShapeDtypeStruct((4, GROUP, GROUP), F32),
                   jax.ShapeDtypeStruct((1, 512), F32)],
        compiler_params=_cp(1))(u, u, dpool, dpool, w_pool, pool_scale)


def _attn_bwd(q, k, v, do, bias, sinks):
    s = q.shape[0]

    def body(q_ref, do_ref, k_ref, v_ref, b_ref, sk_ref, dq_ref, dk_ref, dv_ref, dss_ref, gsk_ref):
        n = pl.program_id(0)

        @pl.when(n == 0)
        def _():
            dk_ref[...] = jnp.zeros_like(dk_ref)
            dv_ref[...] = jnp.zeros_like(dv_ref)
            dss_ref[...] = jnp.zeros_like(dss_ref)
            gsk_ref[...] = jnp.zeros_like(gsk_ref)

        kk, vv, lo, pstart, cstart = _kv_bands(k_ref, v_ref, n)
        bidx = jnp.minimum(n, 1)
        lo_q = lax.broadcasted_iota(jnp.int32, (BLK, 128), 1) < 64
        dkk = [jnp.zeros((2 * BLK, 128), F32), jnp.zeros((2 * BLK, 128), F32)]
        dvv = [jnp.zeros((2 * BLK, 128), F32), jnp.zeros((2 * BLK, 128), F32)]
        for p in range(4):
            h = p // 2
            qt = q_ref[:, p * 128:(p + 1) * 128].astype(F32) * SCALE
            dot = do_ref[:, p * 128:(p + 1) * 128]
            dq = jnp.zeros((BLK, 128), F32)
            for e in range(2):
                head = 2 * p + e
                sel_q = lo_q if e == 0 else jnp.logical_not(lo_q)
                sel_kv = lo if e == 0 else jnp.logical_not(lo)
                qm = jnp.where(sel_q, qt, 0.0).astype(BF16)
                prob, ps = _attn_probs(qm, kk[h], b_ref[bidx, head], sk_ref[0, head])
                dom = jnp.where(sel_q, dot, jnp.zeros_like(dot))
                dp = _dot_nt(dom, vv[h])
                delta = jnp.sum(prob * dp, axis=-1, keepdims=True)
                ds = prob * (dp - delta)
                gsk_ref[pl.ds(head, 1), :] += jnp.broadcast_to(-jnp.sum(ps * delta).reshape(1, 1), (1, 128))
                dss_ref[head] += ds
                dsb = ds.astype(BF16)
                km = jnp.where(sel_kv, kk[h], jnp.zeros_like(kk[h]))
                dq = dq + _dot(dsb, km)
                dkk[h] = dkk[h] + _dot_tn(dsb, qm)
                dvv[h] = dvv[h] + _dot_tn(prob.astype(BF16), dom)
            dq_ref[:, p * 128:(p + 1) * 128] = (dq * SCALE).astype(BF16)
        for acc, ref in ((dkk, dk_ref), (dvv, dv_ref)):
            f0 = acc[0] + pltpu.roll(acc[0], 64, axis=1)
            f1 = acc[1] + pltpu.roll(acc[1], 64, axis=1)
            band = jnp.where(lo, f0, f1)
            ref[pl.ds(pstart, BLK), :] += band[0:BLK]
            ref[pl.ds(cstart, BLK), :] += band[BLK:2 * BLK]

    blk = pl.BlockSpec((BLK, 512), lambda i: (i, 0))
    kv = pl.BlockSpec((s, 128), lambda i: (0, 0))
    return pl.pallas_call(
        body, name="attn_bwd", grid=(s // BLK,),
        in_specs=[blk, blk, kv, kv, pl.BlockSpec((2, NQ, BLK, 2 * BLK), lambda i: (0, 0, 0, 0)),
                  pl.BlockSpec(memory_space=pltpu.SMEM)],
        out_specs=[blk, kv, kv, pl.BlockSpec((NQ, BLK, 2 * BLK), lambda i: (0, 0, 0)),
                   pl.BlockSpec((NQ, 128), lambda i: (0, 0))],
        out_shape=[jax.ShapeDtypeStruct((s, 512), BF16), jax.ShapeDtypeStruct((s, 128), F32),
                   jax.ShapeDtypeStruct((s, 128), F32), jax.ShapeDtypeStruct((NQ, BLK, 2 * BLK), F32),
                   jax.ShapeDtypeStruct((NQ, 128), F32)],
        compiler_params=_cp(1))(q, do, k, v, bias, sinks)


def _relbias_grad(dss, bucket):
    def body(ds_ref, b_ref, o_ref):
        bucket_v = b_ref[...]
        lane = lax.broadcasted_iota(jnp.int32, (NQ, 128), 1)
        head_row = lax.broadcasted_iota(jnp.int32, (NQ, 2 * BLK), 0)
        acc = jnp.zeros((NQ, 128), F32)
        for b in range(NBUCKET):
            sel = bucket_v == b
            stack = jnp.zeros((NQ, 2 * BLK), F32)
            for h in range(NQ):
                col_sums = jnp.sum(jnp.where(sel, ds_ref[h], 0.0), axis=0, keepdims=True)
                stack = jnp.where(head_row == h, col_sums, stack)
            acc = acc + jnp.where(lane == b, jnp.sum(stack, axis=1, keepdims=True), 0.0)
        o_ref[...] = acc

    return pl.pallas_call(
        body, name="relbias_grad",
        in_specs=[pl.BlockSpec(memory_space=pltpu.VMEM), pl.BlockSpec(memory_space=pltpu.VMEM)],
        out_specs=pl.BlockSpec(memory_space=pltpu.VMEM),
        out_shape=jax.ShapeDtypeStruct((NQ, 128), F32),
        compiler_params=_cp())(dss, bucket)


def _inproj_bwd(x, g1, du, dq, dk, dv, w_in, dx1):
    s = x.shape[0]
    tm = min(TM, s)
    cols = ((0, 512), (512, 1024), (1024, 1152), (1152, 1280))

    def body(x_ref, g_ref, du_ref, dq_ref, dk_ref, dv_ref, w_ref, dx1_ref, gx_ref, gw_ref, gg_ref):
        i = pl.program_id(0)

        @pl.when(i == 0)
        def _():
            gw_ref[...] = jnp.zeros_like(gw_ref)
            gg_ref[...] = jnp.zeros_like(gg_ref)

        gv = g_ref[...]
        r1, n1 = _rms(x_ref[...])
        h = (n1 * gv).astype(BF16)
        parts = (du_ref[...], dq_ref[...], dk_ref[...].astype(BF16), dv_ref[...].astype(BF16))
        dh = None
        for (a, b), dpart in zip(cols, parts):
            t = _dot(dpart, w_ref[a:b, :])
            dh = t if dh is None else dh + t
            gw_ref[a:b, :] += _dot_tn(dpart, h)
        dx, dg = _rms_bwd(r1, n1, gv, dh)
        gg_ref[...] += dg
        gx_ref[...] = dx1_ref[...] + dx

    row = lambda w: pl.BlockSpec((tm, w), lambda i: (i, 0))
    vec = pl.BlockSpec((1, D), lambda i: (0, 0))
    full = pl.BlockSpec((IN_W, D), lambda i: (0, 0))
    return pl.pallas_call(
        body, name="inproj_bwd", grid=(s // tm,),
        in_specs=[row(D), vec, row(512), row(512), row(128), row(128), full, row(D)],
        out_specs=[row(D), full, vec],
        out_shape=[jax.ShapeDtypeStruct((s, D), F32), jax.ShapeDtypeStruct((IN_W, D), F32),
                   jax.ShapeDtypeStruct((1, D), F32)],
        compiler_params=_cp(1))(x, g1, du, dq, dk, dv, w_in, dx1)


def _local_step(x, target, small, w_in, w_out, wg, wu, wd):
    g1, g2, g3, g4, w_pool, pool_scale, rel_bias, sinks = small
    bucket = jnp.asarray(_bucket_table())
    wp_bf = w_pool.astype(BF16)
    u, q, k, v = _inproj_fwd(x, g1, w_in)
    pool_o = _pool_fwd(u, wp_bf, pool_scale)
    bias = _bias_table(bucket, rel_bias)
    attn_o = _attn_fwd(q, k, v, bias, sinks)
    mix, x1, h2 = _outproj_fwd(pool_o, attn_o, w_out, x, g2, g3)
    gate, up, df, dy, loss, gg4 = _ffn_fwd(h2, wg, wu, wd, x1, target, g4)
    dgate, dup, dx1, dmix, gg3, gg2 = _ffn_bwd_act(df, gate, up, wg, wu, wd, dy, x1, mix, g3, g2)
    gwg, gwu, gwd = _ffn_bwd_w(h2, gate, up, dgate, dup, df)
    dpool, dattn, gwout = _outproj_bwd(dmix, w_out, pool_o, attn_o)
    du, gwp, gsc = _pool_bwd(u, dpool, wp_bf, pool_scale)
    dq, dk, dv, dss, gsk = _attn_bwd(q, k, v, dattn, bias, sinks)
    grb = _relbias_grad(dss, bucket)
    gx, gwin, gg1 = _inproj_bwd(x, g1, du, dq, dk, dv, w_in, dx1)
    small_grads = (gg1, gg2, gg3, gg4, gwp, gsc, grb[:, :NBUCKET].T, gsk[:, 0].reshape(1, NQ))
    return loss, gx, small_grads, (gwin, gwout, gwg, gwu, gwd)


def _place():
    x, y, c = lax.axis_index("x"), lax.axis_index("y"), lax.axis_index("c")
    chips = ((1 - x, y), (x, 1 - y), (1 - x, 1 - y))
    return x, y, c, chips


def _remote(src, dst, ssem, rsem, dev):
    return pltpu.make_async_remote_copy(src_ref=src, dst_ref=dst, send_sem=ssem, recv_sem=rsem,
                                        device_id=dev, device_id_type=MESH_T)


def _all_gather_weights(shards):
    nt = len(shards)

    def body(*refs):
        srcs, dsts = refs[:nt], refs[nt:2 * nt]
        loc, isend, irecv, dsend, drecv = refs[2 * nt:]
        x, y, c, chips = _place()
        me = 2 * x + y
        sib = (x, y, 1 - c)
        locals_, sends = [], []
        for t in range(nt):
            half = srcs[t].shape[0] // 2
            mine = pl.ds(pl.multiple_of(c * half, 16), half)
            cp = pltpu.make_async_copy(srcs[t], dsts[t].at[me], loc.at[t])
            cp.start()
            locals_.append(cp)
            for j, (px, py) in enumerate(chips):
                cp = _remote(srcs[t].at[mine], dsts[t].at[me, mine], isend.at[t, j], irecv.at[t, j], (px, py, c))
                cp.start()
                sends.append(cp)
        for t in range(nt):
            half = srcs[t].shape[0] // 2
            mine = pl.ds(pl.multiple_of(c * half, 16), half)
            for j, (px, py) in enumerate(chips):
                blk = dsts[t].at[2 * px + py, mine]
                _remote(blk, blk, isend.at[t, j], irecv.at[t, j], (px, py, c)).wait_recv()
                cp = _remote(blk, blk, dsend.at[t, j], drecv.at[t, j], sib)
                cp.start()
                sends.append(cp)
        for t in range(nt):
            half = srcs[t].shape[0] // 2
            other = pl.ds(pl.multiple_of((1 - c) * half, 16), half)
            for j, (px, py) in enumerate(chips):
                blk = dsts[t].at[2 * px + py, other]
                _remote(blk, blk, dsend.at[t, j], drecv.at[t, j], sib).wait_recv()
        for cp in sends:
            cp.wait_send()
        for cp in locals_:
            cp.wait()

    return pl.pallas_call(
        body, name="allgather_weights",
        in_specs=[ANY] * nt, out_specs=[ANY] * nt,
        out_shape=[jax.ShapeDtypeStruct((NSH,) + a.shape, a.dtype) for a in shards],
        scratch_shapes=[pltpu.SemaphoreType.DMA((nt,)), pltpu.SemaphoreType.DMA((nt, 3)),
                        pltpu.SemaphoreType.DMA((nt, 3)), pltpu.SemaphoreType.DMA((nt, 3)),
                        pltpu.SemaphoreType.DMA((nt, 3))],
        compiler_params=_cp())(*shards)


def _to_sibling(arrs, name):
    nt = len(arrs)

    def body(*refs):
        srcs, dsts = refs[:nt], refs[nt:2 * nt]
        ssem, rsem = refs[2 * nt:]
        x, y, c, _ = _place()
        cps = [_remote(srcs[t], dsts[t], ssem.at[t], rsem.at[t], (x, y, 1 - c)) for t in range(nt)]
        for cp in cps:
            cp.start()
        for cp in cps:
            cp.wait()

    return pl.pallas_call(
        body, name=name, in_specs=[ANY] * nt, out_specs=[ANY] * nt,
        out_shape=[jax.ShapeDtypeStruct(a.shape, a.dtype) for a in arrs],
        scratch_shapes=[pltpu.SemaphoreType.DMA((nt,)), pltpu.SemaphoreType.DMA((nt,))],
        compiler_params=_cp())(*arrs)


def _scatter_to_chips(arrs):
    nt = len(arrs)

    def body(*refs):
        srcs, dsts = refs[:nt], refs[nt:2 * nt]
        ssem, rsem = refs[2 * nt:]
        x, y, c, chips = _place()
        cps = []
        for t in range(nt):
            for j, (px, py) in enumerate(chips):
                cps.append(_remote(srcs[t].at[2 * px + py], dsts[t].at[j], ssem.at[t, j], rsem.at[t, j], (px, py, c)))
        for cp in cps:
            cp.start()
        for cp in cps:
            cp.wait()

    return pl.pallas_call(
        body, name="scatter_to_chips", in_specs=[ANY] * nt, out_specs=[ANY] * nt,
        out_shape=[jax.ShapeDtypeStruct((3,) + a.shape[1:], a.dtype) for a in arrs],
        scratch_shapes=[pltpu.SemaphoreType.DMA((nt, 3)), pltpu.SemaphoreType.DMA((nt, 3))],
        compiler_params=_cp())(*arrs)


def _halves(g):
    return g.reshape(NSH, 2, g.shape[1] // 2, g.shape[2])


def _cast_other_half(grads, oc):
    nt = len(grads)

    def body(oc_ref, *refs):
        for t in range(nt):
            refs[nt + t][...] = refs[t][...].astype(BF16)

    ins = [_halves(g) for g in grads]
    in_specs = [pl.BlockSpec((None, None) + a.shape[2:], lambda k, oc_ref: (k, oc_ref[0], 0, 0)) for a in ins]
    out_specs = [pl.BlockSpec((None,) + a.shape[2:], lambda k, oc_ref: (k, 0, 0)) for a in ins]
    return pl.pallas_call(
        body, name="rs_cast_other_half",
        grid_spec=pltpu.PrefetchScalarGridSpec(num_scalar_prefetch=1, grid=(NSH,), in_specs=in_specs,
                                               out_specs=out_specs),
        out_shape=[jax.ShapeDtypeStruct((NSH,) + a.shape[2:], BF16) for a in ins],
        compiler_params=_cp(1))(oc, *ins)


def _chip_partial(grads, recv, cs):
    nt = len(grads)

    def body(cs_ref, *refs):
        k = pl.program_id(0)
        for t in range(nt):
            p = refs[t][...] + refs[nt + t][...].astype(F32)
            refs[2 * nt + t][...] = p.astype(BF16)

            @pl.when(k == cs_ref[1])
            def _():
                refs[3 * nt + t][...] = p

    ins = [_halves(g) for g in grads]
    g_specs = [pl.BlockSpec((None, None) + a.shape[2:], lambda k, cs_ref: (k, cs_ref[0], 0, 0)) for a in ins]
    r_specs = [pl.BlockSpec((None,) + a.shape[2:], lambda k, cs_ref: (k, 0, 0)) for a in ins]
    own_specs = [pl.BlockSpec(a.shape[2:], lambda k, cs_ref: (0, 0)) for a in ins]
    return pl.pallas_call(
        body, name="rs_chip_partial",
        grid_spec=pltpu.PrefetchScalarGridSpec(num_scalar_prefetch=1, grid=(NSH,), in_specs=g_specs + r_specs,
                                               out_specs=r_specs + own_specs),
        out_shape=[jax.ShapeDtypeStruct((NSH,) + a.shape[2:], BF16) for a in ins]
        + [jax.ShapeDtypeStruct(a.shape[2:], F32) for a in ins],
        compiler_params=_cp(1))(cs, *ins, *recv)


def _sum_chips(own, recv):
    nt = len(own)

    def body(*refs):
        for t in range(nt):
            r = refs[nt + t]
            refs[2 * nt + t][...] = ((refs[t][...] + r[0].astype(F32)) + r[1].astype(F32)) + r[2].astype(F32)

    vm = pl.BlockSpec(memory_space=pltpu.VMEM)
    return pl.pallas_call(
        body, name="rs_sum_chips", in_specs=[vm] * (2 * nt), out_specs=[vm] * nt,
        out_shape=[jax.ShapeDtypeStruct(a.shape, F32) for a in own],
        compiler_params=_cp())(*own, *recv)


def _adamw_math(w, g, m, v):
    m = ADAM_B1 * m + (1.0 - ADAM_B1) * g
    v = ADAM_B2 * v + (1.0 - ADAM_B2) * (g * g)
    m_hat = m / (1.0 - ADAM_B1 ** ADAM_STEP)
    v_hat = v / (1.0 - ADAM_B2 ** ADAM_STEP)
    delta = -ADAM_LR * (m_hat / (jnp.sqrt(v_hat) + ADAM_EPS) + ADAM_WD * w)
    return delta, m, v


ADAM_SPLIT = 4


def _adamw_shards(own, sib, ws, ms, vs, c_arr):
    nt = len(own)

    def body(c_ref, *refs):
        hh = pl.program_id(0)
        mine = hh == c_ref[0]
        for t in range(nt):
            g = jnp.where(mine, refs[t][...], refs[nt + t][...])
            delta, m, v = _adamw_math(refs[2 * nt + t][...], g, refs[3 * nt + t][...], refs[4 * nt + t][...])
            refs[5 * nt + t][...] = g
            refs[6 * nt + t][...] = delta
            refs[7 * nt + t][...] = m
            refs[8 * nt + t][...] = v

    def tile(a):
        return (a.shape[0] // ADAM_SPLIT, a.shape[1])

    half_specs = [pl.BlockSpec(tile(a), lambda hh, q, c_ref: (q, 0)) for a in own]
    full_specs = [pl.BlockSpec(tile(a), lambda hh, q, c_ref: (hh * ADAM_SPLIT + q, 0)) for a in own]
    return pl.pallas_call(
        body, name="adamw_shards",
        grid_spec=pltpu.PrefetchScalarGridSpec(num_scalar_prefetch=1, grid=(2, ADAM_SPLIT),
                                               in_specs=half_specs * 2 + full_specs * 3, out_specs=full_specs * 4),
        out_shape=[jax.ShapeDtypeStruct(w.shape, F32) for w in ws] * 4,
        compiler_params=_cp(2))(c_arr, *own, *sib, *ws, *ms, *vs)


def _small_allreduce_adamw(gp, wp, mp, vp):
    rows = gp.shape[0]

    def body(g_ref, w_ref, m_ref, v_ref, go_ref, d_ref, mo_ref, vo_ref, gather, ssem, rsem):
        x, y, c, _ = _place()
        me = 4 * x + 2 * y + c
        gather[me] = g_ref[...]
        cps = []
        for kk in range(1, 8):
            px, py, pc = x ^ (kk >> 2), y ^ ((kk >> 1) & 1), c ^ (kk & 1)
            cp = _remote(g_ref, gather.at[me], ssem.at[kk], rsem.at[kk], (px, py, pc))
            cp.start()
            cps.append(cp)
        for kk in range(1, 8):
            px, py, pc = x ^ (kk >> 2), y ^ ((kk >> 1) & 1), c ^ (kk & 1)
            _remote(g_ref, gather.at[4 * px + 2 * py + pc], ssem.at[kk], rsem.at[kk], (px, py, pc)).wait_recv()
        for cp in cps:
            cp.wait_send()
        g = gather[0]
        for d in range(1, 8):
            g = g + gather[d]
        delta, m, v = _adamw_math(w_ref[...], g, m_ref[...], v_ref[...])
        go_ref[...] = g
        d_ref[...] = delta
        mo_ref[...] = m
        vo_ref[...] = v

    vm = pl.BlockSpec(memory_space=pltpu.VMEM)
    return pl.pallas_call(
        body, name="small_allreduce_adamw", in_specs=[vm] * 4, out_specs=[vm] * 4,
        out_shape=[jax.ShapeDtypeStruct(gp.shape, F32)] * 4,
        scratch_shapes=[pltpu.VMEM((8, rows, 128), F32), pltpu.SemaphoreType.DMA((8,)),
                        pltpu.SemaphoreType.DMA((8,))],
        compiler_params=_cp())(gp, wp, mp, vp)


SMALL_PARTS = (("g1", (1, D)), ("g2", (1, D)), ("g3", (1, D)), ("g4", (1, D)), ("w_pool", (1, 4, GROUP, GROUP)),
               ("pool_scale", (1, POOL_W)), ("rel_bias", (NBUCKET, NQ)), ("sinks", (1, NQ)))


def _pack_small(parts):
    rows = []
    for a in parts:
        flat = a.reshape(-1)
        n = flat.shape[0]
        padded = -(-n // 1024) * 1024
        rows.append(jnp.pad(flat, (0, padded - n)).reshape(-1, 128))
    return jnp.concatenate(rows, axis=0)


def _unpack_small(packed):
    out, r = [], 0
    for _, shape in SMALL_PARTS:
        n = int(np.prod(shape))
        nr = -(-n // 1024) * 8
        out.append(packed[r:r + nr].reshape(-1)[:n].reshape(shape))
        r += nr
    return out


def kernel(x, g_pre_mix, w_in, w_pool, pool_scale, rel_bias, sinks, w_out, g_post_mix, g_pre_ffn, w_gate, w_up, w_down, g_post_ffn, loss_target, m_g_pre_mix, m_w_in, m_w_pool, m_pool_scale, m_rel_bias, m_sinks, m_w_out, m_g_post_mix, m_g_pre_ffn, m_w_gate, m_w_up, m_w_down, m_g_post_ffn, v_g_pre_mix, v_w_in, v_w_pool, v_pool_scale, v_rel_bias, v_sinks, v_w_out, v_g_post_mix, v_g_pre_ffn, v_w_gate, v_w_up, v_w_down, v_g_post_ffn):
    c = lax.axis_index("c")
    chip = 2 * lax.axis_index("x") + lax.axis_index("y")

    def shards(a_in, a_out, a_gate, a_up, a_down):
        return (a_in[0].T, a_out[0], a_gate[0].T, a_up[0].T, a_down[0])

    big_w = shards(w_in, w_out, w_gate, w_up, w_down)
    win_all, wout_all, wg_all, wu_all, wd_all = _all_gather_weights([w.astype(BF16) for w in big_w])
    win_full = win_all.reshape(IN_W, D)
    wout_full = wout_all.reshape(D, D)

    small = (g_pre_mix, g_post_mix, g_pre_ffn, g_post_ffn, w_pool[0], pool_scale, rel_bias, sinks)
    loss, gx, small_grads, (gwin, gwout, gwg, gwu, gwd) = _local_step(
        x[0], loss_target[0], small, win_full, wout_full, wg_all, wu_all, wd_all)

    grads = [gwin.reshape(NSH, WIN_S, D), gwout.reshape(NSH, WOUT_S, D), gwg, gwu, gwd]
    send_a = _cast_other_half(grads, (1 - c).reshape(1).astype(jnp.int32))
    recv_a = _to_sibling(send_a, "rs_sibling_exchange")
    outs = _chip_partial(grads, recv_a, jnp.stack([c, chip]).astype(jnp.int32))
    send_b, own = outs[:5], outs[5:]
    recv_b = _scatter_to_chips(send_b)
    final_half = _sum_chips(own, recv_b)
    sib_half = _to_sibling(final_half, "rs_sibling_share")
    adam = _adamw_shards(final_half, sib_half, big_w, shards(m_w_in, m_w_out, m_w_gate, m_w_up, m_w_down),
                         shards(v_w_in, v_w_out, v_w_gate, v_w_up, v_w_down), c.reshape(1).astype(jnp.int32))
    big_g, big_d, big_m, big_v = adam[0:5], adam[5:10], adam[10:15], adam[15:20]

    gp = _pack_small(small_grads)
    wp = _pack_small((g_pre_mix, g_post_mix, g_pre_ffn, g_post_ffn, w_pool, pool_scale, rel_bias, sinks))
    mp = _pack_small((m_g_pre_mix, m_g_post_mix, m_g_pre_ffn, m_g_post_ffn, m_w_pool, m_pool_scale, m_rel_bias,
                      m_sinks))
    vp = _pack_small((v_g_pre_mix, v_g_post_mix, v_g_pre_ffn, v_g_post_ffn, v_w_pool, v_pool_scale, v_rel_bias,
                      v_sinks))
    small_out = [_unpack_small(p) for p in _small_allreduce_adamw(gp, wp, mp, vp)]

    total_loss = lax.psum(loss[0, 0], ("x", "y", "c"))

    def ordered(small4, big4):
        sg1, sg2, sg3, sg4, swp, ssc, srb, ssk = small4
        bwin, bwout, bwg, bwu, bwd = big4[0].T[None], big4[1][None], big4[2].T[None], big4[3].T[None], big4[4][None]
        return [sg1, bwin, swp, ssc, srb, ssk, bwout, sg2, sg3, bwg, bwu, bwd, sg4]

    res = [total_loss, gx[None]]
    for sm, bg in zip(small_out, (big_g, big_d, big_m, big_v)):
        res += ordered(sm, bg)
    return tuple(res)
```

```python
import functools

import numpy as np
import jax
import jax.numpy as jnp
from jax import lax
from jax.experimental import pallas as pl
from jax.experimental.pallas import tpu as pltpu

F32 = jnp.float32
BF16 = jnp.bfloat16

D = 1024
POOL_W = 512
GROUP = 128
WINDOWS = (2, 4, 8, 16)
HALO = 128
NQ = 8
BLK = 128
NBUCKET = 32
IN_W = 1280
DFF = 2816
NSH = 4
FS = DFF // NSH
WIN_S = IN_W // NSH
WOUT_S = D // NSH
EPS = 1e-6
NEG = -1e30
SCALE = 0.125

ADAM_LR = 0.001
ADAM_B1 = 0.9
ADAM_B2 = 0.999
ADAM_EPS = 1e-08
ADAM_WD = 0.01
ADAM_STEP = 10

TM = 512
TM_POOL = 256
TM_FFN = 512
VMEM_LIMIT = 56 * 1024 * 1024

MESH_T = pl.DeviceIdType.MESH
ANY = pl.BlockSpec(memory_space=pl.ANY)


def _cp(n_grid=0, **kw):
    sem = ("arbitrary",) * n_grid if n_grid else None
    return pltpu.CompilerParams(dimension_semantics=sem, vmem_limit_bytes=VMEM_LIMIT, **kw)


def _dot(a, b):
    return jnp.dot(a, b, preferred_element_type=F32)


def _dot_nt(a, b):
    return lax.dot_general(a, b, (((1,), (1,)), ((), ())), preferred_element_type=F32)


def _dot_tn(a, b):
    return lax.dot_general(a, b, (((0,), (0,)), ((), ())), preferred_element_type=F32)


def _rms(x):
    r = lax.rsqrt(jnp.mean(x * x, axis=-1, keepdims=True) + EPS)
    return r, x * r


def _rms_bwd(r, n, g, dout):
    dn = dout * g
    dx = r * (dn - n * jnp.mean(dn * n, axis=-1, keepdims=True))
    dg = jnp.sum(dout * n, axis=0, keepdims=True)
    return dx, dg


def _split(x, n):
    parts = []
    r = x
    for _ in range(n):
        p = r.astype(BF16)
        parts.append(p)
        r = r - p.astype(F32)
    return parts


def _band_dot(a, x, n):
    acc = None
    for p in _split(x, n):
        t = _dot(a, p)
        acc = t if acc is None else acc + t
    return acc


def _bucket_table():
    qi = np.arange(BLK)[:, None]
    kj = np.arange(2 * BLK)[None, :]
    dist = qi + BLK - kj
    n = np.maximum(dist, 0)
    nf = np.maximum(n, 1).astype(np.float32)
    large = 16 + (np.log(nf / np.float32(16)) / np.float32(np.log(128 / 16)) * np.float32(16)).astype(np.int32)
    large = np.minimum(large, NBUCKET - 1)
    return np.where(n < 16, n, large).astype(np.int32)


def _inproj_fwd(x, g1, w_in):
    s = x.shape[0]
    tm = min(TM, s)

    def body(x_ref, g_ref, w_ref, u_ref, q_ref, k_ref, v_ref):
        _, n = _rms(x_ref[...])
        h = (n * g_ref[...]).astype(BF16)
        proj = _dot_nt(h, w_ref[...])
        u_ref[...] = proj[:, :512]
        q_ref[...] = proj[:, 512:1024].astype(BF16)
        k_ref[...] = proj[:, 1024:1152].astype(BF16)
        v_ref[...] = proj[:, 1152:1280].astype(BF16)

    row = lambda w: pl.BlockSpec((tm, w), lambda i: (i, 0))
    return pl.pallas_call(
        body, name="inproj_fwd", grid=(s // tm,),
        in_specs=[row(D), pl.BlockSpec((1, D), lambda i: (0, 0)), pl.BlockSpec((IN_W, D), lambda i: (0, 0))],
        out_specs=[row(512), row(512), row(128), row(128)],
        out_shape=[jax.ShapeDtypeStruct((s, 512), F32), jax.ShapeDtypeStruct((s, 512), BF16),
                   jax.ShapeDtypeStruct((s, 128), BF16), jax.ShapeDtypeStruct((s, 128), BF16)],
        compiler_params=_cp(1))(x, g1, w_in)


def _pooled(ext, cur, t0, tm):
    rows = lax.broadcasted_iota(jnp.int32, (tm, tm + HALO), 0)
    cols = lax.broadcasted_iota(jnp.int32, (tm, tm + HALO), 1)
    d = rows + HALO - cols
    t = t0 + lax.broadcasted_iota(jnp.int32, (tm, GROUP), 0)
    out = []
    for g, w in enumerate(WINDOWS):
        a = jnp.where((d >= 0) & (d < w), 1.0, 0.0).astype(BF16)
        ws = _band_dot(a, ext[:, g * GROUP:(g + 1) * GROUP], 3)
        cnt = jnp.minimum(t + 1, w).astype(F32)
        out.append(ws / cnt - cur[:, g * GROUP:(g + 1) * GROUP])
    return out


def _pool_fwd(u, w_pool, pool_scale):
    s = u.shape[0]
    tm = min(TM_POOL, s)
    hb = tm // HALO

    def body(uc_ref, uh_ref, wp_ref, sc_ref, o_ref):
        i = pl.program_id(0)
        cur = uc_ref[...]
        halo = jnp.where(i > 0, uh_ref[...], 0.0)
        ext = jnp.concatenate([halo, cur], axis=0)
        pooled = _pooled(ext, cur, i * tm, tm)
        for g in range(4):
            sl = slice(g * GROUP, (g + 1) * GROUP)
            mixed = _dot(pooled[g].astype(BF16), wp_ref[g])
            o_ref[:, sl] = (mixed * sc_ref[:, sl]).astype(BF16)

    return pl.pallas_call(
        body, name="pool_fwd", grid=(s // tm,),
        in_specs=[pl.BlockSpec((tm, 512), lambda i: (i, 0)),
                  pl.BlockSpec((HALO, 512), lambda i: (jnp.maximum(i * hb - 1, 0), 0)),
                  pl.BlockSpec((4, GROUP, GROUP), lambda i: (0, 0, 0)),
                  pl.BlockSpec((1, 512), lambda i: (0, 0))],
        out_specs=pl.BlockSpec((tm, 512), lambda i: (i, 0)),
        out_shape=jax.ShapeDtypeStruct((s, 512), BF16),
        compiler_params=_cp(1))(u, u, w_pool, pool_scale)


def _bias_table(bucket, rel_bias):
    def body(b_ref, rb_ref, o_ref):
        bucket_v = b_ref[...]
        rows = lax.broadcasted_iota(jnp.int32, (BLK, 2 * BLK), 0)
        cols = lax.broadcasted_iota(jnp.int32, (BLK, 2 * BLK), 1)
        dist = rows + BLK - cols
        inwin = (dist >= 0) & (dist < BLK)
        for h in range(NQ):
            acc = jnp.zeros((BLK, 2 * BLK), F32)
            for b in range(NBUCKET):
                acc = jnp.where(bucket_v == b, rb_ref[b, h], acc)
            rest = jnp.where(inwin, acc, NEG)
            o_ref[1, h] = rest
            o_ref[0, h] = jnp.where(cols >= BLK, rest, NEG)

    return pl.pallas_call(
        body, name="bias_table",
        in_specs=[pl.BlockSpec(memory_space=pltpu.VMEM), pl.BlockSpec(memory_space=pltpu.SMEM)],
        out_specs=pl.BlockSpec(memory_space=pltpu.VMEM),
        out_shape=jax.ShapeDtypeStruct((2, NQ, BLK, 2 * BLK), F32),
        compiler_params=_cp())(bucket, rel_bias)


def _kv_bands(k_ref, v_ref, n):
    pstart = pl.multiple_of(jnp.maximum(n - 1, 0) * BLK, BLK)
    cstart = pl.multiple_of(n * BLK, BLK)
    lo = lax.broadcasted_iota(jnp.int32, (2 * BLK, 128), 1) < 64
    out = []
    for ref in (k_ref, v_ref):
        band = jnp.concatenate([ref[pl.ds(pstart, BLK), :], ref[pl.ds(cstart, BLK), :]], axis=0).astype(F32)
        rot = pltpu.roll(band, 64, axis=1)
        out.append((jnp.where(lo, band, rot).astype(BF16), jnp.where(lo, rot, band).astype(BF16)))
    return out[0], out[1], lo, pstart, cstart


def _attn_probs(qm, kk, bias, sink):
    s = _dot_nt(qm, kk) + bias
    m = jnp.maximum(jnp.max(s, axis=-1, keepdims=True), sink)
    p = jnp.exp(s - m)
    es = jnp.exp(sink - m)
    inv = 1.0 / (jnp.sum(p, axis=-1, keepdims=True) + es)
    return p * inv, es * inv


def _attn_fwd(q, k, v, bias, sinks):
    s = q.shape[0]

    def body(q_ref, k_ref, v_ref, b_ref, sk_ref, o_ref):
        n = pl.program_id(0)
        kk, vv, lo, _, _ = _kv_bands(k_ref, v_ref, n)
        bidx = jnp.minimum(n, 1)
        lo_q = lax.broadcasted_iota(jnp.int32, (BLK, 128), 1) < 64
        for p in range(4):
            h = p // 2
            qt = q_ref[:, p * 128:(p + 1) * 128].astype(F32) * SCALE
            acc = jnp.zeros((BLK, 128), F32)
            for e in range(2):
                head = 2 * p + e
                sel_q = lo_q if e == 0 else jnp.logical_not(lo_q)
                sel_kv = lo if e == 0 else jnp.logical_not(lo)
                qm = jnp.where(sel_q, qt, 0.0).astype(BF16)
                prob, _ = _attn_probs(qm, kk[h], b_ref[bidx, head], sk_ref[0, head])
                vm = jnp.where(sel_kv, vv[h], jnp.zeros_like(vv[h]))
                acc = acc + _dot(prob.astype(BF16), vm)
            o_ref[:, p * 128:(p + 1) * 128] = acc.astype(BF16)

    return pl.pallas_call(
        body, name="attn_fwd", grid=(s // BLK,),
        in_specs=[pl.BlockSpec((BLK, 512), lambda i: (i, 0)),
                  pl.BlockSpec((s, 128), lambda i: (0, 0)),
                  pl.BlockSpec((s, 128), lambda i: (0, 0)),
                  pl.BlockSpec((2, NQ, BLK, 2 * BLK), lambda i: (0, 0, 0, 0)),
                  pl.BlockSpec(memory_space=pltpu.SMEM)],
        out_specs=pl.BlockSpec((BLK, 512), lambda i: (i, 0)),
        out_shape=jax.ShapeDtypeStruct((s, 512), BF16),
        compiler_params=_cp(1))(q, k, v, bias, sinks)


def _outproj_fwd(pool_o, attn_o, w_out, x, g2, g3):
    s = x.shape[0]
    tm = min(TM, s)

    def body(p_ref, a_ref, w_ref, x_ref, g2_ref, g3_ref, mix_ref, x1_ref, h2_ref):
        mix = _dot(p_ref[...], w_ref[0:512, :]) + _dot(a_ref[...], w_ref[512:1024, :])
        mix_ref[...] = mix
        _, n2 = _rms(mix)
        x1 = x_ref[...] + n2 * g2_ref[...]
        x1_ref[...] = x1
        _, n3 = _rms(x1)
        h2_ref[...] = (n3 * g3_ref[...]).astype(BF16)

    row = lambda w: pl.BlockSpec((tm, w), lambda i: (i, 0))
    vec = pl.BlockSpec((1, D), lambda i: (0, 0))
    return pl.pallas_call(
        body, name="outproj_fwd", grid=(s // tm,),
        in_specs=[row(512), row(512), pl.BlockSpec((D, D), lambda i: (0, 0)), row(D), vec, vec],
        out_specs=[row(D), row(D), row(D)],
        out_shape=[jax.ShapeDtypeStruct((s, D), F32), jax.ShapeDtypeStruct((s, D), F32),
                   jax.ShapeDtypeStruct((s, D), BF16)],
        compiler_params=_cp(1))(pool_o, attn_o, w_out, x, g2, g3)


def _silu_parts(g):
    sg = jax.nn.sigmoid(g)
    return sg, g * sg


def _ffn_fwd(h2, wg, wu, wd, x1, target, g4):
    s = h2.shape[0]
    tm = min(TM_FFN, s)

    def body(h_ref, wg_ref, wu_ref, wd_ref, x1_ref, t_ref, g4_ref,
             gate_ref, up_ref, df_ref, dy_ref, loss_ref, gg4_ref, f_acc):
        i = pl.program_id(0)
        j = pl.program_id(1)
        h = h_ref[...]
        gate = _dot_nt(h, wg_ref[...])
        up = _dot_nt(h, wu_ref[...])
        gate_ref[...] = gate.astype(BF16)
        up_ref[...] = up.astype(BF16)
        _, sl = _silu_parts(gate)
        contrib = _dot((sl * up).astype(BF16), wd_ref[...])

        @pl.when(j == 0)
        def _():
            f_acc[...] = contrib

        @pl.when(j > 0)
        def _():
            f_acc[...] += contrib

        @pl.when((i == 0) & (j == 0))
        def _():
            loss_ref[...] = jnp.zeros_like(loss_ref)
            gg4_ref[...] = jnp.zeros_like(gg4_ref)

        @pl.when(j == NSH - 1)
        def _():
            r4, n4 = _rms(f_acc[...])
            g4v = g4_ref[...]
            err = x1_ref[...] + n4 * g4v - t_ref[...]
            loss_ref[...] += (0.5 / D) * jnp.sum(err * err).reshape(1, 1)
            dy = err * (1.0 / D)
            dy_ref[...] = dy
            df, dg = _rms_bwd(r4, n4, g4v, dy)
            gg4_ref[...] += dg
            df_ref[...] = df.astype(BF16)

    row = lambda w: pl.BlockSpec((tm, w), lambda i, j: (i, 0))
    sh = lambda r, c: pl.BlockSpec((None, r, c), lambda i, j: (j, 0, 0))
    act = pl.BlockSpec((None, tm, FS), lambda i, j: (j, i, 0))
    vec = pl.BlockSpec((1, D), lambda i, j: (0, 0))
    return pl.pallas_call(
        body, name="ffn_fwd", grid=(s // tm, NSH),
        in_specs=[row(D), sh(FS, D), sh(FS, D), sh(FS, D), row(D), row(D), vec],
        out_specs=[act, act, row(D), row(D), pl.BlockSpec((1, 1), lambda i, j: (0, 0)), vec],
        out_shape=[jax.ShapeDtypeStruct((NSH, s, FS), BF16), jax.ShapeDtypeStruct((NSH, s, FS), BF16),
                   jax.ShapeDtypeStruct((s, D), BF16), jax.ShapeDtypeStruct((s, D), F32),
                   jax.ShapeDtypeStruct((1, 1), F32), jax.ShapeDtypeStruct((1, D), F32)],
        scratch_shapes=[pltpu.VMEM((tm, D), F32)],
        compiler_params=_cp(2))(h2, wg, wu, wd, x1, target, g4)


def _ffn_bwd_act(df, gate, up, wg, wu, wd, dy, x1, mix, g3, g2):
    s = df.shape[0]
    tm = min(TM_FFN, s)

    def body(df_ref, gate_ref, up_ref, wg_ref, wu_ref, wd_ref, dy_ref, x1_ref, mix_ref, g3_ref, g2_ref,
             dgate_ref, dup_ref, dx1_ref, dmix_ref, gg3_ref, gg2_ref, acc):
        i = pl.program_id(0)
        j = pl.program_id(1)
        da = _dot_nt(df_ref[...], wd_ref[...])
        g = gate_ref[...].astype(F32)
        u = up_ref[...].astype(F32)
        sg, sl = _silu_parts(g)
        dgate = (da * u * (sg * (1.0 + g * (1.0 - sg)))).astype(BF16)
        dup = (da * sl).astype(BF16)
        dgate_ref[...] = dgate
        dup_ref[...] = dup
        contrib = _dot(dgate, wg_ref[...]) + _dot(dup, wu_ref[...])

        @pl.when(j == 0)
        def _():
            acc[...] = contrib

        @pl.when(j > 0)
        def _():
            acc[...] += contrib

        @pl.when((i == 0) & (j == 0))
        def _():
            gg3_ref[...] = jnp.zeros_like(gg3_ref)
            gg2_ref[...] = jnp.zeros_like(gg2_ref)

        @pl.when(j == NSH - 1)
        def _():
            r3, n3 = _rms(x1_ref[...])
            dx1n, dg3 = _rms_bwd(r3, n3, g3_ref[...], acc[...])
            dx1 = dy_ref[...] + dx1n
            dx1_ref[...] = dx1
            gg3_ref[...] += dg3
            r2, n2 = _rms(mix_ref[...])
            dmix, dg2 = _rms_bwd(r2, n2, g2_ref[...], dx1)
            gg2_ref[...] += dg2
            dmix_ref[...] = dmix.astype(BF16)

    row = lambda w: pl.BlockSpec((tm, w), lambda i, j: (i, 0))
    sh = lambda r, c: pl.BlockSpec((None, r, c), lambda i, j: (j, 0, 0))
    act = pl.BlockSpec((None, tm, FS), lambda i, j: (j, i, 0))
    vec = pl.BlockSpec((1, D), lambda i, j: (0, 0))
    return pl.pallas_call(
        body, name="ffn_bwd_act", grid=(s // tm, NSH),
        in_specs=[row(D), act, act, sh(FS, D), sh(FS, D), sh(FS, D), row(D), row(D), row(D), vec, vec],
        out_specs=[act, act, row(D), row(D), vec, vec],
        out_shape=[jax.ShapeDtypeStruct((NSH, s, FS), BF16), jax.ShapeDtypeStruct((NSH, s, FS), BF16),
                   jax.ShapeDtypeStruct((s, D), F32), jax.ShapeDtypeStruct((s, D), BF16),
                   jax.ShapeDtypeStruct((1, D), F32), jax.ShapeDtypeStruct((1, D), F32)],
        scratch_shapes=[pltpu.VMEM((tm, D), F32)],
        compiler_params=_cp(2))(df, gate, up, wg, wu, wd, dy, x1, mix, g3, g2)


def _ffn_bwd_w(h2, gate, up, dgate, dup, df):
    s = h2.shape[0]
    tm = min(TM_FFN, s)

    def body(h_ref, gate_ref, up_ref, dgate_ref, dup_ref, df_ref, gwg_ref, gwu_ref, gwd_ref):
        i = pl.program_id(1)

        @pl.when(i == 0)
        def _():
            gwg_ref[...] = jnp.zeros_like(gwg_ref)
            gwu_ref[...] = jnp.zeros_like(gwu_ref)
            gwd_ref[...] = jnp.zeros_like(gwd_ref)

        h = h_ref[...]
        gwg_ref[...] += _dot_tn(dgate_ref[...], h)
        gwu_ref[...] += _dot_tn(dup_ref[...], h)
        _, sl = _silu_parts(gate_ref[...].astype(F32))
        a = (sl * up_ref[...].astype(F32)).astype(BF16)
        gwd_ref[...] += _dot_tn(a, df_ref[...])

    row = lambda w: pl.BlockSpec((tm, w), lambda j, i: (i, 0))
    act = pl.BlockSpec((None, tm, FS), lambda j, i: (j, i, 0))
    sh = lambda r, c: pl.BlockSpec((None, r, c), lambda j, i: (j, 0, 0))
    return pl.pallas_call(
        body, name="ffn_bwd_w", grid=(NSH, s // tm),
        in_specs=[row(D), act, act, act, act, row(D)],
        out_specs=[sh(FS, D), sh(FS, D), sh(FS, D)],
        out_shape=[jax.ShapeDtypeStruct((NSH, FS, D), F32)] * 3,
        compiler_params=_cp(2))(h2, gate, up, dgate, dup, df)


def _outproj_bwd(dmix, w_out, pool_o, attn_o, dep=None):
    s = dmix.shape[0]
    tm = min(TM, s)

    def body(dm_ref, w_ref, p_ref, a_ref, *rest):
        dpool_ref, dattn_ref, gw_ref = rest[-3:]
        i = pl.program_id(0)

        @pl.when(i == 0)
        def _():
            gw_ref[...] = jnp.zeros_like(gw_ref)

        dm = dm_ref[...]
        dpool_ref[...] = _dot_nt(dm, w_ref[0:512, :])
        dattn_ref[...] = _dot_nt(dm, w_ref[512:1024, :]).astype(BF16)
        gw_ref[0:512, :] += _dot_tn(p_ref[...], dm)
        gw_ref[512:1024, :] += _dot_tn(a_ref[...], dm)

    row = lambda w: pl.BlockSpec((tm, w), lambda i: (i, 0))
    full = pl.BlockSpec((D, D), lambda i: (0, 0))
    return pl.pallas_call(
        body, name="outproj_bwd", grid=(s // tm,),
        in_specs=[row(D), full, row(512), row(512)] + ([] if dep is None else [ANY]),
        out_specs=[row(512), row(512), full],
        out_shape=[jax.ShapeDtypeStruct((s, 512), F32), jax.ShapeDtypeStruct((s, 512), BF16),
                   jax.ShapeDtypeStruct((D, D), F32)],
        compiler_params=_cp(1))(dmix, w_out, pool_o, attn_o, *([] if dep is None else [dep]))


def _pool_bwd(u, dpool, w_pool, pool_scale):
    s = u.shape[0]
    tm = min(TM_POOL, s)
    hb = tm // HALO
    nt = s // tm
    last_halo = s // HALO - 1

    def body(uc_ref, uh_ref, dc_ref, dn_ref, wp_ref, sc_ref, du_ref, gwp_ref, gsc_ref):
        i = pl.program_id(0)

        @pl.when(i == 0)
        def _():
            gwp_ref[...] = jnp.zeros_like(gwp_ref)
            gsc_ref[...] = jnp.zeros_like(gsc_ref)

        cur = uc_ref[...]
        halo = jnp.where(i > 0, uh_ref[...], 0.0)
        pooled = _pooled(jnp.concatenate([halo, cur], axis=0), cur, i * tm, tm)
        dcur = dc_ref[...]
        dnext = jnp.where(i < nt - 1, dn_ref[...], 0.0)
        dext = jnp.concatenate([dcur, dnext], axis=0)
        rows = lax.broadcasted_iota(jnp.int32, (tm, tm + HALO), 0)
        cols = lax.broadcasted_iota(jnp.int32, (tm, tm + HALO), 1)
        d = cols - rows
        t_ext = i * tm + lax.broadcasted_iota(jnp.int32, (tm + HALO, GROUP), 0)
        for g, w in enumerate(WINDOWS):
            sl = slice(g * GROUP, (g + 1) * GROUP)
            pb = pooled[g].astype(BF16)
            wp = wp_ref[g]
            mixed = _dot(pb, wp)
            gsc_ref[:, sl] += jnp.sum(dcur[:, sl] * mixed, axis=0, keepdims=True)
            dmixed = (dext[:, sl] * sc_ref[:, sl]).astype(BF16)
            gwp_ref[g] += _dot_tn(pb, dmixed[0:tm])
            dpooled = _dot_nt(dmixed, wp)
            z = dpooled / jnp.minimum(t_ext + 1, w).astype(F32)
            b = jnp.where((d >= 0) & (d < w), 1.0, 0.0).astype(BF16)
            du_ref[:, sl] = (_band_dot(b, z, 2) - dpooled[0:tm]).astype(BF16)

    return pl.pallas_call(
        body, name="pool_bwd", grid=(nt,),
        in_specs=[pl.BlockSpec((tm, 512), lambda i: (i, 0)),
                  pl.BlockSpec((HALO, 512), lambda i: (jnp.maximum(i * hb - 1, 0), 0)),
                  pl.BlockSpec((tm, 512), lambda i: (i, 0)),
                  pl.BlockSpec((HALO, 512), lambda i: (jnp.minimum((i + 1) * hb, last_halo), 0)),
                  pl.BlockSpec((4, GROUP, GROUP), lambda i: (0, 0, 0)),
                  pl.BlockSpec((1, 512), lambda i: (0, 0))],
        out_specs=[pl.BlockSpec((tm, 512), lambda i: (i, 0)),
                   pl.BlockSpec((4, GROUP, GROUP), lambda i: (0, 0, 0)),
                   pl.BlockSpec((1, 512), lambda i: (0, 0))],
        out_shape=[jax.ShapeDtypeStruct((s, 512), BF16), jax.ShapeDtypeStruct((4, GROUP, GROUP), F32),
                   jax.ShapeDtypeStruct((1, 512), F32)],
        compiler_params=_cp(1))(u, u, dpool, dpool, w_pool, pool_scale)


def _attn_bwd(q, k, v, do, bias, sinks):
    s = q.shape[0]

    def body(q_ref, do_ref, k_ref, v_ref, b_ref, sk_ref, dq_ref, dk_ref, dv_ref, dss_ref, gsk_ref):
        n = pl.program_id(0)

        @pl.when(n == 0)
        def _():
            dk_ref[...] = jnp.zeros_like(dk_ref)
            dv_ref[...] = jnp.zeros_like(dv_ref)
            dss_ref[...] = jnp.zeros_like(dss_ref)
            gsk_ref[...] = jnp.zeros_like(gsk_ref)

        kk, vv, lo, pstart, cstart = _kv_bands(k_ref, v_ref, n)
        bidx = jnp.minimum(n, 1)
        lo_q = lax.broadcasted_iota(jnp.int32, (BLK, 128), 1) < 64
        dkk = [jnp.zeros((2 * BLK, 128), F32), jnp.zeros((2 * BLK, 128), F32)]
        dvv = [jnp.zeros((2 * BLK, 128), F32), jnp.zeros((2 * BLK, 128), F32)]
        for p in range(4):
            h = p // 2
            qt = q_ref[:, p * 128:(p + 1) * 128].astype(F32) * SCALE
            dot = do_ref[:, p * 128:(p + 1) * 128]
            dq = jnp.zeros((BLK, 128), F32)
            for e in range(2):
                head = 2 * p + e
                sel_q = lo_q if e == 0 else jnp.logical_not(lo_q)
                sel_kv = lo if e == 0 else jnp.logical_not(lo)
                qm = jnp.where(sel_q, qt, 0.0).astype(BF16)
                prob, ps = _attn_probs(qm, kk[h], b_ref[bidx, head], sk_ref[0, head])
                dom = jnp.where(sel_q, dot, jnp.zeros_like(dot))
                dp = _dot_nt(dom, vv[h])
                delta = jnp.sum(prob * dp, axis=-1, keepdims=True)
                ds = prob * (dp - delta)
                gsk_ref[pl.ds(head, 1), :] += jnp.broadcast_to(-jnp.sum(ps * delta).reshape(1, 1), (1, 128))
                dss_ref[head] += ds
                dsb = ds.astype(BF16)
                km = jnp.where(sel_kv, kk[h], jnp.zeros_like(kk[h]))
                dq = dq + _dot(dsb, km)
                dkk[h] = dkk[h] + _dot_tn(dsb, qm)
                dvv[h] = dvv[h] + _dot_tn(prob.astype(BF16), dom)
            dq_ref[:, p * 128:(p + 1) * 128] = (dq * SCALE).astype(BF16)
        for acc, ref in ((dkk, dk_ref), (dvv, dv_ref)):
            f0 = acc[0] + pltpu.roll(acc[0], 64, axis=1)
            f1 = acc[1] + pltpu.roll(acc[1], 64, axis=1)
            band = jnp.where(lo, f0, f1)
            ref[pl.ds(pstart, BLK), :] += band[0:BLK]
            ref[pl.ds(cstart, BLK), :] += band[BLK:2 * BLK]

    blk = pl.BlockSpec((BLK, 512), lambda i: (i, 0))
    kv = pl.BlockSpec((s, 128), lambda i: (0, 0))
    return pl.pallas_call(
        body, name="attn_bwd", grid=(s // BLK,),
        in_specs=[blk, blk, kv, kv, pl.BlockSpec((2, NQ, BLK, 2 * BLK), lambda i: (0, 0, 0, 0)),
                  pl.BlockSpec(memory_space=pltpu.SMEM)],
        out_specs=[blk, kv, kv, pl.BlockSpec((NQ, BLK, 2 * BLK), lambda i: (0, 0, 0)),
                   pl.BlockSpec((NQ, 128), lambda i: (0, 0))],
        out_shape=[jax.ShapeDtypeStruct((s, 512), BF16), jax.ShapeDtypeStruct((s, 128), F32),
                   jax.ShapeDtypeStruct((s, 128), F32), jax.ShapeDtypeStruct((NQ, BLK, 2 * BLK), F32),
                   jax.ShapeDtypeStruct((NQ, 128), F32)],
        compiler_params=_cp(1))(q, do, k, v, bias, sinks)


def _relbias_grad(dss, bucket):
    def body(ds_ref, b_ref, o_ref):
        bucket_v = b_ref[...]
        lane = lax.broadcasted_iota(jnp.int32, (NQ, 128), 1)
        head_row = lax.broadcasted_iota(jnp.int32, (NQ, 2 * BLK), 0)
        acc = jnp.zeros((NQ, 128), F32)
        for b in range(NBUCKET):
            sel = bucket_v == b
            stack = jnp.zeros((NQ, 2 * BLK), F32)
            for h in range(NQ):
                col_sums = jnp.sum(jnp.where(sel, ds_ref[h], 0.0), axis=0, keepdims=True)
                stack = jnp.where(head_row == h, col_sums, stack)
            acc = acc + jnp.where(lane == b, jnp.sum(stack, axis=1, keepdims=True), 0.0)
        o_ref[...] = acc

    return pl.pallas_call(
        body, name="relbias_grad",
        in_specs=[pl.BlockSpec(memory_space=pltpu.VMEM), pl.BlockSpec(memory_space=pltpu.VMEM)],
        out_specs=pl.BlockSpec(memory_space=pltpu.VMEM),
        out_shape=jax.ShapeDtypeStruct((NQ, 128), F32),
        compiler_params=_cp())(dss, bucket)


def _inproj_bwd(x, g1, du, dq, dk, dv, w_in, dx1):
    s = x.shape[0]
    tm = min(TM, s)
    cols = ((0, 512), (512, 1024), (1024, 1152), (1152, 1280))

    def body(x_ref, g_ref, du_ref, dq_ref, dk_ref, dv_ref, w_ref, dx1_ref, gx_ref, gw_ref, gg_ref):
        i = pl.program_id(0)

        @pl.when(i == 0)
        def _():
            gw_ref[...] = jnp.zeros_like(gw_ref)
            gg_ref[...] = jnp.zeros_like(gg_ref)

        gv = g_ref[...]
        r1, n1 = _rms(x_ref[...])
        h = (n1 * gv).astype(BF16)
        parts = (du_ref[...], dq_ref[...], dk_ref[...].astype(BF16), dv_ref[...].astype(BF16))
        dh = None
        for (a, b), dpart in zip(cols, parts):
            t = _dot(dpart, w_ref[a:b, :])
            dh = t if dh is None else dh + t
            gw_ref[a:b, :] += _dot_tn(dpart, h)
        dx, dg = _rms_bwd(r1, n1, gv, dh)
        gg_ref[...] += dg
        gx_ref[...] = dx1_ref[...] + dx

    row = lambda w: pl.BlockSpec((tm, w), lambda i: (i, 0))
    vec = pl.BlockSpec((1, D), lambda i: (0, 0))
    full = pl.BlockSpec((IN_W, D), lambda i: (0, 0))
    return pl.pallas_call(
        body, name="inproj_bwd", grid=(s // tm,),
        in_specs=[row(D), vec, row(512), row(512), row(128), row(128), full, row(D)],
        out_specs=[row(D), full, vec],
        out_shape=[jax.ShapeDtypeStruct((s, D), F32), jax.ShapeDtypeStruct((IN_W, D), F32),
                   jax.ShapeDtypeStruct((1, D), F32)],
        compiler_params=_cp(1))(x, g1, du, dq, dk, dv, w_in, dx1)


def _local_step(x, target, small, w_in, w_out, ffn_weights, ffn_grads_hook=None):
    g1, g2, g3, g4, w_pool, pool_scale, rel_bias, sinks = small
    bucket = jnp.asarray(_bucket_table())
    wp_bf = w_pool.astype(BF16)
    u, q, k, v = _inproj_fwd(x, g1, w_in)
    pool_o = _pool_fwd(u, wp_bf, pool_scale)
    bias = _bias_table(bucket, rel_bias)
    attn_o = _attn_fwd(q, k, v, bias, sinks)
    wg, wu, wd = ffn_weights(attn_o) if callable(ffn_weights) else ffn_weights
    mix, x1, h2 = _outproj_fwd(pool_o, attn_o, w_out, x, g2, g3)
    gate, up, df, dy, loss, gg4 = _ffn_fwd(h2, wg, wu, wd, x1, target, g4)
    dgate, dup, dx1, dmix, gg3, gg2 = _ffn_bwd_act(df, gate, up, wg, wu, wd, dy, x1, mix, g3, g2)
    gwg, gwu, gwd = _ffn_bwd_w(h2, gate, up, dgate, dup, df)
    dep = None if ffn_grads_hook is None else ffn_grads_hook(gwg, gwu, gwd)
    dpool, dattn, gwout = _outproj_bwd(dmix, w_out, pool_o, attn_o, dep)
    du, gwp, gsc = _pool_bwd(u, dpool, wp_bf, pool_scale)
    dq, dk, dv, dss, gsk = _attn_bwd(q, k, v, dattn, bias, sinks)
    grb = _relbias_grad(dss, bucket)
    gx, gwin, gg1 = _inproj_bwd(x, g1, du, dq, dk, dv, w_in, dx1)
    small_grads = (gg1, gg2, gg3, gg4, gwp, gsc, grb[:, :NBUCKET].T, gsk[:, 0].reshape(1, NQ))
    return loss, gx, small_grads, (gwin, gwout, gwg, gwu, gwd)


def _place():
    x, y, c = lax.axis_index("x"), lax.axis_index("y"), lax.axis_index("c")
    chips = ((1 - x, y), (x, 1 - y), (1 - x, 1 - y))
    return x, y, c, chips


def _remote(src, dst, ssem, rsem, dev):
    return pltpu.make_async_remote_copy(src_ref=src, dst_ref=dst, send_sem=ssem, recv_sem=rsem,
                                        device_id=dev, device_id_type=MESH_T)


def _all_gather_weights(shards):
    nt = len(shards)

    def body(*refs):
        srcs, dsts = refs[:nt], refs[nt:2 * nt]
        loc, isend, irecv, dsend, drecv = refs[2 * nt:]
        x, y, c, chips = _place()
        me = 2 * x + y
        sib = (x, y, 1 - c)
        locals_, sends = [], []
        for t in range(nt):
            half = srcs[t].shape[0] // 2
            mine = pl.ds(pl.multiple_of(c * half, 16), half)
            cp = pltpu.make_async_copy(srcs[t], dsts[t].at[me], loc.at[t])
            cp.start()
            locals_.append(cp)
            for j, (px, py) in enumerate(chips):
                cp = _remote(srcs[t].at[mine], dsts[t].at[me, mine], isend.at[t, j], irecv.at[t, j], (px, py, c))
                cp.start()
                sends.append(cp)
        for t in range(nt):
            half = srcs[t].shape[0] // 2
            mine = pl.ds(pl.multiple_of(c * half, 16), half)
            for j, (px, py) in enumerate(chips):
                blk = dsts[t].at[2 * px + py, mine]
                _remote(blk, blk, isend.at[t, j], irecv.at[t, j], (px, py, c)).wait_recv()
                cp = _remote(blk, blk, dsend.at[t, j], drecv.at[t, j], sib)
                cp.start()
                sends.append(cp)
        for t in range(nt):
            half = srcs[t].shape[0] // 2
            other = pl.ds(pl.multiple_of((1 - c) * half, 16), half)
            for j, (px, py) in enumerate(chips):
                blk = dsts[t].at[2 * px + py, other]
                _remote(blk, blk, dsend.at[t, j], drecv.at[t, j], sib).wait_recv()
        for cp in sends:
            cp.wait_send()
        for cp in locals_:
            cp.wait()

    return pl.pallas_call(
        body, name="allgather_weights",
        in_specs=[ANY] * nt, out_specs=[ANY] * nt,
        out_shape=[jax.ShapeDtypeStruct((NSH,) + a.shape, a.dtype) for a in shards],
        scratch_shapes=[pltpu.SemaphoreType.DMA((nt,)), pltpu.SemaphoreType.DMA((nt, 3)),
                        pltpu.SemaphoreType.DMA((nt, 3)), pltpu.SemaphoreType.DMA((nt, 3)),
                        pltpu.SemaphoreType.DMA((nt, 3))],
        compiler_params=_cp())(*shards)


def _to_sibling(arrs, name):
    nt = len(arrs)

    def body(*refs):
        srcs, dsts = refs[:nt], refs[nt:2 * nt]
        ssem, rsem = refs[2 * nt:]
        x, y, c, _ = _place()
        cps = [_remote(srcs[t], dsts[t], ssem.at[t], rsem.at[t], (x, y, 1 - c)) for t in range(nt)]
        for cp in cps:
            cp.start()
        for cp in cps:
            cp.wait()

    return pl.pallas_call(
        body, name=name, in_specs=[ANY] * nt, out_specs=[ANY] * nt,
        out_shape=[jax.ShapeDtypeStruct(a.shape, a.dtype) for a in arrs],
        scratch_shapes=[pltpu.SemaphoreType.DMA((nt,)), pltpu.SemaphoreType.DMA((nt,))],
        compiler_params=_cp())(*arrs)


def _scatter_to_chips(arrs):
    nt = len(arrs)

    def body(*refs):
        srcs, dsts = refs[:nt], refs[nt:2 * nt]
        ssem, rsem = refs[2 * nt:]
        x, y, c, chips = _place()
        cps = []
        for t in range(nt):
            for j, (px, py) in enumerate(chips):
                cps.append(_remote(srcs[t].at[2 * px + py], dsts[t].at[j], ssem.at[t, j], rsem.at[t, j], (px, py, c)))
        for cp in cps:
            cp.start()
        for cp in cps:
            cp.wait()

    return pl.pallas_call(
        body, name="scatter_to_chips", in_specs=[ANY] * nt, out_specs=[ANY] * nt,
        out_shape=[jax.ShapeDtypeStruct((3,) + a.shape[1:], a.dtype) for a in arrs],
        scratch_shapes=[pltpu.SemaphoreType.DMA((nt, 3)), pltpu.SemaphoreType.DMA((nt, 3))],
        compiler_params=_cp())(*arrs)


HBM_SPEC = pl.BlockSpec(memory_space=pltpu.HBM)
SEM_SPEC = pl.BlockSpec(memory_space=pltpu.SEMAPHORE)
DATAFLOW = pltpu.SideEffectType.DATAFLOW_SIDE_EFFECTING


def _gather_copies(srcs, lands, ssem, rsem):
    x, y, c, chips = _place()
    me = 2 * x + y
    out = []
    for t in range(len(srcs)):
        half = srcs[t].shape[0] // 2
        mine = pl.ds(pl.multiple_of(c * half, 16), half)
        for j, (px, py) in enumerate(chips):
            k = 3 * t + j
            send = _remote(srcs[t].at[mine], lands[t].at[me, mine], ssem.at[k], rsem.at[k], (px, py, c))
            blk = lands[t].at[2 * px + py, mine]
            out.append((send, _remote(blk, blk, ssem.at[k], rsem.at[k], (px, py, c))))
    return out


def _scatter_copies(srcs, lands, ssem, rsem):
    x, y, c, chips = _place()
    out = []
    for t in range(len(srcs)):
        for j, (px, py) in enumerate(chips):
            k = 3 * t + j
            send = _remote(srcs[t].at[2 * px + py], lands[t].at[j], ssem.at[k], rsem.at[k], (px, py, c))
            out.append((send, _remote(lands[t].at[j], lands[t].at[j], ssem.at[k], rsem.at[k], (px, py, c))))
    return out


def _split_start(plan, srcs, lands, name):
    ns, nbuf = len(srcs), len(srcs) + len(lands)
    ncopy = 3 * ns

    def body(*refs):
        ssem, rsem, token = refs[nbuf], refs[nbuf + 1], refs[-1]
        for send, _ in plan(refs[:ns], refs[ns:nbuf], ssem, rsem):
            send.start()
        token[...] = jnp.zeros_like(token)

    bufs = [pltpu.with_memory_space_constraint(a, pltpu.HBM) for a in list(srcs) + list(lands)]
    outs = pl.pallas_call(
        body, name=name, in_specs=[HBM_SPEC] * nbuf,
        out_specs=[SEM_SPEC, SEM_SPEC] + [HBM_SPEC] * nbuf + [pl.BlockSpec(memory_space=pltpu.VMEM)],
        out_shape=[pltpu.SemaphoreType.DMA((ncopy,)), pltpu.SemaphoreType.DMA((ncopy,))]
        + [pltpu.HBM(a.shape, a.dtype) for a in bufs] + [jax.ShapeDtypeStruct((8, 128), F32)],
        input_output_aliases={i: 2 + i for i in range(nbuf)},
        compiler_params=pltpu.CompilerParams(has_side_effects=DATAFLOW))(*bufs)
    return outs[0], outs[1], outs[2:2 + ns], outs[2 + ns:2 + nbuf], outs[-1]


def _split_wait(plan, ssem, rsem, srcs, lands, after, name):
    ns, nbuf = len(srcs), len(srcs) + len(lands)

    def body(*refs):
        s_ref, r_ref = refs[nbuf], refs[nbuf + 1]
        for send, recv in plan(refs[:ns], refs[ns:nbuf], s_ref, r_ref):
            send.wait_send()
            recv.wait_recv()

    outs = pl.pallas_call(
        body, name=name, in_specs=[HBM_SPEC] * nbuf + [SEM_SPEC, SEM_SPEC, ANY],
        out_specs=[HBM_SPEC] * nbuf,
        out_shape=[pltpu.HBM(a.shape, a.dtype) for a in list(srcs) + list(lands)],
        input_output_aliases={i: i for i in range(nbuf)},
        compiler_params=pltpu.CompilerParams(has_side_effects=DATAFLOW))(*srcs, *lands, ssem, rsem, after)
    return outs


def _gather_finish(shards, lands):
    nt = len(shards)

    def body(*refs):
        srcs, outs = refs[:nt], refs[2 * nt:3 * nt]
        loc, dsend, drecv = refs[3 * nt:]
        x, y, c, chips = _place()
        me = 2 * x + y
        sib = (x, y, 1 - c)
        pending = []
        for t in range(nt):
            half = srcs[t].shape[0] // 2
            mine = pl.ds(pl.multiple_of(c * half, 16), half)
            cp = pltpu.make_async_copy(srcs[t], outs[t].at[me], loc.at[t])
            cp.start()
            pending.append(cp.wait)
            for j, (px, py) in enumerate(chips):
                blk = outs[t].at[2 * px + py, mine]
                cp = _remote(blk, blk, dsend.at[t, j], drecv.at[t, j], sib)
                cp.start()
                pending.append(cp.wait_send)
        for t in range(nt):
            half = srcs[t].shape[0] // 2
            other = pl.ds(pl.multiple_of((1 - c) * half, 16), half)
            for j, (px, py) in enumerate(chips):
                blk = outs[t].at[2 * px + py, other]
                _remote(blk, blk, dsend.at[t, j], drecv.at[t, j], sib).wait_recv()
        for wait in pending:
            wait()

    return pl.pallas_call(
        body, name="gather_finish", in_specs=[ANY] * (2 * nt), out_specs=[ANY] * nt,
        out_shape=[jax.ShapeDtypeStruct(a.shape, a.dtype) for a in lands],
        input_output_aliases={nt + t: t for t in range(nt)},
        scratch_shapes=[pltpu.SemaphoreType.DMA((nt,)), pltpu.SemaphoreType.DMA((nt, 3)),
                        pltpu.SemaphoreType.DMA((nt, 3))],
        compiler_params=_cp())(*shards, *lands)


def _halves(g):
    return g.reshape(NSH, 2, g.shape[1] // 2, g.shape[2])


def _cast_other_half(grads, oc, tag):
    nt = len(grads)

    def body(oc_ref, *refs):
        for t in range(nt):
            refs[nt + t][...] = refs[t][...].astype(BF16)

    ins = [_halves(g) for g in grads]
    in_specs = [pl.BlockSpec((None, None) + a.shape[2:], lambda k, oc_ref: (k, oc_ref[0], 0, 0)) for a in ins]
    out_specs = [pl.BlockSpec((None,) + a.shape[2:], lambda k, oc_ref: (k, 0, 0)) for a in ins]
    return pl.pallas_call(
        body, name="rs_cast_other_half_" + tag,
        grid_spec=pltpu.PrefetchScalarGridSpec(num_scalar_prefetch=1, grid=(NSH,), in_specs=in_specs,
                                               out_specs=out_specs),
        out_shape=[jax.ShapeDtypeStruct((NSH,) + a.shape[2:], BF16) for a in ins],
        compiler_params=_cp(1))(oc, *ins)


def _chip_partial(grads, recv, cs, tag):
    nt = len(grads)

    def body(cs_ref, *refs):
        k = pl.program_id(0)
        for t in range(nt):
            p = refs[t][...] + refs[nt + t][...].astype(F32)
            refs[2 * nt + t][...] = p.astype(BF16)

            @pl.when(k == cs_ref[1])
            def _():
                refs[3 * nt + t][...] = p

    ins = [_halves(g) for g in grads]
    g_specs = [pl.BlockSpec((None, None) + a.shape[2:], lambda k, cs_ref: (k, cs_ref[0], 0, 0)) for a in ins]
    r_specs = [pl.BlockSpec((None,) + a.shape[2:], lambda k, cs_ref: (k, 0, 0)) for a in ins]
    own_specs = [pl.BlockSpec(a.shape[2:], lambda k, cs_ref: (0, 0)) for a in ins]
    return pl.pallas_call(
        body, name="rs_chip_partial_" + tag,
        grid_spec=pltpu.PrefetchScalarGridSpec(num_scalar_prefetch=1, grid=(NSH,), in_specs=g_specs + r_specs,
                                               out_specs=r_specs + own_specs),
        out_shape=[jax.ShapeDtypeStruct((NSH,) + a.shape[2:], BF16) for a in ins]
        + [jax.ShapeDtypeStruct(a.shape[2:], F32) for a in ins],
        compiler_params=_cp(1))(cs, *ins, *recv)


def _sum_chips(own, recv):
    nt = len(own)

    def body(*refs):
        for t in range(nt):
            r = refs[nt + t]
            refs[2 * nt + t][...] = ((refs[t][...] + r[0].astype(F32)) + r[1].astype(F32)) + r[2].astype(F32)

    vm = pl.BlockSpec(memory_space=pltpu.VMEM)
    return pl.pallas_call(
        body, name="rs_sum_chips", in_specs=[vm] * (2 * nt), out_specs=[vm] * nt,
        out_shape=[jax.ShapeDtypeStruct(a.shape, F32) for a in own],
        compiler_params=_cp())(*own, *recv)


def _adamw_math(w, g, m, v):
    m = ADAM_B1 * m + (1.0 - ADAM_B1) * g
    v = ADAM_B2 * v + (1.0 - ADAM_B2) * (g * g)
    m_hat = m / (1.0 - ADAM_B1 ** ADAM_STEP)
    v_hat = v / (1.0 - ADAM_B2 ** ADAM_STEP)
    delta = -ADAM_LR * (m_hat / (jnp.sqrt(v_hat) + ADAM_EPS) + ADAM_WD * w)
    return delta, m, v


ADAM_SPLIT = 4


def _adamw_shards(own, sib, ws, ms, vs, c_arr):
    nt = len(own)

    def body(c_ref, *refs):
        hh = pl.program_id(0)
        mine = hh == c_ref[0]
        for t in range(nt):
            g = jnp.where(mine, refs[t][...], refs[nt + t][...])
            delta, m, v = _adamw_math(refs[2 * nt + t][...], g, refs[3 * nt + t][...], refs[4 * nt + t][...])
            refs[5 * nt + t][...] = g
            refs[6 * nt + t][...] = delta
            refs[7 * nt + t][...] = m
            refs[8 * nt + t][...] = v

    def tile(a):
        return (a.shape[0] // ADAM_SPLIT, a.shape[1])

    half_specs = [pl.BlockSpec(tile(a), lambda hh, q, c_ref: (q, 0)) for a in own]
    full_specs = [pl.BlockSpec(tile(a), lambda hh, q, c_ref: (hh * ADAM_SPLIT + q, 0)) for a in own]
    return pl.pallas_call(
        body, name="adamw_shards",
        grid_spec=pltpu.PrefetchScalarGridSpec(num_scalar_prefetch=1, grid=(2, ADAM_SPLIT),
                                               in_specs=half_specs * 2 + full_specs * 3, out_specs=full_specs * 4),
        out_shape=[jax.ShapeDtypeStruct(w.shape, F32) for w in ws] * 4,
        compiler_params=_cp(2))(c_arr, *own, *sib, *ws, *ms, *vs)


def _small_allreduce_adamw(gp, wp, mp, vp):
    rows = gp.shape[0]

    def body(g_ref, w_ref, m_ref, v_ref, go_ref, d_ref, mo_ref, vo_ref, gather, ssem, rsem):
        x, y, c, _ = _place()
        me = 4 * x + 2 * y + c
        gather[me] = g_ref[...]
        cps = []
        for kk in range(1, 8):
            px, py, pc = x ^ (kk >> 2), y ^ ((kk >> 1) & 1), c ^ (kk & 1)
            cp = _remote(g_ref, gather.at[me], ssem.at[kk], rsem.at[kk], (px, py, pc))
            cp.start()
            cps.append(cp)
        for kk in range(1, 8):
            px, py, pc = x ^ (kk >> 2), y ^ ((kk >> 1) & 1), c ^ (kk & 1)
            _remote(g_ref, gather.at[4 * px + 2 * py + pc], ssem.at[kk], rsem.at[kk], (px, py, pc)).wait_recv()
        for cp in cps:
            cp.wait_send()
        g = gather[0]
        for d in range(1, 8):
            g = g + gather[d]
        delta, m, v = _adamw_math(w_ref[...], g, m_ref[...], v_ref[...])
        go_ref[...] = g
        d_ref[...] = delta
        mo_ref[...] = m
        vo_ref[...] = v

    vm = pl.BlockSpec(memory_space=pltpu.VMEM)
    return pl.pallas_call(
        body, name="small_allreduce_adamw", in_specs=[vm] * 4, out_specs=[vm] * 4,
        out_shape=[jax.ShapeDtypeStruct(gp.shape, F32)] * 4,
        scratch_shapes=[pltpu.VMEM((8, rows, 128), F32), pltpu.SemaphoreType.DMA((8,)),
                        pltpu.SemaphoreType.DMA((8,))],
        compiler_params=_cp())(gp, wp, mp, vp)


SMALL_PARTS = (("g1", (1, D)), ("g2", (1, D)), ("g3", (1, D)), ("g4", (1, D)), ("w_pool", (1, 4, GROUP, GROUP)),
               ("pool_scale", (1, POOL_W)), ("rel_bias", (NBUCKET, NQ)), ("sinks", (1, NQ)))


def _pack_small(parts):
    rows = []
    for a in parts:
        flat = a.reshape(-1)
        n = flat.shape[0]
        padded = -(-n // 1024) * 1024
        rows.append(jnp.pad(flat, (0, padded - n)).reshape(-1, 128))
    return jnp.concatenate(rows, axis=0)


def _unpack_small(packed):
    out, r = [], 0
    for _, shape in SMALL_PARTS:
        n = int(np.prod(shape))
        nr = -(-n // 1024) * 8
        out.append(packed[r:r + nr].reshape(-1)[:n].reshape(shape))
        r += nr
    return out


def kernel(x, g_pre_mix, w_in, w_pool, pool_scale, rel_bias, sinks, w_out, g_post_mix, g_pre_ffn, w_gate, w_up, w_down, g_post_ffn, loss_target, m_g_pre_mix, m_w_in, m_w_pool, m_pool_scale, m_rel_bias, m_sinks, m_w_out, m_g_post_mix, m_g_pre_ffn, m_w_gate, m_w_up, m_w_down, m_g_post_ffn, v_g_pre_mix, v_w_in, v_w_pool, v_pool_scale, v_rel_bias, v_sinks, v_w_out, v_g_post_mix, v_g_pre_ffn, v_w_gate, v_w_up, v_w_down, v_g_post_ffn):
    c = lax.axis_index("c")
    chip = 2 * lax.axis_index("x") + lax.axis_index("y")

    def shards(a_in, a_out, a_gate, a_up, a_down):
        return (a_in[0].T, a_out[0], a_gate[0].T, a_up[0].T, a_down[0])

    oc = (1 - c).reshape(1).astype(jnp.int32)
    cs = jnp.stack([c, chip]).astype(jnp.int32)

    big_w = shards(w_in, w_out, w_gate, w_up, w_down)
    big_bf = [w.astype(BF16) for w in big_w]
    win_all, wout_all = _all_gather_weights(big_bf[:2])
    ffn_lands = [lax.empty((NSH,) + a.shape, BF16) for a in big_bf[2:]]
    g_ssem, g_rsem, g_srcs, g_lands, g_token = _split_start(_gather_copies, big_bf[2:], ffn_lands, "gather_ffn_start")

    def ffn_weights(after):
        outs = _split_wait(_gather_copies, g_ssem, g_rsem, g_srcs, g_lands, after, "gather_ffn_wait")
        return _gather_finish(outs[:3], outs[3:])

    rs = {}

    def ffn_grads_hook(gwg, gwu, gwd):
        grads = [gwg, gwu, gwd]
        recv_a = _to_sibling(_cast_other_half(grads, oc, "ffn"), "rs_sibling_exchange_ffn")
        outs = _chip_partial(grads, recv_a, cs, "ffn")
        lands = [lax.empty((3,) + a.shape[1:], BF16) for a in outs[:3]]
        rs["own"] = outs[3:]
        rs["ssem"], rs["rsem"], rs["srcs"], rs["lands"], token = _split_start(
            _scatter_copies, outs[:3], lands, "scatter_ffn_start")
        return token

    small = (g_pre_mix + g_token[:1, :1], g_post_mix, g_pre_ffn, g_post_ffn, w_pool[0], pool_scale, rel_bias, sinks)
    loss, gx, small_grads, (gwin, gwout, _, _, _) = _local_step(
        x[0], loss_target[0], small, win_all.reshape(IN_W, D), wout_all.reshape(D, D), ffn_weights, ffn_grads_hook)
    recv_ffn = _split_wait(_scatter_copies, rs["ssem"], rs["rsem"], rs["srcs"], rs["lands"], gx,
                           "scatter_ffn_wait")[3:]

    grads = [gwin.reshape(NSH, WIN_S, D), gwout.reshape(NSH, WOUT_S, D)]
    recv_a = _to_sibling(_cast_other_half(grads, oc, "mix"), "rs_sibling_exchange_mix")
    outs = _chip_partial(grads, recv_a, cs, "mix")
    recv_mix = _scatter_to_chips(outs[:2])
    final_half = _sum_chips(list(outs[2:]) + list(rs["own"]), list(recv_mix) + list(recv_ffn))
    sib_half = _to_sibling(final_half, "rs_sibling_share")
    adam = _adamw_shards(final_half, sib_half, big_w, shards(m_w_in, m_w_out, m_w_gate, m_w_up, m_w_down),
                         shards(v_w_in, v_w_out, v_w_gate, v_w_up, v_w_down), c.reshape(1).astype(jnp.int32))
    big_g, big_d, big_m, big_v = adam[0:5], adam[5:10], adam[10:15], adam[15:20]

    gp = _pack_small(small_grads)
    wp = _pack_small((g_pre_mix, g_post_mix, g_pre_ffn, g_post_ffn, w_pool, pool_scale, rel_bias, sinks))
    mp = _pack_small((m_g_pre_mix, m_g_post_mix, m_g_pre_ffn, m_g_post_ffn, m_w_pool, m_pool_scale, m_rel_bias,
                      m_sinks))
    vp = _pack_small((v_g_pre_mix, v_g_post_mix, v_g_pre_ffn, v_g_post_ffn, v_w_pool, v_pool_scale, v_rel_bias,
                      v_sinks))
    small_out = [_unpack_small(p) for p in _small_allreduce_adamw(gp, wp, mp, vp)]

    total_loss = lax.psum(loss[0, 0], ("x", "y", "c"))

    def ordered(small4, big4):
        sg1, sg2, sg3, sg4, swp, ssc, srb, ssk = small4
        bwin, bwout, bwg, bwu, bwd = big4[0].T[None], big4[1][None], big4[2].T[None], big4[3].T[None], big4[4][None]
        return [sg1, bwin, swp, ssc, srb, ssk, bwout, sg2, sg3, bwg, bwu, bwd, sg4]

    res = [total_loss, gx[None]]
    for sm, bg in zip(small_out, (big_g, big_d, big_m, big_v)):
        res += ordered(sm, bg)
    return tuple(res)
```

```python
import functools

import numpy as np
import jax
import jax.numpy as jnp
from jax import lax
from jax.experimental import pallas as pl
from jax.experimental.pallas import tpu as pltpu

F32 = jnp.float32
BF16 = jnp.bfloat16

D = 1024
POOL_W = 512
GROUP = 128
WINDOWS = (2, 4, 8, 16)
HALO = 128
NQ = 8
BLK = 128
NBUCKET = 32
IN_W = 1280
DFF = 2816
NSH = 4
FS = DFF // NSH
WIN_S = IN_W // NSH
WOUT_S = D // NSH
EPS = 1e-6
NEG = -1e30
SCALE = 0.125

ADAM_LR = 0.001
ADAM_B1 = 0.9
ADAM_B2 = 0.999
ADAM_EPS = 1e-08
ADAM_WD = 0.01
ADAM_STEP = 10

TM = 512
TM_POOL = 256
TM_FFN = 512
VMEM_LIMIT = 56 * 1024 * 1024

MESH_T = pl.DeviceIdType.MESH
ANY = pl.BlockSpec(memory_space=pl.ANY)


def _cp(n_grid=0, **kw):
    sem = ("arbitrary",) * n_grid if n_grid else None
    return pltpu.CompilerParams(dimension_semantics=sem, vmem_limit_bytes=VMEM_LIMIT, **kw)


def _dot(a, b):
    return jnp.dot(a, b, preferred_element_type=F32)


def _dot_nt(a, b):
    return lax.dot_general(a, b, (((1,), (1,)), ((), ())), preferred_element_type=F32)


def _dot_tn(a, b):
    return lax.dot_general(a, b, (((0,), (0,)), ((), ())), preferred_element_type=F32)


def _rms(x):
    r = lax.rsqrt(jnp.mean(x * x, axis=-1, keepdims=True) + EPS)
    return r, x * r


def _rms_bwd(r, n, g, dout):
    dn = dout * g
    dx = r * (dn - n * jnp.mean(dn * n, axis=-1, keepdims=True))
    dg = jnp.sum(dout * n, axis=0, keepdims=True)
    return dx, dg


def _split(x, n):
    parts = []
    r = x
    for _ in range(n):
        p = r.astype(BF16)
        parts.append(p)
        r = r - p.astype(F32)
    return parts


def _band_dot(a, x, n):
    acc = None
    for p in _split(x, n):
        t = _dot(a, p)
        acc = t if acc is None else acc + t
    return acc


def _bucket_table():
    qi = np.arange(BLK)[:, None]
    kj = np.arange(2 * BLK)[None, :]
    dist = qi + BLK - kj
    n = np.maximum(dist, 0)
    nf = np.maximum(n, 1).astype(np.float32)
    large = 16 + (np.log(nf / np.float32(16)) / np.float32(np.log(128 / 16)) * np.float32(16)).astype(np.int32)
    large = np.minimum(large, NBUCKET - 1)
    return np.where(n < 16, n, large).astype(np.int32)


def _inproj_fwd(x, g1, w_in):
    s = x.shape[0]
    tm = min(TM, s)

    def body(x_ref, g_ref, w_ref, u_ref, q_ref, k_ref, v_ref):
        _, n = _rms(x_ref[...])
        h = (n * g_ref[...]).astype(BF16)
        proj = _dot_nt(h, w_ref[...])
        u_ref[...] = proj[:, :512]
        q_ref[...] = proj[:, 512:1024].astype(BF16)
        k_ref[...] = proj[:, 1024:1152].astype(BF16)
        v_ref[...] = proj[:, 1152:1280].astype(BF16)

    row = lambda w: pl.BlockSpec((tm, w), lambda i: (i, 0))
    return pl.pallas_call(
        body, name="inproj_fwd", grid=(s // tm,),
        in_specs=[row(D), pl.BlockSpec((1, D), lambda i: (0, 0)), pl.BlockSpec((IN_W, D), lambda i: (0, 0))],
        out_specs=[row(512), row(512), row(128), row(128)],
        out_shape=[jax.ShapeDtypeStruct((s, 512), F32), jax.ShapeDtypeStruct((s, 512), BF16),
                   jax.ShapeDtypeStruct((s, 128), BF16), jax.ShapeDtypeStruct((s, 128), BF16)],
        compiler_params=_cp(1))(x, g1, w_in)


def _pooled(ext, cur, t0, tm):
    rows = lax.broadcasted_iota(jnp.int32, (tm, tm + HALO), 0)
    cols = lax.broadcasted_iota(jnp.int32, (tm, tm + HALO), 1)
    d = rows + HALO - cols
    t = t0 + lax.broadcasted_iota(jnp.int32, (tm, GROUP), 0)
    out = []
    for g, w in enumerate(WINDOWS):
        a = jnp.where((d >= 0) & (d < w), 1.0, 0.0).astype(BF16)
        ws = _band_dot(a, ext[:, g * GROUP:(g + 1) * GROUP], 3)
        cnt = jnp.minimum(t + 1, w).astype(F32)
        out.append(ws / cnt - cur[:, g * GROUP:(g + 1) * GROUP])
    return out


def _pool_fwd(u, w_pool, pool_scale):
    s = u.shape[0]
    tm = min(TM_POOL, s)
    hb = tm // HALO

    def body(uc_ref, uh_ref, wp_ref, sc_ref, o_ref):
        i = pl.program_id(0)
        cur = uc_ref[...]
        halo = jnp.where(i > 0, uh_ref[...], 0.0)
        ext = jnp.concatenate([halo, cur], axis=0)
        pooled = _pooled(ext, cur, i * tm, tm)
        for g in range(4):
            sl = slice(g * GROUP, (g + 1) * GROUP)
            mixed = _dot(pooled[g].astype(BF16), wp_ref[g])
            o_ref[:, sl] = (mixed * sc_ref[:, sl]).astype(BF16)

    return pl.pallas_call(
        body, name="pool_fwd", grid=(s // tm,),
        in_specs=[pl.BlockSpec((tm, 512), lambda i: (i, 0)),
                  pl.BlockSpec((HALO, 512), lambda i: (jnp.maximum(i * hb - 1, 0), 0)),
                  pl.BlockSpec((4, GROUP, GROUP), lambda i: (0, 0, 0)),
                  pl.BlockSpec((1, 512), lambda i: (0, 0))],
        out_specs=pl.BlockSpec((tm, 512), lambda i: (i, 0)),
        out_shape=jax.ShapeDtypeStruct((s, 512), BF16),
        compiler_params=_cp(1))(u, u, w_pool, pool_scale)


def _bias_table(bucket, rel_bias):
    def body(b_ref, rb_ref, o_ref):
        bucket_v = b_ref[...]
        rows = lax.broadcasted_iota(jnp.int32, (BLK, 2 * BLK), 0)
        cols = lax.broadcasted_iota(jnp.int32, (BLK, 2 * BLK), 1)
        dist = rows + BLK - cols
        inwin = (dist >= 0) & (dist < BLK)
        for h in range(NQ):
            acc = jnp.zeros((BLK, 2 * BLK), F32)
            for b in range(NBUCKET):
                acc = jnp.where(bucket_v == b, rb_ref[b, h], acc)
            rest = jnp.where(inwin, acc, NEG)
            o_ref[1, h] = rest
            o_ref[0, h] = jnp.where(cols >= BLK, rest, NEG)

    return pl.pallas_call(
        body, name="bias_table",
        in_specs=[pl.BlockSpec(memory_space=pltpu.VMEM), pl.BlockSpec(memory_space=pltpu.SMEM)],
        out_specs=pl.BlockSpec(memory_space=pltpu.VMEM),
        out_shape=jax.ShapeDtypeStruct((2, NQ, BLK, 2 * BLK), F32),
        compiler_params=_cp())(bucket, rel_bias)


def _kv_bands(k_ref, v_ref, n):
    pstart = pl.multiple_of(jnp.maximum(n - 1, 0) * BLK, BLK)
    cstart = pl.multiple_of(n * BLK, BLK)
    lo = lax.broadcasted_iota(jnp.int32, (2 * BLK, 128), 1) < 64
    out = []
    for ref in (k_ref, v_ref):
        band = jnp.concatenate([ref[pl.ds(pstart, BLK), :], ref[pl.ds(cstart, BLK), :]], axis=0).astype(F32)
        rot = pltpu.roll(band, 64, axis=1)
        out.append((jnp.where(lo, band, rot).astype(BF16), jnp.where(lo, rot, band).astype(BF16)))
    return out[0], out[1], lo, pstart, cstart


def _attn_probs(qm, kk, bias, sink):
    s = _dot_nt(qm, kk) + bias
    m = jnp.maximum(jnp.max(s, axis=-1, keepdims=True), sink)
    p = jnp.exp(s - m)
    es = jnp.exp(sink - m)
    inv = 1.0 / (jnp.sum(p, axis=-1, keepdims=True) + es)
    return p * inv, es * inv


def _attn_fwd(q, k, v, bias, sinks):
    s = q.shape[0]

    def body(q_ref, k_ref, v_ref, b_ref, sk_ref, o_ref):
        n = pl.program_id(0)
        kk, vv, lo, _, _ = _kv_bands(k_ref, v_ref, n)
        bidx = jnp.minimum(n, 1)
        lo_q = lax.broadcasted_iota(jnp.int32, (BLK, 128), 1) < 64
        for p in range(4):
            h = p // 2
            qt = q_ref[:, p * 128:(p + 1) * 128].astype(F32) * SCALE
            acc = jnp.zeros((BLK, 128), F32)
            for e in range(2):
                head = 2 * p + e
                sel_q = lo_q if e == 0 else jnp.logical_not(lo_q)
                sel_kv = lo if e == 0 else jnp.logical_not(lo)
                qm = jnp.where(sel_q, qt, 0.0).astype(BF16)
                prob, _ = _attn_probs(qm, kk[h], b_ref[bidx, head], sk_ref[0, head])
                vm = jnp.where(sel_kv, vv[h], jnp.zeros_like(vv[h]))
                acc = acc + _dot(prob.astype(BF16), vm)
            o_ref[:, p * 128:(p + 1) * 128] = acc.astype(BF16)

    return pl.pallas_call(
        body, name="attn_fwd", grid=(s // BLK,),
        in_specs=[pl.BlockSpec((BLK, 512), lambda i: (i, 0)),
                  pl.BlockSpec((s, 128), lambda i: (0, 0)),
                  pl.BlockSpec((s, 128), lambda i: (0, 0)),
                  pl.BlockSpec((2, NQ, BLK, 2 * BLK), lambda i: (0, 0, 0, 0)),
                  pl.BlockSpec(memory_space=pltpu.SMEM)],
        out_specs=pl.BlockSpec((BLK, 512), lambda i: (i, 0)),
        out_shape=jax.ShapeDtypeStruct((s, 512), BF16),
        compiler_params=_cp(1))(q, k, v, bias, sinks)


def _outproj_fwd(pool_o, attn_o, w_out, x, g2, g3):
    s = x.shape[0]
    tm = min(TM, s)

    def body(p_ref, a_ref, w_ref, x_ref, g2_ref, g3_ref, mix_ref, x1_ref, h2_ref):
        mix = _dot(p_ref[...], w_ref[0:512, :]) + _dot(a_ref[...], w_ref[512:1024, :])
        mix_ref[...] = mix
        _, n2 = _rms(mix)
        x1 = x_ref[...] + n2 * g2_ref[...]
        x1_ref[...] = x1
        _, n3 = _rms(x1)
        h2_ref[...] = (n3 * g3_ref[...]).astype(BF16)

    row = lambda w: pl.BlockSpec((tm, w), lambda i: (i, 0))
    vec = pl.BlockSpec((1, D), lambda i: (0, 0))
    return pl.pallas_call(
        body, name="outproj_fwd", grid=(s // tm,),
        in_specs=[row(512), row(512), pl.BlockSpec((D, D), lambda i: (0, 0)), row(D), vec, vec],
        out_specs=[row(D), row(D), row(D)],
        out_shape=[jax.ShapeDtypeStruct((s, D), F32), jax.ShapeDtypeStruct((s, D), F32),
                   jax.ShapeDtypeStruct((s, D), BF16)],
        compiler_params=_cp(1))(pool_o, attn_o, w_out, x, g2, g3)


def _silu_parts(g):
    sg = jax.nn.sigmoid(g)
    return sg, g * sg


def _ffn_fwd(h2, wg, wu, wd, x1, target, g4):
    s = h2.shape[0]
    tm = min(TM_FFN, s)

    def body(h_ref, wg_ref, wu_ref, wd_ref, x1_ref, t_ref, g4_ref,
             gate_ref, up_ref, df_ref, dy_ref, loss_ref, gg4_ref, f_acc):
        i = pl.program_id(0)
        j = pl.program_id(1)
        h = h_ref[...]
        gate = _dot_nt(h, wg_ref[...])
        up = _dot_nt(h, wu_ref[...])
        gate_ref[...] = gate.astype(BF16)
        up_ref[...] = up.astype(BF16)
        _, sl = _silu_parts(gate)
        contrib = _dot((sl * up).astype(BF16), wd_ref[...])

        @pl.when(j == 0)
        def _():
            f_acc[...] = contrib

        @pl.when(j > 0)
        def _():
            f_acc[...] += contrib

        @pl.when((i == 0) & (j == 0))
        def _():
            loss_ref[...] = jnp.zeros_like(loss_ref)
            gg4_ref[...] = jnp.zeros_like(gg4_ref)

        @pl.when(j == NSH - 1)
        def _():
            r4, n4 = _rms(f_acc[...])
            g4v = g4_ref[...]
            err = x1_ref[...] + n4 * g4v - t_ref[...]
            loss_ref[...] += (0.5 / D) * jnp.sum(err * err).reshape(1, 1)
            dy = err * (1.0 / D)
            dy_ref[...] = dy
            df, dg = _rms_bwd(r4, n4, g4v, dy)
            gg4_ref[...] += dg
            df_ref[...] = df.astype(BF16)

    row = lambda w: pl.BlockSpec((tm, w), lambda i, j: (i, 0))
    sh = lambda r, c: pl.BlockSpec((None, r, c), lambda i, j: (j, 0, 0))
    act = pl.BlockSpec((None, tm, FS), lambda i, j: (j, i, 0))
    vec = pl.BlockSpec((1, D), lambda i, j: (0, 0))
    return pl.pallas_call(
        body, name="ffn_fwd", grid=(s // tm, NSH),
        in_specs=[row(D), sh(FS, D), sh(FS, D), sh(FS, D), row(D), row(D), vec],
        out_specs=[act, act, row(D), row(D), pl.BlockSpec((1, 1), lambda i, j: (0, 0)), vec],
        out_shape=[jax.ShapeDtypeStruct((NSH, s, FS), BF16), jax.ShapeDtypeStruct((NSH, s, FS), BF16),
                   jax.ShapeDtypeStruct((s, D), BF16), jax.ShapeDtypeStruct((s, D), F32),
                   jax.ShapeDtypeStruct((1, 1), F32), jax.ShapeDtypeStruct((1, D), F32)],
        scratch_shapes=[pltpu.VMEM((tm, D), F32)],
        compiler_params=_cp(2))(h2, wg, wu, wd, x1, target, g4)


def _ffn_bwd_act(df, gate, up, wg, wu, wd, dy, x1, mix, g3, g2):
    s = df.shape[0]
    tm = min(TM_FFN, s)

    def body(df_ref, gate_ref, up_ref, wg_ref, wu_ref, wd_ref, dy_ref, x1_ref, mix_ref, g3_ref, g2_ref,
             dgate_ref, dup_ref, dx1_ref, dmix_ref, gg3_ref, gg2_ref, acc):
        i = pl.program_id(0)
        j = pl.program_id(1)
        da = _dot_nt(df_ref[...], wd_ref[...])
        g = gate_ref[...].astype(F32)
        u = up_ref[...].astype(F32)
        sg, sl = _silu_parts(g)
        dgate = (da * u * (sg * (1.0 + g * (1.0 - sg)))).astype(BF16)
        dup = (da * sl).astype(BF16)
        dgate_ref[...] = dgate
        dup_ref[...] = dup
        contrib = _dot(dgate, wg_ref[...]) + _dot(dup, wu_ref[...])

        @pl.when(j == 0)
        def _():
            acc[...] = contrib

        @pl.when(j > 0)
        def _():
            acc[...] += contrib

        @pl.when((i == 0) & (j == 0))
        def _():
            gg3_ref[...] = jnp.zeros_like(gg3_ref)
            gg2_ref[...] = jnp.zeros_like(gg2_ref)

        @pl.when(j == NSH - 1)
        def _():
            r3, n3 = _rms(x1_ref[...])
            dx1n, dg3 = _rms_bwd(r3, n3, g3_ref[...], acc[...])
            dx1 = dy_ref[...] + dx1n
            dx1_ref[...] = dx1
            gg3_ref[...] += dg3
            r2, n2 = _rms(mix_ref[...])
            dmix, dg2 = _rms_bwd(r2, n2, g2_ref[...], dx1)
            gg2_ref[...] += dg2
            dmix_ref[...] = dmix.astype(BF16)

    row = lambda w: pl.BlockSpec((tm, w), lambda i, j: (i, 0))
    sh = lambda r, c: pl.BlockSpec((None, r, c), lambda i, j: (j, 0, 0))
    act = pl.BlockSpec((None, tm, FS), lambda i, j: (j, i, 0))
    vec = pl.BlockSpec((1, D), lambda i, j: (0, 0))
    return pl.pallas_call(
        body, name="ffn_bwd_act", grid=(s // tm, NSH),
        in_specs=[row(D), act, act, sh(FS, D), sh(FS, D), sh(FS, D), row(D), row(D), row(D), vec, vec],
        out_specs=[act, act, row(D), row(D), vec, vec],
        out_shape=[jax.ShapeDtypeStruct((NSH, s, FS), BF16), jax.ShapeDtypeStruct((NSH, s, FS), BF16),
                   jax.ShapeDtypeStruct((s, D), F32), jax.ShapeDtypeStruct((s, D), BF16),
                   jax.ShapeDtypeStruct((1, D), F32), jax.ShapeDtypeStruct((1, D), F32)],
        scratch_shapes=[pltpu.VMEM((tm, D), F32)],
        compiler_params=_cp(2))(df, gate, up, wg, wu, wd, dy, x1, mix, g3, g2)


def _ffn_bwd_w(h2, gate, up, dgate, dup, df):
    s = h2.shape[0]
    tm = min(TM_FFN, s)

    def body(h_ref, gate_ref, up_ref, dgate_ref, dup_ref, df_ref, gwg_ref, gwu_ref, gwd_ref):
        i = pl.program_id(1)

        @pl.when(i == 0)
        def _():
            gwg_ref[...] = jnp.zeros_like(gwg_ref)
            gwu_ref[...] = jnp.zeros_like(gwu_ref)
            gwd_ref[...] = jnp.zeros_like(gwd_ref)

        h = h_ref[...]
        gwg_ref[...] += _dot_tn(dgate_ref[...], h)
        gwu_ref[...] += _dot_tn(dup_ref[...], h)
        _, sl = _silu_parts(gate_ref[...].astype(F32))
        a = (sl * up_ref[...].astype(F32)).astype(BF16)
        gwd_ref[...] += _dot_tn(a, df_ref[...])

    row = lambda w: pl.BlockSpec((tm, w), lambda j, i: (i, 0))
    act = pl.BlockSpec((None, tm, FS), lambda j, i: (j, i, 0))
    sh = lambda r, c: pl.BlockSpec((None, r, c), lambda j, i: (j, 0, 0))
    return pl.pallas_call(
        body, name="ffn_bwd_w", grid=(NSH, s // tm),
        in_specs=[row(D), act, act, act, act, row(D)],
        out_specs=[sh(FS, D), sh(FS, D), sh(FS, D)],
        out_shape=[jax.ShapeDtypeStruct((NSH, FS, D), F32)] * 3,
        compiler_params=_cp(2))(h2, gate, up, dgate, dup, df)


def _outproj_bwd(dmix, w_out, pool_o, attn_o, dep=None):
    s = dmix.shape[0]
    tm = min(TM, s)

    def body(dm_ref, w_ref, p_ref, a_ref, *rest):
        dpool_ref, dattn_ref, gw_ref = rest[-3:]
        i = pl.program_id(0)

        @pl.when(i == 0)
        def _():
            gw_ref[...] = jnp.zeros_like(gw_ref)

        dm = dm_ref[...]
        dpool_ref[...] = _dot_nt(dm, w_ref[0:512, :])
        dattn_ref[...] = _dot_nt(dm, w_ref[512:1024, :]).astype(BF16)
        gw_ref[0:512, :] += _dot_tn(p_ref[...], dm)
        gw_ref[512:1024, :] += _dot_tn(a_ref[...], dm)

    row = lambda w: pl.BlockSpec((tm, w), lambda i: (i, 0))
    full = pl.BlockSpec((D, D), lambda i: (0, 0))
    return pl.pallas_call(
        body, name="outproj_bwd", grid=(s // tm,),
        in_specs=[row(D), full, row(512), row(512)] + ([] if dep is None else [ANY]),
        out_specs=[row(512), row(512), full],
        out_shape=[jax.ShapeDtypeStruct((s, 512), F32), jax.ShapeDtypeStruct((s, 512), BF16),
                   jax.ShapeDtypeStruct((D, D), F32)],
        compiler_params=_cp(1))(dmix, w_out, pool_o, attn_o, *([] if dep is None else [dep]))


def _pool_bwd(u, dpool, w_pool, pool_scale):
    s = u.shape[0]
    tm = min(TM_POOL, s)
    hb = tm // HALO
    nt = s // tm
    last_halo = s // HALO - 1

    def body(uc_ref, uh_ref, dc_ref, dn_ref, wp_ref, sc_ref, du_ref, gwp_ref, gsc_ref):
        i = pl.program_id(0)

        @pl.when(i == 0)
        def _():
            gwp_ref[...] = jnp.zeros_like(gwp_ref)
            gsc_ref[...] = jnp.zeros_like(gsc_ref)

        cur = uc_ref[...]
        halo = jnp.where(i > 0, uh_ref[...], 0.0)
        pooled = _pooled(jnp.concatenate([halo, cur], axis=0), cur, i * tm, tm)
        dcur = dc_ref[...]
        dnext = jnp.where(i < nt - 1, dn_ref[...], 0.0)
        dext = jnp.concatenate([dcur, dnext], axis=0)
        rows = lax.broadcasted_iota(jnp.int32, (tm, tm + HALO), 0)
        cols = lax.broadcasted_iota(jnp.int32, (tm, tm + HALO), 1)
        d = cols - rows
        t_ext = i * tm + lax.broadcasted_iota(jnp.int32, (tm + HALO, GROUP), 0)
        for g, w in enumerate(WINDOWS):
            sl = slice(g * GROUP, (g + 1) * GROUP)
            pb = pooled[g].astype(BF16)
            wp = wp_ref[g]
            mixed = _dot(pb, wp)
            gsc_ref[:, sl] += jnp.sum(dcur[:, sl] * mixed, axis=0, keepdims=True)
            dmixed = (dext[:, sl] * sc_ref[:, sl]).astype(BF16)
            gwp_ref[g] += _dot_tn(pb, dmixed[0:tm])
            dpooled = _dot_nt(dmixed, wp)
            z = dpooled / jnp.minimum(t_ext + 1, w).astype(F32)
            b = jnp.where((d >= 0) & (d < w), 1.0, 0.0).astype(BF16)
            du_ref[:, sl] = (_band_dot(b, z, 2) - dpooled[0:tm]).astype(BF16)

    return pl.pallas_call(
        body, name="pool_bwd", grid=(nt,),
        in_specs=[pl.BlockSpec((tm, 512), lambda i: (i, 0)),
                  pl.BlockSpec((HALO, 512), lambda i: (jnp.maximum(i * hb - 1, 0), 0)),
                  pl.BlockSpec((tm, 512), lambda i: (i, 0)),
                  pl.BlockSpec((HALO, 512), lambda i: (jnp.minimum((i + 1) * hb, last_halo), 0)),
                  pl.BlockSpec((4, GROUP, GROUP), lambda i: (0, 0, 0)),
                  pl.BlockSpec((1, 512), lambda i: (0, 0))],
        out_specs=[pl.BlockSpec((tm, 512), lambda i: (i, 0)),
                   pl.BlockSpec((4, GROUP, GROUP), lambda i: (0, 0, 0)),
                   pl.BlockSpec((1, 512), lambda i: (0, 0))],
        out_shape=[jax.ShapeDtypeStruct((s, 512), BF16), jax.ShapeDtypeStruct((4, GROUP, GROUP), F32),
                   jax.ShapeDtypeStruct((1, 512), F32)],
        compiler_params=_cp(1))(u, u, dpool, dpool, w_pool, pool_scale)


def _attn_bwd(q, k, v, do, bias, sinks):
    s = q.shape[0]

    def body(q_ref, do_ref, k_ref, v_ref, b_ref, sk_ref, dq_ref, dk_ref, dv_ref, dss_ref, gsk_ref):
        n = pl.program_id(0)

        @pl.when(n == 0)
        def _():
            dk_ref[...] = jnp.zeros_like(dk_ref)
            dv_ref[...] = jnp.zeros_like(dv_ref)
            dss_ref[...] = jnp.zeros_like(dss_ref)
            gsk_ref[...] = jnp.zeros_like(gsk_ref)

        kk, vv, lo, pstart, cstart = _kv_bands(k_ref, v_ref, n)
        bidx = jnp.minimum(n, 1)
        lo_q = lax.broadcasted_iota(jnp.int32, (BLK, 128), 1) < 64
        dkk = [jnp.zeros((2 * BLK, 128), F32), jnp.zeros((2 * BLK, 128), F32)]
        dvv = [jnp.zeros((2 * BLK, 128), F32), jnp.zeros((2 * BLK, 128), F32)]
        for p in range(4):
            h = p // 2
            qt = q_ref[:, p * 128:(p + 1) * 128].astype(F32) * SCALE
            dot = do_ref[:, p * 128:(p + 1) * 128]
            dq = jnp.zeros((BLK, 128), F32)
            for e in range(2):
                head = 2 * p + e
                sel_q = lo_q if e == 0 else jnp.logical_not(lo_q)
                sel_kv = lo if e == 0 else jnp.logical_not(lo)
                qm = jnp.where(sel_q, qt, 0.0).astype(BF16)
                prob, ps = _attn_probs(qm, kk[h], b_ref[bidx, head], sk_ref[0, head])
                dom = jnp.where(sel_q, dot, jnp.zeros_like(dot))
                dp = _dot_nt(dom, vv[h])
                delta = jnp.sum(prob * dp, axis=-1, keepdims=True)
                ds = prob * (dp - delta)
                gsk_ref[pl.ds(head, 1), :] += jnp.broadcast_to(-jnp.sum(ps * delta).reshape(1, 1), (1, 128))
                dss_ref[head] += ds
                dsb = ds.astype(BF16)
                km = jnp.where(sel_kv, kk[h], jnp.zeros_like(kk[h]))
                dq = dq + _dot(dsb, km)
                dkk[h] = dkk[h] + _dot_tn(dsb, qm)
                dvv[h] = dvv[h] + _dot_tn(prob.astype(BF16), dom)
            dq_ref[:, p * 128:(p + 1) * 128] = (dq * SCALE).astype(BF16)
        for acc, ref in ((dkk, dk_ref), (dvv, dv_ref)):
            f0 = acc[0] + pltpu.roll(acc[0], 64, axis=1)
            f1 = acc[1] + pltpu.roll(acc[1], 64, axis=1)
            band = jnp.where(lo, f0, f1)
            ref[pl.ds(pstart, BLK), :] += band[0:BLK]
            ref[pl.ds(cstart, BLK), :] += band[BLK:2 * BLK]

    blk = pl.BlockSpec((BLK, 512), lambda i: (i, 0))
    kv = pl.BlockSpec((s, 128), lambda i: (0, 0))
    return pl.pallas_call(
        body, name="attn_bwd", grid=(s // BLK,),
        in_specs=[blk, blk, kv, kv, pl.BlockSpec((2, NQ, BLK, 2 * BLK), lambda i: (0, 0, 0, 0)),
                  pl.BlockSpec(memory_space=pltpu.SMEM)],
        out_specs=[blk, kv, kv, pl.BlockSpec((NQ, BLK, 2 * BLK), lambda i: (0, 0, 0)),
                   pl.BlockSpec((NQ, 128), lambda i: (0, 0))],
        out_shape=[jax.ShapeDtypeStruct((s, 512), BF16), jax.ShapeDtypeStruct((s, 128), F32),
                   jax.ShapeDtypeStruct((s, 128), F32), jax.ShapeDtypeStruct((NQ, BLK, 2 * BLK), F32),
                   jax.ShapeDtypeStruct((NQ, 128), F32)],
        compiler_params=_cp(1))(q, do, k, v, bias, sinks)


def _relbias_grad(dss, bucket):
    def body(ds_ref, b_ref, o_ref):
        bucket_v = b_ref[...]
        lane = lax.broadcasted_iota(jnp.int32, (NQ, 128), 1)
        head_row = lax.broadcasted_iota(jnp.int32, (NQ, 2 * BLK), 0)
        acc = jnp.zeros((NQ, 128), F32)
        for b in range(NBUCKET):
            sel = bucket_v == b
            stack = jnp.zeros((NQ, 2 * BLK), F32)
            for h in range(NQ):
                col_sums = jnp.sum(jnp.where(sel, ds_ref[h], 0.0), axis=0, keepdims=True)
                stack = jnp.where(head_row == h, col_sums, stack)
            acc = acc + jnp.where(lane == b, jnp.sum(stack, axis=1, keepdims=True), 0.0)
        o_ref[...] = acc

    return pl.pallas_call(
        body, name="relbias_grad",
        in_specs=[pl.BlockSpec(memory_space=pltpu.VMEM), pl.BlockSpec(memory_space=pltpu.VMEM)],
        out_specs=pl.BlockSpec(memory_space=pltpu.VMEM),
        out_shape=jax.ShapeDtypeStruct((NQ, 128), F32),
        compiler_params=_cp())(dss, bucket)


def _inproj_bwd(x, g1, du, dq, dk, dv, w_in, dx1):
    s = x.shape[0]
    tm = min(TM, s)
    cols = ((0, 512), (512, 1024), (1024, 1152), (1152, 1280))

    def body(x_ref, g_ref, du_ref, dq_ref, dk_ref, dv_ref, w_ref, dx1_ref, gx_ref, gw_ref, gg_ref):
        i = pl.program_id(0)

        @pl.when(i == 0)
        def _():
            gw_ref[...] = jnp.zeros_like(gw_ref)
            gg_ref[...] = jnp.zeros_like(gg_ref)

        gv = g_ref[...]
        r1, n1 = _rms(x_ref[...])
        h = (n1 * gv).astype(BF16)
        parts = (du_ref[...], dq_ref[...], dk_ref[...].astype(BF16), dv_ref[...].astype(BF16))
        dh = None
        for (a, b), dpart in zip(cols, parts):
            t = _dot(dpart, w_ref[a:b, :])
            dh = t if dh is None else dh + t
            gw_ref[a:b, :] += _dot_tn(dpart, h)
        dx, dg = _rms_bwd(r1, n1, gv, dh)
        gg_ref[...] += dg
        gx_ref[...] = dx1_ref[...] + dx

    row = lambda w: pl.BlockSpec((tm, w), lambda i: (i, 0))
    vec = pl.BlockSpec((1, D), lambda i: (0, 0))
    full = pl.BlockSpec((IN_W, D), lambda i: (0, 0))
    return pl.pallas_call(
        body, name="inproj_bwd", grid=(s // tm,),
        in_specs=[row(D), vec, row(512), row(512), row(128), row(128), full, row(D)],
        out_specs=[row(D), full, vec],
        out_shape=[jax.ShapeDtypeStruct((s, D), F32), jax.ShapeDtypeStruct((IN_W, D), F32),
                   jax.ShapeDtypeStruct((1, D), F32)],
        compiler_params=_cp(1))(x, g1, du, dq, dk, dv, w_in, dx1)


def _local_step(x, target, small, w_in, w_out, ffn_weights, ffn_grads_hook=None):
    g1, g2, g3, g4, w_pool, pool_scale, rel_bias, sinks = small
    bucket = jnp.asarray(_bucket_table())
    wp_bf = w_pool.astype(BF16)
    u, q, k, v = _inproj_fwd(x, g1, w_in)
    pool_o = _pool_fwd(u, wp_bf, pool_scale)
    bias = _bias_table(bucket, rel_bias)
    attn_o = _attn_fwd(q, k, v, bias, sinks)
    wg, wu, wd = ffn_weights(attn_o) if callable(ffn_weights) else ffn_weights
    mix, x1, h2 = _outproj_fwd(pool_o, attn_o, w_out, x, g2, g3)
    gate, up, df, dy, loss, gg4 = _ffn_fwd(h2, wg, wu, wd, x1, target, g4)
    dgate, dup, dx1, dmix, gg3, gg2 = _ffn_bwd_act(df, gate, up, wg, wu, wd, dy, x1, mix, g3, g2)
    gwg, gwu, gwd = _ffn_bwd_w(h2, gate, up, dgate, dup, df)
    dep = None if ffn_grads_hook is None else ffn_grads_hook(gwg, gwu, gwd)
    dpool, dattn, gwout = _outproj_bwd(dmix, w_out, pool_o, attn_o, dep)
    du, gwp, gsc = _pool_bwd(u, dpool, wp_bf, pool_scale)
    dq, dk, dv, dss, gsk = _attn_bwd(q, k, v, dattn, bias, sinks)
    grb = _relbias_grad(dss, bucket)
    gx, gwin, gg1 = _inproj_bwd(x, g1, du, dq, dk, dv, w_in, dx1)
    small_grads = (gg1, gg2, gg3, gg4, gwp, gsc, grb[:, :NBUCKET].T, gsk[:, 0].reshape(1, NQ))
    return loss, gx, small_grads, (gwin, gwout, gwg, gwu, gwd)


def _place():
    x, y, c = lax.axis_index("x"), lax.axis_index("y"), lax.axis_index("c")
    chips = ((1 - x, y), (x, 1 - y), (1 - x, 1 - y))
    return x, y, c, chips


def _remote(src, dst, ssem, rsem, dev):
    return pltpu.make_async_remote_copy(src_ref=src, dst_ref=dst, send_sem=ssem, recv_sem=rsem,
                                        device_id=dev, device_id_type=MESH_T)


def _own_slot(shard):
    me = 2 * lax.axis_index("x") + lax.axis_index("y")
    return lax.dynamic_update_slice(lax.empty((NSH,) + shard.shape, shard.dtype), shard[None], (me, 0, 0))


def _all_gather_weights(shards, lands):
    nt = len(shards)

    def body(*refs):
        srcs, dsts = refs[:nt], refs[2 * nt:3 * nt]
        isend, irecv, dsend, drecv = refs[3 * nt:]
        x, y, c, chips = _place()
        me = 2 * x + y
        sib = (x, y, 1 - c)
        sends = []
        for t in range(nt):
            half = srcs[t].shape[0] // 2
            mine = pl.ds(pl.multiple_of(c * half, 16), half)
            for j, (px, py) in enumerate(chips):
                cp = _remote(srcs[t].at[mine], dsts[t].at[me, mine], isend.at[t, j], irecv.at[t, j], (px, py, c))
                cp.start()
                sends.append(cp)
        for t in range(nt):
            half = srcs[t].shape[0] // 2
            mine = pl.ds(pl.multiple_of(c * half, 16), half)
            for j, (px, py) in enumerate(chips):
                blk = dsts[t].at[2 * px + py, mine]
                _remote(blk, blk, isend.at[t, j], irecv.at[t, j], (px, py, c)).wait_recv()
                cp = _remote(blk, blk, dsend.at[t, j], drecv.at[t, j], sib)
                cp.start()
                sends.append(cp)
        for t in range(nt):
            half = srcs[t].shape[0] // 2
            other = pl.ds(pl.multiple_of((1 - c) * half, 16), half)
            for j, (px, py) in enumerate(chips):
                blk = dsts[t].at[2 * px + py, other]
                _remote(blk, blk, dsend.at[t, j], drecv.at[t, j], sib).wait_recv()
        for cp in sends:
            cp.wait_send()

    return pl.pallas_call(
        body, name="allgather_weights",
        in_specs=[ANY] * (2 * nt), out_specs=[ANY] * nt,
        out_shape=[jax.ShapeDtypeStruct(a.shape, a.dtype) for a in lands],
        input_output_aliases={nt + t: t for t in range(nt)},
        scratch_shapes=[pltpu.SemaphoreType.DMA((nt, 3)), pltpu.SemaphoreType.DMA((nt, 3)),
                        pltpu.SemaphoreType.DMA((nt, 3)), pltpu.SemaphoreType.DMA((nt, 3))],
        compiler_params=_cp())(*shards, *lands)


def _to_sibling(arrs, name):
    nt = len(arrs)

    def body(*refs):
        srcs, dsts = refs[:nt], refs[nt:2 * nt]
        ssem, rsem = refs[2 * nt:]
        x, y, c, _ = _place()
        cps = [_remote(srcs[t], dsts[t], ssem.at[t], rsem.at[t], (x, y, 1 - c)) for t in range(nt)]
        for cp in cps:
            cp.start()
        for cp in cps:
            cp.wait()

    return pl.pallas_call(
        body, name=name, in_specs=[ANY] * nt, out_specs=[ANY] * nt,
        out_shape=[jax.ShapeDtypeStruct(a.shape, a.dtype) for a in arrs],
        scratch_shapes=[pltpu.SemaphoreType.DMA((nt,)), pltpu.SemaphoreType.DMA((nt,))],
        compiler_params=_cp())(*arrs)


def _scatter_to_chips(arrs):
    nt = len(arrs)

    def body(*refs):
        srcs, dsts = refs[:nt], refs[nt:2 * nt]
        ssem, rsem = refs[2 * nt:]
        x, y, c, chips = _place()
        cps = []
        for t in range(nt):
            for j, (px, py) in enumerate(chips):
                cps.append(_remote(srcs[t].at[2 * px + py], dsts[t].at[j], ssem.at[t, j], rsem.at[t, j], (px, py, c)))
        for cp in cps:
            cp.start()
        for cp in cps:
            cp.wait()

    return pl.pallas_call(
        body, name="scatter_to_chips", in_specs=[ANY] * nt, out_specs=[ANY] * nt,
        out_shape=[jax.ShapeDtypeStruct((3,) + a.shape[1:], a.dtype) for a in arrs],
        scratch_shapes=[pltpu.SemaphoreType.DMA((nt, 3)), pltpu.SemaphoreType.DMA((nt, 3))],
        compiler_params=_cp())(*arrs)


HBM_SPEC = pl.BlockSpec(memory_space=pltpu.HBM)
SEM_SPEC = pl.BlockSpec(memory_space=pltpu.SEMAPHORE)
DATAFLOW = pltpu.SideEffectType.DATAFLOW_SIDE_EFFECTING


def _gather_copies(srcs, lands, ssem, rsem):
    x, y, c, chips = _place()
    me = 2 * x + y
    out = []
    for t in range(len(srcs)):
        half = srcs[t].shape[0] // 2
        mine = pl.ds(pl.multiple_of(c * half, 16), half)
        for j, (px, py) in enumerate(chips):
            k = 3 * t + j
            send = _remote(srcs[t].at[mine], lands[t].at[me, mine], ssem.at[k], rsem.at[k], (px, py, c))
            blk = lands[t].at[2 * px + py, mine]
            out.append((send, _remote(blk, blk, ssem.at[k], rsem.at[k], (px, py, c))))
    return out


def _scatter_copies(srcs, lands, ssem, rsem):
    x, y, c, chips = _place()
    out = []
    for t in range(len(srcs)):
        for j, (px, py) in enumerate(chips):
            k = 3 * t + j
            send = _remote(srcs[t].at[2 * px + py], lands[t].at[j], ssem.at[k], rsem.at[k], (px, py, c))
            out.append((send, _remote(lands[t].at[j], lands[t].at[j], ssem.at[k], rsem.at[k], (px, py, c))))
    return out


def _split_start(plan, srcs, lands, after, name):
    ns, nbuf = len(srcs), len(srcs) + len(lands)
    ncopy = 3 * ns

    def body(*refs):
        ssem, rsem, token = refs[nbuf + 1], refs[nbuf + 2], refs[-1]
        for send, _ in plan(refs[:ns], refs[ns:nbuf], ssem, rsem):
            send.start()
        token[...] = jnp.zeros_like(token)

    bufs = [pltpu.with_memory_space_constraint(a, pltpu.HBM) for a in list(srcs) + list(lands)]
    outs = pl.pallas_call(
        body, name=name, in_specs=[HBM_SPEC] * nbuf + [ANY],
        out_specs=[SEM_SPEC, SEM_SPEC] + [HBM_SPEC] * nbuf + [pl.BlockSpec(memory_space=pltpu.VMEM)],
        out_shape=[pltpu.SemaphoreType.DMA((ncopy,)), pltpu.SemaphoreType.DMA((ncopy,))]
        + [pltpu.HBM(a.shape, a.dtype) for a in bufs] + [jax.ShapeDtypeStruct((8, 128), F32)],
        input_output_aliases={i: 2 + i for i in range(nbuf)},
        compiler_params=pltpu.CompilerParams(has_side_effects=DATAFLOW))(*bufs, after)
    return outs[0], outs[1], outs[2:2 + ns], outs[2 + ns:2 + nbuf], outs[-1]


def _split_wait(plan, ssem, rsem, srcs, lands, after, name):
    ns, nbuf = len(srcs), len(srcs) + len(lands)

    def body(*refs):
        s_ref, r_ref = refs[nbuf], refs[nbuf + 1]
        for send, recv in plan(refs[:ns], refs[ns:nbuf], s_ref, r_ref):
            send.wait_send()
            recv.wait_recv()

    outs = pl.pallas_call(
        body, name=name, in_specs=[HBM_SPEC] * nbuf + [SEM_SPEC, SEM_SPEC, ANY],
        out_specs=[HBM_SPEC] * nbuf,
        out_shape=[pltpu.HBM(a.shape, a.dtype) for a in list(srcs) + list(lands)],
        input_output_aliases={i: i for i in range(nbuf)},
        compiler_params=pltpu.CompilerParams(has_side_effects=DATAFLOW))(*srcs, *lands, ssem, rsem, after)
    return outs


def _gather_finish(lands):
    nt = len(lands)

    def body(*refs):
        outs = refs[nt:2 * nt]
        dsend, drecv = refs[2 * nt:]
        x, y, c, chips = _place()
        sib = (x, y, 1 - c)
        sends = []
        for t in range(nt):
            half = outs[t].shape[1] // 2
            mine = pl.ds(pl.multiple_of(c * half, 16), half)
            for j, (px, py) in enumerate(chips):
                blk = outs[t].at[2 * px + py, mine]
                cp = _remote(blk, blk, dsend.at[t, j], drecv.at[t, j], sib)
                cp.start()
                sends.append(cp)
        for t in range(nt):
            half = outs[t].shape[1] // 2
            other = pl.ds(pl.multiple_of((1 - c) * half, 16), half)
            for j, (px, py) in enumerate(chips):
                blk = outs[t].at[2 * px + py, other]
                _remote(blk, blk, dsend.at[t, j], drecv.at[t, j], sib).wait_recv()
        for cp in sends:
            cp.wait_send()

    return pl.pallas_call(
        body, name="gather_finish", in_specs=[ANY] * nt, out_specs=[ANY] * nt,
        out_shape=[jax.ShapeDtypeStruct(a.shape, a.dtype) for a in lands],
        input_output_aliases={t: t for t in range(nt)},
        scratch_shapes=[pltpu.SemaphoreType.DMA((nt, 3)), pltpu.SemaphoreType.DMA((nt, 3))],
        compiler_params=_cp())(*lands)


def _halves(g):
    return g.reshape(NSH, 2, g.shape[1] // 2, g.shape[2])


def _cast_other_half(grads, oc, tag):
    nt = len(grads)

    def body(oc_ref, *refs):
        for t in range(nt):
            refs[nt + t][...] = refs[t][...].astype(BF16)

    ins = [_halves(g) for g in grads]
    in_specs = [pl.BlockSpec((None, None) + a.shape[2:], lambda k, oc_ref: (k, oc_ref[0], 0, 0)) for a in ins]
    out_specs = [pl.BlockSpec((None,) + a.shape[2:], lambda k, oc_ref: (k, 0, 0)) for a in ins]
    return pl.pallas_call(
        body, name="rs_cast_other_half_" + tag,
        grid_spec=pltpu.PrefetchScalarGridSpec(num_scalar_prefetch=1, grid=(NSH,), in_specs=in_specs,
                                               out_specs=out_specs),
        out_shape=[jax.ShapeDtypeStruct((NSH,) + a.shape[2:], BF16) for a in ins],
        compiler_params=_cp(1))(oc, *ins)


def _chip_partial(grads, recv, cs, tag):
    nt = len(grads)

    def body(cs_ref, *refs):
        k = pl.program_id(0)
        for t in range(nt):
            p = refs[t][...] + refs[nt + t][...].astype(F32)
            refs[2 * nt + t][...] = p.astype(BF16)

            @pl.when(k == cs_ref[1])
            def _():
                refs[3 * nt + t][...] = p

    ins = [_halves(g) for g in grads]
    g_specs = [pl.BlockSpec((None, None) + a.shape[2:], lambda k, cs_ref: (k, cs_ref[0], 0, 0)) for a in ins]
    r_specs = [pl.BlockSpec((None,) + a.shape[2:], lambda k, cs_ref: (k, 0, 0)) for a in ins]
    own_specs = [pl.BlockSpec(a.shape[2:], lambda k, cs_ref: (0, 0)) for a in ins]
    return pl.pallas_call(
        body, name="rs_chip_partial_" + tag,
        grid_spec=pltpu.PrefetchScalarGridSpec(num_scalar_prefetch=1, grid=(NSH,), in_specs=g_specs + r_specs,
                                               out_specs=r_specs + own_specs),
        out_shape=[jax.ShapeDtypeStruct((NSH,) + a.shape[2:], BF16) for a in ins]
        + [jax.ShapeDtypeStruct(a.shape[2:], F32) for a in ins],
        compiler_params=_cp(1))(cs, *ins, *recv)


def _sum_chips(own, recv):
    nt = len(own)

    def body(*refs):
        for t in range(nt):
            r = refs[nt + t]
            refs[2 * nt + t][...] = ((refs[t][...] + r[0].astype(F32)) + r[1].astype(F32)) + r[2].astype(F32)

    vm = pl.BlockSpec(memory_space=pltpu.VMEM)
    return pl.pallas_call(
        body, name="rs_sum_chips", in_specs=[vm] * (2 * nt), out_specs=[vm] * nt,
        out_shape=[jax.ShapeDtypeStruct(a.shape, F32) for a in own],
        compiler_params=_cp())(*own, *recv)


def _adamw_math(w, g, m, v):
    m = ADAM_B1 * m + (1.0 - ADAM_B1) * g
    v = ADAM_B2 * v + (1.0 - ADAM_B2) * (g * g)
    m_hat = m / (1.0 - ADAM_B1 ** ADAM_STEP)
    v_hat = v / (1.0 - ADAM_B2 ** ADAM_STEP)
    delta = -ADAM_LR * (m_hat / (jnp.sqrt(v_hat) + ADAM_EPS) + ADAM_WD * w)
    return delta, m, v


ADAM_SPLIT = 4


def _adamw_shards(own, sib, ws, ms, vs, c_arr):
    nt = len(own)

    def body(c_ref, *refs):
        hh = pl.program_id(0)
        mine = hh == c_ref[0]
        for t in range(nt):
            g = jnp.where(mine, refs[t][...], refs[nt + t][...])
            delta, m, v = _adamw_math(refs[2 * nt + t][...], g, refs[3 * nt + t][...], refs[4 * nt + t][...])
            refs[5 * nt + t][...] = g
            refs[6 * nt + t][...] = delta
            refs[7 * nt + t][...] = m
            refs[8 * nt + t][...] = v

    def tile(a):
        return (a.shape[0] // ADAM_SPLIT, a.shape[1])

    half_specs = [pl.BlockSpec(tile(a), lambda hh, q, c_ref: (q, 0)) for a in own]
    full_specs = [pl.BlockSpec(tile(a), lambda hh, q, c_ref: (hh * ADAM_SPLIT + q, 0)) for a in own]
    return pl.pallas_call(
        body, name="adamw_shards",
        grid_spec=pltpu.PrefetchScalarGridSpec(num_scalar_prefetch=1, grid=(2, ADAM_SPLIT),
                                               in_specs=half_specs * 2 + full_specs * 3, out_specs=full_specs * 4),
        out_shape=[jax.ShapeDtypeStruct(w.shape, F32) for w in ws] * 4,
        compiler_params=_cp(2))(c_arr, *own, *sib, *ws, *ms, *vs)


def _small_allreduce_adamw(gp, wp, mp, vp):
    rows = gp.shape[0]

    def body(g_ref, w_ref, m_ref, v_ref, go_ref, d_ref, mo_ref, vo_ref, gather, ssem, rsem):
        x, y, c, _ = _place()
        me = 4 * x + 2 * y + c
        gather[me] = g_ref[...]
        cps = []
        for kk in range(1, 8):
            px, py, pc = x ^ (kk >> 2), y ^ ((kk >> 1) & 1), c ^ (kk & 1)
            cp = _remote(g_ref, gather.at[me], ssem.at[kk], rsem.at[kk], (px, py, pc))
            cp.start()
            cps.append(cp)
        for kk in range(1, 8):
            px, py, pc = x ^ (kk >> 2), y ^ ((kk >> 1) & 1), c ^ (kk & 1)
            _remote(g_ref, gather.at[4 * px + 2 * py + pc], ssem.at[kk], rsem.at[kk], (px, py, pc)).wait_recv()
        for cp in cps:
            cp.wait_send()
        g = gather[0]
        for d in range(1, 8):
            g = g + gather[d]
        delta, m, v = _adamw_math(w_ref[...], g, m_ref[...], v_ref[...])
        go_ref[...] = g
        d_ref[...] = delta
        mo_ref[...] = m
        vo_ref[...] = v

    vm = pl.BlockSpec(memory_space=pltpu.VMEM)
    return pl.pallas_call(
        body, name="small_allreduce_adamw", in_specs=[vm] * 4, out_specs=[vm] * 4,
        out_shape=[jax.ShapeDtypeStruct(gp.shape, F32)] * 4,
        scratch_shapes=[pltpu.VMEM((8, rows, 128), F32), pltpu.SemaphoreType.DMA((8,)),
                        pltpu.SemaphoreType.DMA((8,))],
        compiler_params=_cp())(gp, wp, mp, vp)


SMALL_PARTS = (("g1", (1, D)), ("g2", (1, D)), ("g3", (1, D)), ("g4", (1, D)), ("w_pool", (1, 4, GROUP, GROUP)),
               ("pool_scale", (1, POOL_W)), ("rel_bias", (NBUCKET, NQ)), ("sinks", (1, NQ)), ("loss", (1, 1)))


def _pack_small(parts):
    rows = []
    for a in parts:
        flat = a.reshape(-1)
        n = flat.shape[0]
        padded = -(-n // 1024) * 1024
        rows.append(jnp.pad(flat, (0, padded - n)).reshape(-1, 128))
    return jnp.concatenate(rows, axis=0)


def _unpack_small(packed):
    out, r = [], 0
    for _, shape in SMALL_PARTS:
        n = int(np.prod(shape))
        nr = -(-n // 1024) * 8
        out.append(packed[r:r + nr].reshape(-1)[:n].reshape(shape))
        r += nr
    return out


def kernel(x, g_pre_mix, w_in, w_pool, pool_scale, rel_bias, sinks, w_out, g_post_mix, g_pre_ffn, w_gate, w_up, w_down, g_post_ffn, loss_target, m_g_pre_mix, m_w_in, m_w_pool, m_pool_scale, m_rel_bias, m_sinks, m_w_out, m_g_post_mix, m_g_pre_ffn, m_w_gate, m_w_up, m_w_down, m_g_post_ffn, v_g_pre_mix, v_w_in, v_w_pool, v_pool_scale, v_rel_bias, v_sinks, v_w_out, v_g_post_mix, v_g_pre_ffn, v_w_gate, v_w_up, v_w_down, v_g_post_ffn):
    c = lax.axis_index("c")
    chip = 2 * lax.axis_index("x") + lax.axis_index("y")

    def shards(a_in, a_out, a_gate, a_up, a_down):
        return (a_in[0].T, a_out[0], a_gate[0].T, a_up[0].T, a_down[0])

    oc = (1 - c).reshape(1).astype(jnp.int32)
    cs = jnp.stack([c, chip]).astype(jnp.int32)

    big_w = shards(w_in, w_out, w_gate, w_up, w_down)
    big_bf = [w.astype(BF16) for w in big_w]
    win_all, wout_all = _all_gather_weights(big_bf[:2], [_own_slot(a) for a in big_bf[:2]])
    g_ssem, g_rsem, g_srcs, g_lands, g_token = _split_start(
        _gather_copies, big_bf[2:], [_own_slot(a) for a in big_bf[2:]], win_all, "gather_ffn_start")

    def ffn_weights(after):
        outs = _split_wait(_gather_copies, g_ssem, g_rsem, g_srcs, g_lands, after, "gather_ffn_wait")
        return _gather_finish(outs[3:])

    rs = {}

    def ffn_grads_hook(gwg, gwu, gwd):
        grads = [gwg, gwu, gwd]
        recv_a = _to_sibling(_cast_other_half(grads, oc, "ffn"), "rs_sibling_exchange_ffn")
        outs = _chip_partial(grads, recv_a, cs, "ffn")
        lands = [lax.empty((3,) + a.shape[1:], BF16) for a in outs[:3]]
        rs["own"] = outs[3:]
        rs["ssem"], rs["rsem"], rs["srcs"], rs["lands"], token = _split_start(
            _scatter_copies, outs[:3], lands, outs[3], "scatter_ffn_start")
        return token

    small = (g_pre_mix + g_token[:1, :1], g_post_mix, g_pre_ffn, g_post_ffn, w_pool[0], pool_scale, rel_bias, sinks)
    loss, gx, small_grads, (gwin, gwout, _, _, _) = _local_step(
        x[0], loss_target[0], small, win_all.reshape(IN_W, D), wout_all.reshape(D, D), ffn_weights, ffn_grads_hook)
    recv_ffn = _split_wait(_scatter_copies, rs["ssem"], rs["rsem"], rs["srcs"], rs["lands"], gx,
                           "scatter_ffn_wait")[3:]

    grads = [gwin.reshape(NSH, WIN_S, D), gwout.reshape(NSH, WOUT_S, D)]
    recv_a = _to_sibling(_cast_other_half(grads, oc, "mix"), "rs_sibling_exchange_mix")
    outs = _chip_partial(grads, recv_a, cs, "mix")
    recv_mix = _scatter_to_chips(outs[:2])
    final_half = _sum_chips(list(outs[2:]) + list(rs["own"]), list(recv_mix) + list(recv_ffn))
    sib_half = _to_sibling(final_half, "rs_sibling_share")
    adam = _adamw_shards(final_half, sib_half, big_w, shards(m_w_in, m_w_out, m_w_gate, m_w_up, m_w_down),
                         shards(v_w_in, v_w_out, v_w_gate, v_w_up, v_w_down), c.reshape(1).astype(jnp.int32))
    big_g, big_d, big_m, big_v = adam[0:5], adam[5:10], adam[10:15], adam[15:20]

    unused = jnp.zeros((1, 1), F32)
    gp = _pack_small(small_grads + (loss,))
    wp = _pack_small((g_pre_mix, g_post_mix, g_pre_ffn, g_post_ffn, w_pool, pool_scale, rel_bias, sinks, unused))
    mp = _pack_small((m_g_pre_mix, m_g_post_mix, m_g_pre_ffn, m_g_post_ffn, m_w_pool, m_pool_scale, m_rel_bias,
                      m_sinks, unused))
    vp = _pack_small((v_g_pre_mix, v_g_post_mix, v_g_pre_ffn, v_g_post_ffn, v_w_pool, v_pool_scale, v_rel_bias,
                      v_sinks, unused))
    small_out = [_unpack_small(p) for p in _small_allreduce_adamw(gp, wp, mp, vp)]
    total_loss = small_out[0][8][0, 0]

    def ordered(small4, big4):
        sg1, sg2, sg3, sg4, swp, ssc, srb, ssk = small4[:8]
        bwin, bwout, bwg, bwu, bwd = big4[0].T[None], big4[1][None], big4[2].T[None], big4[3].T[None], big4[4][None]
        return [sg1, bwin, swp, ssc, srb, ssk, bwout, sg2, sg3, bwg, bwu, bwd, sg4]

    res = [total_loss, gx[None]]
    for sm, bg in zip(small_out, (big_g, big_d, big_m, big_v)):
        res += ordered(sm, bg)
    return tuple(res)
```

```python
import functools

import numpy as np
import jax
import jax.numpy as jnp
from jax import lax
from jax.experimental import pallas as pl
from jax.experimental.pallas import tpu as pltpu

F32 = jnp.float32
BF16 = jnp.bfloat16

D = 1024
POOL_W = 512
GROUP = 128
WINDOWS = (2, 4, 8, 16)
HALO = 128
NQ = 8
BLK = 128
NBUCKET = 32
IN_W = 1280
DFF = 2816
NSH = 4
FS = DFF // NSH
WIN_S = IN_W // NSH
WOUT_S = D // NSH
EPS = 1e-6
NEG = -1e30
SCALE = 0.125

ADAM_LR = 0.001
ADAM_B1 = 0.9
ADAM_B2 = 0.999
ADAM_EPS = 1e-08
ADAM_WD = 0.01
ADAM_STEP = 10

TM = 512
TM_POOL = 256
TM_FFN = 512
VMEM_LIMIT = 56 * 1024 * 1024

MESH_T = pl.DeviceIdType.MESH
ANY = pl.BlockSpec(memory_space=pl.ANY)


def _cp(n_grid=0, **kw):
    sem = ("arbitrary",) * n_grid if n_grid else None
    return pltpu.CompilerParams(dimension_semantics=sem, vmem_limit_bytes=VMEM_LIMIT, **kw)


def _dot(a, b):
    return jnp.dot(a, b, preferred_element_type=F32)


def _dot_nt(a, b):
    return lax.dot_general(a, b, (((1,), (1,)), ((), ())), preferred_element_type=F32)


def _dot_tn(a, b):
    return lax.dot_general(a, b, (((0,), (0,)), ((), ())), preferred_element_type=F32)


def _rms(x):
    r = lax.rsqrt(jnp.mean(x * x, axis=-1, keepdims=True) + EPS)
    return r, x * r


def _rms_bwd(r, n, g, dout):
    dn = dout * g
    dx = r * (dn - n * jnp.mean(dn * n, axis=-1, keepdims=True))
    dg = jnp.sum(dout * n, axis=0, keepdims=True)
    return dx, dg


def _split(x, n):
    parts = []
    r = x
    for _ in range(n):
        p = r.astype(BF16)
        parts.append(p)
        r = r - p.astype(F32)
    return parts


def _band_dot(a, x, n):
    acc = None
    for p in _split(x, n):
        t = _dot(a, p)
        acc = t if acc is None else acc + t
    return acc


def _bucket_table():
    qi = np.arange(BLK)[:, None]
    kj = np.arange(2 * BLK)[None, :]
    dist = qi + BLK - kj
    n = np.maximum(dist, 0)
    nf = np.maximum(n, 1).astype(np.float32)
    large = 16 + (np.log(nf / np.float32(16)) / np.float32(np.log(128 / 16)) * np.float32(16)).astype(np.int32)
    large = np.minimum(large, NBUCKET - 1)
    return np.where(n < 16, n, large).astype(np.int32)


def _inproj_fwd(x, g1, w_in):
    s = x.shape[0]
    tm = min(TM, s)

    def body(x_ref, g_ref, w_ref, u_ref, q_ref, k_ref, v_ref):
        _, n = _rms(x_ref[...])
        h = (n * g_ref[...]).astype(BF16)
        proj = _dot_nt(h, w_ref[...])
        u_ref[...] = proj[:, :512]
        q_ref[...] = proj[:, 512:1024].astype(BF16)
        k_ref[...] = proj[:, 1024:1152].astype(BF16)
        v_ref[...] = proj[:, 1152:1280].astype(BF16)

    row = lambda w: pl.BlockSpec((tm, w), lambda i: (i, 0))
    return pl.pallas_call(
        body, name="inproj_fwd", grid=(s // tm,),
        in_specs=[row(D), pl.BlockSpec((1, D), lambda i: (0, 0)), pl.BlockSpec((IN_W, D), lambda i: (0, 0))],
        out_specs=[row(512), row(512), row(128), row(128)],
        out_shape=[jax.ShapeDtypeStruct((s, 512), F32), jax.ShapeDtypeStruct((s, 512), BF16),
                   jax.ShapeDtypeStruct((s, 128), BF16), jax.ShapeDtypeStruct((s, 128), BF16)],
        compiler_params=_cp(1))(x, g1, w_in)


def _pooled(ext, cur, t0, tm):
    rows = lax.broadcasted_iota(jnp.int32, (tm, tm + HALO), 0)
    cols = lax.broadcasted_iota(jnp.int32, (tm, tm + HALO), 1)
    d = rows + HALO - cols
    t = t0 + lax.broadcasted_iota(jnp.int32, (tm, GROUP), 0)
    out = []
    for g, w in enumerate(WINDOWS):
        a = jnp.where((d >= 0) & (d < w), 1.0, 0.0).astype(BF16)
        ws = _band_dot(a, ext[:, g * GROUP:(g + 1) * GROUP], 3)
        cnt = jnp.minimum(t + 1, w).astype(F32)
        out.append(ws / cnt - cur[:, g * GROUP:(g + 1) * GROUP])
    return out


def _pool_fwd(u, w_pool, pool_scale):
    s = u.shape[0]
    tm = min(TM_POOL, s)
    hb = tm // HALO

    def body(uc_ref, uh_ref, wp_ref, sc_ref, o_ref):
        i = pl.program_id(0)
        cur = uc_ref[...]
        halo = jnp.where(i > 0, uh_ref[...], 0.0)
        ext = jnp.concatenate([halo, cur], axis=0)
        pooled = _pooled(ext, cur, i * tm, tm)
        for g in range(4):
            sl = slice(g * GROUP, (g + 1) * GROUP)
            mixed = _dot(pooled[g].astype(BF16), wp_ref[g])
            o_ref[:, sl] = (mixed * sc_ref[:, sl]).astype(BF16)

    return pl.pallas_call(
        body, name="pool_fwd", grid=(s // tm,),
        in_specs=[pl.BlockSpec((tm, 512), lambda i: (i, 0)),
                  pl.BlockSpec((HALO, 512), lambda i: (jnp.maximum(i * hb - 1, 0), 0)),
                  pl.BlockSpec((4, GROUP, GROUP), lambda i: (0, 0, 0)),
                  pl.BlockSpec((1, 512), lambda i: (0, 0))],
        out_specs=pl.BlockSpec((tm, 512), lambda i: (i, 0)),
        out_shape=jax.ShapeDtypeStruct((s, 512), BF16),
        compiler_params=_cp(1))(u, u, w_pool, pool_scale)


def _bias_table(bucket, rel_bias):
    def body(b_ref, rb_ref, o_ref):
        bucket_v = b_ref[...]
        rows = lax.broadcasted_iota(jnp.int32, (BLK, 2 * BLK), 0)
        cols = lax.broadcasted_iota(jnp.int32, (BLK, 2 * BLK), 1)
        dist = rows + BLK - cols
        inwin = (dist >= 0) & (dist < BLK)
        for h in range(NQ):
            acc = jnp.zeros((BLK, 2 * BLK), F32)
            for b in range(NBUCKET):
                acc = jnp.where(bucket_v == b, rb_ref[b, h], acc)
            rest = jnp.where(inwin, acc, NEG)
            o_ref[1, h] = rest
            o_ref[0, h] = jnp.where(cols >= BLK, rest, NEG)

    return pl.pallas_call(
        body, name="bias_table",
        in_specs=[pl.BlockSpec(memory_space=pltpu.VMEM), pl.BlockSpec(memory_space=pltpu.SMEM)],
        out_specs=pl.BlockSpec(memory_space=pltpu.VMEM),
        out_shape=jax.ShapeDtypeStruct((2, NQ, BLK, 2 * BLK), F32),
        compiler_params=_cp())(bucket, rel_bias)


def _kv_bands(k_ref, v_ref, n):
    pstart = pl.multiple_of(jnp.maximum(n - 1, 0) * BLK, BLK)
    cstart = pl.multiple_of(n * BLK, BLK)
    lo = lax.broadcasted_iota(jnp.int32, (2 * BLK, 128), 1) < 64
    out = []
    for ref in (k_ref, v_ref):
        band = jnp.concatenate([ref[pl.ds(pstart, BLK), :], ref[pl.ds(cstart, BLK), :]], axis=0).astype(F32)
        rot = pltpu.roll(band, 64, axis=1)
        out.append((jnp.where(lo, band, rot).astype(BF16), jnp.where(lo, rot, band).astype(BF16)))
    return out[0], out[1], lo, pstart, cstart


def _attn_probs(qm, kk, bias, sink):
    s = _dot_nt(qm, kk) + bias
    m = jnp.maximum(jnp.max(s, axis=-1, keepdims=True), sink)
    p = jnp.exp(s - m)
    es = jnp.exp(sink - m)
    inv = 1.0 / (jnp.sum(p, axis=-1, keepdims=True) + es)
    return p * inv, es * inv


def _attn_fwd(q, k, v, bias, sinks):
    s = q.shape[0]

    def body(q_ref, k_ref, v_ref, b_ref, sk_ref, o_ref):
        n = pl.program_id(0)
        kk, vv, lo, _, _ = _kv_bands(k_ref, v_ref, n)
        bidx = jnp.minimum(n, 1)
        lo_q = lax.broadcasted_iota(jnp.int32, (BLK, 128), 1) < 64
        for p in range(4):
            h = p // 2
            qt = q_ref[:, p * 128:(p + 1) * 128].astype(F32) * SCALE
            acc = jnp.zeros((BLK, 128), F32)
            for e in range(2):
                head = 2 * p + e
                sel_q = lo_q if e == 0 else jnp.logical_not(lo_q)
                sel_kv = lo if e == 0 else jnp.logical_not(lo)
                qm = jnp.where(sel_q, qt, 0.0).astype(BF16)
                prob, _ = _attn_probs(qm, kk[h], b_ref[bidx, head], sk_ref[0, head])
                vm = jnp.where(sel_kv, vv[h], jnp.zeros_like(vv[h]))
                acc = acc + _dot(prob.astype(BF16), vm)
            o_ref[:, p * 128:(p + 1) * 128] = acc.astype(BF16)

    return pl.pallas_call(
        body, name="attn_fwd", grid=(s // BLK,),
        in_specs=[pl.BlockSpec((BLK, 512), lambda i: (i, 0)),
                  pl.BlockSpec((s, 128), lambda i: (0, 0)),
                  pl.BlockSpec((s, 128), lambda i: (0, 0)),
                  pl.BlockSpec((2, NQ, BLK, 2 * BLK), lambda i: (0, 0, 0, 0)),
                  pl.BlockSpec(memory_space=pltpu.SMEM)],
        out_specs=pl.BlockSpec((BLK, 512), lambda i: (i, 0)),
        out_shape=jax.ShapeDtypeStruct((s, 512), BF16),
        compiler_params=_cp(1))(q, k, v, bias, sinks)


def _outproj_fwd(pool_o, attn_o, w_out, x, g2, g3):
    s = x.shape[0]
    tm = min(TM, s)

    def body(p_ref, a_ref, w_ref, x_ref, g2_ref, g3_ref, mix_ref, x1_ref, h2_ref):
        mix = _dot(p_ref[...], w_ref[0:512, :]) + _dot(a_ref[...], w_ref[512:1024, :])
        mix_ref[...] = mix
        _, n2 = _rms(mix)
        x1 = x_ref[...] + n2 * g2_ref[...]
        x1_ref[...] = x1
        _, n3 = _rms(x1)
        h2_ref[...] = (n3 * g3_ref[...]).astype(BF16)

    row = lambda w: pl.BlockSpec((tm, w), lambda i: (i, 0))
    vec = pl.BlockSpec((1, D), lambda i: (0, 0))
    return pl.pallas_call(
        body, name="outproj_fwd", grid=(s // tm,),
        in_specs=[row(512), row(512), pl.BlockSpec((D, D), lambda i: (0, 0)), row(D), vec, vec],
        out_specs=[row(D), row(D), row(D)],
        out_shape=[jax.ShapeDtypeStruct((s, D), F32), jax.ShapeDtypeStruct((s, D), F32),
                   jax.ShapeDtypeStruct((s, D), BF16)],
        compiler_params=_cp(1))(pool_o, attn_o, w_out, x, g2, g3)


def _silu_parts(g):
    sg = jax.nn.sigmoid(g)
    return sg, g * sg


def _ffn_fwd(h2, wg, wu, wd, x1, target, g4):
    s = h2.shape[0]
    tm = min(TM_FFN, s)

    def body(h_ref, wg_ref, wu_ref, wd_ref, x1_ref, t_ref, g4_ref,
             gate_ref, up_ref, df_ref, dy_ref, loss_ref, gg4_ref, f_acc):
        i = pl.program_id(0)
        j = pl.program_id(1)
        h = h_ref[...]
        gate = _dot_nt(h, wg_ref[...])
        up = _dot_nt(h, wu_ref[...])
        gate_ref[...] = gate.astype(BF16)
        up_ref[...] = up.astype(BF16)
        _, sl = _silu_parts(gate)
        contrib = _dot((sl * up).astype(BF16), wd_ref[...])

        @pl.when(j == 0)
        def _():
            f_acc[...] = contrib

        @pl.when(j > 0)
        def _():
            f_acc[...] += contrib

        @pl.when((i == 0) & (j == 0))
        def _():
            loss_ref[...] = jnp.zeros_like(loss_ref)
            gg4_ref[...] = jnp.zeros_like(gg4_ref)

        @pl.when(j == NSH - 1)
        def _():
            r4, n4 = _rms(f_acc[...])
            g4v = g4_ref[...]
            err = x1_ref[...] + n4 * g4v - t_ref[...]
            loss_ref[...] += (0.5 / D) * jnp.sum(err * err).reshape(1, 1)
            dy = err * (1.0 / D)
            dy_ref[...] = dy
            df, dg = _rms_bwd(r4, n4, g4v, dy)
            gg4_ref[...] += dg
            df_ref[...] = df.astype(BF16)

    row = lambda w: pl.BlockSpec((tm, w), lambda i, j: (i, 0))
    sh = lambda r, c: pl.BlockSpec((None, r, c), lambda i, j: (j, 0, 0))
    act = pl.BlockSpec((None, tm, FS), lambda i, j: (j, i, 0))
    vec = pl.BlockSpec((1, D), lambda i, j: (0, 0))
    return pl.pallas_call(
        body, name="ffn_fwd", grid=(s // tm, NSH),
        in_specs=[row(D), sh(FS, D), sh(FS, D), sh(FS, D), row(D), row(D), vec],
        out_specs=[act, act, row(D), row(D), pl.BlockSpec((1, 1), lambda i, j: (0, 0)), vec],
        out_shape=[jax.ShapeDtypeStruct((NSH, s, FS), BF16), jax.ShapeDtypeStruct((NSH, s, FS), BF16),
                   jax.ShapeDtypeStruct((s, D), BF16), jax.ShapeDtypeStruct((s, D), F32),
                   jax.ShapeDtypeStruct((1, 1), F32), jax.ShapeDtypeStruct((1, D), F32)],
        scratch_shapes=[pltpu.VMEM((tm, D), F32)],
        compiler_params=_cp(2))(h2, wg, wu, wd, x1, target, g4)


def _ffn_bwd_act(df, gate, up, wg, wu, wd, dy, x1, mix, g3, g2):
    s = df.shape[0]
    tm = min(TM_FFN, s)

    def body(df_ref, gate_ref, up_ref, wg_ref, wu_ref, wd_ref, dy_ref, x1_ref, mix_ref, g3_ref, g2_ref,
             dgate_ref, dup_ref, dx1_ref, dmix_ref, gg3_ref, gg2_ref, acc):
        i = pl.program_id(0)
        j = pl.program_id(1)
        da = _dot_nt(df_ref[...], wd_ref[...])
        g = gate_ref[...].astype(F32)
        u = up_ref[...].astype(F32)
        sg, sl = _silu_parts(g)
        dgate = (da * u * (sg * (1.0 + g * (1.0 - sg)))).astype(BF16)
        dup = (da * sl).astype(BF16)
        dgate_ref[...] = dgate
        dup_ref[...] = dup
        contrib = _dot(dgate, wg_ref[...]) + _dot(dup, wu_ref[...])

        @pl.when(j == 0)
        def _():
            acc[...] = contrib

        @pl.when(j > 0)
        def _():
            acc[...] += contrib

        @pl.when((i == 0) & (j == 0))
        def _():
            gg3_ref[...] = jnp.zeros_like(gg3_ref)
            gg2_ref[...] = jnp.zeros_like(gg2_ref)

        @pl.when(j == NSH - 1)
        def _():
            r3, n3 = _rms(x1_ref[...])
            dx1n, dg3 = _rms_bwd(r3, n3, g3_ref[...], acc[...])
            dx1 = dy_ref[...] + dx1n
            dx1_ref[...] = dx1
            gg3_ref[...] += dg3
            r2, n2 = _rms(mix_ref[...])
            dmix, dg2 = _rms_bwd(r2, n2, g2_ref[...], dx1)
            gg2_ref[...] += dg2
            dmix_ref[...] = dmix.astype(BF16)

    row = lambda w: pl.BlockSpec((tm, w), lambda i, j: (i, 0))
    sh = lambda r, c: pl.BlockSpec((None, r, c), lambda i, j: (j, 0, 0))
    act = pl.BlockSpec((None, tm, FS), lambda i, j: (j, i, 0))
    vec = pl.BlockSpec((1, D), lambda i, j: (0, 0))
    return pl.pallas_call(
        body, name="ffn_bwd_act", grid=(s // tm, NSH),
        in_specs=[row(D), act, act, sh(FS, D), sh(FS, D), sh(FS, D), row(D), row(D), row(D), vec, vec],
        out_specs=[act, act, row(D), row(D), vec, vec],
        out_shape=[jax.ShapeDtypeStruct((NSH, s, FS), BF16), jax.ShapeDtypeStruct((NSH, s, FS), BF16),
                   jax.ShapeDtypeStruct((s, D), F32), jax.ShapeDtypeStruct((s, D), BF16),
                   jax.ShapeDtypeStruct((1, D), F32), jax.ShapeDtypeStruct((1, D), F32)],
        scratch_shapes=[pltpu.VMEM((tm, D), F32)],
        compiler_params=_cp(2))(df, gate, up, wg, wu, wd, dy, x1, mix, g3, g2)


SMEM_SPEC = pl.BlockSpec(memory_space=pltpu.SMEM)


def _emit_halves(acc_ref, row0, rows, c, own_ref, oth_ref):
    half = rows // 2
    own_ref[...] = acc_ref[pl.ds(row0 + pl.multiple_of(c * half, 8), half), :]
    oth_ref[...] = acc_ref[pl.ds(row0 + pl.multiple_of((1 - c) * half, 8), half), :].astype(BF16)


def _half_shapes(rows):
    return [jax.ShapeDtypeStruct((NSH, rows // 2, D), F32), jax.ShapeDtypeStruct((NSH, rows // 2, D), BF16)]


def _ffn_bwd_w(h2, gate, up, dgate, dup, df, c_arr):
    s = h2.shape[0]
    tm = min(TM_FFN, s)
    nt = s // tm

    def body(c_ref, h_ref, gate_ref, up_ref, dgate_ref, dup_ref, df_ref, *rest):
        outs, accs = rest[:6], rest[6:]
        i = pl.program_id(1)
        h = h_ref[...]
        _, sl = _silu_parts(gate_ref[...].astype(F32))
        a = (sl * up_ref[...].astype(F32)).astype(BF16)
        parts = (_dot_tn(dgate_ref[...], h), _dot_tn(dup_ref[...], h), _dot_tn(a, df_ref[...]))
        for acc, p in zip(accs, parts):
            @pl.when(i == 0)
            def _():
                acc[...] = p

            @pl.when(i > 0)
            def _():
                acc[...] += p

        @pl.when(i == nt - 1)
        def _():
            for t, acc in enumerate(accs):
                _emit_halves(acc, 0, FS, c_ref[0], outs[2 * t], outs[2 * t + 1])

    row = lambda w: pl.BlockSpec((tm, w), lambda j, i: (i, 0))
    act = pl.BlockSpec((None, tm, FS), lambda j, i: (j, i, 0))
    half = pl.BlockSpec((None, FS // 2, D), lambda j, i: (j, 0, 0))
    return pl.pallas_call(
        body, name="ffn_bwd_w", grid=(NSH, nt),
        in_specs=[SMEM_SPEC, row(D), act, act, act, act, row(D)],
        out_specs=[half] * 6,
        out_shape=_half_shapes(FS) * 3,
        scratch_shapes=[pltpu.VMEM((FS, D), F32)] * 3,
        compiler_params=_cp(2))(c_arr, h2, gate, up, dgate, dup, df)


def _outproj_bwd(dmix, w_out, pool_o, attn_o, c_arr, dep=None):
    s = dmix.shape[0]
    tm = min(TM, s)
    nt = s // tm

    def body(c_ref, dm_ref, w_ref, p_ref, a_ref, *rest):
        dpool_ref, dattn_ref, own_ref, oth_ref, gw_ref = rest[-5:]
        i = pl.program_id(0)

        @pl.when(i == 0)
        def _():
            gw_ref[...] = jnp.zeros_like(gw_ref)

        dm = dm_ref[...]
        dpool_ref[...] = _dot_nt(dm, w_ref[0:512, :])
        dattn_ref[...] = _dot_nt(dm, w_ref[512:1024, :]).astype(BF16)
        gw_ref[0:512, :] += _dot_tn(p_ref[...], dm)
        gw_ref[512:1024, :] += _dot_tn(a_ref[...], dm)

        @pl.when(i == nt - 1)
        def _():
            for k in range(NSH):
                _emit_halves(gw_ref, k * WOUT_S, WOUT_S, c_ref[0], own_ref.at[k], oth_ref.at[k])

    row = lambda w: pl.BlockSpec((tm, w), lambda i: (i, 0))
    full = pl.BlockSpec((D, D), lambda i: (0, 0))
    half = pl.BlockSpec((NSH, WOUT_S // 2, D), lambda i: (0, 0, 0))
    return pl.pallas_call(
        body, name="outproj_bwd", grid=(nt,),
        in_specs=[SMEM_SPEC, row(D), full, row(512), row(512)] + ([] if dep is None else [ANY]),
        out_specs=[row(512), row(512), half, half],
        out_shape=[jax.ShapeDtypeStruct((s, 512), F32), jax.ShapeDtypeStruct((s, 512), BF16)] + _half_shapes(WOUT_S),
        scratch_shapes=[pltpu.VMEM((D, D), F32)],
        compiler_params=_cp(1))(c_arr, dmix, w_out, pool_o, attn_o, *([] if dep is None else [dep]))


def _pool_bwd(u, dpool, w_pool, pool_scale, dep=None):
    s = u.shape[0]
    tm = min(TM_POOL, s)
    hb = tm // HALO
    nt = s // tm
    last_halo = s // HALO - 1

    def body(uc_ref, uh_ref, dc_ref, dn_ref, wp_ref, sc_ref, *rest):
        du_ref, gwp_ref, gsc_ref = rest[-3:]
        i = pl.program_id(0)

        @pl.when(i == 0)
        def _():
            gwp_ref[...] = jnp.zeros_like(gwp_ref)
            gsc_ref[...] = jnp.zeros_like(gsc_ref)

        cur = uc_ref[...]
        halo = jnp.where(i > 0, uh_ref[...], 0.0)
        pooled = _pooled(jnp.concatenate([halo, cur], axis=0), cur, i * tm, tm)
        dcur = dc_ref[...]
        dnext = jnp.where(i < nt - 1, dn_ref[...], 0.0)
        dext = jnp.concatenate([dcur, dnext], axis=0)
        rows = lax.broadcasted_iota(jnp.int32, (tm, tm + HALO), 0)
        cols = lax.broadcasted_iota(jnp.int32, (tm, tm + HALO), 1)
        d = cols - rows
        t_ext = i * tm + lax.broadcasted_iota(jnp.int32, (tm + HALO, GROUP), 0)
        for g, w in enumerate(WINDOWS):
            sl = slice(g * GROUP, (g + 1) * GROUP)
            pb = pooled[g].astype(BF16)
            wp = wp_ref[g]
            mixed = _dot(pb, wp)
            gsc_ref[:, sl] += jnp.sum(dcur[:, sl] * mixed, axis=0, keepdims=True)
            dmixed = (dext[:, sl] * sc_ref[:, sl]).astype(BF16)
            gwp_ref[g] += _dot_tn(pb, dmixed[0:tm])
            dpooled = _dot_nt(dmixed, wp)
            z = dpooled / jnp.minimum(t_ext + 1, w).astype(F32)
            b = jnp.where((d >= 0) & (d < w), 1.0, 0.0).astype(BF16)
            du_ref[:, sl] = (_band_dot(b, z, 2) - dpooled[0:tm]).astype(BF16)

    return pl.pallas_call(
        body, name="pool_bwd", grid=(nt,),
        in_specs=[pl.BlockSpec((tm, 512), lambda i: (i, 0)),
                  pl.BlockSpec((HALO, 512), lambda i: (jnp.maximum(i * hb - 1, 0), 0)),
                  pl.BlockSpec((tm, 512), lambda i: (i, 0)),
                  pl.BlockSpec((HALO, 512), lambda i: (jnp.minimum((i + 1) * hb, last_halo), 0)),
                  pl.BlockSpec((4, GROUP, GROUP), lambda i: (0, 0, 0)),
                  pl.BlockSpec((1, 512), lambda i: (0, 0))] + ([] if dep is None else [ANY]),
        out_specs=[pl.BlockSpec((tm, 512), lambda i: (i, 0)),
                   pl.BlockSpec((4, GROUP, GROUP), lambda i: (0, 0, 0)),
                   pl.BlockSpec((1, 512), lambda i: (0, 0))],
        out_shape=[jax.ShapeDtypeStruct((s, 512), BF16), jax.ShapeDtypeStruct((4, GROUP, GROUP), F32),
                   jax.ShapeDtypeStruct((1, 512), F32)],
        compiler_params=_cp(1))(u, u, dpool, dpool, w_pool, pool_scale, *([] if dep is None else [dep]))


def _attn_bwd(q, k, v, do, bias, sinks, dep=None):
    s = q.shape[0]

    def body(q_ref, do_ref, k_ref, v_ref, b_ref, sk_ref, *rest):
        dq_ref, dk_ref, dv_ref, dss_ref, gsk_ref = rest[-5:]
        n = pl.program_id(0)

        @pl.when(n == 0)
        def _():
            dk_ref[...] = jnp.zeros_like(dk_ref)
            dv_ref[...] = jnp.zeros_like(dv_ref)
            dss_ref[...] = jnp.zeros_like(dss_ref)
            gsk_ref[...] = jnp.zeros_like(gsk_ref)

        kk, vv, lo, pstart, cstart = _kv_bands(k_ref, v_ref, n)
        bidx = jnp.minimum(n, 1)
        lo_q = lax.broadcasted_iota(jnp.int32, (BLK, 128), 1) < 64
        dkk = [jnp.zeros((2 * BLK, 128), F32), jnp.zeros((2 * BLK, 128), F32)]
        dvv = [jnp.zeros((2 * BLK, 128), F32), jnp.zeros((2 * BLK, 128), F32)]
        for p in range(4):
            h = p // 2
            qt = q_ref[:, p * 128:(p + 1) * 128].astype(F32) * SCALE
            dot = do_ref[:, p * 128:(p + 1) * 128]
            dq = jnp.zeros((BLK, 128), F32)
            for e in range(2):
                head = 2 * p + e
                sel_q = lo_q if e == 0 else jnp.logical_not(lo_q)
                sel_kv = lo if e == 0 else jnp.logical_not(lo)
                qm = jnp.where(sel_q, qt, 0.0).astype(BF16)
                prob, ps = _attn_probs(qm, kk[h], b_ref[bidx, head], sk_ref[0, head])
                dom = jnp.where(sel_q, dot, jnp.zeros_like(dot))
                dp = _dot_nt(dom, vv[h])
                delta = jnp.sum(prob * dp, axis=-1, keepdims=True)
                ds = prob * (dp - delta)
                gsk_ref[pl.ds(head, 1), :] += jnp.broadcast_to(-jnp.sum(ps * delta).reshape(1, 1), (1, 128))
                dss_ref[head] += ds
                dsb = ds.astype(BF16)
                km = jnp.where(sel_kv, kk[h], jnp.zeros_like(kk[h]))
                dq = dq + _dot(dsb, km)
                dkk[h] = dkk[h] + _dot_tn(dsb, qm)
                dvv[h] = dvv[h] + _dot_tn(prob.astype(BF16), dom)
            dq_ref[:, p * 128:(p + 1) * 128] = (dq * SCALE).astype(BF16)
        for acc, ref in ((dkk, dk_ref), (dvv, dv_ref)):
            f0 = acc[0] + pltpu.roll(acc[0], 64, axis=1)
            f1 = acc[1] + pltpu.roll(acc[1], 64, axis=1)
            band = jnp.where(lo, f0, f1)
            ref[pl.ds(pstart, BLK), :] += band[0:BLK]
            ref[pl.ds(cstart, BLK), :] += band[BLK:2 * BLK]

    blk = pl.BlockSpec((BLK, 512), lambda i: (i, 0))
    kv = pl.BlockSpec((s, 128), lambda i: (0, 0))
    return pl.pallas_call(
        body, name="attn_bwd", grid=(s // BLK,),
        in_specs=[blk, blk, kv, kv, pl.BlockSpec((2, NQ, BLK, 2 * BLK), lambda i: (0, 0, 0, 0)),
                  pl.BlockSpec(memory_space=pltpu.SMEM)] + ([] if dep is None else [ANY]),
        out_specs=[blk, kv, kv, pl.BlockSpec((NQ, BLK, 2 * BLK), lambda i: (0, 0, 0)),
                   pl.BlockSpec((NQ, 128), lambda i: (0, 0))],
        out_shape=[jax.ShapeDtypeStruct((s, 512), BF16), jax.ShapeDtypeStruct((s, 128), F32),
                   jax.ShapeDtypeStruct((s, 128), F32), jax.ShapeDtypeStruct((NQ, BLK, 2 * BLK), F32),
                   jax.ShapeDtypeStruct((NQ, 128), F32)],
        compiler_params=_cp(1))(q, do, k, v, bias, sinks, *([] if dep is None else [dep]))


def _relbias_grad(dss, bucket):
    def body(ds_ref, b_ref, o_ref):
        bucket_v = b_ref[...]
        lane = lax.broadcasted_iota(jnp.int32, (NQ, 128), 1)
        head_row = lax.broadcasted_iota(jnp.int32, (NQ, 2 * BLK), 0)
        acc = jnp.zeros((NQ, 128), F32)
        for b in range(NBUCKET):
            sel = bucket_v == b
            stack = jnp.zeros((NQ, 2 * BLK), F32)
            for h in range(NQ):
                col_sums = jnp.sum(jnp.where(sel, ds_ref[h], 0.0), axis=0, keepdims=True)
                stack = jnp.where(head_row == h, col_sums, stack)
            acc = acc + jnp.where(lane == b, jnp.sum(stack, axis=1, keepdims=True), 0.0)
        o_ref[...] = acc

    return pl.pallas_call(
        body, name="relbias_grad",
        in_specs=[pl.BlockSpec(memory_space=pltpu.VMEM), pl.BlockSpec(memory_space=pltpu.VMEM)],
        out_specs=pl.BlockSpec(memory_space=pltpu.VMEM),
        out_shape=jax.ShapeDtypeStruct((NQ, 128), F32),
        compiler_params=_cp())(dss, bucket)


def _inproj_bwd(x, g1, du, dq, dk, dv, w_in, dx1, c_arr):
    s = x.shape[0]
    tm = min(TM, s)
    nt = s // tm
    cols = ((0, 512), (512, 1024), (1024, 1152), (1152, 1280))

    def body(c_ref, x_ref, g_ref, du_ref, dq_ref, dk_ref, dv_ref, w_ref, dx1_ref, gx_ref, own_ref, oth_ref, gg_ref,
             gw_ref):
        i = pl.program_id(0)

        @pl.when(i == 0)
        def _():
            gw_ref[...] = jnp.zeros_like(gw_ref)
            gg_ref[...] = jnp.zeros_like(gg_ref)

        gv = g_ref[...]
        r1, n1 = _rms(x_ref[...])
        h = (n1 * gv).astype(BF16)
        parts = (du_ref[...], dq_ref[...], dk_ref[...].astype(BF16), dv_ref[...].astype(BF16))
        dh = None
        for (a, b), dpart in zip(cols, parts):
            t = _dot(dpart, w_ref[a:b, :])
            dh = t if dh is None else dh + t
            gw_ref[a:b, :] += _dot_tn(dpart, h)
        dx, dg = _rms_bwd(r1, n1, gv, dh)
        gg_ref[...] += dg
        gx_ref[...] = dx1_ref[...] + dx

        @pl.when(i == nt - 1)
        def _():
            for k in range(NSH):
                _emit_halves(gw_ref, k * WIN_S, WIN_S, c_ref[0], own_ref.at[k], oth_ref.at[k])

    row = lambda w: pl.BlockSpec((tm, w), lambda i: (i, 0))
    vec = pl.BlockSpec((1, D), lambda i: (0, 0))
    full = pl.BlockSpec((IN_W, D), lambda i: (0, 0))
    half = pl.BlockSpec((NSH, WIN_S // 2, D), lambda i: (0, 0, 0))
    return pl.pallas_call(
        body, name="inproj_bwd", grid=(nt,),
        in_specs=[SMEM_SPEC, row(D), vec, row(512), row(512), row(128), row(128), full, row(D)],
        out_specs=[row(D), half, half, vec],
        out_shape=[jax.ShapeDtypeStruct((s, D), F32)] + _half_shapes(WIN_S) + [jax.ShapeDtypeStruct((1, D), F32)],
        scratch_shapes=[pltpu.VMEM((IN_W, D), F32)],
        compiler_params=_cp(1))(c_arr, x, g1, du, dq, dk, dv, w_in, dx1)


def _local_step(x, target, small, w_in, w_out, ffn_weights, c_arr, on_ffn_grads=None, on_wout_grads=None):
    g1, g2, g3, g4, w_pool, pool_scale, rel_bias, sinks = small
    bucket = jnp.asarray(_bucket_table())
    wp_bf = w_pool.astype(BF16)
    u, q, k, v = _inproj_fwd(x, g1, w_in)
    pool_o = _pool_fwd(u, wp_bf, pool_scale)
    bias = _bias_table(bucket, rel_bias)
    attn_o = _attn_fwd(q, k, v, bias, sinks)
    wg, wu, wd = ffn_weights(pool_o, attn_o) if callable(ffn_weights) else ffn_weights
    mix, x1, h2 = _outproj_fwd(pool_o, attn_o, w_out, x, g2, g3)
    gate, up, df, dy, loss, gg4 = _ffn_fwd(h2, wg, wu, wd, x1, target, g4)
    dgate, dup, dx1, dmix, gg3, gg2 = _ffn_bwd_act(df, gate, up, wg, wu, wd, dy, x1, mix, g3, g2)
    ffn_g = _ffn_bwd_w(h2, gate, up, dgate, dup, df, c_arr)
    dep = None if on_ffn_grads is None else on_ffn_grads(ffn_g)
    dpool, dattn, wout_own, wout_oth = _outproj_bwd(dmix, w_out, pool_o, attn_o, c_arr, dep)
    dep = None if on_wout_grads is None else on_wout_grads(wout_own, wout_oth, dpool)
    du, gwp, gsc = _pool_bwd(u, dpool, wp_bf, pool_scale, dep)
    dq, dk, dv, dss, gsk = _attn_bwd(q, k, v, dattn, bias, sinks, dep)
    grb = _relbias_grad(dss, bucket)
    gx, win_own, win_oth, gg1 = _inproj_bwd(x, g1, du, dq, dk, dv, w_in, dx1, c_arr)
    small_grads = (gg1, gg2, gg3, gg4, gwp, gsc, grb[:, :NBUCKET].T, gsk[:, 0].reshape(1, NQ))
    return loss, gx, small_grads, ((win_own, win_oth), (wout_own, wout_oth), ffn_g)


def _place():
    x, y, c = lax.axis_index("x"), lax.axis_index("y"), lax.axis_index("c")
    chips = ((1 - x, y), (x, 1 - y), (1 - x, 1 - y))
    return x, y, c, chips


def _remote(src, dst, ssem, rsem, dev):
    return pltpu.make_async_remote_copy(src_ref=src, dst_ref=dst, send_sem=ssem, recv_sem=rsem,
                                        device_id=dev, device_id_type=MESH_T)


def _own_slot(shard):
    me = 2 * lax.axis_index("x") + lax.axis_index("y")
    return lax.dynamic_update_slice(lax.empty((NSH,) + shard.shape, shard.dtype), shard[None], (me, 0, 0))


def _all_gather_weights(shards, lands):
    nt = len(shards)

    def body(*refs):
        srcs, dsts = refs[:nt], refs[2 * nt:3 * nt]
        isend, irecv, dsend, drecv = refs[3 * nt:]
        x, y, c, chips = _place()
        me = 2 * x + y
        sib = (x, y, 1 - c)
        sends = []
        for t in range(nt):
            half = srcs[t].shape[0] // 2
            mine = pl.ds(pl.multiple_of(c * half, 16), half)
            for j, (px, py) in enumerate(chips):
                cp = _remote(srcs[t].at[mine], dsts[t].at[me, mine], isend.at[t, j], irecv.at[t, j], (px, py, c))
                cp.start()
                sends.append(cp)
        for t in range(nt):
            half = srcs[t].shape[0] // 2
            mine = pl.ds(pl.multiple_of(c * half, 16), half)
            for j, (px, py) in enumerate(chips):
                blk = dsts[t].at[2 * px + py, mine]
                _remote(blk, blk, isend.at[t, j], irecv.at[t, j], (px, py, c)).wait_recv()
                cp = _remote(blk, blk, dsend.at[t, j], drecv.at[t, j], sib)
                cp.start()
                sends.append(cp)
        for t in range(nt):
            half = srcs[t].shape[0] // 2
            other = pl.ds(pl.multiple_of((1 - c) * half, 16), half)
            for j, (px, py) in enumerate(chips):
                blk = dsts[t].at[2 * px + py, other]
                _remote(blk, blk, dsend.at[t, j], drecv.at[t, j], sib).wait_recv()
        for cp in sends:
            cp.wait_send()

    return pl.pallas_call(
        body, name="allgather_weights",
        in_specs=[ANY] * (2 * nt), out_specs=[ANY] * nt,
        out_shape=[jax.ShapeDtypeStruct(a.shape, a.dtype) for a in lands],
        input_output_aliases={nt + t: t for t in range(nt)},
        scratch_shapes=[pltpu.SemaphoreType.DMA((nt, 3)), pltpu.SemaphoreType.DMA((nt, 3)),
                        pltpu.SemaphoreType.DMA((nt, 3)), pltpu.SemaphoreType.DMA((nt, 3))],
        compiler_params=_cp())(*shards, *lands)


def _to_sibling(arrs, name):
    nt = len(arrs)

    def body(*refs):
        srcs, dsts = refs[:nt], refs[nt:2 * nt]
        ssem, rsem = refs[2 * nt:]
        x, y, c, _ = _place()
        cps = [_remote(srcs[t], dsts[t], ssem.at[t], rsem.at[t], (x, y, 1 - c)) for t in range(nt)]
        for cp in cps:
            cp.start()
        for cp in cps:
            cp.wait()

    return pl.pallas_call(
        body, name=name, in_specs=[ANY] * nt, out_specs=[ANY] * nt,
        out_shape=[jax.ShapeDtypeStruct(a.shape, a.dtype) for a in arrs],
        scratch_shapes=[pltpu.SemaphoreType.DMA((nt,)), pltpu.SemaphoreType.DMA((nt,))],
        compiler_params=_cp())(*arrs)


def _scatter_to_chips(arrs):
    nt = len(arrs)

    def body(*refs):
        srcs, dsts = refs[:nt], refs[nt:2 * nt]
        ssem, rsem = refs[2 * nt:]
        x, y, c, chips = _place()
        cps = []
        for t in range(nt):
            for j, (px, py) in enumerate(chips):
                cps.append(_remote(srcs[t].at[2 * px + py], dsts[t].at[j], ssem.at[t, j], rsem.at[t, j], (px, py, c)))
        for cp in cps:
            cp.start()
        for cp in cps:
            cp.wait()

    return pl.pallas_call(
        body, name="scatter_to_chips", in_specs=[ANY] * nt, out_specs=[ANY] * nt,
        out_shape=[jax.ShapeDtypeStruct((3,) + a.shape[1:], a.dtype) for a in arrs],
        scratch_shapes=[pltpu.SemaphoreType.DMA((nt, 3)), pltpu.SemaphoreType.DMA((nt, 3))],
        compiler_params=_cp())(*arrs)


HBM_SPEC = pl.BlockSpec(memory_space=pltpu.HBM)
SEM_SPEC = pl.BlockSpec(memory_space=pltpu.SEMAPHORE)
DATAFLOW = pltpu.SideEffectType.DATAFLOW_SIDE_EFFECTING


def _gather_copies(srcs, lands, ssem, rsem):
    x, y, c, chips = _place()
    me = 2 * x + y
    out = []
    for t in range(len(srcs)):
        half = srcs[t].shape[0] // 2
        mine = pl.ds(pl.multiple_of(c * half, 16), half)
        for j, (px, py) in enumerate(chips):
            k = 3 * t + j
            send = _remote(srcs[t].at[mine], lands[t].at[me, mine], ssem.at[k], rsem.at[k], (px, py, c))
            blk = lands[t].at[2 * px + py, mine]
            out.append((send, _remote(blk, blk, ssem.at[k], rsem.at[k], (px, py, c))))
    return out


def _scatter_copies(srcs, lands, ssem, rsem):
    x, y, c, chips = _place()
    out = []
    for t in range(len(srcs)):
        for j, (px, py) in enumerate(chips):
            k = 3 * t + j
            send = _remote(srcs[t].at[2 * px + py], lands[t].at[j], ssem.at[k], rsem.at[k], (px, py, c))
            out.append((send, _remote(lands[t].at[j], lands[t].at[j], ssem.at[k], rsem.at[k], (px, py, c))))
    return out


def _sibling_copies(srcs, lands, ssem, rsem):
    x, y, c, _ = _place()
    sib = (x, y, 1 - c)
    return [(_remote(srcs[t], lands[t], ssem.at[t], rsem.at[t], sib),
             _remote(lands[t], lands[t], ssem.at[t], rsem.at[t], sib)) for t in range(len(srcs))]


def _split_start(plan, srcs, lands, after, name):
    ns, nbuf = len(srcs), len(srcs) + len(lands)
    ncopy = ns if plan is _sibling_copies else 3 * ns

    def body(*refs):
        ssem, rsem, token = refs[nbuf + 1], refs[nbuf + 2], refs[-1]
        for send, _ in plan(refs[:ns], refs[ns:nbuf], ssem, rsem):
            send.start()
        token[...] = jnp.zeros_like(token)

    bufs = [pltpu.with_memory_space_constraint(a, pltpu.HBM) for a in list(srcs) + list(lands)]
    outs = pl.pallas_call(
        body, name=name, in_specs=[HBM_SPEC] * nbuf + [ANY],
        out_specs=[SEM_SPEC, SEM_SPEC] + [HBM_SPEC] * nbuf + [pl.BlockSpec(memory_space=pltpu.VMEM)],
        out_shape=[pltpu.SemaphoreType.DMA((ncopy,)), pltpu.SemaphoreType.DMA((ncopy,))]
        + [pltpu.HBM(a.shape, a.dtype) for a in bufs] + [jax.ShapeDtypeStruct((8, 128), F32)],
        input_output_aliases={i: 2 + i for i in range(nbuf)},
        compiler_params=pltpu.CompilerParams(has_side_effects=DATAFLOW))(*bufs, after)
    return outs[0], outs[1], outs[2:2 + ns], outs[2 + ns:2 + nbuf], outs[-1]


def _split_wait(plan, ssem, rsem, srcs, lands, after, name):
    ns, nbuf = len(srcs), len(srcs) + len(lands)

    def body(*refs):
        s_ref, r_ref = refs[nbuf], refs[nbuf + 1]
        for send, recv in plan(refs[:ns], refs[ns:nbuf], s_ref, r_ref):
            send.wait_send()
            recv.wait_recv()

    outs = pl.pallas_call(
        body, name=name, in_specs=[HBM_SPEC] * nbuf + [SEM_SPEC, SEM_SPEC] + [ANY] * len(after),
        out_specs=[HBM_SPEC] * nbuf,
        out_shape=[pltpu.HBM(a.shape, a.dtype) for a in list(srcs) + list(lands)],
        input_output_aliases={i: i for i in range(nbuf)},
        compiler_params=pltpu.CompilerParams(has_side_effects=DATAFLOW))(*srcs, *lands, ssem, rsem, *after)
    return outs


def _gather_finish(lands):
    nt = len(lands)

    def body(*refs):
        outs = refs[nt:2 * nt]
        dsend, drecv = refs[2 * nt:]
        x, y, c, chips = _place()
        sib = (x, y, 1 - c)
        sends = []
        for t in range(nt):
            half = outs[t].shape[1] // 2
            mine = pl.ds(pl.multiple_of(c * half, 16), half)
            for j, (px, py) in enumerate(chips):
                blk = outs[t].at[2 * px + py, mine]
                cp = _remote(blk, blk, dsend.at[t, j], drecv.at[t, j], sib)
                cp.start()
                sends.append(cp)
        for t in range(nt):
            half = outs[t].shape[1] // 2
            other = pl.ds(pl.multiple_of((1 - c) * half, 16), half)
            for j, (px, py) in enumerate(chips):
                blk = outs[t].at[2 * px + py, other]
                _remote(blk, blk, dsend.at[t, j], drecv.at[t, j], sib).wait_recv()
        for cp in sends:
            cp.wait_send()

    return pl.pallas_call(
        body, name="gather_finish", in_specs=[ANY] * nt, out_specs=[ANY] * nt,
        out_shape=[jax.ShapeDtypeStruct(a.shape, a.dtype) for a in lands],
        input_output_aliases={t: t for t in range(nt)},
        scratch_shapes=[pltpu.SemaphoreType.DMA((nt, 3)), pltpu.SemaphoreType.DMA((nt, 3))],
        compiler_params=_cp())(*lands)


def _chip_partial(owns, recv, chip_arr, tag):
    nt = len(owns)

    def body(chip_ref, *refs):
        k = pl.program_id(0)
        for t in range(nt):
            p = refs[t][...] + refs[nt + t][...].astype(F32)
            refs[2 * nt + t][...] = p.astype(BF16)

            @pl.when(k == chip_ref[0])
            def _():
                refs[3 * nt + t][...] = p

    blk_specs = [pl.BlockSpec((None,) + a.shape[1:], lambda k, chip_ref: (k, 0, 0)) for a in owns]
    own_specs = [pl.BlockSpec(a.shape[1:], lambda k, chip_ref: (0, 0)) for a in owns]
    return pl.pallas_call(
        body, name="rs_chip_partial_" + tag,
        grid_spec=pltpu.PrefetchScalarGridSpec(num_scalar_prefetch=1, grid=(NSH,), in_specs=blk_specs * 2,
                                               out_specs=blk_specs + own_specs),
        out_shape=[jax.ShapeDtypeStruct(a.shape, BF16) for a in owns]
        + [jax.ShapeDtypeStruct(a.shape[1:], F32) for a in owns],
        compiler_params=_cp(1))(chip_arr, *owns, *recv)


def _sum_chips(own, recv):
    nt = len(own)

    def body(*refs):
        for t in range(nt):
            r = refs[nt + t]
            refs[2 * nt + t][...] = ((refs[t][...] + r[0].astype(F32)) + r[1].astype(F32)) + r[2].astype(F32)

    vm = pl.BlockSpec(memory_space=pltpu.VMEM)
    return pl.pallas_call(
        body, name="rs_sum_chips", in_specs=[vm] * (2 * nt), out_specs=[vm] * nt,
        out_shape=[jax.ShapeDtypeStruct(a.shape, F32) for a in own],
        compiler_params=_cp())(*own, *recv)


def _adamw_math(w, g, m, v):
    m = ADAM_B1 * m + (1.0 - ADAM_B1) * g
    v = ADAM_B2 * v + (1.0 - ADAM_B2) * (g * g)
    m_hat = m / (1.0 - ADAM_B1 ** ADAM_STEP)
    v_hat = v / (1.0 - ADAM_B2 ** ADAM_STEP)
    delta = -ADAM_LR * (m_hat / (jnp.sqrt(v_hat) + ADAM_EPS) + ADAM_WD * w)
    return delta, m, v


ADAM_SPLIT = 4


def _adamw_shards(own, sib, ws, ms, vs, c_arr):
    nt = len(own)

    def body(c_ref, *refs):
        hh = pl.program_id(0)
        mine = hh == c_ref[0]
        for t in range(nt):
            g = jnp.where(mine, refs[t][...], refs[nt + t][...])
            delta, m, v = _adamw_math(refs[2 * nt + t][...], g, refs[3 * nt + t][...], refs[4 * nt + t][...])
            refs[5 * nt + t][...] = g
            refs[6 * nt + t][...] = delta
            refs[7 * nt + t][...] = m
            refs[8 * nt + t][...] = v

    def tile(a):
        return (a.shape[0] // ADAM_SPLIT, a.shape[1])

    half_specs = [pl.BlockSpec(tile(a), lambda hh, q, c_ref: (q, 0)) for a in own]
    full_specs = [pl.BlockSpec(tile(a), lambda hh, q, c_ref: (hh * ADAM_SPLIT + q, 0)) for a in own]
    return pl.pallas_call(
        body, name="adamw_shards",
        grid_spec=pltpu.PrefetchScalarGridSpec(num_scalar_prefetch=1, grid=(2, ADAM_SPLIT),
                                               in_specs=half_specs * 2 + full_specs * 3, out_specs=full_specs * 4),
        out_shape=[jax.ShapeDtypeStruct(w.shape, F32) for w in ws] * 4,
        compiler_params=_cp(2))(c_arr, *own, *sib, *ws, *ms, *vs)


def _small_allreduce_adamw(gp, wp, mp, vp):
    rows = gp.shape[0]

    def body(g_ref, w_ref, m_ref, v_ref, go_ref, d_ref, mo_ref, vo_ref, gather, ssem, rsem):
        x, y, c, _ = _place()
        me = 4 * x + 2 * y + c
        gather[me] = g_ref[...]
        cps = []
        for kk in range(1, 8):
            px, py, pc = x ^ (kk >> 2), y ^ ((kk >> 1) & 1), c ^ (kk & 1)
            cp = _remote(g_ref, gather.at[me], ssem.at[kk], rsem.at[kk], (px, py, pc))
            cp.start()
            cps.append(cp)
        for kk in range(1, 8):
            px, py, pc = x ^ (kk >> 2), y ^ ((kk >> 1) & 1), c ^ (kk & 1)
            _remote(g_ref, gather.at[4 * px + 2 * py + pc], ssem.at[kk], rsem.at[kk], (px, py, pc)).wait_recv()
        for cp in cps:
            cp.wait_send()
        g = gather[0]
        for d in range(1, 8):
            g = g + gather[d]
        delta, m, v = _adamw_math(w_ref[...], g, m_ref[...], v_ref[...])
        go_ref[...] = g
        d_ref[...] = delta
        mo_ref[...] = m
        vo_ref[...] = v

    vm = pl.BlockSpec(memory_space=pltpu.VMEM)
    return pl.pallas_call(
        body, name="small_allreduce_adamw", in_specs=[vm] * 4, out_specs=[vm] * 4,
        out_shape=[jax.ShapeDtypeStruct(gp.shape, F32)] * 4,
        scratch_shapes=[pltpu.VMEM((8, rows, 128), F32), pltpu.SemaphoreType.DMA((8,)),
                        pltpu.SemaphoreType.DMA((8,))],
        compiler_params=_cp())(gp, wp, mp, vp)


SMALL_PARTS = (("g1", (1, D)), ("g2", (1, D)), ("g3", (1, D)), ("g4", (1, D)), ("w_pool", (1, 4, GROUP, GROUP)),
               ("pool_scale", (1, POOL_W)), ("rel_bias", (NBUCKET, NQ)), ("sinks", (1, NQ)), ("loss", (1, 1)))


def _pack_small(parts):
    rows = []
    for a in parts:
        flat = a.reshape(-1)
        n = flat.shape[0]
        padded = -(-n // 1024) * 1024
        rows.append(jnp.pad(flat, (0, padded - n)).reshape(-1, 128))
    return jnp.concatenate(rows, axis=0)


def _unpack_small(packed):
    out, r = [], 0
    for _, shape in SMALL_PARTS:
        n = int(np.prod(shape))
        nr = -(-n // 1024) * 8
        out.append(packed[r:r + nr].reshape(-1)[:n].reshape(shape))
        r += nr
    return out


def kernel(x, g_pre_mix, w_in, w_pool, pool_scale, rel_bias, sinks, w_out, g_post_mix, g_pre_ffn, w_gate, w_up, w_down, g_post_ffn, loss_target, m_g_pre_mix, m_w_in, m_w_pool, m_pool_scale, m_rel_bias, m_sinks, m_w_out, m_g_post_mix, m_g_pre_ffn, m_w_gate, m_w_up, m_w_down, m_g_post_ffn, v_g_pre_mix, v_w_in, v_w_pool, v_pool_scale, v_rel_bias, v_sinks, v_w_out, v_g_post_mix, v_g_pre_ffn, v_w_gate, v_w_up, v_w_down, v_g_post_ffn):
    c = lax.axis_index("c")
    chip = 2 * lax.axis_index("x") + lax.axis_index("y")

    def shards(a_in, a_out, a_gate, a_up, a_down):
        return (a_in[0].T, a_out[0], a_gate[0].T, a_up[0].T, a_down[0])

    c_arr = c.reshape(1).astype(jnp.int32)
    chip_arr = chip.reshape(1).astype(jnp.int32)

    big_w = shards(w_in, w_out, w_gate, w_up, w_down)
    big_bf = [w.astype(BF16) for w in big_w]
    win_all, wout_all = _all_gather_weights(big_bf[:2], [_own_slot(a) for a in big_bf[:2]])
    g_ssem, g_rsem, g_srcs, g_lands, g_token = _split_start(
        _gather_copies, big_bf[2:], [_own_slot(a) for a in big_bf[2:]], win_all, "gather_ffn_start")

    def ffn_weights(*after):
        outs = _split_wait(_gather_copies, g_ssem, g_rsem, g_srcs, g_lands, after, "gather_ffn_wait")
        return _gather_finish(outs[3:])

    rs = {}

    def on_ffn_grads(ffn_g):
        oths = ffn_g[1::2]
        rs["ffn_own"] = ffn_g[0::2]
        rs["x"] = _split_start(_sibling_copies, oths, [lax.empty(a.shape, BF16) for a in oths], ffn_g[0],
                               "rs_exchange_ffn_start")
        return rs["x"][4]

    def on_wout_grads(own, oth, after):
        ssem, rsem, srcs, lands, _ = rs["x"]
        recv_ffn = _split_wait(_sibling_copies, ssem, rsem, srcs, lands, [after], "rs_exchange_ffn_wait")[3:]
        recv_wout = _to_sibling([oth], "rs_exchange_wout")
        outs = _chip_partial([own] + list(rs["ffn_own"]), list(recv_wout) + list(recv_ffn), chip_arr, "early")
        rs["early_own"] = outs[4:]
        rs["s"] = _split_start(_scatter_copies, outs[:4], [lax.empty((3,) + a.shape[1:], BF16) for a in outs[:4]],
                               outs[4], "scatter_early_start")
        return rs["s"][4]

    small = (g_pre_mix + g_token[:1, :1], g_post_mix, g_pre_ffn, g_post_ffn, w_pool[0], pool_scale, rel_bias, sinks)
    loss, gx, small_grads, ((win_own, win_oth), _, _) = _local_step(
        x[0], loss_target[0], small, win_all.reshape(IN_W, D), wout_all.reshape(D, D), ffn_weights, c_arr,
        on_ffn_grads, on_wout_grads)
    ssem, rsem, srcs, lands, _ = rs["s"]
    recv_early = _split_wait(_scatter_copies, ssem, rsem, srcs, lands, [gx], "scatter_early_wait")[4:]

    recv_a = _to_sibling([win_oth], "rs_exchange_win")
    outs = _chip_partial([win_own], recv_a, chip_arr, "win")
    recv_win = _scatter_to_chips(outs[:1])
    final_half = _sum_chips([outs[1]] + list(rs["early_own"]), list(recv_win) + list(recv_early))
    sib_half = _to_sibling(final_half, "rs_sibling_share")
    adam = _adamw_shards(final_half, sib_half, big_w, shards(m_w_in, m_w_out, m_w_gate, m_w_up, m_w_down),
                         shards(v_w_in, v_w_out, v_w_gate, v_w_up, v_w_down), c_arr)
    big_g, big_d, big_m, big_v = adam[0:5], adam[5:10], adam[10:15], adam[15:20]

    unused = jnp.zeros((1, 1), F32)
    gp = _pack_small(small_grads + (loss,))
    wp = _pack_small((g_pre_mix, g_post_mix, g_pre_ffn, g_post_ffn, w_pool, pool_scale, rel_bias, sinks, unused))
    mp = _pack_small((m_g_pre_mix, m_g_post_mix, m_g_pre_ffn, m_g_post_ffn, m_w_pool, m_pool_scale, m_rel_bias,
                      m_sinks, unused))
    vp = _pack_small((v_g_pre_mix, v_g_post_mix, v_g_pre_ffn, v_g_post_ffn, v_w_pool, v_pool_scale, v_rel_bias,
                      v_sinks, unused))
    small_out = [_unpack_small(p) for p in _small_allreduce_adamw(gp, wp, mp, vp)]
    total_loss = small_out[0][8][0, 0]

    def ordered(small4, big4):
        sg1, sg2, sg3, sg4, swp, ssc, srb, ssk = small4[:8]
        bwin, bwout, bwg, bwu, bwd = big4[0].T[None], big4[1][None], big4[2].T[None], big4[3].T[None], big4[4][None]
        return [sg1, bwin, swp, ssc, srb, ssk, bwout, sg2, sg3, bwg, bwu, bwd, sg4]

    res = [total_loss, gx[None]]
    for sm, bg in zip(small_out, (big_g, big_d, big_m, big_v)):
        res += ordered(sm, bg)
    return tuple(res)
```

```python
import functools

import numpy as np
import jax
import jax.numpy as jnp
from jax import lax
from jax.experimental import pallas as pl
from jax.experimental.pallas import tpu as pltpu

F32 = jnp.float32
BF16 = jnp.bfloat16

D = 1024
POOL_W = 512
GROUP = 128
WINDOWS = (2, 4, 8, 16)
HALO = 128
NQ = 8
BLK = 128
NBUCKET = 32
IN_W = 1280
DFF = 2816
NSH = 4
FS = DFF // NSH
WIN_S = IN_W // NSH
WOUT_S = D // NSH
EPS = 1e-6
NEG = -1e30
SCALE = 0.125

ADAM_LR = 0.001
ADAM_B1 = 0.9
ADAM_B2 = 0.999
ADAM_EPS = 1e-08
ADAM_WD = 0.01
ADAM_STEP = 10

TM = 512
TM_POOL = 256
TM_FFN = 512
VMEM_LIMIT = 56 * 1024 * 1024

MESH_T = pl.DeviceIdType.MESH
ANY = pl.BlockSpec(memory_space=pl.ANY)


def _cp(n_grid=0, **kw):
    sem = ("arbitrary",) * n_grid if n_grid else None
    return pltpu.CompilerParams(dimension_semantics=sem, vmem_limit_bytes=VMEM_LIMIT, **kw)


def _dot(a, b):
    return jnp.dot(a, b, preferred_element_type=F32)


def _dot_nt(a, b):
    return lax.dot_general(a, b, (((1,), (1,)), ((), ())), preferred_element_type=F32)


def _dot_tn(a, b):
    return lax.dot_general(a, b, (((0,), (0,)), ((), ())), preferred_element_type=F32)


def _rms(x):
    r = lax.rsqrt(jnp.mean(x * x, axis=-1, keepdims=True) + EPS)
    return r, x * r


def _rms_bwd(r, n, g, dout):
    dn = dout * g
    dx = r * (dn - n * jnp.mean(dn * n, axis=-1, keepdims=True))
    dg = jnp.sum(dout * n, axis=0, keepdims=True)
    return dx, dg


def _split(x, n):
    parts = []
    r = x
    for _ in range(n):
        p = r.astype(BF16)
        parts.append(p)
        r = r - p.astype(F32)
    return parts


def _band_dot(a, x, n):
    acc = None
    for p in _split(x, n):
        t = _dot(a, p)
        acc = t if acc is None else acc + t
    return acc


def _bucket_table():
    qi = np.arange(BLK)[:, None]
    kj = np.arange(2 * BLK)[None, :]
    dist = qi + BLK - kj
    n = np.maximum(dist, 0)
    nf = np.maximum(n, 1).astype(np.float32)
    large = 16 + (np.log(nf / np.float32(16)) / np.float32(np.log(128 / 16)) * np.float32(16)).astype(np.int32)
    large = np.minimum(large, NBUCKET - 1)
    return np.where(n < 16, n, large).astype(np.int32)


def _inproj_fwd(x, g1, w_in):
    s = x.shape[0]
    tm = min(TM, s)

    def body(x_ref, g_ref, w_ref, u_ref, q_ref, k_ref, v_ref):
        _, n = _rms(x_ref[...])
        h = (n * g_ref[...]).astype(BF16)
        proj = _dot_nt(h, w_ref[...])
        u_ref[...] = proj[:, :512]
        q_ref[...] = proj[:, 512:1024].astype(BF16)
        k_ref[...] = proj[:, 1024:1152].astype(BF16)
        v_ref[...] = proj[:, 1152:1280].astype(BF16)

    row = lambda w: pl.BlockSpec((tm, w), lambda i: (i, 0))
    return pl.pallas_call(
        body, name="inproj_fwd", grid=(s // tm,),
        in_specs=[row(D), pl.BlockSpec((1, D), lambda i: (0, 0)), pl.BlockSpec((IN_W, D), lambda i: (0, 0))],
        out_specs=[row(512), row(512), row(128), row(128)],
        out_shape=[jax.ShapeDtypeStruct((s, 512), F32), jax.ShapeDtypeStruct((s, 512), BF16),
                   jax.ShapeDtypeStruct((s, 128), BF16), jax.ShapeDtypeStruct((s, 128), BF16)],
        compiler_params=_cp(1))(x, g1, w_in)


def _pooled(ext, cur, t0, tm):
    rows = lax.broadcasted_iota(jnp.int32, (tm, tm + HALO), 0)
    cols = lax.broadcasted_iota(jnp.int32, (tm, tm + HALO), 1)
    d = rows + HALO - cols
    t = t0 + lax.broadcasted_iota(jnp.int32, (tm, GROUP), 0)
    out = []
    for g, w in enumerate(WINDOWS):
        a = jnp.where((d >= 0) & (d < w), 1.0, 0.0).astype(BF16)
        ws = _band_dot(a, ext[:, g * GROUP:(g + 1) * GROUP], 3)
        cnt = jnp.minimum(t + 1, w).astype(F32)
        out.append(ws / cnt - cur[:, g * GROUP:(g + 1) * GROUP])
    return out


def _pool_fwd(u, w_pool, pool_scale):
    s = u.shape[0]
    tm = min(TM_POOL, s)
    hb = tm // HALO

    def body(uc_ref, uh_ref, wp_ref, sc_ref, o_ref):
        i = pl.program_id(0)
        cur = uc_ref[...]
        halo = jnp.where(i > 0, uh_ref[...], 0.0)
        ext = jnp.concatenate([halo, cur], axis=0)
        pooled = _pooled(ext, cur, i * tm, tm)
        for g in range(4):
            sl = slice(g * GROUP, (g + 1) * GROUP)
            mixed = _dot(pooled[g].astype(BF16), wp_ref[g])
            o_ref[:, sl] = (mixed * sc_ref[:, sl]).astype(BF16)

    return pl.pallas_call(
        body, name="pool_fwd", grid=(s // tm,),
        in_specs=[pl.BlockSpec((tm, 512), lambda i: (i, 0)),
                  pl.BlockSpec((HALO, 512), lambda i: (jnp.maximum(i * hb - 1, 0), 0)),
                  pl.BlockSpec((4, GROUP, GROUP), lambda i: (0, 0, 0)),
                  pl.BlockSpec((1, 512), lambda i: (0, 0))],
        out_specs=pl.BlockSpec((tm, 512), lambda i: (i, 0)),
        out_shape=jax.ShapeDtypeStruct((s, 512), BF16),
        compiler_params=_cp(1))(u, u, w_pool, pool_scale)


def _bias_table(bucket, rel_bias):
    def body(b_ref, rb_ref, o_ref):
        bucket_v = b_ref[...]
        rows = lax.broadcasted_iota(jnp.int32, (BLK, 2 * BLK), 0)
        cols = lax.broadcasted_iota(jnp.int32, (BLK, 2 * BLK), 1)
        dist = rows + BLK - cols
        inwin = (dist >= 0) & (dist < BLK)
        for h in range(NQ):
            acc = jnp.zeros((BLK, 2 * BLK), F32)
            for b in range(NBUCKET):
                acc = jnp.where(bucket_v == b, rb_ref[b, h], acc)
            rest = jnp.where(inwin, acc, NEG)
            o_ref[1, h] = rest
            o_ref[0, h] = jnp.where(cols >= BLK, rest, NEG)

    return pl.pallas_call(
        body, name="bias_table",
        in_specs=[pl.BlockSpec(memory_space=pltpu.VMEM), pl.BlockSpec(memory_space=pltpu.SMEM)],
        out_specs=pl.BlockSpec(memory_space=pltpu.VMEM),
        out_shape=jax.ShapeDtypeStruct((2, NQ, BLK, 2 * BLK), F32),
        compiler_params=_cp())(bucket, rel_bias)


def _kv_bands(k_ref, v_ref, n):
    pstart = pl.multiple_of(jnp.maximum(n - 1, 0) * BLK, BLK)
    cstart = pl.multiple_of(n * BLK, BLK)
    lo = lax.broadcasted_iota(jnp.int32, (2 * BLK, 128), 1) < 64
    out = []
    for ref in (k_ref, v_ref):
        band = jnp.concatenate([ref[pl.ds(pstart, BLK), :], ref[pl.ds(cstart, BLK), :]], axis=0).astype(F32)
        rot = pltpu.roll(band, 64, axis=1)
        out.append((jnp.where(lo, band, rot).astype(BF16), jnp.where(lo, rot, band).astype(BF16)))
    return out[0], out[1], lo, pstart, cstart


def _attn_probs(qm, kk, bias, sink):
    s = _dot_nt(qm, kk) + bias
    m = jnp.maximum(jnp.max(s, axis=-1, keepdims=True), sink)
    p = jnp.exp(s - m)
    es = jnp.exp(sink - m)
    inv = 1.0 / (jnp.sum(p, axis=-1, keepdims=True) + es)
    return p * inv, es * inv


def _attn_fwd(q, k, v, bias, sinks):
    s = q.shape[0]

    def body(q_ref, k_ref, v_ref, b_ref, sk_ref, o_ref):
        n = pl.program_id(0)
        kk, vv, lo, _, _ = _kv_bands(k_ref, v_ref, n)
        bidx = jnp.minimum(n, 1)
        lo_q = lax.broadcasted_iota(jnp.int32, (BLK, 128), 1) < 64
        for p in range(4):
            h = p // 2
            qt = q_ref[:, p * 128:(p + 1) * 128].astype(F32) * SCALE
            acc = jnp.zeros((BLK, 128), F32)
            for e in range(2):
                head = 2 * p + e
                sel_q = lo_q if e == 0 else jnp.logical_not(lo_q)
                sel_kv = lo if e == 0 else jnp.logical_not(lo)
                qm = jnp.where(sel_q, qt, 0.0).astype(BF16)
                prob, _ = _attn_probs(qm, kk[h], b_ref[bidx, head], sk_ref[0, head])
                vm = jnp.where(sel_kv, vv[h], jnp.zeros_like(vv[h]))
                acc = acc + _dot(prob.astype(BF16), vm)
            o_ref[:, p * 128:(p + 1) * 128] = acc.astype(BF16)

    return pl.pallas_call(
        body, name="attn_fwd", grid=(s // BLK,),
        in_specs=[pl.BlockSpec((BLK, 512), lambda i: (i, 0)),
                  pl.BlockSpec((s, 128), lambda i: (0, 0)),
                  pl.BlockSpec((s, 128), lambda i: (0, 0)),
                  pl.BlockSpec((2, NQ, BLK, 2 * BLK), lambda i: (0, 0, 0, 0)),
                  pl.BlockSpec(memory_space=pltpu.SMEM)],
        out_specs=pl.BlockSpec((BLK, 512), lambda i: (i, 0)),
        out_shape=jax.ShapeDtypeStruct((s, 512), BF16),
        compiler_params=_cp(1))(q, k, v, bias, sinks)


def _outproj_fwd(pool_o, attn_o, w_out, x, g2, g3):
    s = x.shape[0]
    tm = min(TM, s)

    def body(p_ref, a_ref, w_ref, x_ref, g2_ref, g3_ref, mix_ref, x1_ref, h2_ref):
        mix = _dot(p_ref[...], w_ref[0:512, :]) + _dot(a_ref[...], w_ref[512:1024, :])
        mix_ref[...] = mix
        _, n2 = _rms(mix)
        x1 = x_ref[...] + n2 * g2_ref[...]
        x1_ref[...] = x1
        _, n3 = _rms(x1)
        h2_ref[...] = (n3 * g3_ref[...]).astype(BF16)

    row = lambda w: pl.BlockSpec((tm, w), lambda i: (i, 0))
    vec = pl.BlockSpec((1, D), lambda i: (0, 0))
    return pl.pallas_call(
        body, name="outproj_fwd", grid=(s // tm,),
        in_specs=[row(512), row(512), pl.BlockSpec((D, D), lambda i: (0, 0)), row(D), vec, vec],
        out_specs=[row(D), row(D), row(D)],
        out_shape=[jax.ShapeDtypeStruct((s, D), F32), jax.ShapeDtypeStruct((s, D), F32),
                   jax.ShapeDtypeStruct((s, D), BF16)],
        compiler_params=_cp(1))(pool_o, attn_o, w_out, x, g2, g3)


def _silu_parts(g):
    sg = jax.nn.sigmoid(g)
    return sg, g * sg


def _ffn_fwd(h2, wg, wu, wd, x1, target, g4):
    s = h2.shape[0]
    tm = min(TM_FFN, s)

    def body(h_ref, wg_ref, wu_ref, wd_ref, x1_ref, t_ref, g4_ref,
             gate_ref, up_ref, df_ref, dy_ref, loss_ref, gg4_ref, f_acc):
        i = pl.program_id(0)
        j = pl.program_id(1)
        h = h_ref[...]
        gate = _dot_nt(h, wg_ref[...])
        up = _dot_nt(h, wu_ref[...])
        gate_ref[...] = gate.astype(BF16)
        up_ref[...] = up.astype(BF16)
        _, sl = _silu_parts(gate)
        contrib = _dot((sl * up).astype(BF16), wd_ref[...])

        @pl.when(j == 0)
        def _():
            f_acc[...] = contrib

        @pl.when(j > 0)
        def _():
            f_acc[...] += contrib

        @pl.when((i == 0) & (j == 0))
        def _():
            loss_ref[...] = jnp.zeros_like(loss_ref)
            gg4_ref[...] = jnp.zeros_like(gg4_ref)

        @pl.when(j == NSH - 1)
        def _():
            r4, n4 = _rms(f_acc[...])
            g4v = g4_ref[...]
            err = x1_ref[...] + n4 * g4v - t_ref[...]
            loss_ref[...] += (0.5 / D) * jnp.sum(err * err).reshape(1, 1)
            dy = err * (1.0 / D)
            dy_ref[...] = dy
            df, dg = _rms_bwd(r4, n4, g4v, dy)
            gg4_ref[...] += dg
            df_ref[...] = df.astype(BF16)

    row = lambda w: pl.BlockSpec((tm, w), lambda i, j: (i, 0))
    sh = lambda r, c: pl.BlockSpec((None, r, c), lambda i, j: (j, 0, 0))
    act = pl.BlockSpec((None, tm, FS), lambda i, j: (j, i, 0))
    vec = pl.BlockSpec((1, D), lambda i, j: (0, 0))
    return pl.pallas_call(
        body, name="ffn_fwd", grid=(s // tm, NSH),
        in_specs=[row(D), sh(FS, D), sh(FS, D), sh(FS, D), row(D), row(D), vec],
        out_specs=[act, act, row(D), row(D), pl.BlockSpec((1, 1), lambda i, j: (0, 0)), vec],
        out_shape=[jax.ShapeDtypeStruct((NSH, s, FS), BF16), jax.ShapeDtypeStruct((NSH, s, FS), BF16),
                   jax.ShapeDtypeStruct((s, D), BF16), jax.ShapeDtypeStruct((s, D), F32),
                   jax.ShapeDtypeStruct((1, 1), F32), jax.ShapeDtypeStruct((1, D), F32)],
        scratch_shapes=[pltpu.VMEM((tm, D), F32)],
        compiler_params=_cp(2))(h2, wg, wu, wd, x1, target, g4)


def _ffn_bwd_act(df, gate, up, wg, wu, wd, dy, x1, mix, g3, g2):
    s = df.shape[0]
    tm = min(TM_FFN, s)

    def body(df_ref, gate_ref, up_ref, wg_ref, wu_ref, wd_ref, dy_ref, x1_ref, mix_ref, g3_ref, g2_ref,
             dgate_ref, dup_ref, dx1_ref, dmix_ref, gg3_ref, gg2_ref, acc):
        i = pl.program_id(0)
        j = pl.program_id(1)
        da = _dot_nt(df_ref[...], wd_ref[...])
        g = gate_ref[...].astype(F32)
        u = up_ref[...].astype(F32)
        sg, sl = _silu_parts(g)
        dgate = (da * u * (sg * (1.0 + g * (1.0 - sg)))).astype(BF16)
        dup = (da * sl).astype(BF16)
        dgate_ref[...] = dgate
        dup_ref[...] = dup
        contrib = _dot(dgate, wg_ref[...]) + _dot(dup, wu_ref[...])

        @pl.when(j == 0)
        def _():
            acc[...] = contrib

        @pl.when(j > 0)
        def _():
            acc[...] += contrib

        @pl.when((i == 0) & (j == 0))
        def _():
            gg3_ref[...] = jnp.zeros_like(gg3_ref)
            gg2_ref[...] = jnp.zeros_like(gg2_ref)

        @pl.when(j == NSH - 1)
        def _():
            r3, n3 = _rms(x1_ref[...])
            dx1n, dg3 = _rms_bwd(r3, n3, g3_ref[...], acc[...])
            dx1 = dy_ref[...] + dx1n
            dx1_ref[...] = dx1
            gg3_ref[...] += dg3
            r2, n2 = _rms(mix_ref[...])
            dmix, dg2 = _rms_bwd(r2, n2, g2_ref[...], dx1)
            gg2_ref[...] += dg2
            dmix_ref[...] = dmix.astype(BF16)

    row = lambda w: pl.BlockSpec((tm, w), lambda i, j: (i, 0))
    sh = lambda r, c: pl.BlockSpec((None, r, c), lambda i, j: (j, 0, 0))
    act = pl.BlockSpec((None, tm, FS), lambda i, j: (j, i, 0))
    vec = pl.BlockSpec((1, D), lambda i, j: (0, 0))
    return pl.pallas_call(
        body, name="ffn_bwd_act", grid=(s // tm, NSH),
        in_specs=[row(D), act, act, sh(FS, D), sh(FS, D), sh(FS, D), row(D), row(D), row(D), vec, vec],
        out_specs=[act, act, row(D), row(D), vec, vec],
        out_shape=[jax.ShapeDtypeStruct((NSH, s, FS), BF16), jax.ShapeDtypeStruct((NSH, s, FS), BF16),
                   jax.ShapeDtypeStruct((s, D), F32), jax.ShapeDtypeStruct((s, D), BF16),
                   jax.ShapeDtypeStruct((1, D), F32), jax.ShapeDtypeStruct((1, D), F32)],
        scratch_shapes=[pltpu.VMEM((tm, D), F32)],
        compiler_params=_cp(2))(df, gate, up, wg, wu, wd, dy, x1, mix, g3, g2)


SMEM_SPEC = pl.BlockSpec(memory_space=pltpu.SMEM)


def _emit_halves(acc_ref, row0, rows, c, own_ref, oth_ref):
    half = rows // 2
    own_ref[...] = acc_ref[pl.ds(row0 + pl.multiple_of(c * half, 8), half), :]
    oth_ref[...] = acc_ref[pl.ds(row0 + pl.multiple_of((1 - c) * half, 8), half), :].astype(BF16)


def _half_shapes(rows):
    return [jax.ShapeDtypeStruct((NSH, rows // 2, D), F32), jax.ShapeDtypeStruct((NSH, rows // 2, D), BF16)]


def _ffn_bwd_w(h2, gate, up, dgate, dup, df, c_arr):
    s = h2.shape[0]
    tm = min(TM_FFN, s)
    nt = s // tm

    def body(c_ref, h_ref, gate_ref, up_ref, dgate_ref, dup_ref, df_ref, *rest):
        outs, accs = rest[:6], rest[6:]
        i = pl.program_id(1)
        @pl.when(i == 0)
        def _():
            for acc in accs:
                acc[...] = jnp.zeros_like(acc)

        h = h_ref[...]
        accs[0][...] += _dot_tn(dgate_ref[...], h)
        accs[1][...] += _dot_tn(dup_ref[...], h)
        _, sl = _silu_parts(gate_ref[...].astype(F32))
        a = (sl * up_ref[...].astype(F32)).astype(BF16)
        accs[2][...] += _dot_tn(a, df_ref[...])

        @pl.when(i == nt - 1)
        def _():
            for t, acc in enumerate(accs):
                _emit_halves(acc, 0, FS, c_ref[0], outs[2 * t], outs[2 * t + 1])

    row = lambda w: pl.BlockSpec((tm, w), lambda j, i: (i, 0))
    act = pl.BlockSpec((None, tm, FS), lambda j, i: (j, i, 0))
    half = pl.BlockSpec((None, FS // 2, D), lambda j, i: (j, 0, 0))
    return pl.pallas_call(
        body, name="ffn_bwd_w", grid=(NSH, nt),
        in_specs=[SMEM_SPEC, row(D), act, act, act, act, row(D)],
        out_specs=[half] * 6,
        out_shape=_half_shapes(FS) * 3,
        scratch_shapes=[pltpu.VMEM((FS, D), F32)] * 3,
        compiler_params=_cp(2))(c_arr, h2, gate, up, dgate, dup, df)


def _outproj_bwd(dmix, w_out, pool_o, attn_o, c_arr, dep=None):
    s = dmix.shape[0]
    tm = min(TM, s)
    nt = s // tm

    def body(c_ref, dm_ref, w_ref, p_ref, a_ref, *rest):
        dpool_ref, dattn_ref, own_ref, oth_ref, gw_ref = rest[-5:]
        i = pl.program_id(0)

        @pl.when(i == 0)
        def _():
            gw_ref[...] = jnp.zeros_like(gw_ref)

        dm = dm_ref[...]
        dpool_ref[...] = _dot_nt(dm, w_ref[0:512, :])
        dattn_ref[...] = _dot_nt(dm, w_ref[512:1024, :]).astype(BF16)
        gw_ref[0:512, :] += _dot_tn(p_ref[...], dm)
        gw_ref[512:1024, :] += _dot_tn(a_ref[...], dm)

        @pl.when(i == nt - 1)
        def _():
            for k in range(NSH):
                _emit_halves(gw_ref, k * WOUT_S, WOUT_S, c_ref[0], own_ref.at[k], oth_ref.at[k])

    row = lambda w: pl.BlockSpec((tm, w), lambda i: (i, 0))
    full = pl.BlockSpec((D, D), lambda i: (0, 0))
    half = pl.BlockSpec((NSH, WOUT_S // 2, D), lambda i: (0, 0, 0))
    return pl.pallas_call(
        body, name="outproj_bwd", grid=(nt,),
        in_specs=[SMEM_SPEC, row(D), full, row(512), row(512)] + ([] if dep is None else [ANY]),
        out_specs=[row(512), row(512), half, half],
        out_shape=[jax.ShapeDtypeStruct((s, 512), F32), jax.ShapeDtypeStruct((s, 512), BF16)] + _half_shapes(WOUT_S),
        scratch_shapes=[pltpu.VMEM((D, D), F32)],
        compiler_params=_cp(1))(c_arr, dmix, w_out, pool_o, attn_o, *([] if dep is None else [dep]))


def _pool_bwd(u, dpool, w_pool, pool_scale, dep=None):
    s = u.shape[0]
    tm = min(TM_POOL, s)
    hb = tm // HALO
    nt = s // tm
    last_halo = s // HALO - 1

    def body(uc_ref, uh_ref, dc_ref, dn_ref, wp_ref, sc_ref, *rest):
        du_ref, gwp_ref, gsc_ref = rest[-3:]
        i = pl.program_id(0)

        @pl.when(i == 0)
        def _():
            gwp_ref[...] = jnp.zeros_like(gwp_ref)
            gsc_ref[...] = jnp.zeros_like(gsc_ref)

        cur = uc_ref[...]
        halo = jnp.where(i > 0, uh_ref[...], 0.0)
        pooled = _pooled(jnp.concatenate([halo, cur], axis=0), cur, i * tm, tm)
        dcur = dc_ref[...]
        dnext = jnp.where(i < nt - 1, dn_ref[...], 0.0)
        dext = jnp.concatenate([dcur, dnext], axis=0)
        rows = lax.broadcasted_iota(jnp.int32, (tm, tm + HALO), 0)
        cols = lax.broadcasted_iota(jnp.int32, (tm, tm + HALO), 1)
        d = cols - rows
        t_ext = i * tm + lax.broadcasted_iota(jnp.int32, (tm + HALO, GROUP), 0)
        for g, w in enumerate(WINDOWS):
            sl = slice(g * GROUP, (g + 1) * GROUP)
            pb = pooled[g].astype(BF16)
            wp = wp_ref[g]
            mixed = _dot(pb, wp)
            gsc_ref[:, sl] += jnp.sum(dcur[:, sl] * mixed, axis=0, keepdims=True)
            dmixed = (dext[:, sl] * sc_ref[:, sl]).astype(BF16)
            gwp_ref[g] += _dot_tn(pb, dmixed[0:tm])
            dpooled = _dot_nt(dmixed, wp)
            z = dpooled / jnp.minimum(t_ext + 1, w).astype(F32)
            b = jnp.where((d >= 0) & (d < w), 1.0, 0.0).astype(BF16)
            du_ref[:, sl] = (_band_dot(b, z, 2) - dpooled[0:tm]).astype(BF16)

    return pl.pallas_call(
        body, name="pool_bwd", grid=(nt,),
        in_specs=[pl.BlockSpec((tm, 512), lambda i: (i, 0)),
                  pl.BlockSpec((HALO, 512), lambda i: (jnp.maximum(i * hb - 1, 0), 0)),
                  pl.BlockSpec((tm, 512), lambda i: (i, 0)),
                  pl.BlockSpec((HALO, 512), lambda i: (jnp.minimum((i + 1) * hb, last_halo), 0)),
                  pl.BlockSpec((4, GROUP, GROUP), lambda i: (0, 0, 0)),
                  pl.BlockSpec((1, 512), lambda i: (0, 0))] + ([] if dep is None else [ANY]),
        out_specs=[pl.BlockSpec((tm, 512), lambda i: (i, 0)),
                   pl.BlockSpec((4, GROUP, GROUP), lambda i: (0, 0, 0)),
                   pl.BlockSpec((1, 512), lambda i: (0, 0))],
        out_shape=[jax.ShapeDtypeStruct((s, 512), BF16), jax.ShapeDtypeStruct((4, GROUP, GROUP), F32),
                   jax.ShapeDtypeStruct((1, 512), F32)],
        compiler_params=_cp(1))(u, u, dpool, dpool, w_pool, pool_scale, *([] if dep is None else [dep]))


def _attn_bwd(q, k, v, do, bias, sinks, dep=None):
    s = q.shape[0]

    def body(q_ref, do_ref, k_ref, v_ref, b_ref, sk_ref, *rest):
        dq_ref, dk_ref, dv_ref, dss_ref, gsk_ref = rest[-5:]
        n = pl.program_id(0)

        @pl.when(n == 0)
        def _():
            dk_ref[...] = jnp.zeros_like(dk_ref)
            dv_ref[...] = jnp.zeros_like(dv_ref)
            dss_ref[...] = jnp.zeros_like(dss_ref)
            gsk_ref[...] = jnp.zeros_like(gsk_ref)

        kk, vv, lo, pstart, cstart = _kv_bands(k_ref, v_ref, n)
        bidx = jnp.minimum(n, 1)
        lo_q = lax.broadcasted_iota(jnp.int32, (BLK, 128), 1) < 64
        dkk = [jnp.zeros((2 * BLK, 128), F32), jnp.zeros((2 * BLK, 128), F32)]
        dvv = [jnp.zeros((2 * BLK, 128), F32), jnp.zeros((2 * BLK, 128), F32)]
        for p in range(4):
            h = p // 2
            qt = q_ref[:, p * 128:(p + 1) * 128].astype(F32) * SCALE
            dot = do_ref[:, p * 128:(p + 1) * 128]
            dq = jnp.zeros((BLK, 128), F32)
            for e in range(2):
                head = 2 * p + e
                sel_q = lo_q if e == 0 else jnp.logical_not(lo_q)
                sel_kv = lo if e == 0 else jnp.logical_not(lo)
                qm = jnp.where(sel_q, qt, 0.0).astype(BF16)
                prob, ps = _attn_probs(qm, kk[h], b_ref[bidx, head], sk_ref[0, head])
                dom = jnp.where(sel_q, dot, jnp.zeros_like(dot))
                dp = _dot_nt(dom, vv[h])
                delta = jnp.sum(prob * dp, axis=-1, keepdims=True)
                ds = prob * (dp - delta)
                gsk_ref[pl.ds(head, 1), :] += jnp.broadcast_to(-jnp.sum(ps * delta).reshape(1, 1), (1, 128))
                dss_ref[head] += ds
                dsb = ds.astype(BF16)
                km = jnp.where(sel_kv, kk[h], jnp.zeros_like(kk[h]))
                dq = dq + _dot(dsb, km)
                dkk[h] = dkk[h] + _dot_tn(dsb, qm)
                dvv[h] = dvv[h] + _dot_tn(prob.astype(BF16), dom)
            dq_ref[:, p * 128:(p + 1) * 128] = (dq * SCALE).astype(BF16)
        for acc, ref in ((dkk, dk_ref), (dvv, dv_ref)):
            f0 = acc[0] + pltpu.roll(acc[0], 64, axis=1)
            f1 = acc[1] + pltpu.roll(acc[1], 64, axis=1)
            band = jnp.where(lo, f0, f1)
            ref[pl.ds(pstart, BLK), :] += band[0:BLK]
            ref[pl.ds(cstart, BLK), :] += band[BLK:2 * BLK]

    blk = pl.BlockSpec((BLK, 512), lambda i: (i, 0))
    kv = pl.BlockSpec((s, 128), lambda i: (0, 0))
    return pl.pallas_call(
        body, name="attn_bwd", grid=(s // BLK,),
        in_specs=[blk, blk, kv, kv, pl.BlockSpec((2, NQ, BLK, 2 * BLK), lambda i: (0, 0, 0, 0)),
                  pl.BlockSpec(memory_space=pltpu.SMEM)] + ([] if dep is None else [ANY]),
        out_specs=[blk, kv, kv, pl.BlockSpec((NQ, BLK, 2 * BLK), lambda i: (0, 0, 0)),
                   pl.BlockSpec((NQ, 128), lambda i: (0, 0))],
        out_shape=[jax.ShapeDtypeStruct((s, 512), BF16), jax.ShapeDtypeStruct((s, 128), F32),
                   jax.ShapeDtypeStruct((s, 128), F32), jax.ShapeDtypeStruct((NQ, BLK, 2 * BLK), F32),
                   jax.ShapeDtypeStruct((NQ, 128), F32)],
        compiler_params=_cp(1))(q, do, k, v, bias, sinks, *([] if dep is None else [dep]))


def _relbias_grad(dss, bucket):
    def body(ds_ref, b_ref, o_ref):
        bucket_v = b_ref[...]
        lane = lax.broadcasted_iota(jnp.int32, (NQ, 128), 1)
        head_row = lax.broadcasted_iota(jnp.int32, (NQ, 2 * BLK), 0)
        acc = jnp.zeros((NQ, 128), F32)
        for b in range(NBUCKET):
            sel = bucket_v == b
            stack = jnp.zeros((NQ, 2 * BLK), F32)
            for h in range(NQ):
                col_sums = jnp.sum(jnp.where(sel, ds_ref[h], 0.0), axis=0, keepdims=True)
                stack = jnp.where(head_row == h, col_sums, stack)
            acc = acc + jnp.where(lane == b, jnp.sum(stack, axis=1, keepdims=True), 0.0)
        o_ref[...] = acc

    return pl.pallas_call(
        body, name="relbias_grad",
        in_specs=[pl.BlockSpec(memory_space=pltpu.VMEM), pl.BlockSpec(memory_space=pltpu.VMEM)],
        out_specs=pl.BlockSpec(memory_space=pltpu.VMEM),
        out_shape=jax.ShapeDtypeStruct((NQ, 128), F32),
        compiler_params=_cp())(dss, bucket)


def _inproj_bwd(x, g1, du, dq, dk, dv, w_in, dx1, c_arr):
    s = x.shape[0]
    tm = min(TM, s)
    nt = s // tm
    cols = ((0, 512), (512, 1024), (1024, 1152), (1152, 1280))

    def body(c_ref, x_ref, g_ref, du_ref, dq_ref, dk_ref, dv_ref, w_ref, dx1_ref, gx_ref, own_ref, oth_ref, gg_ref,
             gw_ref):
        i = pl.program_id(0)

        @pl.when(i == 0)
        def _():
            gw_ref[...] = jnp.zeros_like(gw_ref)
            gg_ref[...] = jnp.zeros_like(gg_ref)

        gv = g_ref[...]
        r1, n1 = _rms(x_ref[...])
        h = (n1 * gv).astype(BF16)
        parts = (du_ref[...], dq_ref[...], dk_ref[...].astype(BF16), dv_ref[...].astype(BF16))
        dh = None
        for (a, b), dpart in zip(cols, parts):
            t = _dot(dpart, w_ref[a:b, :])
            dh = t if dh is None else dh + t
            gw_ref[a:b, :] += _dot_tn(dpart, h)
        dx, dg = _rms_bwd(r1, n1, gv, dh)
        gg_ref[...] += dg
        gx_ref[...] = dx1_ref[...] + dx

        @pl.when(i == nt - 1)
        def _():
            for k in range(NSH):
                _emit_halves(gw_ref, k * WIN_S, WIN_S, c_ref[0], own_ref.at[k], oth_ref.at[k])

    row = lambda w: pl.BlockSpec((tm, w), lambda i: (i, 0))
    vec = pl.BlockSpec((1, D), lambda i: (0, 0))
    full = pl.BlockSpec((IN_W, D), lambda i: (0, 0))
    half = pl.BlockSpec((NSH, WIN_S // 2, D), lambda i: (0, 0, 0))
    return pl.pallas_call(
        body, name="inproj_bwd", grid=(nt,),
        in_specs=[SMEM_SPEC, row(D), vec, row(512), row(512), row(128), row(128), full, row(D)],
        out_specs=[row(D), half, half, vec],
        out_shape=[jax.ShapeDtypeStruct((s, D), F32)] + _half_shapes(WIN_S) + [jax.ShapeDtypeStruct((1, D), F32)],
        scratch_shapes=[pltpu.VMEM((IN_W, D), F32)],
        compiler_params=_cp(1))(c_arr, x, g1, du, dq, dk, dv, w_in, dx1)


def _local_step(x, target, small, w_in, w_out, ffn_weights, c_arr, on_ffn_grads=None, on_wout_grads=None):
    g1, g2, g3, g4, w_pool, pool_scale, rel_bias, sinks = small
    bucket = jnp.asarray(_bucket_table())
    wp_bf = w_pool.astype(BF16)
    u, q, k, v = _inproj_fwd(x, g1, w_in)
    pool_o = _pool_fwd(u, wp_bf, pool_scale)
    bias = _bias_table(bucket, rel_bias)
    attn_o = _attn_fwd(q, k, v, bias, sinks)
    wg, wu, wd = ffn_weights(pool_o, attn_o) if callable(ffn_weights) else ffn_weights
    mix, x1, h2 = _outproj_fwd(pool_o, attn_o, w_out, x, g2, g3)
    gate, up, df, dy, loss, gg4 = _ffn_fwd(h2, wg, wu, wd, x1, target, g4)
    dgate, dup, dx1, dmix, gg3, gg2 = _ffn_bwd_act(df, gate, up, wg, wu, wd, dy, x1, mix, g3, g2)
    ffn_g = _ffn_bwd_w(h2, gate, up, dgate, dup, df, c_arr)
    dep = None if on_ffn_grads is None else on_ffn_grads(ffn_g)
    dpool, dattn, wout_own, wout_oth = _outproj_bwd(dmix, w_out, pool_o, attn_o, c_arr, dep)
    dep = None if on_wout_grads is None else on_wout_grads(wout_own, wout_oth, dpool)
    du, gwp, gsc = _pool_bwd(u, dpool, wp_bf, pool_scale, dep)
    dq, dk, dv, dss, gsk = _attn_bwd(q, k, v, dattn, bias, sinks, dep)
    grb = _relbias_grad(dss, bucket)
    gx, win_own, win_oth, gg1 = _inproj_bwd(x, g1, du, dq, dk, dv, w_in, dx1, c_arr)
    small_grads = (gg1, gg2, gg3, gg4, gwp, gsc, grb[:, :NBUCKET].T, gsk[:, 0].reshape(1, NQ))
    return loss, gx, small_grads, ((win_own, win_oth), (wout_own, wout_oth), ffn_g)


def _place():
    x, y, c = lax.axis_index("x"), lax.axis_index("y"), lax.axis_index("c")
    chips = ((1 - x, y), (x, 1 - y), (1 - x, 1 - y))
    return x, y, c, chips


def _remote(src, dst, ssem, rsem, dev):
    return pltpu.make_async_remote_copy(src_ref=src, dst_ref=dst, send_sem=ssem, recv_sem=rsem,
                                        device_id=dev, device_id_type=MESH_T)


def _own_slot(shard):
    me = 2 * lax.axis_index("x") + lax.axis_index("y")
    return lax.dynamic_update_slice(lax.empty((NSH,) + shard.shape, shard.dtype), shard[None], (me, 0, 0))


def _all_gather_weights(shards, lands):
    nt = len(shards)

    def body(*refs):
        srcs, dsts = refs[:nt], refs[2 * nt:3 * nt]
        isend, irecv, dsend, drecv = refs[3 * nt:]
        x, y, c, chips = _place()
        me = 2 * x + y
        sib = (x, y, 1 - c)
        sends = []
        for t in range(nt):
            half = srcs[t].shape[0] // 2
            mine = pl.ds(pl.multiple_of(c * half, 16), half)
            for j, (px, py) in enumerate(chips):
                cp = _remote(srcs[t].at[mine], dsts[t].at[me, mine], isend.at[t, j], irecv.at[t, j], (px, py, c))
                cp.start()
                sends.append(cp)
        for t in range(nt):
            half = srcs[t].shape[0] // 2
            mine = pl.ds(pl.multiple_of(c * half, 16), half)
            for j, (px, py) in enumerate(chips):
                blk = dsts[t].at[2 * px + py, mine]
                _remote(blk, blk, isend.at[t, j], irecv.at[t, j], (px, py, c)).wait_recv()
                cp = _remote(blk, blk, dsend.at[t, j], drecv.at[t, j], sib)
                cp.start()
                sends.append(cp)
        for t in range(nt):
            half = srcs[t].shape[0] // 2
            other = pl.ds(pl.multiple_of((1 - c) * half, 16), half)
            for j, (px, py) in enumerate(chips):
                blk = dsts[t].at[2 * px + py, other]
                _remote(blk, blk, dsend.at[t, j], drecv.at[t, j], sib).wait_recv()
        for cp in sends:
            cp.wait_send()

    return pl.pallas_call(
        body, name="allgather_weights",
        in_specs=[ANY] * (2 * nt), out_specs=[ANY] * nt,
        out_shape=[jax.ShapeDtypeStruct(a.shape, a.dtype) for a in lands],
        input_output_aliases={nt + t: t for t in range(nt)},
        scratch_shapes=[pltpu.SemaphoreType.DMA((nt, 3)), pltpu.SemaphoreType.DMA((nt, 3)),
                        pltpu.SemaphoreType.DMA((nt, 3)), pltpu.SemaphoreType.DMA((nt, 3))],
        compiler_params=_cp())(*shards, *lands)


def _to_sibling(arrs, name):
    nt = len(arrs)

    def body(*refs):
        srcs, dsts = refs[:nt], refs[nt:2 * nt]
        ssem, rsem = refs[2 * nt:]
        x, y, c, _ = _place()
        cps = [_remote(srcs[t], dsts[t], ssem.at[t], rsem.at[t], (x, y, 1 - c)) for t in range(nt)]
        for cp in cps:
            cp.start()
        for cp in cps:
            cp.wait()

    return pl.pallas_call(
        body, name=name, in_specs=[ANY] * nt, out_specs=[ANY] * nt,
        out_shape=[jax.ShapeDtypeStruct(a.shape, a.dtype) for a in arrs],
        scratch_shapes=[pltpu.SemaphoreType.DMA((nt,)), pltpu.SemaphoreType.DMA((nt,))],
        compiler_params=_cp())(*arrs)


def _scatter_to_chips(arrs):
    nt = len(arrs)

    def body(*refs):
        srcs, dsts = refs[:nt], refs[nt:2 * nt]
        ssem, rsem = refs[2 * nt:]
        x, y, c, chips = _place()
        cps = []
        for t in range(nt):
            for j, (px, py) in enumerate(chips):
                cps.append(_remote(srcs[t].at[2 * px + py], dsts[t].at[j], ssem.at[t, j], rsem.at[t, j], (px, py, c)))
        for cp in cps:
            cp.start()
        for cp in cps:
            cp.wait()

    return pl.pallas_call(
        body, name="scatter_to_chips", in_specs=[ANY] * nt, out_specs=[ANY] * nt,
        out_shape=[jax.ShapeDtypeStruct((3,) + a.shape[1:], a.dtype) for a in arrs],
        scratch_shapes=[pltpu.SemaphoreType.DMA((nt, 3)), pltpu.SemaphoreType.DMA((nt, 3))],
        compiler_params=_cp())(*arrs)


HBM_SPEC = pl.BlockSpec(memory_space=pltpu.HBM)
SEM_SPEC = pl.BlockSpec(memory_space=pltpu.SEMAPHORE)
DATAFLOW = pltpu.SideEffectType.DATAFLOW_SIDE_EFFECTING


def _gather_copies(srcs, lands, ssem, rsem):
    x, y, c, chips = _place()
    me = 2 * x + y
    out = []
    for t in range(len(srcs)):
        half = srcs[t].shape[0] // 2
        mine = pl.ds(pl.multiple_of(c * half, 16), half)
        for j, (px, py) in enumerate(chips):
            k = 3 * t + j
            send = _remote(srcs[t].at[mine], lands[t].at[me, mine], ssem.at[k], rsem.at[k], (px, py, c))
            blk = lands[t].at[2 * px + py, mine]
            out.append((send, _remote(blk, blk, ssem.at[k], rsem.at[k], (px, py, c))))
    return out


def _scatter_copies(srcs, lands, ssem, rsem):
    x, y, c, chips = _place()
    out = []
    for t in range(len(srcs)):
        for j, (px, py) in enumerate(chips):
            k = 3 * t + j
            send = _remote(srcs[t].at[2 * px + py], lands[t].at[j], ssem.at[k], rsem.at[k], (px, py, c))
            out.append((send, _remote(lands[t].at[j], lands[t].at[j], ssem.at[k], rsem.at[k], (px, py, c))))
    return out


def _sibling_copies(srcs, lands, ssem, rsem):
    x, y, c, _ = _place()
    sib = (x, y, 1 - c)
    return [(_remote(srcs[t], lands[t], ssem.at[t], rsem.at[t], sib),
             _remote(lands[t], lands[t], ssem.at[t], rsem.at[t], sib)) for t in range(len(srcs))]


def _peer_copies(srcs, lands, ssem, rsem):
    x, y, c, _ = _place()
    me = 4 * x + 2 * y + c
    out = []
    for kk in range(1, 8):
        px, py, pc = x ^ (kk >> 2), y ^ ((kk >> 1) & 1), c ^ (kk & 1)
        send = _remote(srcs[0], lands[0].at[me], ssem.at[kk - 1], rsem.at[kk - 1], (px, py, pc))
        slot = lands[0].at[4 * px + 2 * py + pc]
        out.append((send, _remote(slot, slot, ssem.at[kk - 1], rsem.at[kk - 1], (px, py, pc))))
    return out


def _split_start(plan, ncopy, srcs, lands, after, name):
    ns, nbuf = len(srcs), len(srcs) + len(lands)

    def body(*refs):
        ssem, rsem, token = refs[nbuf + 1], refs[nbuf + 2], refs[-1]
        for send, _ in plan(refs[:ns], refs[ns:nbuf], ssem, rsem):
            send.start()
        token[...] = jnp.zeros_like(token)

    bufs = [pltpu.with_memory_space_constraint(a, pltpu.HBM) for a in list(srcs) + list(lands)]
    outs = pl.pallas_call(
        body, name=name, in_specs=[HBM_SPEC] * nbuf + [ANY],
        out_specs=[SEM_SPEC, SEM_SPEC] + [HBM_SPEC] * nbuf + [pl.BlockSpec(memory_space=pltpu.VMEM)],
        out_shape=[pltpu.SemaphoreType.DMA((ncopy,)), pltpu.SemaphoreType.DMA((ncopy,))]
        + [pltpu.HBM(a.shape, a.dtype) for a in bufs] + [jax.ShapeDtypeStruct((8, 128), F32)],
        input_output_aliases={i: 2 + i for i in range(nbuf)},
        compiler_params=pltpu.CompilerParams(has_side_effects=DATAFLOW))(*bufs, after)
    return outs[0], outs[1], outs[2:2 + ns], outs[2 + ns:2 + nbuf], outs[-1]


def _split_wait(plan, ssem, rsem, srcs, lands, after, name):
    ns, nbuf = len(srcs), len(srcs) + len(lands)

    def body(*refs):
        s_ref, r_ref = refs[nbuf], refs[nbuf + 1]
        for send, recv in plan(refs[:ns], refs[ns:nbuf], s_ref, r_ref):
            send.wait_send()
            recv.wait_recv()

    outs = pl.pallas_call(
        body, name=name, in_specs=[HBM_SPEC] * nbuf + [SEM_SPEC, SEM_SPEC] + [ANY] * len(after),
        out_specs=[HBM_SPEC] * nbuf,
        out_shape=[pltpu.HBM(a.shape, a.dtype) for a in list(srcs) + list(lands)],
        input_output_aliases={i: i for i in range(nbuf)},
        compiler_params=pltpu.CompilerParams(has_side_effects=DATAFLOW))(*srcs, *lands, ssem, rsem, *after)
    return outs


def _gather_finish(lands):
    nt = len(lands)

    def body(*refs):
        outs = refs[nt:2 * nt]
        dsend, drecv = refs[2 * nt:]
        x, y, c, chips = _place()
        sib = (x, y, 1 - c)
        sends = []
        for t in range(nt):
            half = outs[t].shape[1] // 2
            mine = pl.ds(pl.multiple_of(c * half, 16), half)
            for j, (px, py) in enumerate(chips):
                blk = outs[t].at[2 * px + py, mine]
                cp = _remote(blk, blk, dsend.at[t, j], drecv.at[t, j], sib)
                cp.start()
                sends.append(cp)
        for t in range(nt):
            half = outs[t].shape[1] // 2
            other = pl.ds(pl.multiple_of((1 - c) * half, 16), half)
            for j, (px, py) in enumerate(chips):
                blk = outs[t].at[2 * px + py, other]
                _remote(blk, blk, dsend.at[t, j], drecv.at[t, j], sib).wait_recv()
        for cp in sends:
            cp.wait_send()

    return pl.pallas_call(
        body, name="gather_finish", in_specs=[ANY] * nt, out_specs=[ANY] * nt,
        out_shape=[jax.ShapeDtypeStruct(a.shape, a.dtype) for a in lands],
        input_output_aliases={t: t for t in range(nt)},
        scratch_shapes=[pltpu.SemaphoreType.DMA((nt, 3)), pltpu.SemaphoreType.DMA((nt, 3))],
        compiler_params=_cp())(*lands)


def _chip_partial(owns, recv, chip_arr, tag):
    nt = len(owns)

    def body(chip_ref, *refs):
        k = pl.program_id(0)
        for t in range(nt):
            p = refs[t][...] + refs[nt + t][...].astype(F32)
            refs[2 * nt + t][...] = p.astype(BF16)

            @pl.when(k == chip_ref[0])
            def _():
                refs[3 * nt + t][...] = p

    blk_specs = [pl.BlockSpec((None,) + a.shape[1:], lambda k, chip_ref: (k, 0, 0)) for a in owns]
    own_specs = [pl.BlockSpec(a.shape[1:], lambda k, chip_ref: (0, 0)) for a in owns]
    return pl.pallas_call(
        body, name="rs_chip_partial_" + tag,
        grid_spec=pltpu.PrefetchScalarGridSpec(num_scalar_prefetch=1, grid=(NSH,), in_specs=blk_specs * 2,
                                               out_specs=blk_specs + own_specs),
        out_shape=[jax.ShapeDtypeStruct(a.shape, BF16) for a in owns]
        + [jax.ShapeDtypeStruct(a.shape[1:], F32) for a in owns],
        compiler_params=_cp(1))(chip_arr, *owns, *recv)


def _sum_chips(own, recv, tag):
    nt = len(own)

    def body(*refs):
        for t in range(nt):
            r = refs[nt + t]
            refs[2 * nt + t][...] = ((refs[t][...] + r[0].astype(F32)) + r[1].astype(F32)) + r[2].astype(F32)

    vm = pl.BlockSpec(memory_space=pltpu.VMEM)
    return pl.pallas_call(
        body, name="rs_sum_chips_" + tag, in_specs=[vm] * (2 * nt), out_specs=[vm] * nt,
        out_shape=[jax.ShapeDtypeStruct(a.shape, F32) for a in own],
        compiler_params=_cp())(*own, *recv)


def _adamw_math(w, g, m, v):
    m = ADAM_B1 * m + (1.0 - ADAM_B1) * g
    v = ADAM_B2 * v + (1.0 - ADAM_B2) * (g * g)
    m_hat = m / (1.0 - ADAM_B1 ** ADAM_STEP)
    v_hat = v / (1.0 - ADAM_B2 ** ADAM_STEP)
    delta = -ADAM_LR * (m_hat / (jnp.sqrt(v_hat) + ADAM_EPS) + ADAM_WD * w)
    return delta, m, v


ADAM_SPLIT = 4


def _adamw_shards(own, sib, ws, ms, vs, c_arr, tag):
    nt = len(own)

    def body(c_ref, *refs):
        hh = pl.program_id(0)
        mine = hh == c_ref[0]
        for t in range(nt):
            g = jnp.where(mine, refs[t][...], refs[nt + t][...])
            delta, m, v = _adamw_math(refs[2 * nt + t][...], g, refs[3 * nt + t][...], refs[4 * nt + t][...])
            refs[5 * nt + t][...] = g
            refs[6 * nt + t][...] = delta
            refs[7 * nt + t][...] = m
            refs[8 * nt + t][...] = v

    def tile(a):
        return (a.shape[0] // ADAM_SPLIT, a.shape[1])

    half_specs = [pl.BlockSpec(tile(a), lambda hh, q, c_ref: (q, 0)) for a in own]
    full_specs = [pl.BlockSpec(tile(a), lambda hh, q, c_ref: (hh * ADAM_SPLIT + q, 0)) for a in own]
    return pl.pallas_call(
        body, name="adamw_shards_" + tag,
        grid_spec=pltpu.PrefetchScalarGridSpec(num_scalar_prefetch=1, grid=(2, ADAM_SPLIT),
                                               in_specs=half_specs * 2 + full_specs * 3, out_specs=full_specs * 4),
        out_shape=[jax.ShapeDtypeStruct(w.shape, F32) for w in ws] * 4,
        compiler_params=_cp(2))(c_arr, *own, *sib, *ws, *ms, *vs)


def _small_sum_adamw(gathered, wp, mp, vp):
    def body(g_ref, w_ref, m_ref, v_ref, go_ref, d_ref, mo_ref, vo_ref):
        g = g_ref[0]
        for d in range(1, 8):
            g = g + g_ref[d]
        delta, m, v = _adamw_math(w_ref[...], g, m_ref[...], v_ref[...])
        go_ref[...] = g
        d_ref[...] = delta
        mo_ref[...] = m
        vo_ref[...] = v

    vm = pl.BlockSpec(memory_space=pltpu.VMEM)
    return pl.pallas_call(
        body, name="small_sum_adamw", in_specs=[vm] * 4, out_specs=[vm] * 4,
        out_shape=[jax.ShapeDtypeStruct(wp.shape, F32)] * 4,
        compiler_params=_cp())(gathered, wp, mp, vp)


SMALL_PARTS = (("g1", (1, D)), ("g2", (1, D)), ("g3", (1, D)), ("g4", (1, D)), ("w_pool", (1, 4, GROUP, GROUP)),
               ("pool_scale", (1, POOL_W)), ("rel_bias", (NBUCKET, NQ)), ("sinks", (1, NQ)), ("loss", (1, 1)))


def _pack_small(parts):
    rows = []
    for a in parts:
        flat = a.reshape(-1)
        n = flat.shape[0]
        padded = -(-n // 1024) * 1024
        rows.append(jnp.pad(flat, (0, padded - n)).reshape(-1, 128))
    return jnp.concatenate(rows, axis=0)


def _unpack_small(packed):
    out, r = [], 0
    for _, shape in SMALL_PARTS:
        n = int(np.prod(shape))
        nr = -(-n // 1024) * 8
        out.append(packed[r:r + nr].reshape(-1)[:n].reshape(shape))
        r += nr
    return out


def kernel(x, g_pre_mix, w_in, w_pool, pool_scale, rel_bias, sinks, w_out, g_post_mix, g_pre_ffn, w_gate, w_up, w_down, g_post_ffn, loss_target, m_g_pre_mix, m_w_in, m_w_pool, m_pool_scale, m_rel_bias, m_sinks, m_w_out, m_g_post_mix, m_g_pre_ffn, m_w_gate, m_w_up, m_w_down, m_g_post_ffn, v_g_pre_mix, v_w_in, v_w_pool, v_pool_scale, v_rel_bias, v_sinks, v_w_out, v_g_post_mix, v_g_pre_ffn, v_w_gate, v_w_up, v_w_down, v_g_post_ffn):
    c = lax.axis_index("c")
    chip = 2 * lax.axis_index("x") + lax.axis_index("y")

    def shards(a_in, a_out, a_gate, a_up, a_down):
        return (a_in[0].T, a_out[0], a_gate[0].T, a_up[0].T, a_down[0])

    c_arr = c.reshape(1).astype(jnp.int32)
    chip_arr = chip.reshape(1).astype(jnp.int32)

    big_w = shards(w_in, w_out, w_gate, w_up, w_down)
    big_bf = [w.astype(BF16) for w in big_w]
    win_all, wout_all = _all_gather_weights(big_bf[:2], [_own_slot(a) for a in big_bf[:2]])
    g_ssem, g_rsem, g_srcs, g_lands, g_token = _split_start(
        _gather_copies, 9, big_bf[2:], [_own_slot(a) for a in big_bf[2:]], win_all, "gather_ffn_start")

    def ffn_weights(*after):
        outs = _split_wait(_gather_copies, g_ssem, g_rsem, g_srcs, g_lands, after, "gather_ffn_wait")
        return _gather_finish(outs[3:])

    rs = {}

    def on_ffn_grads(ffn_g):
        oths = ffn_g[1::2]
        rs["ffn_own"] = ffn_g[0::2]
        rs["x"] = _split_start(_sibling_copies, 3, oths, [lax.empty(a.shape, BF16) for a in oths], ffn_g[0],
                               "rs_exchange_ffn_start")
        return rs["x"][4]

    def on_wout_grads(own, oth, after):
        ssem, rsem, srcs, lands, _ = rs["x"]
        recv_ffn = _split_wait(_sibling_copies, ssem, rsem, srcs, lands, [after], "rs_exchange_ffn_wait")[3:]
        recv_wout = _to_sibling([oth], "rs_exchange_wout")
        outs = _chip_partial([own] + list(rs["ffn_own"]), list(recv_wout) + list(recv_ffn), chip_arr, "early")
        rs["early_own"] = outs[4:]
        rs["s"] = _split_start(_scatter_copies, 12, outs[:4],
                               [lax.empty((3,) + a.shape[1:], BF16) for a in outs[:4]], outs[4], "scatter_early_start")
        return rs["s"][4]

    small = (g_pre_mix + g_token[:1, :1], g_post_mix, g_pre_ffn, g_post_ffn, w_pool[0], pool_scale, rel_bias, sinks)
    loss, gx, small_grads, ((win_own, win_oth), _, _) = _local_step(
        x[0], loss_target[0], small, win_all.reshape(IN_W, D), wout_all.reshape(D, D), ffn_weights, c_arr,
        on_ffn_grads, on_wout_grads)
    ssem, rsem, srcs, lands, _ = rs["s"]
    recv_early = _split_wait(_scatter_copies, ssem, rsem, srcs, lands, [gx], "scatter_early_wait")[4:]

    unused = jnp.zeros((1, 1), F32)
    gp = _pack_small(small_grads + (loss,))
    wp = _pack_small((g_pre_mix, g_post_mix, g_pre_ffn, g_post_ffn, w_pool, pool_scale, rel_bias, sinks, unused))
    mp = _pack_small((m_g_pre_mix, m_g_post_mix, m_g_pre_ffn, m_g_post_ffn, m_w_pool, m_pool_scale, m_rel_bias,
                      m_sinks, unused))
    vp = _pack_small((v_g_pre_mix, v_g_post_mix, v_g_pre_ffn, v_g_post_ffn, v_w_pool, v_pool_scale, v_rel_bias,
                      v_sinks, unused))
    slots = lax.dynamic_update_slice(lax.empty((8,) + gp.shape, F32), gp[None], (2 * chip + c, 0, 0))
    p_ssem, p_rsem, p_srcs, p_lands, p_token = _split_start(_peer_copies, 7, [gp], [slots], gx,
                                                            "small_allgather_start")

    moments = (shards(m_w_in, m_w_out, m_w_gate, m_w_up, m_w_down),
               shards(v_w_in, v_w_out, v_w_gate, v_w_up, v_w_down))
    recv_a = _to_sibling([win_oth], "rs_exchange_win")
    outs = _chip_partial([win_own], recv_a, chip_arr, "win")
    w_ssem, w_rsem, w_srcs, w_lands, _ = _split_start(
        _scatter_copies, 3, outs[:1], [lax.empty((3,) + outs[0].shape[1:], BF16)], p_token, "scatter_win_start")
    early_final = _sum_chips(list(rs["early_own"]), list(recv_early), "early")
    early_sib = _to_sibling(early_final, "rs_share_early")
    adam_e = _adamw_shards(early_final, early_sib, big_w[1:], moments[0][1:], moments[1][1:], c_arr, "early")
    recv_win = _split_wait(_scatter_copies, w_ssem, w_rsem, w_srcs, w_lands, [adam_e[0]], "scatter_win_wait")[1:]
    win_final = _sum_chips([outs[1]], recv_win, "win")
    win_sib = _to_sibling(win_final, "rs_share_win")
    adam_w = _adamw_shards(win_final, win_sib, big_w[:1], moments[0][:1], moments[1][:1], c_arr, "win")
    big_g, big_d, big_m, big_v = [[adam_w[i]] + list(adam_e[4 * i:4 * i + 4]) for i in range(4)]

    gathered = _split_wait(_peer_copies, p_ssem, p_rsem, p_srcs, p_lands, [adam_w[0]], "small_allgather_wait")[1]
    small_out = [_unpack_small(p) for p in _small_sum_adamw(gathered, wp, mp, vp)]
    total_loss = small_out[0][8][0, 0]

    def ordered(small4, big4):
        sg1, sg2, sg3, sg4, swp, ssc, srb, ssk = small4[:8]
        bwin, bwout, bwg, bwu, bwd = big4[0].T[None], big4[1][None], big4[2].T[None], big4[3].T[None], big4[4][None]
        return [sg1, bwin, swp, ssc, srb, ssk, bwout, sg2, sg3, bwg, bwu, bwd, sg4]

    res = [total_loss, gx[None]]
    for sm, bg in zip(small_out, (big_g, big_d, big_m, big_v)):
        res += ordered(sm, bg)
    return tuple(res)
```

```python
import functools

import numpy as np
import jax
import jax.numpy as jnp
from jax import lax
from jax.experimental import pallas as pl
from jax.experimental.pallas import tpu as pltpu

F32 = jnp.float32
BF16 = jnp.bfloat16

D = 1024
POOL_W = 512
GROUP = 128
WINDOWS = (2, 4, 8, 16)
HALO = 128
NQ = 8
BLK = 128
NBUCKET = 32
IN_W = 1280
DFF = 2816
NSH = 4
FS = DFF // NSH
WIN_S = IN_W // NSH
WOUT_S = D // NSH
EPS = 1e-6
NEG = -1e30
SCALE = 0.125

ADAM_LR = 0.001
ADAM_B1 = 0.9
ADAM_B2 = 0.999
ADAM_EPS = 1e-08
ADAM_WD = 0.01
ADAM_STEP = 10

TM = 512
TM_POOL = 256
TM_FFN = 512
VMEM_LIMIT = 56 * 1024 * 1024

MESH_T = pl.DeviceIdType.MESH
ANY = pl.BlockSpec(memory_space=pl.ANY)


def _cp(n_grid=0, **kw):
    sem = ("arbitrary",) * n_grid if n_grid else None
    return pltpu.CompilerParams(dimension_semantics=sem, vmem_limit_bytes=VMEM_LIMIT, **kw)


def _dot(a, b):
    return jnp.dot(a, b, preferred_element_type=F32)


def _dot_nt(a, b):
    return lax.dot_general(a, b, (((1,), (1,)), ((), ())), preferred_element_type=F32)


def _dot_tn(a, b):
    return lax.dot_general(a, b, (((0,), (0,)), ((), ())), preferred_element_type=F32)


def _rms(x):
    r = lax.rsqrt(jnp.mean(x * x, axis=-1, keepdims=True) + EPS)
    return r, x * r


def _rms_bwd(r, n, g, dout):
    dn = dout * g
    dx = r * (dn - n * jnp.mean(dn * n, axis=-1, keepdims=True))
    dg = jnp.sum(dout * n, axis=0, keepdims=True)
    return dx, dg


def _split(x, n):
    parts = []
    r = x
    for _ in range(n):
        p = r.astype(BF16)
        parts.append(p)
        r = r - p.astype(F32)
    return parts


def _band_dot(a, x, n):
    acc = None
    for p in _split(x, n):
        t = _dot(a, p)
        acc = t if acc is None else acc + t
    return acc


def _bucket_table():
    qi = np.arange(BLK)[:, None]
    kj = np.arange(2 * BLK)[None, :]
    dist = qi + BLK - kj
    n = np.maximum(dist, 0)
    nf = np.maximum(n, 1).astype(np.float32)
    large = 16 + (np.log(nf / np.float32(16)) / np.float32(np.log(128 / 16)) * np.float32(16)).astype(np.int32)
    large = np.minimum(large, NBUCKET - 1)
    return np.where(n < 16, n, large).astype(np.int32)


def _inproj_fwd(x, g1, w_in):
    s = x.shape[0]
    tm = min(TM, s)

    def body(x_ref, g_ref, w_ref, u_ref, q_ref, k_ref, v_ref):
        _, n = _rms(x_ref[...])
        h = (n * g_ref[...]).astype(BF16)
        proj = _dot_nt(h, w_ref[...])
        u_ref[...] = proj[:, :512]
        q_ref[...] = proj[:, 512:1024].astype(BF16)
        k_ref[...] = proj[:, 1024:1152].astype(BF16)
        v_ref[...] = proj[:, 1152:1280].astype(BF16)

    row = lambda w: pl.BlockSpec((tm, w), lambda i: (i, 0))
    return pl.pallas_call(
        body, name="inproj_fwd", grid=(s // tm,),
        in_specs=[row(D), pl.BlockSpec((1, D), lambda i: (0, 0)), pl.BlockSpec((IN_W, D), lambda i: (0, 0))],
        out_specs=[row(512), row(512), row(128), row(128)],
        out_shape=[jax.ShapeDtypeStruct((s, 512), F32), jax.ShapeDtypeStruct((s, 512), BF16),
                   jax.ShapeDtypeStruct((s, 128), BF16), jax.ShapeDtypeStruct((s, 128), BF16)],
        compiler_params=_cp(1))(x, g1, w_in)


def _pooled(ext, cur, t0, tm):
    rows = lax.broadcasted_iota(jnp.int32, (tm, tm + HALO), 0)
    cols = lax.broadcasted_iota(jnp.int32, (tm, tm + HALO), 1)
    d = rows + HALO - cols
    t = t0 + lax.broadcasted_iota(jnp.int32, (tm, GROUP), 0)
    out = []
    for g, w in enumerate(WINDOWS):
        a = jnp.where((d >= 0) & (d < w), 1.0, 0.0).astype(BF16)
        ws = _band_dot(a, ext[:, g * GROUP:(g + 1) * GROUP], 3)
        cnt = jnp.minimum(t + 1, w).astype(F32)
        out.append(ws / cnt - cur[:, g * GROUP:(g + 1) * GROUP])
    return out


def _pool_fwd(u, w_pool, pool_scale):
    s = u.shape[0]
    tm = min(TM_POOL, s)
    hb = tm // HALO

    def body(uc_ref, uh_ref, wp_ref, sc_ref, o_ref):
        i = pl.program_id(0)
        cur = uc_ref[...]
        halo = jnp.where(i > 0, uh_ref[...], 0.0)
        ext = jnp.concatenate([halo, cur], axis=0)
        pooled = _pooled(ext, cur, i * tm, tm)
        for g in range(4):
            sl = slice(g * GROUP, (g + 1) * GROUP)
            mixed = _dot(pooled[g].astype(BF16), wp_ref[g])
            o_ref[:, sl] = (mixed * sc_ref[:, sl]).astype(BF16)

    return pl.pallas_call(
        body, name="pool_fwd", grid=(s // tm,),
        in_specs=[pl.BlockSpec((tm, 512), lambda i: (i, 0)),
                  pl.BlockSpec((HALO, 512), lambda i: (jnp.maximum(i * hb - 1, 0), 0)),
                  pl.BlockSpec((4, GROUP, GROUP), lambda i: (0, 0, 0)),
                  pl.BlockSpec((1, 512), lambda i: (0, 0))],
        out_specs=pl.BlockSpec((tm, 512), lambda i: (i, 0)),
        out_shape=jax.ShapeDtypeStruct((s, 512), BF16),
        compiler_params=_cp(1))(u, u, w_pool, pool_scale)


def _bias_table(bucket, rel_bias):
    def body(b_ref, rb_ref, o_ref):
        bucket_v = b_ref[...]
        rows = lax.broadcasted_iota(jnp.int32, (BLK, 2 * BLK), 0)
        cols = lax.broadcasted_iota(jnp.int32, (BLK, 2 * BLK), 1)
        dist = rows + BLK - cols
        inwin = (dist >= 0) & (dist < BLK)
        for h in range(NQ):
            acc = jnp.zeros((BLK, 2 * BLK), F32)
            for b in range(NBUCKET):
                acc = jnp.where(bucket_v == b, rb_ref[b, h], acc)
            rest = jnp.where(inwin, acc, NEG)
            o_ref[1, h] = rest
            o_ref[0, h] = jnp.where(cols >= BLK, rest, NEG)

    return pl.pallas_call(
        body, name="bias_table",
        in_specs=[pl.BlockSpec(memory_space=pltpu.VMEM), pl.BlockSpec(memory_space=pltpu.SMEM)],
        out_specs=pl.BlockSpec(memory_space=pltpu.VMEM),
        out_shape=jax.ShapeDtypeStruct((2, NQ, BLK, 2 * BLK), F32),
        compiler_params=_cp())(bucket, rel_bias)


def _kv_bands(k_ref, v_ref, n):
    pstart = pl.multiple_of(jnp.maximum(n - 1, 0) * BLK, BLK)
    cstart = pl.multiple_of(n * BLK, BLK)
    lo = lax.broadcasted_iota(jnp.int32, (2 * BLK, 128), 1) < 64
    out = []
    for ref in (k_ref, v_ref):
        band = jnp.concatenate([ref[pl.ds(pstart, BLK), :], ref[pl.ds(cstart, BLK), :]], axis=0).astype(F32)
        rot = pltpu.roll(band, 64, axis=1)
        out.append((jnp.where(lo, band, rot).astype(BF16), jnp.where(lo, rot, band).astype(BF16)))
    return out[0], out[1], lo, pstart, cstart


def _attn_probs(qm, kk, bias, sink):
    s = _dot_nt(qm, kk) + bias
    m = jnp.maximum(jnp.max(s, axis=-1, keepdims=True), sink)
    p = jnp.exp(s - m)
    es = jnp.exp(sink - m)
    inv = 1.0 / (jnp.sum(p, axis=-1, keepdims=True) + es)
    return p * inv, es * inv


def _attn_fwd(q, k, v, bias, sinks):
    s = q.shape[0]

    def body(q_ref, k_ref, v_ref, b_ref, sk_ref, o_ref):
        n = pl.program_id(0)
        kk, vv, lo, _, _ = _kv_bands(k_ref, v_ref, n)
        bidx = jnp.minimum(n, 1)
        lo_q = lax.broadcasted_iota(jnp.int32, (BLK, 128), 1) < 64
        for p in range(4):
            h = p // 2
            qt = q_ref[:, p * 128:(p + 1) * 128].astype(F32) * SCALE
            acc = jnp.zeros((BLK, 128), F32)
            for e in range(2):
                head = 2 * p + e
                sel_q = lo_q if e == 0 else jnp.logical_not(lo_q)
                sel_kv = lo if e == 0 else jnp.logical_not(lo)
                qm = jnp.where(sel_q, qt, 0.0).astype(BF16)
                prob, _ = _attn_probs(qm, kk[h], b_ref[bidx, head], sk_ref[0, head])
                vm = jnp.where(sel_kv, vv[h], jnp.zeros_like(vv[h]))
                acc = acc + _dot(prob.astype(BF16), vm)
            o_ref[:, p * 128:(p + 1) * 128] = acc.astype(BF16)

    return pl.pallas_call(
        body, name="attn_fwd", grid=(s // BLK,),
        in_specs=[pl.BlockSpec((BLK, 512), lambda i: (i, 0)),
                  pl.BlockSpec((s, 128), lambda i: (0, 0)),
                  pl.BlockSpec((s, 128), lambda i: (0, 0)),
                  pl.BlockSpec((2, NQ, BLK, 2 * BLK), lambda i: (0, 0, 0, 0)),
                  pl.BlockSpec(memory_space=pltpu.SMEM)],
        out_specs=pl.BlockSpec((BLK, 512), lambda i: (i, 0)),
        out_shape=jax.ShapeDtypeStruct((s, 512), BF16),
        compiler_params=_cp(1))(q, k, v, bias, sinks)


def _outproj_fwd(pool_o, attn_o, w_out, x, g2, g3):
    s = x.shape[0]
    tm = min(TM, s)

    def body(p_ref, a_ref, w_ref, x_ref, g2_ref, g3_ref, mix_ref, x1_ref, h2_ref):
        mix = _dot(p_ref[...], w_ref[0:512, :]) + _dot(a_ref[...], w_ref[512:1024, :])
        mix_ref[...] = mix
        _, n2 = _rms(mix)
        x1 = x_ref[...] + n2 * g2_ref[...]
        x1_ref[...] = x1
        _, n3 = _rms(x1)
        h2_ref[...] = (n3 * g3_ref[...]).astype(BF16)

    row = lambda w: pl.BlockSpec((tm, w), lambda i: (i, 0))
    vec = pl.BlockSpec((1, D), lambda i: (0, 0))
    return pl.pallas_call(
        body, name="outproj_fwd", grid=(s // tm,),
        in_specs=[row(512), row(512), pl.BlockSpec((D, D), lambda i: (0, 0)), row(D), vec, vec],
        out_specs=[row(D), row(D), row(D)],
        out_shape=[jax.ShapeDtypeStruct((s, D), F32), jax.ShapeDtypeStruct((s, D), F32),
                   jax.ShapeDtypeStruct((s, D), BF16)],
        compiler_params=_cp(1))(pool_o, attn_o, w_out, x, g2, g3)


def _silu_parts(g):
    sg = jax.nn.sigmoid(g)
    return sg, g * sg


def _ffn_fwd(h2, wg, wu, wd, x1, target, g4):
    s = h2.shape[0]
    tm = min(TM_FFN, s)

    def body(h_ref, wg_ref, wu_ref, wd_ref, x1_ref, t_ref, g4_ref,
             gate_ref, up_ref, df_ref, dy_ref, loss_ref, gg4_ref, f_acc):
        i = pl.program_id(0)
        j = pl.program_id(1)
        h = h_ref[...]
        gate = _dot_nt(h, wg_ref[...])
        up = _dot_nt(h, wu_ref[...])
        gate_ref[...] = gate.astype(BF16)
        up_ref[...] = up.astype(BF16)
        _, sl = _silu_parts(gate)
        contrib = _dot((sl * up).astype(BF16), wd_ref[...])

        @pl.when(j == 0)
        def _():
            f_acc[...] = contrib

        @pl.when(j > 0)
        def _():
            f_acc[...] += contrib

        @pl.when((i == 0) & (j == 0))
        def _():
            loss_ref[...] = jnp.zeros_like(loss_ref)
            gg4_ref[...] = jnp.zeros_like(gg4_ref)

        @pl.when(j == NSH - 1)
        def _():
            r4, n4 = _rms(f_acc[...])
            g4v = g4_ref[...]
            err = x1_ref[...] + n4 * g4v - t_ref[...]
            loss_ref[...] += (0.5 / D) * jnp.sum(err * err).reshape(1, 1)
            dy = err * (1.0 / D)
            dy_ref[...] = dy
            df, dg = _rms_bwd(r4, n4, g4v, dy)
            gg4_ref[...] += dg
            df_ref[...] = df.astype(BF16)

    row = lambda w: pl.BlockSpec((tm, w), lambda i, j: (i, 0))
    sh = lambda r, c: pl.BlockSpec((None, r, c), lambda i, j: (j, 0, 0))
    act = pl.BlockSpec((None, tm, FS), lambda i, j: (j, i, 0))
    vec = pl.BlockSpec((1, D), lambda i, j: (0, 0))
    return pl.pallas_call(
        body, name="ffn_fwd", grid=(s // tm, NSH),
        in_specs=[row(D), sh(FS, D), sh(FS, D), sh(FS, D), row(D), row(D), vec],
        out_specs=[act, act, row(D), row(D), pl.BlockSpec((1, 1), lambda i, j: (0, 0)), vec],
        out_shape=[jax.ShapeDtypeStruct((NSH, s, FS), BF16), jax.ShapeDtypeStruct((NSH, s, FS), BF16),
                   jax.ShapeDtypeStruct((s, D), BF16), jax.ShapeDtypeStruct((s, D), F32),
                   jax.ShapeDtypeStruct((1, 1), F32), jax.ShapeDtypeStruct((1, D), F32)],
        scratch_shapes=[pltpu.VMEM((tm, D), F32)],
        compiler_params=_cp(2))(h2, wg, wu, wd, x1, target, g4)


def _ffn_bwd_act(df, gate, up, wg, wu, wd, dy, x1, mix, g3, g2):
    s = df.shape[0]
    tm = min(TM_FFN, s)

    def body(df_ref, gate_ref, up_ref, wg_ref, wu_ref, wd_ref, dy_ref, x1_ref, mix_ref, g3_ref, g2_ref,
             dgate_ref, dup_ref, dx1_ref, dmix_ref, gg3_ref, gg2_ref, acc):
        i = pl.program_id(0)
        j = pl.program_id(1)
        da = _dot_nt(df_ref[...], wd_ref[...])
        g = gate_ref[...].astype(F32)
        u = up_ref[...].astype(F32)
        sg, sl = _silu_parts(g)
        dgate = (da * u * (sg * (1.0 + g * (1.0 - sg)))).astype(BF16)
        dup = (da * sl).astype(BF16)
        dgate_ref[...] = dgate
        dup_ref[...] = dup
        contrib = _dot(dgate, wg_ref[...]) + _dot(dup, wu_ref[...])

        @pl.when(j == 0)
        def _():
            acc[...] = contrib

        @pl.when(j > 0)
        def _():
            acc[...] += contrib

        @pl.when((i == 0) & (j == 0))
        def _():
            gg3_ref[...] = jnp.zeros_like(gg3_ref)
            gg2_ref[...] = jnp.zeros_like(gg2_ref)

        @pl.when(j == NSH - 1)
        def _():
            r3, n3 = _rms(x1_ref[...])
            dx1n, dg3 = _rms_bwd(r3, n3, g3_ref[...], acc[...])
            dx1 = dy_ref[...] + dx1n
            dx1_ref[...] = dx1
            gg3_ref[...] += dg3
            r2, n2 = _rms(mix_ref[...])
            dmix, dg2 = _rms_bwd(r2, n2, g2_ref[...], dx1)
            gg2_ref[...] += dg2
            dmix_ref[...] = dmix.astype(BF16)

    row = lambda w: pl.BlockSpec((tm, w), lambda i, j: (i, 0))
    sh = lambda r, c: pl.BlockSpec((None, r, c), lambda i, j: (j, 0, 0))
    act = pl.BlockSpec((None, tm, FS), lambda i, j: (j, i, 0))
    vec = pl.BlockSpec((1, D), lambda i, j: (0, 0))
    return pl.pallas_call(
        body, name="ffn_bwd_act", grid=(s // tm, NSH),
        in_specs=[row(D), act, act, sh(FS, D), sh(FS, D), sh(FS, D), row(D), row(D), row(D), vec, vec],
        out_specs=[act, act, row(D), row(D), vec, vec],
        out_shape=[jax.ShapeDtypeStruct((NSH, s, FS), BF16), jax.ShapeDtypeStruct((NSH, s, FS), BF16),
                   jax.ShapeDtypeStruct((s, D), F32), jax.ShapeDtypeStruct((s, D), BF16),
                   jax.ShapeDtypeStruct((1, D), F32), jax.ShapeDtypeStruct((1, D), F32)],
        scratch_shapes=[pltpu.VMEM((tm, D), F32)],
        compiler_params=_cp(2))(df, gate, up, wg, wu, wd, dy, x1, mix, g3, g2)


SMEM_SPEC = pl.BlockSpec(memory_space=pltpu.SMEM)


def _emit_halves(acc_ref, row0, rows, c, own_ref, oth_ref):
    half = rows // 2
    own_ref[...] = acc_ref[pl.ds(row0 + pl.multiple_of(c * half, 8), half), :]
    oth_ref[...] = acc_ref[pl.ds(row0 + pl.multiple_of((1 - c) * half, 8), half), :].astype(BF16)


def _half_shapes(rows):
    return [jax.ShapeDtypeStruct((NSH, rows // 2, D), F32), jax.ShapeDtypeStruct((NSH, rows // 2, D), BF16)]


def _ffn_bwd_w(h2, gate, up, dgate, dup, df, c_arr):
    s = h2.shape[0]
    tm = min(TM_FFN, s)
    nt = s // tm

    def body(c_ref, h_ref, gate_ref, up_ref, dgate_ref, dup_ref, df_ref, *rest):
        outs, accs = rest[:6], rest[6:]
        i = pl.program_id(1)
        @pl.when(i == 0)
        def _():
            for acc in accs:
                acc[...] = jnp.zeros_like(acc)

        h = h_ref[...]
        accs[0][...] += _dot_tn(dgate_ref[...], h)
        accs[1][...] += _dot_tn(dup_ref[...], h)
        _, sl = _silu_parts(gate_ref[...].astype(F32))
        a = (sl * up_ref[...].astype(F32)).astype(BF16)
        accs[2][...] += _dot_tn(a, df_ref[...])

        @pl.when(i == nt - 1)
        def _():
            for t, acc in enumerate(accs):
                _emit_halves(acc, 0, FS, c_ref[0], outs[2 * t], outs[2 * t + 1])

    row = lambda w: pl.BlockSpec((tm, w), lambda j, i: (i, 0))
    act = pl.BlockSpec((None, tm, FS), lambda j, i: (j, i, 0))
    half = pl.BlockSpec((None, FS // 2, D), lambda j, i: (j, 0, 0))
    return pl.pallas_call(
        body, name="ffn_bwd_w", grid=(NSH, nt),
        in_specs=[SMEM_SPEC, row(D), act, act, act, act, row(D)],
        out_specs=[half] * 6,
        out_shape=_half_shapes(FS) * 3,
        scratch_shapes=[pltpu.VMEM((FS, D), F32)] * 3,
        compiler_params=_cp(2))(c_arr, h2, gate, up, dgate, dup, df)


def _outproj_bwd(dmix, w_out, pool_o, attn_o, c_arr, dep=None):
    s = dmix.shape[0]
    tm = min(TM, s)
    nt = s // tm

    def body(c_ref, dm_ref, w_ref, p_ref, a_ref, *rest):
        dpool_ref, dattn_ref, own_ref, oth_ref, gw_ref = rest[-5:]
        i = pl.program_id(0)

        @pl.when(i == 0)
        def _():
            gw_ref[...] = jnp.zeros_like(gw_ref)

        dm = dm_ref[...]
        dpool_ref[...] = _dot_nt(dm, w_ref[0:512, :])
        dattn_ref[...] = _dot_nt(dm, w_ref[512:1024, :]).astype(BF16)
        gw_ref[0:512, :] += _dot_tn(p_ref[...], dm)
        gw_ref[512:1024, :] += _dot_tn(a_ref[...], dm)

        @pl.when(i == nt - 1)
        def _():
            for k in range(NSH):
                _emit_halves(gw_ref, k * WOUT_S, WOUT_S, c_ref[0], own_ref.at[k], oth_ref.at[k])

    row = lambda w: pl.BlockSpec((tm, w), lambda i: (i, 0))
    full = pl.BlockSpec((D, D), lambda i: (0, 0))
    half = pl.BlockSpec((NSH, WOUT_S // 2, D), lambda i: (0, 0, 0))
    return pl.pallas_call(
        body, name="outproj_bwd", grid=(nt,),
        in_specs=[SMEM_SPEC, row(D), full, row(512), row(512)] + ([] if dep is None else [ANY]),
        out_specs=[row(512), row(512), half, half],
        out_shape=[jax.ShapeDtypeStruct((s, 512), F32), jax.ShapeDtypeStruct((s, 512), BF16)] + _half_shapes(WOUT_S),
        scratch_shapes=[pltpu.VMEM((D, D), F32)],
        compiler_params=_cp(1))(c_arr, dmix, w_out, pool_o, attn_o, *([] if dep is None else [dep]))


def _pool_bwd(u, dpool, w_pool, pool_scale, dep=None):
    s = u.shape[0]
    tm = min(TM_POOL, s)
    hb = tm // HALO
    nt = s // tm
    last_halo = s // HALO - 1

    def body(uc_ref, uh_ref, dc_ref, dn_ref, wp_ref, sc_ref, *rest):
        du_ref, gwp_ref, gsc_ref = rest[-3:]
        i = pl.program_id(0)

        @pl.when(i == 0)
        def _():
            gwp_ref[...] = jnp.zeros_like(gwp_ref)
            gsc_ref[...] = jnp.zeros_like(gsc_ref)

        cur = uc_ref[...]
        halo = jnp.where(i > 0, uh_ref[...], 0.0)
        pooled = _pooled(jnp.concatenate([halo, cur], axis=0), cur, i * tm, tm)
        dcur = dc_ref[...]
        dnext = jnp.where(i < nt - 1, dn_ref[...], 0.0)
        dext = jnp.concatenate([dcur, dnext], axis=0)
        rows = lax.broadcasted_iota(jnp.int32, (tm, tm + HALO), 0)
        cols = lax.broadcasted_iota(jnp.int32, (tm, tm + HALO), 1)
        d = cols - rows
        t_ext = i * tm + lax.broadcasted_iota(jnp.int32, (tm + HALO, GROUP), 0)
        for g, w in enumerate(WINDOWS):
            sl = slice(g * GROUP, (g + 1) * GROUP)
            pb = pooled[g].astype(BF16)
            wp = wp_ref[g]
            mixed = _dot(pb, wp)
            gsc_ref[:, sl] += jnp.sum(dcur[:, sl] * mixed, axis=0, keepdims=True)
            dmixed = (dext[:, sl] * sc_ref[:, sl]).astype(BF16)
            gwp_ref[g] += _dot_tn(pb, dmixed[0:tm])
            dpooled = _dot_nt(dmixed, wp)
            z = dpooled / jnp.minimum(t_ext + 1, w).astype(F32)
            b = jnp.where((d >= 0) & (d < w), 1.0, 0.0).astype(BF16)
            du_ref[:, sl] = (_band_dot(b, z, 2) - dpooled[0:tm]).astype(BF16)

    return pl.pallas_call(
        body, name="pool_bwd", grid=(nt,),
        in_specs=[pl.BlockSpec((tm, 512), lambda i: (i, 0)),
                  pl.BlockSpec((HALO, 512), lambda i: (jnp.maximum(i * hb - 1, 0), 0)),
                  pl.BlockSpec((tm, 512), lambda i: (i, 0)),
                  pl.BlockSpec((HALO, 512), lambda i: (jnp.minimum((i + 1) * hb, last_halo), 0)),
                  pl.BlockSpec((4, GROUP, GROUP), lambda i: (0, 0, 0)),
                  pl.BlockSpec((1, 512), lambda i: (0, 0))] + ([] if dep is None else [ANY]),
        out_specs=[pl.BlockSpec((tm, 512), lambda i: (i, 0)),
                   pl.BlockSpec((4, GROUP, GROUP), lambda i: (0, 0, 0)),
                   pl.BlockSpec((1, 512), lambda i: (0, 0))],
        out_shape=[jax.ShapeDtypeStruct((s, 512), BF16), jax.ShapeDtypeStruct((4, GROUP, GROUP), F32),
                   jax.ShapeDtypeStruct((1, 512), F32)],
        compiler_params=_cp(1))(u, u, dpool, dpool, w_pool, pool_scale, *([] if dep is None else [dep]))


def _attn_bwd(q, k, v, do, bias, sinks, dep=None):
    s = q.shape[0]

    def body(q_ref, do_ref, k_ref, v_ref, b_ref, sk_ref, *rest):
        dq_ref, dk_ref, dv_ref, dss_ref, gsk_ref = rest[-5:]
        n = pl.program_id(0)

        @pl.when(n == 0)
        def _():
            dk_ref[...] = jnp.zeros_like(dk_ref)
            dv_ref[...] = jnp.zeros_like(dv_ref)
            dss_ref[...] = jnp.zeros_like(dss_ref)
            gsk_ref[...] = jnp.zeros_like(gsk_ref)

        kk, vv, lo, pstart, cstart = _kv_bands(k_ref, v_ref, n)
        bidx = jnp.minimum(n, 1)
        lo_q = lax.broadcasted_iota(jnp.int32, (BLK, 128), 1) < 64
        dkk = [jnp.zeros((2 * BLK, 128), F32), jnp.zeros((2 * BLK, 128), F32)]
        dvv = [jnp.zeros((2 * BLK, 128), F32), jnp.zeros((2 * BLK, 128), F32)]
        for p in range(4):
            h = p // 2
            qt = q_ref[:, p * 128:(p + 1) * 128].astype(F32) * SCALE
            dot = do_ref[:, p * 128:(p + 1) * 128]
            dq = jnp.zeros((BLK, 128), F32)
            for e in range(2):
                head = 2 * p + e
                sel_q = lo_q if e == 0 else jnp.logical_not(lo_q)
                sel_kv = lo if e == 0 else jnp.logical_not(lo)
                qm = jnp.where(sel_q, qt, 0.0).astype(BF16)
                prob, ps = _attn_probs(qm, kk[h], b_ref[bidx, head], sk_ref[0, head])
                dom = jnp.where(sel_q, dot, jnp.zeros_like(dot))
                dp = _dot_nt(dom, vv[h])
                delta = jnp.sum(prob * dp, axis=-1, keepdims=True)
                ds = prob * (dp - delta)
                gsk_ref[pl.ds(head, 1), :] += jnp.broadcast_to(-jnp.sum(ps * delta).reshape(1, 1), (1, 128))
                dss_ref[head] += ds
                dsb = ds.astype(BF16)
                km = jnp.where(sel_kv, kk[h], jnp.zeros_like(kk[h]))
                dq = dq + _dot(dsb, km)
                dkk[h] = dkk[h] + _dot_tn(dsb, qm)
                dvv[h] = dvv[h] + _dot_tn(prob.astype(BF16), dom)
            dq_ref[:, p * 128:(p + 1) * 128] = (dq * SCALE).astype(BF16)
        for acc, ref in ((dkk, dk_ref), (dvv, dv_ref)):
            f0 = acc[0] + pltpu.roll(acc[0], 64, axis=1)
            f1 = acc[1] + pltpu.roll(acc[1], 64, axis=1)
            band = jnp.where(lo, f0, f1)
            ref[pl.ds(pstart, BLK), :] += band[0:BLK]
            ref[pl.ds(cstart, BLK), :] += band[BLK:2 * BLK]

    blk = pl.BlockSpec((BLK, 512), lambda i: (i, 0))
    kv = pl.BlockSpec((s, 128), lambda i: (0, 0))
    return pl.pallas_call(
        body, name="attn_bwd", grid=(s // BLK,),
        in_specs=[blk, blk, kv, kv, pl.BlockSpec((2, NQ, BLK, 2 * BLK), lambda i: (0, 0, 0, 0)),
                  pl.BlockSpec(memory_space=pltpu.SMEM)] + ([] if dep is None else [ANY]),
        out_specs=[blk, kv, kv, pl.BlockSpec((NQ, BLK, 2 * BLK), lambda i: (0, 0, 0)),
                   pl.BlockSpec((NQ, 128), lambda i: (0, 0))],
        out_shape=[jax.ShapeDtypeStruct((s, 512), BF16), jax.ShapeDtypeStruct((s, 128), F32),
                   jax.ShapeDtypeStruct((s, 128), F32), jax.ShapeDtypeStruct((NQ, BLK, 2 * BLK), F32),
                   jax.ShapeDtypeStruct((NQ, 128), F32)],
        compiler_params=_cp(1))(q, do, k, v, bias, sinks, *([] if dep is None else [dep]))


def _relbias_grad(dss, bucket):
    def body(ds_ref, b_ref, o_ref):
        bucket_v = b_ref[...]
        lane = lax.broadcasted_iota(jnp.int32, (NQ, 128), 1)
        head_row = lax.broadcasted_iota(jnp.int32, (NQ, 2 * BLK), 0)
        acc = jnp.zeros((NQ, 128), F32)
        for b in range(NBUCKET):
            sel = bucket_v == b
            stack = jnp.zeros((NQ, 2 * BLK), F32)
            for h in range(NQ):
                col_sums = jnp.sum(jnp.where(sel, ds_ref[h], 0.0), axis=0, keepdims=True)
                stack = jnp.where(head_row == h, col_sums, stack)
            acc = acc + jnp.where(lane == b, jnp.sum(stack, axis=1, keepdims=True), 0.0)
        o_ref[...] = acc

    return pl.pallas_call(
        body, name="relbias_grad",
        in_specs=[pl.BlockSpec(memory_space=pltpu.VMEM), pl.BlockSpec(memory_space=pltpu.VMEM)],
        out_specs=pl.BlockSpec(memory_space=pltpu.VMEM),
        out_shape=jax.ShapeDtypeStruct((NQ, 128), F32),
        compiler_params=_cp())(dss, bucket)


def _inproj_bwd(x, g1, du, dq, dk, dv, w_in, dx1, c_arr):
    s = x.shape[0]
    tm = min(TM, s)
    nt = s // tm
    cols = ((0, 512), (512, 1024), (1024, 1152), (1152, 1280))

    def body(c_ref, x_ref, g_ref, du_ref, dq_ref, dk_ref, dv_ref, w_ref, dx1_ref, gx_ref, own_ref, oth_ref, gg_ref,
             gw_ref):
        i = pl.program_id(0)

        @pl.when(i == 0)
        def _():
            gw_ref[...] = jnp.zeros_like(gw_ref)
            gg_ref[...] = jnp.zeros_like(gg_ref)

        gv = g_ref[...]
        r1, n1 = _rms(x_ref[...])
        h = (n1 * gv).astype(BF16)
        parts = (du_ref[...], dq_ref[...], dk_ref[...].astype(BF16), dv_ref[...].astype(BF16))
        dh = None
        for (a, b), dpart in zip(cols, parts):
            t = _dot(dpart, w_ref[a:b, :])
            dh = t if dh is None else dh + t
            gw_ref[a:b, :] += _dot_tn(dpart, h)
        dx, dg = _rms_bwd(r1, n1, gv, dh)
        gg_ref[...] += dg
        gx_ref[...] = dx1_ref[...] + dx

        @pl.when(i == nt - 1)
        def _():
            for k in range(NSH):
                _emit_halves(gw_ref, k * WIN_S, WIN_S, c_ref[0], own_ref.at[k], oth_ref.at[k])

    row = lambda w: pl.BlockSpec((tm, w), lambda i: (i, 0))
    vec = pl.BlockSpec((1, D), lambda i: (0, 0))
    full = pl.BlockSpec((IN_W, D), lambda i: (0, 0))
    half = pl.BlockSpec((NSH, WIN_S // 2, D), lambda i: (0, 0, 0))
    return pl.pallas_call(
        body, name="inproj_bwd", grid=(nt,),
        in_specs=[SMEM_SPEC, row(D), vec, row(512), row(512), row(128), row(128), full, row(D)],
        out_specs=[row(D), half, half, vec],
        out_shape=[jax.ShapeDtypeStruct((s, D), F32)] + _half_shapes(WIN_S) + [jax.ShapeDtypeStruct((1, D), F32)],
        scratch_shapes=[pltpu.VMEM((IN_W, D), F32)],
        compiler_params=_cp(1))(c_arr, x, g1, du, dq, dk, dv, w_in, dx1)


def _local_step(x, target, small, w_in, w_out, ffn_weights, c_arr, on_ffn_grads=None, on_wout_grads=None):
    g1, g2, g3, g4, w_pool, pool_scale, rel_bias, sinks = small
    bucket = jnp.asarray(_bucket_table())
    wp_bf = w_pool.astype(BF16)
    u, q, k, v = _inproj_fwd(x, g1, w_in)
    pool_o = _pool_fwd(u, wp_bf, pool_scale)
    bias = _bias_table(bucket, rel_bias)
    attn_o = _attn_fwd(q, k, v, bias, sinks)
    wg, wu, wd = ffn_weights(pool_o, attn_o) if callable(ffn_weights) else ffn_weights
    mix, x1, h2 = _outproj_fwd(pool_o, attn_o, w_out, x, g2, g3)
    gate, up, df, dy, loss, gg4 = _ffn_fwd(h2, wg, wu, wd, x1, target, g4)
    dgate, dup, dx1, dmix, gg3, gg2 = _ffn_bwd_act(df, gate, up, wg, wu, wd, dy, x1, mix, g3, g2)
    ffn_g = _ffn_bwd_w(h2, gate, up, dgate, dup, df, c_arr)
    dep = None if on_ffn_grads is None else on_ffn_grads(ffn_g)
    dpool, dattn, wout_own, wout_oth = _outproj_bwd(dmix, w_out, pool_o, attn_o, c_arr, dep)
    dep = None if on_wout_grads is None else on_wout_grads(wout_own, wout_oth, dpool)
    du, gwp, gsc = _pool_bwd(u, dpool, wp_bf, pool_scale, dep)
    dq, dk, dv, dss, gsk = _attn_bwd(q, k, v, dattn, bias, sinks, dep)
    grb = _relbias_grad(dss, bucket)
    gx, win_own, win_oth, gg1 = _inproj_bwd(x, g1, du, dq, dk, dv, w_in, dx1, c_arr)
    small_grads = (gg1, gg2, gg3, gg4, gwp, gsc, grb[:, :NBUCKET].T, gsk[:, 0].reshape(1, NQ))
    return loss, gx, small_grads, ((win_own, win_oth), (wout_own, wout_oth), ffn_g)


def _place():
    x, y, c = lax.axis_index("x"), lax.axis_index("y"), lax.axis_index("c")
    chips = ((1 - x, y), (x, 1 - y), (1 - x, 1 - y))
    return x, y, c, chips


def _remote(src, dst, ssem, rsem, dev):
    return pltpu.make_async_remote_copy(src_ref=src, dst_ref=dst, send_sem=ssem, recv_sem=rsem,
                                        device_id=dev, device_id_type=MESH_T)


def _own_slot(shard):
    me = 2 * lax.axis_index("x") + lax.axis_index("y")
    return lax.dynamic_update_slice(lax.empty((NSH,) + shard.shape, shard.dtype), shard[None], (me, 0, 0))


def _all_gather_weights(shards, lands):
    nt = len(shards)

    def body(*refs):
        srcs, dsts = refs[:nt], refs[2 * nt:3 * nt]
        isend, irecv, dsend, drecv = refs[3 * nt:]
        x, y, c, chips = _place()
        me = 2 * x + y
        sib = (x, y, 1 - c)
        sends = []
        for t in range(nt):
            half = srcs[t].shape[0] // 2
            mine = pl.ds(pl.multiple_of(c * half, 16), half)
            for j, (px, py) in enumerate(chips):
                cp = _remote(srcs[t].at[mine], dsts[t].at[me, mine], isend.at[t, j], irecv.at[t, j], (px, py, c))
                cp.start()
                sends.append(cp)
        for t in range(nt):
            half = srcs[t].shape[0] // 2
            mine = pl.ds(pl.multiple_of(c * half, 16), half)
            for j, (px, py) in enumerate(chips):
                blk = dsts[t].at[2 * px + py, mine]
                _remote(blk, blk, isend.at[t, j], irecv.at[t, j], (px, py, c)).wait_recv()
                cp = _remote(blk, blk, dsend.at[t, j], drecv.at[t, j], sib)
                cp.start()
                sends.append(cp)
        for t in range(nt):
            half = srcs[t].shape[0] // 2
            other = pl.ds(pl.multiple_of((1 - c) * half, 16), half)
            for j, (px, py) in enumerate(chips):
                blk = dsts[t].at[2 * px + py, other]
                _remote(blk, blk, dsend.at[t, j], drecv.at[t, j], sib).wait_recv()
        for cp in sends:
            cp.wait_send()

    return pl.pallas_call(
        body, name="allgather_weights",
        in_specs=[ANY] * (2 * nt), out_specs=[ANY] * nt,
        out_shape=[jax.ShapeDtypeStruct(a.shape, a.dtype) for a in lands],
        input_output_aliases={nt + t: t for t in range(nt)},
        scratch_shapes=[pltpu.SemaphoreType.DMA((nt, 3)), pltpu.SemaphoreType.DMA((nt, 3)),
                        pltpu.SemaphoreType.DMA((nt, 3)), pltpu.SemaphoreType.DMA((nt, 3))],
        compiler_params=_cp())(*shards, *lands)


def _to_sibling(arrs, name):
    nt = len(arrs)

    def body(*refs):
        srcs, dsts = refs[:nt], refs[nt:2 * nt]
        ssem, rsem = refs[2 * nt:]
        x, y, c, _ = _place()
        cps = [_remote(srcs[t], dsts[t], ssem.at[t], rsem.at[t], (x, y, 1 - c)) for t in range(nt)]
        for cp in cps:
            cp.start()
        for cp in cps:
            cp.wait()

    return pl.pallas_call(
        body, name=name, in_specs=[ANY] * nt, out_specs=[ANY] * nt,
        out_shape=[jax.ShapeDtypeStruct(a.shape, a.dtype) for a in arrs],
        scratch_shapes=[pltpu.SemaphoreType.DMA((nt,)), pltpu.SemaphoreType.DMA((nt,))],
        compiler_params=_cp())(*arrs)


def _scatter_to_chips(arrs):
    nt = len(arrs)

    def body(*refs):
        srcs, dsts = refs[:nt], refs[nt:2 * nt]
        ssem, rsem = refs[2 * nt:]
        x, y, c, chips = _place()
        cps = []
        for t in range(nt):
            for j, (px, py) in enumerate(chips):
                cps.append(_remote(srcs[t].at[2 * px + py], dsts[t].at[j], ssem.at[t, j], rsem.at[t, j], (px, py, c)))
        for cp in cps:
            cp.start()
        for cp in cps:
            cp.wait()

    return pl.pallas_call(
        body, name="scatter_to_chips", in_specs=[ANY] * nt, out_specs=[ANY] * nt,
        out_shape=[jax.ShapeDtypeStruct((3,) + a.shape[1:], a.dtype) for a in arrs],
        scratch_shapes=[pltpu.SemaphoreType.DMA((nt, 3)), pltpu.SemaphoreType.DMA((nt, 3))],
        compiler_params=_cp())(*arrs)


HBM_SPEC = pl.BlockSpec(memory_space=pltpu.HBM)
SEM_SPEC = pl.BlockSpec(memory_space=pltpu.SEMAPHORE)
DATAFLOW = pltpu.SideEffectType.DATAFLOW_SIDE_EFFECTING


def _gather_copies(srcs, lands, ssem, rsem):
    x, y, c, chips = _place()
    me = 2 * x + y
    out = []
    for t in range(len(srcs)):
        half = srcs[t].shape[0] // 2
        mine = pl.ds(pl.multiple_of(c * half, 16), half)
        for j, (px, py) in enumerate(chips):
            k = 3 * t + j
            send = _remote(srcs[t].at[mine], lands[t].at[me, mine], ssem.at[k], rsem.at[k], (px, py, c))
            blk = lands[t].at[2 * px + py, mine]
            out.append((send, _remote(blk, blk, ssem.at[k], rsem.at[k], (px, py, c))))
    return out


def _scatter_copies(srcs, lands, ssem, rsem):
    x, y, c, chips = _place()
    out = []
    for t in range(len(srcs)):
        for j, (px, py) in enumerate(chips):
            k = 3 * t + j
            send = _remote(srcs[t].at[2 * px + py], lands[t].at[j], ssem.at[k], rsem.at[k], (px, py, c))
            out.append((send, _remote(lands[t].at[j], lands[t].at[j], ssem.at[k], rsem.at[k], (px, py, c))))
    return out


def _sibling_copies(srcs, lands, ssem, rsem):
    x, y, c, _ = _place()
    sib = (x, y, 1 - c)
    return [(_remote(srcs[t], lands[t], ssem.at[t], rsem.at[t], sib),
             _remote(lands[t], lands[t], ssem.at[t], rsem.at[t], sib)) for t in range(len(srcs))]


def _peer_copies(srcs, lands, ssem, rsem):
    x, y, c, _ = _place()
    me = 4 * x + 2 * y + c
    out = []
    for kk in range(1, 8):
        px, py, pc = x ^ (kk >> 2), y ^ ((kk >> 1) & 1), c ^ (kk & 1)
        send = _remote(srcs[0], lands[0].at[me], ssem.at[kk - 1], rsem.at[kk - 1], (px, py, pc))
        slot = lands[0].at[4 * px + 2 * py + pc]
        out.append((send, _remote(slot, slot, ssem.at[kk - 1], rsem.at[kk - 1], (px, py, pc))))
    return out


def _split_start(plan, ncopy, srcs, lands, after, name):
    ns, nbuf = len(srcs), len(srcs) + len(lands)

    def body(*refs):
        ssem, rsem, token = refs[nbuf + 1], refs[nbuf + 2], refs[-1]
        for send, _ in plan(refs[:ns], refs[ns:nbuf], ssem, rsem):
            send.start()
        token[...] = jnp.zeros_like(token)

    bufs = [pltpu.with_memory_space_constraint(a, pltpu.HBM) for a in list(srcs) + list(lands)]
    outs = pl.pallas_call(
        body, name=name, in_specs=[HBM_SPEC] * nbuf + [ANY],
        out_specs=[SEM_SPEC, SEM_SPEC] + [HBM_SPEC] * nbuf + [pl.BlockSpec(memory_space=pltpu.VMEM)],
        out_shape=[pltpu.SemaphoreType.DMA((ncopy,)), pltpu.SemaphoreType.DMA((ncopy,))]
        + [pltpu.HBM(a.shape, a.dtype) for a in bufs] + [jax.ShapeDtypeStruct((8, 128), F32)],
        input_output_aliases={i: 2 + i for i in range(nbuf)},
        compiler_params=pltpu.CompilerParams(has_side_effects=DATAFLOW))(*bufs, after)
    return outs[0], outs[1], outs[2:2 + ns], outs[2 + ns:2 + nbuf], outs[-1]


def _split_wait(plan, ssem, rsem, srcs, lands, after, name):
    ns, nbuf = len(srcs), len(srcs) + len(lands)

    def body(*refs):
        s_ref, r_ref = refs[nbuf], refs[nbuf + 1]
        for send, recv in plan(refs[:ns], refs[ns:nbuf], s_ref, r_ref):
            send.wait_send()
            recv.wait_recv()

    outs = pl.pallas_call(
        body, name=name, in_specs=[HBM_SPEC] * nbuf + [SEM_SPEC, SEM_SPEC] + [ANY] * len(after),
        out_specs=[HBM_SPEC] * nbuf,
        out_shape=[pltpu.HBM(a.shape, a.dtype) for a in list(srcs) + list(lands)],
        input_output_aliases={i: i for i in range(nbuf)},
        compiler_params=pltpu.CompilerParams(has_side_effects=DATAFLOW))(*srcs, *lands, ssem, rsem, *after)
    return outs


def _gather_finish(lands):
    nt = len(lands)

    def body(*refs):
        outs = refs[nt:2 * nt]
        dsend, drecv = refs[2 * nt:]
        x, y, c, chips = _place()
        sib = (x, y, 1 - c)
        sends = []
        for t in range(nt):
            half = outs[t].shape[1] // 2
            mine = pl.ds(pl.multiple_of(c * half, 16), half)
            for j, (px, py) in enumerate(chips):
                blk = outs[t].at[2 * px + py, mine]
                cp = _remote(blk, blk, dsend.at[t, j], drecv.at[t, j], sib)
                cp.start()
                sends.append(cp)
        for t in range(nt):
            half = outs[t].shape[1] // 2
            other = pl.ds(pl.multiple_of((1 - c) * half, 16), half)
            for j, (px, py) in enumerate(chips):
                blk = outs[t].at[2 * px + py, other]
                _remote(blk, blk, dsend.at[t, j], drecv.at[t, j], sib).wait_recv()
        for cp in sends:
            cp.wait_send()

    return pl.pallas_call(
        body, name="gather_finish", in_specs=[ANY] * nt, out_specs=[ANY] * nt,
        out_shape=[jax.ShapeDtypeStruct(a.shape, a.dtype) for a in lands],
        input_output_aliases={t: t for t in range(nt)},
        scratch_shapes=[pltpu.SemaphoreType.DMA((nt, 3)), pltpu.SemaphoreType.DMA((nt, 3))],
        compiler_params=_cp())(*lands)


def _chip_partial(owns, recv, chip_arr, tag):
    nt = len(owns)

    def body(chip_ref, *refs):
        k = pl.program_id(0)
        for t in range(nt):
            p = refs[t][...] + refs[nt + t][...].astype(F32)
            refs[2 * nt + t][...] = p.astype(BF16)

            @pl.when(k == chip_ref[0])
            def _():
                refs[3 * nt + t][...] = p

    blk_specs = [pl.BlockSpec((None,) + a.shape[1:], lambda k, chip_ref: (k, 0, 0)) for a in owns]
    own_specs = [pl.BlockSpec(a.shape[1:], lambda k, chip_ref: (0, 0)) for a in owns]
    return pl.pallas_call(
        body, name="rs_chip_partial_" + tag,
        grid_spec=pltpu.PrefetchScalarGridSpec(num_scalar_prefetch=1, grid=(NSH,), in_specs=blk_specs * 2,
                                               out_specs=blk_specs + own_specs),
        out_shape=[jax.ShapeDtypeStruct(a.shape, BF16) for a in owns]
        + [jax.ShapeDtypeStruct(a.shape[1:], F32) for a in owns],
        compiler_params=_cp(1))(chip_arr, *owns, *recv)


def _sum_chips(own, recv, tag, after=()):
    nt = len(own)

    def body(*refs):
        outs = refs[2 * nt + len(after):]
        for t in range(nt):
            r = refs[nt + t]
            outs[t][...] = ((refs[t][...] + r[0].astype(F32)) + r[1].astype(F32)) + r[2].astype(F32)

    vm = pl.BlockSpec(memory_space=pltpu.VMEM)
    return pl.pallas_call(
        body, name="rs_sum_chips_" + tag, in_specs=[vm] * (2 * nt) + [ANY] * len(after), out_specs=[vm] * nt,
        out_shape=[jax.ShapeDtypeStruct(a.shape, F32) for a in own],
        compiler_params=_cp())(*own, *recv, *after)


def _adamw_math(w, g, m, v):
    m = ADAM_B1 * m + (1.0 - ADAM_B1) * g
    v = ADAM_B2 * v + (1.0 - ADAM_B2) * (g * g)
    m_hat = m / (1.0 - ADAM_B1 ** ADAM_STEP)
    v_hat = v / (1.0 - ADAM_B2 ** ADAM_STEP)
    delta = -ADAM_LR * (m_hat / (jnp.sqrt(v_hat) + ADAM_EPS) + ADAM_WD * w)
    return delta, m, v


ADAM_SPLIT = 4


def _adamw_shards(own, sib, ws, ms, vs, c_arr, tag):
    nt = len(own)

    def body(c_ref, *refs):
        hh = pl.program_id(0)
        mine = hh == c_ref[0]
        for t in range(nt):
            g = jnp.where(mine, refs[t][...], refs[nt + t][...])
            delta, m, v = _adamw_math(refs[2 * nt + t][...], g, refs[3 * nt + t][...], refs[4 * nt + t][...])
            refs[5 * nt + t][...] = g
            refs[6 * nt + t][...] = delta
            refs[7 * nt + t][...] = m
            refs[8 * nt + t][...] = v

    def tile(a):
        return (a.shape[0] // ADAM_SPLIT, a.shape[1])

    half_specs = [pl.BlockSpec(tile(a), lambda hh, q, c_ref: (q, 0)) for a in own]
    full_specs = [pl.BlockSpec(tile(a), lambda hh, q, c_ref: (hh * ADAM_SPLIT + q, 0)) for a in own]
    return pl.pallas_call(
        body, name="adamw_shards_" + tag,
        grid_spec=pltpu.PrefetchScalarGridSpec(num_scalar_prefetch=1, grid=(2, ADAM_SPLIT),
                                               in_specs=half_specs * 2 + full_specs * 3, out_specs=full_specs * 4),
        out_shape=[jax.ShapeDtypeStruct(w.shape, F32) for w in ws] * 4,
        compiler_params=_cp(2))(c_arr, *own, *sib, *ws, *ms, *vs)


def _small_sum_adamw(gathered, wp, mp, vp):
    def body(g_ref, w_ref, m_ref, v_ref, go_ref, d_ref, mo_ref, vo_ref):
        g = g_ref[0]
        for d in range(1, 8):
            g = g + g_ref[d]
        delta, m, v = _adamw_math(w_ref[...], g, m_ref[...], v_ref[...])
        go_ref[...] = g
        d_ref[...] = delta
        mo_ref[...] = m
        vo_ref[...] = v

    vm = pl.BlockSpec(memory_space=pltpu.VMEM)
    return pl.pallas_call(
        body, name="small_sum_adamw", in_specs=[vm] * 4, out_specs=[vm] * 4,
        out_shape=[jax.ShapeDtypeStruct(wp.shape, F32)] * 4,
        compiler_params=_cp())(gathered, wp, mp, vp)


SMALL_PARTS = (("g1", (1, D)), ("g2", (1, D)), ("g3", (1, D)), ("g4", (1, D)), ("w_pool", (1, 4, GROUP, GROUP)),
               ("pool_scale", (1, POOL_W)), ("rel_bias", (NBUCKET, NQ)), ("sinks", (1, NQ)), ("loss", (1, 1)))


def _pack_small(parts):
    rows = []
    for a in parts:
        flat = a.reshape(-1)
        n = flat.shape[0]
        padded = -(-n // 1024) * 1024
        rows.append(jnp.pad(flat, (0, padded - n)).reshape(-1, 128))
    return jnp.concatenate(rows, axis=0)


def _unpack_small(packed):
    out, r = [], 0
    for _, shape in SMALL_PARTS:
        n = int(np.prod(shape))
        nr = -(-n // 1024) * 8
        out.append(packed[r:r + nr].reshape(-1)[:n].reshape(shape))
        r += nr
    return out


def kernel(x, g_pre_mix, w_in, w_pool, pool_scale, rel_bias, sinks, w_out, g_post_mix, g_pre_ffn, w_gate, w_up, w_down, g_post_ffn, loss_target, m_g_pre_mix, m_w_in, m_w_pool, m_pool_scale, m_rel_bias, m_sinks, m_w_out, m_g_post_mix, m_g_pre_ffn, m_w_gate, m_w_up, m_w_down, m_g_post_ffn, v_g_pre_mix, v_w_in, v_w_pool, v_pool_scale, v_rel_bias, v_sinks, v_w_out, v_g_post_mix, v_g_pre_ffn, v_w_gate, v_w_up, v_w_down, v_g_post_ffn):
    c = lax.axis_index("c")
    chip = 2 * lax.axis_index("x") + lax.axis_index("y")

    def shards(a_in, a_out, a_gate, a_up, a_down):
        return (a_in[0].T, a_out[0], a_gate[0].T, a_up[0].T, a_down[0])

    c_arr = c.reshape(1).astype(jnp.int32)
    chip_arr = chip.reshape(1).astype(jnp.int32)

    big_w = shards(w_in, w_out, w_gate, w_up, w_down)
    big_bf = [w.astype(BF16) for w in big_w]
    win_all, wout_all = _all_gather_weights(big_bf[:2], [_own_slot(a) for a in big_bf[:2]])
    g_ssem, g_rsem, g_srcs, g_lands, g_token = _split_start(
        _gather_copies, 9, big_bf[2:], [_own_slot(a) for a in big_bf[2:]], win_all, "gather_ffn_start")

    def ffn_weights(*after):
        outs = _split_wait(_gather_copies, g_ssem, g_rsem, g_srcs, g_lands, after, "gather_ffn_wait")
        return _gather_finish(outs[3:])

    rs = {}

    def on_ffn_grads(ffn_g):
        oths = ffn_g[1::2]
        rs["ffn_own"] = ffn_g[0::2]
        rs["x"] = _split_start(_sibling_copies, 3, oths, [lax.empty(a.shape, BF16) for a in oths], ffn_g[0],
                               "rs_exchange_ffn_start")
        return rs["x"][4]

    def on_wout_grads(own, oth, after):
        ssem, rsem, srcs, lands, _ = rs["x"]
        recv_ffn = _split_wait(_sibling_copies, ssem, rsem, srcs, lands, [after], "rs_exchange_ffn_wait")[3:]
        recv_wout = _to_sibling([oth], "rs_exchange_wout")
        outs = _chip_partial([own] + list(rs["ffn_own"]), list(recv_wout) + list(recv_ffn), chip_arr, "early")
        rs["early_own"] = outs[4:]
        rs["s"] = _split_start(_scatter_copies, 12, outs[:4],
                               [lax.empty((3,) + a.shape[1:], BF16) for a in outs[:4]], outs[4], "scatter_early_start")
        return rs["s"][4]

    small = (g_pre_mix + g_token[:1, :1], g_post_mix, g_pre_ffn, g_post_ffn, w_pool[0], pool_scale, rel_bias, sinks)
    loss, gx, small_grads, ((win_own, win_oth), _, _) = _local_step(
        x[0], loss_target[0], small, win_all.reshape(IN_W, D), wout_all.reshape(D, D), ffn_weights, c_arr,
        on_ffn_grads, on_wout_grads)
    ssem, rsem, srcs, lands, _ = rs["s"]
    recv_early = _split_wait(_scatter_copies, ssem, rsem, srcs, lands, [gx], "scatter_early_wait")[4:]

    unused = jnp.zeros((1, 1), F32)
    gp = _pack_small(small_grads + (loss,))
    wp = _pack_small((g_pre_mix, g_post_mix, g_pre_ffn, g_post_ffn, w_pool, pool_scale, rel_bias, sinks, unused))
    mp = _pack_small((m_g_pre_mix, m_g_post_mix, m_g_pre_ffn, m_g_post_ffn, m_w_pool, m_pool_scale, m_rel_bias,
                      m_sinks, unused))
    vp = _pack_small((v_g_pre_mix, v_g_post_mix, v_g_pre_ffn, v_g_post_ffn, v_w_pool, v_pool_scale, v_rel_bias,
                      v_sinks, unused))

    moments = (shards(m_w_in, m_w_out, m_w_gate, m_w_up, m_w_down),
               shards(v_w_in, v_w_out, v_w_gate, v_w_up, v_w_down))
    recv_a = _to_sibling([win_oth], "rs_exchange_win")
    outs = _chip_partial([win_own], recv_a, chip_arr, "win")
    w_ssem, w_rsem, w_srcs, w_lands, w_token = _split_start(
        _scatter_copies, 3, outs[:1], [lax.empty((3,) + outs[0].shape[1:], BF16)], gx, "scatter_win_start")
    slots = lax.dynamic_update_slice(lax.empty((8,) + gp.shape, F32), gp[None], (2 * chip + c, 0, 0))
    p_ssem, p_rsem, p_srcs, p_lands, p_token = _split_start(_peer_copies, 7, [gp], [slots], w_token,
                                                            "small_allgather_start")
    early_final = _sum_chips(list(rs["early_own"]), list(recv_early), "early", [p_token])
    early_sib = _to_sibling(early_final, "rs_share_early")
    adam_e = _adamw_shards(early_final, early_sib, big_w[1:], moments[0][1:], moments[1][1:], c_arr, "early")
    recv_win = _split_wait(_scatter_copies, w_ssem, w_rsem, w_srcs, w_lands, [adam_e[0]], "scatter_win_wait")[1:]
    win_final = _sum_chips([outs[1]], recv_win, "win")
    win_sib = _to_sibling(win_final, "rs_share_win")
    adam_w = _adamw_shards(win_final, win_sib, big_w[:1], moments[0][:1], moments[1][:1], c_arr, "win")
    big_g, big_d, big_m, big_v = [[adam_w[i]] + list(adam_e[4 * i:4 * i + 4]) for i in range(4)]

    gathered = _split_wait(_peer_copies, p_ssem, p_rsem, p_srcs, p_lands, [adam_w[0]], "small_allgather_wait")[1]
    small_out = [_unpack_small(p) for p in _small_sum_adamw(gathered, wp, mp, vp)]
    total_loss = small_out[0][8][0, 0]

    def ordered(small4, big4):
        sg1, sg2, sg3, sg4, swp, ssc, srb, ssk = small4[:8]
        bwin, bwout, bwg, bwu, bwd = big4[0].T[None], big4[1][None], big4[2].T[None], big4[3].T[None], big4[4][None]
        return [sg1, bwin, swp, ssc, srb, ssk, bwout, sg2, sg3, bwg, bwu, bwd, sg4]

    res = [total_loss, gx[None]]
    for sm, bg in zip(small_out, (big_g, big_d, big_m, big_v)):
        res += ordered(sm, bg)
    return tuple(res)
```

```python
import functools

import numpy as np
import jax
import jax.numpy as jnp
from jax import lax
from jax.experimental import pallas as pl
from jax.experimental.pallas import tpu as pltpu

F32 = jnp.float32
BF16 = jnp.bfloat16

D = 1024
POOL_W = 512
GROUP = 128
WINDOWS = (2, 4, 8, 16)
HALO = 128
NQ = 8
BLK = 128
NBUCKET = 32
IN_W = 1280
DFF = 2816
NSH = 4
FS = DFF // NSH
WIN_S = IN_W // NSH
WOUT_S = D // NSH
EPS = 1e-6
NEG = -1e30
SCALE = 0.125

ADAM_LR = 0.001
ADAM_B1 = 0.9
ADAM_B2 = 0.999
ADAM_EPS = 1e-08
ADAM_WD = 0.01
ADAM_STEP = 10

TM = 512
TM_POOL = 256
TM_FFN = 512
VMEM_LIMIT = 56 * 1024 * 1024

MESH_T = pl.DeviceIdType.MESH
ANY = pl.BlockSpec(memory_space=pl.ANY)


def _cp(n_grid=0, **kw):
    sem = ("arbitrary",) * n_grid if n_grid else None
    return pltpu.CompilerParams(dimension_semantics=sem, vmem_limit_bytes=VMEM_LIMIT, **kw)


def _dot(a, b):
    return jnp.dot(a, b, preferred_element_type=F32)


def _dot_nt(a, b):
    return lax.dot_general(a, b, (((1,), (1,)), ((), ())), preferred_element_type=F32)


def _dot_tn(a, b):
    return lax.dot_general(a, b, (((0,), (0,)), ((), ())), preferred_element_type=F32)


def _rms(x):
    r = lax.rsqrt(jnp.mean(x * x, axis=-1, keepdims=True) + EPS)
    return r, x * r


def _rms_bwd(r, n, g, dout):
    dn = dout * g
    dx = r * (dn - n * jnp.mean(dn * n, axis=-1, keepdims=True))
    dg = jnp.sum(dout * n, axis=0, keepdims=True)
    return dx, dg


def _split(x, n):
    parts = []
    r = x
    for _ in range(n):
        p = r.astype(BF16)
        parts.append(p)
        r = r - p.astype(F32)
    return parts


def _band_dot(a, x, n):
    acc = None
    for p in _split(x, n):
        t = _dot(a, p)
        acc = t if acc is None else acc + t
    return acc


def _bucket_table():
    n = (np.arange(BLK)[:, None] - np.arange(BLK)[None, :]) % BLK
    nf = np.maximum(n, 1).astype(np.float32)
    large = 16 + (np.log(nf / np.float32(16)) / np.float32(np.log(128 / 16)) * np.float32(16)).astype(np.int32)
    large = np.minimum(large, NBUCKET - 1)
    return np.where(n < 16, n, large).astype(np.int32)


def _inproj_fwd(x, g1, w_in):
    s = x.shape[0]
    tm = min(TM, s)

    def body(x_ref, g_ref, w_ref, u_ref, q_ref, k_ref, v_ref):
        _, n = _rms(x_ref[...])
        h = (n * g_ref[...]).astype(BF16)
        proj = _dot_nt(h, w_ref[...])
        u_ref[...] = proj[:, :512]
        q_ref[...] = proj[:, 512:1024].astype(BF16)
        k_ref[...] = proj[:, 1024:1152].astype(BF16)
        v_ref[...] = proj[:, 1152:1280].astype(BF16)

    row = lambda w: pl.BlockSpec((tm, w), lambda i: (i, 0))
    return pl.pallas_call(
        body, name="inproj_fwd", grid=(s // tm,),
        in_specs=[row(D), pl.BlockSpec((1, D), lambda i: (0, 0)), pl.BlockSpec((IN_W, D), lambda i: (0, 0))],
        out_specs=[row(512), row(512), row(128), row(128)],
        out_shape=[jax.ShapeDtypeStruct((s, 512), F32), jax.ShapeDtypeStruct((s, 512), BF16),
                   jax.ShapeDtypeStruct((s, 128), BF16), jax.ShapeDtypeStruct((s, 128), BF16)],
        compiler_params=_cp(1))(x, g1, w_in)


def _pooled(ext, cur, t0, tm):
    rows = lax.broadcasted_iota(jnp.int32, (tm, tm + HALO), 0)
    cols = lax.broadcasted_iota(jnp.int32, (tm, tm + HALO), 1)
    d = rows + HALO - cols
    t = t0 + lax.broadcasted_iota(jnp.int32, (tm, GROUP), 0)
    out = []
    for g, w in enumerate(WINDOWS):
        a = jnp.where((d >= 0) & (d < w), 1.0, 0.0).astype(BF16)
        ws = _band_dot(a, ext[:, g * GROUP:(g + 1) * GROUP], 3)
        cnt = jnp.minimum(t + 1, w).astype(F32)
        out.append(ws / cnt - cur[:, g * GROUP:(g + 1) * GROUP])
    return out


def _pool_fwd(u, w_pool, pool_scale):
    s = u.shape[0]
    tm = min(TM_POOL, s)
    hb = tm // HALO

    def body(uc_ref, uh_ref, wp_ref, sc_ref, o_ref):
        i = pl.program_id(0)
        cur = uc_ref[...]
        halo = jnp.where(i > 0, uh_ref[...], 0.0)
        ext = jnp.concatenate([halo, cur], axis=0)
        pooled = _pooled(ext, cur, i * tm, tm)
        for g in range(4):
            sl = slice(g * GROUP, (g + 1) * GROUP)
            mixed = _dot(pooled[g].astype(BF16), wp_ref[g])
            o_ref[:, sl] = (mixed * sc_ref[:, sl]).astype(BF16)

    return pl.pallas_call(
        body, name="pool_fwd", grid=(s // tm,),
        in_specs=[pl.BlockSpec((tm, 512), lambda i: (i, 0)),
                  pl.BlockSpec((HALO, 512), lambda i: (jnp.maximum(i * hb - 1, 0), 0)),
                  pl.BlockSpec((4, GROUP, GROUP), lambda i: (0, 0, 0)),
                  pl.BlockSpec((1, 512), lambda i: (0, 0))],
        out_specs=pl.BlockSpec((tm, 512), lambda i: (i, 0)),
        out_shape=jax.ShapeDtypeStruct((s, 512), BF16),
        compiler_params=_cp(1))(u, u, w_pool, pool_scale)


def _bias_table(bucket, rel_bias):
    def body(b_ref, rb_ref, o_ref):
        bucket_v = b_ref[...]
        rows = lax.broadcasted_iota(jnp.int32, (BLK, BLK), 0)
        cols = lax.broadcasted_iota(jnp.int32, (BLK, BLK), 1)
        for h in range(NQ):
            acc = jnp.zeros((BLK, BLK), F32)
            for b in range(NBUCKET):
                acc = jnp.where(bucket_v == b, rb_ref[b, h], acc)
            o_ref[1, h] = acc
            o_ref[0, h] = jnp.where(cols <= rows, acc, NEG)

    return pl.pallas_call(
        body, name="bias_table",
        in_specs=[pl.BlockSpec(memory_space=pltpu.VMEM), pl.BlockSpec(memory_space=pltpu.SMEM)],
        out_specs=pl.BlockSpec(memory_space=pltpu.VMEM),
        out_shape=jax.ShapeDtypeStruct((2, NQ, BLK, BLK), F32),
        compiler_params=_cp())(bucket, rel_bias)


def _kv_tiles(k_ref, v_ref, n):
    starts = (pl.multiple_of(jnp.maximum(n - 1, 0) * BLK, BLK), pl.multiple_of(n * BLK, BLK))
    lo = lax.broadcasted_iota(jnp.int32, (BLK, 128), 1) < 64
    out = []
    for ref in (k_ref, v_ref):
        tiles = []
        for st in starts:
            t = ref[pl.ds(st, BLK), :].astype(F32)
            rot = pltpu.roll(t, 64, axis=1)
            tiles.append((jnp.where(lo, t, rot).astype(BF16), jnp.where(lo, rot, t).astype(BF16)))
        out.append(tiles)
    return out[0], out[1], lo, starts


def _half_masked(tiles, lo):
    hi = jnp.logical_not(lo)
    return [[[jnp.where(sel, th, jnp.zeros_like(th)) for sel in (lo, hi)] for th in tt] for tt in tiles]


def _attn_probs(qm, kk_prev, kk_own, causal, bias, sink):
    s = jnp.where(causal, _dot_nt(qm, kk_own), _dot_nt(qm, kk_prev)) + bias
    m = jnp.maximum(jnp.max(s, axis=-1, keepdims=True), sink)
    p = jnp.exp(s - m)
    es = jnp.exp(sink - m)
    inv = 1.0 / (jnp.sum(p, axis=-1, keepdims=True) + es)
    return p * inv, es * inv


def _attn_fwd(q, k, v, bias, sinks):
    s = q.shape[0]

    def body(q_ref, k_ref, v_ref, b_ref, sk_ref, o_ref):
        n = pl.program_id(0)
        kk, vv, lo, _ = _kv_tiles(k_ref, v_ref, n)
        vm = _half_masked(vv, lo)
        bidx = jnp.minimum(n, 1)
        causal = lax.broadcasted_iota(jnp.int32, (BLK, BLK), 1) <= lax.broadcasted_iota(jnp.int32, (BLK, BLK), 0)
        for p in range(4):
            h = p // 2
            qt = q_ref[:, p * 128:(p + 1) * 128].astype(F32) * SCALE
            acc = jnp.zeros((BLK, 128), F32)
            for e in range(2):
                head = 2 * p + e
                qm = jnp.where(lo if e == 0 else jnp.logical_not(lo), qt, 0.0).astype(BF16)
                prob, _ = _attn_probs(qm, kk[0][h], kk[1][h], causal, b_ref[bidx, head], sk_ref[0, head])
                p_own = jnp.where(causal, prob, 0.0).astype(BF16)
                p_prev = jnp.where(causal, 0.0, prob).astype(BF16)
                acc = acc + _dot(p_own, vm[1][h][e]) + _dot(p_prev, vm[0][h][e])
            o_ref[:, p * 128:(p + 1) * 128] = acc.astype(BF16)

    return pl.pallas_call(
        body, name="attn_fwd", grid=(s // BLK,),
        in_specs=[pl.BlockSpec((BLK, 512), lambda i: (i, 0)),
                  pl.BlockSpec((s, 128), lambda i: (0, 0)),
                  pl.BlockSpec((s, 128), lambda i: (0, 0)),
                  pl.BlockSpec((2, NQ, BLK, BLK), lambda i: (0, 0, 0, 0)),
                  pl.BlockSpec(memory_space=pltpu.SMEM)],
        out_specs=pl.BlockSpec((BLK, 512), lambda i: (i, 0)),
        out_shape=jax.ShapeDtypeStruct((s, 512), BF16),
        compiler_params=_cp(1))(q, k, v, bias, sinks)


def _outproj_fwd(pool_o, attn_o, w_out, x, g2, g3):
    s = x.shape[0]
    tm = min(TM, s)

    def body(p_ref, a_ref, w_ref, x_ref, g2_ref, g3_ref, mix_ref, x1_ref, h2_ref):
        mix = _dot(p_ref[...], w_ref[0:512, :]) + _dot(a_ref[...], w_ref[512:1024, :])
        mix_ref[...] = mix
        _, n2 = _rms(mix)
        x1 = x_ref[...] + n2 * g2_ref[...]
        x1_ref[...] = x1
        _, n3 = _rms(x1)
        h2_ref[...] = (n3 * g3_ref[...]).astype(BF16)

    row = lambda w: pl.BlockSpec((tm, w), lambda i: (i, 0))
    vec = pl.BlockSpec((1, D), lambda i: (0, 0))
    return pl.pallas_call(
        body, name="outproj_fwd", grid=(s // tm,),
        in_specs=[row(512), row(512), pl.BlockSpec((D, D), lambda i: (0, 0)), row(D), vec, vec],
        out_specs=[row(D), row(D), row(D)],
        out_shape=[jax.ShapeDtypeStruct((s, D), F32), jax.ShapeDtypeStruct((s, D), F32),
                   jax.ShapeDtypeStruct((s, D), BF16)],
        compiler_params=_cp(1))(pool_o, attn_o, w_out, x, g2, g3)


def _silu_parts(g):
    sg = jax.nn.sigmoid(g)
    return sg, g * sg


def _ffn_fwd(h2, wg, wu, wd, x1, target, g4):
    s = h2.shape[0]
    tm = min(TM_FFN, s)

    def body(h_ref, wg_ref, wu_ref, wd_ref, x1_ref, t_ref, g4_ref,
             gate_ref, up_ref, df_ref, dy_ref, loss_ref, gg4_ref, f_acc):
        i = pl.program_id(0)
        j = pl.program_id(1)
        h = h_ref[...]
        gate = _dot_nt(h, wg_ref[...])
        up = _dot_nt(h, wu_ref[...])
        gate_ref[...] = gate.astype(BF16)
        up_ref[...] = up.astype(BF16)
        _, sl = _silu_parts(gate)
        contrib = _dot((sl * up).astype(BF16), wd_ref[...])

        @pl.when(j == 0)
        def _():
            f_acc[...] = contrib

        @pl.when(j > 0)
        def _():
            f_acc[...] += contrib

        @pl.when((i == 0) & (j == 0))
        def _():
            loss_ref[...] = jnp.zeros_like(loss_ref)
            gg4_ref[...] = jnp.zeros_like(gg4_ref)

        @pl.when(j == NSH - 1)
        def _():
            r4, n4 = _rms(f_acc[...])
            g4v = g4_ref[...]
            err = x1_ref[...] + n4 * g4v - t_ref[...]
            loss_ref[...] += (0.5 / D) * jnp.sum(err * err).reshape(1, 1)
            dy = err * (1.0 / D)
            dy_ref[...] = dy
            df, dg = _rms_bwd(r4, n4, g4v, dy)
            gg4_ref[...] += dg
            df_ref[...] = df.astype(BF16)

    row = lambda w: pl.BlockSpec((tm, w), lambda i, j: (i, 0))
    sh = lambda r, c: pl.BlockSpec((None, r, c), lambda i, j: (j, 0, 0))
    act = pl.BlockSpec((None, tm, FS), lambda i, j: (j, i, 0))
    vec = pl.BlockSpec((1, D), lambda i, j: (0, 0))
    return pl.pallas_call(
        body, name="ffn_fwd", grid=(s // tm, NSH),
        in_specs=[row(D), sh(FS, D), sh(FS, D), sh(FS, D), row(D), row(D), vec],
        out_specs=[act, act, row(D), row(D), pl.BlockSpec((1, 1), lambda i, j: (0, 0)), vec],
        out_shape=[jax.ShapeDtypeStruct((NSH, s, FS), BF16), jax.ShapeDtypeStruct((NSH, s, FS), BF16),
                   jax.ShapeDtypeStruct((s, D), BF16), jax.ShapeDtypeStruct((s, D), F32),
                   jax.ShapeDtypeStruct((1, 1), F32), jax.ShapeDtypeStruct((1, D), F32)],
        scratch_shapes=[pltpu.VMEM((tm, D), F32)],
        compiler_params=_cp(2))(h2, wg, wu, wd, x1, target, g4)


def _ffn_bwd_act(df, gate, up, wg, wu, wd, dy, x1, mix, g3, g2):
    s = df.shape[0]
    tm = min(TM_FFN, s)

    def body(df_ref, gate_ref, up_ref, wg_ref, wu_ref, wd_ref, dy_ref, x1_ref, mix_ref, g3_ref, g2_ref,
             dgate_ref, dup_ref, dx1_ref, dmix_ref, gg3_ref, gg2_ref, acc):
        i = pl.program_id(0)
        j = pl.program_id(1)
        da = _dot_nt(df_ref[...], wd_ref[...])
        g = gate_ref[...].astype(F32)
        u = up_ref[...].astype(F32)
        sg, sl = _silu_parts(g)
        dgate = (da * u * (sg * (1.0 + g * (1.0 - sg)))).astype(BF16)
        dup = (da * sl).astype(BF16)
        dgate_ref[...] = dgate
        dup_ref[...] = dup
        contrib = _dot(dgate, wg_ref[...]) + _dot(dup, wu_ref[...])

        @pl.when(j == 0)
        def _():
            acc[...] = contrib

        @pl.when(j > 0)
        def _():
            acc[...] += contrib

        @pl.when((i == 0) & (j == 0))
        def _():
            gg3_ref[...] = jnp.zeros_like(gg3_ref)
            gg2_ref[...] = jnp.zeros_like(gg2_ref)

        @pl.when(j == NSH - 1)
        def _():
            r3, n3 = _rms(x1_ref[...])
            dx1n, dg3 = _rms_bwd(r3, n3, g3_ref[...], acc[...])
            dx1 = dy_ref[...] + dx1n
            dx1_ref[...] = dx1
            gg3_ref[...] += dg3
            r2, n2 = _rms(mix_ref[...])
            dmix, dg2 = _rms_bwd(r2, n2, g2_ref[...], dx1)
            gg2_ref[...] += dg2
            dmix_ref[...] = dmix.astype(BF16)

    row = lambda w: pl.BlockSpec((tm, w), lambda i, j: (i, 0))
    sh = lambda r, c: pl.BlockSpec((None, r, c), lambda i, j: (j, 0, 0))
    act = pl.BlockSpec((None, tm, FS), lambda i, j: (j, i, 0))
    vec = pl.BlockSpec((1, D), lambda i, j: (0, 0))
    return pl.pallas_call(
        body, name="ffn_bwd_act", grid=(s // tm, NSH),
        in_specs=[row(D), act, act, sh(FS, D), sh(FS, D), sh(FS, D), row(D), row(D), row(D), vec, vec],
        out_specs=[act, act, row(D), row(D), vec, vec],
        out_shape=[jax.ShapeDtypeStruct((NSH, s, FS), BF16), jax.ShapeDtypeStruct((NSH, s, FS), BF16),
                   jax.ShapeDtypeStruct((s, D), F32), jax.ShapeDtypeStruct((s, D), BF16),
                   jax.ShapeDtypeStruct((1, D), F32), jax.ShapeDtypeStruct((1, D), F32)],
        scratch_shapes=[pltpu.VMEM((tm, D), F32)],
        compiler_params=_cp(2))(df, gate, up, wg, wu, wd, dy, x1, mix, g3, g2)


SMEM_SPEC = pl.BlockSpec(memory_space=pltpu.SMEM)


def _emit_halves(acc_ref, row0, rows, c, own_ref, oth_ref):
    half = rows // 2
    own_ref[...] = acc_ref[pl.ds(row0 + pl.multiple_of(c * half, 8), half), :]
    oth_ref[...] = acc_ref[pl.ds(row0 + pl.multiple_of((1 - c) * half, 8), half), :].astype(BF16)


def _half_shapes(rows):
    return [jax.ShapeDtypeStruct((NSH, rows // 2, D), F32), jax.ShapeDtypeStruct((NSH, rows // 2, D), BF16)]


def _ffn_bwd_w(h2, gate, up, dgate, dup, df, c_arr):
    s = h2.shape[0]
    tm = min(TM_FFN, s)
    nt = s // tm

    def body(c_ref, h_ref, gate_ref, up_ref, dgate_ref, dup_ref, df_ref, *rest):
        outs, accs = rest[:6], rest[6:]
        i = pl.program_id(1)
        @pl.when(i == 0)
        def _():
            for acc in accs:
                acc[...] = jnp.zeros_like(acc)

        h = h_ref[...]
        accs[0][...] += _dot_tn(dgate_ref[...], h)
        accs[1][...] += _dot_tn(dup_ref[...], h)
        _, sl = _silu_parts(gate_ref[...].astype(F32))
        a = (sl * up_ref[...].astype(F32)).astype(BF16)
        accs[2][...] += _dot_tn(a, df_ref[...])

        @pl.when(i == nt - 1)
        def _():
            for t, acc in enumerate(accs):
                _emit_halves(acc, 0, FS, c_ref[0], outs[2 * t], outs[2 * t + 1])

    row = lambda w: pl.BlockSpec((tm, w), lambda j, i: (i, 0))
    act = pl.BlockSpec((None, tm, FS), lambda j, i: (j, i, 0))
    half = pl.BlockSpec((None, FS // 2, D), lambda j, i: (j, 0, 0))
    return pl.pallas_call(
        body, name="ffn_bwd_w", grid=(NSH, nt),
        in_specs=[SMEM_SPEC, row(D), act, act, act, act, row(D)],
        out_specs=[half] * 6,
        out_shape=_half_shapes(FS) * 3,
        scratch_shapes=[pltpu.VMEM((FS, D), F32)] * 3,
        compiler_params=_cp(2))(c_arr, h2, gate, up, dgate, dup, df)


def _outproj_bwd(dmix, w_out, pool_o, attn_o, c_arr, dep=None):
    s = dmix.shape[0]
    tm = min(TM, s)
    nt = s // tm

    def body(c_ref, dm_ref, w_ref, p_ref, a_ref, *rest):
        dpool_ref, dattn_ref, own_ref, oth_ref, gw_ref = rest[-5:]
        i = pl.program_id(0)

        @pl.when(i == 0)
        def _():
            gw_ref[...] = jnp.zeros_like(gw_ref)

        dm = dm_ref[...]
        dpool_ref[...] = _dot_nt(dm, w_ref[0:512, :])
        dattn_ref[...] = _dot_nt(dm, w_ref[512:1024, :]).astype(BF16)
        gw_ref[0:512, :] += _dot_tn(p_ref[...], dm)
        gw_ref[512:1024, :] += _dot_tn(a_ref[...], dm)

        @pl.when(i == nt - 1)
        def _():
            for k in range(NSH):
                _emit_halves(gw_ref, k * WOUT_S, WOUT_S, c_ref[0], own_ref.at[k], oth_ref.at[k])

    row = lambda w: pl.BlockSpec((tm, w), lambda i: (i, 0))
    full = pl.BlockSpec((D, D), lambda i: (0, 0))
    half = pl.BlockSpec((NSH, WOUT_S // 2, D), lambda i: (0, 0, 0))
    return pl.pallas_call(
        body, name="outproj_bwd", grid=(nt,),
        in_specs=[SMEM_SPEC, row(D), full, row(512), row(512)] + ([] if dep is None else [ANY]),
        out_specs=[row(512), row(512), half, half],
        out_shape=[jax.ShapeDtypeStruct((s, 512), F32), jax.ShapeDtypeStruct((s, 512), BF16)] + _half_shapes(WOUT_S),
        scratch_shapes=[pltpu.VMEM((D, D), F32)],
        compiler_params=_cp(1))(c_arr, dmix, w_out, pool_o, attn_o, *([] if dep is None else [dep]))


def _pool_bwd(u, dpool, w_pool, pool_scale, dep=None):
    s = u.shape[0]
    tm = min(TM_POOL, s)
    hb = tm // HALO
    nt = s // tm
    last_halo = s // HALO - 1

    def body(uc_ref, uh_ref, dc_ref, dn_ref, wp_ref, sc_ref, *rest):
        du_ref, gwp_ref, gsc_ref = rest[-3:]
        i = pl.program_id(0)

        @pl.when(i == 0)
        def _():
            gwp_ref[...] = jnp.zeros_like(gwp_ref)
            gsc_ref[...] = jnp.zeros_like(gsc_ref)

        cur = uc_ref[...]
        halo = jnp.where(i > 0, uh_ref[...], 0.0)
        pooled = _pooled(jnp.concatenate([halo, cur], axis=0), cur, i * tm, tm)
        dcur = dc_ref[...]
        dnext = jnp.where(i < nt - 1, dn_ref[...], 0.0)
        dext = jnp.concatenate([dcur, dnext], axis=0)
        rows = lax.broadcasted_iota(jnp.int32, (tm, tm + HALO), 0)
        cols = lax.broadcasted_iota(jnp.int32, (tm, tm + HALO), 1)
        d = cols - rows
        t_ext = i * tm + lax.broadcasted_iota(jnp.int32, (tm + HALO, GROUP), 0)
        for g, w in enumerate(WINDOWS):
            sl = slice(g * GROUP, (g + 1) * GROUP)
            pb = pooled[g].astype(BF16)
            wp = wp_ref[g]
            mixed = _dot(pb, wp)
            gsc_ref[:, sl] += jnp.sum(dcur[:, sl] * mixed, axis=0, keepdims=True)
            dmixed = (dext[:, sl] * sc_ref[:, sl]).astype(BF16)
            gwp_ref[g] += _dot_tn(pb, dmixed[0:tm])
            dpooled = _dot_nt(dmixed, wp)
            z = dpooled / jnp.minimum(t_ext + 1, w).astype(F32)
            b = jnp.where((d >= 0) & (d < w), 1.0, 0.0).astype(BF16)
            du_ref[:, sl] = (_band_dot(b, z, 2) - dpooled[0:tm]).astype(BF16)

    return pl.pallas_call(
        body, name="pool_bwd", grid=(nt,),
        in_specs=[pl.BlockSpec((tm, 512), lambda i: (i, 0)),
                  pl.BlockSpec((HALO, 512), lambda i: (jnp.maximum(i * hb - 1, 0), 0)),
                  pl.BlockSpec((tm, 512), lambda i: (i, 0)),
                  pl.BlockSpec((HALO, 512), lambda i: (jnp.minimum((i + 1) * hb, last_halo), 0)),
                  pl.BlockSpec((4, GROUP, GROUP), lambda i: (0, 0, 0)),
                  pl.BlockSpec((1, 512), lambda i: (0, 0))] + ([] if dep is None else [ANY]),
        out_specs=[pl.BlockSpec((tm, 512), lambda i: (i, 0)),
                   pl.BlockSpec((4, GROUP, GROUP), lambda i: (0, 0, 0)),
                   pl.BlockSpec((1, 512), lambda i: (0, 0))],
        out_shape=[jax.ShapeDtypeStruct((s, 512), BF16), jax.ShapeDtypeStruct((4, GROUP, GROUP), F32),
                   jax.ShapeDtypeStruct((1, 512), F32)],
        compiler_params=_cp(1))(u, u, dpool, dpool, w_pool, pool_scale, *([] if dep is None else [dep]))


def _attn_bwd(q, k, v, do, bias, sinks, dep=None):
    s = q.shape[0]

    def body(q_ref, do_ref, k_ref, v_ref, b_ref, sk_ref, *rest):
        dq_ref, dk_ref, dv_ref, dss_ref, gsk_ref = rest[-5:]
        n = pl.program_id(0)

        @pl.when(n == 0)
        def _():
            dk_ref[...] = jnp.zeros_like(dk_ref)
            dv_ref[...] = jnp.zeros_like(dv_ref)
            dss_ref[...] = jnp.zeros_like(dss_ref)
            gsk_ref[...] = jnp.zeros_like(gsk_ref)

        kk, vv, lo, starts = _kv_tiles(k_ref, v_ref, n)
        km = _half_masked(kk, lo)
        bidx = jnp.minimum(n, 1)
        causal = lax.broadcasted_iota(jnp.int32, (BLK, BLK), 1) <= lax.broadcasted_iota(jnp.int32, (BLK, BLK), 0)
        zero = jnp.zeros((BLK, 128), F32)
        dkk = [[zero, zero], [zero, zero]]
        dvv = [[zero, zero], [zero, zero]]
        for p in range(4):
            h = p // 2
            qt = q_ref[:, p * 128:(p + 1) * 128].astype(F32) * SCALE
            dot = do_ref[:, p * 128:(p + 1) * 128]
            dq = zero
            for e in range(2):
                head = 2 * p + e
                sel_q = lo if e == 0 else jnp.logical_not(lo)
                qm = jnp.where(sel_q, qt, 0.0).astype(BF16)
                prob, ps = _attn_probs(qm, kk[0][h], kk[1][h], causal, b_ref[bidx, head], sk_ref[0, head])
                dom = jnp.where(sel_q, dot, jnp.zeros_like(dot))
                dp = jnp.where(causal, _dot_nt(dom, vv[1][h]), _dot_nt(dom, vv[0][h]))
                delta = jnp.sum(prob * dp, axis=-1, keepdims=True)
                ds = prob * (dp - delta)
                gsk_ref[pl.ds(head, 1), :] += jnp.broadcast_to(-jnp.sum(ps * delta).reshape(1, 1), (1, 128))
                dss_ref[head] += ds
                parts = ((jnp.where(causal, 0.0, ds).astype(BF16), jnp.where(causal, 0.0, prob).astype(BF16)),
                         (jnp.where(causal, ds, 0.0).astype(BF16), jnp.where(causal, prob, 0.0).astype(BF16)))
                for t, (ds_t, p_t) in enumerate(parts):
                    dq = dq + _dot(ds_t, km[t][h][e])
                    dkk[t][h] = dkk[t][h] + _dot_tn(ds_t, qm)
                    dvv[t][h] = dvv[t][h] + _dot_tn(p_t, dom)
            dq_ref[:, p * 128:(p + 1) * 128] = (dq * SCALE).astype(BF16)
        for acc, ref in ((dkk, dk_ref), (dvv, dv_ref)):
            for t in range(2):
                f0 = acc[t][0] + pltpu.roll(acc[t][0], 64, axis=1)
                f1 = acc[t][1] + pltpu.roll(acc[t][1], 64, axis=1)
                ref[pl.ds(starts[t], BLK), :] += jnp.where(lo, f0, f1)

    blk = pl.BlockSpec((BLK, 512), lambda i: (i, 0))
    kv = pl.BlockSpec((s, 128), lambda i: (0, 0))
    return pl.pallas_call(
        body, name="attn_bwd", grid=(s // BLK,),
        in_specs=[blk, blk, kv, kv, pl.BlockSpec((2, NQ, BLK, BLK), lambda i: (0, 0, 0, 0)),
                  pl.BlockSpec(memory_space=pltpu.SMEM)] + ([] if dep is None else [ANY]),
        out_specs=[blk, kv, kv, pl.BlockSpec((NQ, BLK, BLK), lambda i: (0, 0, 0)),
                   pl.BlockSpec((NQ, 128), lambda i: (0, 0))],
        out_shape=[jax.ShapeDtypeStruct((s, 512), BF16), jax.ShapeDtypeStruct((s, 128), F32),
                   jax.ShapeDtypeStruct((s, 128), F32), jax.ShapeDtypeStruct((NQ, BLK, BLK), F32),
                   jax.ShapeDtypeStruct((NQ, 128), F32)],
        compiler_params=_cp(1))(q, do, k, v, bias, sinks, *([] if dep is None else [dep]))


def _relbias_grad(dss, bucket):
    def body(ds_ref, b_ref, o_ref):
        bucket_v = b_ref[...]
        lane = lax.broadcasted_iota(jnp.int32, (NQ, 128), 1)
        head_row = lax.broadcasted_iota(jnp.int32, (NQ, BLK), 0)
        acc = jnp.zeros((NQ, 128), F32)
        for b in range(NBUCKET):
            sel = bucket_v == b
            stack = jnp.zeros((NQ, BLK), F32)
            for h in range(NQ):
                col_sums = jnp.sum(jnp.where(sel, ds_ref[h], 0.0), axis=0, keepdims=True)
                stack = jnp.where(head_row == h, col_sums, stack)
            acc = acc + jnp.where(lane == b, jnp.sum(stack, axis=1, keepdims=True), 0.0)
        o_ref[...] = acc

    return pl.pallas_call(
        body, name="relbias_grad",
        in_specs=[pl.BlockSpec(memory_space=pltpu.VMEM), pl.BlockSpec(memory_space=pltpu.VMEM)],
        out_specs=pl.BlockSpec(memory_space=pltpu.VMEM),
        out_shape=jax.ShapeDtypeStruct((NQ, 128), F32),
        compiler_params=_cp())(dss, bucket)


def _inproj_bwd(x, g1, du, dq, dk, dv, w_in, dx1, c_arr):
    s = x.shape[0]
    tm = min(TM, s)
    nt = s // tm
    cols = ((0, 512), (512, 1024), (1024, 1152), (1152, 1280))

    def body(c_ref, x_ref, g_ref, du_ref, dq_ref, dk_ref, dv_ref, w_ref, dx1_ref, gx_ref, own_ref, oth_ref, gg_ref,
             gw_ref):
        i = pl.program_id(0)

        @pl.when(i == 0)
        def _():
            gw_ref[...] = jnp.zeros_like(gw_ref)
            gg_ref[...] = jnp.zeros_like(gg_ref)

        gv = g_ref[...]
        r1, n1 = _rms(x_ref[...])
        h = (n1 * gv).astype(BF16)
        parts = (du_ref[...], dq_ref[...], dk_ref[...].astype(BF16), dv_ref[...].astype(BF16))
        dh = None
        for (a, b), dpart in zip(cols, parts):
            t = _dot(dpart, w_ref[a:b, :])
            dh = t if dh is None else dh + t
            gw_ref[a:b, :] += _dot_tn(dpart, h)
        dx, dg = _rms_bwd(r1, n1, gv, dh)
        gg_ref[...] += dg
        gx_ref[...] = dx1_ref[...] + dx

        @pl.when(i == nt - 1)
        def _():
            for k in range(NSH):
                _emit_halves(gw_ref, k * WIN_S, WIN_S, c_ref[0], own_ref.at[k], oth_ref.at[k])

    row = lambda w: pl.BlockSpec((tm, w), lambda i: (i, 0))
    vec = pl.BlockSpec((1, D), lambda i: (0, 0))
    full = pl.BlockSpec((IN_W, D), lambda i: (0, 0))
    half = pl.BlockSpec((NSH, WIN_S // 2, D), lambda i: (0, 0, 0))
    return pl.pallas_call(
        body, name="inproj_bwd", grid=(nt,),
        in_specs=[SMEM_SPEC, row(D), vec, row(512), row(512), row(128), row(128), full, row(D)],
        out_specs=[row(D), half, half, vec],
        out_shape=[jax.ShapeDtypeStruct((s, D), F32)] + _half_shapes(WIN_S) + [jax.ShapeDtypeStruct((1, D), F32)],
        scratch_shapes=[pltpu.VMEM((IN_W, D), F32)],
        compiler_params=_cp(1))(c_arr, x, g1, du, dq, dk, dv, w_in, dx1)


def _local_step(x, target, small, w_in, w_out, ffn_weights, c_arr, on_ffn_grads=None, on_wout_grads=None):
    g1, g2, g3, g4, w_pool, pool_scale, rel_bias, sinks = small
    bucket = jnp.asarray(_bucket_table())
    wp_bf = w_pool.astype(BF16)
    u, q, k, v = _inproj_fwd(x, g1, w_in)
    pool_o = _pool_fwd(u, wp_bf, pool_scale)
    bias = _bias_table(bucket, rel_bias)
    attn_o = _attn_fwd(q, k, v, bias, sinks)
    wg, wu, wd = ffn_weights(pool_o, attn_o) if callable(ffn_weights) else ffn_weights
    mix, x1, h2 = _outproj_fwd(pool_o, attn_o, w_out, x, g2, g3)
    gate, up, df, dy, loss, gg4 = _ffn_fwd(h2, wg, wu, wd, x1, target, g4)
    dgate, dup, dx1, dmix, gg3, gg2 = _ffn_bwd_act(df, gate, up, wg, wu, wd, dy, x1, mix, g3, g2)
    ffn_g = _ffn_bwd_w(h2, gate, up, dgate, dup, df, c_arr)
    dep = None if on_ffn_grads is None else on_ffn_grads(ffn_g)
    dpool, dattn, wout_own, wout_oth = _outproj_bwd(dmix, w_out, pool_o, attn_o, c_arr, dep)
    dep = None if on_wout_grads is None else on_wout_grads(wout_own, wout_oth, dpool)
    du, gwp, gsc = _pool_bwd(u, dpool, wp_bf, pool_scale, dep)
    dq, dk, dv, dss, gsk = _attn_bwd(q, k, v, dattn, bias, sinks, dep)
    grb = _relbias_grad(dss, bucket)
    gx, win_own, win_oth, gg1 = _inproj_bwd(x, g1, du, dq, dk, dv, w_in, dx1, c_arr)
    small_grads = (gg1, gg2, gg3, gg4, gwp, gsc, grb[:, :NBUCKET].T, gsk[:, 0].reshape(1, NQ))
    return loss, gx, small_grads, ((win_own, win_oth), (wout_own, wout_oth), ffn_g)


def _place():
    x, y, c = lax.axis_index("x"), lax.axis_index("y"), lax.axis_index("c")
    chips = ((1 - x, y), (x, 1 - y), (1 - x, 1 - y))
    return x, y, c, chips


def _remote(src, dst, ssem, rsem, dev):
    return pltpu.make_async_remote_copy(src_ref=src, dst_ref=dst, send_sem=ssem, recv_sem=rsem,
                                        device_id=dev, device_id_type=MESH_T)


def _own_slot(shard):
    me = 2 * lax.axis_index("x") + lax.axis_index("y")
    return lax.dynamic_update_slice(lax.empty((NSH,) + shard.shape, shard.dtype), shard[None], (me, 0, 0))


def _all_gather_weights(shards, lands):
    nt = len(shards)

    def body(*refs):
        srcs, dsts = refs[:nt], refs[2 * nt:3 * nt]
        isend, irecv, dsend, drecv = refs[3 * nt:]
        x, y, c, chips = _place()
        me = 2 * x + y
        sib = (x, y, 1 - c)
        sends = []
        for t in range(nt):
            half = srcs[t].shape[0] // 2
            mine = pl.ds(pl.multiple_of(c * half, 16), half)
            for j, (px, py) in enumerate(chips):
                cp = _remote(srcs[t].at[mine], dsts[t].at[me, mine], isend.at[t, j], irecv.at[t, j], (px, py, c))
                cp.start()
                sends.append(cp)
        for t in range(nt):
            half = srcs[t].shape[0] // 2
            mine = pl.ds(pl.multiple_of(c * half, 16), half)
            for j, (px, py) in enumerate(chips):
                blk = dsts[t].at[2 * px + py, mine]
                _remote(blk, blk, isend.at[t, j], irecv.at[t, j], (px, py, c)).wait_recv()
                cp = _remote(blk, blk, dsend.at[t, j], drecv.at[t, j], sib)
                cp.start()
                sends.append(cp)
        for t in range(nt):
            half = srcs[t].shape[0] // 2
            other = pl.ds(pl.multiple_of((1 - c) * half, 16), half)
            for j, (px, py) in enumerate(chips):
                blk = dsts[t].at[2 * px + py, other]
                _remote(blk, blk, dsend.at[t, j], drecv.at[t, j], sib).wait_recv()
        for cp in sends:
            cp.wait_send()

    return pl.pallas_call(
        body, name="allgather_weights",
        in_specs=[ANY] * (2 * nt), out_specs=[ANY] * nt,
        out_shape=[jax.ShapeDtypeStruct(a.shape, a.dtype) for a in lands],
        input_output_aliases={nt + t: t for t in range(nt)},
        scratch_shapes=[pltpu.SemaphoreType.DMA((nt, 3)), pltpu.SemaphoreType.DMA((nt, 3)),
                        pltpu.SemaphoreType.DMA((nt, 3)), pltpu.SemaphoreType.DMA((nt, 3))],
        compiler_params=_cp())(*shards, *lands)


def _to_sibling(arrs, name):
    nt = len(arrs)

    def body(*refs):
        srcs, dsts = refs[:nt], refs[nt:2 * nt]
        ssem, rsem = refs[2 * nt:]
        x, y, c, _ = _place()
        cps = [_remote(srcs[t], dsts[t], ssem.at[t], rsem.at[t], (x, y, 1 - c)) for t in range(nt)]
        for cp in cps:
            cp.start()
        for cp in cps:
            cp.wait()

    return pl.pallas_call(
        body, name=name, in_specs=[ANY] * nt, out_specs=[ANY] * nt,
        out_shape=[jax.ShapeDtypeStruct(a.shape, a.dtype) for a in arrs],
        scratch_shapes=[pltpu.SemaphoreType.DMA((nt,)), pltpu.SemaphoreType.DMA((nt,))],
        compiler_params=_cp())(*arrs)


def _scatter_to_chips(arrs):
    nt = len(arrs)

    def body(*refs):
        srcs, dsts = refs[:nt], refs[nt:2 * nt]
        ssem, rsem = refs[2 * nt:]
        x, y, c, chips = _place()
        cps = []
        for t in range(nt):
            for j, (px, py) in enumerate(chips):
                cps.append(_remote(srcs[t].at[2 * px + py], dsts[t].at[j], ssem.at[t, j], rsem.at[t, j], (px, py, c)))
        for cp in cps:
            cp.start()
        for cp in cps:
            cp.wait()

    return pl.pallas_call(
        body, name="scatter_to_chips", in_specs=[ANY] * nt, out_specs=[ANY] * nt,
        out_shape=[jax.ShapeDtypeStruct((3,) + a.shape[1:], a.dtype) for a in arrs],
        scratch_shapes=[pltpu.SemaphoreType.DMA((nt, 3)), pltpu.SemaphoreType.DMA((nt, 3))],
        compiler_params=_cp())(*arrs)


HBM_SPEC = pl.BlockSpec(memory_space=pltpu.HBM)
SEM_SPEC = pl.BlockSpec(memory_space=pltpu.SEMAPHORE)
DATAFLOW = pltpu.SideEffectType.DATAFLOW_SIDE_EFFECTING


def _gather_copies(srcs, lands, ssem, rsem):
    x, y, c, chips = _place()
    me = 2 * x + y
    out = []
    for t in range(len(srcs)):
        half = srcs[t].shape[0] // 2
        mine = pl.ds(pl.multiple_of(c * half, 16), half)
        for j, (px, py) in enumerate(chips):
            k = 3 * t + j
            send = _remote(srcs[t].at[mine], lands[t].at[me, mine], ssem.at[k], rsem.at[k], (px, py, c))
            blk = lands[t].at[2 * px + py, mine]
            out.append((send, _remote(blk, blk, ssem.at[k], rsem.at[k], (px, py, c))))
    return out


def _scatter_copies(srcs, lands, ssem, rsem):
    x, y, c, chips = _place()
    out = []
    for t in range(len(srcs)):
        for j, (px, py) in enumerate(chips):
            k = 3 * t + j
            send = _remote(srcs[t].at[2 * px + py], lands[t].at[j], ssem.at[k], rsem.at[k], (px, py, c))
            out.append((send, _remote(lands[t].at[j], lands[t].at[j], ssem.at[k], rsem.at[k], (px, py, c))))
    return out


def _sibling_copies(srcs, lands, ssem, rsem):
    x, y, c, _ = _place()
    sib = (x, y, 1 - c)
    return [(_remote(srcs[t], lands[t], ssem.at[t], rsem.at[t], sib),
             _remote(lands[t], lands[t], ssem.at[t], rsem.at[t], sib)) for t in range(len(srcs))]


def _peer_copies(srcs, lands, ssem, rsem):
    x, y, c, _ = _place()
    me = 4 * x + 2 * y + c
    out = []
    for kk in range(1, 8):
        px, py, pc = x ^ (kk >> 2), y ^ ((kk >> 1) & 1), c ^ (kk & 1)
        send = _remote(srcs[0], lands[0].at[me], ssem.at[kk - 1], rsem.at[kk - 1], (px, py, pc))
        slot = lands[0].at[4 * px + 2 * py + pc]
        out.append((send, _remote(slot, slot, ssem.at[kk - 1], rsem.at[kk - 1], (px, py, pc))))
    return out


def _split_start(plan, ncopy, srcs, lands, after, name):
    ns, nbuf = len(srcs), len(srcs) + len(lands)

    def body(*refs):
        ssem, rsem, token = refs[nbuf + 1], refs[nbuf + 2], refs[-1]
        for send, _ in plan(refs[:ns], refs[ns:nbuf], ssem, rsem):
            send.start()
        token[...] = jnp.zeros_like(token)

    bufs = [pltpu.with_memory_space_constraint(a, pltpu.HBM) for a in list(srcs) + list(lands)]
    outs = pl.pallas_call(
        body, name=name, in_specs=[HBM_SPEC] * nbuf + [ANY],
        out_specs=[SEM_SPEC, SEM_SPEC] + [HBM_SPEC] * nbuf + [pl.BlockSpec(memory_space=pltpu.VMEM)],
        out_shape=[pltpu.SemaphoreType.DMA((ncopy,)), pltpu.SemaphoreType.DMA((ncopy,))]
        + [pltpu.HBM(a.shape, a.dtype) for a in bufs] + [jax.ShapeDtypeStruct((8, 128), F32)],
        input_output_aliases={i: 2 + i for i in range(nbuf)},
        compiler_params=pltpu.CompilerParams(has_side_effects=DATAFLOW))(*bufs, after)
    return outs[0], outs[1], outs[2:2 + ns], outs[2 + ns:2 + nbuf], outs[-1]


def _split_wait(plan, ssem, rsem, srcs, lands, after, name):
    ns, nbuf = len(srcs), len(srcs) + len(lands)

    def body(*refs):
        s_ref, r_ref = refs[nbuf], refs[nbuf + 1]
        for send, recv in plan(refs[:ns], refs[ns:nbuf], s_ref, r_ref):
            send.wait_send()
            recv.wait_recv()

    outs = pl.pallas_call(
        body, name=name, in_specs=[HBM_SPEC] * nbuf + [SEM_SPEC, SEM_SPEC] + [ANY] * len(after),
        out_specs=[HBM_SPEC] * nbuf,
        out_shape=[pltpu.HBM(a.shape, a.dtype) for a in list(srcs) + list(lands)],
        input_output_aliases={i: i for i in range(nbuf)},
        compiler_params=pltpu.CompilerParams(has_side_effects=DATAFLOW))(*srcs, *lands, ssem, rsem, *after)
    return outs


def _gather_finish(lands):
    nt = len(lands)

    def body(*refs):
        outs = refs[nt:2 * nt]
        dsend, drecv = refs[2 * nt:]
        x, y, c, chips = _place()
        sib = (x, y, 1 - c)
        sends = []
        for t in range(nt):
            half = outs[t].shape[1] // 2
            mine = pl.ds(pl.multiple_of(c * half, 16), half)
            for j, (px, py) in enumerate(chips):
                blk = outs[t].at[2 * px + py, mine]
                cp = _remote(blk, blk, dsend.at[t, j], drecv.at[t, j], sib)
                cp.start()
                sends.append(cp)
        for t in range(nt):
            half = outs[t].shape[1] // 2
            other = pl.ds(pl.multiple_of((1 - c) * half, 16), half)
            for j, (px, py) in enumerate(chips):
                blk = outs[t].at[2 * px + py, other]
                _remote(blk, blk, dsend.at[t, j], drecv.at[t, j], sib).wait_recv()
        for cp in sends:
            cp.wait_send()

    return pl.pallas_call(
        body, name="gather_finish", in_specs=[ANY] * nt, out_specs=[ANY] * nt,
        out_shape=[jax.ShapeDtypeStruct(a.shape, a.dtype) for a in lands],
        input_output_aliases={t: t for t in range(nt)},
        scratch_shapes=[pltpu.SemaphoreType.DMA((nt, 3)), pltpu.SemaphoreType.DMA((nt, 3))],
        compiler_params=_cp())(*lands)


def _chip_partial(owns, recv, chip_arr, tag):
    nt = len(owns)

    def body(chip_ref, *refs):
        k = pl.program_id(0)
        for t in range(nt):
            p = refs[t][...] + refs[nt + t][...].astype(F32)
            refs[2 * nt + t][...] = p.astype(BF16)

            @pl.when(k == chip_ref[0])
            def _():
                refs[3 * nt + t][...] = p

    blk_specs = [pl.BlockSpec((None,) + a.shape[1:], lambda k, chip_ref: (k, 0, 0)) for a in owns]
    own_specs = [pl.BlockSpec(a.shape[1:], lambda k, chip_ref: (0, 0)) for a in owns]
    return pl.pallas_call(
        body, name="rs_chip_partial_" + tag,
        grid_spec=pltpu.PrefetchScalarGridSpec(num_scalar_prefetch=1, grid=(NSH,), in_specs=blk_specs * 2,
                                               out_specs=blk_specs + own_specs),
        out_shape=[jax.ShapeDtypeStruct(a.shape, BF16) for a in owns]
        + [jax.ShapeDtypeStruct(a.shape[1:], F32) for a in owns],
        compiler_params=_cp(1))(chip_arr, *owns, *recv)


def _sum_chips(own, recv, tag, after=()):
    nt = len(own)

    def body(*refs):
        outs = refs[2 * nt + len(after):]
        for t in range(nt):
            r = refs[nt + t]
            outs[t][...] = ((refs[t][...] + r[0].astype(F32)) + r[1].astype(F32)) + r[2].astype(F32)

    vm = pl.BlockSpec(memory_space=pltpu.VMEM)
    return pl.pallas_call(
        body, name="rs_sum_chips_" + tag, in_specs=[vm] * (2 * nt) + [ANY] * len(after), out_specs=[vm] * nt,
        out_shape=[jax.ShapeDtypeStruct(a.shape, F32) for a in own],
        compiler_params=_cp())(*own, *recv, *after)


def _adamw_math(w, g, m, v):
    m = ADAM_B1 * m + (1.0 - ADAM_B1) * g
    v = ADAM_B2 * v + (1.0 - ADAM_B2) * (g * g)
    m_hat = m / (1.0 - ADAM_B1 ** ADAM_STEP)
    v_hat = v / (1.0 - ADAM_B2 ** ADAM_STEP)
    delta = -ADAM_LR * (m_hat / (jnp.sqrt(v_hat) + ADAM_EPS) + ADAM_WD * w)
    return delta, m, v


ADAM_SPLIT = 4


def _adamw_shards(own, sib, ws, ms, vs, c_arr, tag):
    nt = len(own)

    def body(c_ref, *refs):
        hh = pl.program_id(0)
        mine = hh == c_ref[0]
        for t in range(nt):
            g = jnp.where(mine, refs[t][...], refs[nt + t][...])
            delta, m, v = _adamw_math(refs[2 * nt + t][...], g, refs[3 * nt + t][...], refs[4 * nt + t][...])
            refs[5 * nt + t][...] = g
            refs[6 * nt + t][...] = delta
            refs[7 * nt + t][...] = m
            refs[8 * nt + t][...] = v

    def tile(a):
        return (a.shape[0] // ADAM_SPLIT, a.shape[1])

    half_specs = [pl.BlockSpec(tile(a), lambda hh, q, c_ref: (q, 0)) for a in own]
    full_specs = [pl.BlockSpec(tile(a), lambda hh, q, c_ref: (hh * ADAM_SPLIT + q, 0)) for a in own]
    return pl.pallas_call(
        body, name="adamw_shards_" + tag,
        grid_spec=pltpu.PrefetchScalarGridSpec(num_scalar_prefetch=1, grid=(2, ADAM_SPLIT),
                                               in_specs=half_specs * 2 + full_specs * 3, out_specs=full_specs * 4),
        out_shape=[jax.ShapeDtypeStruct(w.shape, F32) for w in ws] * 4,
        compiler_params=_cp(2))(c_arr, *own, *sib, *ws, *ms, *vs)


def _small_sum_adamw(gathered, wp, mp, vp):
    def body(g_ref, w_ref, m_ref, v_ref, go_ref, d_ref, mo_ref, vo_ref):
        g = g_ref[0]
        for d in range(1, 8):
            g = g + g_ref[d]
        delta, m, v = _adamw_math(w_ref[...], g, m_ref[...], v_ref[...])
        go_ref[...] = g
        d_ref[...] = delta
        mo_ref[...] = m
        vo_ref[...] = v

    vm = pl.BlockSpec(memory_space=pltpu.VMEM)
    return pl.pallas_call(
        body, name="small_sum_adamw", in_specs=[vm] * 4, out_specs=[vm] * 4,
        out_shape=[jax.ShapeDtypeStruct(wp.shape, F32)] * 4,
        compiler_params=_cp())(gathered, wp, mp, vp)


SMALL_PARTS = (("g1", (1, D)), ("g2", (1, D)), ("g3", (1, D)), ("g4", (1, D)), ("w_pool", (1, 4, GROUP, GROUP)),
               ("pool_scale", (1, POOL_W)), ("rel_bias", (NBUCKET, NQ)), ("sinks", (1, NQ)), ("loss", (1, 1)))


def _pack_small(parts):
    rows = []
    for a in parts:
        flat = a.reshape(-1)
        n = flat.shape[0]
        padded = -(-n // 1024) * 1024
        rows.append(jnp.pad(flat, (0, padded - n)).reshape(-1, 128))
    return jnp.concatenate(rows, axis=0)


def _unpack_small(packed):
    out, r = [], 0
    for _, shape in SMALL_PARTS:
        n = int(np.prod(shape))
        nr = -(-n // 1024) * 8
        out.append(packed[r:r + nr].reshape(-1)[:n].reshape(shape))
        r += nr
    return out


def kernel(x, g_pre_mix, w_in, w_pool, pool_scale, rel_bias, sinks, w_out, g_post_mix, g_pre_ffn, w_gate, w_up, w_down, g_post_ffn, loss_target, m_g_pre_mix, m_w_in, m_w_pool, m_pool_scale, m_rel_bias, m_sinks, m_w_out, m_g_post_mix, m_g_pre_ffn, m_w_gate, m_w_up, m_w_down, m_g_post_ffn, v_g_pre_mix, v_w_in, v_w_pool, v_pool_scale, v_rel_bias, v_sinks, v_w_out, v_g_post_mix, v_g_pre_ffn, v_w_gate, v_w_up, v_w_down, v_g_post_ffn):
    c = lax.axis_index("c")
    chip = 2 * lax.axis_index("x") + lax.axis_index("y")

    def shards(a_in, a_out, a_gate, a_up, a_down):
        return (a_in[0].T, a_out[0], a_gate[0].T, a_up[0].T, a_down[0])

    c_arr = c.reshape(1).astype(jnp.int32)
    chip_arr = chip.reshape(1).astype(jnp.int32)

    big_w = shards(w_in, w_out, w_gate, w_up, w_down)
    big_bf = [w.astype(BF16) for w in big_w]
    win_all, wout_all = _all_gather_weights(big_bf[:2], [_own_slot(a) for a in big_bf[:2]])
    g_ssem, g_rsem, g_srcs, g_lands, g_token = _split_start(
        _gather_copies, 9, big_bf[2:], [_own_slot(a) for a in big_bf[2:]], win_all, "gather_ffn_start")

    def ffn_weights(*after):
        outs = _split_wait(_gather_copies, g_ssem, g_rsem, g_srcs, g_lands, after, "gather_ffn_wait")
        return _gather_finish(outs[3:])

    rs = {}

    def on_ffn_grads(ffn_g):
        oths = ffn_g[1::2]
        rs["ffn_own"] = ffn_g[0::2]
        rs["x"] = _split_start(_sibling_copies, 3, oths, [lax.empty(a.shape, BF16) for a in oths], ffn_g[0],
                               "rs_exchange_ffn_start")
        return rs["x"][4]

    def on_wout_grads(own, oth, after):
        ssem, rsem, srcs, lands, _ = rs["x"]
        recv_ffn = _split_wait(_sibling_copies, ssem, rsem, srcs, lands, [after], "rs_exchange_ffn_wait")[3:]
        recv_wout = _to_sibling([oth], "rs_exchange_wout")
        outs = _chip_partial([own] + list(rs["ffn_own"]), list(recv_wout) + list(recv_ffn), chip_arr, "early")
        rs["early_own"] = outs[4:]
        rs["s"] = _split_start(_scatter_copies, 12, outs[:4],
                               [lax.empty((3,) + a.shape[1:], BF16) for a in outs[:4]], outs[4], "scatter_early_start")
        return rs["s"][4]

    small = (g_pre_mix + g_token[:1, :1], g_post_mix, g_pre_ffn, g_post_ffn, w_pool[0], pool_scale, rel_bias, sinks)
    loss, gx, small_grads, ((win_own, win_oth), _, _) = _local_step(
        x[0], loss_target[0], small, win_all.reshape(IN_W, D), wout_all.reshape(D, D), ffn_weights, c_arr,
        on_ffn_grads, on_wout_grads)
    ssem, rsem, srcs, lands, _ = rs["s"]
    recv_early = _split_wait(_scatter_copies, ssem, rsem, srcs, lands, [gx], "scatter_early_wait")[4:]

    unused = jnp.zeros((1, 1), F32)
    gp = _pack_small(small_grads + (loss,))
    wp = _pack_small((g_pre_mix, g_post_mix, g_pre_ffn, g_post_ffn, w_pool, pool_scale, rel_bias, sinks, unused))
    mp = _pack_small((m_g_pre_mix, m_g_post_mix, m_g_pre_ffn, m_g_post_ffn, m_w_pool, m_pool_scale, m_rel_bias,
                      m_sinks, unused))
    vp = _pack_small((v_g_pre_mix, v_g_post_mix, v_g_pre_ffn, v_g_post_ffn, v_w_pool, v_pool_scale, v_rel_bias,
                      v_sinks, unused))

    moments = (shards(m_w_in, m_w_out, m_w_gate, m_w_up, m_w_down),
               shards(v_w_in, v_w_out, v_w_gate, v_w_up, v_w_down))
    recv_a = _to_sibling([win_oth], "rs_exchange_win")
    outs = _chip_partial([win_own], recv_a, chip_arr, "win")
    w_ssem, w_rsem, w_srcs, w_lands, w_token = _split_start(
        _scatter_copies, 3, outs[:1], [lax.empty((3,) + outs[0].shape[1:], BF16)], gx, "scatter_win_start")
    slots = lax.dynamic_update_slice(lax.empty((8,) + gp.shape, F32), gp[None], (2 * chip + c, 0, 0))
    p_ssem, p_rsem, p_srcs, p_lands, p_token = _split_start(_peer_copies, 7, [gp], [slots], w_token,
                                                            "small_allgather_start")
    early_final = _sum_chips(list(rs["early_own"]), list(recv_early), "early", [p_token])
    early_sib = _to_sibling(early_final, "rs_share_early")
    adam_e = _adamw_shards(early_final, early_sib, big_w[1:], moments[0][1:], moments[1][1:], c_arr, "early")
    recv_win = _split_wait(_scatter_copies, w_ssem, w_rsem, w_srcs, w_lands, [adam_e[0]], "scatter_win_wait")[1:]
    win_final = _sum_chips([outs[1]], recv_win, "win")
    win_sib = _to_sibling(win_final, "rs_share_win")
    adam_w = _adamw_shards(win_final, win_sib, big_w[:1], moments[0][:1], moments[1][:1], c_arr, "win")
    big_g, big_d, big_m, big_v = [[adam_w[i]] + list(adam_e[4 * i:4 * i + 4]) for i in range(4)]

    gathered = _split_wait(_peer_copies, p_ssem, p_rsem, p_srcs, p_lands, [adam_w[0]], "small_allgather_wait")[1]
    small_out = [_unpack_small(p) for p in _small_sum_adamw(gathered, wp, mp, vp)]
    total_loss = small_out[0][8][0, 0]

    def ordered(small4, big4):
        sg1, sg2, sg3, sg4, swp, ssc, srb, ssk = small4[:8]
        bwin, bwout, bwg, bwu, bwd = big4[0].T[None], big4[1][None], big4[2].T[None], big4[3].T[None], big4[4][None]
        return [sg1, bwin, swp, ssc, srb, ssk, bwout, sg2, sg3, bwg, bwu, bwd, sg4]

    res = [total_loss, gx[None]]
    for sm, bg in zip(small_out, (big_g, big_d, big_m, big_v)):
        res += ordered(sm, bg)
    return tuple(res)
```

```python
import functools

import numpy as np
import jax
import jax.numpy as jnp
from jax import lax
from jax.experimental import pallas as pl
from jax.experimental.pallas import tpu as pltpu

F32 = jnp.float32
BF16 = jnp.bfloat16

D = 1024
POOL_W = 512
GROUP = 128
WINDOWS = (2, 4, 8, 16)
HALO = 128
NQ = 8
BLK = 128
NBUCKET = 32
IN_W = 1280
DFF = 2816
NSH = 4
FS = DFF // NSH
WIN_S = IN_W // NSH
WOUT_S = D // NSH
EPS = 1e-6
NEG = -1e30
SCALE = 0.125

ADAM_LR = 0.001
ADAM_B1 = 0.9
ADAM_B2 = 0.999
ADAM_EPS = 1e-08
ADAM_WD = 0.01
ADAM_STEP = 10

TM = 512
TM_POOL = 256
TM_FFN = 512
VMEM_LIMIT = 56 * 1024 * 1024

MESH_T = pl.DeviceIdType.MESH
ANY = pl.BlockSpec(memory_space=pl.ANY)


def _cp(n_grid=0, **kw):
    sem = ("arbitrary",) * n_grid if n_grid else None
    return pltpu.CompilerParams(dimension_semantics=sem, vmem_limit_bytes=VMEM_LIMIT, **kw)


def _dot(a, b):
    return jnp.dot(a, b, preferred_element_type=F32)


def _dot_nt(a, b):
    return lax.dot_general(a, b, (((1,), (1,)), ((), ())), preferred_element_type=F32)


def _dot_tn(a, b):
    return lax.dot_general(a, b, (((0,), (0,)), ((), ())), preferred_element_type=F32)


def _rms(x):
    r = lax.rsqrt(jnp.mean(x * x, axis=-1, keepdims=True) + EPS)
    return r, x * r


def _rms_bwd(r, n, g, dout):
    dn = dout * g
    dx = r * (dn - n * jnp.mean(dn * n, axis=-1, keepdims=True))
    dg = jnp.sum(dout * n, axis=0, keepdims=True)
    return dx, dg


def _split(x, n):
    parts = []
    r = x
    for _ in range(n):
        p = r.astype(BF16)
        parts.append(p)
        r = r - p.astype(F32)
    return parts


def _band_dot(a, x, n):
    acc = None
    for p in _split(x, n):
        t = _dot(a, p)
        acc = t if acc is None else acc + t
    return acc


def _bucket_table():
    qi = np.arange(BLK)[None, :]
    kj = np.arange(2 * BLK)[:, None]
    dist = qi + BLK - kj
    n = np.maximum(dist, 0)
    nf = np.maximum(n, 1).astype(np.float32)
    large = 16 + (np.log(nf / np.float32(16)) / np.float32(np.log(128 / 16)) * np.float32(16)).astype(np.int32)
    large = np.minimum(large, NBUCKET - 1)
    return np.where(n < 16, n, large).astype(np.int32)


def _inproj_fwd(x, g1, w_in):
    s = x.shape[0]
    tm = min(TM, s)

    def body(x_ref, g_ref, w_ref, u_ref, q_ref, k_ref, v_ref):
        _, n = _rms(x_ref[...])
        h = (n * g_ref[...]).astype(BF16)
        proj = _dot_nt(h, w_ref[...])
        u_ref[...] = proj[:, :512]
        q_ref[...] = proj[:, 512:1024].astype(BF16)
        k_ref[...] = proj[:, 1024:1152].astype(BF16)
        v_ref[...] = proj[:, 1152:1280].astype(BF16)

    row = lambda w: pl.BlockSpec((tm, w), lambda i: (i, 0))
    return pl.pallas_call(
        body, name="inproj_fwd", grid=(s // tm,),
        in_specs=[row(D), pl.BlockSpec((1, D), lambda i: (0, 0)), pl.BlockSpec((IN_W, D), lambda i: (0, 0))],
        out_specs=[row(512), row(512), row(128), row(128)],
        out_shape=[jax.ShapeDtypeStruct((s, 512), F32), jax.ShapeDtypeStruct((s, 512), BF16),
                   jax.ShapeDtypeStruct((s, 128), BF16), jax.ShapeDtypeStruct((s, 128), BF16)],
        compiler_params=_cp(1))(x, g1, w_in)


def _pooled(ext, cur, t0, tm):
    rows = lax.broadcasted_iota(jnp.int32, (tm, tm + HALO), 0)
    cols = lax.broadcasted_iota(jnp.int32, (tm, tm + HALO), 1)
    d = rows + HALO - cols
    t = t0 + lax.broadcasted_iota(jnp.int32, (tm, GROUP), 0)
    out = []
    for g, w in enumerate(WINDOWS):
        a = jnp.where((d >= 0) & (d < w), 1.0, 0.0).astype(BF16)
        ws = _band_dot(a, ext[:, g * GROUP:(g + 1) * GROUP], 3)
        cnt = jnp.minimum(t + 1, w).astype(F32)
        out.append(ws / cnt - cur[:, g * GROUP:(g + 1) * GROUP])
    return out


def _pool_fwd(u, w_pool, pool_scale):
    s = u.shape[0]
    tm = min(TM_POOL, s)
    hb = tm // HALO

    def body(uc_ref, uh_ref, wp_ref, sc_ref, o_ref):
        i = pl.program_id(0)
        cur = uc_ref[...]
        halo = jnp.where(i > 0, uh_ref[...], 0.0)
        ext = jnp.concatenate([halo, cur], axis=0)
        pooled = _pooled(ext, cur, i * tm, tm)
        for g in range(4):
            sl = slice(g * GROUP, (g + 1) * GROUP)
            mixed = _dot(pooled[g].astype(BF16), wp_ref[g])
            o_ref[:, sl] = (mixed * sc_ref[:, sl]).astype(BF16)

    return pl.pallas_call(
        body, name="pool_fwd", grid=(s // tm,),
        in_specs=[pl.BlockSpec((tm, 512), lambda i: (i, 0)),
                  pl.BlockSpec((HALO, 512), lambda i: (jnp.maximum(i * hb - 1, 0), 0)),
                  pl.BlockSpec((4, GROUP, GROUP), lambda i: (0, 0, 0)),
                  pl.BlockSpec((1, 512), lambda i: (0, 0))],
        out_specs=pl.BlockSpec((tm, 512), lambda i: (i, 0)),
        out_shape=jax.ShapeDtypeStruct((s, 512), BF16),
        compiler_params=_cp(1))(u, u, w_pool, pool_scale)


def _bias_table(bucket, rel_bias):
    def body(b_ref, rb_ref, o_ref):
        bucket_v = b_ref[...]
        key = lax.broadcasted_iota(jnp.int32, (2 * BLK, BLK), 0)
        dist = lax.broadcasted_iota(jnp.int32, (2 * BLK, BLK), 1) + BLK - key
        inwin = (dist >= 0) & (dist < BLK)
        for h in range(NQ):
            acc = jnp.zeros((2 * BLK, BLK), F32)
            for b in range(NBUCKET):
                acc = jnp.where(bucket_v == b, rb_ref[b, h], acc)
            rest = jnp.where(inwin, acc, NEG)
            o_ref[1, h] = rest
            o_ref[0, h] = jnp.where(key >= BLK, rest, NEG)

    return pl.pallas_call(
        body, name="bias_table",
        in_specs=[pl.BlockSpec(memory_space=pltpu.VMEM), pl.BlockSpec(memory_space=pltpu.SMEM)],
        out_specs=pl.BlockSpec(memory_space=pltpu.VMEM),
        out_shape=jax.ShapeDtypeStruct((2, NQ, 2 * BLK, BLK), F32),
        compiler_params=_cp())(bucket, rel_bias)


def _kv_band(ref, n, transposed):
    pstart = pl.multiple_of(jnp.maximum(n - 1, 0) * BLK, BLK)
    cstart = pl.multiple_of(n * BLK, BLK)
    lo = lax.broadcasted_iota(jnp.int32, (2 * BLK, 128), 1) < 64
    band = jnp.concatenate([ref[pl.ds(pstart, BLK), :], ref[pl.ds(cstart, BLK), :]], axis=0).astype(F32)
    rot = pltpu.roll(band, 64, axis=1)
    dup = (jnp.where(lo, band, rot), jnp.where(lo, rot, band))
    plain = [d.astype(BF16) for d in dup]
    if not transposed:
        return plain, None, (lo, pstart, cstart)
    row_lo = lax.broadcasted_iota(jnp.int32, (128, 2 * BLK), 0) < 64
    tr = [[jnp.where(row_lo, d.T, 0.0).astype(BF16), jnp.where(row_lo, 0.0, d.T).astype(BF16)] for d in dup]
    return plain, tr, (lo, pstart, cstart)


def _attn_probs(qm, kk, bias, sink):
    s = _dot_nt(kk, qm) + bias
    m = jnp.maximum(jnp.max(s, axis=0, keepdims=True), sink)
    p = jnp.exp(s - m)
    es = jnp.exp(sink - m)
    inv = 1.0 / (jnp.sum(p, axis=0, keepdims=True) + es)
    return p * inv, es * inv


def _attn_fwd(q, k, v, bias, sinks):
    s = q.shape[0]

    def body(q_ref, k_ref, v_ref, b_ref, sk_ref, o_ref):
        n = pl.program_id(0)
        kk, _, _ = _kv_band(k_ref, n, False)
        _, vt, _ = _kv_band(v_ref, n, True)
        bidx = jnp.minimum(n, 1)
        lo_q = lax.broadcasted_iota(jnp.int32, (BLK, 128), 1) < 64
        for p in range(4):
            h = p // 2
            qt = q_ref[:, p * 128:(p + 1) * 128].astype(F32) * SCALE
            acc_t = jnp.zeros((128, BLK), F32)
            for e in range(2):
                head = 2 * p + e
                qm = jnp.where(lo_q if e == 0 else jnp.logical_not(lo_q), qt, 0.0).astype(BF16)
                prob, _ = _attn_probs(qm, kk[h], b_ref[bidx, head], sk_ref[0, head])
                acc_t = acc_t + _dot(vt[h][e], prob.astype(BF16))
            o_ref[:, p * 128:(p + 1) * 128] = acc_t.T.astype(BF16)

    return pl.pallas_call(
        body, name="attn_fwd", grid=(s // BLK,),
        in_specs=[pl.BlockSpec((BLK, 512), lambda i: (i, 0)),
                  pl.BlockSpec((s, 128), lambda i: (0, 0)),
                  pl.BlockSpec((s, 128), lambda i: (0, 0)),
                  pl.BlockSpec((2, NQ, 2 * BLK, BLK), lambda i: (0, 0, 0, 0)),
                  pl.BlockSpec(memory_space=pltpu.SMEM)],
        out_specs=pl.BlockSpec((BLK, 512), lambda i: (i, 0)),
        out_shape=jax.ShapeDtypeStruct((s, 512), BF16),
        compiler_params=_cp(1))(q, k, v, bias, sinks)


def _outproj_fwd(pool_o, attn_o, w_out, x, g2, g3):
    s = x.shape[0]
    tm = min(TM, s)

    def body(p_ref, a_ref, w_ref, x_ref, g2_ref, g3_ref, mix_ref, x1_ref, h2_ref):
        mix = _dot(p_ref[...], w_ref[0:512, :]) + _dot(a_ref[...], w_ref[512:1024, :])
        mix_ref[...] = mix
        _, n2 = _rms(mix)
        x1 = x_ref[...] + n2 * g2_ref[...]
        x1_ref[...] = x1
        _, n3 = _rms(x1)
        h2_ref[...] = (n3 * g3_ref[...]).astype(BF16)

    row = lambda w: pl.BlockSpec((tm, w), lambda i: (i, 0))
    vec = pl.BlockSpec((1, D), lambda i: (0, 0))
    return pl.pallas_call(
        body, name="outproj_fwd", grid=(s // tm,),
        in_specs=[row(512), row(512), pl.BlockSpec((D, D), lambda i: (0, 0)), row(D), vec, vec],
        out_specs=[row(D), row(D), row(D)],
        out_shape=[jax.ShapeDtypeStruct((s, D), F32), jax.ShapeDtypeStruct((s, D), F32),
                   jax.ShapeDtypeStruct((s, D), BF16)],
        compiler_params=_cp(1))(pool_o, attn_o, w_out, x, g2, g3)


def _silu_parts(g):
    sg = jax.nn.sigmoid(g)
    return sg, g * sg


def _ffn_fwd(h2, wg, wu, wd, x1, target, g4):
    s = h2.shape[0]
    tm = min(TM_FFN, s)

    def body(h_ref, wg_ref, wu_ref, wd_ref, x1_ref, t_ref, g4_ref,
             gate_ref, up_ref, df_ref, dy_ref, loss_ref, gg4_ref, f_acc):
        i = pl.program_id(0)
        j = pl.program_id(1)
        h = h_ref[...]
        gate = _dot_nt(h, wg_ref[...])
        up = _dot_nt(h, wu_ref[...])
        gate_ref[...] = gate.astype(BF16)
        up_ref[...] = up.astype(BF16)
        _, sl = _silu_parts(gate)
        contrib = _dot((sl * up).astype(BF16), wd_ref[...])

        @pl.when(j == 0)
        def _():
            f_acc[...] = contrib

        @pl.when(j > 0)
        def _():
            f_acc[...] += contrib

        @pl.when((i == 0) & (j == 0))
        def _():
            loss_ref[...] = jnp.zeros_like(loss_ref)
            gg4_ref[...] = jnp.zeros_like(gg4_ref)

        @pl.when(j == NSH - 1)
        def _():
            r4, n4 = _rms(f_acc[...])
            g4v = g4_ref[...]
            err = x1_ref[...] + n4 * g4v - t_ref[...]
            loss_ref[...] += (0.5 / D) * jnp.sum(err * err).reshape(1, 1)
            dy = err * (1.0 / D)
            dy_ref[...] = dy
            df, dg = _rms_bwd(r4, n4, g4v, dy)
            gg4_ref[...] += dg
            df_ref[...] = df.astype(BF16)

    row = lambda w: pl.BlockSpec((tm, w), lambda i, j: (i, 0))
    sh = lambda r, c: pl.BlockSpec((None, r, c), lambda i, j: (j, 0, 0))
    act = pl.BlockSpec((None, tm, FS), lambda i, j: (j, i, 0))
    vec = pl.BlockSpec((1, D), lambda i, j: (0, 0))
    return pl.pallas_call(
        body, name="ffn_fwd", grid=(s // tm, NSH),
        in_specs=[row(D), sh(FS, D), sh(FS, D), sh(FS, D), row(D), row(D), vec],
        out_specs=[act, act, row(D), row(D), pl.BlockSpec((1, 1), lambda i, j: (0, 0)), vec],
        out_shape=[jax.ShapeDtypeStruct((NSH, s, FS), BF16), jax.ShapeDtypeStruct((NSH, s, FS), BF16),
                   jax.ShapeDtypeStruct((s, D), BF16), jax.ShapeDtypeStruct((s, D), F32),
                   jax.ShapeDtypeStruct((1, 1), F32), jax.ShapeDtypeStruct((1, D), F32)],
        scratch_shapes=[pltpu.VMEM((tm, D), F32)],
        compiler_params=_cp(2))(h2, wg, wu, wd, x1, target, g4)


def _ffn_bwd_act(df, gate, up, wg, wu, wd, dy, x1, mix, g3, g2):
    s = df.shape[0]
    tm = min(TM_FFN, s)

    def body(df_ref, gate_ref, up_ref, wg_ref, wu_ref, wd_ref, dy_ref, x1_ref, mix_ref, g3_ref, g2_ref,
             dgate_ref, dup_ref, dx1_ref, dmix_ref, gg3_ref, gg2_ref, acc):
        i = pl.program_id(0)
        j = pl.program_id(1)
        da = _dot_nt(df_ref[...], wd_ref[...])
        g = gate_ref[...].astype(F32)
        u = up_ref[...].astype(F32)
        sg, sl = _silu_parts(g)
        dgate = (da * u * (sg * (1.0 + g * (1.0 - sg)))).astype(BF16)
        dup = (da * sl).astype(BF16)
        dgate_ref[...] = dgate
        dup_ref[...] = dup
        contrib = _dot(dgate, wg_ref[...]) + _dot(dup, wu_ref[...])

        @pl.when(j == 0)
        def _():
            acc[...] = contrib

        @pl.when(j > 0)
        def _():
            acc[...] += contrib

        @pl.when((i == 0) & (j == 0))
        def _():
            gg3_ref[...] = jnp.zeros_like(gg3_ref)
            gg2_ref[...] = jnp.zeros_like(gg2_ref)

        @pl.when(j == NSH - 1)
        def _():
            r3, n3 = _rms(x1_ref[...])
            dx1n, dg3 = _rms_bwd(r3, n3, g3_ref[...], acc[...])
            dx1 = dy_ref[...] + dx1n
            dx1_ref[...] = dx1
            gg3_ref[...] += dg3
            r2, n2 = _rms(mix_ref[...])
            dmix, dg2 = _rms_bwd(r2, n2, g2_ref[...], dx1)
            gg2_ref[...] += dg2
            dmix_ref[...] = dmix.astype(BF16)

    row = lambda w: pl.BlockSpec((tm, w), lambda i, j: (i, 0))
    sh = lambda r, c: pl.BlockSpec((None, r, c), lambda i, j: (j, 0, 0))
    act = pl.BlockSpec((None, tm, FS), lambda i, j: (j, i, 0))
    vec = pl.BlockSpec((1, D), lambda i, j: (0, 0))
    return pl.pallas_call(
        body, name="ffn_bwd_act", grid=(s // tm, NSH),
        in_specs=[row(D), act, act, sh(FS, D), sh(FS, D), sh(FS, D), row(D), row(D), row(D), vec, vec],
        out_specs=[act, act, row(D), row(D), vec, vec],
        out_shape=[jax.ShapeDtypeStruct((NSH, s, FS), BF16), jax.ShapeDtypeStruct((NSH, s, FS), BF16),
                   jax.ShapeDtypeStruct((s, D), F32), jax.ShapeDtypeStruct((s, D), BF16),
                   jax.ShapeDtypeStruct((1, D), F32), jax.ShapeDtypeStruct((1, D), F32)],
        scratch_shapes=[pltpu.VMEM((tm, D), F32)],
        compiler_params=_cp(2))(df, gate, up, wg, wu, wd, dy, x1, mix, g3, g2)


SMEM_SPEC = pl.BlockSpec(memory_space=pltpu.SMEM)


def _emit_halves(acc_ref, row0, rows, c, own_ref, oth_ref):
    half = rows // 2
    own_ref[...] = acc_ref[pl.ds(row0 + pl.multiple_of(c * half, 8), half), :]
    oth_ref[...] = acc_ref[pl.ds(row0 + pl.multiple_of((1 - c) * half, 8), half), :].astype(BF16)


def _half_shapes(rows):
    return [jax.ShapeDtypeStruct((NSH, rows // 2, D), F32), jax.ShapeDtypeStruct((NSH, rows // 2, D), BF16)]


def _ffn_bwd_w(h2, gate, up, dgate, dup, df, c_arr):
    s = h2.shape[0]
    tm = min(TM_FFN, s)
    nt = s // tm

    def body(c_ref, h_ref, gate_ref, up_ref, dgate_ref, dup_ref, df_ref, *rest):
        outs, accs = rest[:6], rest[6:]
        i = pl.program_id(1)
        @pl.when(i == 0)
        def _():
            for acc in accs:
                acc[...] = jnp.zeros_like(acc)

        h = h_ref[...]
        accs[0][...] += _dot_tn(dgate_ref[...], h)
        accs[1][...] += _dot_tn(dup_ref[...], h)
        _, sl = _silu_parts(gate_ref[...].astype(F32))
        a = (sl * up_ref[...].astype(F32)).astype(BF16)
        accs[2][...] += _dot_tn(a, df_ref[...])

        @pl.when(i == nt - 1)
        def _():
            for t, acc in enumerate(accs):
                _emit_halves(acc, 0, FS, c_ref[0], outs[2 * t], outs[2 * t + 1])

    row = lambda w: pl.BlockSpec((tm, w), lambda j, i: (i, 0))
    act = pl.BlockSpec((None, tm, FS), lambda j, i: (j, i, 0))
    half = pl.BlockSpec((None, FS // 2, D), lambda j, i: (j, 0, 0))
    return pl.pallas_call(
        body, name="ffn_bwd_w", grid=(NSH, nt),
        in_specs=[SMEM_SPEC, row(D), act, act, act, act, row(D)],
        out_specs=[half] * 6,
        out_shape=_half_shapes(FS) * 3,
        scratch_shapes=[pltpu.VMEM((FS, D), F32)] * 3,
        compiler_params=_cp(2))(c_arr, h2, gate, up, dgate, dup, df)


def _outproj_bwd(dmix, w_out, pool_o, attn_o, c_arr, dep=None):
    s = dmix.shape[0]
    tm = min(TM, s)
    nt = s // tm

    def body(c_ref, dm_ref, w_ref, p_ref, a_ref, *rest):
        dpool_ref, dattn_ref, own_ref, oth_ref, gw_ref = rest[-5:]
        i = pl.program_id(0)

        @pl.when(i == 0)
        def _():
            gw_ref[...] = jnp.zeros_like(gw_ref)

        dm = dm_ref[...]
        dpool_ref[...] = _dot_nt(dm, w_ref[0:512, :])
        dattn_ref[...] = _dot_nt(dm, w_ref[512:1024, :]).astype(BF16)
        gw_ref[0:512, :] += _dot_tn(p_ref[...], dm)
        gw_ref[512:1024, :] += _dot_tn(a_ref[...], dm)

        @pl.when(i == nt - 1)
        def _():
            for k in range(NSH):
                _emit_halves(gw_ref, k * WOUT_S, WOUT_S, c_ref[0], own_ref.at[k], oth_ref.at[k])

    row = lambda w: pl.BlockSpec((tm, w), lambda i: (i, 0))
    full = pl.BlockSpec((D, D), lambda i: (0, 0))
    half = pl.BlockSpec((NSH, WOUT_S // 2, D), lambda i: (0, 0, 0))
    return pl.pallas_call(
        body, name="outproj_bwd", grid=(nt,),
        in_specs=[SMEM_SPEC, row(D), full, row(512), row(512)] + ([] if dep is None else [ANY]),
        out_specs=[row(512), row(512), half, half],
        out_shape=[jax.ShapeDtypeStruct((s, 512), F32), jax.ShapeDtypeStruct((s, 512), BF16)] + _half_shapes(WOUT_S),
        scratch_shapes=[pltpu.VMEM((D, D), F32)],
        compiler_params=_cp(1))(c_arr, dmix, w_out, pool_o, attn_o, *([] if dep is None else [dep]))


def _pool_bwd(u, dpool, w_pool, pool_scale, dep=None):
    s = u.shape[0]
    tm = min(TM_POOL, s)
    hb = tm // HALO
    nt = s // tm
    last_halo = s // HALO - 1

    def body(uc_ref, uh_ref, dc_ref, dn_ref, wp_ref, sc_ref, *rest):
        du_ref, gwp_ref, gsc_ref = rest[-3:]
        i = pl.program_id(0)

        @pl.when(i == 0)
        def _():
            gwp_ref[...] = jnp.zeros_like(gwp_ref)
            gsc_ref[...] = jnp.zeros_like(gsc_ref)

        cur = uc_ref[...]
        halo = jnp.where(i > 0, uh_ref[...], 0.0)
        pooled = _pooled(jnp.concatenate([halo, cur], axis=0), cur, i * tm, tm)
        dcur = dc_ref[...]
        dnext = jnp.where(i < nt - 1, dn_ref[...], 0.0)
        dext = jnp.concatenate([dcur, dnext], axis=0)
        rows = lax.broadcasted_iota(jnp.int32, (tm, tm + HALO), 0)
        cols = lax.broadcasted_iota(jnp.int32, (tm, tm + HALO), 1)
        d = cols - rows
        t_ext = i * tm + lax.broadcasted_iota(jnp.int32, (tm + HALO, GROUP), 0)
        for g, w in enumerate(WINDOWS):
            sl = slice(g * GROUP, (g + 1) * GROUP)
            pb = pooled[g].astype(BF16)
            wp = wp_ref[g]
            mixed = _dot(pb, wp)
            gsc_ref[:, sl] += jnp.sum(dcur[:, sl] * mixed, axis=0, keepdims=True)
            dmixed = (dext[:, sl] * sc_ref[:, sl]).astype(BF16)
            gwp_ref[g] += _dot_tn(pb, dmixed[0:tm])
            dpooled = _dot_nt(dmixed, wp)
            z = dpooled / jnp.minimum(t_ext + 1, w).astype(F32)
            b = jnp.where((d >= 0) & (d < w), 1.0, 0.0).astype(BF16)
            du_ref[:, sl] = (_band_dot(b, z, 2) - dpooled[0:tm]).astype(BF16)

    return pl.pallas_call(
        body, name="pool_bwd", grid=(nt,),
        in_specs=[pl.BlockSpec((tm, 512), lambda i: (i, 0)),
                  pl.BlockSpec((HALO, 512), lambda i: (jnp.maximum(i * hb - 1, 0), 0)),
                  pl.BlockSpec((tm, 512), lambda i: (i, 0)),
                  pl.BlockSpec((HALO, 512), lambda i: (jnp.minimum((i + 1) * hb, last_halo), 0)),
                  pl.BlockSpec((4, GROUP, GROUP), lambda i: (0, 0, 0)),
                  pl.BlockSpec((1, 512), lambda i: (0, 0))] + ([] if dep is None else [ANY]),
        out_specs=[pl.BlockSpec((tm, 512), lambda i: (i, 0)),
                   pl.BlockSpec((4, GROUP, GROUP), lambda i: (0, 0, 0)),
                   pl.BlockSpec((1, 512), lambda i: (0, 0))],
        out_shape=[jax.ShapeDtypeStruct((s, 512), BF16), jax.ShapeDtypeStruct((4, GROUP, GROUP), F32),
                   jax.ShapeDtypeStruct((1, 512), F32)],
        compiler_params=_cp(1))(u, u, dpool, dpool, w_pool, pool_scale, *([] if dep is None else [dep]))


def _attn_bwd(q, k, v, do, bias, sinks, dep=None):
    s = q.shape[0]

    def body(q_ref, do_ref, k_ref, v_ref, b_ref, sk_ref, *rest):
        dq_ref, dk_ref, dv_ref, dss_ref, gsk_ref = rest[-5:]
        n = pl.program_id(0)

        @pl.when(n == 0)
        def _():
            dk_ref[...] = jnp.zeros_like(dk_ref)
            dv_ref[...] = jnp.zeros_like(dv_ref)
            dss_ref[...] = jnp.zeros_like(dss_ref)
            gsk_ref[...] = jnp.zeros_like(gsk_ref)

        kk, kt, (lo, pstart, cstart) = _kv_band(k_ref, n, True)
        vv, _, _ = _kv_band(v_ref, n, False)
        bidx = jnp.minimum(n, 1)
        lo_q = lax.broadcasted_iota(jnp.int32, (BLK, 128), 1) < 64
        dkk = [jnp.zeros((2 * BLK, 128), F32), jnp.zeros((2 * BLK, 128), F32)]
        dvv = [jnp.zeros((2 * BLK, 128), F32), jnp.zeros((2 * BLK, 128), F32)]
        for p in range(4):
            h = p // 2
            qt = q_ref[:, p * 128:(p + 1) * 128].astype(F32) * SCALE
            dot = do_ref[:, p * 128:(p + 1) * 128]
            dq_t = jnp.zeros((128, BLK), F32)
            for e in range(2):
                head = 2 * p + e
                sel_q = lo_q if e == 0 else jnp.logical_not(lo_q)
                qm = jnp.where(sel_q, qt, 0.0).astype(BF16)
                prob, ps = _attn_probs(qm, kk[h], b_ref[bidx, head], sk_ref[0, head])
                dom = jnp.where(sel_q, dot, jnp.zeros_like(dot))
                dp = _dot_nt(vv[h], dom)
                delta = jnp.sum(prob * dp, axis=0, keepdims=True)
                ds = prob * (dp - delta)
                gsk_ref[pl.ds(head, 1), :] += jnp.broadcast_to(-jnp.sum(ps * delta).reshape(1, 1), (1, 128))
                dss_ref[head] += ds
                dsb = ds.astype(BF16)
                dq_t = dq_t + _dot(kt[h][e], dsb)
                dkk[h] = dkk[h] + _dot(dsb, qm)
                dvv[h] = dvv[h] + _dot(prob.astype(BF16), dom)
            dq_ref[:, p * 128:(p + 1) * 128] = (dq_t.T * SCALE).astype(BF16)
        for acc, ref in ((dkk, dk_ref), (dvv, dv_ref)):
            f0 = acc[0] + pltpu.roll(acc[0], 64, axis=1)
            f1 = acc[1] + pltpu.roll(acc[1], 64, axis=1)
            band = jnp.where(lo, f0, f1)
            ref[pl.ds(pstart, BLK), :] += band[0:BLK]
            ref[pl.ds(cstart, BLK), :] += band[BLK:2 * BLK]

    blk = pl.BlockSpec((BLK, 512), lambda i: (i, 0))
    kv = pl.BlockSpec((s, 128), lambda i: (0, 0))
    return pl.pallas_call(
        body, name="attn_bwd", grid=(s // BLK,),
        in_specs=[blk, blk, kv, kv, pl.BlockSpec((2, NQ, 2 * BLK, BLK), lambda i: (0, 0, 0, 0)),
                  pl.BlockSpec(memory_space=pltpu.SMEM)] + ([] if dep is None else [ANY]),
        out_specs=[blk, kv, kv, pl.BlockSpec((NQ, 2 * BLK, BLK), lambda i: (0, 0, 0)),
                   pl.BlockSpec((NQ, 128), lambda i: (0, 0))],
        out_shape=[jax.ShapeDtypeStruct((s, 512), BF16), jax.ShapeDtypeStruct((s, 128), F32),
                   jax.ShapeDtypeStruct((s, 128), F32), jax.ShapeDtypeStruct((NQ, 2 * BLK, BLK), F32),
                   jax.ShapeDtypeStruct((NQ, 128), F32)],
        compiler_params=_cp(1))(q, do, k, v, bias, sinks, *([] if dep is None else [dep]))


def _relbias_grad(dss, bucket):
    def body(ds_ref, b_ref, o_ref):
        bucket_v = b_ref[...]
        lane = lax.broadcasted_iota(jnp.int32, (NQ, 128), 1)
        head_row = lax.broadcasted_iota(jnp.int32, (NQ, BLK), 0)
        acc = jnp.zeros((NQ, 128), F32)
        for b in range(NBUCKET):
            sel = bucket_v == b
            stack = jnp.zeros((NQ, BLK), F32)
            for h in range(NQ):
                col_sums = jnp.sum(jnp.where(sel, ds_ref[h], 0.0), axis=0, keepdims=True)
                stack = jnp.where(head_row == h, col_sums, stack)
            acc = acc + jnp.where(lane == b, jnp.sum(stack, axis=1, keepdims=True), 0.0)
        o_ref[...] = acc

    return pl.pallas_call(
        body, name="relbias_grad",
        in_specs=[pl.BlockSpec(memory_space=pltpu.VMEM), pl.BlockSpec(memory_space=pltpu.VMEM)],
        out_specs=pl.BlockSpec(memory_space=pltpu.VMEM),
        out_shape=jax.ShapeDtypeStruct((NQ, 128), F32),
        compiler_params=_cp())(dss, bucket)


def _inproj_bwd(x, g1, du, dq, dk, dv, w_in, dx1, c_arr):
    s = x.shape[0]
    tm = min(TM, s)
    nt = s // tm
    cols = ((0, 512), (512, 1024), (1024, 1152), (1152, 1280))

    def body(c_ref, x_ref, g_ref, du_ref, dq_ref, dk_ref, dv_ref, w_ref, dx1_ref, gx_ref, own_ref, oth_ref, gg_ref,
             gw_ref):
        i = pl.program_id(0)

        @pl.when(i == 0)
        def _():
            gw_ref[...] = jnp.zeros_like(gw_ref)
            gg_ref[...] = jnp.zeros_like(gg_ref)

        gv = g_ref[...]
        r1, n1 = _rms(x_ref[...])
        h = (n1 * gv).astype(BF16)
        parts = (du_ref[...], dq_ref[...], dk_ref[...].astype(BF16), dv_ref[...].astype(BF16))
        dh = None
        for (a, b), dpart in zip(cols, parts):
            t = _dot(dpart, w_ref[a:b, :])
            dh = t if dh is None else dh + t
            gw_ref[a:b, :] += _dot_tn(dpart, h)
        dx, dg = _rms_bwd(r1, n1, gv, dh)
        gg_ref[...] += dg
        gx_ref[...] = dx1_ref[...] + dx

        @pl.when(i == nt - 1)
        def _():
            for k in range(NSH):
                _emit_halves(gw_ref, k * WIN_S, WIN_S, c_ref[0], own_ref.at[k], oth_ref.at[k])

    row = lambda w: pl.BlockSpec((tm, w), lambda i: (i, 0))
    vec = pl.BlockSpec((1, D), lambda i: (0, 0))
    full = pl.BlockSpec((IN_W, D), lambda i: (0, 0))
    half = pl.BlockSpec((NSH, WIN_S // 2, D), lambda i: (0, 0, 0))
    return pl.pallas_call(
        body, name="inproj_bwd", grid=(nt,),
        in_specs=[SMEM_SPEC, row(D), vec, row(512), row(512), row(128), row(128), full, row(D)],
        out_specs=[row(D), half, half, vec],
        out_shape=[jax.ShapeDtypeStruct((s, D), F32)] + _half_shapes(WIN_S) + [jax.ShapeDtypeStruct((1, D), F32)],
        scratch_shapes=[pltpu.VMEM((IN_W, D), F32)],
        compiler_params=_cp(1))(c_arr, x, g1, du, dq, dk, dv, w_in, dx1)


def _local_step(x, target, small, w_in, w_out, ffn_weights, c_arr, on_ffn_grads=None, on_wout_grads=None):
    g1, g2, g3, g4, w_pool, pool_scale, rel_bias, sinks = small
    bucket = jnp.asarray(_bucket_table())
    wp_bf = w_pool.astype(BF16)
    u, q, k, v = _inproj_fwd(x, g1, w_in)
    pool_o = _pool_fwd(u, wp_bf, pool_scale)
    bias = _bias_table(bucket, rel_bias)
    attn_o = _attn_fwd(q, k, v, bias, sinks)
    wg, wu, wd = ffn_weights(pool_o, attn_o) if callable(ffn_weights) else ffn_weights
    mix, x1, h2 = _outproj_fwd(pool_o, attn_o, w_out, x, g2, g3)
    gate, up, df, dy, loss, gg4 = _ffn_fwd(h2, wg, wu, wd, x1, target, g4)
    dgate, dup, dx1, dmix, gg3, gg2 = _ffn_bwd_act(df, gate, up, wg, wu, wd, dy, x1, mix, g3, g2)
    ffn_g = _ffn_bwd_w(h2, gate, up, dgate, dup, df, c_arr)
    dep = None if on_ffn_grads is None else on_ffn_grads(ffn_g)
    dpool, dattn, wout_own, wout_oth = _outproj_bwd(dmix, w_out, pool_o, attn_o, c_arr, dep)
    dep = None if on_wout_grads is None else on_wout_grads(wout_own, wout_oth, dpool)
    du, gwp, gsc = _pool_bwd(u, dpool, wp_bf, pool_scale, dep)
    dq, dk, dv, dss, gsk = _attn_bwd(q, k, v, dattn, bias, sinks, dep)
    grb = _relbias_grad(dss, bucket)
    gx, win_own, win_oth, gg1 = _inproj_bwd(x, g1, du, dq, dk, dv, w_in, dx1, c_arr)
    small_grads = (gg1, gg2, gg3, gg4, gwp, gsc, grb[:, :NBUCKET].T, gsk[:, 0].reshape(1, NQ))
    return loss, gx, small_grads, ((win_own, win_oth), (wout_own, wout_oth), ffn_g)


def _place():
    x, y, c = lax.axis_index("x"), lax.axis_index("y"), lax.axis_index("c")
    chips = ((1 - x, y), (x, 1 - y), (1 - x, 1 - y))
    return x, y, c, chips


def _remote(src, dst, ssem, rsem, dev):
    return pltpu.make_async_remote_copy(src_ref=src, dst_ref=dst, send_sem=ssem, recv_sem=rsem,
                                        device_id=dev, device_id_type=MESH_T)


def _own_slot(shard):
    me = 2 * lax.axis_index("x") + lax.axis_index("y")
    return lax.dynamic_update_slice(lax.empty((NSH,) + shard.shape, shard.dtype), shard[None], (me, 0, 0))


def _all_gather_weights(shards, lands):
    nt = len(shards)

    def body(*refs):
        srcs, dsts = refs[:nt], refs[2 * nt:3 * nt]
        isend, irecv, dsend, drecv = refs[3 * nt:]
        x, y, c, chips = _place()
        me = 2 * x + y
        sib = (x, y, 1 - c)
        sends = []
        for t in range(nt):
            half = srcs[t].shape[0] // 2
            mine = pl.ds(pl.multiple_of(c * half, 16), half)
            for j, (px, py) in enumerate(chips):
                cp = _remote(srcs[t].at[mine], dsts[t].at[me, mine], isend.at[t, j], irecv.at[t, j], (px, py, c))
                cp.start()
                sends.append(cp)
        for t in range(nt):
            half = srcs[t].shape[0] // 2
            mine = pl.ds(pl.multiple_of(c * half, 16), half)
            for j, (px, py) in enumerate(chips):
                blk = dsts[t].at[2 * px + py, mine]
                _remote(blk, blk, isend.at[t, j], irecv.at[t, j], (px, py, c)).wait_recv()
                cp = _remote(blk, blk, dsend.at[t, j], drecv.at[t, j], sib)
                cp.start()
                sends.append(cp)
        for t in range(nt):
            half = srcs[t].shape[0] // 2
            other = pl.ds(pl.multiple_of((1 - c) * half, 16), half)
            for j, (px, py) in enumerate(chips):
                blk = dsts[t].at[2 * px + py, other]
                _remote(blk, blk, dsend.at[t, j], drecv.at[t, j], sib).wait_recv()
        for cp in sends:
            cp.wait_send()

    return pl.pallas_call(
        body, name="allgather_weights",
        in_specs=[ANY] * (2 * nt), out_specs=[ANY] * nt,
        out_shape=[jax.ShapeDtypeStruct(a.shape, a.dtype) for a in lands],
        input_output_aliases={nt + t: t for t in range(nt)},
        scratch_shapes=[pltpu.SemaphoreType.DMA((nt, 3)), pltpu.SemaphoreType.DMA((nt, 3)),
                        pltpu.SemaphoreType.DMA((nt, 3)), pltpu.SemaphoreType.DMA((nt, 3))],
        compiler_params=_cp())(*shards, *lands)


def _to_sibling(arrs, name):
    nt = len(arrs)

    def body(*refs):
        srcs, dsts = refs[:nt], refs[nt:2 * nt]
        ssem, rsem = refs[2 * nt:]
        x, y, c, _ = _place()
        cps = [_remote(srcs[t], dsts[t], ssem.at[t], rsem.at[t], (x, y, 1 - c)) for t in range(nt)]
        for cp in cps:
            cp.start()
        for cp in cps:
            cp.wait()

    return pl.pallas_call(
        body, name=name, in_specs=[ANY] * nt, out_specs=[ANY] * nt,
        out_shape=[jax.ShapeDtypeStruct(a.shape, a.dtype) for a in arrs],
        scratch_shapes=[pltpu.SemaphoreType.DMA((nt,)), pltpu.SemaphoreType.DMA((nt,))],
        compiler_params=_cp())(*arrs)


def _scatter_to_chips(arrs):
    nt = len(arrs)

    def body(*refs):
        srcs, dsts = refs[:nt], refs[nt:2 * nt]
        ssem, rsem = refs[2 * nt:]
        x, y, c, chips = _place()
        cps = []
        for t in range(nt):
            for j, (px, py) in enumerate(chips):
                cps.append(_remote(srcs[t].at[2 * px + py], dsts[t].at[j], ssem.at[t, j], rsem.at[t, j], (px, py, c)))
        for cp in cps:
            cp.start()
        for cp in cps:
            cp.wait()

    return pl.pallas_call(
        body, name="scatter_to_chips", in_specs=[ANY] * nt, out_specs=[ANY] * nt,
        out_shape=[jax.ShapeDtypeStruct((3,) + a.shape[1:], a.dtype) for a in arrs],
        scratch_shapes=[pltpu.SemaphoreType.DMA((nt, 3)), pltpu.SemaphoreType.DMA((nt, 3))],
        compiler_params=_cp())(*arrs)


HBM_SPEC = pl.BlockSpec(memory_space=pltpu.HBM)
SEM_SPEC = pl.BlockSpec(memory_space=pltpu.SEMAPHORE)
DATAFLOW = pltpu.SideEffectType.DATAFLOW_SIDE_EFFECTING


def _gather_copies(srcs, lands, ssem, rsem):
    x, y, c, chips = _place()
    me = 2 * x + y
    out = []
    for t in range(len(srcs)):
        half = srcs[t].shape[0] // 2
        mine = pl.ds(pl.multiple_of(c * half, 16), half)
        for j, (px, py) in enumerate(chips):
            k = 3 * t + j
            send = _remote(srcs[t].at[mine], lands[t].at[me, mine], ssem.at[k], rsem.at[k], (px, py, c))
            blk = lands[t].at[2 * px + py, mine]
            out.append((send, _remote(blk, blk, ssem.at[k], rsem.at[k], (px, py, c))))
    return out


def _scatter_copies(srcs, lands, ssem, rsem):
    x, y, c, chips = _place()
    out = []
    for t in range(len(srcs)):
        for j, (px, py) in enumerate(chips):
            k = 3 * t + j
            send = _remote(srcs[t].at[2 * px + py], lands[t].at[j], ssem.at[k], rsem.at[k], (px, py, c))
            out.append((send, _remote(lands[t].at[j], lands[t].at[j], ssem.at[k], rsem.at[k], (px, py, c))))
    return out


def _sibling_copies(srcs, lands, ssem, rsem):
    x, y, c, _ = _place()
    sib = (x, y, 1 - c)
    return [(_remote(srcs[t], lands[t], ssem.at[t], rsem.at[t], sib),
             _remote(lands[t], lands[t], ssem.at[t], rsem.at[t], sib)) for t in range(len(srcs))]


def _peer_copies(srcs, lands, ssem, rsem):
    x, y, c, _ = _place()
    me = 4 * x + 2 * y + c
    out = []
    for kk in range(1, 8):
        px, py, pc = x ^ (kk >> 2), y ^ ((kk >> 1) & 1), c ^ (kk & 1)
        send = _remote(srcs[0], lands[0].at[me], ssem.at[kk - 1], rsem.at[kk - 1], (px, py, pc))
        slot = lands[0].at[4 * px + 2 * py + pc]
        out.append((send, _remote(slot, slot, ssem.at[kk - 1], rsem.at[kk - 1], (px, py, pc))))
    return out


def _split_start(plan, ncopy, srcs, lands, after, name):
    ns, nbuf = len(srcs), len(srcs) + len(lands)

    def body(*refs):
        ssem, rsem, token = refs[nbuf + 1], refs[nbuf + 2], refs[-1]
        for send, _ in plan(refs[:ns], refs[ns:nbuf], ssem, rsem):
            send.start()
        token[...] = jnp.zeros_like(token)

    bufs = [pltpu.with_memory_space_constraint(a, pltpu.HBM) for a in list(srcs) + list(lands)]
    outs = pl.pallas_call(
        body, name=name, in_specs=[HBM_SPEC] * nbuf + [ANY],
        out_specs=[SEM_SPEC, SEM_SPEC] + [HBM_SPEC] * nbuf + [pl.BlockSpec(memory_space=pltpu.VMEM)],
        out_shape=[pltpu.SemaphoreType.DMA((ncopy,)), pltpu.SemaphoreType.DMA((ncopy,))]
        + [pltpu.HBM(a.shape, a.dtype) for a in bufs] + [jax.ShapeDtypeStruct((8, 128), F32)],
        input_output_aliases={i: 2 + i for i in range(nbuf)},
        compiler_params=pltpu.CompilerParams(has_side_effects=DATAFLOW))(*bufs, after)
    return outs[0], outs[1], outs[2:2 + ns], outs[2 + ns:2 + nbuf], outs[-1]


def _split_wait(plan, ssem, rsem, srcs, lands, after, name):
    ns, nbuf = len(srcs), len(srcs) + len(lands)

    def body(*refs):
        s_ref, r_ref = refs[nbuf], refs[nbuf + 1]
        for send, recv in plan(refs[:ns], refs[ns:nbuf], s_ref, r_ref):
            send.wait_send()
            recv.wait_recv()

    outs = pl.pallas_call(
        body, name=name, in_specs=[HBM_SPEC] * nbuf + [SEM_SPEC, SEM_SPEC] + [ANY] * len(after),
        out_specs=[HBM_SPEC] * nbuf,
        out_shape=[pltpu.HBM(a.shape, a.dtype) for a in list(srcs) + list(lands)],
        input_output_aliases={i: i for i in range(nbuf)},
        compiler_params=pltpu.CompilerParams(has_side_effects=DATAFLOW))(*srcs, *lands, ssem, rsem, *after)
    return outs


def _gather_finish(lands):
    nt = len(lands)

    def body(*refs):
        outs = refs[nt:2 * nt]
        dsend, drecv = refs[2 * nt:]
        x, y, c, chips = _place()
        sib = (x, y, 1 - c)
        sends = []
        for t in range(nt):
            half = outs[t].shape[1] // 2
            mine = pl.ds(pl.multiple_of(c * half, 16), half)
            for j, (px, py) in enumerate(chips):
                blk = outs[t].at[2 * px + py, mine]
                cp = _remote(blk, blk, dsend.at[t, j], drecv.at[t, j], sib)
                cp.start()
                sends.append(cp)
        for t in range(nt):
            half = outs[t].shape[1] // 2
            other = pl.ds(pl.multiple_of((1 - c) * half, 16), half)
            for j, (px, py) in enumerate(chips):
                blk = outs[t].at[2 * px + py, other]
                _remote(blk, blk, dsend.at[t, j], drecv.at[t, j], sib).wait_recv()
        for cp in sends:
            cp.wait_send()

    return pl.pallas_call(
        body, name="gather_finish", in_specs=[ANY] * nt, out_specs=[ANY] * nt,
        out_shape=[jax.ShapeDtypeStruct(a.shape, a.dtype) for a in lands],
        input_output_aliases={t: t for t in range(nt)},
        scratch_shapes=[pltpu.SemaphoreType.DMA((nt, 3)), pltpu.SemaphoreType.DMA((nt, 3))],
        compiler_params=_cp())(*lands)


def _chip_partial(owns, recv, chip_arr, tag):
    nt = len(owns)

    def body(chip_ref, *refs):
        k = pl.program_id(0)
        for t in range(nt):
            p = refs[t][...] + refs[nt + t][...].astype(F32)
            refs[2 * nt + t][...] = p.astype(BF16)

            @pl.when(k == chip_ref[0])
            def _():
                refs[3 * nt + t][...] = p

    blk_specs = [pl.BlockSpec((None,) + a.shape[1:], lambda k, chip_ref: (k, 0, 0)) for a in owns]
    own_specs = [pl.BlockSpec(a.shape[1:], lambda k, chip_ref: (0, 0)) for a in owns]
    return pl.pallas_call(
        body, name="rs_chip_partial_" + tag,
        grid_spec=pltpu.PrefetchScalarGridSpec(num_scalar_prefetch=1, grid=(NSH,), in_specs=blk_specs * 2,
                                               out_specs=blk_specs + own_specs),
        out_shape=[jax.ShapeDtypeStruct(a.shape, BF16) for a in owns]
        + [jax.ShapeDtypeStruct(a.shape[1:], F32) for a in owns],
        compiler_params=_cp(1))(chip_arr, *owns, *recv)


def _sum_chips(own, recv, tag, after=()):
    nt = len(own)

    def body(*refs):
        outs = refs[2 * nt + len(after):]
        for t in range(nt):
            r = refs[nt + t]
            outs[t][...] = ((refs[t][...] + r[0].astype(F32)) + r[1].astype(F32)) + r[2].astype(F32)

    vm = pl.BlockSpec(memory_space=pltpu.VMEM)
    return pl.pallas_call(
        body, name="rs_sum_chips_" + tag, in_specs=[vm] * (2 * nt) + [ANY] * len(after), out_specs=[vm] * nt,
        out_shape=[jax.ShapeDtypeStruct(a.shape, F32) for a in own],
        compiler_params=_cp())(*own, *recv, *after)


def _adamw_math(w, g, m, v):
    m = ADAM_B1 * m + (1.0 - ADAM_B1) * g
    v = ADAM_B2 * v + (1.0 - ADAM_B2) * (g * g)
    m_hat = m / (1.0 - ADAM_B1 ** ADAM_STEP)
    v_hat = v / (1.0 - ADAM_B2 ** ADAM_STEP)
    delta = -ADAM_LR * (m_hat / (jnp.sqrt(v_hat) + ADAM_EPS) + ADAM_WD * w)
    return delta, m, v


ADAM_SPLIT = 4


def _adamw_shards(own, sib, ws, ms, vs, c_arr, tag):
    nt = len(own)

    def body(c_ref, *refs):
        hh = pl.program_id(0)
        mine = hh == c_ref[0]
        for t in range(nt):
            g = jnp.where(mine, refs[t][...], refs[nt + t][...])
            delta, m, v = _adamw_math(refs[2 * nt + t][...], g, refs[3 * nt + t][...], refs[4 * nt + t][...])
            refs[5 * nt + t][...] = g
            refs[6 * nt + t][...] = delta
            refs[7 * nt + t][...] = m
            refs[8 * nt + t][...] = v

    def tile(a):
        return (a.shape[0] // ADAM_SPLIT, a.shape[1])

    half_specs = [pl.BlockSpec(tile(a), lambda hh, q, c_ref: (q, 0)) for a in own]
    full_specs = [pl.BlockSpec(tile(a), lambda hh, q, c_ref: (hh * ADAM_SPLIT + q, 0)) for a in own]
    return pl.pallas_call(
        body, name="adamw_shards_" + tag,
        grid_spec=pltpu.PrefetchScalarGridSpec(num_scalar_prefetch=1, grid=(2, ADAM_SPLIT),
                                               in_specs=half_specs * 2 + full_specs * 3, out_specs=full_specs * 4),
        out_shape=[jax.ShapeDtypeStruct(w.shape, F32) for w in ws] * 4,
        compiler_params=_cp(2))(c_arr, *own, *sib, *ws, *ms, *vs)


def _small_sum_adamw(gathered, wp, mp, vp):
    def body(g_ref, w_ref, m_ref, v_ref, go_ref, d_ref, mo_ref, vo_ref):
        g = g_ref[0]
        for d in range(1, 8):
            g = g + g_ref[d]
        delta, m, v = _adamw_math(w_ref[...], g, m_ref[...], v_ref[...])
        go_ref[...] = g
        d_ref[...] = delta
        mo_ref[...] = m
        vo_ref[...] = v

    vm = pl.BlockSpec(memory_space=pltpu.VMEM)
    return pl.pallas_call(
        body, name="small_sum_adamw", in_specs=[vm] * 4, out_specs=[vm] * 4,
        out_shape=[jax.ShapeDtypeStruct(wp.shape, F32)] * 4,
        compiler_params=_cp())(gathered, wp, mp, vp)


SMALL_PARTS = (("g1", (1, D)), ("g2", (1, D)), ("g3", (1, D)), ("g4", (1, D)), ("w_pool", (1, 4, GROUP, GROUP)),
               ("pool_scale", (1, POOL_W)), ("rel_bias", (NBUCKET, NQ)), ("sinks", (1, NQ)), ("loss", (1, 1)))


def _pack_small(parts):
    rows = []
    for a in parts:
        flat = a.reshape(-1)
        n = flat.shape[0]
        padded = -(-n // 1024) * 1024
        rows.append(jnp.pad(flat, (0, padded - n)).reshape(-1, 128))
    return jnp.concatenate(rows, axis=0)


def _unpack_small(packed):
    out, r = [], 0
    for _, shape in SMALL_PARTS:
        n = int(np.prod(shape))
        nr = -(-n // 1024) * 8
        out.append(packed[r:r + nr].reshape(-1)[:n].reshape(shape))
        r += nr
    return out


def kernel(x, g_pre_mix, w_in, w_pool, pool_scale, rel_bias, sinks, w_out, g_post_mix, g_pre_ffn, w_gate, w_up, w_down, g_post_ffn, loss_target, m_g_pre_mix, m_w_in, m_w_pool, m_pool_scale, m_rel_bias, m_sinks, m_w_out, m_g_post_mix, m_g_pre_ffn, m_w_gate, m_w_up, m_w_down, m_g_post_ffn, v_g_pre_mix, v_w_in, v_w_pool, v_pool_scale, v_rel_bias, v_sinks, v_w_out, v_g_post_mix, v_g_pre_ffn, v_w_gate, v_w_up, v_w_down, v_g_post_ffn):
    c = lax.axis_index("c")
    chip = 2 * lax.axis_index("x") + lax.axis_index("y")

    def shards(a_in, a_out, a_gate, a_up, a_down):
        return (a_in[0].T, a_out[0], a_gate[0].T, a_up[0].T, a_down[0])

    c_arr = c.reshape(1).astype(jnp.int32)
    chip_arr = chip.reshape(1).astype(jnp.int32)

    big_w = shards(w_in, w_out, w_gate, w_up, w_down)
    big_bf = [w.astype(BF16) for w in big_w]
    win_all, wout_all = _all_gather_weights(big_bf[:2], [_own_slot(a) for a in big_bf[:2]])
    g_ssem, g_rsem, g_srcs, g_lands, g_token = _split_start(
        _gather_copies, 9, big_bf[2:], [_own_slot(a) for a in big_bf[2:]], win_all, "gather_ffn_start")

    def ffn_weights(*after):
        outs = _split_wait(_gather_copies, g_ssem, g_rsem, g_srcs, g_lands, after, "gather_ffn_wait")
        return _gather_finish(outs[3:])

    rs = {}

    def on_ffn_grads(ffn_g):
        oths = ffn_g[1::2]
        rs["ffn_own"] = ffn_g[0::2]
        rs["x"] = _split_start(_sibling_copies, 3, oths, [lax.empty(a.shape, BF16) for a in oths], ffn_g[0],
                               "rs_exchange_ffn_start")
        return rs["x"][4]

    def on_wout_grads(own, oth, after):
        ssem, rsem, srcs, lands, _ = rs["x"]
        recv_ffn = _split_wait(_sibling_copies, ssem, rsem, srcs, lands, [after], "rs_exchange_ffn_wait")[3:]
        recv_wout = _to_sibling([oth], "rs_exchange_wout")
        outs = _chip_partial([own] + list(rs["ffn_own"]), list(recv_wout) + list(recv_ffn), chip_arr, "early")
        rs["early_own"] = outs[4:]
        rs["s"] = _split_start(_scatter_copies, 12, outs[:4],
                               [lax.empty((3,) + a.shape[1:], BF16) for a in outs[:4]], outs[4], "scatter_early_start")
        return rs["s"][4]

    small = (g_pre_mix + g_token[:1, :1], g_post_mix, g_pre_ffn, g_post_ffn, w_pool[0], pool_scale, rel_bias, sinks)
    loss, gx, small_grads, ((win_own, win_oth), _, _) = _local_step(
        x[0], loss_target[0], small, win_all.reshape(IN_W, D), wout_all.reshape(D, D), ffn_weights, c_arr,
        on_ffn_grads, on_wout_grads)
    ssem, rsem, srcs, lands, _ = rs["s"]
    recv_early = _split_wait(_scatter_copies, ssem, rsem, srcs, lands, [gx], "scatter_early_wait")[4:]

    unused = jnp.zeros((1, 1), F32)
    gp = _pack_small(small_grads + (loss,))
    wp = _pack_small((g_pre_mix, g_post_mix, g_pre_ffn, g_post_ffn, w_pool, pool_scale, rel_bias, sinks, unused))
    mp = _pack_small((m_g_pre_mix, m_g_post_mix, m_g_pre_ffn, m_g_post_ffn, m_w_pool, m_pool_scale, m_rel_bias,
                      m_sinks, unused))
    vp = _pack_small((v_g_pre_mix, v_g_post_mix, v_g_pre_ffn, v_g_post_ffn, v_w_pool, v_pool_scale, v_rel_bias,
                      v_sinks, unused))

    moments = (shards(m_w_in, m_w_out, m_w_gate, m_w_up, m_w_down),
               shards(v_w_in, v_w_out, v_w_gate, v_w_up, v_w_down))
    recv_a = _to_sibling([win_oth], "rs_exchange_win")
    outs = _chip_partial([win_own], recv_a, chip_arr, "win")
    w_ssem, w_rsem, w_srcs, w_lands, w_token = _split_start(
        _scatter_copies, 3, outs[:1], [lax.empty((3,) + outs[0].shape[1:], BF16)], gx, "scatter_win_start")
    slots = lax.dynamic_update_slice(lax.empty((8,) + gp.shape, F32), gp[None], (2 * chip + c, 0, 0))
    p_ssem, p_rsem, p_srcs, p_lands, p_token = _split_start(_peer_copies, 7, [gp], [slots], w_token,
                                                            "small_allgather_start")
    early_final = _sum_chips(list(rs["early_own"]), list(recv_early), "early", [p_token])
    early_sib = _to_sibling(early_final, "rs_share_early")
    adam_e = _adamw_shards(early_final, early_sib, big_w[1:], moments[0][1:], moments[1][1:], c_arr, "early")
    recv_win = _split_wait(_scatter_copies, w_ssem, w_rsem, w_srcs, w_lands, [adam_e[0]], "scatter_win_wait")[1:]
    win_final = _sum_chips([outs[1]], recv_win, "win")
    win_sib = _to_sibling(win_final, "rs_share_win")
    adam_w = _adamw_shards(win_final, win_sib, big_w[:1], moments[0][:1], moments[1][:1], c_arr, "win")
    big_g, big_d, big_m, big_v = [[adam_w[i]] + list(adam_e[4 * i:4 * i + 4]) for i in range(4)]

    gathered = _split_wait(_peer_copies, p_ssem, p_rsem, p_srcs, p_lands, [adam_w[0]], "small_allgather_wait")[1]
    small_out = [_unpack_small(p) for p in _small_sum_adamw(gathered, wp, mp, vp)]
    total_loss = small_out[0][8][0, 0]

    def ordered(small4, big4):
        sg1, sg2, sg3, sg4, swp, ssc, srb, ssk = small4[:8]
        bwin, bwout, bwg, bwu, bwd = big4[0].T[None], big4[1][None], big4[2].T[None], big4[3].T[None], big4[4][None]
        return [sg1, bwin, swp, ssc, srb, ssk, bwout, sg2, sg3, bwg, bwu, bwd, sg4]

    res = [total_loss, gx[None]]
    for sm, bg in zip(small_out, (big_g, big_d, big_m, big_v)):
        res += ordered(sm, bg)
    return tuple(res)
```

```python
import functools

import numpy as np
import jax
import jax.numpy as jnp
from jax import lax
from jax.experimental import pallas as pl
from jax.experimental.pallas import tpu as pltpu

F32 = jnp.float32
BF16 = jnp.bfloat16

D = 1024
POOL_W = 512
GROUP = 128
WINDOWS = (2, 4, 8, 16)
HALO = 128
NQ = 8
BLK = 128
NBUCKET = 32
IN_W = 1280
DFF = 2816
NSH = 4
FS = DFF // NSH
WIN_S = IN_W // NSH
WOUT_S = D // NSH
EPS = 1e-6
NEG = -1e30
SCALE = 0.125

ADAM_LR = 0.001
ADAM_B1 = 0.9
ADAM_B2 = 0.999
ADAM_EPS = 1e-08
ADAM_WD = 0.01
ADAM_STEP = 10

TM = 512
TM_POOL = 256
TM_FFN = 512
FFN_ROWS = 256
VMEM_LIMIT = 56 * 1024 * 1024

MESH_T = pl.DeviceIdType.MESH
ANY = pl.BlockSpec(memory_space=pl.ANY)


def _cp(n_grid=0, **kw):
    sem = ("arbitrary",) * n_grid if n_grid else None
    return pltpu.CompilerParams(dimension_semantics=sem, vmem_limit_bytes=VMEM_LIMIT, **kw)


def _dot(a, b):
    return jnp.dot(a, b, preferred_element_type=F32)


def _dot_nt(a, b):
    return lax.dot_general(a, b, (((1,), (1,)), ((), ())), preferred_element_type=F32)


def _dot_tn(a, b):
    return lax.dot_general(a, b, (((0,), (0,)), ((), ())), preferred_element_type=F32)


def _rms(x):
    r = lax.rsqrt(jnp.mean(x * x, axis=-1, keepdims=True) + EPS)
    return r, x * r


def _rms_bwd(r, n, g, dout):
    dn = dout * g
    dx = r * (dn - n * jnp.mean(dn * n, axis=-1, keepdims=True))
    dg = jnp.sum(dout * n, axis=0, keepdims=True)
    return dx, dg


def _split(x, n):
    parts = []
    r = x
    for _ in range(n):
        p = r.astype(BF16)
        parts.append(p)
        r = r - p.astype(F32)
    return parts


def _band_dot(a, x, n):
    acc = None
    for p in _split(x, n):
        t = _dot(a, p)
        acc = t if acc is None else acc + t
    return acc


def _bucket_table():
    qi = np.arange(BLK)[None, :]
    kj = np.arange(2 * BLK)[:, None]
    dist = qi + BLK - kj
    n = np.maximum(dist, 0)
    nf = np.maximum(n, 1).astype(np.float32)
    large = 16 + (np.log(nf / np.float32(16)) / np.float32(np.log(128 / 16)) * np.float32(16)).astype(np.int32)
    large = np.minimum(large, NBUCKET - 1)
    return np.where(n < 16, n, large).astype(np.int32)


def _inproj_fwd(x, g1, w_in):
    s = x.shape[0]
    tm = min(TM, s)

    def body(x_ref, g_ref, w_ref, u_ref, q_ref, k_ref, v_ref):
        _, n = _rms(x_ref[...])
        h = (n * g_ref[...]).astype(BF16)
        proj = _dot_nt(h, w_ref[...])
        u_ref[...] = proj[:, :512]
        q_ref[...] = proj[:, 512:1024].astype(BF16)
        k_ref[...] = proj[:, 1024:1152].astype(BF16)
        v_ref[...] = proj[:, 1152:1280].astype(BF16)

    row = lambda w: pl.BlockSpec((tm, w), lambda i: (i, 0))
    return pl.pallas_call(
        body, name="inproj_fwd", grid=(s // tm,),
        in_specs=[row(D), pl.BlockSpec((1, D), lambda i: (0, 0)), pl.BlockSpec((IN_W, D), lambda i: (0, 0))],
        out_specs=[row(512), row(512), row(128), row(128)],
        out_shape=[jax.ShapeDtypeStruct((s, 512), F32), jax.ShapeDtypeStruct((s, 512), BF16),
                   jax.ShapeDtypeStruct((s, 128), BF16), jax.ShapeDtypeStruct((s, 128), BF16)],
        compiler_params=_cp(1))(x, g1, w_in)


def _pooled(ext, cur, t0, tm):
    rows = lax.broadcasted_iota(jnp.int32, (tm, tm + HALO), 0)
    cols = lax.broadcasted_iota(jnp.int32, (tm, tm + HALO), 1)
    d = rows + HALO - cols
    t = t0 + lax.broadcasted_iota(jnp.int32, (tm, GROUP), 0)
    out = []
    for g, w in enumerate(WINDOWS):
        a = jnp.where((d >= 0) & (d < w), 1.0, 0.0).astype(BF16)
        ws = _band_dot(a, ext[:, g * GROUP:(g + 1) * GROUP], 3)
        cnt = jnp.minimum(t + 1, w).astype(F32)
        out.append(ws / cnt - cur[:, g * GROUP:(g + 1) * GROUP])
    return out


def _pool_fwd(u, w_pool, pool_scale):
    s = u.shape[0]
    tm = min(TM_POOL, s)
    hb = tm // HALO

    def body(uc_ref, uh_ref, wp_ref, sc_ref, o_ref):
        i = pl.program_id(0)
        cur = uc_ref[...]
        halo = jnp.where(i > 0, uh_ref[...], 0.0)
        ext = jnp.concatenate([halo, cur], axis=0)
        pooled = _pooled(ext, cur, i * tm, tm)
        for g in range(4):
            sl = slice(g * GROUP, (g + 1) * GROUP)
            mixed = _dot(pooled[g].astype(BF16), wp_ref[g])
            o_ref[:, sl] = (mixed * sc_ref[:, sl]).astype(BF16)

    return pl.pallas_call(
        body, name="pool_fwd", grid=(s // tm,),
        in_specs=[pl.BlockSpec((tm, 512), lambda i: (i, 0)),
                  pl.BlockSpec((HALO, 512), lambda i: (jnp.maximum(i * hb - 1, 0), 0)),
                  pl.BlockSpec((4, GROUP, GROUP), lambda i: (0, 0, 0)),
                  pl.BlockSpec((1, 512), lambda i: (0, 0))],
        out_specs=pl.BlockSpec((tm, 512), lambda i: (i, 0)),
        out_shape=jax.ShapeDtypeStruct((s, 512), BF16),
        compiler_params=_cp(1))(u, u, w_pool, pool_scale)


def _bias_table(bucket, rel_bias):
    def body(b_ref, rb_ref, o_ref):
        bucket_v = b_ref[...]
        key = lax.broadcasted_iota(jnp.int32, (2 * BLK, BLK), 0)
        dist = lax.broadcasted_iota(jnp.int32, (2 * BLK, BLK), 1) + BLK - key
        inwin = (dist >= 0) & (dist < BLK)
        for h in range(NQ):
            acc = jnp.zeros((2 * BLK, BLK), F32)
            for b in range(NBUCKET):
                acc = jnp.where(bucket_v == b, rb_ref[b, h], acc)
            rest = jnp.where(inwin, acc, NEG)
            o_ref[1, h] = rest
            o_ref[0, h] = jnp.where(key >= BLK, rest, NEG)

    return pl.pallas_call(
        body, name="bias_table",
        in_specs=[pl.BlockSpec(memory_space=pltpu.VMEM), pl.BlockSpec(memory_space=pltpu.SMEM)],
        out_specs=pl.BlockSpec(memory_space=pltpu.VMEM),
        out_shape=jax.ShapeDtypeStruct((2, NQ, 2 * BLK, BLK), F32),
        compiler_params=_cp())(bucket, rel_bias)


def _kv_band(ref, n, transposed):
    pstart = pl.multiple_of(jnp.maximum(n - 1, 0) * BLK, BLK)
    cstart = pl.multiple_of(n * BLK, BLK)
    lo = lax.broadcasted_iota(jnp.int32, (2 * BLK, 128), 1) < 64
    band = jnp.concatenate([ref[pl.ds(pstart, BLK), :], ref[pl.ds(cstart, BLK), :]], axis=0).astype(F32)
    rot = pltpu.roll(band, 64, axis=1)
    dup = (jnp.where(lo, band, rot), jnp.where(lo, rot, band))
    plain = [d.astype(BF16) for d in dup]
    if not transposed:
        return plain, None, (lo, pstart, cstart)
    row_lo = lax.broadcasted_iota(jnp.int32, (128, 2 * BLK), 0) < 64
    tr = [[jnp.where(row_lo, d.T, 0.0).astype(BF16), jnp.where(row_lo, 0.0, d.T).astype(BF16)] for d in dup]
    return plain, tr, (lo, pstart, cstart)


def _attn_probs(qm, kk, bias, sink):
    s = _dot_nt(kk, qm) + bias
    m = jnp.maximum(jnp.max(s, axis=0, keepdims=True), sink)
    p = jnp.exp(s - m)
    es = jnp.exp(sink - m)
    inv = 1.0 / (jnp.sum(p, axis=0, keepdims=True) + es)
    return p * inv, es * inv


def _attn_fwd(q, k, v, bias, sinks):
    s = q.shape[0]

    def body(q_ref, k_ref, v_ref, b_ref, sk_ref, o_ref):
        n = pl.program_id(0)
        kk, _, _ = _kv_band(k_ref, n, False)
        _, vt, _ = _kv_band(v_ref, n, True)
        bidx = jnp.minimum(n, 1)
        lo_q = lax.broadcasted_iota(jnp.int32, (BLK, 128), 1) < 64
        for p in range(4):
            h = p // 2
            qt = q_ref[:, p * 128:(p + 1) * 128].astype(F32) * SCALE
            acc_t = jnp.zeros((128, BLK), F32)
            for e in range(2):
                head = 2 * p + e
                qm = jnp.where(lo_q if e == 0 else jnp.logical_not(lo_q), qt, 0.0).astype(BF16)
                prob, _ = _attn_probs(qm, kk[h], b_ref[bidx, head], sk_ref[0, head])
                acc_t = acc_t + _dot(vt[h][e], prob.astype(BF16))
            o_ref[:, p * 128:(p + 1) * 128] = acc_t.T.astype(BF16)

    return pl.pallas_call(
        body, name="attn_fwd", grid=(s // BLK,),
        in_specs=[pl.BlockSpec((BLK, 512), lambda i: (i, 0)),
                  pl.BlockSpec((s, 128), lambda i: (0, 0)),
                  pl.BlockSpec((s, 128), lambda i: (0, 0)),
                  pl.BlockSpec((2, NQ, 2 * BLK, BLK), lambda i: (0, 0, 0, 0)),
                  pl.BlockSpec(memory_space=pltpu.SMEM)],
        out_specs=pl.BlockSpec((BLK, 512), lambda i: (i, 0)),
        out_shape=jax.ShapeDtypeStruct((s, 512), BF16),
        compiler_params=_cp(1))(q, k, v, bias, sinks)


def _outproj_fwd(pool_o, attn_o, w_out, x, g2, g3):
    s = x.shape[0]
    tm = min(TM, s)

    def body(p_ref, a_ref, w_ref, x_ref, g2_ref, g3_ref, mix_ref, x1_ref, h2_ref):
        mix = _dot(p_ref[...], w_ref[0:512, :]) + _dot(a_ref[...], w_ref[512:1024, :])
        mix_ref[...] = mix
        _, n2 = _rms(mix)
        x1 = x_ref[...] + n2 * g2_ref[...]
        x1_ref[...] = x1
        _, n3 = _rms(x1)
        h2_ref[...] = (n3 * g3_ref[...]).astype(BF16)

    row = lambda w: pl.BlockSpec((tm, w), lambda i: (i, 0))
    vec = pl.BlockSpec((1, D), lambda i: (0, 0))
    return pl.pallas_call(
        body, name="outproj_fwd", grid=(s // tm,),
        in_specs=[row(512), row(512), pl.BlockSpec((D, D), lambda i: (0, 0)), row(D), vec, vec],
        out_specs=[row(D), row(D), row(D)],
        out_shape=[jax.ShapeDtypeStruct((s, D), F32), jax.ShapeDtypeStruct((s, D), F32),
                   jax.ShapeDtypeStruct((s, D), BF16)],
        compiler_params=_cp(1))(pool_o, attn_o, w_out, x, g2, g3)


def _silu_parts(g):
    sg = jax.nn.sigmoid(g)
    return sg, g * sg


def _ffn_fwd(h2, wg, wu, wd, x1, target, g4):
    s = h2.shape[0]
    tm = min(TM_FFN, s)

    def body(h_ref, wg_ref, wu_ref, wd_ref, x1_ref, t_ref, g4_ref,
             gate_ref, up_ref, df_ref, dy_ref, loss_ref, gg4_ref, f_acc):
        i = pl.program_id(0)
        j = pl.program_id(1)
        first = j == 0
        for r in range(0, tm, FFN_ROWS):
            rows = pl.ds(r, min(FFN_ROWS, tm))
            h = h_ref[rows, :]
            gate = _dot_nt(h, wg_ref[...])
            up = _dot_nt(h, wu_ref[...])
            gate_ref[rows, :] = gate.astype(BF16)
            up_ref[rows, :] = up.astype(BF16)
            _, sl = _silu_parts(gate)
            contrib = _dot((sl * up).astype(BF16), wd_ref[...])
            f_acc[rows, :] = jnp.where(first, contrib, f_acc[rows, :] + contrib)

        @pl.when((i == 0) & (j == 0))
        def _():
            loss_ref[...] = jnp.zeros_like(loss_ref)
            gg4_ref[...] = jnp.zeros_like(gg4_ref)

        @pl.when(j == NSH - 1)
        def _():
            r4, n4 = _rms(f_acc[...])
            g4v = g4_ref[...]
            err = x1_ref[...] + n4 * g4v - t_ref[...]
            loss_ref[...] += (0.5 / D) * jnp.sum(err * err).reshape(1, 1)
            dy = err * (1.0 / D)
            dy_ref[...] = dy
            df, dg = _rms_bwd(r4, n4, g4v, dy)
            gg4_ref[...] += dg
            df_ref[...] = df.astype(BF16)

    row = lambda w: pl.BlockSpec((tm, w), lambda i, j: (i, 0))
    sh = lambda r, c: pl.BlockSpec((None, r, c), lambda i, j: (j, 0, 0))
    act = pl.BlockSpec((None, tm, FS), lambda i, j: (j, i, 0))
    vec = pl.BlockSpec((1, D), lambda i, j: (0, 0))
    return pl.pallas_call(
        body, name="ffn_fwd", grid=(s // tm, NSH),
        in_specs=[row(D), sh(FS, D), sh(FS, D), sh(FS, D), row(D), row(D), vec],
        out_specs=[act, act, row(D), row(D), pl.BlockSpec((1, 1), lambda i, j: (0, 0)), vec],
        out_shape=[jax.ShapeDtypeStruct((NSH, s, FS), BF16), jax.ShapeDtypeStruct((NSH, s, FS), BF16),
                   jax.ShapeDtypeStruct((s, D), BF16), jax.ShapeDtypeStruct((s, D), F32),
                   jax.ShapeDtypeStruct((1, 1), F32), jax.ShapeDtypeStruct((1, D), F32)],
        scratch_shapes=[pltpu.VMEM((tm, D), F32)],
        compiler_params=_cp(2))(h2, wg, wu, wd, x1, target, g4)


def _ffn_bwd_act(df, gate, up, wg, wu, wd, dy, x1, mix, g3, g2):
    s = df.shape[0]
    tm = min(TM_FFN, s)

    def body(df_ref, gate_ref, up_ref, wg_ref, wu_ref, wd_ref, dy_ref, x1_ref, mix_ref, g3_ref, g2_ref,
             dgate_ref, dup_ref, dx1_ref, dmix_ref, gg3_ref, gg2_ref, acc):
        i = pl.program_id(0)
        j = pl.program_id(1)
        first = j == 0
        for r in range(0, tm, FFN_ROWS):
            rows = pl.ds(r, min(FFN_ROWS, tm))
            da = _dot_nt(df_ref[rows, :], wd_ref[...])
            g = gate_ref[rows, :].astype(F32)
            u = up_ref[rows, :].astype(F32)
            sg, sl = _silu_parts(g)
            dgate = (da * u * (sg * (1.0 + g * (1.0 - sg)))).astype(BF16)
            dup = (da * sl).astype(BF16)
            dgate_ref[rows, :] = dgate
            dup_ref[rows, :] = dup
            contrib = _dot(dgate, wg_ref[...]) + _dot(dup, wu_ref[...])
            acc[rows, :] = jnp.where(first, contrib, acc[rows, :] + contrib)

        @pl.when((i == 0) & (j == 0))
        def _():
            gg3_ref[...] = jnp.zeros_like(gg3_ref)
            gg2_ref[...] = jnp.zeros_like(gg2_ref)

        @pl.when(j == NSH - 1)
        def _():
            r3, n3 = _rms(x1_ref[...])
            dx1n, dg3 = _rms_bwd(r3, n3, g3_ref[...], acc[...])
            dx1 = dy_ref[...] + dx1n
            dx1_ref[...] = dx1
            gg3_ref[...] += dg3
            r2, n2 = _rms(mix_ref[...])
            dmix, dg2 = _rms_bwd(r2, n2, g2_ref[...], dx1)
            gg2_ref[...] += dg2
            dmix_ref[...] = dmix.astype(BF16)

    row = lambda w: pl.BlockSpec((tm, w), lambda i, j: (i, 0))
    sh = lambda r, c: pl.BlockSpec((None, r, c), lambda i, j: (j, 0, 0))
    act = pl.BlockSpec((None, tm, FS), lambda i, j: (j, i, 0))
    vec = pl.BlockSpec((1, D), lambda i, j: (0, 0))
    return pl.pallas_call(
        body, name="ffn_bwd_act", grid=(s // tm, NSH),
        in_specs=[row(D), act, act, sh(FS, D), sh(FS, D), sh(FS, D), row(D), row(D), row(D), vec, vec],
        out_specs=[act, act, row(D), row(D), vec, vec],
        out_shape=[jax.ShapeDtypeStruct((NSH, s, FS), BF16), jax.ShapeDtypeStruct((NSH, s, FS), BF16),
                   jax.ShapeDtypeStruct((s, D), F32), jax.ShapeDtypeStruct((s, D), BF16),
                   jax.ShapeDtypeStruct((1, D), F32), jax.ShapeDtypeStruct((1, D), F32)],
        scratch_shapes=[pltpu.VMEM((tm, D), F32)],
        compiler_params=_cp(2))(df, gate, up, wg, wu, wd, dy, x1, mix, g3, g2)


SMEM_SPEC = pl.BlockSpec(memory_space=pltpu.SMEM)


def _emit_halves(acc_ref, row0, rows, c, own_ref, oth_ref):
    half = rows // 2
    own_ref[...] = acc_ref[pl.ds(row0 + pl.multiple_of(c * half, 8), half), :]
    oth_ref[...] = acc_ref[pl.ds(row0 + pl.multiple_of((1 - c) * half, 8), half), :].astype(BF16)


def _half_shapes(rows):
    return [jax.ShapeDtypeStruct((NSH, rows // 2, D), F32), jax.ShapeDtypeStruct((NSH, rows // 2, D), BF16)]


def _ffn_bwd_w(h2, gate, up, dgate, dup, df, c_arr):
    s = h2.shape[0]
    tm = min(TM_FFN, s)
    nt = s // tm

    def body(c_ref, h_ref, gate_ref, up_ref, dgate_ref, dup_ref, df_ref, *rest):
        outs, accs = rest[:6], rest[6:]
        i = pl.program_id(1)
        @pl.when(i == 0)
        def _():
            for acc in accs:
                acc[...] = jnp.zeros_like(acc)

        h = h_ref[...]
        accs[0][...] += _dot_tn(dgate_ref[...], h)
        accs[1][...] += _dot_tn(dup_ref[...], h)
        _, sl = _silu_parts(gate_ref[...].astype(F32))
        a = (sl * up_ref[...].astype(F32)).astype(BF16)
        accs[2][...] += _dot_tn(a, df_ref[...])

        @pl.when(i == nt - 1)
        def _():
            for t, acc in enumerate(accs):
                _emit_halves(acc, 0, FS, c_ref[0], outs[2 * t], outs[2 * t + 1])

    row = lambda w: pl.BlockSpec((tm, w), lambda j, i: (i, 0))
    act = pl.BlockSpec((None, tm, FS), lambda j, i: (j, i, 0))
    half = pl.BlockSpec((None, FS // 2, D), lambda j, i: (j, 0, 0))
    return pl.pallas_call(
        body, name="ffn_bwd_w", grid=(NSH, nt),
        in_specs=[SMEM_SPEC, row(D), act, act, act, act, row(D)],
        out_specs=[half] * 6,
        out_shape=_half_shapes(FS) * 3,
        scratch_shapes=[pltpu.VMEM((FS, D), F32)] * 3,
        compiler_params=_cp(2))(c_arr, h2, gate, up, dgate, dup, df)


def _outproj_bwd(dmix, w_out, pool_o, attn_o, c_arr, dep=None):
    s = dmix.shape[0]
    tm = min(TM, s)
    nt = s // tm

    def body(c_ref, dm_ref, w_ref, p_ref, a_ref, *rest):
        dpool_ref, dattn_ref, own_ref, oth_ref, gw_ref = rest[-5:]
        i = pl.program_id(0)

        @pl.when(i == 0)
        def _():
            gw_ref[...] = jnp.zeros_like(gw_ref)

        dm = dm_ref[...]
        dpool_ref[...] = _dot_nt(dm, w_ref[0:512, :])
        dattn_ref[...] = _dot_nt(dm, w_ref[512:1024, :]).astype(BF16)
        gw_ref[0:512, :] += _dot_tn(p_ref[...], dm)
        gw_ref[512:1024, :] += _dot_tn(a_ref[...], dm)

        @pl.when(i == nt - 1)
        def _():
            for k in range(NSH):
                _emit_halves(gw_ref, k * WOUT_S, WOUT_S, c_ref[0], own_ref.at[k], oth_ref.at[k])

    row = lambda w: pl.BlockSpec((tm, w), lambda i: (i, 0))
    full = pl.BlockSpec((D, D), lambda i: (0, 0))
    half = pl.BlockSpec((NSH, WOUT_S // 2, D), lambda i: (0, 0, 0))
    return pl.pallas_call(
        body, name="outproj_bwd", grid=(nt,),
        in_specs=[SMEM_SPEC, row(D), full, row(512), row(512)] + ([] if dep is None else [ANY]),
        out_specs=[row(512), row(512), half, half],
        out_shape=[jax.ShapeDtypeStruct((s, 512), F32), jax.ShapeDtypeStruct((s, 512), BF16)] + _half_shapes(WOUT_S),
        scratch_shapes=[pltpu.VMEM((D, D), F32)],
        compiler_params=_cp(1))(c_arr, dmix, w_out, pool_o, attn_o, *([] if dep is None else [dep]))


def _pool_bwd(u, dpool, w_pool, pool_scale, dep=None):
    s = u.shape[0]
    tm = min(TM_POOL, s)
    hb = tm // HALO
    nt = s // tm
    last_halo = s // HALO - 1

    def body(uc_ref, uh_ref, dc_ref, dn_ref, wp_ref, sc_ref, *rest):
        du_ref, gwp_ref, gsc_ref = rest[-3:]
        i = pl.program_id(0)

        @pl.when(i == 0)
        def _():
            gwp_ref[...] = jnp.zeros_like(gwp_ref)
            gsc_ref[...] = jnp.zeros_like(gsc_ref)

        cur = uc_ref[...]
        halo = jnp.where(i > 0, uh_ref[...], 0.0)
        pooled = _pooled(jnp.concatenate([halo, cur], axis=0), cur, i * tm, tm)
        dcur = dc_ref[...]
        dnext = jnp.where(i < nt - 1, dn_ref[...], 0.0)
        dext = jnp.concatenate([dcur, dnext], axis=0)
        rows = lax.broadcasted_iota(jnp.int32, (tm, tm + HALO), 0)
        cols = lax.broadcasted_iota(jnp.int32, (tm, tm + HALO), 1)
        d = cols - rows
        t_ext = i * tm + lax.broadcasted_iota(jnp.int32, (tm + HALO, GROUP), 0)
        for g, w in enumerate(WINDOWS):
            sl = slice(g * GROUP, (g + 1) * GROUP)
            pb = pooled[g].astype(BF16)
            wp = wp_ref[g]
            mixed = _dot(pb, wp)
            gsc_ref[:, sl] += jnp.sum(dcur[:, sl] * mixed, axis=0, keepdims=True)
            dmixed = (dext[:, sl] * sc_ref[:, sl]).astype(BF16)
            gwp_ref[g] += _dot_tn(pb, dmixed[0:tm])
            dpooled = _dot_nt(dmixed, wp)
            z = dpooled / jnp.minimum(t_ext + 1, w).astype(F32)
            b = jnp.where((d >= 0) & (d < w), 1.0, 0.0).astype(BF16)
            du_ref[:, sl] = (_band_dot(b, z, 2) - dpooled[0:tm]).astype(BF16)

    return pl.pallas_call(
        body, name="pool_bwd", grid=(nt,),
        in_specs=[pl.BlockSpec((tm, 512), lambda i: (i, 0)),
                  pl.BlockSpec((HALO, 512), lambda i: (jnp.maximum(i * hb - 1, 0), 0)),
                  pl.BlockSpec((tm, 512), lambda i: (i, 0)),
                  pl.BlockSpec((HALO, 512), lambda i: (jnp.minimum((i + 1) * hb, last_halo), 0)),
                  pl.BlockSpec((4, GROUP, GROUP), lambda i: (0, 0, 0)),
                  pl.BlockSpec((1, 512), lambda i: (0, 0))] + ([] if dep is None else [ANY]),
        out_specs=[pl.BlockSpec((tm, 512), lambda i: (i, 0)),
                   pl.BlockSpec((4, GROUP, GROUP), lambda i: (0, 0, 0)),
                   pl.BlockSpec((1, 512), lambda i: (0, 0))],
        out_shape=[jax.ShapeDtypeStruct((s, 512), BF16), jax.ShapeDtypeStruct((4, GROUP, GROUP), F32),
                   jax.ShapeDtypeStruct((1, 512), F32)],
        compiler_params=_cp(1))(u, u, dpool, dpool, w_pool, pool_scale, *([] if dep is None else [dep]))


def _attn_bwd(q, k, v, do, bias, sinks, dep=None):
    s = q.shape[0]

    def body(q_ref, do_ref, k_ref, v_ref, b_ref, sk_ref, *rest):
        dq_ref, dk_ref, dv_ref, dss_ref, gsk_ref = rest[-5:]
        n = pl.program_id(0)

        @pl.when(n == 0)
        def _():
            dk_ref[...] = jnp.zeros_like(dk_ref)
            dv_ref[...] = jnp.zeros_like(dv_ref)
            dss_ref[...] = jnp.zeros_like(dss_ref)
            gsk_ref[...] = jnp.zeros_like(gsk_ref)

        kk, kt, (lo, pstart, cstart) = _kv_band(k_ref, n, True)
        vv, _, _ = _kv_band(v_ref, n, False)
        bidx = jnp.minimum(n, 1)
        lo_q = lax.broadcasted_iota(jnp.int32, (BLK, 128), 1) < 64
        dkk = [jnp.zeros((2 * BLK, 128), F32), jnp.zeros((2 * BLK, 128), F32)]
        dvv = [jnp.zeros((2 * BLK, 128), F32), jnp.zeros((2 * BLK, 128), F32)]
        for p in range(4):
            h = p // 2
            qt = q_ref[:, p * 128:(p + 1) * 128].astype(F32) * SCALE
            dot = do_ref[:, p * 128:(p + 1) * 128]
            dq_t = jnp.zeros((128, BLK), F32)
            for e in range(2):
                head = 2 * p + e
                sel_q = lo_q if e == 0 else jnp.logical_not(lo_q)
                qm = jnp.where(sel_q, qt, 0.0).astype(BF16)
                prob, ps = _attn_probs(qm, kk[h], b_ref[bidx, head], sk_ref[0, head])
                dom = jnp.where(sel_q, dot, jnp.zeros_like(dot))
                dp = _dot_nt(vv[h], dom)
                delta = jnp.sum(prob * dp, axis=0, keepdims=True)
                ds = prob * (dp - delta)
                gsk_ref[pl.ds(head, 1), :] += jnp.broadcast_to(-jnp.sum(ps * delta).reshape(1, 1), (1, 128))
                dss_ref[head] += ds
                dsb = ds.astype(BF16)
                dq_t = dq_t + _dot(kt[h][e], dsb)
                dkk[h] = dkk[h] + _dot(dsb, qm)
                dvv[h] = dvv[h] + _dot(prob.astype(BF16), dom)
            dq_ref[:, p * 128:(p + 1) * 128] = (dq_t.T * SCALE).astype(BF16)
        for acc, ref in ((dkk, dk_ref), (dvv, dv_ref)):
            f0 = acc[0] + pltpu.roll(acc[0], 64, axis=1)
            f1 = acc[1] + pltpu.roll(acc[1], 64, axis=1)
            band = jnp.where(lo, f0, f1)
            ref[pl.ds(pstart, BLK), :] += band[0:BLK]
            ref[pl.ds(cstart, BLK), :] += band[BLK:2 * BLK]

    blk = pl.BlockSpec((BLK, 512), lambda i: (i, 0))
    kv = pl.BlockSpec((s, 128), lambda i: (0, 0))
    return pl.pallas_call(
        body, name="attn_bwd", grid=(s // BLK,),
        in_specs=[blk, blk, kv, kv, pl.BlockSpec((2, NQ, 2 * BLK, BLK), lambda i: (0, 0, 0, 0)),
                  pl.BlockSpec(memory_space=pltpu.SMEM)] + ([] if dep is None else [ANY]),
        out_specs=[blk, kv, kv, pl.BlockSpec((NQ, 2 * BLK, BLK), lambda i: (0, 0, 0)),
                   pl.BlockSpec((NQ, 128), lambda i: (0, 0))],
        out_shape=[jax.ShapeDtypeStruct((s, 512), BF16), jax.ShapeDtypeStruct((s, 128), F32),
                   jax.ShapeDtypeStruct((s, 128), F32), jax.ShapeDtypeStruct((NQ, 2 * BLK, BLK), F32),
                   jax.ShapeDtypeStruct((NQ, 128), F32)],
        compiler_params=_cp(1))(q, do, k, v, bias, sinks, *([] if dep is None else [dep]))


def _relbias_grad(dss, bucket):
    def body(ds_ref, b_ref, o_ref):
        bucket_v = b_ref[...]
        lane = lax.broadcasted_iota(jnp.int32, (NQ, 128), 1)
        head_row = lax.broadcasted_iota(jnp.int32, (NQ, BLK), 0)
        acc = jnp.zeros((NQ, 128), F32)
        for b in range(NBUCKET):
            sel = bucket_v == b
            stack = jnp.zeros((NQ, BLK), F32)
            for h in range(NQ):
                col_sums = jnp.sum(jnp.where(sel, ds_ref[h], 0.0), axis=0, keepdims=True)
                stack = jnp.where(head_row == h, col_sums, stack)
            acc = acc + jnp.where(lane == b, jnp.sum(stack, axis=1, keepdims=True), 0.0)
        o_ref[...] = acc

    return pl.pallas_call(
        body, name="relbias_grad",
        in_specs=[pl.BlockSpec(memory_space=pltpu.VMEM), pl.BlockSpec(memory_space=pltpu.VMEM)],
        out_specs=pl.BlockSpec(memory_space=pltpu.VMEM),
        out_shape=jax.ShapeDtypeStruct((NQ, 128), F32),
        compiler_params=_cp())(dss, bucket)


def _inproj_bwd(x, g1, du, dq, dk, dv, w_in, dx1, c_arr):
    s = x.shape[0]
    tm = min(TM, s)
    nt = s // tm
    cols = ((0, 512), (512, 1024), (1024, 1152), (1152, 1280))

    def body(c_ref, x_ref, g_ref, du_ref, dq_ref, dk_ref, dv_ref, w_ref, dx1_ref, gx_ref, own_ref, oth_ref, gg_ref,
             gw_ref):
        i = pl.program_id(0)

        @pl.when(i == 0)
        def _():
            gw_ref[...] = jnp.zeros_like(gw_ref)
            gg_ref[...] = jnp.zeros_like(gg_ref)

        gv = g_ref[...]
        r1, n1 = _rms(x_ref[...])
        h = (n1 * gv).astype(BF16)
        parts = (du_ref[...], dq_ref[...], dk_ref[...].astype(BF16), dv_ref[...].astype(BF16))
        dh = None
        for (a, b), dpart in zip(cols, parts):
            t = _dot(dpart, w_ref[a:b, :])
            dh = t if dh is None else dh + t
            gw_ref[a:b, :] += _dot_tn(dpart, h)
        dx, dg = _rms_bwd(r1, n1, gv, dh)
        gg_ref[...] += dg
        gx_ref[...] = dx1_ref[...] + dx

        @pl.when(i == nt - 1)
        def _():
            for k in range(NSH):
                _emit_halves(gw_ref, k * WIN_S, WIN_S, c_ref[0], own_ref.at[k], oth_ref.at[k])

    row = lambda w: pl.BlockSpec((tm, w), lambda i: (i, 0))
    vec = pl.BlockSpec((1, D), lambda i: (0, 0))
    full = pl.BlockSpec((IN_W, D), lambda i: (0, 0))
    half = pl.BlockSpec((NSH, WIN_S // 2, D), lambda i: (0, 0, 0))
    return pl.pallas_call(
        body, name="inproj_bwd", grid=(nt,),
        in_specs=[SMEM_SPEC, row(D), vec, row(512), row(512), row(128), row(128), full, row(D)],
        out_specs=[row(D), half, half, vec],
        out_shape=[jax.ShapeDtypeStruct((s, D), F32)] + _half_shapes(WIN_S) + [jax.ShapeDtypeStruct((1, D), F32)],
        scratch_shapes=[pltpu.VMEM((IN_W, D), F32)],
        compiler_params=_cp(1))(c_arr, x, g1, du, dq, dk, dv, w_in, dx1)


def _local_step(x, target, small, w_in, w_out, ffn_weights, c_arr, on_ffn_grads=None, on_wout_grads=None):
    g1, g2, g3, g4, w_pool, pool_scale, rel_bias, sinks = small
    bucket = jnp.asarray(_bucket_table())
    wp_bf = w_pool.astype(BF16)
    u, q, k, v = _inproj_fwd(x, g1, w_in)
    pool_o = _pool_fwd(u, wp_bf, pool_scale)
    bias = _bias_table(bucket, rel_bias)
    attn_o = _attn_fwd(q, k, v, bias, sinks)
    wg, wu, wd = ffn_weights(pool_o, attn_o) if callable(ffn_weights) else ffn_weights
    mix, x1, h2 = _outproj_fwd(pool_o, attn_o, w_out, x, g2, g3)
    gate, up, df, dy, loss, gg4 = _ffn_fwd(h2, wg, wu, wd, x1, target, g4)
    dgate, dup, dx1, dmix, gg3, gg2 = _ffn_bwd_act(df, gate, up, wg, wu, wd, dy, x1, mix, g3, g2)
    ffn_g = _ffn_bwd_w(h2, gate, up, dgate, dup, df, c_arr)
    dep = None if on_ffn_grads is None else on_ffn_grads(ffn_g)
    dpool, dattn, wout_own, wout_oth = _outproj_bwd(dmix, w_out, pool_o, attn_o, c_arr, dep)
    dep = None if on_wout_grads is None else on_wout_grads(wout_own, wout_oth, dpool)
    du, gwp, gsc = _pool_bwd(u, dpool, wp_bf, pool_scale, dep)
    dq, dk, dv, dss, gsk = _attn_bwd(q, k, v, dattn, bias, sinks, dep)
    grb = _relbias_grad(dss, bucket)
    gx, win_own, win_oth, gg1 = _inproj_bwd(x, g1, du, dq, dk, dv, w_in, dx1, c_arr)
    small_grads = (gg1, gg2, gg3, gg4, gwp, gsc, grb[:, :NBUCKET].T, gsk[:, 0].reshape(1, NQ))
    return loss, gx, small_grads, ((win_own, win_oth), (wout_own, wout_oth), ffn_g)


def _place():
    x, y, c = lax.axis_index("x"), lax.axis_index("y"), lax.axis_index("c")
    chips = ((1 - x, y), (x, 1 - y), (1 - x, 1 - y))
    return x, y, c, chips


def _remote(src, dst, ssem, rsem, dev):
    return pltpu.make_async_remote_copy(src_ref=src, dst_ref=dst, send_sem=ssem, recv_sem=rsem,
                                        device_id=dev, device_id_type=MESH_T)


def _own_slot(shard):
    me = 2 * lax.axis_index("x") + lax.axis_index("y")
    return lax.dynamic_update_slice(lax.empty((NSH,) + shard.shape, shard.dtype), shard[None], (me, 0, 0))


def _all_gather_weights(shards, lands):
    nt = len(shards)

    def body(*refs):
        srcs, dsts = refs[:nt], refs[2 * nt:3 * nt]
        isend, irecv, dsend, drecv = refs[3 * nt:]
        x, y, c, chips = _place()
        me = 2 * x + y
        sib = (x, y, 1 - c)
        sends = []
        for t in range(nt):
            half = srcs[t].shape[0] // 2
            mine = pl.ds(pl.multiple_of(c * half, 16), half)
            for j, (px, py) in enumerate(chips):
                cp = _remote(srcs[t].at[mine], dsts[t].at[me, mine], isend.at[t, j], irecv.at[t, j], (px, py, c))
                cp.start()
                sends.append(cp)
        for t in range(nt):
            half = srcs[t].shape[0] // 2
            mine = pl.ds(pl.multiple_of(c * half, 16), half)
            for j, (px, py) in enumerate(chips):
                blk = dsts[t].at[2 * px + py, mine]
                _remote(blk, blk, isend.at[t, j], irecv.at[t, j], (px, py, c)).wait_recv()
                cp = _remote(blk, blk, dsend.at[t, j], drecv.at[t, j], sib)
                cp.start()
                sends.append(cp)
        for t in range(nt):
            half = srcs[t].shape[0] // 2
            other = pl.ds(pl.multiple_of((1 - c) * half, 16), half)
            for j, (px, py) in enumerate(chips):
                blk = dsts[t].at[2 * px + py, other]
                _remote(blk, blk, dsend.at[t, j], drecv.at[t, j], sib).wait_recv()
        for cp in sends:
            cp.wait_send()

    return pl.pallas_call(
        body, name="allgather_weights",
        in_specs=[ANY] * (2 * nt), out_specs=[ANY] * nt,
        out_shape=[jax.ShapeDtypeStruct(a.shape, a.dtype) for a in lands],
        input_output_aliases={nt + t: t for t in range(nt)},
        scratch_shapes=[pltpu.SemaphoreType.DMA((nt, 3)), pltpu.SemaphoreType.DMA((nt, 3)),
                        pltpu.SemaphoreType.DMA((nt, 3)), pltpu.SemaphoreType.DMA((nt, 3))],
        compiler_params=_cp())(*shards, *lands)


def _to_sibling(arrs, name):
    nt = len(arrs)

    def body(*refs):
        srcs, dsts = refs[:nt], refs[nt:2 * nt]
        ssem, rsem = refs[2 * nt:]
        x, y, c, _ = _place()
        cps = [_remote(srcs[t], dsts[t], ssem.at[t], rsem.at[t], (x, y, 1 - c)) for t in range(nt)]
        for cp in cps:
            cp.start()
        for cp in cps:
            cp.wait()

    return pl.pallas_call(
        body, name=name, in_specs=[ANY] * nt, out_specs=[ANY] * nt,
        out_shape=[jax.ShapeDtypeStruct(a.shape, a.dtype) for a in arrs],
        scratch_shapes=[pltpu.SemaphoreType.DMA((nt,)), pltpu.SemaphoreType.DMA((nt,))],
        compiler_params=_cp())(*arrs)


def _scatter_to_chips(arrs):
    nt = len(arrs)

    def body(*refs):
        srcs, dsts = refs[:nt], refs[nt:2 * nt]
        ssem, rsem = refs[2 * nt:]
        x, y, c, chips = _place()
        cps = []
        for t in range(nt):
            for j, (px, py) in enumerate(chips):
                cps.append(_remote(srcs[t].at[2 * px + py], dsts[t].at[j], ssem.at[t, j], rsem.at[t, j], (px, py, c)))
        for cp in cps:
            cp.start()
        for cp in cps:
            cp.wait()

    return pl.pallas_call(
        body, name="scatter_to_chips", in_specs=[ANY] * nt, out_specs=[ANY] * nt,
        out_shape=[jax.ShapeDtypeStruct((3,) + a.shape[1:], a.dtype) for a in arrs],
        scratch_shapes=[pltpu.SemaphoreType.DMA((nt, 3)), pltpu.SemaphoreType.DMA((nt, 3))],
        compiler_params=_cp())(*arrs)


HBM_SPEC = pl.BlockSpec(memory_space=pltpu.HBM)
SEM_SPEC = pl.BlockSpec(memory_space=pltpu.SEMAPHORE)
DATAFLOW = pltpu.SideEffectType.DATAFLOW_SIDE_EFFECTING


def _gather_copies(srcs, lands, ssem, rsem):
    x, y, c, chips = _place()
    me = 2 * x + y
    out = []
    for t in range(len(srcs)):
        half = srcs[t].shape[0] // 2
        mine = pl.ds(pl.multiple_of(c * half, 16), half)
        for j, (px, py) in enumerate(chips):
            k = 3 * t + j
            send = _remote(srcs[t].at[mine], lands[t].at[me, mine], ssem.at[k], rsem.at[k], (px, py, c))
            blk = lands[t].at[2 * px + py, mine]
            out.append((send, _remote(blk, blk, ssem.at[k], rsem.at[k], (px, py, c))))
    return out


def _scatter_copies(srcs, lands, ssem, rsem):
    x, y, c, chips = _place()
    out = []
    for t in range(len(srcs)):
        for j, (px, py) in enumerate(chips):
            k = 3 * t + j
            send = _remote(srcs[t].at[2 * px + py], lands[t].at[j], ssem.at[k], rsem.at[k], (px, py, c))
            out.append((send, _remote(lands[t].at[j], lands[t].at[j], ssem.at[k], rsem.at[k], (px, py, c))))
    return out


def _sibling_copies(srcs, lands, ssem, rsem):
    x, y, c, _ = _place()
    sib = (x, y, 1 - c)
    return [(_remote(srcs[t], lands[t], ssem.at[t], rsem.at[t], sib),
             _remote(lands[t], lands[t], ssem.at[t], rsem.at[t], sib)) for t in range(len(srcs))]


def _peer_copies(srcs, lands, ssem, rsem):
    x, y, c, _ = _place()
    me = 4 * x + 2 * y + c
    out = []
    for kk in range(1, 8):
        px, py, pc = x ^ (kk >> 2), y ^ ((kk >> 1) & 1), c ^ (kk & 1)
        send = _remote(srcs[0], lands[0].at[me], ssem.at[kk - 1], rsem.at[kk - 1], (px, py, pc))
        slot = lands[0].at[4 * px + 2 * py + pc]
        out.append((send, _remote(slot, slot, ssem.at[kk - 1], rsem.at[kk - 1], (px, py, pc))))
    return out


def _split_start(plan, ncopy, srcs, lands, after, name):
    ns, nbuf = len(srcs), len(srcs) + len(lands)

    def body(*refs):
        ssem, rsem, token = refs[nbuf + 1], refs[nbuf + 2], refs[-1]
        for send, _ in plan(refs[:ns], refs[ns:nbuf], ssem, rsem):
            send.start()
        token[...] = jnp.zeros_like(token)

    bufs = [pltpu.with_memory_space_constraint(a, pltpu.HBM) for a in list(srcs) + list(lands)]
    outs = pl.pallas_call(
        body, name=name, in_specs=[HBM_SPEC] * nbuf + [ANY],
        out_specs=[SEM_SPEC, SEM_SPEC] + [HBM_SPEC] * nbuf + [pl.BlockSpec(memory_space=pltpu.VMEM)],
        out_shape=[pltpu.SemaphoreType.DMA((ncopy,)), pltpu.SemaphoreType.DMA((ncopy,))]
        + [pltpu.HBM(a.shape, a.dtype) for a in bufs] + [jax.ShapeDtypeStruct((8, 128), F32)],
        input_output_aliases={i: 2 + i for i in range(nbuf)},
        compiler_params=pltpu.CompilerParams(has_side_effects=DATAFLOW))(*bufs, after)
    return outs[0], outs[1], outs[2:2 + ns], outs[2 + ns:2 + nbuf], outs[-1]


def _split_wait(plan, ssem, rsem, srcs, lands, after, name):
    ns, nbuf = len(srcs), len(srcs) + len(lands)

    def body(*refs):
        s_ref, r_ref = refs[nbuf], refs[nbuf + 1]
        for send, recv in plan(refs[:ns], refs[ns:nbuf], s_ref, r_ref):
            send.wait_send()
            recv.wait_recv()

    outs = pl.pallas_call(
        body, name=name, in_specs=[HBM_SPEC] * nbuf + [SEM_SPEC, SEM_SPEC] + [ANY] * len(after),
        out_specs=[HBM_SPEC] * nbuf,
        out_shape=[pltpu.HBM(a.shape, a.dtype) for a in list(srcs) + list(lands)],
        input_output_aliases={i: i for i in range(nbuf)},
        compiler_params=pltpu.CompilerParams(has_side_effects=DATAFLOW))(*srcs, *lands, ssem, rsem, *after)
    return outs


def _gather_finish(lands):
    nt = len(lands)

    def body(*refs):
        outs = refs[nt:2 * nt]
        dsend, drecv = refs[2 * nt:]
        x, y, c, chips = _place()
        sib = (x, y, 1 - c)
        sends = []
        for t in range(nt):
            half = outs[t].shape[1] // 2
            mine = pl.ds(pl.multiple_of(c * half, 16), half)
            for j, (px, py) in enumerate(chips):
                blk = outs[t].at[2 * px + py, mine]
                cp = _remote(blk, blk, dsend.at[t, j], drecv.at[t, j], sib)
                cp.start()
                sends.append(cp)
        for t in range(nt):
            half = outs[t].shape[1] // 2
            other = pl.ds(pl.multiple_of((1 - c) * half, 16), half)
            for j, (px, py) in enumerate(chips):
                blk = outs[t].at[2 * px + py, other]
                _remote(blk, blk, dsend.at[t, j], drecv.at[t, j], sib).wait_recv()
        for cp in sends:
            cp.wait_send()

    return pl.pallas_call(
        body, name="gather_finish", in_specs=[ANY] * nt, out_specs=[ANY] * nt,
        out_shape=[jax.ShapeDtypeStruct(a.shape, a.dtype) for a in lands],
        input_output_aliases={t: t for t in range(nt)},
        scratch_shapes=[pltpu.SemaphoreType.DMA((nt, 3)), pltpu.SemaphoreType.DMA((nt, 3))],
        compiler_params=_cp())(*lands)


def _chip_partial(owns, recv, chip_arr, tag):
    nt = len(owns)

    def body(chip_ref, *refs):
        k = pl.program_id(0)
        for t in range(nt):
            p = refs[t][...] + refs[nt + t][...].astype(F32)
            refs[2 * nt + t][...] = p.astype(BF16)

            @pl.when(k == chip_ref[0])
            def _():
                refs[3 * nt + t][...] = p

    blk_specs = [pl.BlockSpec((None,) + a.shape[1:], lambda k, chip_ref: (k, 0, 0)) for a in owns]
    own_specs = [pl.BlockSpec(a.shape[1:], lambda k, chip_ref: (0, 0)) for a in owns]
    return pl.pallas_call(
        body, name="rs_chip_partial_" + tag,
        grid_spec=pltpu.PrefetchScalarGridSpec(num_scalar_prefetch=1, grid=(NSH,), in_specs=blk_specs * 2,
                                               out_specs=blk_specs + own_specs),
        out_shape=[jax.ShapeDtypeStruct(a.shape, BF16) for a in owns]
        + [jax.ShapeDtypeStruct(a.shape[1:], F32) for a in owns],
        compiler_params=_cp(1))(chip_arr, *owns, *recv)


def _sum_chips(own, recv, tag, after=()):
    nt = len(own)

    def body(*refs):
        outs = refs[2 * nt + len(after):]
        for t in range(nt):
            r = refs[nt + t]
            outs[t][...] = ((refs[t][...] + r[0].astype(F32)) + r[1].astype(F32)) + r[2].astype(F32)

    vm = pl.BlockSpec(memory_space=pltpu.VMEM)
    return pl.pallas_call(
        body, name="rs_sum_chips_" + tag, in_specs=[vm] * (2 * nt) + [ANY] * len(after), out_specs=[vm] * nt,
        out_shape=[jax.ShapeDtypeStruct(a.shape, F32) for a in own],
        compiler_params=_cp())(*own, *recv, *after)


def _adamw_math(w, g, m, v):
    m = ADAM_B1 * m + (1.0 - ADAM_B1) * g
    v = ADAM_B2 * v + (1.0 - ADAM_B2) * (g * g)
    m_hat = m / (1.0 - ADAM_B1 ** ADAM_STEP)
    v_hat = v / (1.0 - ADAM_B2 ** ADAM_STEP)
    delta = -ADAM_LR * (m_hat / (jnp.sqrt(v_hat) + ADAM_EPS) + ADAM_WD * w)
    return delta, m, v


ADAM_SPLIT = 4


def _adamw_shards(own, sib, ws, ms, vs, c_arr, tag):
    nt = len(own)

    def body(c_ref, *refs):
        hh = pl.program_id(0)
        mine = hh == c_ref[0]
        for t in range(nt):
            g = jnp.where(mine, refs[t][...], refs[nt + t][...])
            delta, m, v = _adamw_math(refs[2 * nt + t][...], g, refs[3 * nt + t][...], refs[4 * nt + t][...])
            refs[5 * nt + t][...] = g
            refs[6 * nt + t][...] = delta
            refs[7 * nt + t][...] = m
            refs[8 * nt + t][...] = v

    def tile(a):
        return (a.shape[0] // ADAM_SPLIT, a.shape[1])

    half_specs = [pl.BlockSpec(tile(a), lambda hh, q, c_ref: (q, 0)) for a in own]
    full_specs = [pl.BlockSpec(tile(a), lambda hh, q, c_ref: (hh * ADAM_SPLIT + q, 0)) for a in own]
    return pl.pallas_call(
        body, name="adamw_shards_" + tag,
        grid_spec=pltpu.PrefetchScalarGridSpec(num_scalar_prefetch=1, grid=(2, ADAM_SPLIT),
                                               in_specs=half_specs * 2 + full_specs * 3, out_specs=full_specs * 4),
        out_shape=[jax.ShapeDtypeStruct(w.shape, F32) for w in ws] * 4,
        compiler_params=_cp(2))(c_arr, *own, *sib, *ws, *ms, *vs)


def _small_sum_adamw(gathered, wp, mp, vp):
    def body(g_ref, w_ref, m_ref, v_ref, go_ref, d_ref, mo_ref, vo_ref):
        g = g_ref[0]
        for d in range(1, 8):
            g = g + g_ref[d]
        delta, m, v = _adamw_math(w_ref[...], g, m_ref[...], v_ref[...])
        go_ref[...] = g
        d_ref[...] = delta
        mo_ref[...] = m
        vo_ref[...] = v

    vm = pl.BlockSpec(memory_space=pltpu.VMEM)
    return pl.pallas_call(
        body, name="small_sum_adamw", in_specs=[vm] * 4, out_specs=[vm] * 4,
        out_shape=[jax.ShapeDtypeStruct(wp.shape, F32)] * 4,
        compiler_params=_cp())(gathered, wp, mp, vp)


SMALL_PARTS = (("g1", (1, D)), ("g2", (1, D)), ("g3", (1, D)), ("g4", (1, D)), ("w_pool", (1, 4, GROUP, GROUP)),
               ("pool_scale", (1, POOL_W)), ("rel_bias", (NBUCKET, NQ)), ("sinks", (1, NQ)), ("loss", (1, 1)))


def _pack_small(parts):
    rows = []
    for a in parts:
        flat = a.reshape(-1)
        n = flat.shape[0]
        padded = -(-n // 1024) * 1024
        rows.append(jnp.pad(flat, (0, padded - n)).reshape(-1, 128))
    return jnp.concatenate(rows, axis=0)


def _unpack_small(packed):
    out, r = [], 0
    for _, shape in SMALL_PARTS:
        n = int(np.prod(shape))
        nr = -(-n // 1024) * 8
        out.append(packed[r:r + nr].reshape(-1)[:n].reshape(shape))
        r += nr
    return out


def kernel(x, g_pre_mix, w_in, w_pool, pool_scale, rel_bias, sinks, w_out, g_post_mix, g_pre_ffn, w_gate, w_up, w_down, g_post_ffn, loss_target, m_g_pre_mix, m_w_in, m_w_pool, m_pool_scale, m_rel_bias, m_sinks, m_w_out, m_g_post_mix, m_g_pre_ffn, m_w_gate, m_w_up, m_w_down, m_g_post_ffn, v_g_pre_mix, v_w_in, v_w_pool, v_pool_scale, v_rel_bias, v_sinks, v_w_out, v_g_post_mix, v_g_pre_ffn, v_w_gate, v_w_up, v_w_down, v_g_post_ffn):
    c = lax.axis_index("c")
    chip = 2 * lax.axis_index("x") + lax.axis_index("y")

    def shards(a_in, a_out, a_gate, a_up, a_down):
        return (a_in[0].T, a_out[0], a_gate[0].T, a_up[0].T, a_down[0])

    c_arr = c.reshape(1).astype(jnp.int32)
    chip_arr = chip.reshape(1).astype(jnp.int32)

    big_w = shards(w_in, w_out, w_gate, w_up, w_down)
    big_bf = [w.astype(BF16) for w in big_w]
    win_all, wout_all = _all_gather_weights(big_bf[:2], [_own_slot(a) for a in big_bf[:2]])
    g_ssem, g_rsem, g_srcs, g_lands, g_token = _split_start(
        _gather_copies, 9, big_bf[2:], [_own_slot(a) for a in big_bf[2:]], win_all, "gather_ffn_start")

    def ffn_weights(*after):
        outs = _split_wait(_gather_copies, g_ssem, g_rsem, g_srcs, g_lands, after, "gather_ffn_wait")
        return _gather_finish(outs[3:])

    rs = {}

    def on_ffn_grads(ffn_g):
        oths = ffn_g[1::2]
        rs["ffn_own"] = ffn_g[0::2]
        rs["x"] = _split_start(_sibling_copies, 3, oths, [lax.empty(a.shape, BF16) for a in oths], ffn_g[0],
                               "rs_exchange_ffn_start")
        return rs["x"][4]

    def on_wout_grads(own, oth, after):
        ssem, rsem, srcs, lands, _ = rs["x"]
        recv_ffn = _split_wait(_sibling_copies, ssem, rsem, srcs, lands, [after], "rs_exchange_ffn_wait")[3:]
        recv_wout = _to_sibling([oth], "rs_exchange_wout")
        outs = _chip_partial([own] + list(rs["ffn_own"]), list(recv_wout) + list(recv_ffn), chip_arr, "early")
        rs["early_own"] = outs[4:]
        rs["s"] = _split_start(_scatter_copies, 12, outs[:4],
                               [lax.empty((3,) + a.shape[1:], BF16) for a in outs[:4]], outs[4], "scatter_early_start")
        return rs["s"][4]

    small = (g_pre_mix + g_token[:1, :1], g_post_mix, g_pre_ffn, g_post_ffn, w_pool[0], pool_scale, rel_bias, sinks)
    loss, gx, small_grads, ((win_own, win_oth), _, _) = _local_step(
        x[0], loss_target[0], small, win_all.reshape(IN_W, D), wout_all.reshape(D, D), ffn_weights, c_arr,
        on_ffn_grads, on_wout_grads)
    ssem, rsem, srcs, lands, _ = rs["s"]
    recv_early = _split_wait(_scatter_copies, ssem, rsem, srcs, lands, [gx], "scatter_early_wait")[4:]

    unused = jnp.zeros((1, 1), F32)
    gp = _pack_small(small_grads + (loss,))
    wp = _pack_small((g_pre_mix, g_post_mix, g_pre_ffn, g_post_ffn, w_pool, pool_scale, rel_bias, sinks, unused))
    mp = _pack_small((m_g_pre_mix, m_g_post_mix, m_g_pre_ffn, m_g_post_ffn, m_w_pool, m_pool_scale, m_rel_bias,
                      m_sinks, unused))
    vp = _pack_small((v_g_pre_mix, v_g_post_mix, v_g_pre_ffn, v_g_post_ffn, v_w_pool, v_pool_scale, v_rel_bias,
                      v_sinks, unused))

    moments = (shards(m_w_in, m_w_out, m_w_gate, m_w_up, m_w_down),
               shards(v_w_in, v_w_out, v_w_gate, v_w_up, v_w_down))
    recv_a = _to_sibling([win_oth], "rs_exchange_win")
    outs = _chip_partial([win_own], recv_a, chip_arr, "win")
    w_ssem, w_rsem, w_srcs, w_lands, w_token = _split_start(
        _scatter_copies, 3, outs[:1], [lax.empty((3,) + outs[0].shape[1:], BF16)], gx, "scatter_win_start")
    slots = lax.dynamic_update_slice(lax.empty((8,) + gp.shape, F32), gp[None], (2 * chip + c, 0, 0))
    p_ssem, p_rsem, p_srcs, p_lands, p_token = _split_start(_peer_copies, 7, [gp], [slots], w_token,
                                                            "small_allgather_start")
    early_final = _sum_chips(list(rs["early_own"]), list(recv_early), "early", [p_token])
    early_sib = _to_sibling(early_final, "rs_share_early")
    adam_e = _adamw_shards(early_final, early_sib, big_w[1:], moments[0][1:], moments[1][1:], c_arr, "early")
    recv_win = _split_wait(_scatter_copies, w_ssem, w_rsem, w_srcs, w_lands, [adam_e[0]], "scatter_win_wait")[1:]
    win_final = _sum_chips([outs[1]], recv_win, "win")
    win_sib = _to_sibling(win_final, "rs_share_win")
    adam_w = _adamw_shards(win_final, win_sib, big_w[:1], moments[0][:1], moments[1][:1], c_arr, "win")
    big_g, big_d, big_m, big_v = [[adam_w[i]] + list(adam_e[4 * i:4 * i + 4]) for i in range(4)]

    gathered = _split_wait(_peer_copies, p_ssem, p_rsem, p_srcs, p_lands, [adam_w[0]], "small_allgather_wait")[1]
    small_out = [_unpack_small(p) for p in _small_sum_adamw(gathered, wp, mp, vp)]
    total_loss = small_out[0][8][0, 0]

    def ordered(small4, big4):
        sg1, sg2, sg3, sg4, swp, ssc, srb, ssk = small4[:8]
        bwin, bwout, bwg, bwu, bwd = big4[0].T[None], big4[1][None], big4[2].T[None], big4[3].T[None], big4[4][None]
        return [sg1, bwin, swp, ssc, srb, ssk, bwout, sg2, sg3, bwg, bwu, bwd, sg4]

    res = [total_loss, gx[None]]
    for sm, bg in zip(small_out, (big_g, big_d, big_m, big_v)):
        res += ordered(sm, bg)
    return tuple(res)
```

```python
import numpy as np
import jax
import jax.numpy as jnp
from jax import lax
from jax.experimental import pallas as pl
from jax.experimental.pallas import tpu as pltpu

F32 = jnp.float32
BF16 = jnp.bfloat16

D = 1024
POOL_W = 512
GROUP = 128
WINDOWS = (2, 4, 8, 16)
HALO = 128
NQ = 8
BLK = 128
NBUCKET = 32
IN_W = 1280
DFF = 2816
NSH = 4
FS = DFF // NSH
WIN_S = IN_W // NSH
WOUT_S = D // NSH
EPS = 1e-6
NEG = -1e30
SCALE = 0.125

ADAM_LR = 0.001
ADAM_B1 = 0.9
ADAM_B2 = 0.999
ADAM_EPS = 1e-08
ADAM_WD = 0.01
ADAM_STEP = 10

TM = 512
TM_POOL = 256
TM_FFN = 512
TM_FFN_FWD = 1024
FFN_ROWS = 256
VMEM_LIMIT = 60 * 1024 * 1024

MESH_T = pl.DeviceIdType.MESH
ANY = pl.BlockSpec(memory_space=pl.ANY)


def _cp(n_grid=0, **kw):
    sem = ("arbitrary",) * n_grid if n_grid else None
    return pltpu.CompilerParams(dimension_semantics=sem, vmem_limit_bytes=VMEM_LIMIT, **kw)


def _dot(a, b):
    return jnp.dot(a, b, preferred_element_type=F32)


def _dot_nt(a, b):
    return lax.dot_general(a, b, (((1,), (1,)), ((), ())), preferred_element_type=F32)


def _dot_tn(a, b):
    return lax.dot_general(a, b, (((0,), (0,)), ((), ())), preferred_element_type=F32)


def _rms(x):
    r = lax.rsqrt(jnp.mean(x * x, axis=-1, keepdims=True) + EPS)
    return r, x * r


def _rms_bwd(r, n, g, dout):
    dn = dout * g
    dx = r * (dn - n * jnp.mean(dn * n, axis=-1, keepdims=True))
    dg = jnp.sum(dout * n, axis=0, keepdims=True)
    return dx, dg


def _split(x, n):
    parts = []
    r = x
    for _ in range(n):
        p = r.astype(BF16)
        parts.append(p)
        r = r - p.astype(F32)
    return parts


def _band_dot(a, x, n):
    acc = None
    for p in _split(x, n):
        t = _dot(a, p)
        acc = t if acc is None else acc + t
    return acc


def _bucket_table():
    qi = np.arange(BLK)[None, :]
    kj = np.arange(2 * BLK)[:, None]
    dist = qi + BLK - kj
    n = np.maximum(dist, 0)
    nf = np.maximum(n, 1).astype(np.float32)
    large = 16 + (np.log(nf / np.float32(16)) / np.float32(np.log(128 / 16)) * np.float32(16)).astype(np.int32)
    large = np.minimum(large, NBUCKET - 1)
    return np.where(n < 16, n, large).astype(np.int32)


def _inproj_fwd(x, g1, w_in):
    s = x.shape[0]
    tm = min(TM, s)

    def body(x_ref, g_ref, w_ref, u_ref, q_ref, k_ref, v_ref):
        _, n = _rms(x_ref[...])
        h = (n * g_ref[...]).astype(BF16)
        proj = _dot_nt(h, w_ref[...])
        u_ref[...] = proj[:, :512]
        q_ref[...] = proj[:, 512:1024].astype(BF16)
        k_ref[...] = proj[:, 1024:1152].astype(BF16)
        v_ref[...] = proj[:, 1152:1280].astype(BF16)

    row = lambda w: pl.BlockSpec((tm, w), lambda i: (i, 0))
    return pl.pallas_call(
        body, name="inproj_fwd", grid=(s // tm,),
        in_specs=[row(D), pl.BlockSpec((1, D), lambda i: (0, 0)), pl.BlockSpec((IN_W, D), lambda i: (0, 0))],
        out_specs=[row(512), row(512), row(128), row(128)],
        out_shape=[jax.ShapeDtypeStruct((s, 512), F32), jax.ShapeDtypeStruct((s, 512), BF16),
                   jax.ShapeDtypeStruct((s, 128), BF16), jax.ShapeDtypeStruct((s, 128), BF16)],
        compiler_params=_cp(1))(x, g1, w_in)


def _pooled(ext, cur, t0, tm):
    rows = lax.broadcasted_iota(jnp.int32, (tm, tm + HALO), 0)
    cols = lax.broadcasted_iota(jnp.int32, (tm, tm + HALO), 1)
    d = rows + HALO - cols
    t = t0 + lax.broadcasted_iota(jnp.int32, (tm, GROUP), 0)
    out = []
    for g, w in enumerate(WINDOWS):
        a = jnp.where((d >= 0) & (d < w), 1.0, 0.0).astype(BF16)
        ws = _band_dot(a, ext[:, g * GROUP:(g + 1) * GROUP], 3)
        cnt = jnp.minimum(t + 1, w).astype(F32)
        out.append(ws / cnt - cur[:, g * GROUP:(g + 1) * GROUP])
    return out


def _pool_fwd(u, w_pool, pool_scale):
    s = u.shape[0]
    tm = min(TM_POOL, s)
    hb = tm // HALO

    def body(uc_ref, uh_ref, wp_ref, sc_ref, o_ref):
        i = pl.program_id(0)
        cur = uc_ref[...]
        halo = jnp.where(i > 0, uh_ref[...], 0.0)
        ext = jnp.concatenate([halo, cur], axis=0)
        pooled = _pooled(ext, cur, i * tm, tm)
        for g in range(4):
            sl = slice(g * GROUP, (g + 1) * GROUP)
            mixed = _dot(pooled[g].astype(BF16), wp_ref[g])
            o_ref[:, sl] = (mixed * sc_ref[:, sl]).astype(BF16)

    return pl.pallas_call(
        body, name="pool_fwd", grid=(s // tm,),
        in_specs=[pl.BlockSpec((tm, 512), lambda i: (i, 0)),
                  pl.BlockSpec((HALO, 512), lambda i: (jnp.maximum(i * hb - 1, 0), 0)),
                  pl.BlockSpec((4, GROUP, GROUP), lambda i: (0, 0, 0)),
                  pl.BlockSpec((1, 512), lambda i: (0, 0))],
        out_specs=pl.BlockSpec((tm, 512), lambda i: (i, 0)),
        out_shape=jax.ShapeDtypeStruct((s, 512), BF16),
        compiler_params=_cp(1))(u, u, w_pool, pool_scale)


def _bias_table(bucket, rel_bias):
    def body(b_ref, rb_ref, o_ref):
        bucket_v = b_ref[...]
        key = lax.broadcasted_iota(jnp.int32, (2 * BLK, BLK), 0)
        dist = lax.broadcasted_iota(jnp.int32, (2 * BLK, BLK), 1) + BLK - key
        inwin = (dist >= 0) & (dist < BLK)
        for h in range(NQ):
            acc = jnp.zeros((2 * BLK, BLK), F32)
            for b in range(NBUCKET):
                acc = jnp.where(bucket_v == b, rb_ref[b, h], acc)
            rest = jnp.where(inwin, acc, NEG)
            o_ref[1, h] = rest
            o_ref[0, h] = jnp.where(key >= BLK, rest, NEG)

    return pl.pallas_call(
        body, name="bias_table",
        in_specs=[pl.BlockSpec(memory_space=pltpu.VMEM), pl.BlockSpec(memory_space=pltpu.SMEM)],
        out_specs=pl.BlockSpec(memory_space=pltpu.VMEM),
        out_shape=jax.ShapeDtypeStruct((2, NQ, 2 * BLK, BLK), F32),
        compiler_params=_cp())(bucket, rel_bias)


def _kv_band(ref, n, transposed):
    pstart = pl.multiple_of(jnp.maximum(n - 1, 0) * BLK, BLK)
    cstart = pl.multiple_of(n * BLK, BLK)
    lo = lax.broadcasted_iota(jnp.int32, (2 * BLK, 128), 1) < 64
    band = jnp.concatenate([ref[pl.ds(pstart, BLK), :], ref[pl.ds(cstart, BLK), :]], axis=0).astype(F32)
    rot = pltpu.roll(band, 64, axis=1)
    dup = (jnp.where(lo, band, rot), jnp.where(lo, rot, band))
    plain = [d.astype(BF16) for d in dup]
    if not transposed:
        return plain, None, (lo, pstart, cstart)
    row_lo = lax.broadcasted_iota(jnp.int32, (128, 2 * BLK), 0) < 64
    tr = [[jnp.where(row_lo, d.T, 0.0).astype(BF16), jnp.where(row_lo, 0.0, d.T).astype(BF16)] for d in dup]
    return plain, tr, (lo, pstart, cstart)


def _attn_probs(qm, kk, bias, sink):
    s = _dot_nt(kk, qm) + bias
    m = jnp.maximum(jnp.max(s, axis=0, keepdims=True), sink)
    p = jnp.exp(s - m)
    es = jnp.exp(sink - m)
    inv = 1.0 / (jnp.sum(p, axis=0, keepdims=True) + es)
    return p * inv, es * inv


def _attn_fwd(q, k, v, bias, sinks):
    s = q.shape[0]

    def body(q_ref, k_ref, v_ref, b_ref, sk_ref, o_ref):
        n = pl.program_id(0)
        kk, _, _ = _kv_band(k_ref, n, False)
        _, vt, _ = _kv_band(v_ref, n, True)
        bidx = jnp.minimum(n, 1)
        lo_q = lax.broadcasted_iota(jnp.int32, (BLK, 128), 1) < 64
        for p in range(4):
            h = p // 2
            qt = q_ref[:, p * 128:(p + 1) * 128].astype(F32) * SCALE
            acc_t = jnp.zeros((128, BLK), F32)
            for e in range(2):
                head = 2 * p + e
                qm = jnp.where(lo_q if e == 0 else jnp.logical_not(lo_q), qt, 0.0).astype(BF16)
                prob, _ = _attn_probs(qm, kk[h], b_ref[bidx, head], sk_ref[0, head])
                acc_t = acc_t + _dot(vt[h][e], prob.astype(BF16))
            o_ref[:, p * 128:(p + 1) * 128] = acc_t.T.astype(BF16)

    return pl.pallas_call(
        body, name="attn_fwd", grid=(s // BLK,),
        in_specs=[pl.BlockSpec((BLK, 512), lambda i: (i, 0)),
                  pl.BlockSpec((s, 128), lambda i: (0, 0)),
                  pl.BlockSpec((s, 128), lambda i: (0, 0)),
                  pl.BlockSpec((2, NQ, 2 * BLK, BLK), lambda i: (0, 0, 0, 0)),
                  pl.BlockSpec(memory_space=pltpu.SMEM)],
        out_specs=pl.BlockSpec((BLK, 512), lambda i: (i, 0)),
        out_shape=jax.ShapeDtypeStruct((s, 512), BF16),
        compiler_params=_cp(1))(q, k, v, bias, sinks)


def _outproj_fwd(pool_o, attn_o, w_out, x, g2, g3):
    s = x.shape[0]
    tm = min(TM, s)

    def body(p_ref, a_ref, w_ref, x_ref, g2_ref, g3_ref, mix_ref, x1_ref, h2_ref):
        mix = _dot(p_ref[...], w_ref[0:512, :]) + _dot(a_ref[...], w_ref[512:1024, :])
        mix_ref[...] = mix
        _, n2 = _rms(mix)
        x1 = x_ref[...] + n2 * g2_ref[...]
        x1_ref[...] = x1
        _, n3 = _rms(x1)
        h2_ref[...] = (n3 * g3_ref[...]).astype(BF16)

    row = lambda w: pl.BlockSpec((tm, w), lambda i: (i, 0))
    vec = pl.BlockSpec((1, D), lambda i: (0, 0))
    return pl.pallas_call(
        body, name="outproj_fwd", grid=(s // tm,),
        in_specs=[row(512), row(512), pl.BlockSpec((D, D), lambda i: (0, 0)), row(D), vec, vec],
        out_specs=[row(D), row(D), row(D)],
        out_shape=[jax.ShapeDtypeStruct((s, D), F32), jax.ShapeDtypeStruct((s, D), F32),
                   jax.ShapeDtypeStruct((s, D), BF16)],
        compiler_params=_cp(1))(pool_o, attn_o, w_out, x, g2, g3)


def _silu_parts(g):
    sg = jax.nn.sigmoid(g)
    return sg, g * sg


def _ffn_fwd(h2, wg, wu, wd, x1, target, g4):
    s = h2.shape[0]
    tm = min(TM_FFN_FWD, s)

    def body(h_ref, wg_ref, wu_ref, wd_ref, x1_ref, t_ref, g4_ref,
             gate_ref, up_ref, df_ref, dy_ref, loss_ref, gg4_ref, f_acc):
        i = pl.program_id(0)
        j = pl.program_id(1)
        first = j == 0
        for r in range(0, tm, FFN_ROWS):
            rows = pl.ds(r, min(FFN_ROWS, tm))
            h = h_ref[rows, :]
            gate = _dot_nt(h, wg_ref[...])
            up = _dot_nt(h, wu_ref[...])
            gate_ref[rows, :] = gate.astype(BF16)
            up_ref[rows, :] = up.astype(BF16)
            _, sl = _silu_parts(gate)
            contrib = _dot((sl * up).astype(BF16), wd_ref[...])
            f_acc[rows, :] = jnp.where(first, contrib, f_acc[rows, :] + contrib)

        @pl.when((i == 0) & (j == 0))
        def _():
            loss_ref[...] = jnp.zeros_like(loss_ref)
            gg4_ref[...] = jnp.zeros_like(gg4_ref)

        @pl.when(j == NSH - 1)
        def _():
            r4, n4 = _rms(f_acc[...])
            g4v = g4_ref[...]
            err = x1_ref[...] + n4 * g4v - t_ref[...]
            loss_ref[...] += (0.5 / D) * jnp.sum(err * err).reshape(1, 1)
            dy = err * (1.0 / D)
            dy_ref[...] = dy
            df, dg = _rms_bwd(r4, n4, g4v, dy)
            gg4_ref[...] += dg
            df_ref[...] = df.astype(BF16)

    row = lambda w: pl.BlockSpec((tm, w), lambda i, j: (i, 0))
    sh = lambda r, c: pl.BlockSpec((None, r, c), lambda i, j: (j, 0, 0))
    act = pl.BlockSpec((None, tm, FS), lambda i, j: (j, i, 0))
    vec = pl.BlockSpec((1, D), lambda i, j: (0, 0))
    return pl.pallas_call(
        body, name="ffn_fwd", grid=(s // tm, NSH),
        in_specs=[row(D), sh(FS, D), sh(FS, D), sh(FS, D), row(D), row(D), vec],
        out_specs=[act, act, row(D), row(D), pl.BlockSpec((1, 1), lambda i, j: (0, 0)), vec],
        out_shape=[jax.ShapeDtypeStruct((NSH, s, FS), BF16), jax.ShapeDtypeStruct((NSH, s, FS), BF16),
                   jax.ShapeDtypeStruct((s, D), BF16), jax.ShapeDtypeStruct((s, D), F32),
                   jax.ShapeDtypeStruct((1, 1), F32), jax.ShapeDtypeStruct((1, D), F32)],
        scratch_shapes=[pltpu.VMEM((tm, D), F32)],
        compiler_params=_cp(2))(h2, wg, wu, wd, x1, target, g4)


def _ffn_bwd_act(df, gate, up, wg, wu, wd, dy, x1, mix, g3, g2):
    s = df.shape[0]
    tm = min(TM_FFN, s)

    def body(df_ref, gate_ref, up_ref, wg_ref, wu_ref, wd_ref, dy_ref, x1_ref, mix_ref, g3_ref, g2_ref,
             dgate_ref, dup_ref, dx1_ref, dmix_ref, gg3_ref, gg2_ref, acc):
        i = pl.program_id(0)
        j = pl.program_id(1)
        first = j == 0
        for r in range(0, tm, FFN_ROWS):
            rows = pl.ds(r, min(FFN_ROWS, tm))
            da = _dot_nt(df_ref[rows, :], wd_ref[...])
            g = gate_ref[rows, :].astype(F32)
            u = up_ref[rows, :].astype(F32)
            sg, sl = _silu_parts(g)
            dgate = (da * u * (sg * (1.0 + g * (1.0 - sg)))).astype(BF16)
            dup = (da * sl).astype(BF16)
            dgate_ref[rows, :] = dgate
            dup_ref[rows, :] = dup
            contrib = _dot(dgate, wg_ref[...]) + _dot(dup, wu_ref[...])
            acc[rows, :] = jnp.where(first, contrib, acc[rows, :] + contrib)

        @pl.when((i == 0) & (j == 0))
        def _():
            gg3_ref[...] = jnp.zeros_like(gg3_ref)
            gg2_ref[...] = jnp.zeros_like(gg2_ref)

        @pl.when(j == NSH - 1)
        def _():
            r3, n3 = _rms(x1_ref[...])
            dx1n, dg3 = _rms_bwd(r3, n3, g3_ref[...], acc[...])
            dx1 = dy_ref[...] + dx1n
            dx1_ref[...] = dx1
            gg3_ref[...] += dg3
            r2, n2 = _rms(mix_ref[...])
            dmix, dg2 = _rms_bwd(r2, n2, g2_ref[...], dx1)
            gg2_ref[...] += dg2
            dmix_ref[...] = dmix.astype(BF16)

    row = lambda w: pl.BlockSpec((tm, w), lambda i, j: (i, 0))
    sh = lambda r, c: pl.BlockSpec((None, r, c), lambda i, j: (j, 0, 0))
    act = pl.BlockSpec((None, tm, FS), lambda i, j: (j, i, 0))
    vec = pl.BlockSpec((1, D), lambda i, j: (0, 0))
    return pl.pallas_call(
        body, name="ffn_bwd_act", grid=(s // tm, NSH),
        in_specs=[row(D), act, act, sh(FS, D), sh(FS, D), sh(FS, D), row(D), row(D), row(D), vec, vec],
        out_specs=[act, act, row(D), row(D), vec, vec],
        out_shape=[jax.ShapeDtypeStruct((NSH, s, FS), BF16), jax.ShapeDtypeStruct((NSH, s, FS), BF16),
                   jax.ShapeDtypeStruct((s, D), F32), jax.ShapeDtypeStruct((s, D), BF16),
                   jax.ShapeDtypeStruct((1, D), F32), jax.ShapeDtypeStruct((1, D), F32)],
        scratch_shapes=[pltpu.VMEM((tm, D), F32)],
        compiler_params=_cp(2))(df, gate, up, wg, wu, wd, dy, x1, mix, g3, g2)


SMEM_SPEC = pl.BlockSpec(memory_space=pltpu.SMEM)


def _emit_halves(acc_ref, row0, rows, c, own_ref, oth_ref):
    half = rows // 2
    own_ref[...] = acc_ref[pl.ds(row0 + pl.multiple_of(c * half, 8), half), :]
    oth_ref[...] = acc_ref[pl.ds(row0 + pl.multiple_of((1 - c) * half, 8), half), :].astype(BF16)


def _half_shapes(rows):
    return [jax.ShapeDtypeStruct((NSH, rows // 2, D), F32), jax.ShapeDtypeStruct((NSH, rows // 2, D), BF16)]


def _ffn_bwd_w(h2, gate, up, dgate, dup, df, c_arr):
    s = h2.shape[0]
    tm = min(TM_FFN, s)
    nt = s // tm

    def body(c_ref, h_ref, gate_ref, up_ref, dgate_ref, dup_ref, df_ref, *rest):
        outs, accs = rest[:6], rest[6:]
        i = pl.program_id(1)
        @pl.when(i == 0)
        def _():
            for acc in accs:
                acc[...] = jnp.zeros_like(acc)

        h = h_ref[...]
        accs[0][...] += _dot_tn(dgate_ref[...], h)
        accs[1][...] += _dot_tn(dup_ref[...], h)
        _, sl = _silu_parts(gate_ref[...].astype(F32))
        a = (sl * up_ref[...].astype(F32)).astype(BF16)
        accs[2][...] += _dot_tn(a, df_ref[...])

        @pl.when(i == nt - 1)
        def _():
            for t, acc in enumerate(accs):
                _emit_halves(acc, 0, FS, c_ref[0], outs[2 * t], outs[2 * t + 1])

    row = lambda w: pl.BlockSpec((tm, w), lambda j, i: (i, 0))
    act = pl.BlockSpec((None, tm, FS), lambda j, i: (j, i, 0))
    half = pl.BlockSpec((None, FS // 2, D), lambda j, i: (j, 0, 0))
    return pl.pallas_call(
        body, name="ffn_bwd_w", grid=(NSH, nt),
        in_specs=[SMEM_SPEC, row(D), act, act, act, act, row(D)],
        out_specs=[half] * 6,
        out_shape=_half_shapes(FS) * 3,
        scratch_shapes=[pltpu.VMEM((FS, D), F32)] * 3,
        compiler_params=_cp(2))(c_arr, h2, gate, up, dgate, dup, df)


def _outproj_bwd(dmix, w_out, pool_o, attn_o, c_arr, dep=None):
    s = dmix.shape[0]
    tm = min(TM, s)
    nt = s // tm

    def body(c_ref, dm_ref, w_ref, p_ref, a_ref, *rest):
        dpool_ref, dattn_ref, own_ref, oth_ref, gw_ref = rest[-5:]
        i = pl.program_id(0)

        @pl.when(i == 0)
        def _():
            gw_ref[...] = jnp.zeros_like(gw_ref)

        dm = dm_ref[...]
        dpool_ref[...] = _dot_nt(dm, w_ref[0:512, :])
        dattn_ref[...] = _dot_nt(dm, w_ref[512:1024, :]).astype(BF16)
        gw_ref[0:512, :] += _dot_tn(p_ref[...], dm)
        gw_ref[512:1024, :] += _dot_tn(a_ref[...], dm)

        @pl.when(i == nt - 1)
        def _():
            for k in range(NSH):
                _emit_halves(gw_ref, k * WOUT_S, WOUT_S, c_ref[0], own_ref.at[k], oth_ref.at[k])

    row = lambda w: pl.BlockSpec((tm, w), lambda i: (i, 0))
    full = pl.BlockSpec((D, D), lambda i: (0, 0))
    half = pl.BlockSpec((NSH, WOUT_S // 2, D), lambda i: (0, 0, 0))
    return pl.pallas_call(
        body, name="outproj_bwd", grid=(nt,),
        in_specs=[SMEM_SPEC, row(D), full, row(512), row(512)] + ([] if dep is None else [ANY]),
        out_specs=[row(512), row(512), half, half],
        out_shape=[jax.ShapeDtypeStruct((s, 512), F32), jax.ShapeDtypeStruct((s, 512), BF16)] + _half_shapes(WOUT_S),
        scratch_shapes=[pltpu.VMEM((D, D), F32)],
        compiler_params=_cp(1))(c_arr, dmix, w_out, pool_o, attn_o, *([] if dep is None else [dep]))


def _pool_bwd(u, dpool, w_pool, pool_scale, dep=None):
    s = u.shape[0]
    tm = min(TM_POOL, s)
    hb = tm // HALO
    nt = s // tm
    last_halo = s // HALO - 1

    def body(uc_ref, uh_ref, dc_ref, dn_ref, wp_ref, sc_ref, *rest):
        du_ref, gwp_ref, gsc_ref = rest[-3:]
        i = pl.program_id(0)

        @pl.when(i == 0)
        def _():
            gwp_ref[...] = jnp.zeros_like(gwp_ref)
            gsc_ref[...] = jnp.zeros_like(gsc_ref)

        cur = uc_ref[...]
        halo = jnp.where(i > 0, uh_ref[...], 0.0)
        pooled = _pooled(jnp.concatenate([halo, cur], axis=0), cur, i * tm, tm)
        dcur = dc_ref[...]
        dnext = jnp.where(i < nt - 1, dn_ref[...], 0.0)
        dext = jnp.concatenate([dcur, dnext], axis=0)
        rows = lax.broadcasted_iota(jnp.int32, (tm, tm + HALO), 0)
        cols = lax.broadcasted_iota(jnp.int32, (tm, tm + HALO), 1)
        d = cols - rows
        t_ext = i * tm + lax.broadcasted_iota(jnp.int32, (tm + HALO, GROUP), 0)
        for g, w in enumerate(WINDOWS):
            sl = slice(g * GROUP, (g + 1) * GROUP)
            pb = pooled[g].astype(BF16)
            wp = wp_ref[g]
            mixed = _dot(pb, wp)
            gsc_ref[:, sl] += jnp.sum(dcur[:, sl] * mixed, axis=0, keepdims=True)
            dmixed = (dext[:, sl] * sc_ref[:, sl]).astype(BF16)
            gwp_ref[g] += _dot_tn(pb, dmixed[0:tm])
            dpooled = _dot_nt(dmixed, wp)
            z = dpooled / jnp.minimum(t_ext + 1, w).astype(F32)
            b = jnp.where((d >= 0) & (d < w), 1.0, 0.0).astype(BF16)
            du_ref[:, sl] = (_band_dot(b, z, 2) - dpooled[0:tm]).astype(BF16)

    return pl.pallas_call(
        body, name="pool_bwd", grid=(nt,),
        in_specs=[pl.BlockSpec((tm, 512), lambda i: (i, 0)),
                  pl.BlockSpec((HALO, 512), lambda i: (jnp.maximum(i * hb - 1, 0), 0)),
                  pl.BlockSpec((tm, 512), lambda i: (i, 0)),
                  pl.BlockSpec((HALO, 512), lambda i: (jnp.minimum((i + 1) * hb, last_halo), 0)),
                  pl.BlockSpec((4, GROUP, GROUP), lambda i: (0, 0, 0)),
                  pl.BlockSpec((1, 512), lambda i: (0, 0))] + ([] if dep is None else [ANY]),
        out_specs=[pl.BlockSpec((tm, 512), lambda i: (i, 0)),
                   pl.BlockSpec((4, GROUP, GROUP), lambda i: (0, 0, 0)),
                   pl.BlockSpec((1, 512), lambda i: (0, 0))],
        out_shape=[jax.ShapeDtypeStruct((s, 512), BF16), jax.ShapeDtypeStruct((4, GROUP, GROUP), F32),
                   jax.ShapeDtypeStruct((1, 512), F32)],
        compiler_params=_cp(1))(u, u, dpool, dpool, w_pool, pool_scale, *([] if dep is None else [dep]))


def _attn_bwd(q, k, v, do, bias, sinks, dep=None):
    s = q.shape[0]

    def body(q_ref, do_ref, k_ref, v_ref, b_ref, sk_ref, *rest):
        dq_ref, dk_ref, dv_ref, dss_ref, gsk_ref = rest[-5:]
        n = pl.program_id(0)

        @pl.when(n == 0)
        def _():
            dk_ref[...] = jnp.zeros_like(dk_ref)
            dv_ref[...] = jnp.zeros_like(dv_ref)
            dss_ref[...] = jnp.zeros_like(dss_ref)
            gsk_ref[...] = jnp.zeros_like(gsk_ref)

        kk, kt, (lo, pstart, cstart) = _kv_band(k_ref, n, True)
        vv, _, _ = _kv_band(v_ref, n, False)
        bidx = jnp.minimum(n, 1)
        lo_q = lax.broadcasted_iota(jnp.int32, (BLK, 128), 1) < 64
        dkk = [jnp.zeros((2 * BLK, 128), F32), jnp.zeros((2 * BLK, 128), F32)]
        dvv = [jnp.zeros((2 * BLK, 128), F32), jnp.zeros((2 * BLK, 128), F32)]
        for p in range(4):
            h = p // 2
            qt = q_ref[:, p * 128:(p + 1) * 128].astype(F32) * SCALE
            dot = do_ref[:, p * 128:(p + 1) * 128]
            dq_t = jnp.zeros((128, BLK), F32)
            for e in range(2):
                head = 2 * p + e
                sel_q = lo_q if e == 0 else jnp.logical_not(lo_q)
                qm = jnp.where(sel_q, qt, 0.0).astype(BF16)
                prob, ps = _attn_probs(qm, kk[h], b_ref[bidx, head], sk_ref[0, head])
                dom = jnp.where(sel_q, dot, jnp.zeros_like(dot))
                dp = _dot_nt(vv[h], dom)
                delta = jnp.sum(prob * dp, axis=0, keepdims=True)
                ds = prob * (dp - delta)
                gsk_ref[pl.ds(head, 1), :] += jnp.broadcast_to(-jnp.sum(ps * delta).reshape(1, 1), (1, 128))
                dss_ref[head] += ds
                dsb = ds.astype(BF16)
                dq_t = dq_t + _dot(kt[h][e], dsb)
                dkk[h] = dkk[h] + _dot(dsb, qm)
                dvv[h] = dvv[h] + _dot(prob.astype(BF16), dom)
            dq_ref[:, p * 128:(p + 1) * 128] = (dq_t.T * SCALE).astype(BF16)
        for acc, ref in ((dkk, dk_ref), (dvv, dv_ref)):
            f0 = acc[0] + pltpu.roll(acc[0], 64, axis=1)
            f1 = acc[1] + pltpu.roll(acc[1], 64, axis=1)
            band = jnp.where(lo, f0, f1)
            ref[pl.ds(pstart, BLK), :] += band[0:BLK]
            ref[pl.ds(cstart, BLK), :] += band[BLK:2 * BLK]

    blk = pl.BlockSpec((BLK, 512), lambda i: (i, 0))
    kv = pl.BlockSpec((s, 128), lambda i: (0, 0))
    return pl.pallas_call(
        body, name="attn_bwd", grid=(s // BLK,),
        in_specs=[blk, blk, kv, kv, pl.BlockSpec((2, NQ, 2 * BLK, BLK), lambda i: (0, 0, 0, 0)),
                  pl.BlockSpec(memory_space=pltpu.SMEM)] + ([] if dep is None else [ANY]),
        out_specs=[blk, kv, kv, pl.BlockSpec((NQ, 2 * BLK, BLK), lambda i: (0, 0, 0)),
                   pl.BlockSpec((NQ, 128), lambda i: (0, 0))],
        out_shape=[jax.ShapeDtypeStruct((s, 512), BF16), jax.ShapeDtypeStruct((s, 128), F32),
                   jax.ShapeDtypeStruct((s, 128), F32), jax.ShapeDtypeStruct((NQ, 2 * BLK, BLK), F32),
                   jax.ShapeDtypeStruct((NQ, 128), F32)],
        compiler_params=_cp(1))(q, do, k, v, bias, sinks, *([] if dep is None else [dep]))


def _relbias_grad(dss, bucket):
    def body(ds_ref, b_ref, o_ref):
        bucket_v = b_ref[...]
        lane = lax.broadcasted_iota(jnp.int32, (NQ, 128), 1)
        head_row = lax.broadcasted_iota(jnp.int32, (NQ, BLK), 0)
        acc = jnp.zeros((NQ, 128), F32)
        for b in range(NBUCKET):
            sel = bucket_v == b
            stack = jnp.zeros((NQ, BLK), F32)
            for h in range(NQ):
                col_sums = jnp.sum(jnp.where(sel, ds_ref[h], 0.0), axis=0, keepdims=True)
                stack = jnp.where(head_row == h, col_sums, stack)
            acc = acc + jnp.where(lane == b, jnp.sum(stack, axis=1, keepdims=True), 0.0)
        o_ref[...] = acc

    return pl.pallas_call(
        body, name="relbias_grad",
        in_specs=[pl.BlockSpec(memory_space=pltpu.VMEM), pl.BlockSpec(memory_space=pltpu.VMEM)],
        out_specs=pl.BlockSpec(memory_space=pltpu.VMEM),
        out_shape=jax.ShapeDtypeStruct((NQ, 128), F32),
        compiler_params=_cp())(dss, bucket)


def _inproj_bwd(x, g1, du, dq, dk, dv, w_in, dx1, c_arr, dep=None):
    s = x.shape[0]
    tm = min(TM, s)
    nt = s // tm
    cols = ((0, 512), (512, 1024), (1024, 1152), (1152, 1280))

    def body(c_ref, x_ref, g_ref, du_ref, dq_ref, dk_ref, dv_ref, w_ref, dx1_ref, *rest):
        gx_ref, own_ref, oth_ref, gg_ref, gw_ref = rest[-5:]
        i = pl.program_id(0)

        @pl.when(i == 0)
        def _():
            gw_ref[...] = jnp.zeros_like(gw_ref)
            gg_ref[...] = jnp.zeros_like(gg_ref)

        gv = g_ref[...]
        r1, n1 = _rms(x_ref[...])
        h = (n1 * gv).astype(BF16)
        parts = (du_ref[...], dq_ref[...], dk_ref[...].astype(BF16), dv_ref[...].astype(BF16))
        dh = None
        for (a, b), dpart in zip(cols, parts):
            t = _dot(dpart, w_ref[a:b, :])
            dh = t if dh is None else dh + t
            gw_ref[a:b, :] += _dot_tn(dpart, h)
        dx, dg = _rms_bwd(r1, n1, gv, dh)
        gg_ref[...] += dg
        gx_ref[...] = dx1_ref[...] + dx

        @pl.when(i == nt - 1)
        def _():
            for k in range(NSH):
                _emit_halves(gw_ref, k * WIN_S, WIN_S, c_ref[0], own_ref.at[k], oth_ref.at[k])

    row = lambda w: pl.BlockSpec((tm, w), lambda i: (i, 0))
    vec = pl.BlockSpec((1, D), lambda i: (0, 0))
    full = pl.BlockSpec((IN_W, D), lambda i: (0, 0))
    half = pl.BlockSpec((NSH, WIN_S // 2, D), lambda i: (0, 0, 0))
    return pl.pallas_call(
        body, name="inproj_bwd", grid=(nt,),
        in_specs=[SMEM_SPEC, row(D), vec, row(512), row(512), row(128), row(128), full, row(D)]
        + ([] if dep is None else [ANY]),
        out_specs=[row(D), half, half, vec],
        out_shape=[jax.ShapeDtypeStruct((s, D), F32)] + _half_shapes(WIN_S) + [jax.ShapeDtypeStruct((1, D), F32)],
        scratch_shapes=[pltpu.VMEM((IN_W, D), F32)],
        compiler_params=_cp(1))(c_arr, x, g1, du, dq, dk, dv, w_in, dx1, *([] if dep is None else [dep]))


def _local_step(x, target, small, w_in, w_out, ffn_weights, c_arr, on_ffn_grads=None, on_wout_grads=None,
                on_attn_grads=None):
    g1, g2, g3, g4, w_pool, pool_scale, rel_bias, sinks = small
    bucket = jnp.asarray(_bucket_table())
    wp_bf = w_pool.astype(BF16)
    u, q, k, v = _inproj_fwd(x, g1, w_in)
    pool_o = _pool_fwd(u, wp_bf, pool_scale)
    bias = _bias_table(bucket, rel_bias)
    attn_o = _attn_fwd(q, k, v, bias, sinks)
    g2_fwd = g2
    if callable(w_out):
        w_out, after = w_out(pool_o, attn_o)
        g2_fwd = g2 + after[:1, :1]
    mix, x1, h2 = _outproj_fwd(pool_o, attn_o, w_out, x, g2_fwd, g3)
    wg, wu, wd = ffn_weights(h2) if callable(ffn_weights) else ffn_weights
    gate, up, df, dy, loss, gg4 = _ffn_fwd(h2, wg, wu, wd, x1, target, g4)
    dgate, dup, dx1, dmix, gg3, gg2 = _ffn_bwd_act(df, gate, up, wg, wu, wd, dy, x1, mix, g3, g2)
    ffn_g = _ffn_bwd_w(h2, gate, up, dgate, dup, df, c_arr)
    dep = None if on_ffn_grads is None else on_ffn_grads(ffn_g)
    dpool, dattn, wout_own, wout_oth = _outproj_bwd(dmix, w_out, pool_o, attn_o, c_arr, dep)
    dep = None if on_wout_grads is None else on_wout_grads(wout_own, wout_oth, dpool)
    du, gwp, gsc = _pool_bwd(u, dpool, wp_bf, pool_scale, dep)
    dq, dk, dv, dss, gsk = _attn_bwd(q, k, v, dattn, bias, sinks, dep)
    grb = _relbias_grad(dss, bucket)
    dep = None if on_attn_grads is None else on_attn_grads([du, dq])
    gx, win_own, win_oth, gg1 = _inproj_bwd(x, g1, du, dq, dk, dv, w_in, dx1, c_arr, dep)
    small_grads = (gg1, gg2, gg3, gg4, gwp, gsc, grb[:, :NBUCKET].T, gsk[:, 0].reshape(1, NQ))
    return loss, gx, small_grads, ((win_own, win_oth), (wout_own, wout_oth), ffn_g)


def _place():
    x, y, c = lax.axis_index("x"), lax.axis_index("y"), lax.axis_index("c")
    chips = ((1 - x, y), (x, 1 - y), (1 - x, 1 - y))
    return x, y, c, chips


def _remote(src, dst, ssem, rsem, dev):
    return pltpu.make_async_remote_copy(src_ref=src, dst_ref=dst, send_sem=ssem, recv_sem=rsem,
                                        device_id=dev, device_id_type=MESH_T)


def _own_slot(shard):
    me = 2 * lax.axis_index("x") + lax.axis_index("y")
    return lax.dynamic_update_slice(lax.empty((NSH,) + shard.shape, shard.dtype), shard[None], (me, 0, 0))


def _all_gather_weights(shards, lands):
    nt = len(shards)

    def body(*refs):
        srcs, dsts = refs[:nt], refs[2 * nt:3 * nt]
        isend, irecv, dsend, drecv = refs[3 * nt:]
        x, y, c, chips = _place()
        me = 2 * x + y
        sib = (x, y, 1 - c)
        sends = []
        for t in range(nt):
            half = srcs[t].shape[0] // 2
            mine = pl.ds(pl.multiple_of(c * half, 16), half)
            for j, (px, py) in enumerate(chips):
                cp = _remote(srcs[t].at[mine], dsts[t].at[me, mine], isend.at[t, j], irecv.at[t, j], (px, py, c))
                cp.start()
                sends.append(cp)
        for t in range(nt):
            half = srcs[t].shape[0] // 2
            mine = pl.ds(pl.multiple_of(c * half, 16), half)
            for j, (px, py) in enumerate(chips):
                blk = dsts[t].at[2 * px + py, mine]
                _remote(blk, blk, isend.at[t, j], irecv.at[t, j], (px, py, c)).wait_recv()
                cp = _remote(blk, blk, dsend.at[t, j], drecv.at[t, j], sib)
                cp.start()
                sends.append(cp)
        for t in range(nt):
            half = srcs[t].shape[0] // 2
            other = pl.ds(pl.multiple_of((1 - c) * half, 16), half)
            for j, (px, py) in enumerate(chips):
                blk = dsts[t].at[2 * px + py, other]
                _remote(blk, blk, dsend.at[t, j], drecv.at[t, j], sib).wait_recv()
        for cp in sends:
            cp.wait_send()

    return pl.pallas_call(
        body, name="allgather_weights",
        in_specs=[ANY] * (2 * nt), out_specs=[ANY] * nt,
        out_shape=[jax.ShapeDtypeStruct(a.shape, a.dtype) for a in lands],
        input_output_aliases={nt + t: t for t in range(nt)},
        scratch_shapes=[pltpu.SemaphoreType.DMA((nt, 3)), pltpu.SemaphoreType.DMA((nt, 3)),
                        pltpu.SemaphoreType.DMA((nt, 3)), pltpu.SemaphoreType.DMA((nt, 3))],
        compiler_params=_cp())(*shards, *lands)


def _to_sibling(arrs, name):
    nt = len(arrs)

    def body(*refs):
        srcs, dsts = refs[:nt], refs[nt:2 * nt]
        ssem, rsem = refs[2 * nt:]
        x, y, c, _ = _place()
        cps = [_remote(srcs[t], dsts[t], ssem.at[t], rsem.at[t], (x, y, 1 - c)) for t in range(nt)]
        for cp in cps:
            cp.start()
        for cp in cps:
            cp.wait()

    return pl.pallas_call(
        body, name=name, in_specs=[ANY] * nt, out_specs=[ANY] * nt,
        out_shape=[jax.ShapeDtypeStruct(a.shape, a.dtype) for a in arrs],
        scratch_shapes=[pltpu.SemaphoreType.DMA((nt,)), pltpu.SemaphoreType.DMA((nt,))],
        compiler_params=_cp())(*arrs)


HBM_SPEC = pl.BlockSpec(memory_space=pltpu.HBM)
SEM_SPEC = pl.BlockSpec(memory_space=pltpu.SEMAPHORE)
DATAFLOW = pltpu.SideEffectType.DATAFLOW_SIDE_EFFECTING


def _gather_copies(srcs, lands, ssem, rsem):
    x, y, c, chips = _place()
    me = 2 * x + y
    out = []
    for t in range(len(srcs)):
        half = srcs[t].shape[0] // 2
        mine = pl.ds(pl.multiple_of(c * half, 16), half)
        for j, (px, py) in enumerate(chips):
            k = 3 * t + j
            send = _remote(srcs[t].at[mine], lands[t].at[me, mine], ssem.at[k], rsem.at[k], (px, py, c))
            blk = lands[t].at[2 * px + py, mine]
            out.append((send, _remote(blk, blk, ssem.at[k], rsem.at[k], (px, py, c))))
    return out


def _scatter_copies(srcs, lands, ssem, rsem):
    x, y, c, chips = _place()
    out = []
    for t in range(len(srcs)):
        for j, (px, py) in enumerate(chips):
            k = 3 * t + j
            send = _remote(srcs[t].at[2 * px + py], lands[t].at[j], ssem.at[k], rsem.at[k], (px, py, c))
            out.append((send, _remote(lands[t].at[j], lands[t].at[j], ssem.at[k], rsem.at[k], (px, py, c))))
    return out


def _sibling_copies(srcs, lands, ssem, rsem):
    x, y, c, _ = _place()
    sib = (x, y, 1 - c)
    return [(_remote(srcs[t], lands[t], ssem.at[t], rsem.at[t], sib),
             _remote(lands[t], lands[t], ssem.at[t], rsem.at[t], sib)) for t in range(len(srcs))]


def _forward_copies(srcs, lands, ssem, rsem):
    x, y, c, chips = _place()
    sib = (x, y, 1 - c)
    out = []
    for t in range(len(lands)):
        half = lands[t].shape[1] // 2
        mine = pl.ds(pl.multiple_of(c * half, 16), half)
        other = pl.ds(pl.multiple_of((1 - c) * half, 16), half)
        for j, (px, py) in enumerate(chips):
            k = 3 * t + j
            blk_m, blk_o = lands[t].at[2 * px + py, mine], lands[t].at[2 * px + py, other]
            out.append((_remote(blk_m, blk_m, ssem.at[k], rsem.at[k], sib),
                        _remote(blk_o, blk_o, ssem.at[k], rsem.at[k], sib)))
    return out


def _peer_copies(srcs, lands, ssem, rsem):
    x, y, c, _ = _place()
    me = 4 * x + 2 * y + c
    out = []
    for kk in range(1, 8):
        px, py, pc = x ^ (kk >> 2), y ^ ((kk >> 1) & 1), c ^ (kk & 1)
        send = _remote(srcs[0], lands[0].at[me], ssem.at[kk - 1], rsem.at[kk - 1], (px, py, pc))
        slot = lands[0].at[4 * px + 2 * py + pc]
        out.append((send, _remote(slot, slot, ssem.at[kk - 1], rsem.at[kk - 1], (px, py, pc))))
    return out


def _split_start(plan, ncopy, srcs, lands, after, name):
    ns, nbuf = len(srcs), len(srcs) + len(lands)

    def body(*refs):
        ssem, rsem, token = refs[nbuf + 1], refs[nbuf + 2], refs[-1]
        for send, _ in plan(refs[:ns], refs[ns:nbuf], ssem, rsem):
            send.start()
        token[...] = jnp.zeros_like(token)

    bufs = [pltpu.with_memory_space_constraint(a, pltpu.HBM) for a in list(srcs) + list(lands)]
    outs = pl.pallas_call(
        body, name=name, in_specs=[HBM_SPEC] * nbuf + [ANY],
        out_specs=[SEM_SPEC, SEM_SPEC] + [HBM_SPEC] * nbuf + [pl.BlockSpec(memory_space=pltpu.VMEM)],
        out_shape=[pltpu.SemaphoreType.DMA((ncopy,)), pltpu.SemaphoreType.DMA((ncopy,))]
        + [pltpu.HBM(a.shape, a.dtype) for a in bufs] + [jax.ShapeDtypeStruct((8, 128), F32)],
        input_output_aliases={i: 2 + i for i in range(nbuf)},
        compiler_params=pltpu.CompilerParams(has_side_effects=DATAFLOW))(*bufs, after)
    return outs[0], outs[1], outs[2:2 + ns], outs[2 + ns:2 + nbuf], outs[-1]


def _split_wait(plan, ssem, rsem, srcs, lands, after, name):
    ns, nbuf = len(srcs), len(srcs) + len(lands)

    def body(*refs):
        s_ref, r_ref = refs[nbuf], refs[nbuf + 1]
        for send, recv in plan(refs[:ns], refs[ns:nbuf], s_ref, r_ref):
            send.wait_send()
            recv.wait_recv()

    outs = pl.pallas_call(
        body, name=name, in_specs=[HBM_SPEC] * nbuf + [SEM_SPEC, SEM_SPEC] + [ANY] * len(after),
        out_specs=[HBM_SPEC] * nbuf,
        out_shape=[pltpu.HBM(a.shape, a.dtype) for a in list(srcs) + list(lands)],
        input_output_aliases={i: i for i in range(nbuf)},
        compiler_params=pltpu.CompilerParams(has_side_effects=DATAFLOW))(*srcs, *lands, ssem, rsem, *after)
    return outs


def _gather_finish(lands):
    nt = len(lands)

    def body(*refs):
        outs = refs[nt:2 * nt]
        dsend, drecv = refs[2 * nt:]
        x, y, c, chips = _place()
        sib = (x, y, 1 - c)
        sends = []
        for t in range(nt):
            half = outs[t].shape[1] // 2
            mine = pl.ds(pl.multiple_of(c * half, 16), half)
            for j, (px, py) in enumerate(chips):
                blk = outs[t].at[2 * px + py, mine]
                cp = _remote(blk, blk, dsend.at[t, j], drecv.at[t, j], sib)
                cp.start()
                sends.append(cp)
        for t in range(nt):
            half = outs[t].shape[1] // 2
            other = pl.ds(pl.multiple_of((1 - c) * half, 16), half)
            for j, (px, py) in enumerate(chips):
                blk = outs[t].at[2 * px + py, other]
                _remote(blk, blk, dsend.at[t, j], drecv.at[t, j], sib).wait_recv()
        for cp in sends:
            cp.wait_send()

    return pl.pallas_call(
        body, name="gather_finish", in_specs=[ANY] * nt, out_specs=[ANY] * nt,
        out_shape=[jax.ShapeDtypeStruct(a.shape, a.dtype) for a in lands],
        input_output_aliases={t: t for t in range(nt)},
        scratch_shapes=[pltpu.SemaphoreType.DMA((nt, 3)), pltpu.SemaphoreType.DMA((nt, 3))],
        compiler_params=_cp())(*lands)


def _chip_partial(owns, recv, chip_arr, tag):
    nt = len(owns)

    def body(chip_ref, *refs):
        k = pl.program_id(0)
        for t in range(nt):
            p = refs[t][...] + refs[nt + t][...].astype(F32)
            refs[2 * nt + t][...] = p.astype(BF16)

            @pl.when(k == chip_ref[0])
            def _():
                refs[3 * nt + t][...] = p

    blk_specs = [pl.BlockSpec((None,) + a.shape[1:], lambda k, chip_ref: (k, 0, 0)) for a in owns]
    own_specs = [pl.BlockSpec(a.shape[1:], lambda k, chip_ref: (0, 0)) for a in owns]
    return pl.pallas_call(
        body, name="rs_chip_partial_" + tag,
        grid_spec=pltpu.PrefetchScalarGridSpec(num_scalar_prefetch=1, grid=(NSH,), in_specs=blk_specs * 2,
                                               out_specs=blk_specs + own_specs),
        out_shape=[jax.ShapeDtypeStruct(a.shape, BF16) for a in owns]
        + [jax.ShapeDtypeStruct(a.shape[1:], F32) for a in owns],
        compiler_params=_cp(1))(chip_arr, *owns, *recv)


def _sum_chips(own, recv, tag, after=()):
    nt = len(own)

    def body(*refs):
        outs = refs[2 * nt + len(after):]
        for t in range(nt):
            r = refs[nt + t]
            outs[t][...] = ((refs[t][...] + r[0].astype(F32)) + r[1].astype(F32)) + r[2].astype(F32)

    vm = pl.BlockSpec(memory_space=pltpu.VMEM)
    return pl.pallas_call(
        body, name="rs_sum_chips_" + tag, in_specs=[vm] * (2 * nt) + [ANY] * len(after), out_specs=[vm] * nt,
        out_shape=[jax.ShapeDtypeStruct(a.shape, F32) for a in own],
        compiler_params=_cp())(*own, *recv, *after)


def _adamw_math(w, g, m, v):
    m = ADAM_B1 * m + (1.0 - ADAM_B1) * g
    v = ADAM_B2 * v + (1.0 - ADAM_B2) * (g * g)
    m_hat = m / (1.0 - ADAM_B1 ** ADAM_STEP)
    v_hat = v / (1.0 - ADAM_B2 ** ADAM_STEP)
    delta = -ADAM_LR * (m_hat / (jnp.sqrt(v_hat) + ADAM_EPS) + ADAM_WD * w)
    return delta, m, v


ADAM_SPLIT = 4


def _adamw_shards(own, sib, ws, ms, vs, c_arr, tag):
    nt = len(own)

    def body(c_ref, *refs):
        hh = pl.program_id(0)
        mine = hh == c_ref[0]
        for t in range(nt):
            g = jnp.where(mine, refs[t][...], refs[nt + t][...])
            delta, m, v = _adamw_math(refs[2 * nt + t][...], g, refs[3 * nt + t][...], refs[4 * nt + t][...])
            refs[5 * nt + t][...] = g
            refs[6 * nt + t][...] = delta
            refs[7 * nt + t][...] = m
            refs[8 * nt + t][...] = v

    def tile(a):
        return (a.shape[0] // ADAM_SPLIT, a.shape[1])

    half_specs = [pl.BlockSpec(tile(a), lambda hh, q, c_ref: (q, 0)) for a in own]
    full_specs = [pl.BlockSpec(tile(a), lambda hh, q, c_ref: (hh * ADAM_SPLIT + q, 0)) for a in own]
    return pl.pallas_call(
        body, name="adamw_shards_" + tag,
        grid_spec=pltpu.PrefetchScalarGridSpec(num_scalar_prefetch=1, grid=(2, ADAM_SPLIT),
                                               in_specs=half_specs * 2 + full_specs * 3, out_specs=full_specs * 4),
        out_shape=[jax.ShapeDtypeStruct(w.shape, F32) for w in ws] * 4,
        compiler_params=_cp(2))(c_arr, *own, *sib, *ws, *ms, *vs)


SMALL_TAIL_ROW = 552


def _small_sum_adamw(gathered, wp, mp, vp):
    rows = wp.shape[0]

    def body(g_ref, w_ref, m_ref, v_ref, *rest):
        outs, res = rest[:-1], rest[-1]
        g = g_ref[0]
        for d in range(1, 8):
            g = g + g_ref[d]
        delta, m, v = _adamw_math(w_ref[...], g, m_ref[...], v_ref[...])
        for kind, val in enumerate((g, delta, m, v)):
            res[kind] = val
            gains, w_pool_o, scale_o, tail_o = outs[7 * kind:7 * kind + 4], *outs[7 * kind + 4:7 * kind + 7]
            for t in range(4):
                for i in range(8):
                    gains[t][:, 128 * i:128 * (i + 1)] = res[kind, pl.ds(8 * t + i, 1), :]
            for grp in range(4):
                w_pool_o[grp] = res[kind, pl.ds(32 + GROUP * grp, GROUP), :]
            for i in range(4):
                scale_o[:, 128 * i:128 * (i + 1)] = res[kind, pl.ds(544 + i, 1), :]
            tail_o[...] = res[kind, pl.ds(SMALL_TAIL_ROW, rows - SMALL_TAIL_ROW), :]

    vm = pl.BlockSpec(memory_space=pltpu.VMEM)
    one_kind = [jax.ShapeDtypeStruct((1, D), F32)] * 4 + [
        jax.ShapeDtypeStruct((4, GROUP, GROUP), F32), jax.ShapeDtypeStruct((1, POOL_W), F32),
        jax.ShapeDtypeStruct((rows - SMALL_TAIL_ROW, 128), F32)]
    return pl.pallas_call(
        body, name="small_sum_adamw", in_specs=[vm] * 4, out_specs=[vm] * 28,
        out_shape=one_kind * 4,
        scratch_shapes=[pltpu.VMEM((4, rows, 128), F32)],
        compiler_params=_cp())(gathered, wp, mp, vp)


def _pack_small(parts):
    rows = []
    for a in parts:
        flat = a.reshape(-1)
        n = flat.shape[0]
        padded = -(-n // 1024) * 1024
        rows.append(jnp.pad(flat, (0, padded - n)).reshape(-1, 128))
    return jnp.concatenate(rows, axis=0)


def kernel(x, g_pre_mix, w_in, w_pool, pool_scale, rel_bias, sinks, w_out, g_post_mix, g_pre_ffn, w_gate, w_up, w_down, g_post_ffn, loss_target, m_g_pre_mix, m_w_in, m_w_pool, m_pool_scale, m_rel_bias, m_sinks, m_w_out, m_g_post_mix, m_g_pre_ffn, m_w_gate, m_w_up, m_w_down, m_g_post_ffn, v_g_pre_mix, v_w_in, v_w_pool, v_pool_scale, v_rel_bias, v_sinks, v_w_out, v_g_post_mix, v_g_pre_ffn, v_w_gate, v_w_up, v_w_down, v_g_post_ffn):
    c = lax.axis_index("c")
    chip = 2 * lax.axis_index("x") + lax.axis_index("y")

    def shards(a_in, a_out, a_gate, a_up, a_down):
        return (a_in[0].T, a_out[0], a_gate[0].T, a_up[0].T, a_down[0])

    c_arr = c.reshape(1).astype(jnp.int32)
    chip_arr = chip.reshape(1).astype(jnp.int32)

    big_w = shards(w_in, w_out, w_gate, w_up, w_down)
    big_bf = [w.astype(BF16) for w in big_w]
    (win_all,) = _all_gather_weights(big_bf[:1], [_own_slot(big_bf[0])])
    g_ssem, g_rsem, g_srcs, g_lands, g_token = _split_start(
        _gather_copies, 12, big_bf[1:], [_own_slot(a) for a in big_bf[1:]], win_all, "gather_rest_start")
    fw = {}

    def w_out_ready(*after):
        lands = _split_wait(_gather_copies, g_ssem, g_rsem, g_srcs, g_lands, after, "gather_rest_wait")[4:]
        (wout_all,) = _gather_finish(lands[:1])
        fw["f"] = _split_start(_forward_copies, 9, [], lands[1:], wout_all, "gather_forward_start")
        return wout_all.reshape(D, D), fw["f"][4]

    def ffn_weights(after):
        ssem, rsem, _, lands, _ = fw["f"]
        return _split_wait(_forward_copies, ssem, rsem, [], lands, [after], "gather_forward_wait")

    rs = {}

    def on_ffn_grads(ffn_g):
        oths = ffn_g[1::2]
        rs["ffn_own"] = ffn_g[0::2]
        rs["x"] = _split_start(_sibling_copies, 3, oths, [lax.empty(a.shape, BF16) for a in oths], ffn_g[0],
                               "rs_exchange_ffn_start")
        return rs["x"][4]

    def on_wout_grads(own, oth, after):
        ssem, rsem, srcs, lands, _ = rs["x"]
        recv_ffn = _split_wait(_sibling_copies, ssem, rsem, srcs, lands, [after], "rs_exchange_ffn_wait")[3:]
        recv_wout = _to_sibling([oth], "rs_exchange_wout")
        outs = _chip_partial([own] + list(rs["ffn_own"]), list(recv_wout) + list(recv_ffn), chip_arr, "early")
        rs["early_own"] = outs[4:]
        rs["s"] = _split_start(_scatter_copies, 12, outs[:4],
                               [lax.empty((3,) + a.shape[1:], BF16) for a in outs[:4]], outs[4], "scatter_early_start")
        return rs["s"][4]

    def on_attn_grads(after):
        ssem, rsem, srcs, lands, _ = rs["s"]
        recv_early = _split_wait(_scatter_copies, ssem, rsem, srcs, lands, after, "scatter_early_wait")[4:]
        finals = _sum_chips(list(rs["early_own"]), list(recv_early), "early")
        rs["sh"] = _split_start(_sibling_copies, 4, finals, [lax.empty(a.shape, F32) for a in finals], finals[0],
                                "rs_share_early_start")
        return rs["sh"][4]

    small = (g_pre_mix + g_token[:1, :1], g_post_mix, g_pre_ffn, g_post_ffn, w_pool[0], pool_scale, rel_bias, sinks)
    loss, gx, small_grads, ((win_own, win_oth), _, _) = _local_step(
        x[0], loss_target[0], small, win_all.reshape(IN_W, D), w_out_ready, ffn_weights, c_arr,
        on_ffn_grads, on_wout_grads, on_attn_grads)

    unused = jnp.zeros((1, 1), F32)
    gp = _pack_small(small_grads + (loss,))
    wp = _pack_small((g_pre_mix, g_post_mix, g_pre_ffn, g_post_ffn, w_pool, pool_scale, rel_bias, sinks, unused))
    mp = _pack_small((m_g_pre_mix, m_g_post_mix, m_g_pre_ffn, m_g_post_ffn, m_w_pool, m_pool_scale, m_rel_bias,
                      m_sinks, unused))
    vp = _pack_small((v_g_pre_mix, v_g_post_mix, v_g_pre_ffn, v_g_post_ffn, v_w_pool, v_pool_scale, v_rel_bias,
                      v_sinks, unused))

    moments = (shards(m_w_in, m_w_out, m_w_gate, m_w_up, m_w_down),
               shards(v_w_in, v_w_out, v_w_gate, v_w_up, v_w_down))
    recv_a = _to_sibling([win_oth], "rs_exchange_win")
    outs = _chip_partial([win_own], recv_a, chip_arr, "win")
    w_ssem, w_rsem, w_srcs, w_lands, w_token = _split_start(
        _scatter_copies, 3, outs[:1], [lax.empty((3,) + outs[0].shape[1:], BF16)], gx, "scatter_win_start")
    slots = lax.dynamic_update_slice(lax.empty((8,) + gp.shape, F32), gp[None], (2 * chip + c, 0, 0))
    p_ssem, p_rsem, p_srcs, p_lands, p_token = _split_start(_peer_copies, 7, [gp], [slots], w_token,
                                                            "small_allgather_start")
    ssem, rsem, srcs, lands, _ = rs["sh"]
    shared = _split_wait(_sibling_copies, ssem, rsem, srcs, lands, [p_token], "rs_share_early_wait")
    adam_e = _adamw_shards(shared[:4], shared[4:], big_w[1:], moments[0][1:], moments[1][1:], c_arr, "early")
    recv_win = _split_wait(_scatter_copies, w_ssem, w_rsem, w_srcs, w_lands, [adam_e[0]], "scatter_win_wait")[1:]
    win_final = _sum_chips([outs[1]], recv_win, "win")
    win_sib = _to_sibling(win_final, "rs_share_win")
    adam_w = _adamw_shards(win_final, win_sib, big_w[:1], moments[0][:1], moments[1][:1], c_arr, "win")
    big_g, big_d, big_m, big_v = [[adam_w[i]] + list(adam_e[4 * i:4 * i + 4]) for i in range(4)]

    gathered = _split_wait(_peer_copies, p_ssem, p_rsem, p_srcs, p_lands, [adam_w[0]], "small_allgather_wait")[1]
    flat = _small_sum_adamw(gathered, wp, mp, vp)
    small_out = [flat[7 * kind:7 * kind + 7] for kind in range(4)]
    total_loss = small_out[0][6][16, 0]

    def ordered(small7, big4):
        sg1, sg2, sg3, sg4, swp, ssc, tail = small7
        swp, srb, ssk = swp[None], tail[0:2].reshape(NBUCKET, NQ), tail[8:9, 0:NQ]
        bwin, bwout, bwg, bwu, bwd = big4[0].T[None], big4[1][None], big4[2].T[None], big4[3].T[None], big4[4][None]
        return [sg1, bwin, swp, ssc, srb, ssk, bwout, sg2, sg3, bwg, bwu, bwd, sg4]

    res = [total_loss, gx[None]]
    for sm, bg in zip(small_out, (big_g, big_d, big_m, big_v)):
        res += ordered(sm, bg)
    return tuple(res)
```

```python
import numpy as np
import jax
import jax.numpy as jnp
from jax import lax
from jax.experimental import pallas as pl
from jax.experimental.pallas import tpu as pltpu

F32 = jnp.float32
BF16 = jnp.bfloat16

D = 1024
POOL_W = 512
GROUP = 128
WINDOWS = (2, 4, 8, 16)
HALO = 128
NQ = 8
BLK = 128
NBUCKET = 32
IN_W = 1280
DFF = 2816
NSH = 4
FS = DFF // NSH
WIN_S = IN_W // NSH
WOUT_S = D // NSH
EPS = 1e-6
NEG = -1e30
SCALE = 0.125

ADAM_LR = 0.001
ADAM_B1 = 0.9
ADAM_B2 = 0.999
ADAM_EPS = 1e-08
ADAM_WD = 0.01
ADAM_STEP = 10

TM = 512
TM_POOL = 256
TM_FFN = 512
TM_FFN_FWD = 1024
FFN_ROWS = 256
VMEM_LIMIT = 60 * 1024 * 1024

MESH_T = pl.DeviceIdType.MESH
ANY = pl.BlockSpec(memory_space=pl.ANY)


def _cp(n_grid=0, **kw):
    sem = ("arbitrary",) * n_grid if n_grid else None
    return pltpu.CompilerParams(dimension_semantics=sem, vmem_limit_bytes=VMEM_LIMIT, **kw)


def _dot(a, b):
    return jnp.dot(a, b, preferred_element_type=F32)


def _dot_nt(a, b):
    return lax.dot_general(a, b, (((1,), (1,)), ((), ())), preferred_element_type=F32)


def _dot_tn(a, b):
    return lax.dot_general(a, b, (((0,), (0,)), ((), ())), preferred_element_type=F32)


def _rms(x):
    r = lax.rsqrt(jnp.mean(x * x, axis=-1, keepdims=True) + EPS)
    return r, x * r


def _rms_bwd(r, n, g, dout):
    dn = dout * g
    dx = r * (dn - n * jnp.mean(dn * n, axis=-1, keepdims=True))
    dg = jnp.sum(dout * n, axis=0, keepdims=True)
    return dx, dg


def _split(x, n):
    parts = []
    r = x
    for _ in range(n):
        p = r.astype(BF16)
        parts.append(p)
        r = r - p.astype(F32)
    return parts


def _band_dot(a, x, n):
    acc = None
    for p in _split(x, n):
        t = _dot(a, p)
        acc = t if acc is None else acc + t
    return acc


def _bucket_table():
    qi = np.arange(BLK)[None, :]
    kj = np.arange(2 * BLK)[:, None]
    dist = qi + BLK - kj
    n = np.maximum(dist, 0)
    nf = np.maximum(n, 1).astype(np.float32)
    large = 16 + (np.log(nf / np.float32(16)) / np.float32(np.log(128 / 16)) * np.float32(16)).astype(np.int32)
    large = np.minimum(large, NBUCKET - 1)
    return np.where(n < 16, n, large).astype(np.int32)


def _inproj_fwd(x, g1, w_in):
    s = x.shape[0]
    tm = min(TM, s)

    def body(x_ref, g_ref, w_ref, u_ref, q_ref, k_ref, v_ref):
        _, n = _rms(x_ref[...])
        h = (n * g_ref[...]).astype(BF16)
        proj = _dot_nt(h, w_ref[...])
        u_ref[...] = proj[:, :512]
        q_ref[...] = proj[:, 512:1024].astype(BF16)
        k_ref[...] = proj[:, 1024:1152].astype(BF16)
        v_ref[...] = proj[:, 1152:1280].astype(BF16)

    row = lambda w: pl.BlockSpec((tm, w), lambda i: (i, 0))
    return pl.pallas_call(
        body, name="inproj_fwd", grid=(s // tm,),
        in_specs=[row(D), pl.BlockSpec((1, D), lambda i: (0, 0)), pl.BlockSpec((IN_W, D), lambda i: (0, 0))],
        out_specs=[row(512), row(512), row(128), row(128)],
        out_shape=[jax.ShapeDtypeStruct((s, 512), F32), jax.ShapeDtypeStruct((s, 512), BF16),
                   jax.ShapeDtypeStruct((s, 128), BF16), jax.ShapeDtypeStruct((s, 128), BF16)],
        compiler_params=_cp(1))(x, g1, w_in)


def _window_sums(ext, w, lag_sign, tm, terms):
    rows = lax.broadcasted_iota(jnp.int32, (HALO, 2 * HALO), 0)
    cols = lax.broadcasted_iota(jnp.int32, (HALO, 2 * HALO), 1)
    d = rows + HALO - cols if lag_sign > 0 else cols - rows
    band = jnp.where((d >= 0) & (d < w), 1.0, 0.0).astype(BF16)
    return jnp.concatenate([_band_dot(band, ext[r:r + 2 * HALO], terms) for r in range(0, tm, HALO)], axis=0)


def _pooled(ext, cur, t0, tm):
    t = t0 + lax.broadcasted_iota(jnp.int32, (tm, GROUP), 0)
    out = []
    for g, w in enumerate(WINDOWS):
        sl = slice(g * GROUP, (g + 1) * GROUP)
        cnt = jnp.minimum(t + 1, w).astype(F32)
        out.append(_window_sums(ext[:, sl], w, 1, tm, 2) / cnt - cur[:, sl])
    return out


def _pool_fwd(u, w_pool, pool_scale):
    s = u.shape[0]
    tm = min(TM_POOL, s)
    hb = tm // HALO

    def body(uc_ref, uh_ref, wp_ref, sc_ref, o_ref):
        i = pl.program_id(0)
        cur = uc_ref[...]
        halo = jnp.where(i > 0, uh_ref[...], 0.0)
        ext = jnp.concatenate([halo, cur], axis=0)
        pooled = _pooled(ext, cur, i * tm, tm)
        for g in range(4):
            sl = slice(g * GROUP, (g + 1) * GROUP)
            mixed = _dot(pooled[g].astype(BF16), wp_ref[g])
            o_ref[:, sl] = (mixed * sc_ref[:, sl]).astype(BF16)

    return pl.pallas_call(
        body, name="pool_fwd", grid=(s // tm,),
        in_specs=[pl.BlockSpec((tm, 512), lambda i: (i, 0)),
                  pl.BlockSpec((HALO, 512), lambda i: (jnp.maximum(i * hb - 1, 0), 0)),
                  pl.BlockSpec((4, GROUP, GROUP), lambda i: (0, 0, 0)),
                  pl.BlockSpec((1, 512), lambda i: (0, 0))],
        out_specs=pl.BlockSpec((tm, 512), lambda i: (i, 0)),
        out_shape=jax.ShapeDtypeStruct((s, 512), BF16),
        compiler_params=_cp(1))(u, u, w_pool, pool_scale)


def _bias_table(bucket, rel_bias):
    def body(b_ref, rb_ref, o_ref):
        bucket_v = b_ref[...]
        key = lax.broadcasted_iota(jnp.int32, (2 * BLK, BLK), 0)
        dist = lax.broadcasted_iota(jnp.int32, (2 * BLK, BLK), 1) + BLK - key
        inwin = (dist >= 0) & (dist < BLK)
        for h in range(NQ):
            acc = jnp.zeros((2 * BLK, BLK), F32)
            for b in range(NBUCKET):
                acc = jnp.where(bucket_v == b, rb_ref[b, h], acc)
            rest = jnp.where(inwin, acc, NEG)
            o_ref[1, h] = rest
            o_ref[0, h] = jnp.where(key >= BLK, rest, NEG)

    return pl.pallas_call(
        body, name="bias_table",
        in_specs=[pl.BlockSpec(memory_space=pltpu.VMEM), pl.BlockSpec(memory_space=pltpu.SMEM)],
        out_specs=pl.BlockSpec(memory_space=pltpu.VMEM),
        out_shape=jax.ShapeDtypeStruct((2, NQ, 2 * BLK, BLK), F32),
        compiler_params=_cp())(bucket, rel_bias)


def _kv_band(ref, n, transposed):
    pstart = pl.multiple_of(jnp.maximum(n - 1, 0) * BLK, BLK)
    cstart = pl.multiple_of(n * BLK, BLK)
    lo = lax.broadcasted_iota(jnp.int32, (2 * BLK, 128), 1) < 64
    band = jnp.concatenate([ref[pl.ds(pstart, BLK), :], ref[pl.ds(cstart, BLK), :]], axis=0).astype(F32)
    rot = pltpu.roll(band, 64, axis=1)
    dup = (jnp.where(lo, band, rot), jnp.where(lo, rot, band))
    plain = [d.astype(BF16) for d in dup]
    if not transposed:
        return plain, None, (lo, pstart, cstart)
    row_lo = lax.broadcasted_iota(jnp.int32, (128, 2 * BLK), 0) < 64
    tr = [[jnp.where(row_lo, d.T, 0.0).astype(BF16), jnp.where(row_lo, 0.0, d.T).astype(BF16)] for d in dup]
    return plain, tr, (lo, pstart, cstart)


def _attn_probs(qm, kk, bias, sink):
    s = _dot_nt(kk, qm) + bias
    m = jnp.maximum(jnp.max(s, axis=0, keepdims=True), sink)
    p = jnp.exp(s - m)
    es = jnp.exp(sink - m)
    inv = 1.0 / (jnp.sum(p, axis=0, keepdims=True) + es)
    return p * inv, es * inv


def _attn_fwd(q, k, v, bias, sinks):
    s = q.shape[0]

    def body(q_ref, k_ref, v_ref, b_ref, sk_ref, o_ref):
        n = pl.program_id(0)
        kk, _, _ = _kv_band(k_ref, n, False)
        _, vt, _ = _kv_band(v_ref, n, True)
        bidx = jnp.minimum(n, 1)
        lo_q = lax.broadcasted_iota(jnp.int32, (BLK, 128), 1) < 64
        for p in range(4):
            h = p // 2
            qt = q_ref[:, p * 128:(p + 1) * 128].astype(F32) * SCALE
            acc_t = jnp.zeros((128, BLK), F32)
            for e in range(2):
                head = 2 * p + e
                qm = jnp.where(lo_q if e == 0 else jnp.logical_not(lo_q), qt, 0.0).astype(BF16)
                prob, _ = _attn_probs(qm, kk[h], b_ref[bidx, head], sk_ref[0, head])
                acc_t = acc_t + _dot(vt[h][e], prob.astype(BF16))
            o_ref[:, p * 128:(p + 1) * 128] = acc_t.T.astype(BF16)

    return pl.pallas_call(
        body, name="attn_fwd", grid=(s // BLK,),
        in_specs=[pl.BlockSpec((BLK, 512), lambda i: (i, 0)),
                  pl.BlockSpec((s, 128), lambda i: (0, 0)),
                  pl.BlockSpec((s, 128), lambda i: (0, 0)),
                  pl.BlockSpec((2, NQ, 2 * BLK, BLK), lambda i: (0, 0, 0, 0)),
                  pl.BlockSpec(memory_space=pltpu.SMEM)],
        out_specs=pl.BlockSpec((BLK, 512), lambda i: (i, 0)),
        out_shape=jax.ShapeDtypeStruct((s, 512), BF16),
        compiler_params=_cp(1))(q, k, v, bias, sinks)


def _outproj_fwd(pool_o, attn_o, w_out, x, g2, g3):
    s = x.shape[0]
    tm = min(TM, s)

    def body(p_ref, a_ref, w_ref, x_ref, g2_ref, g3_ref, mix_ref, x1_ref, h2_ref):
        mix = _dot(p_ref[...], w_ref[0:512, :]) + _dot(a_ref[...], w_ref[512:1024, :])
        mix_ref[...] = mix
        _, n2 = _rms(mix)
        x1 = x_ref[...] + n2 * g2_ref[...]
        x1_ref[...] = x1
        _, n3 = _rms(x1)
        h2_ref[...] = (n3 * g3_ref[...]).astype(BF16)

    row = lambda w: pl.BlockSpec((tm, w), lambda i: (i, 0))
    vec = pl.BlockSpec((1, D), lambda i: (0, 0))
    return pl.pallas_call(
        body, name="outproj_fwd", grid=(s // tm,),
        in_specs=[row(512), row(512), pl.BlockSpec((D, D), lambda i: (0, 0)), row(D), vec, vec],
        out_specs=[row(D), row(D), row(D)],
        out_shape=[jax.ShapeDtypeStruct((s, D), F32), jax.ShapeDtypeStruct((s, D), F32),
                   jax.ShapeDtypeStruct((s, D), BF16)],
        compiler_params=_cp(1))(pool_o, attn_o, w_out, x, g2, g3)


def _silu_parts(g):
    sg = jax.nn.sigmoid(g)
    return sg, g * sg


def _ffn_fwd(h2, wg, wu, wd, x1, target, g4):
    s = h2.shape[0]
    tm = min(TM_FFN_FWD, s)

    def body(h_ref, wg_ref, wu_ref, wd_ref, x1_ref, t_ref, g4_ref,
             gate_ref, up_ref, df_ref, dy_ref, loss_ref, gg4_ref, f_acc):
        i = pl.program_id(0)
        j = pl.program_id(1)
        first = j == 0
        for r in range(0, tm, FFN_ROWS):
            rows = pl.ds(r, min(FFN_ROWS, tm))
            h = h_ref[rows, :]
            gate = _dot_nt(h, wg_ref[...])
            up = _dot_nt(h, wu_ref[...])
            gate_ref[rows, :] = gate.astype(BF16)
            up_ref[rows, :] = up.astype(BF16)
            _, sl = _silu_parts(gate)
            contrib = _dot((sl * up).astype(BF16), wd_ref[...])
            f_acc[rows, :] = jnp.where(first, contrib, f_acc[rows, :] + contrib)

        @pl.when((i == 0) & (j == 0))
        def _():
            loss_ref[...] = jnp.zeros_like(loss_ref)
            gg4_ref[...] = jnp.zeros_like(gg4_ref)

        @pl.when(j == NSH - 1)
        def _():
            r4, n4 = _rms(f_acc[...])
            g4v = g4_ref[...]
            err = x1_ref[...] + n4 * g4v - t_ref[...]
            loss_ref[...] += (0.5 / D) * jnp.sum(err * err).reshape(1, 1)
            dy = err * (1.0 / D)
            dy_ref[...] = dy
            df, dg = _rms_bwd(r4, n4, g4v, dy)
            gg4_ref[...] += dg
            df_ref[...] = df.astype(BF16)

    row = lambda w: pl.BlockSpec((tm, w), lambda i, j: (i, 0))
    sh = lambda r, c: pl.BlockSpec((None, r, c), lambda i, j: (j, 0, 0))
    act = pl.BlockSpec((None, tm, FS), lambda i, j: (j, i, 0))
    vec = pl.BlockSpec((1, D), lambda i, j: (0, 0))
    return pl.pallas_call(
        body, name="ffn_fwd", grid=(s // tm, NSH),
        in_specs=[row(D), sh(FS, D), sh(FS, D), sh(FS, D), row(D), row(D), vec],
        out_specs=[act, act, row(D), row(D), pl.BlockSpec((1, 1), lambda i, j: (0, 0)), vec],
        out_shape=[jax.ShapeDtypeStruct((NSH, s, FS), BF16), jax.ShapeDtypeStruct((NSH, s, FS), BF16),
                   jax.ShapeDtypeStruct((s, D), BF16), jax.ShapeDtypeStruct((s, D), F32),
                   jax.ShapeDtypeStruct((1, 1), F32), jax.ShapeDtypeStruct((1, D), F32)],
        scratch_shapes=[pltpu.VMEM((tm, D), F32)],
        compiler_params=_cp(2))(h2, wg, wu, wd, x1, target, g4)


def _ffn_bwd_act(df, gate, up, wg, wu, wd, dy, x1, mix, g3, g2):
    s = df.shape[0]
    tm = min(TM_FFN, s)

    def body(df_ref, gate_ref, up_ref, wg_ref, wu_ref, wd_ref, dy_ref, x1_ref, mix_ref, g3_ref, g2_ref,
             dgate_ref, dup_ref, dx1_ref, dmix_ref, gg3_ref, gg2_ref, acc):
        i = pl.program_id(0)
        j = pl.program_id(1)
        first = j == 0
        for r in range(0, tm, FFN_ROWS):
            rows = pl.ds(r, min(FFN_ROWS, tm))
            da = _dot_nt(df_ref[rows, :], wd_ref[...])
            g = gate_ref[rows, :].astype(F32)
            u = up_ref[rows, :].astype(F32)
            sg, sl = _silu_parts(g)
            dgate = (da * u * (sg * (1.0 + g * (1.0 - sg)))).astype(BF16)
            dup = (da * sl).astype(BF16)
            dgate_ref[rows, :] = dgate
            dup_ref[rows, :] = dup
            contrib = _dot(dgate, wg_ref[...]) + _dot(dup, wu_ref[...])
            acc[rows, :] = jnp.where(first, contrib, acc[rows, :] + contrib)

        @pl.when((i == 0) & (j == 0))
        def _():
            gg3_ref[...] = jnp.zeros_like(gg3_ref)
            gg2_ref[...] = jnp.zeros_like(gg2_ref)

        @pl.when(j == NSH - 1)
        def _():
            r3, n3 = _rms(x1_ref[...])
            dx1n, dg3 = _rms_bwd(r3, n3, g3_ref[...], acc[...])
            dx1 = dy_ref[...] + dx1n
            dx1_ref[...] = dx1
            gg3_ref[...] += dg3
            r2, n2 = _rms(mix_ref[...])
            dmix, dg2 = _rms_bwd(r2, n2, g2_ref[...], dx1)
            gg2_ref[...] += dg2
            dmix_ref[...] = dmix.astype(BF16)

    row = lambda w: pl.BlockSpec((tm, w), lambda i, j: (i, 0))
    sh = lambda r, c: pl.BlockSpec((None, r, c), lambda i, j: (j, 0, 0))
    act = pl.BlockSpec((None, tm, FS), lambda i, j: (j, i, 0))
    vec = pl.BlockSpec((1, D), lambda i, j: (0, 0))
    return pl.pallas_call(
        body, name="ffn_bwd_act", grid=(s // tm, NSH),
        in_specs=[row(D), act, act, sh(FS, D), sh(FS, D), sh(FS, D), row(D), row(D), row(D), vec, vec],
        out_specs=[act, act, row(D), row(D), vec, vec],
        out_shape=[jax.ShapeDtypeStruct((NSH, s, FS), BF16), jax.ShapeDtypeStruct((NSH, s, FS), BF16),
                   jax.ShapeDtypeStruct((s, D), F32), jax.ShapeDtypeStruct((s, D), BF16),
                   jax.ShapeDtypeStruct((1, D), F32), jax.ShapeDtypeStruct((1, D), F32)],
        scratch_shapes=[pltpu.VMEM((tm, D), F32)],
        compiler_params=_cp(2))(df, gate, up, wg, wu, wd, dy, x1, mix, g3, g2)


SMEM_SPEC = pl.BlockSpec(memory_space=pltpu.SMEM)


def _emit_halves(acc_ref, row0, rows, c, own_ref, oth_ref):
    half = rows // 2
    own_ref[...] = acc_ref[pl.ds(row0 + pl.multiple_of(c * half, 8), half), :]
    oth_ref[...] = acc_ref[pl.ds(row0 + pl.multiple_of((1 - c) * half, 8), half), :].astype(BF16)


def _half_shapes(rows):
    return [jax.ShapeDtypeStruct((NSH, rows // 2, D), F32), jax.ShapeDtypeStruct((NSH, rows // 2, D), BF16)]


def _ffn_bwd_w(h2, gate, up, dgate, dup, df, c_arr):
    s = h2.shape[0]
    tm = min(TM_FFN_FWD, s)
    nt = s // tm

    def body(c_ref, h_ref, gate_ref, up_ref, dgate_ref, dup_ref, df_ref, *rest):
        outs, accs = rest[:6], rest[6:]
        i = pl.program_id(1)
        @pl.when(i == 0)
        def _():
            for acc in accs:
                acc[...] = jnp.zeros_like(acc)

        h = h_ref[...]
        accs[0][...] += _dot_tn(dgate_ref[...], h)
        accs[1][...] += _dot_tn(dup_ref[...], h)
        _, sl = _silu_parts(gate_ref[...].astype(F32))
        a = (sl * up_ref[...].astype(F32)).astype(BF16)
        accs[2][...] += _dot_tn(a, df_ref[...])

        @pl.when(i == nt - 1)
        def _():
            for t, acc in enumerate(accs):
                _emit_halves(acc, 0, FS, c_ref[0], outs[2 * t], outs[2 * t + 1])

    row = lambda w: pl.BlockSpec((tm, w), lambda j, i: (i, 0))
    act = pl.BlockSpec((None, tm, FS), lambda j, i: (j, i, 0))
    half = pl.BlockSpec((None, FS // 2, D), lambda j, i: (j, 0, 0))
    return pl.pallas_call(
        body, name="ffn_bwd_w", grid=(NSH, nt),
        in_specs=[SMEM_SPEC, row(D), act, act, act, act, row(D)],
        out_specs=[half] * 6,
        out_shape=_half_shapes(FS) * 3,
        scratch_shapes=[pltpu.VMEM((FS, D), F32)] * 3,
        compiler_params=_cp(2))(c_arr, h2, gate, up, dgate, dup, df)


def _outproj_bwd(dmix, w_out, pool_o, attn_o, c_arr, dep=None):
    s = dmix.shape[0]
    tm = min(TM, s)
    nt = s // tm

    def body(c_ref, dm_ref, w_ref, p_ref, a_ref, *rest):
        dpool_ref, dattn_ref, own_ref, oth_ref, gw_ref = rest[-5:]
        i = pl.program_id(0)

        @pl.when(i == 0)
        def _():
            gw_ref[...] = jnp.zeros_like(gw_ref)

        dm = dm_ref[...]
        dpool_ref[...] = _dot_nt(dm, w_ref[0:512, :])
        dattn_ref[...] = _dot_nt(dm, w_ref[512:1024, :]).astype(BF16)
        gw_ref[0:512, :] += _dot_tn(p_ref[...], dm)
        gw_ref[512:1024, :] += _dot_tn(a_ref[...], dm)

        @pl.when(i == nt - 1)
        def _():
            for k in range(NSH):
                _emit_halves(gw_ref, k * WOUT_S, WOUT_S, c_ref[0], own_ref.at[k], oth_ref.at[k])

    row = lambda w: pl.BlockSpec((tm, w), lambda i: (i, 0))
    full = pl.BlockSpec((D, D), lambda i: (0, 0))
    half = pl.BlockSpec((NSH, WOUT_S // 2, D), lambda i: (0, 0, 0))
    return pl.pallas_call(
        body, name="outproj_bwd", grid=(nt,),
        in_specs=[SMEM_SPEC, row(D), full, row(512), row(512)] + ([] if dep is None else [ANY]),
        out_specs=[row(512), row(512), half, half],
        out_shape=[jax.ShapeDtypeStruct((s, 512), F32), jax.ShapeDtypeStruct((s, 512), BF16)] + _half_shapes(WOUT_S),
        scratch_shapes=[pltpu.VMEM((D, D), F32)],
        compiler_params=_cp(1))(c_arr, dmix, w_out, pool_o, attn_o, *([] if dep is None else [dep]))


def _pool_bwd(u, dpool, w_pool, pool_scale, dep=None):
    s = u.shape[0]
    tm = min(TM_POOL, s)
    hb = tm // HALO
    nt = s // tm
    last_halo = s // HALO - 1

    def body(uc_ref, uh_ref, dc_ref, dn_ref, wp_ref, sc_ref, *rest):
        du_ref, gwp_ref, gsc_ref = rest[-3:]
        i = pl.program_id(0)

        @pl.when(i == 0)
        def _():
            gwp_ref[...] = jnp.zeros_like(gwp_ref)
            gsc_ref[...] = jnp.zeros_like(gsc_ref)

        cur = uc_ref[...]
        halo = jnp.where(i > 0, uh_ref[...], 0.0)
        pooled = _pooled(jnp.concatenate([halo, cur], axis=0), cur, i * tm, tm)
        dcur = dc_ref[...]
        dnext = jnp.where(i < nt - 1, dn_ref[...], 0.0)
        dext = jnp.concatenate([dcur, dnext], axis=0)
        t_ext = i * tm + lax.broadcasted_iota(jnp.int32, (tm + HALO, GROUP), 0)
        for g, w in enumerate(WINDOWS):
            sl = slice(g * GROUP, (g + 1) * GROUP)
            pb = pooled[g].astype(BF16)
            wp = wp_ref[g]
            mixed = _dot(pb, wp)
            gsc_ref[:, sl] += jnp.sum(dcur[:, sl] * mixed, axis=0, keepdims=True)
            dmixed = (dext[:, sl] * sc_ref[:, sl]).astype(BF16)
            gwp_ref[g] += _dot_tn(pb, dmixed[0:tm])
            dpooled = _dot_nt(dmixed, wp)
            z = dpooled / jnp.minimum(t_ext + 1, w).astype(F32)
            du_ref[:, sl] = (_window_sums(z, w, -1, tm, 2) - dpooled[0:tm]).astype(BF16)

    return pl.pallas_call(
        body, name="pool_bwd", grid=(nt,),
        in_specs=[pl.BlockSpec((tm, 512), lambda i: (i, 0)),
                  pl.BlockSpec((HALO, 512), lambda i: (jnp.maximum(i * hb - 1, 0), 0)),
                  pl.BlockSpec((tm, 512), lambda i: (i, 0)),
                  pl.BlockSpec((HALO, 512), lambda i: (jnp.minimum((i + 1) * hb, last_halo), 0)),
                  pl.BlockSpec((4, GROUP, GROUP), lambda i: (0, 0, 0)),
                  pl.BlockSpec((1, 512), lambda i: (0, 0))] + ([] if dep is None else [ANY]),
        out_specs=[pl.BlockSpec((tm, 512), lambda i: (i, 0)),
                   pl.BlockSpec((4, GROUP, GROUP), lambda i: (0, 0, 0)),
                   pl.BlockSpec((1, 512), lambda i: (0, 0))],
        out_shape=[jax.ShapeDtypeStruct((s, 512), BF16), jax.ShapeDtypeStruct((4, GROUP, GROUP), F32),
                   jax.ShapeDtypeStruct((1, 512), F32)],
        compiler_params=_cp(1))(u, u, dpool, dpool, w_pool, pool_scale, *([] if dep is None else [dep]))


def _attn_bwd(q, k, v, do, bias, sinks, dep=None):
    s = q.shape[0]

    def body(q_ref, do_ref, k_ref, v_ref, b_ref, sk_ref, *rest):
        dq_ref, dk_ref, dv_ref, dss_ref, gsk_ref = rest[-5:]
        n = pl.program_id(0)

        @pl.when(n == 0)
        def _():
            dk_ref[...] = jnp.zeros_like(dk_ref)
            dv_ref[...] = jnp.zeros_like(dv_ref)
            dss_ref[...] = jnp.zeros_like(dss_ref)
            gsk_ref[...] = jnp.zeros_like(gsk_ref)

        kk, kt, (lo, pstart, cstart) = _kv_band(k_ref, n, True)
        vv, _, _ = _kv_band(v_ref, n, False)
        bidx = jnp.minimum(n, 1)
        lo_q = lax.broadcasted_iota(jnp.int32, (BLK, 128), 1) < 64
        dkk = [jnp.zeros((2 * BLK, 128), F32), jnp.zeros((2 * BLK, 128), F32)]
        dvv = [jnp.zeros((2 * BLK, 128), F32), jnp.zeros((2 * BLK, 128), F32)]
        for p in range(4):
            h = p // 2
            qt = q_ref[:, p * 128:(p + 1) * 128].astype(F32) * SCALE
            dot = do_ref[:, p * 128:(p + 1) * 128]
            dq_t = jnp.zeros((128, BLK), F32)
            for e in range(2):
                head = 2 * p + e
                sel_q = lo_q if e == 0 else jnp.logical_not(lo_q)
                qm = jnp.where(sel_q, qt, 0.0).astype(BF16)
                prob, ps = _attn_probs(qm, kk[h], b_ref[bidx, head], sk_ref[0, head])
                dom = jnp.where(sel_q, dot, jnp.zeros_like(dot))
                dp = _dot_nt(vv[h], dom)
                delta = jnp.sum(prob * dp, axis=0, keepdims=True)
                ds = prob * (dp - delta)
                gsk_ref[pl.ds(head, 1), :] += jnp.broadcast_to(-jnp.sum(ps * delta).reshape(1, 1), (1, 128))
                dss_ref[head] += ds
                dsb = ds.astype(BF16)
                dq_t = dq_t + _dot(kt[h][e], dsb)
                dkk[h] = dkk[h] + _dot(dsb, qm)
                dvv[h] = dvv[h] + _dot(prob.astype(BF16), dom)
            dq_ref[:, p * 128:(p + 1) * 128] = (dq_t.T * SCALE).astype(BF16)
        for acc, ref in ((dkk, dk_ref), (dvv, dv_ref)):
            f0 = acc[0] + pltpu.roll(acc[0], 64, axis=1)
            f1 = acc[1] + pltpu.roll(acc[1], 64, axis=1)
            band = jnp.where(lo, f0, f1)
            ref[pl.ds(pstart, BLK), :] += band[0:BLK]
            ref[pl.ds(cstart, BLK), :] += band[BLK:2 * BLK]

    blk = pl.BlockSpec((BLK, 512), lambda i: (i, 0))
    kv = pl.BlockSpec((s, 128), lambda i: (0, 0))
    return pl.pallas_call(
        body, name="attn_bwd", grid=(s // BLK,),
        in_specs=[blk, blk, kv, kv, pl.BlockSpec((2, NQ, 2 * BLK, BLK), lambda i: (0, 0, 0, 0)),
                  pl.BlockSpec(memory_space=pltpu.SMEM)] + ([] if dep is None else [ANY]),
        out_specs=[blk, kv, kv, pl.BlockSpec((NQ, 2 * BLK, BLK), lambda i: (0, 0, 0)),
                   pl.BlockSpec((NQ, 128), lambda i: (0, 0))],
        out_shape=[jax.ShapeDtypeStruct((s, 512), BF16), jax.ShapeDtypeStruct((s, 128), F32),
                   jax.ShapeDtypeStruct((s, 128), F32), jax.ShapeDtypeStruct((NQ, 2 * BLK, BLK), F32),
                   jax.ShapeDtypeStruct((NQ, 128), F32)],
        compiler_params=_cp(1))(q, do, k, v, bias, sinks, *([] if dep is None else [dep]))


def _relbias_grad(dss, bucket):
    def body(ds_ref, b_ref, o_ref):
        bucket_v = b_ref[...]
        lane = lax.broadcasted_iota(jnp.int32, (NQ, 128), 1)
        head_row = lax.broadcasted_iota(jnp.int32, (NQ, BLK), 0)
        acc = jnp.zeros((NQ, 128), F32)
        for b in range(NBUCKET):
            sel = bucket_v == b
            stack = jnp.zeros((NQ, BLK), F32)
            for h in range(NQ):
                col_sums = jnp.sum(jnp.where(sel, ds_ref[h], 0.0), axis=0, keepdims=True)
                stack = jnp.where(head_row == h, col_sums, stack)
            acc = acc + jnp.where(lane == b, jnp.sum(stack, axis=1, keepdims=True), 0.0)
        o_ref[...] = acc

    return pl.pallas_call(
        body, name="relbias_grad",
        in_specs=[pl.BlockSpec(memory_space=pltpu.VMEM), pl.BlockSpec(memory_space=pltpu.VMEM)],
        out_specs=pl.BlockSpec(memory_space=pltpu.VMEM),
        out_shape=jax.ShapeDtypeStruct((NQ, 128), F32),
        compiler_params=_cp())(dss, bucket)


def _inproj_bwd(x, g1, du, dq, dk, dv, w_in, dx1, c_arr, dep=None):
    s = x.shape[0]
    tm = min(TM, s)
    nt = s // tm
    cols = ((0, 512), (512, 1024), (1024, 1152), (1152, 1280))

    def body(c_ref, x_ref, g_ref, du_ref, dq_ref, dk_ref, dv_ref, w_ref, dx1_ref, *rest):
        gx_ref, own_ref, oth_ref, gg_ref, gw_ref = rest[-5:]
        i = pl.program_id(0)

        @pl.when(i == 0)
        def _():
            gw_ref[...] = jnp.zeros_like(gw_ref)
            gg_ref[...] = jnp.zeros_like(gg_ref)

        gv = g_ref[...]
        r1, n1 = _rms(x_ref[...])
        h = (n1 * gv).astype(BF16)
        parts = (du_ref[...], dq_ref[...], dk_ref[...].astype(BF16), dv_ref[...].astype(BF16))
        dh = None
        for (a, b), dpart in zip(cols, parts):
            t = _dot(dpart, w_ref[a:b, :])
            dh = t if dh is None else dh + t
            gw_ref[a:b, :] += _dot_tn(dpart, h)
        dx, dg = _rms_bwd(r1, n1, gv, dh)
        gg_ref[...] += dg
        gx_ref[...] = dx1_ref[...] + dx

        @pl.when(i == nt - 1)
        def _():
            for k in range(NSH):
                _emit_halves(gw_ref, k * WIN_S, WIN_S, c_ref[0], own_ref.at[k], oth_ref.at[k])

    row = lambda w: pl.BlockSpec((tm, w), lambda i: (i, 0))
    vec = pl.BlockSpec((1, D), lambda i: (0, 0))
    full = pl.BlockSpec((IN_W, D), lambda i: (0, 0))
    half = pl.BlockSpec((NSH, WIN_S // 2, D), lambda i: (0, 0, 0))
    return pl.pallas_call(
        body, name="inproj_bwd", grid=(nt,),
        in_specs=[SMEM_SPEC, row(D), vec, row(512), row(512), row(128), row(128), full, row(D)]
        + ([] if dep is None else [ANY]),
        out_specs=[row(D), half, half, vec],
        out_shape=[jax.ShapeDtypeStruct((s, D), F32)] + _half_shapes(WIN_S) + [jax.ShapeDtypeStruct((1, D), F32)],
        scratch_shapes=[pltpu.VMEM((IN_W, D), F32)],
        compiler_params=_cp(1))(c_arr, x, g1, du, dq, dk, dv, w_in, dx1, *([] if dep is None else [dep]))


def _local_step(x, target, small, w_in, w_out, ffn_weights, c_arr, on_ffn_grads=None, on_wout_grads=None,
                on_attn_grads=None):
    g1, g2, g3, g4, w_pool, pool_scale, rel_bias, sinks = small
    bucket = jnp.asarray(_bucket_table())
    wp_bf = w_pool.astype(BF16)
    u, q, k, v = _inproj_fwd(x, g1, w_in)
    pool_o = _pool_fwd(u, wp_bf, pool_scale)
    bias = _bias_table(bucket, rel_bias)
    attn_o = _attn_fwd(q, k, v, bias, sinks)
    g2_fwd = g2
    if callable(w_out):
        w_out, after = w_out(pool_o, attn_o)
        g2_fwd = g2 + after[:1, :1]
    mix, x1, h2 = _outproj_fwd(pool_o, attn_o, w_out, x, g2_fwd, g3)
    wg, wu, wd = ffn_weights(h2) if callable(ffn_weights) else ffn_weights
    gate, up, df, dy, loss, gg4 = _ffn_fwd(h2, wg, wu, wd, x1, target, g4)
    dgate, dup, dx1, dmix, gg3, gg2 = _ffn_bwd_act(df, gate, up, wg, wu, wd, dy, x1, mix, g3, g2)
    ffn_g = _ffn_bwd_w(h2, gate, up, dgate, dup, df, c_arr)
    dep = None if on_ffn_grads is None else on_ffn_grads(ffn_g)
    dpool, dattn, wout_own, wout_oth = _outproj_bwd(dmix, w_out, pool_o, attn_o, c_arr, dep)
    dep = None if on_wout_grads is None else on_wout_grads(wout_own, wout_oth, dpool)
    du, gwp, gsc = _pool_bwd(u, dpool, wp_bf, pool_scale, dep)
    dq, dk, dv, dss, gsk = _attn_bwd(q, k, v, dattn, bias, sinks, dep)
    grb = _relbias_grad(dss, bucket)
    dep = None if on_attn_grads is None else on_attn_grads([du, dq])
    gx, win_own, win_oth, gg1 = _inproj_bwd(x, g1, du, dq, dk, dv, w_in, dx1, c_arr, dep)
    small_grads = (gg1, gg2, gg3, gg4, gwp, gsc, grb[:, :NBUCKET].T, gsk[:, 0].reshape(1, NQ))
    return loss, gx, small_grads, ((win_own, win_oth), (wout_own, wout_oth), ffn_g)


def _place():
    x, y, c = lax.axis_index("x"), lax.axis_index("y"), lax.axis_index("c")
    chips = ((1 - x, y), (x, 1 - y), (1 - x, 1 - y))
    return x, y, c, chips


def _remote(src, dst, ssem, rsem, dev):
    return pltpu.make_async_remote_copy(src_ref=src, dst_ref=dst, send_sem=ssem, recv_sem=rsem,
                                        device_id=dev, device_id_type=MESH_T)


def _own_slot(shard):
    me = 2 * lax.axis_index("x") + lax.axis_index("y")
    return lax.dynamic_update_slice(lax.empty((NSH,) + shard.shape, shard.dtype), shard[None], (me, 0, 0))


def _all_gather_weights(shards, lands):
    nt = len(shards)

    def body(*refs):
        srcs, dsts = refs[:nt], refs[2 * nt:3 * nt]
        isend, irecv, dsend, drecv = refs[3 * nt:]
        x, y, c, chips = _place()
        me = 2 * x + y
        sib = (x, y, 1 - c)
        sends = []
        for t in range(nt):
            half = srcs[t].shape[0] // 2
            mine = pl.ds(pl.multiple_of(c * half, 16), half)
            for j, (px, py) in enumerate(chips):
                cp = _remote(srcs[t].at[mine], dsts[t].at[me, mine], isend.at[t, j], irecv.at[t, j], (px, py, c))
                cp.start()
                sends.append(cp)
        for t in range(nt):
            half = srcs[t].shape[0] // 2
            mine = pl.ds(pl.multiple_of(c * half, 16), half)
            for j, (px, py) in enumerate(chips):
                blk = dsts[t].at[2 * px + py, mine]
                _remote(blk, blk, isend.at[t, j], irecv.at[t, j], (px, py, c)).wait_recv()
                cp = _remote(blk, blk, dsend.at[t, j], drecv.at[t, j], sib)
                cp.start()
                sends.append(cp)
        for t in range(nt):
            half = srcs[t].shape[0] // 2
            other = pl.ds(pl.multiple_of((1 - c) * half, 16), half)
            for j, (px, py) in enumerate(chips):
                blk = dsts[t].at[2 * px + py, other]
                _remote(blk, blk, dsend.at[t, j], drecv.at[t, j], sib).wait_recv()
        for cp in sends:
            cp.wait_send()

    return pl.pallas_call(
        body, name="allgather_weights",
        in_specs=[ANY] * (2 * nt), out_specs=[ANY] * nt,
        out_shape=[jax.ShapeDtypeStruct(a.shape, a.dtype) for a in lands],
        input_output_aliases={nt + t: t for t in range(nt)},
        scratch_shapes=[pltpu.SemaphoreType.DMA((nt, 3)), pltpu.SemaphoreType.DMA((nt, 3)),
                        pltpu.SemaphoreType.DMA((nt, 3)), pltpu.SemaphoreType.DMA((nt, 3))],
        compiler_params=_cp())(*shards, *lands)


def _to_sibling(arrs, name):
    nt = len(arrs)

    def body(*refs):
        srcs, dsts = refs[:nt], refs[nt:2 * nt]
        ssem, rsem = refs[2 * nt:]
        x, y, c, _ = _place()
        cps = [_remote(srcs[t], dsts[t], ssem.at[t], rsem.at[t], (x, y, 1 - c)) for t in range(nt)]
        for cp in cps:
            cp.start()
        for cp in cps:
            cp.wait()

    return pl.pallas_call(
        body, name=name, in_specs=[ANY] * nt, out_specs=[ANY] * nt,
        out_shape=[jax.ShapeDtypeStruct(a.shape, a.dtype) for a in arrs],
        scratch_shapes=[pltpu.SemaphoreType.DMA((nt,)), pltpu.SemaphoreType.DMA((nt,))],
        compiler_params=_cp())(*arrs)


HBM_SPEC = pl.BlockSpec(memory_space=pltpu.HBM)
SEM_SPEC = pl.BlockSpec(memory_space=pltpu.SEMAPHORE)
DATAFLOW = pltpu.SideEffectType.DATAFLOW_SIDE_EFFECTING


def _gather_copies(srcs, lands, ssem, rsem):
    x, y, c, chips = _place()
    me = 2 * x + y
    out = []
    for t in range(len(srcs)):
        half = srcs[t].shape[0] // 2
        mine = pl.ds(pl.multiple_of(c * half, 16), half)
        for j, (px, py) in enumerate(chips):
            k = 3 * t + j
            send = _remote(srcs[t].at[mine], lands[t].at[me, mine], ssem.at[k], rsem.at[k], (px, py, c))
            blk = lands[t].at[2 * px + py, mine]
            out.append((send, _remote(blk, blk, ssem.at[k], rsem.at[k], (px, py, c))))
    return out


def _scatter_copies(srcs, lands, ssem, rsem):
    x, y, c, chips = _place()
    out = []
    for t in range(len(srcs)):
        for j, (px, py) in enumerate(chips):
            k = 3 * t + j
            send = _remote(srcs[t].at[2 * px + py], lands[t].at[j], ssem.at[k], rsem.at[k], (px, py, c))
            out.append((send, _remote(lands[t].at[j], lands[t].at[j], ssem.at[k], rsem.at[k], (px, py, c))))
    return out


def _sibling_copies(srcs, lands, ssem, rsem):
    x, y, c, _ = _place()
    sib = (x, y, 1 - c)
    return [(_remote(srcs[t], lands[t], ssem.at[t], rsem.at[t], sib),
             _remote(lands[t], lands[t], ssem.at[t], rsem.at[t], sib)) for t in range(len(srcs))]


def _forward_copies(srcs, lands, ssem, rsem):
    x, y, c, chips = _place()
    sib = (x, y, 1 - c)
    out = []
    for t in range(len(lands)):
        half = lands[t].shape[1] // 2
        mine = pl.ds(pl.multiple_of(c * half, 16), half)
        other = pl.ds(pl.multiple_of((1 - c) * half, 16), half)
        for j, (px, py) in enumerate(chips):
            k = 3 * t + j
            blk_m, blk_o = lands[t].at[2 * px + py, mine], lands[t].at[2 * px + py, other]
            out.append((_remote(blk_m, blk_m, ssem.at[k], rsem.at[k], sib),
                        _remote(blk_o, blk_o, ssem.at[k], rsem.at[k], sib)))
    return out


def _peer_copies(srcs, lands, ssem, rsem):
    x, y, c, _ = _place()
    me = 4 * x + 2 * y + c
    out = []
    for kk in range(1, 8):
        px, py, pc = x ^ (kk >> 2), y ^ ((kk >> 1) & 1), c ^ (kk & 1)
        send = _remote(srcs[0], lands[0].at[me], ssem.at[kk - 1], rsem.at[kk - 1], (px, py, pc))
        slot = lands[0].at[4 * px + 2 * py + pc]
        out.append((send, _remote(slot, slot, ssem.at[kk - 1], rsem.at[kk - 1], (px, py, pc))))
    return out


def _split_start(plan, ncopy, srcs, lands, after, name):
    ns, nbuf = len(srcs), len(srcs) + len(lands)

    def body(*refs):
        ssem, rsem, token = refs[nbuf + 1], refs[nbuf + 2], refs[-1]
        for send, _ in plan(refs[:ns], refs[ns:nbuf], ssem, rsem):
            send.start()
        token[...] = jnp.zeros_like(token)

    bufs = [pltpu.with_memory_space_constraint(a, pltpu.HBM) for a in list(srcs) + list(lands)]
    outs = pl.pallas_call(
        body, name=name, in_specs=[HBM_SPEC] * nbuf + [ANY],
        out_specs=[SEM_SPEC, SEM_SPEC] + [HBM_SPEC] * nbuf + [pl.BlockSpec(memory_space=pltpu.VMEM)],
        out_shape=[pltpu.SemaphoreType.DMA((ncopy,)), pltpu.SemaphoreType.DMA((ncopy,))]
        + [pltpu.HBM(a.shape, a.dtype) for a in bufs] + [jax.ShapeDtypeStruct((8, 128), F32)],
        input_output_aliases={i: 2 + i for i in range(nbuf)},
        compiler_params=pltpu.CompilerParams(has_side_effects=DATAFLOW))(*bufs, after)
    return outs[0], outs[1], outs[2:2 + ns], outs[2 + ns:2 + nbuf], outs[-1]


def _split_wait(plan, ssem, rsem, srcs, lands, after, name):
    ns, nbuf = len(srcs), len(srcs) + len(lands)

    def body(*refs):
        s_ref, r_ref = refs[nbuf], refs[nbuf + 1]
        for send, recv in plan(refs[:ns], refs[ns:nbuf], s_ref, r_ref):
            send.wait_send()
            recv.wait_recv()

    outs = pl.pallas_call(
        body, name=name, in_specs=[HBM_SPEC] * nbuf + [SEM_SPEC, SEM_SPEC] + [ANY] * len(after),
        out_specs=[HBM_SPEC] * nbuf,
        out_shape=[pltpu.HBM(a.shape, a.dtype) for a in list(srcs) + list(lands)],
        input_output_aliases={i: i for i in range(nbuf)},
        compiler_params=pltpu.CompilerParams(has_side_effects=DATAFLOW))(*srcs, *lands, ssem, rsem, *after)
    return outs


def _gather_finish(lands):
    nt = len(lands)

    def body(*refs):
        outs = refs[nt:2 * nt]
        dsend, drecv = refs[2 * nt:]
        x, y, c, chips = _place()
        sib = (x, y, 1 - c)
        sends = []
        for t in range(nt):
            half = outs[t].shape[1] // 2
            mine = pl.ds(pl.multiple_of(c * half, 16), half)
            for j, (px, py) in enumerate(chips):
                blk = outs[t].at[2 * px + py, mine]
                cp = _remote(blk, blk, dsend.at[t, j], drecv.at[t, j], sib)
                cp.start()
                sends.append(cp)
        for t in range(nt):
            half = outs[t].shape[1] // 2
            other = pl.ds(pl.multiple_of((1 - c) * half, 16), half)
            for j, (px, py) in enumerate(chips):
                blk = outs[t].at[2 * px + py, other]
                _remote(blk, blk, dsend.at[t, j], drecv.at[t, j], sib).wait_recv()
        for cp in sends:
            cp.wait_send()

    return pl.pallas_call(
        body, name="gather_finish", in_specs=[ANY] * nt, out_specs=[ANY] * nt,
        out_shape=[jax.ShapeDtypeStruct(a.shape, a.dtype) for a in lands],
        input_output_aliases={t: t for t in range(nt)},
        scratch_shapes=[pltpu.SemaphoreType.DMA((nt, 3)), pltpu.SemaphoreType.DMA((nt, 3))],
        compiler_params=_cp())(*lands)


def _chip_partial(owns, recv, chip_arr, tag):
    nt = len(owns)

    def body(chip_ref, *refs):
        k = pl.program_id(0)
        for t in range(nt):
            p = refs[t][...] + refs[nt + t][...].astype(F32)
            refs[2 * nt + t][...] = p.astype(BF16)

            @pl.when(k == chip_ref[0])
            def _():
                refs[3 * nt + t][...] = p

    blk_specs = [pl.BlockSpec((None,) + a.shape[1:], lambda k, chip_ref: (k, 0, 0)) for a in owns]
    own_specs = [pl.BlockSpec(a.shape[1:], lambda k, chip_ref: (0, 0)) for a in owns]
    return pl.pallas_call(
        body, name="rs_chip_partial_" + tag,
        grid_spec=pltpu.PrefetchScalarGridSpec(num_scalar_prefetch=1, grid=(NSH,), in_specs=blk_specs * 2,
                                               out_specs=blk_specs + own_specs),
        out_shape=[jax.ShapeDtypeStruct(a.shape, BF16) for a in owns]
        + [jax.ShapeDtypeStruct(a.shape[1:], F32) for a in owns],
        compiler_params=_cp(1))(chip_arr, *owns, *recv)


def _sum_chips(own, recv, tag, after=()):
    nt = len(own)

    def body(*refs):
        outs = refs[2 * nt + len(after):]
        for t in range(nt):
            r = refs[nt + t]
            outs[t][...] = ((refs[t][...] + r[0].astype(F32)) + r[1].astype(F32)) + r[2].astype(F32)

    vm = pl.BlockSpec(memory_space=pltpu.VMEM)
    return pl.pallas_call(
        body, name="rs_sum_chips_" + tag, in_specs=[vm] * (2 * nt) + [ANY] * len(after), out_specs=[vm] * nt,
        out_shape=[jax.ShapeDtypeStruct(a.shape, F32) for a in own],
        compiler_params=_cp())(*own, *recv, *after)


def _adamw_math(w, g, m, v):
    m = ADAM_B1 * m + (1.0 - ADAM_B1) * g
    v = ADAM_B2 * v + (1.0 - ADAM_B2) * (g * g)
    m_hat = m / (1.0 - ADAM_B1 ** ADAM_STEP)
    v_hat = v / (1.0 - ADAM_B2 ** ADAM_STEP)
    delta = -ADAM_LR * (m_hat / (jnp.sqrt(v_hat) + ADAM_EPS) + ADAM_WD * w)
    return delta, m, v


ADAM_SPLIT = 4


def _adamw_shards(own, sib, ws, ms, vs, c_arr, tag):
    nt = len(own)

    def body(c_ref, *refs):
        hh = pl.program_id(0)
        mine = hh == c_ref[0]
        for t in range(nt):
            g = jnp.where(mine, refs[t][...], refs[nt + t][...])
            delta, m, v = _adamw_math(refs[2 * nt + t][...], g, refs[3 * nt + t][...], refs[4 * nt + t][...])
            refs[5 * nt + t][...] = g
            refs[6 * nt + t][...] = delta
            refs[7 * nt + t][...] = m
            refs[8 * nt + t][...] = v

    def tile(a):
        return (a.shape[0] // ADAM_SPLIT, a.shape[1])

    half_specs = [pl.BlockSpec(tile(a), lambda hh, q, c_ref: (q, 0)) for a in own]
    full_specs = [pl.BlockSpec(tile(a), lambda hh, q, c_ref: (hh * ADAM_SPLIT + q, 0)) for a in own]
    return pl.pallas_call(
        body, name="adamw_shards_" + tag,
        grid_spec=pltpu.PrefetchScalarGridSpec(num_scalar_prefetch=1, grid=(2, ADAM_SPLIT),
                                               in_specs=half_specs * 2 + full_specs * 3, out_specs=full_specs * 4),
        out_shape=[jax.ShapeDtypeStruct(w.shape, F32) for w in ws] * 4,
        compiler_params=_cp(2))(c_arr, *own, *sib, *ws, *ms, *vs)


SMALL_TAIL_ROW = 552


def _small_sum_adamw(gathered, wp, mp, vp):
    rows = wp.shape[0]

    def body(g_ref, w_ref, m_ref, v_ref, *rest):
        outs, res = rest[:-1], rest[-1]
        g = g_ref[0]
        for d in range(1, 8):
            g = g + g_ref[d]
        delta, m, v = _adamw_math(w_ref[...], g, m_ref[...], v_ref[...])
        for kind, val in enumerate((g, delta, m, v)):
            res[kind] = val
            gains, w_pool_o, scale_o, tail_o = outs[7 * kind:7 * kind + 4], *outs[7 * kind + 4:7 * kind + 7]
            for t in range(4):
                for i in range(8):
                    gains[t][:, 128 * i:128 * (i + 1)] = res[kind, pl.ds(8 * t + i, 1), :]
            for grp in range(4):
                w_pool_o[grp] = res[kind, pl.ds(32 + GROUP * grp, GROUP), :]
            for i in range(4):
                scale_o[:, 128 * i:128 * (i + 1)] = res[kind, pl.ds(544 + i, 1), :]
            tail_o[...] = res[kind, pl.ds(SMALL_TAIL_ROW, rows - SMALL_TAIL_ROW), :]

    vm = pl.BlockSpec(memory_space=pltpu.VMEM)
    one_kind = [jax.ShapeDtypeStruct((1, D), F32)] * 4 + [
        jax.ShapeDtypeStruct((4, GROUP, GROUP), F32), jax.ShapeDtypeStruct((1, POOL_W), F32),
        jax.ShapeDtypeStruct((rows - SMALL_TAIL_ROW, 128), F32)]
    return pl.pallas_call(
        body, name="small_sum_adamw", in_specs=[vm] * 4, out_specs=[vm] * 28,
        out_shape=one_kind * 4,
        scratch_shapes=[pltpu.VMEM((4, rows, 128), F32)],
        compiler_params=_cp())(gathered, wp, mp, vp)


def _pack_small(parts):
    rows = []
    for a in parts:
        flat = a.reshape(-1)
        n = flat.shape[0]
        padded = -(-n // 1024) * 1024
        rows.append(jnp.pad(flat, (0, padded - n)).reshape(-1, 128))
    return jnp.concatenate(rows, axis=0)


def kernel(x, g_pre_mix, w_in, w_pool, pool_scale, rel_bias, sinks, w_out, g_post_mix, g_pre_ffn, w_gate, w_up, w_down, g_post_ffn, loss_target, m_g_pre_mix, m_w_in, m_w_pool, m_pool_scale, m_rel_bias, m_sinks, m_w_out, m_g_post_mix, m_g_pre_ffn, m_w_gate, m_w_up, m_w_down, m_g_post_ffn, v_g_pre_mix, v_w_in, v_w_pool, v_pool_scale, v_rel_bias, v_sinks, v_w_out, v_g_post_mix, v_g_pre_ffn, v_w_gate, v_w_up, v_w_down, v_g_post_ffn):
    c = lax.axis_index("c")
    chip = 2 * lax.axis_index("x") + lax.axis_index("y")

    def shards(a_in, a_out, a_gate, a_up, a_down):
        return (a_in[0].T, a_out[0], a_gate[0].T, a_up[0].T, a_down[0])

    c_arr = c.reshape(1).astype(jnp.int32)
    chip_arr = chip.reshape(1).astype(jnp.int32)

    big_w = shards(w_in, w_out, w_gate, w_up, w_down)
    big_bf = [w.astype(BF16) for w in big_w]
    (win_all,) = _all_gather_weights(big_bf[:1], [_own_slot(big_bf[0])])
    g_ssem, g_rsem, g_srcs, g_lands, g_token = _split_start(
        _gather_copies, 12, big_bf[1:], [_own_slot(a) for a in big_bf[1:]], win_all, "gather_rest_start")
    fw = {}

    def w_out_ready(*after):
        lands = _split_wait(_gather_copies, g_ssem, g_rsem, g_srcs, g_lands, after, "gather_rest_wait")[4:]
        (wout_all,) = _gather_finish(lands[:1])
        fw["f"] = _split_start(_forward_copies, 9, [], lands[1:], wout_all, "gather_forward_start")
        return wout_all.reshape(D, D), fw["f"][4]

    def ffn_weights(after):
        ssem, rsem, _, lands, _ = fw["f"]
        return _split_wait(_forward_copies, ssem, rsem, [], lands, [after], "gather_forward_wait")

    rs = {}

    def on_ffn_grads(ffn_g):
        oths = ffn_g[1::2]
        rs["ffn_own"] = ffn_g[0::2]
        rs["x"] = _split_start(_sibling_copies, 3, oths, [lax.empty(a.shape, BF16) for a in oths], ffn_g[0],
                               "rs_exchange_ffn_start")
        return rs["x"][4]

    def on_wout_grads(own, oth, after):
        ssem, rsem, srcs, lands, _ = rs["x"]
        recv_ffn = _split_wait(_sibling_copies, ssem, rsem, srcs, lands, [after], "rs_exchange_ffn_wait")[3:]
        recv_wout = _to_sibling([oth], "rs_exchange_wout")
        outs = _chip_partial([own] + list(rs["ffn_own"]), list(recv_wout) + list(recv_ffn), chip_arr, "early")
        rs["early_own"] = outs[4:]
        rs["s"] = _split_start(_scatter_copies, 12, outs[:4],
                               [lax.empty((3,) + a.shape[1:], BF16) for a in outs[:4]], outs[4], "scatter_early_start")
        return rs["s"][4]

    def on_attn_grads(after):
        ssem, rsem, srcs, lands, _ = rs["s"]
        recv_early = _split_wait(_scatter_copies, ssem, rsem, srcs, lands, after, "scatter_early_wait")[4:]
        finals = _sum_chips(list(rs["early_own"]), list(recv_early), "early")
        rs["sh"] = _split_start(_sibling_copies, 4, finals, [lax.empty(a.shape, F32) for a in finals], finals[0],
                                "rs_share_early_start")
        return rs["sh"][4]

    small = (g_pre_mix + g_token[:1, :1], g_post_mix, g_pre_ffn, g_post_ffn, w_pool[0], pool_scale, rel_bias, sinks)
    loss, gx, small_grads, ((win_own, win_oth), _, _) = _local_step(
        x[0], loss_target[0], small, win_all.reshape(IN_W, D), w_out_ready, ffn_weights, c_arr,
        on_ffn_grads, on_wout_grads, on_attn_grads)

    unused = jnp.zeros((1, 1), F32)
    gp = _pack_small(small_grads + (loss,))
    wp = _pack_small((g_pre_mix, g_post_mix, g_pre_ffn, g_post_ffn, w_pool, pool_scale, rel_bias, sinks, unused))
    mp = _pack_small((m_g_pre_mix, m_g_post_mix, m_g_pre_ffn, m_g_post_ffn, m_w_pool, m_pool_scale, m_rel_bias,
                      m_sinks, unused))
    vp = _pack_small((v_g_pre_mix, v_g_post_mix, v_g_pre_ffn, v_g_post_ffn, v_w_pool, v_pool_scale, v_rel_bias,
                      v_sinks, unused))

    moments = (shards(m_w_in, m_w_out, m_w_gate, m_w_up, m_w_down),
               shards(v_w_in, v_w_out, v_w_gate, v_w_up, v_w_down))
    recv_a = _to_sibling([win_oth], "rs_exchange_win")
    outs = _chip_partial([win_own], recv_a, chip_arr, "win")
    w_ssem, w_rsem, w_srcs, w_lands, w_token = _split_start(
        _scatter_copies, 3, outs[:1], [lax.empty((3,) + outs[0].shape[1:], BF16)], gx, "scatter_win_start")
    slots = lax.dynamic_update_slice(lax.empty((8,) + gp.shape, F32), gp[None], (2 * chip + c, 0, 0))
    p_ssem, p_rsem, p_srcs, p_lands, p_token = _split_start(_peer_copies, 7, [gp], [slots], w_token,
                                                            "small_allgather_start")
    ssem, rsem, srcs, lands, _ = rs["sh"]
    shared = _split_wait(_sibling_copies, ssem, rsem, srcs, lands, [p_token], "rs_share_early_wait")
    adam_e = _adamw_shards(shared[:4], shared[4:], big_w[1:], moments[0][1:], moments[1][1:], c_arr, "early")
    recv_win = _split_wait(_scatter_copies, w_ssem, w_rsem, w_srcs, w_lands, [adam_e[0]], "scatter_win_wait")[1:]
    win_final = _sum_chips([outs[1]], recv_win, "win")
    win_sib = _to_sibling(win_final, "rs_share_win")
    adam_w = _adamw_shards(win_final, win_sib, big_w[:1], moments[0][:1], moments[1][:1], c_arr, "win")
    big_g, big_d, big_m, big_v = [[adam_w[i]] + list(adam_e[4 * i:4 * i + 4]) for i in range(4)]

    gathered = _split_wait(_peer_copies, p_ssem, p_rsem, p_srcs, p_lands, [adam_w[0]], "small_allgather_wait")[1]
    flat = _small_sum_adamw(gathered, wp, mp, vp)
    small_out = [flat[7 * kind:7 * kind + 7] for kind in range(4)]
    total_loss = small_out[0][6][16, 0]

    def ordered(small7, big4):
        sg1, sg2, sg3, sg4, swp, ssc, tail = small7
        swp, srb, ssk = swp[None], tail[0:2].reshape(NBUCKET, NQ), tail[8:9, 0:NQ]
        bwin, bwout, bwg, bwu, bwd = big4[0].T[None], big4[1][None], big4[2].T[None], big4[3].T[None], big4[4][None]
        return [sg1, bwin, swp, ssc, srb, ssk, bwout, sg2, sg3, bwg, bwu, bwd, sg4]

    res = [total_loss, gx[None]]
    for sm, bg in zip(small_out, (big_g, big_d, big_m, big_v)):
        res += ordered(sm, bg)
    return tuple(res)
```

```python
import numpy as np
import jax
import jax.numpy as jnp
from jax import lax
from jax.experimental import pallas as pl
from jax.experimental.pallas import tpu as pltpu

F32 = jnp.float32
BF16 = jnp.bfloat16

D = 1024
POOL_W = 512
GROUP = 128
WINDOWS = (2, 4, 8, 16)
HALO = 128
NQ = 8
BLK = 128
NBUCKET = 32
IN_W = 1280
DFF = 2816
NSH = 4
FS = DFF // NSH
WIN_S = IN_W // NSH
WOUT_S = D // NSH
EPS = 1e-6
NEG = -1e30
SCALE = 0.125

ADAM_LR = 0.001
ADAM_B1 = 0.9
ADAM_B2 = 0.999
ADAM_EPS = 1e-08
ADAM_WD = 0.01
ADAM_STEP = 10

TM = 512
TM_POOL = 256
TM_FFN = 512
TM_FFN_FWD = 1024
FFN_ROWS = 256
VMEM_LIMIT = 60 * 1024 * 1024

MESH_T = pl.DeviceIdType.MESH
ANY = pl.BlockSpec(memory_space=pl.ANY)


def _cp(n_grid=0, **kw):
    sem = ("arbitrary",) * n_grid if n_grid else None
    return pltpu.CompilerParams(dimension_semantics=sem, vmem_limit_bytes=VMEM_LIMIT, **kw)


def _dot(a, b):
    return jnp.dot(a, b, preferred_element_type=F32)


def _dot_nt(a, b):
    return lax.dot_general(a, b, (((1,), (1,)), ((), ())), preferred_element_type=F32)


def _dot_tn(a, b):
    return lax.dot_general(a, b, (((0,), (0,)), ((), ())), preferred_element_type=F32)


def _rms(x):
    r = lax.rsqrt(jnp.mean(x * x, axis=-1, keepdims=True) + EPS)
    return r, x * r


def _rms_bwd(r, n, g, dout):
    dn = dout * g
    dx = r * (dn - n * jnp.mean(dn * n, axis=-1, keepdims=True))
    dg = jnp.sum(dout * n, axis=0, keepdims=True)
    return dx, dg


def _split(x, n):
    parts = []
    r = x
    for _ in range(n):
        p = r.astype(BF16)
        parts.append(p)
        r = r - p.astype(F32)
    return parts


def _band_dot(a, x, n):
    acc = None
    for p in _split(x, n):
        t = _dot(a, p)
        acc = t if acc is None else acc + t
    return acc


def _bucket_table():
    qi = np.arange(BLK)[None, :]
    kj = np.arange(2 * BLK)[:, None]
    dist = qi + BLK - kj
    n = np.maximum(dist, 0)
    nf = np.maximum(n, 1).astype(np.float32)
    large = 16 + (np.log(nf / np.float32(16)) / np.float32(np.log(128 / 16)) * np.float32(16)).astype(np.int32)
    large = np.minimum(large, NBUCKET - 1)
    return np.where(n < 16, n, large).astype(np.int32)


def _inproj_fwd(x, g1, w_in):
    s = x.shape[0]
    tm = min(TM, s)

    def body(x_ref, g_ref, w_ref, u_ref, q_ref, k_ref, v_ref):
        _, n = _rms(x_ref[...])
        h = (n * g_ref[...]).astype(BF16)
        proj = _dot_nt(h, w_ref[...])
        u_ref[...] = proj[:, :512]
        q_ref[...] = proj[:, 512:1024].astype(BF16)
        k_ref[...] = proj[:, 1024:1152].astype(BF16)
        v_ref[...] = proj[:, 1152:1280].astype(BF16)

    row = lambda w: pl.BlockSpec((tm, w), lambda i: (i, 0))
    return pl.pallas_call(
        body, name="inproj_fwd", grid=(s // tm,),
        in_specs=[row(D), pl.BlockSpec((1, D), lambda i: (0, 0)), pl.BlockSpec((IN_W, D), lambda i: (0, 0))],
        out_specs=[row(512), row(512), row(128), row(128)],
        out_shape=[jax.ShapeDtypeStruct((s, 512), F32), jax.ShapeDtypeStruct((s, 512), BF16),
                   jax.ShapeDtypeStruct((s, 128), BF16), jax.ShapeDtypeStruct((s, 128), BF16)],
        compiler_params=_cp(1))(x, g1, w_in)


def _window_sums(ext, w, lag_sign, tm, terms):
    rows = lax.broadcasted_iota(jnp.int32, (HALO, 2 * HALO), 0)
    cols = lax.broadcasted_iota(jnp.int32, (HALO, 2 * HALO), 1)
    d = rows + HALO - cols if lag_sign > 0 else cols - rows
    band = jnp.where((d >= 0) & (d < w), 1.0, 0.0).astype(BF16)
    return jnp.concatenate([_band_dot(band, ext[r:r + 2 * HALO], terms) for r in range(0, tm, HALO)], axis=0)


def _pooled(ext, cur, t0, tm):
    t = t0 + lax.broadcasted_iota(jnp.int32, (tm, GROUP), 0)
    out = []
    for g, w in enumerate(WINDOWS):
        sl = slice(g * GROUP, (g + 1) * GROUP)
        cnt = jnp.minimum(t + 1, w).astype(F32)
        out.append(_window_sums(ext[:, sl], w, 1, tm, 2) / cnt - cur[:, sl])
    return out


def _pool_fwd(u, w_pool, pool_scale):
    s = u.shape[0]
    tm = min(TM_POOL, s)
    hb = tm // HALO

    def body(uc_ref, uh_ref, wp_ref, sc_ref, o_ref):
        i = pl.program_id(0)
        cur = uc_ref[...]
        halo = jnp.where(i > 0, uh_ref[...], 0.0)
        ext = jnp.concatenate([halo, cur], axis=0)
        pooled = _pooled(ext, cur, i * tm, tm)
        for g in range(4):
            sl = slice(g * GROUP, (g + 1) * GROUP)
            mixed = _dot(pooled[g].astype(BF16), wp_ref[g])
            o_ref[:, sl] = (mixed * sc_ref[:, sl]).astype(BF16)

    return pl.pallas_call(
        body, name="pool_fwd", grid=(s // tm,),
        in_specs=[pl.BlockSpec((tm, 512), lambda i: (i, 0)),
                  pl.BlockSpec((HALO, 512), lambda i: (jnp.maximum(i * hb - 1, 0), 0)),
                  pl.BlockSpec((4, GROUP, GROUP), lambda i: (0, 0, 0)),
                  pl.BlockSpec((1, 512), lambda i: (0, 0))],
        out_specs=pl.BlockSpec((tm, 512), lambda i: (i, 0)),
        out_shape=jax.ShapeDtypeStruct((s, 512), BF16),
        compiler_params=_cp(1))(u, u, w_pool, pool_scale)


def _bias_table(bucket, rel_bias):
    def body(b_ref, rb_ref, o_ref):
        bucket_v = b_ref[...]
        key = lax.broadcasted_iota(jnp.int32, (2 * BLK, BLK), 0)
        dist = lax.broadcasted_iota(jnp.int32, (2 * BLK, BLK), 1) + BLK - key
        inwin = (dist >= 0) & (dist < BLK)
        for h in range(NQ):
            acc = jnp.zeros((2 * BLK, BLK), F32)
            for b in range(NBUCKET):
                acc = jnp.where(bucket_v == b, rb_ref[b, h], acc)
            rest = jnp.where(inwin, acc, NEG)
            o_ref[1, h] = rest
            o_ref[0, h] = jnp.where(key >= BLK, rest, NEG)

    return pl.pallas_call(
        body, name="bias_table",
        in_specs=[pl.BlockSpec(memory_space=pltpu.VMEM), pl.BlockSpec(memory_space=pltpu.SMEM)],
        out_specs=pl.BlockSpec(memory_space=pltpu.VMEM),
        out_shape=jax.ShapeDtypeStruct((2, NQ, 2 * BLK, BLK), F32),
        compiler_params=_cp())(bucket, rel_bias)


def _kv_band(ref, n, transposed):
    pstart = pl.multiple_of(jnp.maximum(n - 1, 0) * BLK, BLK)
    cstart = pl.multiple_of(n * BLK, BLK)
    lo = lax.broadcasted_iota(jnp.int32, (2 * BLK, 128), 1) < 64
    band = jnp.concatenate([ref[pl.ds(pstart, BLK), :], ref[pl.ds(cstart, BLK), :]], axis=0).astype(F32)
    rot = pltpu.roll(band, 64, axis=1)
    dup = (jnp.where(lo, band, rot), jnp.where(lo, rot, band))
    plain = [d.astype(BF16) for d in dup]
    if not transposed:
        return plain, None, (lo, pstart, cstart)
    row_lo = lax.broadcasted_iota(jnp.int32, (128, 2 * BLK), 0) < 64
    tr = [[jnp.where(row_lo, d.T, 0.0).astype(BF16), jnp.where(row_lo, 0.0, d.T).astype(BF16)] for d in dup]
    return plain, tr, (lo, pstart, cstart)


def _attn_probs(qm, kk, bias, sink):
    s = _dot_nt(kk, qm) + bias
    m = jnp.maximum(jnp.max(s, axis=0, keepdims=True), sink)
    p = jnp.exp(s - m)
    es = jnp.exp(sink - m)
    inv = 1.0 / (jnp.sum(p, axis=0, keepdims=True) + es)
    return p * inv, es * inv


def _attn_fwd(q, k, v, bias, sinks):
    s = q.shape[0]

    def body(q_ref, k_ref, v_ref, b_ref, sk_ref, o_ref):
        n = pl.program_id(0)
        kk, _, _ = _kv_band(k_ref, n, False)
        _, vt, _ = _kv_band(v_ref, n, True)
        bidx = jnp.minimum(n, 1)
        lo_q = lax.broadcasted_iota(jnp.int32, (BLK, 128), 1) < 64
        for p in range(4):
            h = p // 2
            qt = q_ref[:, p * 128:(p + 1) * 128].astype(F32) * SCALE
            acc_t = jnp.zeros((128, BLK), F32)
            for e in range(2):
                head = 2 * p + e
                qm = jnp.where(lo_q if e == 0 else jnp.logical_not(lo_q), qt, 0.0).astype(BF16)
                prob, _ = _attn_probs(qm, kk[h], b_ref[bidx, head], sk_ref[0, head])
                acc_t = acc_t + _dot(vt[h][e], prob.astype(BF16))
            o_ref[:, p * 128:(p + 1) * 128] = acc_t.T.astype(BF16)

    return pl.pallas_call(
        body, name="attn_fwd", grid=(s // BLK,),
        in_specs=[pl.BlockSpec((BLK, 512), lambda i: (i, 0)),
                  pl.BlockSpec((s, 128), lambda i: (0, 0)),
                  pl.BlockSpec((s, 128), lambda i: (0, 0)),
                  pl.BlockSpec((2, NQ, 2 * BLK, BLK), lambda i: (0, 0, 0, 0)),
                  pl.BlockSpec(memory_space=pltpu.SMEM)],
        out_specs=pl.BlockSpec((BLK, 512), lambda i: (i, 0)),
        out_shape=jax.ShapeDtypeStruct((s, 512), BF16),
        compiler_params=_cp(1))(q, k, v, bias, sinks)


def _outproj_fwd(pool_o, attn_o, w_out, x, g2, g3):
    s = x.shape[0]
    tm = min(TM, s)

    def body(p_ref, a_ref, w_ref, x_ref, g2_ref, g3_ref, mix_ref, x1_ref, h2_ref):
        mix = _dot(p_ref[...], w_ref[0:512, :]) + _dot(a_ref[...], w_ref[512:1024, :])
        mix_ref[...] = mix
        _, n2 = _rms(mix)
        x1 = x_ref[...] + n2 * g2_ref[...]
        x1_ref[...] = x1
        _, n3 = _rms(x1)
        h2_ref[...] = (n3 * g3_ref[...]).astype(BF16)

    row = lambda w: pl.BlockSpec((tm, w), lambda i: (i, 0))
    vec = pl.BlockSpec((1, D), lambda i: (0, 0))
    return pl.pallas_call(
        body, name="outproj_fwd", grid=(s // tm,),
        in_specs=[row(512), row(512), pl.BlockSpec((D, D), lambda i: (0, 0)), row(D), vec, vec],
        out_specs=[row(D), row(D), row(D)],
        out_shape=[jax.ShapeDtypeStruct((s, D), F32), jax.ShapeDtypeStruct((s, D), F32),
                   jax.ShapeDtypeStruct((s, D), BF16)],
        compiler_params=_cp(1))(pool_o, attn_o, w_out, x, g2, g3)


def _silu_parts(g):
    sg = jax.nn.sigmoid(g)
    return sg, g * sg


def _ffn_fwd(h2, wg, wu, wd, x1, target, g4):
    s = h2.shape[0]
    tm = min(TM_FFN_FWD, s)

    def body(h_ref, wg_ref, wu_ref, wd_ref, x1_ref, t_ref, g4_ref,
             gate_ref, up_ref, df_ref, dy_ref, loss_ref, gg4_ref, f_acc):
        i = pl.program_id(0)
        j = pl.program_id(1)
        first = j == 0
        for r in range(0, tm, FFN_ROWS):
            rows = pl.ds(r, min(FFN_ROWS, tm))
            h = h_ref[rows, :]
            gate = _dot_nt(h, wg_ref[...])
            up = _dot_nt(h, wu_ref[...])
            gate_ref[rows, :] = gate.astype(BF16)
            up_ref[rows, :] = up.astype(BF16)
            _, sl = _silu_parts(gate)
            contrib = _dot((sl * up).astype(BF16), wd_ref[...])
            f_acc[rows, :] = jnp.where(first, contrib, f_acc[rows, :] + contrib)

        @pl.when((i == 0) & (j == 0))
        def _():
            loss_ref[...] = jnp.zeros_like(loss_ref)
            gg4_ref[...] = jnp.zeros_like(gg4_ref)

        @pl.when(j == NSH - 1)
        def _():
            r4, n4 = _rms(f_acc[...])
            g4v = g4_ref[...]
            err = x1_ref[...] + n4 * g4v - t_ref[...]
            loss_ref[...] += (0.5 / D) * jnp.sum(err * err).reshape(1, 1)
            dy = err * (1.0 / D)
            dy_ref[...] = dy
            df, dg = _rms_bwd(r4, n4, g4v, dy)
            gg4_ref[...] += dg
            df_ref[...] = df.astype(BF16)

    row = lambda w: pl.BlockSpec((tm, w), lambda i, j: (i, 0))
    sh = lambda r, c: pl.BlockSpec((None, r, c), lambda i, j: (j, 0, 0))
    act = pl.BlockSpec((None, tm, FS), lambda i, j: (j, i, 0))
    vec = pl.BlockSpec((1, D), lambda i, j: (0, 0))
    return pl.pallas_call(
        body, name="ffn_fwd", grid=(s // tm, NSH),
        in_specs=[row(D), sh(FS, D), sh(FS, D), sh(FS, D), row(D), row(D), vec],
        out_specs=[act, act, row(D), row(D), pl.BlockSpec((1, 1), lambda i, j: (0, 0)), vec],
        out_shape=[jax.ShapeDtypeStruct((NSH, s, FS), BF16), jax.ShapeDtypeStruct((NSH, s, FS), BF16),
                   jax.ShapeDtypeStruct((s, D), BF16), jax.ShapeDtypeStruct((s, D), F32),
                   jax.ShapeDtypeStruct((1, 1), F32), jax.ShapeDtypeStruct((1, D), F32)],
        scratch_shapes=[pltpu.VMEM((tm, D), F32)],
        compiler_params=_cp(2))(h2, wg, wu, wd, x1, target, g4)


def _ffn_bwd_act(df, gate, up, wg, wu, wd, dy, x1, mix, g3, g2):
    s = df.shape[0]
    tm = min(TM_FFN, s)

    def body(df_ref, gate_ref, up_ref, wg_ref, wu_ref, wd_ref, dy_ref, x1_ref, mix_ref, g3_ref, g2_ref,
             dgate_ref, dup_ref, dx1_ref, dmix_ref, gg3_ref, gg2_ref, acc):
        i = pl.program_id(0)
        j = pl.program_id(1)
        first = j == 0
        for r in range(0, tm, FFN_ROWS):
            rows = pl.ds(r, min(FFN_ROWS, tm))
            da = _dot_nt(df_ref[rows, :], wd_ref[...])
            g = gate_ref[rows, :].astype(F32)
            u = up_ref[rows, :].astype(F32)
            sg, sl = _silu_parts(g)
            dgate = (da * u * (sg * (1.0 + g * (1.0 - sg)))).astype(BF16)
            dup = (da * sl).astype(BF16)
            dgate_ref[rows, :] = dgate
            dup_ref[rows, :] = dup
            contrib = _dot(dgate, wg_ref[...]) + _dot(dup, wu_ref[...])
            acc[rows, :] = jnp.where(first, contrib, acc[rows, :] + contrib)

        @pl.when((i == 0) & (j == 0))
        def _():
            gg3_ref[...] = jnp.zeros_like(gg3_ref)
            gg2_ref[...] = jnp.zeros_like(gg2_ref)

        @pl.when(j == NSH - 1)
        def _():
            r3, n3 = _rms(x1_ref[...])
            dx1n, dg3 = _rms_bwd(r3, n3, g3_ref[...], acc[...])
            dx1 = dy_ref[...] + dx1n
            dx1_ref[...] = dx1
            gg3_ref[...] += dg3
            r2, n2 = _rms(mix_ref[...])
            dmix, dg2 = _rms_bwd(r2, n2, g2_ref[...], dx1)
            gg2_ref[...] += dg2
            dmix_ref[...] = dmix.astype(BF16)

    row = lambda w: pl.BlockSpec((tm, w), lambda i, j: (i, 0))
    sh = lambda r, c: pl.BlockSpec((None, r, c), lambda i, j: (j, 0, 0))
    act = pl.BlockSpec((None, tm, FS), lambda i, j: (j, i, 0))
    vec = pl.BlockSpec((1, D), lambda i, j: (0, 0))
    return pl.pallas_call(
        body, name="ffn_bwd_act", grid=(s // tm, NSH),
        in_specs=[row(D), act, act, sh(FS, D), sh(FS, D), sh(FS, D), row(D), row(D), row(D), vec, vec],
        out_specs=[act, act, row(D), row(D), vec, vec],
        out_shape=[jax.ShapeDtypeStruct((NSH, s, FS), BF16), jax.ShapeDtypeStruct((NSH, s, FS), BF16),
                   jax.ShapeDtypeStruct((s, D), F32), jax.ShapeDtypeStruct((s, D), BF16),
                   jax.ShapeDtypeStruct((1, D), F32), jax.ShapeDtypeStruct((1, D), F32)],
        scratch_shapes=[pltpu.VMEM((tm, D), F32)],
        compiler_params=_cp(2))(df, gate, up, wg, wu, wd, dy, x1, mix, g3, g2)


SMEM_SPEC = pl.BlockSpec(memory_space=pltpu.SMEM)


def _emit_halves(acc_ref, row0, rows, c, own_ref, oth_ref):
    half = rows // 2
    own_ref[...] = acc_ref[pl.ds(row0 + pl.multiple_of(c * half, 8), half), :]
    oth_ref[...] = acc_ref[pl.ds(row0 + pl.multiple_of((1 - c) * half, 8), half), :].astype(BF16)


def _half_shapes(rows):
    return [jax.ShapeDtypeStruct((NSH, rows // 2, D), F32), jax.ShapeDtypeStruct((NSH, rows // 2, D), BF16)]


def _ffn_bwd_w(h2, gate, up, dgate, dup, df, c_arr):
    s = h2.shape[0]
    tm = min(TM_FFN_FWD, s)
    nt = s // tm

    def body(c_ref, h_ref, gate_ref, up_ref, dgate_ref, dup_ref, df_ref, *rest):
        outs, accs = rest[:6], rest[6:]
        i = pl.program_id(1)
        @pl.when(i == 0)
        def _():
            for acc in accs:
                acc[...] = jnp.zeros_like(acc)

        h = h_ref[...]
        accs[0][...] += _dot_tn(dgate_ref[...], h)
        accs[1][...] += _dot_tn(dup_ref[...], h)
        _, sl = _silu_parts(gate_ref[...].astype(F32))
        a = (sl * up_ref[...].astype(F32)).astype(BF16)
        accs[2][...] += _dot_tn(a, df_ref[...])

        @pl.when(i == nt - 1)
        def _():
            for t, acc in enumerate(accs):
                _emit_halves(acc, 0, FS, c_ref[0], outs[2 * t], outs[2 * t + 1])

    row = lambda w: pl.BlockSpec((tm, w), lambda j, i: (i, 0))
    act = pl.BlockSpec((None, tm, FS), lambda j, i: (j, i, 0))
    half = pl.BlockSpec((None, FS // 2, D), lambda j, i: (j, 0, 0))
    return pl.pallas_call(
        body, name="ffn_bwd_w", grid=(NSH, nt),
        in_specs=[SMEM_SPEC, row(D), act, act, act, act, row(D)],
        out_specs=[half] * 6,
        out_shape=_half_shapes(FS) * 3,
        scratch_shapes=[pltpu.VMEM((FS, D), F32)] * 3,
        compiler_params=_cp(2))(c_arr, h2, gate, up, dgate, dup, df)


def _outproj_bwd(dmix, w_out, pool_o, attn_o, c_arr, dep=None):
    s = dmix.shape[0]
    tm = min(TM, s)
    nt = s // tm

    def body(c_ref, dm_ref, w_ref, p_ref, a_ref, *rest):
        dpool_ref, dattn_ref, own_ref, oth_ref, gw_ref = rest[-5:]
        i = pl.program_id(0)

        @pl.when(i == 0)
        def _():
            gw_ref[...] = jnp.zeros_like(gw_ref)

        dm = dm_ref[...]
        dpool_ref[...] = _dot_nt(dm, w_ref[0:512, :])
        dattn_ref[...] = _dot_nt(dm, w_ref[512:1024, :]).astype(BF16)
        gw_ref[0:512, :] += _dot_tn(p_ref[...], dm)
        gw_ref[512:1024, :] += _dot_tn(a_ref[...], dm)

        @pl.when(i == nt - 1)
        def _():
            for k in range(NSH):
                _emit_halves(gw_ref, k * WOUT_S, WOUT_S, c_ref[0], own_ref.at[k], oth_ref.at[k])

    row = lambda w: pl.BlockSpec((tm, w), lambda i: (i, 0))
    full = pl.BlockSpec((D, D), lambda i: (0, 0))
    half = pl.BlockSpec((NSH, WOUT_S // 2, D), lambda i: (0, 0, 0))
    return pl.pallas_call(
        body, name="outproj_bwd", grid=(nt,),
        in_specs=[SMEM_SPEC, row(D), full, row(512), row(512)] + ([] if dep is None else [ANY]),
        out_specs=[row(512), row(512), half, half],
        out_shape=[jax.ShapeDtypeStruct((s, 512), F32), jax.ShapeDtypeStruct((s, 512), BF16)] + _half_shapes(WOUT_S),
        scratch_shapes=[pltpu.VMEM((D, D), F32)],
        compiler_params=_cp(1))(c_arr, dmix, w_out, pool_o, attn_o, *([] if dep is None else [dep]))


def _pool_bwd(u, dpool, w_pool, pool_scale, dep=None):
    s = u.shape[0]
    tm = min(TM_POOL, s)
    hb = tm // HALO
    nt = s // tm
    last_halo = s // HALO - 1

    def body(uc_ref, uh_ref, dc_ref, dn_ref, wp_ref, sc_ref, *rest):
        du_ref, gwp_ref, gsc_ref = rest[-3:]
        i = pl.program_id(0)

        @pl.when(i == 0)
        def _():
            gwp_ref[...] = jnp.zeros_like(gwp_ref)
            gsc_ref[...] = jnp.zeros_like(gsc_ref)

        cur = uc_ref[...]
        halo = jnp.where(i > 0, uh_ref[...], 0.0)
        pooled = _pooled(jnp.concatenate([halo, cur], axis=0), cur, i * tm, tm)
        dcur = dc_ref[...]
        dnext = jnp.where(i < nt - 1, dn_ref[...], 0.0)
        dext = jnp.concatenate([dcur, dnext], axis=0)
        t_ext = i * tm + lax.broadcasted_iota(jnp.int32, (tm + HALO, GROUP), 0)
        for g, w in enumerate(WINDOWS):
            sl = slice(g * GROUP, (g + 1) * GROUP)
            pb = pooled[g].astype(BF16)
            wp = wp_ref[g]
            mixed = _dot(pb, wp)
            gsc_ref[:, sl] += jnp.sum(dcur[:, sl] * mixed, axis=0, keepdims=True)
            dmixed = (dext[:, sl] * sc_ref[:, sl]).astype(BF16)
            gwp_ref[g] += _dot_tn(pb, dmixed[0:tm])
            dpooled = _dot_nt(dmixed, wp)
            z = dpooled / jnp.minimum(t_ext + 1, w).astype(F32)
            du_ref[:, sl] = (_window_sums(z, w, -1, tm, 2) - dpooled[0:tm]).astype(BF16)

    return pl.pallas_call(
        body, name="pool_bwd", grid=(nt,),
        in_specs=[pl.BlockSpec((tm, 512), lambda i: (i, 0)),
                  pl.BlockSpec((HALO, 512), lambda i: (jnp.maximum(i * hb - 1, 0), 0)),
                  pl.BlockSpec((tm, 512), lambda i: (i, 0)),
                  pl.BlockSpec((HALO, 512), lambda i: (jnp.minimum((i + 1) * hb, last_halo), 0)),
                  pl.BlockSpec((4, GROUP, GROUP), lambda i: (0, 0, 0)),
                  pl.BlockSpec((1, 512), lambda i: (0, 0))] + ([] if dep is None else [ANY]),
        out_specs=[pl.BlockSpec((tm, 512), lambda i: (i, 0)),
                   pl.BlockSpec((4, GROUP, GROUP), lambda i: (0, 0, 0)),
                   pl.BlockSpec((1, 512), lambda i: (0, 0))],
        out_shape=[jax.ShapeDtypeStruct((s, 512), BF16), jax.ShapeDtypeStruct((4, GROUP, GROUP), F32),
                   jax.ShapeDtypeStruct((1, 512), F32)],
        compiler_params=_cp(1))(u, u, dpool, dpool, w_pool, pool_scale, *([] if dep is None else [dep]))


def _attn_bwd(q, k, v, do, bias, sinks, dep=None):
    s = q.shape[0]

    def body(q_ref, do_ref, k_ref, v_ref, b_ref, sk_ref, *rest):
        dq_ref, dk_ref, dv_ref, dss_ref, gsk_ref = rest[-5:]
        n = pl.program_id(0)

        @pl.when(n == 0)
        def _():
            dk_ref[...] = jnp.zeros_like(dk_ref)
            dv_ref[...] = jnp.zeros_like(dv_ref)
            dss_ref[...] = jnp.zeros_like(dss_ref)
            gsk_ref[...] = jnp.zeros_like(gsk_ref)

        kk, kt, (lo, pstart, cstart) = _kv_band(k_ref, n, True)
        vv, _, _ = _kv_band(v_ref, n, False)
        bidx = jnp.minimum(n, 1)
        lo_q = lax.broadcasted_iota(jnp.int32, (BLK, 128), 1) < 64
        dkk = [jnp.zeros((2 * BLK, 128), F32), jnp.zeros((2 * BLK, 128), F32)]
        dvv = [jnp.zeros((2 * BLK, 128), F32), jnp.zeros((2 * BLK, 128), F32)]
        for p in range(4):
            h = p // 2
            qt = q_ref[:, p * 128:(p + 1) * 128].astype(F32) * SCALE
            dot = do_ref[:, p * 128:(p + 1) * 128]
            dq_t = jnp.zeros((128, BLK), F32)
            for e in range(2):
                head = 2 * p + e
                sel_q = lo_q if e == 0 else jnp.logical_not(lo_q)
                qm = jnp.where(sel_q, qt, 0.0).astype(BF16)
                prob, ps = _attn_probs(qm, kk[h], b_ref[bidx, head], sk_ref[0, head])
                dom = jnp.where(sel_q, dot, jnp.zeros_like(dot))
                dp = _dot_nt(vv[h], dom)
                delta = jnp.sum(prob * dp, axis=0, keepdims=True)
                ds = prob * (dp - delta)
                gsk_ref[pl.ds(head, 1), :] += jnp.broadcast_to(-jnp.sum(ps * delta).reshape(1, 1), (1, 128))
                dss_ref[head] += ds
                dsb = ds.astype(BF16)
                dq_t = dq_t + _dot(kt[h][e], dsb)
                dkk[h] = dkk[h] + _dot(dsb, qm)
                dvv[h] = dvv[h] + _dot(prob.astype(BF16), dom)
            dq_ref[:, p * 128:(p + 1) * 128] = (dq_t.T * SCALE).astype(BF16)
        for acc, ref in ((dkk, dk_ref), (dvv, dv_ref)):
            f0 = acc[0] + pltpu.roll(acc[0], 64, axis=1)
            f1 = acc[1] + pltpu.roll(acc[1], 64, axis=1)
            band = jnp.where(lo, f0, f1)
            ref[pl.ds(pstart, BLK), :] += band[0:BLK]
            ref[pl.ds(cstart, BLK), :] += band[BLK:2 * BLK]

    blk = pl.BlockSpec((BLK, 512), lambda i: (i, 0))
    kv = pl.BlockSpec((s, 128), lambda i: (0, 0))
    return pl.pallas_call(
        body, name="attn_bwd", grid=(s // BLK,),
        in_specs=[blk, blk, kv, kv, pl.BlockSpec((2, NQ, 2 * BLK, BLK), lambda i: (0, 0, 0, 0)),
                  pl.BlockSpec(memory_space=pltpu.SMEM)] + ([] if dep is None else [ANY]),
        out_specs=[blk, kv, kv, pl.BlockSpec((NQ, 2 * BLK, BLK), lambda i: (0, 0, 0)),
                   pl.BlockSpec((NQ, 128), lambda i: (0, 0))],
        out_shape=[jax.ShapeDtypeStruct((s, 512), BF16), jax.ShapeDtypeStruct((s, 128), F32),
                   jax.ShapeDtypeStruct((s, 128), F32), jax.ShapeDtypeStruct((NQ, 2 * BLK, BLK), F32),
                   jax.ShapeDtypeStruct((NQ, 128), F32)],
        compiler_params=_cp(1))(q, do, k, v, bias, sinks, *([] if dep is None else [dep]))


def _relbias_grad(dss, bucket):
    def body(ds_ref, b_ref, o_ref):
        bucket_v = b_ref[...]
        lane = lax.broadcasted_iota(jnp.int32, (NQ, 128), 1)
        head_row = lax.broadcasted_iota(jnp.int32, (NQ, BLK), 0)
        acc = jnp.zeros((NQ, 128), F32)
        for b in range(NBUCKET):
            sel = bucket_v == b
            stack = jnp.zeros((NQ, BLK), F32)
            for h in range(NQ):
                col_sums = jnp.sum(jnp.where(sel, ds_ref[h], 0.0), axis=0, keepdims=True)
                stack = jnp.where(head_row == h, col_sums, stack)
            acc = acc + jnp.where(lane == b, jnp.sum(stack, axis=1, keepdims=True), 0.0)
        o_ref[...] = acc

    return pl.pallas_call(
        body, name="relbias_grad",
        in_specs=[pl.BlockSpec(memory_space=pltpu.VMEM), pl.BlockSpec(memory_space=pltpu.VMEM)],
        out_specs=pl.BlockSpec(memory_space=pltpu.VMEM),
        out_shape=jax.ShapeDtypeStruct((NQ, 128), F32),
        compiler_params=_cp())(dss, bucket)


def _inproj_bwd(x, g1, du, dq, dk, dv, w_in, dx1, c_arr, dep=None):
    s = x.shape[0]
    tm = min(TM, s)
    nt = s // tm
    cols = ((0, 512), (512, 1024), (1024, 1152), (1152, 1280))

    def body(c_ref, x_ref, g_ref, du_ref, dq_ref, dk_ref, dv_ref, w_ref, dx1_ref, *rest):
        gx_ref, own_ref, oth_ref, gg_ref, gw_ref = rest[-5:]
        i = pl.program_id(0)

        @pl.when(i == 0)
        def _():
            gw_ref[...] = jnp.zeros_like(gw_ref)
            gg_ref[...] = jnp.zeros_like(gg_ref)

        gv = g_ref[...]
        r1, n1 = _rms(x_ref[...])
        h = (n1 * gv).astype(BF16)
        parts = (du_ref[...], dq_ref[...], dk_ref[...].astype(BF16), dv_ref[...].astype(BF16))
        dh = None
        for (a, b), dpart in zip(cols, parts):
            t = _dot(dpart, w_ref[a:b, :])
            dh = t if dh is None else dh + t
            gw_ref[a:b, :] += _dot_tn(dpart, h)
        dx, dg = _rms_bwd(r1, n1, gv, dh)
        gg_ref[...] += dg
        gx_ref[...] = dx1_ref[...] + dx

        @pl.when(i == nt - 1)
        def _():
            for k in range(NSH):
                _emit_halves(gw_ref, k * WIN_S, WIN_S, c_ref[0], own_ref.at[k], oth_ref.at[k])

    row = lambda w: pl.BlockSpec((tm, w), lambda i: (i, 0))
    vec = pl.BlockSpec((1, D), lambda i: (0, 0))
    full = pl.BlockSpec((IN_W, D), lambda i: (0, 0))
    half = pl.BlockSpec((NSH, WIN_S // 2, D), lambda i: (0, 0, 0))
    return pl.pallas_call(
        body, name="inproj_bwd", grid=(nt,),
        in_specs=[SMEM_SPEC, row(D), vec, row(512), row(512), row(128), row(128), full, row(D)]
        + ([] if dep is None else [ANY]),
        out_specs=[row(D), half, half, vec],
        out_shape=[jax.ShapeDtypeStruct((s, D), F32)] + _half_shapes(WIN_S) + [jax.ShapeDtypeStruct((1, D), F32)],
        scratch_shapes=[pltpu.VMEM((IN_W, D), F32)],
        compiler_params=_cp(1))(c_arr, x, g1, du, dq, dk, dv, w_in, dx1, *([] if dep is None else [dep]))


def _local_step(x, target, small, w_in, w_out, ffn_weights, c_arr, on_ffn_grads=None, on_wout_grads=None,
                on_attn_grads=None):
    g1, g2, g3, g4, w_pool, pool_scale, rel_bias, sinks = small
    bucket = jnp.asarray(_bucket_table())
    wp_bf = w_pool.astype(BF16)
    u, q, k, v = _inproj_fwd(x, g1, w_in)
    pool_o = _pool_fwd(u, wp_bf, pool_scale)
    bias = _bias_table(bucket, rel_bias)
    attn_o = _attn_fwd(q, k, v, bias, sinks)
    g2_fwd = g2
    if callable(w_out):
        w_out, after = w_out(pool_o, attn_o)
        g2_fwd = g2 + after[:1, :1]
    mix, x1, h2 = _outproj_fwd(pool_o, attn_o, w_out, x, g2_fwd, g3)
    wg, wu, wd = ffn_weights(h2) if callable(ffn_weights) else ffn_weights
    gate, up, df, dy, loss, gg4 = _ffn_fwd(h2, wg, wu, wd, x1, target, g4)
    dgate, dup, dx1, dmix, gg3, gg2 = _ffn_bwd_act(df, gate, up, wg, wu, wd, dy, x1, mix, g3, g2)
    ffn_g = _ffn_bwd_w(h2, gate, up, dgate, dup, df, c_arr)
    dep = None if on_ffn_grads is None else on_ffn_grads(ffn_g)
    dpool, dattn, wout_own, wout_oth = _outproj_bwd(dmix, w_out, pool_o, attn_o, c_arr, dep)
    dep = None if on_wout_grads is None else on_wout_grads(wout_own, wout_oth, dpool)
    du, gwp, gsc = _pool_bwd(u, dpool, wp_bf, pool_scale, dep)
    dq, dk, dv, dss, gsk = _attn_bwd(q, k, v, dattn, bias, sinks, dep)
    grb = _relbias_grad(dss, bucket)
    dep = None if on_attn_grads is None else on_attn_grads([du, dq])
    gx, win_own, win_oth, gg1 = _inproj_bwd(x, g1, du, dq, dk, dv, w_in, dx1, c_arr, dep)
    small_grads = (gg1, gg2, gg3, gg4, gwp, gsc, grb[:, :NBUCKET].T, gsk[:, 0].reshape(1, NQ))
    return loss, gx, small_grads, ((win_own, win_oth), (wout_own, wout_oth), ffn_g)


def _place():
    x, y, c = lax.axis_index("x"), lax.axis_index("y"), lax.axis_index("c")
    chips = ((1 - x, y), (x, 1 - y), (1 - x, 1 - y))
    return x, y, c, chips


def _remote(src, dst, ssem, rsem, dev):
    return pltpu.make_async_remote_copy(src_ref=src, dst_ref=dst, send_sem=ssem, recv_sem=rsem,
                                        device_id=dev, device_id_type=MESH_T)


def _own_slot(shard):
    me = 2 * lax.axis_index("x") + lax.axis_index("y")
    return lax.dynamic_update_slice(lax.empty((NSH,) + shard.shape, shard.dtype), shard[None], (me, 0, 0))


def _to_sibling(arrs, name):
    nt = len(arrs)

    def body(*refs):
        srcs, dsts = refs[:nt], refs[nt:2 * nt]
        ssem, rsem = refs[2 * nt:]
        x, y, c, _ = _place()
        cps = [_remote(srcs[t], dsts[t], ssem.at[t], rsem.at[t], (x, y, 1 - c)) for t in range(nt)]
        for cp in cps:
            cp.start()
        for cp in cps:
            cp.wait()

    return pl.pallas_call(
        body, name=name, in_specs=[ANY] * nt, out_specs=[ANY] * nt,
        out_shape=[jax.ShapeDtypeStruct(a.shape, a.dtype) for a in arrs],
        scratch_shapes=[pltpu.SemaphoreType.DMA((nt,)), pltpu.SemaphoreType.DMA((nt,))],
        compiler_params=_cp())(*arrs)


HBM_SPEC = pl.BlockSpec(memory_space=pltpu.HBM)
SEM_SPEC = pl.BlockSpec(memory_space=pltpu.SEMAPHORE)
DATAFLOW = pltpu.SideEffectType.DATAFLOW_SIDE_EFFECTING


def _gather_copies(srcs, lands, ssem, rsem):
    x, y, c, chips = _place()
    me = 2 * x + y
    out = []
    for t in range(len(srcs)):
        half = srcs[t].shape[0] // 2
        mine = pl.ds(pl.multiple_of(c * half, 16), half)
        for j, (px, py) in enumerate(chips):
            k = 3 * t + j
            send = _remote(srcs[t].at[mine], lands[t].at[me, mine], ssem.at[k], rsem.at[k], (px, py, c))
            blk = lands[t].at[2 * px + py, mine]
            out.append((send, _remote(blk, blk, ssem.at[k], rsem.at[k], (px, py, c))))
    return out


def _scatter_copies(srcs, lands, ssem, rsem):
    x, y, c, chips = _place()
    out = []
    for t in range(len(srcs)):
        for j, (px, py) in enumerate(chips):
            k = 3 * t + j
            send = _remote(srcs[t].at[2 * px + py], lands[t].at[j], ssem.at[k], rsem.at[k], (px, py, c))
            out.append((send, _remote(lands[t].at[j], lands[t].at[j], ssem.at[k], rsem.at[k], (px, py, c))))
    return out


def _sibling_copies(srcs, lands, ssem, rsem):
    x, y, c, _ = _place()
    sib = (x, y, 1 - c)
    return [(_remote(srcs[t], lands[t], ssem.at[t], rsem.at[t], sib),
             _remote(lands[t], lands[t], ssem.at[t], rsem.at[t], sib)) for t in range(len(srcs))]


def _forward_copies(srcs, lands, ssem, rsem):
    x, y, c, chips = _place()
    sib = (x, y, 1 - c)
    out = []
    for t in range(len(lands)):
        half = lands[t].shape[1] // 2
        mine = pl.ds(pl.multiple_of(c * half, 16), half)
        other = pl.ds(pl.multiple_of((1 - c) * half, 16), half)
        for j, (px, py) in enumerate(chips):
            k = 3 * t + j
            blk_m, blk_o = lands[t].at[2 * px + py, mine], lands[t].at[2 * px + py, other]
            out.append((_remote(blk_m, blk_m, ssem.at[k], rsem.at[k], sib),
                        _remote(blk_o, blk_o, ssem.at[k], rsem.at[k], sib)))
    return out


def _peer_copies(srcs, lands, ssem, rsem):
    x, y, c, _ = _place()
    me = 4 * x + 2 * y + c
    out = []
    for kk in range(1, 8):
        px, py, pc = x ^ (kk >> 2), y ^ ((kk >> 1) & 1), c ^ (kk & 1)
        send = _remote(srcs[0], lands[0].at[me], ssem.at[kk - 1], rsem.at[kk - 1], (px, py, pc))
        slot = lands[0].at[4 * px + 2 * py + pc]
        out.append((send, _remote(slot, slot, ssem.at[kk - 1], rsem.at[kk - 1], (px, py, pc))))
    return out


def _split_start(plan, ncopy, srcs, lands, after, name):
    ns, nbuf = len(srcs), len(srcs) + len(lands)
    extra = [] if after is None else [after]
    n_in = nbuf + len(extra)

    def body(*refs):
        ssem, rsem, token = refs[n_in], refs[n_in + 1], refs[-1]
        for send, _ in plan(refs[:ns], refs[ns:nbuf], ssem, rsem):
            send.start()
        token[...] = jnp.zeros_like(token)

    bufs = [pltpu.with_memory_space_constraint(a, pltpu.HBM) for a in list(srcs) + list(lands)]
    outs = pl.pallas_call(
        body, name=name, in_specs=[HBM_SPEC] * nbuf + [ANY] * len(extra),
        out_specs=[SEM_SPEC, SEM_SPEC] + [HBM_SPEC] * nbuf + [pl.BlockSpec(memory_space=pltpu.VMEM)],
        out_shape=[pltpu.SemaphoreType.DMA((ncopy,)), pltpu.SemaphoreType.DMA((ncopy,))]
        + [pltpu.HBM(a.shape, a.dtype) for a in bufs] + [jax.ShapeDtypeStruct((8, 128), F32)],
        input_output_aliases={i: 2 + i for i in range(nbuf)},
        compiler_params=pltpu.CompilerParams(has_side_effects=DATAFLOW))(*bufs, *extra)
    return outs[0], outs[1], outs[2:2 + ns], outs[2 + ns:2 + nbuf], outs[-1]


def _split_wait(plan, ssem, rsem, srcs, lands, after, name, only=None):
    ns, nbuf = len(srcs), len(srcs) + len(lands)

    def body(*refs):
        s_ref, r_ref = refs[nbuf], refs[nbuf + 1]
        for k, (send, recv) in enumerate(plan(refs[:ns], refs[ns:nbuf], s_ref, r_ref)):
            if only is None or k in only:
                send.wait_send()
                recv.wait_recv()

    outs = pl.pallas_call(
        body, name=name, in_specs=[HBM_SPEC] * nbuf + [SEM_SPEC, SEM_SPEC] + [ANY] * len(after),
        out_specs=[HBM_SPEC] * nbuf,
        out_shape=[pltpu.HBM(a.shape, a.dtype) for a in list(srcs) + list(lands)],
        input_output_aliases={i: i for i in range(nbuf)},
        compiler_params=pltpu.CompilerParams(has_side_effects=DATAFLOW))(*srcs, *lands, ssem, rsem, *after)
    return outs


def _gather_finish(lands):
    nt = len(lands)

    def body(*refs):
        outs = refs[nt:2 * nt]
        dsend, drecv = refs[2 * nt:]
        x, y, c, chips = _place()
        sib = (x, y, 1 - c)
        sends = []
        for t in range(nt):
            half = outs[t].shape[1] // 2
            mine = pl.ds(pl.multiple_of(c * half, 16), half)
            for j, (px, py) in enumerate(chips):
                blk = outs[t].at[2 * px + py, mine]
                cp = _remote(blk, blk, dsend.at[t, j], drecv.at[t, j], sib)
                cp.start()
                sends.append(cp)
        for t in range(nt):
            half = outs[t].shape[1] // 2
            other = pl.ds(pl.multiple_of((1 - c) * half, 16), half)
            for j, (px, py) in enumerate(chips):
                blk = outs[t].at[2 * px + py, other]
                _remote(blk, blk, dsend.at[t, j], drecv.at[t, j], sib).wait_recv()
        for cp in sends:
            cp.wait_send()

    return pl.pallas_call(
        body, name="gather_finish", in_specs=[ANY] * nt, out_specs=[ANY] * nt,
        out_shape=[jax.ShapeDtypeStruct(a.shape, a.dtype) for a in lands],
        input_output_aliases={t: t for t in range(nt)},
        scratch_shapes=[pltpu.SemaphoreType.DMA((nt, 3)), pltpu.SemaphoreType.DMA((nt, 3))],
        compiler_params=_cp())(*lands)


def _chip_partial(owns, recv, chip_arr, tag):
    nt = len(owns)

    def body(chip_ref, *refs):
        k = pl.program_id(0)
        for t in range(nt):
            p = refs[t][...] + refs[nt + t][...].astype(F32)
            refs[2 * nt + t][...] = p.astype(BF16)

            @pl.when(k == chip_ref[0])
            def _():
                refs[3 * nt + t][...] = p

    blk_specs = [pl.BlockSpec((None,) + a.shape[1:], lambda k, chip_ref: (k, 0, 0)) for a in owns]
    own_specs = [pl.BlockSpec(a.shape[1:], lambda k, chip_ref: (0, 0)) for a in owns]
    return pl.pallas_call(
        body, name="rs_chip_partial_" + tag,
        grid_spec=pltpu.PrefetchScalarGridSpec(num_scalar_prefetch=1, grid=(NSH,), in_specs=blk_specs * 2,
                                               out_specs=blk_specs + own_specs),
        out_shape=[jax.ShapeDtypeStruct(a.shape, BF16) for a in owns]
        + [jax.ShapeDtypeStruct(a.shape[1:], F32) for a in owns],
        compiler_params=_cp(1))(chip_arr, *owns, *recv)


def _sum_chips(own, recv, tag, after=()):
    nt = len(own)

    def body(*refs):
        outs = refs[2 * nt + len(after):]
        for t in range(nt):
            r = refs[nt + t]
            outs[t][...] = ((refs[t][...] + r[0].astype(F32)) + r[1].astype(F32)) + r[2].astype(F32)

    vm = pl.BlockSpec(memory_space=pltpu.VMEM)
    return pl.pallas_call(
        body, name="rs_sum_chips_" + tag, in_specs=[vm] * (2 * nt) + [ANY] * len(after), out_specs=[vm] * nt,
        out_shape=[jax.ShapeDtypeStruct(a.shape, F32) for a in own],
        compiler_params=_cp())(*own, *recv, *after)


def _adamw_math(w, g, m, v):
    m = ADAM_B1 * m + (1.0 - ADAM_B1) * g
    v = ADAM_B2 * v + (1.0 - ADAM_B2) * (g * g)
    m_hat = m / (1.0 - ADAM_B1 ** ADAM_STEP)
    v_hat = v / (1.0 - ADAM_B2 ** ADAM_STEP)
    delta = -ADAM_LR * (m_hat / (jnp.sqrt(v_hat) + ADAM_EPS) + ADAM_WD * w)
    return delta, m, v


ADAM_SPLIT = 4


def _adamw_shards(own, sib, ws, ms, vs, c_arr, tag):
    nt = len(own)

    def body(c_ref, *refs):
        hh = pl.program_id(0)
        mine = hh == c_ref[0]
        for t in range(nt):
            g = jnp.where(mine, refs[t][...], refs[nt + t][...])
            delta, m, v = _adamw_math(refs[2 * nt + t][...], g, refs[3 * nt + t][...], refs[4 * nt + t][...])
            refs[5 * nt + t][...] = g
            refs[6 * nt + t][...] = delta
            refs[7 * nt + t][...] = m
            refs[8 * nt + t][...] = v

    def tile(a):
        return (a.shape[0] // ADAM_SPLIT, a.shape[1])

    half_specs = [pl.BlockSpec(tile(a), lambda hh, q, c_ref: (q, 0)) for a in own]
    full_specs = [pl.BlockSpec(tile(a), lambda hh, q, c_ref: (hh * ADAM_SPLIT + q, 0)) for a in own]
    return pl.pallas_call(
        body, name="adamw_shards_" + tag,
        grid_spec=pltpu.PrefetchScalarGridSpec(num_scalar_prefetch=1, grid=(2, ADAM_SPLIT),
                                               in_specs=half_specs * 2 + full_specs * 3, out_specs=full_specs * 4),
        out_shape=[jax.ShapeDtypeStruct(w.shape, F32) for w in ws] * 4,
        compiler_params=_cp(2))(c_arr, *own, *sib, *ws, *ms, *vs)


SMALL_TAIL_ROW = 552


def _small_sum_adamw(gathered, wp, mp, vp):
    rows = wp.shape[0]

    def body(g_ref, w_ref, m_ref, v_ref, *rest):
        outs, res = rest[:-1], rest[-1]
        g = g_ref[0]
        for d in range(1, 8):
            g = g + g_ref[d]
        delta, m, v = _adamw_math(w_ref[...], g, m_ref[...], v_ref[...])
        for kind, val in enumerate((g, delta, m, v)):
            res[kind] = val
            gains, w_pool_o, scale_o, tail_o = outs[7 * kind:7 * kind + 4], *outs[7 * kind + 4:7 * kind + 7]
            for t in range(4):
                for i in range(8):
                    gains[t][:, 128 * i:128 * (i + 1)] = res[kind, pl.ds(8 * t + i, 1), :]
            for grp in range(4):
                w_pool_o[grp] = res[kind, pl.ds(32 + GROUP * grp, GROUP), :]
            for i in range(4):
                scale_o[:, 128 * i:128 * (i + 1)] = res[kind, pl.ds(544 + i, 1), :]
            tail_o[...] = res[kind, pl.ds(SMALL_TAIL_ROW, rows - SMALL_TAIL_ROW), :]

    vm = pl.BlockSpec(memory_space=pltpu.VMEM)
    one_kind = [jax.ShapeDtypeStruct((1, D), F32)] * 4 + [
        jax.ShapeDtypeStruct((4, GROUP, GROUP), F32), jax.ShapeDtypeStruct((1, POOL_W), F32),
        jax.ShapeDtypeStruct((rows - SMALL_TAIL_ROW, 128), F32)]
    return pl.pallas_call(
        body, name="small_sum_adamw", in_specs=[vm] * 4, out_specs=[vm] * 28,
        out_shape=one_kind * 4,
        scratch_shapes=[pltpu.VMEM((4, rows, 128), F32)],
        compiler_params=_cp())(gathered, wp, mp, vp)


def _pack_small(parts):
    rows = []
    for a in parts:
        flat = a.reshape(-1)
        n = flat.shape[0]
        padded = -(-n // 1024) * 1024
        rows.append(jnp.pad(flat, (0, padded - n)).reshape(-1, 128))
    return jnp.concatenate(rows, axis=0)


def kernel(x, g_pre_mix, w_in, w_pool, pool_scale, rel_bias, sinks, w_out, g_post_mix, g_pre_ffn, w_gate, w_up, w_down, g_post_ffn, loss_target, m_g_pre_mix, m_w_in, m_w_pool, m_pool_scale, m_rel_bias, m_sinks, m_w_out, m_g_post_mix, m_g_pre_ffn, m_w_gate, m_w_up, m_w_down, m_g_post_ffn, v_g_pre_mix, v_w_in, v_w_pool, v_pool_scale, v_rel_bias, v_sinks, v_w_out, v_g_post_mix, v_g_pre_ffn, v_w_gate, v_w_up, v_w_down, v_g_post_ffn):
    c = lax.axis_index("c")
    chip = 2 * lax.axis_index("x") + lax.axis_index("y")

    def shards(a_in, a_out, a_gate, a_up, a_down):
        return (a_in[0].T, a_out[0], a_gate[0].T, a_up[0].T, a_down[0])

    c_arr = c.reshape(1).astype(jnp.int32)
    chip_arr = chip.reshape(1).astype(jnp.int32)

    big_w = shards(w_in, w_out, w_gate, w_up, w_down)
    big_bf = [w.astype(BF16) for w in big_w]
    g_ssem, g_rsem, g_srcs, g_lands, _ = _split_start(
        _gather_copies, 15, big_bf, [_own_slot(a) for a in big_bf], None, "gather_start")
    first = _split_wait(_gather_copies, g_ssem, g_rsem, g_srcs, g_lands, [], "gather_win_wait", only=(0, 1, 2))
    (win_all,) = _gather_finish(first[5:6])
    fw = {}

    def w_out_ready(*after):
        lands = _split_wait(_gather_copies, g_ssem, g_rsem, first[:5], [win_all] + list(first[6:]), after,
                            "gather_rest_wait", only=tuple(range(3, 15)))[5:]
        (wout_all,) = _gather_finish(lands[1:2])
        fw["f"] = _split_start(_forward_copies, 9, [], lands[2:], wout_all, "gather_forward_start")
        return wout_all.reshape(D, D), fw["f"][4]

    def ffn_weights(after):
        ssem, rsem, _, lands, _ = fw["f"]
        return _split_wait(_forward_copies, ssem, rsem, [], lands, [after], "gather_forward_wait")

    rs = {}

    def on_ffn_grads(ffn_g):
        oths = ffn_g[1::2]
        rs["ffn_own"] = ffn_g[0::2]
        rs["x"] = _split_start(_sibling_copies, 3, oths, [lax.empty(a.shape, BF16) for a in oths], ffn_g[0],
                               "rs_exchange_ffn_start")
        return rs["x"][4]

    def on_wout_grads(own, oth, after):
        ssem, rsem, srcs, lands, _ = rs["x"]
        recv_ffn = _split_wait(_sibling_copies, ssem, rsem, srcs, lands, [after], "rs_exchange_ffn_wait")[3:]
        recv_wout = _to_sibling([oth], "rs_exchange_wout")
        outs = _chip_partial([own] + list(rs["ffn_own"]), list(recv_wout) + list(recv_ffn), chip_arr, "early")
        rs["early_own"] = outs[4:]
        rs["s"] = _split_start(_scatter_copies, 12, outs[:4],
                               [lax.empty((3,) + a.shape[1:], BF16) for a in outs[:4]], outs[4], "scatter_early_start")
        return rs["s"][4]

    def on_attn_grads(after):
        ssem, rsem, srcs, lands, _ = rs["s"]
        recv_early = _split_wait(_scatter_copies, ssem, rsem, srcs, lands, after, "scatter_early_wait")[4:]
        finals = _sum_chips(list(rs["early_own"]), list(recv_early), "early")
        rs["sh"] = _split_start(_sibling_copies, 4, finals, [lax.empty(a.shape, F32) for a in finals], finals[0],
                                "rs_share_early_start")
        return rs["sh"][4]

    small = (g_pre_mix, g_post_mix, g_pre_ffn, g_post_ffn, w_pool[0], pool_scale, rel_bias, sinks)
    loss, gx, small_grads, ((win_own, win_oth), _, _) = _local_step(
        x[0], loss_target[0], small, win_all.reshape(IN_W, D), w_out_ready, ffn_weights, c_arr,
        on_ffn_grads, on_wout_grads, on_attn_grads)

    unused = jnp.zeros((1, 1), F32)
    gp = _pack_small(small_grads + (loss,))
    wp = _pack_small((g_pre_mix, g_post_mix, g_pre_ffn, g_post_ffn, w_pool, pool_scale, rel_bias, sinks, unused))
    mp = _pack_small((m_g_pre_mix, m_g_post_mix, m_g_pre_ffn, m_g_post_ffn, m_w_pool, m_pool_scale, m_rel_bias,
                      m_sinks, unused))
    vp = _pack_small((v_g_pre_mix, v_g_post_mix, v_g_pre_ffn, v_g_post_ffn, v_w_pool, v_pool_scale, v_rel_bias,
                      v_sinks, unused))

    moments = (shards(m_w_in, m_w_out, m_w_gate, m_w_up, m_w_down),
               shards(v_w_in, v_w_out, v_w_gate, v_w_up, v_w_down))
    recv_a = _to_sibling([win_oth], "rs_exchange_win")
    outs = _chip_partial([win_own], recv_a, chip_arr, "win")
    w_ssem, w_rsem, w_srcs, w_lands, w_token = _split_start(
        _scatter_copies, 3, outs[:1], [lax.empty((3,) + outs[0].shape[1:], BF16)], gx, "scatter_win_start")
    slots = lax.dynamic_update_slice(lax.empty((8,) + gp.shape, F32), gp[None], (2 * chip + c, 0, 0))
    p_ssem, p_rsem, p_srcs, p_lands, p_token = _split_start(_peer_copies, 7, [gp], [slots], w_token,
                                                            "small_allgather_start")
    ssem, rsem, srcs, lands, _ = rs["sh"]
    shared = _split_wait(_sibling_copies, ssem, rsem, srcs, lands, [p_token], "rs_share_early_wait")
    adam_e = _adamw_shards(shared[:4], shared[4:], big_w[1:], moments[0][1:], moments[1][1:], c_arr, "early")
    recv_win = _split_wait(_scatter_copies, w_ssem, w_rsem, w_srcs, w_lands, [adam_e[0]], "scatter_win_wait")[1:]
    win_final = _sum_chips([outs[1]], recv_win, "win")
    win_sib = _to_sibling(win_final, "rs_share_win")
    adam_w = _adamw_shards(win_final, win_sib, big_w[:1], moments[0][:1], moments[1][:1], c_arr, "win")
    big_g, big_d, big_m, big_v = [[adam_w[i]] + list(adam_e[4 * i:4 * i + 4]) for i in range(4)]

    gathered = _split_wait(_peer_copies, p_ssem, p_rsem, p_srcs, p_lands, [adam_w[0]], "small_allgather_wait")[1]
    flat = _small_sum_adamw(gathered, wp, mp, vp)
    small_out = [flat[7 * kind:7 * kind + 7] for kind in range(4)]
    total_loss = small_out[0][6][16, 0]

    def ordered(small7, big4):
        sg1, sg2, sg3, sg4, swp, ssc, tail = small7
        swp, srb, ssk = swp[None], tail[0:2].reshape(NBUCKET, NQ), tail[8:9, 0:NQ]
        bwin, bwout, bwg, bwu, bwd = big4[0].T[None], big4[1][None], big4[2].T[None], big4[3].T[None], big4[4][None]
        return [sg1, bwin, swp, ssc, srb, ssk, bwout, sg2, sg3, bwg, bwu, bwd, sg4]

    res = [total_loss, gx[None]]
    for sm, bg in zip(small_out, (big_g, big_d, big_m, big_v)):
        res += ordered(sm, bg)
    return tuple(res)
```

```python
import numpy as np
import jax
import jax.numpy as jnp
from jax import lax
from jax.experimental import pallas as pl
from jax.experimental.pallas import tpu as pltpu

F32 = jnp.float32
BF16 = jnp.bfloat16

D = 1024
POOL_W = 512
GROUP = 128
WINDOWS = (2, 4, 8, 16)
HALO = 128
NQ = 8
BLK = 128
NBUCKET = 32
IN_W = 1280
DFF = 2816
NSH = 4
FS = DFF // NSH
WIN_S = IN_W // NSH
WOUT_S = D // NSH
EPS = 1e-6
NEG = -1e30
SCALE = 0.125

ADAM_LR = 0.001
ADAM_B1 = 0.9
ADAM_B2 = 0.999
ADAM_EPS = 1e-08
ADAM_WD = 0.01
ADAM_STEP = 10

TM = 512
TM_POOL = 256
TM_FFN = 512
TM_FFN_FWD = 1024
FFN_ROWS = 256
VMEM_LIMIT = 60 * 1024 * 1024

MESH_T = pl.DeviceIdType.MESH
ANY = pl.BlockSpec(memory_space=pl.ANY)


def _cp(n_grid=0, **kw):
    sem = ("arbitrary",) * n_grid if n_grid else None
    return pltpu.CompilerParams(dimension_semantics=sem, vmem_limit_bytes=VMEM_LIMIT, **kw)


def _dot(a, b):
    return jnp.dot(a, b, preferred_element_type=F32)


def _dot_nt(a, b):
    return lax.dot_general(a, b, (((1,), (1,)), ((), ())), preferred_element_type=F32)


def _dot_tn(a, b):
    return lax.dot_general(a, b, (((0,), (0,)), ((), ())), preferred_element_type=F32)


def _rms(x):
    r = lax.rsqrt(jnp.mean(x * x, axis=-1, keepdims=True) + EPS)
    return r, x * r


def _rms_bwd(r, n, g, dout):
    dn = dout * g
    dx = r * (dn - n * jnp.mean(dn * n, axis=-1, keepdims=True))
    dg = jnp.sum(dout * n, axis=0, keepdims=True)
    return dx, dg


def _split(x, n):
    parts = []
    r = x
    for _ in range(n):
        p = r.astype(BF16)
        parts.append(p)
        r = r - p.astype(F32)
    return parts


def _band_dot(a, x, n):
    acc = None
    for p in _split(x, n):
        t = _dot(a, p)
        acc = t if acc is None else acc + t
    return acc


def _bucket_table():
    qi = np.arange(BLK)[None, :]
    kj = np.arange(2 * BLK)[:, None]
    dist = qi + BLK - kj
    n = np.maximum(dist, 0)
    nf = np.maximum(n, 1).astype(np.float32)
    large = 16 + (np.log(nf / np.float32(16)) / np.float32(np.log(128 / 16)) * np.float32(16)).astype(np.int32)
    large = np.minimum(large, NBUCKET - 1)
    return np.where(n < 16, n, large).astype(np.int32)


def _prenorm(x, g1):
    s = x.shape[0]
    tm = min(TM, s)

    def body(x_ref, g_ref, h_ref):
        _, n = _rms(x_ref[...])
        h_ref[...] = (n * g_ref[...]).astype(BF16)

    row = pl.BlockSpec((tm, D), lambda i: (i, 0))
    return pl.pallas_call(
        body, name="prenorm", grid=(s // tm,),
        in_specs=[row, pl.BlockSpec((1, D), lambda i: (0, 0))], out_specs=row,
        out_shape=jax.ShapeDtypeStruct((s, D), BF16),
        compiler_params=_cp(1))(x, g1)


def _inproj_fwd(h1, w_in):
    s = h1.shape[0]
    tm = min(TM, s)

    def body(h_ref, w_ref, u_ref, q_ref, k_ref, v_ref):
        proj = _dot_nt(h_ref[...], w_ref[...])
        u_ref[...] = proj[:, :512]
        q_ref[...] = proj[:, 512:1024].astype(BF16)
        k_ref[...] = proj[:, 1024:1152].astype(BF16)
        v_ref[...] = proj[:, 1152:1280].astype(BF16)

    row = lambda w: pl.BlockSpec((tm, w), lambda i: (i, 0))
    return pl.pallas_call(
        body, name="inproj_fwd", grid=(s // tm,),
        in_specs=[row(D), pl.BlockSpec((IN_W, D), lambda i: (0, 0))],
        out_specs=[row(512), row(512), row(128), row(128)],
        out_shape=[jax.ShapeDtypeStruct((s, 512), F32), jax.ShapeDtypeStruct((s, 512), BF16),
                   jax.ShapeDtypeStruct((s, 128), BF16), jax.ShapeDtypeStruct((s, 128), BF16)],
        compiler_params=_cp(1))(h1, w_in)


def _window_sums(ext, w, lag_sign, tm, terms):
    rows = lax.broadcasted_iota(jnp.int32, (HALO, 2 * HALO), 0)
    cols = lax.broadcasted_iota(jnp.int32, (HALO, 2 * HALO), 1)
    d = rows + HALO - cols if lag_sign > 0 else cols - rows
    band = jnp.where((d >= 0) & (d < w), 1.0, 0.0).astype(BF16)
    return jnp.concatenate([_band_dot(band, ext[r:r + 2 * HALO], terms) for r in range(0, tm, HALO)], axis=0)


def _pooled(ext, cur, t0, tm):
    t = t0 + lax.broadcasted_iota(jnp.int32, (tm, GROUP), 0)
    out = []
    for g, w in enumerate(WINDOWS):
        sl = slice(g * GROUP, (g + 1) * GROUP)
        cnt = jnp.minimum(t + 1, w).astype(F32)
        out.append(_window_sums(ext[:, sl], w, 1, tm, 2) / cnt - cur[:, sl])
    return out


def _pool_fwd(u, w_pool, pool_scale):
    s = u.shape[0]
    tm = min(TM_POOL, s)
    hb = tm // HALO

    def body(uc_ref, uh_ref, wp_ref, sc_ref, o_ref):
        i = pl.program_id(0)
        cur = uc_ref[...]
        halo = jnp.where(i > 0, uh_ref[...], 0.0)
        ext = jnp.concatenate([halo, cur], axis=0)
        pooled = _pooled(ext, cur, i * tm, tm)
        for g in range(4):
            sl = slice(g * GROUP, (g + 1) * GROUP)
            mixed = _dot(pooled[g].astype(BF16), wp_ref[g])
            o_ref[:, sl] = (mixed * sc_ref[:, sl]).astype(BF16)

    return pl.pallas_call(
        body, name="pool_fwd", grid=(s // tm,),
        in_specs=[pl.BlockSpec((tm, 512), lambda i: (i, 0)),
                  pl.BlockSpec((HALO, 512), lambda i: (jnp.maximum(i * hb - 1, 0), 0)),
                  pl.BlockSpec((4, GROUP, GROUP), lambda i: (0, 0, 0)),
                  pl.BlockSpec((1, 512), lambda i: (0, 0))],
        out_specs=pl.BlockSpec((tm, 512), lambda i: (i, 0)),
        out_shape=jax.ShapeDtypeStruct((s, 512), BF16),
        compiler_params=_cp(1))(u, u, w_pool, pool_scale)


def _bias_table(bucket, rel_bias):
    def body(b_ref, rb_ref, o_ref):
        bucket_v = b_ref[...]
        key = lax.broadcasted_iota(jnp.int32, (2 * BLK, BLK), 0)
        dist = lax.broadcasted_iota(jnp.int32, (2 * BLK, BLK), 1) + BLK - key
        inwin = (dist >= 0) & (dist < BLK)
        for h in range(NQ):
            acc = jnp.zeros((2 * BLK, BLK), F32)
            for b in range(NBUCKET):
                acc = jnp.where(bucket_v == b, rb_ref[b, h], acc)
            rest = jnp.where(inwin, acc, NEG)
            o_ref[1, h] = rest
            o_ref[0, h] = jnp.where(key >= BLK, rest, NEG)

    return pl.pallas_call(
        body, name="bias_table",
        in_specs=[pl.BlockSpec(memory_space=pltpu.VMEM), pl.BlockSpec(memory_space=pltpu.SMEM)],
        out_specs=pl.BlockSpec(memory_space=pltpu.VMEM),
        out_shape=jax.ShapeDtypeStruct((2, NQ, 2 * BLK, BLK), F32),
        compiler_params=_cp())(bucket, rel_bias)


def _kv_band(ref, n, transposed):
    pstart = pl.multiple_of(jnp.maximum(n - 1, 0) * BLK, BLK)
    cstart = pl.multiple_of(n * BLK, BLK)
    lo = lax.broadcasted_iota(jnp.int32, (2 * BLK, 128), 1) < 64
    band = jnp.concatenate([ref[pl.ds(pstart, BLK), :], ref[pl.ds(cstart, BLK), :]], axis=0).astype(F32)
    rot = pltpu.roll(band, 64, axis=1)
    dup = (jnp.where(lo, band, rot), jnp.where(lo, rot, band))
    plain = [d.astype(BF16) for d in dup]
    if not transposed:
        return plain, None, (lo, pstart, cstart)
    row_lo = lax.broadcasted_iota(jnp.int32, (128, 2 * BLK), 0) < 64
    tr = [[jnp.where(row_lo, d.T, 0.0).astype(BF16), jnp.where(row_lo, 0.0, d.T).astype(BF16)] for d in dup]
    return plain, tr, (lo, pstart, cstart)


def _attn_probs(qm, kk, bias, sink):
    s = _dot_nt(kk, qm) + bias
    m = jnp.maximum(jnp.max(s, axis=0, keepdims=True), sink)
    p = jnp.exp(s - m)
    es = jnp.exp(sink - m)
    inv = 1.0 / (jnp.sum(p, axis=0, keepdims=True) + es)
    return p * inv, es * inv


def _attn_fwd(q, k, v, bias, sinks):
    s = q.shape[0]

    def body(q_ref, k_ref, v_ref, b_ref, sk_ref, o_ref):
        n = pl.program_id(0)
        kk, _, _ = _kv_band(k_ref, n, False)
        _, vt, _ = _kv_band(v_ref, n, True)
        bidx = jnp.minimum(n, 1)
        lo_q = lax.broadcasted_iota(jnp.int32, (BLK, 128), 1) < 64
        for p in range(4):
            h = p // 2
            qt = q_ref[:, p * 128:(p + 1) * 128].astype(F32) * SCALE
            acc_t = jnp.zeros((128, BLK), F32)
            for e in range(2):
                head = 2 * p + e
                qm = jnp.where(lo_q if e == 0 else jnp.logical_not(lo_q), qt, 0.0).astype(BF16)
                prob, _ = _attn_probs(qm, kk[h], b_ref[bidx, head], sk_ref[0, head])
                acc_t = acc_t + _dot(vt[h][e], prob.astype(BF16))
            o_ref[:, p * 128:(p + 1) * 128] = acc_t.T.astype(BF16)

    return pl.pallas_call(
        body, name="attn_fwd", grid=(s // BLK,),
        in_specs=[pl.BlockSpec((BLK, 512), lambda i: (i, 0)),
                  pl.BlockSpec((s, 128), lambda i: (0, 0)),
                  pl.BlockSpec((s, 128), lambda i: (0, 0)),
                  pl.BlockSpec((2, NQ, 2 * BLK, BLK), lambda i: (0, 0, 0, 0)),
                  pl.BlockSpec(memory_space=pltpu.SMEM)],
        out_specs=pl.BlockSpec((BLK, 512), lambda i: (i, 0)),
        out_shape=jax.ShapeDtypeStruct((s, 512), BF16),
        compiler_params=_cp(1))(q, k, v, bias, sinks)


def _outproj_fwd(pool_o, attn_o, w_out, x, g2, g3):
    s = x.shape[0]
    tm = min(TM, s)

    def body(p_ref, a_ref, w_ref, x_ref, g2_ref, g3_ref, mix_ref, x1_ref, h2_ref):
        mix = _dot(p_ref[...], w_ref[0:512, :]) + _dot(a_ref[...], w_ref[512:1024, :])
        mix_ref[...] = mix
        _, n2 = _rms(mix)
        x1 = x_ref[...] + n2 * g2_ref[...]
        x1_ref[...] = x1
        _, n3 = _rms(x1)
        h2_ref[...] = (n3 * g3_ref[...]).astype(BF16)

    row = lambda w: pl.BlockSpec((tm, w), lambda i: (i, 0))
    vec = pl.BlockSpec((1, D), lambda i: (0, 0))
    return pl.pallas_call(
        body, name="outproj_fwd", grid=(s // tm,),
        in_specs=[row(512), row(512), pl.BlockSpec((D, D), lambda i: (0, 0)), row(D), vec, vec],
        out_specs=[row(D), row(D), row(D)],
        out_shape=[jax.ShapeDtypeStruct((s, D), F32), jax.ShapeDtypeStruct((s, D), F32),
                   jax.ShapeDtypeStruct((s, D), BF16)],
        compiler_params=_cp(1))(pool_o, attn_o, w_out, x, g2, g3)


def _silu_parts(g):
    sg = jax.nn.sigmoid(g)
    return sg, g * sg


def _ffn_fwd(h2, wg, wu, wd, x1, target, g4):
    s = h2.shape[0]
    tm = min(TM_FFN_FWD, s)

    def body(h_ref, wg_ref, wu_ref, wd_ref, x1_ref, t_ref, g4_ref,
             gate_ref, up_ref, df_ref, dy_ref, loss_ref, gg4_ref, f_acc):
        i = pl.program_id(0)
        j = pl.program_id(1)
        first = j == 0
        for r in range(0, tm, FFN_ROWS):
            rows = pl.ds(r, min(FFN_ROWS, tm))
            h = h_ref[rows, :]
            gate = _dot_nt(h, wg_ref[...])
            up = _dot_nt(h, wu_ref[...])
            gate_ref[rows, :] = gate.astype(BF16)
            up_ref[rows, :] = up.astype(BF16)
            _, sl = _silu_parts(gate)
            contrib = _dot((sl * up).astype(BF16), wd_ref[...])
            f_acc[rows, :] = jnp.where(first, contrib, f_acc[rows, :] + contrib)

        @pl.when((i == 0) & (j == 0))
        def _():
            loss_ref[...] = jnp.zeros_like(loss_ref)
            gg4_ref[...] = jnp.zeros_like(gg4_ref)

        @pl.when(j == NSH - 1)
        def _():
            r4, n4 = _rms(f_acc[...])
            g4v = g4_ref[...]
            err = x1_ref[...] + n4 * g4v - t_ref[...]
            loss_ref[...] += (0.5 / D) * jnp.sum(err * err).reshape(1, 1)
            dy = err * (1.0 / D)
            dy_ref[...] = dy
            df, dg = _rms_bwd(r4, n4, g4v, dy)
            gg4_ref[...] += dg
            df_ref[...] = df.astype(BF16)

    row = lambda w: pl.BlockSpec((tm, w), lambda i, j: (i, 0))
    sh = lambda r, c: pl.BlockSpec((None, r, c), lambda i, j: (j, 0, 0))
    act = pl.BlockSpec((None, tm, FS), lambda i, j: (j, i, 0))
    vec = pl.BlockSpec((1, D), lambda i, j: (0, 0))
    return pl.pallas_call(
        body, name="ffn_fwd", grid=(s // tm, NSH),
        in_specs=[row(D), sh(FS, D), sh(FS, D), sh(FS, D), row(D), row(D), vec],
        out_specs=[act, act, row(D), row(D), pl.BlockSpec((1, 1), lambda i, j: (0, 0)), vec],
        out_shape=[jax.ShapeDtypeStruct((NSH, s, FS), BF16), jax.ShapeDtypeStruct((NSH, s, FS), BF16),
                   jax.ShapeDtypeStruct((s, D), BF16), jax.ShapeDtypeStruct((s, D), F32),
                   jax.ShapeDtypeStruct((1, 1), F32), jax.ShapeDtypeStruct((1, D), F32)],
        scratch_shapes=[pltpu.VMEM((tm, D), F32)],
        compiler_params=_cp(2))(h2, wg, wu, wd, x1, target, g4)


def _ffn_bwd_act(df, gate, up, wg, wu, wd, dy, x1, mix, g3, g2):
    s = df.shape[0]
    tm = min(TM_FFN, s)

    def body(df_ref, gate_ref, up_ref, wg_ref, wu_ref, wd_ref, dy_ref, x1_ref, mix_ref, g3_ref, g2_ref,
             dgate_ref, dup_ref, dx1_ref, dmix_ref, gg3_ref, gg2_ref, acc):
        i = pl.program_id(0)
        j = pl.program_id(1)
        first = j == 0
        for r in range(0, tm, FFN_ROWS):
            rows = pl.ds(r, min(FFN_ROWS, tm))
            da = _dot_nt(df_ref[rows, :], wd_ref[...])
            g = gate_ref[rows, :].astype(F32)
            u = up_ref[rows, :].astype(F32)
            sg, sl = _silu_parts(g)
            dgate = (da * u * (sg * (1.0 + g * (1.0 - sg)))).astype(BF16)
            dup = (da * sl).astype(BF16)
            dgate_ref[rows, :] = dgate
            dup_ref[rows, :] = dup
            contrib = _dot(dgate, wg_ref[...]) + _dot(dup, wu_ref[...])
            acc[rows, :] = jnp.where(first, contrib, acc[rows, :] + contrib)

        @pl.when((i == 0) & (j == 0))
        def _():
            gg3_ref[...] = jnp.zeros_like(gg3_ref)
            gg2_ref[...] = jnp.zeros_like(gg2_ref)

        @pl.when(j == NSH - 1)
        def _():
            r3, n3 = _rms(x1_ref[...])
            dx1n, dg3 = _rms_bwd(r3, n3, g3_ref[...], acc[...])
            dx1 = dy_ref[...] + dx1n
            dx1_ref[...] = dx1
            gg3_ref[...] += dg3
            r2, n2 = _rms(mix_ref[...])
            dmix, dg2 = _rms_bwd(r2, n2, g2_ref[...], dx1)
            gg2_ref[...] += dg2
            dmix_ref[...] = dmix.astype(BF16)

    row = lambda w: pl.BlockSpec((tm, w), lambda i, j: (i, 0))
    sh = lambda r, c: pl.BlockSpec((None, r, c), lambda i, j: (j, 0, 0))
    act = pl.BlockSpec((None, tm, FS), lambda i, j: (j, i, 0))
    vec = pl.BlockSpec((1, D), lambda i, j: (0, 0))
    return pl.pallas_call(
        body, name="ffn_bwd_act", grid=(s // tm, NSH),
        in_specs=[row(D), act, act, sh(FS, D), sh(FS, D), sh(FS, D), row(D), row(D), row(D), vec, vec],
        out_specs=[act, act, row(D), row(D), vec, vec],
        out_shape=[jax.ShapeDtypeStruct((NSH, s, FS), BF16), jax.ShapeDtypeStruct((NSH, s, FS), BF16),
                   jax.ShapeDtypeStruct((s, D), F32), jax.ShapeDtypeStruct((s, D), BF16),
                   jax.ShapeDtypeStruct((1, D), F32), jax.ShapeDtypeStruct((1, D), F32)],
        scratch_shapes=[pltpu.VMEM((tm, D), F32)],
        compiler_params=_cp(2))(df, gate, up, wg, wu, wd, dy, x1, mix, g3, g2)


SMEM_SPEC = pl.BlockSpec(memory_space=pltpu.SMEM)


def _emit_halves(acc_ref, row0, rows, c, own_ref, oth_ref):
    half = rows // 2
    own_ref[...] = acc_ref[pl.ds(row0 + pl.multiple_of(c * half, 8), half), :]
    oth_ref[...] = acc_ref[pl.ds(row0 + pl.multiple_of((1 - c) * half, 8), half), :].astype(BF16)


def _half_shapes(rows):
    return [jax.ShapeDtypeStruct((NSH, rows // 2, D), F32), jax.ShapeDtypeStruct((NSH, rows // 2, D), BF16)]


def _ffn_bwd_w(h2, gate, up, dgate, dup, df, c_arr):
    s = h2.shape[0]
    tm = min(TM_FFN_FWD, s)
    nt = s // tm

    def body(c_ref, h_ref, gate_ref, up_ref, dgate_ref, dup_ref, df_ref, *rest):
        outs, accs = rest[:6], rest[6:]
        i = pl.program_id(1)
        @pl.when(i == 0)
        def _():
            for acc in accs:
                acc[...] = jnp.zeros_like(acc)

        h = h_ref[...]
        accs[0][...] += _dot_tn(dgate_ref[...], h)
        accs[1][...] += _dot_tn(dup_ref[...], h)
        _, sl = _silu_parts(gate_ref[...].astype(F32))
        a = (sl * up_ref[...].astype(F32)).astype(BF16)
        accs[2][...] += _dot_tn(a, df_ref[...])

        @pl.when(i == nt - 1)
        def _():
            for t, acc in enumerate(accs):
                _emit_halves(acc, 0, FS, c_ref[0], outs[2 * t], outs[2 * t + 1])

    row = lambda w: pl.BlockSpec((tm, w), lambda j, i: (i, 0))
    act = pl.BlockSpec((None, tm, FS), lambda j, i: (j, i, 0))
    half = pl.BlockSpec((None, FS // 2, D), lambda j, i: (j, 0, 0))
    return pl.pallas_call(
        body, name="ffn_bwd_w", grid=(NSH, nt),
        in_specs=[SMEM_SPEC, row(D), act, act, act, act, row(D)],
        out_specs=[half] * 6,
        out_shape=_half_shapes(FS) * 3,
        scratch_shapes=[pltpu.VMEM((FS, D), F32)] * 3,
        compiler_params=_cp(2))(c_arr, h2, gate, up, dgate, dup, df)


def _outproj_bwd(dmix, w_out, pool_o, attn_o, c_arr, dep=None):
    s = dmix.shape[0]
    tm = min(TM, s)
    nt = s // tm

    def body(c_ref, dm_ref, w_ref, p_ref, a_ref, *rest):
        dpool_ref, dattn_ref, own_ref, oth_ref, gw_ref = rest[-5:]
        i = pl.program_id(0)

        @pl.when(i == 0)
        def _():
            gw_ref[...] = jnp.zeros_like(gw_ref)

        dm = dm_ref[...]
        dpool_ref[...] = _dot_nt(dm, w_ref[0:512, :])
        dattn_ref[...] = _dot_nt(dm, w_ref[512:1024, :]).astype(BF16)
        gw_ref[0:512, :] += _dot_tn(p_ref[...], dm)
        gw_ref[512:1024, :] += _dot_tn(a_ref[...], dm)

        @pl.when(i == nt - 1)
        def _():
            for k in range(NSH):
                _emit_halves(gw_ref, k * WOUT_S, WOUT_S, c_ref[0], own_ref.at[k], oth_ref.at[k])

    row = lambda w: pl.BlockSpec((tm, w), lambda i: (i, 0))
    full = pl.BlockSpec((D, D), lambda i: (0, 0))
    half = pl.BlockSpec((NSH, WOUT_S // 2, D), lambda i: (0, 0, 0))
    return pl.pallas_call(
        body, name="outproj_bwd", grid=(nt,),
        in_specs=[SMEM_SPEC, row(D), full, row(512), row(512)] + ([] if dep is None else [ANY]),
        out_specs=[row(512), row(512), half, half],
        out_shape=[jax.ShapeDtypeStruct((s, 512), F32), jax.ShapeDtypeStruct((s, 512), BF16)] + _half_shapes(WOUT_S),
        scratch_shapes=[pltpu.VMEM((D, D), F32)],
        compiler_params=_cp(1))(c_arr, dmix, w_out, pool_o, attn_o, *([] if dep is None else [dep]))


def _pool_bwd(u, dpool, w_pool, pool_scale, dep=None):
    s = u.shape[0]
    tm = min(TM_POOL, s)
    hb = tm // HALO
    nt = s // tm
    last_halo = s // HALO - 1

    def body(uc_ref, uh_ref, dc_ref, dn_ref, wp_ref, sc_ref, *rest):
        du_ref, gwp_ref, gsc_ref = rest[-3:]
        i = pl.program_id(0)

        @pl.when(i == 0)
        def _():
            gwp_ref[...] = jnp.zeros_like(gwp_ref)
            gsc_ref[...] = jnp.zeros_like(gsc_ref)

        cur = uc_ref[...]
        halo = jnp.where(i > 0, uh_ref[...], 0.0)
        pooled = _pooled(jnp.concatenate([halo, cur], axis=0), cur, i * tm, tm)
        dcur = dc_ref[...]
        dnext = jnp.where(i < nt - 1, dn_ref[...], 0.0)
        dext = jnp.concatenate([dcur, dnext], axis=0)
        t_ext = i * tm + lax.broadcasted_iota(jnp.int32, (tm + HALO, GROUP), 0)
        for g, w in enumerate(WINDOWS):
            sl = slice(g * GROUP, (g + 1) * GROUP)
            pb = pooled[g].astype(BF16)
            wp = wp_ref[g]
            mixed = _dot(pb, wp)
            gsc_ref[:, sl] += jnp.sum(dcur[:, sl] * mixed, axis=0, keepdims=True)
            dmixed = (dext[:, sl] * sc_ref[:, sl]).astype(BF16)
            gwp_ref[g] += _dot_tn(pb, dmixed[0:tm])
            dpooled = _dot_nt(dmixed, wp)
            z = dpooled / jnp.minimum(t_ext + 1, w).astype(F32)
            du_ref[:, sl] = (_window_sums(z, w, -1, tm, 2) - dpooled[0:tm]).astype(BF16)

    return pl.pallas_call(
        body, name="pool_bwd", grid=(nt,),
        in_specs=[pl.BlockSpec((tm, 512), lambda i: (i, 0)),
                  pl.BlockSpec((HALO, 512), lambda i: (jnp.maximum(i * hb - 1, 0), 0)),
                  pl.BlockSpec((tm, 512), lambda i: (i, 0)),
                  pl.BlockSpec((HALO, 512), lambda i: (jnp.minimum((i + 1) * hb, last_halo), 0)),
                  pl.BlockSpec((4, GROUP, GROUP), lambda i: (0, 0, 0)),
                  pl.BlockSpec((1, 512), lambda i: (0, 0))] + ([] if dep is None else [ANY]),
        out_specs=[pl.BlockSpec((tm, 512), lambda i: (i, 0)),
                   pl.BlockSpec((4, GROUP, GROUP), lambda i: (0, 0, 0)),
                   pl.BlockSpec((1, 512), lambda i: (0, 0))],
        out_shape=[jax.ShapeDtypeStruct((s, 512), BF16), jax.ShapeDtypeStruct((4, GROUP, GROUP), F32),
                   jax.ShapeDtypeStruct((1, 512), F32)],
        compiler_params=_cp(1))(u, u, dpool, dpool, w_pool, pool_scale, *([] if dep is None else [dep]))


def _attn_bwd(q, k, v, do, bias, sinks, dep=None):
    s = q.shape[0]

    def body(q_ref, do_ref, k_ref, v_ref, b_ref, sk_ref, *rest):
        dq_ref, dk_ref, dv_ref, dss_ref, gsk_ref = rest[-5:]
        n = pl.program_id(0)

        @pl.when(n == 0)
        def _():
            dk_ref[...] = jnp.zeros_like(dk_ref)
            dv_ref[...] = jnp.zeros_like(dv_ref)
            dss_ref[...] = jnp.zeros_like(dss_ref)
            gsk_ref[...] = jnp.zeros_like(gsk_ref)

        kk, kt, (lo, pstart, cstart) = _kv_band(k_ref, n, True)
        vv, _, _ = _kv_band(v_ref, n, False)
        bidx = jnp.minimum(n, 1)
        lo_q = lax.broadcasted_iota(jnp.int32, (BLK, 128), 1) < 64
        dkk = [jnp.zeros((2 * BLK, 128), F32), jnp.zeros((2 * BLK, 128), F32)]
        dvv = [jnp.zeros((2 * BLK, 128), F32), jnp.zeros((2 * BLK, 128), F32)]
        for p in range(4):
            h = p // 2
            qt = q_ref[:, p * 128:(p + 1) * 128].astype(F32) * SCALE
            dot = do_ref[:, p * 128:(p + 1) * 128]
            dq_t = jnp.zeros((128, BLK), F32)
            for e in range(2):
                head = 2 * p + e
                sel_q = lo_q if e == 0 else jnp.logical_not(lo_q)
                qm = jnp.where(sel_q, qt, 0.0).astype(BF16)
                prob, ps = _attn_probs(qm, kk[h], b_ref[bidx, head], sk_ref[0, head])
                dom = jnp.where(sel_q, dot, jnp.zeros_like(dot))
                dp = _dot_nt(vv[h], dom)
                delta = jnp.sum(prob * dp, axis=0, keepdims=True)
                ds = prob * (dp - delta)
                gsk_ref[pl.ds(head, 1), :] += jnp.broadcast_to(-jnp.sum(ps * delta).reshape(1, 1), (1, 128))
                dss_ref[head] += ds
                dsb = ds.astype(BF16)
                dq_t = dq_t + _dot(kt[h][e], dsb)
                dkk[h] = dkk[h] + _dot(dsb, qm)
                dvv[h] = dvv[h] + _dot(prob.astype(BF16), dom)
            dq_ref[:, p * 128:(p + 1) * 128] = (dq_t.T * SCALE).astype(BF16)
        for acc, ref in ((dkk, dk_ref), (dvv, dv_ref)):
            f0 = acc[0] + pltpu.roll(acc[0], 64, axis=1)
            f1 = acc[1] + pltpu.roll(acc[1], 64, axis=1)
            band = jnp.where(lo, f0, f1)
            ref[pl.ds(pstart, BLK), :] += band[0:BLK]
            ref[pl.ds(cstart, BLK), :] += band[BLK:2 * BLK]

    blk = pl.BlockSpec((BLK, 512), lambda i: (i, 0))
    kv = pl.BlockSpec((s, 128), lambda i: (0, 0))
    return pl.pallas_call(
        body, name="attn_bwd", grid=(s // BLK,),
        in_specs=[blk, blk, kv, kv, pl.BlockSpec((2, NQ, 2 * BLK, BLK), lambda i: (0, 0, 0, 0)),
                  pl.BlockSpec(memory_space=pltpu.SMEM)] + ([] if dep is None else [ANY]),
        out_specs=[blk, kv, kv, pl.BlockSpec((NQ, 2 * BLK, BLK), lambda i: (0, 0, 0)),
                   pl.BlockSpec((NQ, 128), lambda i: (0, 0))],
        out_shape=[jax.ShapeDtypeStruct((s, 512), BF16), jax.ShapeDtypeStruct((s, 128), F32),
                   jax.ShapeDtypeStruct((s, 128), F32), jax.ShapeDtypeStruct((NQ, 2 * BLK, BLK), F32),
                   jax.ShapeDtypeStruct((NQ, 128), F32)],
        compiler_params=_cp(1))(q, do, k, v, bias, sinks, *([] if dep is None else [dep]))


def _relbias_grad(dss, bucket):
    def body(ds_ref, b_ref, o_ref):
        bucket_v = b_ref[...]
        lane = lax.broadcasted_iota(jnp.int32, (NQ, 128), 1)
        head_row = lax.broadcasted_iota(jnp.int32, (NQ, BLK), 0)
        acc = jnp.zeros((NQ, 128), F32)
        for b in range(NBUCKET):
            sel = bucket_v == b
            stack = jnp.zeros((NQ, BLK), F32)
            for h in range(NQ):
                col_sums = jnp.sum(jnp.where(sel, ds_ref[h], 0.0), axis=0, keepdims=True)
                stack = jnp.where(head_row == h, col_sums, stack)
            acc = acc + jnp.where(lane == b, jnp.sum(stack, axis=1, keepdims=True), 0.0)
        o_ref[...] = acc

    return pl.pallas_call(
        body, name="relbias_grad",
        in_specs=[pl.BlockSpec(memory_space=pltpu.VMEM), pl.BlockSpec(memory_space=pltpu.VMEM)],
        out_specs=pl.BlockSpec(memory_space=pltpu.VMEM),
        out_shape=jax.ShapeDtypeStruct((NQ, 128), F32),
        compiler_params=_cp())(dss, bucket)


def _inproj_bwd(x, g1, du, dq, dk, dv, w_in, dx1, c_arr, dep=None):
    s = x.shape[0]
    tm = min(TM, s)
    nt = s // tm
    cols = ((0, 512), (512, 1024), (1024, 1152), (1152, 1280))

    def body(c_ref, x_ref, g_ref, du_ref, dq_ref, dk_ref, dv_ref, w_ref, dx1_ref, *rest):
        gx_ref, own_ref, oth_ref, gg_ref, gw_ref = rest[-5:]
        i = pl.program_id(0)

        @pl.when(i == 0)
        def _():
            gw_ref[...] = jnp.zeros_like(gw_ref)
            gg_ref[...] = jnp.zeros_like(gg_ref)

        gv = g_ref[...]
        r1, n1 = _rms(x_ref[...])
        h = (n1 * gv).astype(BF16)
        parts = (du_ref[...], dq_ref[...], dk_ref[...].astype(BF16), dv_ref[...].astype(BF16))
        dh = None
        for (a, b), dpart in zip(cols, parts):
            t = _dot(dpart, w_ref[a:b, :])
            dh = t if dh is None else dh + t
            gw_ref[a:b, :] += _dot_tn(dpart, h)
        dx, dg = _rms_bwd(r1, n1, gv, dh)
        gg_ref[...] += dg
        gx_ref[...] = dx1_ref[...] + dx

        @pl.when(i == nt - 1)
        def _():
            for k in range(NSH):
                _emit_halves(gw_ref, k * WIN_S, WIN_S, c_ref[0], own_ref.at[k], oth_ref.at[k])

    row = lambda w: pl.BlockSpec((tm, w), lambda i: (i, 0))
    vec = pl.BlockSpec((1, D), lambda i: (0, 0))
    full = pl.BlockSpec((IN_W, D), lambda i: (0, 0))
    half = pl.BlockSpec((NSH, WIN_S // 2, D), lambda i: (0, 0, 0))
    return pl.pallas_call(
        body, name="inproj_bwd", grid=(nt,),
        in_specs=[SMEM_SPEC, row(D), vec, row(512), row(512), row(128), row(128), full, row(D)]
        + ([] if dep is None else [ANY]),
        out_specs=[row(D), half, half, vec],
        out_shape=[jax.ShapeDtypeStruct((s, D), F32)] + _half_shapes(WIN_S) + [jax.ShapeDtypeStruct((1, D), F32)],
        scratch_shapes=[pltpu.VMEM((IN_W, D), F32)],
        compiler_params=_cp(1))(c_arr, x, g1, du, dq, dk, dv, w_in, dx1, *([] if dep is None else [dep]))


def _local_step(x, target, small, w_in, w_out, ffn_weights, c_arr, on_ffn_grads=None, on_wout_grads=None,
                on_attn_grads=None):
    g1, g2, g3, g4, w_pool, pool_scale, rel_bias, sinks = small
    bucket = jnp.asarray(_bucket_table())
    wp_bf = w_pool.astype(BF16)
    bias = _bias_table(bucket, rel_bias)
    h1 = _prenorm(x, g1)
    if callable(w_in):
        w_in = w_in(bias, h1)
    u, q, k, v = _inproj_fwd(h1, w_in)
    pool_o = _pool_fwd(u, wp_bf, pool_scale)
    attn_o = _attn_fwd(q, k, v, bias, sinks)
    g2_fwd = g2
    if callable(w_out):
        w_out, after = w_out(pool_o, attn_o)
        g2_fwd = g2 + after[:1, :1]
    mix, x1, h2 = _outproj_fwd(pool_o, attn_o, w_out, x, g2_fwd, g3)
    wg, wu, wd = ffn_weights(h2) if callable(ffn_weights) else ffn_weights
    gate, up, df, dy, loss, gg4 = _ffn_fwd(h2, wg, wu, wd, x1, target, g4)
    dgate, dup, dx1, dmix, gg3, gg2 = _ffn_bwd_act(df, gate, up, wg, wu, wd, dy, x1, mix, g3, g2)
    ffn_g = _ffn_bwd_w(h2, gate, up, dgate, dup, df, c_arr)
    dep = None if on_ffn_grads is None else on_ffn_grads(ffn_g)
    dpool, dattn, wout_own, wout_oth = _outproj_bwd(dmix, w_out, pool_o, attn_o, c_arr, dep)
    dep = None if on_wout_grads is None else on_wout_grads(wout_own, wout_oth, dpool)
    du, gwp, gsc = _pool_bwd(u, dpool, wp_bf, pool_scale, dep)
    dq, dk, dv, dss, gsk = _attn_bwd(q, k, v, dattn, bias, sinks, dep)
    grb = _relbias_grad(dss, bucket)
    dep = None if on_attn_grads is None else on_attn_grads([du, dq])
    gx, win_own, win_oth, gg1 = _inproj_bwd(x, g1, du, dq, dk, dv, w_in, dx1, c_arr, dep)
    small_grads = (gg1, gg2, gg3, gg4, gwp, gsc, grb[:, :NBUCKET].T, gsk[:, 0].reshape(1, NQ))
    return loss, gx, small_grads, ((win_own, win_oth), (wout_own, wout_oth), ffn_g)


def _place():
    x, y, c = lax.axis_index("x"), lax.axis_index("y"), lax.axis_index("c")
    chips = ((1 - x, y), (x, 1 - y), (1 - x, 1 - y))
    return x, y, c, chips


def _remote(src, dst, ssem, rsem, dev):
    return pltpu.make_async_remote_copy(src_ref=src, dst_ref=dst, send_sem=ssem, recv_sem=rsem,
                                        device_id=dev, device_id_type=MESH_T)


def _own_slot(shard):
    me = 2 * lax.axis_index("x") + lax.axis_index("y")
    return lax.dynamic_update_slice(lax.empty((NSH,) + shard.shape, shard.dtype), shard[None], (me, 0, 0))


def _to_sibling(arrs, name):
    nt = len(arrs)

    def body(*refs):
        srcs, dsts = refs[:nt], refs[nt:2 * nt]
        ssem, rsem = refs[2 * nt:]
        x, y, c, _ = _place()
        cps = [_remote(srcs[t], dsts[t], ssem.at[t], rsem.at[t], (x, y, 1 - c)) for t in range(nt)]
        for cp in cps:
            cp.start()
        for cp in cps:
            cp.wait()

    return pl.pallas_call(
        body, name=name, in_specs=[ANY] * nt, out_specs=[ANY] * nt,
        out_shape=[jax.ShapeDtypeStruct(a.shape, a.dtype) for a in arrs],
        scratch_shapes=[pltpu.SemaphoreType.DMA((nt,)), pltpu.SemaphoreType.DMA((nt,))],
        compiler_params=_cp())(*arrs)


HBM_SPEC = pl.BlockSpec(memory_space=pltpu.HBM)
SEM_SPEC = pl.BlockSpec(memory_space=pltpu.SEMAPHORE)
DATAFLOW = pltpu.SideEffectType.DATAFLOW_SIDE_EFFECTING


def _gather_copies(srcs, lands, ssem, rsem):
    x, y, c, chips = _place()
    me = 2 * x + y
    out = []
    for t in range(len(srcs)):
        half = srcs[t].shape[0] // 2
        mine = pl.ds(pl.multiple_of(c * half, 16), half)
        for j, (px, py) in enumerate(chips):
            k = 3 * t + j
            send = _remote(srcs[t].at[mine], lands[t].at[me, mine], ssem.at[k], rsem.at[k], (px, py, c))
            blk = lands[t].at[2 * px + py, mine]
            out.append((send, _remote(blk, blk, ssem.at[k], rsem.at[k], (px, py, c))))
    return out


def _scatter_copies(srcs, lands, ssem, rsem):
    x, y, c, chips = _place()
    out = []
    for t in range(len(srcs)):
        for j, (px, py) in enumerate(chips):
            k = 3 * t + j
            send = _remote(srcs[t].at[2 * px + py], lands[t].at[j], ssem.at[k], rsem.at[k], (px, py, c))
            out.append((send, _remote(lands[t].at[j], lands[t].at[j], ssem.at[k], rsem.at[k], (px, py, c))))
    return out


def _sibling_copies(srcs, lands, ssem, rsem):
    x, y, c, _ = _place()
    sib = (x, y, 1 - c)
    return [(_remote(srcs[t], lands[t], ssem.at[t], rsem.at[t], sib),
             _remote(lands[t], lands[t], ssem.at[t], rsem.at[t], sib)) for t in range(len(srcs))]


def _forward_copies(srcs, lands, ssem, rsem):
    x, y, c, chips = _place()
    sib = (x, y, 1 - c)
    out = []
    for t in range(len(lands)):
        half = lands[t].shape[1] // 2
        mine = pl.ds(pl.multiple_of(c * half, 16), half)
        other = pl.ds(pl.multiple_of((1 - c) * half, 16), half)
        for j, (px, py) in enumerate(chips):
            k = 3 * t + j
            blk_m, blk_o = lands[t].at[2 * px + py, mine], lands[t].at[2 * px + py, other]
            out.append((_remote(blk_m, blk_m, ssem.at[k], rsem.at[k], sib),
                        _remote(blk_o, blk_o, ssem.at[k], rsem.at[k], sib)))
    return out


def _peer_copies(srcs, lands, ssem, rsem):
    x, y, c, _ = _place()
    me = 4 * x + 2 * y + c
    out = []
    for kk in range(1, 8):
        px, py, pc = x ^ (kk >> 2), y ^ ((kk >> 1) & 1), c ^ (kk & 1)
        send = _remote(srcs[0], lands[0].at[me], ssem.at[kk - 1], rsem.at[kk - 1], (px, py, pc))
        slot = lands[0].at[4 * px + 2 * py + pc]
        out.append((send, _remote(slot, slot, ssem.at[kk - 1], rsem.at[kk - 1], (px, py, pc))))
    return out


def _split_start(plan, ncopy, srcs, lands, after, name):
    ns, nbuf = len(srcs), len(srcs) + len(lands)
    extra = [] if after is None else [after]
    n_in = nbuf + len(extra)

    def body(*refs):
        ssem, rsem, token = refs[n_in], refs[n_in + 1], refs[-1]
        for send, _ in plan(refs[:ns], refs[ns:nbuf], ssem, rsem):
            send.start()
        token[...] = jnp.zeros_like(token)

    bufs = [pltpu.with_memory_space_constraint(a, pltpu.HBM) for a in list(srcs) + list(lands)]
    outs = pl.pallas_call(
        body, name=name, in_specs=[HBM_SPEC] * nbuf + [ANY] * len(extra),
        out_specs=[SEM_SPEC, SEM_SPEC] + [HBM_SPEC] * nbuf + [pl.BlockSpec(memory_space=pltpu.VMEM)],
        out_shape=[pltpu.SemaphoreType.DMA((ncopy,)), pltpu.SemaphoreType.DMA((ncopy,))]
        + [pltpu.HBM(a.shape, a.dtype) for a in bufs] + [jax.ShapeDtypeStruct((8, 128), F32)],
        input_output_aliases={i: 2 + i for i in range(nbuf)},
        compiler_params=pltpu.CompilerParams(has_side_effects=DATAFLOW))(*bufs, *extra)
    return outs[0], outs[1], outs[2:2 + ns], outs[2 + ns:2 + nbuf], outs[-1]


def _split_wait(plan, ssem, rsem, srcs, lands, after, name, only=None):
    ns, nbuf = len(srcs), len(srcs) + len(lands)

    def body(*refs):
        s_ref, r_ref = refs[nbuf], refs[nbuf + 1]
        for k, (send, recv) in enumerate(plan(refs[:ns], refs[ns:nbuf], s_ref, r_ref)):
            if only is None or k in only:
                send.wait_send()
                recv.wait_recv()

    outs = pl.pallas_call(
        body, name=name, in_specs=[HBM_SPEC] * nbuf + [SEM_SPEC, SEM_SPEC] + [ANY] * len(after),
        out_specs=[HBM_SPEC] * nbuf,
        out_shape=[pltpu.HBM(a.shape, a.dtype) for a in list(srcs) + list(lands)],
        input_output_aliases={i: i for i in range(nbuf)},
        compiler_params=pltpu.CompilerParams(has_side_effects=DATAFLOW))(*srcs, *lands, ssem, rsem, *after)
    return outs


def _gather_finish(lands):
    nt = len(lands)

    def body(*refs):
        outs = refs[nt:2 * nt]
        dsend, drecv = refs[2 * nt:]
        x, y, c, chips = _place()
        sib = (x, y, 1 - c)
        sends = []
        for t in range(nt):
            half = outs[t].shape[1] // 2
            mine = pl.ds(pl.multiple_of(c * half, 16), half)
            for j, (px, py) in enumerate(chips):
                blk = outs[t].at[2 * px + py, mine]
                cp = _remote(blk, blk, dsend.at[t, j], drecv.at[t, j], sib)
                cp.start()
                sends.append(cp)
        for t in range(nt):
            half = outs[t].shape[1] // 2
            other = pl.ds(pl.multiple_of((1 - c) * half, 16), half)
            for j, (px, py) in enumerate(chips):
                blk = outs[t].at[2 * px + py, other]
                _remote(blk, blk, dsend.at[t, j], drecv.at[t, j], sib).wait_recv()
        for cp in sends:
            cp.wait_send()

    return pl.pallas_call(
        body, name="gather_finish", in_specs=[ANY] * nt, out_specs=[ANY] * nt,
        out_shape=[jax.ShapeDtypeStruct(a.shape, a.dtype) for a in lands],
        input_output_aliases={t: t for t in range(nt)},
        scratch_shapes=[pltpu.SemaphoreType.DMA((nt, 3)), pltpu.SemaphoreType.DMA((nt, 3))],
        compiler_params=_cp())(*lands)


def _chip_partial(owns, recv, chip_arr, tag):
    nt = len(owns)

    def body(chip_ref, *refs):
        k = pl.program_id(0)
        for t in range(nt):
            p = refs[t][...] + refs[nt + t][...].astype(F32)
            refs[2 * nt + t][...] = p.astype(BF16)

            @pl.when(k == chip_ref[0])
            def _():
                refs[3 * nt + t][...] = p

    blk_specs = [pl.BlockSpec((None,) + a.shape[1:], lambda k, chip_ref: (k, 0, 0)) for a in owns]
    own_specs = [pl.BlockSpec(a.shape[1:], lambda k, chip_ref: (0, 0)) for a in owns]
    return pl.pallas_call(
        body, name="rs_chip_partial_" + tag,
        grid_spec=pltpu.PrefetchScalarGridSpec(num_scalar_prefetch=1, grid=(NSH,), in_specs=blk_specs * 2,
                                               out_specs=blk_specs + own_specs),
        out_shape=[jax.ShapeDtypeStruct(a.shape, BF16) for a in owns]
        + [jax.ShapeDtypeStruct(a.shape[1:], F32) for a in owns],
        compiler_params=_cp(1))(chip_arr, *owns, *recv)


def _sum_chips(own, recv, tag, after=()):
    nt = len(own)

    def body(*refs):
        outs = refs[2 * nt + len(after):]
        for t in range(nt):
            r = refs[nt + t]
            outs[t][...] = ((refs[t][...] + r[0].astype(F32)) + r[1].astype(F32)) + r[2].astype(F32)

    vm = pl.BlockSpec(memory_space=pltpu.VMEM)
    return pl.pallas_call(
        body, name="rs_sum_chips_" + tag, in_specs=[vm] * (2 * nt) + [ANY] * len(after), out_specs=[vm] * nt,
        out_shape=[jax.ShapeDtypeStruct(a.shape, F32) for a in own],
        compiler_params=_cp())(*own, *recv, *after)


def _adamw_math(w, g, m, v):
    m = ADAM_B1 * m + (1.0 - ADAM_B1) * g
    v = ADAM_B2 * v + (1.0 - ADAM_B2) * (g * g)
    m_hat = m / (1.0 - ADAM_B1 ** ADAM_STEP)
    v_hat = v / (1.0 - ADAM_B2 ** ADAM_STEP)
    delta = -ADAM_LR * (m_hat / (jnp.sqrt(v_hat) + ADAM_EPS) + ADAM_WD * w)
    return delta, m, v


ADAM_SPLIT = 4


def _adamw_shards(own, sib, ws, ms, vs, c_arr, tag):
    nt = len(own)

    def body(c_ref, *refs):
        hh = pl.program_id(0)
        mine = hh == c_ref[0]
        for t in range(nt):
            g = jnp.where(mine, refs[t][...], refs[nt + t][...])
            delta, m, v = _adamw_math(refs[2 * nt + t][...], g, refs[3 * nt + t][...], refs[4 * nt + t][...])
            refs[5 * nt + t][...] = g
            refs[6 * nt + t][...] = delta
            refs[7 * nt + t][...] = m
            refs[8 * nt + t][...] = v

    def tile(a):
        return (a.shape[0] // ADAM_SPLIT, a.shape[1])

    half_specs = [pl.BlockSpec(tile(a), lambda hh, q, c_ref: (q, 0)) for a in own]
    full_specs = [pl.BlockSpec(tile(a), lambda hh, q, c_ref: (hh * ADAM_SPLIT + q, 0)) for a in own]
    return pl.pallas_call(
        body, name="adamw_shards_" + tag,
        grid_spec=pltpu.PrefetchScalarGridSpec(num_scalar_prefetch=1, grid=(2, ADAM_SPLIT),
                                               in_specs=half_specs * 2 + full_specs * 3, out_specs=full_specs * 4),
        out_shape=[jax.ShapeDtypeStruct(w.shape, F32) for w in ws] * 4,
        compiler_params=_cp(2))(c_arr, *own, *sib, *ws, *ms, *vs)


SMALL_TAIL_ROW = 552


def _small_sum_adamw(gathered, wp, mp, vp):
    rows = wp.shape[0]

    def body(g_ref, w_ref, m_ref, v_ref, *rest):
        outs, res = rest[:-1], rest[-1]
        g = g_ref[0]
        for d in range(1, 8):
            g = g + g_ref[d]
        delta, m, v = _adamw_math(w_ref[...], g, m_ref[...], v_ref[...])
        for kind, val in enumerate((g, delta, m, v)):
            res[kind] = val
            gains, w_pool_o, scale_o, tail_o = outs[7 * kind:7 * kind + 4], *outs[7 * kind + 4:7 * kind + 7]
            for t in range(4):
                for i in range(8):
                    gains[t][:, 128 * i:128 * (i + 1)] = res[kind, pl.ds(8 * t + i, 1), :]
            for grp in range(4):
                w_pool_o[grp] = res[kind, pl.ds(32 + GROUP * grp, GROUP), :]
            for i in range(4):
                scale_o[:, 128 * i:128 * (i + 1)] = res[kind, pl.ds(544 + i, 1), :]
            tail_o[...] = res[kind, pl.ds(SMALL_TAIL_ROW, rows - SMALL_TAIL_ROW), :]

    vm = pl.BlockSpec(memory_space=pltpu.VMEM)
    one_kind = [jax.ShapeDtypeStruct((1, D), F32)] * 4 + [
        jax.ShapeDtypeStruct((4, GROUP, GROUP), F32), jax.ShapeDtypeStruct((1, POOL_W), F32),
        jax.ShapeDtypeStruct((rows - SMALL_TAIL_ROW, 128), F32)]
    return pl.pallas_call(
        body, name="small_sum_adamw", in_specs=[vm] * 4, out_specs=[vm] * 28,
        out_shape=one_kind * 4,
        scratch_shapes=[pltpu.VMEM((4, rows, 128), F32)],
        compiler_params=_cp())(gathered, wp, mp, vp)


def _pack_small(parts):
    rows = []
    for a in parts:
        flat = a.reshape(-1)
        n = flat.shape[0]
        padded = -(-n // 1024) * 1024
        rows.append(jnp.pad(flat, (0, padded - n)).reshape(-1, 128))
    return jnp.concatenate(rows, axis=0)


def kernel(x, g_pre_mix, w_in, w_pool, pool_scale, rel_bias, sinks, w_out, g_post_mix, g_pre_ffn, w_gate, w_up, w_down, g_post_ffn, loss_target, m_g_pre_mix, m_w_in, m_w_pool, m_pool_scale, m_rel_bias, m_sinks, m_w_out, m_g_post_mix, m_g_pre_ffn, m_w_gate, m_w_up, m_w_down, m_g_post_ffn, v_g_pre_mix, v_w_in, v_w_pool, v_pool_scale, v_rel_bias, v_sinks, v_w_out, v_g_post_mix, v_g_pre_ffn, v_w_gate, v_w_up, v_w_down, v_g_post_ffn):
    c = lax.axis_index("c")
    chip = 2 * lax.axis_index("x") + lax.axis_index("y")

    def shards(a_in, a_out, a_gate, a_up, a_down):
        return (a_in[0].T, a_out[0], a_gate[0].T, a_up[0].T, a_down[0])

    c_arr = c.reshape(1).astype(jnp.int32)
    chip_arr = chip.reshape(1).astype(jnp.int32)

    big_w = shards(w_in, w_out, w_gate, w_up, w_down)
    big_bf = [w.astype(BF16) for w in big_w]
    g_ssem, g_rsem, g_srcs, g_lands, _ = _split_start(
        _gather_copies, 15, big_bf, [_own_slot(a) for a in big_bf], None, "gather_start")
    fw = {}

    def w_in_ready(*after):
        fw["first"] = _split_wait(_gather_copies, g_ssem, g_rsem, g_srcs, g_lands, after, "gather_win_wait",
                                  only=(0, 1, 2))
        (fw["win"],) = _gather_finish(fw["first"][5:6])
        return fw["win"].reshape(IN_W, D)

    def w_out_ready(*after):
        first = fw["first"]
        lands = _split_wait(_gather_copies, g_ssem, g_rsem, first[:5], [fw["win"]] + list(first[6:]), after,
                            "gather_rest_wait", only=tuple(range(3, 15)))[5:]
        (wout_all,) = _gather_finish(lands[1:2])
        fw["f"] = _split_start(_forward_copies, 9, [], lands[2:], wout_all, "gather_forward_start")
        return wout_all.reshape(D, D), fw["f"][4]

    def ffn_weights(after):
        ssem, rsem, _, lands, _ = fw["f"]
        return _split_wait(_forward_copies, ssem, rsem, [], lands, [after], "gather_forward_wait")

    rs = {}

    def on_ffn_grads(ffn_g):
        oths = ffn_g[1::2]
        rs["ffn_own"] = ffn_g[0::2]
        rs["x"] = _split_start(_sibling_copies, 3, oths, [lax.empty(a.shape, BF16) for a in oths], ffn_g[0],
                               "rs_exchange_ffn_start")
        return rs["x"][4]

    def on_wout_grads(own, oth, after):
        ssem, rsem, srcs, lands, _ = rs["x"]
        recv_ffn = _split_wait(_sibling_copies, ssem, rsem, srcs, lands, [after], "rs_exchange_ffn_wait")[3:]
        recv_wout = _to_sibling([oth], "rs_exchange_wout")
        outs = _chip_partial([own] + list(rs["ffn_own"]), list(recv_wout) + list(recv_ffn), chip_arr, "early")
        rs["early_own"] = outs[4:]
        rs["s"] = _split_start(_scatter_copies, 12, outs[:4],
                               [lax.empty((3,) + a.shape[1:], BF16) for a in outs[:4]], outs[4], "scatter_early_start")
        return rs["s"][4]

    def on_attn_grads(after):
        ssem, rsem, srcs, lands, _ = rs["s"]
        recv_early = _split_wait(_scatter_copies, ssem, rsem, srcs, lands, after, "scatter_early_wait")[4:]
        finals = _sum_chips(list(rs["early_own"]), list(recv_early), "early")
        rs["sh"] = _split_start(_sibling_copies, 4, finals, [lax.empty(a.shape, F32) for a in finals], finals[0],
                                "rs_share_early_start")
        return rs["sh"][4]

    small = (g_pre_mix, g_post_mix, g_pre_ffn, g_post_ffn, w_pool[0], pool_scale, rel_bias, sinks)
    loss, gx, small_grads, ((win_own, win_oth), _, _) = _local_step(
        x[0], loss_target[0], small, w_in_ready, w_out_ready, ffn_weights, c_arr,
        on_ffn_grads, on_wout_grads, on_attn_grads)

    unused = jnp.zeros((1, 1), F32)
    gp = _pack_small(small_grads + (loss,))
    wp = _pack_small((g_pre_mix, g_post_mix, g_pre_ffn, g_post_ffn, w_pool, pool_scale, rel_bias, sinks, unused))
    mp = _pack_small((m_g_pre_mix, m_g_post_mix, m_g_pre_ffn, m_g_post_ffn, m_w_pool, m_pool_scale, m_rel_bias,
                      m_sinks, unused))
    vp = _pack_small((v_g_pre_mix, v_g_post_mix, v_g_pre_ffn, v_g_post_ffn, v_w_pool, v_pool_scale, v_rel_bias,
                      v_sinks, unused))

    moments = (shards(m_w_in, m_w_out, m_w_gate, m_w_up, m_w_down),
               shards(v_w_in, v_w_out, v_w_gate, v_w_up, v_w_down))
    recv_a = _to_sibling([win_oth], "rs_exchange_win")
    outs = _chip_partial([win_own], recv_a, chip_arr, "win")
    w_ssem, w_rsem, w_srcs, w_lands, w_token = _split_start(
        _scatter_copies, 3, outs[:1], [lax.empty((3,) + outs[0].shape[1:], BF16)], gx, "scatter_win_start")
    slots = lax.dynamic_update_slice(lax.empty((8,) + gp.shape, F32), gp[None], (2 * chip + c, 0, 0))
    p_ssem, p_rsem, p_srcs, p_lands, p_token = _split_start(_peer_copies, 7, [gp], [slots], w_token,
                                                            "small_allgather_start")
    ssem, rsem, srcs, lands, _ = rs["sh"]
    shared = _split_wait(_sibling_copies, ssem, rsem, srcs, lands, [p_token], "rs_share_early_wait")
    adam_e = _adamw_shards(shared[:4], shared[4:], big_w[1:], moments[0][1:], moments[1][1:], c_arr, "early")
    recv_win = _split_wait(_scatter_copies, w_ssem, w_rsem, w_srcs, w_lands, [adam_e[0]], "scatter_win_wait")[1:]
    win_final = _sum_chips([outs[1]], recv_win, "win")
    win_sib = _to_sibling(win_final, "rs_share_win")
    adam_w = _adamw_shards(win_final, win_sib, big_w[:1], moments[0][:1], moments[1][:1], c_arr, "win")
    big_g, big_d, big_m, big_v = [[adam_w[i]] + list(adam_e[4 * i:4 * i + 4]) for i in range(4)]

    gathered = _split_wait(_peer_copies, p_ssem, p_rsem, p_srcs, p_lands, [adam_w[0]], "small_allgather_wait")[1]
    flat = _small_sum_adamw(gathered, wp, mp, vp)
    small_out = [flat[7 * kind:7 * kind + 7] for kind in range(4)]
    total_loss = small_out[0][6][16, 0]

    def ordered(small7, big4):
        sg1, sg2, sg3, sg4, swp, ssc, tail = small7
        swp, srb, ssk = swp[None], tail[0:2].reshape(NBUCKET, NQ), tail[8:9, 0:NQ]
        bwin, bwout, bwg, bwu, bwd = big4[0].T[None], big4[1][None], big4[2].T[None], big4[3].T[None], big4[4][None]
        return [sg1, bwin, swp, ssc, srb, ssk, bwout, sg2, sg3, bwg, bwu, bwd, sg4]

    res = [total_loss, gx[None]]
    for sm, bg in zip(small_out, (big_g, big_d, big_m, big_v)):
        res += ordered(sm, bg)
    return tuple(res)
```

```python
import numpy as np
import jax
import jax.numpy as jnp
from jax import lax
from jax.experimental import pallas as pl
from jax.experimental.pallas import tpu as pltpu

F32 = jnp.float32
BF16 = jnp.bfloat16

D = 1024
POOL_W = 512
GROUP = 128
WINDOWS = (2, 4, 8, 16)
HALO = 128
NQ = 8
BLK = 128
NBUCKET = 32
IN_W = 1280
DFF = 2816
NSH = 4
FS = DFF // NSH
WIN_S = IN_W // NSH
WOUT_S = D // NSH
EPS = 1e-6
NEG = -1e30
SCALE = 0.125

ADAM_LR = 0.001
ADAM_B1 = 0.9
ADAM_B2 = 0.999
ADAM_EPS = 1e-08
ADAM_WD = 0.01
ADAM_STEP = 10

TM = 512
TM_POOL = 256
TM_FFN = 512
TM_FFN_FWD = 1024
FFN_ROWS = 256
VMEM_LIMIT = 60 * 1024 * 1024

MESH_T = pl.DeviceIdType.MESH
ANY = pl.BlockSpec(memory_space=pl.ANY)


def _cp(n_grid=0, **kw):
    sem = ("arbitrary",) * n_grid if n_grid else None
    return pltpu.CompilerParams(dimension_semantics=sem, vmem_limit_bytes=VMEM_LIMIT, **kw)


def _dot(a, b):
    return jnp.dot(a, b, preferred_element_type=F32)


def _dot_nt(a, b):
    return lax.dot_general(a, b, (((1,), (1,)), ((), ())), preferred_element_type=F32)


def _dot_tn(a, b):
    return lax.dot_general(a, b, (((0,), (0,)), ((), ())), preferred_element_type=F32)


def _rms(x):
    r = lax.rsqrt(jnp.mean(x * x, axis=-1, keepdims=True) + EPS)
    return r, x * r


def _rms_bwd(r, n, g, dout):
    dn = dout * g
    dx = r * (dn - n * jnp.mean(dn * n, axis=-1, keepdims=True))
    dg = jnp.sum(dout * n, axis=0, keepdims=True)
    return dx, dg


def _split(x, n):
    parts = []
    r = x
    for _ in range(n):
        p = r.astype(BF16)
        parts.append(p)
        r = r - p.astype(F32)
    return parts


def _band_dot(a, x, n):
    acc = None
    for p in _split(x, n):
        t = _dot(a, p)
        acc = t if acc is None else acc + t
    return acc


def _bucket_table():
    qi = np.arange(BLK)[None, :]
    kj = np.arange(2 * BLK)[:, None]
    dist = qi + BLK - kj
    n = np.maximum(dist, 0)
    nf = np.maximum(n, 1).astype(np.float32)
    large = 16 + (np.log(nf / np.float32(16)) / np.float32(np.log(128 / 16)) * np.float32(16)).astype(np.int32)
    large = np.minimum(large, NBUCKET - 1)
    return np.where(n < 16, n, large).astype(np.int32)


def _prenorm(x, g1):
    s = x.shape[0]
    tm = min(TM, s)

    def body(x_ref, g_ref, h_ref):
        _, n = _rms(x_ref[...])
        h_ref[...] = (n * g_ref[...]).astype(BF16)

    row = pl.BlockSpec((tm, D), lambda i: (i, 0))
    return pl.pallas_call(
        body, name="prenorm", grid=(s // tm,),
        in_specs=[row, pl.BlockSpec((1, D), lambda i: (0, 0))], out_specs=row,
        out_shape=jax.ShapeDtypeStruct((s, D), BF16),
        compiler_params=_cp(1))(x, g1)


def _inproj_fwd(h1, w_in):
    s = h1.shape[0]
    tm = min(TM, s)

    def body(h_ref, w_ref, u_ref, q_ref, k_ref, v_ref):
        proj = _dot_nt(h_ref[...], w_ref[...])
        u_ref[...] = proj[:, :512]
        q_ref[...] = proj[:, 512:1024].astype(BF16)
        k_ref[...] = proj[:, 1024:1152].astype(BF16)
        v_ref[...] = proj[:, 1152:1280].astype(BF16)

    row = lambda w: pl.BlockSpec((tm, w), lambda i: (i, 0))
    return pl.pallas_call(
        body, name="inproj_fwd", grid=(s // tm,),
        in_specs=[row(D), pl.BlockSpec((IN_W, D), lambda i: (0, 0))],
        out_specs=[row(512), row(512), row(128), row(128)],
        out_shape=[jax.ShapeDtypeStruct((s, 512), F32), jax.ShapeDtypeStruct((s, 512), BF16),
                   jax.ShapeDtypeStruct((s, 128), BF16), jax.ShapeDtypeStruct((s, 128), BF16)],
        compiler_params=_cp(1))(h1, w_in)


def _window_sums(ext, w, lag_sign, tm, terms):
    rows = lax.broadcasted_iota(jnp.int32, (HALO, 2 * HALO), 0)
    cols = lax.broadcasted_iota(jnp.int32, (HALO, 2 * HALO), 1)
    d = rows + HALO - cols if lag_sign > 0 else cols - rows
    band = jnp.where((d >= 0) & (d < w), 1.0, 0.0).astype(BF16)
    return jnp.concatenate([_band_dot(band, ext[r:r + 2 * HALO], terms) for r in range(0, tm, HALO)], axis=0)


def _pooled(ext, cur, t0, tm):
    t = t0 + lax.broadcasted_iota(jnp.int32, (tm, GROUP), 0)
    out = []
    for g, w in enumerate(WINDOWS):
        sl = slice(g * GROUP, (g + 1) * GROUP)
        cnt = jnp.minimum(t + 1, w).astype(F32)
        out.append(_window_sums(ext[:, sl], w, 1, tm, 2) / cnt - cur[:, sl])
    return out


def _pool_fwd(u, w_pool, pool_scale):
    s = u.shape[0]
    tm = min(TM_POOL, s)
    hb = tm // HALO

    def body(uc_ref, uh_ref, wp_ref, sc_ref, o_ref):
        i = pl.program_id(0)
        cur = uc_ref[...]
        halo = jnp.where(i > 0, uh_ref[...], 0.0)
        ext = jnp.concatenate([halo, cur], axis=0)
        pooled = _pooled(ext, cur, i * tm, tm)
        for g in range(4):
            sl = slice(g * GROUP, (g + 1) * GROUP)
            mixed = _dot(pooled[g].astype(BF16), wp_ref[g])
            o_ref[:, sl] = (mixed * sc_ref[:, sl]).astype(BF16)

    return pl.pallas_call(
        body, name="pool_fwd", grid=(s // tm,),
        in_specs=[pl.BlockSpec((tm, 512), lambda i: (i, 0)),
                  pl.BlockSpec((HALO, 512), lambda i: (jnp.maximum(i * hb - 1, 0), 0)),
                  pl.BlockSpec((4, GROUP, GROUP), lambda i: (0, 0, 0)),
                  pl.BlockSpec((1, 512), lambda i: (0, 0))],
        out_specs=pl.BlockSpec((tm, 512), lambda i: (i, 0)),
        out_shape=jax.ShapeDtypeStruct((s, 512), BF16),
        compiler_params=_cp(1))(u, u, w_pool, pool_scale)


def _bias_table(bucket, rel_bias):
    def body(b_ref, rb_ref, o_ref):
        bucket_v = b_ref[...]
        key = lax.broadcasted_iota(jnp.int32, (2 * BLK, BLK), 0)
        dist = lax.broadcasted_iota(jnp.int32, (2 * BLK, BLK), 1) + BLK - key
        inwin = (dist >= 0) & (dist < BLK)
        for h in range(NQ):
            acc = jnp.zeros((2 * BLK, BLK), F32)
            for b in range(NBUCKET):
                acc = jnp.where(bucket_v == b, rb_ref[b, h], acc)
            rest = jnp.where(inwin, acc, NEG)
            o_ref[1, h] = rest
            o_ref[0, h] = jnp.where(key >= BLK, rest, NEG)

    return pl.pallas_call(
        body, name="bias_table",
        in_specs=[pl.BlockSpec(memory_space=pltpu.VMEM), pl.BlockSpec(memory_space=pltpu.SMEM)],
        out_specs=pl.BlockSpec(memory_space=pltpu.VMEM),
        out_shape=jax.ShapeDtypeStruct((2, NQ, 2 * BLK, BLK), F32),
        compiler_params=_cp())(bucket, rel_bias)


def _kv_band(ref, n, transposed):
    pstart = pl.multiple_of(jnp.maximum(n - 1, 0) * BLK, BLK)
    cstart = pl.multiple_of(n * BLK, BLK)
    lo = lax.broadcasted_iota(jnp.int32, (2 * BLK, 128), 1) < 64
    band = jnp.concatenate([ref[pl.ds(pstart, BLK), :], ref[pl.ds(cstart, BLK), :]], axis=0).astype(F32)
    rot = pltpu.roll(band, 64, axis=1)
    dup = (jnp.where(lo, band, rot), jnp.where(lo, rot, band))
    plain = [d.astype(BF16) for d in dup]
    if not transposed:
        return plain, None, (lo, pstart, cstart)
    row_lo = lax.broadcasted_iota(jnp.int32, (128, 2 * BLK), 0) < 64
    tr = [[jnp.where(row_lo, d.T, 0.0).astype(BF16), jnp.where(row_lo, 0.0, d.T).astype(BF16)] for d in dup]
    return plain, tr, (lo, pstart, cstart)


def _attn_probs(qm, kk, bias, sink):
    s = _dot_nt(kk, qm) + bias
    m = jnp.maximum(jnp.max(s, axis=0, keepdims=True), sink)
    p = jnp.exp(s - m)
    es = jnp.exp(sink - m)
    inv = 1.0 / (jnp.sum(p, axis=0, keepdims=True) + es)
    return p * inv, es * inv


def _attn_fwd(q, k, v, bias, sinks):
    s = q.shape[0]

    def body(q_ref, k_ref, v_ref, b_ref, sk_ref, o_ref):
        n = pl.program_id(0)
        kk, _, _ = _kv_band(k_ref, n, False)
        _, vt, _ = _kv_band(v_ref, n, True)
        bidx = jnp.minimum(n, 1)
        lo_q = lax.broadcasted_iota(jnp.int32, (BLK, 128), 1) < 64
        for p in range(4):
            h = p // 2
            qt = q_ref[:, p * 128:(p + 1) * 128].astype(F32) * SCALE
            acc_t = jnp.zeros((128, BLK), F32)
            for e in range(2):
                head = 2 * p + e
                qm = jnp.where(lo_q if e == 0 else jnp.logical_not(lo_q), qt, 0.0).astype(BF16)
                prob, _ = _attn_probs(qm, kk[h], b_ref[bidx, head], sk_ref[0, head])
                acc_t = acc_t + _dot(vt[h][e], prob.astype(BF16))
            o_ref[:, p * 128:(p + 1) * 128] = acc_t.T.astype(BF16)

    return pl.pallas_call(
        body, name="attn_fwd", grid=(s // BLK,),
        in_specs=[pl.BlockSpec((BLK, 512), lambda i: (i, 0)),
                  pl.BlockSpec((s, 128), lambda i: (0, 0)),
                  pl.BlockSpec((s, 128), lambda i: (0, 0)),
                  pl.BlockSpec((2, NQ, 2 * BLK, BLK), lambda i: (0, 0, 0, 0)),
                  pl.BlockSpec(memory_space=pltpu.SMEM)],
        out_specs=pl.BlockSpec((BLK, 512), lambda i: (i, 0)),
        out_shape=jax.ShapeDtypeStruct((s, 512), BF16),
        compiler_params=_cp(1))(q, k, v, bias, sinks)


def _outproj_fwd(pool_o, attn_o, w_out, x, g2, g3):
    s = x.shape[0]
    tm = min(TM, s)

    def body(p_ref, a_ref, w_ref, x_ref, g2_ref, g3_ref, mix_ref, x1_ref, h2_ref):
        mix = _dot(p_ref[...], w_ref[0:512, :]) + _dot(a_ref[...], w_ref[512:1024, :])
        mix_ref[...] = mix
        _, n2 = _rms(mix)
        x1 = x_ref[...] + n2 * g2_ref[...]
        x1_ref[...] = x1
        _, n3 = _rms(x1)
        h2_ref[...] = (n3 * g3_ref[...]).astype(BF16)

    row = lambda w: pl.BlockSpec((tm, w), lambda i: (i, 0))
    vec = pl.BlockSpec((1, D), lambda i: (0, 0))
    return pl.pallas_call(
        body, name="outproj_fwd", grid=(s // tm,),
        in_specs=[row(512), row(512), pl.BlockSpec((D, D), lambda i: (0, 0)), row(D), vec, vec],
        out_specs=[row(D), row(D), row(D)],
        out_shape=[jax.ShapeDtypeStruct((s, D), F32), jax.ShapeDtypeStruct((s, D), F32),
                   jax.ShapeDtypeStruct((s, D), BF16)],
        compiler_params=_cp(1))(pool_o, attn_o, w_out, x, g2, g3)


def _silu_parts(g):
    sg = jax.nn.sigmoid(g)
    return sg, g * sg


def _ffn_fwd(h2, wg, wu, wd, x1, target, g4):
    s = h2.shape[0]
    tm = min(TM_FFN_FWD, s)

    def body(h_ref, wg_ref, wu_ref, wd_ref, x1_ref, t_ref, g4_ref,
             gate_ref, up_ref, df_ref, dy_ref, loss_ref, gg4_ref, f_acc):
        i = pl.program_id(0)
        j = pl.program_id(1)
        first = j == 0
        for r in range(0, tm, FFN_ROWS):
            rows = pl.ds(r, min(FFN_ROWS, tm))
            h = h_ref[rows, :]
            gate = _dot_nt(h, wg_ref[...])
            up = _dot_nt(h, wu_ref[...])
            gate_ref[rows, :] = gate.astype(BF16)
            up_ref[rows, :] = up.astype(BF16)
            _, sl = _silu_parts(gate)
            contrib = _dot((sl * up).astype(BF16), wd_ref[...])
            f_acc[rows, :] = jnp.where(first, contrib, f_acc[rows, :] + contrib)

        @pl.when((i == 0) & (j == 0))
        def _():
            loss_ref[...] = jnp.zeros_like(loss_ref)
            gg4_ref[...] = jnp.zeros_like(gg4_ref)

        @pl.when(j == NSH - 1)
        def _():
            r4, n4 = _rms(f_acc[...])
            g4v = g4_ref[...]
            err = x1_ref[...] + n4 * g4v - t_ref[...]
            loss_ref[...] += (0.5 / D) * jnp.sum(err * err).reshape(1, 1)
            dy = err * (1.0 / D)
            dy_ref[...] = dy
            df, dg = _rms_bwd(r4, n4, g4v, dy)
            gg4_ref[...] += dg
            df_ref[...] = df.astype(BF16)

    row = lambda w: pl.BlockSpec((tm, w), lambda i, j: (i, 0))
    sh = lambda r, c: pl.BlockSpec((None, r, c), lambda i, j: (j, 0, 0))
    act = pl.BlockSpec((None, tm, FS), lambda i, j: (j, i, 0))
    vec = pl.BlockSpec((1, D), lambda i, j: (0, 0))
    return pl.pallas_call(
        body, name="ffn_fwd", grid=(s // tm, NSH),
        in_specs=[row(D), sh(FS, D), sh(FS, D), sh(FS, D), row(D), row(D), vec],
        out_specs=[act, act, row(D), row(D), pl.BlockSpec((1, 1), lambda i, j: (0, 0)), vec],
        out_shape=[jax.ShapeDtypeStruct((NSH, s, FS), BF16), jax.ShapeDtypeStruct((NSH, s, FS), BF16),
                   jax.ShapeDtypeStruct((s, D), BF16), jax.ShapeDtypeStruct((s, D), F32),
                   jax.ShapeDtypeStruct((1, 1), F32), jax.ShapeDtypeStruct((1, D), F32)],
        scratch_shapes=[pltpu.VMEM((tm, D), F32)],
        compiler_params=_cp(2))(h2, wg, wu, wd, x1, target, g4)


def _ffn_bwd_act(df, gate, up, wg, wu, wd, dy, x1, mix, g3, g2):
    s = df.shape[0]
    tm = min(TM_FFN, s)

    def body(df_ref, gate_ref, up_ref, wg_ref, wu_ref, wd_ref, dy_ref, x1_ref, mix_ref, g3_ref, g2_ref,
             dgate_ref, dup_ref, dx1_ref, dmix_ref, gg3_ref, gg2_ref, acc):
        i = pl.program_id(0)
        j = pl.program_id(1)
        first = j == 0
        for r in range(0, tm, FFN_ROWS):
            rows = pl.ds(r, min(FFN_ROWS, tm))
            da = _dot_nt(df_ref[rows, :], wd_ref[...])
            g = gate_ref[rows, :].astype(F32)
            u = up_ref[rows, :].astype(F32)
            sg, sl = _silu_parts(g)
            dgate = (da * u * (sg * (1.0 + g * (1.0 - sg)))).astype(BF16)
            dup = (da * sl).astype(BF16)
            dgate_ref[rows, :] = dgate
            dup_ref[rows, :] = dup
            contrib = _dot(dgate, wg_ref[...]) + _dot(dup, wu_ref[...])
            acc[rows, :] = jnp.where(first, contrib, acc[rows, :] + contrib)

        @pl.when((i == 0) & (j == 0))
        def _():
            gg3_ref[...] = jnp.zeros_like(gg3_ref)
            gg2_ref[...] = jnp.zeros_like(gg2_ref)

        @pl.when(j == NSH - 1)
        def _():
            r3, n3 = _rms(x1_ref[...])
            dx1n, dg3 = _rms_bwd(r3, n3, g3_ref[...], acc[...])
            dx1 = dy_ref[...] + dx1n
            dx1_ref[...] = dx1
            gg3_ref[...] += dg3
            r2, n2 = _rms(mix_ref[...])
            dmix, dg2 = _rms_bwd(r2, n2, g2_ref[...], dx1)
            gg2_ref[...] += dg2
            dmix_ref[...] = dmix.astype(BF16)

    row = lambda w: pl.BlockSpec((tm, w), lambda i, j: (i, 0))
    sh = lambda r, c: pl.BlockSpec((None, r, c), lambda i, j: (j, 0, 0))
    act = pl.BlockSpec((None, tm, FS), lambda i, j: (j, i, 0))
    vec = pl.BlockSpec((1, D), lambda i, j: (0, 0))
    return pl.pallas_call(
        body, name="ffn_bwd_act", grid=(s // tm, NSH),
        in_specs=[row(D), act, act, sh(FS, D), sh(FS, D), sh(FS, D), row(D), row(D), row(D), vec, vec],
        out_specs=[act, act, row(D), row(D), vec, vec],
        out_shape=[jax.ShapeDtypeStruct((NSH, s, FS), BF16), jax.ShapeDtypeStruct((NSH, s, FS), BF16),
                   jax.ShapeDtypeStruct((s, D), F32), jax.ShapeDtypeStruct((s, D), BF16),
                   jax.ShapeDtypeStruct((1, D), F32), jax.ShapeDtypeStruct((1, D), F32)],
        scratch_shapes=[pltpu.VMEM((tm, D), F32)],
        compiler_params=_cp(2))(df, gate, up, wg, wu, wd, dy, x1, mix, g3, g2)


SMEM_SPEC = pl.BlockSpec(memory_space=pltpu.SMEM)


def _emit_halves(acc_ref, row0, rows, c, own_ref, oth_ref):
    half = rows // 2
    own_ref[...] = acc_ref[pl.ds(row0 + pl.multiple_of(c * half, 8), half), :]
    oth_ref[...] = acc_ref[pl.ds(row0 + pl.multiple_of((1 - c) * half, 8), half), :].astype(BF16)


def _half_shapes(rows):
    return [jax.ShapeDtypeStruct((NSH, rows // 2, D), F32), jax.ShapeDtypeStruct((NSH, rows // 2, D), BF16)]


def _ffn_bwd_w(h2, gate, up, dgate, dup, df, c_arr):
    s = h2.shape[0]
    tm = min(TM_FFN_FWD, s)
    nt = s // tm

    def body(c_ref, h_ref, gate_ref, up_ref, dgate_ref, dup_ref, df_ref, *rest):
        outs, accs = rest[:6], rest[6:]
        i = pl.program_id(1)
        @pl.when(i == 0)
        def _():
            for acc in accs:
                acc[...] = jnp.zeros_like(acc)

        h = h_ref[...]
        accs[0][...] += _dot_tn(dgate_ref[...], h)
        accs[1][...] += _dot_tn(dup_ref[...], h)
        _, sl = _silu_parts(gate_ref[...].astype(F32))
        a = (sl * up_ref[...].astype(F32)).astype(BF16)
        accs[2][...] += _dot_tn(a, df_ref[...])

        @pl.when(i == nt - 1)
        def _():
            for t, acc in enumerate(accs):
                _emit_halves(acc, 0, FS, c_ref[0], outs[2 * t], outs[2 * t + 1])

    row = lambda w: pl.BlockSpec((tm, w), lambda j, i: (i, 0))
    act = pl.BlockSpec((None, tm, FS), lambda j, i: (j, i, 0))
    half = pl.BlockSpec((None, FS // 2, D), lambda j, i: (j, 0, 0))
    return pl.pallas_call(
        body, name="ffn_bwd_w", grid=(NSH, nt),
        in_specs=[SMEM_SPEC, row(D), act, act, act, act, row(D)],
        out_specs=[half] * 6,
        out_shape=_half_shapes(FS) * 3,
        scratch_shapes=[pltpu.VMEM((FS, D), F32)] * 3,
        compiler_params=_cp(2))(c_arr, h2, gate, up, dgate, dup, df)


def _outproj_bwd(dmix, w_out, pool_o, attn_o, c_arr, dep=None):
    s = dmix.shape[0]
    tm = min(TM, s)
    nt = s // tm

    def body(c_ref, dm_ref, w_ref, p_ref, a_ref, *rest):
        dpool_ref, dattn_ref, own_ref, oth_ref, gw_ref = rest[-5:]
        i = pl.program_id(0)

        @pl.when(i == 0)
        def _():
            gw_ref[...] = jnp.zeros_like(gw_ref)

        dm = dm_ref[...]
        dpool_ref[...] = _dot_nt(dm, w_ref[0:512, :])
        dattn_ref[...] = _dot_nt(dm, w_ref[512:1024, :]).astype(BF16)
        gw_ref[0:512, :] += _dot_tn(p_ref[...], dm)
        gw_ref[512:1024, :] += _dot_tn(a_ref[...], dm)

        @pl.when(i == nt - 1)
        def _():
            for k in range(NSH):
                _emit_halves(gw_ref, k * WOUT_S, WOUT_S, c_ref[0], own_ref.at[k], oth_ref.at[k])

    row = lambda w: pl.BlockSpec((tm, w), lambda i: (i, 0))
    full = pl.BlockSpec((D, D), lambda i: (0, 0))
    half = pl.BlockSpec((NSH, WOUT_S // 2, D), lambda i: (0, 0, 0))
    return pl.pallas_call(
        body, name="outproj_bwd", grid=(nt,),
        in_specs=[SMEM_SPEC, row(D), full, row(512), row(512)] + ([] if dep is None else [ANY]),
        out_specs=[row(512), row(512), half, half],
        out_shape=[jax.ShapeDtypeStruct((s, 512), F32), jax.ShapeDtypeStruct((s, 512), BF16)] + _half_shapes(WOUT_S),
        scratch_shapes=[pltpu.VMEM((D, D), F32)],
        compiler_params=_cp(1))(c_arr, dmix, w_out, pool_o, attn_o, *([] if dep is None else [dep]))


def _pool_bwd(u, dpool, w_pool, pool_scale, dep=None):
    s = u.shape[0]
    tm = min(TM_POOL, s)
    hb = tm // HALO
    nt = s // tm
    last_halo = s // HALO - 1

    def body(uc_ref, uh_ref, dc_ref, dn_ref, wp_ref, sc_ref, *rest):
        du_ref, gwp_ref, gsc_ref = rest[-3:]
        i = pl.program_id(0)

        @pl.when(i == 0)
        def _():
            gwp_ref[...] = jnp.zeros_like(gwp_ref)
            gsc_ref[...] = jnp.zeros_like(gsc_ref)

        cur = uc_ref[...]
        halo = jnp.where(i > 0, uh_ref[...], 0.0)
        pooled = _pooled(jnp.concatenate([halo, cur], axis=0), cur, i * tm, tm)
        dcur = dc_ref[...]
        dnext = jnp.where(i < nt - 1, dn_ref[...], 0.0)
        dext = jnp.concatenate([dcur, dnext], axis=0)
        t_ext = i * tm + lax.broadcasted_iota(jnp.int32, (tm + HALO, GROUP), 0)
        for g, w in enumerate(WINDOWS):
            sl = slice(g * GROUP, (g + 1) * GROUP)
            pb = pooled[g].astype(BF16)
            wp = wp_ref[g]
            mixed = _dot(pb, wp)
            gsc_ref[:, sl] += jnp.sum(dcur[:, sl] * mixed, axis=0, keepdims=True)
            dmixed = (dext[:, sl] * sc_ref[:, sl]).astype(BF16)
            gwp_ref[g] += _dot_tn(pb, dmixed[0:tm])
            dpooled = _dot_nt(dmixed, wp)
            z = dpooled / jnp.minimum(t_ext + 1, w).astype(F32)
            du_ref[:, sl] = (_window_sums(z, w, -1, tm, 2) - dpooled[0:tm]).astype(BF16)

    return pl.pallas_call(
        body, name="pool_bwd", grid=(nt,),
        in_specs=[pl.BlockSpec((tm, 512), lambda i: (i, 0)),
                  pl.BlockSpec((HALO, 512), lambda i: (jnp.maximum(i * hb - 1, 0), 0)),
                  pl.BlockSpec((tm, 512), lambda i: (i, 0)),
                  pl.BlockSpec((HALO, 512), lambda i: (jnp.minimum((i + 1) * hb, last_halo), 0)),
                  pl.BlockSpec((4, GROUP, GROUP), lambda i: (0, 0, 0)),
                  pl.BlockSpec((1, 512), lambda i: (0, 0))] + ([] if dep is None else [ANY]),
        out_specs=[pl.BlockSpec((tm, 512), lambda i: (i, 0)),
                   pl.BlockSpec((4, GROUP, GROUP), lambda i: (0, 0, 0)),
                   pl.BlockSpec((1, 512), lambda i: (0, 0))],
        out_shape=[jax.ShapeDtypeStruct((s, 512), BF16), jax.ShapeDtypeStruct((4, GROUP, GROUP), F32),
                   jax.ShapeDtypeStruct((1, 512), F32)],
        compiler_params=_cp(1))(u, u, dpool, dpool, w_pool, pool_scale, *([] if dep is None else [dep]))


def _attn_bwd(q, k, v, do, bias, sinks, bucket, dep=None):
    s = q.shape[0]
    nblk = s // BLK

    def body(q_ref, do_ref, k_ref, v_ref, b_ref, sk_ref, bk_ref, *rest):
        dq_ref, dk_ref, dv_ref, grb_ref, gsk_ref, dss_ref = rest[-6:]
        n = pl.program_id(0)

        @pl.when(n == 0)
        def _():
            dk_ref[...] = jnp.zeros_like(dk_ref)
            dv_ref[...] = jnp.zeros_like(dv_ref)
            dss_ref[...] = jnp.zeros_like(dss_ref)
            gsk_ref[...] = jnp.zeros_like(gsk_ref)

        kk, kt, (lo, pstart, cstart) = _kv_band(k_ref, n, True)
        vv, _, _ = _kv_band(v_ref, n, False)
        bidx = jnp.minimum(n, 1)
        lo_q = lax.broadcasted_iota(jnp.int32, (BLK, 128), 1) < 64
        dkk = [jnp.zeros((2 * BLK, 128), F32), jnp.zeros((2 * BLK, 128), F32)]
        dvv = [jnp.zeros((2 * BLK, 128), F32), jnp.zeros((2 * BLK, 128), F32)]
        for p in range(4):
            h = p // 2
            qt = q_ref[:, p * 128:(p + 1) * 128].astype(F32) * SCALE
            dot = do_ref[:, p * 128:(p + 1) * 128]
            dq_t = jnp.zeros((128, BLK), F32)
            for e in range(2):
                head = 2 * p + e
                sel_q = lo_q if e == 0 else jnp.logical_not(lo_q)
                qm = jnp.where(sel_q, qt, 0.0).astype(BF16)
                prob, ps = _attn_probs(qm, kk[h], b_ref[bidx, head], sk_ref[0, head])
                dom = jnp.where(sel_q, dot, jnp.zeros_like(dot))
                dp = _dot_nt(vv[h], dom)
                delta = jnp.sum(prob * dp, axis=0, keepdims=True)
                ds = prob * (dp - delta)
                gsk_ref[pl.ds(head, 1), :] += jnp.broadcast_to(-jnp.sum(ps * delta).reshape(1, 1), (1, 128))
                dss_ref[head] += ds
                dsb = ds.astype(BF16)
                dq_t = dq_t + _dot(kt[h][e], dsb)
                dkk[h] = dkk[h] + _dot(dsb, qm)
                dvv[h] = dvv[h] + _dot(prob.astype(BF16), dom)
            dq_ref[:, p * 128:(p + 1) * 128] = (dq_t.T * SCALE).astype(BF16)
        for acc, ref in ((dkk, dk_ref), (dvv, dv_ref)):
            f0 = acc[0] + pltpu.roll(acc[0], 64, axis=1)
            f1 = acc[1] + pltpu.roll(acc[1], 64, axis=1)
            band = jnp.where(lo, f0, f1)
            ref[pl.ds(pstart, BLK), :] += band[0:BLK]
            ref[pl.ds(cstart, BLK), :] += band[BLK:2 * BLK]

        @pl.when(n == nblk - 1)
        def _():
            _relbias_grad(dss_ref, bk_ref[...], grb_ref)

    blk = pl.BlockSpec((BLK, 512), lambda i: (i, 0))
    kv = pl.BlockSpec((s, 128), lambda i: (0, 0))
    row8 = pl.BlockSpec((NQ, 128), lambda i: (0, 0))
    return pl.pallas_call(
        body, name="attn_bwd", grid=(nblk,),
        in_specs=[blk, blk, kv, kv, pl.BlockSpec((2, NQ, 2 * BLK, BLK), lambda i: (0, 0, 0, 0)),
                  pl.BlockSpec(memory_space=pltpu.SMEM), pl.BlockSpec((2 * BLK, BLK), lambda i: (0, 0))]
        + ([] if dep is None else [ANY]),
        out_specs=[blk, kv, kv, row8, row8],
        out_shape=[jax.ShapeDtypeStruct((s, 512), BF16), jax.ShapeDtypeStruct((s, 128), F32),
                   jax.ShapeDtypeStruct((s, 128), F32), jax.ShapeDtypeStruct((NQ, 128), F32),
                   jax.ShapeDtypeStruct((NQ, 128), F32)],
        scratch_shapes=[pltpu.VMEM((NQ, 2 * BLK, BLK), F32)],
        compiler_params=_cp(1))(q, do, k, v, bias, sinks, bucket, *([] if dep is None else [dep]))


def _relbias_grad(ds_ref, bucket_v, o_ref):
    lane = lax.broadcasted_iota(jnp.int32, (NQ, 128), 1)
    head_row = lax.broadcasted_iota(jnp.int32, (NQ, BLK), 0)
    acc = jnp.zeros((NQ, 128), F32)
    for b in range(NBUCKET):
        sel = bucket_v == b
        stack = jnp.zeros((NQ, BLK), F32)
        for h in range(NQ):
            col_sums = jnp.sum(jnp.where(sel, ds_ref[h], 0.0), axis=0, keepdims=True)
            stack = jnp.where(head_row == h, col_sums, stack)
        acc = acc + jnp.where(lane == b, jnp.sum(stack, axis=1, keepdims=True), 0.0)
    o_ref[...] = acc


def _inproj_bwd(x, g1, du, dq, dk, dv, w_in, dx1, c_arr, dep=None):
    s = x.shape[0]
    tm = min(TM, s)
    nt = s // tm
    cols = ((0, 512), (512, 1024), (1024, 1152), (1152, 1280))

    def body(c_ref, x_ref, g_ref, du_ref, dq_ref, dk_ref, dv_ref, w_ref, dx1_ref, *rest):
        gx_ref, own_ref, oth_ref, gg_ref, gw_ref = rest[-5:]
        i = pl.program_id(0)

        @pl.when(i == 0)
        def _():
            gw_ref[...] = jnp.zeros_like(gw_ref)
            gg_ref[...] = jnp.zeros_like(gg_ref)

        gv = g_ref[...]
        r1, n1 = _rms(x_ref[...])
        h = (n1 * gv).astype(BF16)
        parts = (du_ref[...], dq_ref[...], dk_ref[...].astype(BF16), dv_ref[...].astype(BF16))
        dh = None
        for (a, b), dpart in zip(cols, parts):
            t = _dot(dpart, w_ref[a:b, :])
            dh = t if dh is None else dh + t
            gw_ref[a:b, :] += _dot_tn(dpart, h)
        dx, dg = _rms_bwd(r1, n1, gv, dh)
        gg_ref[...] += dg
        gx_ref[...] = dx1_ref[...] + dx

        @pl.when(i == nt - 1)
        def _():
            for k in range(NSH):
                _emit_halves(gw_ref, k * WIN_S, WIN_S, c_ref[0], own_ref.at[k], oth_ref.at[k])

    row = lambda w: pl.BlockSpec((tm, w), lambda i: (i, 0))
    vec = pl.BlockSpec((1, D), lambda i: (0, 0))
    full = pl.BlockSpec((IN_W, D), lambda i: (0, 0))
    half = pl.BlockSpec((NSH, WIN_S // 2, D), lambda i: (0, 0, 0))
    return pl.pallas_call(
        body, name="inproj_bwd", grid=(nt,),
        in_specs=[SMEM_SPEC, row(D), vec, row(512), row(512), row(128), row(128), full, row(D)]
        + ([] if dep is None else [ANY]),
        out_specs=[row(D), half, half, vec],
        out_shape=[jax.ShapeDtypeStruct((s, D), F32)] + _half_shapes(WIN_S) + [jax.ShapeDtypeStruct((1, D), F32)],
        scratch_shapes=[pltpu.VMEM((IN_W, D), F32)],
        compiler_params=_cp(1))(c_arr, x, g1, du, dq, dk, dv, w_in, dx1, *([] if dep is None else [dep]))


def _local_step(x, target, small, w_in, w_out, ffn_weights, c_arr, on_ffn_grads=None, on_wout_grads=None,
                on_attn_grads=None):
    g1, g2, g3, g4, w_pool, pool_scale, rel_bias, sinks = small
    bucket = jnp.asarray(_bucket_table())
    wp_bf = w_pool.astype(BF16)
    bias = _bias_table(bucket, rel_bias)
    h1 = _prenorm(x, g1)
    if callable(w_in):
        w_in = w_in(bias, h1)
    u, q, k, v = _inproj_fwd(h1, w_in)
    pool_o = _pool_fwd(u, wp_bf, pool_scale)
    attn_o = _attn_fwd(q, k, v, bias, sinks)
    g2_fwd = g2
    if callable(w_out):
        w_out, after = w_out(pool_o, attn_o)
        g2_fwd = g2 + after[:1, :1]
    mix, x1, h2 = _outproj_fwd(pool_o, attn_o, w_out, x, g2_fwd, g3)
    wg, wu, wd = ffn_weights(h2) if callable(ffn_weights) else ffn_weights
    gate, up, df, dy, loss, gg4 = _ffn_fwd(h2, wg, wu, wd, x1, target, g4)
    dgate, dup, dx1, dmix, gg3, gg2 = _ffn_bwd_act(df, gate, up, wg, wu, wd, dy, x1, mix, g3, g2)
    ffn_g = _ffn_bwd_w(h2, gate, up, dgate, dup, df, c_arr)
    dep = None if on_ffn_grads is None else on_ffn_grads(ffn_g)
    dpool, dattn, wout_own, wout_oth = _outproj_bwd(dmix, w_out, pool_o, attn_o, c_arr, dep)
    dep = None if on_wout_grads is None else on_wout_grads(wout_own, wout_oth, dpool)
    du, gwp, gsc = _pool_bwd(u, dpool, wp_bf, pool_scale, dep)
    dq, dk, dv, grb, gsk = _attn_bwd(q, k, v, dattn, bias, sinks, bucket, dep)
    dep = None if on_attn_grads is None else on_attn_grads([du, dq])
    gx, win_own, win_oth, gg1 = _inproj_bwd(x, g1, du, dq, dk, dv, w_in, dx1, c_arr, dep)
    small_grads = (gg1, gg2, gg3, gg4, gwp, gsc, grb[:, :NBUCKET].T, gsk[:, 0].reshape(1, NQ))
    return loss, gx, small_grads, ((win_own, win_oth), (wout_own, wout_oth), ffn_g)


def _place():
    x, y, c = lax.axis_index("x"), lax.axis_index("y"), lax.axis_index("c")
    chips = ((1 - x, y), (x, 1 - y), (1 - x, 1 - y))
    return x, y, c, chips


def _remote(src, dst, ssem, rsem, dev):
    return pltpu.make_async_remote_copy(src_ref=src, dst_ref=dst, send_sem=ssem, recv_sem=rsem,
                                        device_id=dev, device_id_type=MESH_T)


def _own_slot(shard):
    me = 2 * lax.axis_index("x") + lax.axis_index("y")
    return lax.dynamic_update_slice(lax.empty((NSH,) + shard.shape, shard.dtype), shard[None], (me, 0, 0))


def _to_sibling(arrs, name):
    nt = len(arrs)

    def body(*refs):
        srcs, dsts = refs[:nt], refs[nt:2 * nt]
        ssem, rsem = refs[2 * nt:]
        x, y, c, _ = _place()
        cps = [_remote(srcs[t], dsts[t], ssem.at[t], rsem.at[t], (x, y, 1 - c)) for t in range(nt)]
        for cp in cps:
            cp.start()
        for cp in cps:
            cp.wait()

    return pl.pallas_call(
        body, name=name, in_specs=[ANY] * nt, out_specs=[ANY] * nt,
        out_shape=[jax.ShapeDtypeStruct(a.shape, a.dtype) for a in arrs],
        scratch_shapes=[pltpu.SemaphoreType.DMA((nt,)), pltpu.SemaphoreType.DMA((nt,))],
        compiler_params=_cp())(*arrs)


HBM_SPEC = pl.BlockSpec(memory_space=pltpu.HBM)
SEM_SPEC = pl.BlockSpec(memory_space=pltpu.SEMAPHORE)
DATAFLOW = pltpu.SideEffectType.DATAFLOW_SIDE_EFFECTING


def _gather_copies(srcs, lands, ssem, rsem):
    x, y, c, chips = _place()
    me = 2 * x + y
    out = []
    for t in range(len(srcs)):
        half = srcs[t].shape[0] // 2
        mine = pl.ds(pl.multiple_of(c * half, 16), half)
        for j, (px, py) in enumerate(chips):
            k = 3 * t + j
            send = _remote(srcs[t].at[mine], lands[t].at[me, mine], ssem.at[k], rsem.at[k], (px, py, c))
            blk = lands[t].at[2 * px + py, mine]
            out.append((send, _remote(blk, blk, ssem.at[k], rsem.at[k], (px, py, c))))
    return out


def _scatter_copies(srcs, lands, ssem, rsem):
    x, y, c, chips = _place()
    out = []
    for t in range(len(srcs)):
        for j, (px, py) in enumerate(chips):
            k = 3 * t + j
            send = _remote(srcs[t].at[2 * px + py], lands[t].at[j], ssem.at[k], rsem.at[k], (px, py, c))
            out.append((send, _remote(lands[t].at[j], lands[t].at[j], ssem.at[k], rsem.at[k], (px, py, c))))
    return out


def _sibling_copies(srcs, lands, ssem, rsem):
    x, y, c, _ = _place()
    sib = (x, y, 1 - c)
    return [(_remote(srcs[t], lands[t], ssem.at[t], rsem.at[t], sib),
             _remote(lands[t], lands[t], ssem.at[t], rsem.at[t], sib)) for t in range(len(srcs))]


def _forward_copies(srcs, lands, ssem, rsem):
    x, y, c, chips = _place()
    sib = (x, y, 1 - c)
    out = []
    for t in range(len(lands)):
        half = lands[t].shape[1] // 2
        mine = pl.ds(pl.multiple_of(c * half, 16), half)
        other = pl.ds(pl.multiple_of((1 - c) * half, 16), half)
        for j, (px, py) in enumerate(chips):
            k = 3 * t + j
            blk_m, blk_o = lands[t].at[2 * px + py, mine], lands[t].at[2 * px + py, other]
            out.append((_remote(blk_m, blk_m, ssem.at[k], rsem.at[k], sib),
                        _remote(blk_o, blk_o, ssem.at[k], rsem.at[k], sib)))
    return out


def _peer_copies(srcs, lands, ssem, rsem):
    x, y, c, _ = _place()
    me = 4 * x + 2 * y + c
    out = []
    for kk in range(1, 8):
        px, py, pc = x ^ (kk >> 2), y ^ ((kk >> 1) & 1), c ^ (kk & 1)
        send = _remote(srcs[0], lands[0].at[me], ssem.at[kk - 1], rsem.at[kk - 1], (px, py, pc))
        slot = lands[0].at[4 * px + 2 * py + pc]
        out.append((send, _remote(slot, slot, ssem.at[kk - 1], rsem.at[kk - 1], (px, py, pc))))
    return out


def _split_start(plan, ncopy, srcs, lands, after, name):
    ns, nbuf = len(srcs), len(srcs) + len(lands)
    extra = [] if after is None else [after]
    n_in = nbuf + len(extra)

    def body(*refs):
        ssem, rsem, token = refs[n_in], refs[n_in + 1], refs[-1]
        for send, _ in plan(refs[:ns], refs[ns:nbuf], ssem, rsem):
            send.start()
        token[...] = jnp.zeros_like(token)

    bufs = [pltpu.with_memory_space_constraint(a, pltpu.HBM) for a in list(srcs) + list(lands)]
    outs = pl.pallas_call(
        body, name=name, in_specs=[HBM_SPEC] * nbuf + [ANY] * len(extra),
        out_specs=[SEM_SPEC, SEM_SPEC] + [HBM_SPEC] * nbuf + [pl.BlockSpec(memory_space=pltpu.VMEM)],
        out_shape=[pltpu.SemaphoreType.DMA((ncopy,)), pltpu.SemaphoreType.DMA((ncopy,))]
        + [pltpu.HBM(a.shape, a.dtype) for a in bufs] + [jax.ShapeDtypeStruct((8, 128), F32)],
        input_output_aliases={i: 2 + i for i in range(nbuf)},
        compiler_params=pltpu.CompilerParams(has_side_effects=DATAFLOW))(*bufs, *extra)
    return outs[0], outs[1], outs[2:2 + ns], outs[2 + ns:2 + nbuf], outs[-1]


def _split_wait(plan, ssem, rsem, srcs, lands, after, name, only=None):
    ns, nbuf = len(srcs), len(srcs) + len(lands)

    def body(*refs):
        s_ref, r_ref = refs[nbuf], refs[nbuf + 1]
        for k, (send, recv) in enumerate(plan(refs[:ns], refs[ns:nbuf], s_ref, r_ref)):
            if only is None or k in only:
                send.wait_send()
                recv.wait_recv()

    outs = pl.pallas_call(
        body, name=name, in_specs=[HBM_SPEC] * nbuf + [SEM_SPEC, SEM_SPEC] + [ANY] * len(after),
        out_specs=[HBM_SPEC] * nbuf,
        out_shape=[pltpu.HBM(a.shape, a.dtype) for a in list(srcs) + list(lands)],
        input_output_aliases={i: i for i in range(nbuf)},
        compiler_params=pltpu.CompilerParams(has_side_effects=DATAFLOW))(*srcs, *lands, ssem, rsem, *after)
    return outs


def _gather_finish(lands):
    nt = len(lands)

    def body(*refs):
        outs = refs[nt:2 * nt]
        dsend, drecv = refs[2 * nt:]
        x, y, c, chips = _place()
        sib = (x, y, 1 - c)
        sends = []
        for t in range(nt):
            half = outs[t].shape[1] // 2
            mine = pl.ds(pl.multiple_of(c * half, 16), half)
            for j, (px, py) in enumerate(chips):
                blk = outs[t].at[2 * px + py, mine]
                cp = _remote(blk, blk, dsend.at[t, j], drecv.at[t, j], sib)
                cp.start()
                sends.append(cp)
        for t in range(nt):
            half = outs[t].shape[1] // 2
            other = pl.ds(pl.multiple_of((1 - c) * half, 16), half)
            for j, (px, py) in enumerate(chips):
                blk = outs[t].at[2 * px + py, other]
                _remote(blk, blk, dsend.at[t, j], drecv.at[t, j], sib).wait_recv()
        for cp in sends:
            cp.wait_send()

    return pl.pallas_call(
        body, name="gather_finish", in_specs=[ANY] * nt, out_specs=[ANY] * nt,
        out_shape=[jax.ShapeDtypeStruct(a.shape, a.dtype) for a in lands],
        input_output_aliases={t: t for t in range(nt)},
        scratch_shapes=[pltpu.SemaphoreType.DMA((nt, 3)), pltpu.SemaphoreType.DMA((nt, 3))],
        compiler_params=_cp())(*lands)


def _chip_partial(owns, recv, chip_arr, tag):
    nt = len(owns)

    def body(chip_ref, *refs):
        k = pl.program_id(0)
        for t in range(nt):
            p = refs[t][...] + refs[nt + t][...].astype(F32)
            refs[2 * nt + t][...] = p.astype(BF16)

            @pl.when(k == chip_ref[0])
            def _():
                refs[3 * nt + t][...] = p

    blk_specs = [pl.BlockSpec((None,) + a.shape[1:], lambda k, chip_ref: (k, 0, 0)) for a in owns]
    own_specs = [pl.BlockSpec(a.shape[1:], lambda k, chip_ref: (0, 0)) for a in owns]
    return pl.pallas_call(
        body, name="rs_chip_partial_" + tag,
        grid_spec=pltpu.PrefetchScalarGridSpec(num_scalar_prefetch=1, grid=(NSH,), in_specs=blk_specs * 2,
                                               out_specs=blk_specs + own_specs),
        out_shape=[jax.ShapeDtypeStruct(a.shape, BF16) for a in owns]
        + [jax.ShapeDtypeStruct(a.shape[1:], F32) for a in owns],
        compiler_params=_cp(1))(chip_arr, *owns, *recv)


def _sum_chips(own, recv, tag, after=()):
    nt = len(own)

    def body(*refs):
        outs = refs[2 * nt + len(after):]
        for t in range(nt):
            r = refs[nt + t]
            outs[t][...] = ((refs[t][...] + r[0].astype(F32)) + r[1].astype(F32)) + r[2].astype(F32)

    vm = pl.BlockSpec(memory_space=pltpu.VMEM)
    return pl.pallas_call(
        body, name="rs_sum_chips_" + tag, in_specs=[vm] * (2 * nt) + [ANY] * len(after), out_specs=[vm] * nt,
        out_shape=[jax.ShapeDtypeStruct(a.shape, F32) for a in own],
        compiler_params=_cp())(*own, *recv, *after)


def _adamw_math(w, g, m, v):
    m = ADAM_B1 * m + (1.0 - ADAM_B1) * g
    v = ADAM_B2 * v + (1.0 - ADAM_B2) * (g * g)
    m_hat = m / (1.0 - ADAM_B1 ** ADAM_STEP)
    v_hat = v / (1.0 - ADAM_B2 ** ADAM_STEP)
    delta = -ADAM_LR * (m_hat / (jnp.sqrt(v_hat) + ADAM_EPS) + ADAM_WD * w)
    return delta, m, v


ADAM_SPLIT = 4


def _adamw_shards(own, sib, ws, ms, vs, c_arr, tag):
    nt = len(own)

    def body(c_ref, *refs):
        hh = pl.program_id(0)
        mine = hh == c_ref[0]
        for t in range(nt):
            g = jnp.where(mine, refs[t][...], refs[nt + t][...])
            delta, m, v = _adamw_math(refs[2 * nt + t][...], g, refs[3 * nt + t][...], refs[4 * nt + t][...])
            refs[5 * nt + t][...] = g
            refs[6 * nt + t][...] = delta
            refs[7 * nt + t][...] = m
            refs[8 * nt + t][...] = v

    def tile(a):
        return (a.shape[0] // ADAM_SPLIT, a.shape[1])

    half_specs = [pl.BlockSpec(tile(a), lambda hh, q, c_ref: (q, 0)) for a in own]
    full_specs = [pl.BlockSpec(tile(a), lambda hh, q, c_ref: (hh * ADAM_SPLIT + q, 0)) for a in own]
    return pl.pallas_call(
        body, name="adamw_shards_" + tag,
        grid_spec=pltpu.PrefetchScalarGridSpec(num_scalar_prefetch=1, grid=(2, ADAM_SPLIT),
                                               in_specs=half_specs * 2 + full_specs * 3, out_specs=full_specs * 4),
        out_shape=[jax.ShapeDtypeStruct(w.shape, F32) for w in ws] * 4,
        compiler_params=_cp(2))(c_arr, *own, *sib, *ws, *ms, *vs)


SMALL_TAIL_ROW = 552


def _small_sum_adamw(gathered, wp, mp, vp):
    rows = wp.shape[0]

    def body(g_ref, w_ref, m_ref, v_ref, *rest):
        outs, res = rest[:-1], rest[-1]
        g = g_ref[0]
        for d in range(1, 8):
            g = g + g_ref[d]
        delta, m, v = _adamw_math(w_ref[...], g, m_ref[...], v_ref[...])
        for kind, val in enumerate((g, delta, m, v)):
            res[kind] = val
            gains, w_pool_o, scale_o, tail_o = outs[7 * kind:7 * kind + 4], *outs[7 * kind + 4:7 * kind + 7]
            for t in range(4):
                for i in range(8):
                    gains[t][:, 128 * i:128 * (i + 1)] = res[kind, pl.ds(8 * t + i, 1), :]
            for grp in range(4):
                w_pool_o[grp] = res[kind, pl.ds(32 + GROUP * grp, GROUP), :]
            for i in range(4):
                scale_o[:, 128 * i:128 * (i + 1)] = res[kind, pl.ds(544 + i, 1), :]
            tail_o[...] = res[kind, pl.ds(SMALL_TAIL_ROW, rows - SMALL_TAIL_ROW), :]

    vm = pl.BlockSpec(memory_space=pltpu.VMEM)
    one_kind = [jax.ShapeDtypeStruct((1, D), F32)] * 4 + [
        jax.ShapeDtypeStruct((4, GROUP, GROUP), F32), jax.ShapeDtypeStruct((1, POOL_W), F32),
        jax.ShapeDtypeStruct((rows - SMALL_TAIL_ROW, 128), F32)]
    return pl.pallas_call(
        body, name="small_sum_adamw", in_specs=[vm] * 4, out_specs=[vm] * 28,
        out_shape=one_kind * 4,
        scratch_shapes=[pltpu.VMEM((4, rows, 128), F32)],
        compiler_params=_cp())(gathered, wp, mp, vp)


def _pack_small(parts):
    rows = []
    for a in parts:
        flat = a.reshape(-1)
        n = flat.shape[0]
        padded = -(-n // 1024) * 1024
        rows.append(jnp.pad(flat, (0, padded - n)).reshape(-1, 128))
    return jnp.concatenate(rows, axis=0)


def kernel(x, g_pre_mix, w_in, w_pool, pool_scale, rel_bias, sinks, w_out, g_post_mix, g_pre_ffn, w_gate, w_up, w_down, g_post_ffn, loss_target, m_g_pre_mix, m_w_in, m_w_pool, m_pool_scale, m_rel_bias, m_sinks, m_w_out, m_g_post_mix, m_g_pre_ffn, m_w_gate, m_w_up, m_w_down, m_g_post_ffn, v_g_pre_mix, v_w_in, v_w_pool, v_pool_scale, v_rel_bias, v_sinks, v_w_out, v_g_post_mix, v_g_pre_ffn, v_w_gate, v_w_up, v_w_down, v_g_post_ffn):
    c = lax.axis_index("c")
    chip = 2 * lax.axis_index("x") + lax.axis_index("y")

    def shards(a_in, a_out, a_gate, a_up, a_down):
        return (a_in[0].T, a_out[0], a_gate[0].T, a_up[0].T, a_down[0])

    c_arr = c.reshape(1).astype(jnp.int32)
    chip_arr = chip.reshape(1).astype(jnp.int32)

    big_w = shards(w_in, w_out, w_gate, w_up, w_down)
    big_bf = [w.astype(BF16) for w in big_w]
    g_ssem, g_rsem, g_srcs, g_lands, _ = _split_start(
        _gather_copies, 15, big_bf, [_own_slot(a) for a in big_bf], None, "gather_start")
    fw = {}

    def w_in_ready(*after):
        fw["first"] = _split_wait(_gather_copies, g_ssem, g_rsem, g_srcs, g_lands, after, "gather_win_wait",
                                  only=(0, 1, 2))
        (fw["win"],) = _gather_finish(fw["first"][5:6])
        return fw["win"].reshape(IN_W, D)

    def w_out_ready(*after):
        first = fw["first"]
        lands = _split_wait(_gather_copies, g_ssem, g_rsem, first[:5], [fw["win"]] + list(first[6:]), after,
                            "gather_rest_wait", only=tuple(range(3, 15)))[5:]
        (wout_all,) = _gather_finish(lands[1:2])
        fw["f"] = _split_start(_forward_copies, 9, [], lands[2:], wout_all, "gather_forward_start")
        return wout_all.reshape(D, D), fw["f"][4]

    def ffn_weights(after):
        ssem, rsem, _, lands, _ = fw["f"]
        return _split_wait(_forward_copies, ssem, rsem, [], lands, [after], "gather_forward_wait")

    rs = {}

    def on_ffn_grads(ffn_g):
        oths = ffn_g[1::2]
        rs["ffn_own"] = ffn_g[0::2]
        rs["x"] = _split_start(_sibling_copies, 3, oths, [lax.empty(a.shape, BF16) for a in oths], ffn_g[0],
                               "rs_exchange_ffn_start")
        return rs["x"][4]

    def on_wout_grads(own, oth, after):
        ssem, rsem, srcs, lands, _ = rs["x"]
        recv_ffn = _split_wait(_sibling_copies, ssem, rsem, srcs, lands, [after], "rs_exchange_ffn_wait")[3:]
        recv_wout = _to_sibling([oth], "rs_exchange_wout")
        outs = _chip_partial([own] + list(rs["ffn_own"]), list(recv_wout) + list(recv_ffn), chip_arr, "early")
        rs["early_own"] = outs[4:]
        rs["s"] = _split_start(_scatter_copies, 12, outs[:4],
                               [lax.empty((3,) + a.shape[1:], BF16) for a in outs[:4]], outs[4], "scatter_early_start")
        return rs["s"][4]

    def on_attn_grads(after):
        ssem, rsem, srcs, lands, _ = rs["s"]
        recv_early = _split_wait(_scatter_copies, ssem, rsem, srcs, lands, after, "scatter_early_wait")[4:]
        finals = _sum_chips(list(rs["early_own"]), list(recv_early), "early")
        rs["sh"] = _split_start(_sibling_copies, 4, finals, [lax.empty(a.shape, F32) for a in finals], finals[0],
                                "rs_share_early_start")
        return rs["sh"][4]

    small = (g_pre_mix, g_post_mix, g_pre_ffn, g_post_ffn, w_pool[0], pool_scale, rel_bias, sinks)
    loss, gx, small_grads, ((win_own, win_oth), _, _) = _local_step(
        x[0], loss_target[0], small, w_in_ready, w_out_ready, ffn_weights, c_arr,
        on_ffn_grads, on_wout_grads, on_attn_grads)

    unused = jnp.zeros((1, 1), F32)
    gp = _pack_small(small_grads + (loss,))
    wp = _pack_small((g_pre_mix, g_post_mix, g_pre_ffn, g_post_ffn, w_pool, pool_scale, rel_bias, sinks, unused))
    mp = _pack_small((m_g_pre_mix, m_g_post_mix, m_g_pre_ffn, m_g_post_ffn, m_w_pool, m_pool_scale, m_rel_bias,
                      m_sinks, unused))
    vp = _pack_small((v_g_pre_mix, v_g_post_mix, v_g_pre_ffn, v_g_post_ffn, v_w_pool, v_pool_scale, v_rel_bias,
                      v_sinks, unused))

    moments = (shards(m_w_in, m_w_out, m_w_gate, m_w_up, m_w_down),
               shards(v_w_in, v_w_out, v_w_gate, v_w_up, v_w_down))
    recv_a = _to_sibling([win_oth], "rs_exchange_win")
    outs = _chip_partial([win_own], recv_a, chip_arr, "win")
    w_ssem, w_rsem, w_srcs, w_lands, w_token = _split_start(
        _scatter_copies, 3, outs[:1], [lax.empty((3,) + outs[0].shape[1:], BF16)], gx, "scatter_win_start")
    slots = lax.dynamic_update_slice(lax.empty((8,) + gp.shape, F32), gp[None], (2 * chip + c, 0, 0))
    p_ssem, p_rsem, p_srcs, p_lands, p_token = _split_start(_peer_copies, 7, [gp], [slots], w_token,
                                                            "small_allgather_start")
    ssem, rsem, srcs, lands, _ = rs["sh"]
    shared = _split_wait(_sibling_copies, ssem, rsem, srcs, lands, [p_token], "rs_share_early_wait")
    adam_e = _adamw_shards(shared[:4], shared[4:], big_w[1:], moments[0][1:], moments[1][1:], c_arr, "early")
    recv_win = _split_wait(_scatter_copies, w_ssem, w_rsem, w_srcs, w_lands, [adam_e[0]], "scatter_win_wait")[1:]
    win_final = _sum_chips([outs[1]], recv_win, "win")
    win_sib = _to_sibling(win_final, "rs_share_win")
    adam_w = _adamw_shards(win_final, win_sib, big_w[:1], moments[0][:1], moments[1][:1], c_arr, "win")
    big_g, big_d, big_m, big_v = [[adam_w[i]] + list(adam_e[4 * i:4 * i + 4]) for i in range(4)]

    gathered = _split_wait(_peer_copies, p_ssem, p_rsem, p_srcs, p_lands, [adam_w[0]], "small_allgather_wait")[1]
    flat = _small_sum_adamw(gathered, wp, mp, vp)
    small_out = [flat[7 * kind:7 * kind + 7] for kind in range(4)]
    total_loss = small_out[0][6][16, 0]

    def ordered(small7, big4):
        sg1, sg2, sg3, sg4, swp, ssc, tail = small7
        swp, srb, ssk = swp[None], tail[0:2].reshape(NBUCKET, NQ), tail[8:9, 0:NQ]
        bwin, bwout, bwg, bwu, bwd = big4[0].T[None], big4[1][None], big4[2].T[None], big4[3].T[None], big4[4][None]
        return [sg1, bwin, swp, ssc, srb, ssk, bwout, sg2, sg3, bwg, bwu, bwd, sg4]

    res = [total_loss, gx[None]]
    for sm, bg in zip(small_out, (big_g, big_d, big_m, big_v)):
        res += ordered(sm, bg)
    return tuple(res)
```

```python
import numpy as np
import jax
import jax.numpy as jnp
from jax import lax
from jax.experimental import pallas as pl
from jax.experimental.pallas import tpu as pltpu

F32 = jnp.float32
BF16 = jnp.bfloat16

D = 1024
POOL_W = 512
GROUP = 128
WINDOWS = (2, 4, 8, 16)
HALO = 128
NQ = 8
BLK = 128
NBUCKET = 32
IN_W = 1280
DFF = 2816
NSH = 4
FS = DFF // NSH
WIN_S = IN_W // NSH
WOUT_S = D // NSH
EPS = 1e-6
NEG = -1e30
SCALE = 0.125

ADAM_LR = 0.001
ADAM_B1 = 0.9
ADAM_B2 = 0.999
ADAM_EPS = 1e-08
ADAM_WD = 0.01
ADAM_STEP = 10

TM = 512
TM_POOL = 256
TM_FFN = 512
TM_FFN_FWD = 1024
FFN_ROWS = 256
VMEM_LIMIT = 60 * 1024 * 1024

MESH_T = pl.DeviceIdType.MESH
ANY = pl.BlockSpec(memory_space=pl.ANY)


def _cp(n_grid=0, **kw):
    sem = ("arbitrary",) * n_grid if n_grid else None
    return pltpu.CompilerParams(dimension_semantics=sem, vmem_limit_bytes=VMEM_LIMIT, **kw)


def _dot(a, b):
    return jnp.dot(a, b, preferred_element_type=F32)


def _dot_nt(a, b):
    return lax.dot_general(a, b, (((1,), (1,)), ((), ())), preferred_element_type=F32)


def _dot_tn(a, b):
    return lax.dot_general(a, b, (((0,), (0,)), ((), ())), preferred_element_type=F32)


def _rms(x):
    r = lax.rsqrt(jnp.mean(x * x, axis=-1, keepdims=True) + EPS)
    return r, x * r


def _rms_bwd(r, n, g, dout):
    dn = dout * g
    dx = r * (dn - n * jnp.mean(dn * n, axis=-1, keepdims=True))
    dg = jnp.sum(dout * n, axis=0, keepdims=True)
    return dx, dg


def _split(x, n):
    parts = []
    r = x
    for _ in range(n):
        p = r.astype(BF16)
        parts.append(p)
        r = r - p.astype(F32)
    return parts


def _band_dot(a, x, n):
    acc = None
    for p in _split(x, n):
        t = _dot(a, p)
        acc = t if acc is None else acc + t
    return acc


def _bucket_table():
    qi = np.arange(BLK)[None, :]
    kj = np.arange(2 * BLK)[:, None]
    dist = qi + BLK - kj
    n = np.maximum(dist, 0)
    nf = np.maximum(n, 1).astype(np.float32)
    large = 16 + (np.log(nf / np.float32(16)) / np.float32(np.log(128 / 16)) * np.float32(16)).astype(np.int32)
    large = np.minimum(large, NBUCKET - 1)
    return np.where(n < 16, n, large).astype(np.int32)


def _prenorm(x, g1):
    s = x.shape[0]
    tm = min(TM, s)

    def body(x_ref, g_ref, h_ref):
        _, n = _rms(x_ref[...])
        h_ref[...] = (n * g_ref[...]).astype(BF16)

    row = pl.BlockSpec((tm, D), lambda i: (i, 0))
    return pl.pallas_call(
        body, name="prenorm", grid=(s // tm,),
        in_specs=[row, pl.BlockSpec((1, D), lambda i: (0, 0))], out_specs=row,
        out_shape=jax.ShapeDtypeStruct((s, D), BF16),
        compiler_params=_cp(1))(x, g1)


def _inproj_fwd(h1, w_in):
    s = h1.shape[0]
    tm = min(TM, s)

    def body(h_ref, w_ref, u_ref, q_ref, k_ref, v_ref):
        proj = _dot_nt(h_ref[...], w_ref[...])
        u_ref[...] = proj[:, :512]
        q_ref[...] = proj[:, 512:1024].astype(BF16)
        k_ref[...] = proj[:, 1024:1152].astype(BF16)
        v_ref[...] = proj[:, 1152:1280].astype(BF16)

    row = lambda w: pl.BlockSpec((tm, w), lambda i: (i, 0))
    return pl.pallas_call(
        body, name="inproj_fwd", grid=(s // tm,),
        in_specs=[row(D), pl.BlockSpec((IN_W, D), lambda i: (0, 0))],
        out_specs=[row(512), row(512), row(128), row(128)],
        out_shape=[jax.ShapeDtypeStruct((s, 512), F32), jax.ShapeDtypeStruct((s, 512), BF16),
                   jax.ShapeDtypeStruct((s, 128), BF16), jax.ShapeDtypeStruct((s, 128), BF16)],
        compiler_params=_cp(1))(h1, w_in)


def _window_sums(ext, w, lag_sign, tm, terms):
    rows = lax.broadcasted_iota(jnp.int32, (HALO, 2 * HALO), 0)
    cols = lax.broadcasted_iota(jnp.int32, (HALO, 2 * HALO), 1)
    d = rows + HALO - cols if lag_sign > 0 else cols - rows
    band = jnp.where((d >= 0) & (d < w), 1.0, 0.0).astype(BF16)
    return jnp.concatenate([_band_dot(band, ext[r:r + 2 * HALO], terms) for r in range(0, tm, HALO)], axis=0)


def _pooled(ext, cur, t0, tm):
    t = t0 + lax.broadcasted_iota(jnp.int32, (tm, GROUP), 0)
    out = []
    for g, w in enumerate(WINDOWS):
        sl = slice(g * GROUP, (g + 1) * GROUP)
        cnt = jnp.minimum(t + 1, w).astype(F32)
        out.append(_window_sums(ext[:, sl], w, 1, tm, 2) / cnt - cur[:, sl])
    return out


def _pool_fwd(u, w_pool, pool_scale):
    s = u.shape[0]
    tm = min(TM_POOL, s)
    hb = tm // HALO

    def body(uc_ref, uh_ref, wp_ref, sc_ref, o_ref):
        i = pl.program_id(0)
        cur = uc_ref[...]
        halo = jnp.where(i > 0, uh_ref[...], 0.0)
        ext = jnp.concatenate([halo, cur], axis=0)
        pooled = _pooled(ext, cur, i * tm, tm)
        for g in range(4):
            sl = slice(g * GROUP, (g + 1) * GROUP)
            mixed = _dot(pooled[g].astype(BF16), wp_ref[g])
            o_ref[:, sl] = (mixed * sc_ref[:, sl]).astype(BF16)

    return pl.pallas_call(
        body, name="pool_fwd", grid=(s // tm,),
        in_specs=[pl.BlockSpec((tm, 512), lambda i: (i, 0)),
                  pl.BlockSpec((HALO, 512), lambda i: (jnp.maximum(i * hb - 1, 0), 0)),
                  pl.BlockSpec((4, GROUP, GROUP), lambda i: (0, 0, 0)),
                  pl.BlockSpec((1, 512), lambda i: (0, 0))],
        out_specs=pl.BlockSpec((tm, 512), lambda i: (i, 0)),
        out_shape=jax.ShapeDtypeStruct((s, 512), BF16),
        compiler_params=_cp(1))(u, u, w_pool, pool_scale)


def _bias_table(bucket, rel_bias):
    def body(b_ref, rb_ref, o_ref):
        bucket_v = b_ref[...]
        key = lax.broadcasted_iota(jnp.int32, (2 * BLK, BLK), 0)
        dist = lax.broadcasted_iota(jnp.int32, (2 * BLK, BLK), 1) + BLK - key
        inwin = (dist >= 0) & (dist < BLK)
        for h in range(NQ):
            acc = jnp.zeros((2 * BLK, BLK), F32)
            for b in range(NBUCKET):
                acc = jnp.where(bucket_v == b, rb_ref[b, h], acc)
            rest = jnp.where(inwin, acc, NEG)
            o_ref[1, h] = rest
            o_ref[0, h] = jnp.where(key >= BLK, rest, NEG)

    return pl.pallas_call(
        body, name="bias_table",
        in_specs=[pl.BlockSpec(memory_space=pltpu.VMEM), pl.BlockSpec(memory_space=pltpu.SMEM)],
        out_specs=pl.BlockSpec(memory_space=pltpu.VMEM),
        out_shape=jax.ShapeDtypeStruct((2, NQ, 2 * BLK, BLK), F32),
        compiler_params=_cp())(bucket, rel_bias)


def _kv_band(ref, n, transposed):
    pstart = pl.multiple_of(jnp.maximum(n - 1, 0) * BLK, BLK)
    cstart = pl.multiple_of(n * BLK, BLK)
    lo = lax.broadcasted_iota(jnp.int32, (2 * BLK, 128), 1) < 64
    band = jnp.concatenate([ref[pl.ds(pstart, BLK), :], ref[pl.ds(cstart, BLK), :]], axis=0).astype(F32)
    rot = pltpu.roll(band, 64, axis=1)
    dup = (jnp.where(lo, band, rot), jnp.where(lo, rot, band))
    plain = [d.astype(BF16) for d in dup]
    if not transposed:
        return plain, None, (lo, pstart, cstart)
    row_lo = lax.broadcasted_iota(jnp.int32, (128, 2 * BLK), 0) < 64
    tr = [[jnp.where(row_lo, d.T, 0.0).astype(BF16), jnp.where(row_lo, 0.0, d.T).astype(BF16)] for d in dup]
    return plain, tr, (lo, pstart, cstart)


def _attn_probs(qm, kk, bias, sink):
    s = _dot_nt(kk, qm) + bias
    m = jnp.maximum(jnp.max(s, axis=0, keepdims=True), sink)
    p = jnp.exp(s - m)
    es = jnp.exp(sink - m)
    inv = 1.0 / (jnp.sum(p, axis=0, keepdims=True) + es)
    return p * inv, es * inv


def _attn_fwd(q, k, v, bias, sinks):
    s = q.shape[0]

    def body(q_ref, k_ref, v_ref, b_ref, sk_ref, o_ref):
        n = pl.program_id(0)
        kk, _, _ = _kv_band(k_ref, n, False)
        _, vt, _ = _kv_band(v_ref, n, True)
        bidx = jnp.minimum(n, 1)
        lo_q = lax.broadcasted_iota(jnp.int32, (BLK, 128), 1) < 64
        for p in range(4):
            h = p // 2
            qt = q_ref[:, p * 128:(p + 1) * 128].astype(F32) * SCALE
            acc_t = jnp.zeros((128, BLK), F32)
            for e in range(2):
                head = 2 * p + e
                qm = jnp.where(lo_q if e == 0 else jnp.logical_not(lo_q), qt, 0.0).astype(BF16)
                prob, _ = _attn_probs(qm, kk[h], b_ref[bidx, head], sk_ref[0, head])
                acc_t = acc_t + _dot(vt[h][e], prob.astype(BF16))
            o_ref[:, p * 128:(p + 1) * 128] = acc_t.T.astype(BF16)

    return pl.pallas_call(
        body, name="attn_fwd", grid=(s // BLK,),
        in_specs=[pl.BlockSpec((BLK, 512), lambda i: (i, 0)),
                  pl.BlockSpec((s, 128), lambda i: (0, 0)),
                  pl.BlockSpec((s, 128), lambda i: (0, 0)),
                  pl.BlockSpec((2, NQ, 2 * BLK, BLK), lambda i: (0, 0, 0, 0)),
                  pl.BlockSpec(memory_space=pltpu.SMEM)],
        out_specs=pl.BlockSpec((BLK, 512), lambda i: (i, 0)),
        out_shape=jax.ShapeDtypeStruct((s, 512), BF16),
        compiler_params=_cp(1))(q, k, v, bias, sinks)


def _outproj_fwd(pool_o, attn_o, w_out, x, g2, g3):
    s = x.shape[0]
    tm = min(TM, s)

    def body(p_ref, a_ref, w_ref, x_ref, g2_ref, g3_ref, mix_ref, x1_ref, h2_ref):
        mix = _dot(p_ref[...], w_ref[0:512, :]) + _dot(a_ref[...], w_ref[512:1024, :])
        mix_ref[...] = mix
        _, n2 = _rms(mix)
        x1 = x_ref[...] + n2 * g2_ref[...]
        x1_ref[...] = x1
        _, n3 = _rms(x1)
        h2_ref[...] = (n3 * g3_ref[...]).astype(BF16)

    row = lambda w: pl.BlockSpec((tm, w), lambda i: (i, 0))
    vec = pl.BlockSpec((1, D), lambda i: (0, 0))
    return pl.pallas_call(
        body, name="outproj_fwd", grid=(s // tm,),
        in_specs=[row(512), row(512), pl.BlockSpec((D, D), lambda i: (0, 0)), row(D), vec, vec],
        out_specs=[row(D), row(D), row(D)],
        out_shape=[jax.ShapeDtypeStruct((s, D), F32), jax.ShapeDtypeStruct((s, D), F32),
                   jax.ShapeDtypeStruct((s, D), BF16)],
        compiler_params=_cp(1))(pool_o, attn_o, w_out, x, g2, g3)


def _silu_parts(g):
    sg = jax.nn.sigmoid(g)
    return sg, g * sg


def _ffn_fwd(h2, wg, wu, wd, x1, target, g4):
    s = h2.shape[0]
    tm = min(TM_FFN_FWD, s)

    def body(h_ref, wg_ref, wu_ref, wd_ref, x1_ref, t_ref, g4_ref,
             gate_ref, up_ref, df_ref, dy_ref, loss_ref, gg4_ref, f_acc):
        i = pl.program_id(0)
        j = pl.program_id(1)
        first = j == 0
        for r in range(0, tm, FFN_ROWS):
            rows = pl.ds(r, min(FFN_ROWS, tm))
            h = h_ref[rows, :]
            gate = _dot_nt(h, wg_ref[...])
            up = _dot_nt(h, wu_ref[...])
            gate_ref[rows, :] = gate.astype(BF16)
            up_ref[rows, :] = up.astype(BF16)
            _, sl = _silu_parts(gate)
            contrib = _dot((sl * up).astype(BF16), wd_ref[...])
            f_acc[rows, :] = jnp.where(first, contrib, f_acc[rows, :] + contrib)

        @pl.when((i == 0) & (j == 0))
        def _():
            loss_ref[...] = jnp.zeros_like(loss_ref)
            gg4_ref[...] = jnp.zeros_like(gg4_ref)

        @pl.when(j == NSH - 1)
        def _():
            r4, n4 = _rms(f_acc[...])
            g4v = g4_ref[...]
            err = x1_ref[...] + n4 * g4v - t_ref[...]
            loss_ref[...] += (0.5 / D) * jnp.sum(err * err).reshape(1, 1)
            dy = err * (1.0 / D)
            dy_ref[...] = dy
            df, dg = _rms_bwd(r4, n4, g4v, dy)
            gg4_ref[...] += dg
            df_ref[...] = df.astype(BF16)

    row = lambda w: pl.BlockSpec((tm, w), lambda i, j: (i, 0))
    late = lambda w: pl.BlockSpec((tm, w), lambda i, j: (jnp.where(j == 0, jnp.maximum(i - 1, 0), i), 0))
    sh = lambda r, c: pl.BlockSpec((None, r, c), lambda i, j: (j, 0, 0))
    act = pl.BlockSpec((None, tm, FS), lambda i, j: (j, i, 0))
    vec = pl.BlockSpec((1, D), lambda i, j: (0, 0))
    return pl.pallas_call(
        body, name="ffn_fwd", grid=(s // tm, NSH),
        in_specs=[row(D), sh(FS, D), sh(FS, D), sh(FS, D), late(D), late(D), vec],
        out_specs=[act, act, row(D), row(D), pl.BlockSpec((1, 1), lambda i, j: (0, 0)), vec],
        out_shape=[jax.ShapeDtypeStruct((NSH, s, FS), BF16), jax.ShapeDtypeStruct((NSH, s, FS), BF16),
                   jax.ShapeDtypeStruct((s, D), BF16), jax.ShapeDtypeStruct((s, D), F32),
                   jax.ShapeDtypeStruct((1, 1), F32), jax.ShapeDtypeStruct((1, D), F32)],
        scratch_shapes=[pltpu.VMEM((tm, D), F32)],
        compiler_params=_cp(2))(h2, wg, wu, wd, x1, target, g4)


def _ffn_bwd_act(df, gate, up, wg, wu, wd, dy, x1, mix, g3, g2):
    s = df.shape[0]
    tm = min(TM_FFN, s)

    def body(df_ref, gate_ref, up_ref, wg_ref, wu_ref, wd_ref, dy_ref, x1_ref, mix_ref, g3_ref, g2_ref,
             dgate_ref, dup_ref, dx1_ref, dmix_ref, gg3_ref, gg2_ref, acc):
        i = pl.program_id(0)
        j = pl.program_id(1)
        first = j == 0
        for r in range(0, tm, FFN_ROWS):
            rows = pl.ds(r, min(FFN_ROWS, tm))
            da = _dot_nt(df_ref[rows, :], wd_ref[...])
            g = gate_ref[rows, :].astype(F32)
            u = up_ref[rows, :].astype(F32)
            sg, sl = _silu_parts(g)
            dgate = (da * u * (sg * (1.0 + g * (1.0 - sg)))).astype(BF16)
            dup = (da * sl).astype(BF16)
            dgate_ref[rows, :] = dgate
            dup_ref[rows, :] = dup
            contrib = _dot(dgate, wg_ref[...]) + _dot(dup, wu_ref[...])
            acc[rows, :] = jnp.where(first, contrib, acc[rows, :] + contrib)

        @pl.when((i == 0) & (j == 0))
        def _():
            gg3_ref[...] = jnp.zeros_like(gg3_ref)
            gg2_ref[...] = jnp.zeros_like(gg2_ref)

        @pl.when(j == NSH - 1)
        def _():
            r3, n3 = _rms(x1_ref[...])
            dx1n, dg3 = _rms_bwd(r3, n3, g3_ref[...], acc[...])
            dx1 = dy_ref[...] + dx1n
            dx1_ref[...] = dx1
            gg3_ref[...] += dg3
            r2, n2 = _rms(mix_ref[...])
            dmix, dg2 = _rms_bwd(r2, n2, g2_ref[...], dx1)
            gg2_ref[...] += dg2
            dmix_ref[...] = dmix.astype(BF16)

    row = lambda w: pl.BlockSpec((tm, w), lambda i, j: (i, 0))
    late = lambda w: pl.BlockSpec((tm, w), lambda i, j: (jnp.where(j == 0, jnp.maximum(i - 1, 0), i), 0))
    sh = lambda r, c: pl.BlockSpec((None, r, c), lambda i, j: (j, 0, 0))
    act = pl.BlockSpec((None, tm, FS), lambda i, j: (j, i, 0))
    vec = pl.BlockSpec((1, D), lambda i, j: (0, 0))
    return pl.pallas_call(
        body, name="ffn_bwd_act", grid=(s // tm, NSH),
        in_specs=[row(D), act, act, sh(FS, D), sh(FS, D), sh(FS, D), late(D), late(D), late(D), vec, vec],
        out_specs=[act, act, row(D), row(D), vec, vec],
        out_shape=[jax.ShapeDtypeStruct((NSH, s, FS), BF16), jax.ShapeDtypeStruct((NSH, s, FS), BF16),
                   jax.ShapeDtypeStruct((s, D), F32), jax.ShapeDtypeStruct((s, D), BF16),
                   jax.ShapeDtypeStruct((1, D), F32), jax.ShapeDtypeStruct((1, D), F32)],
        scratch_shapes=[pltpu.VMEM((tm, D), F32)],
        compiler_params=_cp(2))(df, gate, up, wg, wu, wd, dy, x1, mix, g3, g2)


SMEM_SPEC = pl.BlockSpec(memory_space=pltpu.SMEM)


def _emit_halves(acc_ref, row0, rows, c, own_ref, oth_ref):
    half = rows // 2
    own_ref[...] = acc_ref[pl.ds(row0 + pl.multiple_of(c * half, 8), half), :]
    oth_ref[...] = acc_ref[pl.ds(row0 + pl.multiple_of((1 - c) * half, 8), half), :].astype(BF16)


def _half_shapes(rows):
    return [jax.ShapeDtypeStruct((NSH, rows // 2, D), F32), jax.ShapeDtypeStruct((NSH, rows // 2, D), BF16)]


def _ffn_bwd_w(h2, gate, up, dgate, dup, df, c_arr):
    s = h2.shape[0]
    tm = min(TM_FFN_FWD, s)
    nt = s // tm

    def body(c_ref, h_ref, gate_ref, up_ref, dgate_ref, dup_ref, df_ref, *rest):
        outs, accs = rest[:6], rest[6:]
        i = pl.program_id(1)
        @pl.when(i == 0)
        def _():
            for acc in accs:
                acc[...] = jnp.zeros_like(acc)

        h = h_ref[...]
        accs[0][...] += _dot_tn(dgate_ref[...], h)
        accs[1][...] += _dot_tn(dup_ref[...], h)
        _, sl = _silu_parts(gate_ref[...].astype(F32))
        a = (sl * up_ref[...].astype(F32)).astype(BF16)
        accs[2][...] += _dot_tn(a, df_ref[...])

        @pl.when(i == nt - 1)
        def _():
            for t, acc in enumerate(accs):
                _emit_halves(acc, 0, FS, c_ref[0], outs[2 * t], outs[2 * t + 1])

    row = lambda w: pl.BlockSpec((tm, w), lambda j, i: (i, 0))
    act = pl.BlockSpec((None, tm, FS), lambda j, i: (j, i, 0))
    half = pl.BlockSpec((None, FS // 2, D), lambda j, i: (j, 0, 0))
    return pl.pallas_call(
        body, name="ffn_bwd_w", grid=(NSH, nt),
        in_specs=[SMEM_SPEC, row(D), act, act, act, act, row(D)],
        out_specs=[half] * 6,
        out_shape=_half_shapes(FS) * 3,
        scratch_shapes=[pltpu.VMEM((FS, D), F32)] * 3,
        compiler_params=_cp(2))(c_arr, h2, gate, up, dgate, dup, df)


def _outproj_bwd(dmix, w_out, pool_o, attn_o, c_arr, dep=None):
    s = dmix.shape[0]
    tm = min(TM, s)
    nt = s // tm

    def body(c_ref, dm_ref, w_ref, p_ref, a_ref, *rest):
        dpool_ref, dattn_ref, own_ref, oth_ref, gw_ref = rest[-5:]
        i = pl.program_id(0)

        @pl.when(i == 0)
        def _():
            gw_ref[...] = jnp.zeros_like(gw_ref)

        dm = dm_ref[...]
        dpool_ref[...] = _dot_nt(dm, w_ref[0:512, :])
        dattn_ref[...] = _dot_nt(dm, w_ref[512:1024, :]).astype(BF16)
        gw_ref[0:512, :] += _dot_tn(p_ref[...], dm)
        gw_ref[512:1024, :] += _dot_tn(a_ref[...], dm)

        @pl.when(i == nt - 1)
        def _():
            for k in range(NSH):
                _emit_halves(gw_ref, k * WOUT_S, WOUT_S, c_ref[0], own_ref.at[k], oth_ref.at[k])

    row = lambda w: pl.BlockSpec((tm, w), lambda i: (i, 0))
    full = pl.BlockSpec((D, D), lambda i: (0, 0))
    half = pl.BlockSpec((NSH, WOUT_S // 2, D), lambda i: (0, 0, 0))
    return pl.pallas_call(
        body, name="outproj_bwd", grid=(nt,),
        in_specs=[SMEM_SPEC, row(D), full, row(512), row(512)] + ([] if dep is None else [ANY]),
        out_specs=[row(512), row(512), half, half],
        out_shape=[jax.ShapeDtypeStruct((s, 512), F32), jax.ShapeDtypeStruct((s, 512), BF16)] + _half_shapes(WOUT_S),
        scratch_shapes=[pltpu.VMEM((D, D), F32)],
        compiler_params=_cp(1))(c_arr, dmix, w_out, pool_o, attn_o, *([] if dep is None else [dep]))


def _pool_bwd(u, dpool, w_pool, pool_scale, dep=None):
    s = u.shape[0]
    tm = min(TM_POOL, s)
    hb = tm // HALO
    nt = s // tm
    last_halo = s // HALO - 1

    def body(uc_ref, uh_ref, dc_ref, dn_ref, wp_ref, sc_ref, *rest):
        du_ref, gwp_ref, gsc_ref = rest[-3:]
        i = pl.program_id(0)

        @pl.when(i == 0)
        def _():
            gwp_ref[...] = jnp.zeros_like(gwp_ref)
            gsc_ref[...] = jnp.zeros_like(gsc_ref)

        cur = uc_ref[...]
        halo = jnp.where(i > 0, uh_ref[...], 0.0)
        pooled = _pooled(jnp.concatenate([halo, cur], axis=0), cur, i * tm, tm)
        dcur = dc_ref[...]
        dnext = jnp.where(i < nt - 1, dn_ref[...], 0.0)
        dext = jnp.concatenate([dcur, dnext], axis=0)
        t_ext = i * tm + lax.broadcasted_iota(jnp.int32, (tm + HALO, GROUP), 0)
        for g, w in enumerate(WINDOWS):
            sl = slice(g * GROUP, (g + 1) * GROUP)
            pb = pooled[g].astype(BF16)
            wp = wp_ref[g]
            mixed = _dot(pb, wp)
            gsc_ref[:, sl] += jnp.sum(dcur[:, sl] * mixed, axis=0, keepdims=True)
            dmixed = (dext[:, sl] * sc_ref[:, sl]).astype(BF16)
            gwp_ref[g] += _dot_tn(pb, dmixed[0:tm])
            dpooled = _dot_nt(dmixed, wp)
            z = dpooled / jnp.minimum(t_ext + 1, w).astype(F32)
            du_ref[:, sl] = (_window_sums(z, w, -1, tm, 2) - dpooled[0:tm]).astype(BF16)

    return pl.pallas_call(
        body, name="pool_bwd", grid=(nt,),
        in_specs=[pl.BlockSpec((tm, 512), lambda i: (i, 0)),
                  pl.BlockSpec((HALO, 512), lambda i: (jnp.maximum(i * hb - 1, 0), 0)),
                  pl.BlockSpec((tm, 512), lambda i: (i, 0)),
                  pl.BlockSpec((HALO, 512), lambda i: (jnp.minimum((i + 1) * hb, last_halo), 0)),
                  pl.BlockSpec((4, GROUP, GROUP), lambda i: (0, 0, 0)),
                  pl.BlockSpec((1, 512), lambda i: (0, 0))] + ([] if dep is None else [ANY]),
        out_specs=[pl.BlockSpec((tm, 512), lambda i: (i, 0)),
                   pl.BlockSpec((4, GROUP, GROUP), lambda i: (0, 0, 0)),
                   pl.BlockSpec((1, 512), lambda i: (0, 0))],
        out_shape=[jax.ShapeDtypeStruct((s, 512), BF16), jax.ShapeDtypeStruct((4, GROUP, GROUP), F32),
                   jax.ShapeDtypeStruct((1, 512), F32)],
        compiler_params=_cp(1))(u, u, dpool, dpool, w_pool, pool_scale, *([] if dep is None else [dep]))


def _attn_bwd(q, k, v, do, bias, sinks, bucket, dep=None):
    s = q.shape[0]
    nblk = s // BLK

    def body(q_ref, do_ref, k_ref, v_ref, b_ref, sk_ref, bk_ref, *rest):
        dq_ref, dk_ref, dv_ref, grb_ref, gsk_ref, dss_ref = rest[-6:]
        n = pl.program_id(0)

        @pl.when(n == 0)
        def _():
            dk_ref[...] = jnp.zeros_like(dk_ref)
            dv_ref[...] = jnp.zeros_like(dv_ref)
            dss_ref[...] = jnp.zeros_like(dss_ref)
            gsk_ref[...] = jnp.zeros_like(gsk_ref)

        kk, kt, (lo, pstart, cstart) = _kv_band(k_ref, n, True)
        vv, _, _ = _kv_band(v_ref, n, False)
        bidx = jnp.minimum(n, 1)
        lo_q = lax.broadcasted_iota(jnp.int32, (BLK, 128), 1) < 64
        dkk = [jnp.zeros((2 * BLK, 128), F32), jnp.zeros((2 * BLK, 128), F32)]
        dvv = [jnp.zeros((2 * BLK, 128), F32), jnp.zeros((2 * BLK, 128), F32)]
        for p in range(4):
            h = p // 2
            qt = q_ref[:, p * 128:(p + 1) * 128].astype(F32) * SCALE
            dot = do_ref[:, p * 128:(p + 1) * 128]
            dq_t = jnp.zeros((128, BLK), F32)
            for e in range(2):
                head = 2 * p + e
                sel_q = lo_q if e == 0 else jnp.logical_not(lo_q)
                qm = jnp.where(sel_q, qt, 0.0).astype(BF16)
                prob, ps = _attn_probs(qm, kk[h], b_ref[bidx, head], sk_ref[0, head])
                dom = jnp.where(sel_q, dot, jnp.zeros_like(dot))
                dp = _dot_nt(vv[h], dom)
                delta = jnp.sum(prob * dp, axis=0, keepdims=True)
                ds = prob * (dp - delta)
                gsk_ref[pl.ds(head, 1), :] += jnp.broadcast_to(-jnp.sum(ps * delta).reshape(1, 1), (1, 128))
                dss_ref[head] += ds
                dsb = ds.astype(BF16)
                dq_t = dq_t + _dot(kt[h][e], dsb)
                dkk[h] = dkk[h] + _dot(dsb, qm)
                dvv[h] = dvv[h] + _dot(prob.astype(BF16), dom)
            dq_ref[:, p * 128:(p + 1) * 128] = (dq_t.T * SCALE).astype(BF16)
        for acc, ref in ((dkk, dk_ref), (dvv, dv_ref)):
            f0 = acc[0] + pltpu.roll(acc[0], 64, axis=1)
            f1 = acc[1] + pltpu.roll(acc[1], 64, axis=1)
            band = jnp.where(lo, f0, f1)
            ref[pl.ds(pstart, BLK), :] += band[0:BLK]
            ref[pl.ds(cstart, BLK), :] += band[BLK:2 * BLK]

        @pl.when(n == nblk - 1)
        def _():
            _relbias_grad(dss_ref, bk_ref[...], grb_ref)

    blk = pl.BlockSpec((BLK, 512), lambda i: (i, 0))
    kv = pl.BlockSpec((s, 128), lambda i: (0, 0))
    row8 = pl.BlockSpec((NQ, 128), lambda i: (0, 0))
    return pl.pallas_call(
        body, name="attn_bwd", grid=(nblk,),
        in_specs=[blk, blk, kv, kv, pl.BlockSpec((2, NQ, 2 * BLK, BLK), lambda i: (0, 0, 0, 0)),
                  pl.BlockSpec(memory_space=pltpu.SMEM), pl.BlockSpec((2 * BLK, BLK), lambda i: (0, 0))]
        + ([] if dep is None else [ANY]),
        out_specs=[blk, kv, kv, row8, row8],
        out_shape=[jax.ShapeDtypeStruct((s, 512), BF16), jax.ShapeDtypeStruct((s, 128), F32),
                   jax.ShapeDtypeStruct((s, 128), F32), jax.ShapeDtypeStruct((NQ, 128), F32),
                   jax.ShapeDtypeStruct((NQ, 128), F32)],
        scratch_shapes=[pltpu.VMEM((NQ, 2 * BLK, BLK), F32)],
        compiler_params=_cp(1))(q, do, k, v, bias, sinks, bucket, *([] if dep is None else [dep]))


def _relbias_grad(ds_ref, bucket_v, o_ref):
    lane = lax.broadcasted_iota(jnp.int32, (NQ, 128), 1)
    head_row = lax.broadcasted_iota(jnp.int32, (NQ, BLK), 0)
    acc = jnp.zeros((NQ, 128), F32)
    for b in range(NBUCKET):
        sel = bucket_v == b
        stack = jnp.zeros((NQ, BLK), F32)
        for h in range(NQ):
            col_sums = jnp.sum(jnp.where(sel, ds_ref[h], 0.0), axis=0, keepdims=True)
            stack = jnp.where(head_row == h, col_sums, stack)
        acc = acc + jnp.where(lane == b, jnp.sum(stack, axis=1, keepdims=True), 0.0)
    o_ref[...] = acc


def _inproj_bwd(x, g1, du, dq, dk, dv, w_in, dx1, c_arr, dep=None):
    s = x.shape[0]
    tm = min(TM, s)
    nt = s // tm
    cols = ((0, 512), (512, 1024), (1024, 1152), (1152, 1280))

    def body(c_ref, x_ref, g_ref, du_ref, dq_ref, dk_ref, dv_ref, w_ref, dx1_ref, *rest):
        gx_ref, own_ref, oth_ref, gg_ref, gw_ref = rest[-5:]
        i = pl.program_id(0)

        @pl.when(i == 0)
        def _():
            gw_ref[...] = jnp.zeros_like(gw_ref)
            gg_ref[...] = jnp.zeros_like(gg_ref)

        gv = g_ref[...]
        r1, n1 = _rms(x_ref[...])
        h = (n1 * gv).astype(BF16)
        parts = (du_ref[...], dq_ref[...], dk_ref[...].astype(BF16), dv_ref[...].astype(BF16))
        dh = None
        for (a, b), dpart in zip(cols, parts):
            t = _dot(dpart, w_ref[a:b, :])
            dh = t if dh is None else dh + t
            gw_ref[a:b, :] += _dot_tn(dpart, h)
        dx, dg = _rms_bwd(r1, n1, gv, dh)
        gg_ref[...] += dg
        gx_ref[...] = dx1_ref[...] + dx

        @pl.when(i == nt - 1)
        def _():
            for k in range(NSH):
                _emit_halves(gw_ref, k * WIN_S, WIN_S, c_ref[0], own_ref.at[k], oth_ref.at[k])

    row = lambda w: pl.BlockSpec((tm, w), lambda i: (i, 0))
    vec = pl.BlockSpec((1, D), lambda i: (0, 0))
    full = pl.BlockSpec((IN_W, D), lambda i: (0, 0))
    half = pl.BlockSpec((NSH, WIN_S // 2, D), lambda i: (0, 0, 0))
    return pl.pallas_call(
        body, name="inproj_bwd", grid=(nt,),
        in_specs=[SMEM_SPEC, row(D), vec, row(512), row(512), row(128), row(128), full, row(D)]
        + ([] if dep is None else [ANY]),
        out_specs=[row(D), half, half, vec],
        out_shape=[jax.ShapeDtypeStruct((s, D), F32)] + _half_shapes(WIN_S) + [jax.ShapeDtypeStruct((1, D), F32)],
        scratch_shapes=[pltpu.VMEM((IN_W, D), F32)],
        compiler_params=_cp(1))(c_arr, x, g1, du, dq, dk, dv, w_in, dx1, *([] if dep is None else [dep]))


def _local_step(x, target, small, w_in, w_out, ffn_weights, c_arr, on_ffn_grads=None, on_wout_grads=None,
                on_attn_grads=None):
    g1, g2, g3, g4, w_pool, pool_scale, rel_bias, sinks = small
    bucket = jnp.asarray(_bucket_table())
    wp_bf = w_pool.astype(BF16)
    bias = _bias_table(bucket, rel_bias)
    h1 = _prenorm(x, g1)
    if callable(w_in):
        w_in = w_in(bias, h1)
    u, q, k, v = _inproj_fwd(h1, w_in)
    pool_o = _pool_fwd(u, wp_bf, pool_scale)
    attn_o = _attn_fwd(q, k, v, bias, sinks)
    g2_fwd = g2
    if callable(w_out):
        w_out, after = w_out(pool_o, attn_o)
        g2_fwd = g2 + after[:1, :1]
    mix, x1, h2 = _outproj_fwd(pool_o, attn_o, w_out, x, g2_fwd, g3)
    wg, wu, wd = ffn_weights(h2) if callable(ffn_weights) else ffn_weights
    gate, up, df, dy, loss, gg4 = _ffn_fwd(h2, wg, wu, wd, x1, target, g4)
    dgate, dup, dx1, dmix, gg3, gg2 = _ffn_bwd_act(df, gate, up, wg, wu, wd, dy, x1, mix, g3, g2)
    ffn_g = _ffn_bwd_w(h2, gate, up, dgate, dup, df, c_arr)
    dep = None if on_ffn_grads is None else on_ffn_grads(ffn_g)
    dpool, dattn, wout_own, wout_oth = _outproj_bwd(dmix, w_out, pool_o, attn_o, c_arr, dep)
    dep = None if on_wout_grads is None else on_wout_grads(wout_own, wout_oth, dpool)
    du, gwp, gsc = _pool_bwd(u, dpool, wp_bf, pool_scale, dep)
    dq, dk, dv, grb, gsk = _attn_bwd(q, k, v, dattn, bias, sinks, bucket, dep)
    dep = None if on_attn_grads is None else on_attn_grads([du, dq])
    gx, win_own, win_oth, gg1 = _inproj_bwd(x, g1, du, dq, dk, dv, w_in, dx1, c_arr, dep)
    small_grads = (gg1, gg2, gg3, gg4, gwp, gsc, grb[:, :NBUCKET].T, gsk[:, 0].reshape(1, NQ))
    return loss, gx, small_grads, ((win_own, win_oth), (wout_own, wout_oth), ffn_g)


def _place():
    x, y, c = lax.axis_index("x"), lax.axis_index("y"), lax.axis_index("c")
    chips = ((1 - x, y), (x, 1 - y), (1 - x, 1 - y))
    return x, y, c, chips


def _remote(src, dst, ssem, rsem, dev):
    return pltpu.make_async_remote_copy(src_ref=src, dst_ref=dst, send_sem=ssem, recv_sem=rsem,
                                        device_id=dev, device_id_type=MESH_T)


def _own_slot(shard):
    me = 2 * lax.axis_index("x") + lax.axis_index("y")
    return lax.dynamic_update_slice(lax.empty((NSH,) + shard.shape, shard.dtype), shard[None], (me, 0, 0))


def _to_sibling(arrs, name):
    nt = len(arrs)

    def body(*refs):
        srcs, dsts = refs[:nt], refs[nt:2 * nt]
        ssem, rsem = refs[2 * nt:]
        x, y, c, _ = _place()
        cps = [_remote(srcs[t], dsts[t], ssem.at[t], rsem.at[t], (x, y, 1 - c)) for t in range(nt)]
        for cp in cps:
            cp.start()
        for cp in cps:
            cp.wait()

    return pl.pallas_call(
        body, name=name, in_specs=[ANY] * nt, out_specs=[ANY] * nt,
        out_shape=[jax.ShapeDtypeStruct(a.shape, a.dtype) for a in arrs],
        scratch_shapes=[pltpu.SemaphoreType.DMA((nt,)), pltpu.SemaphoreType.DMA((nt,))],
        compiler_params=_cp())(*arrs)


HBM_SPEC = pl.BlockSpec(memory_space=pltpu.HBM)
SEM_SPEC = pl.BlockSpec(memory_space=pltpu.SEMAPHORE)
DATAFLOW = pltpu.SideEffectType.DATAFLOW_SIDE_EFFECTING


def _gather_copies(srcs, lands, ssem, rsem):
    x, y, c, chips = _place()
    me = 2 * x + y
    out = []
    for t in range(len(srcs)):
        half = srcs[t].shape[0] // 2
        mine = pl.ds(pl.multiple_of(c * half, 16), half)
        for j, (px, py) in enumerate(chips):
            k = 3 * t + j
            send = _remote(srcs[t].at[mine], lands[t].at[me, mine], ssem.at[k], rsem.at[k], (px, py, c))
            blk = lands[t].at[2 * px + py, mine]
            out.append((send, _remote(blk, blk, ssem.at[k], rsem.at[k], (px, py, c))))
    return out


def _scatter_copies(srcs, lands, ssem, rsem):
    x, y, c, chips = _place()
    out = []
    for t in range(len(srcs)):
        for j, (px, py) in enumerate(chips):
            k = 3 * t + j
            send = _remote(srcs[t].at[2 * px + py], lands[t].at[j], ssem.at[k], rsem.at[k], (px, py, c))
            out.append((send, _remote(lands[t].at[j], lands[t].at[j], ssem.at[k], rsem.at[k], (px, py, c))))
    return out


def _sibling_copies(srcs, lands, ssem, rsem):
    x, y, c, _ = _place()
    sib = (x, y, 1 - c)
    return [(_remote(srcs[t], lands[t], ssem.at[t], rsem.at[t], sib),
             _remote(lands[t], lands[t], ssem.at[t], rsem.at[t], sib)) for t in range(len(srcs))]


def _forward_copies(srcs, lands, ssem, rsem):
    x, y, c, chips = _place()
    sib = (x, y, 1 - c)
    out = []
    for t in range(len(lands)):
        half = lands[t].shape[1] // 2
        mine = pl.ds(pl.multiple_of(c * half, 16), half)
        other = pl.ds(pl.multiple_of((1 - c) * half, 16), half)
        for j, (px, py) in enumerate(chips):
            k = 3 * t + j
            blk_m, blk_o = lands[t].at[2 * px + py, mine], lands[t].at[2 * px + py, other]
            out.append((_remote(blk_m, blk_m, ssem.at[k], rsem.at[k], sib),
                        _remote(blk_o, blk_o, ssem.at[k], rsem.at[k], sib)))
    return out


def _peer_copies(srcs, lands, ssem, rsem):
    x, y, c, _ = _place()
    me = 4 * x + 2 * y + c
    out = []
    for kk in range(1, 8):
        px, py, pc = x ^ (kk >> 2), y ^ ((kk >> 1) & 1), c ^ (kk & 1)
        send = _remote(srcs[0], lands[0].at[me], ssem.at[kk - 1], rsem.at[kk - 1], (px, py, pc))
        slot = lands[0].at[4 * px + 2 * py + pc]
        out.append((send, _remote(slot, slot, ssem.at[kk - 1], rsem.at[kk - 1], (px, py, pc))))
    return out


def _split_start(plan, ncopy, srcs, lands, after, name):
    ns, nbuf = len(srcs), len(srcs) + len(lands)
    extra = [] if after is None else [after]
    n_in = nbuf + len(extra)

    def body(*refs):
        ssem, rsem, token = refs[n_in], refs[n_in + 1], refs[-1]
        for send, _ in plan(refs[:ns], refs[ns:nbuf], ssem, rsem):
            send.start()
        token[...] = jnp.zeros_like(token)

    bufs = [pltpu.with_memory_space_constraint(a, pltpu.HBM) for a in list(srcs) + list(lands)]
    outs = pl.pallas_call(
        body, name=name, in_specs=[HBM_SPEC] * nbuf + [ANY] * len(extra),
        out_specs=[SEM_SPEC, SEM_SPEC] + [HBM_SPEC] * nbuf + [pl.BlockSpec(memory_space=pltpu.VMEM)],
        out_shape=[pltpu.SemaphoreType.DMA((ncopy,)), pltpu.SemaphoreType.DMA((ncopy,))]
        + [pltpu.HBM(a.shape, a.dtype) for a in bufs] + [jax.ShapeDtypeStruct((8, 128), F32)],
        input_output_aliases={i: 2 + i for i in range(nbuf)},
        compiler_params=pltpu.CompilerParams(has_side_effects=DATAFLOW))(*bufs, *extra)
    return outs[0], outs[1], outs[2:2 + ns], outs[2 + ns:2 + nbuf], outs[-1]


def _split_wait(plan, ssem, rsem, srcs, lands, after, name, only=None):
    ns, nbuf = len(srcs), len(srcs) + len(lands)

    def body(*refs):
        s_ref, r_ref = refs[nbuf], refs[nbuf + 1]
        for k, (send, recv) in enumerate(plan(refs[:ns], refs[ns:nbuf], s_ref, r_ref)):
            if only is None or k in only:
                send.wait_send()
                recv.wait_recv()

    outs = pl.pallas_call(
        body, name=name, in_specs=[HBM_SPEC] * nbuf + [SEM_SPEC, SEM_SPEC] + [ANY] * len(after),
        out_specs=[HBM_SPEC] * nbuf,
        out_shape=[pltpu.HBM(a.shape, a.dtype) for a in list(srcs) + list(lands)],
        input_output_aliases={i: i for i in range(nbuf)},
        compiler_params=pltpu.CompilerParams(has_side_effects=DATAFLOW))(*srcs, *lands, ssem, rsem, *after)
    return outs


def _gather_finish(lands):
    nt = len(lands)

    def body(*refs):
        outs = refs[nt:2 * nt]
        dsend, drecv = refs[2 * nt:]
        x, y, c, chips = _place()
        sib = (x, y, 1 - c)
        sends = []
        for t in range(nt):
            half = outs[t].shape[1] // 2
            mine = pl.ds(pl.multiple_of(c * half, 16), half)
            for j, (px, py) in enumerate(chips):
                blk = outs[t].at[2 * px + py, mine]
                cp = _remote(blk, blk, dsend.at[t, j], drecv.at[t, j], sib)
                cp.start()
                sends.append(cp)
        for t in range(nt):
            half = outs[t].shape[1] // 2
            other = pl.ds(pl.multiple_of((1 - c) * half, 16), half)
            for j, (px, py) in enumerate(chips):
                blk = outs[t].at[2 * px + py, other]
                _remote(blk, blk, dsend.at[t, j], drecv.at[t, j], sib).wait_recv()
        for cp in sends:
            cp.wait_send()

    return pl.pallas_call(
        body, name="gather_finish", in_specs=[ANY] * nt, out_specs=[ANY] * nt,
        out_shape=[jax.ShapeDtypeStruct(a.shape, a.dtype) for a in lands],
        input_output_aliases={t: t for t in range(nt)},
        scratch_shapes=[pltpu.SemaphoreType.DMA((nt, 3)), pltpu.SemaphoreType.DMA((nt, 3))],
        compiler_params=_cp())(*lands)


def _chip_partial(owns, recv, chip_arr, tag):
    nt = len(owns)

    def body(chip_ref, *refs):
        k = pl.program_id(0)
        for t in range(nt):
            p = refs[t][...] + refs[nt + t][...].astype(F32)
            refs[2 * nt + t][...] = p.astype(BF16)

            @pl.when(k == chip_ref[0])
            def _():
                refs[3 * nt + t][...] = p

    blk_specs = [pl.BlockSpec((None,) + a.shape[1:], lambda k, chip_ref: (k, 0, 0)) for a in owns]
    own_specs = [pl.BlockSpec(a.shape[1:], lambda k, chip_ref: (0, 0)) for a in owns]
    return pl.pallas_call(
        body, name="rs_chip_partial_" + tag,
        grid_spec=pltpu.PrefetchScalarGridSpec(num_scalar_prefetch=1, grid=(NSH,), in_specs=blk_specs * 2,
                                               out_specs=blk_specs + own_specs),
        out_shape=[jax.ShapeDtypeStruct(a.shape, BF16) for a in owns]
        + [jax.ShapeDtypeStruct(a.shape[1:], F32) for a in owns],
        compiler_params=_cp(1))(chip_arr, *owns, *recv)


def _sum_chips(own, recv, tag, after=()):
    nt = len(own)

    def body(*refs):
        outs = refs[2 * nt + len(after):]
        for t in range(nt):
            r = refs[nt + t]
            outs[t][...] = ((refs[t][...] + r[0].astype(F32)) + r[1].astype(F32)) + r[2].astype(F32)

    vm = pl.BlockSpec(memory_space=pltpu.VMEM)
    return pl.pallas_call(
        body, name="rs_sum_chips_" + tag, in_specs=[vm] * (2 * nt) + [ANY] * len(after), out_specs=[vm] * nt,
        out_shape=[jax.ShapeDtypeStruct(a.shape, F32) for a in own],
        compiler_params=_cp())(*own, *recv, *after)


def _adamw_math(w, g, m, v):
    m = ADAM_B1 * m + (1.0 - ADAM_B1) * g
    v = ADAM_B2 * v + (1.0 - ADAM_B2) * (g * g)
    m_hat = m / (1.0 - ADAM_B1 ** ADAM_STEP)
    v_hat = v / (1.0 - ADAM_B2 ** ADAM_STEP)
    delta = -ADAM_LR * (m_hat / (jnp.sqrt(v_hat) + ADAM_EPS) + ADAM_WD * w)
    return delta, m, v


ADAM_SPLIT = 4


def _adamw_shards(own, sib, ws, ms, vs, c_arr, tag):
    nt = len(own)

    def body(c_ref, *refs):
        hh = pl.program_id(0)
        mine = hh == c_ref[0]
        for t in range(nt):
            g = jnp.where(mine, refs[t][...], refs[nt + t][...])
            delta, m, v = _adamw_math(refs[2 * nt + t][...], g, refs[3 * nt + t][...], refs[4 * nt + t][...])
            refs[5 * nt + t][...] = g
            refs[6 * nt + t][...] = delta
            refs[7 * nt + t][...] = m
            refs[8 * nt + t][...] = v

    def tile(a):
        return (a.shape[0] // ADAM_SPLIT, a.shape[1])

    half_specs = [pl.BlockSpec(tile(a), lambda hh, q, c_ref: (q, 0)) for a in own]
    full_specs = [pl.BlockSpec(tile(a), lambda hh, q, c_ref: (hh * ADAM_SPLIT + q, 0)) for a in own]
    return pl.pallas_call(
        body, name="adamw_shards_" + tag,
        grid_spec=pltpu.PrefetchScalarGridSpec(num_scalar_prefetch=1, grid=(2, ADAM_SPLIT),
                                               in_specs=half_specs * 2 + full_specs * 3, out_specs=full_specs * 4),
        out_shape=[jax.ShapeDtypeStruct(w.shape, F32) for w in ws] * 4,
        compiler_params=_cp(2))(c_arr, *own, *sib, *ws, *ms, *vs)


SMALL_TAIL_ROW = 552


def _small_sum_adamw(gathered, wp, mp, vp):
    rows = wp.shape[0]

    def body(g_ref, w_ref, m_ref, v_ref, *rest):
        outs, res = rest[:-1], rest[-1]
        g = g_ref[0]
        for d in range(1, 8):
            g = g + g_ref[d]
        delta, m, v = _adamw_math(w_ref[...], g, m_ref[...], v_ref[...])
        for kind, val in enumerate((g, delta, m, v)):
            res[kind] = val
            gains, w_pool_o, scale_o, tail_o = outs[7 * kind:7 * kind + 4], *outs[7 * kind + 4:7 * kind + 7]
            for t in range(4):
                for i in range(8):
                    gains[t][:, 128 * i:128 * (i + 1)] = res[kind, pl.ds(8 * t + i, 1), :]
            for grp in range(4):
                w_pool_o[grp] = res[kind, pl.ds(32 + GROUP * grp, GROUP), :]
            for i in range(4):
                scale_o[:, 128 * i:128 * (i + 1)] = res[kind, pl.ds(544 + i, 1), :]
            tail_o[...] = res[kind, pl.ds(SMALL_TAIL_ROW, rows - SMALL_TAIL_ROW), :]

    vm = pl.BlockSpec(memory_space=pltpu.VMEM)
    one_kind = [jax.ShapeDtypeStruct((1, D), F32)] * 4 + [
        jax.ShapeDtypeStruct((4, GROUP, GROUP), F32), jax.ShapeDtypeStruct((1, POOL_W), F32),
        jax.ShapeDtypeStruct((rows - SMALL_TAIL_ROW, 128), F32)]
    return pl.pallas_call(
        body, name="small_sum_adamw", in_specs=[vm] * 4, out_specs=[vm] * 28,
        out_shape=one_kind * 4,
        scratch_shapes=[pltpu.VMEM((4, rows, 128), F32)],
        compiler_params=_cp())(gathered, wp, mp, vp)


def _pack_small(parts):
    rows = []
    for a in parts:
        flat = a.reshape(-1)
        n = flat.shape[0]
        padded = -(-n // 1024) * 1024
        rows.append(jnp.pad(flat, (0, padded - n)).reshape(-1, 128))
    return jnp.concatenate(rows, axis=0)


def kernel(x, g_pre_mix, w_in, w_pool, pool_scale, rel_bias, sinks, w_out, g_post_mix, g_pre_ffn, w_gate, w_up, w_down, g_post_ffn, loss_target, m_g_pre_mix, m_w_in, m_w_pool, m_pool_scale, m_rel_bias, m_sinks, m_w_out, m_g_post_mix, m_g_pre_ffn, m_w_gate, m_w_up, m_w_down, m_g_post_ffn, v_g_pre_mix, v_w_in, v_w_pool, v_pool_scale, v_rel_bias, v_sinks, v_w_out, v_g_post_mix, v_g_pre_ffn, v_w_gate, v_w_up, v_w_down, v_g_post_ffn):
    c = lax.axis_index("c")
    chip = 2 * lax.axis_index("x") + lax.axis_index("y")

    def shards(a_in, a_out, a_gate, a_up, a_down):
        return (a_in[0].T, a_out[0], a_gate[0].T, a_up[0].T, a_down[0])

    c_arr = c.reshape(1).astype(jnp.int32)
    chip_arr = chip.reshape(1).astype(jnp.int32)

    big_w = shards(w_in, w_out, w_gate, w_up, w_down)
    big_bf = [w.astype(BF16) for w in big_w]
    g_ssem, g_rsem, g_srcs, g_lands, _ = _split_start(
        _gather_copies, 15, big_bf, [_own_slot(a) for a in big_bf], None, "gather_start")
    fw = {}

    def w_in_ready(*after):
        fw["first"] = _split_wait(_gather_copies, g_ssem, g_rsem, g_srcs, g_lands, after, "gather_win_wait",
                                  only=(0, 1, 2))
        (fw["win"],) = _gather_finish(fw["first"][5:6])
        return fw["win"].reshape(IN_W, D)

    def w_out_ready(*after):
        first = fw["first"]
        lands = _split_wait(_gather_copies, g_ssem, g_rsem, first[:5], [fw["win"]] + list(first[6:]), after,
                            "gather_rest_wait", only=tuple(range(3, 15)))[5:]
        (wout_all,) = _gather_finish(lands[1:2])
        fw["f"] = _split_start(_forward_copies, 9, [], lands[2:], wout_all, "gather_forward_start")
        return wout_all.reshape(D, D), fw["f"][4]

    def ffn_weights(after):
        ssem, rsem, _, lands, _ = fw["f"]
        return _split_wait(_forward_copies, ssem, rsem, [], lands, [after], "gather_forward_wait")

    rs = {}

    def on_ffn_grads(ffn_g):
        oths = ffn_g[1::2]
        rs["ffn_own"] = ffn_g[0::2]
        rs["x"] = _split_start(_sibling_copies, 3, oths, [lax.empty(a.shape, BF16) for a in oths], ffn_g[0],
                               "rs_exchange_ffn_start")
        return rs["x"][4]

    def on_wout_grads(own, oth, after):
        ssem, rsem, srcs, lands, _ = rs["x"]
        recv_ffn = _split_wait(_sibling_copies, ssem, rsem, srcs, lands, [after], "rs_exchange_ffn_wait")[3:]
        recv_wout = _to_sibling([oth], "rs_exchange_wout")
        outs = _chip_partial([own] + list(rs["ffn_own"]), list(recv_wout) + list(recv_ffn), chip_arr, "early")
        rs["early_own"] = outs[4:]
        rs["s"] = _split_start(_scatter_copies, 12, outs[:4],
                               [lax.empty((3,) + a.shape[1:], BF16) for a in outs[:4]], outs[4], "scatter_early_start")
        return rs["s"][4]

    def on_attn_grads(after):
        ssem, rsem, srcs, lands, _ = rs["s"]
        recv_early = _split_wait(_scatter_copies, ssem, rsem, srcs, lands, after, "scatter_early_wait")[4:]
        finals = _sum_chips(list(rs["early_own"]), list(recv_early), "early")
        rs["sh"] = _split_start(_sibling_copies, 4, finals, [lax.empty(a.shape, F32) for a in finals], finals[0],
                                "rs_share_early_start")
        return rs["sh"][4]

    small = (g_pre_mix, g_post_mix, g_pre_ffn, g_post_ffn, w_pool[0], pool_scale, rel_bias, sinks)
    loss, gx, small_grads, ((win_own, win_oth), _, _) = _local_step(
        x[0], loss_target[0], small, w_in_ready, w_out_ready, ffn_weights, c_arr,
        on_ffn_grads, on_wout_grads, on_attn_grads)

    unused = jnp.zeros((1, 1), F32)
    gp = _pack_small(small_grads + (loss,))
    wp = _pack_small((g_pre_mix, g_post_mix, g_pre_ffn, g_post_ffn, w_pool, pool_scale, rel_bias, sinks, unused))
    mp = _pack_small((m_g_pre_mix, m_g_post_mix, m_g_pre_ffn, m_g_post_ffn, m_w_pool, m_pool_scale, m_rel_bias,
                      m_sinks, unused))
    vp = _pack_small((v_g_pre_mix, v_g_post_mix, v_g_pre_ffn, v_g_post_ffn, v_w_pool, v_pool_scale, v_rel_bias,
                      v_sinks, unused))

    moments = (shards(m_w_in, m_w_out, m_w_gate, m_w_up, m_w_down),
               shards(v_w_in, v_w_out, v_w_gate, v_w_up, v_w_down))
    recv_a = _to_sibling([win_oth], "rs_exchange_win")
    outs = _chip_partial([win_own], recv_a, chip_arr, "win")
    w_ssem, w_rsem, w_srcs, w_lands, w_token = _split_start(
        _scatter_copies, 3, outs[:1], [lax.empty((3,) + outs[0].shape[1:], BF16)], gx, "scatter_win_start")
    slots = lax.dynamic_update_slice(lax.empty((8,) + gp.shape, F32), gp[None], (2 * chip + c, 0, 0))
    p_ssem, p_rsem, p_srcs, p_lands, p_token = _split_start(_peer_copies, 7, [gp], [slots], w_token,
                                                            "small_allgather_start")
    ssem, rsem, srcs, lands, _ = rs["sh"]
    shared = _split_wait(_sibling_copies, ssem, rsem, srcs, lands, [p_token], "rs_share_early_wait")
    adam_e = _adamw_shards(shared[:4], shared[4:], big_w[1:], moments[0][1:], moments[1][1:], c_arr, "early")
    recv_win = _split_wait(_scatter_copies, w_ssem, w_rsem, w_srcs, w_lands, [adam_e[0]], "scatter_win_wait")[1:]
    win_final = _sum_chips([outs[1]], recv_win, "win")
    win_sib = _to_sibling(win_final, "rs_share_win")
    adam_w = _adamw_shards(win_final, win_sib, big_w[:1], moments[0][:1], moments[1][:1], c_arr, "win")
    big_g, big_d, big_m, big_v = [[adam_w[i]] + list(adam_e[4 * i:4 * i + 4]) for i in range(4)]

    gathered = _split_wait(_peer_copies, p_ssem, p_rsem, p_srcs, p_lands, [adam_w[0]], "small_allgather_wait")[1]
    flat = _small_sum_adamw(gathered, wp, mp, vp)
    small_out = [flat[7 * kind:7 * kind + 7] for kind in range(4)]
    total_loss = small_out[0][6][16, 0]

    def ordered(small7, big4):
        sg1, sg2, sg3, sg4, swp, ssc, tail = small7
        swp, srb, ssk = swp[None], tail[0:2].reshape(NBUCKET, NQ), tail[8:9, 0:NQ]
        bwin, bwout, bwg, bwu, bwd = big4[0].T[None], big4[1][None], big4[2].T[None], big4[3].T[None], big4[4][None]
        return [sg1, bwin, swp, ssc, srb, ssk, bwout, sg2, sg3, bwg, bwu, bwd, sg4]

    res = [total_loss, gx[None]]
    for sm, bg in zip(small_out, (big_g, big_d, big_m, big_v)):
        res += ordered(sm, bg)
    return tuple(res)
```

```python
import numpy as np
import jax
import jax.numpy as jnp
from jax import lax
from jax.experimental import pallas as pl
from jax.experimental.pallas import tpu as pltpu

F32 = jnp.float32
BF16 = jnp.bfloat16

D = 1024
POOL_W = 512
GROUP = 128
WINDOWS = (2, 4, 8, 16)
HALO = 128
NQ = 8
BLK = 128
NBUCKET = 32
IN_W = 1280
DFF = 2816
NSH = 4
FS = DFF // NSH
WIN_S = IN_W // NSH
WOUT_S = D // NSH
EPS = 1e-6
NEG = -1e30
SCALE = 0.125

ADAM_LR = 0.001
ADAM_B1 = 0.9
ADAM_B2 = 0.999
ADAM_EPS = 1e-08
ADAM_WD = 0.01
ADAM_STEP = 10

TM = 512
TM_POOL = 256
TM_FFN = 512
TM_FFN_FWD = 1024
FFN_ROWS = 256
VMEM_LIMIT = 60 * 1024 * 1024

MESH_T = pl.DeviceIdType.MESH
ANY = pl.BlockSpec(memory_space=pl.ANY)


def _cp(n_grid=0, **kw):
    sem = ("arbitrary",) * n_grid if n_grid else None
    return pltpu.CompilerParams(dimension_semantics=sem, vmem_limit_bytes=VMEM_LIMIT, **kw)


def _dot(a, b):
    return jnp.dot(a, b, preferred_element_type=F32)


def _dot_nt(a, b):
    return lax.dot_general(a, b, (((1,), (1,)), ((), ())), preferred_element_type=F32)


def _dot_tn(a, b):
    return lax.dot_general(a, b, (((0,), (0,)), ((), ())), preferred_element_type=F32)


def _rms(x):
    r = lax.rsqrt(jnp.mean(x * x, axis=-1, keepdims=True) + EPS)
    return r, x * r


def _rms_bwd(r, n, g, dout):
    dn = dout * g
    dx = r * (dn - n * jnp.mean(dn * n, axis=-1, keepdims=True))
    dg = jnp.sum(dout * n, axis=0, keepdims=True)
    return dx, dg


def _split(x, n):
    parts = []
    r = x
    for _ in range(n):
        p = r.astype(BF16)
        parts.append(p)
        r = r - p.astype(F32)
    return parts


def _band_dot(a, x, n):
    acc = None
    for p in _split(x, n):
        t = _dot(a, p)
        acc = t if acc is None else acc + t
    return acc


def _bucket_table():
    qi = np.arange(BLK)[None, :]
    kj = np.arange(2 * BLK)[:, None]
    dist = qi + BLK - kj
    n = np.maximum(dist, 0)
    nf = np.maximum(n, 1).astype(np.float32)
    large = 16 + (np.log(nf / np.float32(16)) / np.float32(np.log(128 / 16)) * np.float32(16)).astype(np.int32)
    large = np.minimum(large, NBUCKET - 1)
    return np.where(n < 16, n, large).astype(np.int32)


def _prenorm(x, g1):
    s = x.shape[0]
    tm = min(TM, s)

    def body(x_ref, g_ref, h_ref):
        _, n = _rms(x_ref[...])
        h_ref[...] = (n * g_ref[...]).astype(BF16)

    row = pl.BlockSpec((tm, D), lambda i: (i, 0))
    return pl.pallas_call(
        body, name="prenorm", grid=(s // tm,),
        in_specs=[row, pl.BlockSpec((1, D), lambda i: (0, 0))], out_specs=row,
        out_shape=jax.ShapeDtypeStruct((s, D), BF16),
        compiler_params=_cp(1))(x, g1)


def _inproj_fwd(h1, w_in):
    s = h1.shape[0]
    tm = min(TM, s)

    def body(h_ref, w_ref, u_ref, q_ref, k_ref, v_ref):
        proj = _dot_nt(h_ref[...], w_ref[...])
        u_ref[...] = proj[:, :512]
        q_ref[...] = proj[:, 512:1024].astype(BF16)
        k_ref[...] = proj[:, 1024:1152].astype(BF16)
        v_ref[...] = proj[:, 1152:1280].astype(BF16)

    row = lambda w: pl.BlockSpec((tm, w), lambda i: (i, 0))
    return pl.pallas_call(
        body, name="inproj_fwd", grid=(s // tm,),
        in_specs=[row(D), pl.BlockSpec((IN_W, D), lambda i: (0, 0))],
        out_specs=[row(512), row(512), row(128), row(128)],
        out_shape=[jax.ShapeDtypeStruct((s, 512), F32), jax.ShapeDtypeStruct((s, 512), BF16),
                   jax.ShapeDtypeStruct((s, 128), BF16), jax.ShapeDtypeStruct((s, 128), BF16)],
        compiler_params=_cp(1))(h1, w_in)


def _window_sums(ext, w, lag_sign, tm, terms):
    rows = lax.broadcasted_iota(jnp.int32, (HALO, 2 * HALO), 0)
    cols = lax.broadcasted_iota(jnp.int32, (HALO, 2 * HALO), 1)
    d = rows + HALO - cols if lag_sign > 0 else cols - rows
    band = jnp.where((d >= 0) & (d < w), 1.0, 0.0).astype(BF16)
    return jnp.concatenate([_band_dot(band, ext[r:r + 2 * HALO], terms) for r in range(0, tm, HALO)], axis=0)


def _pooled(ext, cur, t0, tm):
    t = t0 + lax.broadcasted_iota(jnp.int32, (tm, GROUP), 0)
    out = []
    for g, w in enumerate(WINDOWS):
        sl = slice(g * GROUP, (g + 1) * GROUP)
        cnt = jnp.minimum(t + 1, w).astype(F32)
        out.append(_window_sums(ext[:, sl], w, 1, tm, 2) / cnt - cur[:, sl])
    return out


def _pool_fwd(u, w_pool, pool_scale):
    s = u.shape[0]
    tm = min(TM_POOL, s)
    hb = tm // HALO

    def body(uc_ref, uh_ref, wp_ref, sc_ref, o_ref):
        i = pl.program_id(0)
        cur = uc_ref[...]
        halo = jnp.where(i > 0, uh_ref[...], 0.0)
        ext = jnp.concatenate([halo, cur], axis=0)
        pooled = _pooled(ext, cur, i * tm, tm)
        for g in range(4):
            sl = slice(g * GROUP, (g + 1) * GROUP)
            mixed = _dot(pooled[g].astype(BF16), wp_ref[g])
            o_ref[:, sl] = (mixed * sc_ref[:, sl]).astype(BF16)

    return pl.pallas_call(
        body, name="pool_fwd", grid=(s // tm,),
        in_specs=[pl.BlockSpec((tm, 512), lambda i: (i, 0)),
                  pl.BlockSpec((HALO, 512), lambda i: (jnp.maximum(i * hb - 1, 0), 0)),
                  pl.BlockSpec((4, GROUP, GROUP), lambda i: (0, 0, 0)),
                  pl.BlockSpec((1, 512), lambda i: (0, 0))],
        out_specs=pl.BlockSpec((tm, 512), lambda i: (i, 0)),
        out_shape=jax.ShapeDtypeStruct((s, 512), BF16),
        compiler_params=_cp(1))(u, u, w_pool, pool_scale)


def _bias_table(bucket, rel_bias):
    def body(b_ref, rb_ref, o_ref):
        bucket_v = b_ref[...]
        key = lax.broadcasted_iota(jnp.int32, (2 * BLK, BLK), 0)
        dist = lax.broadcasted_iota(jnp.int32, (2 * BLK, BLK), 1) + BLK - key
        inwin = (dist >= 0) & (dist < BLK)
        for h in range(NQ):
            acc = jnp.zeros((2 * BLK, BLK), F32)
            for b in range(NBUCKET):
                acc = jnp.where(bucket_v == b, rb_ref[b, h], acc)
            rest = jnp.where(inwin, acc, NEG)
            o_ref[1, h] = rest
            o_ref[0, h] = jnp.where(key >= BLK, rest, NEG)

    return pl.pallas_call(
        body, name="bias_table",
        in_specs=[pl.BlockSpec(memory_space=pltpu.VMEM), pl.BlockSpec(memory_space=pltpu.SMEM)],
        out_specs=pl.BlockSpec(memory_space=pltpu.VMEM),
        out_shape=jax.ShapeDtypeStruct((2, NQ, 2 * BLK, BLK), F32),
        compiler_params=_cp())(bucket, rel_bias)


def _kv_band(ref, n, transposed):
    pstart = pl.multiple_of(jnp.maximum(n - 1, 0) * BLK, BLK)
    cstart = pl.multiple_of(n * BLK, BLK)
    lo = lax.broadcasted_iota(jnp.int32, (2 * BLK, 128), 1) < 64
    band = jnp.concatenate([ref[pl.ds(pstart, BLK), :], ref[pl.ds(cstart, BLK), :]], axis=0).astype(F32)
    rot = pltpu.roll(band, 64, axis=1)
    dup = (jnp.where(lo, band, rot), jnp.where(lo, rot, band))
    plain = [d.astype(BF16) for d in dup]
    if not transposed:
        return plain, None, (lo, pstart, cstart)
    row_lo = lax.broadcasted_iota(jnp.int32, (128, 2 * BLK), 0) < 64
    tr = [[jnp.where(row_lo, d.T, 0.0).astype(BF16), jnp.where(row_lo, 0.0, d.T).astype(BF16)] for d in dup]
    return plain, tr, (lo, pstart, cstart)


def _attn_probs(qm, kk, bias, sink):
    s = _dot_nt(kk, qm) + bias
    m = jnp.maximum(jnp.max(s, axis=0, keepdims=True), sink)
    p = jnp.exp(s - m)
    es = jnp.exp(sink - m)
    inv = 1.0 / (jnp.sum(p, axis=0, keepdims=True) + es)
    return p * inv, es * inv


def _attn_fwd(q, k, v, bias, sinks):
    s = q.shape[0]

    def body(q_ref, k_ref, v_ref, b_ref, sk_ref, o_ref):
        n = pl.program_id(0)
        kk, _, _ = _kv_band(k_ref, n, False)
        _, vt, _ = _kv_band(v_ref, n, True)
        bidx = jnp.minimum(n, 1)
        lo_q = lax.broadcasted_iota(jnp.int32, (BLK, 128), 1) < 64
        for p in range(4):
            h = p // 2
            qt = q_ref[:, p * 128:(p + 1) * 128].astype(F32) * SCALE
            acc_t = jnp.zeros((128, BLK), F32)
            for e in range(2):
                head = 2 * p + e
                qm = jnp.where(lo_q if e == 0 else jnp.logical_not(lo_q), qt, 0.0).astype(BF16)
                prob, _ = _attn_probs(qm, kk[h], b_ref[bidx, head], sk_ref[0, head])
                acc_t = acc_t + _dot(vt[h][e], prob.astype(BF16))
            o_ref[:, p * 128:(p + 1) * 128] = acc_t.T.astype(BF16)

    return pl.pallas_call(
        body, name="attn_fwd", grid=(s // BLK,),
        in_specs=[pl.BlockSpec((BLK, 512), lambda i: (i, 0)),
                  pl.BlockSpec((s, 128), lambda i: (0, 0)),
                  pl.BlockSpec((s, 128), lambda i: (0, 0)),
                  pl.BlockSpec((2, NQ, 2 * BLK, BLK), lambda i: (0, 0, 0, 0)),
                  pl.BlockSpec(memory_space=pltpu.SMEM)],
        out_specs=pl.BlockSpec((BLK, 512), lambda i: (i, 0)),
        out_shape=jax.ShapeDtypeStruct((s, 512), BF16),
        compiler_params=_cp(1))(q, k, v, bias, sinks)


def _outproj_fwd(pool_o, attn_o, w_out, x, g2, g3):
    s = x.shape[0]
    tm = min(TM, s)

    def body(p_ref, a_ref, w_ref, x_ref, g2_ref, g3_ref, mix_ref, x1_ref, h2_ref):
        mix = _dot(p_ref[...], w_ref[0:512, :]) + _dot(a_ref[...], w_ref[512:1024, :])
        mix_ref[...] = mix
        _, n2 = _rms(mix)
        x1 = x_ref[...] + n2 * g2_ref[...]
        x1_ref[...] = x1
        _, n3 = _rms(x1)
        h2_ref[...] = (n3 * g3_ref[...]).astype(BF16)

    row = lambda w: pl.BlockSpec((tm, w), lambda i: (i, 0))
    vec = pl.BlockSpec((1, D), lambda i: (0, 0))
    return pl.pallas_call(
        body, name="outproj_fwd", grid=(s // tm,),
        in_specs=[row(512), row(512), pl.BlockSpec((D, D), lambda i: (0, 0)), row(D), vec, vec],
        out_specs=[row(D), row(D), row(D)],
        out_shape=[jax.ShapeDtypeStruct((s, D), F32), jax.ShapeDtypeStruct((s, D), F32),
                   jax.ShapeDtypeStruct((s, D), BF16)],
        compiler_params=_cp(1))(pool_o, attn_o, w_out, x, g2, g3)


def _silu_parts(g):
    sg = jax.nn.sigmoid(g)
    return sg, g * sg


def _ffn_fwd(h2, wg, wu, wd, x1, target, g4):
    s = h2.shape[0]
    tm = min(TM_FFN_FWD, s)

    def body(h_ref, wg_ref, wu_ref, wd_ref, x1_ref, t_ref, g4_ref,
             gate_ref, up_ref, df_ref, dy_ref, loss_ref, gg4_ref, f_acc):
        i = pl.program_id(0)
        j = pl.program_id(1)
        first = j == 0
        for r in range(0, tm, FFN_ROWS):
            rows = pl.ds(r, min(FFN_ROWS, tm))
            h = h_ref[rows, :]
            gate = _dot_nt(h, wg_ref[...])
            up = _dot_nt(h, wu_ref[...])
            gate_ref[rows, :] = gate.astype(BF16)
            up_ref[rows, :] = up.astype(BF16)
            _, sl = _silu_parts(gate)
            contrib = _dot((sl * up).astype(BF16), wd_ref[...])
            f_acc[rows, :] = jnp.where(first, contrib, f_acc[rows, :] + contrib)

        @pl.when((i == 0) & (j == 0))
        def _():
            loss_ref[...] = jnp.zeros_like(loss_ref)
            gg4_ref[...] = jnp.zeros_like(gg4_ref)

        @pl.when(j == NSH - 1)
        def _():
            r4, n4 = _rms(f_acc[...])
            g4v = g4_ref[...]
            err = x1_ref[...] + n4 * g4v - t_ref[...]
            loss_ref[...] += (0.5 / D) * jnp.sum(err * err).reshape(1, 1)
            dy = err * (1.0 / D)
            dy_ref[...] = dy
            df, dg = _rms_bwd(r4, n4, g4v, dy)
            gg4_ref[...] += dg
            df_ref[...] = df.astype(BF16)

    row = lambda w: pl.BlockSpec((tm, w), lambda i, j: (i, 0))
    sh = lambda r, c: pl.BlockSpec((None, r, c), lambda i, j: (j, 0, 0))
    act = pl.BlockSpec((None, tm, FS), lambda i, j: (j, i, 0))
    vec = pl.BlockSpec((1, D), lambda i, j: (0, 0))
    return pl.pallas_call(
        body, name="ffn_fwd", grid=(s // tm, NSH),
        in_specs=[row(D), sh(FS, D), sh(FS, D), sh(FS, D), row(D), row(D), vec],
        out_specs=[act, act, row(D), row(D), pl.BlockSpec((1, 1), lambda i, j: (0, 0)), vec],
        out_shape=[jax.ShapeDtypeStruct((NSH, s, FS), BF16), jax.ShapeDtypeStruct((NSH, s, FS), BF16),
                   jax.ShapeDtypeStruct((s, D), BF16), jax.ShapeDtypeStruct((s, D), F32),
                   jax.ShapeDtypeStruct((1, 1), F32), jax.ShapeDtypeStruct((1, D), F32)],
        scratch_shapes=[pltpu.VMEM((tm, D), F32)],
        compiler_params=_cp(2))(h2, wg, wu, wd, x1, target, g4)


def _ffn_bwd_act(df, gate, up, wg, wu, wd, dy, x1, mix, g3, g2):
    s = df.shape[0]
    tm = min(TM_FFN, s)

    def body(df_ref, gate_ref, up_ref, wg_ref, wu_ref, wd_ref, dy_ref, x1_ref, mix_ref, g3_ref, g2_ref,
             dgate_ref, dup_ref, dx1_ref, dmix_ref, gg3_ref, gg2_ref, acc):
        i = pl.program_id(0)
        j = pl.program_id(1)
        first = j == 0
        for r in range(0, tm, FFN_ROWS):
            rows = pl.ds(r, min(FFN_ROWS, tm))
            da = _dot_nt(df_ref[rows, :], wd_ref[...])
            g = gate_ref[rows, :].astype(F32)
            u = up_ref[rows, :].astype(F32)
            sg, sl = _silu_parts(g)
            dgate = (da * u * (sg * (1.0 + g * (1.0 - sg)))).astype(BF16)
            dup = (da * sl).astype(BF16)
            dgate_ref[rows, :] = dgate
            dup_ref[rows, :] = dup
            contrib = _dot(dgate, wg_ref[...]) + _dot(dup, wu_ref[...])
            acc[rows, :] = jnp.where(first, contrib, acc[rows, :] + contrib)

        @pl.when((i == 0) & (j == 0))
        def _():
            gg3_ref[...] = jnp.zeros_like(gg3_ref)
            gg2_ref[...] = jnp.zeros_like(gg2_ref)

        @pl.when(j == NSH - 1)
        def _():
            r3, n3 = _rms(x1_ref[...])
            dx1n, dg3 = _rms_bwd(r3, n3, g3_ref[...], acc[...])
            dx1 = dy_ref[...] + dx1n
            dx1_ref[...] = dx1
            gg3_ref[...] += dg3
            r2, n2 = _rms(mix_ref[...])
            dmix, dg2 = _rms_bwd(r2, n2, g2_ref[...], dx1)
            gg2_ref[...] += dg2
            dmix_ref[...] = dmix.astype(BF16)

    row = lambda w: pl.BlockSpec((tm, w), lambda i, j: (i, 0))
    sh = lambda r, c: pl.BlockSpec((None, r, c), lambda i, j: (j, 0, 0))
    act = pl.BlockSpec((None, tm, FS), lambda i, j: (j, i, 0))
    vec = pl.BlockSpec((1, D), lambda i, j: (0, 0))
    return pl.pallas_call(
        body, name="ffn_bwd_act", grid=(s // tm, NSH),
        in_specs=[row(D), act, act, sh(FS, D), sh(FS, D), sh(FS, D), row(D), row(D), row(D), vec, vec],
        out_specs=[act, act, row(D), row(D), vec, vec],
        out_shape=[jax.ShapeDtypeStruct((NSH, s, FS), BF16), jax.ShapeDtypeStruct((NSH, s, FS), BF16),
                   jax.ShapeDtypeStruct((s, D), F32), jax.ShapeDtypeStruct((s, D), BF16),
                   jax.ShapeDtypeStruct((1, D), F32), jax.ShapeDtypeStruct((1, D), F32)],
        scratch_shapes=[pltpu.VMEM((tm, D), F32)],
        compiler_params=_cp(2))(df, gate, up, wg, wu, wd, dy, x1, mix, g3, g2)


SMEM_SPEC = pl.BlockSpec(memory_space=pltpu.SMEM)


def _emit_halves(acc_ref, row0, rows, c, own_ref, oth_ref):
    half = rows // 2
    own_ref[...] = acc_ref[pl.ds(row0 + pl.multiple_of(c * half, 8), half), :]
    oth_ref[...] = acc_ref[pl.ds(row0 + pl.multiple_of((1 - c) * half, 8), half), :].astype(BF16)


def _half_shapes(rows):
    return [jax.ShapeDtypeStruct((NSH, rows // 2, D), F32), jax.ShapeDtypeStruct((NSH, rows // 2, D), BF16)]


def _ffn_bwd_w(h2, gate, up, dgate, dup, df, c_arr):
    s = h2.shape[0]
    tm = min(TM_FFN_FWD, s)
    nt = s // tm

    def body(c_ref, h_ref, gate_ref, up_ref, dgate_ref, dup_ref, df_ref, *rest):
        outs, accs = rest[:6], rest[6:]
        i = pl.program_id(1)
        @pl.when(i == 0)
        def _():
            for acc in accs:
                acc[...] = jnp.zeros_like(acc)

        h = h_ref[...]
        accs[0][...] += _dot_tn(dgate_ref[...], h)
        accs[1][...] += _dot_tn(dup_ref[...], h)
        _, sl = _silu_parts(gate_ref[...].astype(F32))
        a = (sl * up_ref[...].astype(F32)).astype(BF16)
        accs[2][...] += _dot_tn(a, df_ref[...])

        @pl.when(i == nt - 1)
        def _():
            for t, acc in enumerate(accs):
                _emit_halves(acc, 0, FS, c_ref[0], outs[2 * t], outs[2 * t + 1])

    row = lambda w: pl.BlockSpec((tm, w), lambda j, i: (i, 0))
    act = pl.BlockSpec((None, tm, FS), lambda j, i: (j, i, 0))
    half = pl.BlockSpec((None, FS // 2, D), lambda j, i: (j, 0, 0))
    return pl.pallas_call(
        body, name="ffn_bwd_w", grid=(NSH, nt),
        in_specs=[SMEM_SPEC, row(D), act, act, act, act, row(D)],
        out_specs=[half] * 6,
        out_shape=_half_shapes(FS) * 3,
        scratch_shapes=[pltpu.VMEM((FS, D), F32)] * 3,
        compiler_params=_cp(2))(c_arr, h2, gate, up, dgate, dup, df)


def _outproj_bwd(dmix, w_out, pool_o, attn_o, c_arr, dep=None):
    s = dmix.shape[0]
    tm = min(TM, s)
    nt = s // tm

    def body(c_ref, dm_ref, w_ref, p_ref, a_ref, *rest):
        dpool_ref, dattn_ref, own_ref, oth_ref, gw_ref = rest[-5:]
        i = pl.program_id(0)

        @pl.when(i == 0)
        def _():
            gw_ref[...] = jnp.zeros_like(gw_ref)

        dm = dm_ref[...]
        dpool_ref[...] = _dot_nt(dm, w_ref[0:512, :])
        dattn_ref[...] = _dot_nt(dm, w_ref[512:1024, :]).astype(BF16)
        gw_ref[0:512, :] += _dot_tn(p_ref[...], dm)
        gw_ref[512:1024, :] += _dot_tn(a_ref[...], dm)

        @pl.when(i == nt - 1)
        def _():
            for k in range(NSH):
                _emit_halves(gw_ref, k * WOUT_S, WOUT_S, c_ref[0], own_ref.at[k], oth_ref.at[k])

    row = lambda w: pl.BlockSpec((tm, w), lambda i: (i, 0))
    full = pl.BlockSpec((D, D), lambda i: (0, 0))
    half = pl.BlockSpec((NSH, WOUT_S // 2, D), lambda i: (0, 0, 0))
    return pl.pallas_call(
        body, name="outproj_bwd", grid=(nt,),
        in_specs=[SMEM_SPEC, row(D), full, row(512), row(512)] + ([] if dep is None else [ANY]),
        out_specs=[row(512), row(512), half, half],
        out_shape=[jax.ShapeDtypeStruct((s, 512), F32), jax.ShapeDtypeStruct((s, 512), BF16)] + _half_shapes(WOUT_S),
        scratch_shapes=[pltpu.VMEM((D, D), F32)],
        compiler_params=_cp(1))(c_arr, dmix, w_out, pool_o, attn_o, *([] if dep is None else [dep]))


def _pool_bwd(u, dpool, w_pool, pool_scale, dep=None):
    s = u.shape[0]
    tm = min(TM_POOL, s)
    hb = tm // HALO
    nt = s // tm
    last_halo = s // HALO - 1

    def body(uc_ref, uh_ref, dc_ref, dn_ref, wp_ref, sc_ref, *rest):
        du_ref, gwp_ref, gsc_ref = rest[-3:]
        i = pl.program_id(0)

        @pl.when(i == 0)
        def _():
            gwp_ref[...] = jnp.zeros_like(gwp_ref)
            gsc_ref[...] = jnp.zeros_like(gsc_ref)

        cur = uc_ref[...]
        halo = jnp.where(i > 0, uh_ref[...], 0.0)
        pooled = _pooled(jnp.concatenate([halo, cur], axis=0), cur, i * tm, tm)
        dcur = dc_ref[...]
        dnext = jnp.where(i < nt - 1, dn_ref[...], 0.0)
        dext = jnp.concatenate([dcur, dnext], axis=0)
        t_ext = i * tm + lax.broadcasted_iota(jnp.int32, (tm + HALO, GROUP), 0)
        for g, w in enumerate(WINDOWS):
            sl = slice(g * GROUP, (g + 1) * GROUP)
            pb = pooled[g].astype(BF16)
            wp = wp_ref[g]
            mixed = _dot(pb, wp)
            gsc_ref[:, sl] += jnp.sum(dcur[:, sl] * mixed, axis=0, keepdims=True)
            dmixed = (dext[:, sl] * sc_ref[:, sl]).astype(BF16)
            gwp_ref[g] += _dot_tn(pb, dmixed[0:tm])
            dpooled = _dot_nt(dmixed, wp)
            z = dpooled / jnp.minimum(t_ext + 1, w).astype(F32)
            du_ref[:, sl] = (_window_sums(z, w, -1, tm, 2) - dpooled[0:tm]).astype(BF16)

    return pl.pallas_call(
        body, name="pool_bwd", grid=(nt,),
        in_specs=[pl.BlockSpec((tm, 512), lambda i: (i, 0)),
                  pl.BlockSpec((HALO, 512), lambda i: (jnp.maximum(i * hb - 1, 0), 0)),
                  pl.BlockSpec((tm, 512), lambda i: (i, 0)),
                  pl.BlockSpec((HALO, 512), lambda i: (jnp.minimum((i + 1) * hb, last_halo), 0)),
                  pl.BlockSpec((4, GROUP, GROUP), lambda i: (0, 0, 0)),
                  pl.BlockSpec((1, 512), lambda i: (0, 0))] + ([] if dep is None else [ANY]),
        out_specs=[pl.BlockSpec((tm, 512), lambda i: (i, 0)),
                   pl.BlockSpec((4, GROUP, GROUP), lambda i: (0, 0, 0)),
                   pl.BlockSpec((1, 512), lambda i: (0, 0))],
        out_shape=[jax.ShapeDtypeStruct((s, 512), BF16), jax.ShapeDtypeStruct((4, GROUP, GROUP), F32),
                   jax.ShapeDtypeStruct((1, 512), F32)],
        compiler_params=_cp(1))(u, u, dpool, dpool, w_pool, pool_scale, *([] if dep is None else [dep]))


def _attn_bwd(q, k, v, do, bias, sinks, bucket, dep=None):
    s = q.shape[0]
    nblk = s // BLK

    def body(q_ref, do_ref, k_ref, v_ref, b_ref, sk_ref, bk_ref, *rest):
        dq_ref, dk_ref, dv_ref, grb_ref, gsk_ref, dss_ref = rest[-6:]
        n = pl.program_id(0)

        @pl.when(n == 0)
        def _():
            dk_ref[...] = jnp.zeros_like(dk_ref)
            dv_ref[...] = jnp.zeros_like(dv_ref)
            dss_ref[...] = jnp.zeros_like(dss_ref)
            gsk_ref[...] = jnp.zeros_like(gsk_ref)

        kk, kt, (lo, pstart, cstart) = _kv_band(k_ref, n, True)
        vv, _, _ = _kv_band(v_ref, n, False)
        bidx = jnp.minimum(n, 1)
        lo_q = lax.broadcasted_iota(jnp.int32, (BLK, 128), 1) < 64
        dkk = [jnp.zeros((2 * BLK, 128), F32), jnp.zeros((2 * BLK, 128), F32)]
        dvv = [jnp.zeros((2 * BLK, 128), F32), jnp.zeros((2 * BLK, 128), F32)]
        for p in range(4):
            h = p // 2
            qt = q_ref[:, p * 128:(p + 1) * 128].astype(F32) * SCALE
            dot = do_ref[:, p * 128:(p + 1) * 128]
            dq_t = jnp.zeros((128, BLK), F32)
            for e in range(2):
                head = 2 * p + e
                sel_q = lo_q if e == 0 else jnp.logical_not(lo_q)
                qm = jnp.where(sel_q, qt, 0.0).astype(BF16)
                prob, ps = _attn_probs(qm, kk[h], b_ref[bidx, head], sk_ref[0, head])
                dom = jnp.where(sel_q, dot, jnp.zeros_like(dot))
                dp = _dot_nt(vv[h], dom)
                delta = jnp.sum(prob * dp, axis=0, keepdims=True)
                ds = prob * (dp - delta)
                gsk_ref[pl.ds(head, 1), :] += jnp.broadcast_to(-jnp.sum(ps * delta).reshape(1, 1), (1, 128))
                dss_ref[head] += ds
                dsb = ds.astype(BF16)
                dq_t = dq_t + _dot(kt[h][e], dsb)
                dkk[h] = dkk[h] + _dot(dsb, qm)
                dvv[h] = dvv[h] + _dot(prob.astype(BF16), dom)
            dq_ref[:, p * 128:(p + 1) * 128] = (dq_t.T * SCALE).astype(BF16)
        for acc, ref in ((dkk, dk_ref), (dvv, dv_ref)):
            f0 = acc[0] + pltpu.roll(acc[0], 64, axis=1)
            f1 = acc[1] + pltpu.roll(acc[1], 64, axis=1)
            band = jnp.where(lo, f0, f1)
            ref[pl.ds(pstart, BLK), :] += band[0:BLK]
            ref[pl.ds(cstart, BLK), :] += band[BLK:2 * BLK]

        @pl.when(n == nblk - 1)
        def _():
            _relbias_grad(dss_ref, bk_ref[...], grb_ref)

    blk = pl.BlockSpec((BLK, 512), lambda i: (i, 0))
    kv = pl.BlockSpec((s, 128), lambda i: (0, 0))
    row8 = pl.BlockSpec((NQ, 128), lambda i: (0, 0))
    return pl.pallas_call(
        body, name="attn_bwd", grid=(nblk,),
        in_specs=[blk, blk, kv, kv, pl.BlockSpec((2, NQ, 2 * BLK, BLK), lambda i: (0, 0, 0, 0)),
                  pl.BlockSpec(memory_space=pltpu.SMEM), pl.BlockSpec((2 * BLK, BLK), lambda i: (0, 0))]
        + ([] if dep is None else [ANY]),
        out_specs=[blk, kv, kv, row8, row8],
        out_shape=[jax.ShapeDtypeStruct((s, 512), BF16), jax.ShapeDtypeStruct((s, 128), F32),
                   jax.ShapeDtypeStruct((s, 128), F32), jax.ShapeDtypeStruct((NQ, 128), F32),
                   jax.ShapeDtypeStruct((NQ, 128), F32)],
        scratch_shapes=[pltpu.VMEM((NQ, 2 * BLK, BLK), F32)],
        compiler_params=_cp(1))(q, do, k, v, bias, sinks, bucket, *([] if dep is None else [dep]))


def _relbias_grad(ds_ref, bucket_v, o_ref):
    lane = lax.broadcasted_iota(jnp.int32, (NQ, 128), 1)
    head_row = lax.broadcasted_iota(jnp.int32, (NQ, BLK), 0)
    acc = jnp.zeros((NQ, 128), F32)
    for b in range(NBUCKET):
        sel = bucket_v == b
        stack = jnp.zeros((NQ, BLK), F32)
        for h in range(NQ):
            col_sums = jnp.sum(jnp.where(sel, ds_ref[h], 0.0), axis=0, keepdims=True)
            stack = jnp.where(head_row == h, col_sums, stack)
        acc = acc + jnp.where(lane == b, jnp.sum(stack, axis=1, keepdims=True), 0.0)
    o_ref[...] = acc


def _inproj_bwd(x, g1, du, dq, dk, dv, w_in, dx1, c_arr, dep=None):
    s = x.shape[0]
    tm = min(TM, s)
    nt = s // tm
    cols = ((0, 512), (512, 1024), (1024, 1152), (1152, 1280))

    def body(c_ref, x_ref, g_ref, du_ref, dq_ref, dk_ref, dv_ref, w_ref, dx1_ref, *rest):
        gx_ref, own_ref, oth_ref, gg_ref, gw_ref = rest[-5:]
        i = pl.program_id(0)

        @pl.when(i == 0)
        def _():
            gw_ref[...] = jnp.zeros_like(gw_ref)
            gg_ref[...] = jnp.zeros_like(gg_ref)

        gv = g_ref[...]
        r1, n1 = _rms(x_ref[...])
        h = (n1 * gv).astype(BF16)
        parts = (du_ref[...], dq_ref[...], dk_ref[...].astype(BF16), dv_ref[...].astype(BF16))
        dh = None
        for (a, b), dpart in zip(cols, parts):
            t = _dot(dpart, w_ref[a:b, :])
            dh = t if dh is None else dh + t
            gw_ref[a:b, :] += _dot_tn(dpart, h)
        dx, dg = _rms_bwd(r1, n1, gv, dh)
        gg_ref[...] += dg
        gx_ref[...] = dx1_ref[...] + dx

        @pl.when(i == nt - 1)
        def _():
            for k in range(NSH):
                _emit_halves(gw_ref, k * WIN_S, WIN_S, c_ref[0], own_ref.at[k], oth_ref.at[k])

    row = lambda w: pl.BlockSpec((tm, w), lambda i: (i, 0))
    vec = pl.BlockSpec((1, D), lambda i: (0, 0))
    full = pl.BlockSpec((IN_W, D), lambda i: (0, 0))
    half = pl.BlockSpec((NSH, WIN_S // 2, D), lambda i: (0, 0, 0))
    return pl.pallas_call(
        body, name="inproj_bwd", grid=(nt,),
        in_specs=[SMEM_SPEC, row(D), vec, row(512), row(512), row(128), row(128), full, row(D)]
        + ([] if dep is None else [ANY]),
        out_specs=[row(D), half, half, vec],
        out_shape=[jax.ShapeDtypeStruct((s, D), F32)] + _half_shapes(WIN_S) + [jax.ShapeDtypeStruct((1, D), F32)],
        scratch_shapes=[pltpu.VMEM((IN_W, D), F32)],
        compiler_params=_cp(1))(c_arr, x, g1, du, dq, dk, dv, w_in, dx1, *([] if dep is None else [dep]))


def _local_step(x, target, small, w_in, w_out, ffn_weights, c_arr, on_ffn_grads=None, on_wout_grads=None,
                on_attn_grads=None):
    g1, g2, g3, g4, w_pool, pool_scale, rel_bias, sinks = small
    bucket = jnp.asarray(_bucket_table())
    wp_bf = w_pool.astype(BF16)
    bias = _bias_table(bucket, rel_bias)
    h1 = _prenorm(x, g1)
    if callable(w_in):
        w_in = w_in(bias, h1)
    u, q, k, v = _inproj_fwd(h1, w_in)
    pool_o = _pool_fwd(u, wp_bf, pool_scale)
    attn_o = _attn_fwd(q, k, v, bias, sinks)
    g2_fwd = g2
    if callable(w_out):
        w_out, after = w_out(pool_o, attn_o)
        g2_fwd = g2 + after[:1, :1]
    mix, x1, h2 = _outproj_fwd(pool_o, attn_o, w_out, x, g2_fwd, g3)
    wg, wu, wd = ffn_weights(h2) if callable(ffn_weights) else ffn_weights
    gate, up, df, dy, loss, gg4 = _ffn_fwd(h2, wg, wu, wd, x1, target, g4)
    dgate, dup, dx1, dmix, gg3, gg2 = _ffn_bwd_act(df, gate, up, wg, wu, wd, dy, x1, mix, g3, g2)
    ffn_g = _ffn_bwd_w(h2, gate, up, dgate, dup, df, c_arr)
    dep = None if on_ffn_grads is None else on_ffn_grads(ffn_g)
    dpool, dattn, wout_own, wout_oth = _outproj_bwd(dmix, w_out, pool_o, attn_o, c_arr, dep)
    dep = None if on_wout_grads is None else on_wout_grads(wout_own, wout_oth, dpool)
    du, gwp, gsc = _pool_bwd(u, dpool, wp_bf, pool_scale, dep)
    dq, dk, dv, grb, gsk = _attn_bwd(q, k, v, dattn, bias, sinks, bucket, dep)
    dep = None if on_attn_grads is None else on_attn_grads([du, dq])
    gx, win_own, win_oth, gg1 = _inproj_bwd(x, g1, du, dq, dk, dv, w_in, dx1, c_arr, dep)
    small_grads = (gg1, gg2, gg3, gg4, gwp, gsc, grb[:, :NBUCKET].T, gsk[:, 0].reshape(1, NQ))
    return loss, gx, small_grads, ((win_own, win_oth), (wout_own, wout_oth), ffn_g)


def _place():
    x, y, c = lax.axis_index("x"), lax.axis_index("y"), lax.axis_index("c")
    chips = ((1 - x, y), (x, 1 - y), (1 - x, 1 - y))
    return x, y, c, chips


def _remote(src, dst, ssem, rsem, dev):
    return pltpu.make_async_remote_copy(src_ref=src, dst_ref=dst, send_sem=ssem, recv_sem=rsem,
                                        device_id=dev, device_id_type=MESH_T)


def _to_sibling(arrs, name):
    nt = len(arrs)

    def body(*refs):
        srcs, dsts = refs[:nt], refs[nt:2 * nt]
        ssem, rsem = refs[2 * nt:]
        x, y, c, _ = _place()
        cps = [_remote(srcs[t], dsts[t], ssem.at[t], rsem.at[t], (x, y, 1 - c)) for t in range(nt)]
        for cp in cps:
            cp.start()
        for cp in cps:
            cp.wait()

    return pl.pallas_call(
        body, name=name, in_specs=[ANY] * nt, out_specs=[ANY] * nt,
        out_shape=[jax.ShapeDtypeStruct(a.shape, a.dtype) for a in arrs],
        scratch_shapes=[pltpu.SemaphoreType.DMA((nt,)), pltpu.SemaphoreType.DMA((nt,))],
        compiler_params=_cp())(*arrs)


HBM_SPEC = pl.BlockSpec(memory_space=pltpu.HBM)
SEM_SPEC = pl.BlockSpec(memory_space=pltpu.SEMAPHORE)
DATAFLOW = pltpu.SideEffectType.DATAFLOW_SIDE_EFFECTING


def _gather_copies(srcs, lands, ssem, rsem):
    x, y, c, chips = _place()
    me = 2 * x + y
    out = []
    for t in range(len(srcs)):
        half = srcs[t].shape[0] // 2
        mine = pl.ds(pl.multiple_of(c * half, 16), half)
        for j, (px, py) in enumerate(chips):
            k = 3 * t + j
            send = _remote(srcs[t].at[mine], lands[t].at[me, mine], ssem.at[k], rsem.at[k], (px, py, c))
            blk = lands[t].at[2 * px + py, mine]
            out.append((send, _remote(blk, blk, ssem.at[k], rsem.at[k], (px, py, c))))
    return out


def _scatter_copies(srcs, lands, ssem, rsem):
    x, y, c, chips = _place()
    out = []
    for t in range(len(srcs)):
        for j, (px, py) in enumerate(chips):
            k = 3 * t + j
            send = _remote(srcs[t].at[2 * px + py], lands[t].at[j], ssem.at[k], rsem.at[k], (px, py, c))
            out.append((send, _remote(lands[t].at[j], lands[t].at[j], ssem.at[k], rsem.at[k], (px, py, c))))
    return out


def _sibling_copies(srcs, lands, ssem, rsem):
    x, y, c, _ = _place()
    sib = (x, y, 1 - c)
    return [(_remote(srcs[t], lands[t], ssem.at[t], rsem.at[t], sib),
             _remote(lands[t], lands[t], ssem.at[t], rsem.at[t], sib)) for t in range(len(srcs))]


def _forward_copies(srcs, lands, ssem, rsem):
    x, y, c, chips = _place()
    sib = (x, y, 1 - c)
    out = []
    for t in range(len(lands)):
        half = lands[t].shape[1] // 2
        mine = pl.ds(pl.multiple_of(c * half, 16), half)
        other = pl.ds(pl.multiple_of((1 - c) * half, 16), half)
        for j, (px, py) in enumerate(chips):
            k = 3 * t + j
            blk_m, blk_o = lands[t].at[2 * px + py, mine], lands[t].at[2 * px + py, other]
            out.append((_remote(blk_m, blk_m, ssem.at[k], rsem.at[k], sib),
                        _remote(blk_o, blk_o, ssem.at[k], rsem.at[k], sib)))
    return out


def _peer_copies(srcs, lands, ssem, rsem):
    x, y, c, _ = _place()
    me = 4 * x + 2 * y + c
    out = []
    for kk in range(1, 8):
        px, py, pc = x ^ (kk >> 2), y ^ ((kk >> 1) & 1), c ^ (kk & 1)
        send = _remote(srcs[0], lands[0].at[me], ssem.at[kk - 1], rsem.at[kk - 1], (px, py, pc))
        slot = lands[0].at[4 * px + 2 * py + pc]
        out.append((send, _remote(slot, slot, ssem.at[kk - 1], rsem.at[kk - 1], (px, py, pc))))
    return out


def _split_start(plan, ncopy, srcs, lands, after, name):
    ns, nbuf = len(srcs), len(srcs) + len(lands)
    extra = [] if after is None else [after]
    n_in = nbuf + len(extra)

    def body(*refs):
        ssem, rsem, token = refs[n_in], refs[n_in + 1], refs[-1]
        for send, _ in plan(refs[:ns], refs[ns:nbuf], ssem, rsem):
            send.start()
        token[...] = jnp.zeros_like(token)

    bufs = [pltpu.with_memory_space_constraint(a, pltpu.HBM) for a in list(srcs) + list(lands)]
    outs = pl.pallas_call(
        body, name=name, in_specs=[HBM_SPEC] * nbuf + [ANY] * len(extra),
        out_specs=[SEM_SPEC, SEM_SPEC] + [HBM_SPEC] * nbuf + [pl.BlockSpec(memory_space=pltpu.VMEM)],
        out_shape=[pltpu.SemaphoreType.DMA((ncopy,)), pltpu.SemaphoreType.DMA((ncopy,))]
        + [pltpu.HBM(a.shape, a.dtype) for a in bufs] + [jax.ShapeDtypeStruct((8, 128), F32)],
        input_output_aliases={i: 2 + i for i in range(nbuf)},
        compiler_params=pltpu.CompilerParams(has_side_effects=DATAFLOW))(*bufs, *extra)
    return outs[0], outs[1], outs[2:2 + ns], outs[2 + ns:2 + nbuf], outs[-1]


def _split_wait(plan, ssem, rsem, srcs, lands, after, name):
    ns, nbuf = len(srcs), len(srcs) + len(lands)

    def body(*refs):
        s_ref, r_ref = refs[nbuf], refs[nbuf + 1]
        for send, recv in plan(refs[:ns], refs[ns:nbuf], s_ref, r_ref):
            send.wait_send()
            recv.wait_recv()

    outs = pl.pallas_call(
        body, name=name, in_specs=[HBM_SPEC] * nbuf + [SEM_SPEC, SEM_SPEC] + [ANY] * len(after),
        out_specs=[HBM_SPEC] * nbuf,
        out_shape=[pltpu.HBM(a.shape, a.dtype) for a in list(srcs) + list(lands)],
        input_output_aliases={i: i for i in range(nbuf)},
        compiler_params=pltpu.CompilerParams(has_side_effects=DATAFLOW))(*srcs, *lands, ssem, rsem, *after)
    return outs


def _gather_finish(lands):
    nt = len(lands)

    def body(*refs):
        outs = refs[nt:2 * nt]
        dsend, drecv = refs[2 * nt:]
        x, y, c, chips = _place()
        sib = (x, y, 1 - c)
        sends = []
        for t in range(nt):
            half = outs[t].shape[1] // 2
            mine = pl.ds(pl.multiple_of(c * half, 16), half)
            for j, (px, py) in enumerate(chips):
                blk = outs[t].at[2 * px + py, mine]
                cp = _remote(blk, blk, dsend.at[t, j], drecv.at[t, j], sib)
                cp.start()
                sends.append(cp)
        for t in range(nt):
            half = outs[t].shape[1] // 2
            other = pl.ds(pl.multiple_of((1 - c) * half, 16), half)
            for j, (px, py) in enumerate(chips):
                blk = outs[t].at[2 * px + py, other]
                _remote(blk, blk, dsend.at[t, j], drecv.at[t, j], sib).wait_recv()
        for cp in sends:
            cp.wait_send()

    return pl.pallas_call(
        body, name="gather_finish", in_specs=[ANY] * nt, out_specs=[ANY] * nt,
        out_shape=[jax.ShapeDtypeStruct(a.shape, a.dtype) for a in lands],
        input_output_aliases={t: t for t in range(nt)},
        scratch_shapes=[pltpu.SemaphoreType.DMA((nt, 3)), pltpu.SemaphoreType.DMA((nt, 3))],
        compiler_params=_cp())(*lands)


def _chip_partial(owns, recv, chip_arr, tag):
    nt = len(owns)

    def body(chip_ref, *refs):
        k = pl.program_id(0)
        for t in range(nt):
            p = refs[t][...] + refs[nt + t][...].astype(F32)
            refs[2 * nt + t][...] = p.astype(BF16)

            @pl.when(k == chip_ref[0])
            def _():
                refs[3 * nt + t][...] = p

    blk_specs = [pl.BlockSpec((None,) + a.shape[1:], lambda k, chip_ref: (k, 0, 0)) for a in owns]
    own_specs = [pl.BlockSpec(a.shape[1:], lambda k, chip_ref: (0, 0)) for a in owns]
    return pl.pallas_call(
        body, name="rs_chip_partial_" + tag,
        grid_spec=pltpu.PrefetchScalarGridSpec(num_scalar_prefetch=1, grid=(NSH,), in_specs=blk_specs * 2,
                                               out_specs=blk_specs + own_specs),
        out_shape=[jax.ShapeDtypeStruct(a.shape, BF16) for a in owns]
        + [jax.ShapeDtypeStruct(a.shape[1:], F32) for a in owns],
        compiler_params=_cp(1))(chip_arr, *owns, *recv)


def _sum_chips(own, recv, tag, after=()):
    nt = len(own)

    def body(*refs):
        outs = refs[2 * nt + len(after):]
        for t in range(nt):
            r = refs[nt + t]
            outs[t][...] = ((refs[t][...] + r[0].astype(F32)) + r[1].astype(F32)) + r[2].astype(F32)

    vm = pl.BlockSpec(memory_space=pltpu.VMEM)
    return pl.pallas_call(
        body, name="rs_sum_chips_" + tag, in_specs=[vm] * (2 * nt) + [ANY] * len(after), out_specs=[vm] * nt,
        out_shape=[jax.ShapeDtypeStruct(a.shape, F32) for a in own],
        compiler_params=_cp())(*own, *recv, *after)


def _adamw_math(w, g, m, v):
    m = ADAM_B1 * m + (1.0 - ADAM_B1) * g
    v = ADAM_B2 * v + (1.0 - ADAM_B2) * (g * g)
    m_hat = m / (1.0 - ADAM_B1 ** ADAM_STEP)
    v_hat = v / (1.0 - ADAM_B2 ** ADAM_STEP)
    delta = -ADAM_LR * (m_hat / (jnp.sqrt(v_hat) + ADAM_EPS) + ADAM_WD * w)
    return delta, m, v


ADAM_SPLIT = 4


def _adamw_shards(own, sib, ws, ms, vs, c_arr, tag):
    nt = len(own)

    def body(c_ref, *refs):
        hh = pl.program_id(0)
        mine = hh == c_ref[0]
        for t in range(nt):
            g = jnp.where(mine, refs[t][...], refs[nt + t][...])
            delta, m, v = _adamw_math(refs[2 * nt + t][...], g, refs[3 * nt + t][...], refs[4 * nt + t][...])
            refs[5 * nt + t][...] = g
            refs[6 * nt + t][...] = delta
            refs[7 * nt + t][...] = m
            refs[8 * nt + t][...] = v

    def tile(a):
        return (a.shape[0] // ADAM_SPLIT, a.shape[1])

    half_specs = [pl.BlockSpec(tile(a), lambda hh, q, c_ref: (q, 0)) for a in own]
    full_specs = [pl.BlockSpec(tile(a), lambda hh, q, c_ref: (hh * ADAM_SPLIT + q, 0)) for a in own]
    return pl.pallas_call(
        body, name="adamw_shards_" + tag,
        grid_spec=pltpu.PrefetchScalarGridSpec(num_scalar_prefetch=1, grid=(2, ADAM_SPLIT),
                                               in_specs=half_specs * 2 + full_specs * 3, out_specs=full_specs * 4),
        out_shape=[jax.ShapeDtypeStruct(w.shape, F32) for w in ws] * 4,
        compiler_params=_cp(2))(c_arr, *own, *sib, *ws, *ms, *vs)


SMALL_WPOOL_ROW = 32
SMALL_SCALE_ROW = SMALL_WPOOL_ROW + 4 * GROUP
SMALL_BIAS_ROW = SMALL_SCALE_ROW + 8
SMALL_SINKS_ROW = SMALL_BIAS_ROW + NBUCKET
SMALL_LOSS_ROW = SMALL_SINKS_ROW + 8


def _small_sum_adamw(gathered, wp, mp, vp):
    rows = wp.shape[0]

    def body(g_ref, w_ref, m_ref, v_ref, *rest):
        outs, res = rest[:-1], rest[-1]
        g = g_ref[0]
        for d in range(1, 8):
            g = g + g_ref[d]
        delta, m, v = _adamw_math(w_ref[...], g, m_ref[...], v_ref[...])
        for kind, val in enumerate((g, delta, m, v)):
            res[kind] = val
            gains = outs[8 * kind:8 * kind + 4]
            w_pool_o, scale_o, bias_o, sinks_o = outs[8 * kind + 4:8 * kind + 8]
            for t in range(4):
                for i in range(8):
                    gains[t][:, 128 * i:128 * (i + 1)] = res[kind, pl.ds(8 * t + i, 1), :]
            for grp in range(4):
                w_pool_o[grp] = res[kind, pl.ds(SMALL_WPOOL_ROW + GROUP * grp, GROUP), :]
            for i in range(4):
                scale_o[:, 128 * i:128 * (i + 1)] = res[kind, pl.ds(SMALL_SCALE_ROW + i, 1), :]
            bias_o[...] = res[kind, pl.ds(SMALL_BIAS_ROW, NBUCKET), :][:, 0:NQ]
            sinks_o[...] = res[kind, pl.ds(SMALL_SINKS_ROW, 1), :][:, 0:NQ]
        outs[-1][...] = res[0, pl.ds(SMALL_LOSS_ROW, 1), :]

    vm = pl.BlockSpec(memory_space=pltpu.VMEM)
    one_kind = [jax.ShapeDtypeStruct((1, D), F32)] * 4 + [
        jax.ShapeDtypeStruct((4, GROUP, GROUP), F32), jax.ShapeDtypeStruct((1, POOL_W), F32),
        jax.ShapeDtypeStruct((NBUCKET, NQ), F32), jax.ShapeDtypeStruct((1, NQ), F32)]
    return pl.pallas_call(
        body, name="small_sum_adamw", in_specs=[vm] * 4, out_specs=[vm] * 33,
        out_shape=one_kind * 4 + [jax.ShapeDtypeStruct((1, 128), F32)],
        scratch_shapes=[pltpu.VMEM((4, rows, 128), F32)],
        compiler_params=_cp())(gathered, wp, mp, vp)


def _pack_small(gains4, w_pool, pool_scale, rel_bias, sinks, loss):
    parts = list(gains4) + [w_pool, pool_scale, jnp.pad(rel_bias, ((0, 0), (0, 128 - NQ))), sinks, loss]
    rows = []
    for a in parts:
        flat = a.reshape(-1)
        n = flat.shape[0]
        padded = -(-n // 1024) * 1024
        rows.append(jnp.pad(flat, (0, padded - n)).reshape(-1, 128))
    return jnp.concatenate(rows, axis=0)


def kernel(x, g_pre_mix, w_in, w_pool, pool_scale, rel_bias, sinks, w_out, g_post_mix, g_pre_ffn, w_gate, w_up, w_down, g_post_ffn, loss_target, m_g_pre_mix, m_w_in, m_w_pool, m_pool_scale, m_rel_bias, m_sinks, m_w_out, m_g_post_mix, m_g_pre_ffn, m_w_gate, m_w_up, m_w_down, m_g_post_ffn, v_g_pre_mix, v_w_in, v_w_pool, v_pool_scale, v_rel_bias, v_sinks, v_w_out, v_g_post_mix, v_g_pre_ffn, v_w_gate, v_w_up, v_w_down, v_g_post_ffn):
    c = lax.axis_index("c")
    chip = 2 * lax.axis_index("x") + lax.axis_index("y")

    def shards(a_in, a_out, a_gate, a_up, a_down):
        return (a_in[0].T, a_out[0], a_gate[0].T, a_up[0].T, a_down[0])

    c_arr = c.reshape(1).astype(jnp.int32)
    chip_arr = chip.reshape(1).astype(jnp.int32)

    big_w = shards(w_in, w_out, w_gate, w_up, w_down)

    def landing(a):
        return lax.empty((NSH,) + a.shape, a.dtype)

    def with_own(land, shard):
        return lax.dynamic_update_slice(land, shard[None], (chip, 0, 0))

    bf_win = big_w[0].astype(BF16)
    a_ssem, a_rsem, a_srcs, a_lands, a_token = _split_start(_gather_copies, 3, [bf_win], [landing(bf_win)], None,
                                                            "gather_win_start")
    bf_rest = [(w + a_token[:1, :1]).astype(BF16) for w in big_w[1:]]
    b_ssem, b_rsem, b_srcs, b_lands, _ = _split_start(_gather_copies, 12, bf_rest, [landing(a) for a in bf_rest],
                                                      None, "gather_rest_start")
    fw = {}

    def w_in_ready(*after):
        src, land = _split_wait(_gather_copies, a_ssem, a_rsem, a_srcs, a_lands, after, "gather_win_wait")
        (win_all,) = _gather_finish([with_own(land, src)])
        return win_all.reshape(IN_W, D)

    def w_out_ready(*after):
        outs = _split_wait(_gather_copies, b_ssem, b_rsem, b_srcs, b_lands, after, "gather_rest_wait")
        lands = [with_own(land, src) for src, land in zip(outs[:4], outs[4:])]
        (wout_all,) = _gather_finish(lands[:1])
        fw["f"] = _split_start(_forward_copies, 9, [], lands[1:], wout_all, "gather_forward_start")
        return wout_all.reshape(D, D), fw["f"][4]

    def ffn_weights(after):
        ssem, rsem, _, lands, _ = fw["f"]
        return _split_wait(_forward_copies, ssem, rsem, [], lands, [after], "gather_forward_wait")

    rs = {}

    def on_ffn_grads(ffn_g):
        oths = ffn_g[1::2]
        rs["ffn_own"] = ffn_g[0::2]
        rs["x"] = _split_start(_sibling_copies, 3, oths, [lax.empty(a.shape, BF16) for a in oths], ffn_g[0],
                               "rs_exchange_ffn_start")
        return rs["x"][4]

    def on_wout_grads(own, oth, after):
        ssem, rsem, srcs, lands, _ = rs["x"]
        recv_ffn = _split_wait(_sibling_copies, ssem, rsem, srcs, lands, [after], "rs_exchange_ffn_wait")[3:]
        recv_wout = _to_sibling([oth], "rs_exchange_wout")
        outs = _chip_partial([own] + list(rs["ffn_own"]), list(recv_wout) + list(recv_ffn), chip_arr, "early")
        rs["early_own"] = outs[4:]
        rs["s"] = _split_start(_scatter_copies, 12, outs[:4],
                               [lax.empty((3,) + a.shape[1:], BF16) for a in outs[:4]], outs[4], "scatter_early_start")
        return rs["s"][4]

    def on_attn_grads(after):
        ssem, rsem, srcs, lands, _ = rs["s"]
        recv_early = _split_wait(_scatter_copies, ssem, rsem, srcs, lands, after, "scatter_early_wait")[4:]
        finals = _sum_chips(list(rs["early_own"]), list(recv_early), "early")
        rs["sh"] = _split_start(_sibling_copies, 4, finals, [lax.empty(a.shape, F32) for a in finals], finals[0],
                                "rs_share_early_start")
        return rs["sh"][4]

    small = (g_pre_mix, g_post_mix, g_pre_ffn, g_post_ffn, w_pool[0], pool_scale, rel_bias, sinks)
    loss, gx, small_grads, ((win_own, win_oth), _, _) = _local_step(
        x[0], loss_target[0], small, w_in_ready, w_out_ready, ffn_weights, c_arr,
        on_ffn_grads, on_wout_grads, on_attn_grads)

    unused = jnp.zeros((1, 1), F32)
    gp = _pack_small(small_grads[:4], *small_grads[4:], loss)
    wp = _pack_small((g_pre_mix, g_post_mix, g_pre_ffn, g_post_ffn), w_pool, pool_scale, rel_bias, sinks, unused)
    mp = _pack_small((m_g_pre_mix, m_g_post_mix, m_g_pre_ffn, m_g_post_ffn), m_w_pool, m_pool_scale, m_rel_bias,
                     m_sinks, unused)
    vp = _pack_small((v_g_pre_mix, v_g_post_mix, v_g_pre_ffn, v_g_post_ffn), v_w_pool, v_pool_scale, v_rel_bias,
                     v_sinks, unused)

    moments = (shards(m_w_in, m_w_out, m_w_gate, m_w_up, m_w_down),
               shards(v_w_in, v_w_out, v_w_gate, v_w_up, v_w_down))
    recv_a = _to_sibling([win_oth], "rs_exchange_win")
    outs = _chip_partial([win_own], recv_a, chip_arr, "win")
    w_ssem, w_rsem, w_srcs, w_lands, w_token = _split_start(
        _scatter_copies, 3, outs[:1], [lax.empty((3,) + outs[0].shape[1:], BF16)], gx, "scatter_win_start")
    slots = lax.dynamic_update_slice(lax.empty((8,) + gp.shape, F32), gp[None], (2 * chip + c, 0, 0))
    p_ssem, p_rsem, p_srcs, p_lands, p_token = _split_start(_peer_copies, 7, [gp], [slots], w_token,
                                                            "small_allgather_start")
    ssem, rsem, srcs, lands, _ = rs["sh"]
    shared = _split_wait(_sibling_copies, ssem, rsem, srcs, lands, [p_token], "rs_share_early_wait")
    adam_e = _adamw_shards(shared[:4], shared[4:], big_w[1:], moments[0][1:], moments[1][1:], c_arr, "early")
    recv_win = _split_wait(_scatter_copies, w_ssem, w_rsem, w_srcs, w_lands, [adam_e[0]], "scatter_win_wait")[1:]
    win_final = _sum_chips([outs[1]], recv_win, "win")
    win_sib = _to_sibling(win_final, "rs_share_win")
    adam_w = _adamw_shards(win_final, win_sib, big_w[:1], moments[0][:1], moments[1][:1], c_arr, "win")
    big_g, big_d, big_m, big_v = [[adam_w[i]] + list(adam_e[4 * i:4 * i + 4]) for i in range(4)]

    gathered = _split_wait(_peer_copies, p_ssem, p_rsem, p_srcs, p_lands, [adam_w[0]], "small_allgather_wait")[1]
    flat = _small_sum_adamw(gathered, wp, mp, vp)
    small_out = [flat[8 * kind:8 * kind + 8] for kind in range(4)]
    total_loss = flat[-1][0, 0]

    def ordered(small8, big4):
        sg1, sg2, sg3, sg4, swp, ssc, srb, ssk = small8
        swp = swp[None]
        bwin, bwout, bwg, bwu, bwd = big4[0].T[None], big4[1][None], big4[2].T[None], big4[3].T[None], big4[4][None]
        return [sg1, bwin, swp, ssc, srb, ssk, bwout, sg2, sg3, bwg, bwu, bwd, sg4]

    res = [total_loss, gx[None]]
    for sm, bg in zip(small_out, (big_g, big_d, big_m, big_v)):
        res += ordered(sm, bg)
    return tuple(res)
```

```python
import numpy as np
import jax
import jax.numpy as jnp
from jax import lax
from jax.experimental import pallas as pl
from jax.experimental.pallas import tpu as pltpu

F32 = jnp.float32
BF16 = jnp.bfloat16

D = 1024
POOL_W = 512
GROUP = 128
WINDOWS = (2, 4, 8, 16)
HALO = 128
NQ = 8
BLK = 128
NBUCKET = 32
IN_W = 1280
DFF = 2816
NSH = 4
FS = DFF // NSH
WIN_S = IN_W // NSH
WOUT_S = D // NSH
EPS = 1e-6
NEG = -1e30
SCALE = 0.125

ADAM_LR = 0.001
ADAM_B1 = 0.9
ADAM_B2 = 0.999
ADAM_EPS = 1e-08
ADAM_WD = 0.01
ADAM_STEP = 10

TM = 512
TM_POOL = 256
TM_FFN = 512
TM_FFN_FWD = 1024
FFN_ROWS = 256
VMEM_LIMIT = 60 * 1024 * 1024

MESH_T = pl.DeviceIdType.MESH
ANY = pl.BlockSpec(memory_space=pl.ANY)


def _cp(n_grid=0, **kw):
    sem = ("arbitrary",) * n_grid if n_grid else None
    return pltpu.CompilerParams(dimension_semantics=sem, vmem_limit_bytes=VMEM_LIMIT, **kw)


def _dot(a, b):
    return jnp.dot(a, b, preferred_element_type=F32)


def _dot_nt(a, b):
    return lax.dot_general(a, b, (((1,), (1,)), ((), ())), preferred_element_type=F32)


def _dot_tn(a, b):
    return lax.dot_general(a, b, (((0,), (0,)), ((), ())), preferred_element_type=F32)


def _rms(x):
    r = lax.rsqrt(jnp.mean(x * x, axis=-1, keepdims=True) + EPS)
    return r, x * r


def _rms_bwd(r, n, g, dout):
    dn = dout * g
    dx = r * (dn - n * jnp.mean(dn * n, axis=-1, keepdims=True))
    dg = jnp.sum(dout * n, axis=0, keepdims=True)
    return dx, dg


def _split(x, n):
    parts = []
    r = x
    for _ in range(n):
        p = r.astype(BF16)
        parts.append(p)
        r = r - p.astype(F32)
    return parts


def _band_dot(a, x, n):
    acc = None
    for p in _split(x, n):
        t = _dot(a, p)
        acc = t if acc is None else acc + t
    return acc


def _bucket_table():
    qi = np.arange(BLK)[None, :]
    kj = np.arange(2 * BLK)[:, None]
    dist = qi + BLK - kj
    n = np.maximum(dist, 0)
    nf = np.maximum(n, 1).astype(np.float32)
    large = 16 + (np.log(nf / np.float32(16)) / np.float32(np.log(128 / 16)) * np.float32(16)).astype(np.int32)
    large = np.minimum(large, NBUCKET - 1)
    return np.where(n < 16, n, large).astype(np.int32)


def _prenorm(x, g1):
    s = x.shape[0]
    tm = min(TM, s)

    def body(x_ref, g_ref, h_ref):
        _, n = _rms(x_ref[...])
        h_ref[...] = (n * g_ref[...]).astype(BF16)

    row = pl.BlockSpec((tm, D), lambda i: (i, 0))
    return pl.pallas_call(
        body, name="prenorm", grid=(s // tm,),
        in_specs=[row, pl.BlockSpec((1, D), lambda i: (0, 0))], out_specs=row,
        out_shape=jax.ShapeDtypeStruct((s, D), BF16),
        compiler_params=_cp(1))(x, g1)


def _inproj_fwd(h1, w_in):
    s = h1.shape[0]
    tm = min(TM, s)

    def body(h_ref, w_ref, u_ref, q_ref, k_ref, v_ref):
        proj = _dot_nt(h_ref[...], w_ref[...])
        u_ref[...] = proj[:, :512]
        q_ref[...] = proj[:, 512:1024].astype(BF16)
        k_ref[...] = proj[:, 1024:1152].astype(BF16)
        v_ref[...] = proj[:, 1152:1280].astype(BF16)

    row = lambda w: pl.BlockSpec((tm, w), lambda i: (i, 0))
    return pl.pallas_call(
        body, name="inproj_fwd", grid=(s // tm,),
        in_specs=[row(D), pl.BlockSpec((IN_W, D), lambda i: (0, 0))],
        out_specs=[row(512), row(512), row(128), row(128)],
        out_shape=[jax.ShapeDtypeStruct((s, 512), F32), jax.ShapeDtypeStruct((s, 512), BF16),
                   jax.ShapeDtypeStruct((s, 128), BF16), jax.ShapeDtypeStruct((s, 128), BF16)],
        compiler_params=_cp(1))(h1, w_in)


def _window_sums(ext, w, lag_sign, tm, terms):
    rows = lax.broadcasted_iota(jnp.int32, (HALO, 2 * HALO), 0)
    cols = lax.broadcasted_iota(jnp.int32, (HALO, 2 * HALO), 1)
    d = rows + HALO - cols if lag_sign > 0 else cols - rows
    band = jnp.where((d >= 0) & (d < w), 1.0, 0.0).astype(BF16)
    return jnp.concatenate([_band_dot(band, ext[r:r + 2 * HALO], terms) for r in range(0, tm, HALO)], axis=0)


def _pooled(ext, cur, t0, tm):
    t = t0 + lax.broadcasted_iota(jnp.int32, (tm, GROUP), 0)
    out = []
    for g, w in enumerate(WINDOWS):
        sl = slice(g * GROUP, (g + 1) * GROUP)
        cnt = jnp.minimum(t + 1, w).astype(F32)
        out.append(_window_sums(ext[:, sl], w, 1, tm, 2) / cnt - cur[:, sl])
    return out


def _pool_fwd(u, w_pool, pool_scale):
    s = u.shape[0]
    tm = min(TM_POOL, s)
    hb = tm // HALO

    def body(uc_ref, uh_ref, wp_ref, sc_ref, o_ref):
        i = pl.program_id(0)
        cur = uc_ref[...]
        halo = jnp.where(i > 0, uh_ref[...], 0.0)
        ext = jnp.concatenate([halo, cur], axis=0)
        pooled = _pooled(ext, cur, i * tm, tm)
        for g in range(4):
            sl = slice(g * GROUP, (g + 1) * GROUP)
            mixed = _dot(pooled[g].astype(BF16), wp_ref[g])
            o_ref[:, sl] = (mixed * sc_ref[:, sl]).astype(BF16)

    return pl.pallas_call(
        body, name="pool_fwd", grid=(s // tm,),
        in_specs=[pl.BlockSpec((tm, 512), lambda i: (i, 0)),
                  pl.BlockSpec((HALO, 512), lambda i: (jnp.maximum(i * hb - 1, 0), 0)),
                  pl.BlockSpec((4, GROUP, GROUP), lambda i: (0, 0, 0)),
                  pl.BlockSpec((1, 512), lambda i: (0, 0))],
        out_specs=pl.BlockSpec((tm, 512), lambda i: (i, 0)),
        out_shape=jax.ShapeDtypeStruct((s, 512), BF16),
        compiler_params=_cp(1))(u, u, w_pool, pool_scale)


def _bias_table(bucket, rel_bias):
    def body(b_ref, rb_ref, o_ref):
        bucket_v = b_ref[...]
        key = lax.broadcasted_iota(jnp.int32, (2 * BLK, BLK), 0)
        dist = lax.broadcasted_iota(jnp.int32, (2 * BLK, BLK), 1) + BLK - key
        inwin = (dist >= 0) & (dist < BLK)
        for h in range(NQ):
            acc = jnp.zeros((2 * BLK, BLK), F32)
            for b in range(NBUCKET):
                acc = jnp.where(bucket_v == b, rb_ref[b, h], acc)
            rest = jnp.where(inwin, acc, NEG)
            o_ref[1, h] = rest
            o_ref[0, h] = jnp.where(key >= BLK, rest, NEG)

    return pl.pallas_call(
        body, name="bias_table",
        in_specs=[pl.BlockSpec(memory_space=pltpu.VMEM), pl.BlockSpec(memory_space=pltpu.SMEM)],
        out_specs=pl.BlockSpec(memory_space=pltpu.VMEM),
        out_shape=jax.ShapeDtypeStruct((2, NQ, 2 * BLK, BLK), F32),
        compiler_params=_cp())(bucket, rel_bias)


def _kv_band(ref, n, transposed):
    pstart = pl.multiple_of(jnp.maximum(n - 1, 0) * BLK, BLK)
    cstart = pl.multiple_of(n * BLK, BLK)
    lo = lax.broadcasted_iota(jnp.int32, (2 * BLK, 128), 1) < 64
    band = jnp.concatenate([ref[pl.ds(pstart, BLK), :], ref[pl.ds(cstart, BLK), :]], axis=0).astype(F32)
    rot = pltpu.roll(band, 64, axis=1)
    dup = (jnp.where(lo, band, rot), jnp.where(lo, rot, band))
    plain = [d.astype(BF16) for d in dup]
    if not transposed:
        return plain, None, (lo, pstart, cstart)
    row_lo = lax.broadcasted_iota(jnp.int32, (128, 2 * BLK), 0) < 64
    tr = [[jnp.where(row_lo, d.T, 0.0).astype(BF16), jnp.where(row_lo, 0.0, d.T).astype(BF16)] for d in dup]
    return plain, tr, (lo, pstart, cstart)


def _attn_probs(qm, kk, bias, sink):
    s = _dot_nt(kk, qm) + bias
    m = jnp.maximum(jnp.max(s, axis=0, keepdims=True), sink)
    p = jnp.exp(s - m)
    es = jnp.exp(sink - m)
    inv = 1.0 / (jnp.sum(p, axis=0, keepdims=True) + es)
    return p * inv, es * inv


def _attn_fwd(q, k, v, bias, sinks):
    s = q.shape[0]

    def body(q_ref, k_ref, v_ref, b_ref, sk_ref, o_ref):
        n = pl.program_id(0)
        kk, _, _ = _kv_band(k_ref, n, False)
        _, vt, _ = _kv_band(v_ref, n, True)
        bidx = jnp.minimum(n, 1)
        lo_q = lax.broadcasted_iota(jnp.int32, (BLK, 128), 1) < 64
        for p in range(4):
            h = p // 2
            qt = q_ref[:, p * 128:(p + 1) * 128].astype(F32) * SCALE
            acc_t = jnp.zeros((128, BLK), F32)
            for e in range(2):
                head = 2 * p + e
                qm = jnp.where(lo_q if e == 0 else jnp.logical_not(lo_q), qt, 0.0).astype(BF16)
                prob, _ = _attn_probs(qm, kk[h], b_ref[bidx, head], sk_ref[0, head])
                acc_t = acc_t + _dot(vt[h][e], prob.astype(BF16))
            o_ref[:, p * 128:(p + 1) * 128] = acc_t.T.astype(BF16)

    return pl.pallas_call(
        body, name="attn_fwd", grid=(s // BLK,),
        in_specs=[pl.BlockSpec((BLK, 512), lambda i: (i, 0)),
                  pl.BlockSpec((s, 128), lambda i: (0, 0)),
                  pl.BlockSpec((s, 128), lambda i: (0, 0)),
                  pl.BlockSpec((2, NQ, 2 * BLK, BLK), lambda i: (0, 0, 0, 0)),
                  pl.BlockSpec(memory_space=pltpu.SMEM)],
        out_specs=pl.BlockSpec((BLK, 512), lambda i: (i, 0)),
        out_shape=jax.ShapeDtypeStruct((s, 512), BF16),
        compiler_params=_cp(1))(q, k, v, bias, sinks)


def _outproj_fwd(pool_o, attn_o, w_out, x, g2, g3):
    s = x.shape[0]
    tm = min(TM, s)

    def body(p_ref, a_ref, w_ref, x_ref, g2_ref, g3_ref, mix_ref, x1_ref, h2_ref):
        mix = _dot(p_ref[...], w_ref[0:512, :]) + _dot(a_ref[...], w_ref[512:1024, :])
        mix_ref[...] = mix
        _, n2 = _rms(mix)
        x1 = x_ref[...] + n2 * g2_ref[...]
        x1_ref[...] = x1
        _, n3 = _rms(x1)
        h2_ref[...] = (n3 * g3_ref[...]).astype(BF16)

    row = lambda w: pl.BlockSpec((tm, w), lambda i: (i, 0))
    vec = pl.BlockSpec((1, D), lambda i: (0, 0))
    return pl.pallas_call(
        body, name="outproj_fwd", grid=(s // tm,),
        in_specs=[row(512), row(512), pl.BlockSpec((D, D), lambda i: (0, 0)), row(D), vec, vec],
        out_specs=[row(D), row(D), row(D)],
        out_shape=[jax.ShapeDtypeStruct((s, D), F32), jax.ShapeDtypeStruct((s, D), F32),
                   jax.ShapeDtypeStruct((s, D), BF16)],
        compiler_params=_cp(1))(pool_o, attn_o, w_out, x, g2, g3)


def _silu_parts(g):
    sg = jax.nn.sigmoid(g)
    return sg, g * sg


def _ffn_fwd(h2, wg, wu, wd, x1, target, g4):
    s = h2.shape[0]
    tm = min(TM_FFN_FWD, s)

    def body(h_ref, wg_ref, wu_ref, wd_ref, x1_ref, t_ref, g4_ref,
             gate_ref, up_ref, df_ref, dy_ref, loss_ref, gg4_ref, f_acc):
        i = pl.program_id(0)
        j = pl.program_id(1)
        first = j == 0
        for r in range(0, tm, FFN_ROWS):
            rows = pl.ds(r, min(FFN_ROWS, tm))
            h = h_ref[rows, :]
            gate = _dot_nt(h, wg_ref[...])
            up = _dot_nt(h, wu_ref[...])
            gate_ref[rows, :] = gate.astype(BF16)
            up_ref[rows, :] = up.astype(BF16)
            _, sl = _silu_parts(gate)
            contrib = _dot((sl * up).astype(BF16), wd_ref[...])
            f_acc[rows, :] = jnp.where(first, contrib, f_acc[rows, :] + contrib)

        @pl.when((i == 0) & (j == 0))
        def _():
            loss_ref[...] = jnp.zeros_like(loss_ref)
            gg4_ref[...] = jnp.zeros_like(gg4_ref)

        @pl.when(j == NSH - 1)
        def _():
            r4, n4 = _rms(f_acc[...])
            g4v = g4_ref[...]
            err = x1_ref[...] + n4 * g4v - t_ref[...]
            loss_ref[...] += (0.5 / D) * jnp.sum(err * err).reshape(1, 1)
            dy = err * (1.0 / D)
            dy_ref[...] = dy
            df, dg = _rms_bwd(r4, n4, g4v, dy)
            gg4_ref[...] += dg
            df_ref[...] = df.astype(BF16)

    row = lambda w: pl.BlockSpec((tm, w), lambda i, j: (i, 0))
    sh = lambda r, c: pl.BlockSpec((None, r, c), lambda i, j: (j, 0, 0))
    act = pl.BlockSpec((None, tm, FS), lambda i, j: (j, i, 0))
    vec = pl.BlockSpec((1, D), lambda i, j: (0, 0))
    return pl.pallas_call(
        body, name="ffn_fwd", grid=(s // tm, NSH),
        in_specs=[row(D), sh(FS, D), sh(FS, D), sh(FS, D), row(D), row(D), vec],
        out_specs=[act, act, row(D), row(D), pl.BlockSpec((1, 1), lambda i, j: (0, 0)), vec],
        out_shape=[jax.ShapeDtypeStruct((NSH, s, FS), BF16), jax.ShapeDtypeStruct((NSH, s, FS), BF16),
                   jax.ShapeDtypeStruct((s, D), BF16), jax.ShapeDtypeStruct((s, D), F32),
                   jax.ShapeDtypeStruct((1, 1), F32), jax.ShapeDtypeStruct((1, D), F32)],
        scratch_shapes=[pltpu.VMEM((tm, D), F32)],
        compiler_params=_cp(2))(h2, wg, wu, wd, x1, target, g4)


def _ffn_bwd_act(df, gate, up, wg, wu, wd, dy, x1, mix, g3, g2):
    s = df.shape[0]
    tm = min(TM_FFN, s)

    def body(df_ref, gate_ref, up_ref, wg_ref, wu_ref, wd_ref, dy_ref, x1_ref, mix_ref, g3_ref, g2_ref,
             dgate_ref, dup_ref, dx1_ref, dmix_ref, gg3_ref, gg2_ref, acc):
        i = pl.program_id(0)
        j = pl.program_id(1)
        first = j == 0
        for r in range(0, tm, FFN_ROWS):
            rows = pl.ds(r, min(FFN_ROWS, tm))
            da = _dot_nt(df_ref[rows, :], wd_ref[...])
            g = gate_ref[rows, :].astype(F32)
            u = up_ref[rows, :].astype(F32)
            sg, sl = _silu_parts(g)
            dgate = (da * u * (sg * (1.0 + g * (1.0 - sg)))).astype(BF16)
            dup = (da * sl).astype(BF16)
            dgate_ref[rows, :] = dgate
            dup_ref[rows, :] = dup
            contrib = _dot(dgate, wg_ref[...]) + _dot(dup, wu_ref[...])
            acc[rows, :] = jnp.where(first, contrib, acc[rows, :] + contrib)

        @pl.when((i == 0) & (j == 0))
        def _():
            gg3_ref[...] = jnp.zeros_like(gg3_ref)
            gg2_ref[...] = jnp.zeros_like(gg2_ref)

        @pl.when(j == NSH - 1)
        def _():
            r3, n3 = _rms(x1_ref[...])
            dx1n, dg3 = _rms_bwd(r3, n3, g3_ref[...], acc[...])
            dx1 = dy_ref[...] + dx1n
            dx1_ref[...] = dx1
            gg3_ref[...] += dg3
            r2, n2 = _rms(mix_ref[...])
            dmix, dg2 = _rms_bwd(r2, n2, g2_ref[...], dx1)
            gg2_ref[...] += dg2
            dmix_ref[...] = dmix.astype(BF16)

    row = lambda w: pl.BlockSpec((tm, w), lambda i, j: (i, 0))
    sh = lambda r, c: pl.BlockSpec((None, r, c), lambda i, j: (j, 0, 0))
    act = pl.BlockSpec((None, tm, FS), lambda i, j: (j, i, 0))
    vec = pl.BlockSpec((1, D), lambda i, j: (0, 0))
    return pl.pallas_call(
        body, name="ffn_bwd_act", grid=(s // tm, NSH),
        in_specs=[row(D), act, act, sh(FS, D), sh(FS, D), sh(FS, D), row(D), row(D), row(D), vec, vec],
        out_specs=[act, act, row(D), row(D), vec, vec],
        out_shape=[jax.ShapeDtypeStruct((NSH, s, FS), BF16), jax.ShapeDtypeStruct((NSH, s, FS), BF16),
                   jax.ShapeDtypeStruct((s, D), F32), jax.ShapeDtypeStruct((s, D), BF16),
                   jax.ShapeDtypeStruct((1, D), F32), jax.ShapeDtypeStruct((1, D), F32)],
        scratch_shapes=[pltpu.VMEM((tm, D), F32)],
        compiler_params=_cp(2))(df, gate, up, wg, wu, wd, dy, x1, mix, g3, g2)


SMEM_SPEC = pl.BlockSpec(memory_space=pltpu.SMEM)


def _emit_halves(acc_ref, row0, rows, c, own_ref, oth_ref):
    half = rows // 2
    own_ref[...] = acc_ref[pl.ds(row0 + pl.multiple_of(c * half, 8), half), :]
    oth_ref[...] = acc_ref[pl.ds(row0 + pl.multiple_of((1 - c) * half, 8), half), :].astype(BF16)


def _half_shapes(rows):
    return [jax.ShapeDtypeStruct((NSH, rows // 2, D), F32), jax.ShapeDtypeStruct((NSH, rows // 2, D), BF16)]


def _ffn_bwd_w(h2, gate, up, dgate, dup, df, c_arr):
    s = h2.shape[0]
    tm = min(TM_FFN_FWD, s)
    nt = s // tm

    def body(c_ref, h_ref, gate_ref, up_ref, dgate_ref, dup_ref, df_ref, *rest):
        outs, accs = rest[:6], rest[6:]
        i = pl.program_id(1)
        @pl.when(i == 0)
        def _():
            for acc in accs:
                acc[...] = jnp.zeros_like(acc)

        h = h_ref[...]
        accs[0][...] += _dot_tn(dgate_ref[...], h)
        accs[1][...] += _dot_tn(dup_ref[...], h)
        _, sl = _silu_parts(gate_ref[...].astype(F32))
        a = (sl * up_ref[...].astype(F32)).astype(BF16)
        accs[2][...] += _dot_tn(a, df_ref[...])

        @pl.when(i == nt - 1)
        def _():
            for t, acc in enumerate(accs):
                _emit_halves(acc, 0, FS, c_ref[0], outs[2 * t], outs[2 * t + 1])

    row = lambda w: pl.BlockSpec((tm, w), lambda j, i: (i, 0))
    act = pl.BlockSpec((None, tm, FS), lambda j, i: (j, i, 0))
    half = pl.BlockSpec((None, FS // 2, D), lambda j, i: (j, 0, 0))
    return pl.pallas_call(
        body, name="ffn_bwd_w", grid=(NSH, nt),
        in_specs=[SMEM_SPEC, row(D), act, act, act, act, row(D)],
        out_specs=[half] * 6,
        out_shape=_half_shapes(FS) * 3,
        scratch_shapes=[pltpu.VMEM((FS, D), F32)] * 3,
        compiler_params=_cp(2))(c_arr, h2, gate, up, dgate, dup, df)


def _outproj_bwd(dmix, w_out, pool_o, attn_o, c_arr, dep=None):
    s = dmix.shape[0]
    tm = min(TM, s)
    nt = s // tm

    def body(c_ref, dm_ref, w_ref, p_ref, a_ref, *rest):
        dpool_ref, dattn_ref, own_ref, oth_ref, gw_ref = rest[-5:]
        i = pl.program_id(0)

        @pl.when(i == 0)
        def _():
            gw_ref[...] = jnp.zeros_like(gw_ref)

        dm = dm_ref[...]
        dpool_ref[...] = _dot_nt(dm, w_ref[0:512, :])
        dattn_ref[...] = _dot_nt(dm, w_ref[512:1024, :]).astype(BF16)
        gw_ref[0:512, :] += _dot_tn(p_ref[...], dm)
        gw_ref[512:1024, :] += _dot_tn(a_ref[...], dm)

        @pl.when(i == nt - 1)
        def _():
            for k in range(NSH):
                _emit_halves(gw_ref, k * WOUT_S, WOUT_S, c_ref[0], own_ref.at[k], oth_ref.at[k])

    row = lambda w: pl.BlockSpec((tm, w), lambda i: (i, 0))
    full = pl.BlockSpec((D, D), lambda i: (0, 0))
    half = pl.BlockSpec((NSH, WOUT_S // 2, D), lambda i: (0, 0, 0))
    return pl.pallas_call(
        body, name="outproj_bwd", grid=(nt,),
        in_specs=[SMEM_SPEC, row(D), full, row(512), row(512)] + ([] if dep is None else [ANY]),
        out_specs=[row(512), row(512), half, half],
        out_shape=[jax.ShapeDtypeStruct((s, 512), F32), jax.ShapeDtypeStruct((s, 512), BF16)] + _half_shapes(WOUT_S),
        scratch_shapes=[pltpu.VMEM((D, D), F32)],
        compiler_params=_cp(1))(c_arr, dmix, w_out, pool_o, attn_o, *([] if dep is None else [dep]))


def _pool_bwd(u, dpool, w_pool, pool_scale, dep=None):
    s = u.shape[0]
    tm = min(TM_POOL, s)
    hb = tm // HALO
    nt = s // tm
    last_halo = s // HALO - 1

    def body(uc_ref, uh_ref, dc_ref, dn_ref, wp_ref, sc_ref, *rest):
        du_ref, gwp_ref, gsc_ref = rest[-3:]
        i = pl.program_id(0)

        @pl.when(i == 0)
        def _():
            gwp_ref[...] = jnp.zeros_like(gwp_ref)
            gsc_ref[...] = jnp.zeros_like(gsc_ref)

        cur = uc_ref[...]
        halo = jnp.where(i > 0, uh_ref[...], 0.0)
        pooled = _pooled(jnp.concatenate([halo, cur], axis=0), cur, i * tm, tm)
        dcur = dc_ref[...]
        dnext = jnp.where(i < nt - 1, dn_ref[...], 0.0)
        dext = jnp.concatenate([dcur, dnext], axis=0)
        t_ext = i * tm + lax.broadcasted_iota(jnp.int32, (tm + HALO, GROUP), 0)
        for g, w in enumerate(WINDOWS):
            sl = slice(g * GROUP, (g + 1) * GROUP)
            pb = pooled[g].astype(BF16)
            wp = wp_ref[g]
            mixed = _dot(pb, wp)
            gsc_ref[:, sl] += jnp.sum(dcur[:, sl] * mixed, axis=0, keepdims=True)
            dmixed = (dext[:, sl] * sc_ref[:, sl]).astype(BF16)
            gwp_ref[g] += _dot_tn(pb, dmixed[0:tm])
            dpooled = _dot_nt(dmixed, wp)
            z = dpooled / jnp.minimum(t_ext + 1, w).astype(F32)
            du_ref[:, sl] = (_window_sums(z, w, -1, tm, 2) - dpooled[0:tm]).astype(BF16)

    return pl.pallas_call(
        body, name="pool_bwd", grid=(nt,),
        in_specs=[pl.BlockSpec((tm, 512), lambda i: (i, 0)),
                  pl.BlockSpec((HALO, 512), lambda i: (jnp.maximum(i * hb - 1, 0), 0)),
                  pl.BlockSpec((tm, 512), lambda i: (i, 0)),
                  pl.BlockSpec((HALO, 512), lambda i: (jnp.minimum((i + 1) * hb, last_halo), 0)),
                  pl.BlockSpec((4, GROUP, GROUP), lambda i: (0, 0, 0)),
                  pl.BlockSpec((1, 512), lambda i: (0, 0))] + ([] if dep is None else [ANY]),
        out_specs=[pl.BlockSpec((tm, 512), lambda i: (i, 0)),
                   pl.BlockSpec((4, GROUP, GROUP), lambda i: (0, 0, 0)),
                   pl.BlockSpec((1, 512), lambda i: (0, 0))],
        out_shape=[jax.ShapeDtypeStruct((s, 512), BF16), jax.ShapeDtypeStruct((4, GROUP, GROUP), F32),
                   jax.ShapeDtypeStruct((1, 512), F32)],
        compiler_params=_cp(1))(u, u, dpool, dpool, w_pool, pool_scale, *([] if dep is None else [dep]))


def _attn_bwd(q, k, v, do, bias, sinks, bucket, dep=None):
    s = q.shape[0]
    nblk = s // BLK

    def body(q_ref, do_ref, k_ref, v_ref, b_ref, sk_ref, bk_ref, *rest):
        dq_ref, dk_ref, dv_ref, grb_ref, gsk_ref, dss_ref = rest[-6:]
        n = pl.program_id(0)

        @pl.when(n == 0)
        def _():
            dk_ref[...] = jnp.zeros_like(dk_ref)
            dv_ref[...] = jnp.zeros_like(dv_ref)
            dss_ref[...] = jnp.zeros_like(dss_ref)
            gsk_ref[...] = jnp.zeros_like(gsk_ref)

        kk, kt, (lo, pstart, cstart) = _kv_band(k_ref, n, True)
        vv, _, _ = _kv_band(v_ref, n, False)
        bidx = jnp.minimum(n, 1)
        lo_q = lax.broadcasted_iota(jnp.int32, (BLK, 128), 1) < 64
        dkk = [jnp.zeros((2 * BLK, 128), F32), jnp.zeros((2 * BLK, 128), F32)]
        dvv = [jnp.zeros((2 * BLK, 128), F32), jnp.zeros((2 * BLK, 128), F32)]
        for p in range(4):
            h = p // 2
            qt = q_ref[:, p * 128:(p + 1) * 128].astype(F32) * SCALE
            dot = do_ref[:, p * 128:(p + 1) * 128]
            dq_t = jnp.zeros((128, BLK), F32)
            for e in range(2):
                head = 2 * p + e
                sel_q = lo_q if e == 0 else jnp.logical_not(lo_q)
                qm = jnp.where(sel_q, qt, 0.0).astype(BF16)
                prob, ps = _attn_probs(qm, kk[h], b_ref[bidx, head], sk_ref[0, head])
                dom = jnp.where(sel_q, dot, jnp.zeros_like(dot))
                dp = _dot_nt(vv[h], dom)
                delta = jnp.sum(prob * dp, axis=0, keepdims=True)
                ds = prob * (dp - delta)
                gsk_ref[pl.ds(head, 1), :] += jnp.broadcast_to(-jnp.sum(ps * delta).reshape(1, 1), (1, 128))
                dss_ref[head] += ds
                dsb = ds.astype(BF16)
                dq_t = dq_t + _dot(kt[h][e], dsb)
                dkk[h] = dkk[h] + _dot(dsb, qm)
                dvv[h] = dvv[h] + _dot(prob.astype(BF16), dom)
            dq_ref[:, p * 128:(p + 1) * 128] = (dq_t.T * SCALE).astype(BF16)
        for acc, ref in ((dkk, dk_ref), (dvv, dv_ref)):
            f0 = acc[0] + pltpu.roll(acc[0], 64, axis=1)
            f1 = acc[1] + pltpu.roll(acc[1], 64, axis=1)
            band = jnp.where(lo, f0, f1)
            ref[pl.ds(pstart, BLK), :] += band[0:BLK]
            ref[pl.ds(cstart, BLK), :] += band[BLK:2 * BLK]

        @pl.when(n == nblk - 1)
        def _():
            _relbias_grad(dss_ref, bk_ref[...], grb_ref)

    blk = pl.BlockSpec((BLK, 512), lambda i: (i, 0))
    kv = pl.BlockSpec((s, 128), lambda i: (0, 0))
    row8 = pl.BlockSpec((NQ, 128), lambda i: (0, 0))
    return pl.pallas_call(
        body, name="attn_bwd", grid=(nblk,),
        in_specs=[blk, blk, kv, kv, pl.BlockSpec((2, NQ, 2 * BLK, BLK), lambda i: (0, 0, 0, 0)),
                  pl.BlockSpec(memory_space=pltpu.SMEM), pl.BlockSpec((2 * BLK, BLK), lambda i: (0, 0))]
        + ([] if dep is None else [ANY]),
        out_specs=[blk, kv, kv, row8, row8],
        out_shape=[jax.ShapeDtypeStruct((s, 512), BF16), jax.ShapeDtypeStruct((s, 128), F32),
                   jax.ShapeDtypeStruct((s, 128), F32), jax.ShapeDtypeStruct((NQ, 128), F32),
                   jax.ShapeDtypeStruct((NQ, 128), F32)],
        scratch_shapes=[pltpu.VMEM((NQ, 2 * BLK, BLK), F32)],
        compiler_params=_cp(1))(q, do, k, v, bias, sinks, bucket, *([] if dep is None else [dep]))


def _relbias_grad(ds_ref, bucket_v, o_ref):
    lane = lax.broadcasted_iota(jnp.int32, (NQ, 128), 1)
    head_row = lax.broadcasted_iota(jnp.int32, (NQ, BLK), 0)
    acc = jnp.zeros((NQ, 128), F32)
    for b in range(NBUCKET):
        sel = bucket_v == b
        stack = jnp.zeros((NQ, BLK), F32)
        for h in range(NQ):
            col_sums = jnp.sum(jnp.where(sel, ds_ref[h], 0.0), axis=0, keepdims=True)
            stack = jnp.where(head_row == h, col_sums, stack)
        acc = acc + jnp.where(lane == b, jnp.sum(stack, axis=1, keepdims=True), 0.0)
    o_ref[...] = acc


def _inproj_bwd(x, g1, du, dq, dk, dv, w_in, dx1, c_arr, dep=None):
    s = x.shape[0]
    tm = min(TM, s)
    nt = s // tm
    cols = ((0, 512), (512, 1024), (1024, 1152), (1152, 1280))

    def body(c_ref, x_ref, g_ref, du_ref, dq_ref, dk_ref, dv_ref, w_ref, dx1_ref, *rest):
        gx_ref, own_ref, oth_ref, gg_ref, gw_ref = rest[-5:]
        i = pl.program_id(0)

        @pl.when(i == 0)
        def _():
            gw_ref[...] = jnp.zeros_like(gw_ref)
            gg_ref[...] = jnp.zeros_like(gg_ref)

        gv = g_ref[...]
        r1, n1 = _rms(x_ref[...])
        h = (n1 * gv).astype(BF16)
        parts = (du_ref[...], dq_ref[...], dk_ref[...].astype(BF16), dv_ref[...].astype(BF16))
        dh = None
        for (a, b), dpart in zip(cols, parts):
            t = _dot(dpart, w_ref[a:b, :])
            dh = t if dh is None else dh + t
            gw_ref[a:b, :] += _dot_tn(dpart, h)
        dx, dg = _rms_bwd(r1, n1, gv, dh)
        gg_ref[...] += dg
        gx_ref[...] = dx1_ref[...] + dx

        @pl.when(i == nt - 1)
        def _():
            for k in range(NSH):
                _emit_halves(gw_ref, k * WIN_S, WIN_S, c_ref[0], own_ref.at[k], oth_ref.at[k])

    row = lambda w: pl.BlockSpec((tm, w), lambda i: (i, 0))
    vec = pl.BlockSpec((1, D), lambda i: (0, 0))
    full = pl.BlockSpec((IN_W, D), lambda i: (0, 0))
    half = pl.BlockSpec((NSH, WIN_S // 2, D), lambda i: (0, 0, 0))
    return pl.pallas_call(
        body, name="inproj_bwd", grid=(nt,),
        in_specs=[SMEM_SPEC, row(D), vec, row(512), row(512), row(128), row(128), full, row(D)]
        + ([] if dep is None else [ANY]),
        out_specs=[row(D), half, half, vec],
        out_shape=[jax.ShapeDtypeStruct((s, D), F32)] + _half_shapes(WIN_S) + [jax.ShapeDtypeStruct((1, D), F32)],
        scratch_shapes=[pltpu.VMEM((IN_W, D), F32)],
        compiler_params=_cp(1))(c_arr, x, g1, du, dq, dk, dv, w_in, dx1, *([] if dep is None else [dep]))


def _local_step(x, target, small, w_in, w_out, ffn_weights, c_arr, on_ffn_grads=None, on_wout_grads=None,
                on_attn_grads=None):
    g1, g2, g3, g4, w_pool, pool_scale, rel_bias, sinks = small
    bucket = jnp.asarray(_bucket_table())
    wp_bf = w_pool.astype(BF16)
    bias = _bias_table(bucket, rel_bias)
    h1 = _prenorm(x, g1)
    if callable(w_in):
        w_in = w_in(bias, h1)
    u, q, k, v = _inproj_fwd(h1, w_in)
    pool_o = _pool_fwd(u, wp_bf, pool_scale)
    attn_o = _attn_fwd(q, k, v, bias, sinks)
    g2_fwd = g2
    if callable(w_out):
        w_out, after = w_out(pool_o, attn_o)
        g2_fwd = g2 + after[:1, :1]
    mix, x1, h2 = _outproj_fwd(pool_o, attn_o, w_out, x, g2_fwd, g3)
    wg, wu, wd = ffn_weights(h2) if callable(ffn_weights) else ffn_weights
    gate, up, df, dy, loss, gg4 = _ffn_fwd(h2, wg, wu, wd, x1, target, g4)
    dgate, dup, dx1, dmix, gg3, gg2 = _ffn_bwd_act(df, gate, up, wg, wu, wd, dy, x1, mix, g3, g2)
    ffn_g = _ffn_bwd_w(h2, gate, up, dgate, dup, df, c_arr)
    dep = None if on_ffn_grads is None else on_ffn_grads(ffn_g)
    dpool, dattn, wout_own, wout_oth = _outproj_bwd(dmix, w_out, pool_o, attn_o, c_arr, dep)
    dep = None if on_wout_grads is None else on_wout_grads(wout_own, wout_oth, dpool)
    du, gwp, gsc = _pool_bwd(u, dpool, wp_bf, pool_scale, dep)
    dq, dk, dv, grb, gsk = _attn_bwd(q, k, v, dattn, bias, sinks, bucket, dep)
    dep = None if on_attn_grads is None else on_attn_grads([du, dq])
    gx, win_own, win_oth, gg1 = _inproj_bwd(x, g1, du, dq, dk, dv, w_in, dx1, c_arr, dep)
    small_grads = (gg1, gg2, gg3, gg4, gwp, gsc, grb[:, :NBUCKET].T, gsk[:, 0].reshape(1, NQ))
    return loss, gx, small_grads, ((win_own, win_oth), (wout_own, wout_oth), ffn_g)


def _place():
    x, y, c = lax.axis_index("x"), lax.axis_index("y"), lax.axis_index("c")
    chips = ((1 - x, y), (x, 1 - y), (1 - x, 1 - y))
    return x, y, c, chips


def _remote(src, dst, ssem, rsem, dev):
    return pltpu.make_async_remote_copy(src_ref=src, dst_ref=dst, send_sem=ssem, recv_sem=rsem,
                                        device_id=dev, device_id_type=MESH_T)


def _to_sibling(arrs, name):
    nt = len(arrs)

    def body(*refs):
        srcs, dsts = refs[:nt], refs[nt:2 * nt]
        ssem, rsem = refs[2 * nt:]
        x, y, c, _ = _place()
        cps = [_remote(srcs[t], dsts[t], ssem.at[t], rsem.at[t], (x, y, 1 - c)) for t in range(nt)]
        for cp in cps:
            cp.start()
        for cp in cps:
            cp.wait()

    return pl.pallas_call(
        body, name=name, in_specs=[ANY] * nt, out_specs=[ANY] * nt,
        out_shape=[jax.ShapeDtypeStruct(a.shape, a.dtype) for a in arrs],
        scratch_shapes=[pltpu.SemaphoreType.DMA((nt,)), pltpu.SemaphoreType.DMA((nt,))],
        compiler_params=_cp())(*arrs)


HBM_SPEC = pl.BlockSpec(memory_space=pltpu.HBM)
SEM_SPEC = pl.BlockSpec(memory_space=pltpu.SEMAPHORE)
DATAFLOW = pltpu.SideEffectType.DATAFLOW_SIDE_EFFECTING


def _gather_copies(srcs, lands, ssem, rsem):
    x, y, c, chips = _place()
    me = 2 * x + y
    out = []
    for t in range(len(srcs)):
        half = srcs[t].shape[0] // 2
        mine = pl.ds(pl.multiple_of(c * half, 16), half)
        for j, (px, py) in enumerate(chips):
            k = 3 * t + j
            send = _remote(srcs[t].at[mine], lands[t].at[me, mine], ssem.at[k], rsem.at[k], (px, py, c))
            blk = lands[t].at[2 * px + py, mine]
            out.append((send, _remote(blk, blk, ssem.at[k], rsem.at[k], (px, py, c))))
    return out


def _scatter_copies(srcs, lands, ssem, rsem):
    x, y, c, chips = _place()
    out = []
    for t in range(len(srcs)):
        for j, (px, py) in enumerate(chips):
            k = 3 * t + j
            send = _remote(srcs[t].at[2 * px + py], lands[t].at[j], ssem.at[k], rsem.at[k], (px, py, c))
            out.append((send, _remote(lands[t].at[j], lands[t].at[j], ssem.at[k], rsem.at[k], (px, py, c))))
    return out


def _sibling_copies(srcs, lands, ssem, rsem):
    x, y, c, _ = _place()
    sib = (x, y, 1 - c)
    return [(_remote(srcs[t], lands[t], ssem.at[t], rsem.at[t], sib),
             _remote(lands[t], lands[t], ssem.at[t], rsem.at[t], sib)) for t in range(len(srcs))]


def _forward_copies(srcs, lands, ssem, rsem):
    x, y, c, chips = _place()
    sib = (x, y, 1 - c)
    out = []
    for t in range(len(lands)):
        half = lands[t].shape[1] // 2
        mine = pl.ds(pl.multiple_of(c * half, 16), half)
        other = pl.ds(pl.multiple_of((1 - c) * half, 16), half)
        for j, (px, py) in enumerate(chips):
            k = 3 * t + j
            blk_m, blk_o = lands[t].at[2 * px + py, mine], lands[t].at[2 * px + py, other]
            out.append((_remote(blk_m, blk_m, ssem.at[k], rsem.at[k], sib),
                        _remote(blk_o, blk_o, ssem.at[k], rsem.at[k], sib)))
    return out


def _peer_copies(srcs, lands, ssem, rsem):
    x, y, c, _ = _place()
    me = 4 * x + 2 * y + c
    out = []
    for kk in range(1, 8):
        px, py, pc = x ^ (kk >> 2), y ^ ((kk >> 1) & 1), c ^ (kk & 1)
        send = _remote(srcs[0], lands[0].at[me], ssem.at[kk - 1], rsem.at[kk - 1], (px, py, pc))
        slot = lands[0].at[4 * px + 2 * py + pc]
        out.append((send, _remote(slot, slot, ssem.at[kk - 1], rsem.at[kk - 1], (px, py, pc))))
    return out


def _split_start(plan, ncopy, srcs, lands, after, name):
    ns, nbuf = len(srcs), len(srcs) + len(lands)
    extra = [] if after is None else [after]
    n_in = nbuf + len(extra)

    def body(*refs):
        ssem, rsem, token = refs[n_in], refs[n_in + 1], refs[-1]
        for send, _ in plan(refs[:ns], refs[ns:nbuf], ssem, rsem):
            send.start()
        token[...] = jnp.zeros_like(token)

    bufs = [pltpu.with_memory_space_constraint(a, pltpu.HBM) for a in list(srcs) + list(lands)]
    outs = pl.pallas_call(
        body, name=name, in_specs=[HBM_SPEC] * nbuf + [ANY] * len(extra),
        out_specs=[SEM_SPEC, SEM_SPEC] + [HBM_SPEC] * nbuf + [pl.BlockSpec(memory_space=pltpu.VMEM)],
        out_shape=[pltpu.SemaphoreType.DMA((ncopy,)), pltpu.SemaphoreType.DMA((ncopy,))]
        + [pltpu.HBM(a.shape, a.dtype) for a in bufs] + [jax.ShapeDtypeStruct((8, 128), F32)],
        input_output_aliases={i: 2 + i for i in range(nbuf)},
        compiler_params=pltpu.CompilerParams(has_side_effects=DATAFLOW))(*bufs, *extra)
    return outs[0], outs[1], outs[2:2 + ns], outs[2 + ns:2 + nbuf], outs[-1]


def _split_wait(plan, ssem, rsem, srcs, lands, after, name, only=None):
    ns, nbuf = len(srcs), len(srcs) + len(lands)

    def body(*refs):
        s_ref, r_ref = refs[nbuf], refs[nbuf + 1]
        for k, (send, recv) in enumerate(plan(refs[:ns], refs[ns:nbuf], s_ref, r_ref)):
            if only is None or k in only:
                send.wait_send()
                recv.wait_recv()

    outs = pl.pallas_call(
        body, name=name, in_specs=[HBM_SPEC] * nbuf + [SEM_SPEC, SEM_SPEC] + [ANY] * len(after),
        out_specs=[HBM_SPEC] * nbuf,
        out_shape=[pltpu.HBM(a.shape, a.dtype) for a in list(srcs) + list(lands)],
        input_output_aliases={i: i for i in range(nbuf)},
        compiler_params=pltpu.CompilerParams(has_side_effects=DATAFLOW))(*srcs, *lands, ssem, rsem, *after)
    return outs


def _gather_finish(lands):
    nt = len(lands)

    def body(*refs):
        outs = refs[nt:2 * nt]
        dsend, drecv = refs[2 * nt:]
        x, y, c, chips = _place()
        sib = (x, y, 1 - c)
        sends = []
        for t in range(nt):
            half = outs[t].shape[1] // 2
            mine = pl.ds(pl.multiple_of(c * half, 16), half)
            for j, (px, py) in enumerate(chips):
                blk = outs[t].at[2 * px + py, mine]
                cp = _remote(blk, blk, dsend.at[t, j], drecv.at[t, j], sib)
                cp.start()
                sends.append(cp)
        for t in range(nt):
            half = outs[t].shape[1] // 2
            other = pl.ds(pl.multiple_of((1 - c) * half, 16), half)
            for j, (px, py) in enumerate(chips):
                blk = outs[t].at[2 * px + py, other]
                _remote(blk, blk, dsend.at[t, j], drecv.at[t, j], sib).wait_recv()
        for cp in sends:
            cp.wait_send()

    return pl.pallas_call(
        body, name="gather_finish", in_specs=[ANY] * nt, out_specs=[ANY] * nt,
        out_shape=[jax.ShapeDtypeStruct(a.shape, a.dtype) for a in lands],
        input_output_aliases={t: t for t in range(nt)},
        scratch_shapes=[pltpu.SemaphoreType.DMA((nt, 3)), pltpu.SemaphoreType.DMA((nt, 3))],
        compiler_params=_cp())(*lands)


def _chip_partial(owns, recv, chip_arr, tag):
    nt = len(owns)

    def body(chip_ref, *refs):
        k = pl.program_id(0)
        for t in range(nt):
            p = refs[t][...] + refs[nt + t][...].astype(F32)
            refs[2 * nt + t][...] = p.astype(BF16)

            @pl.when(k == chip_ref[0])
            def _():
                refs[3 * nt + t][...] = p

    blk_specs = [pl.BlockSpec((None,) + a.shape[1:], lambda k, chip_ref: (k, 0, 0)) for a in owns]
    own_specs = [pl.BlockSpec(a.shape[1:], lambda k, chip_ref: (0, 0)) for a in owns]
    return pl.pallas_call(
        body, name="rs_chip_partial_" + tag,
        grid_spec=pltpu.PrefetchScalarGridSpec(num_scalar_prefetch=1, grid=(NSH,), in_specs=blk_specs * 2,
                                               out_specs=blk_specs + own_specs),
        out_shape=[jax.ShapeDtypeStruct(a.shape, BF16) for a in owns]
        + [jax.ShapeDtypeStruct(a.shape[1:], F32) for a in owns],
        compiler_params=_cp(1))(chip_arr, *owns, *recv)


def _sum_chips(own, recv, tag, after=()):
    nt = len(own)

    def body(*refs):
        outs = refs[2 * nt + len(after):]
        for t in range(nt):
            r = refs[nt + t]
            outs[t][...] = ((refs[t][...] + r[0].astype(F32)) + r[1].astype(F32)) + r[2].astype(F32)

    vm = pl.BlockSpec(memory_space=pltpu.VMEM)
    return pl.pallas_call(
        body, name="rs_sum_chips_" + tag, in_specs=[vm] * (2 * nt) + [ANY] * len(after), out_specs=[vm] * nt,
        out_shape=[jax.ShapeDtypeStruct(a.shape, F32) for a in own],
        compiler_params=_cp())(*own, *recv, *after)


def _adamw_math(w, g, m, v):
    m = ADAM_B1 * m + (1.0 - ADAM_B1) * g
    v = ADAM_B2 * v + (1.0 - ADAM_B2) * (g * g)
    m_hat = m / (1.0 - ADAM_B1 ** ADAM_STEP)
    v_hat = v / (1.0 - ADAM_B2 ** ADAM_STEP)
    delta = -ADAM_LR * (m_hat / (jnp.sqrt(v_hat) + ADAM_EPS) + ADAM_WD * w)
    return delta, m, v


ADAM_SPLIT = 4


def _adamw_shards(own, sib, ws, ms, vs, c_arr, tag):
    nt = len(own)

    def body(c_ref, *refs):
        hh = pl.program_id(0)
        mine = hh == c_ref[0]
        for t in range(nt):
            g = jnp.where(mine, refs[t][...], refs[nt + t][...])
            delta, m, v = _adamw_math(refs[2 * nt + t][...], g, refs[3 * nt + t][...], refs[4 * nt + t][...])
            refs[5 * nt + t][...] = g
            refs[6 * nt + t][...] = delta
            refs[7 * nt + t][...] = m
            refs[8 * nt + t][...] = v

    def tile(a):
        return (a.shape[0] // ADAM_SPLIT, a.shape[1])

    half_specs = [pl.BlockSpec(tile(a), lambda hh, q, c_ref: (q, 0)) for a in own]
    full_specs = [pl.BlockSpec(tile(a), lambda hh, q, c_ref: (hh * ADAM_SPLIT + q, 0)) for a in own]
    return pl.pallas_call(
        body, name="adamw_shards_" + tag,
        grid_spec=pltpu.PrefetchScalarGridSpec(num_scalar_prefetch=1, grid=(2, ADAM_SPLIT),
                                               in_specs=half_specs * 2 + full_specs * 3, out_specs=full_specs * 4),
        out_shape=[jax.ShapeDtypeStruct(w.shape, F32) for w in ws] * 4,
        compiler_params=_cp(2))(c_arr, *own, *sib, *ws, *ms, *vs)


SMALL_WPOOL_ROW = 32
SMALL_SCALE_ROW = SMALL_WPOOL_ROW + 4 * GROUP
SMALL_BIAS_ROW = SMALL_SCALE_ROW + 8
SMALL_SINKS_ROW = SMALL_BIAS_ROW + NBUCKET
SMALL_LOSS_ROW = SMALL_SINKS_ROW + 8


def _small_sum_adamw(gathered, wp, mp, vp):
    rows = wp.shape[0]

    def body(g_ref, w_ref, m_ref, v_ref, *rest):
        outs, res = rest[:-1], rest[-1]
        g = g_ref[0]
        for d in range(1, 8):
            g = g + g_ref[d]
        delta, m, v = _adamw_math(w_ref[...], g, m_ref[...], v_ref[...])
        for kind, val in enumerate((g, delta, m, v)):
            res[kind] = val
            gains = outs[8 * kind:8 * kind + 4]
            w_pool_o, scale_o, bias_o, sinks_o = outs[8 * kind + 4:8 * kind + 8]
            for t in range(4):
                for i in range(8):
                    gains[t][:, 128 * i:128 * (i + 1)] = res[kind, pl.ds(8 * t + i, 1), :]
            for grp in range(4):
                w_pool_o[grp] = res[kind, pl.ds(SMALL_WPOOL_ROW + GROUP * grp, GROUP), :]
            for i in range(4):
                scale_o[:, 128 * i:128 * (i + 1)] = res[kind, pl.ds(SMALL_SCALE_ROW + i, 1), :]
            bias_o[...] = res[kind, pl.ds(SMALL_BIAS_ROW, NBUCKET), :][:, 0:NQ]
            sinks_o[...] = res[kind, pl.ds(SMALL_SINKS_ROW, 1), :][:, 0:NQ]
        outs[-1][...] = res[0, pl.ds(SMALL_LOSS_ROW, 1), :]

    vm = pl.BlockSpec(memory_space=pltpu.VMEM)
    one_kind = [jax.ShapeDtypeStruct((1, D), F32)] * 4 + [
        jax.ShapeDtypeStruct((4, GROUP, GROUP), F32), jax.ShapeDtypeStruct((1, POOL_W), F32),
        jax.ShapeDtypeStruct((NBUCKET, NQ), F32), jax.ShapeDtypeStruct((1, NQ), F32)]
    return pl.pallas_call(
        body, name="small_sum_adamw", in_specs=[vm] * 4, out_specs=[vm] * 33,
        out_shape=one_kind * 4 + [jax.ShapeDtypeStruct((1, 128), F32)],
        scratch_shapes=[pltpu.VMEM((4, rows, 128), F32)],
        compiler_params=_cp())(gathered, wp, mp, vp)


def _pack_small(gains4, w_pool, pool_scale, rel_bias, sinks, loss):
    parts = list(gains4) + [w_pool, pool_scale, jnp.pad(rel_bias, ((0, 0), (0, 128 - NQ))), sinks, loss]
    rows = []
    for a in parts:
        flat = a.reshape(-1)
        n = flat.shape[0]
        padded = -(-n // 1024) * 1024
        rows.append(jnp.pad(flat, (0, padded - n)).reshape(-1, 128))
    return jnp.concatenate(rows, axis=0)


def kernel(x, g_pre_mix, w_in, w_pool, pool_scale, rel_bias, sinks, w_out, g_post_mix, g_pre_ffn, w_gate, w_up, w_down, g_post_ffn, loss_target, m_g_pre_mix, m_w_in, m_w_pool, m_pool_scale, m_rel_bias, m_sinks, m_w_out, m_g_post_mix, m_g_pre_ffn, m_w_gate, m_w_up, m_w_down, m_g_post_ffn, v_g_pre_mix, v_w_in, v_w_pool, v_pool_scale, v_rel_bias, v_sinks, v_w_out, v_g_post_mix, v_g_pre_ffn, v_w_gate, v_w_up, v_w_down, v_g_post_ffn):
    c = lax.axis_index("c")
    chip = 2 * lax.axis_index("x") + lax.axis_index("y")

    def shards(a_in, a_out, a_gate, a_up, a_down):
        return (a_in[0].T, a_out[0], a_gate[0].T, a_up[0].T, a_down[0])

    c_arr = c.reshape(1).astype(jnp.int32)
    chip_arr = chip.reshape(1).astype(jnp.int32)

    big_w = shards(w_in, w_out, w_gate, w_up, w_down)
    big_bf = [w.astype(BF16) for w in big_w]
    g_ssem, g_rsem, g_srcs, g_lands, _ = _split_start(
        _gather_copies, 15, big_bf, [lax.empty((NSH,) + a.shape, BF16) for a in big_bf], None, "gather_start")
    fw = {}

    def with_own(land, shard):
        return lax.dynamic_update_slice(land, shard[None], (chip, 0, 0))

    def w_in_ready(*after):
        fw["first"] = _split_wait(_gather_copies, g_ssem, g_rsem, g_srcs, g_lands, after, "gather_win_wait",
                                  only=(0, 1, 2))
        (fw["win"],) = _gather_finish([with_own(fw["first"][5], fw["first"][0])])
        return fw["win"].reshape(IN_W, D)

    def w_out_ready(*after):
        first = fw["first"]
        outs = _split_wait(_gather_copies, g_ssem, g_rsem, first[:5], [fw["win"]] + list(first[6:]), after,
                           "gather_rest_wait", only=tuple(range(3, 15)))
        lands = [with_own(land, src) for src, land in zip(outs[1:5], outs[6:])]
        (wout_all,) = _gather_finish(lands[:1])
        fw["f"] = _split_start(_forward_copies, 9, [], lands[1:], wout_all, "gather_forward_start")
        return wout_all.reshape(D, D), fw["f"][4]

    def ffn_weights(after):
        ssem, rsem, _, lands, _ = fw["f"]
        return _split_wait(_forward_copies, ssem, rsem, [], lands, [after], "gather_forward_wait")

    rs = {}

    def on_ffn_grads(ffn_g):
        oths = ffn_g[1::2]
        rs["ffn_own"] = ffn_g[0::2]
        rs["x"] = _split_start(_sibling_copies, 3, oths, [lax.empty(a.shape, BF16) for a in oths], ffn_g[0],
                               "rs_exchange_ffn_start")
        return rs["x"][4]

    def on_wout_grads(own, oth, after):
        ssem, rsem, srcs, lands, _ = rs["x"]
        recv_ffn = _split_wait(_sibling_copies, ssem, rsem, srcs, lands, [after], "rs_exchange_ffn_wait")[3:]
        recv_wout = _to_sibling([oth], "rs_exchange_wout")
        outs = _chip_partial([own] + list(rs["ffn_own"]), list(recv_wout) + list(recv_ffn), chip_arr, "early")
        rs["early_own"] = outs[4:]
        rs["s"] = _split_start(_scatter_copies, 12, outs[:4],
                               [lax.empty((3,) + a.shape[1:], BF16) for a in outs[:4]], outs[4], "scatter_early_start")
        return rs["s"][4]

    def on_attn_grads(after):
        ssem, rsem, srcs, lands, _ = rs["s"]
        recv_early = _split_wait(_scatter_copies, ssem, rsem, srcs, lands, after, "scatter_early_wait")[4:]
        finals = _sum_chips(list(rs["early_own"]), list(recv_early), "early")
        rs["sh"] = _split_start(_sibling_copies, 4, finals, [lax.empty(a.shape, F32) for a in finals], finals[0],
                                "rs_share_early_start")
        return rs["sh"][4]

    small = (g_pre_mix, g_post_mix, g_pre_ffn, g_post_ffn, w_pool[0], pool_scale, rel_bias, sinks)
    loss, gx, small_grads, ((win_own, win_oth), _, _) = _local_step(
        x[0], loss_target[0], small, w_in_ready, w_out_ready, ffn_weights, c_arr,
        on_ffn_grads, on_wout_grads, on_attn_grads)

    unused = jnp.zeros((1, 1), F32)
    gp = _pack_small(small_grads[:4], *small_grads[4:], loss)
    wp = _pack_small((g_pre_mix, g_post_mix, g_pre_ffn, g_post_ffn), w_pool, pool_scale, rel_bias, sinks, unused)
    mp = _pack_small((m_g_pre_mix, m_g_post_mix, m_g_pre_ffn, m_g_post_ffn), m_w_pool, m_pool_scale, m_rel_bias,
                     m_sinks, unused)
    vp = _pack_small((v_g_pre_mix, v_g_post_mix, v_g_pre_ffn, v_g_post_ffn), v_w_pool, v_pool_scale, v_rel_bias,
                     v_sinks, unused)

    moments = (shards(m_w_in, m_w_out, m_w_gate, m_w_up, m_w_down),
               shards(v_w_in, v_w_out, v_w_gate, v_w_up, v_w_down))
    recv_a = _to_sibling([win_oth], "rs_exchange_win")
    outs = _chip_partial([win_own], recv_a, chip_arr, "win")
    w_ssem, w_rsem, w_srcs, w_lands, w_token = _split_start(
        _scatter_copies, 3, outs[:1], [lax.empty((3,) + outs[0].shape[1:], BF16)], gx, "scatter_win_start")
    slots = lax.dynamic_update_slice(lax.empty((8,) + gp.shape, F32), gp[None], (2 * chip + c, 0, 0))
    p_ssem, p_rsem, p_srcs, p_lands, p_token = _split_start(_peer_copies, 7, [gp], [slots], w_token,
                                                            "small_allgather_start")
    ssem, rsem, srcs, lands, _ = rs["sh"]
    shared = _split_wait(_sibling_copies, ssem, rsem, srcs, lands, [p_token], "rs_share_early_wait")
    adam_e = _adamw_shards(shared[:4], shared[4:], big_w[1:], moments[0][1:], moments[1][1:], c_arr, "early")
    recv_win = _split_wait(_scatter_copies, w_ssem, w_rsem, w_srcs, w_lands, [adam_e[0]], "scatter_win_wait")[1:]
    win_final = _sum_chips([outs[1]], recv_win, "win")
    win_sib = _to_sibling(win_final, "rs_share_win")
    adam_w = _adamw_shards(win_final, win_sib, big_w[:1], moments[0][:1], moments[1][:1], c_arr, "win")
    big_g, big_d, big_m, big_v = [[adam_w[i]] + list(adam_e[4 * i:4 * i + 4]) for i in range(4)]

    gathered = _split_wait(_peer_copies, p_ssem, p_rsem, p_srcs, p_lands, [adam_w[0]], "small_allgather_wait")[1]
    flat = _small_sum_adamw(gathered, wp, mp, vp)
    small_out = [flat[8 * kind:8 * kind + 8] for kind in range(4)]
    total_loss = flat[-1][0, 0]

    def ordered(small8, big4):
        sg1, sg2, sg3, sg4, swp, ssc, srb, ssk = small8
        swp = swp[None]
        bwin, bwout, bwg, bwu, bwd = big4[0].T[None], big4[1][None], big4[2].T[None], big4[3].T[None], big4[4][None]
        return [sg1, bwin, swp, ssc, srb, ssk, bwout, sg2, sg3, bwg, bwu, bwd, sg4]

    res = [total_loss, gx[None]]
    for sm, bg in zip(small_out, (big_g, big_d, big_m, big_v)):
        res += ordered(sm, bg)
    return tuple(res)
```

```python
import numpy as np
import jax
import jax.numpy as jnp
from jax import lax
from jax.experimental import pallas as pl
from jax.experimental.pallas import tpu as pltpu

F32 = jnp.float32
BF16 = jnp.bfloat16

D = 1024
POOL_W = 512
GROUP = 128
WINDOWS = (2, 4, 8, 16)
HALO = 128
NQ = 8
BLK = 128
NBUCKET = 32
IN_W = 1280
DFF = 2816
NSH = 4
FS = DFF // NSH
WIN_S = IN_W // NSH
WOUT_S = D // NSH
EPS = 1e-6
NEG = -1e30
SCALE = 0.125

ADAM_LR = 0.001
ADAM_B1 = 0.9
ADAM_B2 = 0.999
ADAM_EPS = 1e-08
ADAM_WD = 0.01
ADAM_STEP = 10

TM = 512
TM_POOL = 256
TM_FFN = 512
TM_FFN_FWD = 1024
FFN_ROWS = 256
VMEM_LIMIT = 60 * 1024 * 1024

MESH_T = pl.DeviceIdType.MESH
ANY = pl.BlockSpec(memory_space=pl.ANY)


def _cp(n_grid=0, **kw):
    sem = ("arbitrary",) * n_grid if n_grid else None
    return pltpu.CompilerParams(dimension_semantics=sem, vmem_limit_bytes=VMEM_LIMIT, **kw)


def _dot(a, b):
    return jnp.dot(a, b, preferred_element_type=F32)


def _dot_nt(a, b):
    return lax.dot_general(a, b, (((1,), (1,)), ((), ())), preferred_element_type=F32)


def _dot_tn(a, b):
    return lax.dot_general(a, b, (((0,), (0,)), ((), ())), preferred_element_type=F32)


def _rms(x):
    r = lax.rsqrt(jnp.mean(x * x, axis=-1, keepdims=True) + EPS)
    return r, x * r


def _rms_bwd(r, n, g, dout):
    dn = dout * g
    dx = r * (dn - n * jnp.mean(dn * n, axis=-1, keepdims=True))
    dg = jnp.sum(dout * n, axis=0, keepdims=True)
    return dx, dg


def _split(x, n):
    parts = []
    r = x
    for _ in range(n):
        p = r.astype(BF16)
        parts.append(p)
        r = r - p.astype(F32)
    return parts


def _band_dot(a, x, n):
    acc = None
    for p in _split(x, n):
        t = _dot(a, p)
        acc = t if acc is None else acc + t
    return acc


def _bucket_table():
    qi = np.arange(BLK)[None, :]
    kj = np.arange(2 * BLK)[:, None]
    dist = qi + BLK - kj
    n = np.maximum(dist, 0)
    nf = np.maximum(n, 1).astype(np.float32)
    large = 16 + (np.log(nf / np.float32(16)) / np.float32(np.log(128 / 16)) * np.float32(16)).astype(np.int32)
    large = np.minimum(large, NBUCKET - 1)
    return np.where(n < 16, n, large).astype(np.int32)


def _prenorm(x, g1):
    s = x.shape[0]
    tm = min(TM, s)

    def body(x_ref, g_ref, h_ref):
        _, n = _rms(x_ref[...])
        h_ref[...] = (n * g_ref[...]).astype(BF16)

    row = pl.BlockSpec((tm, D), lambda i: (i, 0))
    return pl.pallas_call(
        body, name="prenorm", grid=(s // tm,),
        in_specs=[row, pl.BlockSpec((1, D), lambda i: (0, 0))], out_specs=row,
        out_shape=jax.ShapeDtypeStruct((s, D), BF16),
        compiler_params=_cp(1))(x, g1)


def _inproj_fwd(h1, w_in):
    s = h1.shape[0]
    tm = min(TM, s)

    def body(h_ref, w_ref, u_ref, q_ref, k_ref, v_ref):
        proj = _dot_nt(h_ref[...], w_ref[...])
        u_ref[...] = proj[:, :512]
        q_ref[...] = proj[:, 512:1024].astype(BF16)
        k_ref[...] = proj[:, 1024:1152].astype(BF16)
        v_ref[...] = proj[:, 1152:1280].astype(BF16)

    row = lambda w: pl.BlockSpec((tm, w), lambda i: (i, 0))
    return pl.pallas_call(
        body, name="inproj_fwd", grid=(s // tm,),
        in_specs=[row(D), pl.BlockSpec((IN_W, D), lambda i: (0, 0))],
        out_specs=[row(512), row(512), row(128), row(128)],
        out_shape=[jax.ShapeDtypeStruct((s, 512), F32), jax.ShapeDtypeStruct((s, 512), BF16),
                   jax.ShapeDtypeStruct((s, 128), BF16), jax.ShapeDtypeStruct((s, 128), BF16)],
        compiler_params=_cp(1))(h1, w_in)


def _window_sums(ext, w, lag_sign, tm, terms):
    rows = lax.broadcasted_iota(jnp.int32, (HALO, 2 * HALO), 0)
    cols = lax.broadcasted_iota(jnp.int32, (HALO, 2 * HALO), 1)
    d = rows + HALO - cols if lag_sign > 0 else cols - rows
    band = jnp.where((d >= 0) & (d < w), 1.0, 0.0).astype(BF16)
    return jnp.concatenate([_band_dot(band, ext[r:r + 2 * HALO], terms) for r in range(0, tm, HALO)], axis=0)


def _pooled(ext, cur, t0, tm):
    t = t0 + lax.broadcasted_iota(jnp.int32, (tm, GROUP), 0)
    out = []
    for g, w in enumerate(WINDOWS):
        sl = slice(g * GROUP, (g + 1) * GROUP)
        cnt = jnp.minimum(t + 1, w).astype(F32)
        out.append(_window_sums(ext[:, sl], w, 1, tm, 2) / cnt - cur[:, sl])
    return out


def _pool_fwd(u, w_pool, pool_scale):
    s = u.shape[0]
    tm = min(TM_POOL, s)
    hb = tm // HALO

    def body(uc_ref, uh_ref, wp_ref, sc_ref, o_ref):
        i = pl.program_id(0)
        cur = uc_ref[...]
        halo = jnp.where(i > 0, uh_ref[...], 0.0)
        ext = jnp.concatenate([halo, cur], axis=0)
        pooled = _pooled(ext, cur, i * tm, tm)
        for g in range(4):
            sl = slice(g * GROUP, (g + 1) * GROUP)
            mixed = _dot(pooled[g].astype(BF16), wp_ref[g])
            o_ref[:, sl] = (mixed * sc_ref[:, sl]).astype(BF16)

    return pl.pallas_call(
        body, name="pool_fwd", grid=(s // tm,),
        in_specs=[pl.BlockSpec((tm, 512), lambda i: (i, 0)),
                  pl.BlockSpec((HALO, 512), lambda i: (jnp.maximum(i * hb - 1, 0), 0)),
                  pl.BlockSpec((4, GROUP, GROUP), lambda i: (0, 0, 0)),
                  pl.BlockSpec((1, 512), lambda i: (0, 0))],
        out_specs=pl.BlockSpec((tm, 512), lambda i: (i, 0)),
        out_shape=jax.ShapeDtypeStruct((s, 512), BF16),
        compiler_params=_cp(1))(u, u, w_pool, pool_scale)


def _bias_table(bucket, rel_bias):
    def body(b_ref, rb_ref, o_ref):
        bucket_v = b_ref[...]
        key = lax.broadcasted_iota(jnp.int32, (2 * BLK, BLK), 0)
        dist = lax.broadcasted_iota(jnp.int32, (2 * BLK, BLK), 1) + BLK - key
        inwin = (dist >= 0) & (dist < BLK)
        for h in range(NQ):
            acc = jnp.zeros((2 * BLK, BLK), F32)
            for b in range(NBUCKET):
                acc = jnp.where(bucket_v == b, rb_ref[b, h], acc)
            rest = jnp.where(inwin, acc, NEG)
            o_ref[1, h] = rest
            o_ref[0, h] = jnp.where(key >= BLK, rest, NEG)

    return pl.pallas_call(
        body, name="bias_table",
        in_specs=[pl.BlockSpec(memory_space=pltpu.VMEM), pl.BlockSpec(memory_space=pltpu.SMEM)],
        out_specs=pl.BlockSpec(memory_space=pltpu.VMEM),
        out_shape=jax.ShapeDtypeStruct((2, NQ, 2 * BLK, BLK), F32),
        compiler_params=_cp())(bucket, rel_bias)


HD = 64


def _kv_band(ref, n, transposed):
    pstart = pl.multiple_of(jnp.maximum(n - 1, 0) * BLK, BLK)
    cstart = pl.multiple_of(n * BLK, BLK)
    lo = lax.broadcasted_iota(jnp.int32, (2 * BLK, 128), 1) < HD
    band = jnp.concatenate([ref[pl.ds(pstart, BLK), :], ref[pl.ds(cstart, BLK), :]], axis=0).astype(F32)
    rot = pltpu.roll(band, HD, axis=1)
    halves = ((jnp.where(lo, band, 0.0), jnp.where(lo, 0.0, rot)), (jnp.where(lo, rot, 0.0), jnp.where(lo, 0.0, band)))
    if transposed:
        out = [jnp.concatenate([a.T, b.T], axis=1).astype(BF16) for a, b in halves]
    else:
        out = [jnp.concatenate([a, b], axis=0).astype(BF16) for a, b in halves]
    return out, (lo, pstart, cstart)


def _pair_probs(qt, kk, bias, sk_ref, p):
    sink = jnp.concatenate([jnp.full((1, 1, BLK), sk_ref[0, 2 * p + e], F32) for e in range(2)], axis=0)
    s = _dot_nt(kk, qt).reshape(2, 2 * BLK, BLK) + bias
    m = jnp.maximum(jnp.max(s, axis=1, keepdims=True), sink)
    pr = jnp.exp(s - m)
    es = jnp.exp(sink - m)
    inv = 1.0 / (jnp.sum(pr, axis=1, keepdims=True) + es)
    return pr * inv, es * inv


def _attn_fwd(q, k, v, bias, sinks):
    s = q.shape[0]

    def body(q_ref, k_ref, v_ref, b_ref, sk_ref, o_ref):
        n = pl.program_id(0)
        kk, _ = _kv_band(k_ref, n, False)
        vt, _ = _kv_band(v_ref, n, True)
        bidx = jnp.minimum(n, 1)
        for p in range(4):
            h = p // 2
            qt = (q_ref[:, p * 128:(p + 1) * 128].astype(F32) * SCALE).astype(BF16)
            prob, _ = _pair_probs(qt, kk[h], b_ref[bidx, pl.ds(2 * p, 2)], sk_ref, p)
            acc_t = _dot(vt[h], prob.reshape(4 * BLK, BLK).astype(BF16))
            o_ref[:, p * 128:(p + 1) * 128] = acc_t.T.astype(BF16)

    return pl.pallas_call(
        body, name="attn_fwd", grid=(s // BLK,),
        in_specs=[pl.BlockSpec((BLK, 512), lambda i: (i, 0)),
                  pl.BlockSpec((s, 128), lambda i: (0, 0)),
                  pl.BlockSpec((s, 128), lambda i: (0, 0)),
                  pl.BlockSpec((2, NQ, 2 * BLK, BLK), lambda i: (0, 0, 0, 0)),
                  pl.BlockSpec(memory_space=pltpu.SMEM)],
        out_specs=pl.BlockSpec((BLK, 512), lambda i: (i, 0)),
        out_shape=jax.ShapeDtypeStruct((s, 512), BF16),
        compiler_params=_cp(1))(q, k, v, bias, sinks)


def _outproj_fwd(pool_o, attn_o, w_out, x, g2, g3):
    s = x.shape[0]
    tm = min(TM, s)

    def body(p_ref, a_ref, w_ref, x_ref, g2_ref, g3_ref, mix_ref, x1_ref, h2_ref):
        mix = _dot(p_ref[...], w_ref[0:512, :]) + _dot(a_ref[...], w_ref[512:1024, :])
        mix_ref[...] = mix
        _, n2 = _rms(mix)
        x1 = x_ref[...] + n2 * g2_ref[...]
        x1_ref[...] = x1
        _, n3 = _rms(x1)
        h2_ref[...] = (n3 * g3_ref[...]).astype(BF16)

    row = lambda w: pl.BlockSpec((tm, w), lambda i: (i, 0))
    vec = pl.BlockSpec((1, D), lambda i: (0, 0))
    return pl.pallas_call(
        body, name="outproj_fwd", grid=(s // tm,),
        in_specs=[row(512), row(512), pl.BlockSpec((D, D), lambda i: (0, 0)), row(D), vec, vec],
        out_specs=[row(D), row(D), row(D)],
        out_shape=[jax.ShapeDtypeStruct((s, D), F32), jax.ShapeDtypeStruct((s, D), F32),
                   jax.ShapeDtypeStruct((s, D), BF16)],
        compiler_params=_cp(1))(pool_o, attn_o, w_out, x, g2, g3)


def _silu_parts(g):
    sg = jax.nn.sigmoid(g)
    return sg, g * sg


def _ffn_fwd(h2, wg, wu, wd, x1, target, g4):
    s = h2.shape[0]
    tm = min(TM_FFN_FWD, s)

    def body(h_ref, wg_ref, wu_ref, wd_ref, x1_ref, t_ref, g4_ref,
             gate_ref, up_ref, df_ref, dy_ref, loss_ref, gg4_ref, f_acc):
        i = pl.program_id(0)
        j = pl.program_id(1)
        first = j == 0
        for r in range(0, tm, FFN_ROWS):
            rows = pl.ds(r, min(FFN_ROWS, tm))
            h = h_ref[rows, :]
            gate = _dot_nt(h, wg_ref[...])
            up = _dot_nt(h, wu_ref[...])
            gate_ref[rows, :] = gate.astype(BF16)
            up_ref[rows, :] = up.astype(BF16)
            _, sl = _silu_parts(gate)
            contrib = _dot((sl * up).astype(BF16), wd_ref[...])
            f_acc[rows, :] = jnp.where(first, contrib, f_acc[rows, :] + contrib)

        @pl.when((i == 0) & (j == 0))
        def _():
            loss_ref[...] = jnp.zeros_like(loss_ref)
            gg4_ref[...] = jnp.zeros_like(gg4_ref)

        @pl.when(j == NSH - 1)
        def _():
            r4, n4 = _rms(f_acc[...])
            g4v = g4_ref[...]
            err = x1_ref[...] + n4 * g4v - t_ref[...]
            loss_ref[...] += (0.5 / D) * jnp.sum(err * err).reshape(1, 1)
            dy = err * (1.0 / D)
            dy_ref[...] = dy
            df, dg = _rms_bwd(r4, n4, g4v, dy)
            gg4_ref[...] += dg
            df_ref[...] = df.astype(BF16)

    row = lambda w: pl.BlockSpec((tm, w), lambda i, j: (i, 0))
    sh = lambda r, c: pl.BlockSpec((None, r, c), lambda i, j: (j, 0, 0))
    act = pl.BlockSpec((None, tm, FS), lambda i, j: (j, i, 0))
    vec = pl.BlockSpec((1, D), lambda i, j: (0, 0))
    return pl.pallas_call(
        body, name="ffn_fwd", grid=(s // tm, NSH),
        in_specs=[row(D), sh(FS, D), sh(FS, D), sh(FS, D), row(D), row(D), vec],
        out_specs=[act, act, row(D), row(D), pl.BlockSpec((1, 1), lambda i, j: (0, 0)), vec],
        out_shape=[jax.ShapeDtypeStruct((NSH, s, FS), BF16), jax.ShapeDtypeStruct((NSH, s, FS), BF16),
                   jax.ShapeDtypeStruct((s, D), BF16), jax.ShapeDtypeStruct((s, D), F32),
                   jax.ShapeDtypeStruct((1, 1), F32), jax.ShapeDtypeStruct((1, D), F32)],
        scratch_shapes=[pltpu.VMEM((tm, D), F32)],
        compiler_params=_cp(2))(h2, wg, wu, wd, x1, target, g4)


def _ffn_bwd_act(df, gate, up, wg, wu, wd, dy, x1, mix, g3, g2):
    s = df.shape[0]
    tm = min(TM_FFN, s)

    def body(df_ref, gate_ref, up_ref, wg_ref, wu_ref, wd_ref, dy_ref, x1_ref, mix_ref, g3_ref, g2_ref,
             dgate_ref, dup_ref, dx1_ref, dmix_ref, gg3_ref, gg2_ref, acc):
        i = pl.program_id(0)
        j = pl.program_id(1)
        first = j == 0
        for r in range(0, tm, FFN_ROWS):
            rows = pl.ds(r, min(FFN_ROWS, tm))
            da = _dot_nt(df_ref[rows, :], wd_ref[...])
            g = gate_ref[rows, :].astype(F32)
            u = up_ref[rows, :].astype(F32)
            sg, sl = _silu_parts(g)
            dgate = (da * u * (sg * (1.0 + g * (1.0 - sg)))).astype(BF16)
            dup = (da * sl).astype(BF16)
            dgate_ref[rows, :] = dgate
            dup_ref[rows, :] = dup
            contrib = _dot(dgate, wg_ref[...]) + _dot(dup, wu_ref[...])
            acc[rows, :] = jnp.where(first, contrib, acc[rows, :] + contrib)

        @pl.when((i == 0) & (j == 0))
        def _():
            gg3_ref[...] = jnp.zeros_like(gg3_ref)
            gg2_ref[...] = jnp.zeros_like(gg2_ref)

        @pl.when(j == NSH - 1)
        def _():
            r3, n3 = _rms(x1_ref[...])
            dx1n, dg3 = _rms_bwd(r3, n3, g3_ref[...], acc[...])
            dx1 = dy_ref[...] + dx1n
            dx1_ref[...] = dx1
            gg3_ref[...] += dg3
            r2, n2 = _rms(mix_ref[...])
            dmix, dg2 = _rms_bwd(r2, n2, g2_ref[...], dx1)
            gg2_ref[...] += dg2
            dmix_ref[...] = dmix.astype(BF16)

    row = lambda w: pl.BlockSpec((tm, w), lambda i, j: (i, 0))
    sh = lambda r, c: pl.BlockSpec((None, r, c), lambda i, j: (j, 0, 0))
    act = pl.BlockSpec((None, tm, FS), lambda i, j: (j, i, 0))
    vec = pl.BlockSpec((1, D), lambda i, j: (0, 0))
    return pl.pallas_call(
        body, name="ffn_bwd_act", grid=(s // tm, NSH),
        in_specs=[row(D), act, act, sh(FS, D), sh(FS, D), sh(FS, D), row(D), row(D), row(D), vec, vec],
        out_specs=[act, act, row(D), row(D), vec, vec],
        out_shape=[jax.ShapeDtypeStruct((NSH, s, FS), BF16), jax.ShapeDtypeStruct((NSH, s, FS), BF16),
                   jax.ShapeDtypeStruct((s, D), F32), jax.ShapeDtypeStruct((s, D), BF16),
                   jax.ShapeDtypeStruct((1, D), F32), jax.ShapeDtypeStruct((1, D), F32)],
        scratch_shapes=[pltpu.VMEM((tm, D), F32)],
        compiler_params=_cp(2))(df, gate, up, wg, wu, wd, dy, x1, mix, g3, g2)


SMEM_SPEC = pl.BlockSpec(memory_space=pltpu.SMEM)


def _emit_halves(acc_ref, row0, rows, c, own_ref, oth_ref):
    half = rows // 2
    own_ref[...] = acc_ref[pl.ds(row0 + pl.multiple_of(c * half, 8), half), :]
    oth_ref[...] = acc_ref[pl.ds(row0 + pl.multiple_of((1 - c) * half, 8), half), :].astype(BF16)


def _half_shapes(rows):
    return [jax.ShapeDtypeStruct((NSH, rows // 2, D), F32), jax.ShapeDtypeStruct((NSH, rows // 2, D), BF16)]


def _ffn_bwd_w(h2, gate, up, dgate, dup, df, c_arr):
    s = h2.shape[0]
    tm = min(TM_FFN_FWD, s)
    nt = s // tm

    def body(c_ref, h_ref, gate_ref, up_ref, dgate_ref, dup_ref, df_ref, *rest):
        outs, accs = rest[:6], rest[6:]
        i = pl.program_id(1)
        @pl.when(i == 0)
        def _():
            for acc in accs:
                acc[...] = jnp.zeros_like(acc)

        h = h_ref[...]
        accs[0][...] += _dot_tn(dgate_ref[...], h)
        accs[1][...] += _dot_tn(dup_ref[...], h)
        _, sl = _silu_parts(gate_ref[...].astype(F32))
        a = (sl * up_ref[...].astype(F32)).astype(BF16)
        accs[2][...] += _dot_tn(a, df_ref[...])

        @pl.when(i == nt - 1)
        def _():
            for t, acc in enumerate(accs):
                _emit_halves(acc, 0, FS, c_ref[0], outs[2 * t], outs[2 * t + 1])

    row = lambda w: pl.BlockSpec((tm, w), lambda j, i: (i, 0))
    act = pl.BlockSpec((None, tm, FS), lambda j, i: (j, i, 0))
    half = pl.BlockSpec((None, FS // 2, D), lambda j, i: (j, 0, 0))
    return pl.pallas_call(
        body, name="ffn_bwd_w", grid=(NSH, nt),
        in_specs=[SMEM_SPEC, row(D), act, act, act, act, row(D)],
        out_specs=[half] * 6,
        out_shape=_half_shapes(FS) * 3,
        scratch_shapes=[pltpu.VMEM((FS, D), F32)] * 3,
        compiler_params=_cp(2))(c_arr, h2, gate, up, dgate, dup, df)


def _outproj_bwd(dmix, w_out, pool_o, attn_o, c_arr, dep=None):
    s = dmix.shape[0]
    tm = min(TM, s)
    nt = s // tm

    def body(c_ref, dm_ref, w_ref, p_ref, a_ref, *rest):
        dpool_ref, dattn_ref, own_ref, oth_ref, gw_ref = rest[-5:]
        i = pl.program_id(0)

        @pl.when(i == 0)
        def _():
            gw_ref[...] = jnp.zeros_like(gw_ref)

        dm = dm_ref[...]
        dpool_ref[...] = _dot_nt(dm, w_ref[0:512, :])
        dattn_ref[...] = _dot_nt(dm, w_ref[512:1024, :]).astype(BF16)
        gw_ref[0:512, :] += _dot_tn(p_ref[...], dm)
        gw_ref[512:1024, :] += _dot_tn(a_ref[...], dm)

        @pl.when(i == nt - 1)
        def _():
            for k in range(NSH):
                _emit_halves(gw_ref, k * WOUT_S, WOUT_S, c_ref[0], own_ref.at[k], oth_ref.at[k])

    row = lambda w: pl.BlockSpec((tm, w), lambda i: (i, 0))
    full = pl.BlockSpec((D, D), lambda i: (0, 0))
    half = pl.BlockSpec((NSH, WOUT_S // 2, D), lambda i: (0, 0, 0))
    return pl.pallas_call(
        body, name="outproj_bwd", grid=(nt,),
        in_specs=[SMEM_SPEC, row(D), full, row(512), row(512)] + ([] if dep is None else [ANY]),
        out_specs=[row(512), row(512), half, half],
        out_shape=[jax.ShapeDtypeStruct((s, 512), F32), jax.ShapeDtypeStruct((s, 512), BF16)] + _half_shapes(WOUT_S),
        scratch_shapes=[pltpu.VMEM((D, D), F32)],
        compiler_params=_cp(1))(c_arr, dmix, w_out, pool_o, attn_o, *([] if dep is None else [dep]))


def _pool_bwd(u, dpool, w_pool, pool_scale, dep=None):
    s = u.shape[0]
    tm = min(TM_POOL, s)
    hb = tm // HALO
    nt = s // tm
    last_halo = s // HALO - 1

    def body(uc_ref, uh_ref, dc_ref, dn_ref, wp_ref, sc_ref, *rest):
        du_ref, gwp_ref, gsc_ref = rest[-3:]
        i = pl.program_id(0)

        @pl.when(i == 0)
        def _():
            gwp_ref[...] = jnp.zeros_like(gwp_ref)
            gsc_ref[...] = jnp.zeros_like(gsc_ref)

        cur = uc_ref[...]
        halo = jnp.where(i > 0, uh_ref[...], 0.0)
        pooled = _pooled(jnp.concatenate([halo, cur], axis=0), cur, i * tm, tm)
        dcur = dc_ref[...]
        dnext = jnp.where(i < nt - 1, dn_ref[...], 0.0)
        dext = jnp.concatenate([dcur, dnext], axis=0)
        t_ext = i * tm + lax.broadcasted_iota(jnp.int32, (tm + HALO, GROUP), 0)
        for g, w in enumerate(WINDOWS):
            sl = slice(g * GROUP, (g + 1) * GROUP)
            pb = pooled[g].astype(BF16)
            wp = wp_ref[g]
            mixed = _dot(pb, wp)
            gsc_ref[:, sl] += jnp.sum(dcur[:, sl] * mixed, axis=0, keepdims=True)
            dmixed = (dext[:, sl] * sc_ref[:, sl]).astype(BF16)
            gwp_ref[g] += _dot_tn(pb, dmixed[0:tm])
            dpooled = _dot_nt(dmixed, wp)
            z = dpooled / jnp.minimum(t_ext + 1, w).astype(F32)
            du_ref[:, sl] = (_window_sums(z, w, -1, tm, 2) - dpooled[0:tm]).astype(BF16)

    return pl.pallas_call(
        body, name="pool_bwd", grid=(nt,),
        in_specs=[pl.BlockSpec((tm, 512), lambda i: (i, 0)),
                  pl.BlockSpec((HALO, 512), lambda i: (jnp.maximum(i * hb - 1, 0), 0)),
                  pl.BlockSpec((tm, 512), lambda i: (i, 0)),
                  pl.BlockSpec((HALO, 512), lambda i: (jnp.minimum((i + 1) * hb, last_halo), 0)),
                  pl.BlockSpec((4, GROUP, GROUP), lambda i: (0, 0, 0)),
                  pl.BlockSpec((1, 512), lambda i: (0, 0))] + ([] if dep is None else [ANY]),
        out_specs=[pl.BlockSpec((tm, 512), lambda i: (i, 0)),
                   pl.BlockSpec((4, GROUP, GROUP), lambda i: (0, 0, 0)),
                   pl.BlockSpec((1, 512), lambda i: (0, 0))],
        out_shape=[jax.ShapeDtypeStruct((s, 512), BF16), jax.ShapeDtypeStruct((4, GROUP, GROUP), F32),
                   jax.ShapeDtypeStruct((1, 512), F32)],
        compiler_params=_cp(1))(u, u, dpool, dpool, w_pool, pool_scale, *([] if dep is None else [dep]))


def _attn_bwd(q, k, v, do, bias, sinks, bucket, dep=None):
    s = q.shape[0]
    nblk = s // BLK

    def body(q_ref, do_ref, k_ref, v_ref, b_ref, sk_ref, bk_ref, *rest):
        dq_ref, dk_ref, dv_ref, grb_ref, gsk_ref, dss_ref = rest[-6:]
        n = pl.program_id(0)

        @pl.when(n == 0)
        def _():
            dk_ref[...] = jnp.zeros_like(dk_ref)
            dv_ref[...] = jnp.zeros_like(dv_ref)
            dss_ref[...] = jnp.zeros_like(dss_ref)
            gsk_ref[...] = jnp.zeros_like(gsk_ref)

        kk, (lo, pstart, cstart) = _kv_band(k_ref, n, False)
        kt, _ = _kv_band(k_ref, n, True)
        vv, _ = _kv_band(v_ref, n, False)
        bidx = jnp.minimum(n, 1)
        lo_q = lax.broadcasted_iota(jnp.int32, (BLK, 128), 1) < HD
        dkk = [jnp.zeros((2 * BLK, 128), F32), jnp.zeros((2 * BLK, 128), F32)]
        dvv = [jnp.zeros((2 * BLK, 128), F32), jnp.zeros((2 * BLK, 128), F32)]

        def by_head(t):
            return jnp.concatenate([jnp.where(lo_q, t, 0.0), jnp.where(lo_q, 0.0, t)], axis=0).astype(BF16)

        for p in range(4):
            h = p // 2
            qt = q_ref[:, p * 128:(p + 1) * 128].astype(F32) * SCALE
            dot = do_ref[:, p * 128:(p + 1) * 128]
            prob, ps = _pair_probs(qt.astype(BF16), kk[h], b_ref[bidx, pl.ds(2 * p, 2)], sk_ref, p)
            dp = _dot_nt(vv[h], dot).reshape(2, 2 * BLK, BLK)
            delta = jnp.sum(prob * dp, axis=1, keepdims=True)
            ds = prob * (dp - delta)
            sink_part = ps * delta
            for e in range(2):
                gs = -jnp.sum(sink_part[e]).reshape(1, 1)
                gsk_ref[pl.ds(2 * p + e, 1), :] += jnp.broadcast_to(gs, (1, 128))
            dss_ref[pl.ds(2 * p, 2)] += ds
            dsb = ds.astype(BF16)
            pb = prob.astype(BF16)
            dq_t = _dot(kt[h], dsb.reshape(4 * BLK, BLK))
            dq_ref[:, p * 128:(p + 1) * 128] = (dq_t.T * SCALE).astype(BF16)
            dkk[h] = dkk[h] + _dot(jnp.concatenate([dsb[0], dsb[1]], axis=1), by_head(qt))
            dvv[h] = dvv[h] + _dot(jnp.concatenate([pb[0], pb[1]], axis=1), by_head(dot.astype(F32)))
        for acc, ref in ((dkk, dk_ref), (dvv, dv_ref)):
            f0 = acc[0] + pltpu.roll(acc[0], HD, axis=1)
            f1 = acc[1] + pltpu.roll(acc[1], HD, axis=1)
            band = jnp.where(lo, f0, f1)
            ref[pl.ds(pstart, BLK), :] += band[0:BLK]
            ref[pl.ds(cstart, BLK), :] += band[BLK:2 * BLK]

        @pl.when(n == nblk - 1)
        def _():
            _relbias_grad(dss_ref, bk_ref[...], grb_ref)

    blk = pl.BlockSpec((BLK, 512), lambda i: (i, 0))
    kv = pl.BlockSpec((s, 128), lambda i: (0, 0))
    row8 = pl.BlockSpec((NQ, 128), lambda i: (0, 0))
    return pl.pallas_call(
        body, name="attn_bwd", grid=(nblk,),
        in_specs=[blk, blk, kv, kv, pl.BlockSpec((2, NQ, 2 * BLK, BLK), lambda i: (0, 0, 0, 0)),
                  pl.BlockSpec(memory_space=pltpu.SMEM), pl.BlockSpec((2 * BLK, BLK), lambda i: (0, 0))]
        + ([] if dep is None else [ANY]),
        out_specs=[blk, kv, kv, row8, row8],
        out_shape=[jax.ShapeDtypeStruct((s, 512), BF16), jax.ShapeDtypeStruct((s, 128), F32),
                   jax.ShapeDtypeStruct((s, 128), F32), jax.ShapeDtypeStruct((NQ, 128), F32),
                   jax.ShapeDtypeStruct((NQ, 128), F32)],
        scratch_shapes=[pltpu.VMEM((NQ, 2 * BLK, BLK), F32)],
        compiler_params=_cp(1))(q, do, k, v, bias, sinks, bucket, *([] if dep is None else [dep]))


def _relbias_grad(ds_ref, bucket_v, o_ref):
    lane = lax.broadcasted_iota(jnp.int32, (NQ, 128), 1)
    head_row = lax.broadcasted_iota(jnp.int32, (NQ, BLK), 0)
    acc = jnp.zeros((NQ, 128), F32)
    for b in range(NBUCKET):
        sel = bucket_v == b
        stack = jnp.zeros((NQ, BLK), F32)
        for h in range(NQ):
            col_sums = jnp.sum(jnp.where(sel, ds_ref[h], 0.0), axis=0, keepdims=True)
            stack = jnp.where(head_row == h, col_sums, stack)
        acc = acc + jnp.where(lane == b, jnp.sum(stack, axis=1, keepdims=True), 0.0)
    o_ref[...] = acc


def _inproj_bwd(x, g1, du, dq, dk, dv, w_in, dx1, c_arr, dep=None):
    s = x.shape[0]
    tm = min(TM, s)
    nt = s // tm
    cols = ((0, 512), (512, 1024), (1024, 1152), (1152, 1280))

    def body(c_ref, x_ref, g_ref, du_ref, dq_ref, dk_ref, dv_ref, w_ref, dx1_ref, *rest):
        gx_ref, own_ref, oth_ref, gg_ref, gw_ref = rest[-5:]
        i = pl.program_id(0)

        @pl.when(i == 0)
        def _():
            gw_ref[...] = jnp.zeros_like(gw_ref)
            gg_ref[...] = jnp.zeros_like(gg_ref)

        gv = g_ref[...]
        r1, n1 = _rms(x_ref[...])
        h = (n1 * gv).astype(BF16)
        parts = (du_ref[...], dq_ref[...], dk_ref[...].astype(BF16), dv_ref[...].astype(BF16))
        dh = None
        for (a, b), dpart in zip(cols, parts):
            t = _dot(dpart, w_ref[a:b, :])
            dh = t if dh is None else dh + t
            gw_ref[a:b, :] += _dot_tn(dpart, h)
        dx, dg = _rms_bwd(r1, n1, gv, dh)
        gg_ref[...] += dg
        gx_ref[...] = dx1_ref[...] + dx

        @pl.when(i == nt - 1)
        def _():
            for k in range(NSH):
                _emit_halves(gw_ref, k * WIN_S, WIN_S, c_ref[0], own_ref.at[k], oth_ref.at[k])

    row = lambda w: pl.BlockSpec((tm, w), lambda i: (i, 0))
    vec = pl.BlockSpec((1, D), lambda i: (0, 0))
    full = pl.BlockSpec((IN_W, D), lambda i: (0, 0))
    half = pl.BlockSpec((NSH, WIN_S // 2, D), lambda i: (0, 0, 0))
    return pl.pallas_call(
        body, name="inproj_bwd", grid=(nt,),
        in_specs=[SMEM_SPEC, row(D), vec, row(512), row(512), row(128), row(128), full, row(D)]
        + ([] if dep is None else [ANY]),
        out_specs=[row(D), half, half, vec],
        out_shape=[jax.ShapeDtypeStruct((s, D), F32)] + _half_shapes(WIN_S) + [jax.ShapeDtypeStruct((1, D), F32)],
        scratch_shapes=[pltpu.VMEM((IN_W, D), F32)],
        compiler_params=_cp(1))(c_arr, x, g1, du, dq, dk, dv, w_in, dx1, *([] if dep is None else [dep]))


def _local_step(x, target, small, w_in, w_out, ffn_weights, c_arr, on_ffn_grads=None, on_wout_grads=None,
                on_attn_grads=None):
    g1, g2, g3, g4, w_pool, pool_scale, rel_bias, sinks = small
    bucket = jnp.asarray(_bucket_table())
    wp_bf = w_pool.astype(BF16)
    bias = _bias_table(bucket, rel_bias)
    h1 = _prenorm(x, g1)
    if callable(w_in):
        w_in = w_in(bias, h1)
    u, q, k, v = _inproj_fwd(h1, w_in)
    pool_o = _pool_fwd(u, wp_bf, pool_scale)
    attn_o = _attn_fwd(q, k, v, bias, sinks)
    g2_fwd = g2
    if callable(w_out):
        w_out, after = w_out(pool_o, attn_o)
        g2_fwd = g2 + after[:1, :1]
    mix, x1, h2 = _outproj_fwd(pool_o, attn_o, w_out, x, g2_fwd, g3)
    wg, wu, wd = ffn_weights(h2) if callable(ffn_weights) else ffn_weights
    gate, up, df, dy, loss, gg4 = _ffn_fwd(h2, wg, wu, wd, x1, target, g4)
    dgate, dup, dx1, dmix, gg3, gg2 = _ffn_bwd_act(df, gate, up, wg, wu, wd, dy, x1, mix, g3, g2)
    ffn_g = _ffn_bwd_w(h2, gate, up, dgate, dup, df, c_arr)
    dep = None if on_ffn_grads is None else on_ffn_grads(ffn_g)
    dpool, dattn, wout_own, wout_oth = _outproj_bwd(dmix, w_out, pool_o, attn_o, c_arr, dep)
    dep = None if on_wout_grads is None else on_wout_grads(wout_own, wout_oth, dpool)
    du, gwp, gsc = _pool_bwd(u, dpool, wp_bf, pool_scale, dep)
    dq, dk, dv, grb, gsk = _attn_bwd(q, k, v, dattn, bias, sinks, bucket, dep)
    dep = None if on_attn_grads is None else on_attn_grads([du, dq])
    gx, win_own, win_oth, gg1 = _inproj_bwd(x, g1, du, dq, dk, dv, w_in, dx1, c_arr, dep)
    small_grads = (gg1, gg2, gg3, gg4, gwp, gsc, grb[:, :NBUCKET].T, gsk[:, 0].reshape(1, NQ))
    return loss, gx, small_grads, ((win_own, win_oth), (wout_own, wout_oth), ffn_g)


def _place():
    x, y, c = lax.axis_index("x"), lax.axis_index("y"), lax.axis_index("c")
    chips = ((1 - x, y), (x, 1 - y), (1 - x, 1 - y))
    return x, y, c, chips


def _remote(src, dst, ssem, rsem, dev):
    return pltpu.make_async_remote_copy(src_ref=src, dst_ref=dst, send_sem=ssem, recv_sem=rsem,
                                        device_id=dev, device_id_type=MESH_T)


def _to_sibling(arrs, name):
    nt = len(arrs)

    def body(*refs):
        srcs, dsts = refs[:nt], refs[nt:2 * nt]
        ssem, rsem = refs[2 * nt:]
        x, y, c, _ = _place()
        cps = [_remote(srcs[t], dsts[t], ssem.at[t], rsem.at[t], (x, y, 1 - c)) for t in range(nt)]
        for cp in cps:
            cp.start()
        for cp in cps:
            cp.wait()

    return pl.pallas_call(
        body, name=name, in_specs=[ANY] * nt, out_specs=[ANY] * nt,
        out_shape=[jax.ShapeDtypeStruct(a.shape, a.dtype) for a in arrs],
        scratch_shapes=[pltpu.SemaphoreType.DMA((nt,)), pltpu.SemaphoreType.DMA((nt,))],
        compiler_params=_cp())(*arrs)


HBM_SPEC = pl.BlockSpec(memory_space=pltpu.HBM)
SEM_SPEC = pl.BlockSpec(memory_space=pltpu.SEMAPHORE)
DATAFLOW = pltpu.SideEffectType.DATAFLOW_SIDE_EFFECTING


def _gather_copies(srcs, lands, ssem, rsem):
    x, y, c, chips = _place()
    me = 2 * x + y
    out = []
    for t in range(len(srcs)):
        half = srcs[t].shape[0] // 2
        mine = pl.ds(pl.multiple_of(c * half, 16), half)
        for j, (px, py) in enumerate(chips):
            k = 3 * t + j
            send = _remote(srcs[t].at[mine], lands[t].at[me, mine], ssem.at[k], rsem.at[k], (px, py, c))
            blk = lands[t].at[2 * px + py, mine]
            out.append((send, _remote(blk, blk, ssem.at[k], rsem.at[k], (px, py, c))))
    return out


def _scatter_copies(srcs, lands, ssem, rsem):
    x, y, c, chips = _place()
    out = []
    for t in range(len(srcs)):
        for j, (px, py) in enumerate(chips):
            k = 3 * t + j
            send = _remote(srcs[t].at[2 * px + py], lands[t].at[j], ssem.at[k], rsem.at[k], (px, py, c))
            out.append((send, _remote(lands[t].at[j], lands[t].at[j], ssem.at[k], rsem.at[k], (px, py, c))))
    return out


def _sibling_copies(srcs, lands, ssem, rsem):
    x, y, c, _ = _place()
    sib = (x, y, 1 - c)
    return [(_remote(srcs[t], lands[t], ssem.at[t], rsem.at[t], sib),
             _remote(lands[t], lands[t], ssem.at[t], rsem.at[t], sib)) for t in range(len(srcs))]


def _forward_copies(srcs, lands, ssem, rsem):
    x, y, c, chips = _place()
    sib = (x, y, 1 - c)
    out = []
    for t in range(len(lands)):
        half = lands[t].shape[1] // 2
        mine = pl.ds(pl.multiple_of(c * half, 16), half)
        other = pl.ds(pl.multiple_of((1 - c) * half, 16), half)
        for j, (px, py) in enumerate(chips):
            k = 3 * t + j
            blk_m, blk_o = lands[t].at[2 * px + py, mine], lands[t].at[2 * px + py, other]
            out.append((_remote(blk_m, blk_m, ssem.at[k], rsem.at[k], sib),
                        _remote(blk_o, blk_o, ssem.at[k], rsem.at[k], sib)))
    return out


def _peer_copies(srcs, lands, ssem, rsem):
    x, y, c, _ = _place()
    me = 4 * x + 2 * y + c
    out = []
    for kk in range(1, 8):
        px, py, pc = x ^ (kk >> 2), y ^ ((kk >> 1) & 1), c ^ (kk & 1)
        send = _remote(srcs[0], lands[0].at[me], ssem.at[kk - 1], rsem.at[kk - 1], (px, py, pc))
        slot = lands[0].at[4 * px + 2 * py + pc]
        out.append((send, _remote(slot, slot, ssem.at[kk - 1], rsem.at[kk - 1], (px, py, pc))))
    return out


def _split_start(plan, ncopy, srcs, lands, after, name):
    ns, nbuf = len(srcs), len(srcs) + len(lands)
    extra = [] if after is None else [after]
    n_in = nbuf + len(extra)

    def body(*refs):
        ssem, rsem, token = refs[n_in], refs[n_in + 1], refs[-1]
        for send, _ in plan(refs[:ns], refs[ns:nbuf], ssem, rsem):
            send.start()
        token[...] = jnp.zeros_like(token)

    bufs = [pltpu.with_memory_space_constraint(a, pltpu.HBM) for a in list(srcs) + list(lands)]
    outs = pl.pallas_call(
        body, name=name, in_specs=[HBM_SPEC] * nbuf + [ANY] * len(extra),
        out_specs=[SEM_SPEC, SEM_SPEC] + [HBM_SPEC] * nbuf + [pl.BlockSpec(memory_space=pltpu.VMEM)],
        out_shape=[pltpu.SemaphoreType.DMA((ncopy,)), pltpu.SemaphoreType.DMA((ncopy,))]
        + [pltpu.HBM(a.shape, a.dtype) for a in bufs] + [jax.ShapeDtypeStruct((8, 128), F32)],
        input_output_aliases={i: 2 + i for i in range(nbuf)},
        compiler_params=pltpu.CompilerParams(has_side_effects=DATAFLOW))(*bufs, *extra)
    return outs[0], outs[1], outs[2:2 + ns], outs[2 + ns:2 + nbuf], outs[-1]


def _split_wait(plan, ssem, rsem, srcs, lands, after, name, only=None):
    ns, nbuf = len(srcs), len(srcs) + len(lands)

    def body(*refs):
        s_ref, r_ref = refs[nbuf], refs[nbuf + 1]
        for k, (send, recv) in enumerate(plan(refs[:ns], refs[ns:nbuf], s_ref, r_ref)):
            if only is None or k in only:
                send.wait_send()
                recv.wait_recv()

    outs = pl.pallas_call(
        body, name=name, in_specs=[HBM_SPEC] * nbuf + [SEM_SPEC, SEM_SPEC] + [ANY] * len(after),
        out_specs=[HBM_SPEC] * nbuf,
        out_shape=[pltpu.HBM(a.shape, a.dtype) for a in list(srcs) + list(lands)],
        input_output_aliases={i: i for i in range(nbuf)},
        compiler_params=pltpu.CompilerParams(has_side_effects=DATAFLOW))(*srcs, *lands, ssem, rsem, *after)
    return outs


def _gather_finish(lands):
    nt = len(lands)

    def body(*refs):
        outs = refs[nt:2 * nt]
        dsend, drecv = refs[2 * nt:]
        x, y, c, chips = _place()
        sib = (x, y, 1 - c)
        sends = []
        for t in range(nt):
            half = outs[t].shape[1] // 2
            mine = pl.ds(pl.multiple_of(c * half, 16), half)
            for j, (px, py) in enumerate(chips):
                blk = outs[t].at[2 * px + py, mine]
                cp = _remote(blk, blk, dsend.at[t, j], drecv.at[t, j], sib)
                cp.start()
                sends.append(cp)
        for t in range(nt):
            half = outs[t].shape[1] // 2
            other = pl.ds(pl.multiple_of((1 - c) * half, 16), half)
            for j, (px, py) in enumerate(chips):
                blk = outs[t].at[2 * px + py, other]
                _remote(blk, blk, dsend.at[t, j], drecv.at[t, j], sib).wait_recv()
        for cp in sends:
            cp.wait_send()

    return pl.pallas_call(
        body, name="gather_finish", in_specs=[ANY] * nt, out_specs=[ANY] * nt,
        out_shape=[jax.ShapeDtypeStruct(a.shape, a.dtype) for a in lands],
        input_output_aliases={t: t for t in range(nt)},
        scratch_shapes=[pltpu.SemaphoreType.DMA((nt, 3)), pltpu.SemaphoreType.DMA((nt, 3))],
        compiler_params=_cp())(*lands)


def _chip_partial(owns, recv, chip_arr, tag):
    nt = len(owns)

    def body(chip_ref, *refs):
        k = pl.program_id(0)
        for t in range(nt):
            p = refs[t][...] + refs[nt + t][...].astype(F32)
            refs[2 * nt + t][...] = p.astype(BF16)

            @pl.when(k == chip_ref[0])
            def _():
                refs[3 * nt + t][...] = p

    blk_specs = [pl.BlockSpec((None,) + a.shape[1:], lambda k, chip_ref: (k, 0, 0)) for a in owns]
    own_specs = [pl.BlockSpec(a.shape[1:], lambda k, chip_ref: (0, 0)) for a in owns]
    return pl.pallas_call(
        body, name="rs_chip_partial_" + tag,
        grid_spec=pltpu.PrefetchScalarGridSpec(num_scalar_prefetch=1, grid=(NSH,), in_specs=blk_specs * 2,
                                               out_specs=blk_specs + own_specs),
        out_shape=[jax.ShapeDtypeStruct(a.shape, BF16) for a in owns]
        + [jax.ShapeDtypeStruct(a.shape[1:], F32) for a in owns],
        compiler_params=_cp(1))(chip_arr, *owns, *recv)


def _sum_chips(own, recv, tag, after=()):
    nt = len(own)

    def body(*refs):
        outs = refs[2 * nt + len(after):]
        for t in range(nt):
            r = refs[nt + t]
            outs[t][...] = ((refs[t][...] + r[0].astype(F32)) + r[1].astype(F32)) + r[2].astype(F32)

    vm = pl.BlockSpec(memory_space=pltpu.VMEM)
    return pl.pallas_call(
        body, name="rs_sum_chips_" + tag, in_specs=[vm] * (2 * nt) + [ANY] * len(after), out_specs=[vm] * nt,
        out_shape=[jax.ShapeDtypeStruct(a.shape, F32) for a in own],
        compiler_params=_cp())(*own, *recv, *after)


def _adamw_math(w, g, m, v):
    m = ADAM_B1 * m + (1.0 - ADAM_B1) * g
    v = ADAM_B2 * v + (1.0 - ADAM_B2) * (g * g)
    m_hat = m / (1.0 - ADAM_B1 ** ADAM_STEP)
    v_hat = v / (1.0 - ADAM_B2 ** ADAM_STEP)
    delta = -ADAM_LR * (m_hat / (jnp.sqrt(v_hat) + ADAM_EPS) + ADAM_WD * w)
    return delta, m, v


ADAM_SPLIT = 4


def _adamw_shards(own, sib, ws, ms, vs, c_arr, tag):
    nt = len(own)

    def body(c_ref, *refs):
        hh = pl.program_id(0)
        mine = hh == c_ref[0]
        for t in range(nt):
            g = jnp.where(mine, refs[t][...], refs[nt + t][...])
            delta, m, v = _adamw_math(refs[2 * nt + t][...], g, refs[3 * nt + t][...], refs[4 * nt + t][...])
            refs[5 * nt + t][...] = g
            refs[6 * nt + t][...] = delta
            refs[7 * nt + t][...] = m
            refs[8 * nt + t][...] = v

    def tile(a):
        return (a.shape[0] // ADAM_SPLIT, a.shape[1])

    half_specs = [pl.BlockSpec(tile(a), lambda hh, q, c_ref: (q, 0)) for a in own]
    full_specs = [pl.BlockSpec(tile(a), lambda hh, q, c_ref: (hh * ADAM_SPLIT + q, 0)) for a in own]
    return pl.pallas_call(
        body, name="adamw_shards_" + tag,
        grid_spec=pltpu.PrefetchScalarGridSpec(num_scalar_prefetch=1, grid=(2, ADAM_SPLIT),
                                               in_specs=half_specs * 2 + full_specs * 3, out_specs=full_specs * 4),
        out_shape=[jax.ShapeDtypeStruct(w.shape, F32) for w in ws] * 4,
        compiler_params=_cp(2))(c_arr, *own, *sib, *ws, *ms, *vs)


SMALL_WPOOL_ROW = 32
SMALL_SCALE_ROW = SMALL_WPOOL_ROW + 4 * GROUP
SMALL_BIAS_ROW = SMALL_SCALE_ROW + 8
SMALL_SINKS_ROW = SMALL_BIAS_ROW + NBUCKET
SMALL_LOSS_ROW = SMALL_SINKS_ROW + 8


def _small_sum_adamw(gathered, wp, mp, vp):
    rows = wp.shape[0]

    def body(g_ref, w_ref, m_ref, v_ref, *rest):
        outs, res = rest[:-1], rest[-1]
        g = g_ref[0]
        for d in range(1, 8):
            g = g + g_ref[d]
        delta, m, v = _adamw_math(w_ref[...], g, m_ref[...], v_ref[...])
        for kind, val in enumerate((g, delta, m, v)):
            res[kind] = val
            gains = outs[8 * kind:8 * kind + 4]
            w_pool_o, scale_o, bias_o, sinks_o = outs[8 * kind + 4:8 * kind + 8]
            for t in range(4):
                for i in range(8):
                    gains[t][:, 128 * i:128 * (i + 1)] = res[kind, pl.ds(8 * t + i, 1), :]
            for grp in range(4):
                w_pool_o[grp] = res[kind, pl.ds(SMALL_WPOOL_ROW + GROUP * grp, GROUP), :]
            for i in range(4):
                scale_o[:, 128 * i:128 * (i + 1)] = res[kind, pl.ds(SMALL_SCALE_ROW + i, 1), :]
            bias_o[...] = res[kind, pl.ds(SMALL_BIAS_ROW, NBUCKET), :][:, 0:NQ]
            sinks_o[...] = res[kind, pl.ds(SMALL_SINKS_ROW, 1), :][:, 0:NQ]
        outs[-1][...] = res[0, pl.ds(SMALL_LOSS_ROW, 1), :]

    vm = pl.BlockSpec(memory_space=pltpu.VMEM)
    one_kind = [jax.ShapeDtypeStruct((1, D), F32)] * 4 + [
        jax.ShapeDtypeStruct((4, GROUP, GROUP), F32), jax.ShapeDtypeStruct((1, POOL_W), F32),
        jax.ShapeDtypeStruct((NBUCKET, NQ), F32), jax.ShapeDtypeStruct((1, NQ), F32)]
    return pl.pallas_call(
        body, name="small_sum_adamw", in_specs=[vm] * 4, out_specs=[vm] * 33,
        out_shape=one_kind * 4 + [jax.ShapeDtypeStruct((1, 128), F32)],
        scratch_shapes=[pltpu.VMEM((4, rows, 128), F32)],
        compiler_params=_cp())(gathered, wp, mp, vp)


def _pack_small(gains4, w_pool, pool_scale, rel_bias, sinks, loss):
    parts = list(gains4) + [w_pool, pool_scale, jnp.pad(rel_bias, ((0, 0), (0, 128 - NQ))), sinks, loss]
    rows = []
    for a in parts:
        flat = a.reshape(-1)
        n = flat.shape[0]
        padded = -(-n // 1024) * 1024
        rows.append(jnp.pad(flat, (0, padded - n)).reshape(-1, 128))
    return jnp.concatenate(rows, axis=0)


def kernel(x, g_pre_mix, w_in, w_pool, pool_scale, rel_bias, sinks, w_out, g_post_mix, g_pre_ffn, w_gate, w_up, w_down, g_post_ffn, loss_target, m_g_pre_mix, m_w_in, m_w_pool, m_pool_scale, m_rel_bias, m_sinks, m_w_out, m_g_post_mix, m_g_pre_ffn, m_w_gate, m_w_up, m_w_down, m_g_post_ffn, v_g_pre_mix, v_w_in, v_w_pool, v_pool_scale, v_rel_bias, v_sinks, v_w_out, v_g_post_mix, v_g_pre_ffn, v_w_gate, v_w_up, v_w_down, v_g_post_ffn):
    c = lax.axis_index("c")
    chip = 2 * lax.axis_index("x") + lax.axis_index("y")

    def shards(a_in, a_out, a_gate, a_up, a_down):
        return (a_in[0].T, a_out[0], a_gate[0].T, a_up[0].T, a_down[0])

    c_arr = c.reshape(1).astype(jnp.int32)
    chip_arr = chip.reshape(1).astype(jnp.int32)

    big_w = shards(w_in, w_out, w_gate, w_up, w_down)
    big_bf = [w.astype(BF16) for w in big_w]
    g_ssem, g_rsem, g_srcs, g_lands, _ = _split_start(
        _gather_copies, 15, big_bf, [lax.empty((NSH,) + a.shape, BF16) for a in big_bf], None, "gather_start")
    fw = {}

    def with_own(land, shard):
        return lax.dynamic_update_slice(land, shard[None], (chip, 0, 0))

    def w_in_ready(*after):
        fw["first"] = _split_wait(_gather_copies, g_ssem, g_rsem, g_srcs, g_lands, after, "gather_win_wait",
                                  only=(0, 1, 2))
        (fw["win"],) = _gather_finish([with_own(fw["first"][5], fw["first"][0])])
        return fw["win"].reshape(IN_W, D)

    def w_out_ready(*after):
        first = fw["first"]
        outs = _split_wait(_gather_copies, g_ssem, g_rsem, first[:5], [fw["win"]] + list(first[6:]), after,
                           "gather_rest_wait", only=tuple(range(3, 15)))
        lands = [with_own(land, src) for src, land in zip(outs[1:5], outs[6:])]
        (wout_all,) = _gather_finish(lands[:1])
        fw["f"] = _split_start(_forward_copies, 9, [], lands[1:], wout_all, "gather_forward_start")
        return wout_all.reshape(D, D), fw["f"][4]

    def ffn_weights(after):
        ssem, rsem, _, lands, _ = fw["f"]
        return _split_wait(_forward_copies, ssem, rsem, [], lands, [after], "gather_forward_wait")

    rs = {}

    def on_ffn_grads(ffn_g):
        oths = ffn_g[1::2]
        rs["ffn_own"] = ffn_g[0::2]
        rs["x"] = _split_start(_sibling_copies, 3, oths, [lax.empty(a.shape, BF16) for a in oths], ffn_g[0],
                               "rs_exchange_ffn_start")
        return rs["x"][4]

    def on_wout_grads(own, oth, after):
        ssem, rsem, srcs, lands, _ = rs["x"]
        recv_ffn = _split_wait(_sibling_copies, ssem, rsem, srcs, lands, [after], "rs_exchange_ffn_wait")[3:]
        recv_wout = _to_sibling([oth], "rs_exchange_wout")
        outs = _chip_partial([own] + list(rs["ffn_own"]), list(recv_wout) + list(recv_ffn), chip_arr, "early")
        rs["early_own"] = outs[4:]
        rs["s"] = _split_start(_scatter_copies, 12, outs[:4],
                               [lax.empty((3,) + a.shape[1:], BF16) for a in outs[:4]], outs[4], "scatter_early_start")
        return rs["s"][4]

    def on_attn_grads(after):
        ssem, rsem, srcs, lands, _ = rs["s"]
        recv_early = _split_wait(_scatter_copies, ssem, rsem, srcs, lands, after, "scatter_early_wait")[4:]
        finals = _sum_chips(list(rs["early_own"]), list(recv_early), "early")
        rs["sh"] = _split_start(_sibling_copies, 4, finals, [lax.empty(a.shape, F32) for a in finals], finals[0],
                                "rs_share_early_start")
        return rs["sh"][4]

    small = (g_pre_mix, g_post_mix, g_pre_ffn, g_post_ffn, w_pool[0], pool_scale, rel_bias, sinks)
    loss, gx, small_grads, ((win_own, win_oth), _, _) = _local_step(
        x[0], loss_target[0], small, w_in_ready, w_out_ready, ffn_weights, c_arr,
        on_ffn_grads, on_wout_grads, on_attn_grads)

    unused = jnp.zeros((1, 1), F32)
    gp = _pack_small(small_grads[:4], *small_grads[4:], loss)
    wp = _pack_small((g_pre_mix, g_post_mix, g_pre_ffn, g_post_ffn), w_pool, pool_scale, rel_bias, sinks, unused)
    mp = _pack_small((m_g_pre_mix, m_g_post_mix, m_g_pre_ffn, m_g_post_ffn), m_w_pool, m_pool_scale, m_rel_bias,
                     m_sinks, unused)
    vp = _pack_small((v_g_pre_mix, v_g_post_mix, v_g_pre_ffn, v_g_post_ffn), v_w_pool, v_pool_scale, v_rel_bias,
                     v_sinks, unused)

    moments = (shards(m_w_in, m_w_out, m_w_gate, m_w_up, m_w_down),
               shards(v_w_in, v_w_out, v_w_gate, v_w_up, v_w_down))
    recv_a = _to_sibling([win_oth], "rs_exchange_win")
    outs = _chip_partial([win_own], recv_a, chip_arr, "win")
    w_ssem, w_rsem, w_srcs, w_lands, w_token = _split_start(
        _scatter_copies, 3, outs[:1], [lax.empty((3,) + outs[0].shape[1:], BF16)], gx, "scatter_win_start")
    slots = lax.dynamic_update_slice(lax.empty((8,) + gp.shape, F32), gp[None], (2 * chip + c, 0, 0))
    p_ssem, p_rsem, p_srcs, p_lands, p_token = _split_start(_peer_copies, 7, [gp], [slots], w_token,
                                                            "small_allgather_start")
    ssem, rsem, srcs, lands, _ = rs["sh"]
    shared = _split_wait(_sibling_copies, ssem, rsem, srcs, lands, [p_token], "rs_share_early_wait")
    adam_e = _adamw_shards(shared[:4], shared[4:], big_w[1:], moments[0][1:], moments[1][1:], c_arr, "early")
    recv_win = _split_wait(_scatter_copies, w_ssem, w_rsem, w_srcs, w_lands, [adam_e[0]], "scatter_win_wait")[1:]
    win_final = _sum_chips([outs[1]], recv_win, "win")
    win_sib = _to_sibling(win_final, "rs_share_win")
    adam_w = _adamw_shards(win_final, win_sib, big_w[:1], moments[0][:1], moments[1][:1], c_arr, "win")
    big_g, big_d, big_m, big_v = [[adam_w[i]] + list(adam_e[4 * i:4 * i + 4]) for i in range(4)]

    gathered = _split_wait(_peer_copies, p_ssem, p_rsem, p_srcs, p_lands, [adam_w[0]], "small_allgather_wait")[1]
    flat = _small_sum_adamw(gathered, wp, mp, vp)
    small_out = [flat[8 * kind:8 * kind + 8] for kind in range(4)]
    total_loss = flat[-1][0, 0]

    def ordered(small8, big4):
        sg1, sg2, sg3, sg4, swp, ssc, srb, ssk = small8
        swp = swp[None]
        bwin, bwout, bwg, bwu, bwd = big4[0].T[None], big4[1][None], big4[2].T[None], big4[3].T[None], big4[4][None]
        return [sg1, bwin, swp, ssc, srb, ssk, bwout, sg2, sg3, bwg, bwu, bwd, sg4]

    res = [total_loss, gx[None]]
    for sm, bg in zip(small_out, (big_g, big_d, big_m, big_v)):
        res += ordered(sm, bg)
    return tuple(res)
```

```python
import numpy as np
import jax
import jax.numpy as jnp
from jax import lax
from jax.experimental import pallas as pl
from jax.experimental.pallas import tpu as pltpu

F32 = jnp.float32
BF16 = jnp.bfloat16

D = 1024
POOL_W = 512
GROUP = 128
WINDOWS = (2, 4, 8, 16)
HALO = 128
NQ = 8
BLK = 128
NBUCKET = 32
IN_W = 1280
DFF = 2816
NSH = 4
FS = DFF // NSH
WIN_S = IN_W // NSH
WOUT_S = D // NSH
EPS = 1e-6
NEG = -1e30
SCALE = 0.125

ADAM_LR = 0.001
ADAM_B1 = 0.9
ADAM_B2 = 0.999
ADAM_EPS = 1e-08
ADAM_WD = 0.01
ADAM_STEP = 10

TM = 512
TM_POOL = 256
TM_FFN = 512
TM_FFN_FWD = 1024
FFN_ROWS = 256
VMEM_LIMIT = 60 * 1024 * 1024

MESH_T = pl.DeviceIdType.MESH
ANY = pl.BlockSpec(memory_space=pl.ANY)


def _cp(n_grid=0, **kw):
    sem = ("arbitrary",) * n_grid if n_grid else None
    return pltpu.CompilerParams(dimension_semantics=sem, vmem_limit_bytes=VMEM_LIMIT, **kw)


def _dot(a, b):
    return jnp.dot(a, b, preferred_element_type=F32)


def _dot_nt(a, b):
    return lax.dot_general(a, b, (((1,), (1,)), ((), ())), preferred_element_type=F32)


def _dot_tn(a, b):
    return lax.dot_general(a, b, (((0,), (0,)), ((), ())), preferred_element_type=F32)


def _rms(x):
    r = lax.rsqrt(jnp.mean(x * x, axis=-1, keepdims=True) + EPS)
    return r, x * r


def _rms_bwd(r, n, g, dout):
    dn = dout * g
    dx = r * (dn - n * jnp.mean(dn * n, axis=-1, keepdims=True))
    dg = jnp.sum(dout * n, axis=0, keepdims=True)
    return dx, dg


def _split(x, n):
    parts = []
    r = x
    for _ in range(n):
        p = r.astype(BF16)
        parts.append(p)
        r = r - p.astype(F32)
    return parts


def _band_dot(a, x, n):
    acc = None
    for p in _split(x, n):
        t = _dot(a, p)
        acc = t if acc is None else acc + t
    return acc


def _bucket_table():
    qi = np.arange(BLK)[None, :]
    kj = np.arange(2 * BLK)[:, None]
    dist = qi + BLK - kj
    n = np.maximum(dist, 0)
    nf = np.maximum(n, 1).astype(np.float32)
    large = 16 + (np.log(nf / np.float32(16)) / np.float32(np.log(128 / 16)) * np.float32(16)).astype(np.int32)
    large = np.minimum(large, NBUCKET - 1)
    return np.where(n < 16, n, large).astype(np.int32)


def _prenorm(x, g1):
    s = x.shape[0]
    tm = min(TM, s)

    def body(x_ref, g_ref, h_ref):
        _, n = _rms(x_ref[...])
        h_ref[...] = (n * g_ref[...]).astype(BF16)

    row = pl.BlockSpec((tm, D), lambda i: (i, 0))
    return pl.pallas_call(
        body, name="prenorm", grid=(s // tm,),
        in_specs=[row, pl.BlockSpec((1, D), lambda i: (0, 0))], out_specs=row,
        out_shape=jax.ShapeDtypeStruct((s, D), BF16),
        compiler_params=_cp(1))(x, g1)


def _inproj_fwd(h1, w_in):
    s = h1.shape[0]
    tm = min(TM, s)

    def body(h_ref, w_ref, u_ref, q_ref, k_ref, v_ref):
        proj = _dot_nt(h_ref[...], w_ref[...])
        u_ref[...] = proj[:, :512]
        q_ref[...] = proj[:, 512:1024].astype(BF16)
        k_ref[...] = proj[:, 1024:1152].astype(BF16)
        v_ref[...] = proj[:, 1152:1280].astype(BF16)

    row = lambda w: pl.BlockSpec((tm, w), lambda i: (i, 0))
    return pl.pallas_call(
        body, name="inproj_fwd", grid=(s // tm,),
        in_specs=[row(D), pl.BlockSpec((IN_W, D), lambda i: (0, 0))],
        out_specs=[row(512), row(512), row(128), row(128)],
        out_shape=[jax.ShapeDtypeStruct((s, 512), F32), jax.ShapeDtypeStruct((s, 512), BF16),
                   jax.ShapeDtypeStruct((s, 128), BF16), jax.ShapeDtypeStruct((s, 128), BF16)],
        compiler_params=_cp(1))(h1, w_in)


def _window_sums(ext, w, lag_sign, tm, terms):
    rows = lax.broadcasted_iota(jnp.int32, (HALO, 2 * HALO), 0)
    cols = lax.broadcasted_iota(jnp.int32, (HALO, 2 * HALO), 1)
    d = rows + HALO - cols if lag_sign > 0 else cols - rows
    band = jnp.where((d >= 0) & (d < w), 1.0, 0.0).astype(BF16)
    return jnp.concatenate([_band_dot(band, ext[r:r + 2 * HALO], terms) for r in range(0, tm, HALO)], axis=0)


def _pooled(ext, cur, t0, tm):
    t = t0 + lax.broadcasted_iota(jnp.int32, (tm, GROUP), 0)
    out = []
    for g, w in enumerate(WINDOWS):
        sl = slice(g * GROUP, (g + 1) * GROUP)
        cnt = jnp.minimum(t + 1, w).astype(F32)
        out.append(_window_sums(ext[:, sl], w, 1, tm, 2) / cnt - cur[:, sl])
    return out


def _pool_fwd(u, w_pool, pool_scale):
    s = u.shape[0]
    tm = min(TM_POOL, s)
    hb = tm // HALO

    def body(uc_ref, uh_ref, wp_ref, sc_ref, o_ref, pooled_ref):
        i = pl.program_id(0)
        cur = uc_ref[...]
        halo = jnp.where(i > 0, uh_ref[...], 0.0)
        ext = jnp.concatenate([halo, cur], axis=0)
        pooled = _pooled(ext, cur, i * tm, tm)
        for g in range(4):
            sl = slice(g * GROUP, (g + 1) * GROUP)
            pb = pooled[g].astype(BF16)
            pooled_ref[:, sl] = pb
            o_ref[:, sl] = (_dot(pb, wp_ref[g]) * sc_ref[:, sl]).astype(BF16)

    row = pl.BlockSpec((tm, 512), lambda i: (i, 0))
    return pl.pallas_call(
        body, name="pool_fwd", grid=(s // tm,),
        in_specs=[row, pl.BlockSpec((HALO, 512), lambda i: (jnp.maximum(i * hb - 1, 0), 0)),
                  pl.BlockSpec((4, GROUP, GROUP), lambda i: (0, 0, 0)),
                  pl.BlockSpec((1, 512), lambda i: (0, 0))],
        out_specs=[row, row],
        out_shape=[jax.ShapeDtypeStruct((s, 512), BF16)] * 2,
        compiler_params=_cp(1))(u, u, w_pool, pool_scale)


def _bias_table(bucket, rel_bias):
    def body(b_ref, rb_ref, o_ref):
        bucket_v = b_ref[...]
        key = lax.broadcasted_iota(jnp.int32, (2 * BLK, BLK), 0)
        dist = lax.broadcasted_iota(jnp.int32, (2 * BLK, BLK), 1) + BLK - key
        inwin = (dist >= 0) & (dist < BLK)
        for h in range(NQ):
            acc = jnp.zeros((2 * BLK, BLK), F32)
            for b in range(NBUCKET):
                acc = jnp.where(bucket_v == b, rb_ref[b, h], acc)
            rest = jnp.where(inwin, acc, NEG)
            o_ref[1, h] = rest
            o_ref[0, h] = jnp.where(key >= BLK, rest, NEG)

    return pl.pallas_call(
        body, name="bias_table",
        in_specs=[pl.BlockSpec(memory_space=pltpu.VMEM), pl.BlockSpec(memory_space=pltpu.SMEM)],
        out_specs=pl.BlockSpec(memory_space=pltpu.VMEM),
        out_shape=jax.ShapeDtypeStruct((2, NQ, 2 * BLK, BLK), F32),
        compiler_params=_cp())(bucket, rel_bias)


HD = 64


def _kv_band(ref, n, transposed):
    pstart = pl.multiple_of(jnp.maximum(n - 1, 0) * BLK, BLK)
    cstart = pl.multiple_of(n * BLK, BLK)
    lo = lax.broadcasted_iota(jnp.int32, (2 * BLK, 128), 1) < HD
    band = jnp.concatenate([ref[pl.ds(pstart, BLK), :], ref[pl.ds(cstart, BLK), :]], axis=0).astype(F32)
    rot = pltpu.roll(band, HD, axis=1)
    halves = ((jnp.where(lo, band, 0.0), jnp.where(lo, 0.0, rot)), (jnp.where(lo, rot, 0.0), jnp.where(lo, 0.0, band)))
    if transposed:
        out = [jnp.concatenate([a.T, b.T], axis=1).astype(BF16) for a, b in halves]
    else:
        out = [jnp.concatenate([a, b], axis=0).astype(BF16) for a, b in halves]
    return out, (lo, pstart, cstart)


def _pair_probs(qt, kk, bias, sk_ref, p):
    sink = jnp.concatenate([jnp.full((1, 1, BLK), sk_ref[0, 2 * p + e], F32) for e in range(2)], axis=0)
    s = _dot_nt(kk, qt).reshape(2, 2 * BLK, BLK) + bias
    m = jnp.maximum(jnp.max(s, axis=1, keepdims=True), sink)
    pr = jnp.exp(s - m)
    es = jnp.exp(sink - m)
    inv = 1.0 / (jnp.sum(pr, axis=1, keepdims=True) + es)
    return pr * inv, es * inv


def _attn_fwd(q, k, v, bias, sinks):
    s = q.shape[0]
    nblk = s // BLK

    def body(q_ref, k_ref, v_ref, b_ref, sk_ref, o_ref, prob_ref, ps_ref):
        n = pl.program_id(0)
        kk, _ = _kv_band(k_ref, n, False)
        vt, _ = _kv_band(v_ref, n, True)
        bidx = jnp.minimum(n, 1)
        for p in range(4):
            h = p // 2
            qt = (q_ref[:, p * 128:(p + 1) * 128].astype(F32) * SCALE).astype(BF16)
            prob, ps = _pair_probs(qt, kk[h], b_ref[bidx, pl.ds(2 * p, 2)], sk_ref, p)
            pb = prob.astype(BF16)
            prob_ref[pl.ds(2 * p, 2)] = pb
            ps_ref[pl.ds(2 * p, 2), :] = ps.reshape(2, BLK)
            acc_t = _dot(vt[h], pb.reshape(4 * BLK, BLK))
            o_ref[:, p * 128:(p + 1) * 128] = acc_t.T.astype(BF16)

    return pl.pallas_call(
        body, name="attn_fwd", grid=(nblk,),
        in_specs=[pl.BlockSpec((BLK, 512), lambda i: (i, 0)),
                  pl.BlockSpec((s, 128), lambda i: (0, 0)),
                  pl.BlockSpec((s, 128), lambda i: (0, 0)),
                  pl.BlockSpec((2, NQ, 2 * BLK, BLK), lambda i: (0, 0, 0, 0)),
                  pl.BlockSpec(memory_space=pltpu.SMEM)],
        out_specs=[pl.BlockSpec((BLK, 512), lambda i: (i, 0)),
                   pl.BlockSpec((None, NQ, 2 * BLK, BLK), lambda i: (i, 0, 0, 0)),
                   pl.BlockSpec((None, NQ, BLK), lambda i: (i, 0, 0))],
        out_shape=[jax.ShapeDtypeStruct((s, 512), BF16), jax.ShapeDtypeStruct((nblk, NQ, 2 * BLK, BLK), BF16),
                   jax.ShapeDtypeStruct((nblk, NQ, BLK), F32)],
        compiler_params=_cp(1))(q, k, v, bias, sinks)


def _outproj_fwd(pool_o, attn_o, w_out, x, g2, g3):
    s = x.shape[0]
    tm = min(TM, s)

    def body(p_ref, a_ref, w_ref, x_ref, g2_ref, g3_ref, mix_ref, x1_ref, h2_ref):
        mix = _dot(p_ref[...], w_ref[0:512, :]) + _dot(a_ref[...], w_ref[512:1024, :])
        mix_ref[...] = mix
        _, n2 = _rms(mix)
        x1 = x_ref[...] + n2 * g2_ref[...]
        x1_ref[...] = x1
        _, n3 = _rms(x1)
        h2_ref[...] = (n3 * g3_ref[...]).astype(BF16)

    row = lambda w: pl.BlockSpec((tm, w), lambda i: (i, 0))
    vec = pl.BlockSpec((1, D), lambda i: (0, 0))
    return pl.pallas_call(
        body, name="outproj_fwd", grid=(s // tm,),
        in_specs=[row(512), row(512), pl.BlockSpec((D, D), lambda i: (0, 0)), row(D), vec, vec],
        out_specs=[row(D), row(D), row(D)],
        out_shape=[jax.ShapeDtypeStruct((s, D), F32), jax.ShapeDtypeStruct((s, D), F32),
                   jax.ShapeDtypeStruct((s, D), BF16)],
        compiler_params=_cp(1))(pool_o, attn_o, w_out, x, g2, g3)


def _silu_parts(g):
    sg = jax.nn.sigmoid(g)
    return sg, g * sg


def _ffn_fwd(h2, wg, wu, wd, x1, target, g4):
    s = h2.shape[0]
    tm = min(TM_FFN_FWD, s)

    def body(h_ref, wg_ref, wu_ref, wd_ref, x1_ref, t_ref, g4_ref,
             gate_ref, up_ref, df_ref, dy_ref, loss_ref, gg4_ref, f_acc):
        i = pl.program_id(0)
        j = pl.program_id(1)
        first = j == 0
        for r in range(0, tm, FFN_ROWS):
            rows = pl.ds(r, min(FFN_ROWS, tm))
            h = h_ref[rows, :]
            gate = _dot_nt(h, wg_ref[...])
            up = _dot_nt(h, wu_ref[...])
            gate_ref[rows, :] = gate.astype(BF16)
            up_ref[rows, :] = up.astype(BF16)
            _, sl = _silu_parts(gate)
            contrib = _dot((sl * up).astype(BF16), wd_ref[...])
            f_acc[rows, :] = jnp.where(first, contrib, f_acc[rows, :] + contrib)

        @pl.when((i == 0) & (j == 0))
        def _():
            loss_ref[...] = jnp.zeros_like(loss_ref)
            gg4_ref[...] = jnp.zeros_like(gg4_ref)

        @pl.when(j == NSH - 1)
        def _():
            r4, n4 = _rms(f_acc[...])
            g4v = g4_ref[...]
            err = x1_ref[...] + n4 * g4v - t_ref[...]
            loss_ref[...] += (0.5 / D) * jnp.sum(err * err).reshape(1, 1)
            dy = err * (1.0 / D)
            dy_ref[...] = dy
            df, dg = _rms_bwd(r4, n4, g4v, dy)
            gg4_ref[...] += dg
            df_ref[...] = df.astype(BF16)

    row = lambda w: pl.BlockSpec((tm, w), lambda i, j: (i, 0))
    sh = lambda r, c: pl.BlockSpec((None, r, c), lambda i, j: (j, 0, 0))
    act = pl.BlockSpec((None, tm, FS), lambda i, j: (j, i, 0))
    vec = pl.BlockSpec((1, D), lambda i, j: (0, 0))
    return pl.pallas_call(
        body, name="ffn_fwd", grid=(s // tm, NSH),
        in_specs=[row(D), sh(FS, D), sh(FS, D), sh(FS, D), row(D), row(D), vec],
        out_specs=[act, act, row(D), row(D), pl.BlockSpec((1, 1), lambda i, j: (0, 0)), vec],
        out_shape=[jax.ShapeDtypeStruct((NSH, s, FS), BF16), jax.ShapeDtypeStruct((NSH, s, FS), BF16),
                   jax.ShapeDtypeStruct((s, D), BF16), jax.ShapeDtypeStruct((s, D), F32),
                   jax.ShapeDtypeStruct((1, 1), F32), jax.ShapeDtypeStruct((1, D), F32)],
        scratch_shapes=[pltpu.VMEM((tm, D), F32)],
        compiler_params=_cp(2))(h2, wg, wu, wd, x1, target, g4)


def _ffn_bwd_act(df, gate, up, wg, wu, wd, dy, x1, mix, g3, g2):
    s = df.shape[0]
    tm = min(TM_FFN, s)

    def body(df_ref, gate_ref, up_ref, wg_ref, wu_ref, wd_ref, dy_ref, x1_ref, mix_ref, g3_ref, g2_ref,
             dgate_ref, dup_ref, dx1_ref, dmix_ref, gg3_ref, gg2_ref, acc):
        i = pl.program_id(0)
        j = pl.program_id(1)
        first = j == 0
        for r in range(0, tm, FFN_ROWS):
            rows = pl.ds(r, min(FFN_ROWS, tm))
            da = _dot_nt(df_ref[rows, :], wd_ref[...])
            g = gate_ref[rows, :].astype(F32)
            u = up_ref[rows, :].astype(F32)
            sg, sl = _silu_parts(g)
            dgate = (da * u * (sg * (1.0 + g * (1.0 - sg)))).astype(BF16)
            dup = (da * sl).astype(BF16)
            dgate_ref[rows, :] = dgate
            dup_ref[rows, :] = dup
            contrib = _dot(dgate, wg_ref[...]) + _dot(dup, wu_ref[...])
            acc[rows, :] = jnp.where(first, contrib, acc[rows, :] + contrib)

        @pl.when((i == 0) & (j == 0))
        def _():
            gg3_ref[...] = jnp.zeros_like(gg3_ref)
            gg2_ref[...] = jnp.zeros_like(gg2_ref)

        @pl.when(j == NSH - 1)
        def _():
            r3, n3 = _rms(x1_ref[...])
            dx1n, dg3 = _rms_bwd(r3, n3, g3_ref[...], acc[...])
            dx1 = dy_ref[...] + dx1n
            dx1_ref[...] = dx1
            gg3_ref[...] += dg3
            r2, n2 = _rms(mix_ref[...])
            dmix, dg2 = _rms_bwd(r2, n2, g2_ref[...], dx1)
            gg2_ref[...] += dg2
            dmix_ref[...] = dmix.astype(BF16)

    row = lambda w: pl.BlockSpec((tm, w), lambda i, j: (i, 0))
    sh = lambda r, c: pl.BlockSpec((None, r, c), lambda i, j: (j, 0, 0))
    act = pl.BlockSpec((None, tm, FS), lambda i, j: (j, i, 0))
    vec = pl.BlockSpec((1, D), lambda i, j: (0, 0))
    return pl.pallas_call(
        body, name="ffn_bwd_act", grid=(s // tm, NSH),
        in_specs=[row(D), act, act, sh(FS, D), sh(FS, D), sh(FS, D), row(D), row(D), row(D), vec, vec],
        out_specs=[act, act, row(D), row(D), vec, vec],
        out_shape=[jax.ShapeDtypeStruct((NSH, s, FS), BF16), jax.ShapeDtypeStruct((NSH, s, FS), BF16),
                   jax.ShapeDtypeStruct((s, D), F32), jax.ShapeDtypeStruct((s, D), BF16),
                   jax.ShapeDtypeStruct((1, D), F32), jax.ShapeDtypeStruct((1, D), F32)],
        scratch_shapes=[pltpu.VMEM((tm, D), F32)],
        compiler_params=_cp(2))(df, gate, up, wg, wu, wd, dy, x1, mix, g3, g2)


SMEM_SPEC = pl.BlockSpec(memory_space=pltpu.SMEM)


def _emit_halves(acc_ref, row0, rows, c, own_ref, oth_ref):
    half = rows // 2
    own_ref[...] = acc_ref[pl.ds(row0 + pl.multiple_of(c * half, 8), half), :]
    oth_ref[...] = acc_ref[pl.ds(row0 + pl.multiple_of((1 - c) * half, 8), half), :].astype(BF16)


def _half_shapes(rows):
    return [jax.ShapeDtypeStruct((NSH, rows // 2, D), F32), jax.ShapeDtypeStruct((NSH, rows // 2, D), BF16)]


def _ffn_bwd_w(h2, gate, up, dgate, dup, df, c_arr):
    s = h2.shape[0]
    tm = min(TM_FFN_FWD, s)
    nt = s // tm

    def body(c_ref, h_ref, gate_ref, up_ref, dgate_ref, dup_ref, df_ref, *rest):
        outs, accs = rest[:6], rest[6:]
        i = pl.program_id(1)
        @pl.when(i == 0)
        def _():
            for acc in accs:
                acc[...] = jnp.zeros_like(acc)

        h = h_ref[...]
        accs[0][...] += _dot_tn(dgate_ref[...], h)
        accs[1][...] += _dot_tn(dup_ref[...], h)
        _, sl = _silu_parts(gate_ref[...].astype(F32))
        a = (sl * up_ref[...].astype(F32)).astype(BF16)
        accs[2][...] += _dot_tn(a, df_ref[...])

        @pl.when(i == nt - 1)
        def _():
            for t, acc in enumerate(accs):
                _emit_halves(acc, 0, FS, c_ref[0], outs[2 * t], outs[2 * t + 1])

    row = lambda w: pl.BlockSpec((tm, w), lambda j, i: (i, 0))
    act = pl.BlockSpec((None, tm, FS), lambda j, i: (j, i, 0))
    half = pl.BlockSpec((None, FS // 2, D), lambda j, i: (j, 0, 0))
    return pl.pallas_call(
        body, name="ffn_bwd_w", grid=(NSH, nt),
        in_specs=[SMEM_SPEC, row(D), act, act, act, act, row(D)],
        out_specs=[half] * 6,
        out_shape=_half_shapes(FS) * 3,
        scratch_shapes=[pltpu.VMEM((FS, D), F32)] * 3,
        compiler_params=_cp(2))(c_arr, h2, gate, up, dgate, dup, df)


def _outproj_bwd(dmix, w_out, pool_o, attn_o, c_arr, dep=None):
    s = dmix.shape[0]
    tm = min(TM, s)
    nt = s // tm

    def body(c_ref, dm_ref, w_ref, p_ref, a_ref, *rest):
        dpool_ref, dattn_ref, own_ref, oth_ref, gw_ref = rest[-5:]
        i = pl.program_id(0)

        @pl.when(i == 0)
        def _():
            gw_ref[...] = jnp.zeros_like(gw_ref)

        dm = dm_ref[...]
        dpool_ref[...] = _dot_nt(dm, w_ref[0:512, :])
        dattn_ref[...] = _dot_nt(dm, w_ref[512:1024, :]).astype(BF16)
        gw_ref[0:512, :] += _dot_tn(p_ref[...], dm)
        gw_ref[512:1024, :] += _dot_tn(a_ref[...], dm)

        @pl.when(i == nt - 1)
        def _():
            for k in range(NSH):
                _emit_halves(gw_ref, k * WOUT_S, WOUT_S, c_ref[0], own_ref.at[k], oth_ref.at[k])

    row = lambda w: pl.BlockSpec((tm, w), lambda i: (i, 0))
    full = pl.BlockSpec((D, D), lambda i: (0, 0))
    half = pl.BlockSpec((NSH, WOUT_S // 2, D), lambda i: (0, 0, 0))
    return pl.pallas_call(
        body, name="outproj_bwd", grid=(nt,),
        in_specs=[SMEM_SPEC, row(D), full, row(512), row(512)] + ([] if dep is None else [ANY]),
        out_specs=[row(512), row(512), half, half],
        out_shape=[jax.ShapeDtypeStruct((s, 512), F32), jax.ShapeDtypeStruct((s, 512), BF16)] + _half_shapes(WOUT_S),
        scratch_shapes=[pltpu.VMEM((D, D), F32)],
        compiler_params=_cp(1))(c_arr, dmix, w_out, pool_o, attn_o, *([] if dep is None else [dep]))


def _pool_bwd(pooled, dpool, w_pool, pool_scale, dep=None):
    s = pooled.shape[0]
    tm = min(TM_POOL, s)
    hb = tm // HALO
    nt = s // tm
    last_halo = s // HALO - 1

    def body(p_ref, dc_ref, dn_ref, wp_ref, sc_ref, *rest):
        du_ref, gwp_ref, gsc_ref = rest[-3:]
        i = pl.program_id(0)

        @pl.when(i == 0)
        def _():
            gwp_ref[...] = jnp.zeros_like(gwp_ref)
            gsc_ref[...] = jnp.zeros_like(gsc_ref)

        dcur = dc_ref[...]
        dnext = jnp.where(i < nt - 1, dn_ref[...], 0.0)
        dext = jnp.concatenate([dcur, dnext], axis=0)
        t_ext = i * tm + lax.broadcasted_iota(jnp.int32, (tm + HALO, GROUP), 0)
        for g, w in enumerate(WINDOWS):
            sl = slice(g * GROUP, (g + 1) * GROUP)
            pb = p_ref[:, sl]
            wp = wp_ref[g]
            mixed = _dot(pb, wp)
            gsc_ref[:, sl] += jnp.sum(dcur[:, sl] * mixed, axis=0, keepdims=True)
            dmixed = (dext[:, sl] * sc_ref[:, sl]).astype(BF16)
            gwp_ref[g] += _dot_tn(pb, dmixed[0:tm])
            dpooled = _dot_nt(dmixed, wp)
            z = dpooled / jnp.minimum(t_ext + 1, w).astype(F32)
            du_ref[:, sl] = (_window_sums(z, w, -1, tm, 2) - dpooled[0:tm]).astype(BF16)

    return pl.pallas_call(
        body, name="pool_bwd", grid=(nt,),
        in_specs=[pl.BlockSpec((tm, 512), lambda i: (i, 0)),
                  pl.BlockSpec((tm, 512), lambda i: (i, 0)),
                  pl.BlockSpec((HALO, 512), lambda i: (jnp.minimum((i + 1) * hb, last_halo), 0)),
                  pl.BlockSpec((4, GROUP, GROUP), lambda i: (0, 0, 0)),
                  pl.BlockSpec((1, 512), lambda i: (0, 0))] + ([] if dep is None else [ANY]),
        out_specs=[pl.BlockSpec((tm, 512), lambda i: (i, 0)),
                   pl.BlockSpec((4, GROUP, GROUP), lambda i: (0, 0, 0)),
                   pl.BlockSpec((1, 512), lambda i: (0, 0))],
        out_shape=[jax.ShapeDtypeStruct((s, 512), BF16), jax.ShapeDtypeStruct((4, GROUP, GROUP), F32),
                   jax.ShapeDtypeStruct((1, 512), F32)],
        compiler_params=_cp(1))(pooled, dpool, dpool, w_pool, pool_scale, *([] if dep is None else [dep]))


def _attn_bwd(q, k, v, do, probs, sink_share, bucket, dep=None):
    s = q.shape[0]
    nblk = s // BLK

    def body(q_ref, do_ref, k_ref, v_ref, prob_ref, ps_ref, bk_ref, *rest):
        dq_ref, dk_ref, dv_ref, grb_ref, gsk_ref, dss_ref = rest[-6:]
        n = pl.program_id(0)

        @pl.when(n == 0)
        def _():
            dk_ref[...] = jnp.zeros_like(dk_ref)
            dv_ref[...] = jnp.zeros_like(dv_ref)
            dss_ref[...] = jnp.zeros_like(dss_ref)
            gsk_ref[...] = jnp.zeros_like(gsk_ref)

        kt, (lo, pstart, cstart) = _kv_band(k_ref, n, True)
        vv, _ = _kv_band(v_ref, n, False)
        lo_q = lax.broadcasted_iota(jnp.int32, (BLK, 128), 1) < HD
        dkk = [jnp.zeros((2 * BLK, 128), F32), jnp.zeros((2 * BLK, 128), F32)]
        dvv = [jnp.zeros((2 * BLK, 128), F32), jnp.zeros((2 * BLK, 128), F32)]

        def by_head(t):
            return jnp.concatenate([jnp.where(lo_q, t, 0.0), jnp.where(lo_q, 0.0, t)], axis=0).astype(BF16)

        for p in range(4):
            h = p // 2
            qt = q_ref[:, p * 128:(p + 1) * 128].astype(F32) * SCALE
            dot = do_ref[:, p * 128:(p + 1) * 128]
            pb = prob_ref[pl.ds(2 * p, 2)]
            prob = pb.astype(F32)
            ps = ps_ref[pl.ds(2 * p, 2), :].reshape(2, 1, BLK)
            dp = _dot_nt(vv[h], dot).reshape(2, 2 * BLK, BLK)
            delta = jnp.sum(prob * dp, axis=1, keepdims=True)
            ds = prob * (dp - delta)
            sink_part = ps * delta
            for e in range(2):
                gs = -jnp.sum(sink_part[e]).reshape(1, 1)
                gsk_ref[pl.ds(2 * p + e, 1), :] += jnp.broadcast_to(gs, (1, 128))
            dss_ref[pl.ds(2 * p, 2)] += ds
            dsb = ds.astype(BF16)
            dq_t = _dot(kt[h], dsb.reshape(4 * BLK, BLK))
            dq_ref[:, p * 128:(p + 1) * 128] = (dq_t.T * SCALE).astype(BF16)
            dkk[h] = dkk[h] + _dot(jnp.concatenate([dsb[0], dsb[1]], axis=1), by_head(qt))
            dvv[h] = dvv[h] + _dot(jnp.concatenate([pb[0], pb[1]], axis=1), by_head(dot.astype(F32)))
        for acc, ref in ((dkk, dk_ref), (dvv, dv_ref)):
            f0 = acc[0] + pltpu.roll(acc[0], HD, axis=1)
            f1 = acc[1] + pltpu.roll(acc[1], HD, axis=1)
            band = jnp.where(lo, f0, f1)
            ref[pl.ds(pstart, BLK), :] += band[0:BLK]
            ref[pl.ds(cstart, BLK), :] += band[BLK:2 * BLK]

        @pl.when(n == nblk - 1)
        def _():
            _relbias_grad(dss_ref, bk_ref[...], grb_ref)

    blk = pl.BlockSpec((BLK, 512), lambda i: (i, 0))
    kv = pl.BlockSpec((s, 128), lambda i: (0, 0))
    row8 = pl.BlockSpec((NQ, 128), lambda i: (0, 0))
    return pl.pallas_call(
        body, name="attn_bwd", grid=(nblk,),
        in_specs=[blk, blk, kv, kv, pl.BlockSpec((None, NQ, 2 * BLK, BLK), lambda i: (i, 0, 0, 0)),
                  pl.BlockSpec((None, NQ, BLK), lambda i: (i, 0, 0)), pl.BlockSpec((2 * BLK, BLK), lambda i: (0, 0))]
        + ([] if dep is None else [ANY]),
        out_specs=[blk, kv, kv, row8, row8],
        out_shape=[jax.ShapeDtypeStruct((s, 512), BF16), jax.ShapeDtypeStruct((s, 128), F32),
                   jax.ShapeDtypeStruct((s, 128), F32), jax.ShapeDtypeStruct((NQ, 128), F32),
                   jax.ShapeDtypeStruct((NQ, 128), F32)],
        scratch_shapes=[pltpu.VMEM((NQ, 2 * BLK, BLK), F32)],
        compiler_params=_cp(1))(q, do, k, v, probs, sink_share, bucket, *([] if dep is None else [dep]))


def _relbias_grad(ds_ref, bucket_v, o_ref):
    lane = lax.broadcasted_iota(jnp.int32, (NQ, 128), 1)
    head_row = lax.broadcasted_iota(jnp.int32, (NQ, BLK), 0)
    acc = jnp.zeros((NQ, 128), F32)
    for b in range(NBUCKET):
        sel = bucket_v == b
        stack = jnp.zeros((NQ, BLK), F32)
        for h in range(NQ):
            col_sums = jnp.sum(jnp.where(sel, ds_ref[h], 0.0), axis=0, keepdims=True)
            stack = jnp.where(head_row == h, col_sums, stack)
        acc = acc + jnp.where(lane == b, jnp.sum(stack, axis=1, keepdims=True), 0.0)
    o_ref[...] = acc


def _inproj_bwd(x, g1, du, dq, dk, dv, w_in, dx1, c_arr, dep=None):
    s = x.shape[0]
    tm = min(TM, s)
    nt = s // tm
    cols = ((0, 512), (512, 1024), (1024, 1152), (1152, 1280))

    def body(c_ref, x_ref, g_ref, du_ref, dq_ref, dk_ref, dv_ref, w_ref, dx1_ref, *rest):
        gx_ref, own_ref, oth_ref, gg_ref, gw_ref = rest[-5:]
        i = pl.program_id(0)

        @pl.when(i == 0)
        def _():
            gw_ref[...] = jnp.zeros_like(gw_ref)
            gg_ref[...] = jnp.zeros_like(gg_ref)

        gv = g_ref[...]
        r1, n1 = _rms(x_ref[...])
        h = (n1 * gv).astype(BF16)
        parts = (du_ref[...], dq_ref[...], dk_ref[...].astype(BF16), dv_ref[...].astype(BF16))
        dh = None
        for (a, b), dpart in zip(cols, parts):
            t = _dot(dpart, w_ref[a:b, :])
            dh = t if dh is None else dh + t
            gw_ref[a:b, :] += _dot_tn(dpart, h)
        dx, dg = _rms_bwd(r1, n1, gv, dh)
        gg_ref[...] += dg
        gx_ref[...] = dx1_ref[...] + dx

        @pl.when(i == nt - 1)
        def _():
            for k in range(NSH):
                _emit_halves(gw_ref, k * WIN_S, WIN_S, c_ref[0], own_ref.at[k], oth_ref.at[k])

    row = lambda w: pl.BlockSpec((tm, w), lambda i: (i, 0))
    vec = pl.BlockSpec((1, D), lambda i: (0, 0))
    full = pl.BlockSpec((IN_W, D), lambda i: (0, 0))
    half = pl.BlockSpec((NSH, WIN_S // 2, D), lambda i: (0, 0, 0))
    return pl.pallas_call(
        body, name="inproj_bwd", grid=(nt,),
        in_specs=[SMEM_SPEC, row(D), vec, row(512), row(512), row(128), row(128), full, row(D)]
        + ([] if dep is None else [ANY]),
        out_specs=[row(D), half, half, vec],
        out_shape=[jax.ShapeDtypeStruct((s, D), F32)] + _half_shapes(WIN_S) + [jax.ShapeDtypeStruct((1, D), F32)],
        scratch_shapes=[pltpu.VMEM((IN_W, D), F32)],
        compiler_params=_cp(1))(c_arr, x, g1, du, dq, dk, dv, w_in, dx1, *([] if dep is None else [dep]))


def _local_step(x, target, small, w_in, w_out, ffn_weights, c_arr, on_ffn_grads=None, on_wout_grads=None,
                on_attn_grads=None):
    g1, g2, g3, g4, w_pool, pool_scale, rel_bias, sinks = small
    bucket = jnp.asarray(_bucket_table())
    wp_bf = w_pool.astype(BF16)
    bias = _bias_table(bucket, rel_bias)
    h1 = _prenorm(x, g1)
    if callable(w_in):
        w_in = w_in(bias, h1)
    u, q, k, v = _inproj_fwd(h1, w_in)
    pool_o, pooled = _pool_fwd(u, wp_bf, pool_scale)
    attn_o, probs, sink_share = _attn_fwd(q, k, v, bias, sinks)
    g2_fwd = g2
    if callable(w_out):
        w_out, after = w_out(pool_o, attn_o)
        g2_fwd = g2 + after[:1, :1]
    mix, x1, h2 = _outproj_fwd(pool_o, attn_o, w_out, x, g2_fwd, g3)
    wg, wu, wd = ffn_weights(h2) if callable(ffn_weights) else ffn_weights
    gate, up, df, dy, loss, gg4 = _ffn_fwd(h2, wg, wu, wd, x1, target, g4)
    dgate, dup, dx1, dmix, gg3, gg2 = _ffn_bwd_act(df, gate, up, wg, wu, wd, dy, x1, mix, g3, g2)
    ffn_g = _ffn_bwd_w(h2, gate, up, dgate, dup, df, c_arr)
    dep = None if on_ffn_grads is None else on_ffn_grads(ffn_g)
    dpool, dattn, wout_own, wout_oth = _outproj_bwd(dmix, w_out, pool_o, attn_o, c_arr, dep)
    dep = None if on_wout_grads is None else on_wout_grads(wout_own, wout_oth, dpool)
    du, gwp, gsc = _pool_bwd(pooled, dpool, wp_bf, pool_scale, dep)
    dq, dk, dv, grb, gsk = _attn_bwd(q, k, v, dattn, probs, sink_share, bucket, dep)
    dep = None if on_attn_grads is None else on_attn_grads([du, dq])
    gx, win_own, win_oth, gg1 = _inproj_bwd(x, g1, du, dq, dk, dv, w_in, dx1, c_arr, dep)
    small_grads = (gg1, gg2, gg3, gg4, gwp, gsc, grb[:, :NBUCKET].T, gsk[:, 0].reshape(1, NQ))
    return loss, gx, small_grads, ((win_own, win_oth), (wout_own, wout_oth), ffn_g)


def _place():
    x, y, c = lax.axis_index("x"), lax.axis_index("y"), lax.axis_index("c")
    chips = ((1 - x, y), (x, 1 - y), (1 - x, 1 - y))
    return x, y, c, chips


def _remote(src, dst, ssem, rsem, dev):
    return pltpu.make_async_remote_copy(src_ref=src, dst_ref=dst, send_sem=ssem, recv_sem=rsem,
                                        device_id=dev, device_id_type=MESH_T)


def _to_sibling(arrs, name):
    nt = len(arrs)

    def body(*refs):
        srcs, dsts = refs[:nt], refs[nt:2 * nt]
        ssem, rsem = refs[2 * nt:]
        x, y, c, _ = _place()
        cps = [_remote(srcs[t], dsts[t], ssem.at[t], rsem.at[t], (x, y, 1 - c)) for t in range(nt)]
        for cp in cps:
            cp.start()
        for cp in cps:
            cp.wait()

    return pl.pallas_call(
        body, name=name, in_specs=[ANY] * nt, out_specs=[ANY] * nt,
        out_shape=[jax.ShapeDtypeStruct(a.shape, a.dtype) for a in arrs],
        scratch_shapes=[pltpu.SemaphoreType.DMA((nt,)), pltpu.SemaphoreType.DMA((nt,))],
        compiler_params=_cp())(*arrs)


HBM_SPEC = pl.BlockSpec(memory_space=pltpu.HBM)
SEM_SPEC = pl.BlockSpec(memory_space=pltpu.SEMAPHORE)
DATAFLOW = pltpu.SideEffectType.DATAFLOW_SIDE_EFFECTING


def _gather_copies(srcs, lands, ssem, rsem):
    x, y, c, chips = _place()
    me = 2 * x + y
    out = []
    for t in range(len(srcs)):
        half = srcs[t].shape[0] // 2
        mine = pl.ds(pl.multiple_of(c * half, 16), half)
        for j, (px, py) in enumerate(chips):
            k = 3 * t + j
            send = _remote(srcs[t].at[mine], lands[t].at[me, mine], ssem.at[k], rsem.at[k], (px, py, c))
            blk = lands[t].at[2 * px + py, mine]
            out.append((send, _remote(blk, blk, ssem.at[k], rsem.at[k], (px, py, c))))
    return out


def _scatter_copies(srcs, lands, ssem, rsem):
    x, y, c, chips = _place()
    out = []
    for t in range(len(srcs)):
        for j, (px, py) in enumerate(chips):
            k = 3 * t + j
            send = _remote(srcs[t].at[2 * px + py], lands[t].at[j], ssem.at[k], rsem.at[k], (px, py, c))
            out.append((send, _remote(lands[t].at[j], lands[t].at[j], ssem.at[k], rsem.at[k], (px, py, c))))
    return out


def _sibling_copies(srcs, lands, ssem, rsem):
    x, y, c, _ = _place()
    sib = (x, y, 1 - c)
    return [(_remote(srcs[t], lands[t], ssem.at[t], rsem.at[t], sib),
             _remote(lands[t], lands[t], ssem.at[t], rsem.at[t], sib)) for t in range(len(srcs))]


def _forward_copies(srcs, lands, ssem, rsem):
    x, y, c, chips = _place()
    sib = (x, y, 1 - c)
    out = []
    for t in range(len(lands)):
        half = lands[t].shape[1] // 2
        mine = pl.ds(pl.multiple_of(c * half, 16), half)
        other = pl.ds(pl.multiple_of((1 - c) * half, 16), half)
        for j, (px, py) in enumerate(chips):
            k = 3 * t + j
            blk_m, blk_o = lands[t].at[2 * px + py, mine], lands[t].at[2 * px + py, other]
            out.append((_remote(blk_m, blk_m, ssem.at[k], rsem.at[k], sib),
                        _remote(blk_o, blk_o, ssem.at[k], rsem.at[k], sib)))
    return out


def _peer_copies(srcs, lands, ssem, rsem):
    x, y, c, _ = _place()
    me = 4 * x + 2 * y + c
    out = []
    for kk in range(1, 8):
        px, py, pc = x ^ (kk >> 2), y ^ ((kk >> 1) & 1), c ^ (kk & 1)
        send = _remote(srcs[0], lands[0].at[me], ssem.at[kk - 1], rsem.at[kk - 1], (px, py, pc))
        slot = lands[0].at[4 * px + 2 * py + pc]
        out.append((send, _remote(slot, slot, ssem.at[kk - 1], rsem.at[kk - 1], (px, py, pc))))
    return out


def _split_start(plan, ncopy, srcs, lands, after, name):
    ns, nbuf = len(srcs), len(srcs) + len(lands)
    extra = [] if after is None else [after]
    n_in = nbuf + len(extra)

    def body(*refs):
        ssem, rsem, token = refs[n_in], refs[n_in + 1], refs[-1]
        for send, _ in plan(refs[:ns], refs[ns:nbuf], ssem, rsem):
            send.start()
        token[...] = jnp.zeros_like(token)

    bufs = [pltpu.with_memory_space_constraint(a, pltpu.HBM) for a in list(srcs) + list(lands)]
    outs = pl.pallas_call(
        body, name=name, in_specs=[HBM_SPEC] * nbuf + [ANY] * len(extra),
        out_specs=[SEM_SPEC, SEM_SPEC] + [HBM_SPEC] * nbuf + [pl.BlockSpec(memory_space=pltpu.VMEM)],
        out_shape=[pltpu.SemaphoreType.DMA((ncopy,)), pltpu.SemaphoreType.DMA((ncopy,))]
        + [pltpu.HBM(a.shape, a.dtype) for a in bufs] + [jax.ShapeDtypeStruct((8, 128), F32)],
        input_output_aliases={i: 2 + i for i in range(nbuf)},
        compiler_params=pltpu.CompilerParams(has_side_effects=DATAFLOW))(*bufs, *extra)
    return outs[0], outs[1], outs[2:2 + ns], outs[2 + ns:2 + nbuf], outs[-1]


def _split_wait(plan, ssem, rsem, srcs, lands, after, name, only=None):
    ns, nbuf = len(srcs), len(srcs) + len(lands)

    def body(*refs):
        s_ref, r_ref = refs[nbuf], refs[nbuf + 1]
        for k, (send, recv) in enumerate(plan(refs[:ns], refs[ns:nbuf], s_ref, r_ref)):
            if only is None or k in only:
                send.wait_send()
                recv.wait_recv()

    outs = pl.pallas_call(
        body, name=name, in_specs=[HBM_SPEC] * nbuf + [SEM_SPEC, SEM_SPEC] + [ANY] * len(after),
        out_specs=[HBM_SPEC] * nbuf,
        out_shape=[pltpu.HBM(a.shape, a.dtype) for a in list(srcs) + list(lands)],
        input_output_aliases={i: i for i in range(nbuf)},
        compiler_params=pltpu.CompilerParams(has_side_effects=DATAFLOW))(*srcs, *lands, ssem, rsem, *after)
    return outs


def _gather_finish(lands):
    nt = len(lands)

    def body(*refs):
        outs = refs[nt:2 * nt]
        dsend, drecv = refs[2 * nt:]
        x, y, c, chips = _place()
        sib = (x, y, 1 - c)
        sends = []
        for t in range(nt):
            half = outs[t].shape[1] // 2
            mine = pl.ds(pl.multiple_of(c * half, 16), half)
            for j, (px, py) in enumerate(chips):
                blk = outs[t].at[2 * px + py, mine]
                cp = _remote(blk, blk, dsend.at[t, j], drecv.at[t, j], sib)
                cp.start()
                sends.append(cp)
        for t in range(nt):
            half = outs[t].shape[1] // 2
            other = pl.ds(pl.multiple_of((1 - c) * half, 16), half)
            for j, (px, py) in enumerate(chips):
                blk = outs[t].at[2 * px + py, other]
                _remote(blk, blk, dsend.at[t, j], drecv.at[t, j], sib).wait_recv()
        for cp in sends:
            cp.wait_send()

    return pl.pallas_call(
        body, name="gather_finish", in_specs=[ANY] * nt, out_specs=[ANY] * nt,
        out_shape=[jax.ShapeDtypeStruct(a.shape, a.dtype) for a in lands],
        input_output_aliases={t: t for t in range(nt)},
        scratch_shapes=[pltpu.SemaphoreType.DMA((nt, 3)), pltpu.SemaphoreType.DMA((nt, 3))],
        compiler_params=_cp())(*lands)


def _chip_partial(owns, recv, chip_arr, tag):
    nt = len(owns)

    def body(chip_ref, *refs):
        k = pl.program_id(0)
        for t in range(nt):
            p = refs[t][...] + refs[nt + t][...].astype(F32)
            refs[2 * nt + t][...] = p.astype(BF16)

            @pl.when(k == chip_ref[0])
            def _():
                refs[3 * nt + t][...] = p

    blk_specs = [pl.BlockSpec((None,) + a.shape[1:], lambda k, chip_ref: (k, 0, 0)) for a in owns]
    own_specs = [pl.BlockSpec(a.shape[1:], lambda k, chip_ref: (0, 0)) for a in owns]
    return pl.pallas_call(
        body, name="rs_chip_partial_" + tag,
        grid_spec=pltpu.PrefetchScalarGridSpec(num_scalar_prefetch=1, grid=(NSH,), in_specs=blk_specs * 2,
                                               out_specs=blk_specs + own_specs),
        out_shape=[jax.ShapeDtypeStruct(a.shape, BF16) for a in owns]
        + [jax.ShapeDtypeStruct(a.shape[1:], F32) for a in owns],
        compiler_params=_cp(1))(chip_arr, *owns, *recv)


def _sum_chips(own, recv, tag, after=()):
    nt = len(own)

    def body(*refs):
        outs = refs[2 * nt + len(after):]
        for t in range(nt):
            r = refs[nt + t]
            outs[t][...] = ((refs[t][...] + r[0].astype(F32)) + r[1].astype(F32)) + r[2].astype(F32)

    vm = pl.BlockSpec(memory_space=pltpu.VMEM)
    return pl.pallas_call(
        body, name="rs_sum_chips_" + tag, in_specs=[vm] * (2 * nt) + [ANY] * len(after), out_specs=[vm] * nt,
        out_shape=[jax.ShapeDtypeStruct(a.shape, F32) for a in own],
        compiler_params=_cp())(*own, *recv, *after)


def _adamw_math(w, g, m, v):
    m = ADAM_B1 * m + (1.0 - ADAM_B1) * g
    v = ADAM_B2 * v + (1.0 - ADAM_B2) * (g * g)
    m_hat = m / (1.0 - ADAM_B1 ** ADAM_STEP)
    v_hat = v / (1.0 - ADAM_B2 ** ADAM_STEP)
    delta = -ADAM_LR * (m_hat / (jnp.sqrt(v_hat) + ADAM_EPS) + ADAM_WD * w)
    return delta, m, v


ADAM_SPLIT = 4


def _adamw_shards(own, sib, ws, ms, vs, c_arr, tag):
    nt = len(own)

    def body(c_ref, *refs):
        hh = pl.program_id(0)
        mine = hh == c_ref[0]
        for t in range(nt):
            g = jnp.where(mine, refs[t][...], refs[nt + t][...])
            delta, m, v = _adamw_math(refs[2 * nt + t][...], g, refs[3 * nt + t][...], refs[4 * nt + t][...])
            refs[5 * nt + t][...] = g
            refs[6 * nt + t][...] = delta
            refs[7 * nt + t][...] = m
            refs[8 * nt + t][...] = v

    def tile(a):
        return (a.shape[0] // ADAM_SPLIT, a.shape[1])

    half_specs = [pl.BlockSpec(tile(a), lambda hh, q, c_ref: (q, 0)) for a in own]
    full_specs = [pl.BlockSpec(tile(a), lambda hh, q, c_ref: (hh * ADAM_SPLIT + q, 0)) for a in own]
    return pl.pallas_call(
        body, name="adamw_shards_" + tag,
        grid_spec=pltpu.PrefetchScalarGridSpec(num_scalar_prefetch=1, grid=(2, ADAM_SPLIT),
                                               in_specs=half_specs * 2 + full_specs * 3, out_specs=full_specs * 4),
        out_shape=[jax.ShapeDtypeStruct(w.shape, F32) for w in ws] * 4,
        compiler_params=_cp(2))(c_arr, *own, *sib, *ws, *ms, *vs)


SMALL_WPOOL_ROW = 32
SMALL_SCALE_ROW = SMALL_WPOOL_ROW + 4 * GROUP
SMALL_BIAS_ROW = SMALL_SCALE_ROW + 8
SMALL_SINKS_ROW = SMALL_BIAS_ROW + NBUCKET
SMALL_LOSS_ROW = SMALL_SINKS_ROW + 8


def _small_sum_adamw(gathered, wp, mp, vp):
    rows = wp.shape[0]

    def body(g_ref, w_ref, m_ref, v_ref, *rest):
        outs, res = rest[:-1], rest[-1]
        g = g_ref[0]
        for d in range(1, 8):
            g = g + g_ref[d]
        delta, m, v = _adamw_math(w_ref[...], g, m_ref[...], v_ref[...])
        for kind, val in enumerate((g, delta, m, v)):
            res[kind] = val
            gains = outs[8 * kind:8 * kind + 4]
            w_pool_o, scale_o, bias_o, sinks_o = outs[8 * kind + 4:8 * kind + 8]
            for t in range(4):
                for i in range(8):
                    gains[t][:, 128 * i:128 * (i + 1)] = res[kind, pl.ds(8 * t + i, 1), :]
            for grp in range(4):
                w_pool_o[grp] = res[kind, pl.ds(SMALL_WPOOL_ROW + GROUP * grp, GROUP), :]
            for i in range(4):
                scale_o[:, 128 * i:128 * (i + 1)] = res[kind, pl.ds(SMALL_SCALE_ROW + i, 1), :]
            bias_o[...] = res[kind, pl.ds(SMALL_BIAS_ROW, NBUCKET), :][:, 0:NQ]
            sinks_o[...] = res[kind, pl.ds(SMALL_SINKS_ROW, 1), :][:, 0:NQ]
        outs[-1][...] = res[0, pl.ds(SMALL_LOSS_ROW, 1), :]

    vm = pl.BlockSpec(memory_space=pltpu.VMEM)
    one_kind = [jax.ShapeDtypeStruct((1, D), F32)] * 4 + [
        jax.ShapeDtypeStruct((4, GROUP, GROUP), F32), jax.ShapeDtypeStruct((1, POOL_W), F32),
        jax.ShapeDtypeStruct((NBUCKET, NQ), F32), jax.ShapeDtypeStruct((1, NQ), F32)]
    return pl.pallas_call(
        body, name="small_sum_adamw", in_specs=[vm] * 4, out_specs=[vm] * 33,
        out_shape=one_kind * 4 + [jax.ShapeDtypeStruct((1, 128), F32)],
        scratch_shapes=[pltpu.VMEM((4, rows, 128), F32)],
        compiler_params=_cp())(gathered, wp, mp, vp)


def _pack_small(gains4, w_pool, pool_scale, rel_bias, sinks, loss):
    parts = list(gains4) + [w_pool, pool_scale, jnp.pad(rel_bias, ((0, 0), (0, 128 - NQ))), sinks, loss]
    rows = []
    for a in parts:
        flat = a.reshape(-1)
        n = flat.shape[0]
        padded = -(-n // 1024) * 1024
        rows.append(jnp.pad(flat, (0, padded - n)).reshape(-1, 128))
    return jnp.concatenate(rows, axis=0)


def kernel(x, g_pre_mix, w_in, w_pool, pool_scale, rel_bias, sinks, w_out, g_post_mix, g_pre_ffn, w_gate, w_up, w_down, g_post_ffn, loss_target, m_g_pre_mix, m_w_in, m_w_pool, m_pool_scale, m_rel_bias, m_sinks, m_w_out, m_g_post_mix, m_g_pre_ffn, m_w_gate, m_w_up, m_w_down, m_g_post_ffn, v_g_pre_mix, v_w_in, v_w_pool, v_pool_scale, v_rel_bias, v_sinks, v_w_out, v_g_post_mix, v_g_pre_ffn, v_w_gate, v_w_up, v_w_down, v_g_post_ffn):
    c = lax.axis_index("c")
    chip = 2 * lax.axis_index("x") + lax.axis_index("y")

    def shards(a_in, a_out, a_gate, a_up, a_down):
        return (a_in[0].T, a_out[0], a_gate[0].T, a_up[0].T, a_down[0])

    c_arr = c.reshape(1).astype(jnp.int32)
    chip_arr = chip.reshape(1).astype(jnp.int32)

    big_w = shards(w_in, w_out, w_gate, w_up, w_down)
    big_bf = [w.astype(BF16) for w in big_w]
    g_ssem, g_rsem, g_srcs, g_lands, _ = _split_start(
        _gather_copies, 15, big_bf, [lax.empty((NSH,) + a.shape, BF16) for a in big_bf], None, "gather_start")
    fw = {}

    def with_own(land, shard):
        return lax.dynamic_update_slice(land, shard[None], (chip, 0, 0))

    def w_in_ready(*after):
        fw["first"] = _split_wait(_gather_copies, g_ssem, g_rsem, g_srcs, g_lands, after, "gather_win_wait",
                                  only=(0, 1, 2))
        (fw["win"],) = _gather_finish([with_own(fw["first"][5], fw["first"][0])])
        return fw["win"].reshape(IN_W, D)

    def w_out_ready(*after):
        first = fw["first"]
        outs = _split_wait(_gather_copies, g_ssem, g_rsem, first[:5], [fw["win"]] + list(first[6:]), after,
                           "gather_rest_wait", only=tuple(range(3, 15)))
        lands = [with_own(land, src) for src, land in zip(outs[1:5], outs[6:])]
        (wout_all,) = _gather_finish(lands[:1])
        fw["f"] = _split_start(_forward_copies, 9, [], lands[1:], wout_all, "gather_forward_start")
        return wout_all.reshape(D, D), fw["f"][4]

    def ffn_weights(after):
        ssem, rsem, _, lands, _ = fw["f"]
        return _split_wait(_forward_copies, ssem, rsem, [], lands, [after], "gather_forward_wait")

    rs = {}

    def on_ffn_grads(ffn_g):
        oths = ffn_g[1::2]
        rs["ffn_own"] = ffn_g[0::2]
        rs["x"] = _split_start(_sibling_copies, 3, oths, [lax.empty(a.shape, BF16) for a in oths], ffn_g[0],
                               "rs_exchange_ffn_start")
        return rs["x"][4]

    def on_wout_grads(own, oth, after):
        ssem, rsem, srcs, lands, _ = rs["x"]
        recv_ffn = _split_wait(_sibling_copies, ssem, rsem, srcs, lands, [after], "rs_exchange_ffn_wait")[3:]
        recv_wout = _to_sibling([oth], "rs_exchange_wout")
        outs = _chip_partial([own] + list(rs["ffn_own"]), list(recv_wout) + list(recv_ffn), chip_arr, "early")
        rs["early_own"] = outs[4:]
        rs["s"] = _split_start(_scatter_copies, 12, outs[:4],
                               [lax.empty((3,) + a.shape[1:], BF16) for a in outs[:4]], outs[4], "scatter_early_start")
        return rs["s"][4]

    def on_attn_grads(after):
        ssem, rsem, srcs, lands, _ = rs["s"]
        recv_early = _split_wait(_scatter_copies, ssem, rsem, srcs, lands, after, "scatter_early_wait")[4:]
        finals = _sum_chips(list(rs["early_own"]), list(recv_early), "early")
        rs["sh"] = _split_start(_sibling_copies, 4, finals, [lax.empty(a.shape, F32) for a in finals], finals[0],
                                "rs_share_early_start")
        return rs["sh"][4]

    small = (g_pre_mix, g_post_mix, g_pre_ffn, g_post_ffn, w_pool[0], pool_scale, rel_bias, sinks)
    loss, gx, small_grads, ((win_own, win_oth), _, _) = _local_step(
        x[0], loss_target[0], small, w_in_ready, w_out_ready, ffn_weights, c_arr,
        on_ffn_grads, on_wout_grads, on_attn_grads)

    unused = jnp.zeros((1, 1), F32)
    gp = _pack_small(small_grads[:4], *small_grads[4:], loss)
    wp = _pack_small((g_pre_mix, g_post_mix, g_pre_ffn, g_post_ffn), w_pool, pool_scale, rel_bias, sinks, unused)
    mp = _pack_small((m_g_pre_mix, m_g_post_mix, m_g_pre_ffn, m_g_post_ffn), m_w_pool, m_pool_scale, m_rel_bias,
                     m_sinks, unused)
    vp = _pack_small((v_g_pre_mix, v_g_post_mix, v_g_pre_ffn, v_g_post_ffn), v_w_pool, v_pool_scale, v_rel_bias,
                     v_sinks, unused)

    moments = (shards(m_w_in, m_w_out, m_w_gate, m_w_up, m_w_down),
               shards(v_w_in, v_w_out, v_w_gate, v_w_up, v_w_down))
    recv_a = _to_sibling([win_oth], "rs_exchange_win")
    outs = _chip_partial([win_own], recv_a, chip_arr, "win")
    w_ssem, w_rsem, w_srcs, w_lands, w_token = _split_start(
        _scatter_copies, 3, outs[:1], [lax.empty((3,) + outs[0].shape[1:], BF16)], gx, "scatter_win_start")
    slots = lax.dynamic_update_slice(lax.empty((8,) + gp.shape, F32), gp[None], (2 * chip + c, 0, 0))
    p_ssem, p_rsem, p_srcs, p_lands, p_token = _split_start(_peer_copies, 7, [gp], [slots], w_token,
                                                            "small_allgather_start")
    ssem, rsem, srcs, lands, _ = rs["sh"]
    shared = _split_wait(_sibling_copies, ssem, rsem, srcs, lands, [p_token], "rs_share_early_wait")
    adam_e = _adamw_shards(shared[:4], shared[4:], big_w[1:], moments[0][1:], moments[1][1:], c_arr, "early")
    recv_win = _split_wait(_scatter_copies, w_ssem, w_rsem, w_srcs, w_lands, [adam_e[0]], "scatter_win_wait")[1:]
    win_final = _sum_chips([outs[1]], recv_win, "win")
    win_sib = _to_sibling(win_final, "rs_share_win")
    adam_w = _adamw_shards(win_final, win_sib, big_w[:1], moments[0][:1], moments[1][:1], c_arr, "win")
    big_g, big_d, big_m, big_v = [[adam_w[i]] + list(adam_e[4 * i:4 * i + 4]) for i in range(4)]

    gathered = _split_wait(_peer_copies, p_ssem, p_rsem, p_srcs, p_lands, [adam_w[0]], "small_allgather_wait")[1]
    flat = _small_sum_adamw(gathered, wp, mp, vp)
    small_out = [flat[8 * kind:8 * kind + 8] for kind in range(4)]
    total_loss = flat[-1][0, 0]

    def ordered(small8, big4):
        sg1, sg2, sg3, sg4, swp, ssc, srb, ssk = small8
        swp = swp[None]
        bwin, bwout, bwg, bwu, bwd = big4[0].T[None], big4[1][None], big4[2].T[None], big4[3].T[None], big4[4][None]
        return [sg1, bwin, swp, ssc, srb, ssk, bwout, sg2, sg3, bwg, bwu, bwd, sg4]

    res = [total_loss, gx[None]]
    for sm, bg in zip(small_out, (big_g, big_d, big_m, big_v)):
        res += ordered(sm, bg)
    return tuple(res)
```

```python
import numpy as np
import jax
import jax.numpy as jnp
from jax import lax
from jax.experimental import pallas as pl
from jax.experimental.pallas import tpu as pltpu

F32 = jnp.float32
BF16 = jnp.bfloat16

D = 1024
POOL_W = 512
GROUP = 128
WINDOWS = (2, 4, 8, 16)
HALO = 128
NQ = 8
BLK = 128
NBUCKET = 32
IN_W = 1280
DFF = 2816
NSH = 4
FS = DFF // NSH
WIN_S = IN_W // NSH
WOUT_S = D // NSH
EPS = 1e-6
NEG = -1e30
SCALE = 0.125

ADAM_LR = 0.001
ADAM_B1 = 0.9
ADAM_B2 = 0.999
ADAM_EPS = 1e-08
ADAM_WD = 0.01
ADAM_STEP = 10

TM = 512
TM_POOL = 256
TM_FFN = 512
TM_FFN_FWD = 1024
FFN_ROWS = 256
VMEM_LIMIT = 60 * 1024 * 1024

MESH_T = pl.DeviceIdType.MESH
ANY = pl.BlockSpec(memory_space=pl.ANY)


def _cp(n_grid=0, **kw):
    sem = ("arbitrary",) * n_grid if n_grid else None
    return pltpu.CompilerParams(dimension_semantics=sem, vmem_limit_bytes=VMEM_LIMIT, **kw)


def _dot(a, b):
    return jnp.dot(a, b, preferred_element_type=F32)


def _dot_nt(a, b):
    return lax.dot_general(a, b, (((1,), (1,)), ((), ())), preferred_element_type=F32)


def _dot_tn(a, b):
    return lax.dot_general(a, b, (((0,), (0,)), ((), ())), preferred_element_type=F32)


def _rms(x):
    r = lax.rsqrt(jnp.mean(x * x, axis=-1, keepdims=True) + EPS)
    return r, x * r


def _rms_bwd(r, n, g, dout):
    dn = dout * g
    dx = r * (dn - n * jnp.mean(dn * n, axis=-1, keepdims=True))
    dg = jnp.sum(dout * n, axis=0, keepdims=True)
    return dx, dg


def _split(x, n):
    parts = []
    r = x
    for _ in range(n):
        p = r.astype(BF16)
        parts.append(p)
        r = r - p.astype(F32)
    return parts


def _band_dot(a, x, n):
    acc = None
    for p in _split(x, n):
        t = _dot(a, p)
        acc = t if acc is None else acc + t
    return acc


def _bucket_table():
    qi = np.arange(BLK)[None, :]
    kj = np.arange(2 * BLK)[:, None]
    dist = qi + BLK - kj
    n = np.maximum(dist, 0)
    nf = np.maximum(n, 1).astype(np.float32)
    large = 16 + (np.log(nf / np.float32(16)) / np.float32(np.log(128 / 16)) * np.float32(16)).astype(np.int32)
    large = np.minimum(large, NBUCKET - 1)
    return np.where(n < 16, n, large).astype(np.int32)


def _prenorm(x, g1):
    s = x.shape[0]
    tm = min(TM, s)

    def body(x_ref, g_ref, h_ref):
        _, n = _rms(x_ref[...])
        h_ref[...] = (n * g_ref[...]).astype(BF16)

    row = pl.BlockSpec((tm, D), lambda i: (i, 0))
    return pl.pallas_call(
        body, name="prenorm", grid=(s // tm,),
        in_specs=[row, pl.BlockSpec((1, D), lambda i: (0, 0))], out_specs=row,
        out_shape=jax.ShapeDtypeStruct((s, D), BF16),
        compiler_params=_cp(1))(x, g1)


def _inproj_fwd(h1, w_in):
    s = h1.shape[0]
    tm = min(TM, s)

    def body(h_ref, w_ref, u_ref, q_ref, k_ref, v_ref):
        proj = _dot_nt(h_ref[...], w_ref[...])
        u_ref[...] = proj[:, :512]
        q_ref[...] = proj[:, 512:1024].astype(BF16)
        k_ref[...] = proj[:, 1024:1152].astype(BF16)
        v_ref[...] = proj[:, 1152:1280].astype(BF16)

    row = lambda w: pl.BlockSpec((tm, w), lambda i: (i, 0))
    return pl.pallas_call(
        body, name="inproj_fwd", grid=(s // tm,),
        in_specs=[row(D), pl.BlockSpec((IN_W, D), lambda i: (0, 0))],
        out_specs=[row(512), row(512), row(128), row(128)],
        out_shape=[jax.ShapeDtypeStruct((s, 512), F32), jax.ShapeDtypeStruct((s, 512), BF16),
                   jax.ShapeDtypeStruct((s, 128), BF16), jax.ShapeDtypeStruct((s, 128), BF16)],
        compiler_params=_cp(1))(h1, w_in)


def _window_sums(ext, w, lag_sign, tm, terms):
    rows = lax.broadcasted_iota(jnp.int32, (HALO, 2 * HALO), 0)
    cols = lax.broadcasted_iota(jnp.int32, (HALO, 2 * HALO), 1)
    d = rows + HALO - cols if lag_sign > 0 else cols - rows
    band = jnp.where((d >= 0) & (d < w), 1.0, 0.0).astype(BF16)
    return jnp.concatenate([_band_dot(band, ext[r:r + 2 * HALO], terms) for r in range(0, tm, HALO)], axis=0)


def _pooled(ext, cur, t0, tm):
    t = t0 + lax.broadcasted_iota(jnp.int32, (tm, GROUP), 0)
    out = []
    for g, w in enumerate(WINDOWS):
        sl = slice(g * GROUP, (g + 1) * GROUP)
        cnt = jnp.minimum(t + 1, w).astype(F32)
        out.append(_window_sums(ext[:, sl], w, 1, tm, 2) / cnt - cur[:, sl])
    return out


def _pool_fwd(u, w_pool, pool_scale):
    s = u.shape[0]
    tm = min(TM_POOL, s)
    hb = tm // HALO

    def body(uc_ref, uh_ref, wp_ref, sc_ref, o_ref, pooled_ref):
        i = pl.program_id(0)
        cur = uc_ref[...]
        halo = jnp.where(i > 0, uh_ref[...], 0.0)
        ext = jnp.concatenate([halo, cur], axis=0)
        pooled = _pooled(ext, cur, i * tm, tm)
        for g in range(4):
            sl = slice(g * GROUP, (g + 1) * GROUP)
            pb = pooled[g].astype(BF16)
            pooled_ref[:, sl] = pb
            o_ref[:, sl] = (_dot(pb, wp_ref[g]) * sc_ref[:, sl]).astype(BF16)

    row = pl.BlockSpec((tm, 512), lambda i: (i, 0))
    return pl.pallas_call(
        body, name="pool_fwd", grid=(s // tm,),
        in_specs=[row, pl.BlockSpec((HALO, 512), lambda i: (jnp.maximum(i * hb - 1, 0), 0)),
                  pl.BlockSpec((4, GROUP, GROUP), lambda i: (0, 0, 0)),
                  pl.BlockSpec((1, 512), lambda i: (0, 0))],
        out_specs=[row, row],
        out_shape=[jax.ShapeDtypeStruct((s, 512), BF16)] * 2,
        compiler_params=_cp(1))(u, u, w_pool, pool_scale)


def _bias_table(bucket, rel_bias):
    def body(b_ref, rb_ref, o_ref):
        bucket_v = b_ref[...]
        key = lax.broadcasted_iota(jnp.int32, (2 * BLK, BLK), 0)
        dist = lax.broadcasted_iota(jnp.int32, (2 * BLK, BLK), 1) + BLK - key
        inwin = (dist >= 0) & (dist < BLK)
        for h in range(NQ):
            acc = jnp.zeros((2 * BLK, BLK), F32)
            for b in range(NBUCKET):
                acc = jnp.where(bucket_v == b, rb_ref[b, h], acc)
            rest = jnp.where(inwin, acc, NEG)
            o_ref[1, h] = rest
            o_ref[0, h] = jnp.where(key >= BLK, rest, NEG)

    return pl.pallas_call(
        body, name="bias_table",
        in_specs=[pl.BlockSpec(memory_space=pltpu.VMEM), pl.BlockSpec(memory_space=pltpu.SMEM)],
        out_specs=pl.BlockSpec(memory_space=pltpu.VMEM),
        out_shape=jax.ShapeDtypeStruct((2, NQ, 2 * BLK, BLK), F32),
        compiler_params=_cp())(bucket, rel_bias)


HD = 64


def _kv_band(ref, n, transposed):
    pstart = pl.multiple_of(jnp.maximum(n - 1, 0) * BLK, BLK)
    cstart = pl.multiple_of(n * BLK, BLK)
    lo = lax.broadcasted_iota(jnp.int32, (2 * BLK, 128), 1) < HD
    band = jnp.concatenate([ref[pl.ds(pstart, BLK), :], ref[pl.ds(cstart, BLK), :]], axis=0).astype(F32)
    rot = pltpu.roll(band, HD, axis=1)
    halves = ((jnp.where(lo, band, 0.0), jnp.where(lo, 0.0, rot)), (jnp.where(lo, rot, 0.0), jnp.where(lo, 0.0, band)))
    if transposed:
        out = [jnp.concatenate([a.T, b.T], axis=1).astype(BF16) for a, b in halves]
    else:
        out = [jnp.concatenate([a, b], axis=0).astype(BF16) for a, b in halves]
    return out, (lo, pstart, cstart)


def _pair_probs(qt, kk, bias, sk_ref, p):
    sink = jnp.concatenate([jnp.full((1, 1, BLK), sk_ref[0, 2 * p + e], F32) for e in range(2)], axis=0)
    s = _dot_nt(kk, qt).reshape(2, 2 * BLK, BLK) + bias
    m = jnp.maximum(jnp.max(s, axis=1, keepdims=True), sink)
    pr = jnp.exp(s - m)
    es = jnp.exp(sink - m)
    inv = 1.0 / (jnp.sum(pr, axis=1, keepdims=True) + es)
    return pr * inv, es * inv


def _attn_fwd(q, k, v, bias, sinks):
    s = q.shape[0]
    nblk = s // BLK

    def body(q_ref, k_ref, v_ref, b_ref, sk_ref, o_ref, prob_ref, ps_ref):
        n = pl.program_id(0)
        kk, _ = _kv_band(k_ref, n, False)
        vt, _ = _kv_band(v_ref, n, True)
        bidx = jnp.minimum(n, 1)
        for p in range(4):
            h = p // 2
            qt = (q_ref[:, p * 128:(p + 1) * 128].astype(F32) * SCALE).astype(BF16)
            prob, ps = _pair_probs(qt, kk[h], b_ref[bidx, pl.ds(2 * p, 2)], sk_ref, p)
            pb = prob.astype(BF16)
            prob_ref[pl.ds(2 * p, 2)] = pb
            ps_ref[pl.ds(2 * p, 2), :] = ps.reshape(2, BLK)
            acc_t = _dot(vt[h], pb.reshape(4 * BLK, BLK))
            o_ref[:, p * 128:(p + 1) * 128] = acc_t.T.astype(BF16)

    return pl.pallas_call(
        body, name="attn_fwd", grid=(nblk,),
        in_specs=[pl.BlockSpec((BLK, 512), lambda i: (i, 0)),
                  pl.BlockSpec((s, 128), lambda i: (0, 0)),
                  pl.BlockSpec((s, 128), lambda i: (0, 0)),
                  pl.BlockSpec((2, NQ, 2 * BLK, BLK), lambda i: (0, 0, 0, 0)),
                  pl.BlockSpec(memory_space=pltpu.SMEM)],
        out_specs=[pl.BlockSpec((BLK, 512), lambda i: (i, 0)),
                   pl.BlockSpec((None, NQ, 2 * BLK, BLK), lambda i: (i, 0, 0, 0)),
                   pl.BlockSpec((None, NQ, BLK), lambda i: (i, 0, 0))],
        out_shape=[jax.ShapeDtypeStruct((s, 512), BF16), jax.ShapeDtypeStruct((nblk, NQ, 2 * BLK, BLK), BF16),
                   jax.ShapeDtypeStruct((nblk, NQ, BLK), F32)],
        compiler_params=_cp(1))(q, k, v, bias, sinks)


def _outproj_fwd(pool_o, attn_o, w_out, x, g2, g3):
    s = x.shape[0]
    tm = min(TM, s)

    def body(p_ref, a_ref, w_ref, x_ref, g2_ref, g3_ref, mix_ref, x1_ref, h2_ref):
        mix = _dot(p_ref[...], w_ref[0:512, :]) + _dot(a_ref[...], w_ref[512:1024, :])
        mix_ref[...] = mix
        _, n2 = _rms(mix)
        x1 = x_ref[...] + n2 * g2_ref[...]
        x1_ref[...] = x1
        _, n3 = _rms(x1)
        h2_ref[...] = (n3 * g3_ref[...]).astype(BF16)

    row = lambda w: pl.BlockSpec((tm, w), lambda i: (i, 0))
    vec = pl.BlockSpec((1, D), lambda i: (0, 0))
    return pl.pallas_call(
        body, name="outproj_fwd", grid=(s // tm,),
        in_specs=[row(512), row(512), pl.BlockSpec((D, D), lambda i: (0, 0)), row(D), vec, vec],
        out_specs=[row(D), row(D), row(D)],
        out_shape=[jax.ShapeDtypeStruct((s, D), F32), jax.ShapeDtypeStruct((s, D), F32),
                   jax.ShapeDtypeStruct((s, D), BF16)],
        compiler_params=_cp(1))(pool_o, attn_o, w_out, x, g2, g3)


def _silu_parts(g):
    sg = jax.nn.sigmoid(g)
    return sg, g * sg


def _ffn_fwd(h2, wg, wu, wd, x1, target, g4):
    s = h2.shape[0]
    tm = min(TM_FFN_FWD, s)

    def body(h_ref, wg_ref, wu_ref, wd_ref, x1_ref, t_ref, g4_ref,
             gate_ref, up_ref, df_ref, dy_ref, loss_ref, gg4_ref, f_acc):
        i = pl.program_id(0)
        j = pl.program_id(1)
        first = j == 0
        for r in range(0, tm, FFN_ROWS):
            rows = pl.ds(r, min(FFN_ROWS, tm))
            h = h_ref[rows, :]
            gate = _dot_nt(h, wg_ref[...])
            up = _dot_nt(h, wu_ref[...])
            gate_ref[rows, :] = gate.astype(BF16)
            up_ref[rows, :] = up.astype(BF16)
            _, sl = _silu_parts(gate)
            contrib = _dot((sl * up).astype(BF16), wd_ref[...])
            f_acc[rows, :] = jnp.where(first, contrib, f_acc[rows, :] + contrib)

        @pl.when((i == 0) & (j == 0))
        def _():
            loss_ref[...] = jnp.zeros_like(loss_ref)
            gg4_ref[...] = jnp.zeros_like(gg4_ref)

        @pl.when(j == NSH - 1)
        def _():
            r4, n4 = _rms(f_acc[...])
            g4v = g4_ref[...]
            err = x1_ref[...] + n4 * g4v - t_ref[...]
            loss_ref[...] += (0.5 / D) * jnp.sum(err * err).reshape(1, 1)
            dy = err * (1.0 / D)
            dy_ref[...] = dy
            df, dg = _rms_bwd(r4, n4, g4v, dy)
            gg4_ref[...] += dg
            df_ref[...] = df.astype(BF16)

    row = lambda w: pl.BlockSpec((tm, w), lambda i, j: (i, 0))
    sh = lambda r, c: pl.BlockSpec((None, r, c), lambda i, j: (j, 0, 0))
    act = pl.BlockSpec((None, tm, FS), lambda i, j: (j, i, 0))
    vec = pl.BlockSpec((1, D), lambda i, j: (0, 0))
    return pl.pallas_call(
        body, name="ffn_fwd", grid=(s // tm, NSH),
        in_specs=[row(D), sh(FS, D), sh(FS, D), sh(FS, D), row(D), row(D), vec],
        out_specs=[act, act, row(D), row(D), pl.BlockSpec((1, 1), lambda i, j: (0, 0)), vec],
        out_shape=[jax.ShapeDtypeStruct((NSH, s, FS), BF16), jax.ShapeDtypeStruct((NSH, s, FS), BF16),
                   jax.ShapeDtypeStruct((s, D), BF16), jax.ShapeDtypeStruct((s, D), F32),
                   jax.ShapeDtypeStruct((1, 1), F32), jax.ShapeDtypeStruct((1, D), F32)],
        scratch_shapes=[pltpu.VMEM((tm, D), F32)],
        compiler_params=_cp(2))(h2, wg, wu, wd, x1, target, g4)


def _ffn_bwd_act(df, gate, up, wg, wu, wd, dy, x1, mix, g3, g2):
    s = df.shape[0]
    tm = min(TM_FFN, s)

    def body(df_ref, gate_ref, up_ref, wg_ref, wu_ref, wd_ref, dy_ref, x1_ref, mix_ref, g3_ref, g2_ref,
             dgate_ref, dup_ref, dx1_ref, dmix_ref, gg3_ref, gg2_ref, acc):
        i = pl.program_id(0)
        j = pl.program_id(1)
        first = j == 0
        for r in range(0, tm, FFN_ROWS):
            rows = pl.ds(r, min(FFN_ROWS, tm))
            da = _dot_nt(df_ref[rows, :], wd_ref[...])
            g = gate_ref[rows, :].astype(F32)
            u = up_ref[rows, :].astype(F32)
            sg, sl = _silu_parts(g)
            dgate = (da * u * (sg * (1.0 + g * (1.0 - sg)))).astype(BF16)
            dup = (da * sl).astype(BF16)
            dgate_ref[rows, :] = dgate
            dup_ref[rows, :] = dup
            contrib = _dot(dgate, wg_ref[...]) + _dot(dup, wu_ref[...])
            acc[rows, :] = jnp.where(first, contrib, acc[rows, :] + contrib)

        @pl.when((i == 0) & (j == 0))
        def _():
            gg3_ref[...] = jnp.zeros_like(gg3_ref)
            gg2_ref[...] = jnp.zeros_like(gg2_ref)

        @pl.when(j == NSH - 1)
        def _():
            r3, n3 = _rms(x1_ref[...])
            dx1n, dg3 = _rms_bwd(r3, n3, g3_ref[...], acc[...])
            dx1 = dy_ref[...] + dx1n
            dx1_ref[...] = dx1
            gg3_ref[...] += dg3
            r2, n2 = _rms(mix_ref[...])
            dmix, dg2 = _rms_bwd(r2, n2, g2_ref[...], dx1)
            gg2_ref[...] += dg2
            dmix_ref[...] = dmix.astype(BF16)

    row = lambda w: pl.BlockSpec((tm, w), lambda i, j: (i, 0))
    sh = lambda r, c: pl.BlockSpec((None, r, c), lambda i, j: (j, 0, 0))
    act = pl.BlockSpec((None, tm, FS), lambda i, j: (j, i, 0))
    vec = pl.BlockSpec((1, D), lambda i, j: (0, 0))
    return pl.pallas_call(
        body, name="ffn_bwd_act", grid=(s // tm, NSH),
        in_specs=[row(D), act, act, sh(FS, D), sh(FS, D), sh(FS, D), row(D), row(D), row(D), vec, vec],
        out_specs=[act, act, row(D), row(D), vec, vec],
        out_shape=[jax.ShapeDtypeStruct((NSH, s, FS), BF16), jax.ShapeDtypeStruct((NSH, s, FS), BF16),
                   jax.ShapeDtypeStruct((s, D), F32), jax.ShapeDtypeStruct((s, D), BF16),
                   jax.ShapeDtypeStruct((1, D), F32), jax.ShapeDtypeStruct((1, D), F32)],
        scratch_shapes=[pltpu.VMEM((tm, D), F32)],
        compiler_params=_cp(2))(df, gate, up, wg, wu, wd, dy, x1, mix, g3, g2)


SMEM_SPEC = pl.BlockSpec(memory_space=pltpu.SMEM)


def _emit_halves(acc_ref, row0, rows, c, own_ref, oth_ref):
    half = rows // 2
    own_ref[...] = acc_ref[pl.ds(row0 + pl.multiple_of(c * half, 8), half), :]
    oth_ref[...] = acc_ref[pl.ds(row0 + pl.multiple_of((1 - c) * half, 8), half), :].astype(BF16)


def _half_shapes(rows):
    return [jax.ShapeDtypeStruct((NSH, rows // 2, D), F32), jax.ShapeDtypeStruct((NSH, rows // 2, D), BF16)]


def _ffn_bwd_w(h2, gate, up, dgate, dup, df, c_arr):
    s = h2.shape[0]
    tm = min(TM_FFN_FWD, s)
    nt = s // tm

    def body(c_ref, h_ref, gate_ref, up_ref, dgate_ref, dup_ref, df_ref, *rest):
        outs, accs = rest[:6], rest[6:]
        i = pl.program_id(1)
        @pl.when(i == 0)
        def _():
            for acc in accs:
                acc[...] = jnp.zeros_like(acc)

        h = h_ref[...]
        accs[0][...] += _dot_tn(dgate_ref[...], h)
        accs[1][...] += _dot_tn(dup_ref[...], h)
        _, sl = _silu_parts(gate_ref[...].astype(F32))
        a = (sl * up_ref[...].astype(F32)).astype(BF16)
        accs[2][...] += _dot_tn(a, df_ref[...])

        @pl.when(i == nt - 1)
        def _():
            for t, acc in enumerate(accs):
                _emit_halves(acc, 0, FS, c_ref[0], outs[2 * t], outs[2 * t + 1])

    row = lambda w: pl.BlockSpec((tm, w), lambda j, i: (i, 0))
    act = pl.BlockSpec((None, tm, FS), lambda j, i: (j, i, 0))
    half = pl.BlockSpec((None, FS // 2, D), lambda j, i: (j, 0, 0))
    return pl.pallas_call(
        body, name="ffn_bwd_w", grid=(NSH, nt),
        in_specs=[SMEM_SPEC, row(D), act, act, act, act, row(D)],
        out_specs=[half] * 6,
        out_shape=_half_shapes(FS) * 3,
        scratch_shapes=[pltpu.VMEM((FS, D), F32)] * 3,
        compiler_params=_cp(2))(c_arr, h2, gate, up, dgate, dup, df)


def _outproj_bwd(dmix, w_out, pool_o, attn_o, c_arr, dep=None):
    s = dmix.shape[0]
    tm = min(TM, s)
    nt = s // tm

    def body(c_ref, dm_ref, w_ref, p_ref, a_ref, *rest):
        dpool_ref, dattn_ref, own_ref, oth_ref, gw_ref = rest[-5:]
        i = pl.program_id(0)

        @pl.when(i == 0)
        def _():
            gw_ref[...] = jnp.zeros_like(gw_ref)

        dm = dm_ref[...]
        dpool_ref[...] = _dot_nt(dm, w_ref[0:512, :])
        dattn_ref[...] = _dot_nt(dm, w_ref[512:1024, :]).astype(BF16)
        gw_ref[0:512, :] += _dot_tn(p_ref[...], dm)
        gw_ref[512:1024, :] += _dot_tn(a_ref[...], dm)

        @pl.when(i == nt - 1)
        def _():
            for k in range(NSH):
                _emit_halves(gw_ref, k * WOUT_S, WOUT_S, c_ref[0], own_ref.at[k], oth_ref.at[k])

    row = lambda w: pl.BlockSpec((tm, w), lambda i: (i, 0))
    full = pl.BlockSpec((D, D), lambda i: (0, 0))
    half = pl.BlockSpec((NSH, WOUT_S // 2, D), lambda i: (0, 0, 0))
    return pl.pallas_call(
        body, name="outproj_bwd", grid=(nt,),
        in_specs=[SMEM_SPEC, row(D), full, row(512), row(512)] + ([] if dep is None else [ANY]),
        out_specs=[row(512), row(512), half, half],
        out_shape=[jax.ShapeDtypeStruct((s, 512), F32), jax.ShapeDtypeStruct((s, 512), BF16)] + _half_shapes(WOUT_S),
        scratch_shapes=[pltpu.VMEM((D, D), F32)],
        compiler_params=_cp(1))(c_arr, dmix, w_out, pool_o, attn_o, *([] if dep is None else [dep]))


def _pool_bwd(pooled, dpool, w_pool, pool_scale, dep=None):
    s = pooled.shape[0]
    tm = min(TM_POOL, s)
    hb = tm // HALO
    nt = s // tm
    last_halo = s // HALO - 1

    def body(p_ref, dc_ref, dn_ref, wp_ref, sc_ref, *rest):
        du_ref, gwp_ref, gsc_ref = rest[-3:]
        i = pl.program_id(0)

        @pl.when(i == 0)
        def _():
            gwp_ref[...] = jnp.zeros_like(gwp_ref)
            gsc_ref[...] = jnp.zeros_like(gsc_ref)

        dcur = dc_ref[...]
        dnext = jnp.where(i < nt - 1, dn_ref[...], 0.0)
        dext = jnp.concatenate([dcur, dnext], axis=0)
        t_ext = i * tm + lax.broadcasted_iota(jnp.int32, (tm + HALO, GROUP), 0)
        for g, w in enumerate(WINDOWS):
            sl = slice(g * GROUP, (g + 1) * GROUP)
            pb = p_ref[:, sl]
            wp = wp_ref[g]
            mixed = _dot(pb, wp)
            gsc_ref[:, sl] += jnp.sum(dcur[:, sl] * mixed, axis=0, keepdims=True)
            dmixed = (dext[:, sl] * sc_ref[:, sl]).astype(BF16)
            gwp_ref[g] += _dot_tn(pb, dmixed[0:tm])
            dpooled = _dot_nt(dmixed, wp)
            z = dpooled / jnp.minimum(t_ext + 1, w).astype(F32)
            du_ref[:, sl] = (_window_sums(z, w, -1, tm, 2) - dpooled[0:tm]).astype(BF16)

    return pl.pallas_call(
        body, name="pool_bwd", grid=(nt,),
        in_specs=[pl.BlockSpec((tm, 512), lambda i: (i, 0)),
                  pl.BlockSpec((tm, 512), lambda i: (i, 0)),
                  pl.BlockSpec((HALO, 512), lambda i: (jnp.minimum((i + 1) * hb, last_halo), 0)),
                  pl.BlockSpec((4, GROUP, GROUP), lambda i: (0, 0, 0)),
                  pl.BlockSpec((1, 512), lambda i: (0, 0))] + ([] if dep is None else [ANY]),
        out_specs=[pl.BlockSpec((tm, 512), lambda i: (i, 0)),
                   pl.BlockSpec((4, GROUP, GROUP), lambda i: (0, 0, 0)),
                   pl.BlockSpec((1, 512), lambda i: (0, 0))],
        out_shape=[jax.ShapeDtypeStruct((s, 512), BF16), jax.ShapeDtypeStruct((4, GROUP, GROUP), F32),
                   jax.ShapeDtypeStruct((1, 512), F32)],
        compiler_params=_cp(1))(pooled, dpool, dpool, w_pool, pool_scale, *([] if dep is None else [dep]))


def _attn_bwd(q, k, v, do, probs, sink_share, bucket, dep=None):
    s = q.shape[0]
    nblk = s // BLK

    def body(q_ref, do_ref, k_ref, v_ref, prob_ref, ps_ref, bk_ref, *rest):
        dq_ref, dk_ref, dv_ref, grb_ref, gsk_ref, dss_ref = rest[-6:]
        n = pl.program_id(0)

        @pl.when(n == 0)
        def _():
            dk_ref[...] = jnp.zeros_like(dk_ref)
            dv_ref[...] = jnp.zeros_like(dv_ref)
            dss_ref[...] = jnp.zeros_like(dss_ref)
            gsk_ref[...] = jnp.zeros_like(gsk_ref)

        kt, (lo, pstart, cstart) = _kv_band(k_ref, n, True)
        vv, _ = _kv_band(v_ref, n, False)
        lo_q = lax.broadcasted_iota(jnp.int32, (BLK, 128), 1) < HD
        dkk = [jnp.zeros((2 * BLK, 128), F32), jnp.zeros((2 * BLK, 128), F32)]
        dvv = [jnp.zeros((2 * BLK, 128), F32), jnp.zeros((2 * BLK, 128), F32)]

        def by_head(t):
            return jnp.concatenate([jnp.where(lo_q, t, 0.0), jnp.where(lo_q, 0.0, t)], axis=0).astype(BF16)

        for p in range(4):
            h = p // 2
            qt = q_ref[:, p * 128:(p + 1) * 128].astype(F32) * SCALE
            dot = do_ref[:, p * 128:(p + 1) * 128]
            pb = prob_ref[pl.ds(2 * p, 2)]
            prob = pb.astype(F32)
            ps = ps_ref[pl.ds(2 * p, 2), :].reshape(2, 1, BLK)
            dp = _dot_nt(vv[h], dot).reshape(2, 2 * BLK, BLK)
            delta = jnp.sum(prob * dp, axis=1, keepdims=True)
            ds = prob * (dp - delta)
            sink_part = ps * delta
            for e in range(2):
                gs = -jnp.sum(sink_part[e]).reshape(1, 1)
                gsk_ref[pl.ds(2 * p + e, 1), :] += jnp.broadcast_to(gs, (1, 128))
            dss_ref[pl.ds(2 * p, 2)] += ds
            dsb = ds.astype(BF16)
            dq_t = _dot(kt[h], dsb.reshape(4 * BLK, BLK))
            dq_ref[:, p * 128:(p + 1) * 128] = (dq_t.T * SCALE).astype(BF16)
            dkk[h] = dkk[h] + _dot(jnp.concatenate([dsb[0], dsb[1]], axis=1), by_head(qt))
            dvv[h] = dvv[h] + _dot(jnp.concatenate([pb[0], pb[1]], axis=1), by_head(dot.astype(F32)))
        for acc, ref in ((dkk, dk_ref), (dvv, dv_ref)):
            f0 = acc[0] + pltpu.roll(acc[0], HD, axis=1)
            f1 = acc[1] + pltpu.roll(acc[1], HD, axis=1)
            band = jnp.where(lo, f0, f1)
            ref[pl.ds(pstart, BLK), :] += band[0:BLK]
            ref[pl.ds(cstart, BLK), :] += band[BLK:2 * BLK]

        @pl.when(n == nblk - 1)
        def _():
            _relbias_grad(dss_ref, bk_ref[...], grb_ref)

    blk = pl.BlockSpec((BLK, 512), lambda i: (i, 0))
    kv = pl.BlockSpec((s, 128), lambda i: (0, 0))
    row8 = pl.BlockSpec((NQ, 128), lambda i: (0, 0))
    return pl.pallas_call(
        body, name="attn_bwd", grid=(nblk,),
        in_specs=[blk, blk, kv, kv, pl.BlockSpec((None, NQ, 2 * BLK, BLK), lambda i: (i, 0, 0, 0)),
                  pl.BlockSpec((None, NQ, BLK), lambda i: (i, 0, 0)), pl.BlockSpec((2 * BLK, BLK), lambda i: (0, 0))]
        + ([] if dep is None else [ANY]),
        out_specs=[blk, kv, kv, row8, row8],
        out_shape=[jax.ShapeDtypeStruct((s, 512), BF16), jax.ShapeDtypeStruct((s, 128), F32),
                   jax.ShapeDtypeStruct((s, 128), F32), jax.ShapeDtypeStruct((NQ, 128), F32),
                   jax.ShapeDtypeStruct((NQ, 128), F32)],
        scratch_shapes=[pltpu.VMEM((NQ, 2 * BLK, BLK), F32)],
        compiler_params=_cp(1))(q, do, k, v, probs, sink_share, bucket, *([] if dep is None else [dep]))


def _relbias_grad(ds_ref, bucket_v, o_ref):
    lane = lax.broadcasted_iota(jnp.int32, (NQ, 128), 1)
    head_row = lax.broadcasted_iota(jnp.int32, (NQ, BLK), 0)
    acc = jnp.zeros((NQ, 128), F32)
    for b in range(NBUCKET):
        sel = bucket_v == b
        stack = jnp.zeros((NQ, BLK), F32)
        for h in range(NQ):
            col_sums = jnp.sum(jnp.where(sel, ds_ref[h], 0.0), axis=0, keepdims=True)
            stack = jnp.where(head_row == h, col_sums, stack)
        acc = acc + jnp.where(lane == b, jnp.sum(stack, axis=1, keepdims=True), 0.0)
    o_ref[...] = acc


def _inproj_bwd(x, g1, du, dq, dk, dv, w_in, dx1, c_arr, dep=None):
    s = x.shape[0]
    tm = min(TM, s)
    nt = s // tm
    cols = ((0, 512), (512, 1024), (1024, 1152), (1152, 1280))

    def body(c_ref, x_ref, g_ref, du_ref, dq_ref, dk_ref, dv_ref, w_ref, dx1_ref, *rest):
        gx_ref, own_ref, oth_ref, gg_ref, gw_ref = rest[-5:]
        i = pl.program_id(0)

        @pl.when(i == 0)
        def _():
            gw_ref[...] = jnp.zeros_like(gw_ref)
            gg_ref[...] = jnp.zeros_like(gg_ref)

        gv = g_ref[...]
        r1, n1 = _rms(x_ref[...])
        h = (n1 * gv).astype(BF16)
        parts = (du_ref[...], dq_ref[...], dk_ref[...].astype(BF16), dv_ref[...].astype(BF16))
        dh = None
        for (a, b), dpart in zip(cols, parts):
            t = _dot(dpart, w_ref[a:b, :])
            dh = t if dh is None else dh + t
            gw_ref[a:b, :] += _dot_tn(dpart, h)
        dx, dg = _rms_bwd(r1, n1, gv, dh)
        gg_ref[...] += dg
        gx_ref[...] = dx1_ref[...] + dx

        @pl.when(i == nt - 1)
        def _():
            for k in range(NSH):
                _emit_halves(gw_ref, k * WIN_S, WIN_S, c_ref[0], own_ref.at[k], oth_ref.at[k])

    row = lambda w: pl.BlockSpec((tm, w), lambda i: (i, 0))
    vec = pl.BlockSpec((1, D), lambda i: (0, 0))
    full = pl.BlockSpec((IN_W, D), lambda i: (0, 0))
    half = pl.BlockSpec((NSH, WIN_S // 2, D), lambda i: (0, 0, 0))
    return pl.pallas_call(
        body, name="inproj_bwd", grid=(nt,),
        in_specs=[SMEM_SPEC, row(D), vec, row(512), row(512), row(128), row(128), full, row(D)]
        + ([] if dep is None else [ANY]),
        out_specs=[row(D), half, half, vec],
        out_shape=[jax.ShapeDtypeStruct((s, D), F32)] + _half_shapes(WIN_S) + [jax.ShapeDtypeStruct((1, D), F32)],
        scratch_shapes=[pltpu.VMEM((IN_W, D), F32)],
        compiler_params=_cp(1))(c_arr, x, g1, du, dq, dk, dv, w_in, dx1, *([] if dep is None else [dep]))


def _local_step(x, target, small, w_in, w_out, ffn_weights, c_arr, on_ffn_grads=None, on_wout_grads=None,
                on_attn_grads=None):
    g1, g2, g3, g4, w_pool, pool_scale, rel_bias, sinks = small
    bucket = jnp.asarray(_bucket_table())
    wp_bf = w_pool.astype(BF16)
    bias = _bias_table(bucket, rel_bias)
    h1 = _prenorm(x, g1)
    if callable(w_in):
        w_in = w_in(bias, h1)
    u, q, k, v = _inproj_fwd(h1, w_in)
    pool_o, pooled = _pool_fwd(u, wp_bf, pool_scale)
    attn_o, probs, sink_share = _attn_fwd(q, k, v, bias, sinks)
    g2_fwd = g2
    if callable(w_out):
        w_out, after = w_out(pool_o, attn_o)
        g2_fwd = g2 + after[:1, :1]
    mix, x1, h2 = _outproj_fwd(pool_o, attn_o, w_out, x, g2_fwd, g3)
    wg, wu, wd = ffn_weights(h2) if callable(ffn_weights) else ffn_weights
    gate, up, df, dy, loss, gg4 = _ffn_fwd(h2, wg, wu, wd, x1, target, g4)
    dgate, dup, dx1, dmix, gg3, gg2 = _ffn_bwd_act(df, gate, up, wg, wu, wd, dy, x1, mix, g3, g2)
    ffn_g = _ffn_bwd_w(h2, gate, up, dgate, dup, df, c_arr)
    dep = None if on_ffn_grads is None else on_ffn_grads(ffn_g)
    dpool, dattn, wout_own, wout_oth = _outproj_bwd(dmix, w_out, pool_o, attn_o, c_arr, dep)
    dep = None if on_wout_grads is None else on_wout_grads(wout_own, wout_oth, dpool)
    du, gwp, gsc = _pool_bwd(pooled, dpool, wp_bf, pool_scale, dep)
    dq, dk, dv, grb, gsk = _attn_bwd(q, k, v, dattn, probs, sink_share, bucket, dep)
    dep = None if on_attn_grads is None else on_attn_grads([du, dq])
    gx, win_own, win_oth, gg1 = _inproj_bwd(x, g1, du, dq, dk, dv, w_in, dx1, c_arr, dep)
    small_grads = (gg1, gg2, gg3, gg4, gwp, gsc, grb[:, :NBUCKET].T, gsk[:, 0].reshape(1, NQ))
    return loss, gx, small_grads, ((win_own, win_oth), (wout_own, wout_oth), ffn_g)


def _place():
    x, y, c = lax.axis_index("x"), lax.axis_index("y"), lax.axis_index("c")
    chips = ((1 - x, y), (x, 1 - y), (1 - x, 1 - y))
    return x, y, c, chips


def _remote(src, dst, ssem, rsem, dev):
    return pltpu.make_async_remote_copy(src_ref=src, dst_ref=dst, send_sem=ssem, recv_sem=rsem,
                                        device_id=dev, device_id_type=MESH_T)


def _to_sibling(arrs, name, after=()):
    nt, na = len(arrs), len(after)

    def body(*refs):
        srcs, dsts = refs[:nt], refs[nt + na:2 * nt + na]
        ssem, rsem = refs[2 * nt + na:]
        x, y, c, _ = _place()
        cps = [_remote(srcs[t], dsts[t], ssem.at[t], rsem.at[t], (x, y, 1 - c)) for t in range(nt)]
        for cp in cps:
            cp.start()
        for cp in cps:
            cp.wait()

    return pl.pallas_call(
        body, name=name, in_specs=[ANY] * (nt + na), out_specs=[ANY] * nt,
        out_shape=[jax.ShapeDtypeStruct(a.shape, a.dtype) for a in arrs],
        scratch_shapes=[pltpu.SemaphoreType.DMA((nt,)), pltpu.SemaphoreType.DMA((nt,))],
        compiler_params=_cp())(*arrs, *after)


HBM_SPEC = pl.BlockSpec(memory_space=pltpu.HBM)
SEM_SPEC = pl.BlockSpec(memory_space=pltpu.SEMAPHORE)
DATAFLOW = pltpu.SideEffectType.DATAFLOW_SIDE_EFFECTING


def _gather_copies(srcs, lands, ssem, rsem):
    x, y, c, chips = _place()
    me = 2 * x + y
    out = []
    for t in range(len(srcs)):
        half = srcs[t].shape[0] // 2
        mine = pl.ds(pl.multiple_of(c * half, 16), half)
        for j, (px, py) in enumerate(chips):
            k = 3 * t + j
            send = _remote(srcs[t].at[mine], lands[t].at[me, mine], ssem.at[k], rsem.at[k], (px, py, c))
            blk = lands[t].at[2 * px + py, mine]
            out.append((send, _remote(blk, blk, ssem.at[k], rsem.at[k], (px, py, c))))
    return out


def _scatter_copies(srcs, lands, ssem, rsem):
    x, y, c, chips = _place()
    out = []
    for t in range(len(srcs)):
        for j, (px, py) in enumerate(chips):
            k = 3 * t + j
            send = _remote(srcs[t].at[2 * px + py], lands[t].at[j], ssem.at[k], rsem.at[k], (px, py, c))
            out.append((send, _remote(lands[t].at[j], lands[t].at[j], ssem.at[k], rsem.at[k], (px, py, c))))
    return out


def _sibling_copies(srcs, lands, ssem, rsem):
    x, y, c, _ = _place()
    sib = (x, y, 1 - c)
    return [(_remote(srcs[t], lands[t], ssem.at[t], rsem.at[t], sib),
             _remote(lands[t], lands[t], ssem.at[t], rsem.at[t], sib)) for t in range(len(srcs))]


def _forward_copies(srcs, lands, ssem, rsem):
    x, y, c, chips = _place()
    sib = (x, y, 1 - c)
    out = []
    for t in range(len(lands)):
        half = lands[t].shape[1] // 2
        mine = pl.ds(pl.multiple_of(c * half, 16), half)
        other = pl.ds(pl.multiple_of((1 - c) * half, 16), half)
        for j, (px, py) in enumerate(chips):
            k = 3 * t + j
            blk_m, blk_o = lands[t].at[2 * px + py, mine], lands[t].at[2 * px + py, other]
            out.append((_remote(blk_m, blk_m, ssem.at[k], rsem.at[k], sib),
                        _remote(blk_o, blk_o, ssem.at[k], rsem.at[k], sib)))
    return out


def _peer_copies(srcs, lands, ssem, rsem):
    x, y, c, _ = _place()
    me = 4 * x + 2 * y + c
    out = []
    for kk in range(1, 8):
        px, py, pc = x ^ (kk >> 2), y ^ ((kk >> 1) & 1), c ^ (kk & 1)
        send = _remote(srcs[0], lands[0].at[me], ssem.at[kk - 1], rsem.at[kk - 1], (px, py, pc))
        slot = lands[0].at[4 * px + 2 * py + pc]
        out.append((send, _remote(slot, slot, ssem.at[kk - 1], rsem.at[kk - 1], (px, py, pc))))
    return out


def _split_start(plan, ncopy, srcs, lands, after, name):
    ns, nbuf = len(srcs), len(srcs) + len(lands)
    extra = [] if after is None else [after]
    n_in = nbuf + len(extra)

    def body(*refs):
        ssem, rsem, token = refs[n_in], refs[n_in + 1], refs[-1]
        for send, _ in plan(refs[:ns], refs[ns:nbuf], ssem, rsem):
            send.start()
        token[...] = jnp.zeros_like(token)

    bufs = [pltpu.with_memory_space_constraint(a, pltpu.HBM) for a in list(srcs) + list(lands)]
    outs = pl.pallas_call(
        body, name=name, in_specs=[HBM_SPEC] * nbuf + [ANY] * len(extra),
        out_specs=[SEM_SPEC, SEM_SPEC] + [HBM_SPEC] * nbuf + [pl.BlockSpec(memory_space=pltpu.VMEM)],
        out_shape=[pltpu.SemaphoreType.DMA((ncopy,)), pltpu.SemaphoreType.DMA((ncopy,))]
        + [pltpu.HBM(a.shape, a.dtype) for a in bufs] + [jax.ShapeDtypeStruct((8, 128), F32)],
        input_output_aliases={i: 2 + i for i in range(nbuf)},
        compiler_params=pltpu.CompilerParams(has_side_effects=DATAFLOW))(*bufs, *extra)
    return outs[0], outs[1], outs[2:2 + ns], outs[2 + ns:2 + nbuf], outs[-1]


def _split_wait(plan, ssem, rsem, srcs, lands, after, name, only=None):
    ns, nbuf = len(srcs), len(srcs) + len(lands)

    def body(*refs):
        s_ref, r_ref = refs[nbuf], refs[nbuf + 1]
        for k, (send, recv) in enumerate(plan(refs[:ns], refs[ns:nbuf], s_ref, r_ref)):
            if only is None or k in only:
                send.wait_send()
                recv.wait_recv()

    outs = pl.pallas_call(
        body, name=name, in_specs=[HBM_SPEC] * nbuf + [SEM_SPEC, SEM_SPEC] + [ANY] * len(after),
        out_specs=[HBM_SPEC] * nbuf,
        out_shape=[pltpu.HBM(a.shape, a.dtype) for a in list(srcs) + list(lands)],
        input_output_aliases={i: i for i in range(nbuf)},
        compiler_params=pltpu.CompilerParams(has_side_effects=DATAFLOW))(*srcs, *lands, ssem, rsem, *after)
    return outs


def _gather_finish(lands):
    nt = len(lands)

    def body(*refs):
        outs = refs[nt:2 * nt]
        dsend, drecv = refs[2 * nt:]
        x, y, c, chips = _place()
        sib = (x, y, 1 - c)
        sends = []
        for t in range(nt):
            half = outs[t].shape[1] // 2
            mine = pl.ds(pl.multiple_of(c * half, 16), half)
            for j, (px, py) in enumerate(chips):
                blk = outs[t].at[2 * px + py, mine]
                cp = _remote(blk, blk, dsend.at[t, j], drecv.at[t, j], sib)
                cp.start()
                sends.append(cp)
        for t in range(nt):
            half = outs[t].shape[1] // 2
            other = pl.ds(pl.multiple_of((1 - c) * half, 16), half)
            for j, (px, py) in enumerate(chips):
                blk = outs[t].at[2 * px + py, other]
                _remote(blk, blk, dsend.at[t, j], drecv.at[t, j], sib).wait_recv()
        for cp in sends:
            cp.wait_send()

    return pl.pallas_call(
        body, name="gather_finish", in_specs=[ANY] * nt, out_specs=[ANY] * nt,
        out_shape=[jax.ShapeDtypeStruct(a.shape, a.dtype) for a in lands],
        input_output_aliases={t: t for t in range(nt)},
        scratch_shapes=[pltpu.SemaphoreType.DMA((nt, 3)), pltpu.SemaphoreType.DMA((nt, 3))],
        compiler_params=_cp())(*lands)


def _chip_partial(owns, recv, chip_arr, tag):
    nt = len(owns)

    def body(chip_ref, *refs):
        k = pl.program_id(0)
        for t in range(nt):
            p = refs[t][...] + refs[nt + t][...].astype(F32)
            refs[2 * nt + t][...] = p.astype(BF16)

            @pl.when(k == chip_ref[0])
            def _():
                refs[3 * nt + t][...] = p

    blk_specs = [pl.BlockSpec((None,) + a.shape[1:], lambda k, chip_ref: (k, 0, 0)) for a in owns]
    own_specs = [pl.BlockSpec(a.shape[1:], lambda k, chip_ref: (0, 0)) for a in owns]
    return pl.pallas_call(
        body, name="rs_chip_partial_" + tag,
        grid_spec=pltpu.PrefetchScalarGridSpec(num_scalar_prefetch=1, grid=(NSH,), in_specs=blk_specs * 2,
                                               out_specs=blk_specs + own_specs),
        out_shape=[jax.ShapeDtypeStruct(a.shape, BF16) for a in owns]
        + [jax.ShapeDtypeStruct(a.shape[1:], F32) for a in owns],
        compiler_params=_cp(1))(chip_arr, *owns, *recv)


def _sum_chips(own, recv, tag, after=()):
    nt = len(own)

    def body(*refs):
        outs = refs[2 * nt + len(after):]
        for t in range(nt):
            r = refs[nt + t]
            outs[t][...] = ((refs[t][...] + r[0].astype(F32)) + r[1].astype(F32)) + r[2].astype(F32)

    vm = pl.BlockSpec(memory_space=pltpu.VMEM)
    return pl.pallas_call(
        body, name="rs_sum_chips_" + tag, in_specs=[vm] * (2 * nt) + [ANY] * len(after), out_specs=[vm] * nt,
        out_shape=[jax.ShapeDtypeStruct(a.shape, F32) for a in own],
        compiler_params=_cp())(*own, *recv, *after)


def _adamw_math(w, g, m, v):
    m = ADAM_B1 * m + (1.0 - ADAM_B1) * g
    v = ADAM_B2 * v + (1.0 - ADAM_B2) * (g * g)
    m_hat = m / (1.0 - ADAM_B1 ** ADAM_STEP)
    v_hat = v / (1.0 - ADAM_B2 ** ADAM_STEP)
    delta = -ADAM_LR * (m_hat / (jnp.sqrt(v_hat) + ADAM_EPS) + ADAM_WD * w)
    return delta, m, v


ADAM_SPLIT = 4


def _adamw_shards(own, sib, ws, ms, vs, c_arr, tag):
    nt = len(own)

    def body(c_ref, *refs):
        hh = pl.program_id(0)
        mine = hh == c_ref[0]
        for t in range(nt):
            g = jnp.where(mine, refs[t][...], refs[nt + t][...])
            delta, m, v = _adamw_math(refs[2 * nt + t][...], g, refs[3 * nt + t][...], refs[4 * nt + t][...])
            refs[5 * nt + t][...] = g
            refs[6 * nt + t][...] = delta
            refs[7 * nt + t][...] = m
            refs[8 * nt + t][...] = v

    def tile(a):
        return (a.shape[0] // ADAM_SPLIT, a.shape[1])

    half_specs = [pl.BlockSpec(tile(a), lambda hh, q, c_ref: (q, 0)) for a in own]
    full_specs = [pl.BlockSpec(tile(a), lambda hh, q, c_ref: (hh * ADAM_SPLIT + q, 0)) for a in own]
    return pl.pallas_call(
        body, name="adamw_shards_" + tag,
        grid_spec=pltpu.PrefetchScalarGridSpec(num_scalar_prefetch=1, grid=(2, ADAM_SPLIT),
                                               in_specs=half_specs * 2 + full_specs * 3, out_specs=full_specs * 4),
        out_shape=[jax.ShapeDtypeStruct(w.shape, F32) for w in ws] * 4,
        compiler_params=_cp(2))(c_arr, *own, *sib, *ws, *ms, *vs)


SMALL_WPOOL_ROW = 32
SMALL_SCALE_ROW = SMALL_WPOOL_ROW + 4 * GROUP
SMALL_BIAS_ROW = SMALL_SCALE_ROW + 8
SMALL_SINKS_ROW = SMALL_BIAS_ROW + NBUCKET
SMALL_LOSS_ROW = SMALL_SINKS_ROW + 8


def _small_sum_adamw(gathered, wp, mp, vp):
    rows = wp.shape[0]

    def body(g_ref, w_ref, m_ref, v_ref, *rest):
        outs, res = rest[:-1], rest[-1]
        g = g_ref[0]
        for d in range(1, 8):
            g = g + g_ref[d]
        delta, m, v = _adamw_math(w_ref[...], g, m_ref[...], v_ref[...])
        for kind, val in enumerate((g, delta, m, v)):
            res[kind] = val
            gains = outs[8 * kind:8 * kind + 4]
            w_pool_o, scale_o, bias_o, sinks_o = outs[8 * kind + 4:8 * kind + 8]
            for t in range(4):
                for i in range(8):
                    gains[t][:, 128 * i:128 * (i + 1)] = res[kind, pl.ds(8 * t + i, 1), :]
            for grp in range(4):
                w_pool_o[grp] = res[kind, pl.ds(SMALL_WPOOL_ROW + GROUP * grp, GROUP), :]
            for i in range(4):
                scale_o[:, 128 * i:128 * (i + 1)] = res[kind, pl.ds(SMALL_SCALE_ROW + i, 1), :]
            bias_o[...] = res[kind, pl.ds(SMALL_BIAS_ROW, NBUCKET), :][:, 0:NQ]
            sinks_o[...] = res[kind, pl.ds(SMALL_SINKS_ROW, 1), :][:, 0:NQ]
        outs[-1][...] = res[0, pl.ds(SMALL_LOSS_ROW, 1), :]

    vm = pl.BlockSpec(memory_space=pltpu.VMEM)
    one_kind = [jax.ShapeDtypeStruct((1, D), F32)] * 4 + [
        jax.ShapeDtypeStruct((4, GROUP, GROUP), F32), jax.ShapeDtypeStruct((1, POOL_W), F32),
        jax.ShapeDtypeStruct((NBUCKET, NQ), F32), jax.ShapeDtypeStruct((1, NQ), F32)]
    return pl.pallas_call(
        body, name="small_sum_adamw", in_specs=[vm] * 4, out_specs=[vm] * 33,
        out_shape=one_kind * 4 + [jax.ShapeDtypeStruct((1, 128), F32)],
        scratch_shapes=[pltpu.VMEM((4, rows, 128), F32)],
        compiler_params=_cp())(gathered, wp, mp, vp)


def _pack_small(gains4, w_pool, pool_scale, rel_bias, sinks, loss):
    parts = list(gains4) + [w_pool, pool_scale, jnp.pad(rel_bias, ((0, 0), (0, 128 - NQ))), sinks, loss]
    rows = []
    for a in parts:
        flat = a.reshape(-1)
        n = flat.shape[0]
        padded = -(-n // 1024) * 1024
        rows.append(jnp.pad(flat, (0, padded - n)).reshape(-1, 128))
    return jnp.concatenate(rows, axis=0)


def kernel(x, g_pre_mix, w_in, w_pool, pool_scale, rel_bias, sinks, w_out, g_post_mix, g_pre_ffn, w_gate, w_up, w_down, g_post_ffn, loss_target, m_g_pre_mix, m_w_in, m_w_pool, m_pool_scale, m_rel_bias, m_sinks, m_w_out, m_g_post_mix, m_g_pre_ffn, m_w_gate, m_w_up, m_w_down, m_g_post_ffn, v_g_pre_mix, v_w_in, v_w_pool, v_pool_scale, v_rel_bias, v_sinks, v_w_out, v_g_post_mix, v_g_pre_ffn, v_w_gate, v_w_up, v_w_down, v_g_post_ffn):
    c = lax.axis_index("c")
    chip = 2 * lax.axis_index("x") + lax.axis_index("y")

    def shards(a_in, a_out, a_gate, a_up, a_down):
        return (a_in[0].T, a_out[0], a_gate[0].T, a_up[0].T, a_down[0])

    c_arr = c.reshape(1).astype(jnp.int32)
    chip_arr = chip.reshape(1).astype(jnp.int32)

    big_w = shards(w_in, w_out, w_gate, w_up, w_down)
    big_bf = [w.astype(BF16) for w in big_w]
    g_ssem, g_rsem, g_srcs, g_lands, _ = _split_start(
        _gather_copies, 15, big_bf, [lax.empty((NSH,) + a.shape, BF16) for a in big_bf], None, "gather_start")
    fw = {}

    def with_own(land, shard):
        return lax.dynamic_update_slice(land, shard[None], (chip, 0, 0))

    def w_in_ready(*after):
        fw["first"] = _split_wait(_gather_copies, g_ssem, g_rsem, g_srcs, g_lands, after, "gather_win_wait",
                                  only=(0, 1, 2))
        (fw["win"],) = _gather_finish([with_own(fw["first"][5], fw["first"][0])])
        return fw["win"].reshape(IN_W, D)

    def w_out_ready(*after):
        first = fw["first"]
        outs = _split_wait(_gather_copies, g_ssem, g_rsem, first[:5], [fw["win"]] + list(first[6:]), after,
                           "gather_rest_wait", only=tuple(range(3, 15)))
        lands = [with_own(land, src) for src, land in zip(outs[1:5], outs[6:])]
        (wout_all,) = _gather_finish(lands[:1])
        fw["f"] = _split_start(_forward_copies, 9, [], lands[1:], wout_all, "gather_forward_start")
        return wout_all.reshape(D, D), fw["f"][4]

    def ffn_weights(after):
        ssem, rsem, _, lands, _ = fw["f"]
        return _split_wait(_forward_copies, ssem, rsem, [], lands, [after], "gather_forward_wait")

    rs = {}

    def on_ffn_grads(ffn_g):
        oths = ffn_g[1::2]
        rs["ffn_own"] = ffn_g[0::2]
        rs["x"] = _split_start(_sibling_copies, 3, oths, [lax.empty(a.shape, BF16) for a in oths], ffn_g[0],
                               "rs_exchange_ffn_start")
        return rs["x"][4]

    def on_wout_grads(own, oth, after):
        ssem, rsem, srcs, lands, _ = rs["x"]
        recv_ffn = _split_wait(_sibling_copies, ssem, rsem, srcs, lands, [after], "rs_exchange_ffn_wait")[3:]
        recv_wout = _to_sibling([oth], "rs_exchange_wout")
        outs = _chip_partial([own] + list(rs["ffn_own"]), list(recv_wout) + list(recv_ffn), chip_arr, "early")
        rs["early_own"] = outs[4:]
        rs["s"] = _split_start(_scatter_copies, 12, outs[:4],
                               [lax.empty((3,) + a.shape[1:], BF16) for a in outs[:4]], outs[4], "scatter_early_start")
        return rs["s"][4]

    small = (g_pre_mix, g_post_mix, g_pre_ffn, g_post_ffn, w_pool[0], pool_scale, rel_bias, sinks)
    loss, gx, small_grads, ((win_own, win_oth), _, _) = _local_step(
        x[0], loss_target[0], small, w_in_ready, w_out_ready, ffn_weights, c_arr, on_ffn_grads, on_wout_grads)

    ssem, rsem, srcs, lands, _ = rs["s"]
    recv_early = _split_wait(_scatter_copies, ssem, rsem, srcs, lands, [gx], "scatter_early_wait")[4:]
    finals = _sum_chips(list(rs["early_own"]), list(recv_early), "early")
    sh_ssem, sh_rsem, sh_srcs, sh_lands, sh_token = _split_start(
        _sibling_copies, 4, finals, [lax.empty(a.shape, F32) for a in finals], finals[0], "rs_share_early_start")

    unused = jnp.zeros((1, 1), F32)
    gp = _pack_small(small_grads[:4], *small_grads[4:], loss)
    wp = _pack_small((g_pre_mix, g_post_mix, g_pre_ffn, g_post_ffn), w_pool, pool_scale, rel_bias, sinks, unused)
    mp = _pack_small((m_g_pre_mix, m_g_post_mix, m_g_pre_ffn, m_g_post_ffn), m_w_pool, m_pool_scale, m_rel_bias,
                     m_sinks, unused)
    vp = _pack_small((v_g_pre_mix, v_g_post_mix, v_g_pre_ffn, v_g_post_ffn), v_w_pool, v_pool_scale, v_rel_bias,
                     v_sinks, unused)

    moments = (shards(m_w_in, m_w_out, m_w_gate, m_w_up, m_w_down),
               shards(v_w_in, v_w_out, v_w_gate, v_w_up, v_w_down))
    recv_a = _to_sibling([win_oth], "rs_exchange_win", [sh_token])
    outs = _chip_partial([win_own], recv_a, chip_arr, "win")
    w_ssem, w_rsem, w_srcs, w_lands, w_token = _split_start(
        _scatter_copies, 3, outs[:1], [lax.empty((3,) + outs[0].shape[1:], BF16)], gx, "scatter_win_start")
    slots = lax.dynamic_update_slice(lax.empty((8,) + gp.shape, F32), gp[None], (2 * chip + c, 0, 0))
    p_ssem, p_rsem, p_srcs, p_lands, p_token = _split_start(_peer_copies, 7, [gp], [slots], w_token,
                                                            "small_allgather_start")
    shared = _split_wait(_sibling_copies, sh_ssem, sh_rsem, sh_srcs, sh_lands, [p_token], "rs_share_early_wait")
    adam_e = _adamw_shards(shared[:4], shared[4:], big_w[1:], moments[0][1:], moments[1][1:], c_arr, "early")
    recv_win = _split_wait(_scatter_copies, w_ssem, w_rsem, w_srcs, w_lands, [adam_e[0]], "scatter_win_wait")[1:]
    win_final = _sum_chips([outs[1]], recv_win, "win")
    win_sib = _to_sibling(win_final, "rs_share_win")
    adam_w = _adamw_shards(win_final, win_sib, big_w[:1], moments[0][:1], moments[1][:1], c_arr, "win")
    big_g, big_d, big_m, big_v = [[adam_w[i]] + list(adam_e[4 * i:4 * i + 4]) for i in range(4)]

    gathered = _split_wait(_peer_copies, p_ssem, p_rsem, p_srcs, p_lands, [adam_w[0]], "small_allgather_wait")[1]
    flat = _small_sum_adamw(gathered, wp, mp, vp)
    small_out = [flat[8 * kind:8 * kind + 8] for kind in range(4)]
    total_loss = flat[-1][0, 0]

    def ordered(small8, big4):
        sg1, sg2, sg3, sg4, swp, ssc, srb, ssk = small8
        swp = swp[None]
        bwin, bwout, bwg, bwu, bwd = big4[0].T[None], big4[1][None], big4[2].T[None], big4[3].T[None], big4[4][None]
        return [sg1, bwin, swp, ssc, srb, ssk, bwout, sg2, sg3, bwg, bwu, bwd, sg4]

    res = [total_loss, gx[None]]
    for sm, bg in zip(small_out, (big_g, big_d, big_m, big_v)):
        res += ordered(sm, bg)
    return tuple(res)
```

```python
import numpy as np
import jax
import jax.numpy as jnp
from jax import lax
from jax.experimental import pallas as pl
from jax.experimental.pallas import tpu as pltpu

F32 = jnp.float32
BF16 = jnp.bfloat16

D = 1024
POOL_W = 512
GROUP = 128
WINDOWS = (2, 4, 8, 16)
HALO = 128
NQ = 8
BLK = 128
NBUCKET = 32
IN_W = 1280
DFF = 2816
NSH = 4
FS = DFF // NSH
WIN_S = IN_W // NSH
WOUT_S = D // NSH
EPS = 1e-6
NEG = -1e30
SCALE = 0.125

ADAM_LR = 0.001
ADAM_B1 = 0.9
ADAM_B2 = 0.999
ADAM_EPS = 1e-08
ADAM_WD = 0.01
ADAM_STEP = 10

TM = 512
TM_POOL = 256
TM_FFN = 512
TM_FFN_FWD = 1024
FFN_ROWS = 256
VMEM_LIMIT = 60 * 1024 * 1024

MESH_T = pl.DeviceIdType.MESH
ANY = pl.BlockSpec(memory_space=pl.ANY)


def _cp(n_grid=0, **kw):
    sem = ("arbitrary",) * n_grid if n_grid else None
    return pltpu.CompilerParams(dimension_semantics=sem, vmem_limit_bytes=VMEM_LIMIT, **kw)


def _dot(a, b):
    return jnp.dot(a, b, preferred_element_type=F32)


def _dot_nt(a, b):
    return lax.dot_general(a, b, (((1,), (1,)), ((), ())), preferred_element_type=F32)


def _dot_tn(a, b):
    return lax.dot_general(a, b, (((0,), (0,)), ((), ())), preferred_element_type=F32)


def _rms(x):
    r = lax.rsqrt(jnp.mean(x * x, axis=-1, keepdims=True) + EPS)
    return r, x * r


def _rms_bwd(r, n, g, dout):
    dn = dout * g
    dx = r * (dn - n * jnp.mean(dn * n, axis=-1, keepdims=True))
    dg = jnp.sum(dout * n, axis=0, keepdims=True)
    return dx, dg


def _split(x, n):
    parts = []
    r = x
    for _ in range(n):
        p = r.astype(BF16)
        parts.append(p)
        r = r - p.astype(F32)
    return parts


def _band_dot(a, x, n):
    acc = None
    for p in _split(x, n):
        t = _dot(a, p)
        acc = t if acc is None else acc + t
    return acc


def _bucket_table():
    qi = np.arange(BLK)[None, :]
    kj = np.arange(2 * BLK)[:, None]
    dist = qi + BLK - kj
    n = np.maximum(dist, 0)
    nf = np.maximum(n, 1).astype(np.float32)
    large = 16 + (np.log(nf / np.float32(16)) / np.float32(np.log(128 / 16)) * np.float32(16)).astype(np.int32)
    large = np.minimum(large, NBUCKET - 1)
    return np.where(n < 16, n, large).astype(np.int32)


def _prenorm(x, g1):
    s = x.shape[0]
    tm = min(TM, s)

    def body(x_ref, g_ref, h_ref):
        _, n = _rms(x_ref[...])
        h_ref[...] = (n * g_ref[...]).astype(BF16)

    row = pl.BlockSpec((tm, D), lambda i: (i, 0))
    return pl.pallas_call(
        body, name="prenorm", grid=(s // tm,),
        in_specs=[row, pl.BlockSpec((1, D), lambda i: (0, 0))], out_specs=row,
        out_shape=jax.ShapeDtypeStruct((s, D), BF16),
        compiler_params=_cp(1))(x, g1)


def _inproj_fwd(h1, w_in):
    s = h1.shape[0]
    tm = min(TM, s)

    def body(h_ref, w_ref, u_ref, q_ref, k_ref, v_ref):
        proj = _dot_nt(h_ref[...], w_ref[...])
        u_ref[...] = proj[:, :512]
        q_ref[...] = proj[:, 512:1024].astype(BF16)
        k_ref[...] = proj[:, 1024:1152].astype(BF16)
        v_ref[...] = proj[:, 1152:1280].astype(BF16)

    row = lambda w: pl.BlockSpec((tm, w), lambda i: (i, 0))
    return pl.pallas_call(
        body, name="inproj_fwd", grid=(s // tm,),
        in_specs=[row(D), pl.BlockSpec((IN_W, D), lambda i: (0, 0))],
        out_specs=[row(512), row(512), row(128), row(128)],
        out_shape=[jax.ShapeDtypeStruct((s, 512), F32), jax.ShapeDtypeStruct((s, 512), BF16),
                   jax.ShapeDtypeStruct((s, 128), BF16), jax.ShapeDtypeStruct((s, 128), BF16)],
        compiler_params=_cp(1))(h1, w_in)


def _window_sums(ext, w, lag_sign, tm, terms):
    rows = lax.broadcasted_iota(jnp.int32, (HALO, 2 * HALO), 0)
    cols = lax.broadcasted_iota(jnp.int32, (HALO, 2 * HALO), 1)
    d = rows + HALO - cols if lag_sign > 0 else cols - rows
    band = jnp.where((d >= 0) & (d < w), 1.0, 0.0).astype(BF16)
    return jnp.concatenate([_band_dot(band, ext[r:r + 2 * HALO], terms) for r in range(0, tm, HALO)], axis=0)


def _pooled(ext, cur, t0, tm):
    t = t0 + lax.broadcasted_iota(jnp.int32, (tm, GROUP), 0)
    out = []
    for g, w in enumerate(WINDOWS):
        sl = slice(g * GROUP, (g + 1) * GROUP)
        cnt = jnp.minimum(t + 1, w).astype(F32)
        out.append(_window_sums(ext[:, sl], w, 1, tm, 2) / cnt - cur[:, sl])
    return out


def _pool_fwd(u, w_pool, pool_scale):
    s = u.shape[0]
    tm = min(TM_POOL, s)
    hb = tm // HALO

    def body(uc_ref, uh_ref, wp_ref, sc_ref, o_ref, pooled_ref):
        i = pl.program_id(0)
        cur = uc_ref[...]
        halo = jnp.where(i > 0, uh_ref[...], 0.0)
        ext = jnp.concatenate([halo, cur], axis=0)
        pooled = _pooled(ext, cur, i * tm, tm)
        for g in range(4):
            sl = slice(g * GROUP, (g + 1) * GROUP)
            pb = pooled[g].astype(BF16)
            pooled_ref[:, sl] = pb
            o_ref[:, sl] = (_dot(pb, wp_ref[g]) * sc_ref[:, sl]).astype(BF16)

    row = pl.BlockSpec((tm, 512), lambda i: (i, 0))
    return pl.pallas_call(
        body, name="pool_fwd", grid=(s // tm,),
        in_specs=[row, pl.BlockSpec((HALO, 512), lambda i: (jnp.maximum(i * hb - 1, 0), 0)),
                  pl.BlockSpec((4, GROUP, GROUP), lambda i: (0, 0, 0)),
                  pl.BlockSpec((1, 512), lambda i: (0, 0))],
        out_specs=[row, row],
        out_shape=[jax.ShapeDtypeStruct((s, 512), BF16)] * 2,
        compiler_params=_cp(1))(u, u, w_pool, pool_scale)


def _bias_table(bucket, rel_bias):
    def body(b_ref, rb_ref, o_ref):
        bucket_v = b_ref[...]
        key = lax.broadcasted_iota(jnp.int32, (2 * BLK, BLK), 0)
        dist = lax.broadcasted_iota(jnp.int32, (2 * BLK, BLK), 1) + BLK - key
        inwin = (dist >= 0) & (dist < BLK)
        for h in range(NQ):
            acc = jnp.zeros((2 * BLK, BLK), F32)
            for b in range(NBUCKET):
                acc = jnp.where(bucket_v == b, rb_ref[b, h], acc)
            rest = jnp.where(inwin, acc, NEG)
            o_ref[1, h] = rest
            o_ref[0, h] = jnp.where(key >= BLK, rest, NEG)

    return pl.pallas_call(
        body, name="bias_table",
        in_specs=[pl.BlockSpec(memory_space=pltpu.VMEM), pl.BlockSpec(memory_space=pltpu.SMEM)],
        out_specs=pl.BlockSpec(memory_space=pltpu.VMEM),
        out_shape=jax.ShapeDtypeStruct((2, NQ, 2 * BLK, BLK), F32),
        compiler_params=_cp())(bucket, rel_bias)


HD = 64


def _kv_band(ref, n, transposed):
    pstart = pl.multiple_of(jnp.maximum(n - 1, 0) * BLK, BLK)
    cstart = pl.multiple_of(n * BLK, BLK)
    lo = lax.broadcasted_iota(jnp.int32, (2 * BLK, 128), 1) < HD
    band = jnp.concatenate([ref[pl.ds(pstart, BLK), :], ref[pl.ds(cstart, BLK), :]], axis=0).astype(F32)
    rot = pltpu.roll(band, HD, axis=1)
    halves = ((jnp.where(lo, band, 0.0), jnp.where(lo, 0.0, rot)), (jnp.where(lo, rot, 0.0), jnp.where(lo, 0.0, band)))
    if transposed:
        out = [jnp.concatenate([a.T, b.T], axis=1).astype(BF16) for a, b in halves]
    else:
        out = [jnp.concatenate([a, b], axis=0).astype(BF16) for a, b in halves]
    return out, (lo, pstart, cstart)


def _pair_probs(qt, kk, bias, sk_ref, p):
    sink = jnp.concatenate([jnp.full((1, 1, BLK), sk_ref[0, 2 * p + e], F32) for e in range(2)], axis=0)
    s = _dot_nt(kk, qt).reshape(2, 2 * BLK, BLK) + bias
    m = jnp.maximum(jnp.max(s, axis=1, keepdims=True), sink)
    pr = jnp.exp(s - m)
    es = jnp.exp(sink - m)
    inv = 1.0 / (jnp.sum(pr, axis=1, keepdims=True) + es)
    return pr * inv, es * inv


def _attn_fwd(q, k, v, bias, sinks):
    s = q.shape[0]
    nblk = s // BLK

    def body(q_ref, k_ref, v_ref, b_ref, sk_ref, o_ref, prob_ref, ps_ref):
        n = pl.program_id(0)
        kk, _ = _kv_band(k_ref, n, False)
        vt, _ = _kv_band(v_ref, n, True)
        bidx = jnp.minimum(n, 1)
        for p in range(4):
            h = p // 2
            qt = (q_ref[:, p * 128:(p + 1) * 128].astype(F32) * SCALE).astype(BF16)
            prob, ps = _pair_probs(qt, kk[h], b_ref[bidx, pl.ds(2 * p, 2)], sk_ref, p)
            pb = prob.astype(BF16)
            prob_ref[pl.ds(2 * p, 2)] = pb
            ps_ref[pl.ds(2 * p, 2), :] = ps.reshape(2, BLK)
            acc_t = _dot(vt[h], pb.reshape(4 * BLK, BLK))
            o_ref[:, p * 128:(p + 1) * 128] = acc_t.T.astype(BF16)

    return pl.pallas_call(
        body, name="attn_fwd", grid=(nblk,),
        in_specs=[pl.BlockSpec((BLK, 512), lambda i: (i, 0)),
                  pl.BlockSpec((s, 128), lambda i: (0, 0)),
                  pl.BlockSpec((s, 128), lambda i: (0, 0)),
                  pl.BlockSpec((2, NQ, 2 * BLK, BLK), lambda i: (0, 0, 0, 0)),
                  pl.BlockSpec(memory_space=pltpu.SMEM)],
        out_specs=[pl.BlockSpec((BLK, 512), lambda i: (i, 0)),
                   pl.BlockSpec((None, NQ, 2 * BLK, BLK), lambda i: (i, 0, 0, 0)),
                   pl.BlockSpec((None, NQ, BLK), lambda i: (i, 0, 0))],
        out_shape=[jax.ShapeDtypeStruct((s, 512), BF16), jax.ShapeDtypeStruct((nblk, NQ, 2 * BLK, BLK), BF16),
                   jax.ShapeDtypeStruct((nblk, NQ, BLK), F32)],
        compiler_params=_cp(1))(q, k, v, bias, sinks)


def _outproj_fwd(pool_o, attn_o, w_out, x, g2, g3):
    s = x.shape[0]
    tm = min(TM, s)

    def body(p_ref, a_ref, w_ref, x_ref, g2_ref, g3_ref, mix_ref, x1_ref, h2_ref):
        mix = _dot(p_ref[...], w_ref[0:512, :]) + _dot(a_ref[...], w_ref[512:1024, :])
        mix_ref[...] = mix
        _, n2 = _rms(mix)
        x1 = x_ref[...] + n2 * g2_ref[...]
        x1_ref[...] = x1
        _, n3 = _rms(x1)
        h2_ref[...] = (n3 * g3_ref[...]).astype(BF16)

    row = lambda w: pl.BlockSpec((tm, w), lambda i: (i, 0))
    vec = pl.BlockSpec((1, D), lambda i: (0, 0))
    return pl.pallas_call(
        body, name="outproj_fwd", grid=(s // tm,),
        in_specs=[row(512), row(512), pl.BlockSpec((D, D), lambda i: (0, 0)), row(D), vec, vec],
        out_specs=[row(D), row(D), row(D)],
        out_shape=[jax.ShapeDtypeStruct((s, D), F32), jax.ShapeDtypeStruct((s, D), F32),
                   jax.ShapeDtypeStruct((s, D), BF16)],
        compiler_params=_cp(1))(pool_o, attn_o, w_out, x, g2, g3)


def _silu_parts(g):
    sg = jax.nn.sigmoid(g)
    return sg, g * sg


def _ffn_fwd(h2, wg, wu, wd, x1, target, g4):
    s = h2.shape[0]
    tm = min(TM_FFN_FWD, s)

    def body(h_ref, wg_ref, wu_ref, wd_ref, x1_ref, t_ref, g4_ref,
             gate_ref, up_ref, df_ref, dy_ref, loss_ref, gg4_ref, f_acc):
        i = pl.program_id(0)
        j = pl.program_id(1)
        first = j == 0
        for r in range(0, tm, FFN_ROWS):
            rows = pl.ds(r, min(FFN_ROWS, tm))
            h = h_ref[rows, :]
            gate = _dot_nt(h, wg_ref[...])
            up = _dot_nt(h, wu_ref[...])
            gate_ref[rows, :] = gate.astype(BF16)
            up_ref[rows, :] = up.astype(BF16)
            _, sl = _silu_parts(gate)
            contrib = _dot((sl * up).astype(BF16), wd_ref[...])
            f_acc[rows, :] = jnp.where(first, contrib, f_acc[rows, :] + contrib)

        @pl.when((i == 0) & (j == 0))
        def _():
            loss_ref[...] = jnp.zeros_like(loss_ref)
            gg4_ref[...] = jnp.zeros_like(gg4_ref)

        @pl.when(j == NSH - 1)
        def _():
            r4, n4 = _rms(f_acc[...])
            g4v = g4_ref[...]
            err = x1_ref[...] + n4 * g4v - t_ref[...]
            loss_ref[...] += (0.5 / D) * jnp.sum(err * err).reshape(1, 1)
            dy = err * (1.0 / D)
            dy_ref[...] = dy
            df, dg = _rms_bwd(r4, n4, g4v, dy)
            gg4_ref[...] += dg
            df_ref[...] = df.astype(BF16)

    row = lambda w: pl.BlockSpec((tm, w), lambda i, j: (i, 0))
    sh = lambda r, c: pl.BlockSpec((None, r, c), lambda i, j: (j, 0, 0))
    act = pl.BlockSpec((None, tm, FS), lambda i, j: (j, i, 0))
    vec = pl.BlockSpec((1, D), lambda i, j: (0, 0))
    return pl.pallas_call(
        body, name="ffn_fwd", grid=(s // tm, NSH),
        in_specs=[row(D), sh(FS, D), sh(FS, D), sh(FS, D), row(D), row(D), vec],
        out_specs=[act, act, row(D), row(D), pl.BlockSpec((1, 1), lambda i, j: (0, 0)), vec],
        out_shape=[jax.ShapeDtypeStruct((NSH, s, FS), BF16), jax.ShapeDtypeStruct((NSH, s, FS), BF16),
                   jax.ShapeDtypeStruct((s, D), BF16), jax.ShapeDtypeStruct((s, D), F32),
                   jax.ShapeDtypeStruct((1, 1), F32), jax.ShapeDtypeStruct((1, D), F32)],
        scratch_shapes=[pltpu.VMEM((tm, D), F32)],
        compiler_params=_cp(2))(h2, wg, wu, wd, x1, target, g4)


def _ffn_bwd_act(df, gate, up, wg, wu, wd, dy, x1, mix, g3, g2):
    s = df.shape[0]
    tm = min(TM_FFN, s)

    def body(df_ref, gate_ref, up_ref, wg_ref, wu_ref, wd_ref, dy_ref, x1_ref, mix_ref, g3_ref, g2_ref,
             dgate_ref, dup_ref, dx1_ref, dmix_ref, gg3_ref, gg2_ref, acc):
        i = pl.program_id(0)
        j = pl.program_id(1)
        first = j == 0
        for r in range(0, tm, FFN_ROWS):
            rows = pl.ds(r, min(FFN_ROWS, tm))
            da = _dot_nt(df_ref[rows, :], wd_ref[...])
            g = gate_ref[rows, :].astype(F32)
            u = up_ref[rows, :].astype(F32)
            sg, sl = _silu_parts(g)
            dgate = (da * u * (sg * (1.0 + g * (1.0 - sg)))).astype(BF16)
            dup = (da * sl).astype(BF16)
            dgate_ref[rows, :] = dgate
            dup_ref[rows, :] = dup
            contrib = _dot(dgate, wg_ref[...]) + _dot(dup, wu_ref[...])
            acc[rows, :] = jnp.where(first, contrib, acc[rows, :] + contrib)

        @pl.when((i == 0) & (j == 0))
        def _():
            gg3_ref[...] = jnp.zeros_like(gg3_ref)
            gg2_ref[...] = jnp.zeros_like(gg2_ref)

        @pl.when(j == NSH - 1)
        def _():
            r3, n3 = _rms(x1_ref[...])
            dx1n, dg3 = _rms_bwd(r3, n3, g3_ref[...], acc[...])
            dx1 = dy_ref[...] + dx1n
            dx1_ref[...] = dx1
            gg3_ref[...] += dg3
            r2, n2 = _rms(mix_ref[...])
            dmix, dg2 = _rms_bwd(r2, n2, g2_ref[...], dx1)
            gg2_ref[...] += dg2
            dmix_ref[...] = dmix.astype(BF16)

    row = lambda w: pl.BlockSpec((tm, w), lambda i, j: (i, 0))
    sh = lambda r, c: pl.BlockSpec((None, r, c), lambda i, j: (j, 0, 0))
    act = pl.BlockSpec((None, tm, FS), lambda i, j: (j, i, 0))
    vec = pl.BlockSpec((1, D), lambda i, j: (0, 0))
    return pl.pallas_call(
        body, name="ffn_bwd_act", grid=(s // tm, NSH),
        in_specs=[row(D), act, act, sh(FS, D), sh(FS, D), sh(FS, D), row(D), row(D), row(D), vec, vec],
        out_specs=[act, act, row(D), row(D), vec, vec],
        out_shape=[jax.ShapeDtypeStruct((NSH, s, FS), BF16), jax.ShapeDtypeStruct((NSH, s, FS), BF16),
                   jax.ShapeDtypeStruct((s, D), F32), jax.ShapeDtypeStruct((s, D), BF16),
                   jax.ShapeDtypeStruct((1, D), F32), jax.ShapeDtypeStruct((1, D), F32)],
        scratch_shapes=[pltpu.VMEM((tm, D), F32)],
        compiler_params=_cp(2))(df, gate, up, wg, wu, wd, dy, x1, mix, g3, g2)


SMEM_SPEC = pl.BlockSpec(memory_space=pltpu.SMEM)


def _emit_halves(acc_ref, row0, rows, c, own_ref, oth_ref):
    half = rows // 2
    own_ref[...] = acc_ref[pl.ds(row0 + pl.multiple_of(c * half, 8), half), :]
    oth_ref[...] = acc_ref[pl.ds(row0 + pl.multiple_of((1 - c) * half, 8), half), :].astype(BF16)


def _half_shapes(rows):
    return [jax.ShapeDtypeStruct((NSH, rows // 2, D), F32), jax.ShapeDtypeStruct((NSH, rows // 2, D), BF16)]


def _ffn_bwd_w(h2, gate, up, dgate, dup, df, c_arr):
    s = h2.shape[0]
    tm = min(TM_FFN_FWD, s)
    nt = s // tm

    def body(c_ref, h_ref, gate_ref, up_ref, dgate_ref, dup_ref, df_ref, *rest):
        outs, accs = rest[:6], rest[6:]
        i = pl.program_id(1)
        @pl.when(i == 0)
        def _():
            for acc in accs:
                acc[...] = jnp.zeros_like(acc)

        h = h_ref[...]
        accs[0][...] += _dot_tn(dgate_ref[...], h)
        accs[1][...] += _dot_tn(dup_ref[...], h)
        _, sl = _silu_parts(gate_ref[...].astype(F32))
        a = (sl * up_ref[...].astype(F32)).astype(BF16)
        accs[2][...] += _dot_tn(a, df_ref[...])

        @pl.when(i == nt - 1)
        def _():
            for t, acc in enumerate(accs):
                _emit_halves(acc, 0, FS, c_ref[0], outs[2 * t], outs[2 * t + 1])

    row = lambda w: pl.BlockSpec((tm, w), lambda j, i: (i, 0))
    act = pl.BlockSpec((None, tm, FS), lambda j, i: (j, i, 0))
    half = pl.BlockSpec((None, FS // 2, D), lambda j, i: (j, 0, 0))
    return pl.pallas_call(
        body, name="ffn_bwd_w", grid=(NSH, nt),
        in_specs=[SMEM_SPEC, row(D), act, act, act, act, row(D)],
        out_specs=[half] * 6,
        out_shape=_half_shapes(FS) * 3,
        scratch_shapes=[pltpu.VMEM((FS, D), F32)] * 3,
        compiler_params=_cp(2))(c_arr, h2, gate, up, dgate, dup, df)


def _outproj_bwd(dmix, w_out, pool_o, attn_o, c_arr, dep=None):
    s = dmix.shape[0]
    tm = min(TM, s)
    nt = s // tm

    def body(c_ref, dm_ref, w_ref, p_ref, a_ref, *rest):
        dpool_ref, dattn_ref, own_ref, oth_ref, gw_ref = rest[-5:]
        i = pl.program_id(0)

        @pl.when(i == 0)
        def _():
            gw_ref[...] = jnp.zeros_like(gw_ref)

        dm = dm_ref[...]
        dpool_ref[...] = _dot_nt(dm, w_ref[0:512, :])
        dattn_ref[...] = _dot_nt(dm, w_ref[512:1024, :]).astype(BF16)
        gw_ref[0:512, :] += _dot_tn(p_ref[...], dm)
        gw_ref[512:1024, :] += _dot_tn(a_ref[...], dm)

        @pl.when(i == nt - 1)
        def _():
            for k in range(NSH):
                _emit_halves(gw_ref, k * WOUT_S, WOUT_S, c_ref[0], own_ref.at[k], oth_ref.at[k])

    row = lambda w: pl.BlockSpec((tm, w), lambda i: (i, 0))
    full = pl.BlockSpec((D, D), lambda i: (0, 0))
    half = pl.BlockSpec((NSH, WOUT_S // 2, D), lambda i: (0, 0, 0))
    return pl.pallas_call(
        body, name="outproj_bwd", grid=(nt,),
        in_specs=[SMEM_SPEC, row(D), full, row(512), row(512)] + ([] if dep is None else [ANY]),
        out_specs=[row(512), row(512), half, half],
        out_shape=[jax.ShapeDtypeStruct((s, 512), F32), jax.ShapeDtypeStruct((s, 512), BF16)] + _half_shapes(WOUT_S),
        scratch_shapes=[pltpu.VMEM((D, D), F32)],
        compiler_params=_cp(1))(c_arr, dmix, w_out, pool_o, attn_o, *([] if dep is None else [dep]))


def _pool_bwd(pooled, dpool, w_pool, pool_scale, dep=None):
    s = pooled.shape[0]
    tm = min(TM_POOL, s)
    hb = tm // HALO
    nt = s // tm
    last_halo = s // HALO - 1

    def body(p_ref, dc_ref, dn_ref, wp_ref, sc_ref, *rest):
        du_ref, gwp_ref, gsc_ref = rest[-3:]
        i = pl.program_id(0)

        @pl.when(i == 0)
        def _():
            gwp_ref[...] = jnp.zeros_like(gwp_ref)
            gsc_ref[...] = jnp.zeros_like(gsc_ref)

        dcur = dc_ref[...]
        dnext = jnp.where(i < nt - 1, dn_ref[...], 0.0)
        dext = jnp.concatenate([dcur, dnext], axis=0)
        t_ext = i * tm + lax.broadcasted_iota(jnp.int32, (tm + HALO, GROUP), 0)
        for g, w in enumerate(WINDOWS):
            sl = slice(g * GROUP, (g + 1) * GROUP)
            pb = p_ref[:, sl]
            wp = wp_ref[g]
            mixed = _dot(pb, wp)
            gsc_ref[:, sl] += jnp.sum(dcur[:, sl] * mixed, axis=0, keepdims=True)
            dmixed = (dext[:, sl] * sc_ref[:, sl]).astype(BF16)
            gwp_ref[g] += _dot_tn(pb, dmixed[0:tm])
            dpooled = _dot_nt(dmixed, wp)
            z = dpooled / jnp.minimum(t_ext + 1, w).astype(F32)
            du_ref[:, sl] = (_window_sums(z, w, -1, tm, 2) - dpooled[0:tm]).astype(BF16)

    return pl.pallas_call(
        body, name="pool_bwd", grid=(nt,),
        in_specs=[pl.BlockSpec((tm, 512), lambda i: (i, 0)),
                  pl.BlockSpec((tm, 512), lambda i: (i, 0)),
                  pl.BlockSpec((HALO, 512), lambda i: (jnp.minimum((i + 1) * hb, last_halo), 0)),
                  pl.BlockSpec((4, GROUP, GROUP), lambda i: (0, 0, 0)),
                  pl.BlockSpec((1, 512), lambda i: (0, 0))] + ([] if dep is None else [ANY]),
        out_specs=[pl.BlockSpec((tm, 512), lambda i: (i, 0)),
                   pl.BlockSpec((4, GROUP, GROUP), lambda i: (0, 0, 0)),
                   pl.BlockSpec((1, 512), lambda i: (0, 0))],
        out_shape=[jax.ShapeDtypeStruct((s, 512), BF16), jax.ShapeDtypeStruct((4, GROUP, GROUP), F32),
                   jax.ShapeDtypeStruct((1, 512), F32)],
        compiler_params=_cp(1))(pooled, dpool, dpool, w_pool, pool_scale, *([] if dep is None else [dep]))


def _attn_bwd(q, k, v, do, probs, sink_share, bucket, dep=None):
    s = q.shape[0]
    nblk = s // BLK

    def body(q_ref, do_ref, k_ref, v_ref, prob_ref, ps_ref, bk_ref, *rest):
        dq_ref, dk_ref, dv_ref, grb_ref, gsk_ref, dss_ref = rest[-6:]
        n = pl.program_id(0)

        @pl.when(n == 0)
        def _():
            dk_ref[...] = jnp.zeros_like(dk_ref)
            dv_ref[...] = jnp.zeros_like(dv_ref)
            dss_ref[...] = jnp.zeros_like(dss_ref)
            gsk_ref[...] = jnp.zeros_like(gsk_ref)

        kt, (lo, pstart, cstart) = _kv_band(k_ref, n, True)
        vv, _ = _kv_band(v_ref, n, False)
        lo_q = lax.broadcasted_iota(jnp.int32, (BLK, 128), 1) < HD
        dkk = [jnp.zeros((2 * BLK, 128), F32), jnp.zeros((2 * BLK, 128), F32)]
        dvv = [jnp.zeros((2 * BLK, 128), F32), jnp.zeros((2 * BLK, 128), F32)]

        def by_head(t):
            return jnp.concatenate([jnp.where(lo_q, t, 0.0), jnp.where(lo_q, 0.0, t)], axis=0).astype(BF16)

        for p in range(4):
            h = p // 2
            qt = q_ref[:, p * 128:(p + 1) * 128].astype(F32) * SCALE
            dot = do_ref[:, p * 128:(p + 1) * 128]
            pb = prob_ref[pl.ds(2 * p, 2)]
            prob = pb.astype(F32)
            ps = ps_ref[pl.ds(2 * p, 2), :].reshape(2, 1, BLK)
            dp = _dot_nt(vv[h], dot).reshape(2, 2 * BLK, BLK)
            delta = jnp.sum(prob * dp, axis=1, keepdims=True)
            ds = prob * (dp - delta)
            sink_part = ps * delta
            for e in range(2):
                gs = -jnp.sum(sink_part[e]).reshape(1, 1)
                gsk_ref[pl.ds(2 * p + e, 1), :] += jnp.broadcast_to(gs, (1, 128))
            dss_ref[pl.ds(2 * p, 2)] += ds
            dsb = ds.astype(BF16)
            dq_t = _dot(kt[h], dsb.reshape(4 * BLK, BLK))
            dq_ref[:, p * 128:(p + 1) * 128] = (dq_t.T * SCALE).astype(BF16)
            dkk[h] = dkk[h] + _dot(jnp.concatenate([dsb[0], dsb[1]], axis=1), by_head(qt))
            dvv[h] = dvv[h] + _dot(jnp.concatenate([pb[0], pb[1]], axis=1), by_head(dot.astype(F32)))
        for acc, ref in ((dkk, dk_ref), (dvv, dv_ref)):
            f0 = acc[0] + pltpu.roll(acc[0], HD, axis=1)
            f1 = acc[1] + pltpu.roll(acc[1], HD, axis=1)
            band = jnp.where(lo, f0, f1)
            ref[pl.ds(pstart, BLK), :] += band[0:BLK]
            ref[pl.ds(cstart, BLK), :] += band[BLK:2 * BLK]

        @pl.when(n == nblk - 1)
        def _():
            _relbias_grad(dss_ref, bk_ref[...], grb_ref)

    blk = pl.BlockSpec((BLK, 512), lambda i: (i, 0))
    kv = pl.BlockSpec((s, 128), lambda i: (0, 0))
    row8 = pl.BlockSpec((NQ, 128), lambda i: (0, 0))
    return pl.pallas_call(
        body, name="attn_bwd", grid=(nblk,),
        in_specs=[blk, blk, kv, kv, pl.BlockSpec((None, NQ, 2 * BLK, BLK), lambda i: (i, 0, 0, 0)),
                  pl.BlockSpec((None, NQ, BLK), lambda i: (i, 0, 0)), pl.BlockSpec((2 * BLK, BLK), lambda i: (0, 0))]
        + ([] if dep is None else [ANY]),
        out_specs=[blk, kv, kv, row8, row8],
        out_shape=[jax.ShapeDtypeStruct((s, 512), BF16), jax.ShapeDtypeStruct((s, 128), F32),
                   jax.ShapeDtypeStruct((s, 128), F32), jax.ShapeDtypeStruct((NQ, 128), F32),
                   jax.ShapeDtypeStruct((NQ, 128), F32)],
        scratch_shapes=[pltpu.VMEM((NQ, 2 * BLK, BLK), F32)],
        compiler_params=_cp(1))(q, do, k, v, probs, sink_share, bucket, *([] if dep is None else [dep]))


def _relbias_grad(ds_ref, bucket_v, o_ref):
    lane = lax.broadcasted_iota(jnp.int32, (NQ, 128), 1)
    head_row = lax.broadcasted_iota(jnp.int32, (NQ, BLK), 0)
    acc = jnp.zeros((NQ, 128), F32)
    for b in range(NBUCKET):
        sel = bucket_v == b
        stack = jnp.zeros((NQ, BLK), F32)
        for h in range(NQ):
            col_sums = jnp.sum(jnp.where(sel, ds_ref[h], 0.0), axis=0, keepdims=True)
            stack = jnp.where(head_row == h, col_sums, stack)
        acc = acc + jnp.where(lane == b, jnp.sum(stack, axis=1, keepdims=True), 0.0)
    o_ref[...] = acc


def _inproj_bwd(x, g1, du, dq, dk, dv, w_in, dx1, c_arr, dep=None):
    s = x.shape[0]
    tm = min(TM, s)
    nt = s // tm
    cols = ((0, 512), (512, 1024), (1024, 1152), (1152, 1280))

    def body(c_ref, x_ref, g_ref, du_ref, dq_ref, dk_ref, dv_ref, w_ref, dx1_ref, *rest):
        gx_ref, own_ref, oth_ref, gg_ref, gw_ref = rest[-5:]
        i = pl.program_id(0)

        @pl.when(i == 0)
        def _():
            gw_ref[...] = jnp.zeros_like(gw_ref)
            gg_ref[...] = jnp.zeros_like(gg_ref)

        gv = g_ref[...]
        r1, n1 = _rms(x_ref[...])
        h = (n1 * gv).astype(BF16)
        parts = (du_ref[...], dq_ref[...], dk_ref[...].astype(BF16), dv_ref[...].astype(BF16))
        dh = None
        for (a, b), dpart in zip(cols, parts):
            t = _dot(dpart, w_ref[a:b, :])
            dh = t if dh is None else dh + t
            gw_ref[a:b, :] += _dot_tn(dpart, h)
        dx, dg = _rms_bwd(r1, n1, gv, dh)
        gg_ref[...] += dg
        gx_ref[...] = dx1_ref[...] + dx

        @pl.when(i == nt - 1)
        def _():
            for k in range(NSH):
                _emit_halves(gw_ref, k * WIN_S, WIN_S, c_ref[0], own_ref.at[k], oth_ref.at[k])

    row = lambda w: pl.BlockSpec((tm, w), lambda i: (i, 0))
    vec = pl.BlockSpec((1, D), lambda i: (0, 0))
    full = pl.BlockSpec((IN_W, D), lambda i: (0, 0))
    half = pl.BlockSpec((NSH, WIN_S // 2, D), lambda i: (0, 0, 0))
    return pl.pallas_call(
        body, name="inproj_bwd", grid=(nt,),
        in_specs=[SMEM_SPEC, row(D), vec, row(512), row(512), row(128), row(128), full, row(D)]
        + ([] if dep is None else [ANY]),
        out_specs=[row(D), half, half, vec],
        out_shape=[jax.ShapeDtypeStruct((s, D), F32)] + _half_shapes(WIN_S) + [jax.ShapeDtypeStruct((1, D), F32)],
        scratch_shapes=[pltpu.VMEM((IN_W, D), F32)],
        compiler_params=_cp(1))(c_arr, x, g1, du, dq, dk, dv, w_in, dx1, *([] if dep is None else [dep]))


def _local_step(x, target, small, w_in, w_out, ffn_weights, c_arr, on_ffn_grads=None, on_wout_grads=None,
                on_attn_grads=None):
    g1, g2, g3, g4, w_pool, pool_scale, rel_bias, sinks = small
    bucket = jnp.asarray(_bucket_table())
    wp_bf = w_pool.astype(BF16)
    bias = _bias_table(bucket, rel_bias)
    h1 = _prenorm(x, g1)
    if callable(w_in):
        w_in = w_in(bias, h1)
    u, q, k, v = _inproj_fwd(h1, w_in)
    pool_o, pooled = _pool_fwd(u, wp_bf, pool_scale)
    attn_o, probs, sink_share = _attn_fwd(q, k, v, bias, sinks)
    g2_fwd = g2
    if callable(w_out):
        w_out, after = w_out(pool_o, attn_o)
        if after is not None:
            g2_fwd = g2 + after[:1, :1]
    mix, x1, h2 = _outproj_fwd(pool_o, attn_o, w_out, x, g2_fwd, g3)
    wg, wu, wd = ffn_weights(h2) if callable(ffn_weights) else ffn_weights
    gate, up, df, dy, loss, gg4 = _ffn_fwd(h2, wg, wu, wd, x1, target, g4)
    dgate, dup, dx1, dmix, gg3, gg2 = _ffn_bwd_act(df, gate, up, wg, wu, wd, dy, x1, mix, g3, g2)
    ffn_g = _ffn_bwd_w(h2, gate, up, dgate, dup, df, c_arr)
    dep = None if on_ffn_grads is None else on_ffn_grads(ffn_g)
    dpool, dattn, wout_own, wout_oth = _outproj_bwd(dmix, w_out, pool_o, attn_o, c_arr, dep)
    dep = None if on_wout_grads is None else on_wout_grads(wout_own, wout_oth, dpool)
    du, gwp, gsc = _pool_bwd(pooled, dpool, wp_bf, pool_scale, dep)
    dq, dk, dv, grb, gsk = _attn_bwd(q, k, v, dattn, probs, sink_share, bucket, dep)
    dep = None if on_attn_grads is None else on_attn_grads([du, dq])
    gx, win_own, win_oth, gg1 = _inproj_bwd(x, g1, du, dq, dk, dv, w_in, dx1, c_arr, dep)
    small_grads = (gg1, gg2, gg3, gg4, gwp, gsc, grb[:, :NBUCKET].T, gsk[:, 0].reshape(1, NQ))
    return loss, gx, small_grads, ((win_own, win_oth), (wout_own, wout_oth), ffn_g)


def _place():
    x, y, c = lax.axis_index("x"), lax.axis_index("y"), lax.axis_index("c")
    chips = ((1 - x, y), (x, 1 - y), (1 - x, 1 - y))
    return x, y, c, chips


def _remote(src, dst, ssem, rsem, dev):
    return pltpu.make_async_remote_copy(src_ref=src, dst_ref=dst, send_sem=ssem, recv_sem=rsem,
                                        device_id=dev, device_id_type=MESH_T)


def _to_sibling(arrs, name, after=()):
    nt, na = len(arrs), len(after)

    def body(*refs):
        srcs, dsts = refs[:nt], refs[nt + na:2 * nt + na]
        ssem, rsem = refs[2 * nt + na:]
        x, y, c, _ = _place()
        cps = [_remote(srcs[t], dsts[t], ssem.at[t], rsem.at[t], (x, y, 1 - c)) for t in range(nt)]
        for cp in cps:
            cp.start()
        for cp in cps:
            cp.wait()

    return pl.pallas_call(
        body, name=name, in_specs=[ANY] * (nt + na), out_specs=[ANY] * nt,
        out_shape=[jax.ShapeDtypeStruct(a.shape, a.dtype) for a in arrs],
        scratch_shapes=[pltpu.SemaphoreType.DMA((nt,)), pltpu.SemaphoreType.DMA((nt,))],
        compiler_params=_cp())(*arrs, *after)


HBM_SPEC = pl.BlockSpec(memory_space=pltpu.HBM)
SEM_SPEC = pl.BlockSpec(memory_space=pltpu.SEMAPHORE)
DATAFLOW = pltpu.SideEffectType.DATAFLOW_SIDE_EFFECTING


def _gather_copies(srcs, lands, ssem, rsem):
    x, y, c, chips = _place()
    me = 2 * x + y
    out = []
    for t in range(len(srcs)):
        half = srcs[t].shape[0] // 2
        mine = pl.ds(pl.multiple_of(c * half, 16), half)
        for j, (px, py) in enumerate(chips):
            k = 3 * t + j
            send = _remote(srcs[t].at[mine], lands[t].at[me, mine], ssem.at[k], rsem.at[k], (px, py, c))
            blk = lands[t].at[2 * px + py, mine]
            out.append((send, _remote(blk, blk, ssem.at[k], rsem.at[k], (px, py, c))))
    return out


def _scatter_copies(srcs, lands, ssem, rsem):
    x, y, c, chips = _place()
    out = []
    for t in range(len(srcs)):
        for j, (px, py) in enumerate(chips):
            k = 3 * t + j
            send = _remote(srcs[t].at[2 * px + py], lands[t].at[j], ssem.at[k], rsem.at[k], (px, py, c))
            out.append((send, _remote(lands[t].at[j], lands[t].at[j], ssem.at[k], rsem.at[k], (px, py, c))))
    return out


def _sibling_copies(srcs, lands, ssem, rsem):
    x, y, c, _ = _place()
    sib = (x, y, 1 - c)
    return [(_remote(srcs[t], lands[t], ssem.at[t], rsem.at[t], sib),
             _remote(lands[t], lands[t], ssem.at[t], rsem.at[t], sib)) for t in range(len(srcs))]


def _peer_copies(srcs, lands, ssem, rsem):
    x, y, c, _ = _place()
    me = 4 * x + 2 * y + c
    out = []
    for kk in range(1, 8):
        px, py, pc = x ^ (kk >> 2), y ^ ((kk >> 1) & 1), c ^ (kk & 1)
        send = _remote(srcs[0], lands[0].at[me], ssem.at[kk - 1], rsem.at[kk - 1], (px, py, pc))
        slot = lands[0].at[4 * px + 2 * py + pc]
        out.append((send, _remote(slot, slot, ssem.at[kk - 1], rsem.at[kk - 1], (px, py, pc))))
    return out


def _split_start(plan, ncopy, srcs, lands, after, name):
    ns, nbuf = len(srcs), len(srcs) + len(lands)
    extra = [] if after is None else [after]
    n_in = nbuf + len(extra)

    def body(*refs):
        ssem, rsem, token = refs[n_in], refs[n_in + 1], refs[-1]
        for send, _ in plan(refs[:ns], refs[ns:nbuf], ssem, rsem):
            send.start()
        token[...] = jnp.zeros_like(token)

    bufs = [pltpu.with_memory_space_constraint(a, pltpu.HBM) for a in list(srcs) + list(lands)]
    outs = pl.pallas_call(
        body, name=name, in_specs=[HBM_SPEC] * nbuf + [ANY] * len(extra),
        out_specs=[SEM_SPEC, SEM_SPEC] + [HBM_SPEC] * nbuf + [pl.BlockSpec(memory_space=pltpu.VMEM)],
        out_shape=[pltpu.SemaphoreType.DMA((ncopy,)), pltpu.SemaphoreType.DMA((ncopy,))]
        + [pltpu.HBM(a.shape, a.dtype) for a in bufs] + [jax.ShapeDtypeStruct((8, 128), F32)],
        input_output_aliases={i: 2 + i for i in range(nbuf)},
        compiler_params=pltpu.CompilerParams(has_side_effects=DATAFLOW))(*bufs, *extra)
    return outs[0], outs[1], outs[2:2 + ns], outs[2 + ns:2 + nbuf], outs[-1]


def _split_wait(plan, ssem, rsem, srcs, lands, after, name, only=None):
    ns, nbuf = len(srcs), len(srcs) + len(lands)

    def body(*refs):
        s_ref, r_ref = refs[nbuf], refs[nbuf + 1]
        for k, (send, recv) in enumerate(plan(refs[:ns], refs[ns:nbuf], s_ref, r_ref)):
            if only is None or k in only:
                send.wait_send()
                recv.wait_recv()

    outs = pl.pallas_call(
        body, name=name, in_specs=[HBM_SPEC] * nbuf + [SEM_SPEC, SEM_SPEC] + [ANY] * len(after),
        out_specs=[HBM_SPEC] * nbuf,
        out_shape=[pltpu.HBM(a.shape, a.dtype) for a in list(srcs) + list(lands)],
        input_output_aliases={i: i for i in range(nbuf)},
        compiler_params=pltpu.CompilerParams(has_side_effects=DATAFLOW))(*srcs, *lands, ssem, rsem, *after)
    return outs


def _gather_finish(lands, tag):
    nt = len(lands)

    def body(*refs):
        outs = refs[nt:2 * nt]
        dsend, drecv = refs[2 * nt:]
        x, y, c, chips = _place()
        sib = (x, y, 1 - c)
        sends = []
        for t in range(nt):
            half = outs[t].shape[1] // 2
            mine = pl.ds(pl.multiple_of(c * half, 16), half)
            for j, (px, py) in enumerate(chips):
                blk = outs[t].at[2 * px + py, mine]
                cp = _remote(blk, blk, dsend.at[t, j], drecv.at[t, j], sib)
                cp.start()
                sends.append(cp)
        for t in range(nt):
            half = outs[t].shape[1] // 2
            other = pl.ds(pl.multiple_of((1 - c) * half, 16), half)
            for j, (px, py) in enumerate(chips):
                blk = outs[t].at[2 * px + py, other]
                _remote(blk, blk, dsend.at[t, j], drecv.at[t, j], sib).wait_recv()
        for cp in sends:
            cp.wait_send()

    return pl.pallas_call(
        body, name="gather_finish_" + tag, in_specs=[ANY] * nt, out_specs=[ANY] * nt,
        out_shape=[jax.ShapeDtypeStruct(a.shape, a.dtype) for a in lands],
        input_output_aliases={t: t for t in range(nt)},
        scratch_shapes=[pltpu.SemaphoreType.DMA((nt, 3)), pltpu.SemaphoreType.DMA((nt, 3))],
        compiler_params=_cp())(*lands)


def _chip_partial(owns, recv, chip_arr, tag):
    nt = len(owns)

    def body(chip_ref, *refs):
        k = pl.program_id(0)
        for t in range(nt):
            p = refs[t][...] + refs[nt + t][...].astype(F32)
            refs[2 * nt + t][...] = p.astype(BF16)

            @pl.when(k == chip_ref[0])
            def _():
                refs[3 * nt + t][...] = p

    blk_specs = [pl.BlockSpec((None,) + a.shape[1:], lambda k, chip_ref: (k, 0, 0)) for a in owns]
    own_specs = [pl.BlockSpec(a.shape[1:], lambda k, chip_ref: (0, 0)) for a in owns]
    return pl.pallas_call(
        body, name="rs_chip_partial_" + tag,
        grid_spec=pltpu.PrefetchScalarGridSpec(num_scalar_prefetch=1, grid=(NSH,), in_specs=blk_specs * 2,
                                               out_specs=blk_specs + own_specs),
        out_shape=[jax.ShapeDtypeStruct(a.shape, BF16) for a in owns]
        + [jax.ShapeDtypeStruct(a.shape[1:], F32) for a in owns],
        compiler_params=_cp(1))(chip_arr, *owns, *recv)


def _sum_chips(own, recv, tag, after=()):
    nt = len(own)

    def body(*refs):
        outs = refs[2 * nt + len(after):]
        for t in range(nt):
            r = refs[nt + t]
            outs[t][...] = ((refs[t][...] + r[0].astype(F32)) + r[1].astype(F32)) + r[2].astype(F32)

    vm = pl.BlockSpec(memory_space=pltpu.VMEM)
    return pl.pallas_call(
        body, name="rs_sum_chips_" + tag, in_specs=[vm] * (2 * nt) + [ANY] * len(after), out_specs=[vm] * nt,
        out_shape=[jax.ShapeDtypeStruct(a.shape, F32) for a in own],
        compiler_params=_cp())(*own, *recv, *after)


def _adamw_math(w, g, m, v):
    m = ADAM_B1 * m + (1.0 - ADAM_B1) * g
    v = ADAM_B2 * v + (1.0 - ADAM_B2) * (g * g)
    m_hat = m / (1.0 - ADAM_B1 ** ADAM_STEP)
    v_hat = v / (1.0 - ADAM_B2 ** ADAM_STEP)
    delta = -ADAM_LR * (m_hat / (jnp.sqrt(v_hat) + ADAM_EPS) + ADAM_WD * w)
    return delta, m, v


ADAM_SPLIT = 4


def _adamw_shards(own, sib, ws, ms, vs, c_arr, tag):
    nt = len(own)

    def body(c_ref, *refs):
        hh = pl.program_id(0)
        mine = hh == c_ref[0]
        for t in range(nt):
            g = jnp.where(mine, refs[t][...], refs[nt + t][...])
            delta, m, v = _adamw_math(refs[2 * nt + t][...], g, refs[3 * nt + t][...], refs[4 * nt + t][...])
            refs[5 * nt + t][...] = g
            refs[6 * nt + t][...] = delta
            refs[7 * nt + t][...] = m
            refs[8 * nt + t][...] = v

    def tile(a):
        return (a.shape[0] // ADAM_SPLIT, a.shape[1])

    half_specs = [pl.BlockSpec(tile(a), lambda hh, q, c_ref: (q, 0)) for a in own]
    full_specs = [pl.BlockSpec(tile(a), lambda hh, q, c_ref: (hh * ADAM_SPLIT + q, 0)) for a in own]
    return pl.pallas_call(
        body, name="adamw_shards_" + tag,
        grid_spec=pltpu.PrefetchScalarGridSpec(num_scalar_prefetch=1, grid=(2, ADAM_SPLIT),
                                               in_specs=half_specs * 2 + full_specs * 3, out_specs=full_specs * 4),
        out_shape=[jax.ShapeDtypeStruct(w.shape, F32) for w in ws] * 4,
        compiler_params=_cp(2))(c_arr, *own, *sib, *ws, *ms, *vs)


SMALL_WPOOL_ROW = 32
SMALL_SCALE_ROW = SMALL_WPOOL_ROW + 4 * GROUP
SMALL_BIAS_ROW = SMALL_SCALE_ROW + 8
SMALL_SINKS_ROW = SMALL_BIAS_ROW + NBUCKET
SMALL_LOSS_ROW = SMALL_SINKS_ROW + 8


def _small_sum_adamw(gathered, wp, mp, vp):
    rows = wp.shape[0]

    def body(g_ref, w_ref, m_ref, v_ref, *rest):
        outs, res = rest[:-1], rest[-1]
        g = g_ref[0]
        for d in range(1, 8):
            g = g + g_ref[d]
        delta, m, v = _adamw_math(w_ref[...], g, m_ref[...], v_ref[...])
        for kind, val in enumerate((g, delta, m, v)):
            res[kind] = val
            gains = outs[8 * kind:8 * kind + 4]
            w_pool_o, scale_o, bias_o, sinks_o = outs[8 * kind + 4:8 * kind + 8]
            for t in range(4):
                for i in range(8):
                    gains[t][:, 128 * i:128 * (i + 1)] = res[kind, pl.ds(8 * t + i, 1), :]
            for grp in range(4):
                w_pool_o[grp] = res[kind, pl.ds(SMALL_WPOOL_ROW + GROUP * grp, GROUP), :]
            for i in range(4):
                scale_o[:, 128 * i:128 * (i + 1)] = res[kind, pl.ds(SMALL_SCALE_ROW + i, 1), :]
            bias_o[...] = res[kind, pl.ds(SMALL_BIAS_ROW, NBUCKET), :][:, 0:NQ]
            sinks_o[...] = res[kind, pl.ds(SMALL_SINKS_ROW, 1), :][:, 0:NQ]
        outs[-1][...] = res[0, pl.ds(SMALL_LOSS_ROW, 1), :]

    vm = pl.BlockSpec(memory_space=pltpu.VMEM)
    one_kind = [jax.ShapeDtypeStruct((1, D), F32)] * 4 + [
        jax.ShapeDtypeStruct((4, GROUP, GROUP), F32), jax.ShapeDtypeStruct((1, POOL_W), F32),
        jax.ShapeDtypeStruct((NBUCKET, NQ), F32), jax.ShapeDtypeStruct((1, NQ), F32)]
    return pl.pallas_call(
        body, name="small_sum_adamw", in_specs=[vm] * 4, out_specs=[vm] * 33,
        out_shape=one_kind * 4 + [jax.ShapeDtypeStruct((1, 128), F32)],
        scratch_shapes=[pltpu.VMEM((4, rows, 128), F32)],
        compiler_params=_cp())(gathered, wp, mp, vp)


def _pack_small(gains4, w_pool, pool_scale, rel_bias, sinks, loss):
    parts = list(gains4) + [w_pool, pool_scale, jnp.pad(rel_bias, ((0, 0), (0, 128 - NQ))), sinks, loss]
    rows = []
    for a in parts:
        flat = a.reshape(-1)
        n = flat.shape[0]
        padded = -(-n // 1024) * 1024
        rows.append(jnp.pad(flat, (0, padded - n)).reshape(-1, 128))
    return jnp.concatenate(rows, axis=0)


def kernel(x, g_pre_mix, w_in, w_pool, pool_scale, rel_bias, sinks, w_out, g_post_mix, g_pre_ffn, w_gate, w_up, w_down, g_post_ffn, loss_target, m_g_pre_mix, m_w_in, m_w_pool, m_pool_scale, m_rel_bias, m_sinks, m_w_out, m_g_post_mix, m_g_pre_ffn, m_w_gate, m_w_up, m_w_down, m_g_post_ffn, v_g_pre_mix, v_w_in, v_w_pool, v_pool_scale, v_rel_bias, v_sinks, v_w_out, v_g_post_mix, v_g_pre_ffn, v_w_gate, v_w_up, v_w_down, v_g_post_ffn):
    c = lax.axis_index("c")
    chip = 2 * lax.axis_index("x") + lax.axis_index("y")

    def shards(a_in, a_out, a_gate, a_up, a_down):
        return (a_in[0].T, a_out[0], a_gate[0].T, a_up[0].T, a_down[0])

    c_arr = c.reshape(1).astype(jnp.int32)
    chip_arr = chip.reshape(1).astype(jnp.int32)

    big_w = shards(w_in, w_out, w_gate, w_up, w_down)
    big_bf = [w.astype(BF16) for w in big_w]
    g_ssem, g_rsem, g_srcs, g_lands, _ = _split_start(
        _gather_copies, 15, big_bf, [lax.empty((NSH,) + a.shape, BF16) for a in big_bf], None, "gather_start")
    fw = {}

    def with_own(land, shard):
        return lax.dynamic_update_slice(land, shard[None], (chip, 0, 0))

    def w_in_ready(*after):
        fw["first"] = _split_wait(_gather_copies, g_ssem, g_rsem, g_srcs, g_lands, after, "gather_win_wait",
                                  only=(0, 1, 2))
        (fw["win"],) = _gather_finish([with_own(fw["first"][5], fw["first"][0])], "win")
        return fw["win"].reshape(IN_W, D)

    def w_out_ready(*after):
        first = fw["first"]
        fw["second"] = _split_wait(_gather_copies, g_ssem, g_rsem, first[:5], [fw["win"]] + list(first[6:]), after,
                                   "gather_wout_wait", only=(3, 4, 5))
        (fw["wout"],) = _gather_finish([with_own(fw["second"][6], fw["second"][1])], "wout")
        return fw["wout"].reshape(D, D), None

    def ffn_weights(after):
        second = fw["second"]
        outs = _split_wait(_gather_copies, g_ssem, g_rsem, second[:5], [second[5], fw["wout"]] + list(second[7:]),
                           [after], "gather_ffn_wait", only=tuple(range(6, 15)))
        return _gather_finish([with_own(land, src) for src, land in zip(outs[2:5], outs[7:])], "ffn")

    rs = {}

    def on_ffn_grads(ffn_g):
        oths = ffn_g[1::2]
        rs["ffn_own"] = ffn_g[0::2]
        rs["x"] = _split_start(_sibling_copies, 3, oths, [lax.empty(a.shape, BF16) for a in oths], ffn_g[0],
                               "rs_exchange_ffn_start")
        return rs["x"][4]

    def on_wout_grads(own, oth, after):
        ssem, rsem, srcs, lands, _ = rs["x"]
        recv_ffn = _split_wait(_sibling_copies, ssem, rsem, srcs, lands, [after], "rs_exchange_ffn_wait")[3:]
        recv_wout = _to_sibling([oth], "rs_exchange_wout")
        outs = _chip_partial([own] + list(rs["ffn_own"]), list(recv_wout) + list(recv_ffn), chip_arr, "early")
        rs["early_own"] = outs[4:]
        rs["s"] = _split_start(_scatter_copies, 12, outs[:4],
                               [lax.empty((3,) + a.shape[1:], BF16) for a in outs[:4]], outs[4], "scatter_early_start")
        return rs["s"][4]

    small = (g_pre_mix, g_post_mix, g_pre_ffn, g_post_ffn, w_pool[0], pool_scale, rel_bias, sinks)
    loss, gx, small_grads, ((win_own, win_oth), _, _) = _local_step(
        x[0], loss_target[0], small, w_in_ready, w_out_ready, ffn_weights, c_arr, on_ffn_grads, on_wout_grads)

    ssem, rsem, srcs, lands, _ = rs["s"]
    recv_early = _split_wait(_scatter_copies, ssem, rsem, srcs, lands, [gx], "scatter_early_wait")[4:]
    finals = _sum_chips(list(rs["early_own"]), list(recv_early), "early")
    sh_ssem, sh_rsem, sh_srcs, sh_lands, sh_token = _split_start(
        _sibling_copies, 4, finals, [lax.empty(a.shape, F32) for a in finals], finals[0], "rs_share_early_start")

    unused = jnp.zeros((1, 1), F32)
    gp = _pack_small(small_grads[:4], *small_grads[4:], loss)
    wp = _pack_small((g_pre_mix, g_post_mix, g_pre_ffn, g_post_ffn), w_pool, pool_scale, rel_bias, sinks, unused)
    mp = _pack_small((m_g_pre_mix, m_g_post_mix, m_g_pre_ffn, m_g_post_ffn), m_w_pool, m_pool_scale, m_rel_bias,
                     m_sinks, unused)
    vp = _pack_small((v_g_pre_mix, v_g_post_mix, v_g_pre_ffn, v_g_post_ffn), v_w_pool, v_pool_scale, v_rel_bias,
                     v_sinks, unused)

    moments = (shards(m_w_in, m_w_out, m_w_gate, m_w_up, m_w_down),
               shards(v_w_in, v_w_out, v_w_gate, v_w_up, v_w_down))
    recv_a = _to_sibling([win_oth], "rs_exchange_win", [sh_token])
    outs = _chip_partial([win_own], recv_a, chip_arr, "win")
    w_ssem, w_rsem, w_srcs, w_lands, w_token = _split_start(
        _scatter_copies, 3, outs[:1], [lax.empty((3,) + outs[0].shape[1:], BF16)], gx, "scatter_win_start")
    slots = lax.dynamic_update_slice(lax.empty((8,) + gp.shape, F32), gp[None], (2 * chip + c, 0, 0))
    p_ssem, p_rsem, p_srcs, p_lands, p_token = _split_start(_peer_copies, 7, [gp], [slots], w_token,
                                                            "small_allgather_start")
    shared = _split_wait(_sibling_copies, sh_ssem, sh_rsem, sh_srcs, sh_lands, [p_token], "rs_share_early_wait")
    adam_e = _adamw_shards(shared[:4], shared[4:], big_w[1:], moments[0][1:], moments[1][1:], c_arr, "early")
    recv_win = _split_wait(_scatter_copies, w_ssem, w_rsem, w_srcs, w_lands, [adam_e[0]], "scatter_win_wait")[1:]
    win_final = _sum_chips([outs[1]], recv_win, "win")
    win_sib = _to_sibling(win_final, "rs_share_win")
    adam_w = _adamw_shards(win_final, win_sib, big_w[:1], moments[0][:1], moments[1][:1], c_arr, "win")
    big_g, big_d, big_m, big_v = [[adam_w[i]] + list(adam_e[4 * i:4 * i + 4]) for i in range(4)]

    gathered = _split_wait(_peer_copies, p_ssem, p_rsem, p_srcs, p_lands, [adam_w[0]], "small_allgather_wait")[1]
    flat = _small_sum_adamw(gathered, wp, mp, vp)
    small_out = [flat[8 * kind:8 * kind + 8] for kind in range(4)]
    total_loss = flat[-1][0, 0]

    def ordered(small8, big4):
        sg1, sg2, sg3, sg4, swp, ssc, srb, ssk = small8
        swp = swp[None]
        bwin, bwout, bwg, bwu, bwd = big4[0].T[None], big4[1][None], big4[2].T[None], big4[3].T[None], big4[4][None]
        return [sg1, bwin, swp, ssc, srb, ssk, bwout, sg2, sg3, bwg, bwu, bwd, sg4]

    res = [total_loss, gx[None]]
    for sm, bg in zip(small_out, (big_g, big_d, big_m, big_v)):
        res += ordered(sm, bg)
    return tuple(res)
```

```python
import numpy as np
import jax
import jax.numpy as jnp
from jax import lax
from jax.experimental import pallas as pl
from jax.experimental.pallas import tpu as pltpu

F32 = jnp.float32
BF16 = jnp.bfloat16

D = 1024
POOL_W = 512
GROUP = 128
WINDOWS = (2, 4, 8, 16)
HALO = 128
NQ = 8
BLK = 128
NBUCKET = 32
IN_W = 1280
DFF = 2816
NSH = 4
FS = DFF // NSH
WIN_S = IN_W // NSH
WOUT_S = D // NSH
EPS = 1e-6
NEG = -1e30
SCALE = 0.125

ADAM_LR = 0.001
ADAM_B1 = 0.9
ADAM_B2 = 0.999
ADAM_EPS = 1e-08
ADAM_WD = 0.01
ADAM_STEP = 10

TM = 512
TM_POOL = 256
TM_FFN = 512
TM_FFN_FWD = 1024
FFN_ROWS = 256
VMEM_LIMIT = 60 * 1024 * 1024

MESH_T = pl.DeviceIdType.MESH
ANY = pl.BlockSpec(memory_space=pl.ANY)


def _cp(n_grid=0, **kw):
    sem = ("arbitrary",) * n_grid if n_grid else None
    return pltpu.CompilerParams(dimension_semantics=sem, vmem_limit_bytes=VMEM_LIMIT, **kw)


def _dot(a, b):
    return jnp.dot(a, b, preferred_element_type=F32)


def _dot_nt(a, b):
    return lax.dot_general(a, b, (((1,), (1,)), ((), ())), preferred_element_type=F32)


def _dot_tn(a, b):
    return lax.dot_general(a, b, (((0,), (0,)), ((), ())), preferred_element_type=F32)


def _rms(x):
    r = lax.rsqrt(jnp.mean(x * x, axis=-1, keepdims=True) + EPS)
    return r, x * r


def _rms_bwd(r, n, g, dout):
    dn = dout * g
    dx = r * (dn - n * jnp.mean(dn * n, axis=-1, keepdims=True))
    dg = jnp.sum(dout * n, axis=0, keepdims=True)
    return dx, dg


def _split(x, n):
    parts = []
    r = x
    for _ in range(n):
        p = r.astype(BF16)
        parts.append(p)
        r = r - p.astype(F32)
    return parts


def _band_dot(a, x, n):
    acc = None
    for p in _split(x, n):
        t = _dot(a, p)
        acc = t if acc is None else acc + t
    return acc


def _bucket_table():
    qi = np.arange(BLK)[None, :]
    kj = np.arange(2 * BLK)[:, None]
    dist = qi + BLK - kj
    n = np.maximum(dist, 0)
    nf = np.maximum(n, 1).astype(np.float32)
    large = 16 + (np.log(nf / np.float32(16)) / np.float32(np.log(128 / 16)) * np.float32(16)).astype(np.int32)
    large = np.minimum(large, NBUCKET - 1)
    return np.where(n < 16, n, large).astype(np.int32)


def _prenorm(x, g1):
    s = x.shape[0]
    tm = min(TM, s)

    def body(x_ref, g_ref, h_ref):
        _, n = _rms(x_ref[...])
        h_ref[...] = (n * g_ref[...]).astype(BF16)

    row = pl.BlockSpec((tm, D), lambda i: (i, 0))
    return pl.pallas_call(
        body, name="prenorm", grid=(s // tm,),
        in_specs=[row, pl.BlockSpec((1, D), lambda i: (0, 0))], out_specs=row,
        out_shape=jax.ShapeDtypeStruct((s, D), BF16),
        compiler_params=_cp(1))(x, g1)


def _inproj_fwd(h1, w_in):
    s = h1.shape[0]
    tm = min(TM, s)

    def body(h_ref, w_ref, u_ref, q_ref, k_ref, v_ref):
        proj = _dot_nt(h_ref[...], w_ref[...])
        u_ref[...] = proj[:, :512]
        q_ref[...] = proj[:, 512:1024].astype(BF16)
        k_ref[...] = proj[:, 1024:1152].astype(BF16)
        v_ref[...] = proj[:, 1152:1280].astype(BF16)

    row = lambda w: pl.BlockSpec((tm, w), lambda i: (i, 0))
    return pl.pallas_call(
        body, name="inproj_fwd", grid=(s // tm,),
        in_specs=[row(D), pl.BlockSpec((IN_W, D), lambda i: (0, 0))],
        out_specs=[row(512), row(512), row(128), row(128)],
        out_shape=[jax.ShapeDtypeStruct((s, 512), F32), jax.ShapeDtypeStruct((s, 512), BF16),
                   jax.ShapeDtypeStruct((s, 128), BF16), jax.ShapeDtypeStruct((s, 128), BF16)],
        compiler_params=_cp(1))(h1, w_in)


def _window_sums(ext, w, lag_sign, tm, terms):
    rows = lax.broadcasted_iota(jnp.int32, (HALO, 2 * HALO), 0)
    cols = lax.broadcasted_iota(jnp.int32, (HALO, 2 * HALO), 1)
    d = rows + HALO - cols if lag_sign > 0 else cols - rows
    band = jnp.where((d >= 0) & (d < w), 1.0, 0.0).astype(BF16)
    return jnp.concatenate([_band_dot(band, ext[r:r + 2 * HALO], terms) for r in range(0, tm, HALO)], axis=0)


def _pooled(ext, cur, t0, tm):
    t = t0 + lax.broadcasted_iota(jnp.int32, (tm, GROUP), 0)
    out = []
    for g, w in enumerate(WINDOWS):
        sl = slice(g * GROUP, (g + 1) * GROUP)
        cnt = jnp.minimum(t + 1, w).astype(F32)
        out.append(_window_sums(ext[:, sl], w, 1, tm, 2) / cnt - cur[:, sl])
    return out


def _pool_fwd(u, w_pool, pool_scale):
    s = u.shape[0]
    tm = min(TM_POOL, s)
    hb = tm // HALO

    def body(uc_ref, uh_ref, wp_ref, sc_ref, o_ref, pooled_ref):
        i = pl.program_id(0)
        cur = uc_ref[...]
        halo = jnp.where(i > 0, uh_ref[...], 0.0)
        ext = jnp.concatenate([halo, cur], axis=0)
        pooled = _pooled(ext, cur, i * tm, tm)
        for g in range(4):
            sl = slice(g * GROUP, (g + 1) * GROUP)
            pb = pooled[g].astype(BF16)
            pooled_ref[:, sl] = pb
            o_ref[:, sl] = (_dot(pb, wp_ref[g]) * sc_ref[:, sl]).astype(BF16)

    row = pl.BlockSpec((tm, 512), lambda i: (i, 0))
    return pl.pallas_call(
        body, name="pool_fwd", grid=(s // tm,),
        in_specs=[row, pl.BlockSpec((HALO, 512), lambda i: (jnp.maximum(i * hb - 1, 0), 0)),
                  pl.BlockSpec((4, GROUP, GROUP), lambda i: (0, 0, 0)),
                  pl.BlockSpec((1, 512), lambda i: (0, 0))],
        out_specs=[row, row],
        out_shape=[jax.ShapeDtypeStruct((s, 512), BF16)] * 2,
        compiler_params=_cp(1))(u, u, w_pool, pool_scale)


def _bias_table(bucket, rel_bias):
    def body(b_ref, rb_ref, o_ref):
        bucket_v = b_ref[...]
        key = lax.broadcasted_iota(jnp.int32, (2 * BLK, BLK), 0)
        dist = lax.broadcasted_iota(jnp.int32, (2 * BLK, BLK), 1) + BLK - key
        inwin = (dist >= 0) & (dist < BLK)
        for h in range(NQ):
            acc = jnp.zeros((2 * BLK, BLK), F32)
            for b in range(NBUCKET):
                acc = jnp.where(bucket_v == b, rb_ref[b, h], acc)
            rest = jnp.where(inwin, acc, NEG)
            o_ref[1, h] = rest
            o_ref[0, h] = jnp.where(key >= BLK, rest, NEG)

    return pl.pallas_call(
        body, name="bias_table",
        in_specs=[pl.BlockSpec(memory_space=pltpu.VMEM), pl.BlockSpec(memory_space=pltpu.SMEM)],
        out_specs=pl.BlockSpec(memory_space=pltpu.VMEM),
        out_shape=jax.ShapeDtypeStruct((2, NQ, 2 * BLK, BLK), F32),
        compiler_params=_cp())(bucket, rel_bias)


HD = 64


def _kv_band(ref, n, transposed):
    pstart = pl.multiple_of(jnp.maximum(n - 1, 0) * BLK, BLK)
    cstart = pl.multiple_of(n * BLK, BLK)
    lo = lax.broadcasted_iota(jnp.int32, (2 * BLK, 128), 1) < HD
    band = jnp.concatenate([ref[pl.ds(pstart, BLK), :], ref[pl.ds(cstart, BLK), :]], axis=0).astype(F32)
    rot = pltpu.roll(band, HD, axis=1)
    halves = ((jnp.where(lo, band, 0.0), jnp.where(lo, 0.0, rot)), (jnp.where(lo, rot, 0.0), jnp.where(lo, 0.0, band)))
    if transposed:
        out = [jnp.concatenate([a.T, b.T], axis=1).astype(BF16) for a, b in halves]
    else:
        out = [jnp.concatenate([a, b], axis=0).astype(BF16) for a, b in halves]
    return out, (lo, pstart, cstart)


def _pair_probs(qt, kk, bias, sk_ref, p):
    sink = jnp.concatenate([jnp.full((1, 1, BLK), sk_ref[0, 2 * p + e], F32) for e in range(2)], axis=0)
    s = _dot_nt(kk, qt).reshape(2, 2 * BLK, BLK) + bias
    m = jnp.maximum(jnp.max(s, axis=1, keepdims=True), sink)
    pr = jnp.exp(s - m)
    es = jnp.exp(sink - m)
    inv = 1.0 / (jnp.sum(pr, axis=1, keepdims=True) + es)
    return pr * inv, es * inv


def _attn_fwd(q, k, v, bias, sinks):
    s = q.shape[0]
    nblk = s // BLK

    def body(q_ref, k_ref, v_ref, b_ref, sk_ref, o_ref, prob_ref, ps_ref):
        n = pl.program_id(0)
        kk, _ = _kv_band(k_ref, n, False)
        vt, _ = _kv_band(v_ref, n, True)
        bidx = jnp.minimum(n, 1)
        for p in range(4):
            h = p // 2
            qt = (q_ref[:, p * 128:(p + 1) * 128].astype(F32) * SCALE).astype(BF16)
            prob, ps = _pair_probs(qt, kk[h], b_ref[bidx, pl.ds(2 * p, 2)], sk_ref, p)
            pb = prob.astype(BF16)
            prob_ref[pl.ds(2 * p, 2)] = pb
            ps_ref[pl.ds(2 * p, 2), :] = ps.reshape(2, BLK)
            acc_t = _dot(vt[h], pb.reshape(4 * BLK, BLK))
            o_ref[:, p * 128:(p + 1) * 128] = acc_t.T.astype(BF16)

    return pl.pallas_call(
        body, name="attn_fwd", grid=(nblk,),
        in_specs=[pl.BlockSpec((BLK, 512), lambda i: (i, 0)),
                  pl.BlockSpec((s, 128), lambda i: (0, 0)),
                  pl.BlockSpec((s, 128), lambda i: (0, 0)),
                  pl.BlockSpec((2, NQ, 2 * BLK, BLK), lambda i: (0, 0, 0, 0)),
                  pl.BlockSpec(memory_space=pltpu.SMEM)],
        out_specs=[pl.BlockSpec((BLK, 512), lambda i: (i, 0)),
                   pl.BlockSpec((None, NQ, 2 * BLK, BLK), lambda i: (i, 0, 0, 0)),
                   pl.BlockSpec((None, NQ, BLK), lambda i: (i, 0, 0))],
        out_shape=[jax.ShapeDtypeStruct((s, 512), BF16), jax.ShapeDtypeStruct((nblk, NQ, 2 * BLK, BLK), BF16),
                   jax.ShapeDtypeStruct((nblk, NQ, BLK), F32)],
        compiler_params=_cp(1))(q, k, v, bias, sinks)


def _outproj_fwd(pool_o, attn_o, w_out, x, g2, g3):
    s = x.shape[0]
    tm = min(TM, s)

    def body(p_ref, a_ref, w_ref, x_ref, g2_ref, g3_ref, mix_ref, x1_ref, h2_ref):
        mix = _dot(p_ref[...], w_ref[0:512, :]) + _dot(a_ref[...], w_ref[512:1024, :])
        mix_ref[...] = mix
        _, n2 = _rms(mix)
        x1 = x_ref[...] + n2 * g2_ref[...]
        x1_ref[...] = x1
        _, n3 = _rms(x1)
        h2_ref[...] = (n3 * g3_ref[...]).astype(BF16)

    row = lambda w: pl.BlockSpec((tm, w), lambda i: (i, 0))
    vec = pl.BlockSpec((1, D), lambda i: (0, 0))
    return pl.pallas_call(
        body, name="outproj_fwd", grid=(s // tm,),
        in_specs=[row(512), row(512), pl.BlockSpec((D, D), lambda i: (0, 0)), row(D), vec, vec],
        out_specs=[row(D), row(D), row(D)],
        out_shape=[jax.ShapeDtypeStruct((s, D), F32), jax.ShapeDtypeStruct((s, D), F32),
                   jax.ShapeDtypeStruct((s, D), BF16)],
        compiler_params=_cp(1))(pool_o, attn_o, w_out, x, g2, g3)


def _silu_parts(g):
    sg = jax.nn.sigmoid(g)
    return sg, g * sg


def _ffn_fwd(h2, wg, wu, wd, x1, target, g4):
    s = h2.shape[0]
    tm = min(TM_FFN_FWD, s)

    def body(h_ref, wg_ref, wu_ref, wd_ref, x1_ref, t_ref, g4_ref,
             gate_ref, up_ref, df_ref, dy_ref, loss_ref, gg4_ref, f_acc):
        i = pl.program_id(0)
        j = pl.program_id(1)
        first = j == 0
        for r in range(0, tm, FFN_ROWS):
            rows = pl.ds(r, min(FFN_ROWS, tm))
            h = h_ref[rows, :]
            gate = _dot_nt(h, wg_ref[...])
            up = _dot_nt(h, wu_ref[...])
            gate_ref[rows, :] = gate.astype(BF16)
            up_ref[rows, :] = up.astype(BF16)
            _, sl = _silu_parts(gate)
            contrib = _dot((sl * up).astype(BF16), wd_ref[...])
            f_acc[rows, :] = jnp.where(first, contrib, f_acc[rows, :] + contrib)

        @pl.when((i == 0) & (j == 0))
        def _():
            loss_ref[...] = jnp.zeros_like(loss_ref)
            gg4_ref[...] = jnp.zeros_like(gg4_ref)

        @pl.when(j == NSH - 1)
        def _():
            r4, n4 = _rms(f_acc[...])
            g4v = g4_ref[...]
            err = x1_ref[...] + n4 * g4v - t_ref[...]
            loss_ref[...] += (0.5 / D) * jnp.sum(err * err).reshape(1, 1)
            dy = err * (1.0 / D)
            dy_ref[...] = dy
            df, dg = _rms_bwd(r4, n4, g4v, dy)
            gg4_ref[...] += dg
            df_ref[...] = df.astype(BF16)

    row = lambda w: pl.BlockSpec((tm, w), lambda i, j: (i, 0))
    sh = lambda r, c: pl.BlockSpec((None, r, c), lambda i, j: (j, 0, 0))
    act = pl.BlockSpec((None, tm, FS), lambda i, j: (j, i, 0))
    vec = pl.BlockSpec((1, D), lambda i, j: (0, 0))
    return pl.pallas_call(
        body, name="ffn_fwd", grid=(s // tm, NSH),
        in_specs=[row(D), sh(FS, D), sh(FS, D), sh(FS, D), row(D), row(D), vec],
        out_specs=[act, act, row(D), row(D), pl.BlockSpec((1, 1), lambda i, j: (0, 0)), vec],
        out_shape=[jax.ShapeDtypeStruct((NSH, s, FS), BF16), jax.ShapeDtypeStruct((NSH, s, FS), BF16),
                   jax.ShapeDtypeStruct((s, D), BF16), jax.ShapeDtypeStruct((s, D), F32),
                   jax.ShapeDtypeStruct((1, 1), F32), jax.ShapeDtypeStruct((1, D), F32)],
        scratch_shapes=[pltpu.VMEM((tm, D), F32)],
        compiler_params=_cp(2))(h2, wg, wu, wd, x1, target, g4)


def _ffn_bwd_act(df, gate, up, wg, wu, wd, dy, x1, mix, g3, g2):
    s = df.shape[0]
    tm = min(TM_FFN, s)

    def body(df_ref, gate_ref, up_ref, wg_ref, wu_ref, wd_ref, dy_ref, x1_ref, mix_ref, g3_ref, g2_ref,
             dgate_ref, dup_ref, dx1_ref, dmix_ref, gg3_ref, gg2_ref, acc):
        i = pl.program_id(0)
        j = pl.program_id(1)
        first = j == 0
        for r in range(0, tm, FFN_ROWS):
            rows = pl.ds(r, min(FFN_ROWS, tm))
            da = _dot_nt(df_ref[rows, :], wd_ref[...])
            g = gate_ref[rows, :].astype(F32)
            u = up_ref[rows, :].astype(F32)
            sg, sl = _silu_parts(g)
            dgate = (da * u * (sg * (1.0 + g * (1.0 - sg)))).astype(BF16)
            dup = (da * sl).astype(BF16)
            dgate_ref[rows, :] = dgate
            dup_ref[rows, :] = dup
            contrib = _dot(dgate, wg_ref[...]) + _dot(dup, wu_ref[...])
            acc[rows, :] = jnp.where(first, contrib, acc[rows, :] + contrib)

        @pl.when((i == 0) & (j == 0))
        def _():
            gg3_ref[...] = jnp.zeros_like(gg3_ref)
            gg2_ref[...] = jnp.zeros_like(gg2_ref)

        @pl.when(j == NSH - 1)
        def _():
            r3, n3 = _rms(x1_ref[...])
            dx1n, dg3 = _rms_bwd(r3, n3, g3_ref[...], acc[...])
            dx1 = dy_ref[...] + dx1n
            dx1_ref[...] = dx1
            gg3_ref[...] += dg3
            r2, n2 = _rms(mix_ref[...])
            dmix, dg2 = _rms_bwd(r2, n2, g2_ref[...], dx1)
            gg2_ref[...] += dg2
            dmix_ref[...] = dmix.astype(BF16)

    row = lambda w: pl.BlockSpec((tm, w), lambda i, j: (i, 0))
    sh = lambda r, c: pl.BlockSpec((None, r, c), lambda i, j: (j, 0, 0))
    act = pl.BlockSpec((None, tm, FS), lambda i, j: (j, i, 0))
    vec = pl.BlockSpec((1, D), lambda i, j: (0, 0))
    return pl.pallas_call(
        body, name="ffn_bwd_act", grid=(s // tm, NSH),
        in_specs=[row(D), act, act, sh(FS, D), sh(FS, D), sh(FS, D), row(D), row(D), row(D), vec, vec],
        out_specs=[act, act, row(D), row(D), vec, vec],
        out_shape=[jax.ShapeDtypeStruct((NSH, s, FS), BF16), jax.ShapeDtypeStruct((NSH, s, FS), BF16),
                   jax.ShapeDtypeStruct((s, D), F32), jax.ShapeDtypeStruct((s, D), BF16),
                   jax.ShapeDtypeStruct((1, D), F32), jax.ShapeDtypeStruct((1, D), F32)],
        scratch_shapes=[pltpu.VMEM((tm, D), F32)],
        compiler_params=_cp(2))(df, gate, up, wg, wu, wd, dy, x1, mix, g3, g2)


SMEM_SPEC = pl.BlockSpec(memory_space=pltpu.SMEM)


def _emit_halves(acc_ref, row0, rows, c, own_ref, oth_ref):
    half = rows // 2
    own_ref[...] = acc_ref[pl.ds(row0 + pl.multiple_of(c * half, 8), half), :]
    oth_ref[...] = acc_ref[pl.ds(row0 + pl.multiple_of((1 - c) * half, 8), half), :].astype(BF16)


def _half_shapes(rows):
    return [jax.ShapeDtypeStruct((NSH, rows // 2, D), F32), jax.ShapeDtypeStruct((NSH, rows // 2, D), BF16)]


def _ffn_bwd_w(h2, gate, up, dgate, dup, df, c_arr):
    s = h2.shape[0]
    tm = min(TM_FFN_FWD, s)
    nt = s // tm

    def body(c_ref, h_ref, gate_ref, up_ref, dgate_ref, dup_ref, df_ref, *rest):
        outs, accs = rest[:6], rest[6:]
        i = pl.program_id(1)
        @pl.when(i == 0)
        def _():
            for acc in accs:
                acc[...] = jnp.zeros_like(acc)

        h = h_ref[...]
        accs[0][...] += _dot_tn(dgate_ref[...], h)
        accs[1][...] += _dot_tn(dup_ref[...], h)
        _, sl = _silu_parts(gate_ref[...].astype(F32))
        a = (sl * up_ref[...].astype(F32)).astype(BF16)
        accs[2][...] += _dot_tn(a, df_ref[...])

        @pl.when(i == nt - 1)
        def _():
            for t, acc in enumerate(accs):
                _emit_halves(acc, 0, FS, c_ref[0], outs[2 * t], outs[2 * t + 1])

    row = lambda w: pl.BlockSpec((tm, w), lambda j, i: (i, 0))
    act = pl.BlockSpec((None, tm, FS), lambda j, i: (j, i, 0))
    half = pl.BlockSpec((None, FS // 2, D), lambda j, i: (j, 0, 0))
    return pl.pallas_call(
        body, name="ffn_bwd_w", grid=(NSH, nt),
        in_specs=[SMEM_SPEC, row(D), act, act, act, act, row(D)],
        out_specs=[half] * 6,
        out_shape=_half_shapes(FS) * 3,
        scratch_shapes=[pltpu.VMEM((FS, D), F32)] * 3,
        compiler_params=_cp(2))(c_arr, h2, gate, up, dgate, dup, df)


def _outproj_bwd(dmix, w_out, pool_o, attn_o, c_arr, dep=None):
    s = dmix.shape[0]
    tm = min(TM, s)
    nt = s // tm

    def body(c_ref, dm_ref, w_ref, p_ref, a_ref, *rest):
        dpool_ref, dattn_ref, own_ref, oth_ref, gw_ref = rest[-5:]
        i = pl.program_id(0)

        @pl.when(i == 0)
        def _():
            gw_ref[...] = jnp.zeros_like(gw_ref)

        dm = dm_ref[...]
        dpool_ref[...] = _dot_nt(dm, w_ref[0:512, :])
        dattn_ref[...] = _dot_nt(dm, w_ref[512:1024, :]).astype(BF16)
        gw_ref[0:512, :] += _dot_tn(p_ref[...], dm)
        gw_ref[512:1024, :] += _dot_tn(a_ref[...], dm)

        @pl.when(i == nt - 1)
        def _():
            for k in range(NSH):
                _emit_halves(gw_ref, k * WOUT_S, WOUT_S, c_ref[0], own_ref.at[k], oth_ref.at[k])

    row = lambda w: pl.BlockSpec((tm, w), lambda i: (i, 0))
    full = pl.BlockSpec((D, D), lambda i: (0, 0))
    half = pl.BlockSpec((NSH, WOUT_S // 2, D), lambda i: (0, 0, 0))
    return pl.pallas_call(
        body, name="outproj_bwd", grid=(nt,),
        in_specs=[SMEM_SPEC, row(D), full, row(512), row(512)] + ([] if dep is None else [ANY]),
        out_specs=[row(512), row(512), half, half],
        out_shape=[jax.ShapeDtypeStruct((s, 512), F32), jax.ShapeDtypeStruct((s, 512), BF16)] + _half_shapes(WOUT_S),
        scratch_shapes=[pltpu.VMEM((D, D), F32)],
        compiler_params=_cp(1))(c_arr, dmix, w_out, pool_o, attn_o, *([] if dep is None else [dep]))


def _pool_bwd(pooled, dpool, w_pool, pool_scale, dep=None):
    s = pooled.shape[0]
    tm = min(TM_POOL, s)
    hb = tm // HALO
    nt = s // tm
    last_halo = s // HALO - 1

    def body(p_ref, dc_ref, dn_ref, wp_ref, sc_ref, *rest):
        du_ref, gwp_ref, gsc_ref = rest[-3:]
        i = pl.program_id(0)

        @pl.when(i == 0)
        def _():
            gwp_ref[...] = jnp.zeros_like(gwp_ref)
            gsc_ref[...] = jnp.zeros_like(gsc_ref)

        dcur = dc_ref[...]
        dnext = jnp.where(i < nt - 1, dn_ref[...], 0.0)
        dext = jnp.concatenate([dcur, dnext], axis=0)
        t_ext = i * tm + lax.broadcasted_iota(jnp.int32, (tm + HALO, GROUP), 0)
        for g, w in enumerate(WINDOWS):
            sl = slice(g * GROUP, (g + 1) * GROUP)
            pb = p_ref[:, sl]
            wp = wp_ref[g]
            mixed = _dot(pb, wp)
            gsc_ref[:, sl] += jnp.sum(dcur[:, sl] * mixed, axis=0, keepdims=True)
            dmixed = (dext[:, sl] * sc_ref[:, sl]).astype(BF16)
            gwp_ref[g] += _dot_tn(pb, dmixed[0:tm])
            dpooled = _dot_nt(dmixed, wp)
            z = dpooled / jnp.minimum(t_ext + 1, w).astype(F32)
            du_ref[:, sl] = (_window_sums(z, w, -1, tm, 2) - dpooled[0:tm]).astype(BF16)

    return pl.pallas_call(
        body, name="pool_bwd", grid=(nt,),
        in_specs=[pl.BlockSpec((tm, 512), lambda i: (i, 0)),
                  pl.BlockSpec((tm, 512), lambda i: (i, 0)),
                  pl.BlockSpec((HALO, 512), lambda i: (jnp.minimum((i + 1) * hb, last_halo), 0)),
                  pl.BlockSpec((4, GROUP, GROUP), lambda i: (0, 0, 0)),
                  pl.BlockSpec((1, 512), lambda i: (0, 0))] + ([] if dep is None else [ANY]),
        out_specs=[pl.BlockSpec((tm, 512), lambda i: (i, 0)),
                   pl.BlockSpec((4, GROUP, GROUP), lambda i: (0, 0, 0)),
                   pl.BlockSpec((1, 512), lambda i: (0, 0))],
        out_shape=[jax.ShapeDtypeStruct((s, 512), BF16), jax.ShapeDtypeStruct((4, GROUP, GROUP), F32),
                   jax.ShapeDtypeStruct((1, 512), F32)],
        compiler_params=_cp(1))(pooled, dpool, dpool, w_pool, pool_scale, *([] if dep is None else [dep]))


def _attn_bwd(q, k, v, do, probs, sink_share, bucket, dep=None):
    s = q.shape[0]
    nblk = s // BLK

    def body(q_ref, do_ref, k_ref, v_ref, prob_ref, ps_ref, bk_ref, *rest):
        dq_ref, dk_ref, dv_ref, grb_ref, gsk_ref, dss_ref = rest[-6:]
        n = pl.program_id(0)

        @pl.when(n == 0)
        def _():
            dk_ref[...] = jnp.zeros_like(dk_ref)
            dv_ref[...] = jnp.zeros_like(dv_ref)
            dss_ref[...] = jnp.zeros_like(dss_ref)
            gsk_ref[...] = jnp.zeros_like(gsk_ref)

        kt, (lo, pstart, cstart) = _kv_band(k_ref, n, True)
        vv, _ = _kv_band(v_ref, n, False)
        lo_q = lax.broadcasted_iota(jnp.int32, (BLK, 128), 1) < HD
        dkk = [jnp.zeros((2 * BLK, 128), F32), jnp.zeros((2 * BLK, 128), F32)]
        dvv = [jnp.zeros((2 * BLK, 128), F32), jnp.zeros((2 * BLK, 128), F32)]

        def by_head(t):
            return jnp.concatenate([jnp.where(lo_q, t, 0.0), jnp.where(lo_q, 0.0, t)], axis=0).astype(BF16)

        for p in range(4):
            h = p // 2
            qt = q_ref[:, p * 128:(p + 1) * 128].astype(F32) * SCALE
            dot = do_ref[:, p * 128:(p + 1) * 128]
            pb = prob_ref[pl.ds(2 * p, 2)]
            prob = pb.astype(F32)
            ps = ps_ref[pl.ds(2 * p, 2), :].reshape(2, 1, BLK)
            dp = _dot_nt(vv[h], dot).reshape(2, 2 * BLK, BLK)
            delta = jnp.sum(prob * dp, axis=1, keepdims=True)
            ds = prob * (dp - delta)
            sink_part = ps * delta
            for e in range(2):
                gs = -jnp.sum(sink_part[e]).reshape(1, 1)
                gsk_ref[pl.ds(2 * p + e, 1), :] += jnp.broadcast_to(gs, (1, 128))
            dss_ref[pl.ds(2 * p, 2)] += ds
            dsb = ds.astype(BF16)
            dq_t = _dot(kt[h], dsb.reshape(4 * BLK, BLK))
            dq_ref[:, p * 128:(p + 1) * 128] = (dq_t.T * SCALE).astype(BF16)
            dkk[h] = dkk[h] + _dot(jnp.concatenate([dsb[0], dsb[1]], axis=1), by_head(qt))
            dvv[h] = dvv[h] + _dot(jnp.concatenate([pb[0], pb[1]], axis=1), by_head(dot.astype(F32)))
        for acc, ref in ((dkk, dk_ref), (dvv, dv_ref)):
            f0 = acc[0] + pltpu.roll(acc[0], HD, axis=1)
            f1 = acc[1] + pltpu.roll(acc[1], HD, axis=1)
            band = jnp.where(lo, f0, f1)
            ref[pl.ds(pstart, BLK), :] += band[0:BLK]
            ref[pl.ds(cstart, BLK), :] += band[BLK:2 * BLK]

        @pl.when(n == nblk - 1)
        def _():
            _relbias_grad(dss_ref, bk_ref[...], grb_ref)

    blk = pl.BlockSpec((BLK, 512), lambda i: (i, 0))
    kv = pl.BlockSpec((s, 128), lambda i: (0, 0))
    row8 = pl.BlockSpec((NQ, 128), lambda i: (0, 0))
    return pl.pallas_call(
        body, name="attn_bwd", grid=(nblk,),
        in_specs=[blk, blk, kv, kv, pl.BlockSpec((None, NQ, 2 * BLK, BLK), lambda i: (i, 0, 0, 0)),
                  pl.BlockSpec((None, NQ, BLK), lambda i: (i, 0, 0)), pl.BlockSpec((2 * BLK, BLK), lambda i: (0, 0))]
        + ([] if dep is None else [ANY]),
        out_specs=[blk, kv, kv, row8, row8],
        out_shape=[jax.ShapeDtypeStruct((s, 512), BF16), jax.ShapeDtypeStruct((s, 128), F32),
                   jax.ShapeDtypeStruct((s, 128), F32), jax.ShapeDtypeStruct((NQ, 128), F32),
                   jax.ShapeDtypeStruct((NQ, 128), F32)],
        scratch_shapes=[pltpu.VMEM((NQ, 2 * BLK, BLK), F32)],
        compiler_params=_cp(1))(q, do, k, v, probs, sink_share, bucket, *([] if dep is None else [dep]))


def _relbias_grad(ds_ref, bucket_v, o_ref):
    lane = lax.broadcasted_iota(jnp.int32, (NQ, 128), 1)
    head_row = lax.broadcasted_iota(jnp.int32, (NQ, BLK), 0)
    acc = jnp.zeros((NQ, 128), F32)
    for b in range(NBUCKET):
        sel = bucket_v == b
        stack = jnp.zeros((NQ, BLK), F32)
        for h in range(NQ):
            col_sums = jnp.sum(jnp.where(sel, ds_ref[h], 0.0), axis=0, keepdims=True)
            stack = jnp.where(head_row == h, col_sums, stack)
        acc = acc + jnp.where(lane == b, jnp.sum(stack, axis=1, keepdims=True), 0.0)
    o_ref[...] = acc


def _inproj_bwd(x, g1, du, dq, dk, dv, w_in, dx1, c_arr, dep=None):
    s = x.shape[0]
    tm = min(TM, s)
    nt = s // tm
    cols = ((0, 512), (512, 1024), (1024, 1152), (1152, 1280))

    def body(c_ref, x_ref, g_ref, du_ref, dq_ref, dk_ref, dv_ref, w_ref, dx1_ref, *rest):
        gx_ref, own_ref, oth_ref, gg_ref, gw_ref = rest[-5:]
        i = pl.program_id(0)

        @pl.when(i == 0)
        def _():
            gw_ref[...] = jnp.zeros_like(gw_ref)
            gg_ref[...] = jnp.zeros_like(gg_ref)

        gv = g_ref[...]
        r1, n1 = _rms(x_ref[...])
        h = (n1 * gv).astype(BF16)
        parts = (du_ref[...], dq_ref[...], dk_ref[...].astype(BF16), dv_ref[...].astype(BF16))
        dh = None
        for (a, b), dpart in zip(cols, parts):
            t = _dot(dpart, w_ref[a:b, :])
            dh = t if dh is None else dh + t
            gw_ref[a:b, :] += _dot_tn(dpart, h)
        dx, dg = _rms_bwd(r1, n1, gv, dh)
        gg_ref[...] += dg
        gx_ref[...] = dx1_ref[...] + dx

        @pl.when(i == nt - 1)
        def _():
            for k in range(NSH):
                _emit_halves(gw_ref, k * WIN_S, WIN_S, c_ref[0], own_ref.at[k], oth_ref.at[k])

    row = lambda w: pl.BlockSpec((tm, w), lambda i: (i, 0))
    vec = pl.BlockSpec((1, D), lambda i: (0, 0))
    full = pl.BlockSpec((IN_W, D), lambda i: (0, 0))
    half = pl.BlockSpec((NSH, WIN_S // 2, D), lambda i: (0, 0, 0))
    return pl.pallas_call(
        body, name="inproj_bwd", grid=(nt,),
        in_specs=[SMEM_SPEC, row(D), vec, row(512), row(512), row(128), row(128), full, row(D)]
        + ([] if dep is None else [ANY]),
        out_specs=[row(D), half, half, vec],
        out_shape=[jax.ShapeDtypeStruct((s, D), F32)] + _half_shapes(WIN_S) + [jax.ShapeDtypeStruct((1, D), F32)],
        scratch_shapes=[pltpu.VMEM((IN_W, D), F32)],
        compiler_params=_cp(1))(c_arr, x, g1, du, dq, dk, dv, w_in, dx1, *([] if dep is None else [dep]))


def _local_step(x, target, small, w_in, w_out, ffn_weights, c_arr, on_ffn_grads=None, on_wout_grads=None,
                on_attn_grads=None):
    g1, g2, g3, g4, w_pool, pool_scale, rel_bias, sinks = small
    bucket = jnp.asarray(_bucket_table())
    wp_bf = w_pool.astype(BF16)
    bias = _bias_table(bucket, rel_bias)
    h1 = _prenorm(x, g1)
    if callable(w_in):
        w_in = w_in(bias, h1)
    u, q, k, v = _inproj_fwd(h1, w_in)
    pool_o, pooled = _pool_fwd(u, wp_bf, pool_scale)
    attn_o, probs, sink_share = _attn_fwd(q, k, v, bias, sinks)
    g2_fwd = g2
    if callable(w_out):
        w_out, after = w_out(pool_o, attn_o)
        if after is not None:
            g2_fwd = g2 + after[:1, :1]
    mix, x1, h2 = _outproj_fwd(pool_o, attn_o, w_out, x, g2_fwd, g3)
    wg, wu, wd = ffn_weights(h2) if callable(ffn_weights) else ffn_weights
    gate, up, df, dy, loss, gg4 = _ffn_fwd(h2, wg, wu, wd, x1, target, g4)
    dgate, dup, dx1, dmix, gg3, gg2 = _ffn_bwd_act(df, gate, up, wg, wu, wd, dy, x1, mix, g3, g2)
    ffn_g = _ffn_bwd_w(h2, gate, up, dgate, dup, df, c_arr)
    dep = None if on_ffn_grads is None else on_ffn_grads(ffn_g)
    dpool, dattn, wout_own, wout_oth = _outproj_bwd(dmix, w_out, pool_o, attn_o, c_arr, dep)
    dep = None if on_wout_grads is None else on_wout_grads(wout_own, wout_oth, dpool)
    du, gwp, gsc = _pool_bwd(pooled, dpool, wp_bf, pool_scale, dep)
    dq, dk, dv, grb, gsk = _attn_bwd(q, k, v, dattn, probs, sink_share, bucket, dep)
    dep = None if on_attn_grads is None else on_attn_grads([du, dq])
    gx, win_own, win_oth, gg1 = _inproj_bwd(x, g1, du, dq, dk, dv, w_in, dx1, c_arr, dep)
    small_grads = (gg1, gg2, gg3, gg4, gwp, gsc, grb[:, :NBUCKET].T, gsk[:, 0].reshape(1, NQ))
    return loss, gx, small_grads, ((win_own, win_oth), (wout_own, wout_oth), ffn_g)


def _place():
    x, y, c = lax.axis_index("x"), lax.axis_index("y"), lax.axis_index("c")
    chips = ((1 - x, y), (x, 1 - y), (1 - x, 1 - y))
    return x, y, c, chips


def _remote(src, dst, ssem, rsem, dev):
    return pltpu.make_async_remote_copy(src_ref=src, dst_ref=dst, send_sem=ssem, recv_sem=rsem,
                                        device_id=dev, device_id_type=MESH_T)


def _to_sibling(arrs, name, after=()):
    nt, na = len(arrs), len(after)

    def body(*refs):
        srcs, dsts = refs[:nt], refs[nt + na:2 * nt + na]
        ssem, rsem = refs[2 * nt + na:]
        x, y, c, _ = _place()
        cps = [_remote(srcs[t], dsts[t], ssem.at[t], rsem.at[t], (x, y, 1 - c)) for t in range(nt)]
        for cp in cps:
            cp.start()
        for cp in cps:
            cp.wait()

    return pl.pallas_call(
        body, name=name, in_specs=[ANY] * (nt + na), out_specs=[ANY] * nt,
        out_shape=[jax.ShapeDtypeStruct(a.shape, a.dtype) for a in arrs],
        scratch_shapes=[pltpu.SemaphoreType.DMA((nt,)), pltpu.SemaphoreType.DMA((nt,))],
        compiler_params=_cp())(*arrs, *after)


HBM_SPEC = pl.BlockSpec(memory_space=pltpu.HBM)
SEM_SPEC = pl.BlockSpec(memory_space=pltpu.SEMAPHORE)
DATAFLOW = pltpu.SideEffectType.DATAFLOW_SIDE_EFFECTING


def _gather_copies(srcs, lands, ssem, rsem):
    x, y, c, chips = _place()
    me = 2 * x + y
    out = []
    for t in range(len(srcs)):
        half = srcs[t].shape[0] // 2
        mine = pl.ds(pl.multiple_of(c * half, 16), half)
        for j, (px, py) in enumerate(chips):
            k = 3 * t + j
            send = _remote(srcs[t].at[mine], lands[t].at[me, mine], ssem.at[k], rsem.at[k], (px, py, c))
            blk = lands[t].at[2 * px + py, mine]
            out.append((send, _remote(blk, blk, ssem.at[k], rsem.at[k], (px, py, c))))
    return out


def _scatter_copies(srcs, lands, ssem, rsem):
    x, y, c, chips = _place()
    out = []
    for t in range(len(srcs)):
        for j, (px, py) in enumerate(chips):
            k = 3 * t + j
            send = _remote(srcs[t].at[2 * px + py], lands[t].at[j], ssem.at[k], rsem.at[k], (px, py, c))
            out.append((send, _remote(lands[t].at[j], lands[t].at[j], ssem.at[k], rsem.at[k], (px, py, c))))
    return out


def _sibling_copies(srcs, lands, ssem, rsem):
    x, y, c, _ = _place()
    sib = (x, y, 1 - c)
    return [(_remote(srcs[t], lands[t], ssem.at[t], rsem.at[t], sib),
             _remote(lands[t], lands[t], ssem.at[t], rsem.at[t], sib)) for t in range(len(srcs))]


def _forward_copies(srcs, lands, ssem, rsem):
    x, y, c, chips = _place()
    sib = (x, y, 1 - c)
    out = []
    for t in range(len(lands)):
        half = lands[t].shape[1] // 2
        mine = pl.ds(pl.multiple_of(c * half, 16), half)
        other = pl.ds(pl.multiple_of((1 - c) * half, 16), half)
        for j, (px, py) in enumerate(chips):
            k = 3 * t + j
            blk_m, blk_o = lands[t].at[2 * px + py, mine], lands[t].at[2 * px + py, other]
            out.append((_remote(blk_m, blk_m, ssem.at[k], rsem.at[k], sib),
                        _remote(blk_o, blk_o, ssem.at[k], rsem.at[k], sib)))
    return out


def _peer_copies(srcs, lands, ssem, rsem):
    x, y, c, _ = _place()
    me = 4 * x + 2 * y + c
    out = []
    for kk in range(1, 8):
        px, py, pc = x ^ (kk >> 2), y ^ ((kk >> 1) & 1), c ^ (kk & 1)
        send = _remote(srcs[0], lands[0].at[me], ssem.at[kk - 1], rsem.at[kk - 1], (px, py, pc))
        slot = lands[0].at[4 * px + 2 * py + pc]
        out.append((send, _remote(slot, slot, ssem.at[kk - 1], rsem.at[kk - 1], (px, py, pc))))
    return out


def _split_start(plan, ncopy, srcs, lands, after, name):
    ns, nbuf = len(srcs), len(srcs) + len(lands)
    extra = [] if after is None else [after]
    n_in = nbuf + len(extra)

    def body(*refs):
        ssem, rsem, token = refs[n_in], refs[n_in + 1], refs[-1]
        for send, _ in plan(refs[:ns], refs[ns:nbuf], ssem, rsem):
            send.start()
        token[...] = jnp.zeros_like(token)

    bufs = [pltpu.with_memory_space_constraint(a, pltpu.HBM) for a in list(srcs) + list(lands)]
    outs = pl.pallas_call(
        body, name=name, in_specs=[HBM_SPEC] * nbuf + [ANY] * len(extra),
        out_specs=[SEM_SPEC, SEM_SPEC] + [HBM_SPEC] * nbuf + [pl.BlockSpec(memory_space=pltpu.VMEM)],
        out_shape=[pltpu.SemaphoreType.DMA((ncopy,)), pltpu.SemaphoreType.DMA((ncopy,))]
        + [pltpu.HBM(a.shape, a.dtype) for a in bufs] + [jax.ShapeDtypeStruct((8, 128), F32)],
        input_output_aliases={i: 2 + i for i in range(nbuf)},
        compiler_params=pltpu.CompilerParams(has_side_effects=DATAFLOW))(*bufs, *extra)
    return outs[0], outs[1], outs[2:2 + ns], outs[2 + ns:2 + nbuf], outs[-1]


def _split_wait(plan, ssem, rsem, srcs, lands, after, name, only=None):
    ns, nbuf = len(srcs), len(srcs) + len(lands)

    def body(*refs):
        s_ref, r_ref = refs[nbuf], refs[nbuf + 1]
        for k, (send, recv) in enumerate(plan(refs[:ns], refs[ns:nbuf], s_ref, r_ref)):
            if only is None or k in only:
                send.wait_send()
                recv.wait_recv()

    outs = pl.pallas_call(
        body, name=name, in_specs=[HBM_SPEC] * nbuf + [SEM_SPEC, SEM_SPEC] + [ANY] * len(after),
        out_specs=[HBM_SPEC] * nbuf,
        out_shape=[pltpu.HBM(a.shape, a.dtype) for a in list(srcs) + list(lands)],
        input_output_aliases={i: i for i in range(nbuf)},
        compiler_params=pltpu.CompilerParams(has_side_effects=DATAFLOW))(*srcs, *lands, ssem, rsem, *after)
    return outs


def _gather_finish(lands, tag):
    nt = len(lands)

    def body(*refs):
        outs = refs[nt:2 * nt]
        dsend, drecv = refs[2 * nt:]
        x, y, c, chips = _place()
        sib = (x, y, 1 - c)
        sends = []
        for t in range(nt):
            half = outs[t].shape[1] // 2
            mine = pl.ds(pl.multiple_of(c * half, 16), half)
            for j, (px, py) in enumerate(chips):
                blk = outs[t].at[2 * px + py, mine]
                cp = _remote(blk, blk, dsend.at[t, j], drecv.at[t, j], sib)
                cp.start()
                sends.append(cp)
        for t in range(nt):
            half = outs[t].shape[1] // 2
            other = pl.ds(pl.multiple_of((1 - c) * half, 16), half)
            for j, (px, py) in enumerate(chips):
                blk = outs[t].at[2 * px + py, other]
                _remote(blk, blk, dsend.at[t, j], drecv.at[t, j], sib).wait_recv()
        for cp in sends:
            cp.wait_send()

    return pl.pallas_call(
        body, name="gather_finish_" + tag, in_specs=[ANY] * nt, out_specs=[ANY] * nt,
        out_shape=[jax.ShapeDtypeStruct(a.shape, a.dtype) for a in lands],
        input_output_aliases={t: t for t in range(nt)},
        scratch_shapes=[pltpu.SemaphoreType.DMA((nt, 3)), pltpu.SemaphoreType.DMA((nt, 3))],
        compiler_params=_cp())(*lands)


def _chip_partial(owns, recv, chip_arr, tag):
    nt = len(owns)

    def body(chip_ref, *refs):
        k = pl.program_id(0)
        for t in range(nt):
            p = refs[t][...] + refs[nt + t][...].astype(F32)
            refs[2 * nt + t][...] = p.astype(BF16)

            @pl.when(k == chip_ref[0])
            def _():
                refs[3 * nt + t][...] = p

    blk_specs = [pl.BlockSpec((None,) + a.shape[1:], lambda k, chip_ref: (k, 0, 0)) for a in owns]
    own_specs = [pl.BlockSpec(a.shape[1:], lambda k, chip_ref: (0, 0)) for a in owns]
    return pl.pallas_call(
        body, name="rs_chip_partial_" + tag,
        grid_spec=pltpu.PrefetchScalarGridSpec(num_scalar_prefetch=1, grid=(NSH,), in_specs=blk_specs * 2,
                                               out_specs=blk_specs + own_specs),
        out_shape=[jax.ShapeDtypeStruct(a.shape, BF16) for a in owns]
        + [jax.ShapeDtypeStruct(a.shape[1:], F32) for a in owns],
        compiler_params=_cp(1))(chip_arr, *owns, *recv)


def _sum_chips(own, recv, tag, after=()):
    nt = len(own)

    def body(*refs):
        outs = refs[2 * nt + len(after):]
        for t in range(nt):
            r = refs[nt + t]
            outs[t][...] = ((refs[t][...] + r[0].astype(F32)) + r[1].astype(F32)) + r[2].astype(F32)

    vm = pl.BlockSpec(memory_space=pltpu.VMEM)
    return pl.pallas_call(
        body, name="rs_sum_chips_" + tag, in_specs=[vm] * (2 * nt) + [ANY] * len(after), out_specs=[vm] * nt,
        out_shape=[jax.ShapeDtypeStruct(a.shape, F32) for a in own],
        compiler_params=_cp())(*own, *recv, *after)


def _adamw_math(w, g, m, v):
    m = ADAM_B1 * m + (1.0 - ADAM_B1) * g
    v = ADAM_B2 * v + (1.0 - ADAM_B2) * (g * g)
    m_hat = m / (1.0 - ADAM_B1 ** ADAM_STEP)
    v_hat = v / (1.0 - ADAM_B2 ** ADAM_STEP)
    delta = -ADAM_LR * (m_hat / (jnp.sqrt(v_hat) + ADAM_EPS) + ADAM_WD * w)
    return delta, m, v


ADAM_SPLIT = 4


def _adamw_shards(own, sib, ws, ms, vs, c_arr, tag):
    nt = len(own)

    def body(c_ref, *refs):
        hh = pl.program_id(0)
        mine = hh == c_ref[0]
        for t in range(nt):
            g = jnp.where(mine, refs[t][...], refs[nt + t][...])
            delta, m, v = _adamw_math(refs[2 * nt + t][...], g, refs[3 * nt + t][...], refs[4 * nt + t][...])
            refs[5 * nt + t][...] = g
            refs[6 * nt + t][...] = delta
            refs[7 * nt + t][...] = m
            refs[8 * nt + t][...] = v

    def tile(a):
        return (a.shape[0] // ADAM_SPLIT, a.shape[1])

    half_specs = [pl.BlockSpec(tile(a), lambda hh, q, c_ref: (q, 0)) for a in own]
    full_specs = [pl.BlockSpec(tile(a), lambda hh, q, c_ref: (hh * ADAM_SPLIT + q, 0)) for a in own]
    return pl.pallas_call(
        body, name="adamw_shards_" + tag,
        grid_spec=pltpu.PrefetchScalarGridSpec(num_scalar_prefetch=1, grid=(2, ADAM_SPLIT),
                                               in_specs=half_specs * 2 + full_specs * 3, out_specs=full_specs * 4),
        out_shape=[jax.ShapeDtypeStruct(w.shape, F32) for w in ws] * 4,
        compiler_params=_cp(2))(c_arr, *own, *sib, *ws, *ms, *vs)


SMALL_WPOOL_ROW = 32
SMALL_SCALE_ROW = SMALL_WPOOL_ROW + 4 * GROUP
SMALL_BIAS_ROW = SMALL_SCALE_ROW + 8
SMALL_SINKS_ROW = SMALL_BIAS_ROW + NBUCKET
SMALL_LOSS_ROW = SMALL_SINKS_ROW + 8


def _small_sum_adamw(gathered, wp, mp, vp):
    rows = wp.shape[0]

    def body(g_ref, w_ref, m_ref, v_ref, *rest):
        outs, res = rest[:-1], rest[-1]
        g = g_ref[0]
        for d in range(1, 8):
            g = g + g_ref[d]
        delta, m, v = _adamw_math(w_ref[...], g, m_ref[...], v_ref[...])
        for kind, val in enumerate((g, delta, m, v)):
            res[kind] = val
            gains = outs[8 * kind:8 * kind + 4]
            w_pool_o, scale_o, bias_o, sinks_o = outs[8 * kind + 4:8 * kind + 8]
            for t in range(4):
                for i in range(8):
                    gains[t][:, 128 * i:128 * (i + 1)] = res[kind, pl.ds(8 * t + i, 1), :]
            for grp in range(4):
                w_pool_o[grp] = res[kind, pl.ds(SMALL_WPOOL_ROW + GROUP * grp, GROUP), :]
            for i in range(4):
                scale_o[:, 128 * i:128 * (i + 1)] = res[kind, pl.ds(SMALL_SCALE_ROW + i, 1), :]
            bias_o[...] = res[kind, pl.ds(SMALL_BIAS_ROW, NBUCKET), :][:, 0:NQ]
            sinks_o[...] = res[kind, pl.ds(SMALL_SINKS_ROW, 1), :][:, 0:NQ]
        outs[-1][...] = res[0, pl.ds(SMALL_LOSS_ROW, 1), :]

    vm = pl.BlockSpec(memory_space=pltpu.VMEM)
    one_kind = [jax.ShapeDtypeStruct((1, D), F32)] * 4 + [
        jax.ShapeDtypeStruct((4, GROUP, GROUP), F32), jax.ShapeDtypeStruct((1, POOL_W), F32),
        jax.ShapeDtypeStruct((NBUCKET, NQ), F32), jax.ShapeDtypeStruct((1, NQ), F32)]
    return pl.pallas_call(
        body, name="small_sum_adamw", in_specs=[vm] * 4, out_specs=[vm] * 33,
        out_shape=one_kind * 4 + [jax.ShapeDtypeStruct((1, 128), F32)],
        scratch_shapes=[pltpu.VMEM((4, rows, 128), F32)],
        compiler_params=_cp())(gathered, wp, mp, vp)


def _pack_small(gains4, w_pool, pool_scale, rel_bias, sinks, loss):
    parts = list(gains4) + [w_pool, pool_scale, jnp.pad(rel_bias, ((0, 0), (0, 128 - NQ))), sinks, loss]
    rows = []
    for a in parts:
        flat = a.reshape(-1)
        n = flat.shape[0]
        padded = -(-n // 1024) * 1024
        rows.append(jnp.pad(flat, (0, padded - n)).reshape(-1, 128))
    return jnp.concatenate(rows, axis=0)


def kernel(x, g_pre_mix, w_in, w_pool, pool_scale, rel_bias, sinks, w_out, g_post_mix, g_pre_ffn, w_gate, w_up, w_down, g_post_ffn, loss_target, m_g_pre_mix, m_w_in, m_w_pool, m_pool_scale, m_rel_bias, m_sinks, m_w_out, m_g_post_mix, m_g_pre_ffn, m_w_gate, m_w_up, m_w_down, m_g_post_ffn, v_g_pre_mix, v_w_in, v_w_pool, v_pool_scale, v_rel_bias, v_sinks, v_w_out, v_g_post_mix, v_g_pre_ffn, v_w_gate, v_w_up, v_w_down, v_g_post_ffn):
    c = lax.axis_index("c")
    chip = 2 * lax.axis_index("x") + lax.axis_index("y")

    def shards(a_in, a_out, a_gate, a_up, a_down):
        return (a_in[0].T, a_out[0], a_gate[0].T, a_up[0].T, a_down[0])

    c_arr = c.reshape(1).astype(jnp.int32)
    chip_arr = chip.reshape(1).astype(jnp.int32)

    big_w = shards(w_in, w_out, w_gate, w_up, w_down)
    big_bf = [w.astype(BF16) for w in big_w]
    g_ssem, g_rsem, g_srcs, g_lands, _ = _split_start(
        _gather_copies, 15, big_bf, [lax.empty((NSH,) + a.shape, BF16) for a in big_bf], None, "gather_start")
    fw = {}

    def with_own(land, shard):
        return lax.dynamic_update_slice(land, shard[None], (chip, 0, 0))

    def w_in_ready(*after):
        fw["first"] = _split_wait(_gather_copies, g_ssem, g_rsem, g_srcs, g_lands, after, "gather_win_wait",
                                  only=(0, 1, 2))
        (fw["win"],) = _gather_finish([with_own(fw["first"][5], fw["first"][0])], "win")
        return fw["win"].reshape(IN_W, D)

    def w_out_ready(*after):
        first = fw["first"]
        fw["second"] = _split_wait(_gather_copies, g_ssem, g_rsem, first[:5], [fw["win"]] + list(first[6:]), after,
                                   "gather_wout_wait", only=(3, 4, 5))
        second = fw["second"]
        (wout_all,) = _gather_finish([with_own(second[6], second[1])], "wout")
        outs = _split_wait(_gather_copies, g_ssem, g_rsem, second[:5], [second[5], wout_all] + list(second[7:]),
                           [wout_all], "gather_ffn_wait", only=tuple(range(6, 15)))
        lands = [with_own(land, src) for src, land in zip(outs[2:5], outs[7:])]
        fw["f"] = _split_start(_forward_copies, 9, [], lands, None, "gather_forward_start")
        return wout_all.reshape(D, D), fw["f"][4]

    def ffn_weights(after):
        ssem, rsem, _, lands, _ = fw["f"]
        return _split_wait(_forward_copies, ssem, rsem, [], lands, [after], "gather_forward_wait")

    rs = {}

    def on_ffn_grads(ffn_g):
        oths = ffn_g[1::2]
        rs["ffn_own"] = ffn_g[0::2]
        rs["x"] = _split_start(_sibling_copies, 3, oths, [lax.empty(a.shape, BF16) for a in oths], ffn_g[0],
                               "rs_exchange_ffn_start")
        return rs["x"][4]

    def on_wout_grads(own, oth, after):
        ssem, rsem, srcs, lands, _ = rs["x"]
        recv_ffn = _split_wait(_sibling_copies, ssem, rsem, srcs, lands, [after], "rs_exchange_ffn_wait")[3:]
        recv_wout = _to_sibling([oth], "rs_exchange_wout")
        outs = _chip_partial([own] + list(rs["ffn_own"]), list(recv_wout) + list(recv_ffn), chip_arr, "early")
        rs["early_own"] = outs[4:]
        rs["s"] = _split_start(_scatter_copies, 12, outs[:4],
                               [lax.empty((3,) + a.shape[1:], BF16) for a in outs[:4]], outs[4], "scatter_early_start")
        return rs["s"][4]

    small = (g_pre_mix, g_post_mix, g_pre_ffn, g_post_ffn, w_pool[0], pool_scale, rel_bias, sinks)
    loss, gx, small_grads, ((win_own, win_oth), _, _) = _local_step(
        x[0], loss_target[0], small, w_in_ready, w_out_ready, ffn_weights, c_arr, on_ffn_grads, on_wout_grads)

    ssem, rsem, srcs, lands, _ = rs["s"]
    recv_early = _split_wait(_scatter_copies, ssem, rsem, srcs, lands, [gx], "scatter_early_wait")[4:]
    finals = _sum_chips(list(rs["early_own"]), list(recv_early), "early")
    sh_ssem, sh_rsem, sh_srcs, sh_lands, sh_token = _split_start(
        _sibling_copies, 4, finals, [lax.empty(a.shape, F32) for a in finals], finals[0], "rs_share_early_start")

    unused = jnp.zeros((1, 1), F32)
    gp = _pack_small(small_grads[:4], *small_grads[4:], loss)
    wp = _pack_small((g_pre_mix, g_post_mix, g_pre_ffn, g_post_ffn), w_pool, pool_scale, rel_bias, sinks, unused)
    mp = _pack_small((m_g_pre_mix, m_g_post_mix, m_g_pre_ffn, m_g_post_ffn), m_w_pool, m_pool_scale, m_rel_bias,
                     m_sinks, unused)
    vp = _pack_small((v_g_pre_mix, v_g_post_mix, v_g_pre_ffn, v_g_post_ffn), v_w_pool, v_pool_scale, v_rel_bias,
                     v_sinks, unused)

    moments = (shards(m_w_in, m_w_out, m_w_gate, m_w_up, m_w_down),
               shards(v_w_in, v_w_out, v_w_gate, v_w_up, v_w_down))
    recv_a = _to_sibling([win_oth], "rs_exchange_win", [sh_token])
    outs = _chip_partial([win_own], recv_a, chip_arr, "win")
    w_ssem, w_rsem, w_srcs, w_lands, w_token = _split_start(
        _scatter_copies, 3, outs[:1], [lax.empty((3,) + outs[0].shape[1:], BF16)], gx, "scatter_win_start")
    slots = lax.dynamic_update_slice(lax.empty((8,) + gp.shape, F32), gp[None], (2 * chip + c, 0, 0))
    p_ssem, p_rsem, p_srcs, p_lands, p_token = _split_start(_peer_copies, 7, [gp], [slots], w_token,
                                                            "small_allgather_start")
    shared = _split_wait(_sibling_copies, sh_ssem, sh_rsem, sh_srcs, sh_lands, [p_token], "rs_share_early_wait")
    adam_e = _adamw_shards(shared[:4], shared[4:], big_w[1:], moments[0][1:], moments[1][1:], c_arr, "early")
    recv_win = _split_wait(_scatter_copies, w_ssem, w_rsem, w_srcs, w_lands, [adam_e[0]], "scatter_win_wait")[1:]
    win_final = _sum_chips([outs[1]], recv_win, "win")
    win_sib = _to_sibling(win_final, "rs_share_win")
    adam_w = _adamw_shards(win_final, win_sib, big_w[:1], moments[0][:1], moments[1][:1], c_arr, "win")
    big_g, big_d, big_m, big_v = [[adam_w[i]] + list(adam_e[4 * i:4 * i + 4]) for i in range(4)]

    gathered = _split_wait(_peer_copies, p_ssem, p_rsem, p_srcs, p_lands, [adam_w[0]], "small_allgather_wait")[1]
    flat = _small_sum_adamw(gathered, wp, mp, vp)
    small_out = [flat[8 * kind:8 * kind + 8] for kind in range(4)]
    total_loss = flat[-1][0, 0]

    def ordered(small8, big4):
        sg1, sg2, sg3, sg4, swp, ssc, srb, ssk = small8
        swp = swp[None]
        bwin, bwout, bwg, bwu, bwd = big4[0].T[None], big4[1][None], big4[2].T[None], big4[3].T[None], big4[4][None]
        return [sg1, bwin, swp, ssc, srb, ssk, bwout, sg2, sg3, bwg, bwu, bwd, sg4]

    res = [total_loss, gx[None]]
    for sm, bg in zip(small_out, (big_g, big_d, big_m, big_v)):
        res += ordered(sm, bg)
    return tuple(res)
```

```python
import numpy as np
import jax
import jax.numpy as jnp
from jax import lax
from jax.experimental import pallas as pl
from jax.experimental.pallas import tpu as pltpu

F32 = jnp.float32
BF16 = jnp.bfloat16

D = 1024
POOL_W = 512
GROUP = 128
WINDOWS = (2, 4, 8, 16)
HALO = 128
NQ = 8
BLK = 128
NBUCKET = 32
IN_W = 1280
DFF = 2816
NSH = 4
FS = DFF // NSH
WIN_S = IN_W // NSH
WOUT_S = D // NSH
EPS = 1e-6
NEG = -1e30
SCALE = 0.125

ADAM_LR = 0.001
ADAM_B1 = 0.9
ADAM_B2 = 0.999
ADAM_EPS = 1e-08
ADAM_WD = 0.01
ADAM_STEP = 10

TM = 512
TM_POOL = 256
TM_FFN = 512
TM_FFN_FWD = 1024
FFN_ROWS = 256
VMEM_LIMIT = 60 * 1024 * 1024

MESH_T = pl.DeviceIdType.MESH
ANY = pl.BlockSpec(memory_space=pl.ANY)


def _cp(n_grid=0, **kw):
    sem = ("arbitrary",) * n_grid if n_grid else None
    return pltpu.CompilerParams(dimension_semantics=sem, vmem_limit_bytes=VMEM_LIMIT, **kw)


def _dot(a, b):
    return jnp.dot(a, b, preferred_element_type=F32)


def _dot_nt(a, b):
    return lax.dot_general(a, b, (((1,), (1,)), ((), ())), preferred_element_type=F32)


def _dot_tn(a, b):
    return lax.dot_general(a, b, (((0,), (0,)), ((), ())), preferred_element_type=F32)


def _rms(x):
    r = lax.rsqrt(jnp.mean(x * x, axis=-1, keepdims=True) + EPS)
    return r, x * r


def _rms_bwd(r, n, g, dout):
    dn = dout * g
    dx = r * (dn - n * jnp.mean(dn * n, axis=-1, keepdims=True))
    dg = jnp.sum(dout * n, axis=0, keepdims=True)
    return dx, dg


def _split(x, n):
    parts = []
    r = x
    for _ in range(n):
        p = r.astype(BF16)
        parts.append(p)
        r = r - p.astype(F32)
    return parts


def _band_dot(a, x, n):
    acc = None
    for p in _split(x, n):
        t = _dot(a, p)
        acc = t if acc is None else acc + t
    return acc


def _bucket_table():
    qi = np.arange(BLK)[None, :]
    kj = np.arange(2 * BLK)[:, None]
    dist = qi + BLK - kj
    n = np.maximum(dist, 0)
    nf = np.maximum(n, 1).astype(np.float32)
    large = 16 + (np.log(nf / np.float32(16)) / np.float32(np.log(128 / 16)) * np.float32(16)).astype(np.int32)
    large = np.minimum(large, NBUCKET - 1)
    return np.where(n < 16, n, large).astype(np.int32)


def _prenorm(x, g1):
    s = x.shape[0]
    tm = min(TM, s)

    def body(x_ref, g_ref, h_ref):
        _, n = _rms(x_ref[...])
        h_ref[...] = (n * g_ref[...]).astype(BF16)

    row = pl.BlockSpec((tm, D), lambda i: (i, 0))
    return pl.pallas_call(
        body, name="prenorm", grid=(s // tm,),
        in_specs=[row, pl.BlockSpec((1, D), lambda i: (0, 0))], out_specs=row,
        out_shape=jax.ShapeDtypeStruct((s, D), BF16),
        compiler_params=_cp(1))(x, g1)


def _inproj_fwd(h1, w_in):
    s = h1.shape[0]
    tm = min(TM, s)

    def body(h_ref, w_ref, u_ref, q_ref, k_ref, v_ref):
        proj = _dot_nt(h_ref[...], w_ref[...])
        u_ref[...] = proj[:, :512]
        q_ref[...] = proj[:, 512:1024].astype(BF16)
        k_ref[...] = proj[:, 1024:1152].astype(BF16)
        v_ref[...] = proj[:, 1152:1280].astype(BF16)

    row = lambda w: pl.BlockSpec((tm, w), lambda i: (i, 0))
    return pl.pallas_call(
        body, name="inproj_fwd", grid=(s // tm,),
        in_specs=[row(D), pl.BlockSpec((IN_W, D), lambda i: (0, 0))],
        out_specs=[row(512), row(512), row(128), row(128)],
        out_shape=[jax.ShapeDtypeStruct((s, 512), F32), jax.ShapeDtypeStruct((s, 512), BF16),
                   jax.ShapeDtypeStruct((s, 128), BF16), jax.ShapeDtypeStruct((s, 128), BF16)],
        compiler_params=_cp(1))(h1, w_in)


def _window_sums(ext, w, lag_sign, tm, terms):
    rows = lax.broadcasted_iota(jnp.int32, (HALO, 2 * HALO), 0)
    cols = lax.broadcasted_iota(jnp.int32, (HALO, 2 * HALO), 1)
    d = rows + HALO - cols if lag_sign > 0 else cols - rows
    band = jnp.where((d >= 0) & (d < w), 1.0, 0.0).astype(BF16)
    return jnp.concatenate([_band_dot(band, ext[r:r + 2 * HALO], terms) for r in range(0, tm, HALO)], axis=0)


def _pooled(ext, cur, t0, tm):
    t = t0 + lax.broadcasted_iota(jnp.int32, (tm, GROUP), 0)
    out = []
    for g, w in enumerate(WINDOWS):
        sl = slice(g * GROUP, (g + 1) * GROUP)
        cnt = jnp.minimum(t + 1, w).astype(F32)
        out.append(_window_sums(ext[:, sl], w, 1, tm, 2) / cnt - cur[:, sl])
    return out


def _pool_fwd(u, w_pool, pool_scale):
    s = u.shape[0]
    tm = min(TM_POOL, s)
    hb = tm // HALO

    def body(uc_ref, uh_ref, wp_ref, sc_ref, o_ref, pooled_ref):
        i = pl.program_id(0)
        cur = uc_ref[...]
        halo = jnp.where(i > 0, uh_ref[...], 0.0)
        ext = jnp.concatenate([halo, cur], axis=0)
        pooled = _pooled(ext, cur, i * tm, tm)
        for g in range(4):
            sl = slice(g * GROUP, (g + 1) * GROUP)
            pb = pooled[g].astype(BF16)
            pooled_ref[:, sl] = pb
            o_ref[:, sl] = (_dot(pb, wp_ref[g]) * sc_ref[:, sl]).astype(BF16)

    row = pl.BlockSpec((tm, 512), lambda i: (i, 0))
    return pl.pallas_call(
        body, name="pool_fwd", grid=(s // tm,),
        in_specs=[row, pl.BlockSpec((HALO, 512), lambda i: (jnp.maximum(i * hb - 1, 0), 0)),
                  pl.BlockSpec((4, GROUP, GROUP), lambda i: (0, 0, 0)),
                  pl.BlockSpec((1, 512), lambda i: (0, 0))],
        out_specs=[row, row],
        out_shape=[jax.ShapeDtypeStruct((s, 512), BF16)] * 2,
        compiler_params=_cp(1))(u, u, w_pool, pool_scale)


def _bias_table(bucket, rel_bias):
    def body(b_ref, rb_ref, o_ref):
        bucket_v = b_ref[...]
        key = lax.broadcasted_iota(jnp.int32, (2 * BLK, BLK), 0)
        dist = lax.broadcasted_iota(jnp.int32, (2 * BLK, BLK), 1) + BLK - key
        inwin = (dist >= 0) & (dist < BLK)
        for h in range(NQ):
            acc = jnp.zeros((2 * BLK, BLK), F32)
            for b in range(NBUCKET):
                acc = jnp.where(bucket_v == b, rb_ref[b, h], acc)
            rest = jnp.where(inwin, acc, NEG)
            o_ref[1, h] = rest
            o_ref[0, h] = jnp.where(key >= BLK, rest, NEG)

    return pl.pallas_call(
        body, name="bias_table",
        in_specs=[pl.BlockSpec(memory_space=pltpu.VMEM), pl.BlockSpec(memory_space=pltpu.SMEM)],
        out_specs=pl.BlockSpec(memory_space=pltpu.VMEM),
        out_shape=jax.ShapeDtypeStruct((2, NQ, 2 * BLK, BLK), F32),
        compiler_params=_cp())(bucket, rel_bias)


HD = 64


def _kv_band(ref, n, transposed):
    pstart = pl.multiple_of(jnp.maximum(n - 1, 0) * BLK, BLK)
    cstart = pl.multiple_of(n * BLK, BLK)
    lo = lax.broadcasted_iota(jnp.int32, (2 * BLK, 128), 1) < HD
    band = jnp.concatenate([ref[pl.ds(pstart, BLK), :], ref[pl.ds(cstart, BLK), :]], axis=0).astype(F32)
    rot = pltpu.roll(band, HD, axis=1)
    halves = ((jnp.where(lo, band, 0.0), jnp.where(lo, 0.0, rot)), (jnp.where(lo, rot, 0.0), jnp.where(lo, 0.0, band)))
    if transposed:
        out = [jnp.concatenate([a.T, b.T], axis=1).astype(BF16) for a, b in halves]
    else:
        out = [jnp.concatenate([a, b], axis=0).astype(BF16) for a, b in halves]
    return out, (lo, pstart, cstart)


def _pair_probs(qt, kk, bias, sk_ref, p):
    sink = jnp.concatenate([jnp.full((1, 1, BLK), sk_ref[0, 2 * p + e], F32) for e in range(2)], axis=0)
    s = _dot_nt(kk, qt).reshape(2, 2 * BLK, BLK) + bias
    m = jnp.maximum(jnp.max(s, axis=1, keepdims=True), sink)
    pr = jnp.exp(s - m)
    es = jnp.exp(sink - m)
    inv = 1.0 / (jnp.sum(pr, axis=1, keepdims=True) + es)
    return pr * inv, es * inv


def _attn_fwd(q, k, v, bias, sinks):
    s = q.shape[0]
    nblk = s // BLK

    def body(q_ref, k_ref, v_ref, b_ref, sk_ref, o_ref, prob_ref, ps_ref):
        n = pl.program_id(0)
        kk, _ = _kv_band(k_ref, n, False)
        vt, _ = _kv_band(v_ref, n, True)
        bidx = jnp.minimum(n, 1)
        for p in range(4):
            h = p // 2
            qt = (q_ref[:, p * 128:(p + 1) * 128].astype(F32) * SCALE).astype(BF16)
            prob, ps = _pair_probs(qt, kk[h], b_ref[bidx, pl.ds(2 * p, 2)], sk_ref, p)
            pb = prob.astype(BF16)
            prob_ref[pl.ds(2 * p, 2)] = pb
            ps_ref[pl.ds(2 * p, 2), :] = ps.reshape(2, BLK)
            acc_t = _dot(vt[h], pb.reshape(4 * BLK, BLK))
            o_ref[:, p * 128:(p + 1) * 128] = acc_t.T.astype(BF16)

    return pl.pallas_call(
        body, name="attn_fwd", grid=(nblk,),
        in_specs=[pl.BlockSpec((BLK, 512), lambda i: (i, 0)),
                  pl.BlockSpec((s, 128), lambda i: (0, 0)),
                  pl.BlockSpec((s, 128), lambda i: (0, 0)),
                  pl.BlockSpec((2, NQ, 2 * BLK, BLK), lambda i: (0, 0, 0, 0)),
                  pl.BlockSpec(memory_space=pltpu.SMEM)],
        out_specs=[pl.BlockSpec((BLK, 512), lambda i: (i, 0)),
                   pl.BlockSpec((None, NQ, 2 * BLK, BLK), lambda i: (i, 0, 0, 0)),
                   pl.BlockSpec((None, NQ, BLK), lambda i: (i, 0, 0))],
        out_shape=[jax.ShapeDtypeStruct((s, 512), BF16), jax.ShapeDtypeStruct((nblk, NQ, 2 * BLK, BLK), BF16),
                   jax.ShapeDtypeStruct((nblk, NQ, BLK), F32)],
        compiler_params=_cp(1))(q, k, v, bias, sinks)


def _outproj_fwd(pool_o, attn_o, w_out, x, g2, g3):
    s = x.shape[0]
    tm = min(TM, s)

    def body(p_ref, a_ref, w_ref, x_ref, g2_ref, g3_ref, mix_ref, x1_ref, h2_ref):
        mix = _dot(p_ref[...], w_ref[0:512, :]) + _dot(a_ref[...], w_ref[512:1024, :])
        mix_ref[...] = mix
        _, n2 = _rms(mix)
        x1 = x_ref[...] + n2 * g2_ref[...]
        x1_ref[...] = x1
        _, n3 = _rms(x1)
        h2_ref[...] = (n3 * g3_ref[...]).astype(BF16)

    row = lambda w: pl.BlockSpec((tm, w), lambda i: (i, 0))
    vec = pl.BlockSpec((1, D), lambda i: (0, 0))
    return pl.pallas_call(
        body, name="outproj_fwd", grid=(s // tm,),
        in_specs=[row(512), row(512), pl.BlockSpec((D, D), lambda i: (0, 0)), row(D), vec, vec],
        out_specs=[row(D), row(D), row(D)],
        out_shape=[jax.ShapeDtypeStruct((s, D), F32), jax.ShapeDtypeStruct((s, D), F32),
                   jax.ShapeDtypeStruct((s, D), BF16)],
        compiler_params=_cp(1))(pool_o, attn_o, w_out, x, g2, g3)


def _silu_parts(g):
    sg = jax.nn.sigmoid(g)
    return sg, g * sg


def _ffn_fwd(h2, wg, wu, wd, x1, target, g4):
    s = h2.shape[0]
    tm = min(TM_FFN_FWD, s)

    def body(h_ref, wg_ref, wu_ref, wd_ref, x1_ref, t_ref, g4_ref,
             gate_ref, up_ref, df_ref, dy_ref, loss_ref, gg4_ref, f_acc):
        i = pl.program_id(0)
        j = pl.program_id(1)
        first = j == 0
        for r in range(0, tm, FFN_ROWS):
            rows = pl.ds(r, min(FFN_ROWS, tm))
            h = h_ref[rows, :]
            gate = _dot_nt(h, wg_ref[...])
            up = _dot_nt(h, wu_ref[...])
            gate_ref[rows, :] = gate.astype(BF16)
            up_ref[rows, :] = up.astype(BF16)
            _, sl = _silu_parts(gate)
            contrib = _dot((sl * up).astype(BF16), wd_ref[...])
            f_acc[rows, :] = jnp.where(first, contrib, f_acc[rows, :] + contrib)

        @pl.when((i == 0) & (j == 0))
        def _():
            loss_ref[...] = jnp.zeros_like(loss_ref)
            gg4_ref[...] = jnp.zeros_like(gg4_ref)

        @pl.when(j == NSH - 1)
        def _():
            r4, n4 = _rms(f_acc[...])
            g4v = g4_ref[...]
            err = x1_ref[...] + n4 * g4v - t_ref[...]
            loss_ref[...] += (0.5 / D) * jnp.sum(err * err).reshape(1, 1)
            dy = err * (1.0 / D)
            dy_ref[...] = dy
            df, dg = _rms_bwd(r4, n4, g4v, dy)
            gg4_ref[...] += dg
            df_ref[...] = df.astype(BF16)

    row = lambda w: pl.BlockSpec((tm, w), lambda i, j: (i, 0))
    sh = lambda r, c: pl.BlockSpec((None, r, c), lambda i, j: (j, 0, 0))
    act = pl.BlockSpec((None, tm, FS), lambda i, j: (j, i, 0))
    vec = pl.BlockSpec((1, D), lambda i, j: (0, 0))
    return pl.pallas_call(
        body, name="ffn_fwd", grid=(s // tm, NSH),
        in_specs=[row(D), sh(FS, D), sh(FS, D), sh(FS, D), row(D), row(D), vec],
        out_specs=[act, act, row(D), row(D), pl.BlockSpec((1, 1), lambda i, j: (0, 0)), vec],
        out_shape=[jax.ShapeDtypeStruct((NSH, s, FS), BF16), jax.ShapeDtypeStruct((NSH, s, FS), BF16),
                   jax.ShapeDtypeStruct((s, D), BF16), jax.ShapeDtypeStruct((s, D), F32),
                   jax.ShapeDtypeStruct((1, 1), F32), jax.ShapeDtypeStruct((1, D), F32)],
        scratch_shapes=[pltpu.VMEM((tm, D), F32)],
        compiler_params=_cp(2))(h2, wg, wu, wd, x1, target, g4)


def _ffn_bwd_act(df, gate, up, wg, wu, wd, dy, x1, mix, g3, g2):
    s = df.shape[0]
    tm = min(TM_FFN, s)

    def body(df_ref, gate_ref, up_ref, wg_ref, wu_ref, wd_ref, dy_ref, x1_ref, mix_ref, g3_ref, g2_ref,
             dgate_ref, dup_ref, dx1_ref, dmix_ref, gg3_ref, gg2_ref, acc):
        i = pl.program_id(0)
        j = pl.program_id(1)
        first = j == 0
        for r in range(0, tm, FFN_ROWS):
            rows = pl.ds(r, min(FFN_ROWS, tm))
            da = _dot_nt(df_ref[rows, :], wd_ref[...])
            g = gate_ref[rows, :].astype(F32)
            u = up_ref[rows, :].astype(F32)
            sg, sl = _silu_parts(g)
            dgate = (da * u * (sg * (1.0 + g * (1.0 - sg)))).astype(BF16)
            dup = (da * sl).astype(BF16)
            dgate_ref[rows, :] = dgate
            dup_ref[rows, :] = dup
            contrib = _dot(dgate, wg_ref[...]) + _dot(dup, wu_ref[...])
            acc[rows, :] = jnp.where(first, contrib, acc[rows, :] + contrib)

        @pl.when((i == 0) & (j == 0))
        def _():
            gg3_ref[...] = jnp.zeros_like(gg3_ref)
            gg2_ref[...] = jnp.zeros_like(gg2_ref)

        @pl.when(j == NSH - 1)
        def _():
            r3, n3 = _rms(x1_ref[...])
            dx1n, dg3 = _rms_bwd(r3, n3, g3_ref[...], acc[...])
            dx1 = dy_ref[...] + dx1n
            dx1_ref[...] = dx1
            gg3_ref[...] += dg3
            r2, n2 = _rms(mix_ref[...])
            dmix, dg2 = _rms_bwd(r2, n2, g2_ref[...], dx1)
            gg2_ref[...] += dg2
            dmix_ref[...] = dmix.astype(BF16)

    row = lambda w: pl.BlockSpec((tm, w), lambda i, j: (i, 0))
    sh = lambda r, c: pl.BlockSpec((None, r, c), lambda i, j: (j, 0, 0))
    act = pl.BlockSpec((None, tm, FS), lambda i, j: (j, i, 0))
    vec = pl.BlockSpec((1, D), lambda i, j: (0, 0))
    return pl.pallas_call(
        body, name="ffn_bwd_act", grid=(s // tm, NSH),
        in_specs=[row(D), act, act, sh(FS, D), sh(FS, D), sh(FS, D), row(D), row(D), row(D), vec, vec],
        out_specs=[act, act, row(D), row(D), vec, vec],
        out_shape=[jax.ShapeDtypeStruct((NSH, s, FS), BF16), jax.ShapeDtypeStruct((NSH, s, FS), BF16),
                   jax.ShapeDtypeStruct((s, D), F32), jax.ShapeDtypeStruct((s, D), BF16),
                   jax.ShapeDtypeStruct((1, D), F32), jax.ShapeDtypeStruct((1, D), F32)],
        scratch_shapes=[pltpu.VMEM((tm, D), F32)],
        compiler_params=_cp(2))(df, gate, up, wg, wu, wd, dy, x1, mix, g3, g2)


SMEM_SPEC = pl.BlockSpec(memory_space=pltpu.SMEM)


def _emit_halves(acc_ref, row0, rows, c, own_ref, oth_ref):
    half = rows // 2
    own_ref[...] = acc_ref[pl.ds(row0 + pl.multiple_of(c * half, 8), half), :]
    oth_ref[...] = acc_ref[pl.ds(row0 + pl.multiple_of((1 - c) * half, 8), half), :].astype(BF16)


def _half_shapes(rows):
    return [jax.ShapeDtypeStruct((NSH, rows // 2, D), F32), jax.ShapeDtypeStruct((NSH, rows // 2, D), BF16)]


def _ffn_bwd_w(h2, gate, up, dgate, dup, df, c_arr):
    s = h2.shape[0]
    tm = min(TM_FFN_FWD, s)
    nt = s // tm

    def body(c_ref, h_ref, gate_ref, up_ref, dgate_ref, dup_ref, df_ref, *rest):
        outs, accs = rest[:6], rest[6:]
        i = pl.program_id(1)
        @pl.when(i == 0)
        def _():
            for acc in accs:
                acc[...] = jnp.zeros_like(acc)

        h = h_ref[...]
        accs[0][...] += _dot_tn(dgate_ref[...], h)
        accs[1][...] += _dot_tn(dup_ref[...], h)
        _, sl = _silu_parts(gate_ref[...].astype(F32))
        a = (sl * up_ref[...].astype(F32)).astype(BF16)
        accs[2][...] += _dot_tn(a, df_ref[...])

        @pl.when(i == nt - 1)
        def _():
            for t, acc in enumerate(accs):
                _emit_halves(acc, 0, FS, c_ref[0], outs[2 * t], outs[2 * t + 1])

    row = lambda w: pl.BlockSpec((tm, w), lambda j, i: (i, 0))
    act = pl.BlockSpec((None, tm, FS), lambda j, i: (j, i, 0))
    half = pl.BlockSpec((None, FS // 2, D), lambda j, i: (j, 0, 0))
    return pl.pallas_call(
        body, name="ffn_bwd_w", grid=(NSH, nt),
        in_specs=[SMEM_SPEC, row(D), act, act, act, act, row(D)],
        out_specs=[half] * 6,
        out_shape=_half_shapes(FS) * 3,
        scratch_shapes=[pltpu.VMEM((FS, D), F32)] * 3,
        compiler_params=_cp(2))(c_arr, h2, gate, up, dgate, dup, df)


def _outproj_bwd(dmix, w_out, pool_o, attn_o, c_arr, dep=None):
    s = dmix.shape[0]
    tm = min(TM, s)
    nt = s // tm

    def body(c_ref, dm_ref, w_ref, p_ref, a_ref, *rest):
        dpool_ref, dattn_ref, own_ref, oth_ref, gw_ref = rest[-5:]
        i = pl.program_id(0)

        @pl.when(i == 0)
        def _():
            gw_ref[...] = jnp.zeros_like(gw_ref)

        dm = dm_ref[...]
        dpool_ref[...] = _dot_nt(dm, w_ref[0:512, :])
        dattn_ref[...] = _dot_nt(dm, w_ref[512:1024, :]).astype(BF16)
        gw_ref[0:512, :] += _dot_tn(p_ref[...], dm)
        gw_ref[512:1024, :] += _dot_tn(a_ref[...], dm)

        @pl.when(i == nt - 1)
        def _():
            for k in range(NSH):
                _emit_halves(gw_ref, k * WOUT_S, WOUT_S, c_ref[0], own_ref.at[k], oth_ref.at[k])

    row = lambda w: pl.BlockSpec((tm, w), lambda i: (i, 0))
    full = pl.BlockSpec((D, D), lambda i: (0, 0))
    half = pl.BlockSpec((NSH, WOUT_S // 2, D), lambda i: (0, 0, 0))
    return pl.pallas_call(
        body, name="outproj_bwd", grid=(nt,),
        in_specs=[SMEM_SPEC, row(D), full, row(512), row(512)] + ([] if dep is None else [ANY]),
        out_specs=[row(512), row(512), half, half],
        out_shape=[jax.ShapeDtypeStruct((s, 512), F32), jax.ShapeDtypeStruct((s, 512), BF16)] + _half_shapes(WOUT_S),
        scratch_shapes=[pltpu.VMEM((D, D), F32)],
        compiler_params=_cp(1))(c_arr, dmix, w_out, pool_o, attn_o, *([] if dep is None else [dep]))


def _pool_bwd(pooled, dpool, w_pool, pool_scale, dep=None):
    s = pooled.shape[0]
    tm = min(TM_POOL, s)
    hb = tm // HALO
    nt = s // tm
    last_halo = s // HALO - 1

    def body(p_ref, dc_ref, dn_ref, wp_ref, sc_ref, *rest):
        du_ref, gwp_ref, gsc_ref = rest[-3:]
        i = pl.program_id(0)

        @pl.when(i == 0)
        def _():
            gwp_ref[...] = jnp.zeros_like(gwp_ref)
            gsc_ref[...] = jnp.zeros_like(gsc_ref)

        dcur = dc_ref[...]
        dnext = jnp.where(i < nt - 1, dn_ref[...], 0.0)
        dext = jnp.concatenate([dcur, dnext], axis=0)
        t_ext = i * tm + lax.broadcasted_iota(jnp.int32, (tm + HALO, GROUP), 0)
        for g, w in enumerate(WINDOWS):
            sl = slice(g * GROUP, (g + 1) * GROUP)
            pb = p_ref[:, sl]
            wp = wp_ref[g]
            mixed = _dot(pb, wp)
            gsc_ref[:, sl] += jnp.sum(dcur[:, sl] * mixed, axis=0, keepdims=True)
            dmixed = (dext[:, sl] * sc_ref[:, sl]).astype(BF16)
            gwp_ref[g] += _dot_tn(pb, dmixed[0:tm])
            dpooled = _dot_nt(dmixed, wp)
            z = dpooled / jnp.minimum(t_ext + 1, w).astype(F32)
            du_ref[:, sl] = (_window_sums(z, w, -1, tm, 2) - dpooled[0:tm]).astype(BF16)

    return pl.pallas_call(
        body, name="pool_bwd", grid=(nt,),
        in_specs=[pl.BlockSpec((tm, 512), lambda i: (i, 0)),
                  pl.BlockSpec((tm, 512), lambda i: (i, 0)),
                  pl.BlockSpec((HALO, 512), lambda i: (jnp.minimum((i + 1) * hb, last_halo), 0)),
                  pl.BlockSpec((4, GROUP, GROUP), lambda i: (0, 0, 0)),
                  pl.BlockSpec((1, 512), lambda i: (0, 0))] + ([] if dep is None else [ANY]),
        out_specs=[pl.BlockSpec((tm, 512), lambda i: (i, 0)),
                   pl.BlockSpec((4, GROUP, GROUP), lambda i: (0, 0, 0)),
                   pl.BlockSpec((1, 512), lambda i: (0, 0))],
        out_shape=[jax.ShapeDtypeStruct((s, 512), BF16), jax.ShapeDtypeStruct((4, GROUP, GROUP), F32),
                   jax.ShapeDtypeStruct((1, 512), F32)],
        compiler_params=_cp(1))(pooled, dpool, dpool, w_pool, pool_scale, *([] if dep is None else [dep]))


def _attn_bwd(q, k, v, do, probs, sink_share, bucket, dep=None):
    s = q.shape[0]
    nblk = s // BLK

    def body(q_ref, do_ref, k_ref, v_ref, prob_ref, ps_ref, bk_ref, *rest):
        dq_ref, dk_ref, dv_ref, grb_ref, gsk_ref, dss_ref = rest[-6:]
        n = pl.program_id(0)

        @pl.when(n == 0)
        def _():
            dk_ref[...] = jnp.zeros_like(dk_ref)
            dv_ref[...] = jnp.zeros_like(dv_ref)
            dss_ref[...] = jnp.zeros_like(dss_ref)
            gsk_ref[...] = jnp.zeros_like(gsk_ref)

        kt, (lo, pstart, cstart) = _kv_band(k_ref, n, True)
        vv, _ = _kv_band(v_ref, n, False)
        lo_q = lax.broadcasted_iota(jnp.int32, (BLK, 128), 1) < HD
        dkk = [jnp.zeros((2 * BLK, 128), F32), jnp.zeros((2 * BLK, 128), F32)]
        dvv = [jnp.zeros((2 * BLK, 128), F32), jnp.zeros((2 * BLK, 128), F32)]

        def by_head(t):
            return jnp.concatenate([jnp.where(lo_q, t, 0.0), jnp.where(lo_q, 0.0, t)], axis=0).astype(BF16)

        for p in range(4):
            h = p // 2
            qt = q_ref[:, p * 128:(p + 1) * 128].astype(F32) * SCALE
            dot = do_ref[:, p * 128:(p + 1) * 128]
            pb = prob_ref[pl.ds(2 * p, 2)]
            prob = pb.astype(F32)
            ps = ps_ref[pl.ds(2 * p, 2), :].reshape(2, 1, BLK)
            dp = _dot_nt(vv[h], dot).reshape(2, 2 * BLK, BLK)
            delta = jnp.sum(prob * dp, axis=1, keepdims=True)
            ds = prob * (dp - delta)
            sink_part = ps * delta
            for e in range(2):
                gs = -jnp.sum(sink_part[e]).reshape(1, 1)
                gsk_ref[pl.ds(2 * p + e, 1), :] += jnp.broadcast_to(gs, (1, 128))
            dss_ref[pl.ds(2 * p, 2)] += ds
            dsb = ds.astype(BF16)
            dq_t = _dot(kt[h], dsb.reshape(4 * BLK, BLK))
            dq_ref[:, p * 128:(p + 1) * 128] = (dq_t.T * SCALE).astype(BF16)
            dkk[h] = dkk[h] + _dot(jnp.concatenate([dsb[0], dsb[1]], axis=1), by_head(qt))
            dvv[h] = dvv[h] + _dot(jnp.concatenate([pb[0], pb[1]], axis=1), by_head(dot.astype(F32)))
        for acc, ref in ((dkk, dk_ref), (dvv, dv_ref)):
            f0 = acc[0] + pltpu.roll(acc[0], HD, axis=1)
            f1 = acc[1] + pltpu.roll(acc[1], HD, axis=1)
            band = jnp.where(lo, f0, f1)
            ref[pl.ds(pstart, BLK), :] += band[0:BLK]
            ref[pl.ds(cstart, BLK), :] += band[BLK:2 * BLK]

        @pl.when(n == nblk - 1)
        def _():
            _relbias_grad(dss_ref, bk_ref[...], grb_ref)

    blk = pl.BlockSpec((BLK, 512), lambda i: (i, 0))
    kv = pl.BlockSpec((s, 128), lambda i: (0, 0))
    row8 = pl.BlockSpec((NQ, 128), lambda i: (0, 0))
    return pl.pallas_call(
        body, name="attn_bwd", grid=(nblk,),
        in_specs=[blk, blk, kv, kv, pl.BlockSpec((None, NQ, 2 * BLK, BLK), lambda i: (i, 0, 0, 0)),
                  pl.BlockSpec((None, NQ, BLK), lambda i: (i, 0, 0)), pl.BlockSpec((2 * BLK, BLK), lambda i: (0, 0))]
        + ([] if dep is None else [ANY]),
        out_specs=[blk, kv, kv, row8, row8],
        out_shape=[jax.ShapeDtypeStruct((s, 512), BF16), jax.ShapeDtypeStruct((s, 128), F32),
                   jax.ShapeDtypeStruct((s, 128), F32), jax.ShapeDtypeStruct((NQ, 128), F32),
                   jax.ShapeDtypeStruct((NQ, 128), F32)],
        scratch_shapes=[pltpu.VMEM((NQ, 2 * BLK, BLK), F32)],
        compiler_params=_cp(1))(q, do, k, v, probs, sink_share, bucket, *([] if dep is None else [dep]))


def _relbias_grad(ds_ref, bucket_v, o_ref):
    lane = lax.broadcasted_iota(jnp.int32, (NQ, 128), 1)
    head_row = lax.broadcasted_iota(jnp.int32, (NQ, BLK), 0)
    acc = jnp.zeros((NQ, 128), F32)
    for b in range(NBUCKET):
        sel = bucket_v == b
        stack = jnp.zeros((NQ, BLK), F32)
        for h in range(NQ):
            col_sums = jnp.sum(jnp.where(sel, ds_ref[h], 0.0), axis=0, keepdims=True)
            stack = jnp.where(head_row == h, col_sums, stack)
        acc = acc + jnp.where(lane == b, jnp.sum(stack, axis=1, keepdims=True), 0.0)
    o_ref[...] = acc


def _inproj_bwd(x, g1, du, dq, dk, dv, w_in, dx1, c_arr):
    s = x.shape[0]
    tm = min(TM, s)
    nt = s // tm
    cols = ((0, 512), (512, 1024), (1024, 1152), (1152, 1280))

    def body(c_ref, x_ref, g_ref, du_ref, dq_ref, dk_ref, dv_ref, w_ref, dx1_ref,
             gx_ref, own_ref, oth_ref, gg_ref, gw_ref):
        i = pl.program_id(0)

        @pl.when(i == 0)
        def _():
            gw_ref[...] = jnp.zeros_like(gw_ref)
            gg_ref[...] = jnp.zeros_like(gg_ref)

        gv = g_ref[...]
        r1, n1 = _rms(x_ref[...])
        h = (n1 * gv).astype(BF16)
        parts = (du_ref[...], dq_ref[...], dk_ref[...].astype(BF16), dv_ref[...].astype(BF16))
        dh = None
        for (a, b), dpart in zip(cols, parts):
            t = _dot(dpart, w_ref[a:b, :])
            dh = t if dh is None else dh + t
            gw_ref[a:b, :] += _dot_tn(dpart, h)
        dx, dg = _rms_bwd(r1, n1, gv, dh)
        gg_ref[...] += dg
        gx_ref[...] = dx1_ref[...] + dx

        @pl.when(i == nt - 1)
        def _():
            for k in range(NSH):
                _emit_halves(gw_ref, k * WIN_S, WIN_S, c_ref[0], own_ref.at[k], oth_ref.at[k])

    row = lambda w: pl.BlockSpec((tm, w), lambda i: (i, 0))
    vec = pl.BlockSpec((1, D), lambda i: (0, 0))
    full = pl.BlockSpec((IN_W, D), lambda i: (0, 0))
    half = pl.BlockSpec((NSH, WIN_S // 2, D), lambda i: (0, 0, 0))
    return pl.pallas_call(
        body, name="inproj_bwd", grid=(nt,),
        in_specs=[SMEM_SPEC, row(D), vec, row(512), row(512), row(128), row(128), full, row(D)],
        out_specs=[row(D), half, half, vec],
        out_shape=[jax.ShapeDtypeStruct((s, D), F32)] + _half_shapes(WIN_S) + [jax.ShapeDtypeStruct((1, D), F32)],
        scratch_shapes=[pltpu.VMEM((IN_W, D), F32)],
        compiler_params=_cp(1))(c_arr, x, g1, du, dq, dk, dv, w_in, dx1)


def _local_step(x, target, small, w_in, w_out, ffn_weights, c_arr, on_ffn_grads=None, on_wout_grads=None):
    g1, g2, g3, g4, w_pool, pool_scale, rel_bias, sinks = small
    bucket = jnp.asarray(_bucket_table())
    wp_bf = w_pool.astype(BF16)
    bias = _bias_table(bucket, rel_bias)
    h1 = _prenorm(x, g1)
    if callable(w_in):
        w_in = w_in(bias, h1)
    u, q, k, v = _inproj_fwd(h1, w_in)
    pool_o, pooled = _pool_fwd(u, wp_bf, pool_scale)
    attn_o, probs, sink_share = _attn_fwd(q, k, v, bias, sinks)
    g2_fwd = g2
    if callable(w_out):
        w_out, after = w_out(pool_o, attn_o)
        if after is not None:
            g2_fwd = g2 + after[:1, :1]
    mix, x1, h2 = _outproj_fwd(pool_o, attn_o, w_out, x, g2_fwd, g3)
    wg, wu, wd = ffn_weights(h2) if callable(ffn_weights) else ffn_weights
    gate, up, df, dy, loss, gg4 = _ffn_fwd(h2, wg, wu, wd, x1, target, g4)
    dgate, dup, dx1, dmix, gg3, gg2 = _ffn_bwd_act(df, gate, up, wg, wu, wd, dy, x1, mix, g3, g2)
    ffn_g = _ffn_bwd_w(h2, gate, up, dgate, dup, df, c_arr)
    dep = None if on_ffn_grads is None else on_ffn_grads(ffn_g)
    dpool, dattn, wout_own, wout_oth = _outproj_bwd(dmix, w_out, pool_o, attn_o, c_arr, dep)
    dep = None if on_wout_grads is None else on_wout_grads(wout_own, wout_oth, dpool)
    du, gwp, gsc = _pool_bwd(pooled, dpool, wp_bf, pool_scale, dep)
    dq, dk, dv, grb, gsk = _attn_bwd(q, k, v, dattn, probs, sink_share, bucket, dep)
    gx, win_own, win_oth, gg1 = _inproj_bwd(x, g1, du, dq, dk, dv, w_in, dx1, c_arr)
    small_grads = (gg1, gg2, gg3, gg4, gwp, gsc, grb, gsk[:, 0].reshape(1, NQ))
    return loss, gx, small_grads, ((win_own, win_oth), (wout_own, wout_oth), ffn_g)


def _place():
    x, y, c = lax.axis_index("x"), lax.axis_index("y"), lax.axis_index("c")
    chips = ((1 - x, y), (x, 1 - y), (1 - x, 1 - y))
    return x, y, c, chips


def _remote(src, dst, ssem, rsem, dev):
    return pltpu.make_async_remote_copy(src_ref=src, dst_ref=dst, send_sem=ssem, recv_sem=rsem,
                                        device_id=dev, device_id_type=MESH_T)


def _to_sibling(arrs, name, after=()):
    nt, na = len(arrs), len(after)

    def body(*refs):
        srcs, dsts = refs[:nt], refs[nt + na:2 * nt + na]
        ssem, rsem = refs[2 * nt + na:]
        x, y, c, _ = _place()
        cps = [_remote(srcs[t], dsts[t], ssem.at[t], rsem.at[t], (x, y, 1 - c)) for t in range(nt)]
        for cp in cps:
            cp.start()
        for cp in cps:
            cp.wait()

    return pl.pallas_call(
        body, name=name, in_specs=[ANY] * (nt + na), out_specs=[ANY] * nt,
        out_shape=[jax.ShapeDtypeStruct(a.shape, a.dtype) for a in arrs],
        scratch_shapes=[pltpu.SemaphoreType.DMA((nt,)), pltpu.SemaphoreType.DMA((nt,))],
        compiler_params=_cp())(*arrs, *after)


HBM_SPEC = pl.BlockSpec(memory_space=pltpu.HBM)
SEM_SPEC = pl.BlockSpec(memory_space=pltpu.SEMAPHORE)
DATAFLOW = pltpu.SideEffectType.DATAFLOW_SIDE_EFFECTING


def _gather_copies(srcs, lands, ssem, rsem):
    x, y, c, chips = _place()
    me = 2 * x + y
    out = []
    for t in range(len(srcs)):
        half = srcs[t].shape[0] // 2
        mine = pl.ds(pl.multiple_of(c * half, 16), half)
        for j, (px, py) in enumerate(chips):
            k = 3 * t + j
            send = _remote(srcs[t].at[mine], lands[t].at[me, mine], ssem.at[k], rsem.at[k], (px, py, c))
            blk = lands[t].at[2 * px + py, mine]
            out.append((send, _remote(blk, blk, ssem.at[k], rsem.at[k], (px, py, c))))
    return out


def _scatter_copies(srcs, lands, ssem, rsem):
    x, y, c, chips = _place()
    out = []
    for t in range(len(srcs)):
        for j, (px, py) in enumerate(chips):
            k = 3 * t + j
            send = _remote(srcs[t].at[2 * px + py], lands[t].at[j], ssem.at[k], rsem.at[k], (px, py, c))
            out.append((send, _remote(lands[t].at[j], lands[t].at[j], ssem.at[k], rsem.at[k], (px, py, c))))
    return out


def _sibling_copies(srcs, lands, ssem, rsem):
    x, y, c, _ = _place()
    sib = (x, y, 1 - c)
    return [(_remote(srcs[t], lands[t], ssem.at[t], rsem.at[t], sib),
             _remote(lands[t], lands[t], ssem.at[t], rsem.at[t], sib)) for t in range(len(srcs))]


def _peer_copies(srcs, lands, ssem, rsem):
    x, y, c, _ = _place()
    me = 4 * x + 2 * y + c
    out = []
    for kk in range(1, 8):
        px, py, pc = x ^ (kk >> 2), y ^ ((kk >> 1) & 1), c ^ (kk & 1)
        send = _remote(srcs[0], lands[0].at[me], ssem.at[kk - 1], rsem.at[kk - 1], (px, py, pc))
        slot = lands[0].at[4 * px + 2 * py + pc]
        out.append((send, _remote(slot, slot, ssem.at[kk - 1], rsem.at[kk - 1], (px, py, pc))))
    return out


def _split_start(plan, ncopy, srcs, lands, after, name):
    ns, nbuf = len(srcs), len(srcs) + len(lands)
    extra = [] if after is None else [after]
    n_in = nbuf + len(extra)

    def body(*refs):
        ssem, rsem, token = refs[n_in], refs[n_in + 1], refs[-1]
        for send, _ in plan(refs[:ns], refs[ns:nbuf], ssem, rsem):
            send.start()
        token[...] = jnp.zeros_like(token)

    bufs = [pltpu.with_memory_space_constraint(a, pltpu.HBM) for a in list(srcs) + list(lands)]
    outs = pl.pallas_call(
        body, name=name, in_specs=[HBM_SPEC] * nbuf + [ANY] * len(extra),
        out_specs=[SEM_SPEC, SEM_SPEC] + [HBM_SPEC] * nbuf + [pl.BlockSpec(memory_space=pltpu.VMEM)],
        out_shape=[pltpu.SemaphoreType.DMA((ncopy,)), pltpu.SemaphoreType.DMA((ncopy,))]
        + [pltpu.HBM(a.shape, a.dtype) for a in bufs] + [jax.ShapeDtypeStruct((8, 128), F32)],
        input_output_aliases={i: 2 + i for i in range(nbuf)},
        compiler_params=pltpu.CompilerParams(has_side_effects=DATAFLOW))(*bufs, *extra)
    return outs[0], outs[1], outs[2:2 + ns], outs[2 + ns:2 + nbuf], outs[-1]


def _split_wait(plan, ssem, rsem, srcs, lands, after, name, only=None):
    ns, nbuf = len(srcs), len(srcs) + len(lands)

    def body(*refs):
        s_ref, r_ref = refs[nbuf], refs[nbuf + 1]
        for k, (send, recv) in enumerate(plan(refs[:ns], refs[ns:nbuf], s_ref, r_ref)):
            if only is None or k in only:
                send.wait_send()
                recv.wait_recv()

    outs = pl.pallas_call(
        body, name=name, in_specs=[HBM_SPEC] * nbuf + [SEM_SPEC, SEM_SPEC] + [ANY] * len(after),
        out_specs=[HBM_SPEC] * nbuf,
        out_shape=[pltpu.HBM(a.shape, a.dtype) for a in list(srcs) + list(lands)],
        input_output_aliases={i: i for i in range(nbuf)},
        compiler_params=pltpu.CompilerParams(has_side_effects=DATAFLOW))(*srcs, *lands, ssem, rsem, *after)
    return outs


def _gather_finish(lands, tag):
    nt = len(lands)

    def body(*refs):
        outs = refs[nt:2 * nt]
        dsend, drecv = refs[2 * nt:]
        x, y, c, chips = _place()
        sib = (x, y, 1 - c)
        sends = []
        for t in range(nt):
            half = outs[t].shape[1] // 2
            mine = pl.ds(pl.multiple_of(c * half, 16), half)
            for j, (px, py) in enumerate(chips):
                blk = outs[t].at[2 * px + py, mine]
                cp = _remote(blk, blk, dsend.at[t, j], drecv.at[t, j], sib)
                cp.start()
                sends.append(cp)
        for t in range(nt):
            half = outs[t].shape[1] // 2
            other = pl.ds(pl.multiple_of((1 - c) * half, 16), half)
            for j, (px, py) in enumerate(chips):
                blk = outs[t].at[2 * px + py, other]
                _remote(blk, blk, dsend.at[t, j], drecv.at[t, j], sib).wait_recv()
        for cp in sends:
            cp.wait_send()

    return pl.pallas_call(
        body, name="gather_finish_" + tag, in_specs=[ANY] * nt, out_specs=[ANY] * nt,
        out_shape=[jax.ShapeDtypeStruct(a.shape, a.dtype) for a in lands],
        input_output_aliases={t: t for t in range(nt)},
        scratch_shapes=[pltpu.SemaphoreType.DMA((nt, 3)), pltpu.SemaphoreType.DMA((nt, 3))],
        compiler_params=_cp())(*lands)


def _chip_partial(owns, recv, chip_arr, tag):
    nt = len(owns)

    def body(chip_ref, *refs):
        k = pl.program_id(0)
        for t in range(nt):
            p = refs[t][...] + refs[nt + t][...].astype(F32)
            refs[2 * nt + t][...] = p.astype(BF16)

            @pl.when(k == chip_ref[0])
            def _():
                refs[3 * nt + t][...] = p

    blk_specs = [pl.BlockSpec((None,) + a.shape[1:], lambda k, chip_ref: (k, 0, 0)) for a in owns]
    own_specs = [pl.BlockSpec(a.shape[1:], lambda k, chip_ref: (0, 0)) for a in owns]
    return pl.pallas_call(
        body, name="rs_chip_partial_" + tag,
        grid_spec=pltpu.PrefetchScalarGridSpec(num_scalar_prefetch=1, grid=(NSH,), in_specs=blk_specs * 2,
                                               out_specs=blk_specs + own_specs),
        out_shape=[jax.ShapeDtypeStruct(a.shape, BF16) for a in owns]
        + [jax.ShapeDtypeStruct(a.shape[1:], F32) for a in owns],
        compiler_params=_cp(1))(chip_arr, *owns, *recv)


def _sum_chips(own, recv, tag, after=()):
    nt = len(own)

    def body(*refs):
        outs = refs[2 * nt + len(after):]
        for t in range(nt):
            r = refs[nt + t]
            outs[t][...] = ((refs[t][...] + r[0].astype(F32)) + r[1].astype(F32)) + r[2].astype(F32)

    vm = pl.BlockSpec(memory_space=pltpu.VMEM)
    return pl.pallas_call(
        body, name="rs_sum_chips_" + tag, in_specs=[vm] * (2 * nt) + [ANY] * len(after), out_specs=[vm] * nt,
        out_shape=[jax.ShapeDtypeStruct(a.shape, F32) for a in own],
        compiler_params=_cp())(*own, *recv, *after)


def _adamw_math(w, g, m, v):
    m = ADAM_B1 * m + (1.0 - ADAM_B1) * g
    v = ADAM_B2 * v + (1.0 - ADAM_B2) * (g * g)
    m_hat = m / (1.0 - ADAM_B1 ** ADAM_STEP)
    v_hat = v / (1.0 - ADAM_B2 ** ADAM_STEP)
    delta = -ADAM_LR * (m_hat / (jnp.sqrt(v_hat) + ADAM_EPS) + ADAM_WD * w)
    return delta, m, v


ADAM_SPLIT = 4


def _adamw_shards(own, sib, ws, ms, vs, c_arr, tag):
    nt = len(own)

    def body(c_ref, *refs):
        hh = pl.program_id(0)
        mine = hh == c_ref[0]
        for t in range(nt):
            g = jnp.where(mine, refs[t][...], refs[nt + t][...])
            delta, m, v = _adamw_math(refs[2 * nt + t][...], g, refs[3 * nt + t][...], refs[4 * nt + t][...])
            refs[5 * nt + t][...] = g
            refs[6 * nt + t][...] = delta
            refs[7 * nt + t][...] = m
            refs[8 * nt + t][...] = v

    def tile(a):
        return (a.shape[0] // ADAM_SPLIT, a.shape[1])

    half_specs = [pl.BlockSpec(tile(a), lambda hh, q, c_ref: (q, 0)) for a in own]
    full_specs = [pl.BlockSpec(tile(a), lambda hh, q, c_ref: (hh * ADAM_SPLIT + q, 0)) for a in own]
    return pl.pallas_call(
        body, name="adamw_shards_" + tag,
        grid_spec=pltpu.PrefetchScalarGridSpec(num_scalar_prefetch=1, grid=(2, ADAM_SPLIT),
                                               in_specs=half_specs * 2 + full_specs * 3, out_specs=full_specs * 4),
        out_shape=[jax.ShapeDtypeStruct(w.shape, F32) for w in ws] * 4,
        compiler_params=_cp(2))(c_arr, *own, *sib, *ws, *ms, *vs)


SMALL_WPOOL_ROW = 32
SMALL_SCALE_ROW = SMALL_WPOOL_ROW + 4 * GROUP
SMALL_BIAS_ROW = SMALL_SCALE_ROW + 8
SMALL_SINKS_ROW = SMALL_BIAS_ROW + NQ
SMALL_LOSS_ROW = SMALL_SINKS_ROW + 8


def _small_sum_adamw(gathered, wp, mp, vp):
    rows = wp.shape[0]

    def body(g_ref, w_ref, m_ref, v_ref, *rest):
        outs, res = rest[:-1], rest[-1]
        g = g_ref[0]
        for d in range(1, 8):
            g = g + g_ref[d]
        delta, m, v = _adamw_math(w_ref[...], g, m_ref[...], v_ref[...])
        for kind, val in enumerate((g, delta, m, v)):
            res[kind] = val
            gains = outs[8 * kind:8 * kind + 4]
            w_pool_o, scale_o, bias_o, sinks_o = outs[8 * kind + 4:8 * kind + 8]
            for t in range(4):
                for i in range(8):
                    gains[t][:, 128 * i:128 * (i + 1)] = res[kind, pl.ds(8 * t + i, 1), :]
            for grp in range(4):
                w_pool_o[grp] = res[kind, pl.ds(SMALL_WPOOL_ROW + GROUP * grp, GROUP), :]
            for i in range(4):
                scale_o[:, 128 * i:128 * (i + 1)] = res[kind, pl.ds(SMALL_SCALE_ROW + i, 1), :]
            bias_o[...] = res[kind, pl.ds(SMALL_BIAS_ROW, NQ), :][:, 0:NBUCKET]
            sinks_o[...] = res[kind, pl.ds(SMALL_SINKS_ROW, 1), :][:, 0:NQ]
        outs[-1][...] = res[0, pl.ds(SMALL_LOSS_ROW, 1), :]

    vm = pl.BlockSpec(memory_space=pltpu.VMEM)
    one_kind = [jax.ShapeDtypeStruct((1, D), F32)] * 4 + [
        jax.ShapeDtypeStruct((4, GROUP, GROUP), F32), jax.ShapeDtypeStruct((1, POOL_W), F32),
        jax.ShapeDtypeStruct((NQ, NBUCKET), F32), jax.ShapeDtypeStruct((1, NQ), F32)]
    return pl.pallas_call(
        body, name="small_sum_adamw", in_specs=[vm] * 4, out_specs=[vm] * 33,
        out_shape=one_kind * 4 + [jax.ShapeDtypeStruct((1, 128), F32)],
        scratch_shapes=[pltpu.VMEM((4, rows, 128), F32)],
        compiler_params=_cp())(gathered, wp, mp, vp)


def _pack_small(gains4, w_pool, pool_scale, rel_bias_t, sinks, loss):
    bias_rows = jnp.pad(rel_bias_t, ((0, 0), (0, 128 - rel_bias_t.shape[1])))
    parts = list(gains4) + [w_pool, pool_scale, bias_rows, sinks, loss]
    rows = []
    for a in parts:
        flat = a.reshape(-1)
        n = flat.shape[0]
        padded = -(-n // 1024) * 1024
        rows.append(jnp.pad(flat, (0, padded - n)).reshape(-1, 128))
    return jnp.concatenate(rows, axis=0)


def kernel(x, g_pre_mix, w_in, w_pool, pool_scale, rel_bias, sinks, w_out, g_post_mix, g_pre_ffn, w_gate, w_up, w_down, g_post_ffn, loss_target, m_g_pre_mix, m_w_in, m_w_pool, m_pool_scale, m_rel_bias, m_sinks, m_w_out, m_g_post_mix, m_g_pre_ffn, m_w_gate, m_w_up, m_w_down, m_g_post_ffn, v_g_pre_mix, v_w_in, v_w_pool, v_pool_scale, v_rel_bias, v_sinks, v_w_out, v_g_post_mix, v_g_pre_ffn, v_w_gate, v_w_up, v_w_down, v_g_post_ffn):
    c = lax.axis_index("c")
    chip = 2 * lax.axis_index("x") + lax.axis_index("y")

    def shards(a_in, a_out, a_gate, a_up, a_down):
        return (a_in[0].T, a_out[0], a_gate[0].T, a_up[0].T, a_down[0])

    c_arr = c.reshape(1).astype(jnp.int32)
    chip_arr = chip.reshape(1).astype(jnp.int32)

    big_w = shards(w_in, w_out, w_gate, w_up, w_down)
    big_bf = [w.astype(BF16) for w in big_w]
    g_ssem, g_rsem, g_srcs, g_lands, _ = _split_start(
        _gather_copies, 15, big_bf, [lax.empty((NSH,) + a.shape, BF16) for a in big_bf], None, "gather_start")
    fw = {}

    def with_own(land, shard):
        return lax.dynamic_update_slice(land, shard[None], (chip, 0, 0))

    def w_in_ready(*after):
        fw["first"] = _split_wait(_gather_copies, g_ssem, g_rsem, g_srcs, g_lands, after, "gather_win_wait",
                                  only=(0, 1, 2))
        (fw["win"],) = _gather_finish([with_own(fw["first"][5], fw["first"][0])], "win")
        return fw["win"].reshape(IN_W, D)

    def w_out_ready(*after):
        first = fw["first"]
        fw["second"] = _split_wait(_gather_copies, g_ssem, g_rsem, first[:5], [fw["win"]] + list(first[6:]), after,
                                   "gather_wout_wait", only=(3, 4, 5))
        (fw["wout"],) = _gather_finish([with_own(fw["second"][6], fw["second"][1])], "wout")
        return fw["wout"].reshape(D, D), None

    def ffn_weights(after):
        second = fw["second"]
        outs = _split_wait(_gather_copies, g_ssem, g_rsem, second[:5], [second[5], fw["wout"]] + list(second[7:]),
                           [after], "gather_ffn_wait", only=tuple(range(6, 15)))
        return _gather_finish([with_own(land, src) for src, land in zip(outs[2:5], outs[7:])], "ffn")

    rs = {}

    def on_ffn_grads(ffn_g):
        oths = ffn_g[1::2]
        rs["ffn_own"] = ffn_g[0::2]
        rs["x"] = _split_start(_sibling_copies, 3, oths, [lax.empty(a.shape, BF16) for a in oths], ffn_g[0],
                               "rs_exchange_ffn_start")
        return rs["x"][4]

    def on_wout_grads(own, oth, after):
        ssem, rsem, srcs, lands, _ = rs["x"]
        recv_ffn = _split_wait(_sibling_copies, ssem, rsem, srcs, lands, [after], "rs_exchange_ffn_wait")[3:]
        recv_wout = _to_sibling([oth], "rs_exchange_wout")
        outs = _chip_partial([own] + list(rs["ffn_own"]), list(recv_wout) + list(recv_ffn), chip_arr, "early")
        rs["early_own"] = outs[4:]
        rs["s"] = _split_start(_scatter_copies, 12, outs[:4],
                               [lax.empty((3,) + a.shape[1:], BF16) for a in outs[:4]], outs[4], "scatter_early_start")
        return rs["s"][4]

    small = (g_pre_mix, g_post_mix, g_pre_ffn, g_post_ffn, w_pool[0], pool_scale, rel_bias, sinks)
    loss, gx, small_grads, ((win_own, win_oth), _, _) = _local_step(
        x[0], loss_target[0], small, w_in_ready, w_out_ready, ffn_weights, c_arr, on_ffn_grads, on_wout_grads)

    ssem, rsem, srcs, lands, _ = rs["s"]
    recv_early = _split_wait(_scatter_copies, ssem, rsem, srcs, lands, [gx], "scatter_early_wait")[4:]
    finals = _sum_chips(list(rs["early_own"]), list(recv_early), "early")
    sh_ssem, sh_rsem, sh_srcs, sh_lands, sh_token = _split_start(
        _sibling_copies, 4, finals, [lax.empty(a.shape, F32) for a in finals], finals[0], "rs_share_early_start")

    unused = jnp.zeros((1, 1), F32)
    gp = _pack_small(small_grads[:4], *small_grads[4:], loss)
    wp = _pack_small((g_pre_mix, g_post_mix, g_pre_ffn, g_post_ffn), w_pool, pool_scale, rel_bias.T, sinks, unused)
    mp = _pack_small((m_g_pre_mix, m_g_post_mix, m_g_pre_ffn, m_g_post_ffn), m_w_pool, m_pool_scale, m_rel_bias.T,
                     m_sinks, unused)
    vp = _pack_small((v_g_pre_mix, v_g_post_mix, v_g_pre_ffn, v_g_post_ffn), v_w_pool, v_pool_scale, v_rel_bias.T,
                     v_sinks, unused)

    moments = (shards(m_w_in, m_w_out, m_w_gate, m_w_up, m_w_down),
               shards(v_w_in, v_w_out, v_w_gate, v_w_up, v_w_down))
    recv_a = _to_sibling([win_oth], "rs_exchange_win", [sh_token])
    outs = _chip_partial([win_own], recv_a, chip_arr, "win")
    w_ssem, w_rsem, w_srcs, w_lands, w_token = _split_start(
        _scatter_copies, 3, outs[:1], [lax.empty((3,) + outs[0].shape[1:], BF16)], gx, "scatter_win_start")
    slots = lax.dynamic_update_slice(lax.empty((8,) + gp.shape, F32), gp[None], (2 * chip + c, 0, 0))
    p_ssem, p_rsem, p_srcs, p_lands, p_token = _split_start(_peer_copies, 7, [gp], [slots], w_token,
                                                            "small_allgather_start")
    shared = _split_wait(_sibling_copies, sh_ssem, sh_rsem, sh_srcs, sh_lands, [p_token], "rs_share_early_wait")
    adam_e = _adamw_shards(shared[:4], shared[4:], big_w[1:], moments[0][1:], moments[1][1:], c_arr, "early")
    recv_win = _split_wait(_scatter_copies, w_ssem, w_rsem, w_srcs, w_lands, [adam_e[0]], "scatter_win_wait")[1:]
    win_final = _sum_chips([outs[1]], recv_win, "win")
    win_sib = _to_sibling(win_final, "rs_share_win")
    adam_w = _adamw_shards(win_final, win_sib, big_w[:1], moments[0][:1], moments[1][:1], c_arr, "win")
    big_g, big_d, big_m, big_v = [[adam_w[i]] + list(adam_e[4 * i:4 * i + 4]) for i in range(4)]

    gathered = _split_wait(_peer_copies, p_ssem, p_rsem, p_srcs, p_lands, [adam_w[0]], "small_allgather_wait")[1]
    flat = _small_sum_adamw(gathered, wp, mp, vp)
    small_out = [flat[8 * kind:8 * kind + 8] for kind in range(4)]
    total_loss = flat[-1][0, 0]

    def ordered(small8, big4):
        sg1, sg2, sg3, sg4, swp, ssc, srb, ssk = small8
        swp, srb = swp[None], srb.T
        bwin, bwout, bwg, bwu, bwd = big4[0].T[None], big4[1][None], big4[2].T[None], big4[3].T[None], big4[4][None]
        return [sg1, bwin, swp, ssc, srb, ssk, bwout, sg2, sg3, bwg, bwu, bwd, sg4]

    res = [total_loss, gx[None]]
    for sm, bg in zip(small_out, (big_g, big_d, big_m, big_v)):
        res += ordered(sm, bg)
    return tuple(res)
```

```python
import numpy as np
import jax
import jax.numpy as jnp
from jax import lax
from jax.experimental import pallas as pl
from jax.experimental.pallas import tpu as pltpu

F32 = jnp.float32
BF16 = jnp.bfloat16

D = 1024
POOL_W = 512
GROUP = 128
WINDOWS = (2, 4, 8, 16)
HALO = 128
NQ = 8
BLK = 128
NBUCKET = 32
IN_W = 1280
DFF = 2816
NSH = 4
FS = DFF // NSH
WIN_S = IN_W // NSH
WOUT_S = D // NSH
EPS = 1e-6
NEG = -1e30
SCALE = 0.125

ADAM_LR = 0.001
ADAM_B1 = 0.9
ADAM_B2 = 0.999
ADAM_EPS = 1e-08
ADAM_WD = 0.01
ADAM_STEP = 10

TM = 512
TM_POOL = 512
TM_FFN = 512
TM_FFN_FWD = 1024
FFN_ROWS = 256
VMEM_LIMIT = 60 * 1024 * 1024

MESH_T = pl.DeviceIdType.MESH
ANY = pl.BlockSpec(memory_space=pl.ANY)


def _cp(n_grid=0, **kw):
    sem = ("arbitrary",) * n_grid if n_grid else None
    return pltpu.CompilerParams(dimension_semantics=sem, vmem_limit_bytes=VMEM_LIMIT, **kw)


def _dot(a, b):
    return jnp.dot(a, b, preferred_element_type=F32)


def _dot_nt(a, b):
    return lax.dot_general(a, b, (((1,), (1,)), ((), ())), preferred_element_type=F32)


def _dot_tn(a, b):
    return lax.dot_general(a, b, (((0,), (0,)), ((), ())), preferred_element_type=F32)


def _rms(x):
    r = lax.rsqrt(jnp.mean(x * x, axis=-1, keepdims=True) + EPS)
    return r, x * r


def _rms_bwd(r, n, g, dout):
    dn = dout * g
    dx = r * (dn - n * jnp.mean(dn * n, axis=-1, keepdims=True))
    dg = jnp.sum(dout * n, axis=0, keepdims=True)
    return dx, dg


def _split(x, n):
    parts = []
    r = x
    for _ in range(n):
        p = r.astype(BF16)
        parts.append(p)
        r = r - p.astype(F32)
    return parts


def _band_dot(a, x, n):
    acc = None
    for p in _split(x, n):
        t = _dot(a, p)
        acc = t if acc is None else acc + t
    return acc


def _bucket_table():
    qi = np.arange(BLK)[None, :]
    kj = np.arange(2 * BLK)[:, None]
    dist = qi + BLK - kj
    n = np.maximum(dist, 0)
    nf = np.maximum(n, 1).astype(np.float32)
    large = 16 + (np.log(nf / np.float32(16)) / np.float32(np.log(128 / 16)) * np.float32(16)).astype(np.int32)
    large = np.minimum(large, NBUCKET - 1)
    return np.where(n < 16, n, large).astype(np.int32)


def _prenorm(x, g1):
    s = x.shape[0]
    tm = min(TM, s)

    def body(x_ref, g_ref, h_ref):
        _, n = _rms(x_ref[...])
        h_ref[...] = (n * g_ref[...]).astype(BF16)

    row = pl.BlockSpec((tm, D), lambda i: (i, 0))
    return pl.pallas_call(
        body, name="prenorm", grid=(s // tm,),
        in_specs=[row, pl.BlockSpec((1, D), lambda i: (0, 0))], out_specs=row,
        out_shape=jax.ShapeDtypeStruct((s, D), BF16),
        compiler_params=_cp(1))(x, g1)


def _inproj_fwd(h1, w_in):
    s = h1.shape[0]
    tm = min(TM, s)

    def body(h_ref, w_ref, u_ref, q_ref, k_ref, v_ref):
        proj = _dot_nt(h_ref[...], w_ref[...])
        u_ref[...] = proj[:, :512]
        q_ref[...] = proj[:, 512:1024].astype(BF16)
        k_ref[...] = proj[:, 1024:1152].astype(BF16)
        v_ref[...] = proj[:, 1152:1280].astype(BF16)

    row = lambda w: pl.BlockSpec((tm, w), lambda i: (i, 0))
    return pl.pallas_call(
        body, name="inproj_fwd", grid=(s // tm,),
        in_specs=[row(D), pl.BlockSpec((IN_W, D), lambda i: (0, 0))],
        out_specs=[row(512), row(512), row(128), row(128)],
        out_shape=[jax.ShapeDtypeStruct((s, 512), F32), jax.ShapeDtypeStruct((s, 512), BF16),
                   jax.ShapeDtypeStruct((s, 128), BF16), jax.ShapeDtypeStruct((s, 128), BF16)],
        compiler_params=_cp(1))(h1, w_in)


def _window_sums(ext, w, lag_sign, tm, terms):
    rows = lax.broadcasted_iota(jnp.int32, (HALO, 2 * HALO), 0)
    cols = lax.broadcasted_iota(jnp.int32, (HALO, 2 * HALO), 1)
    d = rows + HALO - cols if lag_sign > 0 else cols - rows
    band = jnp.where((d >= 0) & (d < w), 1.0, 0.0).astype(BF16)
    return jnp.concatenate([_band_dot(band, ext[r:r + 2 * HALO], terms) for r in range(0, tm, HALO)], axis=0)


def _pooled(ext, cur, t0, tm):
    t = t0 + lax.broadcasted_iota(jnp.int32, (tm, GROUP), 0)
    out = []
    for g, w in enumerate(WINDOWS):
        sl = slice(g * GROUP, (g + 1) * GROUP)
        cnt = jnp.minimum(t + 1, w).astype(F32)
        out.append(_window_sums(ext[:, sl], w, 1, tm, 2) / cnt - cur[:, sl])
    return out


def _pool_fwd(u, w_pool, pool_scale):
    s = u.shape[0]
    tm = min(TM_POOL, s)
    hb = tm // HALO

    def body(uc_ref, uh_ref, wp_ref, sc_ref, o_ref, pooled_ref):
        i = pl.program_id(0)
        cur = uc_ref[...]
        halo = jnp.where(i > 0, uh_ref[...], 0.0)
        ext = jnp.concatenate([halo, cur], axis=0)
        pooled = _pooled(ext, cur, i * tm, tm)
        for g in range(4):
            sl = slice(g * GROUP, (g + 1) * GROUP)
            pb = pooled[g].astype(BF16)
            pooled_ref[:, sl] = pb
            o_ref[:, sl] = (_dot(pb, wp_ref[g]) * sc_ref[:, sl]).astype(BF16)

    row = pl.BlockSpec((tm, 512), lambda i: (i, 0))
    return pl.pallas_call(
        body, name="pool_fwd", grid=(s // tm,),
        in_specs=[row, pl.BlockSpec((HALO, 512), lambda i: (jnp.maximum(i * hb - 1, 0), 0)),
                  pl.BlockSpec((4, GROUP, GROUP), lambda i: (0, 0, 0)),
                  pl.BlockSpec((1, 512), lambda i: (0, 0))],
        out_specs=[row, row],
        out_shape=[jax.ShapeDtypeStruct((s, 512), BF16)] * 2,
        compiler_params=_cp(1))(u, u, w_pool, pool_scale)


def _bias_table(bucket, rel_bias):
    def body(b_ref, rb_ref, o_ref):
        bucket_v = b_ref[...]
        key = lax.broadcasted_iota(jnp.int32, (2 * BLK, BLK), 0)
        dist = lax.broadcasted_iota(jnp.int32, (2 * BLK, BLK), 1) + BLK - key
        inwin = (dist >= 0) & (dist < BLK)
        for h in range(NQ):
            acc = jnp.zeros((2 * BLK, BLK), F32)
            for b in range(NBUCKET):
                acc = jnp.where(bucket_v == b, rb_ref[b, h], acc)
            rest = jnp.where(inwin, acc, NEG)
            o_ref[1, h] = rest
            o_ref[0, h] = jnp.where(key >= BLK, rest, NEG)

    return pl.pallas_call(
        body, name="bias_table",
        in_specs=[pl.BlockSpec(memory_space=pltpu.VMEM), pl.BlockSpec(memory_space=pltpu.SMEM)],
        out_specs=pl.BlockSpec(memory_space=pltpu.VMEM),
        out_shape=jax.ShapeDtypeStruct((2, NQ, 2 * BLK, BLK), F32),
        compiler_params=_cp())(bucket, rel_bias)


HD = 64


def _kv_band(ref, n, transposed):
    pstart = pl.multiple_of(jnp.maximum(n - 1, 0) * BLK, BLK)
    cstart = pl.multiple_of(n * BLK, BLK)
    lo = lax.broadcasted_iota(jnp.int32, (2 * BLK, 128), 1) < HD
    band = jnp.concatenate([ref[pl.ds(pstart, BLK), :], ref[pl.ds(cstart, BLK), :]], axis=0).astype(F32)
    rot = pltpu.roll(band, HD, axis=1)
    halves = ((jnp.where(lo, band, 0.0), jnp.where(lo, 0.0, rot)), (jnp.where(lo, rot, 0.0), jnp.where(lo, 0.0, band)))
    if transposed:
        out = [jnp.concatenate([a.T, b.T], axis=1).astype(BF16) for a, b in halves]
    else:
        out = [jnp.concatenate([a, b], axis=0).astype(BF16) for a, b in halves]
    return out, (lo, pstart, cstart)


def _pair_probs(qt, kk, bias, sk_ref, p):
    sink = jnp.concatenate([jnp.full((1, 1, BLK), sk_ref[0, 2 * p + e], F32) for e in range(2)], axis=0)
    s = _dot_nt(kk, qt).reshape(2, 2 * BLK, BLK) + bias
    m = jnp.maximum(jnp.max(s, axis=1, keepdims=True), sink)
    pr = jnp.exp(s - m)
    es = jnp.exp(sink - m)
    inv = 1.0 / (jnp.sum(pr, axis=1, keepdims=True) + es)
    return pr * inv, es * inv


def _attn_fwd(q, k, v, bias, sinks):
    s = q.shape[0]
    nblk = s // BLK

    def body(q_ref, k_ref, v_ref, b_ref, sk_ref, o_ref, prob_ref, ps_ref):
        n = pl.program_id(0)
        kk, _ = _kv_band(k_ref, n, False)
        vt, _ = _kv_band(v_ref, n, True)
        bidx = jnp.minimum(n, 1)
        for p in range(4):
            h = p // 2
            qt = (q_ref[:, p * 128:(p + 1) * 128].astype(F32) * SCALE).astype(BF16)
            prob, ps = _pair_probs(qt, kk[h], b_ref[bidx, pl.ds(2 * p, 2)], sk_ref, p)
            pb = prob.astype(BF16)
            prob_ref[pl.ds(2 * p, 2)] = pb
            ps_ref[pl.ds(2 * p, 2), :] = ps.reshape(2, BLK)
            acc_t = _dot(vt[h], pb.reshape(4 * BLK, BLK))
            o_ref[:, p * 128:(p + 1) * 128] = acc_t.T.astype(BF16)

    return pl.pallas_call(
        body, name="attn_fwd", grid=(nblk,),
        in_specs=[pl.BlockSpec((BLK, 512), lambda i: (i, 0)),
                  pl.BlockSpec((s, 128), lambda i: (0, 0)),
                  pl.BlockSpec((s, 128), lambda i: (0, 0)),
                  pl.BlockSpec((2, NQ, 2 * BLK, BLK), lambda i: (0, 0, 0, 0)),
                  pl.BlockSpec(memory_space=pltpu.SMEM)],
        out_specs=[pl.BlockSpec((BLK, 512), lambda i: (i, 0)),
                   pl.BlockSpec((None, NQ, 2 * BLK, BLK), lambda i: (i, 0, 0, 0)),
                   pl.BlockSpec((None, NQ, BLK), lambda i: (i, 0, 0))],
        out_shape=[jax.ShapeDtypeStruct((s, 512), BF16), jax.ShapeDtypeStruct((nblk, NQ, 2 * BLK, BLK), BF16),
                   jax.ShapeDtypeStruct((nblk, NQ, BLK), F32)],
        compiler_params=_cp(1))(q, k, v, bias, sinks)


def _outproj_fwd(pool_o, attn_o, w_out, x, g2, g3):
    s = x.shape[0]
    tm = min(TM, s)

    def body(p_ref, a_ref, w_ref, x_ref, g2_ref, g3_ref, mix_ref, x1_ref, h2_ref):
        mix = _dot(p_ref[...], w_ref[0:512, :]) + _dot(a_ref[...], w_ref[512:1024, :])
        mix_ref[...] = mix
        _, n2 = _rms(mix)
        x1 = x_ref[...] + n2 * g2_ref[...]
        x1_ref[...] = x1
        _, n3 = _rms(x1)
        h2_ref[...] = (n3 * g3_ref[...]).astype(BF16)

    row = lambda w: pl.BlockSpec((tm, w), lambda i: (i, 0))
    vec = pl.BlockSpec((1, D), lambda i: (0, 0))
    return pl.pallas_call(
        body, name="outproj_fwd", grid=(s // tm,),
        in_specs=[row(512), row(512), pl.BlockSpec((D, D), lambda i: (0, 0)), row(D), vec, vec],
        out_specs=[row(D), row(D), row(D)],
        out_shape=[jax.ShapeDtypeStruct((s, D), F32), jax.ShapeDtypeStruct((s, D), F32),
                   jax.ShapeDtypeStruct((s, D), BF16)],
        compiler_params=_cp(1))(pool_o, attn_o, w_out, x, g2, g3)


def _silu_parts(g):
    sg = jax.nn.sigmoid(g)
    return sg, g * sg


def _ffn_fwd(h2, wg, wu, wd, x1, target, g4):
    s = h2.shape[0]
    tm = min(TM_FFN_FWD, s)

    def body(h_ref, wg_ref, wu_ref, wd_ref, x1_ref, t_ref, g4_ref,
             gate_ref, up_ref, a_ref, df_ref, dy_ref, loss_ref, gg4_ref, f_acc):
        i = pl.program_id(0)
        j = pl.program_id(1)
        first = j == 0
        for r in range(0, tm, FFN_ROWS):
            rows = pl.ds(r, min(FFN_ROWS, tm))
            h = h_ref[rows, :]
            gate = _dot_nt(h, wg_ref[...])
            up = _dot_nt(h, wu_ref[...])
            gate_ref[rows, :] = gate.astype(BF16)
            up_ref[rows, :] = up.astype(BF16)
            _, sl = _silu_parts(gate)
            a = (sl * up).astype(BF16)
            a_ref[rows, :] = a
            contrib = _dot(a, wd_ref[...])
            f_acc[rows, :] = jnp.where(first, contrib, f_acc[rows, :] + contrib)

        @pl.when((i == 0) & (j == 0))
        def _():
            loss_ref[...] = jnp.zeros_like(loss_ref)
            gg4_ref[...] = jnp.zeros_like(gg4_ref)

        @pl.when(j == NSH - 1)
        def _():
            r4, n4 = _rms(f_acc[...])
            g4v = g4_ref[...]
            err = x1_ref[...] + n4 * g4v - t_ref[...]
            loss_ref[...] += (0.5 / D) * jnp.sum(err * err).reshape(1, 1)
            dy = err * (1.0 / D)
            dy_ref[...] = dy
            df, dg = _rms_bwd(r4, n4, g4v, dy)
            gg4_ref[...] += dg
            df_ref[...] = df.astype(BF16)

    row = lambda w: pl.BlockSpec((tm, w), lambda i, j: (i, 0))
    sh = lambda r, c: pl.BlockSpec((None, r, c), lambda i, j: (j, 0, 0))
    act = pl.BlockSpec((None, tm, FS), lambda i, j: (j, i, 0))
    vec = pl.BlockSpec((1, D), lambda i, j: (0, 0))
    return pl.pallas_call(
        body, name="ffn_fwd", grid=(s // tm, NSH),
        in_specs=[row(D), sh(FS, D), sh(FS, D), sh(FS, D), row(D), row(D), vec],
        out_specs=[act, act, act, row(D), row(D), pl.BlockSpec((1, 1), lambda i, j: (0, 0)), vec],
        out_shape=[jax.ShapeDtypeStruct((NSH, s, FS), BF16)] * 3
        + [jax.ShapeDtypeStruct((s, D), BF16), jax.ShapeDtypeStruct((s, D), F32),
           jax.ShapeDtypeStruct((1, 1), F32), jax.ShapeDtypeStruct((1, D), F32)],
        scratch_shapes=[pltpu.VMEM((tm, D), F32)],
        compiler_params=_cp(2))(h2, wg, wu, wd, x1, target, g4)


def _ffn_bwd_act(df, gate, up, wg, wu, wd, dy, x1, mix, g3, g2):
    s = df.shape[0]
    tm = min(TM_FFN, s)

    def body(df_ref, gate_ref, up_ref, wg_ref, wu_ref, wd_ref, dy_ref, x1_ref, mix_ref, g3_ref, g2_ref,
             dgate_ref, dup_ref, dx1_ref, dmix_ref, gg3_ref, gg2_ref, acc):
        i = pl.program_id(0)
        j = pl.program_id(1)
        first = j == 0
        for r in range(0, tm, FFN_ROWS):
            rows = pl.ds(r, min(FFN_ROWS, tm))
            da = _dot_nt(df_ref[rows, :], wd_ref[...])
            g = gate_ref[rows, :].astype(F32)
            u = up_ref[rows, :].astype(F32)
            sg, sl = _silu_parts(g)
            dgate = (da * u * (sg * (1.0 + g * (1.0 - sg)))).astype(BF16)
            dup = (da * sl).astype(BF16)
            dgate_ref[rows, :] = dgate
            dup_ref[rows, :] = dup
            contrib = _dot(dgate, wg_ref[...]) + _dot(dup, wu_ref[...])
            acc[rows, :] = jnp.where(first, contrib, acc[rows, :] + contrib)

        @pl.when((i == 0) & (j == 0))
        def _():
            gg3_ref[...] = jnp.zeros_like(gg3_ref)
            gg2_ref[...] = jnp.zeros_like(gg2_ref)

        @pl.when(j == NSH - 1)
        def _():
            r3, n3 = _rms(x1_ref[...])
            dx1n, dg3 = _rms_bwd(r3, n3, g3_ref[...], acc[...])
            dx1 = dy_ref[...] + dx1n
            dx1_ref[...] = dx1
            gg3_ref[...] += dg3
            r2, n2 = _rms(mix_ref[...])
            dmix, dg2 = _rms_bwd(r2, n2, g2_ref[...], dx1)
            gg2_ref[...] += dg2
            dmix_ref[...] = dmix.astype(BF16)

    row = lambda w: pl.BlockSpec((tm, w), lambda i, j: (i, 0))
    sh = lambda r, c: pl.BlockSpec((None, r, c), lambda i, j: (j, 0, 0))
    act = pl.BlockSpec((None, tm, FS), lambda i, j: (j, i, 0))
    vec = pl.BlockSpec((1, D), lambda i, j: (0, 0))
    return pl.pallas_call(
        body, name="ffn_bwd_act", grid=(s // tm, NSH),
        in_specs=[row(D), act, act, sh(FS, D), sh(FS, D), sh(FS, D), row(D), row(D), row(D), vec, vec],
        out_specs=[act, act, row(D), row(D), vec, vec],
        out_shape=[jax.ShapeDtypeStruct((NSH, s, FS), BF16), jax.ShapeDtypeStruct((NSH, s, FS), BF16),
                   jax.ShapeDtypeStruct((s, D), F32), jax.ShapeDtypeStruct((s, D), BF16),
                   jax.ShapeDtypeStruct((1, D), F32), jax.ShapeDtypeStruct((1, D), F32)],
        scratch_shapes=[pltpu.VMEM((tm, D), F32)],
        compiler_params=_cp(2))(df, gate, up, wg, wu, wd, dy, x1, mix, g3, g2)


SMEM_SPEC = pl.BlockSpec(memory_space=pltpu.SMEM)


def _emit_halves(acc_ref, row0, rows, c, own_ref, oth_ref):
    half = rows // 2
    own_ref[...] = acc_ref[pl.ds(row0 + pl.multiple_of(c * half, 8), half), :]
    oth_ref[...] = acc_ref[pl.ds(row0 + pl.multiple_of((1 - c) * half, 8), half), :].astype(BF16)


def _half_shapes(rows):
    return [jax.ShapeDtypeStruct((NSH, rows // 2, D), F32), jax.ShapeDtypeStruct((NSH, rows // 2, D), BF16)]


def _ffn_bwd_w(h2, act_a, dgate, dup, df, c_arr):
    s = h2.shape[0]
    tm = min(TM_FFN_FWD, s)
    nt = s // tm

    def body(c_ref, h_ref, a_ref, dgate_ref, dup_ref, df_ref, *rest):
        outs, accs = rest[:6], rest[6:]
        i = pl.program_id(1)

        @pl.when(i == 0)
        def _():
            for acc in accs:
                acc[...] = jnp.zeros_like(acc)

        h = h_ref[...]
        accs[0][...] += _dot_tn(dgate_ref[...], h)
        accs[1][...] += _dot_tn(dup_ref[...], h)
        accs[2][...] += _dot_tn(a_ref[...], df_ref[...])

        @pl.when(i == nt - 1)
        def _():
            for t, acc in enumerate(accs):
                _emit_halves(acc, 0, FS, c_ref[0], outs[2 * t], outs[2 * t + 1])

    row = lambda w: pl.BlockSpec((tm, w), lambda j, i: (i, 0))
    act = pl.BlockSpec((None, tm, FS), lambda j, i: (j, i, 0))
    half = pl.BlockSpec((None, FS // 2, D), lambda j, i: (j, 0, 0))
    return pl.pallas_call(
        body, name="ffn_bwd_w", grid=(NSH, nt),
        in_specs=[SMEM_SPEC, row(D), act, act, act, row(D)],
        out_specs=[half] * 6,
        out_shape=_half_shapes(FS) * 3,
        scratch_shapes=[pltpu.VMEM((FS, D), F32)] * 3,
        compiler_params=_cp(2))(c_arr, h2, act_a, dgate, dup, df)


def _outproj_bwd(dmix, w_out, pool_o, attn_o, c_arr, dep=None):
    s = dmix.shape[0]
    tm = min(TM, s)
    nt = s // tm

    def body(c_ref, dm_ref, w_ref, p_ref, a_ref, *rest):
        dpool_ref, dattn_ref, own_ref, oth_ref, gw_ref = rest[-5:]
        i = pl.program_id(0)

        @pl.when(i == 0)
        def _():
            gw_ref[...] = jnp.zeros_like(gw_ref)

        dm = dm_ref[...]
        dpool_ref[...] = _dot_nt(dm, w_ref[0:512, :])
        dattn_ref[...] = _dot_nt(dm, w_ref[512:1024, :]).astype(BF16)
        gw_ref[0:512, :] += _dot_tn(p_ref[...], dm)
        gw_ref[512:1024, :] += _dot_tn(a_ref[...], dm)

        @pl.when(i == nt - 1)
        def _():
            for k in range(NSH):
                _emit_halves(gw_ref, k * WOUT_S, WOUT_S, c_ref[0], own_ref.at[k], oth_ref.at[k])

    row = lambda w: pl.BlockSpec((tm, w), lambda i: (i, 0))
    full = pl.BlockSpec((D, D), lambda i: (0, 0))
    half = pl.BlockSpec((NSH, WOUT_S // 2, D), lambda i: (0, 0, 0))
    return pl.pallas_call(
        body, name="outproj_bwd", grid=(nt,),
        in_specs=[SMEM_SPEC, row(D), full, row(512), row(512)] + ([] if dep is None else [ANY]),
        out_specs=[row(512), row(512), half, half],
        out_shape=[jax.ShapeDtypeStruct((s, 512), F32), jax.ShapeDtypeStruct((s, 512), BF16)] + _half_shapes(WOUT_S),
        scratch_shapes=[pltpu.VMEM((D, D), F32)],
        compiler_params=_cp(1))(c_arr, dmix, w_out, pool_o, attn_o, *([] if dep is None else [dep]))


def _pool_bwd(pooled, dpool, w_pool, pool_scale, dep=None):
    s = pooled.shape[0]
    tm = min(TM_POOL, s)
    hb = tm // HALO
    nt = s // tm
    last_halo = s // HALO - 1

    def body(p_ref, dc_ref, dn_ref, wp_ref, sc_ref, *rest):
        du_ref, gwp_ref, gsc_ref = rest[-3:]
        i = pl.program_id(0)

        @pl.when(i == 0)
        def _():
            gwp_ref[...] = jnp.zeros_like(gwp_ref)
            gsc_ref[...] = jnp.zeros_like(gsc_ref)

        dcur = dc_ref[...]
        dnext = jnp.where(i < nt - 1, dn_ref[...], 0.0)
        dext = jnp.concatenate([dcur, dnext], axis=0)
        t_ext = i * tm + lax.broadcasted_iota(jnp.int32, (tm + HALO, GROUP), 0)
        for g, w in enumerate(WINDOWS):
            sl = slice(g * GROUP, (g + 1) * GROUP)
            pb = p_ref[:, sl]
            wp = wp_ref[g]
            mixed = _dot(pb, wp)
            gsc_ref[:, sl] += jnp.sum(dcur[:, sl] * mixed, axis=0, keepdims=True)
            dmixed = (dext[:, sl] * sc_ref[:, sl]).astype(BF16)
            gwp_ref[g] += _dot_tn(pb, dmixed[0:tm])
            dpooled = _dot_nt(dmixed, wp)
            z = dpooled / jnp.minimum(t_ext + 1, w).astype(F32)
            du_ref[:, sl] = (_window_sums(z, w, -1, tm, 2) - dpooled[0:tm]).astype(BF16)

    return pl.pallas_call(
        body, name="pool_bwd", grid=(nt,),
        in_specs=[pl.BlockSpec((tm, 512), lambda i: (i, 0)),
                  pl.BlockSpec((tm, 512), lambda i: (i, 0)),
                  pl.BlockSpec((HALO, 512), lambda i: (jnp.minimum((i + 1) * hb, last_halo), 0)),
                  pl.BlockSpec((4, GROUP, GROUP), lambda i: (0, 0, 0)),
                  pl.BlockSpec((1, 512), lambda i: (0, 0))] + ([] if dep is None else [ANY]),
        out_specs=[pl.BlockSpec((tm, 512), lambda i: (i, 0)),
                   pl.BlockSpec((4, GROUP, GROUP), lambda i: (0, 0, 0)),
                   pl.BlockSpec((1, 512), lambda i: (0, 0))],
        out_shape=[jax.ShapeDtypeStruct((s, 512), BF16), jax.ShapeDtypeStruct((4, GROUP, GROUP), F32),
                   jax.ShapeDtypeStruct((1, 512), F32)],
        compiler_params=_cp(1))(pooled, dpool, dpool, w_pool, pool_scale, *([] if dep is None else [dep]))


def _attn_bwd(q, k, v, do, probs, sink_share, bucket, dep=None):
    s = q.shape[0]
    nblk = s // BLK

    def body(q_ref, do_ref, k_ref, v_ref, prob_ref, ps_ref, bk_ref, *rest):
        dq_ref, dk_ref, dv_ref, grb_ref, gsk_ref, dss_ref = rest[-6:]
        n = pl.program_id(0)

        @pl.when(n == 0)
        def _():
            dk_ref[...] = jnp.zeros_like(dk_ref)
            dv_ref[...] = jnp.zeros_like(dv_ref)
            dss_ref[...] = jnp.zeros_like(dss_ref)
            gsk_ref[...] = jnp.zeros_like(gsk_ref)

        kt, (lo, pstart, cstart) = _kv_band(k_ref, n, True)
        vv, _ = _kv_band(v_ref, n, False)
        lo_q = lax.broadcasted_iota(jnp.int32, (BLK, 128), 1) < HD
        dkk = [jnp.zeros((2 * BLK, 128), F32), jnp.zeros((2 * BLK, 128), F32)]
        dvv = [jnp.zeros((2 * BLK, 128), F32), jnp.zeros((2 * BLK, 128), F32)]

        def by_head(t):
            return jnp.concatenate([jnp.where(lo_q, t, 0.0), jnp.where(lo_q, 0.0, t)], axis=0).astype(BF16)

        for p in range(4):
            h = p // 2
            qt = q_ref[:, p * 128:(p + 1) * 128].astype(F32) * SCALE
            dot = do_ref[:, p * 128:(p + 1) * 128]
            pb = prob_ref[pl.ds(2 * p, 2)]
            prob = pb.astype(F32)
            ps = ps_ref[pl.ds(2 * p, 2), :].reshape(2, 1, BLK)
            dp = _dot_nt(vv[h], dot).reshape(2, 2 * BLK, BLK)
            delta = jnp.sum(prob * dp, axis=1, keepdims=True)
            ds = prob * (dp - delta)
            sink_part = ps * delta
            for e in range(2):
                gs = -jnp.sum(sink_part[e]).reshape(1, 1)
                gsk_ref[pl.ds(2 * p + e, 1), :] += jnp.broadcast_to(gs, (1, 128))
            dss_ref[pl.ds(2 * p, 2)] += ds
            dsb = ds.astype(BF16)
            dq_t = _dot(kt[h], dsb.reshape(4 * BLK, BLK))
            dq_ref[:, p * 128:(p + 1) * 128] = (dq_t.T * SCALE).astype(BF16)
            dkk[h] = dkk[h] + _dot(jnp.concatenate([dsb[0], dsb[1]], axis=1), by_head(qt))
            dvv[h] = dvv[h] + _dot(jnp.concatenate([pb[0], pb[1]], axis=1), by_head(dot.astype(F32)))
        for acc, ref in ((dkk, dk_ref), (dvv, dv_ref)):
            f0 = acc[0] + pltpu.roll(acc[0], HD, axis=1)
            f1 = acc[1] + pltpu.roll(acc[1], HD, axis=1)
            band = jnp.where(lo, f0, f1)
            ref[pl.ds(pstart, BLK), :] += band[0:BLK]
            ref[pl.ds(cstart, BLK), :] += band[BLK:2 * BLK]

        @pl.when(n == nblk - 1)
        def _():
            _relbias_grad(dss_ref, bk_ref[...], grb_ref)

    blk = pl.BlockSpec((BLK, 512), lambda i: (i, 0))
    kv = pl.BlockSpec((s, 128), lambda i: (0, 0))
    row8 = pl.BlockSpec((NQ, 128), lambda i: (0, 0))
    return pl.pallas_call(
        body, name="attn_bwd", grid=(nblk,),
        in_specs=[blk, blk, kv, kv, pl.BlockSpec((None, NQ, 2 * BLK, BLK), lambda i: (i, 0, 0, 0)),
                  pl.BlockSpec((None, NQ, BLK), lambda i: (i, 0, 0)), pl.BlockSpec((2 * BLK, BLK), lambda i: (0, 0))]
        + ([] if dep is None else [ANY]),
        out_specs=[blk, kv, kv, row8, row8],
        out_shape=[jax.ShapeDtypeStruct((s, 512), BF16), jax.ShapeDtypeStruct((s, 128), F32),
                   jax.ShapeDtypeStruct((s, 128), F32), jax.ShapeDtypeStruct((NQ, 128), F32),
                   jax.ShapeDtypeStruct((NQ, 128), F32)],
        scratch_shapes=[pltpu.VMEM((NQ, 2 * BLK, BLK), F32)],
        compiler_params=_cp(1))(q, do, k, v, probs, sink_share, bucket, *([] if dep is None else [dep]))


def _relbias_grad(ds_ref, bucket_v, o_ref):
    lane = lax.broadcasted_iota(jnp.int32, (NQ, 128), 1)
    head_row = lax.broadcasted_iota(jnp.int32, (NQ, BLK), 0)
    acc = jnp.zeros((NQ, 128), F32)
    for b in range(NBUCKET):
        sel = bucket_v == b
        stack = jnp.zeros((NQ, BLK), F32)
        for h in range(NQ):
            col_sums = jnp.sum(jnp.where(sel, ds_ref[h], 0.0), axis=0, keepdims=True)
            stack = jnp.where(head_row == h, col_sums, stack)
        acc = acc + jnp.where(lane == b, jnp.sum(stack, axis=1, keepdims=True), 0.0)
    o_ref[...] = acc


def _inproj_bwd(x, g1, du, dq, dk, dv, w_in, dx1, c_arr):
    s = x.shape[0]
    tm = min(TM, s)
    nt = s // tm
    cols = ((0, 512), (512, 1024), (1024, 1152), (1152, 1280))

    def body(c_ref, x_ref, g_ref, du_ref, dq_ref, dk_ref, dv_ref, w_ref, dx1_ref,
             gx_ref, own_ref, oth_ref, gg_ref, gw_ref):
        i = pl.program_id(0)

        @pl.when(i == 0)
        def _():
            gw_ref[...] = jnp.zeros_like(gw_ref)
            gg_ref[...] = jnp.zeros_like(gg_ref)

        gv = g_ref[...]
        r1, n1 = _rms(x_ref[...])
        h = (n1 * gv).astype(BF16)
        parts = (du_ref[...], dq_ref[...], dk_ref[...].astype(BF16), dv_ref[...].astype(BF16))
        dh = None
        for (a, b), dpart in zip(cols, parts):
            t = _dot(dpart, w_ref[a:b, :])
            dh = t if dh is None else dh + t
            gw_ref[a:b, :] += _dot_tn(dpart, h)
        dx, dg = _rms_bwd(r1, n1, gv, dh)
        gg_ref[...] += dg
        gx_ref[...] = dx1_ref[...] + dx

        @pl.when(i == nt - 1)
        def _():
            for k in range(NSH):
                _emit_halves(gw_ref, k * WIN_S, WIN_S, c_ref[0], own_ref.at[k], oth_ref.at[k])

    row = lambda w: pl.BlockSpec((tm, w), lambda i: (i, 0))
    vec = pl.BlockSpec((1, D), lambda i: (0, 0))
    full = pl.BlockSpec((IN_W, D), lambda i: (0, 0))
    half = pl.BlockSpec((NSH, WIN_S // 2, D), lambda i: (0, 0, 0))
    return pl.pallas_call(
        body, name="inproj_bwd", grid=(nt,),
        in_specs=[SMEM_SPEC, row(D), vec, row(512), row(512), row(128), row(128), full, row(D)],
        out_specs=[row(D), half, half, vec],
        out_shape=[jax.ShapeDtypeStruct((s, D), F32)] + _half_shapes(WIN_S) + [jax.ShapeDtypeStruct((1, D), F32)],
        scratch_shapes=[pltpu.VMEM((IN_W, D), F32)],
        compiler_params=_cp(1))(c_arr, x, g1, du, dq, dk, dv, w_in, dx1)


def _local_step(x, target, small, w_in, w_out, ffn_weights, c_arr, on_ffn_grads=None, on_wout_grads=None):
    g1, g2, g3, g4, w_pool, pool_scale, rel_bias, sinks = small
    bucket = jnp.asarray(_bucket_table())
    wp_bf = w_pool.astype(BF16)
    bias = _bias_table(bucket, rel_bias)
    h1 = _prenorm(x, g1)
    if callable(w_in):
        w_in = w_in(bias, h1)
    u, q, k, v = _inproj_fwd(h1, w_in)
    pool_o, pooled = _pool_fwd(u, wp_bf, pool_scale)
    attn_o, probs, sink_share = _attn_fwd(q, k, v, bias, sinks)
    g2_fwd = g2
    if callable(w_out):
        w_out, after = w_out(pool_o, attn_o)
        if after is not None:
            g2_fwd = g2 + after[:1, :1]
    mix, x1, h2 = _outproj_fwd(pool_o, attn_o, w_out, x, g2_fwd, g3)
    wg, wu, wd = ffn_weights(h2) if callable(ffn_weights) else ffn_weights
    gate, up, act_a, df, dy, loss, gg4 = _ffn_fwd(h2, wg, wu, wd, x1, target, g4)
    dgate, dup, dx1, dmix, gg3, gg2 = _ffn_bwd_act(df, gate, up, wg, wu, wd, dy, x1, mix, g3, g2)
    ffn_g = _ffn_bwd_w(h2, act_a, dgate, dup, df, c_arr)
    dep = None if on_ffn_grads is None else on_ffn_grads(ffn_g)
    dpool, dattn, wout_own, wout_oth = _outproj_bwd(dmix, w_out, pool_o, attn_o, c_arr, dep)
    dep = None if on_wout_grads is None else on_wout_grads(wout_own, wout_oth, dpool)
    du, gwp, gsc = _pool_bwd(pooled, dpool, wp_bf, pool_scale, dep)
    dq, dk, dv, grb, gsk = _attn_bwd(q, k, v, dattn, probs, sink_share, bucket, dep)
    gx, win_own, win_oth, gg1 = _inproj_bwd(x, g1, du, dq, dk, dv, w_in, dx1, c_arr)
    small_grads = (gg1, gg2, gg3, gg4, gwp, gsc, grb, gsk[:, 0].reshape(1, NQ))
    return loss, gx, small_grads, ((win_own, win_oth), (wout_own, wout_oth), ffn_g)


def _place():
    x, y, c = lax.axis_index("x"), lax.axis_index("y"), lax.axis_index("c")
    chips = ((1 - x, y), (x, 1 - y), (1 - x, 1 - y))
    return x, y, c, chips


def _remote(src, dst, ssem, rsem, dev):
    return pltpu.make_async_remote_copy(src_ref=src, dst_ref=dst, send_sem=ssem, recv_sem=rsem,
                                        device_id=dev, device_id_type=MESH_T)


def _to_sibling(arrs, name, after=()):
    nt, na = len(arrs), len(after)

    def body(*refs):
        srcs, dsts = refs[:nt], refs[nt + na:2 * nt + na]
        ssem, rsem = refs[2 * nt + na:]
        x, y, c, _ = _place()
        cps = [_remote(srcs[t], dsts[t], ssem.at[t], rsem.at[t], (x, y, 1 - c)) for t in range(nt)]
        for cp in cps:
            cp.start()
        for cp in cps:
            cp.wait()

    return pl.pallas_call(
        body, name=name, in_specs=[ANY] * (nt + na), out_specs=[ANY] * nt,
        out_shape=[jax.ShapeDtypeStruct(a.shape, a.dtype) for a in arrs],
        scratch_shapes=[pltpu.SemaphoreType.DMA((nt,)), pltpu.SemaphoreType.DMA((nt,))],
        compiler_params=_cp())(*arrs, *after)


HBM_SPEC = pl.BlockSpec(memory_space=pltpu.HBM)
SEM_SPEC = pl.BlockSpec(memory_space=pltpu.SEMAPHORE)
DATAFLOW = pltpu.SideEffectType.DATAFLOW_SIDE_EFFECTING


def _gather_copies(srcs, lands, ssem, rsem):
    x, y, c, chips = _place()
    me = 2 * x + y
    out = []
    for t in range(len(srcs)):
        half = srcs[t].shape[0] // 2
        mine = pl.ds(pl.multiple_of(c * half, 16), half)
        for j, (px, py) in enumerate(chips):
            k = 3 * t + j
            send = _remote(srcs[t].at[mine], lands[t].at[me, mine], ssem.at[k], rsem.at[k], (px, py, c))
            blk = lands[t].at[2 * px + py, mine]
            out.append((send, _remote(blk, blk, ssem.at[k], rsem.at[k], (px, py, c))))
    return out


def _scatter_copies(srcs, lands, ssem, rsem):
    x, y, c, chips = _place()
    out = []
    for t in range(len(srcs)):
        for j, (px, py) in enumerate(chips):
            k = 3 * t + j
            send = _remote(srcs[t].at[2 * px + py], lands[t].at[j], ssem.at[k], rsem.at[k], (px, py, c))
            out.append((send, _remote(lands[t].at[j], lands[t].at[j], ssem.at[k], rsem.at[k], (px, py, c))))
    return out


def _sibling_copies(srcs, lands, ssem, rsem):
    x, y, c, _ = _place()
    sib = (x, y, 1 - c)
    return [(_remote(srcs[t], lands[t], ssem.at[t], rsem.at[t], sib),
             _remote(lands[t], lands[t], ssem.at[t], rsem.at[t], sib)) for t in range(len(srcs))]


def _peer_copies(srcs, lands, ssem, rsem):
    x, y, c, _ = _place()
    me = 4 * x + 2 * y + c
    out = []
    for kk in range(1, 8):
        px, py, pc = x ^ (kk >> 2), y ^ ((kk >> 1) & 1), c ^ (kk & 1)
        send = _remote(srcs[0], lands[0].at[me], ssem.at[kk - 1], rsem.at[kk - 1], (px, py, pc))
        slot = lands[0].at[4 * px + 2 * py + pc]
        out.append((send, _remote(slot, slot, ssem.at[kk - 1], rsem.at[kk - 1], (px, py, pc))))
    return out


def _split_start(plan, ncopy, srcs, lands, after, name):
    ns, nbuf = len(srcs), len(srcs) + len(lands)
    extra = [] if after is None else [after]
    n_in = nbuf + len(extra)

    def body(*refs):
        ssem, rsem, token = refs[n_in], refs[n_in + 1], refs[-1]
        for send, _ in plan(refs[:ns], refs[ns:nbuf], ssem, rsem):
            send.start()
        token[...] = jnp.zeros_like(token)

    bufs = [pltpu.with_memory_space_constraint(a, pltpu.HBM) for a in list(srcs) + list(lands)]
    outs = pl.pallas_call(
        body, name=name, in_specs=[HBM_SPEC] * nbuf + [ANY] * len(extra),
        out_specs=[SEM_SPEC, SEM_SPEC] + [HBM_SPEC] * nbuf + [pl.BlockSpec(memory_space=pltpu.VMEM)],
        out_shape=[pltpu.SemaphoreType.DMA((ncopy,)), pltpu.SemaphoreType.DMA((ncopy,))]
        + [pltpu.HBM(a.shape, a.dtype) for a in bufs] + [jax.ShapeDtypeStruct((8, 128), F32)],
        input_output_aliases={i: 2 + i for i in range(nbuf)},
        compiler_params=pltpu.CompilerParams(has_side_effects=DATAFLOW))(*bufs, *extra)
    return outs[0], outs[1], outs[2:2 + ns], outs[2 + ns:2 + nbuf], outs[-1]


def _split_wait(plan, ssem, rsem, srcs, lands, after, name, only=None):
    ns, nbuf = len(srcs), len(srcs) + len(lands)

    def body(*refs):
        s_ref, r_ref = refs[nbuf], refs[nbuf + 1]
        for k, (send, recv) in enumerate(plan(refs[:ns], refs[ns:nbuf], s_ref, r_ref)):
            if only is None or k in only:
                send.wait_send()
                recv.wait_recv()

    outs = pl.pallas_call(
        body, name=name, in_specs=[HBM_SPEC] * nbuf + [SEM_SPEC, SEM_SPEC] + [ANY] * len(after),
        out_specs=[HBM_SPEC] * nbuf,
        out_shape=[pltpu.HBM(a.shape, a.dtype) for a in list(srcs) + list(lands)],
        input_output_aliases={i: i for i in range(nbuf)},
        compiler_params=pltpu.CompilerParams(has_side_effects=DATAFLOW))(*srcs, *lands, ssem, rsem, *after)
    return outs


def _gather_finish(lands, tag):
    nt = len(lands)

    def body(*refs):
        outs = refs[nt:2 * nt]
        dsend, drecv = refs[2 * nt:]
        x, y, c, chips = _place()
        sib = (x, y, 1 - c)
        sends = []
        for t in range(nt):
            half = outs[t].shape[1] // 2
            mine = pl.ds(pl.multiple_of(c * half, 16), half)
            for j, (px, py) in enumerate(chips):
                blk = outs[t].at[2 * px + py, mine]
                cp = _remote(blk, blk, dsend.at[t, j], drecv.at[t, j], sib)
                cp.start()
                sends.append(cp)
        for t in range(nt):
            half = outs[t].shape[1] // 2
            other = pl.ds(pl.multiple_of((1 - c) * half, 16), half)
            for j, (px, py) in enumerate(chips):
                blk = outs[t].at[2 * px + py, other]
                _remote(blk, blk, dsend.at[t, j], drecv.at[t, j], sib).wait_recv()
        for cp in sends:
            cp.wait_send()

    return pl.pallas_call(
        body, name="gather_finish_" + tag, in_specs=[ANY] * nt, out_specs=[ANY] * nt,
        out_shape=[jax.ShapeDtypeStruct(a.shape, a.dtype) for a in lands],
        input_output_aliases={t: t for t in range(nt)},
        scratch_shapes=[pltpu.SemaphoreType.DMA((nt, 3)), pltpu.SemaphoreType.DMA((nt, 3))],
        compiler_params=_cp())(*lands)


def _chip_partial(owns, recv, chip_arr, tag):
    nt = len(owns)

    def body(chip_ref, *refs):
        k = pl.program_id(0)
        for t in range(nt):
            p = refs[t][...] + refs[nt + t][...].astype(F32)
            refs[2 * nt + t][...] = p.astype(BF16)

            @pl.when(k == chip_ref[0])
            def _():
                refs[3 * nt + t][...] = p

    blk_specs = [pl.BlockSpec((None,) + a.shape[1:], lambda k, chip_ref: (k, 0, 0)) for a in owns]
    own_specs = [pl.BlockSpec(a.shape[1:], lambda k, chip_ref: (0, 0)) for a in owns]
    return pl.pallas_call(
        body, name="rs_chip_partial_" + tag,
        grid_spec=pltpu.PrefetchScalarGridSpec(num_scalar_prefetch=1, grid=(NSH,), in_specs=blk_specs * 2,
                                               out_specs=blk_specs + own_specs),
        out_shape=[jax.ShapeDtypeStruct(a.shape, BF16) for a in owns]
        + [jax.ShapeDtypeStruct(a.shape[1:], F32) for a in owns],
        compiler_params=_cp(1))(chip_arr, *owns, *recv)


def _sum_chips(own, recv, tag, after=()):
    nt = len(own)

    def body(*refs):
        outs = refs[2 * nt + len(after):]
        for t in range(nt):
            r = refs[nt + t]
            outs[t][...] = ((refs[t][...] + r[0].astype(F32)) + r[1].astype(F32)) + r[2].astype(F32)

    vm = pl.BlockSpec(memory_space=pltpu.VMEM)
    return pl.pallas_call(
        body, name="rs_sum_chips_" + tag, in_specs=[vm] * (2 * nt) + [ANY] * len(after), out_specs=[vm] * nt,
        out_shape=[jax.ShapeDtypeStruct(a.shape, F32) for a in own],
        compiler_params=_cp())(*own, *recv, *after)


def _adamw_math(w, g, m, v):
    m = ADAM_B1 * m + (1.0 - ADAM_B1) * g
    v = ADAM_B2 * v + (1.0 - ADAM_B2) * (g * g)
    m_hat = m / (1.0 - ADAM_B1 ** ADAM_STEP)
    v_hat = v / (1.0 - ADAM_B2 ** ADAM_STEP)
    delta = -ADAM_LR * (m_hat / (jnp.sqrt(v_hat) + ADAM_EPS) + ADAM_WD * w)
    return delta, m, v


ADAM_SPLIT = 4


def _adamw_shards(own, sib, ws, ms, vs, c_arr, tag):
    nt = len(own)

    def body(c_ref, *refs):
        hh = pl.program_id(0)
        mine = hh == c_ref[0]
        for t in range(nt):
            g = jnp.where(mine, refs[t][...], refs[nt + t][...])
            delta, m, v = _adamw_math(refs[2 * nt + t][...], g, refs[3 * nt + t][...], refs[4 * nt + t][...])
            refs[5 * nt + t][...] = g
            refs[6 * nt + t][...] = delta
            refs[7 * nt + t][...] = m
            refs[8 * nt + t][...] = v

    def tile(a):
        return (a.shape[0] // ADAM_SPLIT, a.shape[1])

    half_specs = [pl.BlockSpec(tile(a), lambda hh, q, c_ref: (q, 0)) for a in own]
    full_specs = [pl.BlockSpec(tile(a), lambda hh, q, c_ref: (hh * ADAM_SPLIT + q, 0)) for a in own]
    return pl.pallas_call(
        body, name="adamw_shards_" + tag,
        grid_spec=pltpu.PrefetchScalarGridSpec(num_scalar_prefetch=1, grid=(2, ADAM_SPLIT),
                                               in_specs=half_specs * 2 + full_specs * 3, out_specs=full_specs * 4),
        out_shape=[jax.ShapeDtypeStruct(w.shape, F32) for w in ws] * 4,
        compiler_params=_cp(2))(c_arr, *own, *sib, *ws, *ms, *vs)


SMALL_WPOOL_ROW = 32
SMALL_SCALE_ROW = SMALL_WPOOL_ROW + 4 * GROUP
SMALL_BIAS_ROW = SMALL_SCALE_ROW + 8
SMALL_SINKS_ROW = SMALL_BIAS_ROW + NQ
SMALL_LOSS_ROW = SMALL_SINKS_ROW + 8


def _small_sum_adamw(gathered, wp, mp, vp):
    rows = wp.shape[0]

    def body(g_ref, w_ref, m_ref, v_ref, *rest):
        outs, res = rest[:-1], rest[-1]
        g = g_ref[0]
        for d in range(1, 8):
            g = g + g_ref[d]
        delta, m, v = _adamw_math(w_ref[...], g, m_ref[...], v_ref[...])
        for kind, val in enumerate((g, delta, m, v)):
            res[kind] = val
            gains = outs[8 * kind:8 * kind + 4]
            w_pool_o, scale_o, bias_o, sinks_o = outs[8 * kind + 4:8 * kind + 8]
            for t in range(4):
                for i in range(8):
                    gains[t][:, 128 * i:128 * (i + 1)] = res[kind, pl.ds(8 * t + i, 1), :]
            for grp in range(4):
                w_pool_o[grp] = res[kind, pl.ds(SMALL_WPOOL_ROW + GROUP * grp, GROUP), :]
            for i in range(4):
                scale_o[:, 128 * i:128 * (i + 1)] = res[kind, pl.ds(SMALL_SCALE_ROW + i, 1), :]
            bias_o[...] = res[kind, pl.ds(SMALL_BIAS_ROW, NQ), :][:, 0:NBUCKET]
            sinks_o[...] = res[kind, pl.ds(SMALL_SINKS_ROW, 1), :][:, 0:NQ]
        outs[-1][...] = res[0, pl.ds(SMALL_LOSS_ROW, 1), :]

    vm = pl.BlockSpec(memory_space=pltpu.VMEM)
    one_kind = [jax.ShapeDtypeStruct((1, D), F32)] * 4 + [
        jax.ShapeDtypeStruct((4, GROUP, GROUP), F32), jax.ShapeDtypeStruct((1, POOL_W), F32),
        jax.ShapeDtypeStruct((NQ, NBUCKET), F32), jax.ShapeDtypeStruct((1, NQ), F32)]
    return pl.pallas_call(
        body, name="small_sum_adamw", in_specs=[vm] * 4, out_specs=[vm] * 33,
        out_shape=one_kind * 4 + [jax.ShapeDtypeStruct((1, 128), F32)],
        scratch_shapes=[pltpu.VMEM((4, rows, 128), F32)],
        compiler_params=_cp())(gathered, wp, mp, vp)


def _pack_small(gains4, w_pool, pool_scale, rel_bias_t, sinks, loss):
    bias_rows = jnp.pad(rel_bias_t, ((0, 0), (0, 128 - rel_bias_t.shape[1])))
    parts = list(gains4) + [w_pool, pool_scale, bias_rows, sinks, loss]
    rows = []
    for a in parts:
        flat = a.reshape(-1)
        n = flat.shape[0]
        padded = -(-n // 1024) * 1024
        rows.append(jnp.pad(flat, (0, padded - n)).reshape(-1, 128))
    return jnp.concatenate(rows, axis=0)


def kernel(x, g_pre_mix, w_in, w_pool, pool_scale, rel_bias, sinks, w_out, g_post_mix, g_pre_ffn, w_gate, w_up, w_down, g_post_ffn, loss_target, m_g_pre_mix, m_w_in, m_w_pool, m_pool_scale, m_rel_bias, m_sinks, m_w_out, m_g_post_mix, m_g_pre_ffn, m_w_gate, m_w_up, m_w_down, m_g_post_ffn, v_g_pre_mix, v_w_in, v_w_pool, v_pool_scale, v_rel_bias, v_sinks, v_w_out, v_g_post_mix, v_g_pre_ffn, v_w_gate, v_w_up, v_w_down, v_g_post_ffn):
    c = lax.axis_index("c")
    chip = 2 * lax.axis_index("x") + lax.axis_index("y")

    def shards(a_in, a_out, a_gate, a_up, a_down):
        return (a_in[0].T, a_out[0], a_gate[0].T, a_up[0].T, a_down[0])

    c_arr = c.reshape(1).astype(jnp.int32)
    chip_arr = chip.reshape(1).astype(jnp.int32)

    big_w = shards(w_in, w_out, w_gate, w_up, w_down)
    big_bf = [w.astype(BF16) for w in big_w]
    g_ssem, g_rsem, g_srcs, g_lands, _ = _split_start(
        _gather_copies, 15, big_bf, [lax.empty((NSH,) + a.shape, BF16) for a in big_bf], None, "gather_start")
    fw = {}

    def with_own(land, shard):
        return lax.dynamic_update_slice(land, shard[None], (chip, 0, 0))

    def w_in_ready(*after):
        fw["first"] = _split_wait(_gather_copies, g_ssem, g_rsem, g_srcs, g_lands, after, "gather_win_wait",
                                  only=(0, 1, 2))
        (fw["win"],) = _gather_finish([with_own(fw["first"][5], fw["first"][0])], "win")
        return fw["win"].reshape(IN_W, D)

    def w_out_ready(*after):
        first = fw["first"]
        fw["second"] = _split_wait(_gather_copies, g_ssem, g_rsem, first[:5], [fw["win"]] + list(first[6:]), after,
                                   "gather_wout_wait", only=(3, 4, 5))
        (fw["wout"],) = _gather_finish([with_own(fw["second"][6], fw["second"][1])], "wout")
        return fw["wout"].reshape(D, D), None

    def ffn_weights(after):
        second = fw["second"]
        outs = _split_wait(_gather_copies, g_ssem, g_rsem, second[:5], [second[5], fw["wout"]] + list(second[7:]),
                           [after], "gather_ffn_wait", only=tuple(range(6, 15)))
        return _gather_finish([with_own(land, src) for src, land in zip(outs[2:5], outs[7:])], "ffn")

    rs = {}

    def on_ffn_grads(ffn_g):
        oths = ffn_g[1::2]
        rs["ffn_own"] = ffn_g[0::2]
        rs["x"] = _split_start(_sibling_copies, 3, oths, [lax.empty(a.shape, BF16) for a in oths], ffn_g[0],
                               "rs_exchange_ffn_start")
        return rs["x"][4]

    def on_wout_grads(own, oth, after):
        ssem, rsem, srcs, lands, _ = rs["x"]
        recv_ffn = _split_wait(_sibling_copies, ssem, rsem, srcs, lands, [after], "rs_exchange_ffn_wait")[3:]
        recv_wout = _to_sibling([oth], "rs_exchange_wout")
        outs = _chip_partial([own] + list(rs["ffn_own"]), list(recv_wout) + list(recv_ffn), chip_arr, "early")
        rs["early_own"] = outs[4:]
        rs["s"] = _split_start(_scatter_copies, 12, outs[:4],
                               [lax.empty((3,) + a.shape[1:], BF16) for a in outs[:4]], outs[4], "scatter_early_start")
        return rs["s"][4]

    small = (g_pre_mix, g_post_mix, g_pre_ffn, g_post_ffn, w_pool[0], pool_scale, rel_bias, sinks)
    loss, gx, small_grads, ((win_own, win_oth), _, _) = _local_step(
        x[0], loss_target[0], small, w_in_ready, w_out_ready, ffn_weights, c_arr, on_ffn_grads, on_wout_grads)

    ssem, rsem, srcs, lands, _ = rs["s"]
    recv_early = _split_wait(_scatter_copies, ssem, rsem, srcs, lands, [gx], "scatter_early_wait")[4:]
    finals = _sum_chips(list(rs["early_own"]), list(recv_early), "early")
    sh_ssem, sh_rsem, sh_srcs, sh_lands, sh_token = _split_start(
        _sibling_copies, 4, finals, [lax.empty(a.shape, F32) for a in finals], finals[0], "rs_share_early_start")

    unused = jnp.zeros((1, 1), F32)
    gp = _pack_small(small_grads[:4], *small_grads[4:], loss)
    wp = _pack_small((g_pre_mix, g_post_mix, g_pre_ffn, g_post_ffn), w_pool, pool_scale, rel_bias.T, sinks, unused)
    mp = _pack_small((m_g_pre_mix, m_g_post_mix, m_g_pre_ffn, m_g_post_ffn), m_w_pool, m_pool_scale, m_rel_bias.T,
                     m_sinks, unused)
    vp = _pack_small((v_g_pre_mix, v_g_post_mix, v_g_pre_ffn, v_g_post_ffn), v_w_pool, v_pool_scale, v_rel_bias.T,
                     v_sinks, unused)

    moments = (shards(m_w_in, m_w_out, m_w_gate, m_w_up, m_w_down),
               shards(v_w_in, v_w_out, v_w_gate, v_w_up, v_w_down))
    recv_a = _to_sibling([win_oth], "rs_exchange_win", [sh_token])
    outs = _chip_partial([win_own], recv_a, chip_arr, "win")
    w_ssem, w_rsem, w_srcs, w_lands, w_token = _split_start(
        _scatter_copies, 3, outs[:1], [lax.empty((3,) + outs[0].shape[1:], BF16)], gx, "scatter_win_start")
    slots = lax.dynamic_update_slice(lax.empty((8,) + gp.shape, F32), gp[None], (2 * chip + c, 0, 0))
    p_ssem, p_rsem, p_srcs, p_lands, p_token = _split_start(_peer_copies, 7, [gp], [slots], w_token,
                                                            "small_allgather_start")
    shared = _split_wait(_sibling_copies, sh_ssem, sh_rsem, sh_srcs, sh_lands, [p_token], "rs_share_early_wait")
    adam_e = _adamw_shards(shared[:4], shared[4:], big_w[1:], moments[0][1:], moments[1][1:], c_arr, "early")
    recv_win = _split_wait(_scatter_copies, w_ssem, w_rsem, w_srcs, w_lands, [adam_e[0]], "scatter_win_wait")[1:]
    win_final = _sum_chips([outs[1]], recv_win, "win")
    win_sib = _to_sibling(win_final, "rs_share_win")
    adam_w = _adamw_shards(win_final, win_sib, big_w[:1], moments[0][:1], moments[1][:1], c_arr, "win")
    big_g, big_d, big_m, big_v = [[adam_w[i]] + list(adam_e[4 * i:4 * i + 4]) for i in range(4)]

    gathered = _split_wait(_peer_copies, p_ssem, p_rsem, p_srcs, p_lands, [adam_w[0]], "small_allgather_wait")[1]
    flat = _small_sum_adamw(gathered, wp, mp, vp)
    small_out = [flat[8 * kind:8 * kind + 8] for kind in range(4)]
    total_loss = flat[-1][0, 0]

    def ordered(small8, big4):
        sg1, sg2, sg3, sg4, swp, ssc, srb, ssk = small8
        swp, srb = swp[None], srb.T
        bwin, bwout, bwg, bwu, bwd = big4[0].T[None], big4[1][None], big4[2].T[None], big4[3].T[None], big4[4][None]
        return [sg1, bwin, swp, ssc, srb, ssk, bwout, sg2, sg3, bwg, bwu, bwd, sg4]

    res = [total_loss, gx[None]]
    for sm, bg in zip(small_out, (big_g, big_d, big_m, big_v)):
        res += ordered(sm, bg)
    return tuple(res)
```

```python
import numpy as np
import jax
import jax.numpy as jnp
from jax import lax
from jax.experimental import pallas as pl
from jax.experimental.pallas import tpu as pltpu

F32 = jnp.float32
BF16 = jnp.bfloat16

D = 1024
POOL_W = 512
GROUP = 128
WINDOWS = (2, 4, 8, 16)
HALO = 128
NQ = 8
BLK = 128
NBUCKET = 32
IN_W = 1280
DFF = 2816
NSH = 4
FS = DFF // NSH
WIN_S = IN_W // NSH
WOUT_S = D // NSH
EPS = 1e-6
NEG = -1e30
SCALE = 0.125

ADAM_LR = 0.001
ADAM_B1 = 0.9
ADAM_B2 = 0.999
ADAM_EPS = 1e-08
ADAM_WD = 0.01
ADAM_STEP = 10

TM = 512
TM_POOL = 512
TM_FFN = 512
TM_FFN_FWD = 1024
FFN_ROWS = 256
VMEM_LIMIT = 60 * 1024 * 1024

MESH_T = pl.DeviceIdType.MESH
ANY = pl.BlockSpec(memory_space=pl.ANY)


def _cp(n_grid=0, **kw):
    sem = ("arbitrary",) * n_grid if n_grid else None
    return pltpu.CompilerParams(dimension_semantics=sem, vmem_limit_bytes=VMEM_LIMIT, **kw)


def _dot(a, b):
    return jnp.dot(a, b, preferred_element_type=F32)


def _dot_nt(a, b):
    return lax.dot_general(a, b, (((1,), (1,)), ((), ())), preferred_element_type=F32)


def _dot_tn(a, b):
    return lax.dot_general(a, b, (((0,), (0,)), ((), ())), preferred_element_type=F32)


def _rms(x):
    r = lax.rsqrt(jnp.mean(x * x, axis=-1, keepdims=True) + EPS)
    return r, x * r


def _rms_bwd(r, n, g, dout):
    dn = dout * g
    dx = r * (dn - n * jnp.mean(dn * n, axis=-1, keepdims=True))
    dg = jnp.sum(dout * n, axis=0, keepdims=True)
    return dx, dg


def _split(x, n):
    parts = []
    r = x
    for _ in range(n):
        p = r.astype(BF16)
        parts.append(p)
        r = r - p.astype(F32)
    return parts


def _band_dot(a, x, n):
    acc = None
    for p in _split(x, n):
        t = _dot(a, p)
        acc = t if acc is None else acc + t
    return acc


def _bucket_table():
    qi = np.arange(BLK)[None, :]
    kj = np.arange(2 * BLK)[:, None]
    dist = qi + BLK - kj
    n = np.maximum(dist, 0)
    nf = np.maximum(n, 1).astype(np.float32)
    large = 16 + (np.log(nf / np.float32(16)) / np.float32(np.log(128 / 16)) * np.float32(16)).astype(np.int32)
    large = np.minimum(large, NBUCKET - 1)
    return np.where(n < 16, n, large).astype(np.int32)


def _prenorm(x, g1):
    s = x.shape[0]
    tm = min(TM, s)

    def body(x_ref, g_ref, h_ref):
        _, n = _rms(x_ref[...])
        h_ref[...] = (n * g_ref[...]).astype(BF16)

    row = pl.BlockSpec((tm, D), lambda i: (i, 0))
    return pl.pallas_call(
        body, name="prenorm", grid=(s // tm,),
        in_specs=[row, pl.BlockSpec((1, D), lambda i: (0, 0))], out_specs=row,
        out_shape=jax.ShapeDtypeStruct((s, D), BF16),
        compiler_params=_cp(1))(x, g1)


def _inproj_fwd(h1, w_in):
    s = h1.shape[0]
    tm = min(TM, s)

    def body(h_ref, w_ref, u_ref, q_ref, k_ref, v_ref):
        proj = _dot_nt(h_ref[...], w_ref[...])
        u_ref[...] = proj[:, :512]
        q_ref[...] = proj[:, 512:1024].astype(BF16)
        k_ref[...] = proj[:, 1024:1152].astype(BF16)
        v_ref[...] = proj[:, 1152:1280].astype(BF16)

    row = lambda w: pl.BlockSpec((tm, w), lambda i: (i, 0))
    return pl.pallas_call(
        body, name="inproj_fwd", grid=(s // tm,),
        in_specs=[row(D), pl.BlockSpec((IN_W, D), lambda i: (0, 0))],
        out_specs=[row(512), row(512), row(128), row(128)],
        out_shape=[jax.ShapeDtypeStruct((s, 512), F32), jax.ShapeDtypeStruct((s, 512), BF16),
                   jax.ShapeDtypeStruct((s, 128), BF16), jax.ShapeDtypeStruct((s, 128), BF16)],
        compiler_params=_cp(1))(h1, w_in)


def _window_sums(ext, w, lag_sign, tm, terms):
    rows = lax.broadcasted_iota(jnp.int32, (HALO, 2 * HALO), 0)
    cols = lax.broadcasted_iota(jnp.int32, (HALO, 2 * HALO), 1)
    d = rows + HALO - cols if lag_sign > 0 else cols - rows
    band = jnp.where((d >= 0) & (d < w), 1.0, 0.0).astype(BF16)
    return jnp.concatenate([_band_dot(band, ext[r:r + 2 * HALO], terms) for r in range(0, tm, HALO)], axis=0)


def _pooled(ext, cur, t0, tm):
    t = t0 + lax.broadcasted_iota(jnp.int32, (tm, GROUP), 0)
    out = []
    for g, w in enumerate(WINDOWS):
        sl = slice(g * GROUP, (g + 1) * GROUP)
        cnt = jnp.minimum(t + 1, w).astype(F32)
        out.append(_window_sums(ext[:, sl], w, 1, tm, 2) / cnt - cur[:, sl])
    return out


def _pool_fwd(u, w_pool, pool_scale):
    s = u.shape[0]
    tm = min(TM_POOL, s)
    hb = tm // HALO

    def body(uc_ref, uh_ref, wp_ref, sc_ref, o_ref, pooled_ref):
        i = pl.program_id(0)
        cur = uc_ref[...]
        halo = jnp.where(i > 0, uh_ref[...], 0.0)
        ext = jnp.concatenate([halo, cur], axis=0)
        pooled = _pooled(ext, cur, i * tm, tm)
        for g in range(4):
            sl = slice(g * GROUP, (g + 1) * GROUP)
            pb = pooled[g].astype(BF16)
            pooled_ref[:, sl] = pb
            o_ref[:, sl] = (_dot(pb, wp_ref[g]) * sc_ref[:, sl]).astype(BF16)

    row = pl.BlockSpec((tm, 512), lambda i: (i, 0))
    return pl.pallas_call(
        body, name="pool_fwd", grid=(s // tm,),
        in_specs=[row, pl.BlockSpec((HALO, 512), lambda i: (jnp.maximum(i * hb - 1, 0), 0)),
                  pl.BlockSpec((4, GROUP, GROUP), lambda i: (0, 0, 0)),
                  pl.BlockSpec((1, 512), lambda i: (0, 0))],
        out_specs=[row, row],
        out_shape=[jax.ShapeDtypeStruct((s, 512), BF16)] * 2,
        compiler_params=_cp(1))(u, u, w_pool, pool_scale)


def _bias_table(bucket, rel_bias):
    def body(b_ref, rb_ref, o_ref):
        bucket_v = b_ref[...]
        key = lax.broadcasted_iota(jnp.int32, (2 * BLK, BLK), 0)
        dist = lax.broadcasted_iota(jnp.int32, (2 * BLK, BLK), 1) + BLK - key
        inwin = (dist >= 0) & (dist < BLK)
        for h in range(NQ):
            acc = jnp.zeros((2 * BLK, BLK), F32)
            for b in range(NBUCKET):
                acc = jnp.where(bucket_v == b, rb_ref[b, h], acc)
            rest = jnp.where(inwin, acc, NEG)
            o_ref[1, h] = rest
            o_ref[0, h] = jnp.where(key >= BLK, rest, NEG)

    return pl.pallas_call(
        body, name="bias_table",
        in_specs=[pl.BlockSpec(memory_space=pltpu.VMEM), pl.BlockSpec(memory_space=pltpu.SMEM)],
        out_specs=pl.BlockSpec(memory_space=pltpu.VMEM),
        out_shape=jax.ShapeDtypeStruct((2, NQ, 2 * BLK, BLK), F32),
        compiler_params=_cp())(bucket, rel_bias)


HD = 64


def _kv_band(ref, n, transposed):
    pstart = pl.multiple_of(jnp.maximum(n - 1, 0) * BLK, BLK)
    cstart = pl.multiple_of(n * BLK, BLK)
    lo = lax.broadcasted_iota(jnp.int32, (2 * BLK, 128), 1) < HD
    band = jnp.concatenate([ref[pl.ds(pstart, BLK), :], ref[pl.ds(cstart, BLK), :]], axis=0).astype(F32)
    rot = pltpu.roll(band, HD, axis=1)
    halves = ((jnp.where(lo, band, 0.0), jnp.where(lo, 0.0, rot)), (jnp.where(lo, rot, 0.0), jnp.where(lo, 0.0, band)))
    if transposed:
        out = [jnp.concatenate([a.T, b.T], axis=1).astype(BF16) for a, b in halves]
    else:
        out = [jnp.concatenate([a, b], axis=0).astype(BF16) for a, b in halves]
    return out, (lo, pstart, cstart)


def _pair_probs(qt, kk, bias, sk_ref, p):
    sink = jnp.concatenate([jnp.full((1, 1, BLK), sk_ref[0, 2 * p + e], F32) for e in range(2)], axis=0)
    s = _dot_nt(kk, qt).reshape(2, 2 * BLK, BLK) + bias
    m = jnp.maximum(jnp.max(s, axis=1, keepdims=True), sink)
    pr = jnp.exp(s - m)
    es = jnp.exp(sink - m)
    inv = 1.0 / (jnp.sum(pr, axis=1, keepdims=True) + es)
    return pr * inv, es * inv


def _attn_fwd(q, k, v, bias, sinks):
    s = q.shape[0]
    nblk = s // BLK

    def body(q_ref, k_ref, v_ref, b_ref, sk_ref, o_ref, prob_ref, ps_ref):
        n = pl.program_id(0)
        kk, _ = _kv_band(k_ref, n, False)
        vt, _ = _kv_band(v_ref, n, True)
        bidx = jnp.minimum(n, 1)
        for p in range(4):
            h = p // 2
            qt = (q_ref[:, p * 128:(p + 1) * 128].astype(F32) * SCALE).astype(BF16)
            prob, ps = _pair_probs(qt, kk[h], b_ref[bidx, pl.ds(2 * p, 2)], sk_ref, p)
            pb = prob.astype(BF16)
            prob_ref[pl.ds(2 * p, 2)] = pb
            ps_ref[pl.ds(2 * p, 2), :] = ps.reshape(2, BLK)
            acc_t = _dot(vt[h], pb.reshape(4 * BLK, BLK))
            o_ref[:, p * 128:(p + 1) * 128] = acc_t.T.astype(BF16)

    return pl.pallas_call(
        body, name="attn_fwd", grid=(nblk,),
        in_specs=[pl.BlockSpec((BLK, 512), lambda i: (i, 0)),
                  pl.BlockSpec((s, 128), lambda i: (0, 0)),
                  pl.BlockSpec((s, 128), lambda i: (0, 0)),
                  pl.BlockSpec((2, NQ, 2 * BLK, BLK), lambda i: (0, 0, 0, 0)),
                  pl.BlockSpec(memory_space=pltpu.SMEM)],
        out_specs=[pl.BlockSpec((BLK, 512), lambda i: (i, 0)),
                   pl.BlockSpec((None, NQ, 2 * BLK, BLK), lambda i: (i, 0, 0, 0)),
                   pl.BlockSpec((None, NQ, BLK), lambda i: (i, 0, 0))],
        out_shape=[jax.ShapeDtypeStruct((s, 512), BF16), jax.ShapeDtypeStruct((nblk, NQ, 2 * BLK, BLK), BF16),
                   jax.ShapeDtypeStruct((nblk, NQ, BLK), F32)],
        compiler_params=_cp(1))(q, k, v, bias, sinks)


def _outproj_fwd(pool_o, attn_o, w_out, x, g2, g3):
    s = x.shape[0]
    tm = min(TM, s)

    def body(p_ref, a_ref, w_ref, x_ref, g2_ref, g3_ref, mix_ref, x1_ref, h2_ref):
        mix = _dot(p_ref[...], w_ref[0:512, :]) + _dot(a_ref[...], w_ref[512:1024, :])
        mix_ref[...] = mix
        _, n2 = _rms(mix)
        x1 = x_ref[...] + n2 * g2_ref[...]
        x1_ref[...] = x1
        _, n3 = _rms(x1)
        h2_ref[...] = (n3 * g3_ref[...]).astype(BF16)

    row = lambda w: pl.BlockSpec((tm, w), lambda i: (i, 0))
    vec = pl.BlockSpec((1, D), lambda i: (0, 0))
    return pl.pallas_call(
        body, name="outproj_fwd", grid=(s // tm,),
        in_specs=[row(512), row(512), pl.BlockSpec((D, D), lambda i: (0, 0)), row(D), vec, vec],
        out_specs=[row(D), row(D), row(D)],
        out_shape=[jax.ShapeDtypeStruct((s, D), F32), jax.ShapeDtypeStruct((s, D), F32),
                   jax.ShapeDtypeStruct((s, D), BF16)],
        compiler_params=_cp(1))(pool_o, attn_o, w_out, x, g2, g3)


def _silu_parts(g):
    sg = jax.nn.sigmoid(g)
    return sg, g * sg


def _ffn_fwd(h2, wg, wu, wd, x1, target, g4):
    s = h2.shape[0]
    tm = min(TM_FFN_FWD, s)

    def body(h_ref, wg_ref, wu_ref, wd_ref, x1_ref, t_ref, g4_ref,
             gate_ref, up_ref, a_ref, df_ref, dy_ref, loss_ref, gg4_ref, f_acc):
        i = pl.program_id(0)
        j = pl.program_id(1)
        first = j == 0
        chunks = [pl.ds(r, min(FFN_ROWS, tm)) for r in range(0, tm, FFN_ROWS)]
        project = lambda rows: (_dot_nt(h_ref[rows, :], wg_ref[...]), _dot_nt(h_ref[rows, :], wu_ref[...]))
        ahead = [project(chunks[0])]
        for k, rows in enumerate(chunks):
            if k + 1 < len(chunks):
                ahead.append(project(chunks[k + 1]))
            gate, up = ahead[k]
            gate_ref[rows, :] = gate.astype(BF16)
            up_ref[rows, :] = up.astype(BF16)
            _, sl = _silu_parts(gate)
            a = (sl * up).astype(BF16)
            a_ref[rows, :] = a
            contrib = _dot(a, wd_ref[...])
            f_acc[rows, :] = jnp.where(first, contrib, f_acc[rows, :] + contrib)

        @pl.when((i == 0) & (j == 0))
        def _():
            loss_ref[...] = jnp.zeros_like(loss_ref)
            gg4_ref[...] = jnp.zeros_like(gg4_ref)

        @pl.when(j == NSH - 1)
        def _():
            r4, n4 = _rms(f_acc[...])
            g4v = g4_ref[...]
            err = x1_ref[...] + n4 * g4v - t_ref[...]
            loss_ref[...] += (0.5 / D) * jnp.sum(err * err).reshape(1, 1)
            dy = err * (1.0 / D)
            dy_ref[...] = dy
            df, dg = _rms_bwd(r4, n4, g4v, dy)
            gg4_ref[...] += dg
            df_ref[...] = df.astype(BF16)

    row = lambda w: pl.BlockSpec((tm, w), lambda i, j: (i, 0))
    sh = lambda r, c: pl.BlockSpec((None, r, c), lambda i, j: (j, 0, 0))
    act = pl.BlockSpec((None, tm, FS), lambda i, j: (j, i, 0))
    vec = pl.BlockSpec((1, D), lambda i, j: (0, 0))
    return pl.pallas_call(
        body, name="ffn_fwd", grid=(s // tm, NSH),
        in_specs=[row(D), sh(FS, D), sh(FS, D), sh(FS, D), row(D), row(D), vec],
        out_specs=[act, act, act, row(D), row(D), pl.BlockSpec((1, 1), lambda i, j: (0, 0)), vec],
        out_shape=[jax.ShapeDtypeStruct((NSH, s, FS), BF16)] * 3
        + [jax.ShapeDtypeStruct((s, D), BF16), jax.ShapeDtypeStruct((s, D), F32),
           jax.ShapeDtypeStruct((1, 1), F32), jax.ShapeDtypeStruct((1, D), F32)],
        scratch_shapes=[pltpu.VMEM((tm, D), F32)],
        compiler_params=_cp(2))(h2, wg, wu, wd, x1, target, g4)


def _ffn_bwd_act(df, gate, up, wg, wu, wd, dy, x1, mix, g3, g2):
    s = df.shape[0]
    tm = min(TM_FFN, s)

    def body(df_ref, gate_ref, up_ref, wg_ref, wu_ref, wd_ref, dy_ref, x1_ref, mix_ref, g3_ref, g2_ref,
             dgate_ref, dup_ref, dx1_ref, dmix_ref, gg3_ref, gg2_ref, acc):
        i = pl.program_id(0)
        j = pl.program_id(1)
        first = j == 0
        chunks = [pl.ds(r, min(FFN_ROWS, tm)) for r in range(0, tm, FFN_ROWS)]
        das = [_dot_nt(df_ref[chunks[0], :], wd_ref[...])]
        for k, rows in enumerate(chunks):
            if k + 1 < len(chunks):
                das.append(_dot_nt(df_ref[chunks[k + 1], :], wd_ref[...]))
            da = das[k]
            g = gate_ref[rows, :].astype(F32)
            u = up_ref[rows, :].astype(F32)
            sg, sl = _silu_parts(g)
            dgate = (da * u * (sg * (1.0 + g * (1.0 - sg)))).astype(BF16)
            dup = (da * sl).astype(BF16)
            dgate_ref[rows, :] = dgate
            dup_ref[rows, :] = dup
            contrib = _dot(dgate, wg_ref[...]) + _dot(dup, wu_ref[...])
            acc[rows, :] = jnp.where(first, contrib, acc[rows, :] + contrib)

        @pl.when((i == 0) & (j == 0))
        def _():
            gg3_ref[...] = jnp.zeros_like(gg3_ref)
            gg2_ref[...] = jnp.zeros_like(gg2_ref)

        @pl.when(j == NSH - 1)
        def _():
            r3, n3 = _rms(x1_ref[...])
            dx1n, dg3 = _rms_bwd(r3, n3, g3_ref[...], acc[...])
            dx1 = dy_ref[...] + dx1n
            dx1_ref[...] = dx1
            gg3_ref[...] += dg3
            r2, n2 = _rms(mix_ref[...])
            dmix, dg2 = _rms_bwd(r2, n2, g2_ref[...], dx1)
            gg2_ref[...] += dg2
            dmix_ref[...] = dmix.astype(BF16)

    row = lambda w: pl.BlockSpec((tm, w), lambda i, j: (i, 0))
    sh = lambda r, c: pl.BlockSpec((None, r, c), lambda i, j: (j, 0, 0))
    act = pl.BlockSpec((None, tm, FS), lambda i, j: (j, i, 0))
    vec = pl.BlockSpec((1, D), lambda i, j: (0, 0))
    return pl.pallas_call(
        body, name="ffn_bwd_act", grid=(s // tm, NSH),
        in_specs=[row(D), act, act, sh(FS, D), sh(FS, D), sh(FS, D), row(D), row(D), row(D), vec, vec],
        out_specs=[act, act, row(D), row(D), vec, vec],
        out_shape=[jax.ShapeDtypeStruct((NSH, s, FS), BF16), jax.ShapeDtypeStruct((NSH, s, FS), BF16),
                   jax.ShapeDtypeStruct((s, D), F32), jax.ShapeDtypeStruct((s, D), BF16),
                   jax.ShapeDtypeStruct((1, D), F32), jax.ShapeDtypeStruct((1, D), F32)],
        scratch_shapes=[pltpu.VMEM((tm, D), F32)],
        compiler_params=_cp(2))(df, gate, up, wg, wu, wd, dy, x1, mix, g3, g2)


SMEM_SPEC = pl.BlockSpec(memory_space=pltpu.SMEM)


def _emit_halves(acc_ref, row0, rows, c, own_ref, oth_ref):
    half = rows // 2
    own_ref[...] = acc_ref[pl.ds(row0 + pl.multiple_of(c * half, 8), half), :]
    oth_ref[...] = acc_ref[pl.ds(row0 + pl.multiple_of((1 - c) * half, 8), half), :].astype(BF16)


def _half_shapes(rows):
    return [jax.ShapeDtypeStruct((NSH, rows // 2, D), F32), jax.ShapeDtypeStruct((NSH, rows // 2, D), BF16)]


def _ffn_bwd_w(h2, act_a, dgate, dup, df, c_arr):
    s = h2.shape[0]
    tm = min(TM_FFN_FWD, s)
    nt = s // tm

    def body(c_ref, h_ref, a_ref, dgate_ref, dup_ref, df_ref, *rest):
        outs, accs = rest[:6], rest[6:]
        i = pl.program_id(1)

        @pl.when(i == 0)
        def _():
            for acc in accs:
                acc[...] = jnp.zeros_like(acc)

        h = h_ref[...]
        accs[0][...] += _dot_tn(dgate_ref[...], h)
        accs[1][...] += _dot_tn(dup_ref[...], h)
        accs[2][...] += _dot_tn(a_ref[...], df_ref[...])

        @pl.when(i == nt - 1)
        def _():
            for t, acc in enumerate(accs):
                _emit_halves(acc, 0, FS, c_ref[0], outs[2 * t], outs[2 * t + 1])

    row = lambda w: pl.BlockSpec((tm, w), lambda j, i: (i, 0))
    act = pl.BlockSpec((None, tm, FS), lambda j, i: (j, i, 0))
    half = pl.BlockSpec((None, FS // 2, D), lambda j, i: (j, 0, 0))
    return pl.pallas_call(
        body, name="ffn_bwd_w", grid=(NSH, nt),
        in_specs=[SMEM_SPEC, row(D), act, act, act, row(D)],
        out_specs=[half] * 6,
        out_shape=_half_shapes(FS) * 3,
        scratch_shapes=[pltpu.VMEM((FS, D), F32)] * 3,
        compiler_params=_cp(2))(c_arr, h2, act_a, dgate, dup, df)


def _outproj_bwd(dmix, w_out, pool_o, attn_o, c_arr, dep=None):
    s = dmix.shape[0]
    tm = min(TM, s)
    nt = s // tm

    def body(c_ref, dm_ref, w_ref, p_ref, a_ref, *rest):
        dpool_ref, dattn_ref, own_ref, oth_ref, gw_ref = rest[-5:]
        i = pl.program_id(0)

        @pl.when(i == 0)
        def _():
            gw_ref[...] = jnp.zeros_like(gw_ref)

        dm = dm_ref[...]
        dpool_ref[...] = _dot_nt(dm, w_ref[0:512, :])
        dattn_ref[...] = _dot_nt(dm, w_ref[512:1024, :]).astype(BF16)
        gw_ref[0:512, :] += _dot_tn(p_ref[...], dm)
        gw_ref[512:1024, :] += _dot_tn(a_ref[...], dm)

        @pl.when(i == nt - 1)
        def _():
            for k in range(NSH):
                _emit_halves(gw_ref, k * WOUT_S, WOUT_S, c_ref[0], own_ref.at[k], oth_ref.at[k])

    row = lambda w: pl.BlockSpec((tm, w), lambda i: (i, 0))
    full = pl.BlockSpec((D, D), lambda i: (0, 0))
    half = pl.BlockSpec((NSH, WOUT_S // 2, D), lambda i: (0, 0, 0))
    return pl.pallas_call(
        body, name="outproj_bwd", grid=(nt,),
        in_specs=[SMEM_SPEC, row(D), full, row(512), row(512)] + ([] if dep is None else [ANY]),
        out_specs=[row(512), row(512), half, half],
        out_shape=[jax.ShapeDtypeStruct((s, 512), F32), jax.ShapeDtypeStruct((s, 512), BF16)] + _half_shapes(WOUT_S),
        scratch_shapes=[pltpu.VMEM((D, D), F32)],
        compiler_params=_cp(1))(c_arr, dmix, w_out, pool_o, attn_o, *([] if dep is None else [dep]))


def _pool_bwd(pooled, dpool, w_pool, pool_scale, dep=None):
    s = pooled.shape[0]
    tm = min(TM_POOL, s)
    hb = tm // HALO
    nt = s // tm
    last_halo = s // HALO - 1

    def body(p_ref, dc_ref, dn_ref, wp_ref, sc_ref, *rest):
        du_ref, gwp_ref, gsc_ref = rest[-3:]
        i = pl.program_id(0)

        @pl.when(i == 0)
        def _():
            gwp_ref[...] = jnp.zeros_like(gwp_ref)
            gsc_ref[...] = jnp.zeros_like(gsc_ref)

        dcur = dc_ref[...]
        dnext = jnp.where(i < nt - 1, dn_ref[...], 0.0)
        dext = jnp.concatenate([dcur, dnext], axis=0)
        t_ext = i * tm + lax.broadcasted_iota(jnp.int32, (tm + HALO, GROUP), 0)
        for g, w in enumerate(WINDOWS):
            sl = slice(g * GROUP, (g + 1) * GROUP)
            pb = p_ref[:, sl]
            wp = wp_ref[g]
            mixed = _dot(pb, wp)
            gsc_ref[:, sl] += jnp.sum(dcur[:, sl] * mixed, axis=0, keepdims=True)
            dmixed = (dext[:, sl] * sc_ref[:, sl]).astype(BF16)
            gwp_ref[g] += _dot_tn(pb, dmixed[0:tm])
            dpooled = _dot_nt(dmixed, wp)
            z = dpooled / jnp.minimum(t_ext + 1, w).astype(F32)
            du_ref[:, sl] = (_window_sums(z, w, -1, tm, 2) - dpooled[0:tm]).astype(BF16)

    return pl.pallas_call(
        body, name="pool_bwd", grid=(nt,),
        in_specs=[pl.BlockSpec((tm, 512), lambda i: (i, 0)),
                  pl.BlockSpec((tm, 512), lambda i: (i, 0)),
                  pl.BlockSpec((HALO, 512), lambda i: (jnp.minimum((i + 1) * hb, last_halo), 0)),
                  pl.BlockSpec((4, GROUP, GROUP), lambda i: (0, 0, 0)),
                  pl.BlockSpec((1, 512), lambda i: (0, 0))] + ([] if dep is None else [ANY]),
        out_specs=[pl.BlockSpec((tm, 512), lambda i: (i, 0)),
                   pl.BlockSpec((4, GROUP, GROUP), lambda i: (0, 0, 0)),
                   pl.BlockSpec((1, 512), lambda i: (0, 0))],
        out_shape=[jax.ShapeDtypeStruct((s, 512), BF16), jax.ShapeDtypeStruct((4, GROUP, GROUP), F32),
                   jax.ShapeDtypeStruct((1, 512), F32)],
        compiler_params=_cp(1))(pooled, dpool, dpool, w_pool, pool_scale, *([] if dep is None else [dep]))


def _attn_bwd(q, k, v, do, probs, sink_share, bucket, dep=None):
    s = q.shape[0]
    nblk = s // BLK

    def body(q_ref, do_ref, k_ref, v_ref, prob_ref, ps_ref, bk_ref, *rest):
        dq_ref, dk_ref, dv_ref, grb_ref, gsk_ref, dss_ref = rest[-6:]
        n = pl.program_id(0)

        @pl.when(n == 0)
        def _():
            dk_ref[...] = jnp.zeros_like(dk_ref)
            dv_ref[...] = jnp.zeros_like(dv_ref)
            dss_ref[...] = jnp.zeros_like(dss_ref)
            gsk_ref[...] = jnp.zeros_like(gsk_ref)

        kt, (lo, pstart, cstart) = _kv_band(k_ref, n, True)
        vv, _ = _kv_band(v_ref, n, False)
        lo_q = lax.broadcasted_iota(jnp.int32, (BLK, 128), 1) < HD
        dkk = [jnp.zeros((2 * BLK, 128), F32), jnp.zeros((2 * BLK, 128), F32)]
        dvv = [jnp.zeros((2 * BLK, 128), F32), jnp.zeros((2 * BLK, 128), F32)]

        def by_head(t):
            return jnp.concatenate([jnp.where(lo_q, t, 0.0), jnp.where(lo_q, 0.0, t)], axis=0).astype(BF16)

        for p in range(4):
            h = p // 2
            qt = q_ref[:, p * 128:(p + 1) * 128].astype(F32) * SCALE
            dot = do_ref[:, p * 128:(p + 1) * 128]
            pb = prob_ref[pl.ds(2 * p, 2)]
            prob = pb.astype(F32)
            ps = ps_ref[pl.ds(2 * p, 2), :].reshape(2, 1, BLK)
            dp = _dot_nt(vv[h], dot).reshape(2, 2 * BLK, BLK)
            delta = jnp.sum(prob * dp, axis=1, keepdims=True)
            ds = prob * (dp - delta)
            sink_part = ps * delta
            for e in range(2):
                gs = -jnp.sum(sink_part[e]).reshape(1, 1)
                gsk_ref[pl.ds(2 * p + e, 1), :] += jnp.broadcast_to(gs, (1, 128))
            dss_ref[pl.ds(2 * p, 2)] += ds
            dsb = ds.astype(BF16)
            dq_t = _dot(kt[h], dsb.reshape(4 * BLK, BLK))
            dq_ref[:, p * 128:(p + 1) * 128] = (dq_t.T * SCALE).astype(BF16)
            dkk[h] = dkk[h] + _dot(jnp.concatenate([dsb[0], dsb[1]], axis=1), by_head(qt))
            dvv[h] = dvv[h] + _dot(jnp.concatenate([pb[0], pb[1]], axis=1), by_head(dot.astype(F32)))
        for acc, ref in ((dkk, dk_ref), (dvv, dv_ref)):
            f0 = acc[0] + pltpu.roll(acc[0], HD, axis=1)
            f1 = acc[1] + pltpu.roll(acc[1], HD, axis=1)
            band = jnp.where(lo, f0, f1)
            ref[pl.ds(pstart, BLK), :] += band[0:BLK]
            ref[pl.ds(cstart, BLK), :] += band[BLK:2 * BLK]

        @pl.when(n == nblk - 1)
        def _():
            _relbias_grad(dss_ref, bk_ref[...], grb_ref)

    blk = pl.BlockSpec((BLK, 512), lambda i: (i, 0))
    kv = pl.BlockSpec((s, 128), lambda i: (0, 0))
    row8 = pl.BlockSpec((NQ, 128), lambda i: (0, 0))
    return pl.pallas_call(
        body, name="attn_bwd", grid=(nblk,),
        in_specs=[blk, blk, kv, kv, pl.BlockSpec((None, NQ, 2 * BLK, BLK), lambda i: (i, 0, 0, 0)),
                  pl.BlockSpec((None, NQ, BLK), lambda i: (i, 0, 0)), pl.BlockSpec((2 * BLK, BLK), lambda i: (0, 0))]
        + ([] if dep is None else [ANY]),
        out_specs=[blk, kv, kv, row8, row8],
        out_shape=[jax.ShapeDtypeStruct((s, 512), BF16), jax.ShapeDtypeStruct((s, 128), F32),
                   jax.ShapeDtypeStruct((s, 128), F32), jax.ShapeDtypeStruct((NQ, 128), F32),
                   jax.ShapeDtypeStruct((NQ, 128), F32)],
        scratch_shapes=[pltpu.VMEM((NQ, 2 * BLK, BLK), F32)],
        compiler_params=_cp(1))(q, do, k, v, probs, sink_share, bucket, *([] if dep is None else [dep]))


def _relbias_grad(ds_ref, bucket_v, o_ref):
    lane = lax.broadcasted_iota(jnp.int32, (NQ, 128), 1)
    head_row = lax.broadcasted_iota(jnp.int32, (NQ, BLK), 0)
    acc = jnp.zeros((NQ, 128), F32)
    for b in range(NBUCKET):
        sel = bucket_v == b
        stack = jnp.zeros((NQ, BLK), F32)
        for h in range(NQ):
            col_sums = jnp.sum(jnp.where(sel, ds_ref[h], 0.0), axis=0, keepdims=True)
            stack = jnp.where(head_row == h, col_sums, stack)
        acc = acc + jnp.where(lane == b, jnp.sum(stack, axis=1, keepdims=True), 0.0)
    o_ref[...] = acc


def _inproj_bwd(x, g1, du, dq, dk, dv, w_in, dx1, c_arr):
    s = x.shape[0]
    tm = min(TM, s)
    nt = s // tm
    cols = ((0, 512), (512, 1024), (1024, 1152), (1152, 1280))

    def body(c_ref, x_ref, g_ref, du_ref, dq_ref, dk_ref, dv_ref, w_ref, dx1_ref,
             gx_ref, own_ref, oth_ref, gg_ref, gw_ref):
        i = pl.program_id(0)

        @pl.when(i == 0)
        def _():
            gw_ref[...] = jnp.zeros_like(gw_ref)
            gg_ref[...] = jnp.zeros_like(gg_ref)

        gv = g_ref[...]
        r1, n1 = _rms(x_ref[...])
        h = (n1 * gv).astype(BF16)
        parts = (du_ref[...], dq_ref[...], dk_ref[...].astype(BF16), dv_ref[...].astype(BF16))
        dh = None
        for (a, b), dpart in zip(cols, parts):
            t = _dot(dpart, w_ref[a:b, :])
            dh = t if dh is None else dh + t
            gw_ref[a:b, :] += _dot_tn(dpart, h)
        dx, dg = _rms_bwd(r1, n1, gv, dh)
        gg_ref[...] += dg
        gx_ref[...] = dx1_ref[...] + dx

        @pl.when(i == nt - 1)
        def _():
            for k in range(NSH):
                _emit_halves(gw_ref, k * WIN_S, WIN_S, c_ref[0], own_ref.at[k], oth_ref.at[k])

    row = lambda w: pl.BlockSpec((tm, w), lambda i: (i, 0))
    vec = pl.BlockSpec((1, D), lambda i: (0, 0))
    full = pl.BlockSpec((IN_W, D), lambda i: (0, 0))
    half = pl.BlockSpec((NSH, WIN_S // 2, D), lambda i: (0, 0, 0))
    return pl.pallas_call(
        body, name="inproj_bwd", grid=(nt,),
        in_specs=[SMEM_SPEC, row(D), vec, row(512), row(512), row(128), row(128), full, row(D)],
        out_specs=[row(D), half, half, vec],
        out_shape=[jax.ShapeDtypeStruct((s, D), F32)] + _half_shapes(WIN_S) + [jax.ShapeDtypeStruct((1, D), F32)],
        scratch_shapes=[pltpu.VMEM((IN_W, D), F32)],
        compiler_params=_cp(1))(c_arr, x, g1, du, dq, dk, dv, w_in, dx1)


def _local_step(x, target, small, w_in, w_out, ffn_weights, c_arr, on_ffn_grads=None, on_wout_grads=None):
    g1, g2, g3, g4, w_pool, pool_scale, rel_bias, sinks = small
    bucket = jnp.asarray(_bucket_table())
    wp_bf = w_pool.astype(BF16)
    bias = _bias_table(bucket, rel_bias)
    h1 = _prenorm(x, g1)
    if callable(w_in):
        w_in = w_in(bias, h1)
    u, q, k, v = _inproj_fwd(h1, w_in)
    pool_o, pooled = _pool_fwd(u, wp_bf, pool_scale)
    attn_o, probs, sink_share = _attn_fwd(q, k, v, bias, sinks)
    g2_fwd = g2
    if callable(w_out):
        w_out, after = w_out(pool_o, attn_o)
        if after is not None:
            g2_fwd = g2 + after[:1, :1]
    mix, x1, h2 = _outproj_fwd(pool_o, attn_o, w_out, x, g2_fwd, g3)
    wg, wu, wd = ffn_weights(h2) if callable(ffn_weights) else ffn_weights
    gate, up, act_a, df, dy, loss, gg4 = _ffn_fwd(h2, wg, wu, wd, x1, target, g4)
    dgate, dup, dx1, dmix, gg3, gg2 = _ffn_bwd_act(df, gate, up, wg, wu, wd, dy, x1, mix, g3, g2)
    ffn_g = _ffn_bwd_w(h2, act_a, dgate, dup, df, c_arr)
    dep = None if on_ffn_grads is None else on_ffn_grads(ffn_g)
    dpool, dattn, wout_own, wout_oth = _outproj_bwd(dmix, w_out, pool_o, attn_o, c_arr, dep)
    dep = None if on_wout_grads is None else on_wout_grads(wout_own, wout_oth, dpool)
    du, gwp, gsc = _pool_bwd(pooled, dpool, wp_bf, pool_scale, dep)
    dq, dk, dv, grb, gsk = _attn_bwd(q, k, v, dattn, probs, sink_share, bucket, dep)
    gx, win_own, win_oth, gg1 = _inproj_bwd(x, g1, du, dq, dk, dv, w_in, dx1, c_arr)
    small_grads = (gg1, gg2, gg3, gg4, gwp, gsc, grb, gsk[:, 0].reshape(1, NQ))
    return loss, gx, small_grads, ((win_own, win_oth), (wout_own, wout_oth), ffn_g)


def _place():
    x, y, c = lax.axis_index("x"), lax.axis_index("y"), lax.axis_index("c")
    chips = ((1 - x, y), (x, 1 - y), (1 - x, 1 - y))
    return x, y, c, chips


def _remote(src, dst, ssem, rsem, dev):
    return pltpu.make_async_remote_copy(src_ref=src, dst_ref=dst, send_sem=ssem, recv_sem=rsem,
                                        device_id=dev, device_id_type=MESH_T)


def _to_sibling(arrs, name, after=()):
    nt, na = len(arrs), len(after)

    def body(*refs):
        srcs, dsts = refs[:nt], refs[nt + na:2 * nt + na]
        ssem, rsem = refs[2 * nt + na:]
        x, y, c, _ = _place()
        cps = [_remote(srcs[t], dsts[t], ssem.at[t], rsem.at[t], (x, y, 1 - c)) for t in range(nt)]
        for cp in cps:
            cp.start()
        for cp in cps:
            cp.wait()

    return pl.pallas_call(
        body, name=name, in_specs=[ANY] * (nt + na), out_specs=[ANY] * nt,
        out_shape=[jax.ShapeDtypeStruct(a.shape, a.dtype) for a in arrs],
        scratch_shapes=[pltpu.SemaphoreType.DMA((nt,)), pltpu.SemaphoreType.DMA((nt,))],
        compiler_params=_cp())(*arrs, *after)


HBM_SPEC = pl.BlockSpec(memory_space=pltpu.HBM)
SEM_SPEC = pl.BlockSpec(memory_space=pltpu.SEMAPHORE)
DATAFLOW = pltpu.SideEffectType.DATAFLOW_SIDE_EFFECTING


def _gather_copies(srcs, lands, ssem, rsem):
    x, y, c, chips = _place()
    me = 2 * x + y
    out = []
    for t in range(len(srcs)):
        half = srcs[t].shape[0] // 2
        mine = pl.ds(pl.multiple_of(c * half, 16), half)
        for j, (px, py) in enumerate(chips):
            k = 3 * t + j
            send = _remote(srcs[t].at[mine], lands[t].at[me, mine], ssem.at[k], rsem.at[k], (px, py, c))
            blk = lands[t].at[2 * px + py, mine]
            out.append((send, _remote(blk, blk, ssem.at[k], rsem.at[k], (px, py, c))))
    return out


def _scatter_copies(srcs, lands, ssem, rsem):
    x, y, c, chips = _place()
    out = []
    for t in range(len(srcs)):
        for j, (px, py) in enumerate(chips):
            k = 3 * t + j
            send = _remote(srcs[t].at[2 * px + py], lands[t].at[j], ssem.at[k], rsem.at[k], (px, py, c))
            out.append((send, _remote(lands[t].at[j], lands[t].at[j], ssem.at[k], rsem.at[k], (px, py, c))))
    return out


def _sibling_copies(srcs, lands, ssem, rsem):
    x, y, c, _ = _place()
    sib = (x, y, 1 - c)
    return [(_remote(srcs[t], lands[t], ssem.at[t], rsem.at[t], sib),
             _remote(lands[t], lands[t], ssem.at[t], rsem.at[t], sib)) for t in range(len(srcs))]


def _peer_copies(srcs, lands, ssem, rsem):
    x, y, c, _ = _place()
    me = 4 * x + 2 * y + c
    out = []
    for kk in range(1, 8):
        px, py, pc = x ^ (kk >> 2), y ^ ((kk >> 1) & 1), c ^ (kk & 1)
        send = _remote(srcs[0], lands[0].at[me], ssem.at[kk - 1], rsem.at[kk - 1], (px, py, pc))
        slot = lands[0].at[4 * px + 2 * py + pc]
        out.append((send, _remote(slot, slot, ssem.at[kk - 1], rsem.at[kk - 1], (px, py, pc))))
    return out


def _split_start(plan, ncopy, srcs, lands, after, name):
    ns, nbuf = len(srcs), len(srcs) + len(lands)
    extra = [] if after is None else [after]
    n_in = nbuf + len(extra)

    def body(*refs):
        ssem, rsem, token = refs[n_in], refs[n_in + 1], refs[-1]
        for send, _ in plan(refs[:ns], refs[ns:nbuf], ssem, rsem):
            send.start()
        token[...] = jnp.zeros_like(token)

    bufs = [pltpu.with_memory_space_constraint(a, pltpu.HBM) for a in list(srcs) + list(lands)]
    outs = pl.pallas_call(
        body, name=name, in_specs=[HBM_SPEC] * nbuf + [ANY] * len(extra),
        out_specs=[SEM_SPEC, SEM_SPEC] + [HBM_SPEC] * nbuf + [pl.BlockSpec(memory_space=pltpu.VMEM)],
        out_shape=[pltpu.SemaphoreType.DMA((ncopy,)), pltpu.SemaphoreType.DMA((ncopy,))]
        + [pltpu.HBM(a.shape, a.dtype) for a in bufs] + [jax.ShapeDtypeStruct((8, 128), F32)],
        input_output_aliases={i: 2 + i for i in range(nbuf)},
        compiler_params=pltpu.CompilerParams(has_side_effects=DATAFLOW))(*bufs, *extra)
    return outs[0], outs[1], outs[2:2 + ns], outs[2 + ns:2 + nbuf], outs[-1]


def _split_wait(plan, ssem, rsem, srcs, lands, after, name, only=None):
    ns, nbuf = len(srcs), len(srcs) + len(lands)

    def body(*refs):
        s_ref, r_ref = refs[nbuf], refs[nbuf + 1]
        for k, (send, recv) in enumerate(plan(refs[:ns], refs[ns:nbuf], s_ref, r_ref)):
            if only is None or k in only:
                send.wait_send()
                recv.wait_recv()

    outs = pl.pallas_call(
        body, name=name, in_specs=[HBM_SPEC] * nbuf + [SEM_SPEC, SEM_SPEC] + [ANY] * len(after),
        out_specs=[HBM_SPEC] * nbuf,
        out_shape=[pltpu.HBM(a.shape, a.dtype) for a in list(srcs) + list(lands)],
        input_output_aliases={i: i for i in range(nbuf)},
        compiler_params=pltpu.CompilerParams(has_side_effects=DATAFLOW))(*srcs, *lands, ssem, rsem, *after)
    return outs


def _gather_finish(lands, tag):
    nt = len(lands)

    def body(*refs):
        outs = refs[nt:2 * nt]
        dsend, drecv = refs[2 * nt:]
        x, y, c, chips = _place()
        sib = (x, y, 1 - c)
        sends = []
        for t in range(nt):
            half = outs[t].shape[1] // 2
            mine = pl.ds(pl.multiple_of(c * half, 16), half)
            for j, (px, py) in enumerate(chips):
                blk = outs[t].at[2 * px + py, mine]
                cp = _remote(blk, blk, dsend.at[t, j], drecv.at[t, j], sib)
                cp.start()
                sends.append(cp)
        for t in range(nt):
            half = outs[t].shape[1] // 2
            other = pl.ds(pl.multiple_of((1 - c) * half, 16), half)
            for j, (px, py) in enumerate(chips):
                blk = outs[t].at[2 * px + py, other]
                _remote(blk, blk, dsend.at[t, j], drecv.at[t, j], sib).wait_recv()
        for cp in sends:
            cp.wait_send()

    return pl.pallas_call(
        body, name="gather_finish_" + tag, in_specs=[ANY] * nt, out_specs=[ANY] * nt,
        out_shape=[jax.ShapeDtypeStruct(a.shape, a.dtype) for a in lands],
        input_output_aliases={t: t for t in range(nt)},
        scratch_shapes=[pltpu.SemaphoreType.DMA((nt, 3)), pltpu.SemaphoreType.DMA((nt, 3))],
        compiler_params=_cp())(*lands)


def _chip_partial(owns, recv, chip_arr, tag):
    nt = len(owns)

    def body(chip_ref, *refs):
        k = pl.program_id(0)
        for t in range(nt):
            p = refs[t][...] + refs[nt + t][...].astype(F32)
            refs[2 * nt + t][...] = p.astype(BF16)

            @pl.when(k == chip_ref[0])
            def _():
                refs[3 * nt + t][...] = p

    blk_specs = [pl.BlockSpec((None,) + a.shape[1:], lambda k, chip_ref: (k, 0, 0)) for a in owns]
    own_specs = [pl.BlockSpec(a.shape[1:], lambda k, chip_ref: (0, 0)) for a in owns]
    return pl.pallas_call(
        body, name="rs_chip_partial_" + tag,
        grid_spec=pltpu.PrefetchScalarGridSpec(num_scalar_prefetch=1, grid=(NSH,), in_specs=blk_specs * 2,
                                               out_specs=blk_specs + own_specs),
        out_shape=[jax.ShapeDtypeStruct(a.shape, BF16) for a in owns]
        + [jax.ShapeDtypeStruct(a.shape[1:], F32) for a in owns],
        compiler_params=_cp(1))(chip_arr, *owns, *recv)


def _sum_chips(own, recv, tag, after=()):
    nt = len(own)

    def body(*refs):
        outs = refs[2 * nt + len(after):]
        for t in range(nt):
            r = refs[nt + t]
            outs[t][...] = ((refs[t][...] + r[0].astype(F32)) + r[1].astype(F32)) + r[2].astype(F32)

    vm = pl.BlockSpec(memory_space=pltpu.VMEM)
    return pl.pallas_call(
        body, name="rs_sum_chips_" + tag, in_specs=[vm] * (2 * nt) + [ANY] * len(after), out_specs=[vm] * nt,
        out_shape=[jax.ShapeDtypeStruct(a.shape, F32) for a in own],
        compiler_params=_cp())(*own, *recv, *after)


def _adamw_math(w, g, m, v):
    m = ADAM_B1 * m + (1.0 - ADAM_B1) * g
    v = ADAM_B2 * v + (1.0 - ADAM_B2) * (g * g)
    m_hat = m / (1.0 - ADAM_B1 ** ADAM_STEP)
    v_hat = v / (1.0 - ADAM_B2 ** ADAM_STEP)
    delta = -ADAM_LR * (m_hat / (jnp.sqrt(v_hat) + ADAM_EPS) + ADAM_WD * w)
    return delta, m, v


ADAM_SPLIT = 4


def _adamw_shards(own, sib, ws, ms, vs, c_arr, tag):
    nt = len(own)

    def body(c_ref, *refs):
        hh = pl.program_id(0)
        mine = hh == c_ref[0]
        for t in range(nt):
            g = jnp.where(mine, refs[t][...], refs[nt + t][...])
            delta, m, v = _adamw_math(refs[2 * nt + t][...], g, refs[3 * nt + t][...], refs[4 * nt + t][...])
            refs[5 * nt + t][...] = g
            refs[6 * nt + t][...] = delta
            refs[7 * nt + t][...] = m
            refs[8 * nt + t][...] = v

    def tile(a):
        return (a.shape[0] // ADAM_SPLIT, a.shape[1])

    half_specs = [pl.BlockSpec(tile(a), lambda hh, q, c_ref: (q, 0)) for a in own]
    full_specs = [pl.BlockSpec(tile(a), lambda hh, q, c_ref: (hh * ADAM_SPLIT + q, 0)) for a in own]
    return pl.pallas_call(
        body, name="adamw_shards_" + tag,
        grid_spec=pltpu.PrefetchScalarGridSpec(num_scalar_prefetch=1, grid=(2, ADAM_SPLIT),
                                               in_specs=half_specs * 2 + full_specs * 3, out_specs=full_specs * 4),
        out_shape=[jax.ShapeDtypeStruct(w.shape, F32) for w in ws] * 4,
        compiler_params=_cp(2))(c_arr, *own, *sib, *ws, *ms, *vs)


SMALL_WPOOL_ROW = 32
SMALL_SCALE_ROW = SMALL_WPOOL_ROW + 4 * GROUP
SMALL_BIAS_ROW = SMALL_SCALE_ROW + 8
SMALL_SINKS_ROW = SMALL_BIAS_ROW + NQ
SMALL_LOSS_ROW = SMALL_SINKS_ROW + 8


def _small_sum_adamw(gathered, wp, mp, vp):
    rows = wp.shape[0]

    def body(g_ref, w_ref, m_ref, v_ref, *rest):
        outs, res = rest[:-1], rest[-1]
        g = g_ref[0]
        for d in range(1, 8):
            g = g + g_ref[d]
        delta, m, v = _adamw_math(w_ref[...], g, m_ref[...], v_ref[...])
        for kind, val in enumerate((g, delta, m, v)):
            res[kind] = val
            gains = outs[8 * kind:8 * kind + 4]
            w_pool_o, scale_o, bias_o, sinks_o = outs[8 * kind + 4:8 * kind + 8]
            for t in range(4):
                for i in range(8):
                    gains[t][:, 128 * i:128 * (i + 1)] = res[kind, pl.ds(8 * t + i, 1), :]
            for grp in range(4):
                w_pool_o[grp] = res[kind, pl.ds(SMALL_WPOOL_ROW + GROUP * grp, GROUP), :]
            for i in range(4):
                scale_o[:, 128 * i:128 * (i + 1)] = res[kind, pl.ds(SMALL_SCALE_ROW + i, 1), :]
            bias_o[...] = res[kind, pl.ds(SMALL_BIAS_ROW, NQ), :][:, 0:NBUCKET]
            sinks_o[...] = res[kind, pl.ds(SMALL_SINKS_ROW, 1), :][:, 0:NQ]
        outs[-1][...] = res[0, pl.ds(SMALL_LOSS_ROW, 1), :]

    vm = pl.BlockSpec(memory_space=pltpu.VMEM)
    one_kind = [jax.ShapeDtypeStruct((1, D), F32)] * 4 + [
        jax.ShapeDtypeStruct((4, GROUP, GROUP), F32), jax.ShapeDtypeStruct((1, POOL_W), F32),
        jax.ShapeDtypeStruct((NQ, NBUCKET), F32), jax.ShapeDtypeStruct((1, NQ), F32)]
    return pl.pallas_call(
        body, name="small_sum_adamw", in_specs=[vm] * 4, out_specs=[vm] * 33,
        out_shape=one_kind * 4 + [jax.ShapeDtypeStruct((1, 128), F32)],
        scratch_shapes=[pltpu.VMEM((4, rows, 128), F32)],
        compiler_params=_cp())(gathered, wp, mp, vp)


def _pack_small(gains4, w_pool, pool_scale, rel_bias_t, sinks, loss):
    bias_rows = jnp.pad(rel_bias_t, ((0, 0), (0, 128 - rel_bias_t.shape[1])))
    parts = list(gains4) + [w_pool, pool_scale, bias_rows, sinks, loss]
    rows = []
    for a in parts:
        flat = a.reshape(-1)
        n = flat.shape[0]
        padded = -(-n // 1024) * 1024
        rows.append(jnp.pad(flat, (0, padded - n)).reshape(-1, 128))
    return jnp.concatenate(rows, axis=0)


def kernel(x, g_pre_mix, w_in, w_pool, pool_scale, rel_bias, sinks, w_out, g_post_mix, g_pre_ffn, w_gate, w_up, w_down, g_post_ffn, loss_target, m_g_pre_mix, m_w_in, m_w_pool, m_pool_scale, m_rel_bias, m_sinks, m_w_out, m_g_post_mix, m_g_pre_ffn, m_w_gate, m_w_up, m_w_down, m_g_post_ffn, v_g_pre_mix, v_w_in, v_w_pool, v_pool_scale, v_rel_bias, v_sinks, v_w_out, v_g_post_mix, v_g_pre_ffn, v_w_gate, v_w_up, v_w_down, v_g_post_ffn):
    c = lax.axis_index("c")
    chip = 2 * lax.axis_index("x") + lax.axis_index("y")

    def shards(a_in, a_out, a_gate, a_up, a_down):
        return (a_in[0].T, a_out[0], a_gate[0].T, a_up[0].T, a_down[0])

    c_arr = c.reshape(1).astype(jnp.int32)
    chip_arr = chip.reshape(1).astype(jnp.int32)

    big_w = shards(w_in, w_out, w_gate, w_up, w_down)
    big_bf = [w.astype(BF16) for w in big_w]
    g_ssem, g_rsem, g_srcs, g_lands, _ = _split_start(
        _gather_copies, 15, big_bf, [lax.empty((NSH,) + a.shape, BF16) for a in big_bf], None, "gather_start")
    fw = {}

    def with_own(land, shard):
        return lax.dynamic_update_slice(land, shard[None], (chip, 0, 0))

    def w_in_ready(*after):
        fw["first"] = _split_wait(_gather_copies, g_ssem, g_rsem, g_srcs, g_lands, after, "gather_win_wait",
                                  only=(0, 1, 2))
        (fw["win"],) = _gather_finish([with_own(fw["first"][5], fw["first"][0])], "win")
        return fw["win"].reshape(IN_W, D)

    def w_out_ready(*after):
        first = fw["first"]
        fw["second"] = _split_wait(_gather_copies, g_ssem, g_rsem, first[:5], [fw["win"]] + list(first[6:]), after,
                                   "gather_wout_wait", only=(3, 4, 5))
        (fw["wout"],) = _gather_finish([with_own(fw["second"][6], fw["second"][1])], "wout")
        return fw["wout"].reshape(D, D), None

    def ffn_weights(after):
        second = fw["second"]
        outs = _split_wait(_gather_copies, g_ssem, g_rsem, second[:5], [second[5], fw["wout"]] + list(second[7:]),
                           [after], "gather_ffn_wait", only=tuple(range(6, 15)))
        return _gather_finish([with_own(land, src) for src, land in zip(outs[2:5], outs[7:])], "ffn")

    rs = {}

    def on_ffn_grads(ffn_g):
        oths = ffn_g[1::2]
        rs["ffn_own"] = ffn_g[0::2]
        rs["x"] = _split_start(_sibling_copies, 3, oths, [lax.empty(a.shape, BF16) for a in oths], ffn_g[0],
                               "rs_exchange_ffn_start")
        return rs["x"][4]

    def on_wout_grads(own, oth, after):
        ssem, rsem, srcs, lands, _ = rs["x"]
        recv_ffn = _split_wait(_sibling_copies, ssem, rsem, srcs, lands, [after], "rs_exchange_ffn_wait")[3:]
        recv_wout = _to_sibling([oth], "rs_exchange_wout")
        outs = _chip_partial([own] + list(rs["ffn_own"]), list(recv_wout) + list(recv_ffn), chip_arr, "early")
        rs["early_own"] = outs[4:]
        rs["s"] = _split_start(_scatter_copies, 12, outs[:4],
                               [lax.empty((3,) + a.shape[1:], BF16) for a in outs[:4]], outs[4], "scatter_early_start")
        return rs["s"][4]

    small = (g_pre_mix, g_post_mix, g_pre_ffn, g_post_ffn, w_pool[0], pool_scale, rel_bias, sinks)
    loss, gx, small_grads, ((win_own, win_oth), _, _) = _local_step(
        x[0], loss_target[0], small, w_in_ready, w_out_ready, ffn_weights, c_arr, on_ffn_grads, on_wout_grads)

    ssem, rsem, srcs, lands, _ = rs["s"]
    recv_early = _split_wait(_scatter_copies, ssem, rsem, srcs, lands, [gx], "scatter_early_wait")[4:]
    finals = _sum_chips(list(rs["early_own"]), list(recv_early), "early")
    sh_ssem, sh_rsem, sh_srcs, sh_lands, sh_token = _split_start(
        _sibling_copies, 4, finals, [lax.empty(a.shape, F32) for a in finals], finals[0], "rs_share_early_start")

    unused = jnp.zeros((1, 1), F32)
    gp = _pack_small(small_grads[:4], *small_grads[4:], loss)
    wp = _pack_small((g_pre_mix, g_post_mix, g_pre_ffn, g_post_ffn), w_pool, pool_scale, rel_bias.T, sinks, unused)
    mp = _pack_small((m_g_pre_mix, m_g_post_mix, m_g_pre_ffn, m_g_post_ffn), m_w_pool, m_pool_scale, m_rel_bias.T,
                     m_sinks, unused)
    vp = _pack_small((v_g_pre_mix, v_g_post_mix, v_g_pre_ffn, v_g_post_ffn), v_w_pool, v_pool_scale, v_rel_bias.T,
                     v_sinks, unused)

    moments = (shards(m_w_in, m_w_out, m_w_gate, m_w_up, m_w_down),
               shards(v_w_in, v_w_out, v_w_gate, v_w_up, v_w_down))
    recv_a = _to_sibling([win_oth], "rs_exchange_win", [sh_token])
    outs = _chip_partial([win_own], recv_a, chip_arr, "win")
    w_ssem, w_rsem, w_srcs, w_lands, w_token = _split_start(
        _scatter_copies, 3, outs[:1], [lax.empty((3,) + outs[0].shape[1:], BF16)], gx, "scatter_win_start")
    slots = lax.dynamic_update_slice(lax.empty((8,) + gp.shape, F32), gp[None], (2 * chip + c, 0, 0))
    p_ssem, p_rsem, p_srcs, p_lands, p_token = _split_start(_peer_copies, 7, [gp], [slots], w_token,
                                                            "small_allgather_start")
    shared = _split_wait(_sibling_copies, sh_ssem, sh_rsem, sh_srcs, sh_lands, [p_token], "rs_share_early_wait")
    adam_e = _adamw_shards(shared[:4], shared[4:], big_w[1:], moments[0][1:], moments[1][1:], c_arr, "early")
    recv_win = _split_wait(_scatter_copies, w_ssem, w_rsem, w_srcs, w_lands, [adam_e[0]], "scatter_win_wait")[1:]
    win_final = _sum_chips([outs[1]], recv_win, "win")
    win_sib = _to_sibling(win_final, "rs_share_win")
    adam_w = _adamw_shards(win_final, win_sib, big_w[:1], moments[0][:1], moments[1][:1], c_arr, "win")
    big_g, big_d, big_m, big_v = [[adam_w[i]] + list(adam_e[4 * i:4 * i + 4]) for i in range(4)]

    gathered = _split_wait(_peer_copies, p_ssem, p_rsem, p_srcs, p_lands, [adam_w[0]], "small_allgather_wait")[1]
    flat = _small_sum_adamw(gathered, wp, mp, vp)
    small_out = [flat[8 * kind:8 * kind + 8] for kind in range(4)]
    total_loss = flat[-1][0, 0]

    def ordered(small8, big4):
        sg1, sg2, sg3, sg4, swp, ssc, srb, ssk = small8
        swp, srb = swp[None], srb.T
        bwin, bwout, bwg, bwu, bwd = big4[0].T[None], big4[1][None], big4[2].T[None], big4[3].T[None], big4[4][None]
        return [sg1, bwin, swp, ssc, srb, ssk, bwout, sg2, sg3, bwg, bwu, bwd, sg4]

    res = [total_loss, gx[None]]
    for sm, bg in zip(small_out, (big_g, big_d, big_m, big_v)):
        res += ordered(sm, bg)
    return tuple(res)
```

```python
import numpy as np
import jax
import jax.numpy as jnp
from jax import lax
from jax.experimental import pallas as pl
from jax.experimental.pallas import tpu as pltpu

F32 = jnp.float32
BF16 = jnp.bfloat16

D = 1024
POOL_W = 512
GROUP = 128
WINDOWS = (2, 4, 8, 16)
HALO = 128
NQ = 8
BLK = 128
NBUCKET = 32
IN_W = 1280
DFF = 2816
NSH = 4
FS = DFF // NSH
WIN_S = IN_W // NSH
WOUT_S = D // NSH
EPS = 1e-6
NEG = -1e30
SCALE = 0.125

ADAM_LR = 0.001
ADAM_B1 = 0.9
ADAM_B2 = 0.999
ADAM_EPS = 1e-08
ADAM_WD = 0.01
ADAM_STEP = 10

TM = 512
TM_POOL = 512
TM_FFN = 512
TM_FFN_FWD = 1024
FFN_ROWS = 256
VMEM_LIMIT = 60 * 1024 * 1024

MESH_T = pl.DeviceIdType.MESH
ANY = pl.BlockSpec(memory_space=pl.ANY)


def _cp(n_grid=0, **kw):
    sem = ("arbitrary",) * n_grid if n_grid else None
    return pltpu.CompilerParams(dimension_semantics=sem, vmem_limit_bytes=VMEM_LIMIT, **kw)


def _dot(a, b):
    return jnp.dot(a, b, preferred_element_type=F32)


def _dot_nt(a, b):
    return lax.dot_general(a, b, (((1,), (1,)), ((), ())), preferred_element_type=F32)


def _dot_tn(a, b):
    return lax.dot_general(a, b, (((0,), (0,)), ((), ())), preferred_element_type=F32)


def _rms(x):
    r = lax.rsqrt(jnp.mean(x * x, axis=-1, keepdims=True) + EPS)
    return r, x * r


def _rms_bwd(r, n, g, dout):
    dn = dout * g
    dx = r * (dn - n * jnp.mean(dn * n, axis=-1, keepdims=True))
    dg = jnp.sum(dout * n, axis=0, keepdims=True)
    return dx, dg


def _split(x, n):
    parts = []
    r = x
    for _ in range(n):
        p = r.astype(BF16)
        parts.append(p)
        r = r - p.astype(F32)
    return parts


def _band_dot(a, x, n):
    acc = None
    for p in _split(x, n):
        t = _dot(a, p)
        acc = t if acc is None else acc + t
    return acc


def _bucket_table():
    qi = np.arange(BLK)[None, :]
    kj = np.arange(2 * BLK)[:, None]
    dist = qi + BLK - kj
    n = np.maximum(dist, 0)
    nf = np.maximum(n, 1).astype(np.float32)
    large = 16 + (np.log(nf / np.float32(16)) / np.float32(np.log(128 / 16)) * np.float32(16)).astype(np.int32)
    large = np.minimum(large, NBUCKET - 1)
    return np.where(n < 16, n, large).astype(np.int32)


def _prenorm(x, g1):
    s = x.shape[0]
    tm = min(TM, s)

    def body(x_ref, g_ref, h_ref):
        _, n = _rms(x_ref[...])
        h_ref[...] = (n * g_ref[...]).astype(BF16)

    row = pl.BlockSpec((tm, D), lambda i: (i, 0))
    return pl.pallas_call(
        body, name="prenorm", grid=(s // tm,),
        in_specs=[row, pl.BlockSpec((1, D), lambda i: (0, 0))], out_specs=row,
        out_shape=jax.ShapeDtypeStruct((s, D), BF16),
        compiler_params=_cp(1))(x, g1)


def _inproj_fwd(h1, w_in):
    s = h1.shape[0]
    tm = min(TM, s)

    def body(h_ref, w_ref, u_ref, q_ref, k_ref, v_ref):
        proj = _dot_nt(h_ref[...], w_ref[...])
        u_ref[...] = proj[:, :512]
        q_ref[...] = proj[:, 512:1024].astype(BF16)
        k_ref[...] = proj[:, 1024:1152].astype(BF16)
        v_ref[...] = proj[:, 1152:1280].astype(BF16)

    row = lambda w: pl.BlockSpec((tm, w), lambda i: (i, 0))
    return pl.pallas_call(
        body, name="inproj_fwd", grid=(s // tm,),
        in_specs=[row(D), pl.BlockSpec((IN_W, D), lambda i: (0, 0))],
        out_specs=[row(512), row(512), row(128), row(128)],
        out_shape=[jax.ShapeDtypeStruct((s, 512), F32), jax.ShapeDtypeStruct((s, 512), BF16),
                   jax.ShapeDtypeStruct((s, 128), BF16), jax.ShapeDtypeStruct((s, 128), BF16)],
        compiler_params=_cp(1))(h1, w_in)


def _window_sums(ext, w, lag_sign, tm, terms):
    rows = lax.broadcasted_iota(jnp.int32, (HALO, 2 * HALO), 0)
    cols = lax.broadcasted_iota(jnp.int32, (HALO, 2 * HALO), 1)
    d = rows + HALO - cols if lag_sign > 0 else cols - rows
    band = jnp.where((d >= 0) & (d < w), 1.0, 0.0).astype(BF16)
    return jnp.concatenate([_band_dot(band, ext[r:r + 2 * HALO], terms) for r in range(0, tm, HALO)], axis=0)


def _pooled(ext, cur, t0, tm):
    t = t0 + lax.broadcasted_iota(jnp.int32, (tm, GROUP), 0)
    out = []
    for g, w in enumerate(WINDOWS):
        sl = slice(g * GROUP, (g + 1) * GROUP)
        cnt = jnp.minimum(t + 1, w).astype(F32)
        out.append(_window_sums(ext[:, sl], w, 1, tm, 2) / cnt - cur[:, sl])
    return out


def _pool_fwd(u, w_pool, pool_scale):
    s = u.shape[0]
    tm = min(TM_POOL, s)
    hb = tm // HALO

    def body(uc_ref, uh_ref, wp_ref, sc_ref, o_ref, pooled_ref):
        i = pl.program_id(0)
        cur = uc_ref[...]
        halo = jnp.where(i > 0, uh_ref[...], 0.0)
        ext = jnp.concatenate([halo, cur], axis=0)
        pooled = _pooled(ext, cur, i * tm, tm)
        for g in range(4):
            sl = slice(g * GROUP, (g + 1) * GROUP)
            pb = pooled[g].astype(BF16)
            pooled_ref[:, sl] = pb
            o_ref[:, sl] = (_dot(pb, wp_ref[g]) * sc_ref[:, sl]).astype(BF16)

    row = pl.BlockSpec((tm, 512), lambda i: (i, 0))
    return pl.pallas_call(
        body, name="pool_fwd", grid=(s // tm,),
        in_specs=[row, pl.BlockSpec((HALO, 512), lambda i: (jnp.maximum(i * hb - 1, 0), 0)),
                  pl.BlockSpec((4, GROUP, GROUP), lambda i: (0, 0, 0)),
                  pl.BlockSpec((1, 512), lambda i: (0, 0))],
        out_specs=[row, row],
        out_shape=[jax.ShapeDtypeStruct((s, 512), BF16)] * 2,
        compiler_params=_cp(1))(u, u, w_pool, pool_scale)


def _bias_table(bucket, rel_bias):
    def body(b_ref, rb_ref, o_ref):
        bucket_v = b_ref[...]
        key = lax.broadcasted_iota(jnp.int32, (2 * BLK, BLK), 0)
        dist = lax.broadcasted_iota(jnp.int32, (2 * BLK, BLK), 1) + BLK - key
        inwin = (dist >= 0) & (dist < BLK)
        for h in range(NQ):
            acc = jnp.zeros((2 * BLK, BLK), F32)
            for b in range(NBUCKET):
                acc = jnp.where(bucket_v == b, rb_ref[b, h], acc)
            rest = jnp.where(inwin, acc, NEG)
            o_ref[1, h] = rest
            o_ref[0, h] = jnp.where(key >= BLK, rest, NEG)

    return pl.pallas_call(
        body, name="bias_table",
        in_specs=[pl.BlockSpec(memory_space=pltpu.VMEM), pl.BlockSpec(memory_space=pltpu.SMEM)],
        out_specs=pl.BlockSpec(memory_space=pltpu.VMEM),
        out_shape=jax.ShapeDtypeStruct((2, NQ, 2 * BLK, BLK), F32),
        compiler_params=_cp())(bucket, rel_bias)


HD = 64


def _kv_band(ref, n, transposed):
    pstart = pl.multiple_of(jnp.maximum(n - 1, 0) * BLK, BLK)
    cstart = pl.multiple_of(n * BLK, BLK)
    lo = lax.broadcasted_iota(jnp.int32, (2 * BLK, 128), 1) < HD
    band = jnp.concatenate([ref[pl.ds(pstart, BLK), :], ref[pl.ds(cstart, BLK), :]], axis=0).astype(F32)
    rot = pltpu.roll(band, HD, axis=1)
    halves = ((jnp.where(lo, band, 0.0), jnp.where(lo, 0.0, rot)), (jnp.where(lo, rot, 0.0), jnp.where(lo, 0.0, band)))
    if transposed:
        out = [jnp.concatenate([a.T, b.T], axis=1).astype(BF16) for a, b in halves]
    else:
        out = [jnp.concatenate([a, b], axis=0).astype(BF16) for a, b in halves]
    return out, (lo, pstart, cstart)


def _pair_probs(qt, kk, bias, sk_ref, p):
    sink = jnp.concatenate([jnp.full((1, 1, BLK), sk_ref[0, 2 * p + e], F32) for e in range(2)], axis=0)
    s = _dot_nt(kk, qt).reshape(2, 2 * BLK, BLK) + bias
    m = jnp.maximum(jnp.max(s, axis=1, keepdims=True), sink)
    pr = jnp.exp(s - m)
    es = jnp.exp(sink - m)
    inv = 1.0 / (jnp.sum(pr, axis=1, keepdims=True) + es)
    return pr * inv, es * inv


def _attn_fwd(q, k, v, bias, sinks):
    s = q.shape[0]
    nblk = s // BLK

    def body(q_ref, k_ref, v_ref, b_ref, sk_ref, o_ref, prob_ref, ps_ref):
        n = pl.program_id(0)
        kk, _ = _kv_band(k_ref, n, False)
        vt, _ = _kv_band(v_ref, n, True)
        bidx = jnp.minimum(n, 1)
        for p in range(4):
            h = p // 2
            qt = (q_ref[:, p * 128:(p + 1) * 128].astype(F32) * SCALE).astype(BF16)
            prob, ps = _pair_probs(qt, kk[h], b_ref[bidx, pl.ds(2 * p, 2)], sk_ref, p)
            pb = prob.astype(BF16)
            prob_ref[pl.ds(2 * p, 2)] = pb
            ps_ref[pl.ds(2 * p, 2), :] = ps.reshape(2, BLK)
            acc_t = _dot(vt[h], pb.reshape(4 * BLK, BLK))
            o_ref[:, p * 128:(p + 1) * 128] = acc_t.T.astype(BF16)

    return pl.pallas_call(
        body, name="attn_fwd", grid=(nblk,),
        in_specs=[pl.BlockSpec((BLK, 512), lambda i: (i, 0)),
                  pl.BlockSpec((s, 128), lambda i: (0, 0)),
                  pl.BlockSpec((s, 128), lambda i: (0, 0)),
                  pl.BlockSpec((2, NQ, 2 * BLK, BLK), lambda i: (0, 0, 0, 0)),
                  pl.BlockSpec(memory_space=pltpu.SMEM)],
        out_specs=[pl.BlockSpec((BLK, 512), lambda i: (i, 0)),
                   pl.BlockSpec((None, NQ, 2 * BLK, BLK), lambda i: (i, 0, 0, 0)),
                   pl.BlockSpec((None, NQ, BLK), lambda i: (i, 0, 0))],
        out_shape=[jax.ShapeDtypeStruct((s, 512), BF16), jax.ShapeDtypeStruct((nblk, NQ, 2 * BLK, BLK), BF16),
                   jax.ShapeDtypeStruct((nblk, NQ, BLK), F32)],
        compiler_params=_cp(1))(q, k, v, bias, sinks)


def _outproj_fwd(pool_o, attn_o, w_out, x, g2, g3):
    s = x.shape[0]
    tm = min(TM, s)

    def body(p_ref, a_ref, w_ref, x_ref, g2_ref, g3_ref, mix_ref, x1_ref, h2_ref):
        mix = _dot(p_ref[...], w_ref[0:512, :]) + _dot(a_ref[...], w_ref[512:1024, :])
        mix_ref[...] = mix
        _, n2 = _rms(mix)
        x1 = x_ref[...] + n2 * g2_ref[...]
        x1_ref[...] = x1
        _, n3 = _rms(x1)
        h2_ref[...] = (n3 * g3_ref[...]).astype(BF16)

    row = lambda w: pl.BlockSpec((tm, w), lambda i: (i, 0))
    vec = pl.BlockSpec((1, D), lambda i: (0, 0))
    return pl.pallas_call(
        body, name="outproj_fwd", grid=(s // tm,),
        in_specs=[row(512), row(512), pl.BlockSpec((D, D), lambda i: (0, 0)), row(D), vec, vec],
        out_specs=[row(D), row(D), row(D)],
        out_shape=[jax.ShapeDtypeStruct((s, D), F32), jax.ShapeDtypeStruct((s, D), F32),
                   jax.ShapeDtypeStruct((s, D), BF16)],
        compiler_params=_cp(1))(pool_o, attn_o, w_out, x, g2, g3)


def _silu_parts(g):
    sg = jax.nn.sigmoid(g)
    return sg, g * sg


def _ffn_fwd(h2, wg, wu, wd, x1, target, g4):
    s = h2.shape[0]
    tm = min(TM_FFN_FWD, s)

    def body(h_ref, wg_ref, wu_ref, wd_ref, x1_ref, t_ref, g4_ref,
             gate_ref, up_ref, a_ref, df_ref, dy_ref, loss_ref, gg4_ref, f_acc):
        i = pl.program_id(0)
        j = pl.program_id(1)
        first = j == 0
        chunks = [pl.ds(r, min(FFN_ROWS, tm)) for r in range(0, tm, FFN_ROWS)]
        project = lambda rows: (_dot_nt(h_ref[rows, :], wg_ref[...]), _dot_nt(h_ref[rows, :], wu_ref[...]))
        ahead = [project(chunks[0])]
        for k, rows in enumerate(chunks):
            if k + 1 < len(chunks):
                ahead.append(project(chunks[k + 1]))
            gate, up = ahead[k]
            gate_ref[rows, :] = gate.astype(BF16)
            up_ref[rows, :] = up.astype(BF16)
            _, sl = _silu_parts(gate)
            a = (sl * up).astype(BF16)
            a_ref[rows, :] = a
            contrib = _dot(a, wd_ref[...])
            f_acc[rows, :] = jnp.where(first, contrib, f_acc[rows, :] + contrib)

        @pl.when((i == 0) & (j == 0))
        def _():
            loss_ref[...] = jnp.zeros_like(loss_ref)
            gg4_ref[...] = jnp.zeros_like(gg4_ref)

        @pl.when(j == NSH - 1)
        def _():
            r4, n4 = _rms(f_acc[...])
            g4v = g4_ref[...]
            err = x1_ref[...] + n4 * g4v - t_ref[...]
            loss_ref[...] += (0.5 / D) * jnp.sum(err * err).reshape(1, 1)
            dy = err * (1.0 / D)
            dy_ref[...] = dy
            df, dg = _rms_bwd(r4, n4, g4v, dy)
            gg4_ref[...] += dg
            df_ref[...] = df.astype(BF16)

    row = lambda w: pl.BlockSpec((tm, w), lambda i, j: (i, 0))
    sh = lambda r, c: pl.BlockSpec((None, r, c), lambda i, j: (j, 0, 0))
    act = pl.BlockSpec((None, tm, FS), lambda i, j: (j, i, 0))
    vec = pl.BlockSpec((1, D), lambda i, j: (0, 0))
    return pl.pallas_call(
        body, name="ffn_fwd", grid=(s // tm, NSH),
        in_specs=[row(D), sh(FS, D), sh(FS, D), sh(FS, D), row(D), row(D), vec],
        out_specs=[act, act, act, row(D), row(D), pl.BlockSpec((1, 1), lambda i, j: (0, 0)), vec],
        out_shape=[jax.ShapeDtypeStruct((NSH, s, FS), BF16)] * 3
        + [jax.ShapeDtypeStruct((s, D), BF16), jax.ShapeDtypeStruct((s, D), F32),
           jax.ShapeDtypeStruct((1, 1), F32), jax.ShapeDtypeStruct((1, D), F32)],
        scratch_shapes=[pltpu.VMEM((tm, D), F32)],
        compiler_params=_cp(2))(h2, wg, wu, wd, x1, target, g4)


def _ffn_bwd_act(df, gate, up, wg, wu, wd, dy, x1, mix, g3, g2):
    s = df.shape[0]
    tm = min(TM_FFN, s)

    def body(df_ref, gate_ref, up_ref, wg_ref, wu_ref, wd_ref, dy_ref, x1_ref, mix_ref, g3_ref, g2_ref,
             dgate_ref, dup_ref, dx1_ref, dmix_ref, gg3_ref, gg2_ref, acc):
        j = pl.program_id(0)
        i = pl.program_id(1)
        first = j == 0
        tile = pl.multiple_of(i * tm, tm)
        chunks = [pl.ds(r, min(FFN_ROWS, tm)) for r in range(0, tm, FFN_ROWS)]
        das = [_dot_nt(df_ref[chunks[0], :], wd_ref[...])]
        for k, rows in enumerate(chunks):
            if k + 1 < len(chunks):
                das.append(_dot_nt(df_ref[chunks[k + 1], :], wd_ref[...]))
            da = das[k]
            g = gate_ref[rows, :].astype(F32)
            u = up_ref[rows, :].astype(F32)
            sg, sl = _silu_parts(g)
            dgate = (da * u * (sg * (1.0 + g * (1.0 - sg)))).astype(BF16)
            dup = (da * sl).astype(BF16)
            dgate_ref[rows, :] = dgate
            dup_ref[rows, :] = dup
            contrib = _dot(dgate, wg_ref[...]) + _dot(dup, wu_ref[...])
            acc_rows = pl.ds(tile + k * FFN_ROWS, rows.size)
            acc[acc_rows, :] = jnp.where(first, contrib, acc[acc_rows, :] + contrib)

        @pl.when((i == 0) & (j == 0))
        def _():
            gg3_ref[...] = jnp.zeros_like(gg3_ref)
            gg2_ref[...] = jnp.zeros_like(gg2_ref)

        @pl.when(j == NSH - 1)
        def _():
            r3, n3 = _rms(x1_ref[...])
            dx1n, dg3 = _rms_bwd(r3, n3, g3_ref[...], acc[pl.ds(tile, tm), :])
            dx1 = dy_ref[...] + dx1n
            dx1_ref[...] = dx1
            gg3_ref[...] += dg3
            r2, n2 = _rms(mix_ref[...])
            dmix, dg2 = _rms_bwd(r2, n2, g2_ref[...], dx1)
            gg2_ref[...] += dg2
            dmix_ref[...] = dmix.astype(BF16)

    row = pl.BlockSpec((tm, D), lambda j, i: (i, 0))
    last = pl.BlockSpec((tm, D), lambda j, i: (i * (j // (NSH - 1)), 0))
    sh = pl.BlockSpec((None, FS, D), lambda j, i: (j, 0, 0))
    act = pl.BlockSpec((None, tm, FS), lambda j, i: (j, i, 0))
    vec = pl.BlockSpec((1, D), lambda j, i: (0, 0))
    return pl.pallas_call(
        body, name="ffn_bwd_act", grid=(NSH, s // tm),
        in_specs=[row, act, act, sh, sh, sh, last, last, last, vec, vec],
        out_specs=[act, act, last, last, vec, vec],
        out_shape=[jax.ShapeDtypeStruct((NSH, s, FS), BF16), jax.ShapeDtypeStruct((NSH, s, FS), BF16),
                   jax.ShapeDtypeStruct((s, D), F32), jax.ShapeDtypeStruct((s, D), BF16),
                   jax.ShapeDtypeStruct((1, D), F32), jax.ShapeDtypeStruct((1, D), F32)],
        scratch_shapes=[pltpu.VMEM((s, D), F32)],
        compiler_params=_cp(2))(df, gate, up, wg, wu, wd, dy, x1, mix, g3, g2)


SMEM_SPEC = pl.BlockSpec(memory_space=pltpu.SMEM)


def _emit_halves(acc_ref, row0, rows, c, own_ref, oth_ref):
    half = rows // 2
    own_ref[...] = acc_ref[pl.ds(row0 + pl.multiple_of(c * half, 8), half), :]
    oth_ref[...] = acc_ref[pl.ds(row0 + pl.multiple_of((1 - c) * half, 8), half), :].astype(BF16)


def _half_shapes(rows):
    return [jax.ShapeDtypeStruct((NSH, rows // 2, D), F32), jax.ShapeDtypeStruct((NSH, rows // 2, D), BF16)]


def _ffn_bwd_w(h2, act_a, dgate, dup, df, c_arr):
    s = h2.shape[0]
    tm = min(TM_FFN_FWD, s)
    nt = s // tm

    def body(c_ref, h_ref, a_ref, dgate_ref, dup_ref, df_ref, *rest):
        outs, accs = rest[:6], rest[6:]
        i = pl.program_id(1)

        @pl.when(i == 0)
        def _():
            for acc in accs:
                acc[...] = jnp.zeros_like(acc)

        h = h_ref[...]
        accs[0][...] += _dot_tn(dgate_ref[...], h)
        accs[1][...] += _dot_tn(dup_ref[...], h)
        accs[2][...] += _dot_tn(a_ref[...], df_ref[...])

        @pl.when(i == nt - 1)
        def _():
            for t, acc in enumerate(accs):
                _emit_halves(acc, 0, FS, c_ref[0], outs[2 * t], outs[2 * t + 1])

    row = lambda w: pl.BlockSpec((tm, w), lambda j, i: (i, 0))
    act = pl.BlockSpec((None, tm, FS), lambda j, i: (j, i, 0))
    half = pl.BlockSpec((None, FS // 2, D), lambda j, i: (j, 0, 0))
    return pl.pallas_call(
        body, name="ffn_bwd_w", grid=(NSH, nt),
        in_specs=[SMEM_SPEC, row(D), act, act, act, row(D)],
        out_specs=[half] * 6,
        out_shape=_half_shapes(FS) * 3,
        scratch_shapes=[pltpu.VMEM((FS, D), F32)] * 3,
        compiler_params=_cp(2))(c_arr, h2, act_a, dgate, dup, df)


def _outproj_bwd(dmix, w_out, pool_o, attn_o, c_arr, dep=None):
    s = dmix.shape[0]
    tm = min(TM, s)
    nt = s // tm

    def body(c_ref, dm_ref, w_ref, p_ref, a_ref, *rest):
        dpool_ref, dattn_ref, own_ref, oth_ref, gw_ref = rest[-5:]
        i = pl.program_id(0)

        @pl.when(i == 0)
        def _():
            gw_ref[...] = jnp.zeros_like(gw_ref)

        dm = dm_ref[...]
        dpool_ref[...] = _dot_nt(dm, w_ref[0:512, :])
        dattn_ref[...] = _dot_nt(dm, w_ref[512:1024, :]).astype(BF16)
        gw_ref[0:512, :] += _dot_tn(p_ref[...], dm)
        gw_ref[512:1024, :] += _dot_tn(a_ref[...], dm)

        @pl.when(i == nt - 1)
        def _():
            for k in range(NSH):
                _emit_halves(gw_ref, k * WOUT_S, WOUT_S, c_ref[0], own_ref.at[k], oth_ref.at[k])

    row = lambda w: pl.BlockSpec((tm, w), lambda i: (i, 0))
    full = pl.BlockSpec((D, D), lambda i: (0, 0))
    half = pl.BlockSpec((NSH, WOUT_S // 2, D), lambda i: (0, 0, 0))
    return pl.pallas_call(
        body, name="outproj_bwd", grid=(nt,),
        in_specs=[SMEM_SPEC, row(D), full, row(512), row(512)] + ([] if dep is None else [ANY]),
        out_specs=[row(512), row(512), half, half],
        out_shape=[jax.ShapeDtypeStruct((s, 512), F32), jax.ShapeDtypeStruct((s, 512), BF16)] + _half_shapes(WOUT_S),
        scratch_shapes=[pltpu.VMEM((D, D), F32)],
        compiler_params=_cp(1))(c_arr, dmix, w_out, pool_o, attn_o, *([] if dep is None else [dep]))


def _pool_bwd(pooled, dpool, w_pool, pool_scale, dep=None):
    s = pooled.shape[0]
    tm = min(TM_POOL, s)
    hb = tm // HALO
    nt = s // tm
    last_halo = s // HALO - 1

    def body(p_ref, dc_ref, dn_ref, wp_ref, sc_ref, *rest):
        du_ref, gwp_ref, gsc_ref = rest[-3:]
        i = pl.program_id(0)

        @pl.when(i == 0)
        def _():
            gwp_ref[...] = jnp.zeros_like(gwp_ref)
            gsc_ref[...] = jnp.zeros_like(gsc_ref)

        dcur = dc_ref[...]
        dnext = jnp.where(i < nt - 1, dn_ref[...], 0.0)
        dext = jnp.concatenate([dcur, dnext], axis=0)
        t_ext = i * tm + lax.broadcasted_iota(jnp.int32, (tm + HALO, GROUP), 0)
        for g, w in enumerate(WINDOWS):
            sl = slice(g * GROUP, (g + 1) * GROUP)
            pb = p_ref[:, sl]
            wp = wp_ref[g]
            mixed = _dot(pb, wp)
            gsc_ref[:, sl] += jnp.sum(dcur[:, sl] * mixed, axis=0, keepdims=True)
            dmixed = (dext[:, sl] * sc_ref[:, sl]).astype(BF16)
            gwp_ref[g] += _dot_tn(pb, dmixed[0:tm])
            dpooled = _dot_nt(dmixed, wp)
            z = dpooled / jnp.minimum(t_ext + 1, w).astype(F32)
            du_ref[:, sl] = (_window_sums(z, w, -1, tm, 2) - dpooled[0:tm]).astype(BF16)

    return pl.pallas_call(
        body, name="pool_bwd", grid=(nt,),
        in_specs=[pl.BlockSpec((tm, 512), lambda i: (i, 0)),
                  pl.BlockSpec((tm, 512), lambda i: (i, 0)),
                  pl.BlockSpec((HALO, 512), lambda i: (jnp.minimum((i + 1) * hb, last_halo), 0)),
                  pl.BlockSpec((4, GROUP, GROUP), lambda i: (0, 0, 0)),
                  pl.BlockSpec((1, 512), lambda i: (0, 0))] + ([] if dep is None else [ANY]),
        out_specs=[pl.BlockSpec((tm, 512), lambda i: (i, 0)),
                   pl.BlockSpec((4, GROUP, GROUP), lambda i: (0, 0, 0)),
                   pl.BlockSpec((1, 512), lambda i: (0, 0))],
        out_shape=[jax.ShapeDtypeStruct((s, 512), BF16), jax.ShapeDtypeStruct((4, GROUP, GROUP), F32),
                   jax.ShapeDtypeStruct((1, 512), F32)],
        compiler_params=_cp(1))(pooled, dpool, dpool, w_pool, pool_scale, *([] if dep is None else [dep]))


def _attn_bwd(q, k, v, do, probs, sink_share, bucket, dep=None):
    s = q.shape[0]
    nblk = s // BLK

    def body(q_ref, do_ref, k_ref, v_ref, prob_ref, ps_ref, bk_ref, *rest):
        dq_ref, dk_ref, dv_ref, grb_ref, gsk_ref, dss_ref = rest[-6:]
        n = pl.program_id(0)

        @pl.when(n == 0)
        def _():
            dk_ref[...] = jnp.zeros_like(dk_ref)
            dv_ref[...] = jnp.zeros_like(dv_ref)
            dss_ref[...] = jnp.zeros_like(dss_ref)
            gsk_ref[...] = jnp.zeros_like(gsk_ref)

        kt, (lo, pstart, cstart) = _kv_band(k_ref, n, True)
        vv, _ = _kv_band(v_ref, n, False)
        lo_q = lax.broadcasted_iota(jnp.int32, (BLK, 128), 1) < HD
        dkk = [jnp.zeros((2 * BLK, 128), F32), jnp.zeros((2 * BLK, 128), F32)]
        dvv = [jnp.zeros((2 * BLK, 128), F32), jnp.zeros((2 * BLK, 128), F32)]

        def by_head(t):
            return jnp.concatenate([jnp.where(lo_q, t, 0.0), jnp.where(lo_q, 0.0, t)], axis=0).astype(BF16)

        for p in range(4):
            h = p // 2
            qt = q_ref[:, p * 128:(p + 1) * 128].astype(F32) * SCALE
            dot = do_ref[:, p * 128:(p + 1) * 128]
            pb = prob_ref[pl.ds(2 * p, 2)]
            prob = pb.astype(F32)
            ps = ps_ref[pl.ds(2 * p, 2), :].reshape(2, 1, BLK)
            dp = _dot_nt(vv[h], dot).reshape(2, 2 * BLK, BLK)
            delta = jnp.sum(prob * dp, axis=1, keepdims=True)
            ds = prob * (dp - delta)
            sink_part = ps * delta
            for e in range(2):
                gs = -jnp.sum(sink_part[e]).reshape(1, 1)
                gsk_ref[pl.ds(2 * p + e, 1), :] += jnp.broadcast_to(gs, (1, 128))
            dss_ref[pl.ds(2 * p, 2)] += ds
            dsb = ds.astype(BF16)
            dq_t = _dot(kt[h], dsb.reshape(4 * BLK, BLK))
            dq_ref[:, p * 128:(p + 1) * 128] = (dq_t.T * SCALE).astype(BF16)
            dkk[h] = dkk[h] + _dot(jnp.concatenate([dsb[0], dsb[1]], axis=1), by_head(qt))
            dvv[h] = dvv[h] + _dot(jnp.concatenate([pb[0], pb[1]], axis=1), by_head(dot.astype(F32)))
        for acc, ref in ((dkk, dk_ref), (dvv, dv_ref)):
            f0 = acc[0] + pltpu.roll(acc[0], HD, axis=1)
            f1 = acc[1] + pltpu.roll(acc[1], HD, axis=1)
            band = jnp.where(lo, f0, f1)
            ref[pl.ds(pstart, BLK), :] += band[0:BLK]
            ref[pl.ds(cstart, BLK), :] += band[BLK:2 * BLK]

        @pl.when(n == nblk - 1)
        def _():
            _relbias_grad(dss_ref, bk_ref[...], grb_ref)

    blk = pl.BlockSpec((BLK, 512), lambda i: (i, 0))
    kv = pl.BlockSpec((s, 128), lambda i: (0, 0))
    row8 = pl.BlockSpec((NQ, 128), lambda i: (0, 0))
    return pl.pallas_call(
        body, name="attn_bwd", grid=(nblk,),
        in_specs=[blk, blk, kv, kv, pl.BlockSpec((None, NQ, 2 * BLK, BLK), lambda i: (i, 0, 0, 0)),
                  pl.BlockSpec((None, NQ, BLK), lambda i: (i, 0, 0)), pl.BlockSpec((2 * BLK, BLK), lambda i: (0, 0))]
        + ([] if dep is None else [ANY]),
        out_specs=[blk, kv, kv, row8, row8],
        out_shape=[jax.ShapeDtypeStruct((s, 512), BF16), jax.ShapeDtypeStruct((s, 128), F32),
                   jax.ShapeDtypeStruct((s, 128), F32), jax.ShapeDtypeStruct((NQ, 128), F32),
                   jax.ShapeDtypeStruct((NQ, 128), F32)],
        scratch_shapes=[pltpu.VMEM((NQ, 2 * BLK, BLK), F32)],
        compiler_params=_cp(1))(q, do, k, v, probs, sink_share, bucket, *([] if dep is None else [dep]))


def _relbias_grad(ds_ref, bucket_v, o_ref):
    lane = lax.broadcasted_iota(jnp.int32, (NQ, 128), 1)
    head_row = lax.broadcasted_iota(jnp.int32, (NQ, BLK), 0)
    acc = jnp.zeros((NQ, 128), F32)
    for b in range(NBUCKET):
        sel = bucket_v == b
        stack = jnp.zeros((NQ, BLK), F32)
        for h in range(NQ):
            col_sums = jnp.sum(jnp.where(sel, ds_ref[h], 0.0), axis=0, keepdims=True)
            stack = jnp.where(head_row == h, col_sums, stack)
        acc = acc + jnp.where(lane == b, jnp.sum(stack, axis=1, keepdims=True), 0.0)
    o_ref[...] = acc


def _inproj_bwd(x, g1, du, dq, dk, dv, w_in, dx1, c_arr):
    s = x.shape[0]
    tm = min(TM, s)
    nt = s // tm
    cols = ((0, 512), (512, 1024), (1024, 1152), (1152, 1280))

    def body(c_ref, x_ref, g_ref, du_ref, dq_ref, dk_ref, dv_ref, w_ref, dx1_ref,
             gx_ref, own_ref, oth_ref, gg_ref, gw_ref):
        i = pl.program_id(0)

        @pl.when(i == 0)
        def _():
            gw_ref[...] = jnp.zeros_like(gw_ref)
            gg_ref[...] = jnp.zeros_like(gg_ref)

        gv = g_ref[...]
        r1, n1 = _rms(x_ref[...])
        h = (n1 * gv).astype(BF16)
        parts = (du_ref[...], dq_ref[...], dk_ref[...].astype(BF16), dv_ref[...].astype(BF16))
        dh = None
        for (a, b), dpart in zip(cols, parts):
            t = _dot(dpart, w_ref[a:b, :])
            dh = t if dh is None else dh + t
            gw_ref[a:b, :] += _dot_tn(dpart, h)
        dx, dg = _rms_bwd(r1, n1, gv, dh)
        gg_ref[...] += dg
        gx_ref[...] = dx1_ref[...] + dx

        @pl.when(i == nt - 1)
        def _():
            for k in range(NSH):
                _emit_halves(gw_ref, k * WIN_S, WIN_S, c_ref[0], own_ref.at[k], oth_ref.at[k])

    row = lambda w: pl.BlockSpec((tm, w), lambda i: (i, 0))
    vec = pl.BlockSpec((1, D), lambda i: (0, 0))
    full = pl.BlockSpec((IN_W, D), lambda i: (0, 0))
    half = pl.BlockSpec((NSH, WIN_S // 2, D), lambda i: (0, 0, 0))
    return pl.pallas_call(
        body, name="inproj_bwd", grid=(nt,),
        in_specs=[SMEM_SPEC, row(D), vec, row(512), row(512), row(128), row(128), full, row(D)],
        out_specs=[row(D), half, half, vec],
        out_shape=[jax.ShapeDtypeStruct((s, D), F32)] + _half_shapes(WIN_S) + [jax.ShapeDtypeStruct((1, D), F32)],
        scratch_shapes=[pltpu.VMEM((IN_W, D), F32)],
        compiler_params=_cp(1))(c_arr, x, g1, du, dq, dk, dv, w_in, dx1)


def _local_step(x, target, small, w_in, w_out, ffn_weights, c_arr, on_ffn_grads=None, on_wout_grads=None):
    g1, g2, g3, g4, w_pool, pool_scale, rel_bias, sinks = small
    bucket = jnp.asarray(_bucket_table())
    wp_bf = w_pool.astype(BF16)
    bias = _bias_table(bucket, rel_bias)
    h1 = _prenorm(x, g1)
    if callable(w_in):
        w_in = w_in(bias, h1)
    u, q, k, v = _inproj_fwd(h1, w_in)
    pool_o, pooled = _pool_fwd(u, wp_bf, pool_scale)
    attn_o, probs, sink_share = _attn_fwd(q, k, v, bias, sinks)
    g2_fwd = g2
    if callable(w_out):
        w_out, after = w_out(pool_o, attn_o)
        if after is not None:
            g2_fwd = g2 + after[:1, :1]
    mix, x1, h2 = _outproj_fwd(pool_o, attn_o, w_out, x, g2_fwd, g3)
    wg, wu, wd = ffn_weights(h2) if callable(ffn_weights) else ffn_weights
    gate, up, act_a, df, dy, loss, gg4 = _ffn_fwd(h2, wg, wu, wd, x1, target, g4)
    dgate, dup, dx1, dmix, gg3, gg2 = _ffn_bwd_act(df, gate, up, wg, wu, wd, dy, x1, mix, g3, g2)
    ffn_g = _ffn_bwd_w(h2, act_a, dgate, dup, df, c_arr)
    dep = None if on_ffn_grads is None else on_ffn_grads(ffn_g)
    dpool, dattn, wout_own, wout_oth = _outproj_bwd(dmix, w_out, pool_o, attn_o, c_arr, dep)
    dep = None if on_wout_grads is None else on_wout_grads(wout_own, wout_oth, dpool)
    du, gwp, gsc = _pool_bwd(pooled, dpool, wp_bf, pool_scale, dep)
    dq, dk, dv, grb, gsk = _attn_bwd(q, k, v, dattn, probs, sink_share, bucket, dep)
    gx, win_own, win_oth, gg1 = _inproj_bwd(x, g1, du, dq, dk, dv, w_in, dx1, c_arr)
    small_grads = (gg1, gg2, gg3, gg4, gwp, gsc, grb, gsk[:, 0].reshape(1, NQ))
    return loss, gx, small_grads, ((win_own, win_oth), (wout_own, wout_oth), ffn_g)


def _place():
    x, y, c = lax.axis_index("x"), lax.axis_index("y"), lax.axis_index("c")
    chips = ((1 - x, y), (x, 1 - y), (1 - x, 1 - y))
    return x, y, c, chips


def _remote(src, dst, ssem, rsem, dev):
    return pltpu.make_async_remote_copy(src_ref=src, dst_ref=dst, send_sem=ssem, recv_sem=rsem,
                                        device_id=dev, device_id_type=MESH_T)


def _to_sibling(arrs, name, after=()):
    nt, na = len(arrs), len(after)

    def body(*refs):
        srcs, dsts = refs[:nt], refs[nt + na:2 * nt + na]
        ssem, rsem = refs[2 * nt + na:]
        x, y, c, _ = _place()
        cps = [_remote(srcs[t], dsts[t], ssem.at[t], rsem.at[t], (x, y, 1 - c)) for t in range(nt)]
        for cp in cps:
            cp.start()
        for cp in cps:
            cp.wait()

    return pl.pallas_call(
        body, name=name, in_specs=[ANY] * (nt + na), out_specs=[ANY] * nt,
        out_shape=[jax.ShapeDtypeStruct(a.shape, a.dtype) for a in arrs],
        scratch_shapes=[pltpu.SemaphoreType.DMA((nt,)), pltpu.SemaphoreType.DMA((nt,))],
        compiler_params=_cp())(*arrs, *after)


HBM_SPEC = pl.BlockSpec(memory_space=pltpu.HBM)
SEM_SPEC = pl.BlockSpec(memory_space=pltpu.SEMAPHORE)
DATAFLOW = pltpu.SideEffectType.DATAFLOW_SIDE_EFFECTING


def _gather_copies(srcs, lands, ssem, rsem):
    x, y, c, chips = _place()
    me = 2 * x + y
    out = []
    for t in range(len(srcs)):
        half = srcs[t].shape[0] // 2
        mine = pl.ds(pl.multiple_of(c * half, 16), half)
        for j, (px, py) in enumerate(chips):
            k = 3 * t + j
            send = _remote(srcs[t].at[mine], lands[t].at[me, mine], ssem.at[k], rsem.at[k], (px, py, c))
            blk = lands[t].at[2 * px + py, mine]
            out.append((send, _remote(blk, blk, ssem.at[k], rsem.at[k], (px, py, c))))
    return out


def _scatter_copies(srcs, lands, ssem, rsem):
    x, y, c, chips = _place()
    out = []
    for t in range(len(srcs)):
        for j, (px, py) in enumerate(chips):
            k = 3 * t + j
            send = _remote(srcs[t].at[2 * px + py], lands[t].at[j], ssem.at[k], rsem.at[k], (px, py, c))
            out.append((send, _remote(lands[t].at[j], lands[t].at[j], ssem.at[k], rsem.at[k], (px, py, c))))
    return out


def _sibling_copies(srcs, lands, ssem, rsem):
    x, y, c, _ = _place()
    sib = (x, y, 1 - c)
    return [(_remote(srcs[t], lands[t], ssem.at[t], rsem.at[t], sib),
             _remote(lands[t], lands[t], ssem.at[t], rsem.at[t], sib)) for t in range(len(srcs))]


def _peer_copies(srcs, lands, ssem, rsem):
    x, y, c, _ = _place()
    me = 4 * x + 2 * y + c
    out = []
    for kk in range(1, 8):
        px, py, pc = x ^ (kk >> 2), y ^ ((kk >> 1) & 1), c ^ (kk & 1)
        send = _remote(srcs[0], lands[0].at[me], ssem.at[kk - 1], rsem.at[kk - 1], (px, py, pc))
        slot = lands[0].at[4 * px + 2 * py + pc]
        out.append((send, _remote(slot, slot, ssem.at[kk - 1], rsem.at[kk - 1], (px, py, pc))))
    return out


def _split_start(plan, ncopy, srcs, lands, after, name):
    ns, nbuf = len(srcs), len(srcs) + len(lands)
    extra = [] if after is None else [after]
    n_in = nbuf + len(extra)

    def body(*refs):
        ssem, rsem, token = refs[n_in], refs[n_in + 1], refs[-1]
        for send, _ in plan(refs[:ns], refs[ns:nbuf], ssem, rsem):
            send.start()
        token[...] = jnp.zeros_like(token)

    bufs = [pltpu.with_memory_space_constraint(a, pltpu.HBM) for a in list(srcs) + list(lands)]
    outs = pl.pallas_call(
        body, name=name, in_specs=[HBM_SPEC] * nbuf + [ANY] * len(extra),
        out_specs=[SEM_SPEC, SEM_SPEC] + [HBM_SPEC] * nbuf + [pl.BlockSpec(memory_space=pltpu.VMEM)],
        out_shape=[pltpu.SemaphoreType.DMA((ncopy,)), pltpu.SemaphoreType.DMA((ncopy,))]
        + [pltpu.HBM(a.shape, a.dtype) for a in bufs] + [jax.ShapeDtypeStruct((8, 128), F32)],
        input_output_aliases={i: 2 + i for i in range(nbuf)},
        compiler_params=pltpu.CompilerParams(has_side_effects=DATAFLOW))(*bufs, *extra)
    return outs[0], outs[1], outs[2:2 + ns], outs[2 + ns:2 + nbuf], outs[-1]


def _split_wait(plan, ssem, rsem, srcs, lands, after, name, only=None):
    ns, nbuf = len(srcs), len(srcs) + len(lands)

    def body(*refs):
        s_ref, r_ref = refs[nbuf], refs[nbuf + 1]
        for k, (send, recv) in enumerate(plan(refs[:ns], refs[ns:nbuf], s_ref, r_ref)):
            if only is None or k in only:
                send.wait_send()
                recv.wait_recv()

    outs = pl.pallas_call(
        body, name=name, in_specs=[HBM_SPEC] * nbuf + [SEM_SPEC, SEM_SPEC] + [ANY] * len(after),
        out_specs=[HBM_SPEC] * nbuf,
        out_shape=[pltpu.HBM(a.shape, a.dtype) for a in list(srcs) + list(lands)],
        input_output_aliases={i: i for i in range(nbuf)},
        compiler_params=pltpu.CompilerParams(has_side_effects=DATAFLOW))(*srcs, *lands, ssem, rsem, *after)
    return outs


def _gather_finish(lands, tag):
    nt = len(lands)

    def body(*refs):
        outs = refs[nt:2 * nt]
        dsend, drecv = refs[2 * nt:]
        x, y, c, chips = _place()
        sib = (x, y, 1 - c)
        sends = []
        for t in range(nt):
            half = outs[t].shape[1] // 2
            mine = pl.ds(pl.multiple_of(c * half, 16), half)
            for j, (px, py) in enumerate(chips):
                blk = outs[t].at[2 * px + py, mine]
                cp = _remote(blk, blk, dsend.at[t, j], drecv.at[t, j], sib)
                cp.start()
                sends.append(cp)
        for t in range(nt):
            half = outs[t].shape[1] // 2
            other = pl.ds(pl.multiple_of((1 - c) * half, 16), half)
            for j, (px, py) in enumerate(chips):
                blk = outs[t].at[2 * px + py, other]
                _remote(blk, blk, dsend.at[t, j], drecv.at[t, j], sib).wait_recv()
        for cp in sends:
            cp.wait_send()

    return pl.pallas_call(
        body, name="gather_finish_" + tag, in_specs=[ANY] * nt, out_specs=[ANY] * nt,
        out_shape=[jax.ShapeDtypeStruct(a.shape, a.dtype) for a in lands],
        input_output_aliases={t: t for t in range(nt)},
        scratch_shapes=[pltpu.SemaphoreType.DMA((nt, 3)), pltpu.SemaphoreType.DMA((nt, 3))],
        compiler_params=_cp())(*lands)


def _chip_partial(owns, recv, chip_arr, tag):
    nt = len(owns)

    def body(chip_ref, *refs):
        k = pl.program_id(0)
        for t in range(nt):
            p = refs[t][...] + refs[nt + t][...].astype(F32)
            refs[2 * nt + t][...] = p.astype(BF16)

            @pl.when(k == chip_ref[0])
            def _():
                refs[3 * nt + t][...] = p

    blk_specs = [pl.BlockSpec((None,) + a.shape[1:], lambda k, chip_ref: (k, 0, 0)) for a in owns]
    own_specs = [pl.BlockSpec(a.shape[1:], lambda k, chip_ref: (0, 0)) for a in owns]
    return pl.pallas_call(
        body, name="rs_chip_partial_" + tag,
        grid_spec=pltpu.PrefetchScalarGridSpec(num_scalar_prefetch=1, grid=(NSH,), in_specs=blk_specs * 2,
                                               out_specs=blk_specs + own_specs),
        out_shape=[jax.ShapeDtypeStruct(a.shape, BF16) for a in owns]
        + [jax.ShapeDtypeStruct(a.shape[1:], F32) for a in owns],
        compiler_params=_cp(1))(chip_arr, *owns, *recv)


def _sum_chips(own, recv, tag, after=()):
    nt = len(own)

    def body(*refs):
        outs = refs[2 * nt + len(after):]
        for t in range(nt):
            r = refs[nt + t]
            outs[t][...] = ((refs[t][...] + r[0].astype(F32)) + r[1].astype(F32)) + r[2].astype(F32)

    vm = pl.BlockSpec(memory_space=pltpu.VMEM)
    return pl.pallas_call(
        body, name="rs_sum_chips_" + tag, in_specs=[vm] * (2 * nt) + [ANY] * len(after), out_specs=[vm] * nt,
        out_shape=[jax.ShapeDtypeStruct(a.shape, F32) for a in own],
        compiler_params=_cp())(*own, *recv, *after)


def _adamw_math(w, g, m, v):
    m = ADAM_B1 * m + (1.0 - ADAM_B1) * g
    v = ADAM_B2 * v + (1.0 - ADAM_B2) * (g * g)
    m_hat = m / (1.0 - ADAM_B1 ** ADAM_STEP)
    v_hat = v / (1.0 - ADAM_B2 ** ADAM_STEP)
    delta = -ADAM_LR * (m_hat / (jnp.sqrt(v_hat) + ADAM_EPS) + ADAM_WD * w)
    return delta, m, v


ADAM_SPLIT = 4


def _adamw_shards(own, sib, ws, ms, vs, c_arr, tag):
    nt = len(own)

    def body(c_ref, *refs):
        hh = pl.program_id(0)
        mine = hh == c_ref[0]
        for t in range(nt):
            g = jnp.where(mine, refs[t][...], refs[nt + t][...])
            delta, m, v = _adamw_math(refs[2 * nt + t][...], g, refs[3 * nt + t][...], refs[4 * nt + t][...])
            refs[5 * nt + t][...] = g
            refs[6 * nt + t][...] = delta
            refs[7 * nt + t][...] = m
            refs[8 * nt + t][...] = v

    def tile(a):
        return (a.shape[0] // ADAM_SPLIT, a.shape[1])

    half_specs = [pl.BlockSpec(tile(a), lambda hh, q, c_ref: (q, 0)) for a in own]
    full_specs = [pl.BlockSpec(tile(a), lambda hh, q, c_ref: (hh * ADAM_SPLIT + q, 0)) for a in own]
    return pl.pallas_call(
        body, name="adamw_shards_" + tag,
        grid_spec=pltpu.PrefetchScalarGridSpec(num_scalar_prefetch=1, grid=(2, ADAM_SPLIT),
                                               in_specs=half_specs * 2 + full_specs * 3, out_specs=full_specs * 4),
        out_shape=[jax.ShapeDtypeStruct(w.shape, F32) for w in ws] * 4,
        compiler_params=_cp(2))(c_arr, *own, *sib, *ws, *ms, *vs)


SMALL_WPOOL_ROW = 32
SMALL_SCALE_ROW = SMALL_WPOOL_ROW + 4 * GROUP
SMALL_BIAS_ROW = SMALL_SCALE_ROW + 8
SMALL_SINKS_ROW = SMALL_BIAS_ROW + NQ
SMALL_LOSS_ROW = SMALL_SINKS_ROW + 8


def _small_sum_adamw(gathered, wp, mp, vp):
    rows = wp.shape[0]

    def body(g_ref, w_ref, m_ref, v_ref, *rest):
        outs, res = rest[:-1], rest[-1]
        g = g_ref[0]
        for d in range(1, 8):
            g = g + g_ref[d]
        delta, m, v = _adamw_math(w_ref[...], g, m_ref[...], v_ref[...])
        for kind, val in enumerate((g, delta, m, v)):
            res[kind] = val
            gains = outs[8 * kind:8 * kind + 4]
            w_pool_o, scale_o, bias_o, sinks_o = outs[8 * kind + 4:8 * kind + 8]
            for t in range(4):
                for i in range(8):
                    gains[t][:, 128 * i:128 * (i + 1)] = res[kind, pl.ds(8 * t + i, 1), :]
            for grp in range(4):
                w_pool_o[grp] = res[kind, pl.ds(SMALL_WPOOL_ROW + GROUP * grp, GROUP), :]
            for i in range(4):
                scale_o[:, 128 * i:128 * (i + 1)] = res[kind, pl.ds(SMALL_SCALE_ROW + i, 1), :]
            bias_o[...] = res[kind, pl.ds(SMALL_BIAS_ROW, NQ), :][:, 0:NBUCKET]
            sinks_o[...] = res[kind, pl.ds(SMALL_SINKS_ROW, 1), :][:, 0:NQ]
        outs[-1][...] = res[0, pl.ds(SMALL_LOSS_ROW, 1), :]

    vm = pl.BlockSpec(memory_space=pltpu.VMEM)
    one_kind = [jax.ShapeDtypeStruct((1, D), F32)] * 4 + [
        jax.ShapeDtypeStruct((4, GROUP, GROUP), F32), jax.ShapeDtypeStruct((1, POOL_W), F32),
        jax.ShapeDtypeStruct((NQ, NBUCKET), F32), jax.ShapeDtypeStruct((1, NQ), F32)]
    return pl.pallas_call(
        body, name="small_sum_adamw", in_specs=[vm] * 4, out_specs=[vm] * 33,
        out_shape=one_kind * 4 + [jax.ShapeDtypeStruct((1, 128), F32)],
        scratch_shapes=[pltpu.VMEM((4, rows, 128), F32)],
        compiler_params=_cp())(gathered, wp, mp, vp)


def _pack_small(gains4, w_pool, pool_scale, rel_bias_t, sinks, loss):
    bias_rows = jnp.pad(rel_bias_t, ((0, 0), (0, 128 - rel_bias_t.shape[1])))
    parts = list(gains4) + [w_pool, pool_scale, bias_rows, sinks, loss]
    rows = []
    for a in parts:
        flat = a.reshape(-1)
        n = flat.shape[0]
        padded = -(-n // 1024) * 1024
        rows.append(jnp.pad(flat, (0, padded - n)).reshape(-1, 128))
    return jnp.concatenate(rows, axis=0)


def kernel(x, g_pre_mix, w_in, w_pool, pool_scale, rel_bias, sinks, w_out, g_post_mix, g_pre_ffn, w_gate, w_up, w_down, g_post_ffn, loss_target, m_g_pre_mix, m_w_in, m_w_pool, m_pool_scale, m_rel_bias, m_sinks, m_w_out, m_g_post_mix, m_g_pre_ffn, m_w_gate, m_w_up, m_w_down, m_g_post_ffn, v_g_pre_mix, v_w_in, v_w_pool, v_pool_scale, v_rel_bias, v_sinks, v_w_out, v_g_post_mix, v_g_pre_ffn, v_w_gate, v_w_up, v_w_down, v_g_post_ffn):
    c = lax.axis_index("c")
    chip = 2 * lax.axis_index("x") + lax.axis_index("y")

    def shards(a_in, a_out, a_gate, a_up, a_down):
        return (a_in[0].T, a_out[0], a_gate[0].T, a_up[0].T, a_down[0])

    c_arr = c.reshape(1).astype(jnp.int32)
    chip_arr = chip.reshape(1).astype(jnp.int32)

    big_w = shards(w_in, w_out, w_gate, w_up, w_down)
    big_bf = [w.astype(BF16) for w in big_w]
    g_ssem, g_rsem, g_srcs, g_lands, _ = _split_start(
        _gather_copies, 15, big_bf, [lax.empty((NSH,) + a.shape, BF16) for a in big_bf], None, "gather_start")
    fw = {}

    def with_own(land, shard):
        return lax.dynamic_update_slice(land, shard[None], (chip, 0, 0))

    def w_in_ready(*after):
        fw["first"] = _split_wait(_gather_copies, g_ssem, g_rsem, g_srcs, g_lands, after, "gather_win_wait",
                                  only=(0, 1, 2))
        (fw["win"],) = _gather_finish([with_own(fw["first"][5], fw["first"][0])], "win")
        return fw["win"].reshape(IN_W, D)

    def w_out_ready(*after):
        first = fw["first"]
        fw["second"] = _split_wait(_gather_copies, g_ssem, g_rsem, first[:5], [fw["win"]] + list(first[6:]), after,
                                   "gather_wout_wait", only=(3, 4, 5))
        (fw["wout"],) = _gather_finish([with_own(fw["second"][6], fw["second"][1])], "wout")
        return fw["wout"].reshape(D, D), None

    def ffn_weights(after):
        second = fw["second"]
        outs = _split_wait(_gather_copies, g_ssem, g_rsem, second[:5], [second[5], fw["wout"]] + list(second[7:]),
                           [after], "gather_ffn_wait", only=tuple(range(6, 15)))
        return _gather_finish([with_own(land, src) for src, land in zip(outs[2:5], outs[7:])], "ffn")

    rs = {}

    def on_ffn_grads(ffn_g):
        oths = ffn_g[1::2]
        rs["ffn_own"] = ffn_g[0::2]
        rs["x"] = _split_start(_sibling_copies, 3, oths, [lax.empty(a.shape, BF16) for a in oths], ffn_g[0],
                               "rs_exchange_ffn_start")
        return rs["x"][4]

    def on_wout_grads(own, oth, after):
        ssem, rsem, srcs, lands, _ = rs["x"]
        recv_ffn = _split_wait(_sibling_copies, ssem, rsem, srcs, lands, [after], "rs_exchange_ffn_wait")[3:]
        recv_wout = _to_sibling([oth], "rs_exchange_wout")
        outs = _chip_partial([own] + list(rs["ffn_own"]), list(recv_wout) + list(recv_ffn), chip_arr, "early")
        rs["early_own"] = outs[4:]
        rs["s"] = _split_start(_scatter_copies, 12, outs[:4],
                               [lax.empty((3,) + a.shape[1:], BF16) for a in outs[:4]], outs[4], "scatter_early_start")
        return rs["s"][4]

    small = (g_pre_mix, g_post_mix, g_pre_ffn, g_post_ffn, w_pool[0], pool_scale, rel_bias, sinks)
    loss, gx, small_grads, ((win_own, win_oth), _, _) = _local_step(
        x[0], loss_target[0], small, w_in_ready, w_out_ready, ffn_weights, c_arr, on_ffn_grads, on_wout_grads)

    ssem, rsem, srcs, lands, _ = rs["s"]
    recv_early = _split_wait(_scatter_copies, ssem, rsem, srcs, lands, [gx], "scatter_early_wait")[4:]
    finals = _sum_chips(list(rs["early_own"]), list(recv_early), "early")
    sh_ssem, sh_rsem, sh_srcs, sh_lands, sh_token = _split_start(
        _sibling_copies, 4, finals, [lax.empty(a.shape, F32) for a in finals], finals[0], "rs_share_early_start")

    unused = jnp.zeros((1, 1), F32)
    gp = _pack_small(small_grads[:4], *small_grads[4:], loss)
    wp = _pack_small((g_pre_mix, g_post_mix, g_pre_ffn, g_post_ffn), w_pool, pool_scale, rel_bias.T, sinks, unused)
    mp = _pack_small((m_g_pre_mix, m_g_post_mix, m_g_pre_ffn, m_g_post_ffn), m_w_pool, m_pool_scale, m_rel_bias.T,
                     m_sinks, unused)
    vp = _pack_small((v_g_pre_mix, v_g_post_mix, v_g_pre_ffn, v_g_post_ffn), v_w_pool, v_pool_scale, v_rel_bias.T,
                     v_sinks, unused)

    moments = (shards(m_w_in, m_w_out, m_w_gate, m_w_up, m_w_down),
               shards(v_w_in, v_w_out, v_w_gate, v_w_up, v_w_down))
    recv_a = _to_sibling([win_oth], "rs_exchange_win", [sh_token])
    outs = _chip_partial([win_own], recv_a, chip_arr, "win")
    w_ssem, w_rsem, w_srcs, w_lands, w_token = _split_start(
        _scatter_copies, 3, outs[:1], [lax.empty((3,) + outs[0].shape[1:], BF16)], gx, "scatter_win_start")
    slots = lax.dynamic_update_slice(lax.empty((8,) + gp.shape, F32), gp[None], (2 * chip + c, 0, 0))
    p_ssem, p_rsem, p_srcs, p_lands, p_token = _split_start(_peer_copies, 7, [gp], [slots], w_token,
                                                            "small_allgather_start")
    shared = _split_wait(_sibling_copies, sh_ssem, sh_rsem, sh_srcs, sh_lands, [p_token], "rs_share_early_wait")
    adam_e = _adamw_shards(shared[:4], shared[4:], big_w[1:], moments[0][1:], moments[1][1:], c_arr, "early")
    recv_win = _split_wait(_scatter_copies, w_ssem, w_rsem, w_srcs, w_lands, [adam_e[0]], "scatter_win_wait")[1:]
    win_final = _sum_chips([outs[1]], recv_win, "win")
    win_sib = _to_sibling(win_final, "rs_share_win")
    adam_w = _adamw_shards(win_final, win_sib, big_w[:1], moments[0][:1], moments[1][:1], c_arr, "win")
    big_g, big_d, big_m, big_v = [[adam_w[i]] + list(adam_e[4 * i:4 * i + 4]) for i in range(4)]

    gathered = _split_wait(_peer_copies, p_ssem, p_rsem, p_srcs, p_lands, [adam_w[0]], "small_allgather_wait")[1]
    flat = _small_sum_adamw(gathered, wp, mp, vp)
    small_out = [flat[8 * kind:8 * kind + 8] for kind in range(4)]
    total_loss = flat[-1][0, 0]

    def ordered(small8, big4):
        sg1, sg2, sg3, sg4, swp, ssc, srb, ssk = small8
        swp, srb = swp[None], srb.T
        bwin, bwout, bwg, bwu, bwd = big4[0].T[None], big4[1][None], big4[2].T[None], big4[3].T[None], big4[4][None]
        return [sg1, bwin, swp, ssc, srb, ssk, bwout, sg2, sg3, bwg, bwu, bwd, sg4]

    res = [total_loss, gx[None]]
    for sm, bg in zip(small_out, (big_g, big_d, big_m, big_v)):
        res += ordered(sm, bg)
    return tuple(res)
```

```python
import numpy as np
import jax
import jax.numpy as jnp
from jax import lax
from jax.experimental import pallas as pl
from jax.experimental.pallas import tpu as pltpu

F32 = jnp.float32
BF16 = jnp.bfloat16

D = 1024
POOL_W = 512
GROUP = 128
WINDOWS = (2, 4, 8, 16)
HALO = 128
NQ = 8
BLK = 128
NBUCKET = 32
IN_W = 1280
DFF = 2816
NSH = 4
FS = DFF // NSH
WIN_S = IN_W // NSH
WOUT_S = D // NSH
EPS = 1e-6
NEG = -1e30
SCALE = 0.125

ADAM_LR = 0.001
ADAM_B1 = 0.9
ADAM_B2 = 0.999
ADAM_EPS = 1e-08
ADAM_WD = 0.01
ADAM_STEP = 10

TM = 512
TM_POOL = 512
TM_FFN = 512
TM_FFN_FWD = 1024
FFN_ROWS = 256
VMEM_LIMIT = 60 * 1024 * 1024

MESH_T = pl.DeviceIdType.MESH
ANY = pl.BlockSpec(memory_space=pl.ANY)


def _cp(n_grid=0, **kw):
    sem = ("arbitrary",) * n_grid if n_grid else None
    return pltpu.CompilerParams(dimension_semantics=sem, vmem_limit_bytes=VMEM_LIMIT, **kw)


def _dot(a, b):
    return jnp.dot(a, b, preferred_element_type=F32)


def _dot_nt(a, b):
    return lax.dot_general(a, b, (((1,), (1,)), ((), ())), preferred_element_type=F32)


def _dot_tn(a, b):
    return lax.dot_general(a, b, (((0,), (0,)), ((), ())), preferred_element_type=F32)


def _rms(x):
    r = lax.rsqrt(jnp.mean(x * x, axis=-1, keepdims=True) + EPS)
    return r, x * r


def _rms_bwd(r, n, g, dout):
    dn = dout * g
    dx = r * (dn - n * jnp.mean(dn * n, axis=-1, keepdims=True))
    dg = jnp.sum(dout * n, axis=0, keepdims=True)
    return dx, dg


def _split(x, n):
    parts = []
    r = x
    for _ in range(n):
        p = r.astype(BF16)
        parts.append(p)
        r = r - p.astype(F32)
    return parts


def _band_dot(a, x, n):
    acc = None
    for p in _split(x, n):
        t = _dot(a, p)
        acc = t if acc is None else acc + t
    return acc


def _bucket_table():
    qi = np.arange(BLK)[None, :]
    kj = np.arange(2 * BLK)[:, None]
    dist = qi + BLK - kj
    n = np.maximum(dist, 0)
    nf = np.maximum(n, 1).astype(np.float32)
    large = 16 + (np.log(nf / np.float32(16)) / np.float32(np.log(128 / 16)) * np.float32(16)).astype(np.int32)
    large = np.minimum(large, NBUCKET - 1)
    return np.where(n < 16, n, large).astype(np.int32)


def _prenorm(x, g1):
    s = x.shape[0]
    tm = min(TM, s)

    def body(x_ref, g_ref, h_ref):
        _, n = _rms(x_ref[...])
        h_ref[...] = (n * g_ref[...]).astype(BF16)

    row = pl.BlockSpec((tm, D), lambda i: (i, 0))
    return pl.pallas_call(
        body, name="prenorm", grid=(s // tm,),
        in_specs=[row, pl.BlockSpec((1, D), lambda i: (0, 0))], out_specs=row,
        out_shape=jax.ShapeDtypeStruct((s, D), BF16),
        compiler_params=_cp(1))(x, g1)


def _inproj_fwd(h1, w_in):
    s = h1.shape[0]
    tm = min(TM, s)

    def body(h_ref, w_ref, u_ref, q_ref, k_ref, v_ref):
        proj = _dot_nt(h_ref[...], w_ref[...])
        u_ref[...] = proj[:, :512]
        q_ref[...] = proj[:, 512:1024].astype(BF16)
        k_ref[...] = proj[:, 1024:1152].astype(BF16)
        v_ref[...] = proj[:, 1152:1280].astype(BF16)

    row = lambda w: pl.BlockSpec((tm, w), lambda i: (i, 0))
    return pl.pallas_call(
        body, name="inproj_fwd", grid=(s // tm,),
        in_specs=[row(D), pl.BlockSpec((IN_W, D), lambda i: (0, 0))],
        out_specs=[row(512), row(512), row(128), row(128)],
        out_shape=[jax.ShapeDtypeStruct((s, 512), F32), jax.ShapeDtypeStruct((s, 512), BF16),
                   jax.ShapeDtypeStruct((s, 128), BF16), jax.ShapeDtypeStruct((s, 128), BF16)],
        compiler_params=_cp(1))(h1, w_in)


def _window_sums(ext, w, lag_sign, tm, terms):
    rows = lax.broadcasted_iota(jnp.int32, (HALO, 2 * HALO), 0)
    cols = lax.broadcasted_iota(jnp.int32, (HALO, 2 * HALO), 1)
    d = rows + HALO - cols if lag_sign > 0 else cols - rows
    band = jnp.where((d >= 0) & (d < w), 1.0, 0.0).astype(BF16)
    return jnp.concatenate([_band_dot(band, ext[r:r + 2 * HALO], terms) for r in range(0, tm, HALO)], axis=0)


def _pooled(ext, cur, t0, tm):
    t = t0 + lax.broadcasted_iota(jnp.int32, (tm, GROUP), 0)
    out = []
    for g, w in enumerate(WINDOWS):
        sl = slice(g * GROUP, (g + 1) * GROUP)
        cnt = jnp.minimum(t + 1, w).astype(F32)
        out.append(_window_sums(ext[:, sl], w, 1, tm, 2) / cnt - cur[:, sl])
    return out


def _pool_fwd(u, w_pool, pool_scale):
    s = u.shape[0]
    tm = min(TM_POOL, s)
    hb = tm // HALO

    def body(uc_ref, uh_ref, wp_ref, sc_ref, o_ref, pooled_ref):
        i = pl.program_id(0)
        cur = uc_ref[...]
        halo = jnp.where(i > 0, uh_ref[...], 0.0)
        ext = jnp.concatenate([halo, cur], axis=0)
        pooled = _pooled(ext, cur, i * tm, tm)
        for g in range(4):
            sl = slice(g * GROUP, (g + 1) * GROUP)
            pb = pooled[g].astype(BF16)
            pooled_ref[:, sl] = pb
            o_ref[:, sl] = (_dot(pb, wp_ref[g]) * sc_ref[:, sl]).astype(BF16)

    row = pl.BlockSpec((tm, 512), lambda i: (i, 0))
    return pl.pallas_call(
        body, name="pool_fwd", grid=(s // tm,),
        in_specs=[row, pl.BlockSpec((HALO, 512), lambda i: (jnp.maximum(i * hb - 1, 0), 0)),
                  pl.BlockSpec((4, GROUP, GROUP), lambda i: (0, 0, 0)),
                  pl.BlockSpec((1, 512), lambda i: (0, 0))],
        out_specs=[row, row],
        out_shape=[jax.ShapeDtypeStruct((s, 512), BF16)] * 2,
        compiler_params=_cp(1))(u, u, w_pool, pool_scale)


def _bias_table(bucket, rel_bias):
    def body(b_ref, rb_ref, o_ref):
        bucket_v = b_ref[...]
        key = lax.broadcasted_iota(jnp.int32, (2 * BLK, BLK), 0)
        dist = lax.broadcasted_iota(jnp.int32, (2 * BLK, BLK), 1) + BLK - key
        inwin = (dist >= 0) & (dist < BLK)
        for h in range(NQ):
            acc = jnp.zeros((2 * BLK, BLK), F32)
            for b in range(NBUCKET):
                acc = jnp.where(bucket_v == b, rb_ref[b, h], acc)
            rest = jnp.where(inwin, acc, NEG)
            o_ref[1, h] = rest
            o_ref[0, h] = jnp.where(key >= BLK, rest, NEG)

    return pl.pallas_call(
        body, name="bias_table",
        in_specs=[pl.BlockSpec(memory_space=pltpu.VMEM), pl.BlockSpec(memory_space=pltpu.SMEM)],
        out_specs=pl.BlockSpec(memory_space=pltpu.VMEM),
        out_shape=jax.ShapeDtypeStruct((2, NQ, 2 * BLK, BLK), F32),
        compiler_params=_cp())(bucket, rel_bias)


HD = 64


def _kv_band(ref, n, transposed):
    pstart = pl.multiple_of(jnp.maximum(n - 1, 0) * BLK, BLK)
    cstart = pl.multiple_of(n * BLK, BLK)
    lo = lax.broadcasted_iota(jnp.int32, (2 * BLK, 128), 1) < HD
    band = jnp.concatenate([ref[pl.ds(pstart, BLK), :], ref[pl.ds(cstart, BLK), :]], axis=0).astype(F32)
    rot = pltpu.roll(band, HD, axis=1)
    halves = ((jnp.where(lo, band, 0.0), jnp.where(lo, 0.0, rot)), (jnp.where(lo, rot, 0.0), jnp.where(lo, 0.0, band)))
    if transposed:
        out = [jnp.concatenate([a.T, b.T], axis=1).astype(BF16) for a, b in halves]
    else:
        out = [jnp.concatenate([a, b], axis=0).astype(BF16) for a, b in halves]
    return out, (lo, pstart, cstart)


def _pair_probs(qt, kk, bias, sk_ref, p):
    sink = jnp.concatenate([jnp.full((1, 1, BLK), sk_ref[0, 2 * p + e], F32) for e in range(2)], axis=0)
    s = _dot_nt(kk, qt).reshape(2, 2 * BLK, BLK) + bias
    m = jnp.maximum(jnp.max(s, axis=1, keepdims=True), sink)
    pr = jnp.exp(s - m)
    es = jnp.exp(sink - m)
    inv = 1.0 / (jnp.sum(pr, axis=1, keepdims=True) + es)
    return pr * inv, es * inv


def _attn_fwd(q, k, v, bias, sinks):
    s = q.shape[0]
    nblk = s // BLK

    def body(q_ref, k_ref, v_ref, b_ref, sk_ref, o_ref, prob_ref, ps_ref):
        n = pl.program_id(0)
        kk, _ = _kv_band(k_ref, n, False)
        vt, _ = _kv_band(v_ref, n, True)
        bidx = jnp.minimum(n, 1)
        for p in range(4):
            h = p // 2
            qt = (q_ref[:, p * 128:(p + 1) * 128].astype(F32) * SCALE).astype(BF16)
            prob, ps = _pair_probs(qt, kk[h], b_ref[bidx, pl.ds(2 * p, 2)], sk_ref, p)
            pb = prob.astype(BF16)
            prob_ref[pl.ds(2 * p, 2)] = pb
            ps_ref[pl.ds(2 * p, 2), :] = ps.reshape(2, BLK)
            acc_t = _dot(vt[h], pb.reshape(4 * BLK, BLK))
            o_ref[:, p * 128:(p + 1) * 128] = acc_t.T.astype(BF16)

    return pl.pallas_call(
        body, name="attn_fwd", grid=(nblk,),
        in_specs=[pl.BlockSpec((BLK, 512), lambda i: (i, 0)),
                  pl.BlockSpec((s, 128), lambda i: (0, 0)),
                  pl.BlockSpec((s, 128), lambda i: (0, 0)),
                  pl.BlockSpec((2, NQ, 2 * BLK, BLK), lambda i: (0, 0, 0, 0)),
                  pl.BlockSpec(memory_space=pltpu.SMEM)],
        out_specs=[pl.BlockSpec((BLK, 512), lambda i: (i, 0)),
                   pl.BlockSpec((None, NQ, 2 * BLK, BLK), lambda i: (i, 0, 0, 0)),
                   pl.BlockSpec((None, NQ, BLK), lambda i: (i, 0, 0))],
        out_shape=[jax.ShapeDtypeStruct((s, 512), BF16), jax.ShapeDtypeStruct((nblk, NQ, 2 * BLK, BLK), BF16),
                   jax.ShapeDtypeStruct((nblk, NQ, BLK), F32)],
        compiler_params=_cp(1))(q, k, v, bias, sinks)


def _outproj_fwd(pool_o, attn_o, w_out, x, g2, g3):
    s = x.shape[0]
    tm = min(TM, s)

    def body(p_ref, a_ref, w_ref, x_ref, g2_ref, g3_ref, mix_ref, x1_ref, h2_ref):
        mix = _dot(p_ref[...], w_ref[0:512, :]) + _dot(a_ref[...], w_ref[512:1024, :])
        mix_ref[...] = mix
        _, n2 = _rms(mix)
        x1 = x_ref[...] + n2 * g2_ref[...]
        x1_ref[...] = x1
        _, n3 = _rms(x1)
        h2_ref[...] = (n3 * g3_ref[...]).astype(BF16)

    row = lambda w: pl.BlockSpec((tm, w), lambda i: (i, 0))
    vec = pl.BlockSpec((1, D), lambda i: (0, 0))
    return pl.pallas_call(
        body, name="outproj_fwd", grid=(s // tm,),
        in_specs=[row(512), row(512), pl.BlockSpec((D, D), lambda i: (0, 0)), row(D), vec, vec],
        out_specs=[row(D), row(D), row(D)],
        out_shape=[jax.ShapeDtypeStruct((s, D), F32), jax.ShapeDtypeStruct((s, D), F32),
                   jax.ShapeDtypeStruct((s, D), BF16)],
        compiler_params=_cp(1))(pool_o, attn_o, w_out, x, g2, g3)


def _silu_parts(g):
    sg = jax.nn.sigmoid(g)
    return sg, g * sg


def _ffn_fwd(h2, wg, wu, wd, x1, target, g4):
    s = h2.shape[0]
    tm = min(TM_FFN, s)

    def body(h_ref, wg_ref, wu_ref, wd_ref, x1_ref, t_ref, g4_ref,
             gate_ref, up_ref, a_ref, df_ref, dy_ref, loss_ref, gg4_ref, f_acc):
        j = pl.program_id(0)
        i = pl.program_id(1)
        first = j == 0
        tile = pl.multiple_of(i * tm, tm)
        chunks = [pl.ds(r, min(FFN_ROWS, tm)) for r in range(0, tm, FFN_ROWS)]
        project = lambda rows: (_dot_nt(h_ref[rows, :], wg_ref[...]), _dot_nt(h_ref[rows, :], wu_ref[...]))
        ahead = [project(chunks[0])]
        for k, rows in enumerate(chunks):
            if k + 1 < len(chunks):
                ahead.append(project(chunks[k + 1]))
            gate, up = ahead[k]
            gate_ref[rows, :] = gate.astype(BF16)
            up_ref[rows, :] = up.astype(BF16)
            _, sl = _silu_parts(gate)
            a = (sl * up).astype(BF16)
            a_ref[rows, :] = a
            contrib = _dot(a, wd_ref[...])
            acc_rows = pl.ds(tile + k * FFN_ROWS, rows.size)
            f_acc[acc_rows, :] = jnp.where(first, contrib, f_acc[acc_rows, :] + contrib)

        @pl.when((i == 0) & (j == 0))
        def _():
            loss_ref[...] = jnp.zeros_like(loss_ref)
            gg4_ref[...] = jnp.zeros_like(gg4_ref)

        @pl.when(j == NSH - 1)
        def _():
            r4, n4 = _rms(f_acc[pl.ds(tile, tm), :])
            g4v = g4_ref[...]
            err = x1_ref[...] + n4 * g4v - t_ref[...]
            loss_ref[...] += (0.5 / D) * jnp.sum(err * err).reshape(1, 1)
            dy = err * (1.0 / D)
            dy_ref[...] = dy
            df, dg = _rms_bwd(r4, n4, g4v, dy)
            gg4_ref[...] += dg
            df_ref[...] = df.astype(BF16)

    row = pl.BlockSpec((tm, D), lambda j, i: (i, 0))
    last = pl.BlockSpec((tm, D), lambda j, i: (i * (j // (NSH - 1)), 0))
    sh = pl.BlockSpec((None, FS, D), lambda j, i: (j, 0, 0))
    act = pl.BlockSpec((None, tm, FS), lambda j, i: (j, i, 0))
    vec = pl.BlockSpec((1, D), lambda j, i: (0, 0))
    return pl.pallas_call(
        body, name="ffn_fwd", grid=(NSH, s // tm),
        in_specs=[row, sh, sh, sh, last, last, vec],
        out_specs=[act, act, act, last, last, pl.BlockSpec((1, 1), lambda j, i: (0, 0)), vec],
        out_shape=[jax.ShapeDtypeStruct((NSH, s, FS), BF16)] * 3
        + [jax.ShapeDtypeStruct((s, D), BF16), jax.ShapeDtypeStruct((s, D), F32),
           jax.ShapeDtypeStruct((1, 1), F32), jax.ShapeDtypeStruct((1, D), F32)],
        scratch_shapes=[pltpu.VMEM((s, D), F32)],
        compiler_params=_cp(2))(h2, wg, wu, wd, x1, target, g4)


def _ffn_bwd_act(df, gate, up, wg, wu, wd, dy, x1, mix, g3, g2):
    s = df.shape[0]
    tm = min(TM_FFN, s)

    def body(df_ref, gate_ref, up_ref, wg_ref, wu_ref, wd_ref, dy_ref, x1_ref, mix_ref, g3_ref, g2_ref,
             dgate_ref, dup_ref, dx1_ref, dmix_ref, gg3_ref, gg2_ref, acc):
        j = pl.program_id(0)
        i = pl.program_id(1)
        first = j == 0
        tile = pl.multiple_of(i * tm, tm)
        chunks = [pl.ds(r, min(FFN_ROWS, tm)) for r in range(0, tm, FFN_ROWS)]
        das = [_dot_nt(df_ref[chunks[0], :], wd_ref[...])]
        for k, rows in enumerate(chunks):
            if k + 1 < len(chunks):
                das.append(_dot_nt(df_ref[chunks[k + 1], :], wd_ref[...]))
            da = das[k]
            g = gate_ref[rows, :].astype(F32)
            u = up_ref[rows, :].astype(F32)
            sg, sl = _silu_parts(g)
            dgate = (da * u * (sg * (1.0 + g * (1.0 - sg)))).astype(BF16)
            dup = (da * sl).astype(BF16)
            dgate_ref[rows, :] = dgate
            dup_ref[rows, :] = dup
            contrib = _dot(dgate, wg_ref[...]) + _dot(dup, wu_ref[...])
            acc_rows = pl.ds(tile + k * FFN_ROWS, rows.size)
            acc[acc_rows, :] = jnp.where(first, contrib, acc[acc_rows, :] + contrib)

        @pl.when((i == 0) & (j == 0))
        def _():
            gg3_ref[...] = jnp.zeros_like(gg3_ref)
            gg2_ref[...] = jnp.zeros_like(gg2_ref)

        @pl.when(j == NSH - 1)
        def _():
            r3, n3 = _rms(x1_ref[...])
            dx1n, dg3 = _rms_bwd(r3, n3, g3_ref[...], acc[pl.ds(tile, tm), :])
            dx1 = dy_ref[...] + dx1n
            dx1_ref[...] = dx1
            gg3_ref[...] += dg3
            r2, n2 = _rms(mix_ref[...])
            dmix, dg2 = _rms_bwd(r2, n2, g2_ref[...], dx1)
            gg2_ref[...] += dg2
            dmix_ref[...] = dmix.astype(BF16)

    row = pl.BlockSpec((tm, D), lambda j, i: (i, 0))
    last = pl.BlockSpec((tm, D), lambda j, i: (i * (j // (NSH - 1)), 0))
    sh = pl.BlockSpec((None, FS, D), lambda j, i: (j, 0, 0))
    act = pl.BlockSpec((None, tm, FS), lambda j, i: (j, i, 0))
    vec = pl.BlockSpec((1, D), lambda j, i: (0, 0))
    return pl.pallas_call(
        body, name="ffn_bwd_act", grid=(NSH, s // tm),
        in_specs=[row, act, act, sh, sh, sh, last, last, last, vec, vec],
        out_specs=[act, act, last, last, vec, vec],
        out_shape=[jax.ShapeDtypeStruct((NSH, s, FS), BF16), jax.ShapeDtypeStruct((NSH, s, FS), BF16),
                   jax.ShapeDtypeStruct((s, D), F32), jax.ShapeDtypeStruct((s, D), BF16),
                   jax.ShapeDtypeStruct((1, D), F32), jax.ShapeDtypeStruct((1, D), F32)],
        scratch_shapes=[pltpu.VMEM((s, D), F32)],
        compiler_params=_cp(2))(df, gate, up, wg, wu, wd, dy, x1, mix, g3, g2)


SMEM_SPEC = pl.BlockSpec(memory_space=pltpu.SMEM)


def _emit_halves(acc_ref, row0, rows, c, own_ref, oth_ref):
    half = rows // 2
    own_ref[...] = acc_ref[pl.ds(row0 + pl.multiple_of(c * half, 8), half), :]
    oth_ref[...] = acc_ref[pl.ds(row0 + pl.multiple_of((1 - c) * half, 8), half), :].astype(BF16)


def _half_shapes(rows):
    return [jax.ShapeDtypeStruct((NSH, rows // 2, D), F32), jax.ShapeDtypeStruct((NSH, rows // 2, D), BF16)]


def _ffn_bwd_w(h2, act_a, dgate, dup, df, c_arr):
    s = h2.shape[0]
    tm = min(TM_FFN_FWD, s)
    nt = s // tm

    def body(c_ref, h_ref, a_ref, dgate_ref, dup_ref, df_ref, *rest):
        outs, accs = rest[:6], rest[6:]
        i = pl.program_id(1)

        @pl.when(i == 0)
        def _():
            for acc in accs:
                acc[...] = jnp.zeros_like(acc)

        h = h_ref[...]
        accs[0][...] += _dot_tn(dgate_ref[...], h)
        accs[1][...] += _dot_tn(dup_ref[...], h)
        accs[2][...] += _dot_tn(a_ref[...], df_ref[...])

        @pl.when(i == nt - 1)
        def _():
            for t, acc in enumerate(accs):
                _emit_halves(acc, 0, FS, c_ref[0], outs[2 * t], outs[2 * t + 1])

    row = lambda w: pl.BlockSpec((tm, w), lambda j, i: (i, 0))
    act = pl.BlockSpec((None, tm, FS), lambda j, i: (j, i, 0))
    half = pl.BlockSpec((None, FS // 2, D), lambda j, i: (j, 0, 0))
    return pl.pallas_call(
        body, name="ffn_bwd_w", grid=(NSH, nt),
        in_specs=[SMEM_SPEC, row(D), act, act, act, row(D)],
        out_specs=[half] * 6,
        out_shape=_half_shapes(FS) * 3,
        scratch_shapes=[pltpu.VMEM((FS, D), F32)] * 3,
        compiler_params=_cp(2))(c_arr, h2, act_a, dgate, dup, df)


def _outproj_bwd(dmix, w_out, pool_o, attn_o, c_arr, dep=None):
    s = dmix.shape[0]
    tm = min(TM, s)
    nt = s // tm

    def body(c_ref, dm_ref, w_ref, p_ref, a_ref, *rest):
        dpool_ref, dattn_ref, own_ref, oth_ref, gw_ref = rest[-5:]
        i = pl.program_id(0)

        @pl.when(i == 0)
        def _():
            gw_ref[...] = jnp.zeros_like(gw_ref)

        dm = dm_ref[...]
        dpool_ref[...] = _dot_nt(dm, w_ref[0:512, :])
        dattn_ref[...] = _dot_nt(dm, w_ref[512:1024, :]).astype(BF16)
        gw_ref[0:512, :] += _dot_tn(p_ref[...], dm)
        gw_ref[512:1024, :] += _dot_tn(a_ref[...], dm)

        @pl.when(i == nt - 1)
        def _():
            for k in range(NSH):
                _emit_halves(gw_ref, k * WOUT_S, WOUT_S, c_ref[0], own_ref.at[k], oth_ref.at[k])

    row = lambda w: pl.BlockSpec((tm, w), lambda i: (i, 0))
    full = pl.BlockSpec((D, D), lambda i: (0, 0))
    half = pl.BlockSpec((NSH, WOUT_S // 2, D), lambda i: (0, 0, 0))
    return pl.pallas_call(
        body, name="outproj_bwd", grid=(nt,),
        in_specs=[SMEM_SPEC, row(D), full, row(512), row(512)] + ([] if dep is None else [ANY]),
        out_specs=[row(512), row(512), half, half],
        out_shape=[jax.ShapeDtypeStruct((s, 512), F32), jax.ShapeDtypeStruct((s, 512), BF16)] + _half_shapes(WOUT_S),
        scratch_shapes=[pltpu.VMEM((D, D), F32)],
        compiler_params=_cp(1))(c_arr, dmix, w_out, pool_o, attn_o, *([] if dep is None else [dep]))


def _pool_bwd(pooled, dpool, w_pool, pool_scale, dep=None):
    s = pooled.shape[0]
    tm = min(TM_POOL, s)
    hb = tm // HALO
    nt = s // tm
    last_halo = s // HALO - 1

    def body(p_ref, dc_ref, dn_ref, wp_ref, sc_ref, *rest):
        du_ref, gwp_ref, gsc_ref = rest[-3:]
        i = pl.program_id(0)

        @pl.when(i == 0)
        def _():
            gwp_ref[...] = jnp.zeros_like(gwp_ref)
            gsc_ref[...] = jnp.zeros_like(gsc_ref)

        dcur = dc_ref[...]
        dnext = jnp.where(i < nt - 1, dn_ref[...], 0.0)
        dext = jnp.concatenate([dcur, dnext], axis=0)
        t_ext = i * tm + lax.broadcasted_iota(jnp.int32, (tm + HALO, GROUP), 0)
        for g, w in enumerate(WINDOWS):
            sl = slice(g * GROUP, (g + 1) * GROUP)
            pb = p_ref[:, sl]
            wp = wp_ref[g]
            mixed = _dot(pb, wp)
            gsc_ref[:, sl] += jnp.sum(dcur[:, sl] * mixed, axis=0, keepdims=True)
            dmixed = (dext[:, sl] * sc_ref[:, sl]).astype(BF16)
            gwp_ref[g] += _dot_tn(pb, dmixed[0:tm])
            dpooled = _dot_nt(dmixed, wp)
            z = dpooled / jnp.minimum(t_ext + 1, w).astype(F32)
            du_ref[:, sl] = (_window_sums(z, w, -1, tm, 2) - dpooled[0:tm]).astype(BF16)

    return pl.pallas_call(
        body, name="pool_bwd", grid=(nt,),
        in_specs=[pl.BlockSpec((tm, 512), lambda i: (i, 0)),
                  pl.BlockSpec((tm, 512), lambda i: (i, 0)),
                  pl.BlockSpec((HALO, 512), lambda i: (jnp.minimum((i + 1) * hb, last_halo), 0)),
                  pl.BlockSpec((4, GROUP, GROUP), lambda i: (0, 0, 0)),
                  pl.BlockSpec((1, 512), lambda i: (0, 0))] + ([] if dep is None else [ANY]),
        out_specs=[pl.BlockSpec((tm, 512), lambda i: (i, 0)),
                   pl.BlockSpec((4, GROUP, GROUP), lambda i: (0, 0, 0)),
                   pl.BlockSpec((1, 512), lambda i: (0, 0))],
        out_shape=[jax.ShapeDtypeStruct((s, 512), BF16), jax.ShapeDtypeStruct((4, GROUP, GROUP), F32),
                   jax.ShapeDtypeStruct((1, 512), F32)],
        compiler_params=_cp(1))(pooled, dpool, dpool, w_pool, pool_scale, *([] if dep is None else [dep]))


def _attn_bwd(q, k, v, do, probs, sink_share, bucket, dep=None):
    s = q.shape[0]
    nblk = s // BLK

    def body(q_ref, do_ref, k_ref, v_ref, prob_ref, ps_ref, bk_ref, *rest):
        dq_ref, dk_ref, dv_ref, grb_ref, gsk_ref, dss_ref = rest[-6:]
        n = pl.program_id(0)

        @pl.when(n == 0)
        def _():
            dk_ref[...] = jnp.zeros_like(dk_ref)
            dv_ref[...] = jnp.zeros_like(dv_ref)
            dss_ref[...] = jnp.zeros_like(dss_ref)
            gsk_ref[...] = jnp.zeros_like(gsk_ref)

        kt, (lo, pstart, cstart) = _kv_band(k_ref, n, True)
        vv, _ = _kv_band(v_ref, n, False)
        lo_q = lax.broadcasted_iota(jnp.int32, (BLK, 128), 1) < HD
        dkk = [jnp.zeros((2 * BLK, 128), F32), jnp.zeros((2 * BLK, 128), F32)]
        dvv = [jnp.zeros((2 * BLK, 128), F32), jnp.zeros((2 * BLK, 128), F32)]

        def by_head(t):
            return jnp.concatenate([jnp.where(lo_q, t, 0.0), jnp.where(lo_q, 0.0, t)], axis=0).astype(BF16)

        for p in range(4):
            h = p // 2
            qt = q_ref[:, p * 128:(p + 1) * 128].astype(F32) * SCALE
            dot = do_ref[:, p * 128:(p + 1) * 128]
            pb = prob_ref[pl.ds(2 * p, 2)]
            prob = pb.astype(F32)
            ps = ps_ref[pl.ds(2 * p, 2), :].reshape(2, 1, BLK)
            dp = _dot_nt(vv[h], dot).reshape(2, 2 * BLK, BLK)
            delta = jnp.sum(prob * dp, axis=1, keepdims=True)
            ds = prob * (dp - delta)
            sink_part = ps * delta
            for e in range(2):
                gs = -jnp.sum(sink_part[e]).reshape(1, 1)
                gsk_ref[pl.ds(2 * p + e, 1), :] += jnp.broadcast_to(gs, (1, 128))
            dss_ref[pl.ds(2 * p, 2)] += ds
            dsb = ds.astype(BF16)
            dq_t = _dot(kt[h], dsb.reshape(4 * BLK, BLK))
            dq_ref[:, p * 128:(p + 1) * 128] = (dq_t.T * SCALE).astype(BF16)
            dkk[h] = dkk[h] + _dot(jnp.concatenate([dsb[0], dsb[1]], axis=1), by_head(qt))
            dvv[h] = dvv[h] + _dot(jnp.concatenate([pb[0], pb[1]], axis=1), by_head(dot.astype(F32)))
        for acc, ref in ((dkk, dk_ref), (dvv, dv_ref)):
            f0 = acc[0] + pltpu.roll(acc[0], HD, axis=1)
            f1 = acc[1] + pltpu.roll(acc[1], HD, axis=1)
            band = jnp.where(lo, f0, f1)
            ref[pl.ds(pstart, BLK), :] += band[0:BLK]
            ref[pl.ds(cstart, BLK), :] += band[BLK:2 * BLK]

        @pl.when(n == nblk - 1)
        def _():
            _relbias_grad(dss_ref, bk_ref[...], grb_ref)

    blk = pl.BlockSpec((BLK, 512), lambda i: (i, 0))
    kv = pl.BlockSpec((s, 128), lambda i: (0, 0))
    row8 = pl.BlockSpec((NQ, 128), lambda i: (0, 0))
    return pl.pallas_call(
        body, name="attn_bwd", grid=(nblk,),
        in_specs=[blk, blk, kv, kv, pl.BlockSpec((None, NQ, 2 * BLK, BLK), lambda i: (i, 0, 0, 0)),
                  pl.BlockSpec((None, NQ, BLK), lambda i: (i, 0, 0)), pl.BlockSpec((2 * BLK, BLK), lambda i: (0, 0))]
        + ([] if dep is None else [ANY]),
        out_specs=[blk, kv, kv, row8, row8],
        out_shape=[jax.ShapeDtypeStruct((s, 512), BF16), jax.ShapeDtypeStruct((s, 128), F32),
                   jax.ShapeDtypeStruct((s, 128), F32), jax.ShapeDtypeStruct((NQ, 128), F32),
                   jax.ShapeDtypeStruct((NQ, 128), F32)],
        scratch_shapes=[pltpu.VMEM((NQ, 2 * BLK, BLK), F32)],
        compiler_params=_cp(1))(q, do, k, v, probs, sink_share, bucket, *([] if dep is None else [dep]))


def _relbias_grad(ds_ref, bucket_v, o_ref):
    lane = lax.broadcasted_iota(jnp.int32, (NQ, 128), 1)
    head_row = lax.broadcasted_iota(jnp.int32, (NQ, BLK), 0)
    acc = jnp.zeros((NQ, 128), F32)
    for b in range(NBUCKET):
        sel = bucket_v == b
        stack = jnp.zeros((NQ, BLK), F32)
        for h in range(NQ):
            col_sums = jnp.sum(jnp.where(sel, ds_ref[h], 0.0), axis=0, keepdims=True)
            stack = jnp.where(head_row == h, col_sums, stack)
        acc = acc + jnp.where(lane == b, jnp.sum(stack, axis=1, keepdims=True), 0.0)
    o_ref[...] = acc


def _inproj_bwd(x, g1, du, dq, dk, dv, w_in, dx1, c_arr):
    s = x.shape[0]
    tm = min(TM, s)
    nt = s // tm
    cols = ((0, 512), (512, 1024), (1024, 1152), (1152, 1280))

    def body(c_ref, x_ref, g_ref, du_ref, dq_ref, dk_ref, dv_ref, w_ref, dx1_ref,
             gx_ref, own_ref, oth_ref, gg_ref, gw_ref):
        i = pl.program_id(0)

        @pl.when(i == 0)
        def _():
            gw_ref[...] = jnp.zeros_like(gw_ref)
            gg_ref[...] = jnp.zeros_like(gg_ref)

        gv = g_ref[...]
        r1, n1 = _rms(x_ref[...])
        h = (n1 * gv).astype(BF16)
        parts = (du_ref[...], dq_ref[...], dk_ref[...].astype(BF16), dv_ref[...].astype(BF16))
        dh = None
        for (a, b), dpart in zip(cols, parts):
            t = _dot(dpart, w_ref[a:b, :])
            dh = t if dh is None else dh + t
            gw_ref[a:b, :] += _dot_tn(dpart, h)
        dx, dg = _rms_bwd(r1, n1, gv, dh)
        gg_ref[...] += dg
        gx_ref[...] = dx1_ref[...] + dx

        @pl.when(i == nt - 1)
        def _():
            for k in range(NSH):
                _emit_halves(gw_ref, k * WIN_S, WIN_S, c_ref[0], own_ref.at[k], oth_ref.at[k])

    row = lambda w: pl.BlockSpec((tm, w), lambda i: (i, 0))
    vec = pl.BlockSpec((1, D), lambda i: (0, 0))
    full = pl.BlockSpec((IN_W, D), lambda i: (0, 0))
    half = pl.BlockSpec((NSH, WIN_S // 2, D), lambda i: (0, 0, 0))
    return pl.pallas_call(
        body, name="inproj_bwd", grid=(nt,),
        in_specs=[SMEM_SPEC, row(D), vec, row(512), row(512), row(128), row(128), full, row(D)],
        out_specs=[row(D), half, half, vec],
        out_shape=[jax.ShapeDtypeStruct((s, D), F32)] + _half_shapes(WIN_S) + [jax.ShapeDtypeStruct((1, D), F32)],
        scratch_shapes=[pltpu.VMEM((IN_W, D), F32)],
        compiler_params=_cp(1))(c_arr, x, g1, du, dq, dk, dv, w_in, dx1)


def _local_step(x, target, small, w_in, w_out, ffn_weights, c_arr, on_ffn_grads=None, on_wout_grads=None):
    g1, g2, g3, g4, w_pool, pool_scale, rel_bias, sinks = small
    bucket = jnp.asarray(_bucket_table())
    wp_bf = w_pool.astype(BF16)
    bias = _bias_table(bucket, rel_bias)
    h1 = _prenorm(x, g1)
    if callable(w_in):
        w_in = w_in(bias, h1)
    u, q, k, v = _inproj_fwd(h1, w_in)
    pool_o, pooled = _pool_fwd(u, wp_bf, pool_scale)
    attn_o, probs, sink_share = _attn_fwd(q, k, v, bias, sinks)
    g2_fwd = g2
    if callable(w_out):
        w_out, after = w_out(pool_o, attn_o)
        if after is not None:
            g2_fwd = g2 + after[:1, :1]
    mix, x1, h2 = _outproj_fwd(pool_o, attn_o, w_out, x, g2_fwd, g3)
    wg, wu, wd = ffn_weights(h2) if callable(ffn_weights) else ffn_weights
    gate, up, act_a, df, dy, loss, gg4 = _ffn_fwd(h2, wg, wu, wd, x1, target, g4)
    dgate, dup, dx1, dmix, gg3, gg2 = _ffn_bwd_act(df, gate, up, wg, wu, wd, dy, x1, mix, g3, g2)
    ffn_g = _ffn_bwd_w(h2, act_a, dgate, dup, df, c_arr)
    dep = None if on_ffn_grads is None else on_ffn_grads(ffn_g)
    dpool, dattn, wout_own, wout_oth = _outproj_bwd(dmix, w_out, pool_o, attn_o, c_arr, dep)
    dep = None if on_wout_grads is None else on_wout_grads(wout_own, wout_oth, dpool)
    du, gwp, gsc = _pool_bwd(pooled, dpool, wp_bf, pool_scale, dep)
    dq, dk, dv, grb, gsk = _attn_bwd(q, k, v, dattn, probs, sink_share, bucket, dep)
    gx, win_own, win_oth, gg1 = _inproj_bwd(x, g1, du, dq, dk, dv, w_in, dx1, c_arr)
    small_grads = (gg1, gg2, gg3, gg4, gwp, gsc, grb, gsk[:, 0].reshape(1, NQ))
    return loss, gx, small_grads, ((win_own, win_oth), (wout_own, wout_oth), ffn_g)


def _place():
    x, y, c = lax.axis_index("x"), lax.axis_index("y"), lax.axis_index("c")
    chips = ((1 - x, y), (x, 1 - y), (1 - x, 1 - y))
    return x, y, c, chips


def _remote(src, dst, ssem, rsem, dev):
    return pltpu.make_async_remote_copy(src_ref=src, dst_ref=dst, send_sem=ssem, recv_sem=rsem,
                                        device_id=dev, device_id_type=MESH_T)


def _to_sibling(arrs, name, after=()):
    nt, na = len(arrs), len(after)

    def body(*refs):
        srcs, dsts = refs[:nt], refs[nt + na:2 * nt + na]
        ssem, rsem = refs[2 * nt + na:]
        x, y, c, _ = _place()
        cps = [_remote(srcs[t], dsts[t], ssem.at[t], rsem.at[t], (x, y, 1 - c)) for t in range(nt)]
        for cp in cps:
            cp.start()
        for cp in cps:
            cp.wait()

    return pl.pallas_call(
        body, name=name, in_specs=[ANY] * (nt + na), out_specs=[ANY] * nt,
        out_shape=[jax.ShapeDtypeStruct(a.shape, a.dtype) for a in arrs],
        scratch_shapes=[pltpu.SemaphoreType.DMA((nt,)), pltpu.SemaphoreType.DMA((nt,))],
        compiler_params=_cp())(*arrs, *after)


HBM_SPEC = pl.BlockSpec(memory_space=pltpu.HBM)
SEM_SPEC = pl.BlockSpec(memory_space=pltpu.SEMAPHORE)
DATAFLOW = pltpu.SideEffectType.DATAFLOW_SIDE_EFFECTING


def _gather_copies(srcs, lands, ssem, rsem):
    x, y, c, chips = _place()
    me = 2 * x + y
    out = []
    for t in range(len(srcs)):
        half = srcs[t].shape[0] // 2
        mine = pl.ds(pl.multiple_of(c * half, 16), half)
        for j, (px, py) in enumerate(chips):
            k = 3 * t + j
            send = _remote(srcs[t].at[mine], lands[t].at[me, mine], ssem.at[k], rsem.at[k], (px, py, c))
            blk = lands[t].at[2 * px + py, mine]
            out.append((send, _remote(blk, blk, ssem.at[k], rsem.at[k], (px, py, c))))
    return out


def _scatter_copies(srcs, lands, ssem, rsem):
    x, y, c, chips = _place()
    out = []
    for t in range(len(srcs)):
        for j, (px, py) in enumerate(chips):
            k = 3 * t + j
            send = _remote(srcs[t].at[2 * px + py], lands[t].at[j], ssem.at[k], rsem.at[k], (px, py, c))
            out.append((send, _remote(lands[t].at[j], lands[t].at[j], ssem.at[k], rsem.at[k], (px, py, c))))
    return out


def _sibling_copies(srcs, lands, ssem, rsem):
    x, y, c, _ = _place()
    sib = (x, y, 1 - c)
    return [(_remote(srcs[t], lands[t], ssem.at[t], rsem.at[t], sib),
             _remote(lands[t], lands[t], ssem.at[t], rsem.at[t], sib)) for t in range(len(srcs))]


def _peer_copies(srcs, lands, ssem, rsem):
    x, y, c, _ = _place()
    me = 4 * x + 2 * y + c
    out = []
    for kk in range(1, 8):
        px, py, pc = x ^ (kk >> 2), y ^ ((kk >> 1) & 1), c ^ (kk & 1)
        send = _remote(srcs[0], lands[0].at[me], ssem.at[kk - 1], rsem.at[kk - 1], (px, py, pc))
        slot = lands[0].at[4 * px + 2 * py + pc]
        out.append((send, _remote(slot, slot, ssem.at[kk - 1], rsem.at[kk - 1], (px, py, pc))))
    return out


def _split_start(plan, ncopy, srcs, lands, after, name):
    ns, nbuf = len(srcs), len(srcs) + len(lands)
    extra = [] if after is None else [after]
    n_in = nbuf + len(extra)

    def body(*refs):
        ssem, rsem, token = refs[n_in], refs[n_in + 1], refs[-1]
        for send, _ in plan(refs[:ns], refs[ns:nbuf], ssem, rsem):
            send.start()
        token[...] = jnp.zeros_like(token)

    bufs = [pltpu.with_memory_space_constraint(a, pltpu.HBM) for a in list(srcs) + list(lands)]
    outs = pl.pallas_call(
        body, name=name, in_specs=[HBM_SPEC] * nbuf + [ANY] * len(extra),
        out_specs=[SEM_SPEC, SEM_SPEC] + [HBM_SPEC] * nbuf + [pl.BlockSpec(memory_space=pltpu.VMEM)],
        out_shape=[pltpu.SemaphoreType.DMA((ncopy,)), pltpu.SemaphoreType.DMA((ncopy,))]
        + [pltpu.HBM(a.shape, a.dtype) for a in bufs] + [jax.ShapeDtypeStruct((8, 128), F32)],
        input_output_aliases={i: 2 + i for i in range(nbuf)},
        compiler_params=pltpu.CompilerParams(has_side_effects=DATAFLOW))(*bufs, *extra)
    return outs[0], outs[1], outs[2:2 + ns], outs[2 + ns:2 + nbuf], outs[-1]


def _split_wait(plan, ssem, rsem, srcs, lands, after, name, only=None):
    ns, nbuf = len(srcs), len(srcs) + len(lands)

    def body(*refs):
        s_ref, r_ref = refs[nbuf], refs[nbuf + 1]
        for k, (send, recv) in enumerate(plan(refs[:ns], refs[ns:nbuf], s_ref, r_ref)):
            if only is None or k in only:
                send.wait_send()
                recv.wait_recv()

    outs = pl.pallas_call(
        body, name=name, in_specs=[HBM_SPEC] * nbuf + [SEM_SPEC, SEM_SPEC] + [ANY] * len(after),
        out_specs=[HBM_SPEC] * nbuf,
        out_shape=[pltpu.HBM(a.shape, a.dtype) for a in list(srcs) + list(lands)],
        input_output_aliases={i: i for i in range(nbuf)},
        compiler_params=pltpu.CompilerParams(has_side_effects=DATAFLOW))(*srcs, *lands, ssem, rsem, *after)
    return outs


def _gather_finish(lands, tag):
    nt = len(lands)

    def body(*refs):
        outs = refs[nt:2 * nt]
        dsend, drecv = refs[2 * nt:]
        x, y, c, chips = _place()
        sib = (x, y, 1 - c)
        sends = []
        for t in range(nt):
            half = outs[t].shape[1] // 2
            mine = pl.ds(pl.multiple_of(c * half, 16), half)
            for j, (px, py) in enumerate(chips):
                blk = outs[t].at[2 * px + py, mine]
                cp = _remote(blk, blk, dsend.at[t, j], drecv.at[t, j], sib)
                cp.start()
                sends.append(cp)
        for t in range(nt):
            half = outs[t].shape[1] // 2
            other = pl.ds(pl.multiple_of((1 - c) * half, 16), half)
            for j, (px, py) in enumerate(chips):
                blk = outs[t].at[2 * px + py, other]
                _remote(blk, blk, dsend.at[t, j], drecv.at[t, j], sib).wait_recv()
        for cp in sends:
            cp.wait_send()

    return pl.pallas_call(
        body, name="gather_finish_" + tag, in_specs=[ANY] * nt, out_specs=[ANY] * nt,
        out_shape=[jax.ShapeDtypeStruct(a.shape, a.dtype) for a in lands],
        input_output_aliases={t: t for t in range(nt)},
        scratch_shapes=[pltpu.SemaphoreType.DMA((nt, 3)), pltpu.SemaphoreType.DMA((nt, 3))],
        compiler_params=_cp())(*lands)


def _chip_partial(owns, recv, chip_arr, tag):
    nt = len(owns)

    def body(chip_ref, *refs):
        k = pl.program_id(0)
        for t in range(nt):
            p = refs[t][...] + refs[nt + t][...].astype(F32)
            refs[2 * nt + t][...] = p.astype(BF16)

            @pl.when(k == chip_ref[0])
            def _():
                refs[3 * nt + t][...] = p

    blk_specs = [pl.BlockSpec((None,) + a.shape[1:], lambda k, chip_ref: (k, 0, 0)) for a in owns]
    own_specs = [pl.BlockSpec(a.shape[1:], lambda k, chip_ref: (0, 0)) for a in owns]
    return pl.pallas_call(
        body, name="rs_chip_partial_" + tag,
        grid_spec=pltpu.PrefetchScalarGridSpec(num_scalar_prefetch=1, grid=(NSH,), in_specs=blk_specs * 2,
                                               out_specs=blk_specs + own_specs),
        out_shape=[jax.ShapeDtypeStruct(a.shape, BF16) for a in owns]
        + [jax.ShapeDtypeStruct(a.shape[1:], F32) for a in owns],
        compiler_params=_cp(1))(chip_arr, *owns, *recv)


def _sum_chips(own, recv, tag, after=()):
    nt = len(own)

    def body(*refs):
        outs = refs[2 * nt + len(after):]
        for t in range(nt):
            r = refs[nt + t]
            outs[t][...] = ((refs[t][...] + r[0].astype(F32)) + r[1].astype(F32)) + r[2].astype(F32)

    vm = pl.BlockSpec(memory_space=pltpu.VMEM)
    return pl.pallas_call(
        body, name="rs_sum_chips_" + tag, in_specs=[vm] * (2 * nt) + [ANY] * len(after), out_specs=[vm] * nt,
        out_shape=[jax.ShapeDtypeStruct(a.shape, F32) for a in own],
        compiler_params=_cp())(*own, *recv, *after)


def _adamw_math(w, g, m, v):
    m = ADAM_B1 * m + (1.0 - ADAM_B1) * g
    v = ADAM_B2 * v + (1.0 - ADAM_B2) * (g * g)
    m_hat = m / (1.0 - ADAM_B1 ** ADAM_STEP)
    v_hat = v / (1.0 - ADAM_B2 ** ADAM_STEP)
    delta = -ADAM_LR * (m_hat / (jnp.sqrt(v_hat) + ADAM_EPS) + ADAM_WD * w)
    return delta, m, v


ADAM_SPLIT = 4


def _adamw_shards(own, sib, ws, ms, vs, c_arr, tag):
    nt = len(own)

    def body(c_ref, *refs):
        hh = pl.program_id(0)
        mine = hh == c_ref[0]
        for t in range(nt):
            g = jnp.where(mine, refs[t][...], refs[nt + t][...])
            delta, m, v = _adamw_math(refs[2 * nt + t][...], g, refs[3 * nt + t][...], refs[4 * nt + t][...])
            refs[5 * nt + t][...] = g
            refs[6 * nt + t][...] = delta
            refs[7 * nt + t][...] = m
            refs[8 * nt + t][...] = v

    def tile(a):
        return (a.shape[0] // ADAM_SPLIT, a.shape[1])

    half_specs = [pl.BlockSpec(tile(a), lambda hh, q, c_ref: (q, 0)) for a in own]
    full_specs = [pl.BlockSpec(tile(a), lambda hh, q, c_ref: (hh * ADAM_SPLIT + q, 0)) for a in own]
    return pl.pallas_call(
        body, name="adamw_shards_" + tag,
        grid_spec=pltpu.PrefetchScalarGridSpec(num_scalar_prefetch=1, grid=(2, ADAM_SPLIT),
                                               in_specs=half_specs * 2 + full_specs * 3, out_specs=full_specs * 4),
        out_shape=[jax.ShapeDtypeStruct(w.shape, F32) for w in ws] * 4,
        compiler_params=_cp(2))(c_arr, *own, *sib, *ws, *ms, *vs)


SMALL_WPOOL_ROW = 32
SMALL_SCALE_ROW = SMALL_WPOOL_ROW + 4 * GROUP
SMALL_BIAS_ROW = SMALL_SCALE_ROW + 8
SMALL_SINKS_ROW = SMALL_BIAS_ROW + NQ
SMALL_LOSS_ROW = SMALL_SINKS_ROW + 8


def _small_sum_adamw(gathered, wp, mp, vp):
    rows = wp.shape[0]

    def body(g_ref, w_ref, m_ref, v_ref, *rest):
        outs, res = rest[:-1], rest[-1]
        g = g_ref[0]
        for d in range(1, 8):
            g = g + g_ref[d]
        delta, m, v = _adamw_math(w_ref[...], g, m_ref[...], v_ref[...])
        for kind, val in enumerate((g, delta, m, v)):
            res[kind] = val
            gains = outs[8 * kind:8 * kind + 4]
            w_pool_o, scale_o, bias_o, sinks_o = outs[8 * kind + 4:8 * kind + 8]
            for t in range(4):
                for i in range(8):
                    gains[t][:, 128 * i:128 * (i + 1)] = res[kind, pl.ds(8 * t + i, 1), :]
            for grp in range(4):
                w_pool_o[grp] = res[kind, pl.ds(SMALL_WPOOL_ROW + GROUP * grp, GROUP), :]
            for i in range(4):
                scale_o[:, 128 * i:128 * (i + 1)] = res[kind, pl.ds(SMALL_SCALE_ROW + i, 1), :]
            bias_o[...] = res[kind, pl.ds(SMALL_BIAS_ROW, NQ), :][:, 0:NBUCKET]
            sinks_o[...] = res[kind, pl.ds(SMALL_SINKS_ROW, 1), :][:, 0:NQ]
        outs[-1][...] = res[0, pl.ds(SMALL_LOSS_ROW, 1), :]

    vm = pl.BlockSpec(memory_space=pltpu.VMEM)
    one_kind = [jax.ShapeDtypeStruct((1, D), F32)] * 4 + [
        jax.ShapeDtypeStruct((4, GROUP, GROUP), F32), jax.ShapeDtypeStruct((1, POOL_W), F32),
        jax.ShapeDtypeStruct((NQ, NBUCKET), F32), jax.ShapeDtypeStruct((1, NQ), F32)]
    return pl.pallas_call(
        body, name="small_sum_adamw", in_specs=[vm] * 4, out_specs=[vm] * 33,
        out_shape=one_kind * 4 + [jax.ShapeDtypeStruct((1, 128), F32)],
        scratch_shapes=[pltpu.VMEM((4, rows, 128), F32)],
        compiler_params=_cp())(gathered, wp, mp, vp)


def _pack_small(gains4, w_pool, pool_scale, rel_bias_t, sinks, loss):
    bias_rows = jnp.pad(rel_bias_t, ((0, 0), (0, 128 - rel_bias_t.shape[1])))
    parts = list(gains4) + [w_pool, pool_scale, bias_rows, sinks, loss]
    rows = []
    for a in parts:
        flat = a.reshape(-1)
        n = flat.shape[0]
        padded = -(-n // 1024) * 1024
        rows.append(jnp.pad(flat, (0, padded - n)).reshape(-1, 128))
    return jnp.concatenate(rows, axis=0)


def kernel(x, g_pre_mix, w_in, w_pool, pool_scale, rel_bias, sinks, w_out, g_post_mix, g_pre_ffn, w_gate, w_up, w_down, g_post_ffn, loss_target, m_g_pre_mix, m_w_in, m_w_pool, m_pool_scale, m_rel_bias, m_sinks, m_w_out, m_g_post_mix, m_g_pre_ffn, m_w_gate, m_w_up, m_w_down, m_g_post_ffn, v_g_pre_mix, v_w_in, v_w_pool, v_pool_scale, v_rel_bias, v_sinks, v_w_out, v_g_post_mix, v_g_pre_ffn, v_w_gate, v_w_up, v_w_down, v_g_post_ffn):
    c = lax.axis_index("c")
    chip = 2 * lax.axis_index("x") + lax.axis_index("y")

    def shards(a_in, a_out, a_gate, a_up, a_down):
        return (a_in[0].T, a_out[0], a_gate[0].T, a_up[0].T, a_down[0])

    c_arr = c.reshape(1).astype(jnp.int32)
    chip_arr = chip.reshape(1).astype(jnp.int32)

    big_w = shards(w_in, w_out, w_gate, w_up, w_down)
    big_bf = [w.astype(BF16) for w in big_w]
    g_ssem, g_rsem, g_srcs, g_lands, _ = _split_start(
        _gather_copies, 15, big_bf, [lax.empty((NSH,) + a.shape, BF16) for a in big_bf], None, "gather_start")
    fw = {}

    def with_own(land, shard):
        return lax.dynamic_update_slice(land, shard[None], (chip, 0, 0))

    def w_in_ready(*after):
        fw["first"] = _split_wait(_gather_copies, g_ssem, g_rsem, g_srcs, g_lands, after, "gather_win_wait",
                                  only=(0, 1, 2))
        (fw["win"],) = _gather_finish([with_own(fw["first"][5], fw["first"][0])], "win")
        return fw["win"].reshape(IN_W, D)

    def w_out_ready(*after):
        first = fw["first"]
        fw["second"] = _split_wait(_gather_copies, g_ssem, g_rsem, first[:5], [fw["win"]] + list(first[6:]), after,
                                   "gather_wout_wait", only=(3, 4, 5))
        (fw["wout"],) = _gather_finish([with_own(fw["second"][6], fw["second"][1])], "wout")
        return fw["wout"].reshape(D, D), None

    def ffn_weights(after):
        second = fw["second"]
        outs = _split_wait(_gather_copies, g_ssem, g_rsem, second[:5], [second[5], fw["wout"]] + list(second[7:]),
                           [after], "gather_ffn_wait", only=tuple(range(6, 15)))
        return _gather_finish([with_own(land, src) for src, land in zip(outs[2:5], outs[7:])], "ffn")

    rs = {}

    def on_ffn_grads(ffn_g):
        oths = ffn_g[1::2]
        rs["ffn_own"] = ffn_g[0::2]
        rs["x"] = _split_start(_sibling_copies, 3, oths, [lax.empty(a.shape, BF16) for a in oths], ffn_g[0],
                               "rs_exchange_ffn_start")
        return rs["x"][4]

    def on_wout_grads(own, oth, after):
        ssem, rsem, srcs, lands, _ = rs["x"]
        recv_ffn = _split_wait(_sibling_copies, ssem, rsem, srcs, lands, [after], "rs_exchange_ffn_wait")[3:]
        recv_wout = _to_sibling([oth], "rs_exchange_wout")
        outs = _chip_partial([own] + list(rs["ffn_own"]), list(recv_wout) + list(recv_ffn), chip_arr, "early")
        rs["early_own"] = outs[4:]
        rs["s"] = _split_start(_scatter_copies, 12, outs[:4],
                               [lax.empty((3,) + a.shape[1:], BF16) for a in outs[:4]], outs[4], "scatter_early_start")
        return rs["s"][4]

    small = (g_pre_mix, g_post_mix, g_pre_ffn, g_post_ffn, w_pool[0], pool_scale, rel_bias, sinks)
    loss, gx, small_grads, ((win_own, win_oth), _, _) = _local_step(
        x[0], loss_target[0], small, w_in_ready, w_out_ready, ffn_weights, c_arr, on_ffn_grads, on_wout_grads)

    ssem, rsem, srcs, lands, _ = rs["s"]
    recv_early = _split_wait(_scatter_copies, ssem, rsem, srcs, lands, [gx], "scatter_early_wait")[4:]
    finals = _sum_chips(list(rs["early_own"]), list(recv_early), "early")
    sh_ssem, sh_rsem, sh_srcs, sh_lands, sh_token = _split_start(
        _sibling_copies, 4, finals, [lax.empty(a.shape, F32) for a in finals], finals[0], "rs_share_early_start")

    unused = jnp.zeros((1, 1), F32)
    gp = _pack_small(small_grads[:4], *small_grads[4:], loss)
    wp = _pack_small((g_pre_mix, g_post_mix, g_pre_ffn, g_post_ffn), w_pool, pool_scale, rel_bias.T, sinks, unused)
    mp = _pack_small((m_g_pre_mix, m_g_post_mix, m_g_pre_ffn, m_g_post_ffn), m_w_pool, m_pool_scale, m_rel_bias.T,
                     m_sinks, unused)
    vp = _pack_small((v_g_pre_mix, v_g_post_mix, v_g_pre_ffn, v_g_post_ffn), v_w_pool, v_pool_scale, v_rel_bias.T,
                     v_sinks, unused)

    moments = (shards(m_w_in, m_w_out, m_w_gate, m_w_up, m_w_down),
               shards(v_w_in, v_w_out, v_w_gate, v_w_up, v_w_down))
    recv_a = _to_sibling([win_oth], "rs_exchange_win", [sh_token])
    outs = _chip_partial([win_own], recv_a, chip_arr, "win")
    w_ssem, w_rsem, w_srcs, w_lands, w_token = _split_start(
        _scatter_copies, 3, outs[:1], [lax.empty((3,) + outs[0].shape[1:], BF16)], gx, "scatter_win_start")
    slots = lax.dynamic_update_slice(lax.empty((8,) + gp.shape, F32), gp[None], (2 * chip + c, 0, 0))
    p_ssem, p_rsem, p_srcs, p_lands, p_token = _split_start(_peer_copies, 7, [gp], [slots], w_token,
                                                            "small_allgather_start")
    shared = _split_wait(_sibling_copies, sh_ssem, sh_rsem, sh_srcs, sh_lands, [p_token], "rs_share_early_wait")
    adam_e = _adamw_shards(shared[:4], shared[4:], big_w[1:], moments[0][1:], moments[1][1:], c_arr, "early")
    recv_win = _split_wait(_scatter_copies, w_ssem, w_rsem, w_srcs, w_lands, [adam_e[0]], "scatter_win_wait")[1:]
    win_final = _sum_chips([outs[1]], recv_win, "win")
    win_sib = _to_sibling(win_final, "rs_share_win")
    adam_w = _adamw_shards(win_final, win_sib, big_w[:1], moments[0][:1], moments[1][:1], c_arr, "win")
    big_g, big_d, big_m, big_v = [[adam_w[i]] + list(adam_e[4 * i:4 * i + 4]) for i in range(4)]

    gathered = _split_wait(_peer_copies, p_ssem, p_rsem, p_srcs, p_lands, [adam_w[0]], "small_allgather_wait")[1]
    flat = _small_sum_adamw(gathered, wp, mp, vp)
    small_out = [flat[8 * kind:8 * kind + 8] for kind in range(4)]
    total_loss = flat[-1][0, 0]

    def ordered(small8, big4):
        sg1, sg2, sg3, sg4, swp, ssc, srb, ssk = small8
        swp, srb = swp[None], srb.T
        bwin, bwout, bwg, bwu, bwd = big4[0].T[None], big4[1][None], big4[2].T[None], big4[3].T[None], big4[4][None]
        return [sg1, bwin, swp, ssc, srb, ssk, bwout, sg2, sg3, bwg, bwu, bwd, sg4]

    res = [total_loss, gx[None]]
    for sm, bg in zip(small_out, (big_g, big_d, big_m, big_v)):
        res += ordered(sm, bg)
    return tuple(res)
```

```python
import numpy as np
import jax
import jax.numpy as jnp
from jax import lax
from jax.experimental import pallas as pl
from jax.experimental.pallas import tpu as pltpu

F32 = jnp.float32
BF16 = jnp.bfloat16

D = 1024
POOL_W = 512
GROUP = 128
WINDOWS = (2, 4, 8, 16)
HALO = 128
NQ = 8
BLK = 128
NBUCKET = 32
IN_W = 1280
DFF = 2816
NSH = 4
FS = DFF // NSH
WIN_S = IN_W // NSH
WOUT_S = D // NSH
EPS = 1e-6
NEG = -1e30
SCALE = 0.125

ADAM_LR = 0.001
ADAM_B1 = 0.9
ADAM_B2 = 0.999
ADAM_EPS = 1e-08
ADAM_WD = 0.01
ADAM_STEP = 10

TM = 512
TM_POOL = 512
TM_FFN = 512
TM_FFN_FWD = 1024
FFN_ROWS = 256
VMEM_LIMIT = 60 * 1024 * 1024

MESH_T = pl.DeviceIdType.MESH
ANY = pl.BlockSpec(memory_space=pl.ANY)


def _cp(n_grid=0, **kw):
    sem = ("arbitrary",) * n_grid if n_grid else None
    return pltpu.CompilerParams(dimension_semantics=sem, vmem_limit_bytes=VMEM_LIMIT, **kw)


def _dot(a, b):
    return jnp.dot(a, b, preferred_element_type=F32)


def _dot_nt(a, b):
    return lax.dot_general(a, b, (((1,), (1,)), ((), ())), preferred_element_type=F32)


def _dot_tn(a, b):
    return lax.dot_general(a, b, (((0,), (0,)), ((), ())), preferred_element_type=F32)


def _rms(x):
    r = lax.rsqrt(jnp.mean(x * x, axis=-1, keepdims=True) + EPS)
    return r, x * r


def _rms_bwd(r, n, g, dout):
    dn = dout * g
    dx = r * (dn - n * jnp.mean(dn * n, axis=-1, keepdims=True))
    dg = jnp.sum(dout * n, axis=0, keepdims=True)
    return dx, dg


def _split(x, n):
    parts = []
    r = x
    for _ in range(n):
        p = r.astype(BF16)
        parts.append(p)
        r = r - p.astype(F32)
    return parts


def _band_dot(a, x, n):
    acc = None
    for p in _split(x, n):
        t = _dot(a, p)
        acc = t if acc is None else acc + t
    return acc


def _bucket_table():
    qi = np.arange(BLK)[None, :]
    kj = np.arange(2 * BLK)[:, None]
    dist = qi + BLK - kj
    n = np.maximum(dist, 0)
    nf = np.maximum(n, 1).astype(np.float32)
    large = 16 + (np.log(nf / np.float32(16)) / np.float32(np.log(128 / 16)) * np.float32(16)).astype(np.int32)
    large = np.minimum(large, NBUCKET - 1)
    return np.where(n < 16, n, large).astype(np.int32)


def _prenorm(x, g1):
    s = x.shape[0]
    tm = min(TM, s)

    def body(x_ref, g_ref, h_ref):
        _, n = _rms(x_ref[...])
        h_ref[...] = (n * g_ref[...]).astype(BF16)

    row = pl.BlockSpec((tm, D), lambda i: (i, 0))
    return pl.pallas_call(
        body, name="prenorm", grid=(s // tm,),
        in_specs=[row, pl.BlockSpec((1, D), lambda i: (0, 0))], out_specs=row,
        out_shape=jax.ShapeDtypeStruct((s, D), BF16),
        compiler_params=_cp(1))(x, g1)


def _inproj_fwd(h1, w_in):
    s = h1.shape[0]
    tm = min(TM, s)

    def body(h_ref, w_ref, u_ref, q_ref, k_ref, v_ref):
        proj = _dot_nt(h_ref[...], w_ref[...])
        u_ref[...] = proj[:, :512]
        q_ref[...] = proj[:, 512:1024].astype(BF16)
        k_ref[...] = proj[:, 1024:1152].astype(BF16)
        v_ref[...] = proj[:, 1152:1280].astype(BF16)

    row = lambda w: pl.BlockSpec((tm, w), lambda i: (i, 0))
    return pl.pallas_call(
        body, name="inproj_fwd", grid=(s // tm,),
        in_specs=[row(D), pl.BlockSpec((IN_W, D), lambda i: (0, 0))],
        out_specs=[row(512), row(512), row(128), row(128)],
        out_shape=[jax.ShapeDtypeStruct((s, 512), F32), jax.ShapeDtypeStruct((s, 512), BF16),
                   jax.ShapeDtypeStruct((s, 128), BF16), jax.ShapeDtypeStruct((s, 128), BF16)],
        compiler_params=_cp(1))(h1, w_in)


def _window_sums(ext, w, lag_sign, tm, terms):
    rows = lax.broadcasted_iota(jnp.int32, (HALO, 2 * HALO), 0)
    cols = lax.broadcasted_iota(jnp.int32, (HALO, 2 * HALO), 1)
    d = rows + HALO - cols if lag_sign > 0 else cols - rows
    band = jnp.where((d >= 0) & (d < w), 1.0, 0.0).astype(BF16)
    return jnp.concatenate([_band_dot(band, ext[r:r + 2 * HALO], terms) for r in range(0, tm, HALO)], axis=0)


def _pooled(ext, cur, t0, tm):
    t = t0 + lax.broadcasted_iota(jnp.int32, (tm, GROUP), 0)
    out = []
    for g, w in enumerate(WINDOWS):
        sl = slice(g * GROUP, (g + 1) * GROUP)
        cnt = jnp.minimum(t + 1, w).astype(F32)
        out.append(_window_sums(ext[:, sl], w, 1, tm, 2) / cnt - cur[:, sl])
    return out


def _pool_fwd(u, w_pool, pool_scale):
    s = u.shape[0]
    tm = min(TM_POOL, s)
    hb = tm // HALO

    def body(uc_ref, uh_ref, wp_ref, sc_ref, o_ref, pooled_ref):
        i = pl.program_id(0)
        cur = uc_ref[...]
        halo = jnp.where(i > 0, uh_ref[...], 0.0)
        ext = jnp.concatenate([halo, cur], axis=0)
        pooled = _pooled(ext, cur, i * tm, tm)
        for g in range(4):
            sl = slice(g * GROUP, (g + 1) * GROUP)
            pb = pooled[g].astype(BF16)
            pooled_ref[:, sl] = pb
            o_ref[:, sl] = (_dot(pb, wp_ref[g]) * sc_ref[:, sl]).astype(BF16)

    row = pl.BlockSpec((tm, 512), lambda i: (i, 0))
    return pl.pallas_call(
        body, name="pool_fwd", grid=(s // tm,),
        in_specs=[row, pl.BlockSpec((HALO, 512), lambda i: (jnp.maximum(i * hb - 1, 0), 0)),
                  pl.BlockSpec((4, GROUP, GROUP), lambda i: (0, 0, 0)),
                  pl.BlockSpec((1, 512), lambda i: (0, 0))],
        out_specs=[row, row],
        out_shape=[jax.ShapeDtypeStruct((s, 512), BF16)] * 2,
        compiler_params=_cp(1))(u, u, w_pool, pool_scale)


def _bias_table(bucket, rel_bias):
    def body(b_ref, rb_ref, o_ref):
        bucket_v = b_ref[...]
        key = lax.broadcasted_iota(jnp.int32, (2 * BLK, BLK), 0)
        dist = lax.broadcasted_iota(jnp.int32, (2 * BLK, BLK), 1) + BLK - key
        inwin = (dist >= 0) & (dist < BLK)
        for h in range(NQ):
            acc = jnp.zeros((2 * BLK, BLK), F32)
            for b in range(NBUCKET):
                acc = jnp.where(bucket_v == b, rb_ref[b, h], acc)
            rest = jnp.where(inwin, acc, NEG)
            o_ref[1, h] = rest
            o_ref[0, h] = jnp.where(key >= BLK, rest, NEG)

    return pl.pallas_call(
        body, name="bias_table",
        in_specs=[pl.BlockSpec(memory_space=pltpu.VMEM), pl.BlockSpec(memory_space=pltpu.SMEM)],
        out_specs=pl.BlockSpec(memory_space=pltpu.VMEM),
        out_shape=jax.ShapeDtypeStruct((2, NQ, 2 * BLK, BLK), F32),
        compiler_params=_cp())(bucket, rel_bias)


HD = 64


def _kv_band(ref, n, transposed):
    pstart = pl.multiple_of(jnp.maximum(n - 1, 0) * BLK, BLK)
    cstart = pl.multiple_of(n * BLK, BLK)
    lo = lax.broadcasted_iota(jnp.int32, (2 * BLK, 128), 1) < HD
    band = jnp.concatenate([ref[pl.ds(pstart, BLK), :], ref[pl.ds(cstart, BLK), :]], axis=0).astype(F32)
    rot = pltpu.roll(band, HD, axis=1)
    halves = ((jnp.where(lo, band, 0.0), jnp.where(lo, 0.0, rot)), (jnp.where(lo, rot, 0.0), jnp.where(lo, 0.0, band)))
    if transposed:
        out = [jnp.concatenate([a.T, b.T], axis=1).astype(BF16) for a, b in halves]
    else:
        out = [jnp.concatenate([a, b], axis=0).astype(BF16) for a, b in halves]
    return out, (lo, pstart, cstart)


def _pair_probs(qt, kk, bias, sk_ref, p):
    sink = jnp.concatenate([jnp.full((1, 1, BLK), sk_ref[0, 2 * p + e], F32) for e in range(2)], axis=0)
    s = _dot_nt(kk, qt).reshape(2, 2 * BLK, BLK) + bias
    m = jnp.maximum(jnp.max(s, axis=1, keepdims=True), sink)
    pr = jnp.exp(s - m)
    es = jnp.exp(sink - m)
    inv = 1.0 / (jnp.sum(pr, axis=1, keepdims=True) + es)
    return pr * inv, es * inv


def _attn_fwd(q, k, v, bias, sinks):
    s = q.shape[0]
    nblk = s // BLK

    def body(q_ref, k_ref, v_ref, b_ref, sk_ref, o_ref, prob_ref, ps_ref):
        n = pl.program_id(0)
        kk, _ = _kv_band(k_ref, n, False)
        vt, _ = _kv_band(v_ref, n, True)
        bidx = jnp.minimum(n, 1)
        for p in range(4):
            h = p // 2
            qt = (q_ref[:, p * 128:(p + 1) * 128].astype(F32) * SCALE).astype(BF16)
            prob, ps = _pair_probs(qt, kk[h], b_ref[bidx, pl.ds(2 * p, 2)], sk_ref, p)
            pb = prob.astype(BF16)
            prob_ref[pl.ds(2 * p, 2)] = pb
            ps_ref[pl.ds(2 * p, 2), :] = ps.reshape(2, BLK)
            acc_t = _dot(vt[h], pb.reshape(4 * BLK, BLK))
            o_ref[:, p * 128:(p + 1) * 128] = acc_t.T.astype(BF16)

    return pl.pallas_call(
        body, name="attn_fwd", grid=(nblk,),
        in_specs=[pl.BlockSpec((BLK, 512), lambda i: (i, 0)),
                  pl.BlockSpec((s, 128), lambda i: (0, 0)),
                  pl.BlockSpec((s, 128), lambda i: (0, 0)),
                  pl.BlockSpec((2, NQ, 2 * BLK, BLK), lambda i: (0, 0, 0, 0)),
                  pl.BlockSpec(memory_space=pltpu.SMEM)],
        out_specs=[pl.BlockSpec((BLK, 512), lambda i: (i, 0)),
                   pl.BlockSpec((None, NQ, 2 * BLK, BLK), lambda i: (i, 0, 0, 0)),
                   pl.BlockSpec((None, NQ, BLK), lambda i: (i, 0, 0))],
        out_shape=[jax.ShapeDtypeStruct((s, 512), BF16), jax.ShapeDtypeStruct((nblk, NQ, 2 * BLK, BLK), BF16),
                   jax.ShapeDtypeStruct((nblk, NQ, BLK), F32)],
        compiler_params=_cp(1))(q, k, v, bias, sinks)


def _outproj_fwd(pool_o, attn_o, w_out, x, g2, g3):
    s = x.shape[0]
    tm = min(TM, s)

    def body(p_ref, a_ref, w_ref, x_ref, g2_ref, g3_ref, mix_ref, x1_ref, h2_ref):
        mix = _dot(p_ref[...], w_ref[0:512, :]) + _dot(a_ref[...], w_ref[512:1024, :])
        mix_ref[...] = mix
        _, n2 = _rms(mix)
        x1 = x_ref[...] + n2 * g2_ref[...]
        x1_ref[...] = x1
        _, n3 = _rms(x1)
        h2_ref[...] = (n3 * g3_ref[...]).astype(BF16)

    row = lambda w: pl.BlockSpec((tm, w), lambda i: (i, 0))
    vec = pl.BlockSpec((1, D), lambda i: (0, 0))
    return pl.pallas_call(
        body, name="outproj_fwd", grid=(s // tm,),
        in_specs=[row(512), row(512), pl.BlockSpec((D, D), lambda i: (0, 0)), row(D), vec, vec],
        out_specs=[row(D), row(D), row(D)],
        out_shape=[jax.ShapeDtypeStruct((s, D), F32), jax.ShapeDtypeStruct((s, D), F32),
                   jax.ShapeDtypeStruct((s, D), BF16)],
        compiler_params=_cp(1))(pool_o, attn_o, w_out, x, g2, g3)


def _silu_parts(g):
    sg = jax.nn.sigmoid(g)
    return sg, g * sg


def _ffn_fwd(h2, wg, wu, wd, x1, target, g4):
    s = h2.shape[0]
    tm = min(TM_FFN_FWD, s)

    def body(h_ref, wg_ref, wu_ref, wd_ref, x1_ref, t_ref, g4_ref,
             gate_ref, up_ref, a_ref, df_ref, dy_ref, loss_ref, gg4_ref, f_acc):
        i = pl.program_id(0)
        j = pl.program_id(1)
        first = j == 0
        chunks = [pl.ds(r, min(FFN_ROWS, tm)) for r in range(0, tm, FFN_ROWS)]
        project = lambda rows: (_dot_nt(h_ref[rows, :], wg_ref[...]), _dot_nt(h_ref[rows, :], wu_ref[...]))
        ahead = [project(chunks[0])]
        for k, rows in enumerate(chunks):
            if k + 1 < len(chunks):
                ahead.append(project(chunks[k + 1]))
            gate, up = ahead[k]
            gate_ref[rows, :] = gate.astype(BF16)
            up_ref[rows, :] = up.astype(BF16)
            _, sl = _silu_parts(gate)
            a = (sl * up).astype(BF16)
            a_ref[rows, :] = a
            contrib = _dot(a, wd_ref[...])
            f_acc[rows, :] = jnp.where(first, contrib, f_acc[rows, :] + contrib)

        @pl.when((i == 0) & (j == 0))
        def _():
            loss_ref[...] = jnp.zeros_like(loss_ref)
            gg4_ref[...] = jnp.zeros_like(gg4_ref)

        @pl.when(j == NSH - 1)
        def _():
            r4, n4 = _rms(f_acc[...])
            g4v = g4_ref[...]
            err = x1_ref[...] + n4 * g4v - t_ref[...]
            loss_ref[...] += (0.5 / D) * jnp.sum(err * err).reshape(1, 1)
            dy = err * (1.0 / D)
            dy_ref[...] = dy
            df, dg = _rms_bwd(r4, n4, g4v, dy)
            gg4_ref[...] += dg
            df_ref[...] = df.astype(BF16)

    row = lambda w: pl.BlockSpec((tm, w), lambda i, j: (i, 0))
    sh = lambda r, c: pl.BlockSpec((None, r, c), lambda i, j: (j, 0, 0))
    act = pl.BlockSpec((None, tm, FS), lambda i, j: (j, i, 0))
    vec = pl.BlockSpec((1, D), lambda i, j: (0, 0))
    return pl.pallas_call(
        body, name="ffn_fwd", grid=(s // tm, NSH),
        in_specs=[row(D), sh(FS, D), sh(FS, D), sh(FS, D), row(D), row(D), vec],
        out_specs=[act, act, act, row(D), row(D), pl.BlockSpec((1, 1), lambda i, j: (0, 0)), vec],
        out_shape=[jax.ShapeDtypeStruct((NSH, s, FS), BF16)] * 3
        + [jax.ShapeDtypeStruct((s, D), BF16), jax.ShapeDtypeStruct((s, D), F32),
           jax.ShapeDtypeStruct((1, 1), F32), jax.ShapeDtypeStruct((1, D), F32)],
        scratch_shapes=[pltpu.VMEM((tm, D), F32)],
        compiler_params=_cp(2))(h2, wg, wu, wd, x1, target, g4)


def _ffn_bwd_act(df, gate, up, wg, wu, wd, dy, x1, mix, g3, g2):
    s = df.shape[0]
    tm = min(TM_FFN, s)

    def body(df_ref, gate_ref, up_ref, wg_ref, wu_ref, wd_ref, dy_ref, x1_ref, mix_ref, g3_ref, g2_ref,
             dgate_ref, dup_ref, dx1_ref, dmix_ref, gg3_ref, gg2_ref, acc):
        j = pl.program_id(0)
        i = pl.program_id(1)
        first = j == 0
        tile = pl.multiple_of(i * tm, tm)
        chunks = [pl.ds(r, min(FFN_ROWS, tm)) for r in range(0, tm, FFN_ROWS)]
        das = [_dot_nt(df_ref[chunks[0], :], wd_ref[...])]
        for k, rows in enumerate(chunks):
            if k + 1 < len(chunks):
                das.append(_dot_nt(df_ref[chunks[k + 1], :], wd_ref[...]))
            da = das[k]
            g = gate_ref[rows, :].astype(F32)
            u = up_ref[rows, :].astype(F32)
            sg, sl = _silu_parts(g)
            dgate = (da * u * (sg * (1.0 + g * (1.0 - sg)))).astype(BF16)
            dup = (da * sl).astype(BF16)
            dgate_ref[rows, :] = dgate
            dup_ref[rows, :] = dup
            contrib = _dot(dgate, wg_ref[...]) + _dot(dup, wu_ref[...])
            acc_rows = pl.ds(tile + k * FFN_ROWS, rows.size)
            acc[acc_rows, :] = jnp.where(first, contrib, acc[acc_rows, :] + contrib)

        @pl.when((i == 0) & (j == 0))
        def _():
            gg3_ref[...] = jnp.zeros_like(gg3_ref)
            gg2_ref[...] = jnp.zeros_like(gg2_ref)

        @pl.when(j == NSH - 1)
        def _():
            r3, n3 = _rms(x1_ref[...])
            dx1n, dg3 = _rms_bwd(r3, n3, g3_ref[...], acc[pl.ds(tile, tm), :])
            dx1 = dy_ref[...] + dx1n
            dx1_ref[...] = dx1
            gg3_ref[...] += dg3
            r2, n2 = _rms(mix_ref[...])
            dmix, dg2 = _rms_bwd(r2, n2, g2_ref[...], dx1)
            gg2_ref[...] += dg2
            dmix_ref[...] = dmix.astype(BF16)

    row = pl.BlockSpec((tm, D), lambda j, i: (i, 0))
    last = pl.BlockSpec((tm, D), lambda j, i: (i * (j // (NSH - 1)), 0))
    sh = pl.BlockSpec((None, FS, D), lambda j, i: (j, 0, 0))
    act = pl.BlockSpec((None, tm, FS), lambda j, i: (j, i, 0))
    vec = pl.BlockSpec((1, D), lambda j, i: (0, 0))
    return pl.pallas_call(
        body, name="ffn_bwd_act", grid=(NSH, s // tm),
        in_specs=[row, act, act, sh, sh, sh, last, last, last, vec, vec],
        out_specs=[act, act, last, last, vec, vec],
        out_shape=[jax.ShapeDtypeStruct((NSH, s, FS), BF16), jax.ShapeDtypeStruct((NSH, s, FS), BF16),
                   jax.ShapeDtypeStruct((s, D), F32), jax.ShapeDtypeStruct((s, D), BF16),
                   jax.ShapeDtypeStruct((1, D), F32), jax.ShapeDtypeStruct((1, D), F32)],
        scratch_shapes=[pltpu.VMEM((s, D), F32)],
        compiler_params=_cp(2))(df, gate, up, wg, wu, wd, dy, x1, mix, g3, g2)


SMEM_SPEC = pl.BlockSpec(memory_space=pltpu.SMEM)


def _emit_halves(acc_ref, row0, rows, c, own_ref, oth_ref):
    half = rows // 2
    own_ref[...] = acc_ref[pl.ds(row0 + pl.multiple_of(c * half, 8), half), :]
    oth_ref[...] = acc_ref[pl.ds(row0 + pl.multiple_of((1 - c) * half, 8), half), :].astype(BF16)


def _half_shapes(rows):
    return [jax.ShapeDtypeStruct((NSH, rows // 2, D), F32), jax.ShapeDtypeStruct((NSH, rows // 2, D), BF16)]


def _ffn_bwd_w(h2, act_a, dgate, dup, df, c_arr):
    s = h2.shape[0]
    tm = min(TM_FFN_FWD, s)
    nt = s // tm

    def body(c_ref, h_ref, a_ref, dgate_ref, dup_ref, df_ref, *rest):
        outs, accs = rest[:6], rest[6:]
        i = pl.program_id(1)

        @pl.when(i == 0)
        def _():
            for acc in accs:
                acc[...] = jnp.zeros_like(acc)

        h = h_ref[...]
        accs[0][...] += _dot_tn(dgate_ref[...], h)
        accs[1][...] += _dot_tn(dup_ref[...], h)
        accs[2][...] += _dot_tn(a_ref[...], df_ref[...])

        @pl.when(i == nt - 1)
        def _():
            for t, acc in enumerate(accs):
                _emit_halves(acc, 0, FS, c_ref[0], outs[2 * t], outs[2 * t + 1])

    row = lambda w: pl.BlockSpec((tm, w), lambda j, i: (i, 0))
    act = pl.BlockSpec((None, tm, FS), lambda j, i: (j, i, 0))
    half = pl.BlockSpec((None, FS // 2, D), lambda j, i: (j, 0, 0))
    return pl.pallas_call(
        body, name="ffn_bwd_w", grid=(NSH, nt),
        in_specs=[SMEM_SPEC, row(D), act, act, act, row(D)],
        out_specs=[half] * 6,
        out_shape=_half_shapes(FS) * 3,
        scratch_shapes=[pltpu.VMEM((FS, D), F32)] * 3,
        compiler_params=_cp(2))(c_arr, h2, act_a, dgate, dup, df)


def _outproj_bwd(dmix, w_out, pool_o, attn_o, c_arr, dep=None):
    s = dmix.shape[0]
    tm = min(TM, s)
    nt = s // tm

    def body(c_ref, dm_ref, w_ref, p_ref, a_ref, *rest):
        dpool_ref, dattn_ref, own_ref, oth_ref, gw_ref = rest[-5:]
        i = pl.program_id(0)

        @pl.when(i == 0)
        def _():
            gw_ref[...] = jnp.zeros_like(gw_ref)

        dm = dm_ref[...]
        dpool_ref[...] = _dot_nt(dm, w_ref[0:512, :])
        dattn_ref[...] = _dot_nt(dm, w_ref[512:1024, :]).astype(BF16)
        gw_ref[0:512, :] += _dot_tn(p_ref[...], dm)
        gw_ref[512:1024, :] += _dot_tn(a_ref[...], dm)

        @pl.when(i == nt - 1)
        def _():
            for k in range(NSH):
                _emit_halves(gw_ref, k * WOUT_S, WOUT_S, c_ref[0], own_ref.at[k], oth_ref.at[k])

    row = lambda w: pl.BlockSpec((tm, w), lambda i: (i, 0))
    full = pl.BlockSpec((D, D), lambda i: (0, 0))
    half = pl.BlockSpec((NSH, WOUT_S // 2, D), lambda i: (0, 0, 0))
    return pl.pallas_call(
        body, name="outproj_bwd", grid=(nt,),
        in_specs=[SMEM_SPEC, row(D), full, row(512), row(512)] + ([] if dep is None else [ANY]),
        out_specs=[row(512), row(512), half, half],
        out_shape=[jax.ShapeDtypeStruct((s, 512), F32), jax.ShapeDtypeStruct((s, 512), BF16)] + _half_shapes(WOUT_S),
        scratch_shapes=[pltpu.VMEM((D, D), F32)],
        compiler_params=_cp(1))(c_arr, dmix, w_out, pool_o, attn_o, *([] if dep is None else [dep]))


def _pool_bwd(pooled, dpool, w_pool, pool_scale, dep=None):
    s = pooled.shape[0]
    tm = min(TM_POOL, s)
    hb = tm // HALO
    nt = s // tm
    last_halo = s // HALO - 1

    def body(p_ref, dc_ref, dn_ref, wp_ref, sc_ref, *rest):
        du_ref, gwp_ref, gsc_ref = rest[-3:]
        i = pl.program_id(0)

        @pl.when(i == 0)
        def _():
            gwp_ref[...] = jnp.zeros_like(gwp_ref)
            gsc_ref[...] = jnp.zeros_like(gsc_ref)

        dcur = dc_ref[...]
        dnext = jnp.where(i < nt - 1, dn_ref[...], 0.0)
        dext = jnp.concatenate([dcur, dnext], axis=0)
        t_ext = i * tm + lax.broadcasted_iota(jnp.int32, (tm + HALO, GROUP), 0)
        for g, w in enumerate(WINDOWS):
            sl = slice(g * GROUP, (g + 1) * GROUP)
            pb = p_ref[:, sl]
            wp = wp_ref[g]
            mixed = _dot(pb, wp)
            gsc_ref[:, sl] += jnp.sum(dcur[:, sl] * mixed, axis=0, keepdims=True)
            dmixed = (dext[:, sl] * sc_ref[:, sl]).astype(BF16)
            gwp_ref[g] += _dot_tn(pb, dmixed[0:tm])
            dpooled = _dot_nt(dmixed, wp)
            z = dpooled / jnp.minimum(t_ext + 1, w).astype(F32)
            du_ref[:, sl] = (_window_sums(z, w, -1, tm, 2) - dpooled[0:tm]).astype(BF16)

    return pl.pallas_call(
        body, name="pool_bwd", grid=(nt,),
        in_specs=[pl.BlockSpec((tm, 512), lambda i: (i, 0)),
                  pl.BlockSpec((tm, 512), lambda i: (i, 0)),
                  pl.BlockSpec((HALO, 512), lambda i: (jnp.minimum((i + 1) * hb, last_halo), 0)),
                  pl.BlockSpec((4, GROUP, GROUP), lambda i: (0, 0, 0)),
                  pl.BlockSpec((1, 512), lambda i: (0, 0))] + ([] if dep is None else [ANY]),
        out_specs=[pl.BlockSpec((tm, 512), lambda i: (i, 0)),
                   pl.BlockSpec((4, GROUP, GROUP), lambda i: (0, 0, 0)),
                   pl.BlockSpec((1, 512), lambda i: (0, 0))],
        out_shape=[jax.ShapeDtypeStruct((s, 512), BF16), jax.ShapeDtypeStruct((4, GROUP, GROUP), F32),
                   jax.ShapeDtypeStruct((1, 512), F32)],
        compiler_params=_cp(1))(pooled, dpool, dpool, w_pool, pool_scale, *([] if dep is None else [dep]))


def _attn_bwd(q, k, v, do, probs, sink_share, bucket, dep=None):
    s = q.shape[0]
    nblk = s // BLK

    def body(q_ref, do_ref, k_ref, v_ref, prob_ref, ps_ref, bk_ref, *rest):
        dq_ref, dk_ref, dv_ref, grb_ref, gsk_ref, dss_ref = rest[-6:]
        n = pl.program_id(0)

        @pl.when(n == 0)
        def _():
            dk_ref[...] = jnp.zeros_like(dk_ref)
            dv_ref[...] = jnp.zeros_like(dv_ref)
            dss_ref[...] = jnp.zeros_like(dss_ref)
            gsk_ref[...] = jnp.zeros_like(gsk_ref)

        kt, (lo, pstart, cstart) = _kv_band(k_ref, n, True)
        vv, _ = _kv_band(v_ref, n, False)
        lo_q = lax.broadcasted_iota(jnp.int32, (BLK, 128), 1) < HD
        dkk = [jnp.zeros((2 * BLK, 128), F32), jnp.zeros((2 * BLK, 128), F32)]
        dvv = [jnp.zeros((2 * BLK, 128), F32), jnp.zeros((2 * BLK, 128), F32)]

        def by_head(t):
            return jnp.concatenate([jnp.where(lo_q, t, 0.0), jnp.where(lo_q, 0.0, t)], axis=0).astype(BF16)

        for p in range(4):
            h = p // 2
            qt = q_ref[:, p * 128:(p + 1) * 128].astype(F32) * SCALE
            dot = do_ref[:, p * 128:(p + 1) * 128]
            pb = prob_ref[pl.ds(2 * p, 2)]
            prob = pb.astype(F32)
            ps = ps_ref[pl.ds(2 * p, 2), :].reshape(2, 1, BLK)
            dp = _dot_nt(vv[h], dot).reshape(2, 2 * BLK, BLK)
            delta = jnp.sum(prob * dp, axis=1, keepdims=True)
            ds = prob * (dp - delta)
            sink_part = ps * delta
            for e in range(2):
                gs = -jnp.sum(sink_part[e]).reshape(1, 1)
                gsk_ref[pl.ds(2 * p + e, 1), :] += jnp.broadcast_to(gs, (1, 128))
            dss_ref[pl.ds(2 * p, 2)] += ds
            dsb = ds.astype(BF16)
            dq_t = _dot(kt[h], dsb.reshape(4 * BLK, BLK))
            dq_ref[:, p * 128:(p + 1) * 128] = (dq_t.T * SCALE).astype(BF16)
            dkk[h] = dkk[h] + _dot(jnp.concatenate([dsb[0], dsb[1]], axis=1), by_head(qt))
            dvv[h] = dvv[h] + _dot(jnp.concatenate([pb[0], pb[1]], axis=1), by_head(dot.astype(F32)))
        for acc, ref in ((dkk, dk_ref), (dvv, dv_ref)):
            f0 = acc[0] + pltpu.roll(acc[0], HD, axis=1)
            f1 = acc[1] + pltpu.roll(acc[1], HD, axis=1)
            band = jnp.where(lo, f0, f1)
            ref[pl.ds(pstart, BLK), :] += band[0:BLK]
            ref[pl.ds(cstart, BLK), :] += band[BLK:2 * BLK]

        @pl.when(n == nblk - 1)
        def _():
            _relbias_grad(dss_ref, bk_ref[...], grb_ref)

    blk = pl.BlockSpec((BLK, 512), lambda i: (i, 0))
    kv = pl.BlockSpec((s, 128), lambda i: (0, 0))
    row8 = pl.BlockSpec((NQ, 128), lambda i: (0, 0))
    return pl.pallas_call(
        body, name="attn_bwd", grid=(nblk,),
        in_specs=[blk, blk, kv, kv, pl.BlockSpec((None, NQ, 2 * BLK, BLK), lambda i: (i, 0, 0, 0)),
                  pl.BlockSpec((None, NQ, BLK), lambda i: (i, 0, 0)), pl.BlockSpec((2 * BLK, BLK), lambda i: (0, 0))]
        + ([] if dep is None else [ANY]),
        out_specs=[blk, kv, kv, row8, row8],
        out_shape=[jax.ShapeDtypeStruct((s, 512), BF16), jax.ShapeDtypeStruct((s, 128), F32),
                   jax.ShapeDtypeStruct((s, 128), F32), jax.ShapeDtypeStruct((NQ, 128), F32),
                   jax.ShapeDtypeStruct((NQ, 128), F32)],
        scratch_shapes=[pltpu.VMEM((NQ, 2 * BLK, BLK), F32)],
        compiler_params=_cp(1))(q, do, k, v, probs, sink_share, bucket, *([] if dep is None else [dep]))


def _relbias_grad(ds_ref, bucket_v, o_ref):
    lane = lax.broadcasted_iota(jnp.int32, (NQ, 128), 1)
    head_row = lax.broadcasted_iota(jnp.int32, (NQ, BLK), 0)
    acc = jnp.zeros((NQ, 128), F32)
    for b in range(NBUCKET):
        sel = bucket_v == b
        stack = jnp.zeros((NQ, BLK), F32)
        for h in range(NQ):
            col_sums = jnp.sum(jnp.where(sel, ds_ref[h], 0.0), axis=0, keepdims=True)
            stack = jnp.where(head_row == h, col_sums, stack)
        acc = acc + jnp.where(lane == b, jnp.sum(stack, axis=1, keepdims=True), 0.0)
    o_ref[...] = acc


def _inproj_bwd(x, g1, du, dq, dk, dv, w_in, dx1, c_arr):
    s = x.shape[0]
    tm = min(TM, s)
    nt = s // tm
    cols = ((0, 512), (512, 1024), (1024, 1152), (1152, 1280))

    def body(c_ref, x_ref, g_ref, du_ref, dq_ref, dk_ref, dv_ref, w_ref, dx1_ref,
             gx_ref, own_ref, oth_ref, gg_ref, gw_ref):
        i = pl.program_id(0)

        @pl.when(i == 0)
        def _():
            gw_ref[...] = jnp.zeros_like(gw_ref)
            gg_ref[...] = jnp.zeros_like(gg_ref)

        gv = g_ref[...]
        r1, n1 = _rms(x_ref[...])
        h = (n1 * gv).astype(BF16)
        parts = (du_ref[...], dq_ref[...], dk_ref[...].astype(BF16), dv_ref[...].astype(BF16))
        dh = None
        for (a, b), dpart in zip(cols, parts):
            t = _dot(dpart, w_ref[a:b, :])
            dh = t if dh is None else dh + t
            gw_ref[a:b, :] += _dot_tn(dpart, h)
        dx, dg = _rms_bwd(r1, n1, gv, dh)
        gg_ref[...] += dg
        gx_ref[...] = dx1_ref[...] + dx

        @pl.when(i == nt - 1)
        def _():
            for k in range(NSH):
                _emit_halves(gw_ref, k * WIN_S, WIN_S, c_ref[0], own_ref.at[k], oth_ref.at[k])

    row = lambda w: pl.BlockSpec((tm, w), lambda i: (i, 0))
    vec = pl.BlockSpec((1, D), lambda i: (0, 0))
    full = pl.BlockSpec((IN_W, D), lambda i: (0, 0))
    half = pl.BlockSpec((NSH, WIN_S // 2, D), lambda i: (0, 0, 0))
    return pl.pallas_call(
        body, name="inproj_bwd", grid=(nt,),
        in_specs=[SMEM_SPEC, row(D), vec, row(512), row(512), row(128), row(128), full, row(D)],
        out_specs=[row(D), half, half, vec],
        out_shape=[jax.ShapeDtypeStruct((s, D), F32)] + _half_shapes(WIN_S) + [jax.ShapeDtypeStruct((1, D), F32)],
        scratch_shapes=[pltpu.VMEM((IN_W, D), F32)],
        compiler_params=_cp(1))(c_arr, x, g1, du, dq, dk, dv, w_in, dx1)


def _local_step(x, target, small, w_in, w_out, ffn_weights, c_arr, on_ffn_grads=None, on_wout_grads=None):
    g1, g2, g3, g4, w_pool, pool_scale, rel_bias, sinks = small
    bucket = jnp.asarray(_bucket_table())
    wp_bf = w_pool.astype(BF16)
    bias = _bias_table(bucket, rel_bias)
    h1 = _prenorm(x, g1)
    if callable(w_in):
        w_in = w_in(bias, h1)
    u, q, k, v = _inproj_fwd(h1, w_in)
    pool_o, pooled = _pool_fwd(u, wp_bf, pool_scale)
    attn_o, probs, sink_share = _attn_fwd(q, k, v, bias, sinks)
    g2_fwd = g2
    if callable(w_out):
        w_out, after = w_out(pool_o, attn_o)
        if after is not None:
            g2_fwd = g2 + after[:1, :1]
    mix, x1, h2 = _outproj_fwd(pool_o, attn_o, w_out, x, g2_fwd, g3)
    wg, wu, wd = ffn_weights(h2) if callable(ffn_weights) else ffn_weights
    gate, up, act_a, df, dy, loss, gg4 = _ffn_fwd(h2, wg, wu, wd, x1, target, g4)
    dgate, dup, dx1, dmix, gg3, gg2 = _ffn_bwd_act(df, gate, up, wg, wu, wd, dy, x1, mix, g3, g2)
    ffn_g = _ffn_bwd_w(h2, act_a, dgate, dup, df, c_arr)
    dep = None if on_ffn_grads is None else on_ffn_grads(ffn_g)
    dpool, dattn, wout_own, wout_oth = _outproj_bwd(dmix, w_out, pool_o, attn_o, c_arr, dep)
    dep = None if on_wout_grads is None else on_wout_grads(wout_own, wout_oth, dpool)
    du, gwp, gsc = _pool_bwd(pooled, dpool, wp_bf, pool_scale, dep)
    dq, dk, dv, grb, gsk = _attn_bwd(q, k, v, dattn, probs, sink_share, bucket, dep)
    gx, win_own, win_oth, gg1 = _inproj_bwd(x, g1, du, dq, dk, dv, w_in, dx1, c_arr)
    small_grads = (gg1, gg2, gg3, gg4, gwp, gsc, grb, gsk[:, 0].reshape(1, NQ))
    return loss, gx, small_grads, ((win_own, win_oth), (wout_own, wout_oth), ffn_g)


def _place():
    x, y, c = lax.axis_index("x"), lax.axis_index("y"), lax.axis_index("c")
    chips = ((1 - x, y), (x, 1 - y), (1 - x, 1 - y))
    return x, y, c, chips


def _remote(src, dst, ssem, rsem, dev):
    return pltpu.make_async_remote_copy(src_ref=src, dst_ref=dst, send_sem=ssem, recv_sem=rsem,
                                        device_id=dev, device_id_type=MESH_T)


def _to_sibling(arrs, name, after=()):
    nt, na = len(arrs), len(after)

    def body(*refs):
        srcs, dsts = refs[:nt], refs[nt + na:2 * nt + na]
        ssem, rsem = refs[2 * nt + na:]
        x, y, c, _ = _place()
        cps = [_remote(srcs[t], dsts[t], ssem.at[t], rsem.at[t], (x, y, 1 - c)) for t in range(nt)]
        for cp in cps:
            cp.start()
        for cp in cps:
            cp.wait()

    return pl.pallas_call(
        body, name=name, in_specs=[ANY] * (nt + na), out_specs=[ANY] * nt,
        out_shape=[jax.ShapeDtypeStruct(a.shape, a.dtype) for a in arrs],
        scratch_shapes=[pltpu.SemaphoreType.DMA((nt,)), pltpu.SemaphoreType.DMA((nt,))],
        compiler_params=_cp())(*arrs, *after)


HBM_SPEC = pl.BlockSpec(memory_space=pltpu.HBM)
SEM_SPEC = pl.BlockSpec(memory_space=pltpu.SEMAPHORE)
DATAFLOW = pltpu.SideEffectType.DATAFLOW_SIDE_EFFECTING


def _gather_copies(srcs, lands, ssem, rsem):
    x, y, c, chips = _place()
    me = 2 * x + y
    out = []
    for t in range(len(srcs)):
        half = srcs[t].shape[0] // 2
        mine = pl.ds(pl.multiple_of(c * half, 16), half)
        for j, (px, py) in enumerate(chips):
            k = 3 * t + j
            send = _remote(srcs[t].at[mine], lands[t].at[me, mine], ssem.at[k], rsem.at[k], (px, py, c))
            blk = lands[t].at[2 * px + py, mine]
            out.append((send, _remote(blk, blk, ssem.at[k], rsem.at[k], (px, py, c))))
    return out


def _scatter_copies(srcs, lands, ssem, rsem):
    x, y, c, chips = _place()
    out = []
    for t in range(len(srcs)):
        for j, (px, py) in enumerate(chips):
            k = 3 * t + j
            send = _remote(srcs[t].at[2 * px + py], lands[t].at[j], ssem.at[k], rsem.at[k], (px, py, c))
            out.append((send, _remote(lands[t].at[j], lands[t].at[j], ssem.at[k], rsem.at[k], (px, py, c))))
    return out


def _sibling_copies(srcs, lands, ssem, rsem):
    x, y, c, _ = _place()
    sib = (x, y, 1 - c)
    return [(_remote(srcs[t], lands[t], ssem.at[t], rsem.at[t], sib),
             _remote(lands[t], lands[t], ssem.at[t], rsem.at[t], sib)) for t in range(len(srcs))]


def _peer_copies(srcs, lands, ssem, rsem):
    x, y, c, _ = _place()
    me = 4 * x + 2 * y + c
    out = []
    for kk in range(1, 8):
        px, py, pc = x ^ (kk >> 2), y ^ ((kk >> 1) & 1), c ^ (kk & 1)
        send = _remote(srcs[0], lands[0].at[me], ssem.at[kk - 1], rsem.at[kk - 1], (px, py, pc))
        slot = lands[0].at[4 * px + 2 * py + pc]
        out.append((send, _remote(slot, slot, ssem.at[kk - 1], rsem.at[kk - 1], (px, py, pc))))
    return out


def _split_start(plan, ncopy, srcs, lands, after, name):
    ns, nbuf = len(srcs), len(srcs) + len(lands)
    extra = [] if after is None else [after]
    n_in = nbuf + len(extra)

    def body(*refs):
        ssem, rsem, token = refs[n_in], refs[n_in + 1], refs[-1]
        for send, _ in plan(refs[:ns], refs[ns:nbuf], ssem, rsem):
            send.start()
        token[...] = jnp.zeros_like(token)

    bufs = [pltpu.with_memory_space_constraint(a, pltpu.HBM) for a in list(srcs) + list(lands)]
    outs = pl.pallas_call(
        body, name=name, in_specs=[HBM_SPEC] * nbuf + [ANY] * len(extra),
        out_specs=[SEM_SPEC, SEM_SPEC] + [HBM_SPEC] * nbuf + [pl.BlockSpec(memory_space=pltpu.VMEM)],
        out_shape=[pltpu.SemaphoreType.DMA((ncopy,)), pltpu.SemaphoreType.DMA((ncopy,))]
        + [pltpu.HBM(a.shape, a.dtype) for a in bufs] + [jax.ShapeDtypeStruct((8, 128), F32)],
        input_output_aliases={i: 2 + i for i in range(nbuf)},
        compiler_params=pltpu.CompilerParams(has_side_effects=DATAFLOW))(*bufs, *extra)
    return outs[0], outs[1], outs[2:2 + ns], outs[2 + ns:2 + nbuf], outs[-1]


def _split_wait(plan, ssem, rsem, srcs, lands, after, name, only=None):
    ns, nbuf = len(srcs), len(srcs) + len(lands)

    def body(*refs):
        s_ref, r_ref = refs[nbuf], refs[nbuf + 1]
        for k, (send, recv) in enumerate(plan(refs[:ns], refs[ns:nbuf], s_ref, r_ref)):
            if only is None or k in only:
                send.wait_send()
                recv.wait_recv()

    outs = pl.pallas_call(
        body, name=name, in_specs=[HBM_SPEC] * nbuf + [SEM_SPEC, SEM_SPEC] + [ANY] * len(after),
        out_specs=[HBM_SPEC] * nbuf,
        out_shape=[pltpu.HBM(a.shape, a.dtype) for a in list(srcs) + list(lands)],
        input_output_aliases={i: i for i in range(nbuf)},
        compiler_params=pltpu.CompilerParams(has_side_effects=DATAFLOW))(*srcs, *lands, ssem, rsem, *after)
    return outs


def _gather_finish(lands, tag):
    nt = len(lands)

    def body(*refs):
        outs = refs[nt:2 * nt]
        dsend, drecv = refs[2 * nt:]
        x, y, c, chips = _place()
        sib = (x, y, 1 - c)
        sends = []
        for t in range(nt):
            half = outs[t].shape[1] // 2
            mine = pl.ds(pl.multiple_of(c * half, 16), half)
            for j, (px, py) in enumerate(chips):
                blk = outs[t].at[2 * px + py, mine]
                cp = _remote(blk, blk, dsend.at[t, j], drecv.at[t, j], sib)
                cp.start()
                sends.append(cp)
        for t in range(nt):
            half = outs[t].shape[1] // 2
            other = pl.ds(pl.multiple_of((1 - c) * half, 16), half)
            for j, (px, py) in enumerate(chips):
                blk = outs[t].at[2 * px + py, other]
                _remote(blk, blk, dsend.at[t, j], drecv.at[t, j], sib).wait_recv()
        for cp in sends:
            cp.wait_send()

    return pl.pallas_call(
        body, name="gather_finish_" + tag, in_specs=[ANY] * nt, out_specs=[ANY] * nt,
        out_shape=[jax.ShapeDtypeStruct(a.shape, a.dtype) for a in lands],
        input_output_aliases={t: t for t in range(nt)},
        scratch_shapes=[pltpu.SemaphoreType.DMA((nt, 3)), pltpu.SemaphoreType.DMA((nt, 3))],
        compiler_params=_cp())(*lands)


def _chip_partial(owns, recv, chip_arr, tag):
    nt = len(owns)

    def body(chip_ref, *refs):
        k = pl.program_id(0)
        for t in range(nt):
            p = refs[t][...] + refs[nt + t][...].astype(F32)
            refs[2 * nt + t][...] = p.astype(BF16)

            @pl.when(k == chip_ref[0])
            def _():
                refs[3 * nt + t][...] = p

    blk_specs = [pl.BlockSpec((None,) + a.shape[1:], lambda k, chip_ref: (k, 0, 0)) for a in owns]
    own_specs = [pl.BlockSpec(a.shape[1:], lambda k, chip_ref: (0, 0)) for a in owns]
    return pl.pallas_call(
        body, name="rs_chip_partial_" + tag,
        grid_spec=pltpu.PrefetchScalarGridSpec(num_scalar_prefetch=1, grid=(NSH,), in_specs=blk_specs * 2,
                                               out_specs=blk_specs + own_specs),
        out_shape=[jax.ShapeDtypeStruct(a.shape, BF16) for a in owns]
        + [jax.ShapeDtypeStruct(a.shape[1:], F32) for a in owns],
        compiler_params=_cp(1))(chip_arr, *owns, *recv)


def _sum_chips(own, recv, c_arr, tag):
    nt = len(own)

    def body(c_ref, *refs):
        outs = refs[2 * nt:]
        for t in range(nt):
            r = refs[nt + t]
            outs[t][...] = ((refs[t][...] + r[0].astype(F32)) + r[1].astype(F32)) + r[2].astype(F32)

    whole = lambda a: pl.BlockSpec(a.shape, lambda g, c_ref: (0,) * a.ndim)
    return pl.pallas_call(
        body, name="rs_sum_chips_" + tag,
        grid_spec=pltpu.PrefetchScalarGridSpec(
            num_scalar_prefetch=1, grid=(1,), in_specs=[whole(a) for a in list(own) + list(recv)],
            out_specs=[pl.BlockSpec((None,) + a.shape, lambda g, c_ref: (c_ref[0], 0, 0)) for a in own]),
        out_shape=[jax.ShapeDtypeStruct((2,) + a.shape, F32) for a in own],
        compiler_params=_cp(1))(c_arr, *own, *recv)


def _half_copies(srcs, lands, ssem, rsem):
    x, y, c, _ = _place()
    sib = (x, y, 1 - c)
    return [(_remote(lands[t].at[c], lands[t].at[c], ssem.at[t], rsem.at[t], sib),
             _remote(lands[t].at[1 - c], lands[t].at[1 - c], ssem.at[t], rsem.at[t], sib))
            for t in range(len(lands))]


def _share_halves(fulls, name):
    nt = len(fulls)

    def body(*refs):
        cps = _half_copies((), refs[nt:2 * nt], refs[2 * nt], refs[2 * nt + 1])
        for send, _ in cps:
            send.start()
        for send, recv in cps:
            send.wait_send()
            recv.wait_recv()

    return pl.pallas_call(
        body, name=name, in_specs=[ANY] * nt, out_specs=[ANY] * nt,
        out_shape=[jax.ShapeDtypeStruct(a.shape, a.dtype) for a in fulls],
        input_output_aliases={t: t for t in range(nt)},
        scratch_shapes=[pltpu.SemaphoreType.DMA((nt,)), pltpu.SemaphoreType.DMA((nt,))],
        compiler_params=_cp())(*fulls)


def _adamw_math(w, g, m, v):
    m = ADAM_B1 * m + (1.0 - ADAM_B1) * g
    v = ADAM_B2 * v + (1.0 - ADAM_B2) * (g * g)
    m_hat = m / (1.0 - ADAM_B1 ** ADAM_STEP)
    v_hat = v / (1.0 - ADAM_B2 ** ADAM_STEP)
    delta = -ADAM_LR * (m_hat / (jnp.sqrt(v_hat) + ADAM_EPS) + ADAM_WD * w)
    return delta, m, v


ADAM_TILES = 8


def _adamw_shards(gs, ws, ms, vs, tag):
    nt = len(gs)

    def body(*refs):
        for t in range(nt):
            delta, m, v = _adamw_math(refs[nt + t][...], refs[t][...], refs[2 * nt + t][...], refs[3 * nt + t][...])
            refs[4 * nt + t][...] = delta
            refs[5 * nt + t][...] = m
            refs[6 * nt + t][...] = v

    specs = [pl.BlockSpec((w.shape[0] // ADAM_TILES, w.shape[1]), lambda q: (q, 0)) for w in ws]
    return pl.pallas_call(
        body, name="adamw_shards_" + tag, grid=(ADAM_TILES,), in_specs=specs * 4, out_specs=specs * 3,
        out_shape=[jax.ShapeDtypeStruct(w.shape, F32) for w in ws] * 3,
        compiler_params=_cp(1))(*gs, *ws, *ms, *vs)


SMALL_WPOOL_ROW = 32
SMALL_SCALE_ROW = SMALL_WPOOL_ROW + 4 * GROUP
SMALL_BIAS_ROW = SMALL_SCALE_ROW + 8
SMALL_SINKS_ROW = SMALL_BIAS_ROW + NQ
SMALL_LOSS_ROW = SMALL_SINKS_ROW + 8


def _small_sum_adamw(gathered, wp, mp, vp):
    rows = wp.shape[0]

    def body(g_ref, w_ref, m_ref, v_ref, *rest):
        outs, res = rest[:-1], rest[-1]
        g = g_ref[0]
        for d in range(1, 8):
            g = g + g_ref[d]
        delta, m, v = _adamw_math(w_ref[...], g, m_ref[...], v_ref[...])
        for kind, val in enumerate((g, delta, m, v)):
            res[kind] = val
            gains = outs[8 * kind:8 * kind + 4]
            w_pool_o, scale_o, bias_o, sinks_o = outs[8 * kind + 4:8 * kind + 8]
            for t in range(4):
                for i in range(8):
                    gains[t][:, 128 * i:128 * (i + 1)] = res[kind, pl.ds(8 * t + i, 1), :]
            for grp in range(4):
                w_pool_o[grp] = res[kind, pl.ds(SMALL_WPOOL_ROW + GROUP * grp, GROUP), :]
            for i in range(4):
                scale_o[:, 128 * i:128 * (i + 1)] = res[kind, pl.ds(SMALL_SCALE_ROW + i, 1), :]
            bias_o[...] = res[kind, pl.ds(SMALL_BIAS_ROW, NQ), :][:, 0:NBUCKET]
            sinks_o[...] = res[kind, pl.ds(SMALL_SINKS_ROW, 1), :][:, 0:NQ]
        outs[-1][...] = res[0, pl.ds(SMALL_LOSS_ROW, 1), :]

    vm = pl.BlockSpec(memory_space=pltpu.VMEM)
    one_kind = [jax.ShapeDtypeStruct((1, D), F32)] * 4 + [
        jax.ShapeDtypeStruct((4, GROUP, GROUP), F32), jax.ShapeDtypeStruct((1, POOL_W), F32),
        jax.ShapeDtypeStruct((NQ, NBUCKET), F32), jax.ShapeDtypeStruct((1, NQ), F32)]
    return pl.pallas_call(
        body, name="small_sum_adamw", in_specs=[vm] * 4, out_specs=[vm] * 33,
        out_shape=one_kind * 4 + [jax.ShapeDtypeStruct((1, 128), F32)],
        scratch_shapes=[pltpu.VMEM((4, rows, 128), F32)],
        compiler_params=_cp())(gathered, wp, mp, vp)


def _pack_small(gains4, w_pool, pool_scale, rel_bias_t, sinks, loss):
    bias_rows = jnp.pad(rel_bias_t, ((0, 0), (0, 128 - rel_bias_t.shape[1])))
    parts = list(gains4) + [w_pool, pool_scale, bias_rows, sinks, loss]
    rows = []
    for a in parts:
        flat = a.reshape(-1)
        n = flat.shape[0]
        padded = -(-n // 1024) * 1024
        rows.append(jnp.pad(flat, (0, padded - n)).reshape(-1, 128))
    return jnp.concatenate(rows, axis=0)


def kernel(x, g_pre_mix, w_in, w_pool, pool_scale, rel_bias, sinks, w_out, g_post_mix, g_pre_ffn, w_gate, w_up, w_down, g_post_ffn, loss_target, m_g_pre_mix, m_w_in, m_w_pool, m_pool_scale, m_rel_bias, m_sinks, m_w_out, m_g_post_mix, m_g_pre_ffn, m_w_gate, m_w_up, m_w_down, m_g_post_ffn, v_g_pre_mix, v_w_in, v_w_pool, v_pool_scale, v_rel_bias, v_sinks, v_w_out, v_g_post_mix, v_g_pre_ffn, v_w_gate, v_w_up, v_w_down, v_g_post_ffn):
    c = lax.axis_index("c")
    chip = 2 * lax.axis_index("x") + lax.axis_index("y")

    def shards(a_in, a_out, a_gate, a_up, a_down):
        return (a_in[0].T, a_out[0], a_gate[0].T, a_up[0].T, a_down[0])

    c_arr = c.reshape(1).astype(jnp.int32)
    chip_arr = chip.reshape(1).astype(jnp.int32)

    big_w = shards(w_in, w_out, w_gate, w_up, w_down)
    big_bf = [w.astype(BF16) for w in big_w]
    g_ssem, g_rsem, g_srcs, g_lands, _ = _split_start(
        _gather_copies, 15, big_bf, [lax.empty((NSH,) + a.shape, BF16) for a in big_bf], None, "gather_start")
    fw = {}

    def with_own(land, shard):
        return lax.dynamic_update_slice(land, shard[None], (chip, 0, 0))

    def w_in_ready(*after):
        fw["first"] = _split_wait(_gather_copies, g_ssem, g_rsem, g_srcs, g_lands, after, "gather_win_wait",
                                  only=(0, 1, 2))
        (fw["win"],) = _gather_finish([with_own(fw["first"][5], fw["first"][0])], "win")
        return fw["win"].reshape(IN_W, D)

    def w_out_ready(*after):
        first = fw["first"]
        fw["second"] = _split_wait(_gather_copies, g_ssem, g_rsem, first[:5], [fw["win"]] + list(first[6:]), after,
                                   "gather_wout_wait", only=(3, 4, 5))
        (fw["wout"],) = _gather_finish([with_own(fw["second"][6], fw["second"][1])], "wout")
        return fw["wout"].reshape(D, D), None

    def ffn_weights(after):
        second = fw["second"]
        outs = _split_wait(_gather_copies, g_ssem, g_rsem, second[:5], [second[5], fw["wout"]] + list(second[7:]),
                           [after], "gather_ffn_wait", only=tuple(range(6, 15)))
        return _gather_finish([with_own(land, src) for src, land in zip(outs[2:5], outs[7:])], "ffn")

    rs = {}

    def on_ffn_grads(ffn_g):
        oths = ffn_g[1::2]
        rs["ffn_own"] = ffn_g[0::2]
        rs["x"] = _split_start(_sibling_copies, 3, oths, [lax.empty(a.shape, BF16) for a in oths], ffn_g[0],
                               "rs_exchange_ffn_start")
        return rs["x"][4]

    def on_wout_grads(own, oth, after):
        ssem, rsem, srcs, lands, _ = rs["x"]
        recv_ffn = _split_wait(_sibling_copies, ssem, rsem, srcs, lands, [after], "rs_exchange_ffn_wait")[3:]
        recv_wout = _to_sibling([oth], "rs_exchange_wout")
        outs = _chip_partial([own] + list(rs["ffn_own"]), list(recv_wout) + list(recv_ffn), chip_arr, "early")
        rs["early_own"] = outs[4:]
        rs["s"] = _split_start(_scatter_copies, 12, outs[:4],
                               [lax.empty((3,) + a.shape[1:], BF16) for a in outs[:4]], outs[4], "scatter_early_start")
        return rs["s"][4]

    small = (g_pre_mix, g_post_mix, g_pre_ffn, g_post_ffn, w_pool[0], pool_scale, rel_bias, sinks)
    loss, gx, small_grads, ((win_own, win_oth), _, _) = _local_step(
        x[0], loss_target[0], small, w_in_ready, w_out_ready, ffn_weights, c_arr, on_ffn_grads, on_wout_grads)

    ssem, rsem, srcs, lands, _ = rs["s"]
    recv_early = _split_wait(_scatter_copies, ssem, rsem, srcs, lands, [gx], "scatter_early_wait")[4:]
    finals = _sum_chips(list(rs["early_own"]), list(recv_early), c_arr, "early")
    sh_ssem, sh_rsem, _, sh_lands, sh_token = _split_start(_half_copies, 4, [], finals, None, "rs_share_early_start")

    unused = jnp.zeros((1, 1), F32)
    gp = _pack_small(small_grads[:4], *small_grads[4:], loss)
    wp = _pack_small((g_pre_mix, g_post_mix, g_pre_ffn, g_post_ffn), w_pool, pool_scale, rel_bias.T, sinks, unused)
    mp = _pack_small((m_g_pre_mix, m_g_post_mix, m_g_pre_ffn, m_g_post_ffn), m_w_pool, m_pool_scale, m_rel_bias.T,
                     m_sinks, unused)
    vp = _pack_small((v_g_pre_mix, v_g_post_mix, v_g_pre_ffn, v_g_post_ffn), v_w_pool, v_pool_scale, v_rel_bias.T,
                     v_sinks, unused)

    moments = (shards(m_w_in, m_w_out, m_w_gate, m_w_up, m_w_down),
               shards(v_w_in, v_w_out, v_w_gate, v_w_up, v_w_down))
    recv_a = _to_sibling([win_oth], "rs_exchange_win", [sh_token])
    outs = _chip_partial([win_own], recv_a, chip_arr, "win")
    w_ssem, w_rsem, w_srcs, w_lands, w_token = _split_start(
        _scatter_copies, 3, outs[:1], [lax.empty((3,) + outs[0].shape[1:], BF16)], gx, "scatter_win_start")
    slots = lax.dynamic_update_slice(lax.empty((8,) + gp.shape, F32), gp[None], (2 * chip + c, 0, 0))
    p_ssem, p_rsem, p_srcs, p_lands, p_token = _split_start(_peer_copies, 7, [gp], [slots], w_token,
                                                            "small_allgather_start")
    def joined(fulls):
        return [a.reshape(2 * a.shape[1], a.shape[2]) for a in fulls]

    g_early = joined(_split_wait(_half_copies, sh_ssem, sh_rsem, [], sh_lands, [p_token], "rs_share_early_wait"))
    adam_e = _adamw_shards(g_early, big_w[1:], moments[0][1:], moments[1][1:], "early")
    recv_win = _split_wait(_scatter_copies, w_ssem, w_rsem, w_srcs, w_lands, [adam_e[0]], "scatter_win_wait")[1:]
    g_win = joined(_share_halves(_sum_chips([outs[1]], recv_win, c_arr, "win"), "rs_share_win"))
    adam_w = _adamw_shards(g_win, big_w[:1], moments[0][:1], moments[1][:1], "win")
    big_g = g_win + g_early
    big_d, big_m, big_v = [[adam_w[i]] + list(adam_e[4 * i:4 * i + 4]) for i in range(3)]

    gathered = _split_wait(_peer_copies, p_ssem, p_rsem, p_srcs, p_lands, [adam_w[0]], "small_allgather_wait")[1]
    flat = _small_sum_adamw(gathered, wp, mp, vp)
    small_out = [flat[8 * kind:8 * kind + 8] for kind in range(4)]
    total_loss = flat[-1][0, 0]

    def ordered(small8, big4):
        sg1, sg2, sg3, sg4, swp, ssc, srb, ssk = small8
        swp, srb = swp[None], srb.T
        bwin, bwout, bwg, bwu, bwd = big4[0].T[None], big4[1][None], big4[2].T[None], big4[3].T[None], big4[4][None]
        return [sg1, bwin, swp, ssc, srb, ssk, bwout, sg2, sg3, bwg, bwu, bwd, sg4]

    res = [total_loss, gx[None]]
    for sm, bg in zip(small_out, (big_g, big_d, big_m, big_v)):
        res += ordered(sm, bg)
    return tuple(res)
```

```python
import numpy as np
import jax
import jax.numpy as jnp
from jax import lax
from jax.experimental import pallas as pl
from jax.experimental.pallas import tpu as pltpu

F32 = jnp.float32
BF16 = jnp.bfloat16

D = 1024
POOL_W = 512
GROUP = 128
WINDOWS = (2, 4, 8, 16)
HALO = 128
NQ = 8
BLK = 128
NBUCKET = 32
IN_W = 1280
DFF = 2816
NSH = 4
FS = DFF // NSH
WIN_S = IN_W // NSH
WOUT_S = D // NSH
EPS = 1e-6
NEG = -1e30
SCALE = 0.125

ADAM_LR = 0.001
ADAM_B1 = 0.9
ADAM_B2 = 0.999
ADAM_EPS = 1e-08
ADAM_WD = 0.01
ADAM_STEP = 10

TM = 512
TM_POOL = 512
TM_FFN = 512
TM_FFN_FWD = 1024
FFN_ROWS = 256
VMEM_LIMIT = 60 * 1024 * 1024

MESH_T = pl.DeviceIdType.MESH
ANY = pl.BlockSpec(memory_space=pl.ANY)


def _cp(n_grid=0, **kw):
    sem = ("arbitrary",) * n_grid if n_grid else None
    return pltpu.CompilerParams(dimension_semantics=sem, vmem_limit_bytes=VMEM_LIMIT, **kw)


def _dot(a, b):
    return jnp.dot(a, b, preferred_element_type=F32)


def _dot_nt(a, b):
    return lax.dot_general(a, b, (((1,), (1,)), ((), ())), preferred_element_type=F32)


def _dot_tn(a, b):
    return lax.dot_general(a, b, (((0,), (0,)), ((), ())), preferred_element_type=F32)


def _rms(x):
    r = lax.rsqrt(jnp.mean(x * x, axis=-1, keepdims=True) + EPS)
    return r, x * r


def _rms_bwd(r, n, g, dout):
    dn = dout * g
    dx = r * (dn - n * jnp.mean(dn * n, axis=-1, keepdims=True))
    dg = jnp.sum(dout * n, axis=0, keepdims=True)
    return dx, dg


def _split(x, n):
    parts = []
    r = x
    for _ in range(n):
        p = r.astype(BF16)
        parts.append(p)
        r = r - p.astype(F32)
    return parts


def _band_dot(a, x, n):
    acc = None
    for p in _split(x, n):
        t = _dot(a, p)
        acc = t if acc is None else acc + t
    return acc


def _bucket_table():
    qi = np.arange(BLK)[None, :]
    kj = np.arange(2 * BLK)[:, None]
    dist = qi + BLK - kj
    n = np.maximum(dist, 0)
    nf = np.maximum(n, 1).astype(np.float32)
    large = 16 + (np.log(nf / np.float32(16)) / np.float32(np.log(128 / 16)) * np.float32(16)).astype(np.int32)
    large = np.minimum(large, NBUCKET - 1)
    return np.where(n < 16, n, large).astype(np.int32)


def _prenorm(x, g1):
    s = x.shape[0]
    tm = min(TM, s)

    def body(x_ref, g_ref, h_ref):
        _, n = _rms(x_ref[...])
        h_ref[...] = (n * g_ref[...]).astype(BF16)

    row = pl.BlockSpec((tm, D), lambda i: (i, 0))
    return pl.pallas_call(
        body, name="prenorm", grid=(s // tm,),
        in_specs=[row, pl.BlockSpec((1, D), lambda i: (0, 0))], out_specs=row,
        out_shape=jax.ShapeDtypeStruct((s, D), BF16),
        compiler_params=_cp(1))(x, g1)


def _inproj_fwd(h1, w_in):
    s = h1.shape[0]
    tm = min(TM, s)

    def body(h_ref, w_ref, u_ref, q_ref, k_ref, v_ref):
        proj = _dot_nt(h_ref[...], w_ref[...])
        u_ref[...] = proj[:, :512]
        q_ref[...] = proj[:, 512:1024].astype(BF16)
        k_ref[...] = proj[:, 1024:1152].astype(BF16)
        v_ref[...] = proj[:, 1152:1280].astype(BF16)

    row = lambda w: pl.BlockSpec((tm, w), lambda i: (i, 0))
    return pl.pallas_call(
        body, name="inproj_fwd", grid=(s // tm,),
        in_specs=[row(D), pl.BlockSpec((IN_W, D), lambda i: (0, 0))],
        out_specs=[row(512), row(512), row(128), row(128)],
        out_shape=[jax.ShapeDtypeStruct((s, 512), F32), jax.ShapeDtypeStruct((s, 512), BF16),
                   jax.ShapeDtypeStruct((s, 128), BF16), jax.ShapeDtypeStruct((s, 128), BF16)],
        compiler_params=_cp(1))(h1, w_in)


def _window_sums(ext, w, lag_sign, tm, terms):
    rows = lax.broadcasted_iota(jnp.int32, (HALO, 2 * HALO), 0)
    cols = lax.broadcasted_iota(jnp.int32, (HALO, 2 * HALO), 1)
    d = rows + HALO - cols if lag_sign > 0 else cols - rows
    band = jnp.where((d >= 0) & (d < w), 1.0, 0.0).astype(BF16)
    return jnp.concatenate([_band_dot(band, ext[r:r + 2 * HALO], terms) for r in range(0, tm, HALO)], axis=0)


def _pooled(ext, cur, t0, tm):
    t = t0 + lax.broadcasted_iota(jnp.int32, (tm, GROUP), 0)
    out = []
    for g, w in enumerate(WINDOWS):
        sl = slice(g * GROUP, (g + 1) * GROUP)
        cnt = jnp.minimum(t + 1, w).astype(F32)
        out.append(_window_sums(ext[:, sl], w, 1, tm, 2) / cnt - cur[:, sl])
    return out


def _pool_fwd(u, w_pool, pool_scale):
    s = u.shape[0]
    tm = min(TM_POOL, s)
    hb = tm // HALO

    def body(uc_ref, uh_ref, wp_ref, sc_ref, o_ref, pooled_ref):
        i = pl.program_id(0)
        cur = uc_ref[...]
        halo = jnp.where(i > 0, uh_ref[...], 0.0)
        ext = jnp.concatenate([halo, cur], axis=0)
        pooled = _pooled(ext, cur, i * tm, tm)
        for g in range(4):
            sl = slice(g * GROUP, (g + 1) * GROUP)
            pb = pooled[g].astype(BF16)
            pooled_ref[:, sl] = pb
            o_ref[:, sl] = (_dot(pb, wp_ref[g]) * sc_ref[:, sl]).astype(BF16)

    row = pl.BlockSpec((tm, 512), lambda i: (i, 0))
    return pl.pallas_call(
        body, name="pool_fwd", grid=(s // tm,),
        in_specs=[row, pl.BlockSpec((HALO, 512), lambda i: (jnp.maximum(i * hb - 1, 0), 0)),
                  pl.BlockSpec((4, GROUP, GROUP), lambda i: (0, 0, 0)),
                  pl.BlockSpec((1, 512), lambda i: (0, 0))],
        out_specs=[row, row],
        out_shape=[jax.ShapeDtypeStruct((s, 512), BF16)] * 2,
        compiler_params=_cp(1))(u, u, w_pool, pool_scale)


def _bias_table(bucket, rel_bias):
    def body(b_ref, rb_ref, o_ref):
        bucket_v = b_ref[...]
        key = lax.broadcasted_iota(jnp.int32, (2 * BLK, BLK), 0)
        dist = lax.broadcasted_iota(jnp.int32, (2 * BLK, BLK), 1) + BLK - key
        inwin = (dist >= 0) & (dist < BLK)
        for h in range(NQ):
            acc = jnp.zeros((2 * BLK, BLK), F32)
            for b in range(NBUCKET):
                acc = jnp.where(bucket_v == b, rb_ref[b, h], acc)
            rest = jnp.where(inwin, acc, NEG)
            o_ref[1, h] = rest
            o_ref[0, h] = jnp.where(key >= BLK, rest, NEG)

    return pl.pallas_call(
        body, name="bias_table",
        in_specs=[pl.BlockSpec(memory_space=pltpu.VMEM), pl.BlockSpec(memory_space=pltpu.SMEM)],
        out_specs=pl.BlockSpec(memory_space=pltpu.VMEM),
        out_shape=jax.ShapeDtypeStruct((2, NQ, 2 * BLK, BLK), F32),
        compiler_params=_cp())(bucket, rel_bias)


HD = 64


def _kv_band(ref, n, transposed):
    pstart = pl.multiple_of(jnp.maximum(n - 1, 0) * BLK, BLK)
    cstart = pl.multiple_of(n * BLK, BLK)
    lo = lax.broadcasted_iota(jnp.int32, (2 * BLK, 128), 1) < HD
    band = jnp.concatenate([ref[pl.ds(pstart, BLK), :], ref[pl.ds(cstart, BLK), :]], axis=0).astype(F32)
    rot = pltpu.roll(band, HD, axis=1)
    halves = ((jnp.where(lo, band, 0.0), jnp.where(lo, 0.0, rot)), (jnp.where(lo, rot, 0.0), jnp.where(lo, 0.0, band)))
    if transposed:
        out = [jnp.concatenate([a.T, b.T], axis=1).astype(BF16) for a, b in halves]
    else:
        out = [jnp.concatenate([a, b], axis=0).astype(BF16) for a, b in halves]
    return out, (lo, pstart, cstart)


def _pair_probs(qt, kk, bias, sk_ref, p):
    sink = jnp.concatenate([jnp.full((1, 1, BLK), sk_ref[0, 2 * p + e], F32) for e in range(2)], axis=0)
    s = _dot_nt(kk, qt).reshape(2, 2 * BLK, BLK) + bias
    m = jnp.maximum(jnp.max(s, axis=1, keepdims=True), sink)
    pr = jnp.exp(s - m)
    es = jnp.exp(sink - m)
    inv = 1.0 / (jnp.sum(pr, axis=1, keepdims=True) + es)
    return pr * inv, es * inv


def _attn_fwd(q, k, v, bias, sinks):
    s = q.shape[0]
    nblk = s // BLK

    def body(q_ref, k_ref, v_ref, b_ref, sk_ref, o_ref, prob_ref, ps_ref):
        n = pl.program_id(0)
        kk, _ = _kv_band(k_ref, n, False)
        vt, _ = _kv_band(v_ref, n, True)
        bidx = jnp.minimum(n, 1)
        for p in range(4):
            h = p // 2
            qt = (q_ref[:, p * 128:(p + 1) * 128].astype(F32) * SCALE).astype(BF16)
            prob, ps = _pair_probs(qt, kk[h], b_ref[bidx, pl.ds(2 * p, 2)], sk_ref, p)
            pb = prob.astype(BF16)
            prob_ref[pl.ds(2 * p, 2)] = pb
            ps_ref[pl.ds(2 * p, 2), :] = ps.reshape(2, BLK)
            acc_t = _dot(vt[h], pb.reshape(4 * BLK, BLK))
            o_ref[:, p * 128:(p + 1) * 128] = acc_t.T.astype(BF16)

    return pl.pallas_call(
        body, name="attn_fwd", grid=(nblk,),
        in_specs=[pl.BlockSpec((BLK, 512), lambda i: (i, 0)),
                  pl.BlockSpec((s, 128), lambda i: (0, 0)),
                  pl.BlockSpec((s, 128), lambda i: (0, 0)),
                  pl.BlockSpec((2, NQ, 2 * BLK, BLK), lambda i: (0, 0, 0, 0)),
                  pl.BlockSpec(memory_space=pltpu.SMEM)],
        out_specs=[pl.BlockSpec((BLK, 512), lambda i: (i, 0)),
                   pl.BlockSpec((None, NQ, 2 * BLK, BLK), lambda i: (i, 0, 0, 0)),
                   pl.BlockSpec((None, NQ, BLK), lambda i: (i, 0, 0))],
        out_shape=[jax.ShapeDtypeStruct((s, 512), BF16), jax.ShapeDtypeStruct((nblk, NQ, 2 * BLK, BLK), BF16),
                   jax.ShapeDtypeStruct((nblk, NQ, BLK), F32)],
        compiler_params=_cp(1))(q, k, v, bias, sinks)


def _outproj_fwd(pool_o, attn_o, w_out, x, g2, g3):
    s = x.shape[0]
    tm = min(TM, s)

    def body(p_ref, a_ref, w_ref, x_ref, g2_ref, g3_ref, mix_ref, x1_ref, h2_ref):
        mix = _dot(p_ref[...], w_ref[0:512, :]) + _dot(a_ref[...], w_ref[512:1024, :])
        mix_ref[...] = mix
        _, n2 = _rms(mix)
        x1 = x_ref[...] + n2 * g2_ref[...]
        x1_ref[...] = x1
        _, n3 = _rms(x1)
        h2_ref[...] = (n3 * g3_ref[...]).astype(BF16)

    row = lambda w: pl.BlockSpec((tm, w), lambda i: (i, 0))
    vec = pl.BlockSpec((1, D), lambda i: (0, 0))
    return pl.pallas_call(
        body, name="outproj_fwd", grid=(s // tm,),
        in_specs=[row(512), row(512), pl.BlockSpec((D, D), lambda i: (0, 0)), row(D), vec, vec],
        out_specs=[row(D), row(D), row(D)],
        out_shape=[jax.ShapeDtypeStruct((s, D), F32), jax.ShapeDtypeStruct((s, D), F32),
                   jax.ShapeDtypeStruct((s, D), BF16)],
        compiler_params=_cp(1))(pool_o, attn_o, w_out, x, g2, g3)


def _silu_parts(g):
    sg = jax.nn.sigmoid(g)
    return sg, g * sg


def _ffn_fwd(h2, wg, wu, wd, x1, target, g4):
    s = h2.shape[0]
    tm = min(TM_FFN_FWD, s)

    def body(h_ref, wg_ref, wu_ref, wd_ref, x1_ref, t_ref, g4_ref,
             gate_ref, up_ref, a_ref, df_ref, dy_ref, loss_ref, gg4_ref, f_acc):
        i = pl.program_id(0)
        j = pl.program_id(1)
        first = j == 0
        chunks = [pl.ds(r, min(FFN_ROWS, tm)) for r in range(0, tm, FFN_ROWS)]
        project = lambda rows: (_dot_nt(h_ref[rows, :], wg_ref[...]), _dot_nt(h_ref[rows, :], wu_ref[...]))
        ahead = [project(chunks[0])]
        for k, rows in enumerate(chunks):
            if k + 1 < len(chunks):
                ahead.append(project(chunks[k + 1]))
            gate, up = ahead[k]
            gate_ref[rows, :] = gate.astype(BF16)
            up_ref[rows, :] = up.astype(BF16)
            _, sl = _silu_parts(gate)
            a = (sl * up).astype(BF16)
            a_ref[rows, :] = a
            contrib = _dot(a, wd_ref[...])
            f_acc[rows, :] = jnp.where(first, contrib, f_acc[rows, :] + contrib)

        @pl.when((i == 0) & (j == 0))
        def _():
            loss_ref[...] = jnp.zeros_like(loss_ref)
            gg4_ref[...] = jnp.zeros_like(gg4_ref)

        @pl.when(j == NSH - 1)
        def _():
            r4, n4 = _rms(f_acc[...])
            g4v = g4_ref[...]
            err = x1_ref[...] + n4 * g4v - t_ref[...]
            loss_ref[...] += (0.5 / D) * jnp.sum(err * err).reshape(1, 1)
            dy = err * (1.0 / D)
            dy_ref[...] = dy
            df, dg = _rms_bwd(r4, n4, g4v, dy)
            gg4_ref[...] += dg
            df_ref[...] = df.astype(BF16)

    row = lambda w: pl.BlockSpec((tm, w), lambda i, j: (i, 0))
    sh = lambda r, c: pl.BlockSpec((None, r, c), lambda i, j: (j, 0, 0))
    act = pl.BlockSpec((None, tm, FS), lambda i, j: (j, i, 0))
    vec = pl.BlockSpec((1, D), lambda i, j: (0, 0))
    return pl.pallas_call(
        body, name="ffn_fwd", grid=(s // tm, NSH),
        in_specs=[row(D), sh(FS, D), sh(FS, D), sh(FS, D), row(D), row(D), vec],
        out_specs=[act, act, act, row(D), row(D), pl.BlockSpec((1, 1), lambda i, j: (0, 0)), vec],
        out_shape=[jax.ShapeDtypeStruct((NSH, s, FS), BF16)] * 3
        + [jax.ShapeDtypeStruct((s, D), BF16), jax.ShapeDtypeStruct((s, D), F32),
           jax.ShapeDtypeStruct((1, 1), F32), jax.ShapeDtypeStruct((1, D), F32)],
        scratch_shapes=[pltpu.VMEM((tm, D), F32)],
        compiler_params=_cp(2))(h2, wg, wu, wd, x1, target, g4)


def _ffn_bwd_act(df, gate, up, wg, wu, wd, dy, x1, mix, g3, g2):
    s = df.shape[0]
    tm = min(TM_FFN, s)

    def body(df_ref, gate_ref, up_ref, wg_ref, wu_ref, wd_ref, dy_ref, x1_ref, mix_ref, g3_ref, g2_ref,
             dgate_ref, dup_ref, dx1_ref, dmix_ref, gg3_ref, gg2_ref, acc):
        j = pl.program_id(0)
        i = pl.program_id(1)
        first = j == 0
        tile = pl.multiple_of(i * tm, tm)
        chunks = [pl.ds(r, min(FFN_ROWS, tm)) for r in range(0, tm, FFN_ROWS)]
        das = [_dot_nt(df_ref[chunks[0], :], wd_ref[...])]
        for k, rows in enumerate(chunks):
            if k + 1 < len(chunks):
                das.append(_dot_nt(df_ref[chunks[k + 1], :], wd_ref[...]))
            da = das[k]
            g = gate_ref[rows, :].astype(F32)
            u = up_ref[rows, :].astype(F32)
            sg, sl = _silu_parts(g)
            dgate = (da * u * (sg * (1.0 + g * (1.0 - sg)))).astype(BF16)
            dup = (da * sl).astype(BF16)
            dgate_ref[rows, :] = dgate
            dup_ref[rows, :] = dup
            contrib = _dot(dgate, wg_ref[...]) + _dot(dup, wu_ref[...])
            acc_rows = pl.ds(tile + k * FFN_ROWS, rows.size)
            acc[acc_rows, :] = jnp.where(first, contrib, acc[acc_rows, :] + contrib)

        @pl.when((i == 0) & (j == 0))
        def _():
            gg3_ref[...] = jnp.zeros_like(gg3_ref)
            gg2_ref[...] = jnp.zeros_like(gg2_ref)

        @pl.when(j == NSH - 1)
        def _():
            r3, n3 = _rms(x1_ref[...])
            dx1n, dg3 = _rms_bwd(r3, n3, g3_ref[...], acc[pl.ds(tile, tm), :])
            dx1 = dy_ref[...] + dx1n
            dx1_ref[...] = dx1
            gg3_ref[...] += dg3
            r2, n2 = _rms(mix_ref[...])
            dmix, dg2 = _rms_bwd(r2, n2, g2_ref[...], dx1)
            gg2_ref[...] += dg2
            dmix_ref[...] = dmix.astype(BF16)

    row = pl.BlockSpec((tm, D), lambda j, i: (i, 0))
    last = pl.BlockSpec((tm, D), lambda j, i: (i * (j // (NSH - 1)), 0))
    sh = pl.BlockSpec((None, FS, D), lambda j, i: (j, 0, 0))
    act = pl.BlockSpec((None, tm, FS), lambda j, i: (j, i, 0))
    vec = pl.BlockSpec((1, D), lambda j, i: (0, 0))
    return pl.pallas_call(
        body, name="ffn_bwd_act", grid=(NSH, s // tm),
        in_specs=[row, act, act, sh, sh, sh, last, last, last, vec, vec],
        out_specs=[act, act, last, last, vec, vec],
        out_shape=[jax.ShapeDtypeStruct((NSH, s, FS), BF16), jax.ShapeDtypeStruct((NSH, s, FS), BF16),
                   jax.ShapeDtypeStruct((s, D), F32), jax.ShapeDtypeStruct((s, D), BF16),
                   jax.ShapeDtypeStruct((1, D), F32), jax.ShapeDtypeStruct((1, D), F32)],
        scratch_shapes=[pltpu.VMEM((s, D), F32)],
        compiler_params=_cp(2))(df, gate, up, wg, wu, wd, dy, x1, mix, g3, g2)


SMEM_SPEC = pl.BlockSpec(memory_space=pltpu.SMEM)


def _emit_halves(acc_ref, row0, rows, c, own_ref, oth_ref):
    half = rows // 2
    own_ref[...] = acc_ref[pl.ds(row0 + pl.multiple_of(c * half, 8), half), :]
    oth_ref[...] = acc_ref[pl.ds(row0 + pl.multiple_of((1 - c) * half, 8), half), :].astype(BF16)


def _half_shapes(rows):
    return [jax.ShapeDtypeStruct((NSH, rows // 2, D), F32), jax.ShapeDtypeStruct((NSH, rows // 2, D), BF16)]


def _ffn_bwd_w(h2, act_a, dgate, dup, df, c_arr):
    s = h2.shape[0]
    tm = min(TM_FFN_FWD, s)
    nt = s // tm

    def body(c_ref, h_ref, a_ref, dgate_ref, dup_ref, df_ref, *rest):
        outs, accs = rest[:6], rest[6:]
        i = pl.program_id(1)

        @pl.when(i == 0)
        def _():
            for acc in accs:
                acc[...] = jnp.zeros_like(acc)

        h = h_ref[...]
        accs[0][...] += _dot_tn(dgate_ref[...], h)
        accs[1][...] += _dot_tn(dup_ref[...], h)
        accs[2][...] += _dot_tn(a_ref[...], df_ref[...])

        @pl.when(i == nt - 1)
        def _():
            for t, acc in enumerate(accs):
                _emit_halves(acc, 0, FS, c_ref[0], outs[2 * t], outs[2 * t + 1])

    row = lambda w: pl.BlockSpec((tm, w), lambda j, i: (i, 0))
    act = pl.BlockSpec((None, tm, FS), lambda j, i: (j, i, 0))
    half = pl.BlockSpec((None, FS // 2, D), lambda j, i: (j, 0, 0))
    return pl.pallas_call(
        body, name="ffn_bwd_w", grid=(NSH, nt),
        in_specs=[SMEM_SPEC, row(D), act, act, act, row(D)],
        out_specs=[half] * 6,
        out_shape=_half_shapes(FS) * 3,
        scratch_shapes=[pltpu.VMEM((FS, D), F32)] * 3,
        compiler_params=_cp(2))(c_arr, h2, act_a, dgate, dup, df)


def _outproj_bwd(dmix, w_out, pool_o, attn_o, c_arr, dep=None):
    s = dmix.shape[0]
    tm = min(TM, s)
    nt = s // tm

    def body(c_ref, dm_ref, w_ref, p_ref, a_ref, *rest):
        dpool_ref, dattn_ref, own_ref, oth_ref, gw_ref = rest[-5:]
        i = pl.program_id(0)

        @pl.when(i == 0)
        def _():
            gw_ref[...] = jnp.zeros_like(gw_ref)

        dm = dm_ref[...]
        dpool_ref[...] = _dot_nt(dm, w_ref[0:512, :])
        dattn_ref[...] = _dot_nt(dm, w_ref[512:1024, :]).astype(BF16)
        gw_ref[0:512, :] += _dot_tn(p_ref[...], dm)
        gw_ref[512:1024, :] += _dot_tn(a_ref[...], dm)

        @pl.when(i == nt - 1)
        def _():
            for k in range(NSH):
                _emit_halves(gw_ref, k * WOUT_S, WOUT_S, c_ref[0], own_ref.at[k], oth_ref.at[k])

    row = lambda w: pl.BlockSpec((tm, w), lambda i: (i, 0))
    full = pl.BlockSpec((D, D), lambda i: (0, 0))
    half = pl.BlockSpec((NSH, WOUT_S // 2, D), lambda i: (0, 0, 0))
    return pl.pallas_call(
        body, name="outproj_bwd", grid=(nt,),
        in_specs=[SMEM_SPEC, row(D), full, row(512), row(512)] + ([] if dep is None else [ANY]),
        out_specs=[row(512), row(512), half, half],
        out_shape=[jax.ShapeDtypeStruct((s, 512), F32), jax.ShapeDtypeStruct((s, 512), BF16)] + _half_shapes(WOUT_S),
        scratch_shapes=[pltpu.VMEM((D, D), F32)],
        compiler_params=_cp(1))(c_arr, dmix, w_out, pool_o, attn_o, *([] if dep is None else [dep]))


def _pool_bwd(pooled, dpool, w_pool, pool_scale, dep=None):
    s = pooled.shape[0]
    tm = min(TM_POOL, s)
    hb = tm // HALO
    nt = s // tm
    last_halo = s // HALO - 1

    def body(p_ref, dc_ref, dn_ref, wp_ref, sc_ref, *rest):
        du_ref, gwp_ref, gsc_ref = rest[-3:]
        i = pl.program_id(0)

        @pl.when(i == 0)
        def _():
            gwp_ref[...] = jnp.zeros_like(gwp_ref)
            gsc_ref[...] = jnp.zeros_like(gsc_ref)

        dcur = dc_ref[...]
        dnext = jnp.where(i < nt - 1, dn_ref[...], 0.0)
        dext = jnp.concatenate([dcur, dnext], axis=0)
        t_ext = i * tm + lax.broadcasted_iota(jnp.int32, (tm + HALO, GROUP), 0)
        for g, w in enumerate(WINDOWS):
            sl = slice(g * GROUP, (g + 1) * GROUP)
            pb = p_ref[:, sl]
            wp = wp_ref[g]
            mixed = _dot(pb, wp)
            gsc_ref[:, sl] += jnp.sum(dcur[:, sl] * mixed, axis=0, keepdims=True)
            dmixed = (dext[:, sl] * sc_ref[:, sl]).astype(BF16)
            gwp_ref[g] += _dot_tn(pb, dmixed[0:tm])
            dpooled = _dot_nt(dmixed, wp)
            z = dpooled / jnp.minimum(t_ext + 1, w).astype(F32)
            du_ref[:, sl] = (_window_sums(z, w, -1, tm, 2) - dpooled[0:tm]).astype(BF16)

    return pl.pallas_call(
        body, name="pool_bwd", grid=(nt,),
        in_specs=[pl.BlockSpec((tm, 512), lambda i: (i, 0)),
                  pl.BlockSpec((tm, 512), lambda i: (i, 0)),
                  pl.BlockSpec((HALO, 512), lambda i: (jnp.minimum((i + 1) * hb, last_halo), 0)),
                  pl.BlockSpec((4, GROUP, GROUP), lambda i: (0, 0, 0)),
                  pl.BlockSpec((1, 512), lambda i: (0, 0))] + ([] if dep is None else [ANY]),
        out_specs=[pl.BlockSpec((tm, 512), lambda i: (i, 0)),
                   pl.BlockSpec((4, GROUP, GROUP), lambda i: (0, 0, 0)),
                   pl.BlockSpec((1, 512), lambda i: (0, 0))],
        out_shape=[jax.ShapeDtypeStruct((s, 512), BF16), jax.ShapeDtypeStruct((4, GROUP, GROUP), F32),
                   jax.ShapeDtypeStruct((1, 512), F32)],
        compiler_params=_cp(1))(pooled, dpool, dpool, w_pool, pool_scale, *([] if dep is None else [dep]))


def _attn_bwd(q, k, v, do, probs, sink_share, bucket, dep=None):
    s = q.shape[0]
    nblk = s // BLK

    def body(q_ref, do_ref, k_ref, v_ref, prob_ref, ps_ref, bk_ref, *rest):
        dq_ref, dk_ref, dv_ref, grb_ref, gsk_ref, dss_ref = rest[-6:]
        n = pl.program_id(0)

        @pl.when(n == 0)
        def _():
            dk_ref[...] = jnp.zeros_like(dk_ref)
            dv_ref[...] = jnp.zeros_like(dv_ref)
            dss_ref[...] = jnp.zeros_like(dss_ref)
            gsk_ref[...] = jnp.zeros_like(gsk_ref)

        kt, (lo, pstart, cstart) = _kv_band(k_ref, n, True)
        vv, _ = _kv_band(v_ref, n, False)
        lo_q = lax.broadcasted_iota(jnp.int32, (BLK, 128), 1) < HD
        dkk = [jnp.zeros((2 * BLK, 128), F32), jnp.zeros((2 * BLK, 128), F32)]
        dvv = [jnp.zeros((2 * BLK, 128), F32), jnp.zeros((2 * BLK, 128), F32)]

        def by_head(t):
            return jnp.concatenate([jnp.where(lo_q, t, 0.0), jnp.where(lo_q, 0.0, t)], axis=0).astype(BF16)

        for p in range(4):
            h = p // 2
            qt = q_ref[:, p * 128:(p + 1) * 128].astype(F32) * SCALE
            dot = do_ref[:, p * 128:(p + 1) * 128]
            pb = prob_ref[pl.ds(2 * p, 2)]
            prob = pb.astype(F32)
            ps = ps_ref[pl.ds(2 * p, 2), :].reshape(2, 1, BLK)
            dp = _dot_nt(vv[h], dot).reshape(2, 2 * BLK, BLK)
            delta = jnp.sum(prob * dp, axis=1, keepdims=True)
            ds = prob * (dp - delta)
            sink_part = ps * delta
            for e in range(2):
                gs = -jnp.sum(sink_part[e]).reshape(1, 1)
                gsk_ref[pl.ds(2 * p + e, 1), :] += jnp.broadcast_to(gs, (1, 128))
            dss_ref[pl.ds(2 * p, 2)] += ds
            dsb = ds.astype(BF16)
            dq_t = _dot(kt[h], dsb.reshape(4 * BLK, BLK))
            dq_ref[:, p * 128:(p + 1) * 128] = (dq_t.T * SCALE).astype(BF16)
            dkk[h] = dkk[h] + _dot(jnp.concatenate([dsb[0], dsb[1]], axis=1), by_head(qt))
            dvv[h] = dvv[h] + _dot(jnp.concatenate([pb[0], pb[1]], axis=1), by_head(dot.astype(F32)))
        for acc, ref in ((dkk, dk_ref), (dvv, dv_ref)):
            f0 = acc[0] + pltpu.roll(acc[0], HD, axis=1)
            f1 = acc[1] + pltpu.roll(acc[1], HD, axis=1)
            band = jnp.where(lo, f0, f1)
            ref[pl.ds(pstart, BLK), :] += band[0:BLK]
            ref[pl.ds(cstart, BLK), :] += band[BLK:2 * BLK]

        @pl.when(n == nblk - 1)
        def _():
            _relbias_grad(dss_ref, bk_ref[...], grb_ref)

    blk = pl.BlockSpec((BLK, 512), lambda i: (i, 0))
    kv = pl.BlockSpec((s, 128), lambda i: (0, 0))
    row8 = pl.BlockSpec((NQ, 128), lambda i: (0, 0))
    return pl.pallas_call(
        body, name="attn_bwd", grid=(nblk,),
        in_specs=[blk, blk, kv, kv, pl.BlockSpec((None, NQ, 2 * BLK, BLK), lambda i: (i, 0, 0, 0)),
                  pl.BlockSpec((None, NQ, BLK), lambda i: (i, 0, 0)), pl.BlockSpec((2 * BLK, BLK), lambda i: (0, 0))]
        + ([] if dep is None else [ANY]),
        out_specs=[blk, kv, kv, row8, row8],
        out_shape=[jax.ShapeDtypeStruct((s, 512), BF16), jax.ShapeDtypeStruct((s, 128), F32),
                   jax.ShapeDtypeStruct((s, 128), F32), jax.ShapeDtypeStruct((NQ, 128), F32),
                   jax.ShapeDtypeStruct((NQ, 128), F32)],
        scratch_shapes=[pltpu.VMEM((NQ, 2 * BLK, BLK), F32)],
        compiler_params=_cp(1))(q, do, k, v, probs, sink_share, bucket, *([] if dep is None else [dep]))


def _relbias_grad(ds_ref, bucket_v, o_ref):
    lane = lax.broadcasted_iota(jnp.int32, (NQ, 128), 1)
    head_row = lax.broadcasted_iota(jnp.int32, (NQ, BLK), 0)
    acc = jnp.zeros((NQ, 128), F32)
    for b in range(NBUCKET):
        sel = bucket_v == b
        stack = jnp.zeros((NQ, BLK), F32)
        for h in range(NQ):
            col_sums = jnp.sum(jnp.where(sel, ds_ref[h], 0.0), axis=0, keepdims=True)
            stack = jnp.where(head_row == h, col_sums, stack)
        acc = acc + jnp.where(lane == b, jnp.sum(stack, axis=1, keepdims=True), 0.0)
    o_ref[...] = acc


def _inproj_bwd(x, g1, du, dq, dk, dv, w_in, dx1, c_arr):
    s = x.shape[0]
    tm = min(TM, s)
    nt = s // tm
    cols = ((0, 512), (512, 1024), (1024, 1152), (1152, 1280))

    def body(c_ref, x_ref, g_ref, du_ref, dq_ref, dk_ref, dv_ref, w_ref, dx1_ref,
             gx_ref, own_ref, oth_ref, gg_ref, gw_ref):
        i = pl.program_id(0)

        @pl.when(i == 0)
        def _():
            gw_ref[...] = jnp.zeros_like(gw_ref)
            gg_ref[...] = jnp.zeros_like(gg_ref)

        gv = g_ref[...]
        r1, n1 = _rms(x_ref[...])
        h = (n1 * gv).astype(BF16)
        parts = (du_ref[...], dq_ref[...], dk_ref[...].astype(BF16), dv_ref[...].astype(BF16))
        dh = None
        for (a, b), dpart in zip(cols, parts):
            t = _dot(dpart, w_ref[a:b, :])
            dh = t if dh is None else dh + t
            gw_ref[a:b, :] += _dot_tn(dpart, h)
        dx, dg = _rms_bwd(r1, n1, gv, dh)
        gg_ref[...] += dg
        gx_ref[...] = dx1_ref[...] + dx

        @pl.when(i == nt - 1)
        def _():
            for k in range(NSH):
                _emit_halves(gw_ref, k * WIN_S, WIN_S, c_ref[0], own_ref.at[k], oth_ref.at[k])

    row = lambda w: pl.BlockSpec((tm, w), lambda i: (i, 0))
    vec = pl.BlockSpec((1, D), lambda i: (0, 0))
    full = pl.BlockSpec((IN_W, D), lambda i: (0, 0))
    half = pl.BlockSpec((NSH, WIN_S // 2, D), lambda i: (0, 0, 0))
    return pl.pallas_call(
        body, name="inproj_bwd", grid=(nt,),
        in_specs=[SMEM_SPEC, row(D), vec, row(512), row(512), row(128), row(128), full, row(D)],
        out_specs=[row(D), half, half, vec],
        out_shape=[jax.ShapeDtypeStruct((s, D), F32)] + _half_shapes(WIN_S) + [jax.ShapeDtypeStruct((1, D), F32)],
        scratch_shapes=[pltpu.VMEM((IN_W, D), F32)],
        compiler_params=_cp(1))(c_arr, x, g1, du, dq, dk, dv, w_in, dx1)


def _local_step(x, target, small, w_in, w_out, ffn_weights, c_arr, on_ffn_grads=None, on_wout_grads=None):
    g1, g2, g3, g4, w_pool, pool_scale, rel_bias, sinks = small
    bucket = jnp.asarray(_bucket_table())
    wp_bf = w_pool.astype(BF16)
    bias = _bias_table(bucket, rel_bias)
    h1 = _prenorm(x, g1)
    if callable(w_in):
        w_in = w_in(bias, h1)
    u, q, k, v = _inproj_fwd(h1, w_in)
    pool_o, pooled = _pool_fwd(u, wp_bf, pool_scale)
    attn_o, probs, sink_share = _attn_fwd(q, k, v, bias, sinks)
    g2_fwd = g2
    if callable(w_out):
        w_out, after = w_out(pool_o, attn_o)
        if after is not None:
            g2_fwd = g2 + after[:1, :1]
    mix, x1, h2 = _outproj_fwd(pool_o, attn_o, w_out, x, g2_fwd, g3)
    wg, wu, wd = ffn_weights(h2) if callable(ffn_weights) else ffn_weights
    gate, up, act_a, df, dy, loss, gg4 = _ffn_fwd(h2, wg, wu, wd, x1, target, g4)
    dgate, dup, dx1, dmix, gg3, gg2 = _ffn_bwd_act(df, gate, up, wg, wu, wd, dy, x1, mix, g3, g2)
    ffn_g = _ffn_bwd_w(h2, act_a, dgate, dup, df, c_arr)
    dep = None if on_ffn_grads is None else on_ffn_grads(ffn_g)
    dpool, dattn, wout_own, wout_oth = _outproj_bwd(dmix, w_out, pool_o, attn_o, c_arr, dep)
    dep = None if on_wout_grads is None else on_wout_grads(wout_own, wout_oth, dpool)
    du, gwp, gsc = _pool_bwd(pooled, dpool, wp_bf, pool_scale, dep)
    dq, dk, dv, grb, gsk = _attn_bwd(q, k, v, dattn, probs, sink_share, bucket, dep)
    gx, win_own, win_oth, gg1 = _inproj_bwd(x, g1, du, dq, dk, dv, w_in, dx1, c_arr)
    small_grads = (gg1, gg2, gg3, gg4, gwp, gsc, grb, gsk[:, 0].reshape(1, NQ))
    return loss, gx, small_grads, ((win_own, win_oth), (wout_own, wout_oth), ffn_g)


def _place():
    x, y, c = lax.axis_index("x"), lax.axis_index("y"), lax.axis_index("c")
    chips = ((1 - x, y), (x, 1 - y), (1 - x, 1 - y))
    return x, y, c, chips


def _remote(src, dst, ssem, rsem, dev):
    return pltpu.make_async_remote_copy(src_ref=src, dst_ref=dst, send_sem=ssem, recv_sem=rsem,
                                        device_id=dev, device_id_type=MESH_T)


def _to_sibling(arrs, name, after=()):
    nt, na = len(arrs), len(after)

    def body(*refs):
        srcs, dsts = refs[:nt], refs[nt + na:2 * nt + na]
        ssem, rsem = refs[2 * nt + na:]
        x, y, c, _ = _place()
        cps = [_remote(srcs[t], dsts[t], ssem.at[t], rsem.at[t], (x, y, 1 - c)) for t in range(nt)]
        for cp in cps:
            cp.start()
        for cp in cps:
            cp.wait()

    return pl.pallas_call(
        body, name=name, in_specs=[ANY] * (nt + na), out_specs=[ANY] * nt,
        out_shape=[jax.ShapeDtypeStruct(a.shape, a.dtype) for a in arrs],
        scratch_shapes=[pltpu.SemaphoreType.DMA((nt,)), pltpu.SemaphoreType.DMA((nt,))],
        compiler_params=_cp())(*arrs, *after)


HBM_SPEC = pl.BlockSpec(memory_space=pltpu.HBM)
SEM_SPEC = pl.BlockSpec(memory_space=pltpu.SEMAPHORE)
DATAFLOW = pltpu.SideEffectType.DATAFLOW_SIDE_EFFECTING


def _gather_copies(srcs, lands, ssem, rsem):
    x, y, c, chips = _place()
    me = 2 * x + y
    out = []
    for t in range(len(srcs)):
        half = srcs[t].shape[0] // 2
        mine = pl.ds(pl.multiple_of(c * half, 16), half)
        for j, (px, py) in enumerate(chips):
            k = 3 * t + j
            send = _remote(srcs[t].at[mine], lands[t].at[me, mine], ssem.at[k], rsem.at[k], (px, py, c))
            blk = lands[t].at[2 * px + py, mine]
            out.append((send, _remote(blk, blk, ssem.at[k], rsem.at[k], (px, py, c))))
    return out


def _scatter_copies(srcs, lands, ssem, rsem):
    x, y, c, chips = _place()
    out = []
    for t in range(len(srcs)):
        for j, (px, py) in enumerate(chips):
            k = 3 * t + j
            send = _remote(srcs[t].at[2 * px + py], lands[t].at[j], ssem.at[k], rsem.at[k], (px, py, c))
            out.append((send, _remote(lands[t].at[j], lands[t].at[j], ssem.at[k], rsem.at[k], (px, py, c))))
    return out


def _sibling_copies(srcs, lands, ssem, rsem):
    x, y, c, _ = _place()
    sib = (x, y, 1 - c)
    return [(_remote(srcs[t], lands[t], ssem.at[t], rsem.at[t], sib),
             _remote(lands[t], lands[t], ssem.at[t], rsem.at[t], sib)) for t in range(len(srcs))]


def _peer_copies(srcs, lands, ssem, rsem):
    x, y, c, _ = _place()
    me = 4 * x + 2 * y + c
    out = []
    for kk in range(1, 8):
        px, py, pc = x ^ (kk >> 2), y ^ ((kk >> 1) & 1), c ^ (kk & 1)
        send = _remote(srcs[0], lands[0].at[me], ssem.at[kk - 1], rsem.at[kk - 1], (px, py, pc))
        slot = lands[0].at[4 * px + 2 * py + pc]
        out.append((send, _remote(slot, slot, ssem.at[kk - 1], rsem.at[kk - 1], (px, py, pc))))
    return out


def _split_start(plan, ncopy, srcs, lands, after, name):
    ns, nbuf = len(srcs), len(srcs) + len(lands)
    extra = [] if after is None else [after]
    n_in = nbuf + len(extra)

    def body(*refs):
        ssem, rsem, token = refs[n_in], refs[n_in + 1], refs[-1]
        for send, _ in plan(refs[:ns], refs[ns:nbuf], ssem, rsem):
            send.start()
        token[...] = jnp.zeros_like(token)

    bufs = [pltpu.with_memory_space_constraint(a, pltpu.HBM) for a in list(srcs) + list(lands)]
    outs = pl.pallas_call(
        body, name=name, in_specs=[HBM_SPEC] * nbuf + [ANY] * len(extra),
        out_specs=[SEM_SPEC, SEM_SPEC] + [HBM_SPEC] * nbuf + [pl.BlockSpec(memory_space=pltpu.VMEM)],
        out_shape=[pltpu.SemaphoreType.DMA((ncopy,)), pltpu.SemaphoreType.DMA((ncopy,))]
        + [pltpu.HBM(a.shape, a.dtype) for a in bufs] + [jax.ShapeDtypeStruct((8, 128), F32)],
        input_output_aliases={i: 2 + i for i in range(nbuf)},
        compiler_params=pltpu.CompilerParams(has_side_effects=DATAFLOW))(*bufs, *extra)
    return outs[0], outs[1], outs[2:2 + ns], outs[2 + ns:2 + nbuf], outs[-1]


def _split_wait(plan, ssem, rsem, srcs, lands, after, name, only=None):
    ns, nbuf = len(srcs), len(srcs) + len(lands)

    def body(*refs):
        s_ref, r_ref = refs[nbuf], refs[nbuf + 1]
        for k, (send, recv) in enumerate(plan(refs[:ns], refs[ns:nbuf], s_ref, r_ref)):
            if only is None or k in only:
                send.wait_send()
                recv.wait_recv()

    outs = pl.pallas_call(
        body, name=name, in_specs=[HBM_SPEC] * nbuf + [SEM_SPEC, SEM_SPEC] + [ANY] * len(after),
        out_specs=[HBM_SPEC] * nbuf,
        out_shape=[pltpu.HBM(a.shape, a.dtype) for a in list(srcs) + list(lands)],
        input_output_aliases={i: i for i in range(nbuf)},
        compiler_params=pltpu.CompilerParams(has_side_effects=DATAFLOW))(*srcs, *lands, ssem, rsem, *after)
    return outs


def _gather_finish(lands, tag):
    nt = len(lands)

    def body(*refs):
        outs = refs[nt:2 * nt]
        dsend, drecv = refs[2 * nt:]
        x, y, c, chips = _place()
        sib = (x, y, 1 - c)
        sends = []
        for t in range(nt):
            half = outs[t].shape[1] // 2
            mine = pl.ds(pl.multiple_of(c * half, 16), half)
            for j, (px, py) in enumerate(chips):
                blk = outs[t].at[2 * px + py, mine]
                cp = _remote(blk, blk, dsend.at[t, j], drecv.at[t, j], sib)
                cp.start()
                sends.append(cp)
        for t in range(nt):
            half = outs[t].shape[1] // 2
            other = pl.ds(pl.multiple_of((1 - c) * half, 16), half)
            for j, (px, py) in enumerate(chips):
                blk = outs[t].at[2 * px + py, other]
                _remote(blk, blk, dsend.at[t, j], drecv.at[t, j], sib).wait_recv()
        for cp in sends:
            cp.wait_send()

    return pl.pallas_call(
        body, name="gather_finish_" + tag, in_specs=[ANY] * nt, out_specs=[ANY] * nt,
        out_shape=[jax.ShapeDtypeStruct(a.shape, a.dtype) for a in lands],
        input_output_aliases={t: t for t in range(nt)},
        scratch_shapes=[pltpu.SemaphoreType.DMA((nt, 3)), pltpu.SemaphoreType.DMA((nt, 3))],
        compiler_params=_cp())(*lands)


def _chip_partial(owns, recv, chip_arr, tag):
    nt = len(owns)

    def body(chip_ref, *refs):
        k = pl.program_id(0)
        for t in range(nt):
            p = refs[t][...] + refs[nt + t][...].astype(F32)
            refs[2 * nt + t][...] = p.astype(BF16)

            @pl.when(k == chip_ref[0])
            def _():
                refs[3 * nt + t][...] = p

    blk_specs = [pl.BlockSpec((None,) + a.shape[1:], lambda k, chip_ref: (k, 0, 0)) for a in owns]
    own_specs = [pl.BlockSpec(a.shape[1:], lambda k, chip_ref: (0, 0)) for a in owns]
    return pl.pallas_call(
        body, name="rs_chip_partial_" + tag,
        grid_spec=pltpu.PrefetchScalarGridSpec(num_scalar_prefetch=1, grid=(NSH,), in_specs=blk_specs * 2,
                                               out_specs=blk_specs + own_specs),
        out_shape=[jax.ShapeDtypeStruct(a.shape, BF16) for a in owns]
        + [jax.ShapeDtypeStruct(a.shape[1:], F32) for a in owns],
        compiler_params=_cp(1))(chip_arr, *owns, *recv)


def _sum_chips(own, recv, tag, after=()):
    nt = len(own)

    def body(*refs):
        outs = refs[2 * nt + len(after):]
        for t in range(nt):
            r = refs[nt + t]
            outs[t][...] = ((refs[t][...] + r[0].astype(F32)) + r[1].astype(F32)) + r[2].astype(F32)

    vm = pl.BlockSpec(memory_space=pltpu.VMEM)
    return pl.pallas_call(
        body, name="rs_sum_chips_" + tag, in_specs=[vm] * (2 * nt) + [ANY] * len(after), out_specs=[vm] * nt,
        out_shape=[jax.ShapeDtypeStruct(a.shape, F32) for a in own],
        compiler_params=_cp())(*own, *recv, *after)


def _adamw_math(w, g, m, v):
    m = ADAM_B1 * m + (1.0 - ADAM_B1) * g
    v = ADAM_B2 * v + (1.0 - ADAM_B2) * (g * g)
    m_hat = m / (1.0 - ADAM_B1 ** ADAM_STEP)
    v_hat = v / (1.0 - ADAM_B2 ** ADAM_STEP)
    delta = -ADAM_LR * (m_hat / (jnp.sqrt(v_hat) + ADAM_EPS) + ADAM_WD * w)
    return delta, m, v


ADAM_SPLIT = 4


def _adamw_shards(own, sib, ws, ms, vs, c_arr, tag):
    nt = len(own)

    def body(c_ref, *refs):
        hh = pl.program_id(0)
        mine = hh == c_ref[0]
        for t in range(nt):
            g = jnp.where(mine, refs[t][...], refs[nt + t][...])
            delta, m, v = _adamw_math(refs[2 * nt + t][...], g, refs[3 * nt + t][...], refs[4 * nt + t][...])
            refs[5 * nt + t][...] = g
            refs[6 * nt + t][...] = delta
            refs[7 * nt + t][...] = m
            refs[8 * nt + t][...] = v

    def tile(a):
        return (a.shape[0] // ADAM_SPLIT, a.shape[1])

    def half_index(used_when_mine):
        def index(hh, q, c_ref):
            mine = 1 - (hh - c_ref[0]) * (hh - c_ref[0])
            used = mine if used_when_mine else 1 - mine
            return (used * q + (1 - used) * hh * (ADAM_SPLIT - 1), 0)
        return index

    own_specs = [pl.BlockSpec(tile(a), half_index(True)) for a in own]
    sib_specs = [pl.BlockSpec(tile(a), half_index(False)) for a in own]
    full_specs = [pl.BlockSpec(tile(a), lambda hh, q, c_ref: (hh * ADAM_SPLIT + q, 0)) for a in own]
    return pl.pallas_call(
        body, name="adamw_shards_" + tag,
        grid_spec=pltpu.PrefetchScalarGridSpec(num_scalar_prefetch=1, grid=(2, ADAM_SPLIT),
                                               in_specs=own_specs + sib_specs + full_specs * 3,
                                               out_specs=full_specs * 4),
        out_shape=[jax.ShapeDtypeStruct(w.shape, F32) for w in ws] * 4,
        compiler_params=_cp(2))(c_arr, *own, *sib, *ws, *ms, *vs)


SMALL_WPOOL_ROW = 32
SMALL_SCALE_ROW = SMALL_WPOOL_ROW + 4 * GROUP
SMALL_BIAS_ROW = SMALL_SCALE_ROW + 8
SMALL_SINKS_ROW = SMALL_BIAS_ROW + NQ
SMALL_LOSS_ROW = SMALL_SINKS_ROW + 8


def _small_sum_adamw(gathered, wp, mp, vp):
    rows = wp.shape[0]

    def body(g_ref, w_ref, m_ref, v_ref, *rest):
        outs, res = rest[:-1], rest[-1]
        g = g_ref[0]
        for d in range(1, 8):
            g = g + g_ref[d]
        delta, m, v = _adamw_math(w_ref[...], g, m_ref[...], v_ref[...])
        for kind, val in enumerate((g, delta, m, v)):
            res[kind] = val
            gains = outs[8 * kind:8 * kind + 4]
            w_pool_o, scale_o, bias_o, sinks_o = outs[8 * kind + 4:8 * kind + 8]
            for t in range(4):
                for i in range(8):
                    gains[t][:, 128 * i:128 * (i + 1)] = res[kind, pl.ds(8 * t + i, 1), :]
            for grp in range(4):
                w_pool_o[grp] = res[kind, pl.ds(SMALL_WPOOL_ROW + GROUP * grp, GROUP), :]
            for i in range(4):
                scale_o[:, 128 * i:128 * (i + 1)] = res[kind, pl.ds(SMALL_SCALE_ROW + i, 1), :]
            bias_o[...] = res[kind, pl.ds(SMALL_BIAS_ROW, NQ), :][:, 0:NBUCKET]
            sinks_o[...] = res[kind, pl.ds(SMALL_SINKS_ROW, 1), :][:, 0:NQ]
        outs[-1][...] = res[0, pl.ds(SMALL_LOSS_ROW, 1), :]

    vm = pl.BlockSpec(memory_space=pltpu.VMEM)
    one_kind = [jax.ShapeDtypeStruct((1, D), F32)] * 4 + [
        jax.ShapeDtypeStruct((4, GROUP, GROUP), F32), jax.ShapeDtypeStruct((1, POOL_W), F32),
        jax.ShapeDtypeStruct((NQ, NBUCKET), F32), jax.ShapeDtypeStruct((1, NQ), F32)]
    return pl.pallas_call(
        body, name="small_sum_adamw", in_specs=[vm] * 4, out_specs=[vm] * 33,
        out_shape=one_kind * 4 + [jax.ShapeDtypeStruct((1, 128), F32)],
        scratch_shapes=[pltpu.VMEM((4, rows, 128), F32)],
        compiler_params=_cp())(gathered, wp, mp, vp)


def _pack_small(gains4, w_pool, pool_scale, rel_bias_t, sinks, loss):
    bias_rows = jnp.pad(rel_bias_t, ((0, 0), (0, 128 - rel_bias_t.shape[1])))
    parts = list(gains4) + [w_pool, pool_scale, bias_rows, sinks, loss]
    rows = []
    for a in parts:
        flat = a.reshape(-1)
        n = flat.shape[0]
        padded = -(-n // 1024) * 1024
        rows.append(jnp.pad(flat, (0, padded - n)).reshape(-1, 128))
    return jnp.concatenate(rows, axis=0)


def kernel(x, g_pre_mix, w_in, w_pool, pool_scale, rel_bias, sinks, w_out, g_post_mix, g_pre_ffn, w_gate, w_up, w_down, g_post_ffn, loss_target, m_g_pre_mix, m_w_in, m_w_pool, m_pool_scale, m_rel_bias, m_sinks, m_w_out, m_g_post_mix, m_g_pre_ffn, m_w_gate, m_w_up, m_w_down, m_g_post_ffn, v_g_pre_mix, v_w_in, v_w_pool, v_pool_scale, v_rel_bias, v_sinks, v_w_out, v_g_post_mix, v_g_pre_ffn, v_w_gate, v_w_up, v_w_down, v_g_post_ffn):
    c = lax.axis_index("c")
    chip = 2 * lax.axis_index("x") + lax.axis_index("y")

    def shards(a_in, a_out, a_gate, a_up, a_down):
        return (a_in[0].T, a_out[0], a_gate[0].T, a_up[0].T, a_down[0])

    c_arr = c.reshape(1).astype(jnp.int32)
    chip_arr = chip.reshape(1).astype(jnp.int32)

    big_w = shards(w_in, w_out, w_gate, w_up, w_down)
    big_bf = [w.astype(BF16) for w in big_w]
    g_ssem, g_rsem, g_srcs, g_lands, _ = _split_start(
        _gather_copies, 15, big_bf, [lax.empty((NSH,) + a.shape, BF16) for a in big_bf], None, "gather_start")
    fw = {}

    def with_own(land, shard):
        return lax.dynamic_update_slice(land, shard[None], (chip, 0, 0))

    def w_in_ready(*after):
        fw["first"] = _split_wait(_gather_copies, g_ssem, g_rsem, g_srcs, g_lands, after, "gather_win_wait",
                                  only=(0, 1, 2))
        (fw["win"],) = _gather_finish([with_own(fw["first"][5], fw["first"][0])], "win")
        return fw["win"].reshape(IN_W, D)

    def w_out_ready(*after):
        first = fw["first"]
        fw["second"] = _split_wait(_gather_copies, g_ssem, g_rsem, first[:5], [fw["win"]] + list(first[6:]), after,
                                   "gather_wout_wait", only=(3, 4, 5))
        (fw["wout"],) = _gather_finish([with_own(fw["second"][6], fw["second"][1])], "wout")
        return fw["wout"].reshape(D, D), None

    def ffn_weights(after):
        second = fw["second"]
        outs = _split_wait(_gather_copies, g_ssem, g_rsem, second[:5], [second[5], fw["wout"]] + list(second[7:]),
                           [after], "gather_ffn_wait", only=tuple(range(6, 15)))
        return _gather_finish([with_own(land, src) for src, land in zip(outs[2:5], outs[7:])], "ffn")

    rs = {}

    def on_ffn_grads(ffn_g):
        oths = ffn_g[1::2]
        rs["ffn_own"] = ffn_g[0::2]
        rs["x"] = _split_start(_sibling_copies, 3, oths, [lax.empty(a.shape, BF16) for a in oths], ffn_g[0],
                               "rs_exchange_ffn_start")
        return rs["x"][4]

    def on_wout_grads(own, oth, after):
        ssem, rsem, srcs, lands, _ = rs["x"]
        recv_ffn = _split_wait(_sibling_copies, ssem, rsem, srcs, lands, [after], "rs_exchange_ffn_wait")[3:]
        recv_wout = _to_sibling([oth], "rs_exchange_wout")
        outs = _chip_partial([own] + list(rs["ffn_own"]), list(recv_wout) + list(recv_ffn), chip_arr, "early")
        rs["early_own"] = outs[4:]
        rs["s"] = _split_start(_scatter_copies, 12, outs[:4],
                               [lax.empty((3,) + a.shape[1:], BF16) for a in outs[:4]], outs[4], "scatter_early_start")
        return rs["s"][4]

    small = (g_pre_mix, g_post_mix, g_pre_ffn, g_post_ffn, w_pool[0], pool_scale, rel_bias, sinks)
    loss, gx, small_grads, ((win_own, win_oth), _, _) = _local_step(
        x[0], loss_target[0], small, w_in_ready, w_out_ready, ffn_weights, c_arr, on_ffn_grads, on_wout_grads)

    ssem, rsem, srcs, lands, _ = rs["s"]
    recv_early = _split_wait(_scatter_copies, ssem, rsem, srcs, lands, [gx], "scatter_early_wait")[4:]
    finals = _sum_chips(list(rs["early_own"]), list(recv_early), "early")
    sh_ssem, sh_rsem, sh_srcs, sh_lands, sh_token = _split_start(
        _sibling_copies, 4, finals, [lax.empty(a.shape, F32) for a in finals], finals[0], "rs_share_early_start")

    unused = jnp.zeros((1, 1), F32)
    gp = _pack_small(small_grads[:4], *small_grads[4:], loss)
    wp = _pack_small((g_pre_mix, g_post_mix, g_pre_ffn, g_post_ffn), w_pool, pool_scale, rel_bias.T, sinks, unused)
    mp = _pack_small((m_g_pre_mix, m_g_post_mix, m_g_pre_ffn, m_g_post_ffn), m_w_pool, m_pool_scale, m_rel_bias.T,
                     m_sinks, unused)
    vp = _pack_small((v_g_pre_mix, v_g_post_mix, v_g_pre_ffn, v_g_post_ffn), v_w_pool, v_pool_scale, v_rel_bias.T,
                     v_sinks, unused)

    moments = (shards(m_w_in, m_w_out, m_w_gate, m_w_up, m_w_down),
               shards(v_w_in, v_w_out, v_w_gate, v_w_up, v_w_down))
    recv_a = _to_sibling([win_oth], "rs_exchange_win", [sh_token])
    outs = _chip_partial([win_own], recv_a, chip_arr, "win")
    w_ssem, w_rsem, w_srcs, w_lands, w_token = _split_start(
        _scatter_copies, 3, outs[:1], [lax.empty((3,) + outs[0].shape[1:], BF16)], gx, "scatter_win_start")
    slots = lax.dynamic_update_slice(lax.empty((8,) + gp.shape, F32), gp[None], (2 * chip + c, 0, 0))
    p_ssem, p_rsem, p_srcs, p_lands, p_token = _split_start(_peer_copies, 7, [gp], [slots], w_token,
                                                            "small_allgather_start")
    shared = _split_wait(_sibling_copies, sh_ssem, sh_rsem, sh_srcs, sh_lands, [p_token], "rs_share_early_wait")
    adam_e = _adamw_shards(shared[:4], shared[4:], big_w[1:], moments[0][1:], moments[1][1:], c_arr, "early")
    recv_win = _split_wait(_scatter_copies, w_ssem, w_rsem, w_srcs, w_lands, [adam_e[0]], "scatter_win_wait")[1:]
    win_final = _sum_chips([outs[1]], recv_win, "win")
    win_sib = _to_sibling(win_final, "rs_share_win")
    adam_w = _adamw_shards(win_final, win_sib, big_w[:1], moments[0][:1], moments[1][:1], c_arr, "win")
    big_g, big_d, big_m, big_v = [[adam_w[i]] + list(adam_e[4 * i:4 * i + 4]) for i in range(4)]

    gathered = _split_wait(_peer_copies, p_ssem, p_rsem, p_srcs, p_lands, [adam_w[0]], "small_allgather_wait")[1]
    flat = _small_sum_adamw(gathered, wp, mp, vp)
    small_out = [flat[8 * kind:8 * kind + 8] for kind in range(4)]
    total_loss = flat[-1][0, 0]

    def ordered(small8, big4):
        sg1, sg2, sg3, sg4, swp, ssc, srb, ssk = small8
        swp, srb = swp[None], srb.T
        bwin, bwout, bwg, bwu, bwd = big4[0].T[None], big4[1][None], big4[2].T[None], big4[3].T[None], big4[4][None]
        return [sg1, bwin, swp, ssc, srb, ssk, bwout, sg2, sg3, bwg, bwu, bwd, sg4]

    res = [total_loss, gx[None]]
    for sm, bg in zip(small_out, (big_g, big_d, big_m, big_v)):
        res += ordered(sm, bg)
    return tuple(res)
```

```python
import numpy as np
import jax
import jax.numpy as jnp
from jax import lax
from jax.experimental import pallas as pl
from jax.experimental.pallas import tpu as pltpu

F32 = jnp.float32
BF16 = jnp.bfloat16

D = 1024
POOL_W = 512
GROUP = 128
WINDOWS = (2, 4, 8, 16)
HALO = 128
NQ = 8
BLK = 128
NBUCKET = 32
IN_W = 1280
DFF = 2816
NSH = 4
FS = DFF // NSH
WIN_S = IN_W // NSH
WOUT_S = D // NSH
EPS = 1e-6
NEG = -1e30
SCALE = 0.125

ADAM_LR = 0.001
ADAM_B1 = 0.9
ADAM_B2 = 0.999
ADAM_EPS = 1e-08
ADAM_WD = 0.01
ADAM_STEP = 10

TM = 1024
TM_POOL = 512
TM_FFN = 512
TM_FFN_FWD = 1024
FFN_ROWS = 256
VMEM_LIMIT = 60 * 1024 * 1024

MESH_T = pl.DeviceIdType.MESH
ANY = pl.BlockSpec(memory_space=pl.ANY)


def _cp(n_grid=0, **kw):
    sem = ("arbitrary",) * n_grid if n_grid else None
    return pltpu.CompilerParams(dimension_semantics=sem, vmem_limit_bytes=VMEM_LIMIT, **kw)


def _dot(a, b):
    return jnp.dot(a, b, preferred_element_type=F32)


def _dot_nt(a, b):
    return lax.dot_general(a, b, (((1,), (1,)), ((), ())), preferred_element_type=F32)


def _dot_tn(a, b):
    return lax.dot_general(a, b, (((0,), (0,)), ((), ())), preferred_element_type=F32)


def _rms(x):
    r = lax.rsqrt(jnp.mean(x * x, axis=-1, keepdims=True) + EPS)
    return r, x * r


def _rms_bwd(r, n, g, dout):
    dn = dout * g
    dx = r * (dn - n * jnp.mean(dn * n, axis=-1, keepdims=True))
    dg = jnp.sum(dout * n, axis=0, keepdims=True)
    return dx, dg


def _split(x, n):
    parts = []
    r = x
    for _ in range(n):
        p = r.astype(BF16)
        parts.append(p)
        r = r - p.astype(F32)
    return parts


def _band_dot(a, x, n):
    acc = None
    for p in _split(x, n):
        t = _dot(a, p)
        acc = t if acc is None else acc + t
    return acc


def _bucket_table():
    qi = np.arange(BLK)[None, :]
    kj = np.arange(2 * BLK)[:, None]
    dist = qi + BLK - kj
    n = np.maximum(dist, 0)
    nf = np.maximum(n, 1).astype(np.float32)
    large = 16 + (np.log(nf / np.float32(16)) / np.float32(np.log(128 / 16)) * np.float32(16)).astype(np.int32)
    large = np.minimum(large, NBUCKET - 1)
    return np.where(n < 16, n, large).astype(np.int32)


def _prenorm(x, g1):
    s = x.shape[0]
    tm = min(TM, s)

    def body(x_ref, g_ref, h_ref):
        _, n = _rms(x_ref[...])
        h_ref[...] = (n * g_ref[...]).astype(BF16)

    row = pl.BlockSpec((tm, D), lambda i: (i, 0))
    return pl.pallas_call(
        body, name="prenorm", grid=(s // tm,),
        in_specs=[row, pl.BlockSpec((1, D), lambda i: (0, 0))], out_specs=row,
        out_shape=jax.ShapeDtypeStruct((s, D), BF16),
        compiler_params=_cp(1))(x, g1)


def _inproj_fwd(h1, w_in):
    s = h1.shape[0]
    tm = min(TM, s)

    def body(h_ref, w_ref, u_ref, q_ref, k_ref, v_ref):
        proj = _dot_nt(h_ref[...], w_ref[...])
        u_ref[...] = proj[:, :512]
        q_ref[...] = proj[:, 512:1024].astype(BF16)
        k_ref[...] = proj[:, 1024:1152].astype(BF16)
        v_ref[...] = proj[:, 1152:1280].astype(BF16)

    row = lambda w: pl.BlockSpec((tm, w), lambda i: (i, 0))
    return pl.pallas_call(
        body, name="inproj_fwd", grid=(s // tm,),
        in_specs=[row(D), pl.BlockSpec((IN_W, D), lambda i: (0, 0))],
        out_specs=[row(512), row(512), row(128), row(128)],
        out_shape=[jax.ShapeDtypeStruct((s, 512), F32), jax.ShapeDtypeStruct((s, 512), BF16),
                   jax.ShapeDtypeStruct((s, 128), BF16), jax.ShapeDtypeStruct((s, 128), BF16)],
        compiler_params=_cp(1))(h1, w_in)


def _window_sums(ext, w, lag_sign, tm, terms):
    rows = lax.broadcasted_iota(jnp.int32, (HALO, 2 * HALO), 0)
    cols = lax.broadcasted_iota(jnp.int32, (HALO, 2 * HALO), 1)
    d = rows + HALO - cols if lag_sign > 0 else cols - rows
    band = jnp.where((d >= 0) & (d < w), 1.0, 0.0).astype(BF16)
    return jnp.concatenate([_band_dot(band, ext[r:r + 2 * HALO], terms) for r in range(0, tm, HALO)], axis=0)


def _pooled(ext, cur, t0, tm):
    t = t0 + lax.broadcasted_iota(jnp.int32, (tm, GROUP), 0)
    out = []
    for g, w in enumerate(WINDOWS):
        sl = slice(g * GROUP, (g + 1) * GROUP)
        cnt = jnp.minimum(t + 1, w).astype(F32)
        out.append(_window_sums(ext[:, sl], w, 1, tm, 2) / cnt - cur[:, sl])
    return out


def _pool_fwd(u, w_pool, pool_scale):
    s = u.shape[0]
    tm = min(TM_POOL, s)
    hb = tm // HALO

    def body(uc_ref, uh_ref, wp_ref, sc_ref, o_ref, pooled_ref):
        i = pl.program_id(0)
        cur = uc_ref[...]
        halo = jnp.where(i > 0, uh_ref[...], 0.0)
        ext = jnp.concatenate([halo, cur], axis=0)
        pooled = _pooled(ext, cur, i * tm, tm)
        for g in range(4):
            sl = slice(g * GROUP, (g + 1) * GROUP)
            pb = pooled[g].astype(BF16)
            pooled_ref[:, sl] = pb
            o_ref[:, sl] = (_dot(pb, wp_ref[g]) * sc_ref[:, sl]).astype(BF16)

    row = pl.BlockSpec((tm, 512), lambda i: (i, 0))
    return pl.pallas_call(
        body, name="pool_fwd", grid=(s // tm,),
        in_specs=[row, pl.BlockSpec((HALO, 512), lambda i: (jnp.maximum(i * hb - 1, 0), 0)),
                  pl.BlockSpec((4, GROUP, GROUP), lambda i: (0, 0, 0)),
                  pl.BlockSpec((1, 512), lambda i: (0, 0))],
        out_specs=[row, row],
        out_shape=[jax.ShapeDtypeStruct((s, 512), BF16)] * 2,
        compiler_params=_cp(1))(u, u, w_pool, pool_scale)


def _bias_table(bucket, rel_bias):
    def body(b_ref, rb_ref, o_ref):
        bucket_v = b_ref[...]
        key = lax.broadcasted_iota(jnp.int32, (2 * BLK, BLK), 0)
        dist = lax.broadcasted_iota(jnp.int32, (2 * BLK, BLK), 1) + BLK - key
        inwin = (dist >= 0) & (dist < BLK)
        for h in range(NQ):
            acc = jnp.zeros((2 * BLK, BLK), F32)
            for b in range(NBUCKET):
                acc = jnp.where(bucket_v == b, rb_ref[b, h], acc)
            rest = jnp.where(inwin, acc, NEG)
            o_ref[1, h] = rest
            o_ref[0, h] = jnp.where(key >= BLK, rest, NEG)

    return pl.pallas_call(
        body, name="bias_table",
        in_specs=[pl.BlockSpec(memory_space=pltpu.VMEM), pl.BlockSpec(memory_space=pltpu.SMEM)],
        out_specs=pl.BlockSpec(memory_space=pltpu.VMEM),
        out_shape=jax.ShapeDtypeStruct((2, NQ, 2 * BLK, BLK), F32),
        compiler_params=_cp())(bucket, rel_bias)


HD = 64


def _kv_band(ref, n, transposed):
    pstart = pl.multiple_of(jnp.maximum(n - 1, 0) * BLK, BLK)
    cstart = pl.multiple_of(n * BLK, BLK)
    lo = lax.broadcasted_iota(jnp.int32, (2 * BLK, 128), 1) < HD
    band = jnp.concatenate([ref[pl.ds(pstart, BLK), :], ref[pl.ds(cstart, BLK), :]], axis=0).astype(F32)
    rot = pltpu.roll(band, HD, axis=1)
    halves = ((jnp.where(lo, band, 0.0), jnp.where(lo, 0.0, rot)), (jnp.where(lo, rot, 0.0), jnp.where(lo, 0.0, band)))
    if transposed:
        out = [jnp.concatenate([a.T, b.T], axis=1).astype(BF16) for a, b in halves]
    else:
        out = [jnp.concatenate([a, b], axis=0).astype(BF16) for a, b in halves]
    return out, (lo, pstart, cstart)


def _pair_probs(qt, kk, bias, sk_ref, p):
    sink = jnp.concatenate([jnp.full((1, 1, BLK), sk_ref[0, 2 * p + e], F32) for e in range(2)], axis=0)
    s = _dot_nt(kk, qt).reshape(2, 2 * BLK, BLK) + bias
    m = jnp.maximum(jnp.max(s, axis=1, keepdims=True), sink)
    pr = jnp.exp(s - m)
    es = jnp.exp(sink - m)
    inv = 1.0 / (jnp.sum(pr, axis=1, keepdims=True) + es)
    return pr * inv, es * inv


def _attn_fwd(q, k, v, bias, sinks):
    s = q.shape[0]
    nblk = s // BLK

    def body(q_ref, k_ref, v_ref, b_ref, sk_ref, o_ref, prob_ref, ps_ref):
        n = pl.program_id(0)
        kk, _ = _kv_band(k_ref, n, False)
        vt, _ = _kv_band(v_ref, n, True)
        bidx = jnp.minimum(n, 1)
        for p in range(4):
            h = p // 2
            qt = (q_ref[:, p * 128:(p + 1) * 128].astype(F32) * SCALE).astype(BF16)
            prob, ps = _pair_probs(qt, kk[h], b_ref[bidx, pl.ds(2 * p, 2)], sk_ref, p)
            pb = prob.astype(BF16)
            prob_ref[pl.ds(2 * p, 2)] = pb
            ps_ref[pl.ds(2 * p, 2), :] = ps.reshape(2, BLK)
            acc_t = _dot(vt[h], pb.reshape(4 * BLK, BLK))
            o_ref[:, p * 128:(p + 1) * 128] = acc_t.T.astype(BF16)

    return pl.pallas_call(
        body, name="attn_fwd", grid=(nblk,),
        in_specs=[pl.BlockSpec((BLK, 512), lambda i: (i, 0)),
                  pl.BlockSpec((s, 128), lambda i: (0, 0)),
                  pl.BlockSpec((s, 128), lambda i: (0, 0)),
                  pl.BlockSpec((2, NQ, 2 * BLK, BLK), lambda i: (0, 0, 0, 0)),
                  pl.BlockSpec(memory_space=pltpu.SMEM)],
        out_specs=[pl.BlockSpec((BLK, 512), lambda i: (i, 0)),
                   pl.BlockSpec((None, NQ, 2 * BLK, BLK), lambda i: (i, 0, 0, 0)),
                   pl.BlockSpec((None, NQ, BLK), lambda i: (i, 0, 0))],
        out_shape=[jax.ShapeDtypeStruct((s, 512), BF16), jax.ShapeDtypeStruct((nblk, NQ, 2 * BLK, BLK), BF16),
                   jax.ShapeDtypeStruct((nblk, NQ, BLK), F32)],
        compiler_params=_cp(1))(q, k, v, bias, sinks)


def _outproj_fwd(pool_o, attn_o, w_out, x, g2, g3):
    s = x.shape[0]
    tm = min(TM, s)

    def body(p_ref, a_ref, w_ref, x_ref, g2_ref, g3_ref, mix_ref, x1_ref, h2_ref):
        mix = _dot(p_ref[...], w_ref[0:512, :]) + _dot(a_ref[...], w_ref[512:1024, :])
        mix_ref[...] = mix
        _, n2 = _rms(mix)
        x1 = x_ref[...] + n2 * g2_ref[...]
        x1_ref[...] = x1
        _, n3 = _rms(x1)
        h2_ref[...] = (n3 * g3_ref[...]).astype(BF16)

    row = lambda w: pl.BlockSpec((tm, w), lambda i: (i, 0))
    vec = pl.BlockSpec((1, D), lambda i: (0, 0))
    return pl.pallas_call(
        body, name="outproj_fwd", grid=(s // tm,),
        in_specs=[row(512), row(512), pl.BlockSpec((D, D), lambda i: (0, 0)), row(D), vec, vec],
        out_specs=[row(D), row(D), row(D)],
        out_shape=[jax.ShapeDtypeStruct((s, D), F32), jax.ShapeDtypeStruct((s, D), F32),
                   jax.ShapeDtypeStruct((s, D), BF16)],
        compiler_params=_cp(1))(pool_o, attn_o, w_out, x, g2, g3)


def _silu_parts(g):
    sg = jax.nn.sigmoid(g)
    return sg, g * sg


def _ffn_fwd(h2, wg, wu, wd, x1, target, g4):
    s = h2.shape[0]
    tm = min(TM_FFN_FWD, s)

    def body(h_ref, wg_ref, wu_ref, wd_ref, x1_ref, t_ref, g4_ref,
             gate_ref, up_ref, a_ref, df_ref, dy_ref, loss_ref, gg4_ref, f_acc):
        i = pl.program_id(0)
        j = pl.program_id(1)
        first = j == 0
        chunks = [pl.ds(r, min(FFN_ROWS, tm)) for r in range(0, tm, FFN_ROWS)]
        project = lambda rows: (_dot_nt(h_ref[rows, :], wg_ref[...]), _dot_nt(h_ref[rows, :], wu_ref[...]))
        ahead = [project(chunks[0])]
        for k, rows in enumerate(chunks):
            if k + 1 < len(chunks):
                ahead.append(project(chunks[k + 1]))
            gate, up = ahead[k]
            gate_ref[rows, :] = gate.astype(BF16)
            up_ref[rows, :] = up.astype(BF16)
            _, sl = _silu_parts(gate)
            a = (sl * up).astype(BF16)
            a_ref[rows, :] = a
            contrib = _dot(a, wd_ref[...])
            f_acc[rows, :] = jnp.where(first, contrib, f_acc[rows, :] + contrib)

        @pl.when((i == 0) & (j == 0))
        def _():
            loss_ref[...] = jnp.zeros_like(loss_ref)
            gg4_ref[...] = jnp.zeros_like(gg4_ref)

        @pl.when(j == NSH - 1)
        def _():
            r4, n4 = _rms(f_acc[...])
            g4v = g4_ref[...]
            err = x1_ref[...] + n4 * g4v - t_ref[...]
            loss_ref[...] += (0.5 / D) * jnp.sum(err * err).reshape(1, 1)
            dy = err * (1.0 / D)
            dy_ref[...] = dy
            df, dg = _rms_bwd(r4, n4, g4v, dy)
            gg4_ref[...] += dg
            df_ref[...] = df.astype(BF16)

    row = lambda w: pl.BlockSpec((tm, w), lambda i, j: (i, 0))
    sh = lambda r, c: pl.BlockSpec((None, r, c), lambda i, j: (j, 0, 0))
    act = pl.BlockSpec((None, tm, FS), lambda i, j: (j, i, 0))
    vec = pl.BlockSpec((1, D), lambda i, j: (0, 0))
    return pl.pallas_call(
        body, name="ffn_fwd", grid=(s // tm, NSH),
        in_specs=[row(D), sh(FS, D), sh(FS, D), sh(FS, D), row(D), row(D), vec],
        out_specs=[act, act, act, row(D), row(D), pl.BlockSpec((1, 1), lambda i, j: (0, 0)), vec],
        out_shape=[jax.ShapeDtypeStruct((NSH, s, FS), BF16)] * 3
        + [jax.ShapeDtypeStruct((s, D), BF16), jax.ShapeDtypeStruct((s, D), F32),
           jax.ShapeDtypeStruct((1, 1), F32), jax.ShapeDtypeStruct((1, D), F32)],
        scratch_shapes=[pltpu.VMEM((tm, D), F32)],
        compiler_params=_cp(2))(h2, wg, wu, wd, x1, target, g4)


def _ffn_bwd_act(df, gate, up, wg, wu, wd, dy, x1, mix, g3, g2):
    s = df.shape[0]
    tm = min(TM_FFN, s)

    def body(df_ref, gate_ref, up_ref, wg_ref, wu_ref, wd_ref, dy_ref, x1_ref, mix_ref, g3_ref, g2_ref,
             dgate_ref, dup_ref, dx1_ref, dmix_ref, gg3_ref, gg2_ref, acc):
        j = pl.program_id(0)
        i = pl.program_id(1)
        first = j == 0
        tile = pl.multiple_of(i * tm, tm)
        chunks = [pl.ds(r, min(FFN_ROWS, tm)) for r in range(0, tm, FFN_ROWS)]
        das = [_dot_nt(df_ref[chunks[0], :], wd_ref[...])]
        for k, rows in enumerate(chunks):
            if k + 1 < len(chunks):
                das.append(_dot_nt(df_ref[chunks[k + 1], :], wd_ref[...]))
            da = das[k]
            g = gate_ref[rows, :].astype(F32)
            u = up_ref[rows, :].astype(F32)
            sg, sl = _silu_parts(g)
            dgate = (da * u * (sg * (1.0 + g * (1.0 - sg)))).astype(BF16)
            dup = (da * sl).astype(BF16)
            dgate_ref[rows, :] = dgate
            dup_ref[rows, :] = dup
            contrib = _dot(dgate, wg_ref[...]) + _dot(dup, wu_ref[...])
            acc_rows = pl.ds(tile + k * FFN_ROWS, rows.size)
            acc[acc_rows, :] = jnp.where(first, contrib, acc[acc_rows, :] + contrib)

        @pl.when((i == 0) & (j == 0))
        def _():
            gg3_ref[...] = jnp.zeros_like(gg3_ref)
            gg2_ref[...] = jnp.zeros_like(gg2_ref)

        @pl.when(j == NSH - 1)
        def _():
            r3, n3 = _rms(x1_ref[...])
            dx1n, dg3 = _rms_bwd(r3, n3, g3_ref[...], acc[pl.ds(tile, tm), :])
            dx1 = dy_ref[...] + dx1n
            dx1_ref[...] = dx1
            gg3_ref[...] += dg3
            r2, n2 = _rms(mix_ref[...])
            dmix, dg2 = _rms_bwd(r2, n2, g2_ref[...], dx1)
            gg2_ref[...] += dg2
            dmix_ref[...] = dmix.astype(BF16)

    row = pl.BlockSpec((tm, D), lambda j, i: (i, 0))
    last = pl.BlockSpec((tm, D), lambda j, i: (i * (j // (NSH - 1)), 0))
    sh = pl.BlockSpec((None, FS, D), lambda j, i: (j, 0, 0))
    act = pl.BlockSpec((None, tm, FS), lambda j, i: (j, i, 0))
    vec = pl.BlockSpec((1, D), lambda j, i: (0, 0))
    return pl.pallas_call(
        body, name="ffn_bwd_act", grid=(NSH, s // tm),
        in_specs=[row, act, act, sh, sh, sh, last, last, last, vec, vec],
        out_specs=[act, act, last, last, vec, vec],
        out_shape=[jax.ShapeDtypeStruct((NSH, s, FS), BF16), jax.ShapeDtypeStruct((NSH, s, FS), BF16),
                   jax.ShapeDtypeStruct((s, D), F32), jax.ShapeDtypeStruct((s, D), BF16),
                   jax.ShapeDtypeStruct((1, D), F32), jax.ShapeDtypeStruct((1, D), F32)],
        scratch_shapes=[pltpu.VMEM((s, D), F32)],
        compiler_params=_cp(2))(df, gate, up, wg, wu, wd, dy, x1, mix, g3, g2)


SMEM_SPEC = pl.BlockSpec(memory_space=pltpu.SMEM)


def _emit_halves(acc_ref, row0, rows, c, own_ref, oth_ref):
    half = rows // 2
    own_ref[...] = acc_ref[pl.ds(row0 + pl.multiple_of(c * half, 8), half), :]
    oth_ref[...] = acc_ref[pl.ds(row0 + pl.multiple_of((1 - c) * half, 8), half), :].astype(BF16)


def _half_shapes(rows):
    return [jax.ShapeDtypeStruct((NSH, rows // 2, D), F32), jax.ShapeDtypeStruct((NSH, rows // 2, D), BF16)]


def _ffn_bwd_w(h2, act_a, dgate, dup, df, c_arr):
    s = h2.shape[0]
    tm = min(TM_FFN_FWD, s)
    nt = s // tm

    def body(c_ref, h_ref, a_ref, dgate_ref, dup_ref, df_ref, *rest):
        outs, accs = rest[:6], rest[6:]
        i = pl.program_id(1)

        @pl.when(i == 0)
        def _():
            for acc in accs:
                acc[...] = jnp.zeros_like(acc)

        h = h_ref[...]
        accs[0][...] += _dot_tn(dgate_ref[...], h)
        accs[1][...] += _dot_tn(dup_ref[...], h)
        accs[2][...] += _dot_tn(a_ref[...], df_ref[...])

        @pl.when(i == nt - 1)
        def _():
            for t, acc in enumerate(accs):
                _emit_halves(acc, 0, FS, c_ref[0], outs[2 * t], outs[2 * t + 1])

    row = lambda w: pl.BlockSpec((tm, w), lambda j, i: (i, 0))
    act = pl.BlockSpec((None, tm, FS), lambda j, i: (j, i, 0))
    half = pl.BlockSpec((None, FS // 2, D), lambda j, i: (j, 0, 0))
    return pl.pallas_call(
        body, name="ffn_bwd_w", grid=(NSH, nt),
        in_specs=[SMEM_SPEC, row(D), act, act, act, row(D)],
        out_specs=[half] * 6,
        out_shape=_half_shapes(FS) * 3,
        scratch_shapes=[pltpu.VMEM((FS, D), F32)] * 3,
        compiler_params=_cp(2))(c_arr, h2, act_a, dgate, dup, df)


def _outproj_bwd(dmix, w_out, pool_o, attn_o, c_arr, dep=None):
    s = dmix.shape[0]
    tm = min(TM, s)
    nt = s // tm

    def body(c_ref, dm_ref, w_ref, p_ref, a_ref, *rest):
        dpool_ref, dattn_ref, own_ref, oth_ref, gw_ref = rest[-5:]
        i = pl.program_id(0)

        @pl.when(i == 0)
        def _():
            gw_ref[...] = jnp.zeros_like(gw_ref)

        dm = dm_ref[...]
        dpool_ref[...] = _dot_nt(dm, w_ref[0:512, :])
        dattn_ref[...] = _dot_nt(dm, w_ref[512:1024, :]).astype(BF16)
        gw_ref[0:512, :] += _dot_tn(p_ref[...], dm)
        gw_ref[512:1024, :] += _dot_tn(a_ref[...], dm)

        @pl.when(i == nt - 1)
        def _():
            for k in range(NSH):
                _emit_halves(gw_ref, k * WOUT_S, WOUT_S, c_ref[0], own_ref.at[k], oth_ref.at[k])

    row = lambda w: pl.BlockSpec((tm, w), lambda i: (i, 0))
    full = pl.BlockSpec((D, D), lambda i: (0, 0))
    half = pl.BlockSpec((NSH, WOUT_S // 2, D), lambda i: (0, 0, 0))
    return pl.pallas_call(
        body, name="outproj_bwd", grid=(nt,),
        in_specs=[SMEM_SPEC, row(D), full, row(512), row(512)] + ([] if dep is None else [ANY]),
        out_specs=[row(512), row(512), half, half],
        out_shape=[jax.ShapeDtypeStruct((s, 512), F32), jax.ShapeDtypeStruct((s, 512), BF16)] + _half_shapes(WOUT_S),
        scratch_shapes=[pltpu.VMEM((D, D), F32)],
        compiler_params=_cp(1))(c_arr, dmix, w_out, pool_o, attn_o, *([] if dep is None else [dep]))


def _pool_bwd(pooled, dpool, w_pool, pool_scale, dep=None):
    s = pooled.shape[0]
    tm = min(TM_POOL, s)
    hb = tm // HALO
    nt = s // tm
    last_halo = s // HALO - 1

    def body(p_ref, dc_ref, dn_ref, wp_ref, sc_ref, *rest):
        du_ref, gwp_ref, gsc_ref = rest[-3:]
        i = pl.program_id(0)

        @pl.when(i == 0)
        def _():
            gwp_ref[...] = jnp.zeros_like(gwp_ref)
            gsc_ref[...] = jnp.zeros_like(gsc_ref)

        dcur = dc_ref[...]
        dnext = jnp.where(i < nt - 1, dn_ref[...], 0.0)
        dext = jnp.concatenate([dcur, dnext], axis=0)
        t_ext = i * tm + lax.broadcasted_iota(jnp.int32, (tm + HALO, GROUP), 0)
        for g, w in enumerate(WINDOWS):
            sl = slice(g * GROUP, (g + 1) * GROUP)
            pb = p_ref[:, sl]
            wp = wp_ref[g]
            mixed = _dot(pb, wp)
            gsc_ref[:, sl] += jnp.sum(dcur[:, sl] * mixed, axis=0, keepdims=True)
            dmixed = (dext[:, sl] * sc_ref[:, sl]).astype(BF16)
            gwp_ref[g] += _dot_tn(pb, dmixed[0:tm])
            dpooled = _dot_nt(dmixed, wp)
            z = dpooled / jnp.minimum(t_ext + 1, w).astype(F32)
            du_ref[:, sl] = (_window_sums(z, w, -1, tm, 2) - dpooled[0:tm]).astype(BF16)

    return pl.pallas_call(
        body, name="pool_bwd", grid=(nt,),
        in_specs=[pl.BlockSpec((tm, 512), lambda i: (i, 0)),
                  pl.BlockSpec((tm, 512), lambda i: (i, 0)),
                  pl.BlockSpec((HALO, 512), lambda i: (jnp.minimum((i + 1) * hb, last_halo), 0)),
                  pl.BlockSpec((4, GROUP, GROUP), lambda i: (0, 0, 0)),
                  pl.BlockSpec((1, 512), lambda i: (0, 0))] + ([] if dep is None else [ANY]),
        out_specs=[pl.BlockSpec((tm, 512), lambda i: (i, 0)),
                   pl.BlockSpec((4, GROUP, GROUP), lambda i: (0, 0, 0)),
                   pl.BlockSpec((1, 512), lambda i: (0, 0))],
        out_shape=[jax.ShapeDtypeStruct((s, 512), BF16), jax.ShapeDtypeStruct((4, GROUP, GROUP), F32),
                   jax.ShapeDtypeStruct((1, 512), F32)],
        compiler_params=_cp(1))(pooled, dpool, dpool, w_pool, pool_scale, *([] if dep is None else [dep]))


def _attn_bwd(q, k, v, do, probs, sink_share, bucket, dep=None):
    s = q.shape[0]
    nblk = s // BLK

    def body(q_ref, do_ref, k_ref, v_ref, prob_ref, ps_ref, bk_ref, *rest):
        dq_ref, dk_ref, dv_ref, grb_ref, gsk_ref, dss_ref = rest[-6:]
        n = pl.program_id(0)

        @pl.when(n == 0)
        def _():
            dk_ref[...] = jnp.zeros_like(dk_ref)
            dv_ref[...] = jnp.zeros_like(dv_ref)
            dss_ref[...] = jnp.zeros_like(dss_ref)
            gsk_ref[...] = jnp.zeros_like(gsk_ref)

        kt, (lo, pstart, cstart) = _kv_band(k_ref, n, True)
        vv, _ = _kv_band(v_ref, n, False)
        lo_q = lax.broadcasted_iota(jnp.int32, (BLK, 128), 1) < HD
        dkk = [jnp.zeros((2 * BLK, 128), F32), jnp.zeros((2 * BLK, 128), F32)]
        dvv = [jnp.zeros((2 * BLK, 128), F32), jnp.zeros((2 * BLK, 128), F32)]

        def by_head(t):
            return jnp.concatenate([jnp.where(lo_q, t, 0.0), jnp.where(lo_q, 0.0, t)], axis=0).astype(BF16)

        for p in range(4):
            h = p // 2
            qt = q_ref[:, p * 128:(p + 1) * 128].astype(F32) * SCALE
            dot = do_ref[:, p * 128:(p + 1) * 128]
            pb = prob_ref[pl.ds(2 * p, 2)]
            prob = pb.astype(F32)
            ps = ps_ref[pl.ds(2 * p, 2), :].reshape(2, 1, BLK)
            dp = _dot_nt(vv[h], dot).reshape(2, 2 * BLK, BLK)
            delta = jnp.sum(prob * dp, axis=1, keepdims=True)
            ds = prob * (dp - delta)
            sink_part = ps * delta
            for e in range(2):
                gs = -jnp.sum(sink_part[e]).reshape(1, 1)
                gsk_ref[pl.ds(2 * p + e, 1), :] += jnp.broadcast_to(gs, (1, 128))
            dss_ref[pl.ds(2 * p, 2)] += ds
            dsb = ds.astype(BF16)
            dq_t = _dot(kt[h], dsb.reshape(4 * BLK, BLK))
            dq_ref[:, p * 128:(p + 1) * 128] = (dq_t.T * SCALE).astype(BF16)
            dkk[h] = dkk[h] + _dot(jnp.concatenate([dsb[0], dsb[1]], axis=1), by_head(qt))
            dvv[h] = dvv[h] + _dot(jnp.concatenate([pb[0], pb[1]], axis=1), by_head(dot.astype(F32)))
        for acc, ref in ((dkk, dk_ref), (dvv, dv_ref)):
            f0 = acc[0] + pltpu.roll(acc[0], HD, axis=1)
            f1 = acc[1] + pltpu.roll(acc[1], HD, axis=1)
            band = jnp.where(lo, f0, f1)
            ref[pl.ds(pstart, BLK), :] += band[0:BLK]
            ref[pl.ds(cstart, BLK), :] += band[BLK:2 * BLK]

        @pl.when(n == nblk - 1)
        def _():
            _relbias_grad(dss_ref, bk_ref[...], grb_ref)

    blk = pl.BlockSpec((BLK, 512), lambda i: (i, 0))
    kv = pl.BlockSpec((s, 128), lambda i: (0, 0))
    row8 = pl.BlockSpec((NQ, 128), lambda i: (0, 0))
    return pl.pallas_call(
        body, name="attn_bwd", grid=(nblk,),
        in_specs=[blk, blk, kv, kv, pl.BlockSpec((None, NQ, 2 * BLK, BLK), lambda i: (i, 0, 0, 0)),
                  pl.BlockSpec((None, NQ, BLK), lambda i: (i, 0, 0)), pl.BlockSpec((2 * BLK, BLK), lambda i: (0, 0))]
        + ([] if dep is None else [ANY]),
        out_specs=[blk, kv, kv, row8, row8],
        out_shape=[jax.ShapeDtypeStruct((s, 512), BF16), jax.ShapeDtypeStruct((s, 128), F32),
                   jax.ShapeDtypeStruct((s, 128), F32), jax.ShapeDtypeStruct((NQ, 128), F32),
                   jax.ShapeDtypeStruct((NQ, 128), F32)],
        scratch_shapes=[pltpu.VMEM((NQ, 2 * BLK, BLK), F32)],
        compiler_params=_cp(1))(q, do, k, v, probs, sink_share, bucket, *([] if dep is None else [dep]))


def _relbias_grad(ds_ref, bucket_v, o_ref):
    lane = lax.broadcasted_iota(jnp.int32, (NQ, 128), 1)
    head_row = lax.broadcasted_iota(jnp.int32, (NQ, BLK), 0)
    acc = jnp.zeros((NQ, 128), F32)
    for b in range(NBUCKET):
        sel = bucket_v == b
        stack = jnp.zeros((NQ, BLK), F32)
        for h in range(NQ):
            col_sums = jnp.sum(jnp.where(sel, ds_ref[h], 0.0), axis=0, keepdims=True)
            stack = jnp.where(head_row == h, col_sums, stack)
        acc = acc + jnp.where(lane == b, jnp.sum(stack, axis=1, keepdims=True), 0.0)
    o_ref[...] = acc


def _inproj_bwd(x, g1, du, dq, dk, dv, w_in, dx1, c_arr):
    s = x.shape[0]
    tm = min(TM, s)
    nt = s // tm
    cols = ((0, 512), (512, 1024), (1024, 1152), (1152, 1280))

    def body(c_ref, x_ref, g_ref, du_ref, dq_ref, dk_ref, dv_ref, w_ref, dx1_ref,
             gx_ref, own_ref, oth_ref, gg_ref, gw_ref):
        i = pl.program_id(0)

        @pl.when(i == 0)
        def _():
            gw_ref[...] = jnp.zeros_like(gw_ref)
            gg_ref[...] = jnp.zeros_like(gg_ref)

        parts = (du_ref[...], dq_ref[...], dk_ref[...].astype(BF16), dv_ref[...].astype(BF16))
        dh = None
        for (a, b), dpart in zip(cols, parts):
            t = _dot(dpart, w_ref[a:b, :])
            dh = t if dh is None else dh + t
        gv = g_ref[...]
        r1, n1 = _rms(x_ref[...])
        h = (n1 * gv).astype(BF16)
        for (a, b), dpart in zip(cols, parts):
            gw_ref[a:b, :] += _dot_tn(dpart, h)
        dx, dg = _rms_bwd(r1, n1, gv, dh)
        gg_ref[...] += dg
        gx_ref[...] = dx1_ref[...] + dx

        @pl.when(i == nt - 1)
        def _():
            for k in range(NSH):
                _emit_halves(gw_ref, k * WIN_S, WIN_S, c_ref[0], own_ref.at[k], oth_ref.at[k])

    row = lambda w: pl.BlockSpec((tm, w), lambda i: (i, 0))
    vec = pl.BlockSpec((1, D), lambda i: (0, 0))
    full = pl.BlockSpec((IN_W, D), lambda i: (0, 0))
    half = pl.BlockSpec((NSH, WIN_S // 2, D), lambda i: (0, 0, 0))
    return pl.pallas_call(
        body, name="inproj_bwd", grid=(nt,),
        in_specs=[SMEM_SPEC, row(D), vec, row(512), row(512), row(128), row(128), full, row(D)],
        out_specs=[row(D), half, half, vec],
        out_shape=[jax.ShapeDtypeStruct((s, D), F32)] + _half_shapes(WIN_S) + [jax.ShapeDtypeStruct((1, D), F32)],
        scratch_shapes=[pltpu.VMEM((IN_W, D), F32)],
        compiler_params=_cp(1))(c_arr, x, g1, du, dq, dk, dv, w_in, dx1)


def _local_step(x, target, small, w_in, w_out, ffn_weights, c_arr, on_ffn_grads=None, on_wout_grads=None):
    g1, g2, g3, g4, w_pool, pool_scale, rel_bias, sinks = small
    bucket = jnp.asarray(_bucket_table())
    wp_bf = w_pool.astype(BF16)
    bias = _bias_table(bucket, rel_bias)
    h1 = _prenorm(x, g1)
    if callable(w_in):
        w_in = w_in(bias, h1)
    u, q, k, v = _inproj_fwd(h1, w_in)
    pool_o, pooled = _pool_fwd(u, wp_bf, pool_scale)
    attn_o, probs, sink_share = _attn_fwd(q, k, v, bias, sinks)
    g2_fwd = g2
    if callable(w_out):
        w_out, after = w_out(pool_o, attn_o)
        if after is not None:
            g2_fwd = g2 + after[:1, :1]
    mix, x1, h2 = _outproj_fwd(pool_o, attn_o, w_out, x, g2_fwd, g3)
    wg, wu, wd = ffn_weights(h2) if callable(ffn_weights) else ffn_weights
    gate, up, act_a, df, dy, loss, gg4 = _ffn_fwd(h2, wg, wu, wd, x1, target, g4)
    dgate, dup, dx1, dmix, gg3, gg2 = _ffn_bwd_act(df, gate, up, wg, wu, wd, dy, x1, mix, g3, g2)
    ffn_g = _ffn_bwd_w(h2, act_a, dgate, dup, df, c_arr)
    dep = None if on_ffn_grads is None else on_ffn_grads(ffn_g)
    dpool, dattn, wout_own, wout_oth = _outproj_bwd(dmix, w_out, pool_o, attn_o, c_arr, dep)
    dep = None if on_wout_grads is None else on_wout_grads(wout_own, wout_oth, dpool)
    du, gwp, gsc = _pool_bwd(pooled, dpool, wp_bf, pool_scale, dep)
    dq, dk, dv, grb, gsk = _attn_bwd(q, k, v, dattn, probs, sink_share, bucket, dep)
    gx, win_own, win_oth, gg1 = _inproj_bwd(x, g1, du, dq, dk, dv, w_in, dx1, c_arr)
    small_grads = (gg1, gg2, gg3, gg4, gwp, gsc, grb, gsk[:, 0].reshape(1, NQ))
    return loss, gx, small_grads, ((win_own, win_oth), (wout_own, wout_oth), ffn_g)


def _place():
    x, y, c = lax.axis_index("x"), lax.axis_index("y"), lax.axis_index("c")
    chips = ((1 - x, y), (x, 1 - y), (1 - x, 1 - y))
    return x, y, c, chips


def _remote(src, dst, ssem, rsem, dev):
    return pltpu.make_async_remote_copy(src_ref=src, dst_ref=dst, send_sem=ssem, recv_sem=rsem,
                                        device_id=dev, device_id_type=MESH_T)


def _to_sibling(arrs, name, after=()):
    nt, na = len(arrs), len(after)

    def body(*refs):
        srcs, dsts = refs[:nt], refs[nt + na:2 * nt + na]
        ssem, rsem = refs[2 * nt + na:]
        x, y, c, _ = _place()
        cps = [_remote(srcs[t], dsts[t], ssem.at[t], rsem.at[t], (x, y, 1 - c)) for t in range(nt)]
        for cp in cps:
            cp.start()
        for cp in cps:
            cp.wait()

    return pl.pallas_call(
        body, name=name, in_specs=[ANY] * (nt + na), out_specs=[ANY] * nt,
        out_shape=[jax.ShapeDtypeStruct(a.shape, a.dtype) for a in arrs],
        scratch_shapes=[pltpu.SemaphoreType.DMA((nt,)), pltpu.SemaphoreType.DMA((nt,))],
        compiler_params=_cp())(*arrs, *after)


HBM_SPEC = pl.BlockSpec(memory_space=pltpu.HBM)
SEM_SPEC = pl.BlockSpec(memory_space=pltpu.SEMAPHORE)
DATAFLOW = pltpu.SideEffectType.DATAFLOW_SIDE_EFFECTING


def _gather_copies(srcs, lands, ssem, rsem):
    x, y, c, chips = _place()
    me = 2 * x + y
    out = []
    for t in range(len(srcs)):
        half = srcs[t].shape[0] // 2
        mine = pl.ds(pl.multiple_of(c * half, 16), half)
        for j, (px, py) in enumerate(chips):
            k = 3 * t + j
            send = _remote(srcs[t].at[mine], lands[t].at[me, mine], ssem.at[k], rsem.at[k], (px, py, c))
            blk = lands[t].at[2 * px + py, mine]
            out.append((send, _remote(blk, blk, ssem.at[k], rsem.at[k], (px, py, c))))
    return out


def _scatter_copies(srcs, lands, ssem, rsem):
    x, y, c, chips = _place()
    out = []
    for t in range(len(srcs)):
        for j, (px, py) in enumerate(chips):
            k = 3 * t + j
            send = _remote(srcs[t].at[2 * px + py], lands[t].at[j], ssem.at[k], rsem.at[k], (px, py, c))
            out.append((send, _remote(lands[t].at[j], lands[t].at[j], ssem.at[k], rsem.at[k], (px, py, c))))
    return out


def _sibling_copies(srcs, lands, ssem, rsem):
    x, y, c, _ = _place()
    sib = (x, y, 1 - c)
    return [(_remote(srcs[t], lands[t], ssem.at[t], rsem.at[t], sib),
             _remote(lands[t], lands[t], ssem.at[t], rsem.at[t], sib)) for t in range(len(srcs))]


def _peer_copies(srcs, lands, ssem, rsem):
    x, y, c, _ = _place()
    me = 4 * x + 2 * y + c
    out = []
    for kk in range(1, 8):
        px, py, pc = x ^ (kk >> 2), y ^ ((kk >> 1) & 1), c ^ (kk & 1)
        send = _remote(srcs[0], lands[0].at[me], ssem.at[kk - 1], rsem.at[kk - 1], (px, py, pc))
        slot = lands[0].at[4 * px + 2 * py + pc]
        out.append((send, _remote(slot, slot, ssem.at[kk - 1], rsem.at[kk - 1], (px, py, pc))))
    return out


def _split_start(plan, ncopy, srcs, lands, after, name):
    ns, nbuf = len(srcs), len(srcs) + len(lands)
    extra = [] if after is None else [after]
    n_in = nbuf + len(extra)

    def body(*refs):
        ssem, rsem, token = refs[n_in], refs[n_in + 1], refs[-1]
        for send, _ in plan(refs[:ns], refs[ns:nbuf], ssem, rsem):
            send.start()
        token[...] = jnp.zeros_like(token)

    bufs = [pltpu.with_memory_space_constraint(a, pltpu.HBM) for a in list(srcs) + list(lands)]
    outs = pl.pallas_call(
        body, name=name, in_specs=[HBM_SPEC] * nbuf + [ANY] * len(extra),
        out_specs=[SEM_SPEC, SEM_SPEC] + [HBM_SPEC] * nbuf + [pl.BlockSpec(memory_space=pltpu.VMEM)],
        out_shape=[pltpu.SemaphoreType.DMA((ncopy,)), pltpu.SemaphoreType.DMA((ncopy,))]
        + [pltpu.HBM(a.shape, a.dtype) for a in bufs] + [jax.ShapeDtypeStruct((8, 128), F32)],
        input_output_aliases={i: 2 + i for i in range(nbuf)},
        compiler_params=pltpu.CompilerParams(has_side_effects=DATAFLOW))(*bufs, *extra)
    return outs[0], outs[1], outs[2:2 + ns], outs[2 + ns:2 + nbuf], outs[-1]


def _split_wait(plan, ssem, rsem, srcs, lands, after, name, only=None):
    ns, nbuf = len(srcs), len(srcs) + len(lands)

    def body(*refs):
        s_ref, r_ref = refs[nbuf], refs[nbuf + 1]
        for k, (send, recv) in enumerate(plan(refs[:ns], refs[ns:nbuf], s_ref, r_ref)):
            if only is None or k in only:
                send.wait_send()
                recv.wait_recv()

    outs = pl.pallas_call(
        body, name=name, in_specs=[HBM_SPEC] * nbuf + [SEM_SPEC, SEM_SPEC] + [ANY] * len(after),
        out_specs=[HBM_SPEC] * nbuf,
        out_shape=[pltpu.HBM(a.shape, a.dtype) for a in list(srcs) + list(lands)],
        input_output_aliases={i: i for i in range(nbuf)},
        compiler_params=pltpu.CompilerParams(has_side_effects=DATAFLOW))(*srcs, *lands, ssem, rsem, *after)
    return outs


def _gather_finish(lands, tag):
    nt = len(lands)

    def body(*refs):
        outs = refs[nt:2 * nt]
        dsend, drecv = refs[2 * nt:]
        x, y, c, chips = _place()
        sib = (x, y, 1 - c)
        sends = []
        for t in range(nt):
            half = outs[t].shape[1] // 2
            mine = pl.ds(pl.multiple_of(c * half, 16), half)
            for j, (px, py) in enumerate(chips):
                blk = outs[t].at[2 * px + py, mine]
                cp = _remote(blk, blk, dsend.at[t, j], drecv.at[t, j], sib)
                cp.start()
                sends.append(cp)
        for t in range(nt):
            half = outs[t].shape[1] // 2
            other = pl.ds(pl.multiple_of((1 - c) * half, 16), half)
            for j, (px, py) in enumerate(chips):
                blk = outs[t].at[2 * px + py, other]
                _remote(blk, blk, dsend.at[t, j], drecv.at[t, j], sib).wait_recv()
        for cp in sends:
            cp.wait_send()

    return pl.pallas_call(
        body, name="gather_finish_" + tag, in_specs=[ANY] * nt, out_specs=[ANY] * nt,
        out_shape=[jax.ShapeDtypeStruct(a.shape, a.dtype) for a in lands],
        input_output_aliases={t: t for t in range(nt)},
        scratch_shapes=[pltpu.SemaphoreType.DMA((nt, 3)), pltpu.SemaphoreType.DMA((nt, 3))],
        compiler_params=_cp())(*lands)


def _chip_partial(owns, recv, chip_arr, tag):
    nt = len(owns)

    def body(chip_ref, *refs):
        k = pl.program_id(0)
        for t in range(nt):
            p = refs[t][...] + refs[nt + t][...].astype(F32)
            refs[2 * nt + t][...] = p.astype(BF16)

            @pl.when(k == chip_ref[0])
            def _():
                refs[3 * nt + t][...] = p

    blk_specs = [pl.BlockSpec((None,) + a.shape[1:], lambda k, chip_ref: (k, 0, 0)) for a in owns]
    own_specs = [pl.BlockSpec(a.shape[1:], lambda k, chip_ref: (0, 0)) for a in owns]
    return pl.pallas_call(
        body, name="rs_chip_partial_" + tag,
        grid_spec=pltpu.PrefetchScalarGridSpec(num_scalar_prefetch=1, grid=(NSH,), in_specs=blk_specs * 2,
                                               out_specs=blk_specs + own_specs),
        out_shape=[jax.ShapeDtypeStruct(a.shape, BF16) for a in owns]
        + [jax.ShapeDtypeStruct(a.shape[1:], F32) for a in owns],
        compiler_params=_cp(1))(chip_arr, *owns, *recv)


def _sum_chips(own, recv, tag, after=()):
    nt = len(own)

    def body(*refs):
        outs = refs[2 * nt + len(after):]
        for t in range(nt):
            r = refs[nt + t]
            outs[t][...] = ((refs[t][...] + r[0].astype(F32)) + r[1].astype(F32)) + r[2].astype(F32)

    vm = pl.BlockSpec(memory_space=pltpu.VMEM)
    return pl.pallas_call(
        body, name="rs_sum_chips_" + tag, in_specs=[vm] * (2 * nt) + [ANY] * len(after), out_specs=[vm] * nt,
        out_shape=[jax.ShapeDtypeStruct(a.shape, F32) for a in own],
        compiler_params=_cp())(*own, *recv, *after)


def _adamw_math(w, g, m, v):
    m = ADAM_B1 * m + (1.0 - ADAM_B1) * g
    v = ADAM_B2 * v + (1.0 - ADAM_B2) * (g * g)
    m_hat = m / (1.0 - ADAM_B1 ** ADAM_STEP)
    v_hat = v / (1.0 - ADAM_B2 ** ADAM_STEP)
    delta = -ADAM_LR * (m_hat / (jnp.sqrt(v_hat) + ADAM_EPS) + ADAM_WD * w)
    return delta, m, v


ADAM_SPLIT = 4


def _adamw_shards(own, sib, ws, ms, vs, c_arr, tag):
    nt = len(own)

    def body(c_ref, *refs):
        hh = pl.program_id(0)
        mine = hh == c_ref[0]
        for t in range(nt):
            g = jnp.where(mine, refs[t][...], refs[nt + t][...])
            delta, m, v = _adamw_math(refs[2 * nt + t][...], g, refs[3 * nt + t][...], refs[4 * nt + t][...])
            refs[5 * nt + t][...] = g
            refs[6 * nt + t][...] = delta
            refs[7 * nt + t][...] = m
            refs[8 * nt + t][...] = v

    def tile(a):
        return (a.shape[0] // ADAM_SPLIT, a.shape[1])

    def half_index(used_when_mine):
        def index(hh, q, c_ref):
            mine = 1 - (hh - c_ref[0]) * (hh - c_ref[0])
            used = mine if used_when_mine else 1 - mine
            return (used * q + (1 - used) * hh * (ADAM_SPLIT - 1), 0)
        return index

    own_specs = [pl.BlockSpec(tile(a), half_index(True)) for a in own]
    sib_specs = [pl.BlockSpec(tile(a), half_index(False)) for a in own]
    full_specs = [pl.BlockSpec(tile(a), lambda hh, q, c_ref: (hh * ADAM_SPLIT + q, 0)) for a in own]
    return pl.pallas_call(
        body, name="adamw_shards_" + tag,
        grid_spec=pltpu.PrefetchScalarGridSpec(num_scalar_prefetch=1, grid=(2, ADAM_SPLIT),
                                               in_specs=own_specs + sib_specs + full_specs * 3,
                                               out_specs=full_specs * 4),
        out_shape=[jax.ShapeDtypeStruct(w.shape, F32) for w in ws] * 4,
        compiler_params=_cp(2))(c_arr, *own, *sib, *ws, *ms, *vs)


SMALL_WPOOL_ROW = 32
SMALL_SCALE_ROW = SMALL_WPOOL_ROW + 4 * GROUP
SMALL_BIAS_ROW = SMALL_SCALE_ROW + 8
SMALL_SINKS_ROW = SMALL_BIAS_ROW + NQ
SMALL_LOSS_ROW = SMALL_SINKS_ROW + 8


def _small_sum_adamw(gathered, wp, mp, vp):
    rows = wp.shape[0]

    def body(g_ref, w_ref, m_ref, v_ref, *rest):
        outs, res = rest[:-1], rest[-1]
        g = g_ref[0]
        for d in range(1, 8):
            g = g + g_ref[d]
        delta, m, v = _adamw_math(w_ref[...], g, m_ref[...], v_ref[...])
        for kind, val in enumerate((g, delta, m, v)):
            res[kind] = val
            gains = outs[8 * kind:8 * kind + 4]
            w_pool_o, scale_o, bias_o, sinks_o = outs[8 * kind + 4:8 * kind + 8]
            for t in range(4):
                for i in range(8):
                    gains[t][:, 128 * i:128 * (i + 1)] = res[kind, pl.ds(8 * t + i, 1), :]
            for grp in range(4):
                w_pool_o[grp] = res[kind, pl.ds(SMALL_WPOOL_ROW + GROUP * grp, GROUP), :]
            for i in range(4):
                scale_o[:, 128 * i:128 * (i + 1)] = res[kind, pl.ds(SMALL_SCALE_ROW + i, 1), :]
            bias_o[...] = res[kind, pl.ds(SMALL_BIAS_ROW, NQ), :][:, 0:NBUCKET]
            sinks_o[...] = res[kind, pl.ds(SMALL_SINKS_ROW, 1), :][:, 0:NQ]
        outs[-1][...] = res[0, pl.ds(SMALL_LOSS_ROW, 1), :]

    vm = pl.BlockSpec(memory_space=pltpu.VMEM)
    one_kind = [jax.ShapeDtypeStruct((1, D), F32)] * 4 + [
        jax.ShapeDtypeStruct((4, GROUP, GROUP), F32), jax.ShapeDtypeStruct((1, POOL_W), F32),
        jax.ShapeDtypeStruct((NQ, NBUCKET), F32), jax.ShapeDtypeStruct((1, NQ), F32)]
    return pl.pallas_call(
        body, name="small_sum_adamw", in_specs=[vm] * 4, out_specs=[vm] * 33,
        out_shape=one_kind * 4 + [jax.ShapeDtypeStruct((1, 128), F32)],
        scratch_shapes=[pltpu.VMEM((4, rows, 128), F32)],
        compiler_params=_cp())(gathered, wp, mp, vp)


def _pack_small(gains4, w_pool, pool_scale, rel_bias_t, sinks, loss):
    bias_rows = jnp.pad(rel_bias_t, ((0, 0), (0, 128 - rel_bias_t.shape[1])))
    parts = list(gains4) + [w_pool, pool_scale, bias_rows, sinks, loss]
    rows = []
    for a in parts:
        flat = a.reshape(-1)
        n = flat.shape[0]
        padded = -(-n // 1024) * 1024
        rows.append(jnp.pad(flat, (0, padded - n)).reshape(-1, 128))
    return jnp.concatenate(rows, axis=0)


def kernel(x, g_pre_mix, w_in, w_pool, pool_scale, rel_bias, sinks, w_out, g_post_mix, g_pre_ffn, w_gate, w_up, w_down, g_post_ffn, loss_target, m_g_pre_mix, m_w_in, m_w_pool, m_pool_scale, m_rel_bias, m_sinks, m_w_out, m_g_post_mix, m_g_pre_ffn, m_w_gate, m_w_up, m_w_down, m_g_post_ffn, v_g_pre_mix, v_w_in, v_w_pool, v_pool_scale, v_rel_bias, v_sinks, v_w_out, v_g_post_mix, v_g_pre_ffn, v_w_gate, v_w_up, v_w_down, v_g_post_ffn):
    c = lax.axis_index("c")
    chip = 2 * lax.axis_index("x") + lax.axis_index("y")

    def shards(a_in, a_out, a_gate, a_up, a_down):
        return (a_in[0].T, a_out[0], a_gate[0].T, a_up[0].T, a_down[0])

    c_arr = c.reshape(1).astype(jnp.int32)
    chip_arr = chip.reshape(1).astype(jnp.int32)

    big_w = shards(w_in, w_out, w_gate, w_up, w_down)
    big_bf = [w.astype(BF16) for w in big_w]
    g_ssem, g_rsem, g_srcs, g_lands, _ = _split_start(
        _gather_copies, 15, big_bf, [lax.empty((NSH,) + a.shape, BF16) for a in big_bf], None, "gather_start")
    fw = {}

    def with_own(land, shard):
        return lax.dynamic_update_slice(land, shard[None], (chip, 0, 0))

    def w_in_ready(*after):
        fw["first"] = _split_wait(_gather_copies, g_ssem, g_rsem, g_srcs, g_lands, after, "gather_win_wait",
                                  only=(0, 1, 2))
        (fw["win"],) = _gather_finish([with_own(fw["first"][5], fw["first"][0])], "win")
        return fw["win"].reshape(IN_W, D)

    def w_out_ready(*after):
        first = fw["first"]
        fw["second"] = _split_wait(_gather_copies, g_ssem, g_rsem, first[:5], [fw["win"]] + list(first[6:]), after,
                                   "gather_wout_wait", only=(3, 4, 5))
        (fw["wout"],) = _gather_finish([with_own(fw["second"][6], fw["second"][1])], "wout")
        return fw["wout"].reshape(D, D), None

    def ffn_weights(after):
        second = fw["second"]
        outs = _split_wait(_gather_copies, g_ssem, g_rsem, second[:5], [second[5], fw["wout"]] + list(second[7:]),
                           [after], "gather_ffn_wait", only=tuple(range(6, 15)))
        return _gather_finish([with_own(land, src) for src, land in zip(outs[2:5], outs[7:])], "ffn")

    rs = {}

    def on_ffn_grads(ffn_g):
        oths = ffn_g[1::2]
        rs["ffn_own"] = ffn_g[0::2]
        rs["x"] = _split_start(_sibling_copies, 3, oths, [lax.empty(a.shape, BF16) for a in oths], ffn_g[0],
                               "rs_exchange_ffn_start")
        return rs["x"][4]

    def on_wout_grads(own, oth, after):
        ssem, rsem, srcs, lands, _ = rs["x"]
        recv_ffn = _split_wait(_sibling_copies, ssem, rsem, srcs, lands, [after], "rs_exchange_ffn_wait")[3:]
        recv_wout = _to_sibling([oth], "rs_exchange_wout")
        outs = _chip_partial([own] + list(rs["ffn_own"]), list(recv_wout) + list(recv_ffn), chip_arr, "early")
        rs["early_own"] = outs[4:]
        rs["s"] = _split_start(_scatter_copies, 12, outs[:4],
                               [lax.empty((3,) + a.shape[1:], BF16) for a in outs[:4]], outs[4], "scatter_early_start")
        return rs["s"][4]

    small = (g_pre_mix, g_post_mix, g_pre_ffn, g_post_ffn, w_pool[0], pool_scale, rel_bias, sinks)
    loss, gx, small_grads, ((win_own, win_oth), _, _) = _local_step(
        x[0], loss_target[0], small, w_in_ready, w_out_ready, ffn_weights, c_arr, on_ffn_grads, on_wout_grads)

    ssem, rsem, srcs, lands, _ = rs["s"]
    recv_early = _split_wait(_scatter_copies, ssem, rsem, srcs, lands, [gx], "scatter_early_wait")[4:]
    finals = _sum_chips(list(rs["early_own"]), list(recv_early), "early")
    sh_ssem, sh_rsem, sh_srcs, sh_lands, sh_token = _split_start(
        _sibling_copies, 4, finals, [lax.empty(a.shape, F32) for a in finals], finals[0], "rs_share_early_start")

    unused = jnp.zeros((1, 1), F32)
    gp = _pack_small(small_grads[:4], *small_grads[4:], loss)
    wp = _pack_small((g_pre_mix, g_post_mix, g_pre_ffn, g_post_ffn), w_pool, pool_scale, rel_bias.T, sinks, unused)
    mp = _pack_small((m_g_pre_mix, m_g_post_mix, m_g_pre_ffn, m_g_post_ffn), m_w_pool, m_pool_scale, m_rel_bias.T,
                     m_sinks, unused)
    vp = _pack_small((v_g_pre_mix, v_g_post_mix, v_g_pre_ffn, v_g_post_ffn), v_w_pool, v_pool_scale, v_rel_bias.T,
                     v_sinks, unused)

    moments = (shards(m_w_in, m_w_out, m_w_gate, m_w_up, m_w_down),
               shards(v_w_in, v_w_out, v_w_gate, v_w_up, v_w_down))
    recv_a = _to_sibling([win_oth], "rs_exchange_win", [sh_token])
    outs = _chip_partial([win_own], recv_a, chip_arr, "win")
    w_ssem, w_rsem, w_srcs, w_lands, w_token = _split_start(
        _scatter_copies, 3, outs[:1], [lax.empty((3,) + outs[0].shape[1:], BF16)], gx, "scatter_win_start")
    slots = lax.dynamic_update_slice(lax.empty((8,) + gp.shape, F32), gp[None], (2 * chip + c, 0, 0))
    p_ssem, p_rsem, p_srcs, p_lands, p_token = _split_start(_peer_copies, 7, [gp], [slots], w_token,
                                                            "small_allgather_start")
    shared = _split_wait(_sibling_copies, sh_ssem, sh_rsem, sh_srcs, sh_lands, [p_token], "rs_share_early_wait")
    adam_e = _adamw_shards(shared[:4], shared[4:], big_w[1:], moments[0][1:], moments[1][1:], c_arr, "early")
    recv_win = _split_wait(_scatter_copies, w_ssem, w_rsem, w_srcs, w_lands, [adam_e[0]], "scatter_win_wait")[1:]
    win_final = _sum_chips([outs[1]], recv_win, "win")
    win_sib = _to_sibling(win_final, "rs_share_win")
    adam_w = _adamw_shards(win_final, win_sib, big_w[:1], moments[0][:1], moments[1][:1], c_arr, "win")
    big_g, big_d, big_m, big_v = [[adam_w[i]] + list(adam_e[4 * i:4 * i + 4]) for i in range(4)]

    gathered = _split_wait(_peer_copies, p_ssem, p_rsem, p_srcs, p_lands, [adam_w[0]], "small_allgather_wait")[1]
    flat = _small_sum_adamw(gathered, wp, mp, vp)
    small_out = [flat[8 * kind:8 * kind + 8] for kind in range(4)]
    total_loss = flat[-1][0, 0]

    def ordered(small8, big4):
        sg1, sg2, sg3, sg4, swp, ssc, srb, ssk = small8
        swp, srb = swp[None], srb.T
        bwin, bwout, bwg, bwu, bwd = big4[0].T[None], big4[1][None], big4[2].T[None], big4[3].T[None], big4[4][None]
        return [sg1, bwin, swp, ssc, srb, ssk, bwout, sg2, sg3, bwg, bwu, bwd, sg4]

    res = [total_loss, gx[None]]
    for sm, bg in zip(small_out, (big_g, big_d, big_m, big_v)):
        res += ordered(sm, bg)
    return tuple(res)
```

```python
import numpy as np
import jax
import jax.numpy as jnp
from jax import lax
from jax.experimental import pallas as pl
from jax.experimental.pallas import tpu as pltpu

F32 = jnp.float32
BF16 = jnp.bfloat16

D = 1024
POOL_W = 512
GROUP = 128
WINDOWS = (2, 4, 8, 16)
HALO = 128
NQ = 8
BLK = 128
NBUCKET = 32
IN_W = 1280
DFF = 2816
NSH = 4
FS = DFF // NSH
WIN_S = IN_W // NSH
WOUT_S = D // NSH
EPS = 1e-6
NEG = -1e30
SCALE = 0.125

ADAM_LR = 0.001
ADAM_B1 = 0.9
ADAM_B2 = 0.999
ADAM_EPS = 1e-08
ADAM_WD = 0.01
ADAM_STEP = 10

TM = 512
TM_IN = 1024
TM_POOL = 512
TM_FFN = 512
TM_FFN_FWD = 1024
FFN_ROWS = 256
VMEM_LIMIT = 60 * 1024 * 1024

MESH_T = pl.DeviceIdType.MESH
ANY = pl.BlockSpec(memory_space=pl.ANY)


def _cp(n_grid=0, **kw):
    sem = ("arbitrary",) * n_grid if n_grid else None
    return pltpu.CompilerParams(dimension_semantics=sem, vmem_limit_bytes=VMEM_LIMIT, **kw)


def _dot(a, b):
    return jnp.dot(a, b, preferred_element_type=F32)


def _dot_nt(a, b):
    return lax.dot_general(a, b, (((1,), (1,)), ((), ())), preferred_element_type=F32)


def _dot_tn(a, b):
    return lax.dot_general(a, b, (((0,), (0,)), ((), ())), preferred_element_type=F32)


def _rms(x):
    r = lax.rsqrt(jnp.mean(x * x, axis=-1, keepdims=True) + EPS)
    return r, x * r


def _rms_bwd(r, n, g, dout):
    dn = dout * g
    dx = r * (dn - n * jnp.mean(dn * n, axis=-1, keepdims=True))
    dg = jnp.sum(dout * n, axis=0, keepdims=True)
    return dx, dg


def _split(x, n):
    parts = []
    r = x
    for _ in range(n):
        p = r.astype(BF16)
        parts.append(p)
        r = r - p.astype(F32)
    return parts


def _band_dot(a, x, n):
    acc = None
    for p in _split(x, n):
        t = _dot(a, p)
        acc = t if acc is None else acc + t
    return acc


def _bucket_table():
    qi = np.arange(BLK)[None, :]
    kj = np.arange(2 * BLK)[:, None]
    dist = qi + BLK - kj
    n = np.maximum(dist, 0)
    nf = np.maximum(n, 1).astype(np.float32)
    large = 16 + (np.log(nf / np.float32(16)) / np.float32(np.log(128 / 16)) * np.float32(16)).astype(np.int32)
    large = np.minimum(large, NBUCKET - 1)
    return np.where(n < 16, n, large).astype(np.int32)


def _prenorm(x, g1):
    s = x.shape[0]
    tm = min(TM_IN, s)

    def body(x_ref, g_ref, h_ref):
        _, n = _rms(x_ref[...])
        h_ref[...] = (n * g_ref[...]).astype(BF16)

    row = pl.BlockSpec((tm, D), lambda i: (i, 0))
    return pl.pallas_call(
        body, name="prenorm", grid=(s // tm,),
        in_specs=[row, pl.BlockSpec((1, D), lambda i: (0, 0))], out_specs=row,
        out_shape=jax.ShapeDtypeStruct((s, D), BF16),
        compiler_params=_cp(1))(x, g1)


def _inproj_fwd(h1, w_in):
    s = h1.shape[0]
    tm = min(TM_IN, s)

    def body(h_ref, w_ref, u_ref, q_ref, k_ref, v_ref):
        proj = _dot_nt(h_ref[...], w_ref[...])
        u_ref[...] = proj[:, :512]
        q_ref[...] = proj[:, 512:1024].astype(BF16)
        k_ref[...] = proj[:, 1024:1152].astype(BF16)
        v_ref[...] = proj[:, 1152:1280].astype(BF16)

    row = lambda w: pl.BlockSpec((tm, w), lambda i: (i, 0))
    return pl.pallas_call(
        body, name="inproj_fwd", grid=(s // tm,),
        in_specs=[row(D), pl.BlockSpec((IN_W, D), lambda i: (0, 0))],
        out_specs=[row(512), row(512), row(128), row(128)],
        out_shape=[jax.ShapeDtypeStruct((s, 512), F32), jax.ShapeDtypeStruct((s, 512), BF16),
                   jax.ShapeDtypeStruct((s, 128), BF16), jax.ShapeDtypeStruct((s, 128), BF16)],
        compiler_params=_cp(1))(h1, w_in)


def _window_sums(ext, w, lag_sign, tm, terms):
    rows = lax.broadcasted_iota(jnp.int32, (HALO, 2 * HALO), 0)
    cols = lax.broadcasted_iota(jnp.int32, (HALO, 2 * HALO), 1)
    d = rows + HALO - cols if lag_sign > 0 else cols - rows
    band = jnp.where((d >= 0) & (d < w), 1.0, 0.0).astype(BF16)
    return jnp.concatenate([_band_dot(band, ext[r:r + 2 * HALO], terms) for r in range(0, tm, HALO)], axis=0)


def _pooled(ext, cur, t0, tm):
    t = t0 + lax.broadcasted_iota(jnp.int32, (tm, GROUP), 0)
    out = []
    for g, w in enumerate(WINDOWS):
        sl = slice(g * GROUP, (g + 1) * GROUP)
        cnt = jnp.minimum(t + 1, w).astype(F32)
        out.append(_window_sums(ext[:, sl], w, 1, tm, 2) / cnt - cur[:, sl])
    return out


def _pool_fwd(u, w_pool, pool_scale):
    s = u.shape[0]
    tm = min(TM_POOL, s)
    hb = tm // HALO

    def body(uc_ref, uh_ref, wp_ref, sc_ref, o_ref, pooled_ref):
        i = pl.program_id(0)
        cur = uc_ref[...]
        halo = jnp.where(i > 0, uh_ref[...], 0.0)
        ext = jnp.concatenate([halo, cur], axis=0)
        pooled = _pooled(ext, cur, i * tm, tm)
        for g in range(4):
            sl = slice(g * GROUP, (g + 1) * GROUP)
            pb = pooled[g].astype(BF16)
            pooled_ref[:, sl] = pb
            o_ref[:, sl] = (_dot(pb, wp_ref[g]) * sc_ref[:, sl]).astype(BF16)

    row = pl.BlockSpec((tm, 512), lambda i: (i, 0))
    return pl.pallas_call(
        body, name="pool_fwd", grid=(s // tm,),
        in_specs=[row, pl.BlockSpec((HALO, 512), lambda i: (jnp.maximum(i * hb - 1, 0), 0)),
                  pl.BlockSpec((4, GROUP, GROUP), lambda i: (0, 0, 0)),
                  pl.BlockSpec((1, 512), lambda i: (0, 0))],
        out_specs=[row, row],
        out_shape=[jax.ShapeDtypeStruct((s, 512), BF16)] * 2,
        compiler_params=_cp(1))(u, u, w_pool, pool_scale)


def _bias_table(bucket, rel_bias):
    def body(b_ref, rb_ref, o_ref):
        bucket_v = b_ref[...]
        key = lax.broadcasted_iota(jnp.int32, (2 * BLK, BLK), 0)
        dist = lax.broadcasted_iota(jnp.int32, (2 * BLK, BLK), 1) + BLK - key
        inwin = (dist >= 0) & (dist < BLK)
        for h in range(NQ):
            acc = jnp.zeros((2 * BLK, BLK), F32)
            for b in range(NBUCKET):
                acc = jnp.where(bucket_v == b, rb_ref[b, h], acc)
            rest = jnp.where(inwin, acc, NEG)
            o_ref[1, h] = rest
            o_ref[0, h] = jnp.where(key >= BLK, rest, NEG)

    return pl.pallas_call(
        body, name="bias_table",
        in_specs=[pl.BlockSpec(memory_space=pltpu.VMEM), pl.BlockSpec(memory_space=pltpu.SMEM)],
        out_specs=pl.BlockSpec(memory_space=pltpu.VMEM),
        out_shape=jax.ShapeDtypeStruct((2, NQ, 2 * BLK, BLK), F32),
        compiler_params=_cp())(bucket, rel_bias)


HD = 64


def _kv_band(ref, n, transposed):
    pstart = pl.multiple_of(jnp.maximum(n - 1, 0) * BLK, BLK)
    cstart = pl.multiple_of(n * BLK, BLK)
    lo = lax.broadcasted_iota(jnp.int32, (2 * BLK, 128), 1) < HD
    band = jnp.concatenate([ref[pl.ds(pstart, BLK), :], ref[pl.ds(cstart, BLK), :]], axis=0).astype(F32)
    rot = pltpu.roll(band, HD, axis=1)
    halves = ((jnp.where(lo, band, 0.0), jnp.where(lo, 0.0, rot)), (jnp.where(lo, rot, 0.0), jnp.where(lo, 0.0, band)))
    if transposed:
        out = [jnp.concatenate([a.T, b.T], axis=1).astype(BF16) for a, b in halves]
    else:
        out = [jnp.concatenate([a, b], axis=0).astype(BF16) for a, b in halves]
    return out, (lo, pstart, cstart)


def _pair_probs(qt, kk, bias, sk_ref, p):
    sink = jnp.concatenate([jnp.full((1, 1, BLK), sk_ref[0, 2 * p + e], F32) for e in range(2)], axis=0)
    s = _dot_nt(kk, qt).reshape(2, 2 * BLK, BLK) + bias
    m = jnp.maximum(jnp.max(s, axis=1, keepdims=True), sink)
    pr = jnp.exp(s - m)
    es = jnp.exp(sink - m)
    inv = 1.0 / (jnp.sum(pr, axis=1, keepdims=True) + es)
    return pr * inv, es * inv


def _attn_fwd(q, k, v, bias, sinks):
    s = q.shape[0]
    nblk = s // BLK

    def body(q_ref, k_ref, v_ref, b_ref, sk_ref, o_ref, prob_ref, ps_ref):
        n = pl.program_id(0)
        kk, _ = _kv_band(k_ref, n, False)
        vt, _ = _kv_band(v_ref, n, True)
        bidx = jnp.minimum(n, 1)
        for p in range(4):
            h = p // 2
            qt = (q_ref[:, p * 128:(p + 1) * 128].astype(F32) * SCALE).astype(BF16)
            prob, ps = _pair_probs(qt, kk[h], b_ref[bidx, pl.ds(2 * p, 2)], sk_ref, p)
            pb = prob.astype(BF16)
            prob_ref[pl.ds(2 * p, 2)] = pb
            ps_ref[pl.ds(2 * p, 2), :] = ps.reshape(2, BLK)
            acc_t = _dot(vt[h], pb.reshape(4 * BLK, BLK))
            o_ref[:, p * 128:(p + 1) * 128] = acc_t.T.astype(BF16)

    return pl.pallas_call(
        body, name="attn_fwd", grid=(nblk,),
        in_specs=[pl.BlockSpec((BLK, 512), lambda i: (i, 0)),
                  pl.BlockSpec((s, 128), lambda i: (0, 0)),
                  pl.BlockSpec((s, 128), lambda i: (0, 0)),
                  pl.BlockSpec((2, NQ, 2 * BLK, BLK), lambda i: (0, 0, 0, 0)),
                  pl.BlockSpec(memory_space=pltpu.SMEM)],
        out_specs=[pl.BlockSpec((BLK, 512), lambda i: (i, 0)),
                   pl.BlockSpec((None, NQ, 2 * BLK, BLK), lambda i: (i, 0, 0, 0)),
                   pl.BlockSpec((None, NQ, BLK), lambda i: (i, 0, 0))],
        out_shape=[jax.ShapeDtypeStruct((s, 512), BF16), jax.ShapeDtypeStruct((nblk, NQ, 2 * BLK, BLK), BF16),
                   jax.ShapeDtypeStruct((nblk, NQ, BLK), F32)],
        compiler_params=_cp(1))(q, k, v, bias, sinks)


def _outproj_fwd(pool_o, attn_o, w_out, x, g2, g3):
    s = x.shape[0]
    tm = min(TM, s)

    def body(p_ref, a_ref, w_ref, x_ref, g2_ref, g3_ref, mix_ref, x1_ref, h2_ref):
        mix = _dot(p_ref[...], w_ref[0:512, :]) + _dot(a_ref[...], w_ref[512:1024, :])
        mix_ref[...] = mix
        _, n2 = _rms(mix)
        x1 = x_ref[...] + n2 * g2_ref[...]
        x1_ref[...] = x1
        _, n3 = _rms(x1)
        h2_ref[...] = (n3 * g3_ref[...]).astype(BF16)

    row = lambda w: pl.BlockSpec((tm, w), lambda i: (i, 0))
    vec = pl.BlockSpec((1, D), lambda i: (0, 0))
    return pl.pallas_call(
        body, name="outproj_fwd", grid=(s // tm,),
        in_specs=[row(512), row(512), pl.BlockSpec((D, D), lambda i: (0, 0)), row(D), vec, vec],
        out_specs=[row(D), row(D), row(D)],
        out_shape=[jax.ShapeDtypeStruct((s, D), F32), jax.ShapeDtypeStruct((s, D), F32),
                   jax.ShapeDtypeStruct((s, D), BF16)],
        compiler_params=_cp(1))(pool_o, attn_o, w_out, x, g2, g3)


def _silu_parts(g):
    sg = jax.nn.sigmoid(g)
    return sg, g * sg


def _ffn_fwd(h2, wg, wu, wd, x1, target, g4):
    s = h2.shape[0]
    tm = min(TM_FFN_FWD, s)

    def body(h_ref, wg_ref, wu_ref, wd_ref, x1_ref, t_ref, g4_ref,
             gate_ref, up_ref, a_ref, df_ref, dy_ref, loss_ref, gg4_ref, f_acc):
        i = pl.program_id(0)
        j = pl.program_id(1)
        first = j == 0
        chunks = [pl.ds(r, min(FFN_ROWS, tm)) for r in range(0, tm, FFN_ROWS)]
        project = lambda rows: (_dot_nt(h_ref[rows, :], wg_ref[...]), _dot_nt(h_ref[rows, :], wu_ref[...]))
        ahead = [project(chunks[0])]
        for k, rows in enumerate(chunks):
            if k + 1 < len(chunks):
                ahead.append(project(chunks[k + 1]))
            gate, up = ahead[k]
            gate_ref[rows, :] = gate.astype(BF16)
            up_ref[rows, :] = up.astype(BF16)
            _, sl = _silu_parts(gate)
            a = (sl * up).astype(BF16)
            a_ref[rows, :] = a
            contrib = _dot(a, wd_ref[...])
            f_acc[rows, :] = jnp.where(first, contrib, f_acc[rows, :] + contrib)

        @pl.when((i == 0) & (j == 0))
        def _():
            loss_ref[...] = jnp.zeros_like(loss_ref)
            gg4_ref[...] = jnp.zeros_like(gg4_ref)

        @pl.when(j == NSH - 1)
        def _():
            r4, n4 = _rms(f_acc[...])
            g4v = g4_ref[...]
            err = x1_ref[...] + n4 * g4v - t_ref[...]
            loss_ref[...] += (0.5 / D) * jnp.sum(err * err).reshape(1, 1)
            dy = err * (1.0 / D)
            dy_ref[...] = dy
            df, dg = _rms_bwd(r4, n4, g4v, dy)
            gg4_ref[...] += dg
            df_ref[...] = df.astype(BF16)

    row = lambda w: pl.BlockSpec((tm, w), lambda i, j: (i, 0))
    sh = lambda r, c: pl.BlockSpec((None, r, c), lambda i, j: (j, 0, 0))
    act = pl.BlockSpec((None, tm, FS), lambda i, j: (j, i, 0))
    vec = pl.BlockSpec((1, D), lambda i, j: (0, 0))
    return pl.pallas_call(
        body, name="ffn_fwd", grid=(s // tm, NSH),
        in_specs=[row(D), sh(FS, D), sh(FS, D), sh(FS, D), row(D), row(D), vec],
        out_specs=[act, act, act, row(D), row(D), pl.BlockSpec((1, 1), lambda i, j: (0, 0)), vec],
        out_shape=[jax.ShapeDtypeStruct((NSH, s, FS), BF16)] * 3
        + [jax.ShapeDtypeStruct((s, D), BF16), jax.ShapeDtypeStruct((s, D), F32),
           jax.ShapeDtypeStruct((1, 1), F32), jax.ShapeDtypeStruct((1, D), F32)],
        scratch_shapes=[pltpu.VMEM((tm, D), F32)],
        compiler_params=_cp(2))(h2, wg, wu, wd, x1, target, g4)


def _ffn_bwd_act(df, gate, up, wg, wu, wd, dy, x1, mix, g3, g2):
    s = df.shape[0]
    tm = min(TM_FFN, s)

    def body(df_ref, gate_ref, up_ref, wg_ref, wu_ref, wd_ref, dy_ref, x1_ref, mix_ref, g3_ref, g2_ref,
             dgate_ref, dup_ref, dx1_ref, dmix_ref, gg3_ref, gg2_ref, acc):
        j = pl.program_id(0)
        i = pl.program_id(1)
        first = j == 0
        tile = pl.multiple_of(i * tm, tm)
        chunks = [pl.ds(r, min(FFN_ROWS, tm)) for r in range(0, tm, FFN_ROWS)]
        das = [_dot_nt(df_ref[chunks[0], :], wd_ref[...])]
        for k, rows in enumerate(chunks):
            if k + 1 < len(chunks):
                das.append(_dot_nt(df_ref[chunks[k + 1], :], wd_ref[...]))
            da = das[k]
            g = gate_ref[rows, :].astype(F32)
            u = up_ref[rows, :].astype(F32)
            sg, sl = _silu_parts(g)
            dgate = (da * u * (sg * (1.0 + g * (1.0 - sg)))).astype(BF16)
            dup = (da * sl).astype(BF16)
            dgate_ref[rows, :] = dgate
            dup_ref[rows, :] = dup
            contrib = _dot(dgate, wg_ref[...]) + _dot(dup, wu_ref[...])
            acc_rows = pl.ds(tile + k * FFN_ROWS, rows.size)
            acc[acc_rows, :] = jnp.where(first, contrib, acc[acc_rows, :] + contrib)

        @pl.when((i == 0) & (j == 0))
        def _():
            gg3_ref[...] = jnp.zeros_like(gg3_ref)
            gg2_ref[...] = jnp.zeros_like(gg2_ref)

        @pl.when(j == NSH - 1)
        def _():
            r3, n3 = _rms(x1_ref[...])
            dx1n, dg3 = _rms_bwd(r3, n3, g3_ref[...], acc[pl.ds(tile, tm), :])
            dx1 = dy_ref[...] + dx1n
            dx1_ref[...] = dx1
            gg3_ref[...] += dg3
            r2, n2 = _rms(mix_ref[...])
            dmix, dg2 = _rms_bwd(r2, n2, g2_ref[...], dx1)
            gg2_ref[...] += dg2
            dmix_ref[...] = dmix.astype(BF16)

    row = pl.BlockSpec((tm, D), lambda j, i: (i, 0))
    last = pl.BlockSpec((tm, D), lambda j, i: (i * (j // (NSH - 1)), 0))
    sh = pl.BlockSpec((None, FS, D), lambda j, i: (j, 0, 0))
    act = pl.BlockSpec((None, tm, FS), lambda j, i: (j, i, 0))
    vec = pl.BlockSpec((1, D), lambda j, i: (0, 0))
    return pl.pallas_call(
        body, name="ffn_bwd_act", grid=(NSH, s // tm),
        in_specs=[row, act, act, sh, sh, sh, last, last, last, vec, vec],
        out_specs=[act, act, last, last, vec, vec],
        out_shape=[jax.ShapeDtypeStruct((NSH, s, FS), BF16), jax.ShapeDtypeStruct((NSH, s, FS), BF16),
                   jax.ShapeDtypeStruct((s, D), F32), jax.ShapeDtypeStruct((s, D), BF16),
                   jax.ShapeDtypeStruct((1, D), F32), jax.ShapeDtypeStruct((1, D), F32)],
        scratch_shapes=[pltpu.VMEM((s, D), F32)],
        compiler_params=_cp(2))(df, gate, up, wg, wu, wd, dy, x1, mix, g3, g2)


SMEM_SPEC = pl.BlockSpec(memory_space=pltpu.SMEM)


def _emit_halves(acc_ref, row0, rows, c, own_ref, oth_ref):
    half = rows // 2
    own_ref[...] = acc_ref[pl.ds(row0 + pl.multiple_of(c * half, 8), half), :]
    oth_ref[...] = acc_ref[pl.ds(row0 + pl.multiple_of((1 - c) * half, 8), half), :].astype(BF16)


def _half_shapes(rows):
    return [jax.ShapeDtypeStruct((NSH, rows // 2, D), F32), jax.ShapeDtypeStruct((NSH, rows // 2, D), BF16)]


def _ffn_bwd_w(h2, act_a, dgate, dup, df, c_arr):
    s = h2.shape[0]
    tm = min(TM_FFN_FWD, s)
    nt = s // tm

    def body(c_ref, h_ref, a_ref, dgate_ref, dup_ref, df_ref, *rest):
        outs, accs = rest[:6], rest[6:]
        i = pl.program_id(1)

        @pl.when(i == 0)
        def _():
            for acc in accs:
                acc[...] = jnp.zeros_like(acc)

        h = h_ref[...]
        accs[0][...] += _dot_tn(dgate_ref[...], h)
        accs[1][...] += _dot_tn(dup_ref[...], h)
        accs[2][...] += _dot_tn(a_ref[...], df_ref[...])

        @pl.when(i == nt - 1)
        def _():
            for t, acc in enumerate(accs):
                _emit_halves(acc, 0, FS, c_ref[0], outs[2 * t], outs[2 * t + 1])

    row = lambda w: pl.BlockSpec((tm, w), lambda j, i: (i, 0))
    act = pl.BlockSpec((None, tm, FS), lambda j, i: (j, i, 0))
    half = pl.BlockSpec((None, FS // 2, D), lambda j, i: (j, 0, 0))
    return pl.pallas_call(
        body, name="ffn_bwd_w", grid=(NSH, nt),
        in_specs=[SMEM_SPEC, row(D), act, act, act, row(D)],
        out_specs=[half] * 6,
        out_shape=_half_shapes(FS) * 3,
        scratch_shapes=[pltpu.VMEM((FS, D), F32)] * 3,
        compiler_params=_cp(2))(c_arr, h2, act_a, dgate, dup, df)


def _outproj_bwd(dmix, w_out, pool_o, attn_o, c_arr, dep=None):
    s = dmix.shape[0]
    tm = min(TM, s)
    nt = s // tm

    def body(c_ref, dm_ref, w_ref, p_ref, a_ref, *rest):
        dpool_ref, dattn_ref, own_ref, oth_ref, gw_ref = rest[-5:]
        i = pl.program_id(0)

        @pl.when(i == 0)
        def _():
            gw_ref[...] = jnp.zeros_like(gw_ref)

        dm = dm_ref[...]
        dpool_ref[...] = _dot_nt(dm, w_ref[0:512, :])
        dattn_ref[...] = _dot_nt(dm, w_ref[512:1024, :]).astype(BF16)
        gw_ref[0:512, :] += _dot_tn(p_ref[...], dm)
        gw_ref[512:1024, :] += _dot_tn(a_ref[...], dm)

        @pl.when(i == nt - 1)
        def _():
            for k in range(NSH):
                _emit_halves(gw_ref, k * WOUT_S, WOUT_S, c_ref[0], own_ref.at[k], oth_ref.at[k])

    row = lambda w: pl.BlockSpec((tm, w), lambda i: (i, 0))
    full = pl.BlockSpec((D, D), lambda i: (0, 0))
    half = pl.BlockSpec((NSH, WOUT_S // 2, D), lambda i: (0, 0, 0))
    return pl.pallas_call(
        body, name="outproj_bwd", grid=(nt,),
        in_specs=[SMEM_SPEC, row(D), full, row(512), row(512)] + ([] if dep is None else [ANY]),
        out_specs=[row(512), row(512), half, half],
        out_shape=[jax.ShapeDtypeStruct((s, 512), F32), jax.ShapeDtypeStruct((s, 512), BF16)] + _half_shapes(WOUT_S),
        scratch_shapes=[pltpu.VMEM((D, D), F32)],
        compiler_params=_cp(1))(c_arr, dmix, w_out, pool_o, attn_o, *([] if dep is None else [dep]))


def _pool_bwd(pooled, dpool, w_pool, pool_scale, dep=None):
    s = pooled.shape[0]
    tm = min(TM_POOL, s)
    hb = tm // HALO
    nt = s // tm
    last_halo = s // HALO - 1

    def body(p_ref, dc_ref, dn_ref, wp_ref, sc_ref, *rest):
        du_ref, gwp_ref, gsc_ref = rest[-3:]
        i = pl.program_id(0)

        @pl.when(i == 0)
        def _():
            gwp_ref[...] = jnp.zeros_like(gwp_ref)
            gsc_ref[...] = jnp.zeros_like(gsc_ref)

        dcur = dc_ref[...]
        dnext = jnp.where(i < nt - 1, dn_ref[...], 0.0)
        dext = jnp.concatenate([dcur, dnext], axis=0)
        t_ext = i * tm + lax.broadcasted_iota(jnp.int32, (tm + HALO, GROUP), 0)
        for g, w in enumerate(WINDOWS):
            sl = slice(g * GROUP, (g + 1) * GROUP)
            pb = p_ref[:, sl]
            wp = wp_ref[g]
            mixed = _dot(pb, wp)
            gsc_ref[:, sl] += jnp.sum(dcur[:, sl] * mixed, axis=0, keepdims=True)
            dmixed = (dext[:, sl] * sc_ref[:, sl]).astype(BF16)
            gwp_ref[g] += _dot_tn(pb, dmixed[0:tm])
            dpooled = _dot_nt(dmixed, wp)
            z = dpooled / jnp.minimum(t_ext + 1, w).astype(F32)
            du_ref[:, sl] = (_window_sums(z, w, -1, tm, 2) - dpooled[0:tm]).astype(BF16)

    return pl.pallas_call(
        body, name="pool_bwd", grid=(nt,),
        in_specs=[pl.BlockSpec((tm, 512), lambda i: (i, 0)),
                  pl.BlockSpec((tm, 512), lambda i: (i, 0)),
                  pl.BlockSpec((HALO, 512), lambda i: (jnp.minimum((i + 1) * hb, last_halo), 0)),
                  pl.BlockSpec((4, GROUP, GROUP), lambda i: (0, 0, 0)),
                  pl.BlockSpec((1, 512), lambda i: (0, 0))] + ([] if dep is None else [ANY]),
        out_specs=[pl.BlockSpec((tm, 512), lambda i: (i, 0)),
                   pl.BlockSpec((4, GROUP, GROUP), lambda i: (0, 0, 0)),
                   pl.BlockSpec((1, 512), lambda i: (0, 0))],
        out_shape=[jax.ShapeDtypeStruct((s, 512), BF16), jax.ShapeDtypeStruct((4, GROUP, GROUP), F32),
                   jax.ShapeDtypeStruct((1, 512), F32)],
        compiler_params=_cp(1))(pooled, dpool, dpool, w_pool, pool_scale, *([] if dep is None else [dep]))


def _attn_bwd(q, k, v, do, probs, sink_share, bucket, dep=None):
    s = q.shape[0]
    nblk = s // BLK

    def body(q_ref, do_ref, k_ref, v_ref, prob_ref, ps_ref, bk_ref, *rest):
        dq_ref, dk_ref, dv_ref, grb_ref, gsk_ref, dss_ref = rest[-6:]
        n = pl.program_id(0)

        @pl.when(n == 0)
        def _():
            dk_ref[...] = jnp.zeros_like(dk_ref)
            dv_ref[...] = jnp.zeros_like(dv_ref)
            dss_ref[...] = jnp.zeros_like(dss_ref)
            gsk_ref[...] = jnp.zeros_like(gsk_ref)

        kt, (lo, pstart, cstart) = _kv_band(k_ref, n, True)
        vv, _ = _kv_band(v_ref, n, False)
        lo_q = lax.broadcasted_iota(jnp.int32, (BLK, 128), 1) < HD
        dkk = [jnp.zeros((2 * BLK, 128), F32), jnp.zeros((2 * BLK, 128), F32)]
        dvv = [jnp.zeros((2 * BLK, 128), F32), jnp.zeros((2 * BLK, 128), F32)]

        def by_head(t):
            return jnp.concatenate([jnp.where(lo_q, t, 0.0), jnp.where(lo_q, 0.0, t)], axis=0).astype(BF16)

        for p in range(4):
            h = p // 2
            qt = q_ref[:, p * 128:(p + 1) * 128].astype(F32) * SCALE
            dot = do_ref[:, p * 128:(p + 1) * 128]
            pb = prob_ref[pl.ds(2 * p, 2)]
            prob = pb.astype(F32)
            ps = ps_ref[pl.ds(2 * p, 2), :].reshape(2, 1, BLK)
            dp = _dot_nt(vv[h], dot).reshape(2, 2 * BLK, BLK)
            delta = jnp.sum(prob * dp, axis=1, keepdims=True)
            ds = prob * (dp - delta)
            sink_part = ps * delta
            for e in range(2):
                gs = -jnp.sum(sink_part[e]).reshape(1, 1)
                gsk_ref[pl.ds(2 * p + e, 1), :] += jnp.broadcast_to(gs, (1, 128))
            dss_ref[pl.ds(2 * p, 2)] += ds
            dsb = ds.astype(BF16)
            dq_t = _dot(kt[h], dsb.reshape(4 * BLK, BLK))
            dq_ref[:, p * 128:(p + 1) * 128] = (dq_t.T * SCALE).astype(BF16)
            dkk[h] = dkk[h] + _dot(jnp.concatenate([dsb[0], dsb[1]], axis=1), by_head(qt))
            dvv[h] = dvv[h] + _dot(jnp.concatenate([pb[0], pb[1]], axis=1), by_head(dot.astype(F32)))
        for acc, ref in ((dkk, dk_ref), (dvv, dv_ref)):
            f0 = acc[0] + pltpu.roll(acc[0], HD, axis=1)
            f1 = acc[1] + pltpu.roll(acc[1], HD, axis=1)
            band = jnp.where(lo, f0, f1)
            ref[pl.ds(pstart, BLK), :] += band[0:BLK]
            ref[pl.ds(cstart, BLK), :] += band[BLK:2 * BLK]

        @pl.when(n == nblk - 1)
        def _():
            _relbias_grad(dss_ref, bk_ref[...], grb_ref)

    blk = pl.BlockSpec((BLK, 512), lambda i: (i, 0))
    kv = pl.BlockSpec((s, 128), lambda i: (0, 0))
    row8 = pl.BlockSpec((NQ, 128), lambda i: (0, 0))
    return pl.pallas_call(
        body, name="attn_bwd", grid=(nblk,),
        in_specs=[blk, blk, kv, kv, pl.BlockSpec((None, NQ, 2 * BLK, BLK), lambda i: (i, 0, 0, 0)),
                  pl.BlockSpec((None, NQ, BLK), lambda i: (i, 0, 0)), pl.BlockSpec((2 * BLK, BLK), lambda i: (0, 0))]
        + ([] if dep is None else [ANY]),
        out_specs=[blk, kv, kv, row8, row8],
        out_shape=[jax.ShapeDtypeStruct((s, 512), BF16), jax.ShapeDtypeStruct((s, 128), F32),
                   jax.ShapeDtypeStruct((s, 128), F32), jax.ShapeDtypeStruct((NQ, 128), F32),
                   jax.ShapeDtypeStruct((NQ, 128), F32)],
        scratch_shapes=[pltpu.VMEM((NQ, 2 * BLK, BLK), F32)],
        compiler_params=_cp(1))(q, do, k, v, probs, sink_share, bucket, *([] if dep is None else [dep]))


def _relbias_grad(ds_ref, bucket_v, o_ref):
    lane = lax.broadcasted_iota(jnp.int32, (NQ, 128), 1)
    head_row = lax.broadcasted_iota(jnp.int32, (NQ, BLK), 0)
    acc = jnp.zeros((NQ, 128), F32)
    for b in range(NBUCKET):
        sel = bucket_v == b
        stack = jnp.zeros((NQ, BLK), F32)
        for h in range(NQ):
            col_sums = jnp.sum(jnp.where(sel, ds_ref[h], 0.0), axis=0, keepdims=True)
            stack = jnp.where(head_row == h, col_sums, stack)
        acc = acc + jnp.where(lane == b, jnp.sum(stack, axis=1, keepdims=True), 0.0)
    o_ref[...] = acc


def _inproj_bwd(x, g1, du, dq, dk, dv, w_in, dx1, c_arr):
    s = x.shape[0]
    tm = min(TM, s)
    nt = s // tm
    cols = ((0, 512), (512, 1024), (1024, 1152), (1152, 1280))

    def body(c_ref, x_ref, g_ref, du_ref, dq_ref, dk_ref, dv_ref, w_ref, dx1_ref,
             gx_ref, own_ref, oth_ref, gg_ref, gw_ref):
        i = pl.program_id(0)

        @pl.when(i == 0)
        def _():
            gw_ref[...] = jnp.zeros_like(gw_ref)
            gg_ref[...] = jnp.zeros_like(gg_ref)

        parts = (du_ref[...], dq_ref[...], dk_ref[...].astype(BF16), dv_ref[...].astype(BF16))
        dh = None
        for (a, b), dpart in zip(cols, parts):
            t = _dot(dpart, w_ref[a:b, :])
            dh = t if dh is None else dh + t
        gv = g_ref[...]
        r1, n1 = _rms(x_ref[...])
        h = (n1 * gv).astype(BF16)
        for (a, b), dpart in zip(cols, parts):
            gw_ref[a:b, :] += _dot_tn(dpart, h)
        dx, dg = _rms_bwd(r1, n1, gv, dh)
        gg_ref[...] += dg
        gx_ref[...] = dx1_ref[...] + dx

        @pl.when(i == nt - 1)
        def _():
            for k in range(NSH):
                _emit_halves(gw_ref, k * WIN_S, WIN_S, c_ref[0], own_ref.at[k], oth_ref.at[k])

    row = lambda w: pl.BlockSpec((tm, w), lambda i: (i, 0))
    vec = pl.BlockSpec((1, D), lambda i: (0, 0))
    full = pl.BlockSpec((IN_W, D), lambda i: (0, 0))
    half = pl.BlockSpec((NSH, WIN_S // 2, D), lambda i: (0, 0, 0))
    return pl.pallas_call(
        body, name="inproj_bwd", grid=(nt,),
        in_specs=[SMEM_SPEC, row(D), vec, row(512), row(512), row(128), row(128), full, row(D)],
        out_specs=[row(D), half, half, vec],
        out_shape=[jax.ShapeDtypeStruct((s, D), F32)] + _half_shapes(WIN_S) + [jax.ShapeDtypeStruct((1, D), F32)],
        scratch_shapes=[pltpu.VMEM((IN_W, D), F32)],
        compiler_params=_cp(1))(c_arr, x, g1, du, dq, dk, dv, w_in, dx1)


def _local_step(x, target, small, w_in, w_out, ffn_weights, c_arr, on_ffn_grads=None, on_wout_grads=None):
    g1, g2, g3, g4, w_pool, pool_scale, rel_bias, sinks = small
    bucket = jnp.asarray(_bucket_table())
    wp_bf = w_pool.astype(BF16)
    bias = _bias_table(bucket, rel_bias)
    h1 = _prenorm(x, g1)
    if callable(w_in):
        w_in = w_in(bias, h1)
    u, q, k, v = _inproj_fwd(h1, w_in)
    pool_o, pooled = _pool_fwd(u, wp_bf, pool_scale)
    attn_o, probs, sink_share = _attn_fwd(q, k, v, bias, sinks)
    g2_fwd = g2
    if callable(w_out):
        w_out, after = w_out(pool_o, attn_o)
        if after is not None:
            g2_fwd = g2 + after[:1, :1]
    mix, x1, h2 = _outproj_fwd(pool_o, attn_o, w_out, x, g2_fwd, g3)
    wg, wu, wd = ffn_weights(h2) if callable(ffn_weights) else ffn_weights
    gate, up, act_a, df, dy, loss, gg4 = _ffn_fwd(h2, wg, wu, wd, x1, target, g4)
    dgate, dup, dx1, dmix, gg3, gg2 = _ffn_bwd_act(df, gate, up, wg, wu, wd, dy, x1, mix, g3, g2)
    ffn_g = _ffn_bwd_w(h2, act_a, dgate, dup, df, c_arr)
    dep = None if on_ffn_grads is None else on_ffn_grads(ffn_g)
    dpool, dattn, wout_own, wout_oth = _outproj_bwd(dmix, w_out, pool_o, attn_o, c_arr, dep)
    dep = None if on_wout_grads is None else on_wout_grads(wout_own, wout_oth, dpool)
    du, gwp, gsc = _pool_bwd(pooled, dpool, wp_bf, pool_scale, dep)
    dq, dk, dv, grb, gsk = _attn_bwd(q, k, v, dattn, probs, sink_share, bucket, dep)
    gx, win_own, win_oth, gg1 = _inproj_bwd(x, g1, du, dq, dk, dv, w_in, dx1, c_arr)
    small_grads = (gg1, gg2, gg3, gg4, gwp, gsc, grb, gsk[:, 0].reshape(1, NQ))
    return loss, gx, small_grads, ((win_own, win_oth), (wout_own, wout_oth), ffn_g)


def _place():
    x, y, c = lax.axis_index("x"), lax.axis_index("y"), lax.axis_index("c")
    chips = ((1 - x, y), (x, 1 - y), (1 - x, 1 - y))
    return x, y, c, chips


def _remote(src, dst, ssem, rsem, dev):
    return pltpu.make_async_remote_copy(src_ref=src, dst_ref=dst, send_sem=ssem, recv_sem=rsem,
                                        device_id=dev, device_id_type=MESH_T)


def _to_sibling(arrs, name, after=()):
    nt, na = len(arrs), len(after)

    def body(*refs):
        srcs, dsts = refs[:nt], refs[nt + na:2 * nt + na]
        ssem, rsem = refs[2 * nt + na:]
        x, y, c, _ = _place()
        cps = [_remote(srcs[t], dsts[t], ssem.at[t], rsem.at[t], (x, y, 1 - c)) for t in range(nt)]
        for cp in cps:
            cp.start()
        for cp in cps:
            cp.wait()

    return pl.pallas_call(
        body, name=name, in_specs=[ANY] * (nt + na), out_specs=[ANY] * nt,
        out_shape=[jax.ShapeDtypeStruct(a.shape, a.dtype) for a in arrs],
        scratch_shapes=[pltpu.SemaphoreType.DMA((nt,)), pltpu.SemaphoreType.DMA((nt,))],
        compiler_params=_cp())(*arrs, *after)


HBM_SPEC = pl.BlockSpec(memory_space=pltpu.HBM)
SEM_SPEC = pl.BlockSpec(memory_space=pltpu.SEMAPHORE)
DATAFLOW = pltpu.SideEffectType.DATAFLOW_SIDE_EFFECTING


def _gather_copies(srcs, lands, ssem, rsem):
    x, y, c, chips = _place()
    me = 2 * x + y
    out = []
    for t in range(len(srcs)):
        half = srcs[t].shape[0] // 2
        mine = pl.ds(pl.multiple_of(c * half, 16), half)
        for j, (px, py) in enumerate(chips):
            k = 3 * t + j
            send = _remote(srcs[t].at[mine], lands[t].at[me, mine], ssem.at[k], rsem.at[k], (px, py, c))
            blk = lands[t].at[2 * px + py, mine]
            out.append((send, _remote(blk, blk, ssem.at[k], rsem.at[k], (px, py, c))))
    return out


def _scatter_copies(srcs, lands, ssem, rsem):
    x, y, c, chips = _place()
    out = []
    for t in range(len(srcs)):
        for j, (px, py) in enumerate(chips):
            k = 3 * t + j
            send = _remote(srcs[t].at[2 * px + py], lands[t].at[j], ssem.at[k], rsem.at[k], (px, py, c))
            out.append((send, _remote(lands[t].at[j], lands[t].at[j], ssem.at[k], rsem.at[k], (px, py, c))))
    return out


def _sibling_copies(srcs, lands, ssem, rsem):
    x, y, c, _ = _place()
    sib = (x, y, 1 - c)
    return [(_remote(srcs[t], lands[t], ssem.at[t], rsem.at[t], sib),
             _remote(lands[t], lands[t], ssem.at[t], rsem.at[t], sib)) for t in range(len(srcs))]


def _peer_copies(srcs, lands, ssem, rsem):
    x, y, c, _ = _place()
    me = 4 * x + 2 * y + c
    out = []
    for kk in range(1, 8):
        px, py, pc = x ^ (kk >> 2), y ^ ((kk >> 1) & 1), c ^ (kk & 1)
        send = _remote(srcs[0], lands[0].at[me], ssem.at[kk - 1], rsem.at[kk - 1], (px, py, pc))
        slot = lands[0].at[4 * px + 2 * py + pc]
        out.append((send, _remote(slot, slot, ssem.at[kk - 1], rsem.at[kk - 1], (px, py, pc))))
    return out


def _split_start(plan, ncopy, srcs, lands, after, name):
    ns, nbuf = len(srcs), len(srcs) + len(lands)
    extra = [] if after is None else [after]
    n_in = nbuf + len(extra)

    def body(*refs):
        ssem, rsem, token = refs[n_in], refs[n_in + 1], refs[-1]
        for send, _ in plan(refs[:ns], refs[ns:nbuf], ssem, rsem):
            send.start()
        token[...] = jnp.zeros_like(token)

    bufs = [pltpu.with_memory_space_constraint(a, pltpu.HBM) for a in list(srcs) + list(lands)]
    outs = pl.pallas_call(
        body, name=name, in_specs=[HBM_SPEC] * nbuf + [ANY] * len(extra),
        out_specs=[SEM_SPEC, SEM_SPEC] + [HBM_SPEC] * nbuf + [pl.BlockSpec(memory_space=pltpu.VMEM)],
        out_shape=[pltpu.SemaphoreType.DMA((ncopy,)), pltpu.SemaphoreType.DMA((ncopy,))]
        + [pltpu.HBM(a.shape, a.dtype) for a in bufs] + [jax.ShapeDtypeStruct((8, 128), F32)],
        input_output_aliases={i: 2 + i for i in range(nbuf)},
        compiler_params=pltpu.CompilerParams(has_side_effects=DATAFLOW))(*bufs, *extra)
    return outs[0], outs[1], outs[2:2 + ns], outs[2 + ns:2 + nbuf], outs[-1]


def _split_wait(plan, ssem, rsem, srcs, lands, after, name, only=None):
    ns, nbuf = len(srcs), len(srcs) + len(lands)

    def body(*refs):
        s_ref, r_ref = refs[nbuf], refs[nbuf + 1]
        for k, (send, recv) in enumerate(plan(refs[:ns], refs[ns:nbuf], s_ref, r_ref)):
            if only is None or k in only:
                send.wait_send()
                recv.wait_recv()

    outs = pl.pallas_call(
        body, name=name, in_specs=[HBM_SPEC] * nbuf + [SEM_SPEC, SEM_SPEC] + [ANY] * len(after),
        out_specs=[HBM_SPEC] * nbuf,
        out_shape=[pltpu.HBM(a.shape, a.dtype) for a in list(srcs) + list(lands)],
        input_output_aliases={i: i for i in range(nbuf)},
        compiler_params=pltpu.CompilerParams(has_side_effects=DATAFLOW))(*srcs, *lands, ssem, rsem, *after)
    return outs


def _gather_finish(lands, tag):
    nt = len(lands)

    def body(*refs):
        outs = refs[nt:2 * nt]
        dsend, drecv = refs[2 * nt:]
        x, y, c, chips = _place()
        sib = (x, y, 1 - c)
        sends = []
        for t in range(nt):
            half = outs[t].shape[1] // 2
            mine = pl.ds(pl.multiple_of(c * half, 16), half)
            for j, (px, py) in enumerate(chips):
                blk = outs[t].at[2 * px + py, mine]
                cp = _remote(blk, blk, dsend.at[t, j], drecv.at[t, j], sib)
                cp.start()
                sends.append(cp)
        for t in range(nt):
            half = outs[t].shape[1] // 2
            other = pl.ds(pl.multiple_of((1 - c) * half, 16), half)
            for j, (px, py) in enumerate(chips):
                blk = outs[t].at[2 * px + py, other]
                _remote(blk, blk, dsend.at[t, j], drecv.at[t, j], sib).wait_recv()
        for cp in sends:
            cp.wait_send()

    return pl.pallas_call(
        body, name="gather_finish_" + tag, in_specs=[ANY] * nt, out_specs=[ANY] * nt,
        out_shape=[jax.ShapeDtypeStruct(a.shape, a.dtype) for a in lands],
        input_output_aliases={t: t for t in range(nt)},
        scratch_shapes=[pltpu.SemaphoreType.DMA((nt, 3)), pltpu.SemaphoreType.DMA((nt, 3))],
        compiler_params=_cp())(*lands)


def _chip_partial(owns, recv, chip_arr, tag):
    nt = len(owns)

    def body(chip_ref, *refs):
        k = pl.program_id(0)
        for t in range(nt):
            p = refs[t][...] + refs[nt + t][...].astype(F32)
            refs[2 * nt + t][...] = p.astype(BF16)

            @pl.when(k == chip_ref[0])
            def _():
                refs[3 * nt + t][...] = p

    blk_specs = [pl.BlockSpec((None,) + a.shape[1:], lambda k, chip_ref: (k, 0, 0)) for a in owns]
    own_specs = [pl.BlockSpec(a.shape[1:], lambda k, chip_ref: (0, 0)) for a in owns]
    return pl.pallas_call(
        body, name="rs_chip_partial_" + tag,
        grid_spec=pltpu.PrefetchScalarGridSpec(num_scalar_prefetch=1, grid=(NSH,), in_specs=blk_specs * 2,
                                               out_specs=blk_specs + own_specs),
        out_shape=[jax.ShapeDtypeStruct(a.shape, BF16) for a in owns]
        + [jax.ShapeDtypeStruct(a.shape[1:], F32) for a in owns],
        compiler_params=_cp(1))(chip_arr, *owns, *recv)


def _sum_chips(own, recv, tag, after=()):
    nt = len(own)

    def body(*refs):
        outs = refs[2 * nt + len(after):]
        for t in range(nt):
            r = refs[nt + t]
            outs[t][...] = ((refs[t][...] + r[0].astype(F32)) + r[1].astype(F32)) + r[2].astype(F32)

    vm = pl.BlockSpec(memory_space=pltpu.VMEM)
    return pl.pallas_call(
        body, name="rs_sum_chips_" + tag, in_specs=[vm] * (2 * nt) + [ANY] * len(after), out_specs=[vm] * nt,
        out_shape=[jax.ShapeDtypeStruct(a.shape, F32) for a in own],
        compiler_params=_cp())(*own, *recv, *after)


def _adamw_math(w, g, m, v):
    m = ADAM_B1 * m + (1.0 - ADAM_B1) * g
    v = ADAM_B2 * v + (1.0 - ADAM_B2) * (g * g)
    m_hat = m / (1.0 - ADAM_B1 ** ADAM_STEP)
    v_hat = v / (1.0 - ADAM_B2 ** ADAM_STEP)
    delta = -ADAM_LR * (m_hat / (jnp.sqrt(v_hat) + ADAM_EPS) + ADAM_WD * w)
    return delta, m, v


ADAM_SPLIT = 4


def _adamw_shards(own, sib, ws, ms, vs, c_arr, tag):
    nt = len(own)

    def body(c_ref, *refs):
        hh = pl.program_id(0)
        mine = hh == c_ref[0]
        for t in range(nt):
            g = jnp.where(mine, refs[t][...], refs[nt + t][...])
            delta, m, v = _adamw_math(refs[2 * nt + t][...], g, refs[3 * nt + t][...], refs[4 * nt + t][...])
            refs[5 * nt + t][...] = g
            refs[6 * nt + t][...] = delta
            refs[7 * nt + t][...] = m
            refs[8 * nt + t][...] = v

    def tile(a):
        return (a.shape[0] // ADAM_SPLIT, a.shape[1])

    def half_index(used_when_mine):
        def index(hh, q, c_ref):
            mine = 1 - (hh - c_ref[0]) * (hh - c_ref[0])
            used = mine if used_when_mine else 1 - mine
            return (used * q + (1 - used) * hh * (ADAM_SPLIT - 1), 0)
        return index

    own_specs = [pl.BlockSpec(tile(a), half_index(True)) for a in own]
    sib_specs = [pl.BlockSpec(tile(a), half_index(False)) for a in own]
    full_specs = [pl.BlockSpec(tile(a), lambda hh, q, c_ref: (hh * ADAM_SPLIT + q, 0)) for a in own]
    return pl.pallas_call(
        body, name="adamw_shards_" + tag,
        grid_spec=pltpu.PrefetchScalarGridSpec(num_scalar_prefetch=1, grid=(2, ADAM_SPLIT),
                                               in_specs=own_specs + sib_specs + full_specs * 3,
                                               out_specs=full_specs * 4),
        out_shape=[jax.ShapeDtypeStruct(w.shape, F32) for w in ws] * 4,
        compiler_params=_cp(2))(c_arr, *own, *sib, *ws, *ms, *vs)


SMALL_WPOOL_ROW = 32
SMALL_SCALE_ROW = SMALL_WPOOL_ROW + 4 * GROUP
SMALL_BIAS_ROW = SMALL_SCALE_ROW + 8
SMALL_SINKS_ROW = SMALL_BIAS_ROW + NQ
SMALL_LOSS_ROW = SMALL_SINKS_ROW + 8


def _small_sum_adamw(gathered, wp, mp, vp):
    rows = wp.shape[0]

    def body(g_ref, w_ref, m_ref, v_ref, *rest):
        outs, res = rest[:-1], rest[-1]
        g = g_ref[0]
        for d in range(1, 8):
            g = g + g_ref[d]
        delta, m, v = _adamw_math(w_ref[...], g, m_ref[...], v_ref[...])
        for kind, val in enumerate((g, delta, m, v)):
            res[kind] = val
            gains = outs[8 * kind:8 * kind + 4]
            w_pool_o, scale_o, bias_o, sinks_o = outs[8 * kind + 4:8 * kind + 8]
            for t in range(4):
                for i in range(8):
                    gains[t][:, 128 * i:128 * (i + 1)] = res[kind, pl.ds(8 * t + i, 1), :]
            for grp in range(4):
                w_pool_o[grp] = res[kind, pl.ds(SMALL_WPOOL_ROW + GROUP * grp, GROUP), :]
            for i in range(4):
                scale_o[:, 128 * i:128 * (i + 1)] = res[kind, pl.ds(SMALL_SCALE_ROW + i, 1), :]
            bias_o[...] = res[kind, pl.ds(SMALL_BIAS_ROW, NQ), :][:, 0:NBUCKET]
            sinks_o[...] = res[kind, pl.ds(SMALL_SINKS_ROW, 1), :][:, 0:NQ]
        outs[-1][...] = res[0, pl.ds(SMALL_LOSS_ROW, 1), :]

    vm = pl.BlockSpec(memory_space=pltpu.VMEM)
    one_kind = [jax.ShapeDtypeStruct((1, D), F32)] * 4 + [
        jax.ShapeDtypeStruct((4, GROUP, GROUP), F32), jax.ShapeDtypeStruct((1, POOL_W), F32),
        jax.ShapeDtypeStruct((NQ, NBUCKET), F32), jax.ShapeDtypeStruct((1, NQ), F32)]
    return pl.pallas_call(
        body, name="small_sum_adamw", in_specs=[vm] * 4, out_specs=[vm] * 33,
        out_shape=one_kind * 4 + [jax.ShapeDtypeStruct((1, 128), F32)],
        scratch_shapes=[pltpu.VMEM((4, rows, 128), F32)],
        compiler_params=_cp())(gathered, wp, mp, vp)


def _pack_small(gains4, w_pool, pool_scale, rel_bias_t, sinks, loss):
    bias_rows = jnp.pad(rel_bias_t, ((0, 0), (0, 128 - rel_bias_t.shape[1])))
    parts = list(gains4) + [w_pool, pool_scale, bias_rows, sinks, loss]
    rows = []
    for a in parts:
        flat = a.reshape(-1)
        n = flat.shape[0]
        padded = -(-n // 1024) * 1024
        rows.append(jnp.pad(flat, (0, padded - n)).reshape(-1, 128))
    return jnp.concatenate(rows, axis=0)


def kernel(x, g_pre_mix, w_in, w_pool, pool_scale, rel_bias, sinks, w_out, g_post_mix, g_pre_ffn, w_gate, w_up, w_down, g_post_ffn, loss_target, m_g_pre_mix, m_w_in, m_w_pool, m_pool_scale, m_rel_bias, m_sinks, m_w_out, m_g_post_mix, m_g_pre_ffn, m_w_gate, m_w_up, m_w_down, m_g_post_ffn, v_g_pre_mix, v_w_in, v_w_pool, v_pool_scale, v_rel_bias, v_sinks, v_w_out, v_g_post_mix, v_g_pre_ffn, v_w_gate, v_w_up, v_w_down, v_g_post_ffn):
    c = lax.axis_index("c")
    chip = 2 * lax.axis_index("x") + lax.axis_index("y")

    def shards(a_in, a_out, a_gate, a_up, a_down):
        return (a_in[0].T, a_out[0], a_gate[0].T, a_up[0].T, a_down[0])

    c_arr = c.reshape(1).astype(jnp.int32)
    chip_arr = chip.reshape(1).astype(jnp.int32)

    big_w = shards(w_in, w_out, w_gate, w_up, w_down)
    big_bf = [w.astype(BF16) for w in big_w]
    g_ssem, g_rsem, g_srcs, g_lands, _ = _split_start(
        _gather_copies, 15, big_bf, [lax.empty((NSH,) + a.shape, BF16) for a in big_bf], None, "gather_start")
    fw = {}

    def with_own(land, shard):
        return lax.dynamic_update_slice(land, shard[None], (chip, 0, 0))

    def w_in_ready(*after):
        fw["first"] = _split_wait(_gather_copies, g_ssem, g_rsem, g_srcs, g_lands, after, "gather_win_wait",
                                  only=(0, 1, 2))
        (fw["win"],) = _gather_finish([with_own(fw["first"][5], fw["first"][0])], "win")
        return fw["win"].reshape(IN_W, D)

    def w_out_ready(*after):
        first = fw["first"]
        fw["second"] = _split_wait(_gather_copies, g_ssem, g_rsem, first[:5], [fw["win"]] + list(first[6:]), after,
                                   "gather_wout_wait", only=(3, 4, 5))
        (fw["wout"],) = _gather_finish([with_own(fw["second"][6], fw["second"][1])], "wout")
        return fw["wout"].reshape(D, D), None

    def ffn_weights(after):
        second = fw["second"]
        outs = _split_wait(_gather_copies, g_ssem, g_rsem, second[:5], [second[5], fw["wout"]] + list(second[7:]),
                           [after], "gather_ffn_wait", only=tuple(range(6, 15)))
        return _gather_finish([with_own(land, src) for src, land in zip(outs[2:5], outs[7:])], "ffn")

    rs = {}

    def on_ffn_grads(ffn_g):
        oths = ffn_g[1::2]
        rs["ffn_own"] = ffn_g[0::2]
        rs["x"] = _split_start(_sibling_copies, 3, oths, [lax.empty(a.shape, BF16) for a in oths], ffn_g[0],
                               "rs_exchange_ffn_start")
        return rs["x"][4]

    def on_wout_grads(own, oth, after):
        ssem, rsem, srcs, lands, _ = rs["x"]
        recv_ffn = _split_wait(_sibling_copies, ssem, rsem, srcs, lands, [after], "rs_exchange_ffn_wait")[3:]
        recv_wout = _to_sibling([oth], "rs_exchange_wout")
        outs = _chip_partial([own] + list(rs["ffn_own"]), list(recv_wout) + list(recv_ffn), chip_arr, "early")
        rs["early_own"] = outs[4:]
        rs["s"] = _split_start(_scatter_copies, 12, outs[:4],
                               [lax.empty((3,) + a.shape[1:], BF16) for a in outs[:4]], outs[4], "scatter_early_start")
        return rs["s"][4]

    small = (g_pre_mix, g_post_mix, g_pre_ffn, g_post_ffn, w_pool[0], pool_scale, rel_bias, sinks)
    loss, gx, small_grads, ((win_own, win_oth), _, _) = _local_step(
        x[0], loss_target[0], small, w_in_ready, w_out_ready, ffn_weights, c_arr, on_ffn_grads, on_wout_grads)

    ssem, rsem, srcs, lands, _ = rs["s"]
    recv_early = _split_wait(_scatter_copies, ssem, rsem, srcs, lands, [gx], "scatter_early_wait")[4:]
    finals = _sum_chips(list(rs["early_own"]), list(recv_early), "early")
    to_sib = [win_oth] + list(finals)
    sh_ssem, sh_rsem, sh_srcs, sh_lands, _ = _split_start(
        _sibling_copies, 5, to_sib, [lax.empty(a.shape, a.dtype) for a in to_sib], finals[0], "rs_share_start")

    unused = jnp.zeros((1, 1), F32)
    gp = _pack_small(small_grads[:4], *small_grads[4:], loss)
    wp = _pack_small((g_pre_mix, g_post_mix, g_pre_ffn, g_post_ffn), w_pool, pool_scale, rel_bias.T, sinks, unused)
    mp = _pack_small((m_g_pre_mix, m_g_post_mix, m_g_pre_ffn, m_g_post_ffn), m_w_pool, m_pool_scale, m_rel_bias.T,
                     m_sinks, unused)
    vp = _pack_small((v_g_pre_mix, v_g_post_mix, v_g_pre_ffn, v_g_post_ffn), v_w_pool, v_pool_scale, v_rel_bias.T,
                     v_sinks, unused)

    moments = (shards(m_w_in, m_w_out, m_w_gate, m_w_up, m_w_down),
               shards(v_w_in, v_w_out, v_w_gate, v_w_up, v_w_down))
    sh_first = _split_wait(_sibling_copies, sh_ssem, sh_rsem, sh_srcs, sh_lands, [], "rs_share_win_wait", only=(0,))
    outs = _chip_partial([win_own], [sh_first[5]], chip_arr, "win")
    w_ssem, w_rsem, w_srcs, w_lands, w_token = _split_start(
        _scatter_copies, 3, outs[:1], [lax.empty((3,) + outs[0].shape[1:], BF16)], gx, "scatter_win_start")
    slots = lax.dynamic_update_slice(lax.empty((8,) + gp.shape, F32), gp[None], (2 * chip + c, 0, 0))
    p_ssem, p_rsem, p_srcs, p_lands, p_token = _split_start(_peer_copies, 7, [gp], [slots], w_token,
                                                            "small_allgather_start")
    shared = _split_wait(_sibling_copies, sh_ssem, sh_rsem, sh_first[:5], sh_first[5:], [p_token],
                         "rs_share_early_wait", only=(1, 2, 3, 4))
    adam_e = _adamw_shards(shared[1:5], shared[6:], big_w[1:], moments[0][1:], moments[1][1:], c_arr, "early")
    recv_win = _split_wait(_scatter_copies, w_ssem, w_rsem, w_srcs, w_lands, [adam_e[0]], "scatter_win_wait")[1:]
    win_final = _sum_chips([outs[1]], recv_win, "win")
    win_sib = _to_sibling(win_final, "rs_share_win")
    adam_w = _adamw_shards(win_final, win_sib, big_w[:1], moments[0][:1], moments[1][:1], c_arr, "win")
    big_g, big_d, big_m, big_v = [[adam_w[i]] + list(adam_e[4 * i:4 * i + 4]) for i in range(4)]

    gathered = _split_wait(_peer_copies, p_ssem, p_rsem, p_srcs, p_lands, [adam_w[0]], "small_allgather_wait")[1]
    flat = _small_sum_adamw(gathered, wp, mp, vp)
    small_out = [flat[8 * kind:8 * kind + 8] for kind in range(4)]
    total_loss = flat[-1][0, 0]

    def ordered(small8, big4):
        sg1, sg2, sg3, sg4, swp, ssc, srb, ssk = small8
        swp, srb = swp[None], srb.T
        bwin, bwout, bwg, bwu, bwd = big4[0].T[None], big4[1][None], big4[2].T[None], big4[3].T[None], big4[4][None]
        return [sg1, bwin, swp, ssc, srb, ssk, bwout, sg2, sg3, bwg, bwu, bwd, sg4]

    res = [total_loss, gx[None]]
    for sm, bg in zip(small_out, (big_g, big_d, big_m, big_v)):
        res += ordered(sm, bg)
    return tuple(res)
```

```python
import numpy as np
import jax
import jax.numpy as jnp
from jax import lax
from jax.experimental import pallas as pl
from jax.experimental.pallas import tpu as pltpu

F32 = jnp.float32
BF16 = jnp.bfloat16

D = 1024
POOL_W = 512
GROUP = 128
WINDOWS = (2, 4, 8, 16)
HALO = 128
NQ = 8
BLK = 128
NBUCKET = 32
IN_W = 1280
DFF = 2816
NSH = 4
FS = DFF // NSH
WIN_S = IN_W // NSH
WOUT_S = D // NSH
EPS = 1e-6
NEG = -1e30
SCALE = 0.125

ADAM_LR = 0.001
ADAM_B1 = 0.9
ADAM_B2 = 0.999
ADAM_EPS = 1e-08
ADAM_WD = 0.01
ADAM_STEP = 10

TM = 512
TM_IN = 1024
TM_POOL = 512
TM_FFN = 512
TM_FFN_FWD = 1024
FFN_ROWS = 256
VMEM_LIMIT = 60 * 1024 * 1024

MESH_T = pl.DeviceIdType.MESH
ANY = pl.BlockSpec(memory_space=pl.ANY)


def _cp(n_grid=0, **kw):
    sem = ("arbitrary",) * n_grid if n_grid else None
    return pltpu.CompilerParams(dimension_semantics=sem, vmem_limit_bytes=VMEM_LIMIT, **kw)


def _dot(a, b):
    return jnp.dot(a, b, preferred_element_type=F32)


def _dot_nt(a, b):
    return lax.dot_general(a, b, (((1,), (1,)), ((), ())), preferred_element_type=F32)


def _dot_tn(a, b):
    return lax.dot_general(a, b, (((0,), (0,)), ((), ())), preferred_element_type=F32)


def _rms(x):
    r = lax.rsqrt(jnp.mean(x * x, axis=-1, keepdims=True) + EPS)
    return r, x * r


def _rms_bwd(r, n, g, dout):
    dn = dout * g
    dx = r * (dn - n * jnp.mean(dn * n, axis=-1, keepdims=True))
    dg = jnp.sum(dout * n, axis=0, keepdims=True)
    return dx, dg


def _split(x, n):
    parts = []
    r = x
    for _ in range(n):
        p = r.astype(BF16)
        parts.append(p)
        r = r - p.astype(F32)
    return parts


def _band_dot(a, x, n):
    acc = None
    for p in _split(x, n):
        t = _dot(a, p)
        acc = t if acc is None else acc + t
    return acc


def _bucket_table():
    qi = np.arange(BLK)[None, :]
    kj = np.arange(2 * BLK)[:, None]
    dist = qi + BLK - kj
    n = np.maximum(dist, 0)
    nf = np.maximum(n, 1).astype(np.float32)
    large = 16 + (np.log(nf / np.float32(16)) / np.float32(np.log(128 / 16)) * np.float32(16)).astype(np.int32)
    large = np.minimum(large, NBUCKET - 1)
    return np.where(n < 16, n, large).astype(np.int32)


def _prenorm(x, g1):
    s = x.shape[0]
    tm = min(TM_IN, s)

    def body(x_ref, g_ref, h_ref):
        _, n = _rms(x_ref[...])
        h_ref[...] = (n * g_ref[...]).astype(BF16)

    row = pl.BlockSpec((tm, D), lambda i: (i, 0))
    return pl.pallas_call(
        body, name="prenorm", grid=(s // tm,),
        in_specs=[row, pl.BlockSpec((1, D), lambda i: (0, 0))], out_specs=row,
        out_shape=jax.ShapeDtypeStruct((s, D), BF16),
        compiler_params=_cp(1))(x, g1)


def _inproj_fwd(h1, w_in):
    s = h1.shape[0]
    tm = min(TM_IN, s)

    def body(h_ref, w_ref, u_ref, q_ref, k_ref, v_ref):
        proj = _dot_nt(h_ref[...], w_ref[...])
        u_ref[...] = proj[:, :512]
        q_ref[...] = proj[:, 512:1024].astype(BF16)
        k_ref[...] = proj[:, 1024:1152].astype(BF16)
        v_ref[...] = proj[:, 1152:1280].astype(BF16)

    row = lambda w: pl.BlockSpec((tm, w), lambda i: (i, 0))
    return pl.pallas_call(
        body, name="inproj_fwd", grid=(s // tm,),
        in_specs=[row(D), pl.BlockSpec((IN_W, D), lambda i: (0, 0))],
        out_specs=[row(512), row(512), row(128), row(128)],
        out_shape=[jax.ShapeDtypeStruct((s, 512), F32), jax.ShapeDtypeStruct((s, 512), BF16),
                   jax.ShapeDtypeStruct((s, 128), BF16), jax.ShapeDtypeStruct((s, 128), BF16)],
        compiler_params=_cp(1))(h1, w_in)


def _window_sums(ext, w, lag_sign, tm, terms):
    rows = lax.broadcasted_iota(jnp.int32, (HALO, 2 * HALO), 0)
    cols = lax.broadcasted_iota(jnp.int32, (HALO, 2 * HALO), 1)
    d = rows + HALO - cols if lag_sign > 0 else cols - rows
    band = jnp.where((d >= 0) & (d < w), 1.0, 0.0).astype(BF16)
    return jnp.concatenate([_band_dot(band, ext[r:r + 2 * HALO], terms) for r in range(0, tm, HALO)], axis=0)


def _pooled(ext, cur, t0, tm):
    t = t0 + lax.broadcasted_iota(jnp.int32, (tm, GROUP), 0)
    out = []
    for g, w in enumerate(WINDOWS):
        sl = slice(g * GROUP, (g + 1) * GROUP)
        cnt = jnp.minimum(t + 1, w).astype(F32)
        out.append(_window_sums(ext[:, sl], w, 1, tm, 2) / cnt - cur[:, sl])
    return out


def _pool_fwd(u, w_pool, pool_scale):
    s = u.shape[0]
    tm = min(TM_POOL, s)
    hb = tm // HALO

    def body(uc_ref, uh_ref, wp_ref, sc_ref, o_ref, pooled_ref):
        i = pl.program_id(0)
        cur = uc_ref[...]
        halo = jnp.where(i > 0, uh_ref[...], 0.0)
        ext = jnp.concatenate([halo, cur], axis=0)
        pooled = _pooled(ext, cur, i * tm, tm)
        for g in range(4):
            sl = slice(g * GROUP, (g + 1) * GROUP)
            pb = pooled[g].astype(BF16)
            pooled_ref[:, sl] = pb
            o_ref[:, sl] = (_dot(pb, wp_ref[g]) * sc_ref[:, sl]).astype(BF16)

    row = pl.BlockSpec((tm, 512), lambda i: (i, 0))
    return pl.pallas_call(
        body, name="pool_fwd", grid=(s // tm,),
        in_specs=[row, pl.BlockSpec((HALO, 512), lambda i: (jnp.maximum(i * hb - 1, 0), 0)),
                  pl.BlockSpec((4, GROUP, GROUP), lambda i: (0, 0, 0)),
                  pl.BlockSpec((1, 512), lambda i: (0, 0))],
        out_specs=[row, row],
        out_shape=[jax.ShapeDtypeStruct((s, 512), BF16)] * 2,
        compiler_params=_cp(1))(u, u, w_pool, pool_scale)


def _bias_table(bucket, rel_bias):
    def body(b_ref, rb_ref, o_ref):
        bucket_v = b_ref[...]
        key = lax.broadcasted_iota(jnp.int32, (2 * BLK, BLK), 0)
        dist = lax.broadcasted_iota(jnp.int32, (2 * BLK, BLK), 1) + BLK - key
        inwin = (dist >= 0) & (dist < BLK)
        for h in range(NQ):
            acc = jnp.zeros((2 * BLK, BLK), F32)
            for b in range(NBUCKET):
                acc = jnp.where(bucket_v == b, rb_ref[b, h], acc)
            rest = jnp.where(inwin, acc, NEG)
            o_ref[1, h] = rest
            o_ref[0, h] = jnp.where(key >= BLK, rest, NEG)

    return pl.pallas_call(
        body, name="bias_table",
        in_specs=[pl.BlockSpec(memory_space=pltpu.VMEM), pl.BlockSpec(memory_space=pltpu.SMEM)],
        out_specs=pl.BlockSpec(memory_space=pltpu.VMEM),
        out_shape=jax.ShapeDtypeStruct((2, NQ, 2 * BLK, BLK), F32),
        compiler_params=_cp())(bucket, rel_bias)


HD = 64


def _kv_band(ref, n, transposed):
    pstart = pl.multiple_of(jnp.maximum(n - 1, 0) * BLK, BLK)
    cstart = pl.multiple_of(n * BLK, BLK)
    lo = lax.broadcasted_iota(jnp.int32, (2 * BLK, 128), 1) < HD
    band = jnp.concatenate([ref[pl.ds(pstart, BLK), :], ref[pl.ds(cstart, BLK), :]], axis=0).astype(F32)
    rot = pltpu.roll(band, HD, axis=1)
    halves = ((jnp.where(lo, band, 0.0), jnp.where(lo, 0.0, rot)), (jnp.where(lo, rot, 0.0), jnp.where(lo, 0.0, band)))
    if transposed:
        out = [jnp.concatenate([a.T, b.T], axis=1).astype(BF16) for a, b in halves]
    else:
        out = [jnp.concatenate([a, b], axis=0).astype(BF16) for a, b in halves]
    return out, (lo, pstart, cstart)


def _pair_probs(qt, kk, bias, sk_ref, p):
    sink = jnp.concatenate([jnp.full((1, 1, BLK), sk_ref[0, 2 * p + e], F32) for e in range(2)], axis=0)
    s = _dot_nt(kk, qt).reshape(2, 2 * BLK, BLK) + bias
    m = jnp.maximum(jnp.max(s, axis=1, keepdims=True), sink)
    pr = jnp.exp(s - m)
    es = jnp.exp(sink - m)
    inv = 1.0 / (jnp.sum(pr, axis=1, keepdims=True) + es)
    return pr * inv, es * inv


def _attn_fwd(q, k, v, bias, sinks):
    s = q.shape[0]
    nblk = s // BLK

    def body(q_ref, k_ref, v_ref, b_ref, sk_ref, o_ref, prob_ref, ps_ref):
        n = pl.program_id(0)
        kk, _ = _kv_band(k_ref, n, False)
        vt, _ = _kv_band(v_ref, n, True)
        bidx = jnp.minimum(n, 1)
        for p in range(4):
            h = p // 2
            qt = (q_ref[:, p * 128:(p + 1) * 128].astype(F32) * SCALE).astype(BF16)
            prob, ps = _pair_probs(qt, kk[h], b_ref[bidx, pl.ds(2 * p, 2)], sk_ref, p)
            pb = prob.astype(BF16)
            prob_ref[pl.ds(2 * p, 2)] = pb
            ps_ref[pl.ds(2 * p, 2), :] = ps.reshape(2, BLK)
            acc_t = _dot(vt[h], pb.reshape(4 * BLK, BLK))
            o_ref[:, p * 128:(p + 1) * 128] = acc_t.T.astype(BF16)

    return pl.pallas_call(
        body, name="attn_fwd", grid=(nblk,),
        in_specs=[pl.BlockSpec((BLK, 512), lambda i: (i, 0)),
                  pl.BlockSpec((s, 128), lambda i: (0, 0)),
                  pl.BlockSpec((s, 128), lambda i: (0, 0)),
                  pl.BlockSpec((2, NQ, 2 * BLK, BLK), lambda i: (0, 0, 0, 0)),
                  pl.BlockSpec(memory_space=pltpu.SMEM)],
        out_specs=[pl.BlockSpec((BLK, 512), lambda i: (i, 0)),
                   pl.BlockSpec((None, NQ, 2 * BLK, BLK), lambda i: (i, 0, 0, 0)),
                   pl.BlockSpec((None, NQ, BLK), lambda i: (i, 0, 0))],
        out_shape=[jax.ShapeDtypeStruct((s, 512), BF16), jax.ShapeDtypeStruct((nblk, NQ, 2 * BLK, BLK), BF16),
                   jax.ShapeDtypeStruct((nblk, NQ, BLK), F32)],
        compiler_params=_cp(1))(q, k, v, bias, sinks)


def _outproj_fwd(pool_o, attn_o, w_out, x, g2, g3):
    s = x.shape[0]
    tm = min(TM, s)

    def body(p_ref, a_ref, w_ref, x_ref, g2_ref, g3_ref, mix_ref, x1_ref, h2_ref):
        mix = _dot(p_ref[...], w_ref[0:512, :]) + _dot(a_ref[...], w_ref[512:1024, :])
        mix_ref[...] = mix
        _, n2 = _rms(mix)
        x1 = x_ref[...] + n2 * g2_ref[...]
        x1_ref[...] = x1
        _, n3 = _rms(x1)
        h2_ref[...] = (n3 * g3_ref[...]).astype(BF16)

    row = lambda w: pl.BlockSpec((tm, w), lambda i: (i, 0))
    vec = pl.BlockSpec((1, D), lambda i: (0, 0))
    return pl.pallas_call(
        body, name="outproj_fwd", grid=(s // tm,),
        in_specs=[row(512), row(512), pl.BlockSpec((D, D), lambda i: (0, 0)), row(D), vec, vec],
        out_specs=[row(D), row(D), row(D)],
        out_shape=[jax.ShapeDtypeStruct((s, D), F32), jax.ShapeDtypeStruct((s, D), F32),
                   jax.ShapeDtypeStruct((s, D), BF16)],
        compiler_params=_cp(1))(pool_o, attn_o, w_out, x, g2, g3)


def _silu_parts(g):
    sg = jax.nn.sigmoid(g)
    return sg, g * sg


def _ffn_fwd(h2, wg, wu, wd, x1, target, g4):
    s = h2.shape[0]
    tm = min(TM_FFN_FWD, s)

    def body(h_ref, wg_ref, wu_ref, wd_ref, x1_ref, t_ref, g4_ref,
             gate_ref, up_ref, a_ref, df_ref, dy_ref, loss_ref, gg4_ref, f_acc):
        i = pl.program_id(0)
        j = pl.program_id(1)
        first = j == 0
        chunks = [pl.ds(r, min(FFN_ROWS, tm)) for r in range(0, tm, FFN_ROWS)]
        project = lambda rows: (_dot_nt(h_ref[rows, :], wg_ref[...]), _dot_nt(h_ref[rows, :], wu_ref[...]))
        ahead = [project(chunks[0])]
        for k, rows in enumerate(chunks):
            if k + 1 < len(chunks):
                ahead.append(project(chunks[k + 1]))
            gate, up = ahead[k]
            gate_ref[rows, :] = gate.astype(BF16)
            up_ref[rows, :] = up.astype(BF16)
            _, sl = _silu_parts(gate)
            a = (sl * up).astype(BF16)
            a_ref[rows, :] = a
            contrib = _dot(a, wd_ref[...])
            f_acc[rows, :] = jnp.where(first, contrib, f_acc[rows, :] + contrib)

        @pl.when((i == 0) & (j == 0))
        def _():
            loss_ref[...] = jnp.zeros_like(loss_ref)
            gg4_ref[...] = jnp.zeros_like(gg4_ref)

        @pl.when(j == NSH - 1)
        def _():
            r4, n4 = _rms(f_acc[...])
            g4v = g4_ref[...]
            err = x1_ref[...] + n4 * g4v - t_ref[...]
            loss_ref[...] += (0.5 / D) * jnp.sum(err * err).reshape(1, 1)
            dy = err * (1.0 / D)
            dy_ref[...] = dy
            df, dg = _rms_bwd(r4, n4, g4v, dy)
            gg4_ref[...] += dg
            df_ref[...] = df.astype(BF16)

    row = lambda w: pl.BlockSpec((tm, w), lambda i, j: (i, 0))
    sh = lambda r, c: pl.BlockSpec((None, r, c), lambda i, j: (j, 0, 0))
    act = pl.BlockSpec((None, tm, FS), lambda i, j: (j, i, 0))
    vec = pl.BlockSpec((1, D), lambda i, j: (0, 0))
    return pl.pallas_call(
        body, name="ffn_fwd", grid=(s // tm, NSH),
        in_specs=[row(D), sh(FS, D), sh(FS, D), sh(FS, D), row(D), row(D), vec],
        out_specs=[act, act, act, row(D), row(D), pl.BlockSpec((1, 1), lambda i, j: (0, 0)), vec],
        out_shape=[jax.ShapeDtypeStruct((NSH, s, FS), BF16)] * 3
        + [jax.ShapeDtypeStruct((s, D), BF16), jax.ShapeDtypeStruct((s, D), F32),
           jax.ShapeDtypeStruct((1, 1), F32), jax.ShapeDtypeStruct((1, D), F32)],
        scratch_shapes=[pltpu.VMEM((tm, D), F32)],
        compiler_params=_cp(2))(h2, wg, wu, wd, x1, target, g4)


def _ffn_bwd_act(df, gate, up, wg, wu, wd, dy, x1, mix, g3, g2):
    s = df.shape[0]
    tm = min(TM_FFN, s)

    def body(df_ref, gate_ref, up_ref, wg_ref, wu_ref, wd_ref, dy_ref, x1_ref, mix_ref, g3_ref, g2_ref,
             dgate_ref, dup_ref, dx1_ref, dmix_ref, gg3_ref, gg2_ref, acc):
        j = pl.program_id(0)
        i = pl.program_id(1)
        first = j == 0
        tile = pl.multiple_of(i * tm, tm)
        chunks = [pl.ds(r, min(FFN_ROWS, tm)) for r in range(0, tm, FFN_ROWS)]
        das = [_dot_nt(df_ref[chunks[0], :], wd_ref[...])]
        for k, rows in enumerate(chunks):
            if k + 1 < len(chunks):
                das.append(_dot_nt(df_ref[chunks[k + 1], :], wd_ref[...]))
            da = das[k]
            g = gate_ref[rows, :].astype(F32)
            u = up_ref[rows, :].astype(F32)
            sg, sl = _silu_parts(g)
            dgate = (da * u * (sg * (1.0 + g * (1.0 - sg)))).astype(BF16)
            dup = (da * sl).astype(BF16)
            dgate_ref[rows, :] = dgate
            dup_ref[rows, :] = dup
            contrib = _dot(dgate, wg_ref[...]) + _dot(dup, wu_ref[...])
            acc_rows = pl.ds(tile + k * FFN_ROWS, rows.size)
            acc[acc_rows, :] = jnp.where(first, contrib, acc[acc_rows, :] + contrib)

        @pl.when((i == 0) & (j == 0))
        def _():
            gg3_ref[...] = jnp.zeros_like(gg3_ref)
            gg2_ref[...] = jnp.zeros_like(gg2_ref)

        @pl.when(j == NSH - 1)
        def _():
            r3, n3 = _rms(x1_ref[...])
            dx1n, dg3 = _rms_bwd(r3, n3, g3_ref[...], acc[pl.ds(tile, tm), :])
            dx1 = dy_ref[...] + dx1n
            dx1_ref[...] = dx1
            gg3_ref[...] += dg3
            r2, n2 = _rms(mix_ref[...])
            dmix, dg2 = _rms_bwd(r2, n2, g2_ref[...], dx1)
            gg2_ref[...] += dg2
            dmix_ref[...] = dmix.astype(BF16)

    row = pl.BlockSpec((tm, D), lambda j, i: (i, 0))
    last = pl.BlockSpec((tm, D), lambda j, i: (i * (j // (NSH - 1)), 0))
    sh = pl.BlockSpec((None, FS, D), lambda j, i: (j, 0, 0))
    act = pl.BlockSpec((None, tm, FS), lambda j, i: (j, i, 0))
    vec = pl.BlockSpec((1, D), lambda j, i: (0, 0))
    return pl.pallas_call(
        body, name="ffn_bwd_act", grid=(NSH, s // tm),
        in_specs=[row, act, act, sh, sh, sh, last, last, last, vec, vec],
        out_specs=[act, act, last, last, vec, vec],
        out_shape=[jax.ShapeDtypeStruct((NSH, s, FS), BF16), jax.ShapeDtypeStruct((NSH, s, FS), BF16),
                   jax.ShapeDtypeStruct((s, D), F32), jax.ShapeDtypeStruct((s, D), BF16),
                   jax.ShapeDtypeStruct((1, D), F32), jax.ShapeDtypeStruct((1, D), F32)],
        scratch_shapes=[pltpu.VMEM((s, D), F32)],
        compiler_params=_cp(2))(df, gate, up, wg, wu, wd, dy, x1, mix, g3, g2)


SMEM_SPEC = pl.BlockSpec(memory_space=pltpu.SMEM)


def _emit_halves(acc_ref, row0, rows, c, own_ref, oth_ref):
    half = rows // 2
    own_ref[...] = acc_ref[pl.ds(row0 + pl.multiple_of(c * half, 8), half), :]
    oth_ref[...] = acc_ref[pl.ds(row0 + pl.multiple_of((1 - c) * half, 8), half), :].astype(BF16)


def _half_shapes(rows):
    return [jax.ShapeDtypeStruct((NSH, rows // 2, D), F32), jax.ShapeDtypeStruct((NSH, rows // 2, D), BF16)]


def _ffn_bwd_w(h2, act_a, dgate, dup, df, c_arr):
    s = h2.shape[0]
    tm = min(TM_FFN_FWD, s)
    nt = s // tm

    def body(c_ref, h_ref, a_ref, dgate_ref, dup_ref, df_ref, *rest):
        outs, accs = rest[:6], rest[6:]
        i = pl.program_id(1)

        @pl.when(i == 0)
        def _():
            for acc in accs:
                acc[...] = jnp.zeros_like(acc)

        h = h_ref[...]
        accs[0][...] += _dot_tn(dgate_ref[...], h)
        accs[1][...] += _dot_tn(dup_ref[...], h)
        accs[2][...] += _dot_tn(a_ref[...], df_ref[...])

        @pl.when(i == nt - 1)
        def _():
            for t, acc in enumerate(accs):
                _emit_halves(acc, 0, FS, c_ref[0], outs[2 * t], outs[2 * t + 1])

    row = lambda w: pl.BlockSpec((tm, w), lambda j, i: (i, 0))
    act = pl.BlockSpec((None, tm, FS), lambda j, i: (j, i, 0))
    half = pl.BlockSpec((None, FS // 2, D), lambda j, i: (j, 0, 0))
    return pl.pallas_call(
        body, name="ffn_bwd_w", grid=(NSH, nt),
        in_specs=[SMEM_SPEC, row(D), act, act, act, row(D)],
        out_specs=[half] * 6,
        out_shape=_half_shapes(FS) * 3,
        scratch_shapes=[pltpu.VMEM((FS, D), F32)] * 3,
        compiler_params=_cp(2))(c_arr, h2, act_a, dgate, dup, df)


def _outproj_bwd(dmix, w_out, pool_o, attn_o, c_arr, dep=None):
    s = dmix.shape[0]
    tm = min(TM, s)
    nt = s // tm

    def body(c_ref, dm_ref, w_ref, p_ref, a_ref, *rest):
        dpool_ref, dattn_ref, own_ref, oth_ref, gw_ref = rest[-5:]
        i = pl.program_id(0)

        @pl.when(i == 0)
        def _():
            gw_ref[...] = jnp.zeros_like(gw_ref)

        dm = dm_ref[...]
        dpool_ref[...] = _dot_nt(dm, w_ref[0:512, :])
        dattn_ref[...] = _dot_nt(dm, w_ref[512:1024, :]).astype(BF16)
        gw_ref[0:512, :] += _dot_tn(p_ref[...], dm)
        gw_ref[512:1024, :] += _dot_tn(a_ref[...], dm)

        @pl.when(i == nt - 1)
        def _():
            for k in range(NSH):
                _emit_halves(gw_ref, k * WOUT_S, WOUT_S, c_ref[0], own_ref.at[k], oth_ref.at[k])

    row = lambda w: pl.BlockSpec((tm, w), lambda i: (i, 0))
    full = pl.BlockSpec((D, D), lambda i: (0, 0))
    half = pl.BlockSpec((NSH, WOUT_S // 2, D), lambda i: (0, 0, 0))
    return pl.pallas_call(
        body, name="outproj_bwd", grid=(nt,),
        in_specs=[SMEM_SPEC, row(D), full, row(512), row(512)] + ([] if dep is None else [ANY]),
        out_specs=[row(512), row(512), half, half],
        out_shape=[jax.ShapeDtypeStruct((s, 512), F32), jax.ShapeDtypeStruct((s, 512), BF16)] + _half_shapes(WOUT_S),
        scratch_shapes=[pltpu.VMEM((D, D), F32)],
        compiler_params=_cp(1))(c_arr, dmix, w_out, pool_o, attn_o, *([] if dep is None else [dep]))


def _pool_bwd(pooled, dpool, w_pool, pool_scale, dep=None):
    s = pooled.shape[0]
    tm = min(TM_POOL, s)
    hb = tm // HALO
    nt = s // tm
    last_halo = s // HALO - 1

    def body(p_ref, dc_ref, dn_ref, wp_ref, sc_ref, *rest):
        du_ref, gwp_ref, gsc_ref = rest[-3:]
        i = pl.program_id(0)

        @pl.when(i == 0)
        def _():
            gwp_ref[...] = jnp.zeros_like(gwp_ref)
            gsc_ref[...] = jnp.zeros_like(gsc_ref)

        dcur = dc_ref[...]
        dnext = jnp.where(i < nt - 1, dn_ref[...], 0.0)
        dext = jnp.concatenate([dcur, dnext], axis=0)
        t_ext = i * tm + lax.broadcasted_iota(jnp.int32, (tm + HALO, GROUP), 0)
        for g, w in enumerate(WINDOWS):
            sl = slice(g * GROUP, (g + 1) * GROUP)
            pb = p_ref[:, sl]
            wp = wp_ref[g]
            mixed = _dot(pb, wp)
            gsc_ref[:, sl] += jnp.sum(dcur[:, sl] * mixed, axis=0, keepdims=True)
            dmixed = (dext[:, sl] * sc_ref[:, sl]).astype(BF16)
            gwp_ref[g] += _dot_tn(pb, dmixed[0:tm])
            dpooled = _dot_nt(dmixed, wp)
            z = dpooled / jnp.minimum(t_ext + 1, w).astype(F32)
            du_ref[:, sl] = (_window_sums(z, w, -1, tm, 2) - dpooled[0:tm]).astype(BF16)

    return pl.pallas_call(
        body, name="pool_bwd", grid=(nt,),
        in_specs=[pl.BlockSpec((tm, 512), lambda i: (i, 0)),
                  pl.BlockSpec((tm, 512), lambda i: (i, 0)),
                  pl.BlockSpec((HALO, 512), lambda i: (jnp.minimum((i + 1) * hb, last_halo), 0)),
                  pl.BlockSpec((4, GROUP, GROUP), lambda i: (0, 0, 0)),
                  pl.BlockSpec((1, 512), lambda i: (0, 0))] + ([] if dep is None else [ANY]),
        out_specs=[pl.BlockSpec((tm, 512), lambda i: (i, 0)),
                   pl.BlockSpec((4, GROUP, GROUP), lambda i: (0, 0, 0)),
                   pl.BlockSpec((1, 512), lambda i: (0, 0))],
        out_shape=[jax.ShapeDtypeStruct((s, 512), BF16), jax.ShapeDtypeStruct((4, GROUP, GROUP), F32),
                   jax.ShapeDtypeStruct((1, 512), F32)],
        compiler_params=_cp(1))(pooled, dpool, dpool, w_pool, pool_scale, *([] if dep is None else [dep]))


def _attn_bwd(q, k, v, do, probs, sink_share, bucket, dep=None):
    s = q.shape[0]
    nblk = s // BLK

    def body(q_ref, do_ref, k_ref, v_ref, prob_ref, ps_ref, bk_ref, *rest):
        dq_ref, dk_ref, dv_ref, grb_ref, gsk_ref, dss_ref = rest[-6:]
        n = pl.program_id(0)

        @pl.when(n == 0)
        def _():
            dk_ref[...] = jnp.zeros_like(dk_ref)
            dv_ref[...] = jnp.zeros_like(dv_ref)
            dss_ref[...] = jnp.zeros_like(dss_ref)
            gsk_ref[...] = jnp.zeros_like(gsk_ref)

        kt, (lo, pstart, cstart) = _kv_band(k_ref, n, True)
        vv, _ = _kv_band(v_ref, n, False)
        lo_q = lax.broadcasted_iota(jnp.int32, (BLK, 128), 1) < HD
        dkk = [jnp.zeros((2 * BLK, 128), F32), jnp.zeros((2 * BLK, 128), F32)]
        dvv = [jnp.zeros((2 * BLK, 128), F32), jnp.zeros((2 * BLK, 128), F32)]

        def by_head(t):
            return jnp.concatenate([jnp.where(lo_q, t, 0.0), jnp.where(lo_q, 0.0, t)], axis=0).astype(BF16)

        for p in range(4):
            h = p // 2
            qt = q_ref[:, p * 128:(p + 1) * 128].astype(F32) * SCALE
            dot = do_ref[:, p * 128:(p + 1) * 128]
            pb = prob_ref[pl.ds(2 * p, 2)]
            prob = pb.astype(F32)
            ps = ps_ref[pl.ds(2 * p, 2), :].reshape(2, 1, BLK)
            dp = _dot_nt(vv[h], dot).reshape(2, 2 * BLK, BLK)
            delta = jnp.sum(prob * dp, axis=1, keepdims=True)
            ds = prob * (dp - delta)
            sink_part = ps * delta
            for e in range(2):
                gs = -jnp.sum(sink_part[e]).reshape(1, 1)
                gsk_ref[pl.ds(2 * p + e, 1), :] += jnp.broadcast_to(gs, (1, 128))
            dss_ref[pl.ds(2 * p, 2)] += ds
            dsb = ds.astype(BF16)
            dq_t = _dot(kt[h], dsb.reshape(4 * BLK, BLK))
            dq_ref[:, p * 128:(p + 1) * 128] = (dq_t.T * SCALE).astype(BF16)
            dkk[h] = dkk[h] + _dot(jnp.concatenate([dsb[0], dsb[1]], axis=1), by_head(qt))
            dvv[h] = dvv[h] + _dot(jnp.concatenate([pb[0], pb[1]], axis=1), by_head(dot.astype(F32)))
        for acc, ref in ((dkk, dk_ref), (dvv, dv_ref)):
            f0 = acc[0] + pltpu.roll(acc[0], HD, axis=1)
            f1 = acc[1] + pltpu.roll(acc[1], HD, axis=1)
            band = jnp.where(lo, f0, f1)
            ref[pl.ds(pstart, BLK), :] += band[0:BLK]
            ref[pl.ds(cstart, BLK), :] += band[BLK:2 * BLK]

        @pl.when(n == nblk - 1)
        def _():
            _relbias_grad(dss_ref, bk_ref[...], grb_ref)

    blk = pl.BlockSpec((BLK, 512), lambda i: (i, 0))
    kv = pl.BlockSpec((s, 128), lambda i: (0, 0))
    row8 = pl.BlockSpec((NQ, 128), lambda i: (0, 0))
    return pl.pallas_call(
        body, name="attn_bwd", grid=(nblk,),
        in_specs=[blk, blk, kv, kv, pl.BlockSpec((None, NQ, 2 * BLK, BLK), lambda i: (i, 0, 0, 0)),
                  pl.BlockSpec((None, NQ, BLK), lambda i: (i, 0, 0)), pl.BlockSpec((2 * BLK, BLK), lambda i: (0, 0))]
        + ([] if dep is None else [ANY]),
        out_specs=[blk, kv, kv, row8, row8],
        out_shape=[jax.ShapeDtypeStruct((s, 512), BF16), jax.ShapeDtypeStruct((s, 128), F32),
                   jax.ShapeDtypeStruct((s, 128), F32), jax.ShapeDtypeStruct((NQ, 128), F32),
                   jax.ShapeDtypeStruct((NQ, 128), F32)],
        scratch_shapes=[pltpu.VMEM((NQ, 2 * BLK, BLK), F32)],
        compiler_params=_cp(1))(q, do, k, v, probs, sink_share, bucket, *([] if dep is None else [dep]))


def _relbias_grad(ds_ref, bucket_v, o_ref):
    lane = lax.broadcasted_iota(jnp.int32, (NQ, 128), 1)
    head_row = lax.broadcasted_iota(jnp.int32, (NQ, BLK), 0)
    acc = jnp.zeros((NQ, 128), F32)
    for b in range(NBUCKET):
        sel = bucket_v == b
        stack = jnp.zeros((NQ, BLK), F32)
        for h in range(NQ):
            col_sums = jnp.sum(jnp.where(sel, ds_ref[h], 0.0), axis=0, keepdims=True)
            stack = jnp.where(head_row == h, col_sums, stack)
        acc = acc + jnp.where(lane == b, jnp.sum(stack, axis=1, keepdims=True), 0.0)
    o_ref[...] = acc


def _inproj_bwd(x, g1, du, dq, dk, dv, w_in, dx1, c_arr):
    s = x.shape[0]
    tm = min(TM, s)
    nt = s // tm
    cols = ((0, 512), (512, 1024), (1024, 1152), (1152, 1280))

    def body(c_ref, x_ref, g_ref, du_ref, dq_ref, dk_ref, dv_ref, w_ref, dx1_ref,
             gx_ref, own_ref, oth_ref, gg_ref, gw_ref):
        i = pl.program_id(0)

        @pl.when(i == 0)
        def _():
            gw_ref[...] = jnp.zeros_like(gw_ref)
            gg_ref[...] = jnp.zeros_like(gg_ref)

        parts = (du_ref[...], dq_ref[...], dk_ref[...].astype(BF16), dv_ref[...].astype(BF16))
        dh = None
        for (a, b), dpart in zip(cols, parts):
            t = _dot(dpart, w_ref[a:b, :])
            dh = t if dh is None else dh + t
        gv = g_ref[...]
        r1, n1 = _rms(x_ref[...])
        h = (n1 * gv).astype(BF16)
        for (a, b), dpart in zip(cols, parts):
            gw_ref[a:b, :] += _dot_tn(dpart, h)
        dx, dg = _rms_bwd(r1, n1, gv, dh)
        gg_ref[...] += dg
        gx_ref[...] = dx1_ref[...] + dx

        @pl.when(i == nt - 1)
        def _():
            for k in range(NSH):
                _emit_halves(gw_ref, k * WIN_S, WIN_S, c_ref[0], own_ref.at[k], oth_ref.at[k])

    row = lambda w: pl.BlockSpec((tm, w), lambda i: (i, 0))
    vec = pl.BlockSpec((1, D), lambda i: (0, 0))
    full = pl.BlockSpec((IN_W, D), lambda i: (0, 0))
    half = pl.BlockSpec((NSH, WIN_S // 2, D), lambda i: (0, 0, 0))
    return pl.pallas_call(
        body, name="inproj_bwd", grid=(nt,),
        in_specs=[SMEM_SPEC, row(D), vec, row(512), row(512), row(128), row(128), full, row(D)],
        out_specs=[row(D), half, half, vec],
        out_shape=[jax.ShapeDtypeStruct((s, D), F32)] + _half_shapes(WIN_S) + [jax.ShapeDtypeStruct((1, D), F32)],
        scratch_shapes=[pltpu.VMEM((IN_W, D), F32)],
        compiler_params=_cp(1))(c_arr, x, g1, du, dq, dk, dv, w_in, dx1)


def _local_step(x, target, small, w_in, w_out, ffn_weights, c_arr, on_ffn_grads=None, on_wout_grads=None):
    g1, g2, g3, g4, w_pool, pool_scale, rel_bias, sinks = small
    bucket = jnp.asarray(_bucket_table())
    wp_bf = w_pool.astype(BF16)
    bias = _bias_table(bucket, rel_bias)
    h1 = _prenorm(x, g1)
    if callable(w_in):
        w_in = w_in(bias, h1)
    u, q, k, v = _inproj_fwd(h1, w_in)
    pool_o, pooled = _pool_fwd(u, wp_bf, pool_scale)
    attn_o, probs, sink_share = _attn_fwd(q, k, v, bias, sinks)
    g2_fwd = g2
    if callable(w_out):
        w_out, after = w_out(pool_o, attn_o)
        if after is not None:
            g2_fwd = g2 + after[:1, :1]
    mix, x1, h2 = _outproj_fwd(pool_o, attn_o, w_out, x, g2_fwd, g3)
    wg, wu, wd = ffn_weights(h2) if callable(ffn_weights) else ffn_weights
    gate, up, act_a, df, dy, loss, gg4 = _ffn_fwd(h2, wg, wu, wd, x1, target, g4)
    dgate, dup, dx1, dmix, gg3, gg2 = _ffn_bwd_act(df, gate, up, wg, wu, wd, dy, x1, mix, g3, g2)
    ffn_g = _ffn_bwd_w(h2, act_a, dgate, dup, df, c_arr)
    dep = None if on_ffn_grads is None else on_ffn_grads(ffn_g)
    dpool, dattn, wout_own, wout_oth = _outproj_bwd(dmix, w_out, pool_o, attn_o, c_arr, dep)
    dep = None if on_wout_grads is None else on_wout_grads(wout_own, wout_oth, dpool)
    du, gwp, gsc = _pool_bwd(pooled, dpool, wp_bf, pool_scale, dep)
    dq, dk, dv, grb, gsk = _attn_bwd(q, k, v, dattn, probs, sink_share, bucket, dep)
    gx, win_own, win_oth, gg1 = _inproj_bwd(x, g1, du, dq, dk, dv, w_in, dx1, c_arr)
    small_grads = (gg1, gg2, gg3, gg4, gwp, gsc, grb, gsk[:, 0].reshape(1, NQ))
    return loss, gx, small_grads, ((win_own, win_oth), (wout_own, wout_oth), ffn_g)


def _place():
    x, y, c = lax.axis_index("x"), lax.axis_index("y"), lax.axis_index("c")
    chips = ((1 - x, y), (x, 1 - y), (1 - x, 1 - y))
    return x, y, c, chips


def _remote(src, dst, ssem, rsem, dev):
    return pltpu.make_async_remote_copy(src_ref=src, dst_ref=dst, send_sem=ssem, recv_sem=rsem,
                                        device_id=dev, device_id_type=MESH_T)


def _to_sibling(arrs, name, after=()):
    nt, na = len(arrs), len(after)

    def body(*refs):
        srcs, dsts = refs[:nt], refs[nt + na:2 * nt + na]
        ssem, rsem = refs[2 * nt + na:]
        x, y, c, _ = _place()
        cps = [_remote(srcs[t], dsts[t], ssem.at[t], rsem.at[t], (x, y, 1 - c)) for t in range(nt)]
        for cp in cps:
            cp.start()
        for cp in cps:
            cp.wait()

    return pl.pallas_call(
        body, name=name, in_specs=[ANY] * (nt + na), out_specs=[ANY] * nt,
        out_shape=[jax.ShapeDtypeStruct(a.shape, a.dtype) for a in arrs],
        scratch_shapes=[pltpu.SemaphoreType.DMA((nt,)), pltpu.SemaphoreType.DMA((nt,))],
        compiler_params=_cp())(*arrs, *after)


HBM_SPEC = pl.BlockSpec(memory_space=pltpu.HBM)
SEM_SPEC = pl.BlockSpec(memory_space=pltpu.SEMAPHORE)
DATAFLOW = pltpu.SideEffectType.DATAFLOW_SIDE_EFFECTING


def _gather_copies(srcs, lands, ssem, rsem):
    x, y, c, chips = _place()
    me = 2 * x + y
    out = []
    for t in range(len(srcs)):
        half = srcs[t].shape[0] // 2
        mine = pl.ds(pl.multiple_of(c * half, 16), half)
        for j, (px, py) in enumerate(chips):
            k = 3 * t + j
            send = _remote(srcs[t].at[mine], lands[t].at[me, mine], ssem.at[k], rsem.at[k], (px, py, c))
            blk = lands[t].at[2 * px + py, mine]
            out.append((send, _remote(blk, blk, ssem.at[k], rsem.at[k], (px, py, c))))
    return out


def _scatter_copies(srcs, lands, ssem, rsem):
    x, y, c, chips = _place()
    out = []
    for t in range(len(srcs)):
        for j, (px, py) in enumerate(chips):
            k = 3 * t + j
            send = _remote(srcs[t].at[2 * px + py], lands[t].at[j], ssem.at[k], rsem.at[k], (px, py, c))
            out.append((send, _remote(lands[t].at[j], lands[t].at[j], ssem.at[k], rsem.at[k], (px, py, c))))
    return out


def _sibling_copies(srcs, lands, ssem, rsem):
    x, y, c, _ = _place()
    sib = (x, y, 1 - c)
    return [(_remote(srcs[t], lands[t], ssem.at[t], rsem.at[t], sib),
             _remote(lands[t], lands[t], ssem.at[t], rsem.at[t], sib)) for t in range(len(srcs))]


def _peer_copies(srcs, lands, ssem, rsem):
    x, y, c, _ = _place()
    me = 4 * x + 2 * y + c
    out = []
    for kk in range(1, 8):
        px, py, pc = x ^ (kk >> 2), y ^ ((kk >> 1) & 1), c ^ (kk & 1)
        send = _remote(srcs[0], lands[0].at[me], ssem.at[kk - 1], rsem.at[kk - 1], (px, py, pc))
        slot = lands[0].at[4 * px + 2 * py + pc]
        out.append((send, _remote(slot, slot, ssem.at[kk - 1], rsem.at[kk - 1], (px, py, pc))))
    return out


def _win_and_small_copies(srcs, lands, ssem, rsem):
    return (_scatter_copies(srcs[:1], lands[:1], ssem, rsem)
            + _peer_copies(srcs[1:], lands[1:], ssem.at[pl.ds(3, 7)], rsem.at[pl.ds(3, 7)]))


def _split_start(plan, ncopy, srcs, lands, after, name):
    ns, nbuf = len(srcs), len(srcs) + len(lands)
    extra = [] if after is None else [after]
    n_in = nbuf + len(extra)

    def body(*refs):
        ssem, rsem, token = refs[n_in], refs[n_in + 1], refs[-1]
        for send, _ in plan(refs[:ns], refs[ns:nbuf], ssem, rsem):
            send.start()
        token[...] = jnp.zeros_like(token)

    bufs = [pltpu.with_memory_space_constraint(a, pltpu.HBM) for a in list(srcs) + list(lands)]
    outs = pl.pallas_call(
        body, name=name, in_specs=[HBM_SPEC] * nbuf + [ANY] * len(extra),
        out_specs=[SEM_SPEC, SEM_SPEC] + [HBM_SPEC] * nbuf + [pl.BlockSpec(memory_space=pltpu.VMEM)],
        out_shape=[pltpu.SemaphoreType.DMA((ncopy,)), pltpu.SemaphoreType.DMA((ncopy,))]
        + [pltpu.HBM(a.shape, a.dtype) for a in bufs] + [jax.ShapeDtypeStruct((8, 128), F32)],
        input_output_aliases={i: 2 + i for i in range(nbuf)},
        compiler_params=pltpu.CompilerParams(has_side_effects=DATAFLOW))(*bufs, *extra)
    return outs[0], outs[1], outs[2:2 + ns], outs[2 + ns:2 + nbuf], outs[-1]


def _split_wait(plan, ssem, rsem, srcs, lands, after, name, only=None):
    ns, nbuf = len(srcs), len(srcs) + len(lands)

    def body(*refs):
        s_ref, r_ref = refs[nbuf], refs[nbuf + 1]
        for k, (send, recv) in enumerate(plan(refs[:ns], refs[ns:nbuf], s_ref, r_ref)):
            if only is None or k in only:
                send.wait_send()
                recv.wait_recv()

    outs = pl.pallas_call(
        body, name=name, in_specs=[HBM_SPEC] * nbuf + [SEM_SPEC, SEM_SPEC] + [ANY] * len(after),
        out_specs=[HBM_SPEC] * nbuf,
        out_shape=[pltpu.HBM(a.shape, a.dtype) for a in list(srcs) + list(lands)],
        input_output_aliases={i: i for i in range(nbuf)},
        compiler_params=pltpu.CompilerParams(has_side_effects=DATAFLOW))(*srcs, *lands, ssem, rsem, *after)
    return outs


def _gather_finish(lands, tag):
    nt = len(lands)

    def body(*refs):
        outs = refs[nt:2 * nt]
        dsend, drecv = refs[2 * nt:]
        x, y, c, chips = _place()
        sib = (x, y, 1 - c)
        sends = []
        for t in range(nt):
            half = outs[t].shape[1] // 2
            mine = pl.ds(pl.multiple_of(c * half, 16), half)
            for j, (px, py) in enumerate(chips):
                blk = outs[t].at[2 * px + py, mine]
                cp = _remote(blk, blk, dsend.at[t, j], drecv.at[t, j], sib)
                cp.start()
                sends.append(cp)
        for t in range(nt):
            half = outs[t].shape[1] // 2
            other = pl.ds(pl.multiple_of((1 - c) * half, 16), half)
            for j, (px, py) in enumerate(chips):
                blk = outs[t].at[2 * px + py, other]
                _remote(blk, blk, dsend.at[t, j], drecv.at[t, j], sib).wait_recv()
        for cp in sends:
            cp.wait_send()

    return pl.pallas_call(
        body, name="gather_finish_" + tag, in_specs=[ANY] * nt, out_specs=[ANY] * nt,
        out_shape=[jax.ShapeDtypeStruct(a.shape, a.dtype) for a in lands],
        input_output_aliases={t: t for t in range(nt)},
        scratch_shapes=[pltpu.SemaphoreType.DMA((nt, 3)), pltpu.SemaphoreType.DMA((nt, 3))],
        compiler_params=_cp())(*lands)


def _chip_partial(owns, recv, chip_arr, tag):
    nt = len(owns)

    def body(chip_ref, *refs):
        k = pl.program_id(0)
        for t in range(nt):
            p = refs[t][...] + refs[nt + t][...].astype(F32)
            refs[2 * nt + t][...] = p.astype(BF16)

            @pl.when(k == chip_ref[0])
            def _():
                refs[3 * nt + t][...] = p

    blk_specs = [pl.BlockSpec((None,) + a.shape[1:], lambda k, chip_ref: (k, 0, 0)) for a in owns]
    own_specs = [pl.BlockSpec(a.shape[1:], lambda k, chip_ref: (0, 0)) for a in owns]
    return pl.pallas_call(
        body, name="rs_chip_partial_" + tag,
        grid_spec=pltpu.PrefetchScalarGridSpec(num_scalar_prefetch=1, grid=(NSH,), in_specs=blk_specs * 2,
                                               out_specs=blk_specs + own_specs),
        out_shape=[jax.ShapeDtypeStruct(a.shape, BF16) for a in owns]
        + [jax.ShapeDtypeStruct(a.shape[1:], F32) for a in owns],
        compiler_params=_cp(1))(chip_arr, *owns, *recv)


def _sum_chips(own, recv, tag, after=()):
    nt = len(own)

    def body(*refs):
        outs = refs[2 * nt + len(after):]
        for t in range(nt):
            r = refs[nt + t]
            outs[t][...] = ((refs[t][...] + r[0].astype(F32)) + r[1].astype(F32)) + r[2].astype(F32)

    vm = pl.BlockSpec(memory_space=pltpu.VMEM)
    return pl.pallas_call(
        body, name="rs_sum_chips_" + tag, in_specs=[vm] * (2 * nt) + [ANY] * len(after), out_specs=[vm] * nt,
        out_shape=[jax.ShapeDtypeStruct(a.shape, F32) for a in own],
        compiler_params=_cp())(*own, *recv, *after)


def _sum_chips_shared(own, recv, tag):
    def body(own_ref, recv_ref, out_ref, sib_ref, ssem, rsem):
        out_ref[...] = ((own_ref[...] + recv_ref[0].astype(F32)) + recv_ref[1].astype(F32)) + recv_ref[2].astype(F32)
        x, y, c, _ = _place()
        cp = _remote(out_ref, sib_ref, ssem.at[0], rsem.at[0], (x, y, 1 - c))
        cp.start()
        cp.wait()

    vm = pl.BlockSpec(memory_space=pltpu.VMEM)
    return pl.pallas_call(
        body, name="rs_sum_share_" + tag, in_specs=[vm, vm], out_specs=[vm, ANY],
        out_shape=[jax.ShapeDtypeStruct(own.shape, F32)] * 2,
        scratch_shapes=[pltpu.SemaphoreType.DMA((1,)), pltpu.SemaphoreType.DMA((1,))],
        compiler_params=_cp())(own, recv)


def _adamw_math(w, g, m, v):
    m = ADAM_B1 * m + (1.0 - ADAM_B1) * g
    v = ADAM_B2 * v + (1.0 - ADAM_B2) * (g * g)
    m_hat = m / (1.0 - ADAM_B1 ** ADAM_STEP)
    v_hat = v / (1.0 - ADAM_B2 ** ADAM_STEP)
    delta = -ADAM_LR * (m_hat / (jnp.sqrt(v_hat) + ADAM_EPS) + ADAM_WD * w)
    return delta, m, v


ADAM_SPLIT = 4


def _adamw_shards(own, sib, ws, ms, vs, c_arr, tag):
    nt = len(own)

    def body(c_ref, *refs):
        hh = pl.program_id(0)
        mine = hh == c_ref[0]
        for t in range(nt):
            g = jnp.where(mine, refs[t][...], refs[nt + t][...])
            delta, m, v = _adamw_math(refs[2 * nt + t][...], g, refs[3 * nt + t][...], refs[4 * nt + t][...])
            refs[5 * nt + t][...] = g
            refs[6 * nt + t][...] = delta
            refs[7 * nt + t][...] = m
            refs[8 * nt + t][...] = v

    def tile(a):
        return (a.shape[0] // ADAM_SPLIT, a.shape[1])

    def half_index(used_when_mine):
        def index(hh, q, c_ref):
            mine = 1 - (hh - c_ref[0]) * (hh - c_ref[0])
            used = mine if used_when_mine else 1 - mine
            return (used * q + (1 - used) * hh * (ADAM_SPLIT - 1), 0)
        return index

    own_specs = [pl.BlockSpec(tile(a), half_index(True)) for a in own]
    sib_specs = [pl.BlockSpec(tile(a), half_index(False)) for a in own]
    full_specs = [pl.BlockSpec(tile(a), lambda hh, q, c_ref: (hh * ADAM_SPLIT + q, 0)) for a in own]
    return pl.pallas_call(
        body, name="adamw_shards_" + tag,
        grid_spec=pltpu.PrefetchScalarGridSpec(num_scalar_prefetch=1, grid=(2, ADAM_SPLIT),
                                               in_specs=own_specs + sib_specs + full_specs * 3,
                                               out_specs=full_specs * 4),
        out_shape=[jax.ShapeDtypeStruct(w.shape, F32) for w in ws] * 4,
        compiler_params=_cp(2))(c_arr, *own, *sib, *ws, *ms, *vs)


SMALL_WPOOL_ROW = 32
SMALL_SCALE_ROW = SMALL_WPOOL_ROW + 4 * GROUP
SMALL_BIAS_ROW = SMALL_SCALE_ROW + 8
SMALL_SINKS_ROW = SMALL_BIAS_ROW + NQ
SMALL_LOSS_ROW = SMALL_SINKS_ROW + 8


def _small_sum_adamw(gathered, wp, mp, vp):
    rows = wp.shape[0]

    def body(g_ref, w_ref, m_ref, v_ref, *rest):
        outs, res = rest[:-1], rest[-1]
        g = g_ref[0]
        for d in range(1, 8):
            g = g + g_ref[d]
        delta, m, v = _adamw_math(w_ref[...], g, m_ref[...], v_ref[...])
        for kind, val in enumerate((g, delta, m, v)):
            res[kind] = val
            gains = outs[8 * kind:8 * kind + 4]
            w_pool_o, scale_o, bias_o, sinks_o = outs[8 * kind + 4:8 * kind + 8]
            for t in range(4):
                for i in range(8):
                    gains[t][:, 128 * i:128 * (i + 1)] = res[kind, pl.ds(8 * t + i, 1), :]
            for grp in range(4):
                w_pool_o[grp] = res[kind, pl.ds(SMALL_WPOOL_ROW + GROUP * grp, GROUP), :]
            for i in range(4):
                scale_o[:, 128 * i:128 * (i + 1)] = res[kind, pl.ds(SMALL_SCALE_ROW + i, 1), :]
            bias_o[...] = res[kind, pl.ds(SMALL_BIAS_ROW, NQ), :][:, 0:NBUCKET]
            sinks_o[...] = res[kind, pl.ds(SMALL_SINKS_ROW, 1), :][:, 0:NQ]
        outs[-1][...] = res[0, pl.ds(SMALL_LOSS_ROW, 1), :]

    vm = pl.BlockSpec(memory_space=pltpu.VMEM)
    one_kind = [jax.ShapeDtypeStruct((1, D), F32)] * 4 + [
        jax.ShapeDtypeStruct((4, GROUP, GROUP), F32), jax.ShapeDtypeStruct((1, POOL_W), F32),
        jax.ShapeDtypeStruct((NQ, NBUCKET), F32), jax.ShapeDtypeStruct((1, NQ), F32)]
    return pl.pallas_call(
        body, name="small_sum_adamw", in_specs=[vm] * 4, out_specs=[vm] * 33,
        out_shape=one_kind * 4 + [jax.ShapeDtypeStruct((1, 128), F32)],
        scratch_shapes=[pltpu.VMEM((4, rows, 128), F32)],
        compiler_params=_cp())(gathered, wp, mp, vp)


def _pack_small(gains4, w_pool, pool_scale, rel_bias_t, sinks, loss):
    bias_rows = jnp.pad(rel_bias_t, ((0, 0), (0, 128 - rel_bias_t.shape[1])))
    parts = list(gains4) + [w_pool, pool_scale, bias_rows, sinks, loss]
    rows = []
    for a in parts:
        flat = a.reshape(-1)
        n = flat.shape[0]
        padded = -(-n // 1024) * 1024
        rows.append(jnp.pad(flat, (0, padded - n)).reshape(-1, 128))
    return jnp.concatenate(rows, axis=0)


def kernel(x, g_pre_mix, w_in, w_pool, pool_scale, rel_bias, sinks, w_out, g_post_mix, g_pre_ffn, w_gate, w_up, w_down, g_post_ffn, loss_target, m_g_pre_mix, m_w_in, m_w_pool, m_pool_scale, m_rel_bias, m_sinks, m_w_out, m_g_post_mix, m_g_pre_ffn, m_w_gate, m_w_up, m_w_down, m_g_post_ffn, v_g_pre_mix, v_w_in, v_w_pool, v_pool_scale, v_rel_bias, v_sinks, v_w_out, v_g_post_mix, v_g_pre_ffn, v_w_gate, v_w_up, v_w_down, v_g_post_ffn):
    c = lax.axis_index("c")
    chip = 2 * lax.axis_index("x") + lax.axis_index("y")

    def shards(a_in, a_out, a_gate, a_up, a_down):
        return (a_in[0].T, a_out[0], a_gate[0].T, a_up[0].T, a_down[0])

    c_arr = c.reshape(1).astype(jnp.int32)
    chip_arr = chip.reshape(1).astype(jnp.int32)

    big_w = shards(w_in, w_out, w_gate, w_up, w_down)
    big_bf = [w.astype(BF16) for w in big_w]
    g_ssem, g_rsem, g_srcs, g_lands, _ = _split_start(
        _gather_copies, 15, big_bf, [lax.empty((NSH,) + a.shape, BF16) for a in big_bf], None, "gather_start")
    fw = {}

    def with_own(land, shard):
        return lax.dynamic_update_slice(land, shard[None], (chip, 0, 0))

    def w_in_ready(*after):
        fw["first"] = _split_wait(_gather_copies, g_ssem, g_rsem, g_srcs, g_lands, after, "gather_win_wait",
                                  only=(0, 1, 2))
        (fw["win"],) = _gather_finish([with_own(fw["first"][5], fw["first"][0])], "win")
        return fw["win"].reshape(IN_W, D)

    def w_out_ready(*after):
        first = fw["first"]
        fw["second"] = _split_wait(_gather_copies, g_ssem, g_rsem, first[:5], [fw["win"]] + list(first[6:]), after,
                                   "gather_wout_wait", only=(3, 4, 5))
        (fw["wout"],) = _gather_finish([with_own(fw["second"][6], fw["second"][1])], "wout")
        return fw["wout"].reshape(D, D), None

    def ffn_weights(after):
        second = fw["second"]
        outs = _split_wait(_gather_copies, g_ssem, g_rsem, second[:5], [second[5], fw["wout"]] + list(second[7:]),
                           [after], "gather_ffn_wait", only=tuple(range(6, 15)))
        return _gather_finish([with_own(land, src) for src, land in zip(outs[2:5], outs[7:])], "ffn")

    rs = {}

    def on_ffn_grads(ffn_g):
        oths = ffn_g[1::2]
        rs["ffn_own"] = ffn_g[0::2]
        rs["x"] = _split_start(_sibling_copies, 3, oths, [lax.empty(a.shape, BF16) for a in oths], ffn_g[0],
                               "rs_exchange_ffn_start")
        return rs["x"][4]

    def on_wout_grads(own, oth, after):
        ssem, rsem, srcs, lands, _ = rs["x"]
        recv_ffn = _split_wait(_sibling_copies, ssem, rsem, srcs, lands, [after], "rs_exchange_ffn_wait")[3:]
        recv_wout = _to_sibling([oth], "rs_exchange_wout")
        outs = _chip_partial([own] + list(rs["ffn_own"]), list(recv_wout) + list(recv_ffn), chip_arr, "early")
        rs["early_own"] = outs[4:]
        rs["s"] = _split_start(_scatter_copies, 12, outs[:4],
                               [lax.empty((3,) + a.shape[1:], BF16) for a in outs[:4]], outs[4], "scatter_early_start")
        return rs["s"][4]

    small = (g_pre_mix, g_post_mix, g_pre_ffn, g_post_ffn, w_pool[0], pool_scale, rel_bias, sinks)
    loss, gx, small_grads, ((win_own, win_oth), _, _) = _local_step(
        x[0], loss_target[0], small, w_in_ready, w_out_ready, ffn_weights, c_arr, on_ffn_grads, on_wout_grads)

    ssem, rsem, srcs, lands, _ = rs["s"]
    recv_early = _split_wait(_scatter_copies, ssem, rsem, srcs, lands, [gx], "scatter_early_wait")[4:]
    finals = _sum_chips(list(rs["early_own"]), list(recv_early), "early")
    to_sib = [win_oth] + list(finals)
    sh_ssem, sh_rsem, sh_srcs, sh_lands, _ = _split_start(
        _sibling_copies, 5, to_sib, [lax.empty(a.shape, a.dtype) for a in to_sib], finals[0], "rs_share_start")

    unused = jnp.zeros((1, 1), F32)
    gp = _pack_small(small_grads[:4], *small_grads[4:], loss)
    wp = _pack_small((g_pre_mix, g_post_mix, g_pre_ffn, g_post_ffn), w_pool, pool_scale, rel_bias.T, sinks, unused)
    mp = _pack_small((m_g_pre_mix, m_g_post_mix, m_g_pre_ffn, m_g_post_ffn), m_w_pool, m_pool_scale, m_rel_bias.T,
                     m_sinks, unused)
    vp = _pack_small((v_g_pre_mix, v_g_post_mix, v_g_pre_ffn, v_g_post_ffn), v_w_pool, v_pool_scale, v_rel_bias.T,
                     v_sinks, unused)

    moments = (shards(m_w_in, m_w_out, m_w_gate, m_w_up, m_w_down),
               shards(v_w_in, v_w_out, v_w_gate, v_w_up, v_w_down))
    sh_first = _split_wait(_sibling_copies, sh_ssem, sh_rsem, sh_srcs, sh_lands, [], "rs_share_win_wait", only=(0,))
    outs = _chip_partial([win_own], [sh_first[5]], chip_arr, "win")
    slots = lax.dynamic_update_slice(lax.empty((8,) + gp.shape, F32), gp[None], (2 * chip + c, 0, 0))
    w_ssem, w_rsem, w_srcs, w_lands, w_token = _split_start(
        _win_and_small_copies, 10, [outs[0], gp], [lax.empty((3,) + outs[0].shape[1:], BF16), slots], gx,
        "scatter_win_start")
    shared = _split_wait(_sibling_copies, sh_ssem, sh_rsem, sh_first[:5], sh_first[5:], [w_token],
                         "rs_share_early_wait", only=(1, 2, 3, 4))
    adam_e = _adamw_shards(shared[1:5], shared[6:], big_w[1:], moments[0][1:], moments[1][1:], c_arr, "early")
    w_first = _split_wait(_win_and_small_copies, w_ssem, w_rsem, w_srcs, w_lands, [adam_e[0]], "scatter_win_wait",
                          only=(0, 1, 2))
    win_final, win_sib = _sum_chips_shared(outs[1], w_first[2], "win")
    adam_w = _adamw_shards([win_final], [win_sib], big_w[:1], moments[0][:1], moments[1][:1], c_arr, "win")
    big_g, big_d, big_m, big_v = [[adam_w[i]] + list(adam_e[4 * i:4 * i + 4]) for i in range(4)]

    gathered = _split_wait(_win_and_small_copies, w_ssem, w_rsem, w_first[:2], w_first[2:], [adam_w[0]],
                           "small_allgather_wait", only=tuple(range(3, 10)))[3]
    flat = _small_sum_adamw(gathered, wp, mp, vp)
    small_out = [flat[8 * kind:8 * kind + 8] for kind in range(4)]
    total_loss = flat[-1][0, 0]

    def ordered(small8, big4):
        sg1, sg2, sg3, sg4, swp, ssc, srb, ssk = small8
        swp, srb = swp[None], srb.T
        bwin, bwout, bwg, bwu, bwd = big4[0].T[None], big4[1][None], big4[2].T[None], big4[3].T[None], big4[4][None]
        return [sg1, bwin, swp, ssc, srb, ssk, bwout, sg2, sg3, bwg, bwu, bwd, sg4]

    res = [total_loss, gx[None]]
    for sm, bg in zip(small_out, (big_g, big_d, big_m, big_v)):
        res += ordered(sm, bg)
    return tuple(res)
```

```python
import numpy as np
import jax
import jax.numpy as jnp
from jax import lax
from jax.experimental import pallas as pl
from jax.experimental.pallas import tpu as pltpu

F32 = jnp.float32
BF16 = jnp.bfloat16

D = 1024
POOL_W = 512
GROUP = 128
WINDOWS = (2, 4, 8, 16)
HALO = 128
NQ = 8
BLK = 128
NBUCKET = 32
IN_W = 1280
DFF = 2816
NSH = 4
FS = DFF // NSH
WIN_S = IN_W // NSH
WOUT_S = D // NSH
EPS = 1e-6
NEG = -1e30
SCALE = 0.125

ADAM_LR = 0.001
ADAM_B1 = 0.9
ADAM_B2 = 0.999
ADAM_EPS = 1e-08
ADAM_WD = 0.01
ADAM_STEP = 10

TM = 512
TM_IN = 1024
TM_POOL = 512
TM_FFN = 512
TM_FFN_FWD = 1024
FFN_ROWS = 256
VMEM_LIMIT = 60 * 1024 * 1024

MESH_T = pl.DeviceIdType.MESH
ANY = pl.BlockSpec(memory_space=pl.ANY)


def _cp(n_grid=0, **kw):
    sem = ("arbitrary",) * n_grid if n_grid else None
    return pltpu.CompilerParams(dimension_semantics=sem, vmem_limit_bytes=VMEM_LIMIT, **kw)


def _dot(a, b):
    return jnp.dot(a, b, preferred_element_type=F32)


def _dot_nt(a, b):
    return lax.dot_general(a, b, (((1,), (1,)), ((), ())), preferred_element_type=F32)


def _dot_tn(a, b):
    return lax.dot_general(a, b, (((0,), (0,)), ((), ())), preferred_element_type=F32)


def _rms(x):
    r = lax.rsqrt(jnp.mean(x * x, axis=-1, keepdims=True) + EPS)
    return r, x * r


def _rms_bwd(r, n, g, dout):
    dn = dout * g
    dx = r * (dn - n * jnp.mean(dn * n, axis=-1, keepdims=True))
    dg = jnp.sum(dout * n, axis=0, keepdims=True)
    return dx, dg


def _split(x, n):
    parts = []
    r = x
    for _ in range(n):
        p = r.astype(BF16)
        parts.append(p)
        r = r - p.astype(F32)
    return parts


def _band_dot(a, x, n):
    acc = None
    for p in _split(x, n):
        t = _dot(a, p)
        acc = t if acc is None else acc + t
    return acc


def _bucket_table():
    qi = np.arange(BLK)[None, :]
    kj = np.arange(2 * BLK)[:, None]
    dist = qi + BLK - kj
    n = np.maximum(dist, 0)
    nf = np.maximum(n, 1).astype(np.float32)
    large = 16 + (np.log(nf / np.float32(16)) / np.float32(np.log(128 / 16)) * np.float32(16)).astype(np.int32)
    large = np.minimum(large, NBUCKET - 1)
    return np.where(n < 16, n, large).astype(np.int32)


def _prenorm(x, g1):
    s = x.shape[0]
    tm = min(TM_IN, s)

    def body(x_ref, g_ref, h_ref):
        _, n = _rms(x_ref[...])
        h_ref[...] = (n * g_ref[...]).astype(BF16)

    row = pl.BlockSpec((tm, D), lambda i: (i, 0))
    return pl.pallas_call(
        body, name="prenorm", grid=(s // tm,),
        in_specs=[row, pl.BlockSpec((1, D), lambda i: (0, 0))], out_specs=row,
        out_shape=jax.ShapeDtypeStruct((s, D), BF16),
        compiler_params=_cp(1))(x, g1)


def _inproj_fwd(h1, w_in):
    s = h1.shape[0]
    tm = min(TM_IN, s)

    def body(h_ref, w_ref, u_ref, q_ref, k_ref, v_ref):
        proj = _dot_nt(h_ref[...], w_ref[...])
        u_ref[...] = proj[:, :512]
        q_ref[...] = proj[:, 512:1024].astype(BF16)
        k_ref[...] = proj[:, 1024:1152].astype(BF16)
        v_ref[...] = proj[:, 1152:1280].astype(BF16)

    row = lambda w: pl.BlockSpec((tm, w), lambda i: (i, 0))
    return pl.pallas_call(
        body, name="inproj_fwd", grid=(s // tm,),
        in_specs=[row(D), pl.BlockSpec((IN_W, D), lambda i: (0, 0))],
        out_specs=[row(512), row(512), row(128), row(128)],
        out_shape=[jax.ShapeDtypeStruct((s, 512), F32), jax.ShapeDtypeStruct((s, 512), BF16),
                   jax.ShapeDtypeStruct((s, 128), BF16), jax.ShapeDtypeStruct((s, 128), BF16)],
        compiler_params=_cp(1))(h1, w_in)


def _window_sums(ext, w, lag_sign, tm, terms):
    rows = lax.broadcasted_iota(jnp.int32, (HALO, 2 * HALO), 0)
    cols = lax.broadcasted_iota(jnp.int32, (HALO, 2 * HALO), 1)
    d = rows + HALO - cols if lag_sign > 0 else cols - rows
    band = jnp.where((d >= 0) & (d < w), 1.0, 0.0).astype(BF16)
    return jnp.concatenate([_band_dot(band, ext[r:r + 2 * HALO], terms) for r in range(0, tm, HALO)], axis=0)


def _pooled(ext, cur, t0, tm):
    t = t0 + lax.broadcasted_iota(jnp.int32, (tm, GROUP), 0)
    out = []
    for g, w in enumerate(WINDOWS):
        sl = slice(g * GROUP, (g + 1) * GROUP)
        cnt = jnp.minimum(t + 1, w).astype(F32)
        out.append(_window_sums(ext[:, sl], w, 1, tm, 2) / cnt - cur[:, sl])
    return out


def _pool_fwd(u, w_pool, pool_scale):
    s = u.shape[0]
    tm = min(TM_POOL, s)
    hb = tm // HALO

    def body(uc_ref, uh_ref, wp_ref, sc_ref, o_ref, pooled_ref):
        i = pl.program_id(0)
        cur = uc_ref[...]
        halo = jnp.where(i > 0, uh_ref[...], 0.0)
        ext = jnp.concatenate([halo, cur], axis=0)
        pooled = _pooled(ext, cur, i * tm, tm)
        for g in range(4):
            sl = slice(g * GROUP, (g + 1) * GROUP)
            pb = pooled[g].astype(BF16)
            pooled_ref[:, sl] = pb
            o_ref[:, sl] = (_dot(pb, wp_ref[g]) * sc_ref[:, sl]).astype(BF16)

    row = pl.BlockSpec((tm, 512), lambda i: (i, 0))
    return pl.pallas_call(
        body, name="pool_fwd", grid=(s // tm,),
        in_specs=[row, pl.BlockSpec((HALO, 512), lambda i: (jnp.maximum(i * hb - 1, 0), 0)),
                  pl.BlockSpec((4, GROUP, GROUP), lambda i: (0, 0, 0)),
                  pl.BlockSpec((1, 512), lambda i: (0, 0))],
        out_specs=[row, row],
        out_shape=[jax.ShapeDtypeStruct((s, 512), BF16)] * 2,
        compiler_params=_cp(1))(u, u, w_pool, pool_scale)


def _bias_table(bucket, rel_bias):
    def body(b_ref, rb_ref, o_ref):
        bucket_v = b_ref[...]
        key = lax.broadcasted_iota(jnp.int32, (2 * BLK, BLK), 0)
        dist = lax.broadcasted_iota(jnp.int32, (2 * BLK, BLK), 1) + BLK - key
        inwin = (dist >= 0) & (dist < BLK)
        for h in range(NQ):
            acc = jnp.zeros((2 * BLK, BLK), F32)
            for b in range(NBUCKET):
                acc = jnp.where(bucket_v == b, rb_ref[b, h], acc)
            rest = jnp.where(inwin, acc, NEG)
            o_ref[1, h] = rest
            o_ref[0, h] = jnp.where(key >= BLK, rest, NEG)

    return pl.pallas_call(
        body, name="bias_table",
        in_specs=[pl.BlockSpec(memory_space=pltpu.VMEM), pl.BlockSpec(memory_space=pltpu.SMEM)],
        out_specs=pl.BlockSpec(memory_space=pltpu.VMEM),
        out_shape=jax.ShapeDtypeStruct((2, NQ, 2 * BLK, BLK), F32),
        compiler_params=_cp())(bucket, rel_bias)


HD = 64


def _kv_band(ref, n, transposed):
    pstart = pl.multiple_of(jnp.maximum(n - 1, 0) * BLK, BLK)
    cstart = pl.multiple_of(n * BLK, BLK)
    lo = lax.broadcasted_iota(jnp.int32, (2 * BLK, 128), 1) < HD
    band = jnp.concatenate([ref[pl.ds(pstart, BLK), :], ref[pl.ds(cstart, BLK), :]], axis=0).astype(F32)
    rot = pltpu.roll(band, HD, axis=1)
    halves = ((jnp.where(lo, band, 0.0), jnp.where(lo, 0.0, rot)), (jnp.where(lo, rot, 0.0), jnp.where(lo, 0.0, band)))
    if transposed:
        out = [jnp.concatenate([a.T, b.T], axis=1).astype(BF16) for a, b in halves]
    else:
        out = [jnp.concatenate([a, b], axis=0).astype(BF16) for a, b in halves]
    return out, (lo, pstart, cstart)


def _pair_probs(qt, kk, bias, sk_ref, p):
    sink = jnp.concatenate([jnp.full((1, 1, BLK), sk_ref[0, 2 * p + e], F32) for e in range(2)], axis=0)
    s = _dot_nt(kk, qt).reshape(2, 2 * BLK, BLK) + bias
    m = jnp.maximum(jnp.max(s, axis=1, keepdims=True), sink)
    pr = jnp.exp(s - m)
    es = jnp.exp(sink - m)
    inv = 1.0 / (jnp.sum(pr, axis=1, keepdims=True) + es)
    return pr * inv, es * inv


def _attn_fwd(q, k, v, bias, sinks):
    s = q.shape[0]
    nblk = s // BLK

    def body(q_ref, k_ref, v_ref, b_ref, sk_ref, o_ref, prob_ref, ps_ref):
        n = pl.program_id(0)
        kk, _ = _kv_band(k_ref, n, False)
        vt, _ = _kv_band(v_ref, n, True)
        bidx = jnp.minimum(n, 1)
        for p in range(4):
            h = p // 2
            qt = (q_ref[:, p * 128:(p + 1) * 128].astype(F32) * SCALE).astype(BF16)
            prob, ps = _pair_probs(qt, kk[h], b_ref[bidx, pl.ds(2 * p, 2)], sk_ref, p)
            pb = prob.astype(BF16)
            prob_ref[pl.ds(2 * p, 2)] = pb
            ps_ref[pl.ds(2 * p, 2), :] = ps.reshape(2, BLK)
            acc_t = _dot(vt[h], pb.reshape(4 * BLK, BLK))
            o_ref[:, p * 128:(p + 1) * 128] = acc_t.T.astype(BF16)

    return pl.pallas_call(
        body, name="attn_fwd", grid=(nblk,),
        in_specs=[pl.BlockSpec((BLK, 512), lambda i: (i, 0)),
                  pl.BlockSpec((s, 128), lambda i: (0, 0)),
                  pl.BlockSpec((s, 128), lambda i: (0, 0)),
                  pl.BlockSpec((2, NQ, 2 * BLK, BLK), lambda i: (0, 0, 0, 0)),
                  pl.BlockSpec(memory_space=pltpu.SMEM)],
        out_specs=[pl.BlockSpec((BLK, 512), lambda i: (i, 0)),
                   pl.BlockSpec((None, NQ, 2 * BLK, BLK), lambda i: (i, 0, 0, 0)),
                   pl.BlockSpec((None, NQ, BLK), lambda i: (i, 0, 0))],
        out_shape=[jax.ShapeDtypeStruct((s, 512), BF16), jax.ShapeDtypeStruct((nblk, NQ, 2 * BLK, BLK), BF16),
                   jax.ShapeDtypeStruct((nblk, NQ, BLK), F32)],
        compiler_params=_cp(1))(q, k, v, bias, sinks)


def _outproj_fwd(pool_o, attn_o, w_out, x, g2, g3):
    s = x.shape[0]
    tm = min(TM, s)

    def body(p_ref, a_ref, w_ref, x_ref, g2_ref, g3_ref, mix_ref, x1_ref, h2_ref):
        mix = _dot(p_ref[...], w_ref[0:512, :]) + _dot(a_ref[...], w_ref[512:1024, :])
        mix_ref[...] = mix
        _, n2 = _rms(mix)
        x1 = x_ref[...] + n2 * g2_ref[...]
        x1_ref[...] = x1
        _, n3 = _rms(x1)
        h2_ref[...] = (n3 * g3_ref[...]).astype(BF16)

    row = lambda w: pl.BlockSpec((tm, w), lambda i: (i, 0))
    vec = pl.BlockSpec((1, D), lambda i: (0, 0))
    return pl.pallas_call(
        body, name="outproj_fwd", grid=(s // tm,),
        in_specs=[row(512), row(512), pl.BlockSpec((D, D), lambda i: (0, 0)), row(D), vec, vec],
        out_specs=[row(D), row(D), row(D)],
        out_shape=[jax.ShapeDtypeStruct((s, D), F32), jax.ShapeDtypeStruct((s, D), F32),
                   jax.ShapeDtypeStruct((s, D), BF16)],
        compiler_params=_cp(1))(pool_o, attn_o, w_out, x, g2, g3)


def _silu_parts(g):
    sg = jax.nn.sigmoid(g)
    return sg, g * sg


def _ffn_fwd(h2, wg, wu, wd, x1, target, g4):
    s = h2.shape[0]
    tm = min(TM_FFN_FWD, s)

    def body(h_ref, wg_ref, wu_ref, wd_ref, x1_ref, t_ref, g4_ref,
             gate_ref, up_ref, a_ref, df_ref, dy_ref, loss_ref, gg4_ref, f_acc):
        i = pl.program_id(0)
        j = pl.program_id(1)
        first = j == 0
        chunks = [pl.ds(r, min(FFN_ROWS, tm)) for r in range(0, tm, FFN_ROWS)]
        project = lambda rows: (_dot_nt(h_ref[rows, :], wg_ref[...]), _dot_nt(h_ref[rows, :], wu_ref[...]))
        ahead = [project(chunks[0])]
        for k, rows in enumerate(chunks):
            if k + 1 < len(chunks):
                ahead.append(project(chunks[k + 1]))
            gate, up = ahead[k]
            gate_ref[rows, :] = gate.astype(BF16)
            up_ref[rows, :] = up.astype(BF16)
            _, sl = _silu_parts(gate)
            a = (sl * up).astype(BF16)
            a_ref[rows, :] = a
            contrib = _dot(a, wd_ref[...])
            f_acc[rows, :] = jnp.where(first, contrib, f_acc[rows, :] + contrib)

        @pl.when((i == 0) & (j == 0))
        def _():
            loss_ref[...] = jnp.zeros_like(loss_ref)
            gg4_ref[...] = jnp.zeros_like(gg4_ref)

        @pl.when(j == NSH - 1)
        def _():
            r4, n4 = _rms(f_acc[...])
            g4v = g4_ref[...]
            err = x1_ref[...] + n4 * g4v - t_ref[...]
            loss_ref[...] += (0.5 / D) * jnp.sum(err * err).reshape(1, 1)
            dy = err * (1.0 / D)
            dy_ref[...] = dy
            df, dg = _rms_bwd(r4, n4, g4v, dy)
            gg4_ref[...] += dg
            df_ref[...] = df.astype(BF16)

    row = lambda w: pl.BlockSpec((tm, w), lambda i, j: (i, 0))
    sh = lambda r, c: pl.BlockSpec((None, r, c), lambda i, j: (j, 0, 0))
    act = pl.BlockSpec((None, tm, FS), lambda i, j: (j, i, 0))
    vec = pl.BlockSpec((1, D), lambda i, j: (0, 0))
    return pl.pallas_call(
        body, name="ffn_fwd", grid=(s // tm, NSH),
        in_specs=[row(D), sh(FS, D), sh(FS, D), sh(FS, D), row(D), row(D), vec],
        out_specs=[act, act, act, row(D), row(D), pl.BlockSpec((1, 1), lambda i, j: (0, 0)), vec],
        out_shape=[jax.ShapeDtypeStruct((NSH, s, FS), BF16)] * 3
        + [jax.ShapeDtypeStruct((s, D), BF16), jax.ShapeDtypeStruct((s, D), F32),
           jax.ShapeDtypeStruct((1, 1), F32), jax.ShapeDtypeStruct((1, D), F32)],
        scratch_shapes=[pltpu.VMEM((tm, D), F32)],
        compiler_params=_cp(2))(h2, wg, wu, wd, x1, target, g4)


def _ffn_bwd_act(df, gate, up, wg, wu, wd, dy, x1, mix, g3, g2):
    s = df.shape[0]
    tm = min(TM_FFN, s)

    def body(df_ref, gate_ref, up_ref, wg_ref, wu_ref, wd_ref, dy_ref, x1_ref, mix_ref, g3_ref, g2_ref,
             dgate_ref, dup_ref, dx1_ref, dmix_ref, gg3_ref, gg2_ref, acc):
        j = pl.program_id(0)
        i = pl.program_id(1)
        first = j == 0
        tile = pl.multiple_of(i * tm, tm)
        chunks = [pl.ds(r, min(FFN_ROWS, tm)) for r in range(0, tm, FFN_ROWS)]
        das = [_dot_nt(df_ref[chunks[0], :], wd_ref[...])]
        for k, rows in enumerate(chunks):
            if k + 1 < len(chunks):
                das.append(_dot_nt(df_ref[chunks[k + 1], :], wd_ref[...]))
            da = das[k]
            g = gate_ref[rows, :].astype(F32)
            u = up_ref[rows, :].astype(F32)
            sg, sl = _silu_parts(g)
            dgate = (da * u * (sg * (1.0 + g * (1.0 - sg)))).astype(BF16)
            dup = (da * sl).astype(BF16)
            dgate_ref[rows, :] = dgate
            dup_ref[rows, :] = dup
            contrib = _dot(dgate, wg_ref[...]) + _dot(dup, wu_ref[...])
            acc_rows = pl.ds(tile + k * FFN_ROWS, rows.size)
            acc[acc_rows, :] = jnp.where(first, contrib, acc[acc_rows, :] + contrib)

        @pl.when((i == 0) & (j == 0))
        def _():
            gg3_ref[...] = jnp.zeros_like(gg3_ref)
            gg2_ref[...] = jnp.zeros_like(gg2_ref)

        @pl.when(j == NSH - 1)
        def _():
            r3, n3 = _rms(x1_ref[...])
            dx1n, dg3 = _rms_bwd(r3, n3, g3_ref[...], acc[pl.ds(tile, tm), :])
            dx1 = dy_ref[...] + dx1n
            dx1_ref[...] = dx1
            gg3_ref[...] += dg3
            r2, n2 = _rms(mix_ref[...])
            dmix, dg2 = _rms_bwd(r2, n2, g2_ref[...], dx1)
            gg2_ref[...] += dg2
            dmix_ref[...] = dmix.astype(BF16)

    row = pl.BlockSpec((tm, D), lambda j, i: (i, 0))
    last = pl.BlockSpec((tm, D), lambda j, i: (i * (j // (NSH - 1)), 0))
    sh = pl.BlockSpec((None, FS, D), lambda j, i: (j, 0, 0))
    act = pl.BlockSpec((None, tm, FS), lambda j, i: (j, i, 0))
    vec = pl.BlockSpec((1, D), lambda j, i: (0, 0))
    return pl.pallas_call(
        body, name="ffn_bwd_act", grid=(NSH, s // tm),
        in_specs=[row, act, act, sh, sh, sh, last, last, last, vec, vec],
        out_specs=[act, act, last, last, vec, vec],
        out_shape=[jax.ShapeDtypeStruct((NSH, s, FS), BF16), jax.ShapeDtypeStruct((NSH, s, FS), BF16),
                   jax.ShapeDtypeStruct((s, D), F32), jax.ShapeDtypeStruct((s, D), BF16),
                   jax.ShapeDtypeStruct((1, D), F32), jax.ShapeDtypeStruct((1, D), F32)],
        scratch_shapes=[pltpu.VMEM((s, D), F32)],
        compiler_params=_cp(2))(df, gate, up, wg, wu, wd, dy, x1, mix, g3, g2)


SMEM_SPEC = pl.BlockSpec(memory_space=pltpu.SMEM)


def _emit_halves(acc_ref, row0, rows, c, own_ref, oth_ref):
    half = rows // 2
    own_ref[...] = acc_ref[pl.ds(row0 + pl.multiple_of(c * half, 8), half), :]
    oth_ref[...] = acc_ref[pl.ds(row0 + pl.multiple_of((1 - c) * half, 8), half), :].astype(BF16)


def _half_shapes(rows):
    return [jax.ShapeDtypeStruct((NSH, rows // 2, D), F32), jax.ShapeDtypeStruct((NSH, rows // 2, D), BF16)]


def _ffn_bwd_w(h2, act_a, dgate, dup, df, c_arr):
    s = h2.shape[0]
    tm = min(TM_FFN_FWD, s)
    nt = s // tm

    def body(c_ref, h_ref, a_ref, dgate_ref, dup_ref, df_ref, *rest):
        outs, accs = rest[:6], rest[6:]
        i = pl.program_id(1)

        @pl.when(i == 0)
        def _():
            for acc in accs:
                acc[...] = jnp.zeros_like(acc)

        h = h_ref[...]
        accs[0][...] += _dot_tn(dgate_ref[...], h)
        accs[1][...] += _dot_tn(dup_ref[...], h)
        accs[2][...] += _dot_tn(a_ref[...], df_ref[...])

        @pl.when(i == nt - 1)
        def _():
            for t, acc in enumerate(accs):
                _emit_halves(acc, 0, FS, c_ref[0], outs[2 * t], outs[2 * t + 1])

    row = lambda w: pl.BlockSpec((tm, w), lambda j, i: (i, 0))
    act = pl.BlockSpec((None, tm, FS), lambda j, i: (j, i, 0))
    half = pl.BlockSpec((None, FS // 2, D), lambda j, i: (j, 0, 0))
    return pl.pallas_call(
        body, name="ffn_bwd_w", grid=(NSH, nt),
        in_specs=[SMEM_SPEC, row(D), act, act, act, row(D)],
        out_specs=[half] * 6,
        out_shape=_half_shapes(FS) * 3,
        scratch_shapes=[pltpu.VMEM((FS, D), F32)] * 3,
        compiler_params=_cp(2))(c_arr, h2, act_a, dgate, dup, df)


def _outproj_bwd(dmix, w_out, pool_o, attn_o, c_arr, dep=None):
    s = dmix.shape[0]
    tm = min(TM, s)
    nt = s // tm

    def body(c_ref, dm_ref, w_ref, p_ref, a_ref, *rest):
        dpool_ref, dattn_ref, own_ref, oth_ref, gw_ref = rest[-5:]
        i = pl.program_id(0)

        @pl.when(i == 0)
        def _():
            gw_ref[...] = jnp.zeros_like(gw_ref)

        dm = dm_ref[...]
        dpool_ref[...] = _dot_nt(dm, w_ref[0:512, :])
        dattn_ref[...] = _dot_nt(dm, w_ref[512:1024, :]).astype(BF16)
        gw_ref[0:512, :] += _dot_tn(p_ref[...], dm)
        gw_ref[512:1024, :] += _dot_tn(a_ref[...], dm)

        @pl.when(i == nt - 1)
        def _():
            for k in range(NSH):
                _emit_halves(gw_ref, k * WOUT_S, WOUT_S, c_ref[0], own_ref.at[k], oth_ref.at[k])

    row = lambda w: pl.BlockSpec((tm, w), lambda i: (i, 0))
    full = pl.BlockSpec((D, D), lambda i: (0, 0))
    half = pl.BlockSpec((NSH, WOUT_S // 2, D), lambda i: (0, 0, 0))
    return pl.pallas_call(
        body, name="outproj_bwd", grid=(nt,),
        in_specs=[SMEM_SPEC, row(D), full, row(512), row(512)] + ([] if dep is None else [ANY]),
        out_specs=[row(512), row(512), half, half],
        out_shape=[jax.ShapeDtypeStruct((s, 512), F32), jax.ShapeDtypeStruct((s, 512), BF16)] + _half_shapes(WOUT_S),
        scratch_shapes=[pltpu.VMEM((D, D), F32)],
        compiler_params=_cp(1))(c_arr, dmix, w_out, pool_o, attn_o, *([] if dep is None else [dep]))


def _pool_bwd(pooled, dpool, w_pool, pool_scale, dep=None):
    s = pooled.shape[0]
    tm = min(TM_POOL, s)
    hb = tm // HALO
    nt = s // tm
    last_halo = s // HALO - 1

    def body(p_ref, dc_ref, dn_ref, wp_ref, sc_ref, *rest):
        du_ref, gwp_ref, gsc_ref = rest[-3:]
        i = pl.program_id(0)

        @pl.when(i == 0)
        def _():
            gwp_ref[...] = jnp.zeros_like(gwp_ref)
            gsc_ref[...] = jnp.zeros_like(gsc_ref)

        dcur = dc_ref[...]
        dnext = jnp.where(i < nt - 1, dn_ref[...], 0.0)
        dext = jnp.concatenate([dcur, dnext], axis=0)
        t_ext = i * tm + lax.broadcasted_iota(jnp.int32, (tm + HALO, GROUP), 0)
        for g, w in enumerate(WINDOWS):
            sl = slice(g * GROUP, (g + 1) * GROUP)
            pb = p_ref[:, sl]
            wp = wp_ref[g]
            mixed = _dot(pb, wp)
            gsc_ref[:, sl] += jnp.sum(dcur[:, sl] * mixed, axis=0, keepdims=True)
            dmixed = (dext[:, sl] * sc_ref[:, sl]).astype(BF16)
            gwp_ref[g] += _dot_tn(pb, dmixed[0:tm])
            dpooled = _dot_nt(dmixed, wp)
            z = dpooled / jnp.minimum(t_ext + 1, w).astype(F32)
            du_ref[:, sl] = (_window_sums(z, w, -1, tm, 2) - dpooled[0:tm]).astype(BF16)

    return pl.pallas_call(
        body, name="pool_bwd", grid=(nt,),
        in_specs=[pl.BlockSpec((tm, 512), lambda i: (i, 0)),
                  pl.BlockSpec((tm, 512), lambda i: (i, 0)),
                  pl.BlockSpec((HALO, 512), lambda i: (jnp.minimum((i + 1) * hb, last_halo), 0)),
                  pl.BlockSpec((4, GROUP, GROUP), lambda i: (0, 0, 0)),
                  pl.BlockSpec((1, 512), lambda i: (0, 0))] + ([] if dep is None else [ANY]),
        out_specs=[pl.BlockSpec((tm, 512), lambda i: (i, 0)),
                   pl.BlockSpec((4, GROUP, GROUP), lambda i: (0, 0, 0)),
                   pl.BlockSpec((1, 512), lambda i: (0, 0))],
        out_shape=[jax.ShapeDtypeStruct((s, 512), BF16), jax.ShapeDtypeStruct((4, GROUP, GROUP), F32),
                   jax.ShapeDtypeStruct((1, 512), F32)],
        compiler_params=_cp(1))(pooled, dpool, dpool, w_pool, pool_scale, *([] if dep is None else [dep]))


def _attn_bwd(q, k, v, do, probs, sink_share, bucket, dep=None):
    s = q.shape[0]
    nblk = s // BLK

    def body(q_ref, do_ref, k_ref, v_ref, prob_ref, ps_ref, bk_ref, *rest):
        dq_ref, dk_ref, dv_ref, grb_ref, gsk_ref, dss_ref = rest[-6:]
        n = pl.program_id(0)

        @pl.when(n == 0)
        def _():
            dk_ref[...] = jnp.zeros_like(dk_ref)
            dv_ref[...] = jnp.zeros_like(dv_ref)
            dss_ref[...] = jnp.zeros_like(dss_ref)
            gsk_ref[...] = jnp.zeros_like(gsk_ref)

        kt, (lo, pstart, cstart) = _kv_band(k_ref, n, True)
        vv, _ = _kv_band(v_ref, n, False)
        lo_q = lax.broadcasted_iota(jnp.int32, (BLK, 128), 1) < HD
        dkk = [jnp.zeros((2 * BLK, 128), F32), jnp.zeros((2 * BLK, 128), F32)]
        dvv = [jnp.zeros((2 * BLK, 128), F32), jnp.zeros((2 * BLK, 128), F32)]

        def by_head(t):
            return jnp.concatenate([jnp.where(lo_q, t, 0.0), jnp.where(lo_q, 0.0, t)], axis=0).astype(BF16)

        for p in range(4):
            h = p // 2
            qt = q_ref[:, p * 128:(p + 1) * 128].astype(F32) * SCALE
            dot = do_ref[:, p * 128:(p + 1) * 128]
            pb = prob_ref[pl.ds(2 * p, 2)]
            prob = pb.astype(F32)
            ps = ps_ref[pl.ds(2 * p, 2), :].reshape(2, 1, BLK)
            dp = _dot_nt(vv[h], dot).reshape(2, 2 * BLK, BLK)
            delta = jnp.sum(prob * dp, axis=1, keepdims=True)
            ds = prob * (dp - delta)
            sink_part = ps * delta
            for e in range(2):
                gs = -jnp.sum(sink_part[e]).reshape(1, 1)
                gsk_ref[pl.ds(2 * p + e, 1), :] += jnp.broadcast_to(gs, (1, 128))
            dss_ref[pl.ds(2 * p, 2)] += ds
            dsb = ds.astype(BF16)
            dq_t = _dot(kt[h], dsb.reshape(4 * BLK, BLK))
            dq_ref[:, p * 128:(p + 1) * 128] = (dq_t.T * SCALE).astype(BF16)
            dkk[h] = dkk[h] + _dot(jnp.concatenate([dsb[0], dsb[1]], axis=1), by_head(qt))
            dvv[h] = dvv[h] + _dot(jnp.concatenate([pb[0], pb[1]], axis=1), by_head(dot.astype(F32)))
        for acc, ref in ((dkk, dk_ref), (dvv, dv_ref)):
            f0 = acc[0] + pltpu.roll(acc[0], HD, axis=1)
            f1 = acc[1] + pltpu.roll(acc[1], HD, axis=1)
            band = jnp.where(lo, f0, f1)
            ref[pl.ds(pstart, BLK), :] += band[0:BLK]
            ref[pl.ds(cstart, BLK), :] += band[BLK:2 * BLK]

        @pl.when(n == nblk - 1)
        def _():
            _relbias_grad(dss_ref, bk_ref[...], grb_ref)

    blk = pl.BlockSpec((BLK, 512), lambda i: (i, 0))
    kv = pl.BlockSpec((s, 128), lambda i: (0, 0))
    row8 = pl.BlockSpec((NQ, 128), lambda i: (0, 0))
    return pl.pallas_call(
        body, name="attn_bwd", grid=(nblk,),
        in_specs=[blk, blk, kv, kv, pl.BlockSpec((None, NQ, 2 * BLK, BLK), lambda i: (i, 0, 0, 0)),
                  pl.BlockSpec((None, NQ, BLK), lambda i: (i, 0, 0)), pl.BlockSpec((2 * BLK, BLK), lambda i: (0, 0))]
        + ([] if dep is None else [ANY]),
        out_specs=[blk, kv, kv, row8, row8],
        out_shape=[jax.ShapeDtypeStruct((s, 512), BF16), jax.ShapeDtypeStruct((s, 128), F32),
                   jax.ShapeDtypeStruct((s, 128), F32), jax.ShapeDtypeStruct((NQ, 128), F32),
                   jax.ShapeDtypeStruct((NQ, 128), F32)],
        scratch_shapes=[pltpu.VMEM((NQ, 2 * BLK, BLK), F32)],
        compiler_params=_cp(1))(q, do, k, v, probs, sink_share, bucket, *([] if dep is None else [dep]))


def _relbias_grad(ds_ref, bucket_v, o_ref):
    lane = lax.broadcasted_iota(jnp.int32, (NQ, 128), 1)
    head_row = lax.broadcasted_iota(jnp.int32, (NQ, BLK), 0)
    acc = jnp.zeros((NQ, 128), F32)
    for b in range(NBUCKET):
        sel = bucket_v == b
        stack = jnp.zeros((NQ, BLK), F32)
        for h in range(NQ):
            col_sums = jnp.sum(jnp.where(sel, ds_ref[h], 0.0), axis=0, keepdims=True)
            stack = jnp.where(head_row == h, col_sums, stack)
        acc = acc + jnp.where(lane == b, jnp.sum(stack, axis=1, keepdims=True), 0.0)
    o_ref[...] = acc


def _inproj_bwd(x, g1, du, dq, dk, dv, w_in, dx1, c_arr):
    s = x.shape[0]
    tm = min(TM, s)
    nt = s // tm
    cols = ((0, 512), (512, 1024), (1024, 1152), (1152, 1280))

    def body(c_ref, x_ref, g_ref, du_ref, dq_ref, dk_ref, dv_ref, w_ref, dx1_ref,
             gx_ref, own_ref, oth_ref, gg_ref, gw_ref):
        i = pl.program_id(0)

        @pl.when(i == 0)
        def _():
            gw_ref[...] = jnp.zeros_like(gw_ref)
            gg_ref[...] = jnp.zeros_like(gg_ref)

        parts = (du_ref[...], dq_ref[...], dk_ref[...].astype(BF16), dv_ref[...].astype(BF16))
        dh = None
        for (a, b), dpart in zip(cols, parts):
            t = _dot(dpart, w_ref[a:b, :])
            dh = t if dh is None else dh + t
        gv = g_ref[...]
        r1, n1 = _rms(x_ref[...])
        h = (n1 * gv).astype(BF16)
        for (a, b), dpart in zip(cols, parts):
            gw_ref[a:b, :] += _dot_tn(dpart, h)
        dx, dg = _rms_bwd(r1, n1, gv, dh)
        gg_ref[...] += dg
        gx_ref[...] = dx1_ref[...] + dx

        @pl.when(i == nt - 1)
        def _():
            for k in range(NSH):
                _emit_halves(gw_ref, k * WIN_S, WIN_S, c_ref[0], own_ref.at[k], oth_ref.at[k])

    row = lambda w: pl.BlockSpec((tm, w), lambda i: (i, 0))
    vec = pl.BlockSpec((1, D), lambda i: (0, 0))
    full = pl.BlockSpec((IN_W, D), lambda i: (0, 0))
    half = pl.BlockSpec((NSH, WIN_S // 2, D), lambda i: (0, 0, 0))
    return pl.pallas_call(
        body, name="inproj_bwd", grid=(nt,),
        in_specs=[SMEM_SPEC, row(D), vec, row(512), row(512), row(128), row(128), full, row(D)],
        out_specs=[row(D), half, half, vec],
        out_shape=[jax.ShapeDtypeStruct((s, D), F32)] + _half_shapes(WIN_S) + [jax.ShapeDtypeStruct((1, D), F32)],
        scratch_shapes=[pltpu.VMEM((IN_W, D), F32)],
        compiler_params=_cp(1))(c_arr, x, g1, du, dq, dk, dv, w_in, dx1)


def _local_step(x, target, small, w_in, w_out, ffn_weights, c_arr, on_ffn_grads=None, on_wout_grads=None):
    g1, g2, g3, g4, w_pool, pool_scale, rel_bias, sinks = small
    bucket = jnp.asarray(_bucket_table())
    wp_bf = w_pool.astype(BF16)
    bias = _bias_table(bucket, rel_bias)
    h1 = _prenorm(x, g1)
    if callable(w_in):
        w_in = w_in(bias, h1)
    u, q, k, v = _inproj_fwd(h1, w_in)
    pool_o, pooled = _pool_fwd(u, wp_bf, pool_scale)
    attn_o, probs, sink_share = _attn_fwd(q, k, v, bias, sinks)
    g2_fwd = g2
    if callable(w_out):
        w_out, after = w_out(pool_o, attn_o)
        if after is not None:
            g2_fwd = g2 + after[:1, :1]
    mix, x1, h2 = _outproj_fwd(pool_o, attn_o, w_out, x, g2_fwd, g3)
    wg, wu, wd = ffn_weights(h2) if callable(ffn_weights) else ffn_weights
    gate, up, act_a, df, dy, loss, gg4 = _ffn_fwd(h2, wg, wu, wd, x1, target, g4)
    dgate, dup, dx1, dmix, gg3, gg2 = _ffn_bwd_act(df, gate, up, wg, wu, wd, dy, x1, mix, g3, g2)
    ffn_g = _ffn_bwd_w(h2, act_a, dgate, dup, df, c_arr)
    dep = None if on_ffn_grads is None else on_ffn_grads(ffn_g)
    dpool, dattn, wout_own, wout_oth = _outproj_bwd(dmix, w_out, pool_o, attn_o, c_arr, dep)
    dep = None if on_wout_grads is None else on_wout_grads(wout_own, wout_oth, dpool)
    du, gwp, gsc = _pool_bwd(pooled, dpool, wp_bf, pool_scale, dep)
    dq, dk, dv, grb, gsk = _attn_bwd(q, k, v, dattn, probs, sink_share, bucket, dep)
    gx, win_own, win_oth, gg1 = _inproj_bwd(x, g1, du, dq, dk, dv, w_in, dx1, c_arr)
    small_grads = (gg1, gg2, gg3, gg4, gwp, gsc, grb, gsk[:, 0].reshape(1, NQ))
    return loss, gx, small_grads, ((win_own, win_oth), (wout_own, wout_oth), ffn_g)


def _place():
    x, y, c = lax.axis_index("x"), lax.axis_index("y"), lax.axis_index("c")
    chips = ((1 - x, y), (x, 1 - y), (1 - x, 1 - y))
    return x, y, c, chips


def _remote(src, dst, ssem, rsem, dev):
    return pltpu.make_async_remote_copy(src_ref=src, dst_ref=dst, send_sem=ssem, recv_sem=rsem,
                                        device_id=dev, device_id_type=MESH_T)


def _to_sibling(arrs, name, after=()):
    nt, na = len(arrs), len(after)

    def body(*refs):
        srcs, dsts = refs[:nt], refs[nt + na:2 * nt + na]
        ssem, rsem = refs[2 * nt + na:]
        x, y, c, _ = _place()
        cps = [_remote(srcs[t], dsts[t], ssem.at[t], rsem.at[t], (x, y, 1 - c)) for t in range(nt)]
        for cp in cps:
            cp.start()
        for cp in cps:
            cp.wait()

    return pl.pallas_call(
        body, name=name, in_specs=[ANY] * (nt + na), out_specs=[ANY] * nt,
        out_shape=[jax.ShapeDtypeStruct(a.shape, a.dtype) for a in arrs],
        scratch_shapes=[pltpu.SemaphoreType.DMA((nt,)), pltpu.SemaphoreType.DMA((nt,))],
        compiler_params=_cp())(*arrs, *after)


HBM_SPEC = pl.BlockSpec(memory_space=pltpu.HBM)
SEM_SPEC = pl.BlockSpec(memory_space=pltpu.SEMAPHORE)
DATAFLOW = pltpu.SideEffectType.DATAFLOW_SIDE_EFFECTING


def _gather_copies(srcs, lands, ssem, rsem):
    x, y, c, chips = _place()
    me = 2 * x + y
    out = []
    for t in range(len(srcs)):
        half = srcs[t].shape[0] // 2
        mine = pl.ds(pl.multiple_of(c * half, 16), half)
        for j, (px, py) in enumerate(chips):
            k = 3 * t + j
            send = _remote(srcs[t].at[mine], lands[t].at[me, mine], ssem.at[k], rsem.at[k], (px, py, c))
            blk = lands[t].at[2 * px + py, mine]
            out.append((send, _remote(blk, blk, ssem.at[k], rsem.at[k], (px, py, c))))
    return out


def _scatter_copies(srcs, lands, ssem, rsem):
    x, y, c, chips = _place()
    out = []
    for t in range(len(srcs)):
        for j, (px, py) in enumerate(chips):
            k = 3 * t + j
            send = _remote(srcs[t].at[2 * px + py], lands[t].at[j], ssem.at[k], rsem.at[k], (px, py, c))
            out.append((send, _remote(lands[t].at[j], lands[t].at[j], ssem.at[k], rsem.at[k], (px, py, c))))
    return out


def _sibling_copies(srcs, lands, ssem, rsem):
    x, y, c, _ = _place()
    sib = (x, y, 1 - c)
    return [(_remote(srcs[t], lands[t], ssem.at[t], rsem.at[t], sib),
             _remote(lands[t], lands[t], ssem.at[t], rsem.at[t], sib)) for t in range(len(srcs))]


def _peer_copies(srcs, lands, ssem, rsem):
    x, y, c, _ = _place()
    me = 4 * x + 2 * y + c
    out = []
    for kk in range(1, 8):
        px, py, pc = x ^ (kk >> 2), y ^ ((kk >> 1) & 1), c ^ (kk & 1)
        send = _remote(srcs[0], lands[0].at[me], ssem.at[kk - 1], rsem.at[kk - 1], (px, py, pc))
        slot = lands[0].at[4 * px + 2 * py + pc]
        out.append((send, _remote(slot, slot, ssem.at[kk - 1], rsem.at[kk - 1], (px, py, pc))))
    return out


def _win_and_small_copies(srcs, lands, ssem, rsem):
    return (_scatter_copies(srcs[:1], lands[:1], ssem, rsem)
            + _peer_copies(srcs[1:], lands[1:], ssem.at[pl.ds(3, 7)], rsem.at[pl.ds(3, 7)]))


def _split_start(plan, ncopy, srcs, lands, after, name):
    ns, nbuf = len(srcs), len(srcs) + len(lands)
    extra = [] if after is None else [after]
    n_in = nbuf + len(extra)

    def body(*refs):
        ssem, rsem, token = refs[n_in], refs[n_in + 1], refs[-1]
        for send, _ in plan(refs[:ns], refs[ns:nbuf], ssem, rsem):
            send.start()
        token[...] = jnp.zeros_like(token)

    bufs = [pltpu.with_memory_space_constraint(a, pltpu.HBM) for a in list(srcs) + list(lands)]
    outs = pl.pallas_call(
        body, name=name, in_specs=[HBM_SPEC] * nbuf + [ANY] * len(extra),
        out_specs=[SEM_SPEC, SEM_SPEC] + [HBM_SPEC] * nbuf + [pl.BlockSpec(memory_space=pltpu.VMEM)],
        out_shape=[pltpu.SemaphoreType.DMA((ncopy,)), pltpu.SemaphoreType.DMA((ncopy,))]
        + [pltpu.HBM(a.shape, a.dtype) for a in bufs] + [jax.ShapeDtypeStruct((8, 128), F32)],
        input_output_aliases={i: 2 + i for i in range(nbuf)},
        compiler_params=pltpu.CompilerParams(has_side_effects=DATAFLOW))(*bufs, *extra)
    return outs[0], outs[1], outs[2:2 + ns], outs[2 + ns:2 + nbuf], outs[-1]


def _split_wait(plan, ssem, rsem, srcs, lands, after, name, only=None):
    ns, nbuf = len(srcs), len(srcs) + len(lands)

    def body(*refs):
        s_ref, r_ref = refs[nbuf], refs[nbuf + 1]
        for k, (send, recv) in enumerate(plan(refs[:ns], refs[ns:nbuf], s_ref, r_ref)):
            if only is None or k in only:
                send.wait_send()
                recv.wait_recv()

    outs = pl.pallas_call(
        body, name=name, in_specs=[HBM_SPEC] * nbuf + [SEM_SPEC, SEM_SPEC] + [ANY] * len(after),
        out_specs=[HBM_SPEC] * nbuf,
        out_shape=[pltpu.HBM(a.shape, a.dtype) for a in list(srcs) + list(lands)],
        input_output_aliases={i: i for i in range(nbuf)},
        compiler_params=pltpu.CompilerParams(has_side_effects=DATAFLOW))(*srcs, *lands, ssem, rsem, *after)
    return outs


def _gather_finish(lands, tag):
    nt = len(lands)

    def body(*refs):
        outs = refs[nt:2 * nt]
        dsend, drecv = refs[2 * nt:]
        x, y, c, chips = _place()
        sib = (x, y, 1 - c)
        sends = []
        for t in range(nt):
            half = outs[t].shape[1] // 2
            mine = pl.ds(pl.multiple_of(c * half, 16), half)
            for j, (px, py) in enumerate(chips):
                blk = outs[t].at[2 * px + py, mine]
                cp = _remote(blk, blk, dsend.at[t, j], drecv.at[t, j], sib)
                cp.start()
                sends.append(cp)
        for t in range(nt):
            half = outs[t].shape[1] // 2
            other = pl.ds(pl.multiple_of((1 - c) * half, 16), half)
            for j, (px, py) in enumerate(chips):
                blk = outs[t].at[2 * px + py, other]
                _remote(blk, blk, dsend.at[t, j], drecv.at[t, j], sib).wait_recv()
        for cp in sends:
            cp.wait_send()

    return pl.pallas_call(
        body, name="gather_finish_" + tag, in_specs=[ANY] * nt, out_specs=[ANY] * nt,
        out_shape=[jax.ShapeDtypeStruct(a.shape, a.dtype) for a in lands],
        input_output_aliases={t: t for t in range(nt)},
        scratch_shapes=[pltpu.SemaphoreType.DMA((nt, 3)), pltpu.SemaphoreType.DMA((nt, 3))],
        compiler_params=_cp())(*lands)


def _chip_partial(owns, recv, chip_arr, tag):
    nt = len(owns)

    def body(chip_ref, *refs):
        k = pl.program_id(0)
        for t in range(nt):
            p = refs[t][...] + refs[nt + t][...].astype(F32)
            refs[2 * nt + t][...] = p.astype(BF16)

            @pl.when(k == chip_ref[0])
            def _():
                refs[3 * nt + t][...] = p

    blk_specs = [pl.BlockSpec((None, a.shape[1] // 2, a.shape[2]), lambda k, r, chip_ref: (k, r, 0)) for a in owns]

    def own_index(k, r, chip_ref):
        mine = chip_ref[0]
        return (jnp.where(k == mine, r, jnp.where(k < mine, 0, 1)), 0)

    own_specs = [pl.BlockSpec((a.shape[1] // 2, a.shape[2]), own_index) for a in owns]
    return pl.pallas_call(
        body, name="rs_chip_partial_" + tag,
        grid_spec=pltpu.PrefetchScalarGridSpec(num_scalar_prefetch=1, grid=(NSH, 2), in_specs=blk_specs * 2,
                                               out_specs=blk_specs + own_specs),
        out_shape=[jax.ShapeDtypeStruct(a.shape, BF16) for a in owns]
        + [jax.ShapeDtypeStruct(a.shape[1:], F32) for a in owns],
        compiler_params=_cp(2))(chip_arr, *owns, *recv)


def _sum_chips(own, recv, tag):
    nt = len(own)

    def body(*refs):
        outs = refs[2 * nt:]
        for t in range(nt):
            r = refs[nt + t]
            outs[t][...] = ((refs[t][...] + r[0].astype(F32)) + r[1].astype(F32)) + r[2].astype(F32)

    own_specs = [pl.BlockSpec((a.shape[0] // 2, a.shape[1]), lambda i: (i, 0)) for a in own]
    recv_specs = [pl.BlockSpec((3, a.shape[0] // 2, a.shape[1]), lambda i: (0, i, 0)) for a in own]
    return pl.pallas_call(
        body, name="rs_sum_chips_" + tag, grid=(2,), in_specs=own_specs + recv_specs, out_specs=own_specs,
        out_shape=[jax.ShapeDtypeStruct(a.shape, F32) for a in own],
        compiler_params=_cp(1))(*own, *recv)


def _sum_chips_shared(own, recv, tag):
    def body(own_ref, recv_ref, out_ref, sib_ref, ssem, rsem):
        out_ref[...] = ((own_ref[...] + recv_ref[0].astype(F32)) + recv_ref[1].astype(F32)) + recv_ref[2].astype(F32)
        x, y, c, _ = _place()
        cp = _remote(out_ref, sib_ref, ssem.at[0], rsem.at[0], (x, y, 1 - c))
        cp.start()
        cp.wait()

    vm = pl.BlockSpec(memory_space=pltpu.VMEM)
    return pl.pallas_call(
        body, name="rs_sum_share_" + tag, in_specs=[vm, vm], out_specs=[vm, ANY],
        out_shape=[jax.ShapeDtypeStruct(own.shape, F32)] * 2,
        scratch_shapes=[pltpu.SemaphoreType.DMA((1,)), pltpu.SemaphoreType.DMA((1,))],
        compiler_params=_cp())(own, recv)


def _adamw_math(w, g, m, v):
    m = ADAM_B1 * m + (1.0 - ADAM_B1) * g
    v = ADAM_B2 * v + (1.0 - ADAM_B2) * (g * g)
    m_hat = m / (1.0 - ADAM_B1 ** ADAM_STEP)
    v_hat = v / (1.0 - ADAM_B2 ** ADAM_STEP)
    delta = -ADAM_LR * (m_hat / (jnp.sqrt(v_hat) + ADAM_EPS) + ADAM_WD * w)
    return delta, m, v


ADAM_SPLIT = 4


def _adamw_shards(own, sib, ws, ms, vs, c_arr, tag):
    nt = len(own)

    def body(c_ref, *refs):
        hh = pl.program_id(0)
        mine = hh == c_ref[0]
        for t in range(nt):
            g = jnp.where(mine, refs[t][...], refs[nt + t][...])
            delta, m, v = _adamw_math(refs[2 * nt + t][...], g, refs[3 * nt + t][...], refs[4 * nt + t][...])
            refs[5 * nt + t][...] = g
            refs[6 * nt + t][...] = delta
            refs[7 * nt + t][...] = m
            refs[8 * nt + t][...] = v

    def tile(a):
        return (a.shape[0] // ADAM_SPLIT, a.shape[1])

    def half_index(used_when_mine):
        def index(hh, q, c_ref):
            mine = 1 - (hh - c_ref[0]) * (hh - c_ref[0])
            used = mine if used_when_mine else 1 - mine
            return (used * q + (1 - used) * hh * (ADAM_SPLIT - 1), 0)
        return index

    own_specs = [pl.BlockSpec(tile(a), half_index(True)) for a in own]
    sib_specs = [pl.BlockSpec(tile(a), half_index(False)) for a in own]
    full_specs = [pl.BlockSpec(tile(a), lambda hh, q, c_ref: (hh * ADAM_SPLIT + q, 0)) for a in own]
    return pl.pallas_call(
        body, name="adamw_shards_" + tag,
        grid_spec=pltpu.PrefetchScalarGridSpec(num_scalar_prefetch=1, grid=(2, ADAM_SPLIT),
                                               in_specs=own_specs + sib_specs + full_specs * 3,
                                               out_specs=full_specs * 4),
        out_shape=[jax.ShapeDtypeStruct(w.shape, F32) for w in ws] * 4,
        compiler_params=_cp(2))(c_arr, *own, *sib, *ws, *ms, *vs)


SMALL_WPOOL_ROW = 32
SMALL_SCALE_ROW = SMALL_WPOOL_ROW + 4 * GROUP
SMALL_BIAS_ROW = SMALL_SCALE_ROW + 8
SMALL_SINKS_ROW = SMALL_BIAS_ROW + NQ
SMALL_LOSS_ROW = SMALL_SINKS_ROW + 8


def _small_sum_adamw(gathered, wp, mp, vp):
    rows = wp.shape[0]

    def body(g_ref, w_ref, m_ref, v_ref, *rest):
        outs, res = rest[:-1], rest[-1]
        g = g_ref[0]
        for d in range(1, 8):
            g = g + g_ref[d]
        delta, m, v = _adamw_math(w_ref[...], g, m_ref[...], v_ref[...])
        for kind, val in enumerate((g, delta, m, v)):
            res[kind] = val
            gains = outs[8 * kind:8 * kind + 4]
            w_pool_o, scale_o, bias_o, sinks_o = outs[8 * kind + 4:8 * kind + 8]
            for t in range(4):
                for i in range(8):
                    gains[t][:, 128 * i:128 * (i + 1)] = res[kind, pl.ds(8 * t + i, 1), :]
            for grp in range(4):
                w_pool_o[grp] = res[kind, pl.ds(SMALL_WPOOL_ROW + GROUP * grp, GROUP), :]
            for i in range(4):
                scale_o[:, 128 * i:128 * (i + 1)] = res[kind, pl.ds(SMALL_SCALE_ROW + i, 1), :]
            bias_o[...] = res[kind, pl.ds(SMALL_BIAS_ROW, NQ), :][:, 0:NBUCKET]
            sinks_o[...] = res[kind, pl.ds(SMALL_SINKS_ROW, 1), :][:, 0:NQ]
        outs[-1][...] = res[0, pl.ds(SMALL_LOSS_ROW, 1), :]

    vm = pl.BlockSpec(memory_space=pltpu.VMEM)
    one_kind = [jax.ShapeDtypeStruct((1, D), F32)] * 4 + [
        jax.ShapeDtypeStruct((4, GROUP, GROUP), F32), jax.ShapeDtypeStruct((1, POOL_W), F32),
        jax.ShapeDtypeStruct((NQ, NBUCKET), F32), jax.ShapeDtypeStruct((1, NQ), F32)]
    return pl.pallas_call(
        body, name="small_sum_adamw", in_specs=[vm] * 4, out_specs=[vm] * 33,
        out_shape=one_kind * 4 + [jax.ShapeDtypeStruct((1, 128), F32)],
        scratch_shapes=[pltpu.VMEM((4, rows, 128), F32)],
        compiler_params=_cp())(gathered, wp, mp, vp)


def _pack_small(gains4, w_pool, pool_scale, rel_bias_t, sinks, loss):
    bias_rows = jnp.pad(rel_bias_t, ((0, 0), (0, 128 - rel_bias_t.shape[1])))
    parts = list(gains4) + [w_pool, pool_scale, bias_rows, sinks, loss]
    rows = []
    for a in parts:
        flat = a.reshape(-1)
        n = flat.shape[0]
        padded = -(-n // 1024) * 1024
        rows.append(jnp.pad(flat, (0, padded - n)).reshape(-1, 128))
    return jnp.concatenate(rows, axis=0)


def kernel(x, g_pre_mix, w_in, w_pool, pool_scale, rel_bias, sinks, w_out, g_post_mix, g_pre_ffn, w_gate, w_up, w_down, g_post_ffn, loss_target, m_g_pre_mix, m_w_in, m_w_pool, m_pool_scale, m_rel_bias, m_sinks, m_w_out, m_g_post_mix, m_g_pre_ffn, m_w_gate, m_w_up, m_w_down, m_g_post_ffn, v_g_pre_mix, v_w_in, v_w_pool, v_pool_scale, v_rel_bias, v_sinks, v_w_out, v_g_post_mix, v_g_pre_ffn, v_w_gate, v_w_up, v_w_down, v_g_post_ffn):
    c = lax.axis_index("c")
    chip = 2 * lax.axis_index("x") + lax.axis_index("y")

    def shards(a_in, a_out, a_gate, a_up, a_down):
        return (a_in[0].T, a_out[0], a_gate[0].T, a_up[0].T, a_down[0])

    c_arr = c.reshape(1).astype(jnp.int32)
    chip_arr = chip.reshape(1).astype(jnp.int32)

    big_w = shards(w_in, w_out, w_gate, w_up, w_down)
    big_bf = [w.astype(BF16) for w in big_w]
    g_ssem, g_rsem, g_srcs, g_lands, _ = _split_start(
        _gather_copies, 15, big_bf, [lax.empty((NSH,) + a.shape, BF16) for a in big_bf], None, "gather_start")
    fw = {}

    def with_own(land, shard):
        return lax.dynamic_update_slice(land, shard[None], (chip, 0, 0))

    def w_in_ready(*after):
        fw["first"] = _split_wait(_gather_copies, g_ssem, g_rsem, g_srcs, g_lands, after, "gather_win_wait",
                                  only=(0, 1, 2))
        (fw["win"],) = _gather_finish([with_own(fw["first"][5], fw["first"][0])], "win")
        return fw["win"].reshape(IN_W, D)

    def w_out_ready(*after):
        first = fw["first"]
        fw["second"] = _split_wait(_gather_copies, g_ssem, g_rsem, first[:5], [fw["win"]] + list(first[6:]), after,
                                   "gather_wout_wait", only=(3, 4, 5))
        (fw["wout"],) = _gather_finish([with_own(fw["second"][6], fw["second"][1])], "wout")
        return fw["wout"].reshape(D, D), None

    def ffn_weights(after):
        second = fw["second"]
        outs = _split_wait(_gather_copies, g_ssem, g_rsem, second[:5], [second[5], fw["wout"]] + list(second[7:]),
                           [after], "gather_ffn_wait", only=tuple(range(6, 15)))
        return _gather_finish([with_own(land, src) for src, land in zip(outs[2:5], outs[7:])], "ffn")

    rs = {}

    def on_ffn_grads(ffn_g):
        oths = ffn_g[1::2]
        rs["ffn_own"] = ffn_g[0::2]
        rs["x"] = _split_start(_sibling_copies, 3, oths, [lax.empty(a.shape, BF16) for a in oths], ffn_g[0],
                               "rs_exchange_ffn_start")
        return rs["x"][4]

    def on_wout_grads(own, oth, after):
        ssem, rsem, srcs, lands, _ = rs["x"]
        recv_ffn = _split_wait(_sibling_copies, ssem, rsem, srcs, lands, [after], "rs_exchange_ffn_wait")[3:]
        recv_wout = _to_sibling([oth], "rs_exchange_wout")
        outs = _chip_partial([own] + list(rs["ffn_own"]), list(recv_wout) + list(recv_ffn), chip_arr, "early")
        rs["early_own"] = outs[4:]
        rs["s"] = _split_start(_scatter_copies, 12, outs[:4],
                               [lax.empty((3,) + a.shape[1:], BF16) for a in outs[:4]], outs[4], "scatter_early_start")
        return rs["s"][4]

    small = (g_pre_mix, g_post_mix, g_pre_ffn, g_post_ffn, w_pool[0], pool_scale, rel_bias, sinks)
    loss, gx, small_grads, ((win_own, win_oth), _, _) = _local_step(
        x[0], loss_target[0], small, w_in_ready, w_out_ready, ffn_weights, c_arr, on_ffn_grads, on_wout_grads)

    ssem, rsem, srcs, lands, _ = rs["s"]
    recv_early = _split_wait(_scatter_copies, ssem, rsem, srcs, lands, [gx], "scatter_early_wait")[4:]
    finals = _sum_chips(list(rs["early_own"]), list(recv_early), "early")
    to_sib = [win_oth] + list(finals)
    sh_ssem, sh_rsem, sh_srcs, sh_lands, _ = _split_start(
        _sibling_copies, 5, to_sib, [lax.empty(a.shape, a.dtype) for a in to_sib], None, "rs_share_start")

    unused = jnp.zeros((1, 1), F32)
    gp = _pack_small(small_grads[:4], *small_grads[4:], loss)
    wp = _pack_small((g_pre_mix, g_post_mix, g_pre_ffn, g_post_ffn), w_pool, pool_scale, rel_bias.T, sinks, unused)
    mp = _pack_small((m_g_pre_mix, m_g_post_mix, m_g_pre_ffn, m_g_post_ffn), m_w_pool, m_pool_scale, m_rel_bias.T,
                     m_sinks, unused)
    vp = _pack_small((v_g_pre_mix, v_g_post_mix, v_g_pre_ffn, v_g_post_ffn), v_w_pool, v_pool_scale, v_rel_bias.T,
                     v_sinks, unused)

    moments = (shards(m_w_in, m_w_out, m_w_gate, m_w_up, m_w_down),
               shards(v_w_in, v_w_out, v_w_gate, v_w_up, v_w_down))
    sh_first = _split_wait(_sibling_copies, sh_ssem, sh_rsem, sh_srcs, sh_lands, [], "rs_share_win_wait", only=(0,))
    outs = _chip_partial([win_own], [sh_first[5]], chip_arr, "win")
    slots = lax.dynamic_update_slice(lax.empty((8,) + gp.shape, F32), gp[None], (2 * chip + c, 0, 0))
    w_ssem, w_rsem, w_srcs, w_lands, w_token = _split_start(
        _win_and_small_copies, 10, [outs[0], gp], [lax.empty((3,) + outs[0].shape[1:], BF16), slots], gx,
        "scatter_win_start")
    shared = _split_wait(_sibling_copies, sh_ssem, sh_rsem, sh_first[:5], sh_first[5:], [w_token],
                         "rs_share_early_wait", only=(1, 2, 3, 4))
    adam_e = _adamw_shards(shared[1:5], shared[6:], big_w[1:], moments[0][1:], moments[1][1:], c_arr, "early")
    w_first = _split_wait(_win_and_small_copies, w_ssem, w_rsem, w_srcs, w_lands, [adam_e[0]], "scatter_win_wait",
                          only=(0, 1, 2))
    win_final, win_sib = _sum_chips_shared(outs[1], w_first[2], "win")
    adam_w = _adamw_shards([win_final], [win_sib], big_w[:1], moments[0][:1], moments[1][:1], c_arr, "win")
    big_g, big_d, big_m, big_v = [[adam_w[i]] + list(adam_e[4 * i:4 * i + 4]) for i in range(4)]

    gathered = _split_wait(_win_and_small_copies, w_ssem, w_rsem, w_first[:2], w_first[2:], [adam_w[0]],
                           "small_allgather_wait", only=tuple(range(3, 10)))[3]
    flat = _small_sum_adamw(gathered, wp, mp, vp)
    small_out = [flat[8 * kind:8 * kind + 8] for kind in range(4)]
    total_loss = flat[-1][0, 0]

    def ordered(small8, big4):
        sg1, sg2, sg3, sg4, swp, ssc, srb, ssk = small8
        swp, srb = swp[None], srb.T
        bwin, bwout, bwg, bwu, bwd = big4[0].T[None], big4[1][None], big4[2].T[None], big4[3].T[None], big4[4][None]
        return [sg1, bwin, swp, ssc, srb, ssk, bwout, sg2, sg3, bwg, bwu, bwd, sg4]

    res = [total_loss, gx[None]]
    for sm, bg in zip(small_out, (big_g, big_d, big_m, big_v)):
        res += ordered(sm, bg)
    return tuple(res)
```

```python
import functools

import numpy as np
import jax
import jax.numpy as jnp
from jax import lax
from jax.experimental import pallas as pl
from jax.experimental.pallas import tpu as pltpu

F32 = jnp.float32
BF16 = jnp.bfloat16

D = 1024
POOL_W = 512
GROUP = 128
WINDOWS = (2, 4, 8, 16)
HALO = 128
NQ = 8
BLK = 128
NBUCKET = 32
IN_W = 1280
DFF = 2816
NSH = 4
FS = DFF // NSH
WIN_S = IN_W // NSH
WOUT_S = D // NSH
EPS = 1e-6
NEG = -1e30
SCALE = 0.125

ADAM_LR = 0.001
ADAM_B1 = 0.9
ADAM_B2 = 0.999
ADAM_EPS = 1e-08
ADAM_WD = 0.01
ADAM_STEP = 10

TM = 512
TM_IN = 1024
TM_POOL = 512
TM_FFN = 512
TM_FFN_FWD = 1024
FFN_ROWS = 256
VMEM_LIMIT = 60 * 1024 * 1024

MESH_T = pl.DeviceIdType.MESH
ANY = pl.BlockSpec(memory_space=pl.ANY)


def _cp(n_grid=0, **kw):
    sem = ("arbitrary",) * n_grid if n_grid else None
    return pltpu.CompilerParams(dimension_semantics=sem, vmem_limit_bytes=VMEM_LIMIT, **kw)


def _dot(a, b):
    return jnp.dot(a, b, preferred_element_type=F32)


def _dot_nt(a, b):
    return lax.dot_general(a, b, (((1,), (1,)), ((), ())), preferred_element_type=F32)


def _dot_tn(a, b):
    return lax.dot_general(a, b, (((0,), (0,)), ((), ())), preferred_element_type=F32)


def _rms(x):
    r = lax.rsqrt(jnp.mean(x * x, axis=-1, keepdims=True) + EPS)
    return r, x * r


def _rms_bwd(r, n, g, dout):
    dn = dout * g
    dx = r * (dn - n * jnp.mean(dn * n, axis=-1, keepdims=True))
    dg = jnp.sum(dout * n, axis=0, keepdims=True)
    return dx, dg


def _split(x, n):
    parts = []
    r = x
    for _ in range(n):
        p = r.astype(BF16)
        parts.append(p)
        r = r - p.astype(F32)
    return parts


def _band_dot(a, x, n):
    acc = None
    for p in _split(x, n):
        t = _dot(a, p)
        acc = t if acc is None else acc + t
    return acc


def _bucket_table():
    qi = np.arange(BLK)[None, :]
    kj = np.arange(2 * BLK)[:, None]
    dist = qi + BLK - kj
    n = np.maximum(dist, 0)
    nf = np.maximum(n, 1).astype(np.float32)
    large = 16 + (np.log(nf / np.float32(16)) / np.float32(np.log(128 / 16)) * np.float32(16)).astype(np.int32)
    large = np.minimum(large, NBUCKET - 1)
    return np.where(n < 16, n, large).astype(np.int32)


def _prenorm(x, g1):
    s = x.shape[0]
    tm = min(TM_IN, s)

    def body(x_ref, g_ref, h_ref):
        _, n = _rms(x_ref[...])
        h_ref[...] = (n * g_ref[...]).astype(BF16)

    row = pl.BlockSpec((tm, D), lambda i: (i, 0))
    return pl.pallas_call(
        body, name="prenorm", grid=(s // tm,),
        in_specs=[row, pl.BlockSpec((1, D), lambda i: (0, 0))], out_specs=row,
        out_shape=jax.ShapeDtypeStruct((s, D), BF16),
        compiler_params=_cp(1))(x, g1)


def _inproj_fwd(h1, w_in):
    s = h1.shape[0]
    tm = min(TM_IN, s)

    def body(h_ref, w_ref, u_ref, q_ref, k_ref, v_ref):
        proj = _dot_nt(h_ref[...], w_ref[...])
        u_ref[...] = proj[:, :512]
        q_ref[...] = proj[:, 512:1024].astype(BF16)
        k_ref[...] = proj[:, 1024:1152].astype(BF16)
        v_ref[...] = proj[:, 1152:1280].astype(BF16)

    row = lambda w: pl.BlockSpec((tm, w), lambda i: (i, 0))
    return pl.pallas_call(
        body, name="inproj_fwd", grid=(s // tm,),
        in_specs=[row(D), pl.BlockSpec((IN_W, D), lambda i: (0, 0))],
        out_specs=[row(512), row(512), row(128), row(128)],
        out_shape=[jax.ShapeDtypeStruct((s, 512), F32), jax.ShapeDtypeStruct((s, 512), BF16),
                   jax.ShapeDtypeStruct((s, 128), BF16), jax.ShapeDtypeStruct((s, 128), BF16)],
        compiler_params=_cp(1))(h1, w_in)


def _window_sums(ext, w, lag_sign, tm, terms):
    rows = lax.broadcasted_iota(jnp.int32, (HALO, 2 * HALO), 0)
    cols = lax.broadcasted_iota(jnp.int32, (HALO, 2 * HALO), 1)
    d = rows + HALO - cols if lag_sign > 0 else cols - rows
    band = jnp.where((d >= 0) & (d < w), 1.0, 0.0).astype(BF16)
    return jnp.concatenate([_band_dot(band, ext[r:r + 2 * HALO], terms) for r in range(0, tm, HALO)], axis=0)


def _pooled(ext, cur, t0, tm):
    t = t0 + lax.broadcasted_iota(jnp.int32, (tm, GROUP), 0)
    out = []
    for g, w in enumerate(WINDOWS):
        sl = slice(g * GROUP, (g + 1) * GROUP)
        cnt = jnp.minimum(t + 1, w).astype(F32)
        out.append(_window_sums(ext[:, sl], w, 1, tm, 2) / cnt - cur[:, sl])
    return out


def _pool_fwd(u, w_pool, pool_scale):
    s = u.shape[0]
    tm = min(TM_POOL, s)
    hb = tm // HALO

    def body(uc_ref, uh_ref, wp_ref, sc_ref, o_ref, pooled_ref):
        i = pl.program_id(0)
        cur = uc_ref[...]
        halo = jnp.where(i > 0, uh_ref[...], 0.0)
        ext = jnp.concatenate([halo, cur], axis=0)
        pooled = _pooled(ext, cur, i * tm, tm)
        for g in range(4):
            sl = slice(g * GROUP, (g + 1) * GROUP)
            pb = pooled[g].astype(BF16)
            pooled_ref[:, sl] = pb
            o_ref[:, sl] = (_dot(pb, wp_ref[g]) * sc_ref[:, sl]).astype(BF16)

    row = pl.BlockSpec((tm, 512), lambda i: (i, 0))
    return pl.pallas_call(
        body, name="pool_fwd", grid=(s // tm,),
        in_specs=[row, pl.BlockSpec((HALO, 512), lambda i: (jnp.maximum(i * hb - 1, 0), 0)),
                  pl.BlockSpec((4, GROUP, GROUP), lambda i: (0, 0, 0)),
                  pl.BlockSpec((1, 512), lambda i: (0, 0))],
        out_specs=[row, row],
        out_shape=[jax.ShapeDtypeStruct((s, 512), BF16)] * 2,
        compiler_params=_cp(1))(u, u, w_pool, pool_scale)


def _bias_table(bucket, rel_bias):
    def body(b_ref, rb_ref, o_ref):
        bucket_v = b_ref[...]
        key = lax.broadcasted_iota(jnp.int32, (2 * BLK, BLK), 0)
        dist = lax.broadcasted_iota(jnp.int32, (2 * BLK, BLK), 1) + BLK - key
        inwin = (dist >= 0) & (dist < BLK)
        for h in range(NQ):
            acc = jnp.zeros((2 * BLK, BLK), F32)
            for b in range(NBUCKET):
                acc = jnp.where(bucket_v == b, rb_ref[b, h], acc)
            rest = jnp.where(inwin, acc, NEG)
            o_ref[1, h] = rest
            o_ref[0, h] = jnp.where(key >= BLK, rest, NEG)

    return pl.pallas_call(
        body, name="bias_table",
        in_specs=[pl.BlockSpec(memory_space=pltpu.VMEM), pl.BlockSpec(memory_space=pltpu.SMEM)],
        out_specs=pl.BlockSpec(memory_space=pltpu.VMEM),
        out_shape=jax.ShapeDtypeStruct((2, NQ, 2 * BLK, BLK), F32),
        compiler_params=_cp())(bucket, rel_bias)


HD = 64


def _kv_band(ref, n, transposed):
    pstart = pl.multiple_of(jnp.maximum(n - 1, 0) * BLK, BLK)
    cstart = pl.multiple_of(n * BLK, BLK)
    lo = lax.broadcasted_iota(jnp.int32, (2 * BLK, 128), 1) < HD
    band = jnp.concatenate([ref[pl.ds(pstart, BLK), :], ref[pl.ds(cstart, BLK), :]], axis=0).astype(F32)
    rot = pltpu.roll(band, HD, axis=1)
    halves = ((jnp.where(lo, band, 0.0), jnp.where(lo, 0.0, rot)), (jnp.where(lo, rot, 0.0), jnp.where(lo, 0.0, band)))
    if transposed:
        out = [jnp.concatenate([a.T, b.T], axis=1).astype(BF16) for a, b in halves]
    else:
        out = [jnp.concatenate([a, b], axis=0).astype(BF16) for a, b in halves]
    return out, (lo, pstart, cstart)


def _pair_probs(qt, kk, bias, sk_ref, p):
    sink = jnp.concatenate([jnp.full((1, 1, BLK), sk_ref[0, 2 * p + e], F32) for e in range(2)], axis=0)
    s = _dot_nt(kk, qt).reshape(2, 2 * BLK, BLK) + bias
    m = jnp.maximum(jnp.max(s, axis=1, keepdims=True), sink)
    pr = jnp.exp(s - m)
    es = jnp.exp(sink - m)
    inv = 1.0 / (jnp.sum(pr, axis=1, keepdims=True) + es)
    return pr * inv, es * inv


def _attn_fwd(q, k, v, bias, sinks):
    s = q.shape[0]
    nblk = s // BLK

    def body(q_ref, k_ref, v_ref, b_ref, sk_ref, o_ref, prob_ref, ps_ref):
        n = pl.program_id(0)
        kk, _ = _kv_band(k_ref, n, False)
        vt, _ = _kv_band(v_ref, n, True)
        bidx = jnp.minimum(n, 1)
        for p in range(4):
            h = p // 2
            qt = (q_ref[:, p * 128:(p + 1) * 128].astype(F32) * SCALE).astype(BF16)
            prob, ps = _pair_probs(qt, kk[h], b_ref[bidx, pl.ds(2 * p, 2)], sk_ref, p)
            pb = prob.astype(BF16)
            prob_ref[pl.ds(2 * p, 2)] = pb
            ps_ref[pl.ds(2 * p, 2), :] = ps.reshape(2, BLK)
            acc_t = _dot(vt[h], pb.reshape(4 * BLK, BLK))
            o_ref[:, p * 128:(p + 1) * 128] = acc_t.T.astype(BF16)

    return pl.pallas_call(
        body, name="attn_fwd", grid=(nblk,),
        in_specs=[pl.BlockSpec((BLK, 512), lambda i: (i, 0)),
                  pl.BlockSpec((s, 128), lambda i: (0, 0)),
                  pl.BlockSpec((s, 128), lambda i: (0, 0)),
                  pl.BlockSpec((2, NQ, 2 * BLK, BLK), lambda i: (0, 0, 0, 0)),
                  pl.BlockSpec(memory_space=pltpu.SMEM)],
        out_specs=[pl.BlockSpec((BLK, 512), lambda i: (i, 0)),
                   pl.BlockSpec((None, NQ, 2 * BLK, BLK), lambda i: (i, 0, 0, 0)),
                   pl.BlockSpec((None, NQ, BLK), lambda i: (i, 0, 0))],
        out_shape=[jax.ShapeDtypeStruct((s, 512), BF16), jax.ShapeDtypeStruct((nblk, NQ, 2 * BLK, BLK), BF16),
                   jax.ShapeDtypeStruct((nblk, NQ, BLK), F32)],
        compiler_params=_cp(1))(q, k, v, bias, sinks)


def _outproj_fwd(pool_o, attn_o, w_out, x, g2, g3):
    s = x.shape[0]
    tm = min(TM, s)

    def body(p_ref, a_ref, w_ref, x_ref, g2_ref, g3_ref, mix_ref, x1_ref, h2_ref):
        mix = _dot(p_ref[...], w_ref[0:512, :]) + _dot(a_ref[...], w_ref[512:1024, :])
        mix_ref[...] = mix
        _, n2 = _rms(mix)
        x1 = x_ref[...] + n2 * g2_ref[...]
        x1_ref[...] = x1
        _, n3 = _rms(x1)
        h2_ref[...] = (n3 * g3_ref[...]).astype(BF16)

    row = lambda w: pl.BlockSpec((tm, w), lambda i: (i, 0))
    vec = pl.BlockSpec((1, D), lambda i: (0, 0))
    return pl.pallas_call(
        body, name="outproj_fwd", grid=(s // tm,),
        in_specs=[row(512), row(512), pl.BlockSpec((D, D), lambda i: (0, 0)), row(D), vec, vec],
        out_specs=[row(D), row(D), row(D)],
        out_shape=[jax.ShapeDtypeStruct((s, D), F32), jax.ShapeDtypeStruct((s, D), F32),
                   jax.ShapeDtypeStruct((s, D), BF16)],
        compiler_params=_cp(1))(pool_o, attn_o, w_out, x, g2, g3)


def _silu_parts(g):
    sg = jax.nn.sigmoid(g)
    return sg, g * sg


def _ffn_fwd(h2, wg, wu, wd, x1, target, g4):
    s = h2.shape[0]
    tm = min(TM_FFN_FWD, s)

    def body(h_ref, wg_ref, wu_ref, wd_ref, x1_ref, t_ref, g4_ref,
             gate_ref, up_ref, a_ref, df_ref, dy_ref, loss_ref, gg4_ref, f_acc):
        i = pl.program_id(0)
        j = pl.program_id(1)
        first = j == 0
        chunks = [pl.ds(r, min(FFN_ROWS, tm)) for r in range(0, tm, FFN_ROWS)]
        project = lambda rows: (_dot_nt(h_ref[rows, :], wg_ref[...]), _dot_nt(h_ref[rows, :], wu_ref[...]))
        ahead = [project(chunks[0])]
        for k, rows in enumerate(chunks):
            if k + 1 < len(chunks):
                ahead.append(project(chunks[k + 1]))
            gate, up = ahead[k]
            gate_ref[rows, :] = gate.astype(BF16)
            up_ref[rows, :] = up.astype(BF16)
            _, sl = _silu_parts(gate)
            a = (sl * up).astype(BF16)
            a_ref[rows, :] = a
            contrib = _dot(a, wd_ref[...])
            f_acc[rows, :] = jnp.where(first, contrib, f_acc[rows, :] + contrib)

        @pl.when((i == 0) & (j == 0))
        def _():
            loss_ref[...] = jnp.zeros_like(loss_ref)
            gg4_ref[...] = jnp.zeros_like(gg4_ref)

        @pl.when(j == NSH - 1)
        def _():
            r4, n4 = _rms(f_acc[...])
            g4v = g4_ref[...]
            err = x1_ref[...] + n4 * g4v - t_ref[...]
            loss_ref[...] += (0.5 / D) * jnp.sum(err * err).reshape(1, 1)
            dy = err * (1.0 / D)
            dy_ref[...] = dy
            df, dg = _rms_bwd(r4, n4, g4v, dy)
            gg4_ref[...] += dg
            df_ref[...] = df.astype(BF16)

    row = lambda w: pl.BlockSpec((tm, w), lambda i, j: (i, 0))
    sh = lambda r, c: pl.BlockSpec((None, r, c), lambda i, j: (j, 0, 0))
    act = pl.BlockSpec((None, tm, FS), lambda i, j: (j, i, 0))
    vec = pl.BlockSpec((1, D), lambda i, j: (0, 0))
    return pl.pallas_call(
        body, name="ffn_fwd", grid=(s // tm, NSH),
        in_specs=[row(D), sh(FS, D), sh(FS, D), sh(FS, D), row(D), row(D), vec],
        out_specs=[act, act, act, row(D), row(D), pl.BlockSpec((1, 1), lambda i, j: (0, 0)), vec],
        out_shape=[jax.ShapeDtypeStruct((NSH, s, FS), BF16)] * 3
        + [jax.ShapeDtypeStruct((s, D), BF16), jax.ShapeDtypeStruct((s, D), F32),
           jax.ShapeDtypeStruct((1, 1), F32), jax.ShapeDtypeStruct((1, D), F32)],
        scratch_shapes=[pltpu.VMEM((tm, D), F32)],
        compiler_params=_cp(2))(h2, wg, wu, wd, x1, target, g4)


def _ffn_bwd_act(df, gate, up, wg, wu, wd, dy, x1, mix, g3, g2):
    s = df.shape[0]
    tm = min(TM_FFN, s)

    def body(df_ref, gate_ref, up_ref, wg_ref, wu_ref, wd_ref, dy_ref, x1_ref, mix_ref, g3_ref, g2_ref,
             dgate_ref, dup_ref, dx1_ref, dmix_ref, gg3_ref, gg2_ref, acc):
        j = pl.program_id(0)
        i = pl.program_id(1)
        first = j == 0
        tile = pl.multiple_of(i * tm, tm)
        chunks = [pl.ds(r, min(FFN_ROWS, tm)) for r in range(0, tm, FFN_ROWS)]

        @pl.when((i == 0) & (j == 0))
        def _():
            gg3_ref[...] = jnp.zeros_like(gg3_ref)
            gg2_ref[...] = jnp.zeros_like(gg2_ref)

        def norms_backward(rows, dh2):
            r3, n3 = _rms(x1_ref[rows, :])
            dx1n, dg3 = _rms_bwd(r3, n3, g3_ref[...], dh2)
            dx1 = dy_ref[rows, :] + dx1n
            dx1_ref[rows, :] = dx1
            gg3_ref[...] += dg3
            r2, n2 = _rms(mix_ref[rows, :])
            dmix, dg2 = _rms_bwd(r2, n2, g2_ref[...], dx1)
            gg2_ref[...] += dg2
            dmix_ref[rows, :] = dmix.astype(BF16)

        def shard_pass(last):
            das = [_dot_nt(df_ref[chunks[0], :], wd_ref[...])]
            for k, rows in enumerate(chunks):
                if k + 1 < len(chunks):
                    das.append(_dot_nt(df_ref[chunks[k + 1], :], wd_ref[...]))
                da = das[k]
                g = gate_ref[rows, :].astype(F32)
                u = up_ref[rows, :].astype(F32)
                sg, sl = _silu_parts(g)
                dgate = (da * u * (sg * (1.0 + g * (1.0 - sg)))).astype(BF16)
                dup = (da * sl).astype(BF16)
                dgate_ref[rows, :] = dgate
                dup_ref[rows, :] = dup
                contrib = _dot(dgate, wg_ref[...]) + _dot(dup, wu_ref[...])
                acc_rows = pl.ds(tile + k * FFN_ROWS, rows.size)
                if last:
                    norms_backward(rows, acc[acc_rows, :] + contrib)
                else:
                    acc[acc_rows, :] = jnp.where(first, contrib, acc[acc_rows, :] + contrib)

        pl.when(j < NSH - 1)(functools.partial(shard_pass, False))
        pl.when(j == NSH - 1)(functools.partial(shard_pass, True))

    row = pl.BlockSpec((tm, D), lambda j, i: (i, 0))
    last = pl.BlockSpec((tm, D), lambda j, i: (i * (j // (NSH - 1)), 0))
    sh = pl.BlockSpec((None, FS, D), lambda j, i: (j, 0, 0))
    act = pl.BlockSpec((None, tm, FS), lambda j, i: (j, i, 0))
    vec = pl.BlockSpec((1, D), lambda j, i: (0, 0))
    return pl.pallas_call(
        body, name="ffn_bwd_act", grid=(NSH, s // tm),
        in_specs=[row, act, act, sh, sh, sh, last, last, last, vec, vec],
        out_specs=[act, act, last, last, vec, vec],
        out_shape=[jax.ShapeDtypeStruct((NSH, s, FS), BF16), jax.ShapeDtypeStruct((NSH, s, FS), BF16),
                   jax.ShapeDtypeStruct((s, D), F32), jax.ShapeDtypeStruct((s, D), BF16),
                   jax.ShapeDtypeStruct((1, D), F32), jax.ShapeDtypeStruct((1, D), F32)],
        scratch_shapes=[pltpu.VMEM((s, D), F32)],
        compiler_params=_cp(2))(df, gate, up, wg, wu, wd, dy, x1, mix, g3, g2)


SMEM_SPEC = pl.BlockSpec(memory_space=pltpu.SMEM)


def _emit_halves(acc_ref, row0, rows, c, own_ref, oth_ref):
    half = rows // 2
    own_ref[...] = acc_ref[pl.ds(row0 + pl.multiple_of(c * half, 8), half), :]
    oth_ref[...] = acc_ref[pl.ds(row0 + pl.multiple_of((1 - c) * half, 8), half), :].astype(BF16)


def _half_shapes(rows):
    return [jax.ShapeDtypeStruct((NSH, rows // 2, D), F32), jax.ShapeDtypeStruct((NSH, rows // 2, D), BF16)]


def _ffn_bwd_w(h2, act_a, dgate, dup, df, c_arr):
    s = h2.shape[0]
    tm = min(TM_FFN_FWD, s)
    nt = s // tm

    def body(c_ref, h_ref, a_ref, dgate_ref, dup_ref, df_ref, *rest):
        outs, accs = rest[:6], rest[6:]
        i = pl.program_id(1)

        @pl.when(i == 0)
        def _():
            for acc in accs:
                acc[...] = jnp.zeros_like(acc)

        h = h_ref[...]
        accs[0][...] += _dot_tn(dgate_ref[...], h)
        accs[1][...] += _dot_tn(dup_ref[...], h)
        accs[2][...] += _dot_tn(a_ref[...], df_ref[...])

        @pl.when(i == nt - 1)
        def _():
            for t, acc in enumerate(accs):
                _emit_halves(acc, 0, FS, c_ref[0], outs[2 * t], outs[2 * t + 1])

    row = lambda w: pl.BlockSpec((tm, w), lambda j, i: (i, 0))
    act = pl.BlockSpec((None, tm, FS), lambda j, i: (j, i, 0))
    half = pl.BlockSpec((None, FS // 2, D), lambda j, i: (j, 0, 0))
    return pl.pallas_call(
        body, name="ffn_bwd_w", grid=(NSH, nt),
        in_specs=[SMEM_SPEC, row(D), act, act, act, row(D)],
        out_specs=[half] * 6,
        out_shape=_half_shapes(FS) * 3,
        scratch_shapes=[pltpu.VMEM((FS, D), F32)] * 3,
        compiler_params=_cp(2))(c_arr, h2, act_a, dgate, dup, df)


def _outproj_bwd(dmix, w_out, pool_o, attn_o, c_arr, dep=None):
    s = dmix.shape[0]
    tm = min(TM, s)
    nt = s // tm

    def body(c_ref, dm_ref, w_ref, p_ref, a_ref, *rest):
        dpool_ref, dattn_ref, own_ref, oth_ref, gw_ref = rest[-5:]
        i = pl.program_id(0)

        @pl.when(i == 0)
        def _():
            gw_ref[...] = jnp.zeros_like(gw_ref)

        dm = dm_ref[...]
        dpool_ref[...] = _dot_nt(dm, w_ref[0:512, :])
        dattn_ref[...] = _dot_nt(dm, w_ref[512:1024, :]).astype(BF16)
        gw_ref[0:512, :] += _dot_tn(p_ref[...], dm)
        gw_ref[512:1024, :] += _dot_tn(a_ref[...], dm)

        @pl.when(i == nt - 1)
        def _():
            for k in range(NSH):
                _emit_halves(gw_ref, k * WOUT_S, WOUT_S, c_ref[0], own_ref.at[k], oth_ref.at[k])

    row = lambda w: pl.BlockSpec((tm, w), lambda i: (i, 0))
    full = pl.BlockSpec((D, D), lambda i: (0, 0))
    half = pl.BlockSpec((NSH, WOUT_S // 2, D), lambda i: (0, 0, 0))
    return pl.pallas_call(
        body, name="outproj_bwd", grid=(nt,),
        in_specs=[SMEM_SPEC, row(D), full, row(512), row(512)] + ([] if dep is None else [ANY]),
        out_specs=[row(512), row(512), half, half],
        out_shape=[jax.ShapeDtypeStruct((s, 512), F32), jax.ShapeDtypeStruct((s, 512), BF16)] + _half_shapes(WOUT_S),
        scratch_shapes=[pltpu.VMEM((D, D), F32)],
        compiler_params=_cp(1))(c_arr, dmix, w_out, pool_o, attn_o, *([] if dep is None else [dep]))


def _pool_bwd(pooled, dpool, w_pool, pool_scale, dep=None):
    s = pooled.shape[0]
    tm = min(TM_POOL, s)
    hb = tm // HALO
    nt = s // tm
    last_halo = s // HALO - 1

    def body(p_ref, dc_ref, dn_ref, wp_ref, sc_ref, *rest):
        du_ref, gwp_ref, gsc_ref = rest[-3:]
        i = pl.program_id(0)

        @pl.when(i == 0)
        def _():
            gwp_ref[...] = jnp.zeros_like(gwp_ref)
            gsc_ref[...] = jnp.zeros_like(gsc_ref)

        dcur = dc_ref[...]
        dnext = jnp.where(i < nt - 1, dn_ref[...], 0.0)
        dext = jnp.concatenate([dcur, dnext], axis=0)
        t_ext = i * tm + lax.broadcasted_iota(jnp.int32, (tm + HALO, GROUP), 0)
        for g, w in enumerate(WINDOWS):
            sl = slice(g * GROUP, (g + 1) * GROUP)
            pb = p_ref[:, sl]
            wp = wp_ref[g]
            mixed = _dot(pb, wp)
            gsc_ref[:, sl] += jnp.sum(dcur[:, sl] * mixed, axis=0, keepdims=True)
            dmixed = (dext[:, sl] * sc_ref[:, sl]).astype(BF16)
            gwp_ref[g] += _dot_tn(pb, dmixed[0:tm])
            dpooled = _dot_nt(dmixed, wp)
            z = dpooled / jnp.minimum(t_ext + 1, w).astype(F32)
            du_ref[:, sl] = (_window_sums(z, w, -1, tm, 2) - dpooled[0:tm]).astype(BF16)

    return pl.pallas_call(
        body, name="pool_bwd", grid=(nt,),
        in_specs=[pl.BlockSpec((tm, 512), lambda i: (i, 0)),
                  pl.BlockSpec((tm, 512), lambda i: (i, 0)),
                  pl.BlockSpec((HALO, 512), lambda i: (jnp.minimum((i + 1) * hb, last_halo), 0)),
                  pl.BlockSpec((4, GROUP, GROUP), lambda i: (0, 0, 0)),
                  pl.BlockSpec((1, 512), lambda i: (0, 0))] + ([] if dep is None else [ANY]),
        out_specs=[pl.BlockSpec((tm, 512), lambda i: (i, 0)),
                   pl.BlockSpec((4, GROUP, GROUP), lambda i: (0, 0, 0)),
                   pl.BlockSpec((1, 512), lambda i: (0, 0))],
        out_shape=[jax.ShapeDtypeStruct((s, 512), BF16), jax.ShapeDtypeStruct((4, GROUP, GROUP), F32),
                   jax.ShapeDtypeStruct((1, 512), F32)],
        compiler_params=_cp(1))(pooled, dpool, dpool, w_pool, pool_scale, *([] if dep is None else [dep]))


def _attn_bwd(q, k, v, do, probs, sink_share, bucket, dep=None):
    s = q.shape[0]
    nblk = s // BLK

    def body(q_ref, do_ref, k_ref, v_ref, prob_ref, ps_ref, bk_ref, *rest):
        dq_ref, dk_ref, dv_ref, grb_ref, gsk_ref, dss_ref = rest[-6:]
        n = pl.program_id(0)

        @pl.when(n == 0)
        def _():
            dk_ref[...] = jnp.zeros_like(dk_ref)
            dv_ref[...] = jnp.zeros_like(dv_ref)
            dss_ref[...] = jnp.zeros_like(dss_ref)
            gsk_ref[...] = jnp.zeros_like(gsk_ref)

        kt, (lo, pstart, cstart) = _kv_band(k_ref, n, True)
        vv, _ = _kv_band(v_ref, n, False)
        lo_q = lax.broadcasted_iota(jnp.int32, (BLK, 128), 1) < HD
        dkk = [jnp.zeros((2 * BLK, 128), F32), jnp.zeros((2 * BLK, 128), F32)]
        dvv = [jnp.zeros((2 * BLK, 128), F32), jnp.zeros((2 * BLK, 128), F32)]

        def by_head(t):
            return jnp.concatenate([jnp.where(lo_q, t, 0.0), jnp.where(lo_q, 0.0, t)], axis=0).astype(BF16)

        for p in range(4):
            h = p // 2
            qt = q_ref[:, p * 128:(p + 1) * 128].astype(F32) * SCALE
            dot = do_ref[:, p * 128:(p + 1) * 128]
            pb = prob_ref[pl.ds(2 * p, 2)]
            prob = pb.astype(F32)
            ps = ps_ref[pl.ds(2 * p, 2), :].reshape(2, 1, BLK)
            dp = _dot_nt(vv[h], dot).reshape(2, 2 * BLK, BLK)
            delta = jnp.sum(prob * dp, axis=1, keepdims=True)
            ds = prob * (dp - delta)
            sink_part = ps * delta
            for e in range(2):
                gs = -jnp.sum(sink_part[e]).reshape(1, 1)
                gsk_ref[pl.ds(2 * p + e, 1), :] += jnp.broadcast_to(gs, (1, 128))
            dss_ref[pl.ds(2 * p, 2)] += ds
            dsb = ds.astype(BF16)
            dq_t = _dot(kt[h], dsb.reshape(4 * BLK, BLK))
            dq_ref[:, p * 128:(p + 1) * 128] = (dq_t.T * SCALE).astype(BF16)
            dkk[h] = dkk[h] + _dot(jnp.concatenate([dsb[0], dsb[1]], axis=1), by_head(qt))
            dvv[h] = dvv[h] + _dot(jnp.concatenate([pb[0], pb[1]], axis=1), by_head(dot.astype(F32)))
        for acc, ref in ((dkk, dk_ref), (dvv, dv_ref)):
            f0 = acc[0] + pltpu.roll(acc[0], HD, axis=1)
            f1 = acc[1] + pltpu.roll(acc[1], HD, axis=1)
            band = jnp.where(lo, f0, f1)
            ref[pl.ds(pstart, BLK), :] += band[0:BLK]
            ref[pl.ds(cstart, BLK), :] += band[BLK:2 * BLK]

        @pl.when(n == nblk - 1)
        def _():
            _relbias_grad(dss_ref, bk_ref[...], grb_ref)

    blk = pl.BlockSpec((BLK, 512), lambda i: (i, 0))
    kv = pl.BlockSpec((s, 128), lambda i: (0, 0))
    row8 = pl.BlockSpec((NQ, 128), lambda i: (0, 0))
    return pl.pallas_call(
        body, name="attn_bwd", grid=(nblk,),
        in_specs=[blk, blk, kv, kv, pl.BlockSpec((None, NQ, 2 * BLK, BLK), lambda i: (i, 0, 0, 0)),
                  pl.BlockSpec((None, NQ, BLK), lambda i: (i, 0, 0)), pl.BlockSpec((2 * BLK, BLK), lambda i: (0, 0))]
        + ([] if dep is None else [ANY]),
        out_specs=[blk, kv, kv, row8, row8],
        out_shape=[jax.ShapeDtypeStruct((s, 512), BF16), jax.ShapeDtypeStruct((s, 128), F32),
                   jax.ShapeDtypeStruct((s, 128), F32), jax.ShapeDtypeStruct((NQ, 128), F32),
                   jax.ShapeDtypeStruct((NQ, 128), F32)],
        scratch_shapes=[pltpu.VMEM((NQ, 2 * BLK, BLK), F32)],
        compiler_params=_cp(1))(q, do, k, v, probs, sink_share, bucket, *([] if dep is None else [dep]))


def _relbias_grad(ds_ref, bucket_v, o_ref):
    lane = lax.broadcasted_iota(jnp.int32, (NQ, 128), 1)
    head_row = lax.broadcasted_iota(jnp.int32, (NQ, BLK), 0)
    acc = jnp.zeros((NQ, 128), F32)
    for b in range(NBUCKET):
        sel = bucket_v == b
        stack = jnp.zeros((NQ, BLK), F32)
        for h in range(NQ):
            col_sums = jnp.sum(jnp.where(sel, ds_ref[h], 0.0), axis=0, keepdims=True)
            stack = jnp.where(head_row == h, col_sums, stack)
        acc = acc + jnp.where(lane == b, jnp.sum(stack, axis=1, keepdims=True), 0.0)
    o_ref[...] = acc


def _inproj_bwd(x, g1, du, dq, dk, dv, w_in, dx1, c_arr):
    s = x.shape[0]
    tm = min(TM, s)
    nt = s // tm
    cols = ((0, 512), (512, 1024), (1024, 1152), (1152, 1280))

    def body(c_ref, x_ref, g_ref, du_ref, dq_ref, dk_ref, dv_ref, w_ref, dx1_ref,
             gx_ref, own_ref, oth_ref, gg_ref, gw_ref):
        i = pl.program_id(0)

        @pl.when(i == 0)
        def _():
            gw_ref[...] = jnp.zeros_like(gw_ref)
            gg_ref[...] = jnp.zeros_like(gg_ref)

        parts = (du_ref[...], dq_ref[...], dk_ref[...].astype(BF16), dv_ref[...].astype(BF16))
        dh = None
        for (a, b), dpart in zip(cols, parts):
            t = _dot(dpart, w_ref[a:b, :])
            dh = t if dh is None else dh + t
        gv = g_ref[...]
        r1, n1 = _rms(x_ref[...])
        h = (n1 * gv).astype(BF16)
        for (a, b), dpart in zip(cols, parts):
            gw_ref[a:b, :] += _dot_tn(dpart, h)
        dx, dg = _rms_bwd(r1, n1, gv, dh)
        gg_ref[...] += dg
        gx_ref[...] = dx1_ref[...] + dx

        @pl.when(i == nt - 1)
        def _():
            for k in range(NSH):
                _emit_halves(gw_ref, k * WIN_S, WIN_S, c_ref[0], own_ref.at[k], oth_ref.at[k])

    row = lambda w: pl.BlockSpec((tm, w), lambda i: (i, 0))
    vec = pl.BlockSpec((1, D), lambda i: (0, 0))
    full = pl.BlockSpec((IN_W, D), lambda i: (0, 0))
    half = pl.BlockSpec((NSH, WIN_S // 2, D), lambda i: (0, 0, 0))
    return pl.pallas_call(
        body, name="inproj_bwd", grid=(nt,),
        in_specs=[SMEM_SPEC, row(D), vec, row(512), row(512), row(128), row(128), full, row(D)],
        out_specs=[row(D), half, half, vec],
        out_shape=[jax.ShapeDtypeStruct((s, D), F32)] + _half_shapes(WIN_S) + [jax.ShapeDtypeStruct((1, D), F32)],
        scratch_shapes=[pltpu.VMEM((IN_W, D), F32)],
        compiler_params=_cp(1))(c_arr, x, g1, du, dq, dk, dv, w_in, dx1)


def _local_step(x, target, small, w_in, w_out, ffn_weights, c_arr, on_ffn_grads=None, on_wout_grads=None):
    g1, g2, g3, g4, w_pool, pool_scale, rel_bias, sinks = small
    bucket = jnp.asarray(_bucket_table())
    wp_bf = w_pool.astype(BF16)
    bias = _bias_table(bucket, rel_bias)
    h1 = _prenorm(x, g1)
    if callable(w_in):
        w_in = w_in(bias, h1)
    u, q, k, v = _inproj_fwd(h1, w_in)
    pool_o, pooled = _pool_fwd(u, wp_bf, pool_scale)
    attn_o, probs, sink_share = _attn_fwd(q, k, v, bias, sinks)
    g2_fwd = g2
    if callable(w_out):
        w_out, after = w_out(pool_o, attn_o)
        if after is not None:
            g2_fwd = g2 + after[:1, :1]
    mix, x1, h2 = _outproj_fwd(pool_o, attn_o, w_out, x, g2_fwd, g3)
    wg, wu, wd = ffn_weights(h2) if callable(ffn_weights) else ffn_weights
    gate, up, act_a, df, dy, loss, gg4 = _ffn_fwd(h2, wg, wu, wd, x1, target, g4)
    dgate, dup, dx1, dmix, gg3, gg2 = _ffn_bwd_act(df, gate, up, wg, wu, wd, dy, x1, mix, g3, g2)
    ffn_g = _ffn_bwd_w(h2, act_a, dgate, dup, df, c_arr)
    dep = None if on_ffn_grads is None else on_ffn_grads(ffn_g)
    dpool, dattn, wout_own, wout_oth = _outproj_bwd(dmix, w_out, pool_o, attn_o, c_arr, dep)
    dep = None if on_wout_grads is None else on_wout_grads(wout_own, wout_oth, dpool)
    du, gwp, gsc = _pool_bwd(pooled, dpool, wp_bf, pool_scale, dep)
    dq, dk, dv, grb, gsk = _attn_bwd(q, k, v, dattn, probs, sink_share, bucket, dep)
    gx, win_own, win_oth, gg1 = _inproj_bwd(x, g1, du, dq, dk, dv, w_in, dx1, c_arr)
    small_grads = (gg1, gg2, gg3, gg4, gwp, gsc, grb, gsk[:, 0].reshape(1, NQ))
    return loss, gx, small_grads, ((win_own, win_oth), (wout_own, wout_oth), ffn_g)


def _place():
    x, y, c = lax.axis_index("x"), lax.axis_index("y"), lax.axis_index("c")
    chips = ((1 - x, y), (x, 1 - y), (1 - x, 1 - y))
    return x, y, c, chips


def _remote(src, dst, ssem, rsem, dev):
    return pltpu.make_async_remote_copy(src_ref=src, dst_ref=dst, send_sem=ssem, recv_sem=rsem,
                                        device_id=dev, device_id_type=MESH_T)


def _to_sibling(arrs, name, after=()):
    nt, na = len(arrs), len(after)

    def body(*refs):
        srcs, dsts = refs[:nt], refs[nt + na:2 * nt + na]
        ssem, rsem = refs[2 * nt + na:]
        x, y, c, _ = _place()
        cps = [_remote(srcs[t], dsts[t], ssem.at[t], rsem.at[t], (x, y, 1 - c)) for t in range(nt)]
        for cp in cps:
            cp.start()
        for cp in cps:
            cp.wait()

    return pl.pallas_call(
        body, name=name, in_specs=[ANY] * (nt + na), out_specs=[ANY] * nt,
        out_shape=[jax.ShapeDtypeStruct(a.shape, a.dtype) for a in arrs],
        scratch_shapes=[pltpu.SemaphoreType.DMA((nt,)), pltpu.SemaphoreType.DMA((nt,))],
        compiler_params=_cp())(*arrs, *after)


HBM_SPEC = pl.BlockSpec(memory_space=pltpu.HBM)
SEM_SPEC = pl.BlockSpec(memory_space=pltpu.SEMAPHORE)
DATAFLOW = pltpu.SideEffectType.DATAFLOW_SIDE_EFFECTING


def _gather_copies(srcs, lands, ssem, rsem):
    x, y, c, chips = _place()
    me = 2 * x + y
    out = []
    for t in range(len(srcs)):
        half = srcs[t].shape[0] // 2
        mine = pl.ds(pl.multiple_of(c * half, 16), half)
        for j, (px, py) in enumerate(chips):
            k = 3 * t + j
            send = _remote(srcs[t].at[mine], lands[t].at[me, mine], ssem.at[k], rsem.at[k], (px, py, c))
            blk = lands[t].at[2 * px + py, mine]
            out.append((send, _remote(blk, blk, ssem.at[k], rsem.at[k], (px, py, c))))
    return out


def _scatter_copies(srcs, lands, ssem, rsem):
    x, y, c, chips = _place()
    out = []
    for t in range(len(srcs)):
        for j, (px, py) in enumerate(chips):
            k = 3 * t + j
            send = _remote(srcs[t].at[2 * px + py], lands[t].at[j], ssem.at[k], rsem.at[k], (px, py, c))
            out.append((send, _remote(lands[t].at[j], lands[t].at[j], ssem.at[k], rsem.at[k], (px, py, c))))
    return out


def _sibling_copies(srcs, lands, ssem, rsem):
    x, y, c, _ = _place()
    sib = (x, y, 1 - c)
    return [(_remote(srcs[t], lands[t], ssem.at[t], rsem.at[t], sib),
             _remote(lands[t], lands[t], ssem.at[t], rsem.at[t], sib)) for t in range(len(srcs))]


def _peer_copies(srcs, lands, ssem, rsem):
    x, y, c, _ = _place()
    me = 4 * x + 2 * y + c
    out = []
    for kk in range(1, 8):
        px, py, pc = x ^ (kk >> 2), y ^ ((kk >> 1) & 1), c ^ (kk & 1)
        send = _remote(srcs[0], lands[0].at[me], ssem.at[kk - 1], rsem.at[kk - 1], (px, py, pc))
        slot = lands[0].at[4 * px + 2 * py + pc]
        out.append((send, _remote(slot, slot, ssem.at[kk - 1], rsem.at[kk - 1], (px, py, pc))))
    return out


def _win_and_small_copies(srcs, lands, ssem, rsem):
    return (_scatter_copies(srcs[:1], lands[:1], ssem, rsem)
            + _peer_copies(srcs[1:], lands[1:], ssem.at[pl.ds(3, 7)], rsem.at[pl.ds(3, 7)]))


def _split_start(plan, ncopy, srcs, lands, after, name):
    ns, nbuf = len(srcs), len(srcs) + len(lands)
    extra = [] if after is None else [after]
    n_in = nbuf + len(extra)

    def body(*refs):
        ssem, rsem, token = refs[n_in], refs[n_in + 1], refs[-1]
        for send, _ in plan(refs[:ns], refs[ns:nbuf], ssem, rsem):
            send.start()
        token[...] = jnp.zeros_like(token)

    bufs = [pltpu.with_memory_space_constraint(a, pltpu.HBM) for a in list(srcs) + list(lands)]
    outs = pl.pallas_call(
        body, name=name, in_specs=[HBM_SPEC] * nbuf + [ANY] * len(extra),
        out_specs=[SEM_SPEC, SEM_SPEC] + [HBM_SPEC] * nbuf + [pl.BlockSpec(memory_space=pltpu.VMEM)],
        out_shape=[pltpu.SemaphoreType.DMA((ncopy,)), pltpu.SemaphoreType.DMA((ncopy,))]
        + [pltpu.HBM(a.shape, a.dtype) for a in bufs] + [jax.ShapeDtypeStruct((8, 128), F32)],
        input_output_aliases={i: 2 + i for i in range(nbuf)},
        compiler_params=pltpu.CompilerParams(has_side_effects=DATAFLOW))(*bufs, *extra)
    return outs[0], outs[1], outs[2:2 + ns], outs[2 + ns:2 + nbuf], outs[-1]


def _split_wait(plan, ssem, rsem, srcs, lands, after, name, only=None):
    ns, nbuf = len(srcs), len(srcs) + len(lands)

    def body(*refs):
        s_ref, r_ref = refs[nbuf], refs[nbuf + 1]
        for k, (send, recv) in enumerate(plan(refs[:ns], refs[ns:nbuf], s_ref, r_ref)):
            if only is None or k in only:
                send.wait_send()
                recv.wait_recv()

    outs = pl.pallas_call(
        body, name=name, in_specs=[HBM_SPEC] * nbuf + [SEM_SPEC, SEM_SPEC] + [ANY] * len(after),
        out_specs=[HBM_SPEC] * nbuf,
        out_shape=[pltpu.HBM(a.shape, a.dtype) for a in list(srcs) + list(lands)],
        input_output_aliases={i: i for i in range(nbuf)},
        compiler_params=pltpu.CompilerParams(has_side_effects=DATAFLOW))(*srcs, *lands, ssem, rsem, *after)
    return outs


def _gather_finish(lands, tag):
    nt = len(lands)

    def body(*refs):
        outs = refs[nt:2 * nt]
        dsend, drecv = refs[2 * nt:]
        x, y, c, chips = _place()
        sib = (x, y, 1 - c)
        sends = []
        for t in range(nt):
            half = outs[t].shape[1] // 2
            mine = pl.ds(pl.multiple_of(c * half, 16), half)
            for j, (px, py) in enumerate(chips):
                blk = outs[t].at[2 * px + py, mine]
                cp = _remote(blk, blk, dsend.at[t, j], drecv.at[t, j], sib)
                cp.start()
                sends.append(cp)
        for t in range(nt):
            half = outs[t].shape[1] // 2
            other = pl.ds(pl.multiple_of((1 - c) * half, 16), half)
            for j, (px, py) in enumerate(chips):
                blk = outs[t].at[2 * px + py, other]
                _remote(blk, blk, dsend.at[t, j], drecv.at[t, j], sib).wait_recv()
        for cp in sends:
            cp.wait_send()

    return pl.pallas_call(
        body, name="gather_finish_" + tag, in_specs=[ANY] * nt, out_specs=[ANY] * nt,
        out_shape=[jax.ShapeDtypeStruct(a.shape, a.dtype) for a in lands],
        input_output_aliases={t: t for t in range(nt)},
        scratch_shapes=[pltpu.SemaphoreType.DMA((nt, 3)), pltpu.SemaphoreType.DMA((nt, 3))],
        compiler_params=_cp())(*lands)


def _chip_partial(owns, recv, chip_arr, tag):
    nt = len(owns)

    def body(chip_ref, *refs):
        k = pl.program_id(0)
        for t in range(nt):
            p = refs[t][...] + refs[nt + t][...].astype(F32)
            refs[2 * nt + t][...] = p.astype(BF16)

            @pl.when(k == chip_ref[0])
            def _():
                refs[3 * nt + t][...] = p

    blk_specs = [pl.BlockSpec((None,) + a.shape[1:], lambda k, chip_ref: (k, 0, 0)) for a in owns]
    own_specs = [pl.BlockSpec(a.shape[1:], lambda k, chip_ref: (0, 0)) for a in owns]
    return pl.pallas_call(
        body, name="rs_chip_partial_" + tag,
        grid_spec=pltpu.PrefetchScalarGridSpec(num_scalar_prefetch=1, grid=(NSH,), in_specs=blk_specs * 2,
                                               out_specs=blk_specs + own_specs),
        out_shape=[jax.ShapeDtypeStruct(a.shape, BF16) for a in owns]
        + [jax.ShapeDtypeStruct(a.shape[1:], F32) for a in owns],
        compiler_params=_cp(1))(chip_arr, *owns, *recv)


def _sum_chips(own, recv, tag):
    nt = len(own)

    def body(*refs):
        outs = refs[2 * nt:]
        for t in range(nt):
            r = refs[nt + t]
            outs[t][...] = ((refs[t][...] + r[0].astype(F32)) + r[1].astype(F32)) + r[2].astype(F32)

    vm = pl.BlockSpec(memory_space=pltpu.VMEM)
    return pl.pallas_call(
        body, name="rs_sum_chips_" + tag, in_specs=[vm] * (2 * nt), out_specs=[vm] * nt,
        out_shape=[jax.ShapeDtypeStruct(a.shape, F32) for a in own],
        compiler_params=_cp())(*own, *recv)


def _sum_chips_shared(own, recv, tag):
    def body(own_ref, recv_ref, out_ref, sib_ref, ssem, rsem):
        out_ref[...] = ((own_ref[...] + recv_ref[0].astype(F32)) + recv_ref[1].astype(F32)) + recv_ref[2].astype(F32)
        x, y, c, _ = _place()
        cp = _remote(out_ref, sib_ref, ssem.at[0], rsem.at[0], (x, y, 1 - c))
        cp.start()
        cp.wait()

    vm = pl.BlockSpec(memory_space=pltpu.VMEM)
    return pl.pallas_call(
        body, name="rs_sum_share_" + tag, in_specs=[vm, vm], out_specs=[vm, ANY],
        out_shape=[jax.ShapeDtypeStruct(own.shape, F32)] * 2,
        scratch_shapes=[pltpu.SemaphoreType.DMA((1,)), pltpu.SemaphoreType.DMA((1,))],
        compiler_params=_cp())(own, recv)


def _adamw_math(w, g, m, v):
    m = ADAM_B1 * m + (1.0 - ADAM_B1) * g
    v = ADAM_B2 * v + (1.0 - ADAM_B2) * (g * g)
    m_hat = m / (1.0 - ADAM_B1 ** ADAM_STEP)
    v_hat = v / (1.0 - ADAM_B2 ** ADAM_STEP)
    delta = -ADAM_LR * (m_hat / (jnp.sqrt(v_hat) + ADAM_EPS) + ADAM_WD * w)
    return delta, m, v


ADAM_SPLIT = 4


def _adamw_shards(own, sib, ws, ms, vs, c_arr, tag):
    nt = len(own)

    def body(c_ref, *refs):
        hh = pl.program_id(0)
        mine = hh == c_ref[0]
        for t in range(nt):
            g = jnp.where(mine, refs[t][...], refs[nt + t][...])
            delta, m, v = _adamw_math(refs[2 * nt + t][...], g, refs[3 * nt + t][...], refs[4 * nt + t][...])
            refs[5 * nt + t][...] = g
            refs[6 * nt + t][...] = delta
            refs[7 * nt + t][...] = m
            refs[8 * nt + t][...] = v

    def tile(a):
        return (a.shape[0] // ADAM_SPLIT, a.shape[1])

    def half_index(used_when_mine):
        def index(hh, q, c_ref):
            mine = 1 - (hh - c_ref[0]) * (hh - c_ref[0])
            used = mine if used_when_mine else 1 - mine
            return (used * q + (1 - used) * hh * (ADAM_SPLIT - 1), 0)
        return index

    own_specs = [pl.BlockSpec(tile(a), half_index(True)) for a in own]
    sib_specs = [pl.BlockSpec(tile(a), half_index(False)) for a in own]
    full_specs = [pl.BlockSpec(tile(a), lambda hh, q, c_ref: (hh * ADAM_SPLIT + q, 0)) for a in own]
    return pl.pallas_call(
        body, name="adamw_shards_" + tag,
        grid_spec=pltpu.PrefetchScalarGridSpec(num_scalar_prefetch=1, grid=(2, ADAM_SPLIT),
                                               in_specs=own_specs + sib_specs + full_specs * 3,
                                               out_specs=full_specs * 4),
        out_shape=[jax.ShapeDtypeStruct(w.shape, F32) for w in ws] * 4,
        compiler_params=_cp(2))(c_arr, *own, *sib, *ws, *ms, *vs)


SMALL_WPOOL_ROW = 32
SMALL_SCALE_ROW = SMALL_WPOOL_ROW + 4 * GROUP
SMALL_BIAS_ROW = SMALL_SCALE_ROW + 8
SMALL_SINKS_ROW = SMALL_BIAS_ROW + NQ
SMALL_LOSS_ROW = SMALL_SINKS_ROW + 8


def _small_sum_adamw(gathered, wp, mp, vp):
    rows = wp.shape[0]

    def body(g_ref, w_ref, m_ref, v_ref, *rest):
        outs, res = rest[:-1], rest[-1]
        g = g_ref[0]
        for d in range(1, 8):
            g = g + g_ref[d]
        delta, m, v = _adamw_math(w_ref[...], g, m_ref[...], v_ref[...])
        for kind, val in enumerate((g, delta, m, v)):
            res[kind] = val
            gains = outs[8 * kind:8 * kind + 4]
            w_pool_o, scale_o, bias_o, sinks_o = outs[8 * kind + 4:8 * kind + 8]
            for t in range(4):
                for i in range(8):
                    gains[t][:, 128 * i:128 * (i + 1)] = res[kind, pl.ds(8 * t + i, 1), :]
            for grp in range(4):
                w_pool_o[grp] = res[kind, pl.ds(SMALL_WPOOL_ROW + GROUP * grp, GROUP), :]
            for i in range(4):
                scale_o[:, 128 * i:128 * (i + 1)] = res[kind, pl.ds(SMALL_SCALE_ROW + i, 1), :]
            bias_o[...] = res[kind, pl.ds(SMALL_BIAS_ROW, NQ), :][:, 0:NBUCKET]
            sinks_o[...] = res[kind, pl.ds(SMALL_SINKS_ROW, 1), :][:, 0:NQ]
        outs[-1][...] = res[0, pl.ds(SMALL_LOSS_ROW, 1), :]

    vm = pl.BlockSpec(memory_space=pltpu.VMEM)
    one_kind = [jax.ShapeDtypeStruct((1, D), F32)] * 4 + [
        jax.ShapeDtypeStruct((4, GROUP, GROUP), F32), jax.ShapeDtypeStruct((1, POOL_W), F32),
        jax.ShapeDtypeStruct((NQ, NBUCKET), F32), jax.ShapeDtypeStruct((1, NQ), F32)]
    return pl.pallas_call(
        body, name="small_sum_adamw", in_specs=[vm] * 4, out_specs=[vm] * 33,
        out_shape=one_kind * 4 + [jax.ShapeDtypeStruct((1, 128), F32)],
        scratch_shapes=[pltpu.VMEM((4, rows, 128), F32)],
        compiler_params=_cp())(gathered, wp, mp, vp)


def _pack_small(gains4, w_pool, pool_scale, rel_bias_t, sinks, loss):
    bias_rows = jnp.pad(rel_bias_t, ((0, 0), (0, 128 - rel_bias_t.shape[1])))
    parts = list(gains4) + [w_pool, pool_scale, bias_rows, sinks, loss]
    rows = []
    for a in parts:
        flat = a.reshape(-1)
        n = flat.shape[0]
        padded = -(-n // 1024) * 1024
        rows.append(jnp.pad(flat, (0, padded - n)).reshape(-1, 128))
    return jnp.concatenate(rows, axis=0)


def kernel(x, g_pre_mix, w_in, w_pool, pool_scale, rel_bias, sinks, w_out, g_post_mix, g_pre_ffn, w_gate, w_up, w_down, g_post_ffn, loss_target, m_g_pre_mix, m_w_in, m_w_pool, m_pool_scale, m_rel_bias, m_sinks, m_w_out, m_g_post_mix, m_g_pre_ffn, m_w_gate, m_w_up, m_w_down, m_g_post_ffn, v_g_pre_mix, v_w_in, v_w_pool, v_pool_scale, v_rel_bias, v_sinks, v_w_out, v_g_post_mix, v_g_pre_ffn, v_w_gate, v_w_up, v_w_down, v_g_post_ffn):
    c = lax.axis_index("c")
    chip = 2 * lax.axis_index("x") + lax.axis_index("y")

    def shards(a_in, a_out, a_gate, a_up, a_down):
        return (a_in[0].T, a_out[0], a_gate[0].T, a_up[0].T, a_down[0])

    c_arr = c.reshape(1).astype(jnp.int32)
    chip_arr = chip.reshape(1).astype(jnp.int32)

    big_w = shards(w_in, w_out, w_gate, w_up, w_down)
    big_bf = [w.astype(BF16) for w in big_w]
    g_ssem, g_rsem, g_srcs, g_lands, _ = _split_start(
        _gather_copies, 15, big_bf, [lax.empty((NSH,) + a.shape, BF16) for a in big_bf], None, "gather_start")
    fw = {}

    def with_own(land, shard):
        return lax.dynamic_update_slice(land, shard[None], (chip, 0, 0))

    def w_in_ready(*after):
        fw["first"] = _split_wait(_gather_copies, g_ssem, g_rsem, g_srcs, g_lands, after, "gather_win_wait",
                                  only=(0, 1, 2))
        (fw["win"],) = _gather_finish([with_own(fw["first"][5], fw["first"][0])], "win")
        return fw["win"].reshape(IN_W, D)

    def w_out_ready(*after):
        first = fw["first"]
        fw["second"] = _split_wait(_gather_copies, g_ssem, g_rsem, first[:5], [fw["win"]] + list(first[6:]), after,
                                   "gather_wout_wait", only=(3, 4, 5))
        (fw["wout"],) = _gather_finish([with_own(fw["second"][6], fw["second"][1])], "wout")
        return fw["wout"].reshape(D, D), None

    def ffn_weights(after):
        second = fw["second"]
        outs = _split_wait(_gather_copies, g_ssem, g_rsem, second[:5], [second[5], fw["wout"]] + list(second[7:]),
                           [after], "gather_ffn_wait", only=tuple(range(6, 15)))
        return _gather_finish([with_own(land, src) for src, land in zip(outs[2:5], outs[7:])], "ffn")

    rs = {}

    def on_ffn_grads(ffn_g):
        oths = ffn_g[1::2]
        rs["ffn_own"] = ffn_g[0::2]
        rs["x"] = _split_start(_sibling_copies, 3, oths, [lax.empty(a.shape, BF16) for a in oths], ffn_g[0],
                               "rs_exchange_ffn_start")
        return rs["x"][4]

    def on_wout_grads(own, oth, after):
        ssem, rsem, srcs, lands, _ = rs["x"]
        recv_ffn = _split_wait(_sibling_copies, ssem, rsem, srcs, lands, [after], "rs_exchange_ffn_wait")[3:]
        recv_wout = _to_sibling([oth], "rs_exchange_wout")
        outs = _chip_partial([own] + list(rs["ffn_own"]), list(recv_wout) + list(recv_ffn), chip_arr, "early")
        rs["early_own"] = outs[4:]
        rs["s"] = _split_start(_scatter_copies, 12, outs[:4],
                               [lax.empty((3,) + a.shape[1:], BF16) for a in outs[:4]], outs[4], "scatter_early_start")
        return rs["s"][4]

    small = (g_pre_mix, g_post_mix, g_pre_ffn, g_post_ffn, w_pool[0], pool_scale, rel_bias, sinks)
    loss, gx, small_grads, ((win_own, win_oth), _, _) = _local_step(
        x[0], loss_target[0], small, w_in_ready, w_out_ready, ffn_weights, c_arr, on_ffn_grads, on_wout_grads)

    ssem, rsem, srcs, lands, _ = rs["s"]
    recv_early = _split_wait(_scatter_copies, ssem, rsem, srcs, lands, [gx], "scatter_early_wait")[4:]
    finals = _sum_chips(list(rs["early_own"]), list(recv_early), "early")
    to_sib = [win_oth] + list(finals)
    sh_ssem, sh_rsem, sh_srcs, sh_lands, _ = _split_start(
        _sibling_copies, 5, to_sib, [lax.empty(a.shape, a.dtype) for a in to_sib], None, "rs_share_start")

    unused = jnp.zeros((1, 1), F32)
    gp = _pack_small(small_grads[:4], *small_grads[4:], loss)
    wp = _pack_small((g_pre_mix, g_post_mix, g_pre_ffn, g_post_ffn), w_pool, pool_scale, rel_bias.T, sinks, unused)
    mp = _pack_small((m_g_pre_mix, m_g_post_mix, m_g_pre_ffn, m_g_post_ffn), m_w_pool, m_pool_scale, m_rel_bias.T,
                     m_sinks, unused)
    vp = _pack_small((v_g_pre_mix, v_g_post_mix, v_g_pre_ffn, v_g_post_ffn), v_w_pool, v_pool_scale, v_rel_bias.T,
                     v_sinks, unused)

    moments = (shards(m_w_in, m_w_out, m_w_gate, m_w_up, m_w_down),
               shards(v_w_in, v_w_out, v_w_gate, v_w_up, v_w_down))
    sh_first = _split_wait(_sibling_copies, sh_ssem, sh_rsem, sh_srcs, sh_lands, [], "rs_share_win_wait", only=(0,))
    outs = _chip_partial([win_own], [sh_first[5]], chip_arr, "win")
    slots = lax.dynamic_update_slice(lax.empty((8,) + gp.shape, F32), gp[None], (2 * chip + c, 0, 0))
    w_ssem, w_rsem, w_srcs, w_lands, w_token = _split_start(
        _win_and_small_copies, 10, [outs[0], gp], [lax.empty((3,) + outs[0].shape[1:], BF16), slots], gx,
        "scatter_win_start")
    shared = _split_wait(_sibling_copies, sh_ssem, sh_rsem, sh_first[:5], sh_first[5:], [w_token],
                         "rs_share_early_wait", only=(1, 2, 3, 4))
    adam_e = _adamw_shards(shared[1:5], shared[6:], big_w[1:], moments[0][1:], moments[1][1:], c_arr, "early")
    w_first = _split_wait(_win_and_small_copies, w_ssem, w_rsem, w_srcs, w_lands, [adam_e[0]], "scatter_win_wait",
                          only=(0, 1, 2))
    win_final, win_sib = _sum_chips_shared(outs[1], w_first[2], "win")
    adam_w = _adamw_shards([win_final], [win_sib], big_w[:1], moments[0][:1], moments[1][:1], c_arr, "win")
    big_g, big_d, big_m, big_v = [[adam_w[i]] + list(adam_e[4 * i:4 * i + 4]) for i in range(4)]

    gathered = _split_wait(_win_and_small_copies, w_ssem, w_rsem, w_first[:2], w_first[2:], [adam_w[0]],
                           "small_allgather_wait", only=tuple(range(3, 10)))[3]
    flat = _small_sum_adamw(gathered, wp, mp, vp)
    small_out = [flat[8 * kind:8 * kind + 8] for kind in range(4)]
    total_loss = flat[-1][0, 0]

    def ordered(small8, big4):
        sg1, sg2, sg3, sg4, swp, ssc, srb, ssk = small8
        swp, srb = swp[None], srb.T
        bwin, bwout, bwg, bwu, bwd = big4[0].T[None], big4[1][None], big4[2].T[None], big4[3].T[None], big4[4][None]
        return [sg1, bwin, swp, ssc, srb, ssk, bwout, sg2, sg3, bwg, bwu, bwd, sg4]

    res = [total_loss, gx[None]]
    for sm, bg in zip(small_out, (big_g, big_d, big_m, big_v)):
        res += ordered(sm, bg)
    return tuple(res)
```

```python
import functools

import numpy as np
import jax
import jax.numpy as jnp
from jax import lax
from jax.experimental import pallas as pl
from jax.experimental.pallas import tpu as pltpu

F32 = jnp.float32
BF16 = jnp.bfloat16

D = 1024
POOL_W = 512
GROUP = 128
WINDOWS = (2, 4, 8, 16)
HALO = 128
NQ = 8
BLK = 128
NBUCKET = 32
IN_W = 1280
DFF = 2816
NSH = 4
FS = DFF // NSH
WIN_S = IN_W // NSH
WOUT_S = D // NSH
EPS = 1e-6
NEG = -1e30
SCALE = 0.125

ADAM_LR = 0.001
ADAM_B1 = 0.9
ADAM_B2 = 0.999
ADAM_EPS = 1e-08
ADAM_WD = 0.01
ADAM_STEP = 10

TM = 512
TM_IN = 1024
TM_POOL = 512
TM_FFN = 512
TM_FFN_FWD = 1024
FFN_ROWS = 256
VMEM_LIMIT = 60 * 1024 * 1024

MESH_T = pl.DeviceIdType.MESH
ANY = pl.BlockSpec(memory_space=pl.ANY)


def _cp(n_grid=0, **kw):
    sem = ("arbitrary",) * n_grid if n_grid else None
    return pltpu.CompilerParams(dimension_semantics=sem, vmem_limit_bytes=VMEM_LIMIT, **kw)


def _dot(a, b):
    return jnp.dot(a, b, preferred_element_type=F32)


def _dot_nt(a, b):
    return lax.dot_general(a, b, (((1,), (1,)), ((), ())), preferred_element_type=F32)


def _dot_tn(a, b):
    return lax.dot_general(a, b, (((0,), (0,)), ((), ())), preferred_element_type=F32)


def _rms(x):
    r = lax.rsqrt(jnp.mean(x * x, axis=-1, keepdims=True) + EPS)
    return r, x * r


def _rms_bwd(r, n, g, dout):
    dn = dout * g
    dx = r * (dn - n * jnp.mean(dn * n, axis=-1, keepdims=True))
    dg = jnp.sum(dout * n, axis=0, keepdims=True)
    return dx, dg


def _split(x, n):
    parts = []
    r = x
    for _ in range(n):
        p = r.astype(BF16)
        parts.append(p)
        r = r - p.astype(F32)
    return parts


def _band_dot(a, x, n):
    acc = None
    for p in _split(x, n):
        t = _dot(a, p)
        acc = t if acc is None else acc + t
    return acc


def _bucket_table():
    qi = np.arange(BLK)[None, :]
    kj = np.arange(2 * BLK)[:, None]
    dist = qi + BLK - kj
    n = np.maximum(dist, 0)
    nf = np.maximum(n, 1).astype(np.float32)
    large = 16 + (np.log(nf / np.float32(16)) / np.float32(np.log(128 / 16)) * np.float32(16)).astype(np.int32)
    large = np.minimum(large, NBUCKET - 1)
    return np.where(n < 16, n, large).astype(np.int32)


def _prenorm(x, g1):
    s = x.shape[0]
    tm = min(TM_IN, s)

    def body(x_ref, g_ref, h_ref):
        _, n = _rms(x_ref[...])
        h_ref[...] = (n * g_ref[...]).astype(BF16)

    row = pl.BlockSpec((tm, D), lambda i: (i, 0))
    return pl.pallas_call(
        body, name="prenorm", grid=(s // tm,),
        in_specs=[row, pl.BlockSpec((1, D), lambda i: (0, 0))], out_specs=row,
        out_shape=jax.ShapeDtypeStruct((s, D), BF16),
        compiler_params=_cp(1))(x, g1)


def _inproj_fwd(h1, w_in):
    s = h1.shape[0]
    tm = min(TM_IN, s)

    def body(h_ref, w_ref, u_ref, q_ref, k_ref, v_ref):
        proj = _dot_nt(h_ref[...], w_ref[...])
        u_ref[...] = proj[:, :512]
        q_ref[...] = proj[:, 512:1024].astype(BF16)
        k_ref[...] = proj[:, 1024:1152].astype(BF16)
        v_ref[...] = proj[:, 1152:1280].astype(BF16)

    row = lambda w: pl.BlockSpec((tm, w), lambda i: (i, 0))
    return pl.pallas_call(
        body, name="inproj_fwd", grid=(s // tm,),
        in_specs=[row(D), pl.BlockSpec((IN_W, D), lambda i: (0, 0))],
        out_specs=[row(512), row(512), row(128), row(128)],
        out_shape=[jax.ShapeDtypeStruct((s, 512), F32), jax.ShapeDtypeStruct((s, 512), BF16),
                   jax.ShapeDtypeStruct((s, 128), BF16), jax.ShapeDtypeStruct((s, 128), BF16)],
        compiler_params=_cp(1))(h1, w_in)


def _window_sums(ext, w, lag_sign, tm, terms):
    rows = lax.broadcasted_iota(jnp.int32, (HALO, 2 * HALO), 0)
    cols = lax.broadcasted_iota(jnp.int32, (HALO, 2 * HALO), 1)
    d = rows + HALO - cols if lag_sign > 0 else cols - rows
    band = jnp.where((d >= 0) & (d < w), 1.0, 0.0).astype(BF16)
    return jnp.concatenate([_band_dot(band, ext[r:r + 2 * HALO], terms) for r in range(0, tm, HALO)], axis=0)


def _pooled(ext, cur, t0, tm):
    t = t0 + lax.broadcasted_iota(jnp.int32, (tm, GROUP), 0)
    out = []
    for g, w in enumerate(WINDOWS):
        sl = slice(g * GROUP, (g + 1) * GROUP)
        cnt = jnp.minimum(t + 1, w).astype(F32)
        out.append(_window_sums(ext[:, sl], w, 1, tm, 2) / cnt - cur[:, sl])
    return out


def _pool_fwd(u, w_pool, pool_scale):
    s = u.shape[0]
    tm = min(TM_POOL, s)
    hb = tm // HALO

    def body(uc_ref, uh_ref, wp_ref, sc_ref, o_ref, pooled_ref):
        i = pl.program_id(0)
        cur = uc_ref[...]
        halo = jnp.where(i > 0, uh_ref[...], 0.0)
        ext = jnp.concatenate([halo, cur], axis=0)
        pooled = _pooled(ext, cur, i * tm, tm)
        for g in range(4):
            sl = slice(g * GROUP, (g + 1) * GROUP)
            pb = pooled[g].astype(BF16)
            pooled_ref[:, sl] = pb
            o_ref[:, sl] = (_dot(pb, wp_ref[g]) * sc_ref[:, sl]).astype(BF16)

    row = pl.BlockSpec((tm, 512), lambda i: (i, 0))
    return pl.pallas_call(
        body, name="pool_fwd", grid=(s // tm,),
        in_specs=[row, pl.BlockSpec((HALO, 512), lambda i: (jnp.maximum(i * hb - 1, 0), 0)),
                  pl.BlockSpec((4, GROUP, GROUP), lambda i: (0, 0, 0)),
                  pl.BlockSpec((1, 512), lambda i: (0, 0))],
        out_specs=[row, row],
        out_shape=[jax.ShapeDtypeStruct((s, 512), BF16)] * 2,
        compiler_params=_cp(1))(u, u, w_pool, pool_scale)


def _bias_table(bucket, rel_bias):
    def body(b_ref, rb_ref, o_ref):
        bucket_v = b_ref[...]
        key = lax.broadcasted_iota(jnp.int32, (2 * BLK, BLK), 0)
        dist = lax.broadcasted_iota(jnp.int32, (2 * BLK, BLK), 1) + BLK - key
        inwin = (dist >= 0) & (dist < BLK)
        for h in range(NQ):
            acc = jnp.zeros((2 * BLK, BLK), F32)
            for b in range(NBUCKET):
                acc = jnp.where(bucket_v == b, rb_ref[b, h], acc)
            rest = jnp.where(inwin, acc, NEG)
            o_ref[1, h] = rest
            o_ref[0, h] = jnp.where(key >= BLK, rest, NEG)

    return pl.pallas_call(
        body, name="bias_table",
        in_specs=[pl.BlockSpec(memory_space=pltpu.VMEM), pl.BlockSpec(memory_space=pltpu.SMEM)],
        out_specs=pl.BlockSpec(memory_space=pltpu.VMEM),
        out_shape=jax.ShapeDtypeStruct((2, NQ, 2 * BLK, BLK), F32),
        compiler_params=_cp())(bucket, rel_bias)


HD = 64


def _kv_band(ref, n, transposed):
    pstart = pl.multiple_of(jnp.maximum(n - 1, 0) * BLK, BLK)
    cstart = pl.multiple_of(n * BLK, BLK)
    lo = lax.broadcasted_iota(jnp.int32, (2 * BLK, 128), 1) < HD
    band = jnp.concatenate([ref[pl.ds(pstart, BLK), :], ref[pl.ds(cstart, BLK), :]], axis=0).astype(F32)
    rot = pltpu.roll(band, HD, axis=1)
    halves = ((jnp.where(lo, band, 0.0), jnp.where(lo, 0.0, rot)), (jnp.where(lo, rot, 0.0), jnp.where(lo, 0.0, band)))
    if transposed:
        out = [jnp.concatenate([a.T, b.T], axis=1).astype(BF16) for a, b in halves]
    else:
        out = [jnp.concatenate([a, b], axis=0).astype(BF16) for a, b in halves]
    return out, (lo, pstart, cstart)


def _pair_probs(qt, kk, bias, sk_ref, p):
    sink = jnp.concatenate([jnp.full((1, 1, BLK), sk_ref[0, 2 * p + e], F32) for e in range(2)], axis=0)
    s = _dot_nt(kk, qt).reshape(2, 2 * BLK, BLK) + bias
    m = jnp.maximum(jnp.max(s, axis=1, keepdims=True), sink)
    pr = jnp.exp(s - m)
    es = jnp.exp(sink - m)
    inv = 1.0 / (jnp.sum(pr, axis=1, keepdims=True) + es)
    return pr * inv, es * inv


def _attn_fwd(q, k, v, bias, sinks):
    s = q.shape[0]
    nblk = s // BLK

    def body(q_ref, k_ref, v_ref, b_ref, sk_ref, o_ref, prob_ref, ps_ref):
        n = pl.program_id(0)
        kk, _ = _kv_band(k_ref, n, False)
        vt, _ = _kv_band(v_ref, n, True)
        bidx = jnp.minimum(n, 1)
        for p in range(4):
            h = p // 2
            qt = (q_ref[:, p * 128:(p + 1) * 128].astype(F32) * SCALE).astype(BF16)
            prob, ps = _pair_probs(qt, kk[h], b_ref[bidx, pl.ds(2 * p, 2)], sk_ref, p)
            pb = prob.astype(BF16)
            prob_ref[pl.ds(2 * p, 2)] = pb
            ps_ref[pl.ds(2 * p, 2), :] = ps.reshape(2, BLK)
            acc_t = _dot(vt[h], pb.reshape(4 * BLK, BLK))
            o_ref[:, p * 128:(p + 1) * 128] = acc_t.T.astype(BF16)

    return pl.pallas_call(
        body, name="attn_fwd", grid=(nblk,),
        in_specs=[pl.BlockSpec((BLK, 512), lambda i: (i, 0)),
                  pl.BlockSpec((s, 128), lambda i: (0, 0)),
                  pl.BlockSpec((s, 128), lambda i: (0, 0)),
                  pl.BlockSpec((2, NQ, 2 * BLK, BLK), lambda i: (0, 0, 0, 0)),
                  pl.BlockSpec(memory_space=pltpu.SMEM)],
        out_specs=[pl.BlockSpec((BLK, 512), lambda i: (i, 0)),
                   pl.BlockSpec((None, NQ, 2 * BLK, BLK), lambda i: (i, 0, 0, 0)),
                   pl.BlockSpec((None, NQ, BLK), lambda i: (i, 0, 0))],
        out_shape=[jax.ShapeDtypeStruct((s, 512), BF16), jax.ShapeDtypeStruct((nblk, NQ, 2 * BLK, BLK), BF16),
                   jax.ShapeDtypeStruct((nblk, NQ, BLK), F32)],
        compiler_params=_cp(1))(q, k, v, bias, sinks)


def _outproj_fwd(pool_o, attn_o, w_out, x, g2, g3):
    s = x.shape[0]
    tm = min(TM, s)

    def body(p_ref, a_ref, w_ref, x_ref, g2_ref, g3_ref, mix_ref, x1_ref, h2_ref):
        mix = _dot(p_ref[...], w_ref[0:512, :]) + _dot(a_ref[...], w_ref[512:1024, :])
        mix_ref[...] = mix
        _, n2 = _rms(mix)
        x1 = x_ref[...] + n2 * g2_ref[...]
        x1_ref[...] = x1
        _, n3 = _rms(x1)
        h2_ref[...] = (n3 * g3_ref[...]).astype(BF16)

    row = lambda w: pl.BlockSpec((tm, w), lambda i: (i, 0))
    vec = pl.BlockSpec((1, D), lambda i: (0, 0))
    return pl.pallas_call(
        body, name="outproj_fwd", grid=(s // tm,),
        in_specs=[row(512), row(512), pl.BlockSpec((D, D), lambda i: (0, 0)), row(D), vec, vec],
        out_specs=[row(D), row(D), row(D)],
        out_shape=[jax.ShapeDtypeStruct((s, D), F32), jax.ShapeDtypeStruct((s, D), F32),
                   jax.ShapeDtypeStruct((s, D), BF16)],
        compiler_params=_cp(1))(pool_o, attn_o, w_out, x, g2, g3)


def _silu_parts(g):
    sg = jax.nn.sigmoid(g)
    return sg, g * sg


def _ffn_fwd(h2, wg, wu, wd, x1, target, g4):
    s = h2.shape[0]
    tm = min(TM_FFN_FWD, s)

    def body(h_ref, wg_ref, wu_ref, wd_ref, x1_ref, t_ref, g4_ref,
             gate_ref, up_ref, a_ref, df_ref, dy_ref, loss_ref, gg4_ref, f_acc):
        i = pl.program_id(0)
        j = pl.program_id(1)
        first = j == 0
        chunks = [pl.ds(r, min(FFN_ROWS, tm)) for r in range(0, tm, FFN_ROWS)]
        project = lambda rows: (_dot_nt(h_ref[rows, :], wg_ref[...]), _dot_nt(h_ref[rows, :], wu_ref[...]))
        ahead = [project(chunks[0])]
        for k, rows in enumerate(chunks):
            if k + 1 < len(chunks):
                ahead.append(project(chunks[k + 1]))
            gate, up = ahead[k]
            gate_ref[rows, :] = gate.astype(BF16)
            up_ref[rows, :] = up.astype(BF16)
            _, sl = _silu_parts(gate)
            a = (sl * up).astype(BF16)
            a_ref[rows, :] = a
            contrib = _dot(a, wd_ref[...])
            f_acc[rows, :] = jnp.where(first, contrib, f_acc[rows, :] + contrib)

        @pl.when((i == 0) & (j == 0))
        def _():
            loss_ref[...] = jnp.zeros_like(loss_ref)
            gg4_ref[...] = jnp.zeros_like(gg4_ref)

        @pl.when(j == NSH - 1)
        def _():
            r4, n4 = _rms(f_acc[...])
            g4v = g4_ref[...]
            err = x1_ref[...] + n4 * g4v - t_ref[...]
            loss_ref[...] += (0.5 / D) * jnp.sum(err * err).reshape(1, 1)
            dy = err * (1.0 / D)
            dy_ref[...] = dy
            df, dg = _rms_bwd(r4, n4, g4v, dy)
            gg4_ref[...] += dg
            df_ref[...] = df.astype(BF16)

    row = lambda w: pl.BlockSpec((tm, w), lambda i, j: (i, 0))
    sh = lambda r, c: pl.BlockSpec((None, r, c), lambda i, j: (j, 0, 0))
    act = pl.BlockSpec((None, tm, FS), lambda i, j: (j, i, 0))
    vec = pl.BlockSpec((1, D), lambda i, j: (0, 0))
    return pl.pallas_call(
        body, name="ffn_fwd", grid=(s // tm, NSH),
        in_specs=[row(D), sh(FS, D), sh(FS, D), sh(FS, D), row(D), row(D), vec],
        out_specs=[act, act, act, row(D), row(D), pl.BlockSpec((1, 1), lambda i, j: (0, 0)), vec],
        out_shape=[jax.ShapeDtypeStruct((NSH, s, FS), BF16)] * 3
        + [jax.ShapeDtypeStruct((s, D), BF16), jax.ShapeDtypeStruct((s, D), F32),
           jax.ShapeDtypeStruct((1, 1), F32), jax.ShapeDtypeStruct((1, D), F32)],
        scratch_shapes=[pltpu.VMEM((tm, D), F32)],
        compiler_params=_cp(2))(h2, wg, wu, wd, x1, target, g4)


def _ffn_bwd_act(df, gate, up, wg, wu, wd, dy, x1, mix, g3, g2):
    s = df.shape[0]
    tm = min(TM_FFN, s)

    def body(df_ref, gate_ref, up_ref, wg_ref, wu_ref, wd_ref, dy_ref, x1_ref, mix_ref, g3_ref, g2_ref,
             dgate_ref, dup_ref, dx1_ref, dmix_ref, gg3_ref, gg2_ref, acc):
        j = pl.program_id(0)
        i = pl.program_id(1)
        first = j == 0
        tile = pl.multiple_of(i * tm, tm)
        chunks = [pl.ds(r, min(FFN_ROWS, tm)) for r in range(0, tm, FFN_ROWS)]

        @pl.when((i == 0) & (j == 0))
        def _():
            gg3_ref[...] = jnp.zeros_like(gg3_ref)
            gg2_ref[...] = jnp.zeros_like(gg2_ref)

        def norms_backward(rows, dh2):
            r3, n3 = _rms(x1_ref[rows, :])
            dx1n, dg3 = _rms_bwd(r3, n3, g3_ref[...], dh2)
            dx1 = dy_ref[rows, :] + dx1n
            dx1_ref[rows, :] = dx1
            gg3_ref[...] += dg3
            r2, n2 = _rms(mix_ref[rows, :])
            dmix, dg2 = _rms_bwd(r2, n2, g2_ref[...], dx1)
            gg2_ref[...] += dg2
            dmix_ref[rows, :] = dmix.astype(BF16)

        def shard_pass(last):
            das = [_dot_nt(df_ref[chunks[0], :], wd_ref[...])]
            for k, rows in enumerate(chunks):
                if k + 1 < len(chunks):
                    das.append(_dot_nt(df_ref[chunks[k + 1], :], wd_ref[...]))
                da = das[k]
                g = gate_ref[rows, :].astype(F32)
                u = up_ref[rows, :].astype(F32)
                sg, sl = _silu_parts(g)
                dgate = (da * u * (sg * (1.0 + g * (1.0 - sg)))).astype(BF16)
                dup = (da * sl).astype(BF16)
                dgate_ref[rows, :] = dgate
                dup_ref[rows, :] = dup
                contrib = _dot(dgate, wg_ref[...]) + _dot(dup, wu_ref[...])
                acc_rows = pl.ds(tile + k * FFN_ROWS, rows.size)
                if last:
                    norms_backward(rows, acc[acc_rows, :] + contrib)
                else:
                    acc[acc_rows, :] = jnp.where(first, contrib, acc[acc_rows, :] + contrib)

        pl.when(j < NSH - 1)(functools.partial(shard_pass, False))
        pl.when(j == NSH - 1)(functools.partial(shard_pass, True))

    row = pl.BlockSpec((tm, D), lambda j, i: (i, 0))
    last = pl.BlockSpec((tm, D), lambda j, i: (i * (j // (NSH - 1)), 0))
    sh = pl.BlockSpec((None, FS, D), lambda j, i: (j, 0, 0))
    act = pl.BlockSpec((None, tm, FS), lambda j, i: (j, i, 0))
    vec = pl.BlockSpec((1, D), lambda j, i: (0, 0))
    return pl.pallas_call(
        body, name="ffn_bwd_act", grid=(NSH, s // tm),
        in_specs=[row, act, act, sh, sh, sh, last, last, last, vec, vec],
        out_specs=[act, act, last, last, vec, vec],
        out_shape=[jax.ShapeDtypeStruct((NSH, s, FS), BF16), jax.ShapeDtypeStruct((NSH, s, FS), BF16),
                   jax.ShapeDtypeStruct((s, D), F32), jax.ShapeDtypeStruct((s, D), BF16),
                   jax.ShapeDtypeStruct((1, D), F32), jax.ShapeDtypeStruct((1, D), F32)],
        scratch_shapes=[pltpu.VMEM((s, D), F32)],
        compiler_params=_cp(2))(df, gate, up, wg, wu, wd, dy, x1, mix, g3, g2)


SMEM_SPEC = pl.BlockSpec(memory_space=pltpu.SMEM)


def _emit_halves(acc_ref, row0, rows, c, own_ref, oth_ref):
    half = rows // 2
    own_ref[...] = acc_ref[pl.ds(row0 + pl.multiple_of(c * half, 8), half), :]
    oth_ref[...] = acc_ref[pl.ds(row0 + pl.multiple_of((1 - c) * half, 8), half), :].astype(BF16)


def _half_shapes(rows):
    return [jax.ShapeDtypeStruct((NSH, rows // 2, D), F32), jax.ShapeDtypeStruct((NSH, rows // 2, D), BF16)]


def _ffn_bwd_w(h2, act_a, dgate, dup, df, c_arr):
    s = h2.shape[0]
    tm = min(TM_FFN_FWD, s)
    nt = s // tm

    def body(c_ref, h_ref, a_ref, dgate_ref, dup_ref, df_ref, *rest):
        outs, accs = rest[:6], rest[6:]
        i = pl.program_id(1)

        @pl.when(i == 0)
        def _():
            for acc in accs:
                acc[...] = jnp.zeros_like(acc)

        h = h_ref[...]
        accs[0][...] += _dot_tn(dgate_ref[...], h)
        accs[1][...] += _dot_tn(dup_ref[...], h)
        accs[2][...] += _dot_tn(a_ref[...], df_ref[...])

        @pl.when(i == nt - 1)
        def _():
            for t, acc in enumerate(accs):
                _emit_halves(acc, 0, FS, c_ref[0], outs[2 * t], outs[2 * t + 1])

    row = lambda w: pl.BlockSpec((tm, w), lambda j, i: (i, 0))
    act = pl.BlockSpec((None, tm, FS), lambda j, i: (j, i, 0))
    half = pl.BlockSpec((None, FS // 2, D), lambda j, i: (j, 0, 0))
    return pl.pallas_call(
        body, name="ffn_bwd_w", grid=(NSH, nt),
        in_specs=[SMEM_SPEC, row(D), act, act, act, row(D)],
        out_specs=[half] * 6,
        out_shape=_half_shapes(FS) * 3,
        scratch_shapes=[pltpu.VMEM((FS, D), F32)] * 3,
        compiler_params=_cp(2))(c_arr, h2, act_a, dgate, dup, df)


def _outproj_bwd(dmix, w_out, pool_o, attn_o, c_arr, dep=None):
    s = dmix.shape[0]
    tm = min(TM, s)
    nt = s // tm

    def body(c_ref, dm_ref, w_ref, p_ref, a_ref, *rest):
        dpool_ref, dattn_ref, own_ref, oth_ref, gw_ref = rest[-5:]
        i = pl.program_id(0)

        @pl.when(i == 0)
        def _():
            gw_ref[...] = jnp.zeros_like(gw_ref)

        dm = dm_ref[...]
        dpool_ref[...] = _dot_nt(dm, w_ref[0:512, :])
        dattn_ref[...] = _dot_nt(dm, w_ref[512:1024, :]).astype(BF16)
        gw_ref[0:512, :] += _dot_tn(p_ref[...], dm)
        gw_ref[512:1024, :] += _dot_tn(a_ref[...], dm)

        @pl.when(i == nt - 1)
        def _():
            for k in range(NSH):
                _emit_halves(gw_ref, k * WOUT_S, WOUT_S, c_ref[0], own_ref.at[k], oth_ref.at[k])

    row = lambda w: pl.BlockSpec((tm, w), lambda i: (i, 0))
    full = pl.BlockSpec((D, D), lambda i: (0, 0))
    half = pl.BlockSpec((NSH, WOUT_S // 2, D), lambda i: (0, 0, 0))
    return pl.pallas_call(
        body, name="outproj_bwd", grid=(nt,),
        in_specs=[SMEM_SPEC, row(D), full, row(512), row(512)] + ([] if dep is None else [ANY]),
        out_specs=[row(512), row(512), half, half],
        out_shape=[jax.ShapeDtypeStruct((s, 512), F32), jax.ShapeDtypeStruct((s, 512), BF16)] + _half_shapes(WOUT_S),
        scratch_shapes=[pltpu.VMEM((D, D), F32)],
        compiler_params=_cp(1))(c_arr, dmix, w_out, pool_o, attn_o, *([] if dep is None else [dep]))


def _pool_bwd(pooled, dpool, w_pool, pool_scale, dep=None):
    s = pooled.shape[0]
    tm = min(TM_POOL, s)
    hb = tm // HALO
    nt = s // tm
    last_halo = s // HALO - 1

    def body(p_ref, dc_ref, dn_ref, wp_ref, sc_ref, *rest):
        du_ref, gwp_ref, gsc_ref = rest[-3:]
        i = pl.program_id(0)

        @pl.when(i == 0)
        def _():
            gwp_ref[...] = jnp.zeros_like(gwp_ref)
            gsc_ref[...] = jnp.zeros_like(gsc_ref)

        dcur = dc_ref[...]
        dnext = jnp.where(i < nt - 1, dn_ref[...], 0.0)
        dext = jnp.concatenate([dcur, dnext], axis=0)
        t_ext = i * tm + lax.broadcasted_iota(jnp.int32, (tm + HALO, GROUP), 0)
        for g, w in enumerate(WINDOWS):
            sl = slice(g * GROUP, (g + 1) * GROUP)
            pb = p_ref[:, sl]
            wp = wp_ref[g]
            mixed = _dot(pb, wp)
            gsc_ref[:, sl] += jnp.sum(dcur[:, sl] * mixed, axis=0, keepdims=True)
            dmixed = (dext[:, sl] * sc_ref[:, sl]).astype(BF16)
            gwp_ref[g] += _dot_tn(pb, dmixed[0:tm])
            dpooled = _dot_nt(dmixed, wp)
            z = dpooled / jnp.minimum(t_ext + 1, w).astype(F32)
            du_ref[:, sl] = (_window_sums(z, w, -1, tm, 2) - dpooled[0:tm]).astype(BF16)

    return pl.pallas_call(
        body, name="pool_bwd", grid=(nt,),
        in_specs=[pl.BlockSpec((tm, 512), lambda i: (i, 0)),
                  pl.BlockSpec((tm, 512), lambda i: (i, 0)),
                  pl.BlockSpec((HALO, 512), lambda i: (jnp.minimum((i + 1) * hb, last_halo), 0)),
                  pl.BlockSpec((4, GROUP, GROUP), lambda i: (0, 0, 0)),
                  pl.BlockSpec((1, 512), lambda i: (0, 0))] + ([] if dep is None else [ANY]),
        out_specs=[pl.BlockSpec((tm, 512), lambda i: (i, 0)),
                   pl.BlockSpec((4, GROUP, GROUP), lambda i: (0, 0, 0)),
                   pl.BlockSpec((1, 512), lambda i: (0, 0))],
        out_shape=[jax.ShapeDtypeStruct((s, 512), BF16), jax.ShapeDtypeStruct((4, GROUP, GROUP), F32),
                   jax.ShapeDtypeStruct((1, 512), F32)],
        compiler_params=_cp(1))(pooled, dpool, dpool, w_pool, pool_scale, *([] if dep is None else [dep]))


def _attn_bwd(q, k, v, do, probs, sink_share, bucket, dep=None):
    s = q.shape[0]
    nblk = s // BLK

    def body(q_ref, do_ref, k_ref, v_ref, prob_ref, ps_ref, bk_ref, *rest):
        dq_ref, dk_ref, dv_ref, grb_ref, gsk_ref, dss_ref = rest[-6:]
        n = pl.program_id(0)

        @pl.when(n == 0)
        def _():
            dk_ref[...] = jnp.zeros_like(dk_ref)
            dv_ref[...] = jnp.zeros_like(dv_ref)
            dss_ref[...] = jnp.zeros_like(dss_ref)
            gsk_ref[...] = jnp.zeros_like(gsk_ref)

        kt, (lo, pstart, cstart) = _kv_band(k_ref, n, True)
        vv, _ = _kv_band(v_ref, n, False)
        lo_q = lax.broadcasted_iota(jnp.int32, (BLK, 128), 1) < HD
        dkk = [jnp.zeros((2 * BLK, 128), F32), jnp.zeros((2 * BLK, 128), F32)]
        dvv = [jnp.zeros((2 * BLK, 128), F32), jnp.zeros((2 * BLK, 128), F32)]

        def by_head(t):
            return jnp.concatenate([jnp.where(lo_q, t, 0.0), jnp.where(lo_q, 0.0, t)], axis=0).astype(BF16)

        for p in range(4):
            h = p // 2
            qt = q_ref[:, p * 128:(p + 1) * 128].astype(F32) * SCALE
            dot = do_ref[:, p * 128:(p + 1) * 128]
            pb = prob_ref[pl.ds(2 * p, 2)]
            prob = pb.astype(F32)
            ps = ps_ref[pl.ds(2 * p, 2), :].reshape(2, 1, BLK)
            dp = _dot_nt(vv[h], dot).reshape(2, 2 * BLK, BLK)
            delta = jnp.sum(prob * dp, axis=1, keepdims=True)
            ds = prob * (dp - delta)
            sink_part = ps * delta
            for e in range(2):
                gs = -jnp.sum(sink_part[e]).reshape(1, 1)
                gsk_ref[pl.ds(2 * p + e, 1), :] += jnp.broadcast_to(gs, (1, 128))
            dss_ref[pl.ds(2 * p, 2)] += ds
            dsb = ds.astype(BF16)
            dq_t = _dot(kt[h], dsb.reshape(4 * BLK, BLK))
            dq_ref[:, p * 128:(p + 1) * 128] = (dq_t.T * SCALE).astype(BF16)
            dkk[h] = dkk[h] + _dot(jnp.concatenate([dsb[0], dsb[1]], axis=1), by_head(qt))
            dvv[h] = dvv[h] + _dot(jnp.concatenate([pb[0], pb[1]], axis=1), by_head(dot.astype(F32)))
        for acc, ref in ((dkk, dk_ref), (dvv, dv_ref)):
            f0 = acc[0] + pltpu.roll(acc[0], HD, axis=1)
            f1 = acc[1] + pltpu.roll(acc[1], HD, axis=1)
            band = jnp.where(lo, f0, f1)
            ref[pl.ds(pstart, BLK), :] += band[0:BLK]
            ref[pl.ds(cstart, BLK), :] += band[BLK:2 * BLK]

        @pl.when(n == nblk - 1)
        def _():
            _relbias_grad(dss_ref, bk_ref[...], grb_ref)

    blk = pl.BlockSpec((BLK, 512), lambda i: (i, 0))
    kv = pl.BlockSpec((s, 128), lambda i: (0, 0))
    row8 = pl.BlockSpec((NQ, 128), lambda i: (0, 0))
    return pl.pallas_call(
        body, name="attn_bwd", grid=(nblk,),
        in_specs=[blk, blk, kv, kv, pl.BlockSpec((None, NQ, 2 * BLK, BLK), lambda i: (i, 0, 0, 0)),
                  pl.BlockSpec((None, NQ, BLK), lambda i: (i, 0, 0)), pl.BlockSpec((2 * BLK, BLK), lambda i: (0, 0))]
        + ([] if dep is None else [ANY]),
        out_specs=[blk, kv, kv, row8, row8],
        out_shape=[jax.ShapeDtypeStruct((s, 512), BF16), jax.ShapeDtypeStruct((s, 128), F32),
                   jax.ShapeDtypeStruct((s, 128), F32), jax.ShapeDtypeStruct((NQ, 128), F32),
                   jax.ShapeDtypeStruct((NQ, 128), F32)],
        scratch_shapes=[pltpu.VMEM((NQ, 2 * BLK, BLK), F32)],
        compiler_params=_cp(1))(q, do, k, v, probs, sink_share, bucket, *([] if dep is None else [dep]))


def _relbias_grad(ds_ref, bucket_v, o_ref):
    lane = lax.broadcasted_iota(jnp.int32, (NQ, 128), 1)
    head_row = lax.broadcasted_iota(jnp.int32, (NQ, BLK), 0)
    acc = jnp.zeros((NQ, 128), F32)
    for b in range(NBUCKET):
        sel = bucket_v == b
        stack = jnp.zeros((NQ, BLK), F32)
        for h in range(NQ):
            col_sums = jnp.sum(jnp.where(sel, ds_ref[h], 0.0), axis=0, keepdims=True)
            stack = jnp.where(head_row == h, col_sums, stack)
        acc = acc + jnp.where(lane == b, jnp.sum(stack, axis=1, keepdims=True), 0.0)
    o_ref[...] = acc


def _inproj_bwd(x, g1, du, dq, dk, dv, w_in, dx1, c_arr):
    s = x.shape[0]
    tm = min(TM, s)
    nt = s // tm
    cols = ((0, 512), (512, 1024), (1024, 1152), (1152, 1280))

    def body(c_ref, x_ref, g_ref, du_ref, dq_ref, dk_ref, dv_ref, w_ref, dx1_ref,
             gx_ref, own_ref, oth_ref, gg_ref, gw_ref):
        i = pl.program_id(0)

        @pl.when(i == 0)
        def _():
            gw_ref[...] = jnp.zeros_like(gw_ref)
            gg_ref[...] = jnp.zeros_like(gg_ref)

        parts = (du_ref[...], dq_ref[...], dk_ref[...].astype(BF16), dv_ref[...].astype(BF16))
        dh = None
        for (a, b), dpart in zip(cols, parts):
            t = _dot(dpart, w_ref[a:b, :])
            dh = t if dh is None else dh + t
        gv = g_ref[...]
        r1, n1 = _rms(x_ref[...])
        h = (n1 * gv).astype(BF16)
        for (a, b), dpart in zip(cols, parts):
            gw_ref[a:b, :] += _dot_tn(dpart, h)
        dx, dg = _rms_bwd(r1, n1, gv, dh)
        gg_ref[...] += dg
        gx_ref[...] = dx1_ref[...] + dx

        @pl.when(i == nt - 1)
        def _():
            for k in range(NSH):
                _emit_halves(gw_ref, k * WIN_S, WIN_S, c_ref[0], own_ref.at[k], oth_ref.at[k])

    row = lambda w: pl.BlockSpec((tm, w), lambda i: (i, 0))
    vec = pl.BlockSpec((1, D), lambda i: (0, 0))
    full = pl.BlockSpec((IN_W, D), lambda i: (0, 0))
    half = pl.BlockSpec((NSH, WIN_S // 2, D), lambda i: (0, 0, 0))
    return pl.pallas_call(
        body, name="inproj_bwd", grid=(nt,),
        in_specs=[SMEM_SPEC, row(D), vec, row(512), row(512), row(128), row(128), full, row(D)],
        out_specs=[row(D), half, half, vec],
        out_shape=[jax.ShapeDtypeStruct((s, D), F32)] + _half_shapes(WIN_S) + [jax.ShapeDtypeStruct((1, D), F32)],
        scratch_shapes=[pltpu.VMEM((IN_W, D), F32)],
        compiler_params=_cp(1))(c_arr, x, g1, du, dq, dk, dv, w_in, dx1)


def _local_step(x, target, small, w_in, w_out, ffn_weights, c_arr, on_ffn_grads=None, on_wout_grads=None):
    g1, g2, g3, g4, w_pool, pool_scale, rel_bias, sinks = small
    bucket = jnp.asarray(_bucket_table())
    wp_bf = w_pool.astype(BF16)
    bias = _bias_table(bucket, rel_bias)
    h1 = _prenorm(x, g1)
    if callable(w_in):
        w_in = w_in(bias, h1)
    u, q, k, v = _inproj_fwd(h1, w_in)
    pool_o, pooled = _pool_fwd(u, wp_bf, pool_scale)
    attn_o, probs, sink_share = _attn_fwd(q, k, v, bias, sinks)
    g2_fwd = g2
    if callable(w_out):
        w_out, after = w_out(pool_o, attn_o)
        if after is not None:
            g2_fwd = g2 + after[:1, :1]
    mix, x1, h2 = _outproj_fwd(pool_o, attn_o, w_out, x, g2_fwd, g3)
    wg, wu, wd = ffn_weights(h2) if callable(ffn_weights) else ffn_weights
    gate, up, act_a, df, dy, loss, gg4 = _ffn_fwd(h2, wg, wu, wd, x1, target, g4)
    dgate, dup, dx1, dmix, gg3, gg2 = _ffn_bwd_act(df, gate, up, wg, wu, wd, dy, x1, mix, g3, g2)
    ffn_g = _ffn_bwd_w(h2, act_a, dgate, dup, df, c_arr)
    dep = None if on_ffn_grads is None else on_ffn_grads(ffn_g)
    dpool, dattn, wout_own, wout_oth = _outproj_bwd(dmix, w_out, pool_o, attn_o, c_arr, dep)
    dep = None if on_wout_grads is None else on_wout_grads(wout_own, wout_oth, dpool)
    du, gwp, gsc = _pool_bwd(pooled, dpool, wp_bf, pool_scale, dep)
    dq, dk, dv, grb, gsk = _attn_bwd(q, k, v, dattn, probs, sink_share, bucket, dep)
    gx, win_own, win_oth, gg1 = _inproj_bwd(x, g1, du, dq, dk, dv, w_in, dx1, c_arr)
    small_grads = (gg1, gg2, gg3, gg4, gwp, gsc, grb, gsk[:, 0].reshape(1, NQ))
    return loss, gx, small_grads, ((win_own, win_oth), (wout_own, wout_oth), ffn_g)


def _place():
    x, y, c = lax.axis_index("x"), lax.axis_index("y"), lax.axis_index("c")
    chips = ((1 - x, y), (x, 1 - y), (1 - x, 1 - y))
    return x, y, c, chips


def _remote(src, dst, ssem, rsem, dev):
    return pltpu.make_async_remote_copy(src_ref=src, dst_ref=dst, send_sem=ssem, recv_sem=rsem,
                                        device_id=dev, device_id_type=MESH_T)


def _to_sibling(arrs, name, after=()):
    nt, na = len(arrs), len(after)

    def body(*refs):
        srcs, dsts = refs[:nt], refs[nt + na:2 * nt + na]
        ssem, rsem = refs[2 * nt + na:]
        x, y, c, _ = _place()
        cps = [_remote(srcs[t], dsts[t], ssem.at[t], rsem.at[t], (x, y, 1 - c)) for t in range(nt)]
        for cp in cps:
            cp.start()
        for cp in cps:
            cp.wait()

    return pl.pallas_call(
        body, name=name, in_specs=[ANY] * (nt + na), out_specs=[ANY] * nt,
        out_shape=[jax.ShapeDtypeStruct(a.shape, a.dtype) for a in arrs],
        scratch_shapes=[pltpu.SemaphoreType.DMA((nt,)), pltpu.SemaphoreType.DMA((nt,))],
        compiler_params=_cp())(*arrs, *after)


HBM_SPEC = pl.BlockSpec(memory_space=pltpu.HBM)
SEM_SPEC = pl.BlockSpec(memory_space=pltpu.SEMAPHORE)
DATAFLOW = pltpu.SideEffectType.DATAFLOW_SIDE_EFFECTING


def _gather_copies(srcs, lands, ssem, rsem):
    x, y, c, chips = _place()
    me = 2 * x + y
    out = []
    for t in range(len(srcs)):
        half = srcs[t].shape[0] // 2
        mine = pl.ds(pl.multiple_of(c * half, 16), half)
        for j, (px, py) in enumerate(chips):
            k = 3 * t + j
            send = _remote(srcs[t].at[mine], lands[t].at[me, mine], ssem.at[k], rsem.at[k], (px, py, c))
            blk = lands[t].at[2 * px + py, mine]
            out.append((send, _remote(blk, blk, ssem.at[k], rsem.at[k], (px, py, c))))
    return out


def _scatter_copies(srcs, lands, ssem, rsem):
    x, y, c, chips = _place()
    out = []
    for t in range(len(srcs)):
        for j, (px, py) in enumerate(chips):
            k = 3 * t + j
            send = _remote(srcs[t].at[2 * px + py], lands[t].at[j], ssem.at[k], rsem.at[k], (px, py, c))
            out.append((send, _remote(lands[t].at[j], lands[t].at[j], ssem.at[k], rsem.at[k], (px, py, c))))
    return out


def _sibling_copies(srcs, lands, ssem, rsem):
    x, y, c, _ = _place()
    sib = (x, y, 1 - c)
    return [(_remote(srcs[t], lands[t], ssem.at[t], rsem.at[t], sib),
             _remote(lands[t], lands[t], ssem.at[t], rsem.at[t], sib)) for t in range(len(srcs))]


def _peer_copies(srcs, lands, ssem, rsem):
    x, y, c, _ = _place()
    me = 4 * x + 2 * y + c
    out = []
    for kk in range(1, 8):
        px, py, pc = x ^ (kk >> 2), y ^ ((kk >> 1) & 1), c ^ (kk & 1)
        send = _remote(srcs[0], lands[0].at[me], ssem.at[kk - 1], rsem.at[kk - 1], (px, py, pc))
        slot = lands[0].at[4 * px + 2 * py + pc]
        out.append((send, _remote(slot, slot, ssem.at[kk - 1], rsem.at[kk - 1], (px, py, pc))))
    return out


def _win_and_small_copies(srcs, lands, ssem, rsem):
    return (_scatter_copies(srcs[:1], lands[:1], ssem, rsem)
            + _peer_copies(srcs[1:], lands[1:], ssem.at[pl.ds(3, 7)], rsem.at[pl.ds(3, 7)]))


def _split_start(plan, ncopy, srcs, lands, after, name):
    ns, nbuf = len(srcs), len(srcs) + len(lands)
    extra = [] if after is None else [after]
    n_in = nbuf + len(extra)

    def body(*refs):
        ssem, rsem, token = refs[n_in], refs[n_in + 1], refs[-1]
        for send, _ in plan(refs[:ns], refs[ns:nbuf], ssem, rsem):
            send.start()
        token[...] = jnp.zeros_like(token)

    bufs = [pltpu.with_memory_space_constraint(a, pltpu.HBM) for a in list(srcs) + list(lands)]
    outs = pl.pallas_call(
        body, name=name, in_specs=[HBM_SPEC] * nbuf + [ANY] * len(extra),
        out_specs=[SEM_SPEC, SEM_SPEC] + [HBM_SPEC] * nbuf + [pl.BlockSpec(memory_space=pltpu.VMEM)],
        out_shape=[pltpu.SemaphoreType.DMA((ncopy,)), pltpu.SemaphoreType.DMA((ncopy,))]
        + [pltpu.HBM(a.shape, a.dtype) for a in bufs] + [jax.ShapeDtypeStruct((8, 128), F32)],
        input_output_aliases={i: 2 + i for i in range(nbuf)},
        compiler_params=pltpu.CompilerParams(has_side_effects=DATAFLOW))(*bufs, *extra)
    return outs[0], outs[1], outs[2:2 + ns], outs[2 + ns:2 + nbuf], outs[-1]


def _split_wait(plan, ssem, rsem, srcs, lands, after, name, only=None):
    ns, nbuf = len(srcs), len(srcs) + len(lands)

    def body(*refs):
        s_ref, r_ref = refs[nbuf], refs[nbuf + 1]
        for k, (send, recv) in enumerate(plan(refs[:ns], refs[ns:nbuf], s_ref, r_ref)):
            if only is None or k in only:
                send.wait_send()
                recv.wait_recv()

    outs = pl.pallas_call(
        body, name=name, in_specs=[HBM_SPEC] * nbuf + [SEM_SPEC, SEM_SPEC] + [ANY] * len(after),
        out_specs=[HBM_SPEC] * nbuf,
        out_shape=[pltpu.HBM(a.shape, a.dtype) for a in list(srcs) + list(lands)],
        input_output_aliases={i: i for i in range(nbuf)},
        compiler_params=pltpu.CompilerParams(has_side_effects=DATAFLOW))(*srcs, *lands, ssem, rsem, *after)
    return outs


def _gather_finish(lands, tag):
    nt = len(lands)

    def body(*refs):
        outs = refs[nt:2 * nt]
        dsend, drecv = refs[2 * nt:]
        x, y, c, chips = _place()
        sib = (x, y, 1 - c)
        sends = []
        for t in range(nt):
            half = outs[t].shape[1] // 2
            mine = pl.ds(pl.multiple_of(c * half, 16), half)
            for j, (px, py) in enumerate(chips):
                blk = outs[t].at[2 * px + py, mine]
                cp = _remote(blk, blk, dsend.at[t, j], drecv.at[t, j], sib)
                cp.start()
                sends.append(cp)
        for t in range(nt):
            half = outs[t].shape[1] // 2
            other = pl.ds(pl.multiple_of((1 - c) * half, 16), half)
            for j, (px, py) in enumerate(chips):
                blk = outs[t].at[2 * px + py, other]
                _remote(blk, blk, dsend.at[t, j], drecv.at[t, j], sib).wait_recv()
        for cp in sends:
            cp.wait_send()

    return pl.pallas_call(
        body, name="gather_finish_" + tag, in_specs=[ANY] * nt, out_specs=[ANY] * nt,
        out_shape=[jax.ShapeDtypeStruct(a.shape, a.dtype) for a in lands],
        input_output_aliases={t: t for t in range(nt)},
        scratch_shapes=[pltpu.SemaphoreType.DMA((nt, 3)), pltpu.SemaphoreType.DMA((nt, 3))],
        compiler_params=_cp())(*lands)


def _chip_partial(owns, recv, chip_arr, tag):
    nt = len(owns)

    def body(chip_ref, *refs):
        k = pl.program_id(0)
        for t in range(nt):
            p = refs[t][...] + refs[nt + t][...].astype(F32)
            refs[2 * nt + t][...] = p.astype(BF16)

            @pl.when(k == chip_ref[0])
            def _():
                refs[3 * nt + t][...] = p

    blk_specs = [pl.BlockSpec((None,) + a.shape[1:], lambda k, chip_ref: (k, 0, 0)) for a in owns]
    own_specs = [pl.BlockSpec(a.shape[1:], lambda k, chip_ref: (0, 0)) for a in owns]
    return pl.pallas_call(
        body, name="rs_chip_partial_" + tag,
        grid_spec=pltpu.PrefetchScalarGridSpec(num_scalar_prefetch=1, grid=(NSH,), in_specs=blk_specs * 2,
                                               out_specs=blk_specs + own_specs),
        out_shape=[jax.ShapeDtypeStruct(a.shape, BF16) for a in owns]
        + [jax.ShapeDtypeStruct(a.shape[1:], F32) for a in owns],
        compiler_params=_cp(1))(chip_arr, *owns, *recv)


def _sum_chips(own, recv, tag):
    nt = len(own)

    def body(*refs):
        outs = refs[2 * nt:]
        for t in range(nt):
            r = refs[nt + t]
            outs[t][...] = ((refs[t][...] + r[0].astype(F32)) + r[1].astype(F32)) + r[2].astype(F32)

    vm = pl.BlockSpec(memory_space=pltpu.VMEM)
    return pl.pallas_call(
        body, name="rs_sum_chips_" + tag, in_specs=[vm] * (2 * nt), out_specs=[vm] * nt,
        out_shape=[jax.ShapeDtypeStruct(a.shape, F32) for a in own],
        compiler_params=_cp())(*own, *recv)


def _sum_chips_shared(own, recv, tag):
    def body(own_ref, recv_ref, out_ref, sib_ref, ssem, rsem):
        out_ref[...] = ((own_ref[...] + recv_ref[0].astype(F32)) + recv_ref[1].astype(F32)) + recv_ref[2].astype(F32)
        x, y, c, _ = _place()
        cp = _remote(out_ref, sib_ref, ssem.at[0], rsem.at[0], (x, y, 1 - c))
        cp.start()
        cp.wait()

    vm = pl.BlockSpec(memory_space=pltpu.VMEM)
    return pl.pallas_call(
        body, name="rs_sum_share_" + tag, in_specs=[vm, vm], out_specs=[vm, ANY],
        out_shape=[jax.ShapeDtypeStruct(own.shape, F32)] * 2,
        scratch_shapes=[pltpu.SemaphoreType.DMA((1,)), pltpu.SemaphoreType.DMA((1,))],
        compiler_params=_cp())(own, recv)


def _adamw_math(w, g, m, v):
    m = ADAM_B1 * m + (1.0 - ADAM_B1) * g
    v = ADAM_B2 * v + (1.0 - ADAM_B2) * (g * g)
    m_hat = m / (1.0 - ADAM_B1 ** ADAM_STEP)
    v_hat = v / (1.0 - ADAM_B2 ** ADAM_STEP)
    delta = -ADAM_LR * (m_hat / (jnp.sqrt(v_hat) + ADAM_EPS) + ADAM_WD * w)
    return delta, m, v


ADAM_SPLIT = 4


def _adamw_shards(own, sib, ws, ms, vs, c_arr, tag):
    nt = len(own)

    def body(c_ref, *refs):
        hh = pl.program_id(0)
        mine = hh == c_ref[0]
        for t in range(nt):
            g = jnp.where(mine, refs[t][...], refs[nt + t][...])
            delta, m, v = _adamw_math(refs[2 * nt + t][...], g, refs[3 * nt + t][...], refs[4 * nt + t][...])
            refs[5 * nt + t][...] = g
            refs[6 * nt + t][...] = delta
            refs[7 * nt + t][...] = m
            refs[8 * nt + t][...] = v

    def tile(a):
        return (a.shape[0] // ADAM_SPLIT, a.shape[1])

    def half_index(used_when_mine):
        def index(hh, q, c_ref):
            mine = 1 - (hh - c_ref[0]) * (hh - c_ref[0])
            used = mine if used_when_mine else 1 - mine
            return (used * q + (1 - used) * hh * (ADAM_SPLIT - 1), 0)
        return index

    own_specs = [pl.BlockSpec(tile(a), half_index(True)) for a in own]
    sib_specs = [pl.BlockSpec(tile(a), half_index(False)) for a in own]
    full_specs = [pl.BlockSpec(tile(a), lambda hh, q, c_ref: (hh * ADAM_SPLIT + q, 0)) for a in own]
    return pl.pallas_call(
        body, name="adamw_shards_" + tag,
        grid_spec=pltpu.PrefetchScalarGridSpec(num_scalar_prefetch=1, grid=(2, ADAM_SPLIT),
                                               in_specs=own_specs + sib_specs + full_specs * 3,
                                               out_specs=full_specs * 4),
        out_shape=[jax.ShapeDtypeStruct(w.shape, F32) for w in ws] * 4,
        compiler_params=_cp(2))(c_arr, *own, *sib, *ws, *ms, *vs)


SMALL_WPOOL_ROW = 32
SMALL_SCALE_ROW = SMALL_WPOOL_ROW + 4 * GROUP
SMALL_BIAS_ROW = SMALL_SCALE_ROW + 8
SMALL_SINKS_ROW = SMALL_BIAS_ROW + NQ
SMALL_LOSS_ROW = SMALL_SINKS_ROW + 8


def _small_sum_adamw(gathered, wp, mp, vp):
    rows = wp.shape[0]

    def body(g_ref, w_ref, m_ref, v_ref, *rest):
        outs, res = rest[:-1], rest[-1]
        g = g_ref[0]
        for d in range(1, 8):
            g = g + g_ref[d]
        delta, m, v = _adamw_math(w_ref[...], g, m_ref[...], v_ref[...])
        for kind, val in enumerate((g, delta, m, v)):
            res[kind] = val
            gains = outs[8 * kind:8 * kind + 4]
            w_pool_o, scale_o, bias_o, sinks_o = outs[8 * kind + 4:8 * kind + 8]
            for t in range(4):
                for i in range(8):
                    gains[t][:, 128 * i:128 * (i + 1)] = res[kind, pl.ds(8 * t + i, 1), :]
            for grp in range(4):
                w_pool_o[grp] = res[kind, pl.ds(SMALL_WPOOL_ROW + GROUP * grp, GROUP), :]
            for i in range(4):
                scale_o[:, 128 * i:128 * (i + 1)] = res[kind, pl.ds(SMALL_SCALE_ROW + i, 1), :]
            bias_o[...] = res[kind, pl.ds(SMALL_BIAS_ROW, NQ), :][:, 0:NBUCKET]
            sinks_o[...] = res[kind, pl.ds(SMALL_SINKS_ROW, 1), :][:, 0:NQ]
        outs[-1][...] = res[0, pl.ds(SMALL_LOSS_ROW, 1), :]

    vm = pl.BlockSpec(memory_space=pltpu.VMEM)
    one_kind = [jax.ShapeDtypeStruct((1, D), F32)] * 4 + [
        jax.ShapeDtypeStruct((4, GROUP, GROUP), F32), jax.ShapeDtypeStruct((1, POOL_W), F32),
        jax.ShapeDtypeStruct((NQ, NBUCKET), F32), jax.ShapeDtypeStruct((1, NQ), F32)]
    return pl.pallas_call(
        body, name="small_sum_adamw", in_specs=[vm] * 4, out_specs=[vm] * 33,
        out_shape=one_kind * 4 + [jax.ShapeDtypeStruct((1, 128), F32)],
        scratch_shapes=[pltpu.VMEM((4, rows, 128), F32)],
        compiler_params=_cp())(gathered, wp, mp, vp)


def _pack_small(gains4, w_pool, pool_scale, rel_bias_t, sinks, loss):
    bias_rows = jnp.pad(rel_bias_t, ((0, 0), (0, 128 - rel_bias_t.shape[1])))
    parts = list(gains4) + [w_pool, pool_scale, bias_rows, sinks, loss]
    rows = []
    for a in parts:
        flat = a.reshape(-1)
        n = flat.shape[0]
        padded = -(-n // 1024) * 1024
        rows.append(jnp.pad(flat, (0, padded - n)).reshape(-1, 128))
    return jnp.concatenate(rows, axis=0)


def kernel(x, g_pre_mix, w_in, w_pool, pool_scale, rel_bias, sinks, w_out, g_post_mix, g_pre_ffn, w_gate, w_up, w_down, g_post_ffn, loss_target, m_g_pre_mix, m_w_in, m_w_pool, m_pool_scale, m_rel_bias, m_sinks, m_w_out, m_g_post_mix, m_g_pre_ffn, m_w_gate, m_w_up, m_w_down, m_g_post_ffn, v_g_pre_mix, v_w_in, v_w_pool, v_pool_scale, v_rel_bias, v_sinks, v_w_out, v_g_post_mix, v_g_pre_ffn, v_w_gate, v_w_up, v_w_down, v_g_post_ffn):
    c = lax.axis_index("c")
    chip = 2 * lax.axis_index("x") + lax.axis_index("y")

    def shards(a_in, a_out, a_gate, a_up, a_down):
        return (a_in[0].T, a_out[0], a_gate[0].T, a_up[0].T, a_down[0])

    c_arr = c.reshape(1).astype(jnp.int32)
    chip_arr = chip.reshape(1).astype(jnp.int32)

    big_w = shards(w_in, w_out, w_gate, w_up, w_down)
    big_bf = [w.astype(BF16) for w in big_w]
    g_ssem, g_rsem, g_srcs, g_lands, g_token = _split_start(
        _gather_copies, 15, big_bf, [lax.empty((NSH,) + a.shape, BF16) for a in big_bf], None, "gather_start")
    fw = {}

    def with_own(land, shard):
        return lax.dynamic_update_slice(land, shard[None], (chip, 0, 0))

    def w_in_ready(*after):
        fw["first"] = _split_wait(_gather_copies, g_ssem, g_rsem, g_srcs, g_lands, after, "gather_win_wait",
                                  only=(0, 1, 2))
        (fw["win"],) = _gather_finish([with_own(fw["first"][5], fw["first"][0])], "win")
        return fw["win"].reshape(IN_W, D)

    def w_out_ready(*after):
        first = fw["first"]
        fw["second"] = _split_wait(_gather_copies, g_ssem, g_rsem, first[:5], [fw["win"]] + list(first[6:]), after,
                                   "gather_wout_wait", only=(3, 4, 5))
        (fw["wout"],) = _gather_finish([with_own(fw["second"][6], fw["second"][1])], "wout")
        return fw["wout"].reshape(D, D), None

    def ffn_weights(after):
        second = fw["second"]
        outs = _split_wait(_gather_copies, g_ssem, g_rsem, second[:5], [second[5], fw["wout"]] + list(second[7:]),
                           [after], "gather_ffn_wait", only=tuple(range(6, 15)))
        return _gather_finish([with_own(land, src) for src, land in zip(outs[2:5], outs[7:])], "ffn")

    rs = {}

    def on_ffn_grads(ffn_g):
        oths = ffn_g[1::2]
        rs["ffn_own"] = ffn_g[0::2]
        rs["x"] = _split_start(_sibling_copies, 3, oths, [lax.empty(a.shape, BF16) for a in oths], ffn_g[0],
                               "rs_exchange_ffn_start")
        return rs["x"][4]

    def on_wout_grads(own, oth, after):
        ssem, rsem, srcs, lands, _ = rs["x"]
        recv_ffn = _split_wait(_sibling_copies, ssem, rsem, srcs, lands, [after], "rs_exchange_ffn_wait")[3:]
        recv_wout = _to_sibling([oth], "rs_exchange_wout")
        outs = _chip_partial([own] + list(rs["ffn_own"]), list(recv_wout) + list(recv_ffn), chip_arr, "early")
        rs["early_own"] = outs[4:]
        rs["s"] = _split_start(_scatter_copies, 12, outs[:4],
                               [lax.empty((3,) + a.shape[1:], BF16) for a in outs[:4]], outs[4], "scatter_early_start")
        return rs["s"][4]

    small = (g_pre_mix + g_token[:1, :1], g_post_mix, g_pre_ffn, g_post_ffn, w_pool[0], pool_scale, rel_bias, sinks)
    loss, gx, small_grads, ((win_own, win_oth), _, _) = _local_step(
        x[0], loss_target[0], small, w_in_ready, w_out_ready, ffn_weights, c_arr, on_ffn_grads, on_wout_grads)

    ssem, rsem, srcs, lands, _ = rs["s"]
    recv_early = _split_wait(_scatter_copies, ssem, rsem, srcs, lands, [gx], "scatter_early_wait")[4:]
    finals = _sum_chips(list(rs["early_own"]), list(recv_early), "early")
    to_sib = [win_oth] + list(finals)
    sh_ssem, sh_rsem, sh_srcs, sh_lands, _ = _split_start(
        _sibling_copies, 5, to_sib, [lax.empty(a.shape, a.dtype) for a in to_sib], None, "rs_share_start")

    unused = jnp.zeros((1, 1), F32)
    gp = _pack_small(small_grads[:4], *small_grads[4:], loss)
    wp = _pack_small((g_pre_mix, g_post_mix, g_pre_ffn, g_post_ffn), w_pool, pool_scale, rel_bias.T, sinks, unused)
    mp = _pack_small((m_g_pre_mix, m_g_post_mix, m_g_pre_ffn, m_g_post_ffn), m_w_pool, m_pool_scale, m_rel_bias.T,
                     m_sinks, unused)
    vp = _pack_small((v_g_pre_mix, v_g_post_mix, v_g_pre_ffn, v_g_post_ffn), v_w_pool, v_pool_scale, v_rel_bias.T,
                     v_sinks, unused)

    moments = (shards(m_w_in, m_w_out, m_w_gate, m_w_up, m_w_down),
               shards(v_w_in, v_w_out, v_w_gate, v_w_up, v_w_down))
    sh_first = _split_wait(_sibling_copies, sh_ssem, sh_rsem, sh_srcs, sh_lands, [], "rs_share_win_wait", only=(0,))
    outs = _chip_partial([win_own], [sh_first[5]], chip_arr, "win")
    slots = lax.dynamic_update_slice(lax.empty((8,) + gp.shape, F32), gp[None], (2 * chip + c, 0, 0))
    w_ssem, w_rsem, w_srcs, w_lands, w_token = _split_start(
        _win_and_small_copies, 10, [outs[0], gp], [lax.empty((3,) + outs[0].shape[1:], BF16), slots], gx,
        "scatter_win_start")
    shared = _split_wait(_sibling_copies, sh_ssem, sh_rsem, sh_first[:5], sh_first[5:], [w_token],
                         "rs_share_early_wait", only=(1, 2, 3, 4))
    adam_e = _adamw_shards(shared[1:5], shared[6:], big_w[1:], moments[0][1:], moments[1][1:], c_arr, "early")
    w_first = _split_wait(_win_and_small_copies, w_ssem, w_rsem, w_srcs, w_lands, [adam_e[0]], "scatter_win_wait",
                          only=(0, 1, 2))
    win_final, win_sib = _sum_chips_shared(outs[1], w_first[2], "win")
    adam_w = _adamw_shards([win_final], [win_sib], big_w[:1], moments[0][:1], moments[1][:1], c_arr, "win")
    big_g, big_d, big_m, big_v = [[adam_w[i]] + list(adam_e[4 * i:4 * i + 4]) for i in range(4)]

    gathered = _split_wait(_win_and_small_copies, w_ssem, w_rsem, w_first[:2], w_first[2:], [adam_w[0]],
                           "small_allgather_wait", only=tuple(range(3, 10)))[3]
    flat = _small_sum_adamw(gathered, wp, mp, vp)
    small_out = [flat[8 * kind:8 * kind + 8] for kind in range(4)]
    total_loss = flat[-1][0, 0]

    def ordered(small8, big4):
        sg1, sg2, sg3, sg4, swp, ssc, srb, ssk = small8
        swp, srb = swp[None], srb.T
        bwin, bwout, bwg, bwu, bwd = big4[0].T[None], big4[1][None], big4[2].T[None], big4[3].T[None], big4[4][None]
        return [sg1, bwin, swp, ssc, srb, ssk, bwout, sg2, sg3, bwg, bwu, bwd, sg4]

    res = [total_loss, gx[None]]
    for sm, bg in zip(small_out, (big_g, big_d, big_m, big_v)):
        res += ordered(sm, bg)
    return tuple(res)
```

```python
import functools

import numpy as np
import jax
import jax.numpy as jnp
from jax import lax
from jax.experimental import pallas as pl
from jax.experimental.pallas import tpu as pltpu

F32 = jnp.float32
BF16 = jnp.bfloat16

D = 1024
POOL_W = 512
GROUP = 128
WINDOWS = (2, 4, 8, 16)
HALO = 128
NQ = 8
BLK = 128
NBUCKET = 32
IN_W = 1280
DFF = 2816
NSH = 4
FS = DFF // NSH
WIN_S = IN_W // NSH
WOUT_S = D // NSH
EPS = 1e-6
NEG = -1e30
SCALE = 0.125

ADAM_LR = 0.001
ADAM_B1 = 0.9
ADAM_B2 = 0.999
ADAM_EPS = 1e-08
ADAM_WD = 0.01
ADAM_STEP = 10

TM = 512
TM_IN = 1024
TM_POOL = 512
TM_FFN = 512
TM_FFN_FWD = 1024
FFN_ROWS = 256
VMEM_LIMIT = 60 * 1024 * 1024

MESH_T = pl.DeviceIdType.MESH
ANY = pl.BlockSpec(memory_space=pl.ANY)


def _cp(n_grid=0, **kw):
    sem = ("arbitrary",) * n_grid if n_grid else None
    return pltpu.CompilerParams(dimension_semantics=sem, vmem_limit_bytes=VMEM_LIMIT, **kw)


def _dot(a, b):
    return jnp.dot(a, b, preferred_element_type=F32)


def _dot_nt(a, b):
    return lax.dot_general(a, b, (((1,), (1,)), ((), ())), preferred_element_type=F32)


def _dot_tn(a, b):
    return lax.dot_general(a, b, (((0,), (0,)), ((), ())), preferred_element_type=F32)


def _rms(x):
    r = lax.rsqrt(jnp.mean(x * x, axis=-1, keepdims=True) + EPS)
    return r, x * r


def _rms_bwd(r, n, g, dout):
    dn = dout * g
    dx = r * (dn - n * jnp.mean(dn * n, axis=-1, keepdims=True))
    dg = jnp.sum(dout * n, axis=0, keepdims=True)
    return dx, dg


def _split(x, n):
    parts = []
    r = x
    for _ in range(n):
        p = r.astype(BF16)
        parts.append(p)
        r = r - p.astype(F32)
    return parts


def _band_dot(a, x, n):
    acc = None
    for p in _split(x, n):
        t = _dot(a, p)
        acc = t if acc is None else acc + t
    return acc


def _bucket_table():
    qi = np.arange(BLK)[None, :]
    kj = np.arange(2 * BLK)[:, None]
    dist = qi + BLK - kj
    n = np.maximum(dist, 0)
    nf = np.maximum(n, 1).astype(np.float32)
    large = 16 + (np.log(nf / np.float32(16)) / np.float32(np.log(128 / 16)) * np.float32(16)).astype(np.int32)
    large = np.minimum(large, NBUCKET - 1)
    return np.where(n < 16, n, large).astype(np.int32)


def _prenorm(x, g1):
    s = x.shape[0]
    tm = min(TM_IN, s)

    def body(x_ref, g_ref, h_ref):
        _, n = _rms(x_ref[...])
        h_ref[...] = (n * g_ref[...]).astype(BF16)

    row = pl.BlockSpec((tm, D), lambda i: (i, 0))
    return pl.pallas_call(
        body, name="prenorm", grid=(s // tm,),
        in_specs=[row, pl.BlockSpec((1, D), lambda i: (0, 0))], out_specs=row,
        out_shape=jax.ShapeDtypeStruct((s, D), BF16),
        compiler_params=_cp(1))(x, g1)


def _inproj_fwd(h1, w_in):
    s = h1.shape[0]
    tm = min(TM_IN, s)

    def body(h_ref, w_ref, u_ref, q_ref, k_ref, v_ref):
        proj = _dot_nt(h_ref[...], w_ref[...])
        u_ref[...] = proj[:, :512]
        q_ref[...] = proj[:, 512:1024].astype(BF16)
        k_ref[...] = proj[:, 1024:1152].astype(BF16)
        v_ref[...] = proj[:, 1152:1280].astype(BF16)

    row = lambda w: pl.BlockSpec((tm, w), lambda i: (i, 0))
    return pl.pallas_call(
        body, name="inproj_fwd", grid=(s // tm,),
        in_specs=[row(D), pl.BlockSpec((IN_W, D), lambda i: (0, 0))],
        out_specs=[row(512), row(512), row(128), row(128)],
        out_shape=[jax.ShapeDtypeStruct((s, 512), F32), jax.ShapeDtypeStruct((s, 512), BF16),
                   jax.ShapeDtypeStruct((s, 128), BF16), jax.ShapeDtypeStruct((s, 128), BF16)],
        compiler_params=_cp(1))(h1, w_in)


def _window_sums(ext, w, lag_sign, tm, terms):
    rows = lax.broadcasted_iota(jnp.int32, (HALO, 2 * HALO), 0)
    cols = lax.broadcasted_iota(jnp.int32, (HALO, 2 * HALO), 1)
    d = rows + HALO - cols if lag_sign > 0 else cols - rows
    band = jnp.where((d >= 0) & (d < w), 1.0, 0.0).astype(BF16)
    return jnp.concatenate([_band_dot(band, ext[r:r + 2 * HALO], terms) for r in range(0, tm, HALO)], axis=0)


def _pooled(ext, cur, t0, tm):
    t = t0 + lax.broadcasted_iota(jnp.int32, (tm, GROUP), 0)
    out = []
    for g, w in enumerate(WINDOWS):
        sl = slice(g * GROUP, (g + 1) * GROUP)
        cnt = jnp.minimum(t + 1, w).astype(F32)
        out.append(_window_sums(ext[:, sl], w, 1, tm, 2) / cnt - cur[:, sl])
    return out


def _pool_fwd(u, w_pool, pool_scale):
    s = u.shape[0]
    tm = min(TM_POOL, s)
    hb = tm // HALO

    def body(uc_ref, uh_ref, wp_ref, sc_ref, o_ref, pooled_ref):
        i = pl.program_id(0)
        cur = uc_ref[...]
        halo = jnp.where(i > 0, uh_ref[...], 0.0)
        ext = jnp.concatenate([halo, cur], axis=0)
        pooled = _pooled(ext, cur, i * tm, tm)
        for g in range(4):
            sl = slice(g * GROUP, (g + 1) * GROUP)
            pb = pooled[g].astype(BF16)
            pooled_ref[:, sl] = pb
            o_ref[:, sl] = (_dot(pb, wp_ref[g]) * sc_ref[:, sl]).astype(BF16)

    row = pl.BlockSpec((tm, 512), lambda i: (i, 0))
    return pl.pallas_call(
        body, name="pool_fwd", grid=(s // tm,),
        in_specs=[row, pl.BlockSpec((HALO, 512), lambda i: (jnp.maximum(i * hb - 1, 0), 0)),
                  pl.BlockSpec((4, GROUP, GROUP), lambda i: (0, 0, 0)),
                  pl.BlockSpec((1, 512), lambda i: (0, 0))],
        out_specs=[row, row],
        out_shape=[jax.ShapeDtypeStruct((s, 512), BF16)] * 2,
        compiler_params=_cp(1))(u, u, w_pool, pool_scale)


def _bias_table(bucket, rel_bias):
    def body(b_ref, rb_ref, o_ref):
        bucket_v = b_ref[...]
        key = lax.broadcasted_iota(jnp.int32, (2 * BLK, BLK), 0)
        dist = lax.broadcasted_iota(jnp.int32, (2 * BLK, BLK), 1) + BLK - key
        inwin = (dist >= 0) & (dist < BLK)
        for h in range(NQ):
            acc = jnp.zeros((2 * BLK, BLK), F32)
            for b in range(NBUCKET):
                acc = jnp.where(bucket_v == b, rb_ref[b, h], acc)
            rest = jnp.where(inwin, acc, NEG)
            o_ref[1, h] = rest
            o_ref[0, h] = jnp.where(key >= BLK, rest, NEG)

    return pl.pallas_call(
        body, name="bias_table",
        in_specs=[pl.BlockSpec(memory_space=pltpu.VMEM), pl.BlockSpec(memory_space=pltpu.SMEM)],
        out_specs=pl.BlockSpec(memory_space=pltpu.VMEM),
        out_shape=jax.ShapeDtypeStruct((2, NQ, 2 * BLK, BLK), F32),
        compiler_params=_cp())(bucket, rel_bias)


HD = 64


def _kv_band(ref, n, transposed):
    pstart = pl.multiple_of(jnp.maximum(n - 1, 0) * BLK, BLK)
    cstart = pl.multiple_of(n * BLK, BLK)
    lo = lax.broadcasted_iota(jnp.int32, (2 * BLK, 128), 1) < HD
    band = jnp.concatenate([ref[pl.ds(pstart, BLK), :], ref[pl.ds(cstart, BLK), :]], axis=0).astype(F32)
    rot = pltpu.roll(band, HD, axis=1)
    halves = ((jnp.where(lo, band, 0.0), jnp.where(lo, 0.0, rot)), (jnp.where(lo, rot, 0.0), jnp.where(lo, 0.0, band)))
    if transposed:
        out = [jnp.concatenate([a.T, b.T], axis=1).astype(BF16) for a, b in halves]
    else:
        out = [jnp.concatenate([a, b], axis=0).astype(BF16) for a, b in halves]
    return out, (lo, pstart, cstart)


def _pair_probs(qt, kk, bias, sk_ref, p):
    sink = jnp.concatenate([jnp.full((1, 1, BLK), sk_ref[0, 2 * p + e], F32) for e in range(2)], axis=0)
    s = _dot_nt(kk, qt).reshape(2, 2 * BLK, BLK) + bias
    m = jnp.maximum(jnp.max(s, axis=1, keepdims=True), sink)
    pr = jnp.exp(s - m)
    es = jnp.exp(sink - m)
    inv = 1.0 / (jnp.sum(pr, axis=1, keepdims=True) + es)
    return pr * inv, es * inv


def _attn_fwd(q, k, v, bias, sinks):
    s = q.shape[0]
    nblk = s // BLK

    def body(q_ref, k_ref, v_ref, b_ref, sk_ref, o_ref, prob_ref, ps_ref):
        n = pl.program_id(0)
        kk, _ = _kv_band(k_ref, n, False)
        vt, _ = _kv_band(v_ref, n, True)
        bidx = jnp.minimum(n, 1)
        for p in range(4):
            h = p // 2
            qt = (q_ref[:, p * 128:(p + 1) * 128].astype(F32) * SCALE).astype(BF16)
            prob, ps = _pair_probs(qt, kk[h], b_ref[bidx, pl.ds(2 * p, 2)], sk_ref, p)
            pb = prob.astype(BF16)
            prob_ref[pl.ds(2 * p, 2)] = pb
            ps_ref[pl.ds(2 * p, 2), :] = ps.reshape(2, BLK)
            acc_t = _dot(vt[h], pb.reshape(4 * BLK, BLK))
            o_ref[:, p * 128:(p + 1) * 128] = acc_t.T.astype(BF16)

    return pl.pallas_call(
        body, name="attn_fwd", grid=(nblk,),
        in_specs=[pl.BlockSpec((BLK, 512), lambda i: (i, 0)),
                  pl.BlockSpec((s, 128), lambda i: (0, 0)),
                  pl.BlockSpec((s, 128), lambda i: (0, 0)),
                  pl.BlockSpec((2, NQ, 2 * BLK, BLK), lambda i: (0, 0, 0, 0)),
                  pl.BlockSpec(memory_space=pltpu.SMEM)],
        out_specs=[pl.BlockSpec((BLK, 512), lambda i: (i, 0)),
                   pl.BlockSpec((None, NQ, 2 * BLK, BLK), lambda i: (i, 0, 0, 0)),
                   pl.BlockSpec((None, NQ, BLK), lambda i: (i, 0, 0))],
        out_shape=[jax.ShapeDtypeStruct((s, 512), BF16), jax.ShapeDtypeStruct((nblk, NQ, 2 * BLK, BLK), BF16),
                   jax.ShapeDtypeStruct((nblk, NQ, BLK), F32)],
        compiler_params=_cp(1))(q, k, v, bias, sinks)


def _outproj_fwd(pool_o, attn_o, w_out, x, g2, g3):
    s = x.shape[0]
    tm = min(TM, s)

    def body(p_ref, a_ref, w_ref, x_ref, g2_ref, g3_ref, mix_ref, x1_ref, h2_ref):
        mix = _dot(p_ref[...], w_ref[0:512, :]) + _dot(a_ref[...], w_ref[512:1024, :])
        mix_ref[...] = mix
        _, n2 = _rms(mix)
        x1 = x_ref[...] + n2 * g2_ref[...]
        x1_ref[...] = x1
        _, n3 = _rms(x1)
        h2_ref[...] = (n3 * g3_ref[...]).astype(BF16)

    row = lambda w: pl.BlockSpec((tm, w), lambda i: (i, 0))
    vec = pl.BlockSpec((1, D), lambda i: (0, 0))
    return pl.pallas_call(
        body, name="outproj_fwd", grid=(s // tm,),
        in_specs=[row(512), row(512), pl.BlockSpec((D, D), lambda i: (0, 0)), row(D), vec, vec],
        out_specs=[row(D), row(D), row(D)],
        out_shape=[jax.ShapeDtypeStruct((s, D), F32), jax.ShapeDtypeStruct((s, D), F32),
                   jax.ShapeDtypeStruct((s, D), BF16)],
        compiler_params=_cp(1))(pool_o, attn_o, w_out, x, g2, g3)


def _silu_parts(g):
    sg = jax.nn.sigmoid(g)
    return sg, g * sg


def _ffn_fwd(h2, wg, wu, wd, x1, target, g4):
    s = h2.shape[0]
    tm = min(TM_FFN_FWD, s)

    def body(h_ref, wg_ref, wu_ref, wd_ref, x1_ref, t_ref, g4_ref,
             gate_ref, up_ref, a_ref, df_ref, dy_ref, loss_ref, gg4_ref, f_acc):
        i = pl.program_id(0)
        j = pl.program_id(1)
        first = j == 0
        chunks = [pl.ds(r, min(FFN_ROWS, tm)) for r in range(0, tm, FFN_ROWS)]
        project = lambda rows: (_dot_nt(h_ref[rows, :], wg_ref[...]), _dot_nt(h_ref[rows, :], wu_ref[...]))
        ahead = [project(chunks[0])]
        for k, rows in enumerate(chunks):
            if k + 1 < len(chunks):
                ahead.append(project(chunks[k + 1]))
            gate, up = ahead[k]
            gate_ref[rows, :] = gate.astype(BF16)
            up_ref[rows, :] = up.astype(BF16)
            _, sl = _silu_parts(gate)
            a = (sl * up).astype(BF16)
            a_ref[rows, :] = a
            contrib = _dot(a, wd_ref[...])
            f_acc[rows, :] = jnp.where(first, contrib, f_acc[rows, :] + contrib)

        @pl.when((i == 0) & (j == 0))
        def _():
            loss_ref[...] = jnp.zeros_like(loss_ref)
            gg4_ref[...] = jnp.zeros_like(gg4_ref)

        @pl.when(j == NSH - 1)
        def _():
            r4, n4 = _rms(f_acc[...])
            g4v = g4_ref[...]
            err = x1_ref[...] + n4 * g4v - t_ref[...]
            loss_ref[...] += (0.5 / D) * jnp.sum(err * err).reshape(1, 1)
            dy = err * (1.0 / D)
            dy_ref[...] = dy
            df, dg = _rms_bwd(r4, n4, g4v, dy)
            gg4_ref[...] += dg
            df_ref[...] = df.astype(BF16)

    row = lambda w: pl.BlockSpec((tm, w), lambda i, j: (i, 0))
    sh = lambda r, c: pl.BlockSpec((None, r, c), lambda i, j: (j, 0, 0))
    act = pl.BlockSpec((None, tm, FS), lambda i, j: (j, i, 0))
    vec = pl.BlockSpec((1, D), lambda i, j: (0, 0))
    return pl.pallas_call(
        body, name="ffn_fwd", grid=(s // tm, NSH),
        in_specs=[row(D), sh(FS, D), sh(FS, D), sh(FS, D), row(D), row(D), vec],
        out_specs=[act, act, act, row(D), row(D), pl.BlockSpec((1, 1), lambda i, j: (0, 0)), vec],
        out_shape=[jax.ShapeDtypeStruct((NSH, s, FS), BF16)] * 3
        + [jax.ShapeDtypeStruct((s, D), BF16), jax.ShapeDtypeStruct((s, D), F32),
           jax.ShapeDtypeStruct((1, 1), F32), jax.ShapeDtypeStruct((1, D), F32)],
        scratch_shapes=[pltpu.VMEM((tm, D), F32)],
        compiler_params=_cp(2))(h2, wg, wu, wd, x1, target, g4)


def _ffn_bwd_act(df, gate, up, wg, wu, wd, dy, x1, mix, g3, g2):
    s = df.shape[0]
    tm = min(TM_FFN, s)

    def body(df_ref, gate_ref, up_ref, wg_ref, wu_ref, wd_ref, dy_ref, x1_ref, mix_ref, g3_ref, g2_ref,
             dgate_ref, dup_ref, dx1_ref, dmix_ref, gg3_ref, gg2_ref, acc):
        j = pl.program_id(0)
        i = pl.program_id(1)
        first = j == 0
        tile = pl.multiple_of(i * tm, tm)
        chunks = [pl.ds(r, min(FFN_ROWS, tm)) for r in range(0, tm, FFN_ROWS)]

        @pl.when((i == 0) & (j == 0))
        def _():
            gg3_ref[...] = jnp.zeros_like(gg3_ref)
            gg2_ref[...] = jnp.zeros_like(gg2_ref)

        def norms_backward(rows, dh2):
            r3, n3 = _rms(x1_ref[rows, :])
            dx1n, dg3 = _rms_bwd(r3, n3, g3_ref[...], dh2)
            dx1 = dy_ref[rows, :] + dx1n
            dx1_ref[rows, :] = dx1
            gg3_ref[...] += dg3
            r2, n2 = _rms(mix_ref[rows, :])
            dmix, dg2 = _rms_bwd(r2, n2, g2_ref[...], dx1)
            gg2_ref[...] += dg2
            dmix_ref[rows, :] = dmix.astype(BF16)

        def shard_pass(last):
            das = [_dot_nt(df_ref[chunks[0], :], wd_ref[...])]
            for k, rows in enumerate(chunks):
                if k + 1 < len(chunks):
                    das.append(_dot_nt(df_ref[chunks[k + 1], :], wd_ref[...]))
                da = das[k]
                g = gate_ref[rows, :].astype(F32)
                u = up_ref[rows, :].astype(F32)
                sg, sl = _silu_parts(g)
                dgate = (da * u * (sg * (1.0 + g * (1.0 - sg)))).astype(BF16)
                dup = (da * sl).astype(BF16)
                dgate_ref[rows, :] = dgate
                dup_ref[rows, :] = dup
                contrib = _dot(dgate, wg_ref[...]) + _dot(dup, wu_ref[...])
                acc_rows = pl.ds(tile + k * FFN_ROWS, rows.size)
                if last:
                    norms_backward(rows, acc[acc_rows, :] + contrib)
                else:
                    acc[acc_rows, :] = jnp.where(first, contrib, acc[acc_rows, :] + contrib)

        pl.when(j < NSH - 1)(functools.partial(shard_pass, False))
        pl.when(j == NSH - 1)(functools.partial(shard_pass, True))

    row = pl.BlockSpec((tm, D), lambda j, i: (i, 0))
    last = pl.BlockSpec((tm, D), lambda j, i: (i * (j // (NSH - 1)), 0))
    sh = pl.BlockSpec((None, FS, D), lambda j, i: (j, 0, 0))
    act = pl.BlockSpec((None, tm, FS), lambda j, i: (j, i, 0))
    vec = pl.BlockSpec((1, D), lambda j, i: (0, 0))
    return pl.pallas_call(
        body, name="ffn_bwd_act", grid=(NSH, s // tm),
        in_specs=[row, act, act, sh, sh, sh, last, last, last, vec, vec],
        out_specs=[act, act, last, last, vec, vec],
        out_shape=[jax.ShapeDtypeStruct((NSH, s, FS), BF16), jax.ShapeDtypeStruct((NSH, s, FS), BF16),
                   jax.ShapeDtypeStruct((s, D), F32), jax.ShapeDtypeStruct((s, D), BF16),
                   jax.ShapeDtypeStruct((1, D), F32), jax.ShapeDtypeStruct((1, D), F32)],
        scratch_shapes=[pltpu.VMEM((s, D), F32)],
        compiler_params=_cp(2))(df, gate, up, wg, wu, wd, dy, x1, mix, g3, g2)


SMEM_SPEC = pl.BlockSpec(memory_space=pltpu.SMEM)


def _emit_halves(acc_ref, row0, rows, c, own_ref, oth_ref):
    half = rows // 2
    own_ref[...] = acc_ref[pl.ds(row0 + pl.multiple_of(c * half, 8), half), :]
    oth_ref[...] = acc_ref[pl.ds(row0 + pl.multiple_of((1 - c) * half, 8), half), :].astype(BF16)


def _half_shapes(rows):
    return [jax.ShapeDtypeStruct((NSH, rows // 2, D), F32), jax.ShapeDtypeStruct((NSH, rows // 2, D), BF16)]


def _ffn_bwd_w(h2, act_a, dgate, dup, df, c_arr):
    s = h2.shape[0]
    tm = min(TM_FFN_FWD, s)
    nt = s // tm

    def body(c_ref, h_ref, a_ref, dgate_ref, dup_ref, df_ref, *rest):
        outs, accs = rest[:6], rest[6:]
        i = pl.program_id(1)

        @pl.when(i == 0)
        def _():
            for acc in accs:
                acc[...] = jnp.zeros_like(acc)

        h = h_ref[...]
        accs[0][...] += _dot_tn(dgate_ref[...], h)
        accs[1][...] += _dot_tn(dup_ref[...], h)
        accs[2][...] += _dot_tn(a_ref[...], df_ref[...])

        @pl.when(i == nt - 1)
        def _():
            for t, acc in enumerate(accs):
                _emit_halves(acc, 0, FS, c_ref[0], outs[2 * t], outs[2 * t + 1])

    row = lambda w: pl.BlockSpec((tm, w), lambda j, i: (i, 0))
    act = pl.BlockSpec((None, tm, FS), lambda j, i: (j, i, 0))
    half = pl.BlockSpec((None, FS // 2, D), lambda j, i: (j, 0, 0))
    return pl.pallas_call(
        body, name="ffn_bwd_w", grid=(NSH, nt),
        in_specs=[SMEM_SPEC, row(D), act, act, act, row(D)],
        out_specs=[half] * 6,
        out_shape=_half_shapes(FS) * 3,
        scratch_shapes=[pltpu.VMEM((FS, D), F32)] * 3,
        compiler_params=_cp(2))(c_arr, h2, act_a, dgate, dup, df)


def _outproj_bwd(dmix, w_out, pool_o, attn_o, c_arr, dep=None):
    s = dmix.shape[0]
    tm = min(TM, s)
    nt = s // tm

    def body(c_ref, dm_ref, w_ref, p_ref, a_ref, *rest):
        dpool_ref, dattn_ref, own_ref, oth_ref, gw_ref = rest[-5:]
        i = pl.program_id(0)

        @pl.when(i == 0)
        def _():
            gw_ref[...] = jnp.zeros_like(gw_ref)

        dm = dm_ref[...]
        dpool_ref[...] = _dot_nt(dm, w_ref[0:512, :])
        dattn_ref[...] = _dot_nt(dm, w_ref[512:1024, :]).astype(BF16)
        gw_ref[0:512, :] += _dot_tn(p_ref[...], dm)
        gw_ref[512:1024, :] += _dot_tn(a_ref[...], dm)

        @pl.when(i == nt - 1)
        def _():
            for k in range(NSH):
                _emit_halves(gw_ref, k * WOUT_S, WOUT_S, c_ref[0], own_ref.at[k], oth_ref.at[k])

    row = lambda w: pl.BlockSpec((tm, w), lambda i: (i, 0))
    full = pl.BlockSpec((D, D), lambda i: (0, 0))
    half = pl.BlockSpec((NSH, WOUT_S // 2, D), lambda i: (0, 0, 0))
    return pl.pallas_call(
        body, name="outproj_bwd", grid=(nt,),
        in_specs=[SMEM_SPEC, row(D), full, row(512), row(512)] + ([] if dep is None else [ANY]),
        out_specs=[row(512), row(512), half, half],
        out_shape=[jax.ShapeDtypeStruct((s, 512), F32), jax.ShapeDtypeStruct((s, 512), BF16)] + _half_shapes(WOUT_S),
        scratch_shapes=[pltpu.VMEM((D, D), F32)],
        compiler_params=_cp(1))(c_arr, dmix, w_out, pool_o, attn_o, *([] if dep is None else [dep]))


def _pool_bwd(pooled, dpool, w_pool, pool_scale, dep=None):
    s = pooled.shape[0]
    tm = min(TM_POOL, s)
    hb = tm // HALO
    nt = s // tm
    last_halo = s // HALO - 1

    def body(p_ref, dc_ref, dn_ref, wp_ref, sc_ref, *rest):
        du_ref, gwp_ref, gsc_ref = rest[-3:]
        i = pl.program_id(0)

        @pl.when(i == 0)
        def _():
            gwp_ref[...] = jnp.zeros_like(gwp_ref)
            gsc_ref[...] = jnp.zeros_like(gsc_ref)

        dcur = dc_ref[...]
        dnext = jnp.where(i < nt - 1, dn_ref[...], 0.0)
        dext = jnp.concatenate([dcur, dnext], axis=0)
        t_ext = i * tm + lax.broadcasted_iota(jnp.int32, (tm + HALO, GROUP), 0)
        for g, w in enumerate(WINDOWS):
            sl = slice(g * GROUP, (g + 1) * GROUP)
            pb = p_ref[:, sl]
            wp = wp_ref[g]
            mixed = _dot(pb, wp)
            gsc_ref[:, sl] += jnp.sum(dcur[:, sl] * mixed, axis=0, keepdims=True)
            dmixed = (dext[:, sl] * sc_ref[:, sl]).astype(BF16)
            gwp_ref[g] += _dot_tn(pb, dmixed[0:tm])
            dpooled = _dot_nt(dmixed, wp)
            z = dpooled / jnp.minimum(t_ext + 1, w).astype(F32)
            du_ref[:, sl] = (_window_sums(z, w, -1, tm, 2) - dpooled[0:tm]).astype(BF16)

    return pl.pallas_call(
        body, name="pool_bwd", grid=(nt,),
        in_specs=[pl.BlockSpec((tm, 512), lambda i: (i, 0)),
                  pl.BlockSpec((tm, 512), lambda i: (i, 0)),
                  pl.BlockSpec((HALO, 512), lambda i: (jnp.minimum((i + 1) * hb, last_halo), 0)),
                  pl.BlockSpec((4, GROUP, GROUP), lambda i: (0, 0, 0)),
                  pl.BlockSpec((1, 512), lambda i: (0, 0))] + ([] if dep is None else [ANY]),
        out_specs=[pl.BlockSpec((tm, 512), lambda i: (i, 0)),
                   pl.BlockSpec((4, GROUP, GROUP), lambda i: (0, 0, 0)),
                   pl.BlockSpec((1, 512), lambda i: (0, 0))],
        out_shape=[jax.ShapeDtypeStruct((s, 512), BF16), jax.ShapeDtypeStruct((4, GROUP, GROUP), F32),
                   jax.ShapeDtypeStruct((1, 512), F32)],
        compiler_params=_cp(1))(pooled, dpool, dpool, w_pool, pool_scale, *([] if dep is None else [dep]))


def _attn_bwd(q, k, v, do, probs, sink_share, bucket, dep=None):
    s = q.shape[0]
    nblk = s // BLK

    def body(q_ref, do_ref, k_ref, v_ref, prob_ref, ps_ref, bk_ref, *rest):
        dq_ref, dk_ref, dv_ref, grb_ref, gsk_ref, dss_ref = rest[-6:]
        n = pl.program_id(0)

        @pl.when(n == 0)
        def _():
            dk_ref[...] = jnp.zeros_like(dk_ref)
            dv_ref[...] = jnp.zeros_like(dv_ref)
            dss_ref[...] = jnp.zeros_like(dss_ref)
            gsk_ref[...] = jnp.zeros_like(gsk_ref)

        kt, (lo, pstart, cstart) = _kv_band(k_ref, n, True)
        vv, _ = _kv_band(v_ref, n, False)
        lo_q = lax.broadcasted_iota(jnp.int32, (BLK, 128), 1) < HD
        dkk = [jnp.zeros((2 * BLK, 128), F32), jnp.zeros((2 * BLK, 128), F32)]
        dvv = [jnp.zeros((2 * BLK, 128), F32), jnp.zeros((2 * BLK, 128), F32)]

        def by_head(t):
            return jnp.concatenate([jnp.where(lo_q, t, 0.0), jnp.where(lo_q, 0.0, t)], axis=0).astype(BF16)

        for p in range(4):
            h = p // 2
            qt = q_ref[:, p * 128:(p + 1) * 128].astype(F32) * SCALE
            dot = do_ref[:, p * 128:(p + 1) * 128]
            pb = prob_ref[pl.ds(2 * p, 2)]
            prob = pb.astype(F32)
            ps = ps_ref[pl.ds(2 * p, 2), :].reshape(2, 1, BLK)
            dp = _dot_nt(vv[h], dot).reshape(2, 2 * BLK, BLK)
            delta = jnp.sum(prob * dp, axis=1, keepdims=True)
            ds = prob * (dp - delta)
            sink_part = ps * delta
            for e in range(2):
                gs = -jnp.sum(sink_part[e]).reshape(1, 1)
                gsk_ref[pl.ds(2 * p + e, 1), :] += jnp.broadcast_to(gs, (1, 128))
            dss_ref[pl.ds(2 * p, 2)] += ds
            dsb = ds.astype(BF16)
            dq_t = _dot(kt[h], dsb.reshape(4 * BLK, BLK))
            dq_ref[:, p * 128:(p + 1) * 128] = (dq_t.T * SCALE).astype(BF16)
            dkk[h] = dkk[h] + _dot(jnp.concatenate([dsb[0], dsb[1]], axis=1), by_head(qt))
            dvv[h] = dvv[h] + _dot(jnp.concatenate([pb[0], pb[1]], axis=1), by_head(dot.astype(F32)))
        for acc, ref in ((dkk, dk_ref), (dvv, dv_ref)):
            f0 = acc[0] + pltpu.roll(acc[0], HD, axis=1)
            f1 = acc[1] + pltpu.roll(acc[1], HD, axis=1)
            band = jnp.where(lo, f0, f1)
            ref[pl.ds(pstart, BLK), :] += band[0:BLK]
            ref[pl.ds(cstart, BLK), :] += band[BLK:2 * BLK]

        @pl.when(n == nblk - 1)
        def _():
            _relbias_grad(dss_ref, bk_ref[...], grb_ref)

    blk = pl.BlockSpec((BLK, 512), lambda i: (i, 0))
    kv = pl.BlockSpec((s, 128), lambda i: (0, 0))
    row8 = pl.BlockSpec((NQ, 128), lambda i: (0, 0))
    return pl.pallas_call(
        body, name="attn_bwd", grid=(nblk,),
        in_specs=[blk, blk, kv, kv, pl.BlockSpec((None, NQ, 2 * BLK, BLK), lambda i: (i, 0, 0, 0)),
                  pl.BlockSpec((None, NQ, BLK), lambda i: (i, 0, 0)), pl.BlockSpec((2 * BLK, BLK), lambda i: (0, 0))]
        + ([] if dep is None else [ANY]),
        out_specs=[blk, kv, kv, row8, row8],
        out_shape=[jax.ShapeDtypeStruct((s, 512), BF16), jax.ShapeDtypeStruct((s, 128), F32),
                   jax.ShapeDtypeStruct((s, 128), F32), jax.ShapeDtypeStruct((NQ, 128), F32),
                   jax.ShapeDtypeStruct((NQ, 128), F32)],
        scratch_shapes=[pltpu.VMEM((NQ, 2 * BLK, BLK), F32)],
        compiler_params=_cp(1))(q, do, k, v, probs, sink_share, bucket, *([] if dep is None else [dep]))


def _relbias_grad(ds_ref, bucket_v, o_ref):
    lane = lax.broadcasted_iota(jnp.int32, (NQ, 128), 1)
    head_row = lax.broadcasted_iota(jnp.int32, (NQ, BLK), 0)
    acc = jnp.zeros((NQ, 128), F32)
    for b in range(NBUCKET):
        sel = bucket_v == b
        stack = jnp.zeros((NQ, BLK), F32)
        for h in range(NQ):
            col_sums = jnp.sum(jnp.where(sel, ds_ref[h], 0.0), axis=0, keepdims=True)
            stack = jnp.where(head_row == h, col_sums, stack)
        acc = acc + jnp.where(lane == b, jnp.sum(stack, axis=1, keepdims=True), 0.0)
    o_ref[...] = acc


def _inproj_bwd(x, g1, du, dq, dk, dv, w_in, dx1, c_arr):
    s = x.shape[0]
    tm = min(TM, s)
    nt = s // tm
    cols = ((0, 512), (512, 1024), (1024, 1152), (1152, 1280))

    def body(c_ref, x_ref, g_ref, du_ref, dq_ref, dk_ref, dv_ref, w_ref, dx1_ref,
             gx_ref, own_ref, oth_ref, gg_ref, gw_ref):
        i = pl.program_id(0)

        @pl.when(i == 0)
        def _():
            gw_ref[...] = jnp.zeros_like(gw_ref)
            gg_ref[...] = jnp.zeros_like(gg_ref)

        parts = (du_ref[...], dq_ref[...], dk_ref[...].astype(BF16), dv_ref[...].astype(BF16))
        dh = None
        for (a, b), dpart in zip(cols, parts):
            t = _dot(dpart, w_ref[a:b, :])
            dh = t if dh is None else dh + t
        gv = g_ref[...]
        r1, n1 = _rms(x_ref[...])
        h = (n1 * gv).astype(BF16)
        for (a, b), dpart in zip(cols, parts):
            gw_ref[a:b, :] += _dot_tn(dpart, h)
        dx, dg = _rms_bwd(r1, n1, gv, dh)
        gg_ref[...] += dg
        gx_ref[...] = dx1_ref[...] + dx

        @pl.when(i == nt - 1)
        def _():
            for k in range(NSH):
                _emit_halves(gw_ref, k * WIN_S, WIN_S, c_ref[0], own_ref.at[k], oth_ref.at[k])

    row = lambda w: pl.BlockSpec((tm, w), lambda i: (i, 0))
    vec = pl.BlockSpec((1, D), lambda i: (0, 0))
    full = pl.BlockSpec((IN_W, D), lambda i: (0, 0))
    half = pl.BlockSpec((NSH, WIN_S // 2, D), lambda i: (0, 0, 0))
    return pl.pallas_call(
        body, name="inproj_bwd", grid=(nt,),
        in_specs=[SMEM_SPEC, row(D), vec, row(512), row(512), row(128), row(128), full, row(D)],
        out_specs=[row(D), half, half, vec],
        out_shape=[jax.ShapeDtypeStruct((s, D), F32)] + _half_shapes(WIN_S) + [jax.ShapeDtypeStruct((1, D), F32)],
        scratch_shapes=[pltpu.VMEM((IN_W, D), F32)],
        compiler_params=_cp(1))(c_arr, x, g1, du, dq, dk, dv, w_in, dx1)


def _local_step(x, target, small, w_in, w_out, ffn_weights, c_arr, on_ffn_grads=None, on_wout_grads=None):
    g1, g2, g3, g4, w_pool, pool_scale, rel_bias, sinks = small
    bucket = jnp.asarray(_bucket_table())
    wp_bf = w_pool.astype(BF16)
    bias = _bias_table(bucket, rel_bias)
    h1 = _prenorm(x, g1)
    if callable(w_in):
        w_in = w_in(bias, h1)
    u, q, k, v = _inproj_fwd(h1, w_in)
    pool_o, pooled = _pool_fwd(u, wp_bf, pool_scale)
    attn_o, probs, sink_share = _attn_fwd(q, k, v, bias, sinks)
    g2_fwd = g2
    if callable(w_out):
        w_out, after = w_out(pool_o, attn_o)
        if after is not None:
            g2_fwd = g2 + after[:1, :1]
    mix, x1, h2 = _outproj_fwd(pool_o, attn_o, w_out, x, g2_fwd, g3)
    wg, wu, wd = ffn_weights(h2) if callable(ffn_weights) else ffn_weights
    gate, up, act_a, df, dy, loss, gg4 = _ffn_fwd(h2, wg, wu, wd, x1, target, g4)
    dgate, dup, dx1, dmix, gg3, gg2 = _ffn_bwd_act(df, gate, up, wg, wu, wd, dy, x1, mix, g3, g2)
    ffn_g = _ffn_bwd_w(h2, act_a, dgate, dup, df, c_arr)
    dep = None if on_ffn_grads is None else on_ffn_grads(ffn_g)
    dpool, dattn, wout_own, wout_oth = _outproj_bwd(dmix, w_out, pool_o, attn_o, c_arr, dep)
    dep = None if on_wout_grads is None else on_wout_grads(wout_own, wout_oth, dpool)
    du, gwp, gsc = _pool_bwd(pooled, dpool, wp_bf, pool_scale, dep)
    dq, dk, dv, grb, gsk = _attn_bwd(q, k, v, dattn, probs, sink_share, bucket, dep)
    gx, win_own, win_oth, gg1 = _inproj_bwd(x, g1, du, dq, dk, dv, w_in, dx1, c_arr)
    small_grads = (gg1, gg2, gg3, gg4, gwp, gsc, grb, gsk[:, 0].reshape(1, NQ))
    return loss, gx, small_grads, ((win_own, win_oth), (wout_own, wout_oth), ffn_g)


def _place():
    x, y, c = lax.axis_index("x"), lax.axis_index("y"), lax.axis_index("c")
    chips = ((1 - x, y), (x, 1 - y), (1 - x, 1 - y))
    return x, y, c, chips


def _remote(src, dst, ssem, rsem, dev):
    return pltpu.make_async_remote_copy(src_ref=src, dst_ref=dst, send_sem=ssem, recv_sem=rsem,
                                        device_id=dev, device_id_type=MESH_T)


def _to_sibling(arrs, name, after=()):
    nt, na = len(arrs), len(after)

    def body(*refs):
        srcs, dsts = refs[:nt], refs[nt + na:2 * nt + na]
        ssem, rsem = refs[2 * nt + na:]
        x, y, c, _ = _place()
        cps = [_remote(srcs[t], dsts[t], ssem.at[t], rsem.at[t], (x, y, 1 - c)) for t in range(nt)]
        for cp in cps:
            cp.start()
        for cp in cps:
            cp.wait()

    return pl.pallas_call(
        body, name=name, in_specs=[ANY] * (nt + na), out_specs=[ANY] * nt,
        out_shape=[jax.ShapeDtypeStruct(a.shape, a.dtype) for a in arrs],
        scratch_shapes=[pltpu.SemaphoreType.DMA((nt,)), pltpu.SemaphoreType.DMA((nt,))],
        compiler_params=_cp())(*arrs, *after)


HBM_SPEC = pl.BlockSpec(memory_space=pltpu.HBM)
SEM_SPEC = pl.BlockSpec(memory_space=pltpu.SEMAPHORE)
DATAFLOW = pltpu.SideEffectType.DATAFLOW_SIDE_EFFECTING


def _gather_copies(srcs, lands, ssem, rsem):
    x, y, c, chips = _place()
    me = 2 * x + y
    out = []
    for t in range(len(srcs)):
        half = srcs[t].shape[0] // 2
        mine = pl.ds(pl.multiple_of(c * half, 16), half)
        for j, (px, py) in enumerate(chips):
            k = 3 * t + j
            send = _remote(srcs[t].at[mine], lands[t].at[me, mine], ssem.at[k], rsem.at[k], (px, py, c))
            blk = lands[t].at[2 * px + py, mine]
            out.append((send, _remote(blk, blk, ssem.at[k], rsem.at[k], (px, py, c))))
    return out


def _scatter_copies(srcs, lands, ssem, rsem):
    x, y, c, chips = _place()
    out = []
    for t in range(len(srcs)):
        for j, (px, py) in enumerate(chips):
            k = 3 * t + j
            send = _remote(srcs[t].at[2 * px + py], lands[t].at[j], ssem.at[k], rsem.at[k], (px, py, c))
            out.append((send, _remote(lands[t].at[j], lands[t].at[j], ssem.at[k], rsem.at[k], (px, py, c))))
    return out


def _sibling_copies(srcs, lands, ssem, rsem):
    x, y, c, _ = _place()
    sib = (x, y, 1 - c)
    return [(_remote(srcs[t], lands[t], ssem.at[t], rsem.at[t], sib),
             _remote(lands[t], lands[t], ssem.at[t], rsem.at[t], sib)) for t in range(len(srcs))]


def _peer_copies(srcs, lands, ssem, rsem):
    x, y, c, _ = _place()
    me = 4 * x + 2 * y + c
    out = []
    for kk in range(1, 8):
        px, py, pc = x ^ (kk >> 2), y ^ ((kk >> 1) & 1), c ^ (kk & 1)
        send = _remote(srcs[0], lands[0].at[me], ssem.at[kk - 1], rsem.at[kk - 1], (px, py, pc))
        slot = lands[0].at[4 * px + 2 * py + pc]
        out.append((send, _remote(slot, slot, ssem.at[kk - 1], rsem.at[kk - 1], (px, py, pc))))
    return out


def _forward_copies(srcs, lands, ssem, rsem):
    x, y, c, chips = _place()
    sib = (x, y, 1 - c)
    out = []
    for t in range(len(lands)):
        half = lands[t].shape[1] // 2
        mine = pl.ds(pl.multiple_of(c * half, 16), half)
        other = pl.ds(pl.multiple_of((1 - c) * half, 16), half)
        for j, (px, py) in enumerate(chips):
            k = 3 * t + j
            blk_m, blk_o = lands[t].at[2 * px + py, mine], lands[t].at[2 * px + py, other]
            out.append((_remote(blk_m, blk_m, ssem.at[k], rsem.at[k], sib),
                        _remote(blk_o, blk_o, ssem.at[k], rsem.at[k], sib)))
    return out


def _win_and_small_copies(srcs, lands, ssem, rsem):
    return (_scatter_copies(srcs[:1], lands[:1], ssem, rsem)
            + _peer_copies(srcs[1:], lands[1:], ssem.at[pl.ds(3, 7)], rsem.at[pl.ds(3, 7)]))


def _split_start(plan, ncopy, srcs, lands, after, name):
    ns, nbuf = len(srcs), len(srcs) + len(lands)
    extra = [] if after is None else [after]
    n_in = nbuf + len(extra)

    def body(*refs):
        ssem, rsem, token = refs[n_in], refs[n_in + 1], refs[-1]
        for send, _ in plan(refs[:ns], refs[ns:nbuf], ssem, rsem):
            send.start()
        token[...] = jnp.zeros_like(token)

    bufs = [pltpu.with_memory_space_constraint(a, pltpu.HBM) for a in list(srcs) + list(lands)]
    outs = pl.pallas_call(
        body, name=name, in_specs=[HBM_SPEC] * nbuf + [ANY] * len(extra),
        out_specs=[SEM_SPEC, SEM_SPEC] + [HBM_SPEC] * nbuf + [pl.BlockSpec(memory_space=pltpu.VMEM)],
        out_shape=[pltpu.SemaphoreType.DMA((ncopy,)), pltpu.SemaphoreType.DMA((ncopy,))]
        + [pltpu.HBM(a.shape, a.dtype) for a in bufs] + [jax.ShapeDtypeStruct((8, 128), F32)],
        input_output_aliases={i: 2 + i for i in range(nbuf)},
        compiler_params=pltpu.CompilerParams(has_side_effects=DATAFLOW))(*bufs, *extra)
    return outs[0], outs[1], outs[2:2 + ns], outs[2 + ns:2 + nbuf], outs[-1]


def _split_wait(plan, ssem, rsem, srcs, lands, after, name, only=None):
    ns, nbuf = len(srcs), len(srcs) + len(lands)

    def body(*refs):
        s_ref, r_ref = refs[nbuf], refs[nbuf + 1]
        for k, (send, recv) in enumerate(plan(refs[:ns], refs[ns:nbuf], s_ref, r_ref)):
            if only is None or k in only:
                send.wait_send()
                recv.wait_recv()

    outs = pl.pallas_call(
        body, name=name, in_specs=[HBM_SPEC] * nbuf + [SEM_SPEC, SEM_SPEC] + [ANY] * len(after),
        out_specs=[HBM_SPEC] * nbuf,
        out_shape=[pltpu.HBM(a.shape, a.dtype) for a in list(srcs) + list(lands)],
        input_output_aliases={i: i for i in range(nbuf)},
        compiler_params=pltpu.CompilerParams(has_side_effects=DATAFLOW))(*srcs, *lands, ssem, rsem, *after)
    return outs


def _gather_finish(lands, tag):
    nt = len(lands)

    def body(*refs):
        outs = refs[nt:2 * nt]
        dsend, drecv = refs[2 * nt:]
        x, y, c, chips = _place()
        sib = (x, y, 1 - c)
        sends = []
        for t in range(nt):
            half = outs[t].shape[1] // 2
            mine = pl.ds(pl.multiple_of(c * half, 16), half)
            for j, (px, py) in enumerate(chips):
                blk = outs[t].at[2 * px + py, mine]
                cp = _remote(blk, blk, dsend.at[t, j], drecv.at[t, j], sib)
                cp.start()
                sends.append(cp)
        for t in range(nt):
            half = outs[t].shape[1] // 2
            other = pl.ds(pl.multiple_of((1 - c) * half, 16), half)
            for j, (px, py) in enumerate(chips):
                blk = outs[t].at[2 * px + py, other]
                _remote(blk, blk, dsend.at[t, j], drecv.at[t, j], sib).wait_recv()
        for cp in sends:
            cp.wait_send()

    return pl.pallas_call(
        body, name="gather_finish_" + tag, in_specs=[ANY] * nt, out_specs=[ANY] * nt,
        out_shape=[jax.ShapeDtypeStruct(a.shape, a.dtype) for a in lands],
        input_output_aliases={t: t for t in range(nt)},
        scratch_shapes=[pltpu.SemaphoreType.DMA((nt, 3)), pltpu.SemaphoreType.DMA((nt, 3))],
        compiler_params=_cp())(*lands)


def _chip_partial(owns, recv, chip_arr, tag):
    nt = len(owns)

    def body(chip_ref, *refs):
        k = pl.program_id(0)
        for t in range(nt):
            p = refs[t][...] + refs[nt + t][...].astype(F32)
            refs[2 * nt + t][...] = p.astype(BF16)

            @pl.when(k == chip_ref[0])
            def _():
                refs[3 * nt + t][...] = p

    blk_specs = [pl.BlockSpec((None,) + a.shape[1:], lambda k, chip_ref: (k, 0, 0)) for a in owns]
    own_specs = [pl.BlockSpec(a.shape[1:], lambda k, chip_ref: (0, 0)) for a in owns]
    return pl.pallas_call(
        body, name="rs_chip_partial_" + tag,
        grid_spec=pltpu.PrefetchScalarGridSpec(num_scalar_prefetch=1, grid=(NSH,), in_specs=blk_specs * 2,
                                               out_specs=blk_specs + own_specs),
        out_shape=[jax.ShapeDtypeStruct(a.shape, BF16) for a in owns]
        + [jax.ShapeDtypeStruct(a.shape[1:], F32) for a in owns],
        compiler_params=_cp(1))(chip_arr, *owns, *recv)


def _sum_chips(own, recv, tag):
    nt = len(own)

    def body(*refs):
        outs = refs[2 * nt:]
        for t in range(nt):
            r = refs[nt + t]
            outs[t][...] = ((refs[t][...] + r[0].astype(F32)) + r[1].astype(F32)) + r[2].astype(F32)

    vm = pl.BlockSpec(memory_space=pltpu.VMEM)
    return pl.pallas_call(
        body, name="rs_sum_chips_" + tag, in_specs=[vm] * (2 * nt), out_specs=[vm] * nt,
        out_shape=[jax.ShapeDtypeStruct(a.shape, F32) for a in own],
        compiler_params=_cp())(*own, *recv)


def _sum_chips_shared(own, recv, tag):
    def body(own_ref, recv_ref, out_ref, sib_ref, ssem, rsem):
        out_ref[...] = ((own_ref[...] + recv_ref[0].astype(F32)) + recv_ref[1].astype(F32)) + recv_ref[2].astype(F32)
        x, y, c, _ = _place()
        cp = _remote(out_ref, sib_ref, ssem.at[0], rsem.at[0], (x, y, 1 - c))
        cp.start()
        cp.wait()

    vm = pl.BlockSpec(memory_space=pltpu.VMEM)
    return pl.pallas_call(
        body, name="rs_sum_share_" + tag, in_specs=[vm, vm], out_specs=[vm, ANY],
        out_shape=[jax.ShapeDtypeStruct(own.shape, F32)] * 2,
        scratch_shapes=[pltpu.SemaphoreType.DMA((1,)), pltpu.SemaphoreType.DMA((1,))],
        compiler_params=_cp())(own, recv)


def _adamw_math(w, g, m, v):
    m = ADAM_B1 * m + (1.0 - ADAM_B1) * g
    v = ADAM_B2 * v + (1.0 - ADAM_B2) * (g * g)
    m_hat = m / (1.0 - ADAM_B1 ** ADAM_STEP)
    v_hat = v / (1.0 - ADAM_B2 ** ADAM_STEP)
    delta = -ADAM_LR * (m_hat / (jnp.sqrt(v_hat) + ADAM_EPS) + ADAM_WD * w)
    return delta, m, v


ADAM_SPLIT = 4


def _adamw_shards(own, sib, ws, ms, vs, c_arr, tag):
    nt = len(own)

    def body(c_ref, *refs):
        hh = pl.program_id(0)
        mine = hh == c_ref[0]
        for t in range(nt):
            g = jnp.where(mine, refs[t][...], refs[nt + t][...])
            delta, m, v = _adamw_math(refs[2 * nt + t][...], g, refs[3 * nt + t][...], refs[4 * nt + t][...])
            refs[5 * nt + t][...] = g
            refs[6 * nt + t][...] = delta
            refs[7 * nt + t][...] = m
            refs[8 * nt + t][...] = v

    def tile(a):
        return (a.shape[0] // ADAM_SPLIT, a.shape[1])

    def half_index(used_when_mine):
        def index(hh, q, c_ref):
            mine = 1 - (hh - c_ref[0]) * (hh - c_ref[0])
            used = mine if used_when_mine else 1 - mine
            return (used * q + (1 - used) * hh * (ADAM_SPLIT - 1), 0)
        return index

    own_specs = [pl.BlockSpec(tile(a), half_index(True)) for a in own]
    sib_specs = [pl.BlockSpec(tile(a), half_index(False)) for a in own]
    full_specs = [pl.BlockSpec(tile(a), lambda hh, q, c_ref: (hh * ADAM_SPLIT + q, 0)) for a in own]
    return pl.pallas_call(
        body, name="adamw_shards_" + tag,
        grid_spec=pltpu.PrefetchScalarGridSpec(num_scalar_prefetch=1, grid=(2, ADAM_SPLIT),
                                               in_specs=own_specs + sib_specs + full_specs * 3,
                                               out_specs=full_specs * 4),
        out_shape=[jax.ShapeDtypeStruct(w.shape, F32) for w in ws] * 4,
        compiler_params=_cp(2))(c_arr, *own, *sib, *ws, *ms, *vs)


SMALL_WPOOL_ROW = 32
SMALL_SCALE_ROW = SMALL_WPOOL_ROW + 4 * GROUP
SMALL_BIAS_ROW = SMALL_SCALE_ROW + 8
SMALL_SINKS_ROW = SMALL_BIAS_ROW + NQ
SMALL_LOSS_ROW = SMALL_SINKS_ROW + 8


def _small_sum_adamw(gathered, wp, mp, vp):
    rows = wp.shape[0]

    def body(g_ref, w_ref, m_ref, v_ref, *rest):
        outs, res = rest[:-1], rest[-1]
        g = g_ref[0]
        for d in range(1, 8):
            g = g + g_ref[d]
        delta, m, v = _adamw_math(w_ref[...], g, m_ref[...], v_ref[...])
        for kind, val in enumerate((g, delta, m, v)):
            res[kind] = val
            gains = outs[8 * kind:8 * kind + 4]
            w_pool_o, scale_o, bias_o, sinks_o = outs[8 * kind + 4:8 * kind + 8]
            for t in range(4):
                for i in range(8):
                    gains[t][:, 128 * i:128 * (i + 1)] = res[kind, pl.ds(8 * t + i, 1), :]
            for grp in range(4):
                w_pool_o[grp] = res[kind, pl.ds(SMALL_WPOOL_ROW + GROUP * grp, GROUP), :]
            for i in range(4):
                scale_o[:, 128 * i:128 * (i + 1)] = res[kind, pl.ds(SMALL_SCALE_ROW + i, 1), :]
            bias_o[...] = res[kind, pl.ds(SMALL_BIAS_ROW, NQ), :][:, 0:NBUCKET]
            sinks_o[...] = res[kind, pl.ds(SMALL_SINKS_ROW, 1), :][:, 0:NQ]
        outs[-1][...] = res[0, pl.ds(SMALL_LOSS_ROW, 1), :]

    vm = pl.BlockSpec(memory_space=pltpu.VMEM)
    one_kind = [jax.ShapeDtypeStruct((1, D), F32)] * 4 + [
        jax.ShapeDtypeStruct((4, GROUP, GROUP), F32), jax.ShapeDtypeStruct((1, POOL_W), F32),
        jax.ShapeDtypeStruct((NQ, NBUCKET), F32), jax.ShapeDtypeStruct((1, NQ), F32)]
    return pl.pallas_call(
        body, name="small_sum_adamw", in_specs=[vm] * 4, out_specs=[vm] * 33,
        out_shape=one_kind * 4 + [jax.ShapeDtypeStruct((1, 128), F32)],
        scratch_shapes=[pltpu.VMEM((4, rows, 128), F32)],
        compiler_params=_cp())(gathered, wp, mp, vp)


def _pack_small(gains4, w_pool, pool_scale, rel_bias_t, sinks, loss):
    bias_rows = jnp.pad(rel_bias_t, ((0, 0), (0, 128 - rel_bias_t.shape[1])))
    parts = list(gains4) + [w_pool, pool_scale, bias_rows, sinks, loss]
    rows = []
    for a in parts:
        flat = a.reshape(-1)
        n = flat.shape[0]
        padded = -(-n // 1024) * 1024
        rows.append(jnp.pad(flat, (0, padded - n)).reshape(-1, 128))
    return jnp.concatenate(rows, axis=0)


def kernel(x, g_pre_mix, w_in, w_pool, pool_scale, rel_bias, sinks, w_out, g_post_mix, g_pre_ffn, w_gate, w_up, w_down, g_post_ffn, loss_target, m_g_pre_mix, m_w_in, m_w_pool, m_pool_scale, m_rel_bias, m_sinks, m_w_out, m_g_post_mix, m_g_pre_ffn, m_w_gate, m_w_up, m_w_down, m_g_post_ffn, v_g_pre_mix, v_w_in, v_w_pool, v_pool_scale, v_rel_bias, v_sinks, v_w_out, v_g_post_mix, v_g_pre_ffn, v_w_gate, v_w_up, v_w_down, v_g_post_ffn):
    c = lax.axis_index("c")
    chip = 2 * lax.axis_index("x") + lax.axis_index("y")

    def shards(a_in, a_out, a_gate, a_up, a_down):
        return (a_in[0].T, a_out[0], a_gate[0].T, a_up[0].T, a_down[0])

    c_arr = c.reshape(1).astype(jnp.int32)
    chip_arr = chip.reshape(1).astype(jnp.int32)

    big_w = shards(w_in, w_out, w_gate, w_up, w_down)
    big_bf = [w.astype(BF16) for w in big_w]
    g_ssem, g_rsem, g_srcs, g_lands, g_token = _split_start(
        _gather_copies, 15, big_bf, [lax.empty((NSH,) + a.shape, BF16) for a in big_bf], None, "gather_start")
    fw = {}

    def with_own(land, shard):
        return lax.dynamic_update_slice(land, shard[None], (chip, 0, 0))

    def w_in_ready(*after):
        fw["first"] = _split_wait(_gather_copies, g_ssem, g_rsem, g_srcs, g_lands, after, "gather_win_wait",
                                  only=(0, 1, 2))
        (fw["win"],) = _gather_finish([with_own(fw["first"][5], fw["first"][0])], "win")
        return fw["win"].reshape(IN_W, D)

    def w_out_ready(*after):
        first = fw["first"]
        fw["second"] = _split_wait(_gather_copies, g_ssem, g_rsem, first[:5], [fw["win"]] + list(first[6:]), after,
                                   "gather_wout_wait", only=(3, 4, 5))
        second = fw["second"]
        (wout_all,) = _gather_finish([with_own(second[6], second[1])], "wout")
        outs = _split_wait(_gather_copies, g_ssem, g_rsem, second[:5], [second[5], wout_all] + list(second[7:]),
                           [wout_all], "gather_ffn_wait", only=tuple(range(6, 15)))
        lands = [with_own(land, src) for src, land in zip(outs[2:5], outs[7:])]
        fw["f"] = _split_start(_forward_copies, 9, [], lands, None, "gather_forward_start")
        return wout_all.reshape(D, D), fw["f"][4]

    def ffn_weights(after):
        ssem, rsem, _, lands, _ = fw["f"]
        return _split_wait(_forward_copies, ssem, rsem, [], lands, [after], "gather_forward_wait")

    rs = {}

    def on_ffn_grads(ffn_g):
        oths = ffn_g[1::2]
        rs["ffn_own"] = ffn_g[0::2]
        rs["x"] = _split_start(_sibling_copies, 3, oths, [lax.empty(a.shape, BF16) for a in oths], ffn_g[0],
                               "rs_exchange_ffn_start")
        return rs["x"][4]

    def on_wout_grads(own, oth, after):
        ssem, rsem, srcs, lands, _ = rs["x"]
        recv_ffn = _split_wait(_sibling_copies, ssem, rsem, srcs, lands, [after], "rs_exchange_ffn_wait")[3:]
        recv_wout = _to_sibling([oth], "rs_exchange_wout")
        outs = _chip_partial([own] + list(rs["ffn_own"]), list(recv_wout) + list(recv_ffn), chip_arr, "early")
        rs["early_own"] = outs[4:]
        rs["s"] = _split_start(_scatter_copies, 12, outs[:4],
                               [lax.empty((3,) + a.shape[1:], BF16) for a in outs[:4]], outs[4], "scatter_early_start")
        return rs["s"][4]

    small = (g_pre_mix + g_token[:1, :1], g_post_mix, g_pre_ffn, g_post_ffn, w_pool[0], pool_scale, rel_bias, sinks)
    loss, gx, small_grads, ((win_own, win_oth), _, _) = _local_step(
        x[0], loss_target[0], small, w_in_ready, w_out_ready, ffn_weights, c_arr, on_ffn_grads, on_wout_grads)

    ssem, rsem, srcs, lands, _ = rs["s"]
    recv_early = _split_wait(_scatter_copies, ssem, rsem, srcs, lands, [gx], "scatter_early_wait")[4:]
    finals = _sum_chips(list(rs["early_own"]), list(recv_early), "early")
    to_sib = [win_oth] + list(finals)
    sh_ssem, sh_rsem, sh_srcs, sh_lands, _ = _split_start(
        _sibling_copies, 5, to_sib, [lax.empty(a.shape, a.dtype) for a in to_sib], None, "rs_share_start")

    unused = jnp.zeros((1, 1), F32)
    gp = _pack_small(small_grads[:4], *small_grads[4:], loss)
    wp = _pack_small((g_pre_mix, g_post_mix, g_pre_ffn, g_post_ffn), w_pool, pool_scale, rel_bias.T, sinks, unused)
    mp = _pack_small((m_g_pre_mix, m_g_post_mix, m_g_pre_ffn, m_g_post_ffn), m_w_pool, m_pool_scale, m_rel_bias.T,
                     m_sinks, unused)
    vp = _pack_small((v_g_pre_mix, v_g_post_mix, v_g_pre_ffn, v_g_post_ffn), v_w_pool, v_pool_scale, v_rel_bias.T,
                     v_sinks, unused)

    moments = (shards(m_w_in, m_w_out, m_w_gate, m_w_up, m_w_down),
               shards(v_w_in, v_w_out, v_w_gate, v_w_up, v_w_down))
    sh_first = _split_wait(_sibling_copies, sh_ssem, sh_rsem, sh_srcs, sh_lands, [], "rs_share_win_wait", only=(0,))
    outs = _chip_partial([win_own], [sh_first[5]], chip_arr, "win")
    slots = lax.dynamic_update_slice(lax.empty((8,) + gp.shape, F32), gp[None], (2 * chip + c, 0, 0))
    w_ssem, w_rsem, w_srcs, w_lands, w_token = _split_start(
        _win_and_small_copies, 10, [outs[0], gp], [lax.empty((3,) + outs[0].shape[1:], BF16), slots], gx,
        "scatter_win_start")
    shared = _split_wait(_sibling_copies, sh_ssem, sh_rsem, sh_first[:5], sh_first[5:], [w_token],
                         "rs_share_early_wait", only=(1, 2, 3, 4))
    adam_e = _adamw_shards(shared[1:5], shared[6:], big_w[1:], moments[0][1:], moments[1][1:], c_arr, "early")
    w_first = _split_wait(_win_and_small_copies, w_ssem, w_rsem, w_srcs, w_lands, [adam_e[0]], "scatter_win_wait",
                          only=(0, 1, 2))
    win_final, win_sib = _sum_chips_shared(outs[1], w_first[2], "win")
    adam_w = _adamw_shards([win_final], [win_sib], big_w[:1], moments[0][:1], moments[1][:1], c_arr, "win")
    big_g, big_d, big_m, big_v = [[adam_w[i]] + list(adam_e[4 * i:4 * i + 4]) for i in range(4)]

    gathered = _split_wait(_win_and_small_copies, w_ssem, w_rsem, w_first[:2], w_first[2:], [adam_w[0]],
                           "small_allgather_wait", only=tuple(range(3, 10)))[3]
    flat = _small_sum_adamw(gathered, wp, mp, vp)
    small_out = [flat[8 * kind:8 * kind + 8] for kind in range(4)]
    total_loss = flat[-1][0, 0]

    def ordered(small8, big4):
        sg1, sg2, sg3, sg4, swp, ssc, srb, ssk = small8
        swp, srb = swp[None], srb.T
        bwin, bwout, bwg, bwu, bwd = big4[0].T[None], big4[1][None], big4[2].T[None], big4[3].T[None], big4[4][None]
        return [sg1, bwin, swp, ssc, srb, ssk, bwout, sg2, sg3, bwg, bwu, bwd, sg4]

    res = [total_loss, gx[None]]
    for sm, bg in zip(small_out, (big_g, big_d, big_m, big_v)):
        res += ordered(sm, bg)
    return tuple(res)
```

```python
import functools

import numpy as np
import jax
import jax.numpy as jnp
from jax import lax
from jax.experimental import pallas as pl
from jax.experimental.pallas import tpu as pltpu

F32 = jnp.float32
BF16 = jnp.bfloat16

D = 1024
POOL_W = 512
GROUP = 128
WINDOWS = (2, 4, 8, 16)
HALO = 128
NQ = 8
BLK = 128
NBUCKET = 32
IN_W = 1280
DFF = 2816
NSH = 4
FS = DFF // NSH
WIN_S = IN_W // NSH
WOUT_S = D // NSH
EPS = 1e-6
NEG = -1e30
SCALE = 0.125

ADAM_LR = 0.001
ADAM_B1 = 0.9
ADAM_B2 = 0.999
ADAM_EPS = 1e-08
ADAM_WD = 0.01
ADAM_STEP = 10

TM = 512
TM_IN = 1024
TM_POOL = 512
TM_FFN = 512
TM_FFN_FWD = 1024
FFN_ROWS = 256
VMEM_LIMIT = 60 * 1024 * 1024

MESH_T = pl.DeviceIdType.MESH
ANY = pl.BlockSpec(memory_space=pl.ANY)


def _cp(n_grid=0, **kw):
    sem = ("arbitrary",) * n_grid if n_grid else None
    return pltpu.CompilerParams(dimension_semantics=sem, vmem_limit_bytes=VMEM_LIMIT, **kw)


def _dot(a, b):
    return jnp.dot(a, b, preferred_element_type=F32)


def _dot_nt(a, b):
    return lax.dot_general(a, b, (((1,), (1,)), ((), ())), preferred_element_type=F32)


def _dot_tn(a, b):
    return lax.dot_general(a, b, (((0,), (0,)), ((), ())), preferred_element_type=F32)


def _rms(x):
    r = lax.rsqrt(jnp.mean(x * x, axis=-1, keepdims=True) + EPS)
    return r, x * r


def _rms_bwd(r, n, g, dout):
    dn = dout * g
    dx = r * (dn - n * jnp.mean(dn * n, axis=-1, keepdims=True))
    dg = jnp.sum(dout * n, axis=0, keepdims=True)
    return dx, dg


def _split(x, n):
    parts = []
    r = x
    for _ in range(n):
        p = r.astype(BF16)
        parts.append(p)
        r = r - p.astype(F32)
    return parts


def _band_dot(a, x, n):
    acc = None
    for p in _split(x, n):
        t = _dot(a, p)
        acc = t if acc is None else acc + t
    return acc


def _bucket_table():
    qi = np.arange(BLK)[None, :]
    kj = np.arange(2 * BLK)[:, None]
    dist = qi + BLK - kj
    n = np.maximum(dist, 0)
    nf = np.maximum(n, 1).astype(np.float32)
    large = 16 + (np.log(nf / np.float32(16)) / np.float32(np.log(128 / 16)) * np.float32(16)).astype(np.int32)
    large = np.minimum(large, NBUCKET - 1)
    return np.where(n < 16, n, large).astype(np.int32)


def _prenorm(x, g1):
    s = x.shape[0]
    tm = min(TM_IN, s)

    def body(x_ref, g_ref, h_ref):
        _, n = _rms(x_ref[...])
        h_ref[...] = (n * g_ref[...]).astype(BF16)

    row = pl.BlockSpec((tm, D), lambda i: (i, 0))
    return pl.pallas_call(
        body, name="prenorm", grid=(s // tm,),
        in_specs=[row, pl.BlockSpec((1, D), lambda i: (0, 0))], out_specs=row,
        out_shape=jax.ShapeDtypeStruct((s, D), BF16),
        compiler_params=_cp(1))(x, g1)


def _inproj_fwd(h1, w_in):
    s = h1.shape[0]
    tm = min(TM_IN, s)

    def body(h_ref, w_ref, u_ref, q_ref, k_ref, v_ref):
        proj = _dot_nt(h_ref[...], w_ref[...])
        u_ref[...] = proj[:, :512]
        q_ref[...] = proj[:, 512:1024].astype(BF16)
        k_ref[...] = proj[:, 1024:1152].astype(BF16)
        v_ref[...] = proj[:, 1152:1280].astype(BF16)

    row = lambda w: pl.BlockSpec((tm, w), lambda i: (i, 0))
    return pl.pallas_call(
        body, name="inproj_fwd", grid=(s // tm,),
        in_specs=[row(D), pl.BlockSpec((IN_W, D), lambda i: (0, 0))],
        out_specs=[row(512), row(512), row(128), row(128)],
        out_shape=[jax.ShapeDtypeStruct((s, 512), F32), jax.ShapeDtypeStruct((s, 512), BF16),
                   jax.ShapeDtypeStruct((s, 128), BF16), jax.ShapeDtypeStruct((s, 128), BF16)],
        compiler_params=_cp(1))(h1, w_in)


def _window_sums(ext, w, lag_sign, tm, terms):
    rows = lax.broadcasted_iota(jnp.int32, (HALO, 2 * HALO), 0)
    cols = lax.broadcasted_iota(jnp.int32, (HALO, 2 * HALO), 1)
    d = rows + HALO - cols if lag_sign > 0 else cols - rows
    band = jnp.where((d >= 0) & (d < w), 1.0, 0.0).astype(BF16)
    return jnp.concatenate([_band_dot(band, ext[r:r + 2 * HALO], terms) for r in range(0, tm, HALO)], axis=0)


def _pooled(ext, cur, t0, tm):
    t = t0 + lax.broadcasted_iota(jnp.int32, (tm, GROUP), 0)
    out = []
    for g, w in enumerate(WINDOWS):
        sl = slice(g * GROUP, (g + 1) * GROUP)
        cnt = jnp.minimum(t + 1, w).astype(F32)
        out.append(_window_sums(ext[:, sl], w, 1, tm, 2) / cnt - cur[:, sl])
    return out


def _pool_fwd(u, w_pool, pool_scale):
    s = u.shape[0]
    tm = min(TM_POOL, s)
    hb = tm // HALO

    def body(uc_ref, uh_ref, wp_ref, sc_ref, o_ref, pooled_ref):
        i = pl.program_id(0)
        cur = uc_ref[...]
        halo = jnp.where(i > 0, uh_ref[...], 0.0)
        ext = jnp.concatenate([halo, cur], axis=0)
        pooled = _pooled(ext, cur, i * tm, tm)
        for g in range(4):
            sl = slice(g * GROUP, (g + 1) * GROUP)
            pb = pooled[g].astype(BF16)
            pooled_ref[:, sl] = pb
            o_ref[:, sl] = (_dot(pb, wp_ref[g]) * sc_ref[:, sl]).astype(BF16)

    row = pl.BlockSpec((tm, 512), lambda i: (i, 0))
    return pl.pallas_call(
        body, name="pool_fwd", grid=(s // tm,),
        in_specs=[row, pl.BlockSpec((HALO, 512), lambda i: (jnp.maximum(i * hb - 1, 0), 0)),
                  pl.BlockSpec((4, GROUP, GROUP), lambda i: (0, 0, 0)),
                  pl.BlockSpec((1, 512), lambda i: (0, 0))],
        out_specs=[row, row],
        out_shape=[jax.ShapeDtypeStruct((s, 512), BF16)] * 2,
        compiler_params=_cp(1))(u, u, w_pool, pool_scale)


def _bias_table(bucket, rel_bias):
    def body(b_ref, rb_ref, o_ref):
        bucket_v = b_ref[...]
        key = lax.broadcasted_iota(jnp.int32, (2 * BLK, BLK), 0)
        dist = lax.broadcasted_iota(jnp.int32, (2 * BLK, BLK), 1) + BLK - key
        inwin = (dist >= 0) & (dist < BLK)
        for h in range(NQ):
            acc = jnp.zeros((2 * BLK, BLK), F32)
            for b in range(NBUCKET):
                acc = jnp.where(bucket_v == b, rb_ref[b, h], acc)
            rest = jnp.where(inwin, acc, NEG)
            o_ref[1, h] = rest
            o_ref[0, h] = jnp.where(key >= BLK, rest, NEG)

    return pl.pallas_call(
        body, name="bias_table",
        in_specs=[pl.BlockSpec(memory_space=pltpu.VMEM), pl.BlockSpec(memory_space=pltpu.SMEM)],
        out_specs=pl.BlockSpec(memory_space=pltpu.VMEM),
        out_shape=jax.ShapeDtypeStruct((2, NQ, 2 * BLK, BLK), F32),
        compiler_params=_cp())(bucket, rel_bias)


HD = 64


def _kv_band(ref, n, transposed):
    pstart = pl.multiple_of(jnp.maximum(n - 1, 0) * BLK, BLK)
    cstart = pl.multiple_of(n * BLK, BLK)
    lo = lax.broadcasted_iota(jnp.int32, (2 * BLK, 128), 1) < HD
    band = jnp.concatenate([ref[pl.ds(pstart, BLK), :], ref[pl.ds(cstart, BLK), :]], axis=0).astype(F32)
    rot = pltpu.roll(band, HD, axis=1)
    halves = ((jnp.where(lo, band, 0.0), jnp.where(lo, 0.0, rot)), (jnp.where(lo, rot, 0.0), jnp.where(lo, 0.0, band)))
    if transposed:
        out = [jnp.concatenate([a.T, b.T], axis=1).astype(BF16) for a, b in halves]
    else:
        out = [jnp.concatenate([a, b], axis=0).astype(BF16) for a, b in halves]
    return out, (lo, pstart, cstart)


def _pair_probs(qt, kk, bias, sk_ref, p):
    sink = jnp.concatenate([jnp.full((1, 1, BLK), sk_ref[0, 2 * p + e], F32) for e in range(2)], axis=0)
    s = _dot_nt(kk, qt).reshape(2, 2 * BLK, BLK) + bias
    m = jnp.maximum(jnp.max(s, axis=1, keepdims=True), sink)
    pr = jnp.exp(s - m)
    es = jnp.exp(sink - m)
    inv = 1.0 / (jnp.sum(pr, axis=1, keepdims=True) + es)
    return pr * inv, es * inv


def _attn_fwd(q, k, v, bias, sinks):
    s = q.shape[0]
    nblk = s // BLK

    def body(q_ref, k_ref, v_ref, b_ref, sk_ref, o_ref, prob_ref, ps_ref):
        n = pl.program_id(0)
        kk, _ = _kv_band(k_ref, n, False)
        vt, _ = _kv_band(v_ref, n, True)
        bidx = jnp.minimum(n, 1)
        for p in range(4):
            h = p // 2
            qt = (q_ref[:, p * 128:(p + 1) * 128].astype(F32) * SCALE).astype(BF16)
            prob, ps = _pair_probs(qt, kk[h], b_ref[bidx, pl.ds(2 * p, 2)], sk_ref, p)
            pb = prob.astype(BF16)
            prob_ref[pl.ds(2 * p, 2)] = pb
            ps_ref[pl.ds(2 * p, 2), :] = ps.reshape(2, BLK)
            acc_t = _dot(vt[h], pb.reshape(4 * BLK, BLK))
            o_ref[:, p * 128:(p + 1) * 128] = acc_t.T.astype(BF16)

    return pl.pallas_call(
        body, name="attn_fwd", grid=(nblk,),
        in_specs=[pl.BlockSpec((BLK, 512), lambda i: (i, 0)),
                  pl.BlockSpec((s, 128), lambda i: (0, 0)),
                  pl.BlockSpec((s, 128), lambda i: (0, 0)),
                  pl.BlockSpec((2, NQ, 2 * BLK, BLK), lambda i: (0, 0, 0, 0)),
                  pl.BlockSpec(memory_space=pltpu.SMEM)],
        out_specs=[pl.BlockSpec((BLK, 512), lambda i: (i, 0)),
                   pl.BlockSpec((None, NQ, 2 * BLK, BLK), lambda i: (i, 0, 0, 0)),
                   pl.BlockSpec((None, NQ, BLK), lambda i: (i, 0, 0))],
        out_shape=[jax.ShapeDtypeStruct((s, 512), BF16), jax.ShapeDtypeStruct((nblk, NQ, 2 * BLK, BLK), BF16),
                   jax.ShapeDtypeStruct((nblk, NQ, BLK), F32)],
        compiler_params=_cp(1))(q, k, v, bias, sinks)


def _outproj_fwd(pool_o, attn_o, w_out, x, g2, g3, between=None):
    s = x.shape[0]
    tm = min(TM, s)
    nt = s // tm

    def part(name, tile0, tiles, filled, dep):
        def body(p_ref, a_ref, w_ref, x_ref, g2_ref, g3_ref, *rest):
            mix_ref, x1_ref, h2_ref = rest[-3:]
            mix = _dot(p_ref[...], w_ref[0:512, :]) + _dot(a_ref[...], w_ref[512:1024, :])
            mix_ref[...] = mix
            _, n2 = _rms(mix)
            x1 = x_ref[...] + n2 * g2_ref[...]
            x1_ref[...] = x1
            _, n3 = _rms(x1)
            h2_ref[...] = (n3 * g3_ref[...]).astype(BF16)

        row = lambda w: pl.BlockSpec((tm, w), lambda i: (i + tile0, 0))
        vec = pl.BlockSpec((1, D), lambda i: (0, 0))
        extra = list(filled) + ([] if dep is None else [dep])
        return pl.pallas_call(
            body, name=name, grid=(tiles,),
            in_specs=[row(512), row(512), pl.BlockSpec((D, D), lambda i: (0, 0)), row(D), vec, vec]
            + [ANY] * len(extra),
            out_specs=[row(D), row(D), row(D)],
            out_shape=[jax.ShapeDtypeStruct((s, D), F32), jax.ShapeDtypeStruct((s, D), F32),
                       jax.ShapeDtypeStruct((s, D), BF16)],
            input_output_aliases={6 + t: t for t in range(len(filled))},
            compiler_params=_cp(1))(pool_o, attn_o, w_out, x, g2, g3, *extra)

    if between is None or nt < 2:
        return part("outproj_fwd", 0, nt, (), None)
    first = part("outproj_fwd_a", 0, nt // 2, (), None)
    return part("outproj_fwd_b", nt // 2, nt - nt // 2, first, between(first[2]))


def _silu_parts(g):
    sg = jax.nn.sigmoid(g)
    return sg, g * sg


def _ffn_fwd(h2, wg, wu, wd, x1, target, g4):
    s = h2.shape[0]
    tm = min(TM_FFN_FWD, s)

    def body(h_ref, wg_ref, wu_ref, wd_ref, x1_ref, t_ref, g4_ref,
             gate_ref, up_ref, a_ref, df_ref, dy_ref, loss_ref, gg4_ref, f_acc):
        i = pl.program_id(0)
        j = pl.program_id(1)
        first = j == 0
        chunks = [pl.ds(r, min(FFN_ROWS, tm)) for r in range(0, tm, FFN_ROWS)]
        project = lambda rows: (_dot_nt(h_ref[rows, :], wg_ref[...]), _dot_nt(h_ref[rows, :], wu_ref[...]))
        ahead = [project(chunks[0])]
        for k, rows in enumerate(chunks):
            if k + 1 < len(chunks):
                ahead.append(project(chunks[k + 1]))
            gate, up = ahead[k]
            gate_ref[rows, :] = gate.astype(BF16)
            up_ref[rows, :] = up.astype(BF16)
            _, sl = _silu_parts(gate)
            a = (sl * up).astype(BF16)
            a_ref[rows, :] = a
            contrib = _dot(a, wd_ref[...])
            f_acc[rows, :] = jnp.where(first, contrib, f_acc[rows, :] + contrib)

        @pl.when((i == 0) & (j == 0))
        def _():
            loss_ref[...] = jnp.zeros_like(loss_ref)
            gg4_ref[...] = jnp.zeros_like(gg4_ref)

        @pl.when(j == NSH - 1)
        def _():
            r4, n4 = _rms(f_acc[...])
            g4v = g4_ref[...]
            err = x1_ref[...] + n4 * g4v - t_ref[...]
            loss_ref[...] += (0.5 / D) * jnp.sum(err * err).reshape(1, 1)
            dy = err * (1.0 / D)
            dy_ref[...] = dy
            df, dg = _rms_bwd(r4, n4, g4v, dy)
            gg4_ref[...] += dg
            df_ref[...] = df.astype(BF16)

    row = lambda w: pl.BlockSpec((tm, w), lambda i, j: (i, 0))
    sh = lambda r, c: pl.BlockSpec((None, r, c), lambda i, j: (j, 0, 0))
    act = pl.BlockSpec((None, tm, FS), lambda i, j: (j, i, 0))
    vec = pl.BlockSpec((1, D), lambda i, j: (0, 0))
    return pl.pallas_call(
        body, name="ffn_fwd", grid=(s // tm, NSH),
        in_specs=[row(D), sh(FS, D), sh(FS, D), sh(FS, D), row(D), row(D), vec],
        out_specs=[act, act, act, row(D), row(D), pl.BlockSpec((1, 1), lambda i, j: (0, 0)), vec],
        out_shape=[jax.ShapeDtypeStruct((NSH, s, FS), BF16)] * 3
        + [jax.ShapeDtypeStruct((s, D), BF16), jax.ShapeDtypeStruct((s, D), F32),
           jax.ShapeDtypeStruct((1, 1), F32), jax.ShapeDtypeStruct((1, D), F32)],
        scratch_shapes=[pltpu.VMEM((tm, D), F32)],
        compiler_params=_cp(2))(h2, wg, wu, wd, x1, target, g4)


def _ffn_bwd_act(df, gate, up, wg, wu, wd, dy, x1, mix, g3, g2):
    s = df.shape[0]
    tm = min(TM_FFN, s)

    def body(df_ref, gate_ref, up_ref, wg_ref, wu_ref, wd_ref, dy_ref, x1_ref, mix_ref, g3_ref, g2_ref,
             dgate_ref, dup_ref, dx1_ref, dmix_ref, gg3_ref, gg2_ref, acc):
        j = pl.program_id(0)
        i = pl.program_id(1)
        first = j == 0
        tile = pl.multiple_of(i * tm, tm)
        chunks = [pl.ds(r, min(FFN_ROWS, tm)) for r in range(0, tm, FFN_ROWS)]

        @pl.when((i == 0) & (j == 0))
        def _():
            gg3_ref[...] = jnp.zeros_like(gg3_ref)
            gg2_ref[...] = jnp.zeros_like(gg2_ref)

        def norms_backward(rows, dh2):
            r3, n3 = _rms(x1_ref[rows, :])
            dx1n, dg3 = _rms_bwd(r3, n3, g3_ref[...], dh2)
            dx1 = dy_ref[rows, :] + dx1n
            dx1_ref[rows, :] = dx1
            gg3_ref[...] += dg3
            r2, n2 = _rms(mix_ref[rows, :])
            dmix, dg2 = _rms_bwd(r2, n2, g2_ref[...], dx1)
            gg2_ref[...] += dg2
            dmix_ref[rows, :] = dmix.astype(BF16)

        def shard_pass(last):
            das = [_dot_nt(df_ref[chunks[0], :], wd_ref[...])]
            for k, rows in enumerate(chunks):
                if k + 1 < len(chunks):
                    das.append(_dot_nt(df_ref[chunks[k + 1], :], wd_ref[...]))
                da = das[k]
                g = gate_ref[rows, :].astype(F32)
                u = up_ref[rows, :].astype(F32)
                sg, sl = _silu_parts(g)
                dgate = (da * u * (sg * (1.0 + g * (1.0 - sg)))).astype(BF16)
                dup = (da * sl).astype(BF16)
                dgate_ref[rows, :] = dgate
                dup_ref[rows, :] = dup
                contrib = _dot(dgate, wg_ref[...]) + _dot(dup, wu_ref[...])
                acc_rows = pl.ds(tile + k * FFN_ROWS, rows.size)
                if last:
                    norms_backward(rows, acc[acc_rows, :] + contrib)
                else:
                    acc[acc_rows, :] = jnp.where(first, contrib, acc[acc_rows, :] + contrib)

        pl.when(j < NSH - 1)(functools.partial(shard_pass, False))
        pl.when(j == NSH - 1)(functools.partial(shard_pass, True))

    row = pl.BlockSpec((tm, D), lambda j, i: (i, 0))
    last = pl.BlockSpec((tm, D), lambda j, i: (i * (j // (NSH - 1)), 0))
    sh = pl.BlockSpec((None, FS, D), lambda j, i: (j, 0, 0))
    act = pl.BlockSpec((None, tm, FS), lambda j, i: (j, i, 0))
    vec = pl.BlockSpec((1, D), lambda j, i: (0, 0))
    return pl.pallas_call(
        body, name="ffn_bwd_act", grid=(NSH, s // tm),
        in_specs=[row, act, act, sh, sh, sh, last, last, last, vec, vec],
        out_specs=[act, act, last, last, vec, vec],
        out_shape=[jax.ShapeDtypeStruct((NSH, s, FS), BF16), jax.ShapeDtypeStruct((NSH, s, FS), BF16),
                   jax.ShapeDtypeStruct((s, D), F32), jax.ShapeDtypeStruct((s, D), BF16),
                   jax.ShapeDtypeStruct((1, D), F32), jax.ShapeDtypeStruct((1, D), F32)],
        scratch_shapes=[pltpu.VMEM((s, D), F32)],
        compiler_params=_cp(2))(df, gate, up, wg, wu, wd, dy, x1, mix, g3, g2)


SMEM_SPEC = pl.BlockSpec(memory_space=pltpu.SMEM)


def _emit_halves(acc_ref, row0, rows, c, own_ref, oth_ref):
    half = rows // 2
    own_ref[...] = acc_ref[pl.ds(row0 + pl.multiple_of(c * half, 8), half), :]
    oth_ref[...] = acc_ref[pl.ds(row0 + pl.multiple_of((1 - c) * half, 8), half), :].astype(BF16)


def _half_shapes(rows):
    return [jax.ShapeDtypeStruct((NSH, rows // 2, D), F32), jax.ShapeDtypeStruct((NSH, rows // 2, D), BF16)]


def _ffn_bwd_w(h2, act_a, dgate, dup, df, c_arr):
    s = h2.shape[0]
    tm = min(TM_FFN_FWD, s)
    nt = s // tm

    def body(c_ref, h_ref, a_ref, dgate_ref, dup_ref, df_ref, *rest):
        outs, accs = rest[:6], rest[6:]
        i = pl.program_id(1)

        @pl.when(i == 0)
        def _():
            for acc in accs:
                acc[...] = jnp.zeros_like(acc)

        h = h_ref[...]
        accs[0][...] += _dot_tn(dgate_ref[...], h)
        accs[1][...] += _dot_tn(dup_ref[...], h)
        accs[2][...] += _dot_tn(a_ref[...], df_ref[...])

        @pl.when(i == nt - 1)
        def _():
            for t, acc in enumerate(accs):
                _emit_halves(acc, 0, FS, c_ref[0], outs[2 * t], outs[2 * t + 1])

    row = lambda w: pl.BlockSpec((tm, w), lambda j, i: (i, 0))
    act = pl.BlockSpec((None, tm, FS), lambda j, i: (j, i, 0))
    half = pl.BlockSpec((None, FS // 2, D), lambda j, i: (j, 0, 0))
    return pl.pallas_call(
        body, name="ffn_bwd_w", grid=(NSH, nt),
        in_specs=[SMEM_SPEC, row(D), act, act, act, row(D)],
        out_specs=[half] * 6,
        out_shape=_half_shapes(FS) * 3,
        scratch_shapes=[pltpu.VMEM((FS, D), F32)] * 3,
        compiler_params=_cp(2))(c_arr, h2, act_a, dgate, dup, df)


def _outproj_bwd(dmix, w_out, pool_o, attn_o, c_arr, dep=None):
    s = dmix.shape[0]
    tm = min(TM, s)
    nt = s // tm

    def body(c_ref, dm_ref, w_ref, p_ref, a_ref, *rest):
        dpool_ref, dattn_ref, own_ref, oth_ref, gw_ref = rest[-5:]
        i = pl.program_id(0)

        @pl.when(i == 0)
        def _():
            gw_ref[...] = jnp.zeros_like(gw_ref)

        dm = dm_ref[...]
        dpool_ref[...] = _dot_nt(dm, w_ref[0:512, :])
        dattn_ref[...] = _dot_nt(dm, w_ref[512:1024, :]).astype(BF16)
        gw_ref[0:512, :] += _dot_tn(p_ref[...], dm)
        gw_ref[512:1024, :] += _dot_tn(a_ref[...], dm)

        @pl.when(i == nt - 1)
        def _():
            for k in range(NSH):
                _emit_halves(gw_ref, k * WOUT_S, WOUT_S, c_ref[0], own_ref.at[k], oth_ref.at[k])

    row = lambda w: pl.BlockSpec((tm, w), lambda i: (i, 0))
    full = pl.BlockSpec((D, D), lambda i: (0, 0))
    half = pl.BlockSpec((NSH, WOUT_S // 2, D), lambda i: (0, 0, 0))
    return pl.pallas_call(
        body, name="outproj_bwd", grid=(nt,),
        in_specs=[SMEM_SPEC, row(D), full, row(512), row(512)] + ([] if dep is None else [ANY]),
        out_specs=[row(512), row(512), half, half],
        out_shape=[jax.ShapeDtypeStruct((s, 512), F32), jax.ShapeDtypeStruct((s, 512), BF16)] + _half_shapes(WOUT_S),
        scratch_shapes=[pltpu.VMEM((D, D), F32)],
        compiler_params=_cp(1))(c_arr, dmix, w_out, pool_o, attn_o, *([] if dep is None else [dep]))


def _pool_bwd(pooled, dpool, w_pool, pool_scale, dep=None):
    s = pooled.shape[0]
    tm = min(TM_POOL, s)
    hb = tm // HALO
    nt = s // tm
    last_halo = s // HALO - 1

    def body(p_ref, dc_ref, dn_ref, wp_ref, sc_ref, *rest):
        du_ref, gwp_ref, gsc_ref = rest[-3:]
        i = pl.program_id(0)

        @pl.when(i == 0)
        def _():
            gwp_ref[...] = jnp.zeros_like(gwp_ref)
            gsc_ref[...] = jnp.zeros_like(gsc_ref)

        dcur = dc_ref[...]
        dnext = jnp.where(i < nt - 1, dn_ref[...], 0.0)
        dext = jnp.concatenate([dcur, dnext], axis=0)
        t_ext = i * tm + lax.broadcasted_iota(jnp.int32, (tm + HALO, GROUP), 0)
        for g, w in enumerate(WINDOWS):
            sl = slice(g * GROUP, (g + 1) * GROUP)
            pb = p_ref[:, sl]
            wp = wp_ref[g]
            mixed = _dot(pb, wp)
            gsc_ref[:, sl] += jnp.sum(dcur[:, sl] * mixed, axis=0, keepdims=True)
            dmixed = (dext[:, sl] * sc_ref[:, sl]).astype(BF16)
            gwp_ref[g] += _dot_tn(pb, dmixed[0:tm])
            dpooled = _dot_nt(dmixed, wp)
            z = dpooled / jnp.minimum(t_ext + 1, w).astype(F32)
            du_ref[:, sl] = (_window_sums(z, w, -1, tm, 2) - dpooled[0:tm]).astype(BF16)

    return pl.pallas_call(
        body, name="pool_bwd", grid=(nt,),
        in_specs=[pl.BlockSpec((tm, 512), lambda i: (i, 0)),
                  pl.BlockSpec((tm, 512), lambda i: (i, 0)),
                  pl.BlockSpec((HALO, 512), lambda i: (jnp.minimum((i + 1) * hb, last_halo), 0)),
                  pl.BlockSpec((4, GROUP, GROUP), lambda i: (0, 0, 0)),
                  pl.BlockSpec((1, 512), lambda i: (0, 0))] + ([] if dep is None else [ANY]),
        out_specs=[pl.BlockSpec((tm, 512), lambda i: (i, 0)),
                   pl.BlockSpec((4, GROUP, GROUP), lambda i: (0, 0, 0)),
                   pl.BlockSpec((1, 512), lambda i: (0, 0))],
        out_shape=[jax.ShapeDtypeStruct((s, 512), BF16), jax.ShapeDtypeStruct((4, GROUP, GROUP), F32),
                   jax.ShapeDtypeStruct((1, 512), F32)],
        compiler_params=_cp(1))(pooled, dpool, dpool, w_pool, pool_scale, *([] if dep is None else [dep]))


def _attn_bwd(q, k, v, do, probs, sink_share, bucket, dep=None):
    s = q.shape[0]
    nblk = s // BLK

    def body(q_ref, do_ref, k_ref, v_ref, prob_ref, ps_ref, bk_ref, *rest):
        dq_ref, dk_ref, dv_ref, grb_ref, gsk_ref, dss_ref = rest[-6:]
        n = pl.program_id(0)

        @pl.when(n == 0)
        def _():
            dk_ref[...] = jnp.zeros_like(dk_ref)
            dv_ref[...] = jnp.zeros_like(dv_ref)
            dss_ref[...] = jnp.zeros_like(dss_ref)
            gsk_ref[...] = jnp.zeros_like(gsk_ref)

        kt, (lo, pstart, cstart) = _kv_band(k_ref, n, True)
        vv, _ = _kv_band(v_ref, n, False)
        lo_q = lax.broadcasted_iota(jnp.int32, (BLK, 128), 1) < HD
        dkk = [jnp.zeros((2 * BLK, 128), F32), jnp.zeros((2 * BLK, 128), F32)]
        dvv = [jnp.zeros((2 * BLK, 128), F32), jnp.zeros((2 * BLK, 128), F32)]

        def by_head(t):
            return jnp.concatenate([jnp.where(lo_q, t, 0.0), jnp.where(lo_q, 0.0, t)], axis=0).astype(BF16)

        for p in range(4):
            h = p // 2
            qt = q_ref[:, p * 128:(p + 1) * 128].astype(F32) * SCALE
            dot = do_ref[:, p * 128:(p + 1) * 128]
            pb = prob_ref[pl.ds(2 * p, 2)]
            prob = pb.astype(F32)
            ps = ps_ref[pl.ds(2 * p, 2), :].reshape(2, 1, BLK)
            dp = _dot_nt(vv[h], dot).reshape(2, 2 * BLK, BLK)
            delta = jnp.sum(prob * dp, axis=1, keepdims=True)
            ds = prob * (dp - delta)
            sink_part = ps * delta
            for e in range(2):
                gs = -jnp.sum(sink_part[e]).reshape(1, 1)
                gsk_ref[pl.ds(2 * p + e, 1), :] += jnp.broadcast_to(gs, (1, 128))
            dss_ref[pl.ds(2 * p, 2)] += ds
            dsb = ds.astype(BF16)
            dq_t = _dot(kt[h], dsb.reshape(4 * BLK, BLK))
            dq_ref[:, p * 128:(p + 1) * 128] = (dq_t.T * SCALE).astype(BF16)
            dkk[h] = dkk[h] + _dot(jnp.concatenate([dsb[0], dsb[1]], axis=1), by_head(qt))
            dvv[h] = dvv[h] + _dot(jnp.concatenate([pb[0], pb[1]], axis=1), by_head(dot.astype(F32)))
        for acc, ref in ((dkk, dk_ref), (dvv, dv_ref)):
            f0 = acc[0] + pltpu.roll(acc[0], HD, axis=1)
            f1 = acc[1] + pltpu.roll(acc[1], HD, axis=1)
            band = jnp.where(lo, f0, f1)
            ref[pl.ds(pstart, BLK), :] += band[0:BLK]
            ref[pl.ds(cstart, BLK), :] += band[BLK:2 * BLK]

        @pl.when(n == nblk - 1)
        def _():
            _relbias_grad(dss_ref, bk_ref[...], grb_ref)

    blk = pl.BlockSpec((BLK, 512), lambda i: (i, 0))
    kv = pl.BlockSpec((s, 128), lambda i: (0, 0))
    row8 = pl.BlockSpec((NQ, 128), lambda i: (0, 0))
    return pl.pallas_call(
        body, name="attn_bwd", grid=(nblk,),
        in_specs=[blk, blk, kv, kv, pl.BlockSpec((None, NQ, 2 * BLK, BLK), lambda i: (i, 0, 0, 0)),
                  pl.BlockSpec((None, NQ, BLK), lambda i: (i, 0, 0)), pl.BlockSpec((2 * BLK, BLK), lambda i: (0, 0))]
        + ([] if dep is None else [ANY]),
        out_specs=[blk, kv, kv, row8, row8],
        out_shape=[jax.ShapeDtypeStruct((s, 512), BF16), jax.ShapeDtypeStruct((s, 128), F32),
                   jax.ShapeDtypeStruct((s, 128), F32), jax.ShapeDtypeStruct((NQ, 128), F32),
                   jax.ShapeDtypeStruct((NQ, 128), F32)],
        scratch_shapes=[pltpu.VMEM((NQ, 2 * BLK, BLK), F32)],
        compiler_params=_cp(1))(q, do, k, v, probs, sink_share, bucket, *([] if dep is None else [dep]))


def _relbias_grad(ds_ref, bucket_v, o_ref):
    lane = lax.broadcasted_iota(jnp.int32, (NQ, 128), 1)
    head_row = lax.broadcasted_iota(jnp.int32, (NQ, BLK), 0)
    acc = jnp.zeros((NQ, 128), F32)
    for b in range(NBUCKET):
        sel = bucket_v == b
        stack = jnp.zeros((NQ, BLK), F32)
        for h in range(NQ):
            col_sums = jnp.sum(jnp.where(sel, ds_ref[h], 0.0), axis=0, keepdims=True)
            stack = jnp.where(head_row == h, col_sums, stack)
        acc = acc + jnp.where(lane == b, jnp.sum(stack, axis=1, keepdims=True), 0.0)
    o_ref[...] = acc


def _inproj_bwd(x, g1, du, dq, dk, dv, w_in, dx1, c_arr):
    s = x.shape[0]
    tm = min(TM, s)
    nt = s // tm
    cols = ((0, 512), (512, 1024), (1024, 1152), (1152, 1280))

    def body(c_ref, x_ref, g_ref, du_ref, dq_ref, dk_ref, dv_ref, w_ref, dx1_ref,
             gx_ref, own_ref, oth_ref, gg_ref, gw_ref):
        i = pl.program_id(0)

        @pl.when(i == 0)
        def _():
            gw_ref[...] = jnp.zeros_like(gw_ref)
            gg_ref[...] = jnp.zeros_like(gg_ref)

        parts = (du_ref[...], dq_ref[...], dk_ref[...].astype(BF16), dv_ref[...].astype(BF16))
        dh = None
        for (a, b), dpart in zip(cols, parts):
            t = _dot(dpart, w_ref[a:b, :])
            dh = t if dh is None else dh + t
        gv = g_ref[...]
        r1, n1 = _rms(x_ref[...])
        h = (n1 * gv).astype(BF16)
        for (a, b), dpart in zip(cols, parts):
            gw_ref[a:b, :] += _dot_tn(dpart, h)
        dx, dg = _rms_bwd(r1, n1, gv, dh)
        gg_ref[...] += dg
        gx_ref[...] = dx1_ref[...] + dx

        @pl.when(i == nt - 1)
        def _():
            for k in range(NSH):
                _emit_halves(gw_ref, k * WIN_S, WIN_S, c_ref[0], own_ref.at[k], oth_ref.at[k])

    row = lambda w: pl.BlockSpec((tm, w), lambda i: (i, 0))
    vec = pl.BlockSpec((1, D), lambda i: (0, 0))
    full = pl.BlockSpec((IN_W, D), lambda i: (0, 0))
    half = pl.BlockSpec((NSH, WIN_S // 2, D), lambda i: (0, 0, 0))
    return pl.pallas_call(
        body, name="inproj_bwd", grid=(nt,),
        in_specs=[SMEM_SPEC, row(D), vec, row(512), row(512), row(128), row(128), full, row(D)],
        out_specs=[row(D), half, half, vec],
        out_shape=[jax.ShapeDtypeStruct((s, D), F32)] + _half_shapes(WIN_S) + [jax.ShapeDtypeStruct((1, D), F32)],
        scratch_shapes=[pltpu.VMEM((IN_W, D), F32)],
        compiler_params=_cp(1))(c_arr, x, g1, du, dq, dk, dv, w_in, dx1)


def _local_step(x, target, small, w_in, w_out, ffn_weights, c_arr, on_ffn_grads=None, on_wout_grads=None,
                mid_outproj=None):
    g1, g2, g3, g4, w_pool, pool_scale, rel_bias, sinks = small
    bucket = jnp.asarray(_bucket_table())
    wp_bf = w_pool.astype(BF16)
    bias = _bias_table(bucket, rel_bias)
    h1 = _prenorm(x, g1)
    if callable(w_in):
        w_in = w_in(bias, h1)
    u, q, k, v = _inproj_fwd(h1, w_in)
    pool_o, pooled = _pool_fwd(u, wp_bf, pool_scale)
    attn_o, probs, sink_share = _attn_fwd(q, k, v, bias, sinks)
    g2_fwd = g2
    if callable(w_out):
        w_out, after = w_out(pool_o, attn_o)
        if after is not None:
            g2_fwd = g2 + after[:1, :1]
    mix, x1, h2 = _outproj_fwd(pool_o, attn_o, w_out, x, g2_fwd, g3, mid_outproj)
    wg, wu, wd = ffn_weights(h2) if callable(ffn_weights) else ffn_weights
    gate, up, act_a, df, dy, loss, gg4 = _ffn_fwd(h2, wg, wu, wd, x1, target, g4)
    dgate, dup, dx1, dmix, gg3, gg2 = _ffn_bwd_act(df, gate, up, wg, wu, wd, dy, x1, mix, g3, g2)
    ffn_g = _ffn_bwd_w(h2, act_a, dgate, dup, df, c_arr)
    dep = None if on_ffn_grads is None else on_ffn_grads(ffn_g)
    dpool, dattn, wout_own, wout_oth = _outproj_bwd(dmix, w_out, pool_o, attn_o, c_arr, dep)
    dep = None if on_wout_grads is None else on_wout_grads(wout_own, wout_oth, dpool)
    du, gwp, gsc = _pool_bwd(pooled, dpool, wp_bf, pool_scale, dep)
    dq, dk, dv, grb, gsk = _attn_bwd(q, k, v, dattn, probs, sink_share, bucket, dep)
    gx, win_own, win_oth, gg1 = _inproj_bwd(x, g1, du, dq, dk, dv, w_in, dx1, c_arr)
    small_grads = (gg1, gg2, gg3, gg4, gwp, gsc, grb, gsk[:, 0].reshape(1, NQ))
    return loss, gx, small_grads, ((win_own, win_oth), (wout_own, wout_oth), ffn_g)


def _place():
    x, y, c = lax.axis_index("x"), lax.axis_index("y"), lax.axis_index("c")
    chips = ((1 - x, y), (x, 1 - y), (1 - x, 1 - y))
    return x, y, c, chips


def _remote(src, dst, ssem, rsem, dev):
    return pltpu.make_async_remote_copy(src_ref=src, dst_ref=dst, send_sem=ssem, recv_sem=rsem,
                                        device_id=dev, device_id_type=MESH_T)


def _to_sibling(arrs, name, after=()):
    nt, na = len(arrs), len(after)

    def body(*refs):
        srcs, dsts = refs[:nt], refs[nt + na:2 * nt + na]
        ssem, rsem = refs[2 * nt + na:]
        x, y, c, _ = _place()
        cps = [_remote(srcs[t], dsts[t], ssem.at[t], rsem.at[t], (x, y, 1 - c)) for t in range(nt)]
        for cp in cps:
            cp.start()
        for cp in cps:
            cp.wait()

    return pl.pallas_call(
        body, name=name, in_specs=[ANY] * (nt + na), out_specs=[ANY] * nt,
        out_shape=[jax.ShapeDtypeStruct(a.shape, a.dtype) for a in arrs],
        scratch_shapes=[pltpu.SemaphoreType.DMA((nt,)), pltpu.SemaphoreType.DMA((nt,))],
        compiler_params=_cp())(*arrs, *after)


HBM_SPEC = pl.BlockSpec(memory_space=pltpu.HBM)
SEM_SPEC = pl.BlockSpec(memory_space=pltpu.SEMAPHORE)
DATAFLOW = pltpu.SideEffectType.DATAFLOW_SIDE_EFFECTING


def _gather_copies(srcs, lands, ssem, rsem):
    x, y, c, chips = _place()
    me = 2 * x + y
    out = []
    for t in range(len(srcs)):
        half = srcs[t].shape[0] // 2
        mine = pl.ds(pl.multiple_of(c * half, 16), half)
        for j, (px, py) in enumerate(chips):
            k = 3 * t + j
            send = _remote(srcs[t].at[mine], lands[t].at[me, mine], ssem.at[k], rsem.at[k], (px, py, c))
            blk = lands[t].at[2 * px + py, mine]
            out.append((send, _remote(blk, blk, ssem.at[k], rsem.at[k], (px, py, c))))
    return out


def _scatter_copies(srcs, lands, ssem, rsem):
    x, y, c, chips = _place()
    out = []
    for t in range(len(srcs)):
        for j, (px, py) in enumerate(chips):
            k = 3 * t + j
            send = _remote(srcs[t].at[2 * px + py], lands[t].at[j], ssem.at[k], rsem.at[k], (px, py, c))
            out.append((send, _remote(lands[t].at[j], lands[t].at[j], ssem.at[k], rsem.at[k], (px, py, c))))
    return out


def _sibling_copies(srcs, lands, ssem, rsem):
    x, y, c, _ = _place()
    sib = (x, y, 1 - c)
    return [(_remote(srcs[t], lands[t], ssem.at[t], rsem.at[t], sib),
             _remote(lands[t], lands[t], ssem.at[t], rsem.at[t], sib)) for t in range(len(srcs))]


def _peer_copies(srcs, lands, ssem, rsem):
    x, y, c, _ = _place()
    me = 4 * x + 2 * y + c
    out = []
    for kk in range(1, 8):
        px, py, pc = x ^ (kk >> 2), y ^ ((kk >> 1) & 1), c ^ (kk & 1)
        send = _remote(srcs[0], lands[0].at[me], ssem.at[kk - 1], rsem.at[kk - 1], (px, py, pc))
        slot = lands[0].at[4 * px + 2 * py + pc]
        out.append((send, _remote(slot, slot, ssem.at[kk - 1], rsem.at[kk - 1], (px, py, pc))))
    return out


def _forward_copies(srcs, lands, ssem, rsem):
    x, y, c, chips = _place()
    sib = (x, y, 1 - c)
    out = []
    for t in range(len(lands)):
        half = lands[t].shape[1] // 2
        mine = pl.ds(pl.multiple_of(c * half, 16), half)
        other = pl.ds(pl.multiple_of((1 - c) * half, 16), half)
        for j, (px, py) in enumerate(chips):
            k = 3 * t + j
            blk_m, blk_o = lands[t].at[2 * px + py, mine], lands[t].at[2 * px + py, other]
            out.append((_remote(blk_m, blk_m, ssem.at[k], rsem.at[k], sib),
                        _remote(blk_o, blk_o, ssem.at[k], rsem.at[k], sib)))
    return out


def _win_and_small_copies(srcs, lands, ssem, rsem):
    return (_scatter_copies(srcs[:1], lands[:1], ssem, rsem)
            + _peer_copies(srcs[1:], lands[1:], ssem.at[pl.ds(3, 7)], rsem.at[pl.ds(3, 7)]))


def _split_start(plan, ncopy, srcs, lands, after, name):
    ns, nbuf = len(srcs), len(srcs) + len(lands)
    extra = [] if after is None else [after]
    n_in = nbuf + len(extra)

    def body(*refs):
        ssem, rsem, token = refs[n_in], refs[n_in + 1], refs[-1]
        for send, _ in plan(refs[:ns], refs[ns:nbuf], ssem, rsem):
            send.start()
        token[...] = jnp.zeros_like(token)

    bufs = [pltpu.with_memory_space_constraint(a, pltpu.HBM) for a in list(srcs) + list(lands)]
    outs = pl.pallas_call(
        body, name=name, in_specs=[HBM_SPEC] * nbuf + [ANY] * len(extra),
        out_specs=[SEM_SPEC, SEM_SPEC] + [HBM_SPEC] * nbuf + [pl.BlockSpec(memory_space=pltpu.VMEM)],
        out_shape=[pltpu.SemaphoreType.DMA((ncopy,)), pltpu.SemaphoreType.DMA((ncopy,))]
        + [pltpu.HBM(a.shape, a.dtype) for a in bufs] + [jax.ShapeDtypeStruct((8, 128), F32)],
        input_output_aliases={i: 2 + i for i in range(nbuf)},
        compiler_params=pltpu.CompilerParams(has_side_effects=DATAFLOW))(*bufs, *extra)
    return outs[0], outs[1], outs[2:2 + ns], outs[2 + ns:2 + nbuf], outs[-1]


def _split_wait(plan, ssem, rsem, srcs, lands, after, name, only=None):
    ns, nbuf = len(srcs), len(srcs) + len(lands)

    def body(*refs):
        s_ref, r_ref = refs[nbuf], refs[nbuf + 1]
        for k, (send, recv) in enumerate(plan(refs[:ns], refs[ns:nbuf], s_ref, r_ref)):
            if only is None or k in only:
                send.wait_send()
                recv.wait_recv()

    outs = pl.pallas_call(
        body, name=name, in_specs=[HBM_SPEC] * nbuf + [SEM_SPEC, SEM_SPEC] + [ANY] * len(after),
        out_specs=[HBM_SPEC] * nbuf,
        out_shape=[pltpu.HBM(a.shape, a.dtype) for a in list(srcs) + list(lands)],
        input_output_aliases={i: i for i in range(nbuf)},
        compiler_params=pltpu.CompilerParams(has_side_effects=DATAFLOW))(*srcs, *lands, ssem, rsem, *after)
    return outs


def _gather_finish(lands, tag):
    nt = len(lands)

    def body(*refs):
        outs = refs[nt:2 * nt]
        dsend, drecv = refs[2 * nt:]
        x, y, c, chips = _place()
        sib = (x, y, 1 - c)
        sends = []
        for t in range(nt):
            half = outs[t].shape[1] // 2
            mine = pl.ds(pl.multiple_of(c * half, 16), half)
            for j, (px, py) in enumerate(chips):
                blk = outs[t].at[2 * px + py, mine]
                cp = _remote(blk, blk, dsend.at[t, j], drecv.at[t, j], sib)
                cp.start()
                sends.append(cp)
        for t in range(nt):
            half = outs[t].shape[1] // 2
            other = pl.ds(pl.multiple_of((1 - c) * half, 16), half)
            for j, (px, py) in enumerate(chips):
                blk = outs[t].at[2 * px + py, other]
                _remote(blk, blk, dsend.at[t, j], drecv.at[t, j], sib).wait_recv()
        for cp in sends:
            cp.wait_send()

    return pl.pallas_call(
        body, name="gather_finish_" + tag, in_specs=[ANY] * nt, out_specs=[ANY] * nt,
        out_shape=[jax.ShapeDtypeStruct(a.shape, a.dtype) for a in lands],
        input_output_aliases={t: t for t in range(nt)},
        scratch_shapes=[pltpu.SemaphoreType.DMA((nt, 3)), pltpu.SemaphoreType.DMA((nt, 3))],
        compiler_params=_cp())(*lands)


def _chip_partial(owns, recv, chip_arr, tag):
    nt = len(owns)

    def body(chip_ref, *refs):
        k = pl.program_id(0)
        for t in range(nt):
            p = refs[t][...] + refs[nt + t][...].astype(F32)
            refs[2 * nt + t][...] = p.astype(BF16)

            @pl.when(k == chip_ref[0])
            def _():
                refs[3 * nt + t][...] = p

    blk_specs = [pl.BlockSpec((None,) + a.shape[1:], lambda k, chip_ref: (k, 0, 0)) for a in owns]
    own_specs = [pl.BlockSpec(a.shape[1:], lambda k, chip_ref: (0, 0)) for a in owns]
    return pl.pallas_call(
        body, name="rs_chip_partial_" + tag,
        grid_spec=pltpu.PrefetchScalarGridSpec(num_scalar_prefetch=1, grid=(NSH,), in_specs=blk_specs * 2,
                                               out_specs=blk_specs + own_specs),
        out_shape=[jax.ShapeDtypeStruct(a.shape, BF16) for a in owns]
        + [jax.ShapeDtypeStruct(a.shape[1:], F32) for a in owns],
        compiler_params=_cp(1))(chip_arr, *owns, *recv)


def _sum_chips(own, recv, tag):
    nt = len(own)

    def body(*refs):
        outs = refs[2 * nt:]
        for t in range(nt):
            r = refs[nt + t]
            outs[t][...] = ((refs[t][...] + r[0].astype(F32)) + r[1].astype(F32)) + r[2].astype(F32)

    vm = pl.BlockSpec(memory_space=pltpu.VMEM)
    return pl.pallas_call(
        body, name="rs_sum_chips_" + tag, in_specs=[vm] * (2 * nt), out_specs=[vm] * nt,
        out_shape=[jax.ShapeDtypeStruct(a.shape, F32) for a in own],
        compiler_params=_cp())(*own, *recv)


def _sum_chips_shared(own, recv, tag):
    def body(own_ref, recv_ref, out_ref, sib_ref, ssem, rsem):
        out_ref[...] = ((own_ref[...] + recv_ref[0].astype(F32)) + recv_ref[1].astype(F32)) + recv_ref[2].astype(F32)
        x, y, c, _ = _place()
        cp = _remote(out_ref, sib_ref, ssem.at[0], rsem.at[0], (x, y, 1 - c))
        cp.start()
        cp.wait()

    vm = pl.BlockSpec(memory_space=pltpu.VMEM)
    return pl.pallas_call(
        body, name="rs_sum_share_" + tag, in_specs=[vm, vm], out_specs=[vm, ANY],
        out_shape=[jax.ShapeDtypeStruct(own.shape, F32)] * 2,
        scratch_shapes=[pltpu.SemaphoreType.DMA((1,)), pltpu.SemaphoreType.DMA((1,))],
        compiler_params=_cp())(own, recv)


def _adamw_math(w, g, m, v):
    m = ADAM_B1 * m + (1.0 - ADAM_B1) * g
    v = ADAM_B2 * v + (1.0 - ADAM_B2) * (g * g)
    m_hat = m / (1.0 - ADAM_B1 ** ADAM_STEP)
    v_hat = v / (1.0 - ADAM_B2 ** ADAM_STEP)
    delta = -ADAM_LR * (m_hat / (jnp.sqrt(v_hat) + ADAM_EPS) + ADAM_WD * w)
    return delta, m, v


ADAM_SPLIT = 4


def _adamw_shards(own, sib, ws, ms, vs, c_arr, tag):
    nt = len(own)

    def body(c_ref, *refs):
        hh = pl.program_id(0)
        mine = hh == c_ref[0]
        for t in range(nt):
            g = jnp.where(mine, refs[t][...], refs[nt + t][...])
            delta, m, v = _adamw_math(refs[2 * nt + t][...], g, refs[3 * nt + t][...], refs[4 * nt + t][...])
            refs[5 * nt + t][...] = g
            refs[6 * nt + t][...] = delta
            refs[7 * nt + t][...] = m
            refs[8 * nt + t][...] = v

    def tile(a):
        return (a.shape[0] // ADAM_SPLIT, a.shape[1])

    def half_index(used_when_mine):
        def index(hh, q, c_ref):
            mine = 1 - (hh - c_ref[0]) * (hh - c_ref[0])
            used = mine if used_when_mine else 1 - mine
            return (used * q + (1 - used) * hh * (ADAM_SPLIT - 1), 0)
        return index

    own_specs = [pl.BlockSpec(tile(a), half_index(True)) for a in own]
    sib_specs = [pl.BlockSpec(tile(a), half_index(False)) for a in own]
    full_specs = [pl.BlockSpec(tile(a), lambda hh, q, c_ref: (hh * ADAM_SPLIT + q, 0)) for a in own]
    return pl.pallas_call(
        body, name="adamw_shards_" + tag,
        grid_spec=pltpu.PrefetchScalarGridSpec(num_scalar_prefetch=1, grid=(2, ADAM_SPLIT),
                                               in_specs=own_specs + sib_specs + full_specs * 3,
                                               out_specs=full_specs * 4),
        out_shape=[jax.ShapeDtypeStruct(w.shape, F32) for w in ws] * 4,
        compiler_params=_cp(2))(c_arr, *own, *sib, *ws, *ms, *vs)


SMALL_WPOOL_ROW = 32
SMALL_SCALE_ROW = SMALL_WPOOL_ROW + 4 * GROUP
SMALL_BIAS_ROW = SMALL_SCALE_ROW + 8
SMALL_SINKS_ROW = SMALL_BIAS_ROW + NQ
SMALL_LOSS_ROW = SMALL_SINKS_ROW + 8


def _small_sum_adamw(gathered, wp, mp, vp):
    rows = wp.shape[0]

    def body(g_ref, w_ref, m_ref, v_ref, *rest):
        outs, res = rest[:-1], rest[-1]
        g = g_ref[0]
        for d in range(1, 8):
            g = g + g_ref[d]
        delta, m, v = _adamw_math(w_ref[...], g, m_ref[...], v_ref[...])
        for kind, val in enumerate((g, delta, m, v)):
            res[kind] = val
            gains = outs[8 * kind:8 * kind + 4]
            w_pool_o, scale_o, bias_o, sinks_o = outs[8 * kind + 4:8 * kind + 8]
            for t in range(4):
                for i in range(8):
                    gains[t][:, 128 * i:128 * (i + 1)] = res[kind, pl.ds(8 * t + i, 1), :]
            for grp in range(4):
                w_pool_o[grp] = res[kind, pl.ds(SMALL_WPOOL_ROW + GROUP * grp, GROUP), :]
            for i in range(4):
                scale_o[:, 128 * i:128 * (i + 1)] = res[kind, pl.ds(SMALL_SCALE_ROW + i, 1), :]
            bias_o[...] = res[kind, pl.ds(SMALL_BIAS_ROW, NQ), :][:, 0:NBUCKET]
            sinks_o[...] = res[kind, pl.ds(SMALL_SINKS_ROW, 1), :][:, 0:NQ]
        outs[-1][...] = res[0, pl.ds(SMALL_LOSS_ROW, 1), :]

    vm = pl.BlockSpec(memory_space=pltpu.VMEM)
    one_kind = [jax.ShapeDtypeStruct((1, D), F32)] * 4 + [
        jax.ShapeDtypeStruct((4, GROUP, GROUP), F32), jax.ShapeDtypeStruct((1, POOL_W), F32),
        jax.ShapeDtypeStruct((NQ, NBUCKET), F32), jax.ShapeDtypeStruct((1, NQ), F32)]
    return pl.pallas_call(
        body, name="small_sum_adamw", in_specs=[vm] * 4, out_specs=[vm] * 33,
        out_shape=one_kind * 4 + [jax.ShapeDtypeStruct((1, 128), F32)],
        scratch_shapes=[pltpu.VMEM((4, rows, 128), F32)],
        compiler_params=_cp())(gathered, wp, mp, vp)


def _pack_small(gains4, w_pool, pool_scale, rel_bias_t, sinks, loss):
    bias_rows = jnp.pad(rel_bias_t, ((0, 0), (0, 128 - rel_bias_t.shape[1])))
    parts = list(gains4) + [w_pool, pool_scale, bias_rows, sinks, loss]
    rows = []
    for a in parts:
        flat = a.reshape(-1)
        n = flat.shape[0]
        padded = -(-n // 1024) * 1024
        rows.append(jnp.pad(flat, (0, padded - n)).reshape(-1, 128))
    return jnp.concatenate(rows, axis=0)


def kernel(x, g_pre_mix, w_in, w_pool, pool_scale, rel_bias, sinks, w_out, g_post_mix, g_pre_ffn, w_gate, w_up, w_down, g_post_ffn, loss_target, m_g_pre_mix, m_w_in, m_w_pool, m_pool_scale, m_rel_bias, m_sinks, m_w_out, m_g_post_mix, m_g_pre_ffn, m_w_gate, m_w_up, m_w_down, m_g_post_ffn, v_g_pre_mix, v_w_in, v_w_pool, v_pool_scale, v_rel_bias, v_sinks, v_w_out, v_g_post_mix, v_g_pre_ffn, v_w_gate, v_w_up, v_w_down, v_g_post_ffn):
    c = lax.axis_index("c")
    chip = 2 * lax.axis_index("x") + lax.axis_index("y")

    def shards(a_in, a_out, a_gate, a_up, a_down):
        return (a_in[0].T, a_out[0], a_gate[0].T, a_up[0].T, a_down[0])

    c_arr = c.reshape(1).astype(jnp.int32)
    chip_arr = chip.reshape(1).astype(jnp.int32)

    big_w = shards(w_in, w_out, w_gate, w_up, w_down)
    big_bf = [w.astype(BF16) for w in big_w]
    g_ssem, g_rsem, g_srcs, g_lands, g_token = _split_start(
        _gather_copies, 15, big_bf, [lax.empty((NSH,) + a.shape, BF16) for a in big_bf], None, "gather_start")
    fw = {}

    def with_own(land, shard):
        return lax.dynamic_update_slice(land, shard[None], (chip, 0, 0))

    def w_in_ready(*after):
        fw["first"] = _split_wait(_gather_copies, g_ssem, g_rsem, g_srcs, g_lands, after, "gather_win_wait",
                                  only=(0, 1, 2))
        (fw["win"],) = _gather_finish([with_own(fw["first"][5], fw["first"][0])], "win")
        return fw["win"].reshape(IN_W, D)

    def w_out_ready(*after):
        first = fw["first"]
        fw["second"] = _split_wait(_gather_copies, g_ssem, g_rsem, first[:5], [fw["win"]] + list(first[6:]), after,
                                   "gather_wout_wait", only=(3, 4, 5))
        (fw["wout"],) = _gather_finish([with_own(fw["second"][6], fw["second"][1])], "wout")
        return fw["wout"].reshape(D, D), None

    def ffn_forward_start(after):
        second = fw["second"]
        outs = _split_wait(_gather_copies, g_ssem, g_rsem, second[:5], [second[5], fw["wout"]] + list(second[7:]),
                           [after], "gather_ffn_wait", only=tuple(range(6, 15)))
        lands = [with_own(land, src) for src, land in zip(outs[2:5], outs[7:])]
        fw["f"] = _split_start(_forward_copies, 9, [], lands, None, "gather_forward_start")
        return fw["f"][4]

    def ffn_weights(after):
        ssem, rsem, _, lands, _ = fw["f"]
        return _split_wait(_forward_copies, ssem, rsem, [], lands, [after], "gather_forward_wait")

    rs = {}

    def on_ffn_grads(ffn_g):
        oths = ffn_g[1::2]
        rs["ffn_own"] = ffn_g[0::2]
        rs["x"] = _split_start(_sibling_copies, 3, oths, [lax.empty(a.shape, BF16) for a in oths], ffn_g[0],
                               "rs_exchange_ffn_start")
        return rs["x"][4]

    def on_wout_grads(own, oth, after):
        ssem, rsem, srcs, lands, _ = rs["x"]
        recv_ffn = _split_wait(_sibling_copies, ssem, rsem, srcs, lands, [after], "rs_exchange_ffn_wait")[3:]
        recv_wout = _to_sibling([oth], "rs_exchange_wout")
        outs = _chip_partial([own] + list(rs["ffn_own"]), list(recv_wout) + list(recv_ffn), chip_arr, "early")
        rs["early_own"] = outs[4:]
        rs["s"] = _split_start(_scatter_copies, 12, outs[:4],
                               [lax.empty((3,) + a.shape[1:], BF16) for a in outs[:4]], outs[4], "scatter_early_start")
        return rs["s"][4]

    small = (g_pre_mix + g_token[:1, :1], g_post_mix, g_pre_ffn, g_post_ffn, w_pool[0], pool_scale, rel_bias, sinks)
    loss, gx, small_grads, ((win_own, win_oth), _, _) = _local_step(
        x[0], loss_target[0], small, w_in_ready, w_out_ready, ffn_weights, c_arr, on_ffn_grads, on_wout_grads,
        ffn_forward_start)

    ssem, rsem, srcs, lands, _ = rs["s"]
    recv_early = _split_wait(_scatter_copies, ssem, rsem, srcs, lands, [gx], "scatter_early_wait")[4:]
    finals = _sum_chips(list(rs["early_own"]), list(recv_early), "early")
    to_sib = [win_oth] + list(finals)
    sh_ssem, sh_rsem, sh_srcs, sh_lands, _ = _split_start(
        _sibling_copies, 5, to_sib, [lax.empty(a.shape, a.dtype) for a in to_sib], None, "rs_share_start")

    unused = jnp.zeros((1, 1), F32)
    gp = _pack_small(small_grads[:4], *small_grads[4:], loss)
    wp = _pack_small((g_pre_mix, g_post_mix, g_pre_ffn, g_post_ffn), w_pool, pool_scale, rel_bias.T, sinks, unused)
    mp = _pack_small((m_g_pre_mix, m_g_post_mix, m_g_pre_ffn, m_g_post_ffn), m_w_pool, m_pool_scale, m_rel_bias.T,
                     m_sinks, unused)
    vp = _pack_small((v_g_pre_mix, v_g_post_mix, v_g_pre_ffn, v_g_post_ffn), v_w_pool, v_pool_scale, v_rel_bias.T,
                     v_sinks, unused)

    moments = (shards(m_w_in, m_w_out, m_w_gate, m_w_up, m_w_down),
               shards(v_w_in, v_w_out, v_w_gate, v_w_up, v_w_down))
    sh_first = _split_wait(_sibling_copies, sh_ssem, sh_rsem, sh_srcs, sh_lands, [], "rs_share_win_wait", only=(0,))
    outs = _chip_partial([win_own], [sh_first[5]], chip_arr, "win")
    slots = lax.dynamic_update_slice(lax.empty((8,) + gp.shape, F32), gp[None], (2 * chip + c, 0, 0))
    w_ssem, w_rsem, w_srcs, w_lands, w_token = _split_start(
        _win_and_small_copies, 10, [outs[0], gp], [lax.empty((3,) + outs[0].shape[1:], BF16), slots], gx,
        "scatter_win_start")
    shared = _split_wait(_sibling_copies, sh_ssem, sh_rsem, sh_first[:5], sh_first[5:], [w_token],
                         "rs_share_early_wait", only=(1, 2, 3, 4))
    adam_e = _adamw_shards(shared[1:5], shared[6:], big_w[1:], moments[0][1:], moments[1][1:], c_arr, "early")
    w_first = _split_wait(_win_and_small_copies, w_ssem, w_rsem, w_srcs, w_lands, [adam_e[0]], "scatter_win_wait",
                          only=(0, 1, 2))
    win_final, win_sib = _sum_chips_shared(outs[1], w_first[2], "win")
    adam_w = _adamw_shards([win_final], [win_sib], big_w[:1], moments[0][:1], moments[1][:1], c_arr, "win")
    big_g, big_d, big_m, big_v = [[adam_w[i]] + list(adam_e[4 * i:4 * i + 4]) for i in range(4)]

    gathered = _split_wait(_win_and_small_copies, w_ssem, w_rsem, w_first[:2], w_first[2:], [adam_w[0]],
                           "small_allgather_wait", only=tuple(range(3, 10)))[3]
    flat = _small_sum_adamw(gathered, wp, mp, vp)
    small_out = [flat[8 * kind:8 * kind + 8] for kind in range(4)]
    total_loss = flat[-1][0, 0]

    def ordered(small8, big4):
        sg1, sg2, sg3, sg4, swp, ssc, srb, ssk = small8
        swp, srb = swp[None], srb.T
        bwin, bwout, bwg, bwu, bwd = big4[0].T[None], big4[1][None], big4[2].T[None], big4[3].T[None], big4[4][None]
        return [sg1, bwin, swp, ssc, srb, ssk, bwout, sg2, sg3, bwg, bwu, bwd, sg4]

    res = [total_loss, gx[None]]
    for sm, bg in zip(small_out, (big_g, big_d, big_m, big_v)):
        res += ordered(sm, bg)
    return tuple(res)
```

```python
import functools

import numpy as np
import jax
import jax.numpy as jnp
from jax import lax
from jax.experimental import pallas as pl
from jax.experimental.pallas import tpu as pltpu

F32 = jnp.float32
BF16 = jnp.bfloat16

D = 1024
POOL_W = 512
GROUP = 128
WINDOWS = (2, 4, 8, 16)
HALO = 128
NQ = 8
BLK = 128
NBUCKET = 32
IN_W = 1280
DFF = 2816
NSH = 4
FS = DFF // NSH
WIN_S = IN_W // NSH
WOUT_S = D // NSH
EPS = 1e-6
NEG = -1e30
SCALE = 0.125

ADAM_LR = 0.001
ADAM_B1 = 0.9
ADAM_B2 = 0.999
ADAM_EPS = 1e-08
ADAM_WD = 0.01
ADAM_STEP = 10

TM = 512
TM_IN = 1024
TM_POOL = 512
TM_FFN = 512
TM_FFN_FWD = 1024
FFN_ROWS = 256
VMEM_LIMIT = 60 * 1024 * 1024

MESH_T = pl.DeviceIdType.MESH
ANY = pl.BlockSpec(memory_space=pl.ANY)


def _cp(n_grid=0, **kw):
    sem = ("arbitrary",) * n_grid if n_grid else None
    return pltpu.CompilerParams(dimension_semantics=sem, vmem_limit_bytes=VMEM_LIMIT, **kw)


def _dot(a, b):
    return jnp.dot(a, b, preferred_element_type=F32)


def _dot_nt(a, b):
    return lax.dot_general(a, b, (((1,), (1,)), ((), ())), preferred_element_type=F32)


def _dot_tn(a, b):
    return lax.dot_general(a, b, (((0,), (0,)), ((), ())), preferred_element_type=F32)


def _rms(x):
    r = lax.rsqrt(jnp.mean(x * x, axis=-1, keepdims=True) + EPS)
    return r, x * r


def _rms_bwd(r, n, g, dout):
    dn = dout * g
    dx = r * (dn - n * jnp.mean(dn * n, axis=-1, keepdims=True))
    dg = jnp.sum(dout * n, axis=0, keepdims=True)
    return dx, dg


def _split(x, n):
    parts = []
    r = x
    for _ in range(n):
        p = r.astype(BF16)
        parts.append(p)
        r = r - p.astype(F32)
    return parts


def _band_dot(a, x, n):
    acc = None
    for p in _split(x, n):
        t = _dot(a, p)
        acc = t if acc is None else acc + t
    return acc


def _bucket_table():
    qi = np.arange(BLK)[None, :]
    kj = np.arange(2 * BLK)[:, None]
    dist = qi + BLK - kj
    n = np.maximum(dist, 0)
    nf = np.maximum(n, 1).astype(np.float32)
    large = 16 + (np.log(nf / np.float32(16)) / np.float32(np.log(128 / 16)) * np.float32(16)).astype(np.int32)
    large = np.minimum(large, NBUCKET - 1)
    return np.where(n < 16, n, large).astype(np.int32)


def _prenorm(x, g1):
    s = x.shape[0]
    tm = min(TM_IN, s)

    def body(x_ref, g_ref, h_ref):
        _, n = _rms(x_ref[...])
        h_ref[...] = (n * g_ref[...]).astype(BF16)

    row = pl.BlockSpec((tm, D), lambda i: (i, 0))
    return pl.pallas_call(
        body, name="prenorm", grid=(s // tm,),
        in_specs=[row, pl.BlockSpec((1, D), lambda i: (0, 0))], out_specs=row,
        out_shape=jax.ShapeDtypeStruct((s, D), BF16),
        compiler_params=_cp(1))(x, g1)


def _inproj_fwd(h1, w_in):
    s = h1.shape[0]
    tm = min(TM_IN, s)

    def body(h_ref, w_ref, u_ref, q_ref, k_ref, v_ref):
        proj = _dot_nt(h_ref[...], w_ref[...])
        u_ref[...] = proj[:, :512]
        q_ref[...] = proj[:, 512:1024].astype(BF16)
        k_ref[...] = proj[:, 1024:1152].astype(BF16)
        v_ref[...] = proj[:, 1152:1280].astype(BF16)

    row = lambda w: pl.BlockSpec((tm, w), lambda i: (i, 0))
    return pl.pallas_call(
        body, name="inproj_fwd", grid=(s // tm,),
        in_specs=[row(D), pl.BlockSpec((IN_W, D), lambda i: (0, 0))],
        out_specs=[row(512), row(512), row(128), row(128)],
        out_shape=[jax.ShapeDtypeStruct((s, 512), F32), jax.ShapeDtypeStruct((s, 512), BF16),
                   jax.ShapeDtypeStruct((s, 128), BF16), jax.ShapeDtypeStruct((s, 128), BF16)],
        compiler_params=_cp(1))(h1, w_in)


def _window_sums(ext, w, lag_sign, tm, terms):
    rows = lax.broadcasted_iota(jnp.int32, (HALO, 2 * HALO), 0)
    cols = lax.broadcasted_iota(jnp.int32, (HALO, 2 * HALO), 1)
    d = rows + HALO - cols if lag_sign > 0 else cols - rows
    band = jnp.where((d >= 0) & (d < w), 1.0, 0.0).astype(BF16)
    return jnp.concatenate([_band_dot(band, ext[r:r + 2 * HALO], terms) for r in range(0, tm, HALO)], axis=0)


def _pooled(ext, cur, t0, tm):
    t = t0 + lax.broadcasted_iota(jnp.int32, (tm, GROUP), 0)
    out = []
    for g, w in enumerate(WINDOWS):
        sl = slice(g * GROUP, (g + 1) * GROUP)
        cnt = jnp.minimum(t + 1, w).astype(F32)
        out.append(_window_sums(ext[:, sl], w, 1, tm, 2) / cnt - cur[:, sl])
    return out


def _pool_fwd(u, w_pool, pool_scale):
    s = u.shape[0]
    tm = min(TM_POOL, s)
    hb = tm // HALO

    def body(uc_ref, uh_ref, wp_ref, sc_ref, o_ref, pooled_ref):
        i = pl.program_id(0)
        cur = uc_ref[...]
        halo = jnp.where(i > 0, uh_ref[...], 0.0)
        ext = jnp.concatenate([halo, cur], axis=0)
        pooled = _pooled(ext, cur, i * tm, tm)
        for g in range(4):
            sl = slice(g * GROUP, (g + 1) * GROUP)
            pb = pooled[g].astype(BF16)
            pooled_ref[:, sl] = pb
            o_ref[:, sl] = (_dot(pb, wp_ref[g]) * sc_ref[:, sl]).astype(BF16)

    row = pl.BlockSpec((tm, 512), lambda i: (i, 0))
    return pl.pallas_call(
        body, name="pool_fwd", grid=(s // tm,),
        in_specs=[row, pl.BlockSpec((HALO, 512), lambda i: (jnp.maximum(i * hb - 1, 0), 0)),
                  pl.BlockSpec((4, GROUP, GROUP), lambda i: (0, 0, 0)),
                  pl.BlockSpec((1, 512), lambda i: (0, 0))],
        out_specs=[row, row],
        out_shape=[jax.ShapeDtypeStruct((s, 512), BF16)] * 2,
        compiler_params=_cp(1))(u, u, w_pool, pool_scale)


def _bias_table(bucket, rel_bias):
    def body(b_ref, rb_ref, o_ref):
        bucket_v = b_ref[...]
        key = lax.broadcasted_iota(jnp.int32, (2 * BLK, BLK), 0)
        dist = lax.broadcasted_iota(jnp.int32, (2 * BLK, BLK), 1) + BLK - key
        inwin = (dist >= 0) & (dist < BLK)
        for h in range(NQ):
            acc = jnp.zeros((2 * BLK, BLK), F32)
            for b in range(NBUCKET):
                acc = jnp.where(bucket_v == b, rb_ref[b, h], acc)
            rest = jnp.where(inwin, acc, NEG)
            o_ref[1, h] = rest
            o_ref[0, h] = jnp.where(key >= BLK, rest, NEG)

    return pl.pallas_call(
        body, name="bias_table",
        in_specs=[pl.BlockSpec(memory_space=pltpu.VMEM), pl.BlockSpec(memory_space=pltpu.SMEM)],
        out_specs=pl.BlockSpec(memory_space=pltpu.VMEM),
        out_shape=jax.ShapeDtypeStruct((2, NQ, 2 * BLK, BLK), F32),
        compiler_params=_cp())(bucket, rel_bias)


HD = 64


def _kv_band(ref, n, transposed):
    pstart = pl.multiple_of(jnp.maximum(n - 1, 0) * BLK, BLK)
    cstart = pl.multiple_of(n * BLK, BLK)
    lo = lax.broadcasted_iota(jnp.int32, (2 * BLK, 128), 1) < HD
    band = jnp.concatenate([ref[pl.ds(pstart, BLK), :], ref[pl.ds(cstart, BLK), :]], axis=0).astype(F32)
    rot = pltpu.roll(band, HD, axis=1)
    halves = ((jnp.where(lo, band, 0.0), jnp.where(lo, 0.0, rot)), (jnp.where(lo, rot, 0.0), jnp.where(lo, 0.0, band)))
    if transposed:
        out = [jnp.concatenate([a.T, b.T], axis=1).astype(BF16) for a, b in halves]
    else:
        out = [jnp.concatenate([a, b], axis=0).astype(BF16) for a, b in halves]
    return out, (lo, pstart, cstart)


def _pair_probs(qt, kk, bias, sk_ref, p):
    sink = jnp.concatenate([jnp.full((1, 1, BLK), sk_ref[0, 2 * p + e], F32) for e in range(2)], axis=0)
    s = _dot_nt(kk, qt).reshape(2, 2 * BLK, BLK) + bias
    m = jnp.maximum(jnp.max(s, axis=1, keepdims=True), sink)
    pr = jnp.exp(s - m)
    es = jnp.exp(sink - m)
    inv = 1.0 / (jnp.sum(pr, axis=1, keepdims=True) + es)
    return pr * inv, es * inv


def _attn_fwd(q, k, v, bias, sinks):
    s = q.shape[0]
    nblk = s // BLK

    def body(q_ref, k_ref, v_ref, b_ref, sk_ref, o_ref, prob_ref, ps_ref):
        n = pl.program_id(0)
        kk, _ = _kv_band(k_ref, n, False)
        vt, _ = _kv_band(v_ref, n, True)
        bidx = jnp.minimum(n, 1)
        for p in range(4):
            h = p // 2
            qt = (q_ref[:, p * 128:(p + 1) * 128].astype(F32) * SCALE).astype(BF16)
            prob, ps = _pair_probs(qt, kk[h], b_ref[bidx, pl.ds(2 * p, 2)], sk_ref, p)
            pb = prob.astype(BF16)
            prob_ref[pl.ds(2 * p, 2)] = pb
            ps_ref[pl.ds(2 * p, 2), :] = ps.reshape(2, BLK)
            acc_t = _dot(vt[h], pb.reshape(4 * BLK, BLK))
            o_ref[:, p * 128:(p + 1) * 128] = acc_t.T.astype(BF16)

    return pl.pallas_call(
        body, name="attn_fwd", grid=(nblk,),
        in_specs=[pl.BlockSpec((BLK, 512), lambda i: (i, 0)),
                  pl.BlockSpec((s, 128), lambda i: (0, 0)),
                  pl.BlockSpec((s, 128), lambda i: (0, 0)),
                  pl.BlockSpec((2, NQ, 2 * BLK, BLK), lambda i: (0, 0, 0, 0)),
                  pl.BlockSpec(memory_space=pltpu.SMEM)],
        out_specs=[pl.BlockSpec((BLK, 512), lambda i: (i, 0)),
                   pl.BlockSpec((None, NQ, 2 * BLK, BLK), lambda i: (i, 0, 0, 0)),
                   pl.BlockSpec((None, NQ, BLK), lambda i: (i, 0, 0))],
        out_shape=[jax.ShapeDtypeStruct((s, 512), BF16), jax.ShapeDtypeStruct((nblk, NQ, 2 * BLK, BLK), BF16),
                   jax.ShapeDtypeStruct((nblk, NQ, BLK), F32)],
        compiler_params=_cp(1))(q, k, v, bias, sinks)


def _outproj_fwd(pool_o, attn_o, w_out, x, g2, g3, between=None):
    s = x.shape[0]
    tm = min(TM, s)
    nt = s // tm

    def part(name, tile0, tiles, filled, dep):
        def body(p_ref, a_ref, w_ref, x_ref, g2_ref, g3_ref, *rest):
            mix_ref, x1_ref, h2_ref = rest[-3:]
            mix = _dot(p_ref[...], w_ref[0:512, :]) + _dot(a_ref[...], w_ref[512:1024, :])
            mix_ref[...] = mix
            _, n2 = _rms(mix)
            x1 = x_ref[...] + n2 * g2_ref[...]
            x1_ref[...] = x1
            _, n3 = _rms(x1)
            h2_ref[...] = (n3 * g3_ref[...]).astype(BF16)

        row = lambda w: pl.BlockSpec((tm, w), lambda i: (i + tile0, 0))
        vec = pl.BlockSpec((1, D), lambda i: (0, 0))
        extra = list(filled) + ([] if dep is None else [dep])
        return pl.pallas_call(
            body, name=name, grid=(tiles,),
            in_specs=[row(512), row(512), pl.BlockSpec((D, D), lambda i: (0, 0)), row(D), vec, vec]
            + [ANY] * len(extra),
            out_specs=[row(D), row(D), row(D)],
            out_shape=[jax.ShapeDtypeStruct((s, D), F32), jax.ShapeDtypeStruct((s, D), F32),
                       jax.ShapeDtypeStruct((s, D), BF16)],
            input_output_aliases={6 + t: t for t in range(len(filled))},
            compiler_params=_cp(1))(pool_o, attn_o, w_out, x, g2, g3, *extra)

    if between is None or nt < 2:
        return part("outproj_fwd", 0, nt, (), None)
    first = part("outproj_fwd_a", 0, nt // 2, (), None)
    return part("outproj_fwd_b", nt // 2, nt - nt // 2, first, between(first[2]))


def _silu_parts(g):
    sg = jax.nn.sigmoid(g)
    return sg, g * sg


def _ffn_fwd(h2, wg, wu, wd, x1, target, g4):
    s = h2.shape[0]
    tm = min(TM_FFN_FWD, s)

    def body(h_ref, wg_ref, wu_ref, wd_ref, x1_ref, t_ref, g4_ref,
             gate_ref, up_ref, a_ref, df_ref, dy_ref, loss_ref, gg4_ref, f_acc):
        i = pl.program_id(0)
        j = pl.program_id(1)
        first = j == 0
        chunks = [pl.ds(r, min(FFN_ROWS, tm)) for r in range(0, tm, FFN_ROWS)]
        project = lambda rows: (_dot_nt(h_ref[rows, :], wg_ref[...]), _dot_nt(h_ref[rows, :], wu_ref[...]))
        ahead = [project(chunks[0])]
        for k, rows in enumerate(chunks):
            if k + 1 < len(chunks):
                ahead.append(project(chunks[k + 1]))
            gate, up = ahead[k]
            gate_ref[rows, :] = gate.astype(BF16)
            up_ref[rows, :] = up.astype(BF16)
            _, sl = _silu_parts(gate)
            a = (sl * up).astype(BF16)
            a_ref[rows, :] = a
            contrib = _dot(a, wd_ref[...])
            f_acc[rows, :] = jnp.where(first, contrib, f_acc[rows, :] + contrib)

        @pl.when((i == 0) & (j == 0))
        def _():
            loss_ref[...] = jnp.zeros_like(loss_ref)
            gg4_ref[...] = jnp.zeros_like(gg4_ref)

        @pl.when(j == NSH - 1)
        def _():
            r4, n4 = _rms(f_acc[...])
            g4v = g4_ref[...]
            err = x1_ref[...] + n4 * g4v - t_ref[...]
            loss_ref[...] += (0.5 / D) * jnp.sum(err * err).reshape(1, 1)
            dy = err * (1.0 / D)
            dy_ref[...] = dy
            df, dg = _rms_bwd(r4, n4, g4v, dy)
            gg4_ref[...] += dg
            df_ref[...] = df.astype(BF16)

    row = lambda w: pl.BlockSpec((tm, w), lambda i, j: (i, 0))
    sh = lambda r, c: pl.BlockSpec((None, r, c), lambda i, j: (j, 0, 0))
    act = pl.BlockSpec((None, tm, FS), lambda i, j: (j, i, 0))
    vec = pl.BlockSpec((1, D), lambda i, j: (0, 0))
    return pl.pallas_call(
        body, name="ffn_fwd", grid=(s // tm, NSH),
        in_specs=[row(D), sh(FS, D), sh(FS, D), sh(FS, D), row(D), row(D), vec],
        out_specs=[act, act, act, row(D), row(D), pl.BlockSpec((1, 1), lambda i, j: (0, 0)), vec],
        out_shape=[jax.ShapeDtypeStruct((NSH, s, FS), BF16)] * 3
        + [jax.ShapeDtypeStruct((s, D), BF16), jax.ShapeDtypeStruct((s, D), F32),
           jax.ShapeDtypeStruct((1, 1), F32), jax.ShapeDtypeStruct((1, D), F32)],
        scratch_shapes=[pltpu.VMEM((tm, D), F32)],
        compiler_params=_cp(2))(h2, wg, wu, wd, x1, target, g4)


def _ffn_bwd_act(df, gate, up, wg, wu, wd, dy, x1, mix, g3, g2):
    s = df.shape[0]
    tm = min(TM_FFN, s)

    def body(df_ref, gate_ref, up_ref, wg_ref, wu_ref, wd_ref, dy_ref, x1_ref, mix_ref, g3_ref, g2_ref,
             dgate_ref, dup_ref, dx1_ref, dmix_ref, gg3_ref, gg2_ref, acc):
        j = pl.program_id(0)
        i = pl.program_id(1)
        first = j == 0
        tile = pl.multiple_of(i * tm, tm)
        chunks = [pl.ds(r, min(FFN_ROWS, tm)) for r in range(0, tm, FFN_ROWS)]

        @pl.when((i == 0) & (j == 0))
        def _():
            gg3_ref[...] = jnp.zeros_like(gg3_ref)
            gg2_ref[...] = jnp.zeros_like(gg2_ref)

        def norms_backward(rows, dh2):
            r3, n3 = _rms(x1_ref[rows, :])
            dx1n, dg3 = _rms_bwd(r3, n3, g3_ref[...], dh2)
            dx1 = dy_ref[rows, :] + dx1n
            dx1_ref[rows, :] = dx1
            gg3_ref[...] += dg3
            r2, n2 = _rms(mix_ref[rows, :])
            dmix, dg2 = _rms_bwd(r2, n2, g2_ref[...], dx1)
            gg2_ref[...] += dg2
            dmix_ref[rows, :] = dmix.astype(BF16)

        def shard_pass(last):
            das = [_dot_nt(df_ref[chunks[0], :], wd_ref[...])]
            for k, rows in enumerate(chunks):
                if k + 1 < len(chunks):
                    das.append(_dot_nt(df_ref[chunks[k + 1], :], wd_ref[...]))
                da = das[k]
                g = gate_ref[rows, :].astype(F32)
                u = up_ref[rows, :].astype(F32)
                sg, sl = _silu_parts(g)
                dgate = (da * u * (sg * (1.0 + g * (1.0 - sg)))).astype(BF16)
                dup = (da * sl).astype(BF16)
                dgate_ref[rows, :] = dgate
                dup_ref[rows, :] = dup
                contrib = _dot(dgate, wg_ref[...]) + _dot(dup, wu_ref[...])
                acc_rows = pl.ds(tile + k * FFN_ROWS, rows.size)
                if last:
                    norms_backward(rows, acc[acc_rows, :] + contrib)
                else:
                    acc[acc_rows, :] = jnp.where(first, contrib, acc[acc_rows, :] + contrib)

        pl.when(j < NSH - 1)(functools.partial(shard_pass, False))
        pl.when(j == NSH - 1)(functools.partial(shard_pass, True))

    row = pl.BlockSpec((tm, D), lambda j, i: (i, 0))
    last = pl.BlockSpec((tm, D), lambda j, i: (i * (j // (NSH - 1)), 0))
    sh = pl.BlockSpec((None, FS, D), lambda j, i: (j, 0, 0))
    act = pl.BlockSpec((None, tm, FS), lambda j, i: (j, i, 0))
    vec = pl.BlockSpec((1, D), lambda j, i: (0, 0))
    return pl.pallas_call(
        body, name="ffn_bwd_act", grid=(NSH, s // tm),
        in_specs=[row, act, act, sh, sh, sh, last, last, last, vec, vec],
        out_specs=[act, act, last, last, vec, vec],
        out_shape=[jax.ShapeDtypeStruct((NSH, s, FS), BF16), jax.ShapeDtypeStruct((NSH, s, FS), BF16),
                   jax.ShapeDtypeStruct((s, D), F32), jax.ShapeDtypeStruct((s, D), BF16),
                   jax.ShapeDtypeStruct((1, D), F32), jax.ShapeDtypeStruct((1, D), F32)],
        scratch_shapes=[pltpu.VMEM((s, D), F32)],
        compiler_params=_cp(2))(df, gate, up, wg, wu, wd, dy, x1, mix, g3, g2)


SMEM_SPEC = pl.BlockSpec(memory_space=pltpu.SMEM)


def _emit_halves(acc_ref, row0, rows, c, own_ref, oth_ref):
    half = rows // 2
    own_ref[...] = acc_ref[pl.ds(row0 + pl.multiple_of(c * half, 8), half), :]
    oth_ref[...] = acc_ref[pl.ds(row0 + pl.multiple_of((1 - c) * half, 8), half), :].astype(BF16)


def _half_shapes(rows):
    return [jax.ShapeDtypeStruct((NSH, rows // 2, D), F32), jax.ShapeDtypeStruct((NSH, rows // 2, D), BF16)]


def _ffn_bwd_w(h2, act_a, dgate, dup, df, c_arr):
    s = h2.shape[0]
    tm = min(TM_FFN_FWD, s)
    nt = s // tm

    def body(c_ref, h_ref, a_ref, dgate_ref, dup_ref, df_ref, *rest):
        outs, accs = rest[:6], rest[6:]
        i = pl.program_id(1)

        @pl.when(i == 0)
        def _():
            for acc in accs:
                acc[...] = jnp.zeros_like(acc)

        h = h_ref[...]
        accs[0][...] += _dot_tn(dgate_ref[...], h)
        accs[1][...] += _dot_tn(dup_ref[...], h)
        accs[2][...] += _dot_tn(a_ref[...], df_ref[...])

        @pl.when(i == nt - 1)
        def _():
            for t, acc in enumerate(accs):
                _emit_halves(acc, 0, FS, c_ref[0], outs[2 * t], outs[2 * t + 1])

    row = lambda w: pl.BlockSpec((tm, w), lambda j, i: (i, 0))
    act = pl.BlockSpec((None, tm, FS), lambda j, i: (j, i, 0))
    half = pl.BlockSpec((None, FS // 2, D), lambda j, i: (j, 0, 0))
    return pl.pallas_call(
        body, name="ffn_bwd_w", grid=(NSH, nt),
        in_specs=[SMEM_SPEC, row(D), act, act, act, row(D)],
        out_specs=[half] * 6,
        out_shape=_half_shapes(FS) * 3,
        scratch_shapes=[pltpu.VMEM((FS, D), F32)] * 3,
        compiler_params=_cp(2))(c_arr, h2, act_a, dgate, dup, df)


def _outproj_bwd(dmix, w_out, pool_o, attn_o, c_arr, dep=None):
    s = dmix.shape[0]
    tm = min(TM, s)
    nt = s // tm

    def body(c_ref, dm_ref, w_ref, p_ref, a_ref, *rest):
        dpool_ref, dattn_ref, own_ref, oth_ref, gw_ref = rest[-5:]
        i = pl.program_id(0)

        @pl.when(i == 0)
        def _():
            gw_ref[...] = jnp.zeros_like(gw_ref)

        dm = dm_ref[...]
        dpool_ref[...] = _dot_nt(dm, w_ref[0:512, :])
        dattn_ref[...] = _dot_nt(dm, w_ref[512:1024, :]).astype(BF16)
        gw_ref[0:512, :] += _dot_tn(p_ref[...], dm)
        gw_ref[512:1024, :] += _dot_tn(a_ref[...], dm)

        @pl.when(i == nt - 1)
        def _():
            for k in range(NSH):
                _emit_halves(gw_ref, k * WOUT_S, WOUT_S, c_ref[0], own_ref.at[k], oth_ref.at[k])

    row = lambda w: pl.BlockSpec((tm, w), lambda i: (i, 0))
    full = pl.BlockSpec((D, D), lambda i: (0, 0))
    half = pl.BlockSpec((NSH, WOUT_S // 2, D), lambda i: (0, 0, 0))
    return pl.pallas_call(
        body, name="outproj_bwd", grid=(nt,),
        in_specs=[SMEM_SPEC, row(D), full, row(512), row(512)] + ([] if dep is None else [ANY]),
        out_specs=[row(512), row(512), half, half],
        out_shape=[jax.ShapeDtypeStruct((s, 512), F32), jax.ShapeDtypeStruct((s, 512), BF16)] + _half_shapes(WOUT_S),
        scratch_shapes=[pltpu.VMEM((D, D), F32)],
        compiler_params=_cp(1))(c_arr, dmix, w_out, pool_o, attn_o, *([] if dep is None else [dep]))


def _pool_bwd(pooled, dpool, w_pool, pool_scale, dep=None):
    s = pooled.shape[0]
    tm = min(TM_POOL, s)
    hb = tm // HALO
    nt = s // tm
    last_halo = s // HALO - 1

    def body(p_ref, dc_ref, dn_ref, wp_ref, sc_ref, *rest):
        du_ref, gwp_ref, gsc_ref = rest[-3:]
        i = pl.program_id(0)

        @pl.when(i == 0)
        def _():
            gwp_ref[...] = jnp.zeros_like(gwp_ref)
            gsc_ref[...] = jnp.zeros_like(gsc_ref)

        dcur = dc_ref[...]
        dnext = jnp.where(i < nt - 1, dn_ref[...], 0.0)
        dext = jnp.concatenate([dcur, dnext], axis=0)
        t_ext = i * tm + lax.broadcasted_iota(jnp.int32, (tm + HALO, GROUP), 0)
        for g, w in enumerate(WINDOWS):
            sl = slice(g * GROUP, (g + 1) * GROUP)
            pb = p_ref[:, sl]
            wp = wp_ref[g]
            mixed = _dot(pb, wp)
            gsc_ref[:, sl] += jnp.sum(dcur[:, sl] * mixed, axis=0, keepdims=True)
            dmixed = (dext[:, sl] * sc_ref[:, sl]).astype(BF16)
            gwp_ref[g] += _dot_tn(pb, dmixed[0:tm])
            dpooled = _dot_nt(dmixed, wp)
            z = dpooled / jnp.minimum(t_ext + 1, w).astype(F32)
            du_ref[:, sl] = (_window_sums(z, w, -1, tm, 2) - dpooled[0:tm]).astype(BF16)

    return pl.pallas_call(
        body, name="pool_bwd", grid=(nt,),
        in_specs=[pl.BlockSpec((tm, 512), lambda i: (i, 0)),
                  pl.BlockSpec((tm, 512), lambda i: (i, 0)),
                  pl.BlockSpec((HALO, 512), lambda i: (jnp.minimum((i + 1) * hb, last_halo), 0)),
                  pl.BlockSpec((4, GROUP, GROUP), lambda i: (0, 0, 0)),
                  pl.BlockSpec((1, 512), lambda i: (0, 0))] + ([] if dep is None else [ANY]),
        out_specs=[pl.BlockSpec((tm, 512), lambda i: (i, 0)),
                   pl.BlockSpec((4, GROUP, GROUP), lambda i: (0, 0, 0)),
                   pl.BlockSpec((1, 512), lambda i: (0, 0))],
        out_shape=[jax.ShapeDtypeStruct((s, 512), BF16), jax.ShapeDtypeStruct((4, GROUP, GROUP), F32),
                   jax.ShapeDtypeStruct((1, 512), F32)],
        compiler_params=_cp(1))(pooled, dpool, dpool, w_pool, pool_scale, *([] if dep is None else [dep]))


def _attn_bwd(q, k, v, do, probs, sink_share, bucket, dep=None):
    s = q.shape[0]
    nblk = s // BLK

    def body(q_ref, do_ref, k_ref, v_ref, prob_ref, ps_ref, bk_ref, *rest):
        dq_ref, dk_ref, dv_ref, grb_ref, gsk_ref, dss_ref = rest[-6:]
        n = pl.program_id(0)

        @pl.when(n == 0)
        def _():
            dk_ref[...] = jnp.zeros_like(dk_ref)
            dv_ref[...] = jnp.zeros_like(dv_ref)
            dss_ref[...] = jnp.zeros_like(dss_ref)
            gsk_ref[...] = jnp.zeros_like(gsk_ref)

        kt, (lo, pstart, cstart) = _kv_band(k_ref, n, True)
        vv, _ = _kv_band(v_ref, n, False)
        lo_q = lax.broadcasted_iota(jnp.int32, (BLK, 128), 1) < HD
        dkk = [jnp.zeros((2 * BLK, 128), F32), jnp.zeros((2 * BLK, 128), F32)]
        dvv = [jnp.zeros((2 * BLK, 128), F32), jnp.zeros((2 * BLK, 128), F32)]

        def by_head(t):
            return jnp.concatenate([jnp.where(lo_q, t, 0.0), jnp.where(lo_q, 0.0, t)], axis=0).astype(BF16)

        for p in range(4):
            h = p // 2
            qt = q_ref[:, p * 128:(p + 1) * 128].astype(F32) * SCALE
            dot = do_ref[:, p * 128:(p + 1) * 128]
            pb = prob_ref[pl.ds(2 * p, 2)]
            prob = pb.astype(F32)
            ps = ps_ref[pl.ds(2 * p, 2), :].reshape(2, 1, BLK)
            dp = _dot_nt(vv[h], dot).reshape(2, 2 * BLK, BLK)
            delta = jnp.sum(prob * dp, axis=1, keepdims=True)
            ds = prob * (dp - delta)
            sink_part = ps * delta
            for e in range(2):
                gs = -jnp.sum(sink_part[e]).reshape(1, 1)
                gsk_ref[pl.ds(2 * p + e, 1), :] += jnp.broadcast_to(gs, (1, 128))
            dss_ref[pl.ds(2 * p, 2)] += ds
            dsb = ds.astype(BF16)
            dq_t = _dot(kt[h], dsb.reshape(4 * BLK, BLK))
            dq_ref[:, p * 128:(p + 1) * 128] = (dq_t.T * SCALE).astype(BF16)
            dkk[h] = dkk[h] + _dot(jnp.concatenate([dsb[0], dsb[1]], axis=1), by_head(qt))
            dvv[h] = dvv[h] + _dot(jnp.concatenate([pb[0], pb[1]], axis=1), by_head(dot.astype(F32)))
        for acc, ref in ((dkk, dk_ref), (dvv, dv_ref)):
            f0 = acc[0] + pltpu.roll(acc[0], HD, axis=1)
            f1 = acc[1] + pltpu.roll(acc[1], HD, axis=1)
            band = jnp.where(lo, f0, f1)
            ref[pl.ds(pstart, BLK), :] += band[0:BLK]
            ref[pl.ds(cstart, BLK), :] += band[BLK:2 * BLK]

        @pl.when(n == nblk - 1)
        def _():
            _relbias_grad(dss_ref, bk_ref[...], grb_ref)

    blk = pl.BlockSpec((BLK, 512), lambda i: (i, 0))
    kv = pl.BlockSpec((s, 128), lambda i: (0, 0))
    row8 = pl.BlockSpec((NQ, 128), lambda i: (0, 0))
    return pl.pallas_call(
        body, name="attn_bwd", grid=(nblk,),
        in_specs=[blk, blk, kv, kv, pl.BlockSpec((None, NQ, 2 * BLK, BLK), lambda i: (i, 0, 0, 0)),
                  pl.BlockSpec((None, NQ, BLK), lambda i: (i, 0, 0)), pl.BlockSpec((2 * BLK, BLK), lambda i: (0, 0))]
        + ([] if dep is None else [ANY]),
        out_specs=[blk, kv, kv, row8, row8],
        out_shape=[jax.ShapeDtypeStruct((s, 512), BF16), jax.ShapeDtypeStruct((s, 128), F32),
                   jax.ShapeDtypeStruct((s, 128), F32), jax.ShapeDtypeStruct((NQ, 128), F32),
                   jax.ShapeDtypeStruct((NQ, 128), F32)],
        scratch_shapes=[pltpu.VMEM((NQ, 2 * BLK, BLK), F32)],
        compiler_params=_cp(1))(q, do, k, v, probs, sink_share, bucket, *([] if dep is None else [dep]))


def _relbias_grad(ds_ref, bucket_v, o_ref):
    lane = lax.broadcasted_iota(jnp.int32, (NQ, 128), 1)
    head_row = lax.broadcasted_iota(jnp.int32, (NQ, BLK), 0)
    acc = jnp.zeros((NQ, 128), F32)
    for b in range(NBUCKET):
        sel = bucket_v == b
        stack = jnp.zeros((NQ, BLK), F32)
        for h in range(NQ):
            col_sums = jnp.sum(jnp.where(sel, ds_ref[h], 0.0), axis=0, keepdims=True)
            stack = jnp.where(head_row == h, col_sums, stack)
        acc = acc + jnp.where(lane == b, jnp.sum(stack, axis=1, keepdims=True), 0.0)
    o_ref[...] = acc


def _inproj_bwd(x, g1, du, dq, dk, dv, w_in, dx1, c_arr):
    s = x.shape[0]
    tm = min(TM, s)
    nt = s // tm
    cols = ((0, 512), (512, 1024), (1024, 1152), (1152, 1280))

    def body(c_ref, x_ref, g_ref, du_ref, dq_ref, dk_ref, dv_ref, w_ref, dx1_ref,
             gx_ref, own_ref, oth_ref, gg_ref, gw_ref):
        i = pl.program_id(0)

        @pl.when(i == 0)
        def _():
            gw_ref[...] = jnp.zeros_like(gw_ref)
            gg_ref[...] = jnp.zeros_like(gg_ref)

        parts = (du_ref[...], dq_ref[...], dk_ref[...].astype(BF16), dv_ref[...].astype(BF16))
        dh = None
        for (a, b), dpart in zip(cols, parts):
            t = _dot(dpart, w_ref[a:b, :])
            dh = t if dh is None else dh + t
        gv = g_ref[...]
        r1, n1 = _rms(x_ref[...])
        h = (n1 * gv).astype(BF16)
        for (a, b), dpart in zip(cols, parts):
            gw_ref[a:b, :] += _dot_tn(dpart, h)
        dx, dg = _rms_bwd(r1, n1, gv, dh)
        gg_ref[...] += dg
        gx_ref[...] = dx1_ref[...] + dx

        @pl.when(i == nt - 1)
        def _():
            for k in range(NSH):
                _emit_halves(gw_ref, k * WIN_S, WIN_S, c_ref[0], own_ref.at[k], oth_ref.at[k])

    row = lambda w: pl.BlockSpec((tm, w), lambda i: (i, 0))
    vec = pl.BlockSpec((1, D), lambda i: (0, 0))
    full = pl.BlockSpec((IN_W, D), lambda i: (0, 0))
    half = pl.BlockSpec((NSH, WIN_S // 2, D), lambda i: (0, 0, 0))
    return pl.pallas_call(
        body, name="inproj_bwd", grid=(nt,),
        in_specs=[SMEM_SPEC, row(D), vec, row(512), row(512), row(128), row(128), full, row(D)],
        out_specs=[row(D), half, half, vec],
        out_shape=[jax.ShapeDtypeStruct((s, D), F32)] + _half_shapes(WIN_S) + [jax.ShapeDtypeStruct((1, D), F32)],
        scratch_shapes=[pltpu.VMEM((IN_W, D), F32)],
        compiler_params=_cp(1))(c_arr, x, g1, du, dq, dk, dv, w_in, dx1)


def _local_step(x, target, small, w_in, w_out, ffn_weights, c_arr, on_ffn_grads=None, on_wout_grads=None,
                mid_outproj=None):
    g1, g2, g3, g4, w_pool, pool_scale, rel_bias, sinks = small
    bucket = jnp.asarray(_bucket_table())
    wp_bf = w_pool.astype(BF16)
    bias = _bias_table(bucket, rel_bias)
    h1 = _prenorm(x, g1)
    if callable(w_in):
        w_in = w_in(bias, h1)
    u, q, k, v = _inproj_fwd(h1, w_in)
    pool_o, pooled = _pool_fwd(u, wp_bf, pool_scale)
    attn_o, probs, sink_share = _attn_fwd(q, k, v, bias, sinks)
    g2_fwd = g2
    if callable(w_out):
        w_out, after = w_out(pool_o, attn_o)
        if after is not None:
            g2_fwd = g2 + after[:1, :1]
    mix, x1, h2 = _outproj_fwd(pool_o, attn_o, w_out, x, g2_fwd, g3, mid_outproj)
    wg, wu, wd = ffn_weights(h2) if callable(ffn_weights) else ffn_weights
    gate, up, act_a, df, dy, loss, gg4 = _ffn_fwd(h2, wg, wu, wd, x1, target, g4)
    dgate, dup, dx1, dmix, gg3, gg2 = _ffn_bwd_act(df, gate, up, wg, wu, wd, dy, x1, mix, g3, g2)
    ffn_g = _ffn_bwd_w(h2, act_a, dgate, dup, df, c_arr)
    dep = None if on_ffn_grads is None else on_ffn_grads(ffn_g)
    dpool, dattn, wout_own, wout_oth = _outproj_bwd(dmix, w_out, pool_o, attn_o, c_arr, dep)
    dep = None if on_wout_grads is None else on_wout_grads(wout_own, wout_oth, dpool)
    du, gwp, gsc = _pool_bwd(pooled, dpool, wp_bf, pool_scale, dep)
    dq, dk, dv, grb, gsk = _attn_bwd(q, k, v, dattn, probs, sink_share, bucket, dep)
    gx, win_own, win_oth, gg1 = _inproj_bwd(x, g1, du, dq, dk, dv, w_in, dx1, c_arr)
    small_grads = (gg1, gg2, gg3, gg4, gwp, gsc, grb, gsk[:, 0].reshape(1, NQ))
    return loss, gx, small_grads, ((win_own, win_oth), (wout_own, wout_oth), ffn_g)


def _place():
    x, y, c = lax.axis_index("x"), lax.axis_index("y"), lax.axis_index("c")
    chips = ((1 - x, y), (x, 1 - y), (1 - x, 1 - y))
    return x, y, c, chips


def _remote(src, dst, ssem, rsem, dev):
    return pltpu.make_async_remote_copy(src_ref=src, dst_ref=dst, send_sem=ssem, recv_sem=rsem,
                                        device_id=dev, device_id_type=MESH_T)


def _to_sibling(arrs, name, after=()):
    nt, na = len(arrs), len(after)

    def body(*refs):
        srcs, dsts = refs[:nt], refs[nt + na:2 * nt + na]
        ssem, rsem = refs[2 * nt + na:]
        x, y, c, _ = _place()
        cps = [_remote(srcs[t], dsts[t], ssem.at[t], rsem.at[t], (x, y, 1 - c)) for t in range(nt)]
        for cp in cps:
            cp.start()
        for cp in cps:
            cp.wait()

    return pl.pallas_call(
        body, name=name, in_specs=[ANY] * (nt + na), out_specs=[ANY] * nt,
        out_shape=[jax.ShapeDtypeStruct(a.shape, a.dtype) for a in arrs],
        scratch_shapes=[pltpu.SemaphoreType.DMA((nt,)), pltpu.SemaphoreType.DMA((nt,))],
        compiler_params=_cp())(*arrs, *after)


HBM_SPEC = pl.BlockSpec(memory_space=pltpu.HBM)
SEM_SPEC = pl.BlockSpec(memory_space=pltpu.SEMAPHORE)
DATAFLOW = pltpu.SideEffectType.DATAFLOW_SIDE_EFFECTING


def _gather_copies(srcs, lands, ssem, rsem):
    x, y, c, chips = _place()
    me = 2 * x + y
    out = []
    for t in range(len(srcs)):
        half = srcs[t].shape[0] // 2
        mine = pl.ds(pl.multiple_of(c * half, 16), half)
        for j, (px, py) in enumerate(chips):
            k = 3 * t + j
            send = _remote(srcs[t].at[mine], lands[t].at[me, mine], ssem.at[k], rsem.at[k], (px, py, c))
            blk = lands[t].at[2 * px + py, mine]
            out.append((send, _remote(blk, blk, ssem.at[k], rsem.at[k], (px, py, c))))
    return out


def _scatter_copies(srcs, lands, ssem, rsem):
    x, y, c, chips = _place()
    out = []
    for t in range(len(srcs)):
        for j, (px, py) in enumerate(chips):
            k = 3 * t + j
            send = _remote(srcs[t].at[2 * px + py], lands[t].at[j], ssem.at[k], rsem.at[k], (px, py, c))
            out.append((send, _remote(lands[t].at[j], lands[t].at[j], ssem.at[k], rsem.at[k], (px, py, c))))
    return out


def _sibling_copies(srcs, lands, ssem, rsem):
    x, y, c, _ = _place()
    sib = (x, y, 1 - c)
    return [(_remote(srcs[t], lands[t], ssem.at[t], rsem.at[t], sib),
             _remote(lands[t], lands[t], ssem.at[t], rsem.at[t], sib)) for t in range(len(srcs))]


def _peer_copies(srcs, lands, ssem, rsem):
    x, y, c, _ = _place()
    me = 4 * x + 2 * y + c
    out = []
    for kk in range(1, 8):
        px, py, pc = x ^ (kk >> 2), y ^ ((kk >> 1) & 1), c ^ (kk & 1)
        send = _remote(srcs[0], lands[0].at[me], ssem.at[kk - 1], rsem.at[kk - 1], (px, py, pc))
        slot = lands[0].at[4 * px + 2 * py + pc]
        out.append((send, _remote(slot, slot, ssem.at[kk - 1], rsem.at[kk - 1], (px, py, pc))))
    return out


def _forward_copies(srcs, lands, ssem, rsem):
    x, y, c, chips = _place()
    sib = (x, y, 1 - c)
    out = []
    for t in range(len(lands)):
        half = lands[t].shape[1] // 2
        mine = pl.ds(pl.multiple_of(c * half, 16), half)
        other = pl.ds(pl.multiple_of((1 - c) * half, 16), half)
        for j, (px, py) in enumerate(chips):
            k = 3 * t + j
            blk_m, blk_o = lands[t].at[2 * px + py, mine], lands[t].at[2 * px + py, other]
            out.append((_remote(blk_m, blk_m, ssem.at[k], rsem.at[k], sib),
                        _remote(blk_o, blk_o, ssem.at[k], rsem.at[k], sib)))
    return out


def _win_and_small_copies(srcs, lands, ssem, rsem):
    return (_scatter_copies(srcs[:1], lands[:1], ssem, rsem)
            + _peer_copies(srcs[1:], lands[1:], ssem.at[pl.ds(3, 7)], rsem.at[pl.ds(3, 7)]))


def _split_start(plan, ncopy, srcs, lands, after, name):
    ns, nbuf = len(srcs), len(srcs) + len(lands)
    extra = [] if after is None else [after]
    n_in = nbuf + len(extra)

    def body(*refs):
        ssem, rsem, token = refs[n_in], refs[n_in + 1], refs[-1]
        for send, _ in plan(refs[:ns], refs[ns:nbuf], ssem, rsem):
            send.start()
        token[...] = jnp.zeros_like(token)

    bufs = [pltpu.with_memory_space_constraint(a, pltpu.HBM) for a in list(srcs) + list(lands)]
    outs = pl.pallas_call(
        body, name=name, in_specs=[HBM_SPEC] * nbuf + [ANY] * len(extra),
        out_specs=[SEM_SPEC, SEM_SPEC] + [HBM_SPEC] * nbuf + [pl.BlockSpec(memory_space=pltpu.VMEM)],
        out_shape=[pltpu.SemaphoreType.DMA((ncopy,)), pltpu.SemaphoreType.DMA((ncopy,))]
        + [pltpu.HBM(a.shape, a.dtype) for a in bufs] + [jax.ShapeDtypeStruct((8, 128), F32)],
        input_output_aliases={i: 2 + i for i in range(nbuf)},
        compiler_params=pltpu.CompilerParams(has_side_effects=DATAFLOW))(*bufs, *extra)
    return outs[0], outs[1], outs[2:2 + ns], outs[2 + ns:2 + nbuf], outs[-1]


def _split_wait(plan, ssem, rsem, srcs, lands, after, name, only=None):
    ns, nbuf = len(srcs), len(srcs) + len(lands)

    def body(*refs):
        s_ref, r_ref = refs[nbuf], refs[nbuf + 1]
        for k, (send, recv) in enumerate(plan(refs[:ns], refs[ns:nbuf], s_ref, r_ref)):
            if only is None or k in only:
                send.wait_send()
                recv.wait_recv()

    outs = pl.pallas_call(
        body, name=name, in_specs=[HBM_SPEC] * nbuf + [SEM_SPEC, SEM_SPEC] + [ANY] * len(after),
        out_specs=[HBM_SPEC] * nbuf,
        out_shape=[pltpu.HBM(a.shape, a.dtype) for a in list(srcs) + list(lands)],
        input_output_aliases={i: i for i in range(nbuf)},
        compiler_params=pltpu.CompilerParams(has_side_effects=DATAFLOW))(*srcs, *lands, ssem, rsem, *after)
    return outs


def _gather_finish(lands, tag):
    nt = len(lands)

    def body(*refs):
        outs = refs[nt:2 * nt]
        dsend, drecv = refs[2 * nt:]
        x, y, c, chips = _place()
        sib = (x, y, 1 - c)
        sends = []
        for t in range(nt):
            half = outs[t].shape[1] // 2
            mine = pl.ds(pl.multiple_of(c * half, 16), half)
            for j, (px, py) in enumerate(chips):
                blk = outs[t].at[2 * px + py, mine]
                cp = _remote(blk, blk, dsend.at[t, j], drecv.at[t, j], sib)
                cp.start()
                sends.append(cp)
        for t in range(nt):
            half = outs[t].shape[1] // 2
            other = pl.ds(pl.multiple_of((1 - c) * half, 16), half)
            for j, (px, py) in enumerate(chips):
                blk = outs[t].at[2 * px + py, other]
                _remote(blk, blk, dsend.at[t, j], drecv.at[t, j], sib).wait_recv()
        for cp in sends:
            cp.wait_send()

    return pl.pallas_call(
        body, name="gather_finish_" + tag, in_specs=[ANY] * nt, out_specs=[ANY] * nt,
        out_shape=[jax.ShapeDtypeStruct(a.shape, a.dtype) for a in lands],
        input_output_aliases={t: t for t in range(nt)},
        scratch_shapes=[pltpu.SemaphoreType.DMA((nt, 3)), pltpu.SemaphoreType.DMA((nt, 3))],
        compiler_params=_cp())(*lands)


def _chip_partial(owns, recv, chip_arr, tag):
    nt = len(owns)

    def body(chip_ref, *refs):
        k = pl.program_id(0)
        for t in range(nt):
            p = refs[t][...] + refs[nt + t][...].astype(F32)
            refs[2 * nt + t][...] = p.astype(BF16)

            @pl.when(k == chip_ref[0])
            def _():
                refs[3 * nt + t][...] = p

    blk_specs = [pl.BlockSpec((None,) + a.shape[1:], lambda k, chip_ref: (k, 0, 0)) for a in owns]
    own_specs = [pl.BlockSpec(a.shape[1:], lambda k, chip_ref: (0, 0)) for a in owns]
    return pl.pallas_call(
        body, name="rs_chip_partial_" + tag,
        grid_spec=pltpu.PrefetchScalarGridSpec(num_scalar_prefetch=1, grid=(NSH,), in_specs=blk_specs * 2,
                                               out_specs=blk_specs + own_specs),
        out_shape=[jax.ShapeDtypeStruct(a.shape, BF16) for a in owns]
        + [jax.ShapeDtypeStruct(a.shape[1:], F32) for a in owns],
        compiler_params=_cp(1))(chip_arr, *owns, *recv)


def _sum_chips(own, recv, tag):
    nt = len(own)

    def body(*refs):
        outs = refs[2 * nt:]
        for t in range(nt):
            r = refs[nt + t]
            outs[t][...] = ((refs[t][...] + r[0].astype(F32)) + r[1].astype(F32)) + r[2].astype(F32)

    vm = pl.BlockSpec(memory_space=pltpu.VMEM)
    return pl.pallas_call(
        body, name="rs_sum_chips_" + tag, in_specs=[vm] * (2 * nt), out_specs=[vm] * nt,
        out_shape=[jax.ShapeDtypeStruct(a.shape, F32) for a in own],
        compiler_params=_cp())(*own, *recv)


def _sum_chips_shared(own, recv, tag):
    def body(own_ref, recv_ref, out_ref, sib_ref, ssem, rsem):
        out_ref[...] = ((own_ref[...] + recv_ref[0].astype(F32)) + recv_ref[1].astype(F32)) + recv_ref[2].astype(F32)
        x, y, c, _ = _place()
        cp = _remote(out_ref, sib_ref, ssem.at[0], rsem.at[0], (x, y, 1 - c))
        cp.start()
        cp.wait()

    vm = pl.BlockSpec(memory_space=pltpu.VMEM)
    return pl.pallas_call(
        body, name="rs_sum_share_" + tag, in_specs=[vm, vm], out_specs=[vm, ANY],
        out_shape=[jax.ShapeDtypeStruct(own.shape, F32)] * 2,
        scratch_shapes=[pltpu.SemaphoreType.DMA((1,)), pltpu.SemaphoreType.DMA((1,))],
        compiler_params=_cp())(own, recv)


def _adamw_math(w, g, m, v):
    m = ADAM_B1 * m + (1.0 - ADAM_B1) * g
    v = ADAM_B2 * v + (1.0 - ADAM_B2) * (g * g)
    m_hat = m / (1.0 - ADAM_B1 ** ADAM_STEP)
    v_hat = v / (1.0 - ADAM_B2 ** ADAM_STEP)
    delta = -ADAM_LR * (m_hat / (jnp.sqrt(v_hat) + ADAM_EPS) + ADAM_WD * w)
    return delta, m, v


ADAM_SPLIT = 4


def _adamw_shards(own, sib, ws, ms, vs, c_arr, tag):
    nt = len(own)

    def body(c_ref, *refs):
        hh = pl.program_id(0)
        mine = hh == c_ref[0]
        for t in range(nt):
            g = jnp.where(mine, refs[t][...], refs[nt + t][...])
            delta, m, v = _adamw_math(refs[2 * nt + t][...], g, refs[3 * nt + t][...], refs[4 * nt + t][...])
            refs[5 * nt + t][...] = g
            refs[6 * nt + t][...] = delta
            refs[7 * nt + t][...] = m
            refs[8 * nt + t][...] = v

    def tile(a):
        return (a.shape[0] // ADAM_SPLIT, a.shape[1])

    def half_index(used_when_mine):
        def index(hh, q, c_ref):
            mine = 1 - (hh - c_ref[0]) * (hh - c_ref[0])
            used = mine if used_when_mine else 1 - mine
            return (used * q + (1 - used) * hh * (ADAM_SPLIT - 1), 0)
        return index

    own_specs = [pl.BlockSpec(tile(a), half_index(True)) for a in own]
    sib_specs = [pl.BlockSpec(tile(a), half_index(False)) for a in own]
    full_specs = [pl.BlockSpec(tile(a), lambda hh, q, c_ref: (hh * ADAM_SPLIT + q, 0)) for a in own]
    return pl.pallas_call(
        body, name="adamw_shards_" + tag,
        grid_spec=pltpu.PrefetchScalarGridSpec(num_scalar_prefetch=1, grid=(2, ADAM_SPLIT),
                                               in_specs=own_specs + sib_specs + full_specs * 3,
                                               out_specs=full_specs * 4),
        out_shape=[jax.ShapeDtypeStruct(w.shape, F32) for w in ws] * 4,
        compiler_params=_cp(2))(c_arr, *own, *sib, *ws, *ms, *vs)


SMALL_WPOOL_ROW = 32
SMALL_SCALE_ROW = SMALL_WPOOL_ROW + 4 * GROUP
SMALL_BIAS_ROW = SMALL_SCALE_ROW + 8
SMALL_SINKS_ROW = SMALL_BIAS_ROW + NQ
SMALL_LOSS_ROW = SMALL_SINKS_ROW + 8


def _small_sum_adamw(gathered, wp, mp, vp):
    rows = wp.shape[0]

    def body(g_ref, w_ref, m_ref, v_ref, *rest):
        outs, res = rest[:-1], rest[-1]
        g = g_ref[0]
        for d in range(1, 8):
            g = g + g_ref[d]
        delta, m, v = _adamw_math(w_ref[...], g, m_ref[...], v_ref[...])
        for kind, val in enumerate((g, delta, m, v)):
            res[kind] = val
            gains = outs[8 * kind:8 * kind + 4]
            w_pool_o, scale_o, bias_o, sinks_o = outs[8 * kind + 4:8 * kind + 8]
            for t in range(4):
                for i in range(8):
                    gains[t][:, 128 * i:128 * (i + 1)] = res[kind, pl.ds(8 * t + i, 1), :]
            for grp in range(4):
                w_pool_o[grp] = res[kind, pl.ds(SMALL_WPOOL_ROW + GROUP * grp, GROUP), :]
            for i in range(4):
                scale_o[:, 128 * i:128 * (i + 1)] = res[kind, pl.ds(SMALL_SCALE_ROW + i, 1), :]
            bias_o[...] = res[kind, pl.ds(SMALL_BIAS_ROW, NQ), :][:, 0:NBUCKET]
            sinks_o[...] = res[kind, pl.ds(SMALL_SINKS_ROW, 1), :][:, 0:NQ]
        outs[-1][...] = res[0, pl.ds(SMALL_LOSS_ROW, 1), :]

    vm = pl.BlockSpec(memory_space=pltpu.VMEM)
    one_kind = [jax.ShapeDtypeStruct((1, D), F32)] * 4 + [
        jax.ShapeDtypeStruct((4, GROUP, GROUP), F32), jax.ShapeDtypeStruct((1, POOL_W), F32),
        jax.ShapeDtypeStruct((NQ, NBUCKET), F32), jax.ShapeDtypeStruct((1, NQ), F32)]
    return pl.pallas_call(
        body, name="small_sum_adamw", in_specs=[vm] * 4, out_specs=[vm] * 33,
        out_shape=one_kind * 4 + [jax.ShapeDtypeStruct((1, 128), F32)],
        scratch_shapes=[pltpu.VMEM((4, rows, 128), F32)],
        compiler_params=_cp())(gathered, wp, mp, vp)


def _pack_small(gains4, w_pool, pool_scale, rel_bias_t, sinks, loss):
    bias_rows = jnp.pad(rel_bias_t, ((0, 0), (0, 128 - rel_bias_t.shape[1])))
    parts = list(gains4) + [w_pool, pool_scale, bias_rows, sinks, loss]
    rows = []
    for a in parts:
        flat = a.reshape(-1)
        n = flat.shape[0]
        padded = -(-n // 1024) * 1024
        rows.append(jnp.pad(flat, (0, padded - n)).reshape(-1, 128))
    return jnp.concatenate(rows, axis=0)


def kernel(x, g_pre_mix, w_in, w_pool, pool_scale, rel_bias, sinks, w_out, g_post_mix, g_pre_ffn, w_gate, w_up, w_down, g_post_ffn, loss_target, m_g_pre_mix, m_w_in, m_w_pool, m_pool_scale, m_rel_bias, m_sinks, m_w_out, m_g_post_mix, m_g_pre_ffn, m_w_gate, m_w_up, m_w_down, m_g_post_ffn, v_g_pre_mix, v_w_in, v_w_pool, v_pool_scale, v_rel_bias, v_sinks, v_w_out, v_g_post_mix, v_g_pre_ffn, v_w_gate, v_w_up, v_w_down, v_g_post_ffn):
    c = lax.axis_index("c")
    chip = 2 * lax.axis_index("x") + lax.axis_index("y")

    def shards(a_in, a_out, a_gate, a_up, a_down):
        return (a_in[0].T, a_out[0], a_gate[0].T, a_up[0].T, a_down[0])

    c_arr = c.reshape(1).astype(jnp.int32)
    chip_arr = chip.reshape(1).astype(jnp.int32)

    big_w = shards(w_in, w_out, w_gate, w_up, w_down)
    big_bf = [w.astype(BF16) for w in big_w]
    g_ssem, g_rsem, g_srcs, g_lands, g_token = _split_start(
        _gather_copies, 15, big_bf, [lax.empty((NSH,) + a.shape, BF16) for a in big_bf], None, "gather_start")
    fw = {}

    def with_own(land, shard):
        return lax.dynamic_update_slice(land, shard[None], (chip, 0, 0))

    def w_in_ready(*after):
        first = fw["first"] = _split_wait(_gather_copies, g_ssem, g_rsem, g_srcs, g_lands, after, "gather_win_wait",
                                          only=(0, 1, 2, 3, 4, 5))
        fw["win"], fw["wout"] = _gather_finish([with_own(first[5], first[0]), with_own(first[6], first[1])], "win")
        return fw["win"].reshape(IN_W, D)

    def w_out_ready(*after):
        return fw["wout"].reshape(D, D), None

    def ffn_forward_start(after):
        first = fw["first"]
        outs = _split_wait(_gather_copies, g_ssem, g_rsem, first[:5], [fw["win"], fw["wout"]] + list(first[7:]),
                           [after], "gather_ffn_wait", only=tuple(range(6, 15)))
        lands = [with_own(land, src) for src, land in zip(outs[2:5], outs[7:])]
        fw["f"] = _split_start(_forward_copies, 9, [], lands, None, "gather_forward_start")
        return fw["f"][4]

    def ffn_weights(after):
        ssem, rsem, _, lands, _ = fw["f"]
        return _split_wait(_forward_copies, ssem, rsem, [], lands, [after], "gather_forward_wait")

    rs = {}

    def on_ffn_grads(ffn_g):
        oths = ffn_g[1::2]
        rs["ffn_own"] = ffn_g[0::2]
        rs["x"] = _split_start(_sibling_copies, 3, oths, [lax.empty(a.shape, BF16) for a in oths], ffn_g[0],
                               "rs_exchange_ffn_start")
        return rs["x"][4]

    def on_wout_grads(own, oth, after):
        ssem, rsem, srcs, lands, _ = rs["x"]
        recv_ffn = _split_wait(_sibling_copies, ssem, rsem, srcs, lands, [after], "rs_exchange_ffn_wait")[3:]
        recv_wout = _to_sibling([oth], "rs_exchange_wout")
        outs = _chip_partial([own] + list(rs["ffn_own"]), list(recv_wout) + list(recv_ffn), chip_arr, "early")
        rs["early_own"] = outs[4:]
        rs["s"] = _split_start(_scatter_copies, 12, outs[:4],
                               [lax.empty((3,) + a.shape[1:], BF16) for a in outs[:4]], outs[4], "scatter_early_start")
        return rs["s"][4]

    small = (g_pre_mix + g_token[:1, :1], g_post_mix, g_pre_ffn, g_post_ffn, w_pool[0], pool_scale, rel_bias, sinks)
    loss, gx, small_grads, ((win_own, win_oth), _, _) = _local_step(
        x[0], loss_target[0], small, w_in_ready, w_out_ready, ffn_weights, c_arr, on_ffn_grads, on_wout_grads,
        ffn_forward_start)

    ssem, rsem, srcs, lands, _ = rs["s"]
    recv_early = _split_wait(_scatter_copies, ssem, rsem, srcs, lands, [gx], "scatter_early_wait")[4:]
    finals = _sum_chips(list(rs["early_own"]), list(recv_early), "early")
    to_sib = [win_oth] + list(finals)
    sh_ssem, sh_rsem, sh_srcs, sh_lands, _ = _split_start(
        _sibling_copies, 5, to_sib, [lax.empty(a.shape, a.dtype) for a in to_sib], None, "rs_share_start")

    unused = jnp.zeros((1, 1), F32)
    gp = _pack_small(small_grads[:4], *small_grads[4:], loss)
    wp = _pack_small((g_pre_mix, g_post_mix, g_pre_ffn, g_post_ffn), w_pool, pool_scale, rel_bias.T, sinks, unused)
    mp = _pack_small((m_g_pre_mix, m_g_post_mix, m_g_pre_ffn, m_g_post_ffn), m_w_pool, m_pool_scale, m_rel_bias.T,
                     m_sinks, unused)
    vp = _pack_small((v_g_pre_mix, v_g_post_mix, v_g_pre_ffn, v_g_post_ffn), v_w_pool, v_pool_scale, v_rel_bias.T,
                     v_sinks, unused)

    moments = (shards(m_w_in, m_w_out, m_w_gate, m_w_up, m_w_down),
               shards(v_w_in, v_w_out, v_w_gate, v_w_up, v_w_down))
    sh_first = _split_wait(_sibling_copies, sh_ssem, sh_rsem, sh_srcs, sh_lands, [], "rs_share_win_wait", only=(0,))
    outs = _chip_partial([win_own], [sh_first[5]], chip_arr, "win")
    slots = lax.dynamic_update_slice(lax.empty((8,) + gp.shape, F32), gp[None], (2 * chip + c, 0, 0))
    w_ssem, w_rsem, w_srcs, w_lands, w_token = _split_start(
        _win_and_small_copies, 10, [outs[0], gp], [lax.empty((3,) + outs[0].shape[1:], BF16), slots], gx,
        "scatter_win_start")
    shared = _split_wait(_sibling_copies, sh_ssem, sh_rsem, sh_first[:5], sh_first[5:], [w_token],
                         "rs_share_early_wait", only=(1, 2, 3, 4))
    adam_e = _adamw_shards(shared[1:5], shared[6:], big_w[1:], moments[0][1:], moments[1][1:], c_arr, "early")
    w_first = _split_wait(_win_and_small_copies, w_ssem, w_rsem, w_srcs, w_lands, [adam_e[0]], "scatter_win_wait",
                          only=(0, 1, 2))
    win_final, win_sib = _sum_chips_shared(outs[1], w_first[2], "win")
    adam_w = _adamw_shards([win_final], [win_sib], big_w[:1], moments[0][:1], moments[1][:1], c_arr, "win")
    big_g, big_d, big_m, big_v = [[adam_w[i]] + list(adam_e[4 * i:4 * i + 4]) for i in range(4)]

    gathered = _split_wait(_win_and_small_copies, w_ssem, w_rsem, w_first[:2], w_first[2:], [adam_w[0]],
                           "small_allgather_wait", only=tuple(range(3, 10)))[3]
    flat = _small_sum_adamw(gathered, wp, mp, vp)
    small_out = [flat[8 * kind:8 * kind + 8] for kind in range(4)]
    total_loss = flat[-1][0, 0]

    def ordered(small8, big4):
        sg1, sg2, sg3, sg4, swp, ssc, srb, ssk = small8
        swp, srb = swp[None], srb.T
        bwin, bwout, bwg, bwu, bwd = big4[0].T[None], big4[1][None], big4[2].T[None], big4[3].T[None], big4[4][None]
        return [sg1, bwin, swp, ssc, srb, ssk, bwout, sg2, sg3, bwg, bwu, bwd, sg4]

    res = [total_loss, gx[None]]
    for sm, bg in zip(small_out, (big_g, big_d, big_m, big_v)):
        res += ordered(sm, bg)
    return tuple(res)
```

```python
import functools

import numpy as np
import jax
import jax.numpy as jnp
from jax import lax
from jax.experimental import pallas as pl
from jax.experimental.pallas import tpu as pltpu

F32 = jnp.float32
BF16 = jnp.bfloat16

D = 1024
POOL_W = 512
GROUP = 128
WINDOWS = (2, 4, 8, 16)
HALO = 128
NQ = 8
BLK = 128
NBUCKET = 32
IN_W = 1280
DFF = 2816
NSH = 4
FS = DFF // NSH
WIN_S = IN_W // NSH
WOUT_S = D // NSH
EPS = 1e-6
NEG = -1e30
SCALE = 0.125

ADAM_LR = 0.001
ADAM_B1 = 0.9
ADAM_B2 = 0.999
ADAM_EPS = 1e-08
ADAM_WD = 0.01
ADAM_STEP = 10

TM = 512
TM_IN = 1024
TM_POOL = 512
TM_FFN = 512
TM_FFN_FWD = 1024
FFN_ROWS = 256
VMEM_LIMIT = 60 * 1024 * 1024

MESH_T = pl.DeviceIdType.MESH
ANY = pl.BlockSpec(memory_space=pl.ANY)


def _cp(n_grid=0, **kw):
    sem = ("arbitrary",) * n_grid if n_grid else None
    return pltpu.CompilerParams(dimension_semantics=sem, vmem_limit_bytes=VMEM_LIMIT, **kw)


def _dot(a, b):
    return jnp.dot(a, b, preferred_element_type=F32)


def _dot_nt(a, b):
    return lax.dot_general(a, b, (((1,), (1,)), ((), ())), preferred_element_type=F32)


def _dot_tn(a, b):
    return lax.dot_general(a, b, (((0,), (0,)), ((), ())), preferred_element_type=F32)


def _rms(x):
    r = lax.rsqrt(jnp.mean(x * x, axis=-1, keepdims=True) + EPS)
    return r, x * r


def _rms_bwd(r, n, g, dout):
    dn = dout * g
    dx = r * (dn - n * jnp.mean(dn * n, axis=-1, keepdims=True))
    dg = jnp.sum(dout * n, axis=0, keepdims=True)
    return dx, dg


def _split(x, n):
    parts = []
    r = x
    for _ in range(n):
        p = r.astype(BF16)
        parts.append(p)
        r = r - p.astype(F32)
    return parts


def _band_dot(a, x, n):
    acc = None
    for p in _split(x, n):
        t = _dot(a, p)
        acc = t if acc is None else acc + t
    return acc


def _bucket_table():
    qi = np.arange(BLK)[None, :]
    kj = np.arange(2 * BLK)[:, None]
    dist = qi + BLK - kj
    n = np.maximum(dist, 0)
    nf = np.maximum(n, 1).astype(np.float32)
    large = 16 + (np.log(nf / np.float32(16)) / np.float32(np.log(128 / 16)) * np.float32(16)).astype(np.int32)
    large = np.minimum(large, NBUCKET - 1)
    return np.where(n < 16, n, large).astype(np.int32)


def _prenorm(x, g1):
    s = x.shape[0]
    tm = min(TM_IN, s)

    def body(x_ref, g_ref, h_ref):
        _, n = _rms(x_ref[...])
        h_ref[...] = (n * g_ref[...]).astype(BF16)

    row = pl.BlockSpec((tm, D), lambda i: (i, 0))
    return pl.pallas_call(
        body, name="prenorm", grid=(s // tm,),
        in_specs=[row, pl.BlockSpec((1, D), lambda i: (0, 0))], out_specs=row,
        out_shape=jax.ShapeDtypeStruct((s, D), BF16),
        compiler_params=_cp(1))(x, g1)


def _inproj_fwd(h1, w_in):
    s = h1.shape[0]
    tm = min(TM_IN, s)

    def body(h_ref, w_ref, u_ref, q_ref, k_ref, v_ref):
        proj = _dot_nt(h_ref[...], w_ref[...])
        u_ref[...] = proj[:, :512]
        q_ref[...] = proj[:, 512:1024].astype(BF16)
        k_ref[...] = proj[:, 1024:1152].astype(BF16)
        v_ref[...] = proj[:, 1152:1280].astype(BF16)

    row = lambda w: pl.BlockSpec((tm, w), lambda i: (i, 0))
    return pl.pallas_call(
        body, name="inproj_fwd", grid=(s // tm,),
        in_specs=[row(D), pl.BlockSpec((IN_W, D), lambda i: (0, 0))],
        out_specs=[row(512), row(512), row(128), row(128)],
        out_shape=[jax.ShapeDtypeStruct((s, 512), F32), jax.ShapeDtypeStruct((s, 512), BF16),
                   jax.ShapeDtypeStruct((s, 128), BF16), jax.ShapeDtypeStruct((s, 128), BF16)],
        compiler_params=_cp(1))(h1, w_in)


def _window_sums(ext, w, lag_sign, tm, terms):
    rows = lax.broadcasted_iota(jnp.int32, (HALO, 2 * HALO), 0)
    cols = lax.broadcasted_iota(jnp.int32, (HALO, 2 * HALO), 1)
    d = rows + HALO - cols if lag_sign > 0 else cols - rows
    band = jnp.where((d >= 0) & (d < w), 1.0, 0.0).astype(BF16)
    return jnp.concatenate([_band_dot(band, ext[r:r + 2 * HALO], terms) for r in range(0, tm, HALO)], axis=0)


def _pooled(ext, cur, t0, tm):
    t = t0 + lax.broadcasted_iota(jnp.int32, (tm, GROUP), 0)
    out = []
    for g, w in enumerate(WINDOWS):
        sl = slice(g * GROUP, (g + 1) * GROUP)
        cnt = jnp.minimum(t + 1, w).astype(F32)
        out.append(_window_sums(ext[:, sl], w, 1, tm, 2) / cnt - cur[:, sl])
    return out


def _pool_fwd(u, w_pool, pool_scale):
    s = u.shape[0]
    tm = min(TM_POOL, s)
    hb = tm // HALO

    def body(uc_ref, uh_ref, wp_ref, sc_ref, o_ref, pooled_ref):
        i = pl.program_id(0)
        cur = uc_ref[...]
        halo = jnp.where(i > 0, uh_ref[...], 0.0)
        ext = jnp.concatenate([halo, cur], axis=0)
        pooled = _pooled(ext, cur, i * tm, tm)
        for g in range(4):
            sl = slice(g * GROUP, (g + 1) * GROUP)
            pb = pooled[g].astype(BF16)
            pooled_ref[:, sl] = pb
            o_ref[:, sl] = (_dot(pb, wp_ref[g]) * sc_ref[:, sl]).astype(BF16)

    row = pl.BlockSpec((tm, 512), lambda i: (i, 0))
    return pl.pallas_call(
        body, name="pool_fwd", grid=(s // tm,),
        in_specs=[row, pl.BlockSpec((HALO, 512), lambda i: (jnp.maximum(i * hb - 1, 0), 0)),
                  pl.BlockSpec((4, GROUP, GROUP), lambda i: (0, 0, 0)),
                  pl.BlockSpec((1, 512), lambda i: (0, 0))],
        out_specs=[row, row],
        out_shape=[jax.ShapeDtypeStruct((s, 512), BF16)] * 2,
        compiler_params=_cp(1))(u, u, w_pool, pool_scale)


def _bias_table(bucket, rel_bias):
    def body(b_ref, rb_ref, o_ref):
        bucket_v = b_ref[...]
        key = lax.broadcasted_iota(jnp.int32, (2 * BLK, BLK), 0)
        dist = lax.broadcasted_iota(jnp.int32, (2 * BLK, BLK), 1) + BLK - key
        inwin = (dist >= 0) & (dist < BLK)
        for h in range(NQ):
            acc = jnp.zeros((2 * BLK, BLK), F32)
            for b in range(NBUCKET):
                acc = jnp.where(bucket_v == b, rb_ref[b, h], acc)
            rest = jnp.where(inwin, acc, NEG)
            o_ref[1, h] = rest
            o_ref[0, h] = jnp.where(key >= BLK, rest, NEG)

    return pl.pallas_call(
        body, name="bias_table",
        in_specs=[pl.BlockSpec(memory_space=pltpu.VMEM), pl.BlockSpec(memory_space=pltpu.SMEM)],
        out_specs=pl.BlockSpec(memory_space=pltpu.VMEM),
        out_shape=jax.ShapeDtypeStruct((2, NQ, 2 * BLK, BLK), F32),
        compiler_params=_cp())(bucket, rel_bias)


HD = 64


def _kv_band(ref, n, transposed):
    pstart = pl.multiple_of(jnp.maximum(n - 1, 0) * BLK, BLK)
    cstart = pl.multiple_of(n * BLK, BLK)
    lo = lax.broadcasted_iota(jnp.int32, (2 * BLK, 128), 1) < HD
    band = jnp.concatenate([ref[pl.ds(pstart, BLK), :], ref[pl.ds(cstart, BLK), :]], axis=0).astype(F32)
    rot = pltpu.roll(band, HD, axis=1)
    halves = ((jnp.where(lo, band, 0.0), jnp.where(lo, 0.0, rot)), (jnp.where(lo, rot, 0.0), jnp.where(lo, 0.0, band)))
    if transposed:
        out = [jnp.concatenate([a.T, b.T], axis=1).astype(BF16) for a, b in halves]
    else:
        out = [jnp.concatenate([a, b], axis=0).astype(BF16) for a, b in halves]
    return out, (lo, pstart, cstart)


def _pair_probs(qt, kk, bias, sk_ref, p):
    sink = jnp.concatenate([jnp.full((1, 1, BLK), sk_ref[0, 2 * p + e], F32) for e in range(2)], axis=0)
    s = _dot_nt(kk, qt).reshape(2, 2 * BLK, BLK) + bias
    m = jnp.maximum(jnp.max(s, axis=1, keepdims=True), sink)
    pr = jnp.exp(s - m)
    es = jnp.exp(sink - m)
    inv = 1.0 / (jnp.sum(pr, axis=1, keepdims=True) + es)
    return pr * inv, es * inv


def _attn_fwd(q, k, v, bias, sinks):
    s = q.shape[0]
    nblk = s // BLK

    def body(q_ref, k_ref, v_ref, b_ref, sk_ref, o_ref, prob_ref, ps_ref):
        n = pl.program_id(0)
        kk, _ = _kv_band(k_ref, n, False)
        vt, _ = _kv_band(v_ref, n, True)
        bidx = jnp.minimum(n, 1)
        for p in range(4):
            h = p // 2
            qt = (q_ref[:, p * 128:(p + 1) * 128].astype(F32) * SCALE).astype(BF16)
            prob, ps = _pair_probs(qt, kk[h], b_ref[bidx, pl.ds(2 * p, 2)], sk_ref, p)
            pb = prob.astype(BF16)
            prob_ref[pl.ds(2 * p, 2)] = pb
            ps_ref[pl.ds(2 * p, 2), :] = ps.reshape(2, BLK)
            acc_t = _dot(vt[h], pb.reshape(4 * BLK, BLK))
            o_ref[:, p * 128:(p + 1) * 128] = acc_t.T.astype(BF16)

    return pl.pallas_call(
        body, name="attn_fwd", grid=(nblk,),
        in_specs=[pl.BlockSpec((BLK, 512), lambda i: (i, 0)),
                  pl.BlockSpec((s, 128), lambda i: (0, 0)),
                  pl.BlockSpec((s, 128), lambda i: (0, 0)),
                  pl.BlockSpec((2, NQ, 2 * BLK, BLK), lambda i: (0, 0, 0, 0)),
                  pl.BlockSpec(memory_space=pltpu.SMEM)],
        out_specs=[pl.BlockSpec((BLK, 512), lambda i: (i, 0)),
                   pl.BlockSpec((None, NQ, 2 * BLK, BLK), lambda i: (i, 0, 0, 0)),
                   pl.BlockSpec((None, NQ, BLK), lambda i: (i, 0, 0))],
        out_shape=[jax.ShapeDtypeStruct((s, 512), BF16), jax.ShapeDtypeStruct((nblk, NQ, 2 * BLK, BLK), BF16),
                   jax.ShapeDtypeStruct((nblk, NQ, BLK), F32)],
        compiler_params=_cp(1))(q, k, v, bias, sinks)


def _outproj_fwd(pool_o, attn_o, w_out, x, g2, g3, between=None):
    s = x.shape[0]
    tm = min(TM, s)
    nt = s // tm

    def part(name, tile0, tiles, filled, dep):
        def body(p_ref, a_ref, w_ref, x_ref, g2_ref, g3_ref, *rest):
            mix_ref, x1_ref, h2_ref = rest[-3:]
            mix = _dot(p_ref[...], w_ref[0:512, :]) + _dot(a_ref[...], w_ref[512:1024, :])
            mix_ref[...] = mix
            _, n2 = _rms(mix)
            x1 = x_ref[...] + n2 * g2_ref[...]
            x1_ref[...] = x1
            _, n3 = _rms(x1)
            h2_ref[...] = (n3 * g3_ref[...]).astype(BF16)

        row = lambda w: pl.BlockSpec((tm, w), lambda i: (i + tile0, 0))
        vec = pl.BlockSpec((1, D), lambda i: (0, 0))
        extra = list(filled) + ([] if dep is None else [dep])
        return pl.pallas_call(
            body, name=name, grid=(tiles,),
            in_specs=[row(512), row(512), pl.BlockSpec((D, D), lambda i: (0, 0)), row(D), vec, vec]
            + [ANY] * len(extra),
            out_specs=[row(D), row(D), row(D)],
            out_shape=[jax.ShapeDtypeStruct((s, D), F32), jax.ShapeDtypeStruct((s, D), F32),
                       jax.ShapeDtypeStruct((s, D), BF16)],
            input_output_aliases={6 + t: t for t in range(len(filled))},
            compiler_params=_cp(1))(pool_o, attn_o, w_out, x, g2, g3, *extra)

    if between is None or nt < 2:
        return part("outproj_fwd", 0, nt, (), None)
    first = part("outproj_fwd_a", 0, nt // 2, (), None)
    return part("outproj_fwd_b", nt // 2, nt - nt // 2, first, between(first[2]))


def _silu_parts(g):
    sg = jax.nn.sigmoid(g)
    return sg, g * sg


def _ffn_fwd(h2, wg, wu, wd, x1, target, g4):
    s = h2.shape[0]
    tm = min(TM_FFN_FWD, s)

    def body(h_ref, wg_ref, wu_ref, wd_ref, x1_ref, t_ref, g4_ref,
             gate_ref, up_ref, a_ref, df_ref, dy_ref, loss_ref, gg4_ref, f_acc):
        i = pl.program_id(0)
        j = pl.program_id(1)
        first = j == 0
        chunks = [pl.ds(r, min(FFN_ROWS, tm)) for r in range(0, tm, FFN_ROWS)]
        project = lambda rows: (_dot_nt(h_ref[rows, :], wg_ref[...]), _dot_nt(h_ref[rows, :], wu_ref[...]))
        ahead = [project(chunks[0])]
        for k, rows in enumerate(chunks):
            if k + 1 < len(chunks):
                ahead.append(project(chunks[k + 1]))
            gate, up = ahead[k]
            gate_ref[rows, :] = gate.astype(BF16)
            up_ref[rows, :] = up.astype(BF16)
            _, sl = _silu_parts(gate)
            a = (sl * up).astype(BF16)
            a_ref[rows, :] = a
            contrib = _dot(a, wd_ref[...])
            f_acc[rows, :] = jnp.where(first, contrib, f_acc[rows, :] + contrib)

        @pl.when((i == 0) & (j == 0))
        def _():
            loss_ref[...] = jnp.zeros_like(loss_ref)
            gg4_ref[...] = jnp.zeros_like(gg4_ref)

        @pl.when(j == NSH - 1)
        def _():
            r4, n4 = _rms(f_acc[...])
            g4v = g4_ref[...]
            err = x1_ref[...] + n4 * g4v - t_ref[...]
            loss_ref[...] += (0.5 / D) * jnp.sum(err * err).reshape(1, 1)
            dy = err * (1.0 / D)
            dy_ref[...] = dy
            df, dg = _rms_bwd(r4, n4, g4v, dy)
            gg4_ref[...] += dg
            df_ref[...] = df.astype(BF16)

    row = lambda w: pl.BlockSpec((tm, w), lambda i, j: (i, 0))
    sh = lambda r, c: pl.BlockSpec((None, r, c), lambda i, j: (j, 0, 0))
    act = pl.BlockSpec((None, tm, FS), lambda i, j: (j, i, 0))
    vec = pl.BlockSpec((1, D), lambda i, j: (0, 0))
    return pl.pallas_call(
        body, name="ffn_fwd", grid=(s // tm, NSH),
        in_specs=[row(D), sh(FS, D), sh(FS, D), sh(FS, D), row(D), row(D), vec],
        out_specs=[act, act, act, row(D), row(D), pl.BlockSpec((1, 1), lambda i, j: (0, 0)), vec],
        out_shape=[jax.ShapeDtypeStruct((NSH, s, FS), BF16)] * 3
        + [jax.ShapeDtypeStruct((s, D), BF16), jax.ShapeDtypeStruct((s, D), F32),
           jax.ShapeDtypeStruct((1, 1), F32), jax.ShapeDtypeStruct((1, D), F32)],
        scratch_shapes=[pltpu.VMEM((tm, D), F32)],
        compiler_params=_cp(2))(h2, wg, wu, wd, x1, target, g4)


def _ffn_bwd_act(df, gate, up, wg, wu, wd, dy, x1, mix, g3, g2):
    s = df.shape[0]
    tm = min(TM_FFN, s)

    def body(df_ref, gate_ref, up_ref, wg_ref, wu_ref, wd_ref, dy_ref, x1_ref, mix_ref, g3_ref, g2_ref,
             dgate_ref, dup_ref, dx1_ref, dmix_ref, gg3_ref, gg2_ref, acc):
        j = pl.program_id(0)
        i = pl.program_id(1)
        first = j == 0
        tile = pl.multiple_of(i * tm, tm)
        chunks = [pl.ds(r, min(FFN_ROWS, tm)) for r in range(0, tm, FFN_ROWS)]

        @pl.when((i == 0) & (j == 0))
        def _():
            gg3_ref[...] = jnp.zeros_like(gg3_ref)
            gg2_ref[...] = jnp.zeros_like(gg2_ref)

        def norms_backward(rows, dh2):
            r3, n3 = _rms(x1_ref[rows, :])
            dx1n, dg3 = _rms_bwd(r3, n3, g3_ref[...], dh2)
            dx1 = dy_ref[rows, :] + dx1n
            dx1_ref[rows, :] = dx1
            gg3_ref[...] += dg3
            r2, n2 = _rms(mix_ref[rows, :])
            dmix, dg2 = _rms_bwd(r2, n2, g2_ref[...], dx1)
            gg2_ref[...] += dg2
            dmix_ref[rows, :] = dmix.astype(BF16)

        def shard_pass(last):
            das = [_dot_nt(df_ref[chunks[0], :], wd_ref[...])]
            for k, rows in enumerate(chunks):
                if k + 1 < len(chunks):
                    das.append(_dot_nt(df_ref[chunks[k + 1], :], wd_ref[...]))
                da = das[k]
                g = gate_ref[rows, :].astype(F32)
                u = up_ref[rows, :].astype(F32)
                sg, sl = _silu_parts(g)
                dgate = (da * u * (sg * (1.0 + g * (1.0 - sg)))).astype(BF16)
                dup = (da * sl).astype(BF16)
                dgate_ref[rows, :] = dgate
                dup_ref[rows, :] = dup
                contrib = _dot(dgate, wg_ref[...]) + _dot(dup, wu_ref[...])
                acc_rows = pl.ds(tile + k * FFN_ROWS, rows.size)
                if last:
                    norms_backward(rows, acc[acc_rows, :] + contrib)
                else:
                    acc[acc_rows, :] = jnp.where(first, contrib, acc[acc_rows, :] + contrib)

        pl.when(j < NSH - 1)(functools.partial(shard_pass, False))
        pl.when(j == NSH - 1)(functools.partial(shard_pass, True))

    row = pl.BlockSpec((tm, D), lambda j, i: (i, 0))
    last = pl.BlockSpec((tm, D), lambda j, i: (i * (j // (NSH - 1)), 0))
    sh = pl.BlockSpec((None, FS, D), lambda j, i: (j, 0, 0))
    act = pl.BlockSpec((None, tm, FS), lambda j, i: (j, i, 0))
    vec = pl.BlockSpec((1, D), lambda j, i: (0, 0))
    return pl.pallas_call(
        body, name="ffn_bwd_act", grid=(NSH, s // tm),
        in_specs=[row, act, act, sh, sh, sh, last, last, last, vec, vec],
        out_specs=[act, act, last, last, vec, vec],
        out_shape=[jax.ShapeDtypeStruct((NSH, s, FS), BF16), jax.ShapeDtypeStruct((NSH, s, FS), BF16),
                   jax.ShapeDtypeStruct((s, D), F32), jax.ShapeDtypeStruct((s, D), BF16),
                   jax.ShapeDtypeStruct((1, D), F32), jax.ShapeDtypeStruct((1, D), F32)],
        scratch_shapes=[pltpu.VMEM((s, D), F32)],
        compiler_params=_cp(2))(df, gate, up, wg, wu, wd, dy, x1, mix, g3, g2)


SMEM_SPEC = pl.BlockSpec(memory_space=pltpu.SMEM)


def _emit_halves(acc_ref, row0, rows, c, own_ref, oth_ref):
    half = rows // 2
    own_ref[...] = acc_ref[pl.ds(row0 + pl.multiple_of(c * half, 8), half), :]
    oth_ref[...] = acc_ref[pl.ds(row0 + pl.multiple_of((1 - c) * half, 8), half), :].astype(BF16)


def _half_shapes(rows):
    return [jax.ShapeDtypeStruct((NSH, rows // 2, D), F32), jax.ShapeDtypeStruct((NSH, rows // 2, D), BF16)]


def _ffn_bwd_w(h2, act_a, dgate, dup, df, c_arr):
    s = h2.shape[0]
    tm = min(TM_FFN_FWD, s)
    nt = s // tm

    def body(c_ref, h_ref, a_ref, dgate_ref, dup_ref, df_ref, *rest):
        outs, accs = rest[:6], rest[6:]
        i = pl.program_id(1)

        @pl.when(i == 0)
        def _():
            for acc in accs:
                acc[...] = jnp.zeros_like(acc)

        h = h_ref[...]
        accs[0][...] += _dot_tn(dgate_ref[...], h)
        accs[1][...] += _dot_tn(dup_ref[...], h)
        accs[2][...] += _dot_tn(a_ref[...], df_ref[...])

        @pl.when(i == nt - 1)
        def _():
            for t, acc in enumerate(accs):
                _emit_halves(acc, 0, FS, c_ref[0], outs[2 * t], outs[2 * t + 1])

    row = lambda w: pl.BlockSpec((tm, w), lambda j, i: (i, 0))
    act = pl.BlockSpec((None, tm, FS), lambda j, i: (j, i, 0))
    half = pl.BlockSpec((None, FS // 2, D), lambda j, i: (j, 0, 0))
    return pl.pallas_call(
        body, name="ffn_bwd_w", grid=(NSH, nt),
        in_specs=[SMEM_SPEC, row(D), act, act, act, row(D)],
        out_specs=[half] * 6,
        out_shape=_half_shapes(FS) * 3,
        scratch_shapes=[pltpu.VMEM((FS, D), F32)] * 3,
        compiler_params=_cp(2))(c_arr, h2, act_a, dgate, dup, df)


def _outproj_bwd(dmix, w_out, pool_o, attn_o, c_arr, dep=None):
    s = dmix.shape[0]
    tm = min(TM, s)
    nt = s // tm

    def body(c_ref, dm_ref, w_ref, p_ref, a_ref, *rest):
        dpool_ref, dattn_ref, own_ref, oth_ref, gw_ref = rest[-5:]
        i = pl.program_id(0)

        @pl.when(i == 0)
        def _():
            gw_ref[...] = jnp.zeros_like(gw_ref)

        dm = dm_ref[...]
        dpool_ref[...] = _dot_nt(dm, w_ref[0:512, :])
        dattn_ref[...] = _dot_nt(dm, w_ref[512:1024, :]).astype(BF16)
        gw_ref[0:512, :] += _dot_tn(p_ref[...], dm)
        gw_ref[512:1024, :] += _dot_tn(a_ref[...], dm)

        @pl.when(i == nt - 1)
        def _():
            for k in range(NSH):
                _emit_halves(gw_ref, k * WOUT_S, WOUT_S, c_ref[0], own_ref.at[k], oth_ref.at[k])

    row = lambda w: pl.BlockSpec((tm, w), lambda i: (i, 0))
    full = pl.BlockSpec((D, D), lambda i: (0, 0))
    half = pl.BlockSpec((NSH, WOUT_S // 2, D), lambda i: (0, 0, 0))
    return pl.pallas_call(
        body, name="outproj_bwd", grid=(nt,),
        in_specs=[SMEM_SPEC, row(D), full, row(512), row(512)] + ([] if dep is None else [ANY]),
        out_specs=[row(512), row(512), half, half],
        out_shape=[jax.ShapeDtypeStruct((s, 512), F32), jax.ShapeDtypeStruct((s, 512), BF16)] + _half_shapes(WOUT_S),
        scratch_shapes=[pltpu.VMEM((D, D), F32)],
        compiler_params=_cp(1))(c_arr, dmix, w_out, pool_o, attn_o, *([] if dep is None else [dep]))


def _pool_bwd(pooled, dpool, w_pool, pool_scale, dep=None):
    s = pooled.shape[0]
    tm = min(TM_POOL, s)
    hb = tm // HALO
    nt = s // tm
    last_halo = s // HALO - 1

    def body(p_ref, dc_ref, dn_ref, wp_ref, sc_ref, *rest):
        du_ref, gwp_ref, gsc_ref = rest[-3:]
        i = pl.program_id(0)

        @pl.when(i == 0)
        def _():
            gwp_ref[...] = jnp.zeros_like(gwp_ref)
            gsc_ref[...] = jnp.zeros_like(gsc_ref)

        dcur = dc_ref[...]
        dnext = jnp.where(i < nt - 1, dn_ref[...], 0.0)
        dext = jnp.concatenate([dcur, dnext], axis=0)
        t_ext = i * tm + lax.broadcasted_iota(jnp.int32, (tm + HALO, GROUP), 0)
        for g, w in enumerate(WINDOWS):
            sl = slice(g * GROUP, (g + 1) * GROUP)
            pb = p_ref[:, sl]
            wp = wp_ref[g]
            mixed = _dot(pb, wp)
            gsc_ref[:, sl] += jnp.sum(dcur[:, sl] * mixed, axis=0, keepdims=True)
            dmixed = (dext[:, sl] * sc_ref[:, sl]).astype(BF16)
            gwp_ref[g] += _dot_tn(pb, dmixed[0:tm])
            dpooled = _dot_nt(dmixed, wp)
            z = dpooled / jnp.minimum(t_ext + 1, w).astype(F32)
            du_ref[:, sl] = (_window_sums(z, w, -1, tm, 2) - dpooled[0:tm]).astype(BF16)

    return pl.pallas_call(
        body, name="pool_bwd", grid=(nt,),
        in_specs=[pl.BlockSpec((tm, 512), lambda i: (i, 0)),
                  pl.BlockSpec((tm, 512), lambda i: (i, 0)),
                  pl.BlockSpec((HALO, 512), lambda i: (jnp.minimum((i + 1) * hb, last_halo), 0)),
                  pl.BlockSpec((4, GROUP, GROUP), lambda i: (0, 0, 0)),
                  pl.BlockSpec((1, 512), lambda i: (0, 0))] + ([] if dep is None else [ANY]),
        out_specs=[pl.BlockSpec((tm, 512), lambda i: (i, 0)),
                   pl.BlockSpec((4, GROUP, GROUP), lambda i: (0, 0, 0)),
                   pl.BlockSpec((1, 512), lambda i: (0, 0))],
        out_shape=[jax.ShapeDtypeStruct((s, 512), BF16), jax.ShapeDtypeStruct((4, GROUP, GROUP), F32),
                   jax.ShapeDtypeStruct((1, 512), F32)],
        compiler_params=_cp(1))(pooled, dpool, dpool, w_pool, pool_scale, *([] if dep is None else [dep]))


def _attn_bwd(q, k, v, do, probs, sink_share, bucket, dep=None):
    s = q.shape[0]
    nblk = s // BLK

    def body(q_ref, do_ref, k_ref, v_ref, prob_ref, ps_ref, bk_ref, *rest):
        dq_ref, dk_ref, dv_ref, grb_ref, gsk_ref, dss_ref = rest[-6:]
        n = pl.program_id(0)

        @pl.when(n == 0)
        def _():
            dk_ref[...] = jnp.zeros_like(dk_ref)
            dv_ref[...] = jnp.zeros_like(dv_ref)
            dss_ref[...] = jnp.zeros_like(dss_ref)
            gsk_ref[...] = jnp.zeros_like(gsk_ref)

        kt, (lo, pstart, cstart) = _kv_band(k_ref, n, True)
        vv, _ = _kv_band(v_ref, n, False)
        lo_q = lax.broadcasted_iota(jnp.int32, (BLK, 128), 1) < HD
        dkk = [jnp.zeros((2 * BLK, 128), F32), jnp.zeros((2 * BLK, 128), F32)]
        dvv = [jnp.zeros((2 * BLK, 128), F32), jnp.zeros((2 * BLK, 128), F32)]

        def by_head(t):
            return jnp.concatenate([jnp.where(lo_q, t, 0.0), jnp.where(lo_q, 0.0, t)], axis=0).astype(BF16)

        for p in range(4):
            h = p // 2
            qt = q_ref[:, p * 128:(p + 1) * 128].astype(F32) * SCALE
            dot = do_ref[:, p * 128:(p + 1) * 128]
            pb = prob_ref[pl.ds(2 * p, 2)]
            prob = pb.astype(F32)
            ps = ps_ref[pl.ds(2 * p, 2), :].reshape(2, 1, BLK)
            dp = _dot_nt(vv[h], dot).reshape(2, 2 * BLK, BLK)
            delta = jnp.sum(prob * dp, axis=1, keepdims=True)
            ds = prob * (dp - delta)
            sink_part = ps * delta
            for e in range(2):
                gs = -jnp.sum(sink_part[e]).reshape(1, 1)
                gsk_ref[pl.ds(2 * p + e, 1), :] += jnp.broadcast_to(gs, (1, 128))
            dss_ref[pl.ds(2 * p, 2)] += ds
            dsb = ds.astype(BF16)
            dq_t = _dot(kt[h], dsb.reshape(4 * BLK, BLK))
            dq_ref[:, p * 128:(p + 1) * 128] = (dq_t.T * SCALE).astype(BF16)
            dkk[h] = dkk[h] + _dot(jnp.concatenate([dsb[0], dsb[1]], axis=1), by_head(qt))
            dvv[h] = dvv[h] + _dot(jnp.concatenate([pb[0], pb[1]], axis=1), by_head(dot.astype(F32)))
        for acc, ref in ((dkk, dk_ref), (dvv, dv_ref)):
            f0 = acc[0] + pltpu.roll(acc[0], HD, axis=1)
            f1 = acc[1] + pltpu.roll(acc[1], HD, axis=1)
            band = jnp.where(lo, f0, f1)
            ref[pl.ds(pstart, BLK), :] += band[0:BLK]
            ref[pl.ds(cstart, BLK), :] += band[BLK:2 * BLK]

        @pl.when(n == nblk - 1)
        def _():
            _relbias_grad(dss_ref, bk_ref[...], grb_ref)

    blk = pl.BlockSpec((BLK, 512), lambda i: (i, 0))
    kv = pl.BlockSpec((s, 128), lambda i: (0, 0))
    row8 = pl.BlockSpec((NQ, 128), lambda i: (0, 0))
    return pl.pallas_call(
        body, name="attn_bwd", grid=(nblk,),
        in_specs=[blk, blk, kv, kv, pl.BlockSpec((None, NQ, 2 * BLK, BLK), lambda i: (i, 0, 0, 0)),
                  pl.BlockSpec((None, NQ, BLK), lambda i: (i, 0, 0)), pl.BlockSpec((2 * BLK, BLK), lambda i: (0, 0))]
        + ([] if dep is None else [ANY]),
        out_specs=[blk, kv, kv, row8, row8],
        out_shape=[jax.ShapeDtypeStruct((s, 512), BF16), jax.ShapeDtypeStruct((s, 128), F32),
                   jax.ShapeDtypeStruct((s, 128), F32), jax.ShapeDtypeStruct((NQ, 128), F32),
                   jax.ShapeDtypeStruct((NQ, 128), F32)],
        scratch_shapes=[pltpu.VMEM((NQ, 2 * BLK, BLK), F32)],
        compiler_params=_cp(1))(q, do, k, v, probs, sink_share, bucket, *([] if dep is None else [dep]))


def _relbias_grad(ds_ref, bucket_v, o_ref):
    lane = lax.broadcasted_iota(jnp.int32, (NQ, 128), 1)
    head_row = lax.broadcasted_iota(jnp.int32, (NQ, BLK), 0)
    acc = jnp.zeros((NQ, 128), F32)
    for b in range(NBUCKET):
        sel = bucket_v == b
        stack = jnp.zeros((NQ, BLK), F32)
        for h in range(NQ):
            col_sums = jnp.sum(jnp.where(sel, ds_ref[h], 0.0), axis=0, keepdims=True)
            stack = jnp.where(head_row == h, col_sums, stack)
        acc = acc + jnp.where(lane == b, jnp.sum(stack, axis=1, keepdims=True), 0.0)
    o_ref[...] = acc


def _inproj_bwd(x, g1, du, dq, dk, dv, w_in, dx1, c_arr):
    s = x.shape[0]
    tm = min(TM, s)
    nt = s // tm
    cols = ((0, 512), (512, 1024), (1024, 1152), (1152, 1280))

    def body(c_ref, x_ref, g_ref, du_ref, dq_ref, dk_ref, dv_ref, w_ref, dx1_ref,
             gx_ref, own_ref, oth_ref, gg_ref, gw_ref):
        i = pl.program_id(0)

        @pl.when(i == 0)
        def _():
            gw_ref[...] = jnp.zeros_like(gw_ref)
            gg_ref[...] = jnp.zeros_like(gg_ref)

        parts = (du_ref[...], dq_ref[...], dk_ref[...].astype(BF16), dv_ref[...].astype(BF16))
        dh = None
        for (a, b), dpart in zip(cols, parts):
            t = _dot(dpart, w_ref[a:b, :])
            dh = t if dh is None else dh + t
        gv = g_ref[...]
        r1, n1 = _rms(x_ref[...])
        h = (n1 * gv).astype(BF16)
        for (a, b), dpart in zip(cols, parts):
            gw_ref[a:b, :] += _dot_tn(dpart, h)
        dx, dg = _rms_bwd(r1, n1, gv, dh)
        gg_ref[...] += dg
        gx_ref[...] = dx1_ref[...] + dx

        @pl.when(i == nt - 1)
        def _():
            for k in range(NSH):
                _emit_halves(gw_ref, k * WIN_S, WIN_S, c_ref[0], own_ref.at[k], oth_ref.at[k])

    row = lambda w: pl.BlockSpec((tm, w), lambda i: (i, 0))
    vec = pl.BlockSpec((1, D), lambda i: (0, 0))
    full = pl.BlockSpec((IN_W, D), lambda i: (0, 0))
    half = pl.BlockSpec((NSH, WIN_S // 2, D), lambda i: (0, 0, 0))
    return pl.pallas_call(
        body, name="inproj_bwd", grid=(nt,),
        in_specs=[SMEM_SPEC, row(D), vec, row(512), row(512), row(128), row(128), full, row(D)],
        out_specs=[row(D), half, half, vec],
        out_shape=[jax.ShapeDtypeStruct((s, D), F32)] + _half_shapes(WIN_S) + [jax.ShapeDtypeStruct((1, D), F32)],
        scratch_shapes=[pltpu.VMEM((IN_W, D), F32)],
        compiler_params=_cp(1))(c_arr, x, g1, du, dq, dk, dv, w_in, dx1)


def _local_step(x, target, small, w_in, w_out, ffn_weights, c_arr, on_ffn_grads=None, on_wout_grads=None,
                mid_outproj=None):
    g1, g2, g3, g4, w_pool, pool_scale, rel_bias, sinks = small
    bucket = jnp.asarray(_bucket_table())
    wp_bf = w_pool.astype(BF16)
    bias = _bias_table(bucket, rel_bias)
    h1 = _prenorm(x, g1)
    if callable(w_in):
        w_in = w_in(bias, h1)
    u, q, k, v = _inproj_fwd(h1, w_in)
    pool_o, pooled = _pool_fwd(u, wp_bf, pool_scale)
    attn_o, probs, sink_share = _attn_fwd(q, k, v, bias, sinks)
    g2_fwd = g2
    if callable(w_out):
        w_out, after = w_out(pool_o, attn_o)
        if after is not None:
            g2_fwd = g2 + after[:1, :1]
    mix, x1, h2 = _outproj_fwd(pool_o, attn_o, w_out, x, g2_fwd, g3, mid_outproj)
    wg, wu, wd = ffn_weights(h2) if callable(ffn_weights) else ffn_weights
    gate, up, act_a, df, dy, loss, gg4 = _ffn_fwd(h2, wg, wu, wd, x1, target, g4)
    dgate, dup, dx1, dmix, gg3, gg2 = _ffn_bwd_act(df, gate, up, wg, wu, wd, dy, x1, mix, g3, g2)
    ffn_g = _ffn_bwd_w(h2, act_a, dgate, dup, df, c_arr)
    dep = None if on_ffn_grads is None else on_ffn_grads(ffn_g)
    dpool, dattn, wout_own, wout_oth = _outproj_bwd(dmix, w_out, pool_o, attn_o, c_arr, dep)
    dep = None if on_wout_grads is None else on_wout_grads(wout_own, wout_oth, dpool)
    du, gwp, gsc = _pool_bwd(pooled, dpool, wp_bf, pool_scale, dep)
    dq, dk, dv, grb, gsk = _attn_bwd(q, k, v, dattn, probs, sink_share, bucket, dep)
    gx, win_own, win_oth, gg1 = _inproj_bwd(x, g1, du, dq, dk, dv, w_in, dx1, c_arr)
    small_grads = (gg1, gg2, gg3, gg4, gwp, gsc, grb, gsk[:, 0].reshape(1, NQ))
    return loss, gx, small_grads, ((win_own, win_oth), (wout_own, wout_oth), ffn_g)


def _place():
    x, y, c = lax.axis_index("x"), lax.axis_index("y"), lax.axis_index("c")
    chips = ((1 - x, y), (x, 1 - y), (1 - x, 1 - y))
    return x, y, c, chips


def _remote(src, dst, ssem, rsem, dev):
    return pltpu.make_async_remote_copy(src_ref=src, dst_ref=dst, send_sem=ssem, recv_sem=rsem,
                                        device_id=dev, device_id_type=MESH_T)


def _to_sibling(arrs, name, after=()):
    nt, na = len(arrs), len(after)

    def body(*refs):
        srcs, dsts = refs[:nt], refs[nt + na:2 * nt + na]
        ssem, rsem = refs[2 * nt + na:]
        x, y, c, _ = _place()
        cps = [_remote(srcs[t], dsts[t], ssem.at[t], rsem.at[t], (x, y, 1 - c)) for t in range(nt)]
        for cp in cps:
            cp.start()
        for cp in cps:
            cp.wait()

    return pl.pallas_call(
        body, name=name, in_specs=[ANY] * (nt + na), out_specs=[ANY] * nt,
        out_shape=[jax.ShapeDtypeStruct(a.shape, a.dtype) for a in arrs],
        scratch_shapes=[pltpu.SemaphoreType.DMA((nt,)), pltpu.SemaphoreType.DMA((nt,))],
        compiler_params=_cp())(*arrs, *after)


HBM_SPEC = pl.BlockSpec(memory_space=pltpu.HBM)
SEM_SPEC = pl.BlockSpec(memory_space=pltpu.SEMAPHORE)
DATAFLOW = pltpu.SideEffectType.DATAFLOW_SIDE_EFFECTING


def _cast_into_slots(ws, chip_arr):
    nt, tiles = len(ws), 4

    def body(chip_ref, *refs):
        for t in range(nt):
            refs[nt + t][...] = refs[t][...].astype(BF16)

    return pl.pallas_call(
        body, name="cast_weights",
        grid_spec=pltpu.PrefetchScalarGridSpec(
            num_scalar_prefetch=1, grid=(tiles,),
            in_specs=[pl.BlockSpec((w.shape[0] // tiles, w.shape[1]), lambda i, chip_ref: (i, 0)) for w in ws],
            out_specs=[pl.BlockSpec((None, w.shape[0] // tiles, w.shape[1]), lambda i, chip_ref: (chip_ref[0], i, 0))
                       for w in ws]),
        out_shape=[jax.ShapeDtypeStruct((NSH,) + w.shape, BF16) for w in ws],
        compiler_params=_cp(1))(chip_arr, *ws)


def _gather_copies(srcs, lands, ssem, rsem):
    x, y, c, chips = _place()
    me = 2 * x + y
    out = []
    for t in range(len(lands)):
        half = lands[t].shape[1] // 2
        mine = pl.ds(pl.multiple_of(c * half, 16), half)
        for j, (px, py) in enumerate(chips):
            k = 3 * t + j
            send = _remote(lands[t].at[me, mine], lands[t].at[me, mine], ssem.at[k], rsem.at[k], (px, py, c))
            blk = lands[t].at[2 * px + py, mine]
            out.append((send, _remote(blk, blk, ssem.at[k], rsem.at[k], (px, py, c))))
    return out


def _scatter_copies(srcs, lands, ssem, rsem):
    x, y, c, chips = _place()
    out = []
    for t in range(len(srcs)):
        for j, (px, py) in enumerate(chips):
            k = 3 * t + j
            send = _remote(srcs[t].at[2 * px + py], lands[t].at[j], ssem.at[k], rsem.at[k], (px, py, c))
            out.append((send, _remote(lands[t].at[j], lands[t].at[j], ssem.at[k], rsem.at[k], (px, py, c))))
    return out


def _sibling_copies(srcs, lands, ssem, rsem):
    x, y, c, _ = _place()
    sib = (x, y, 1 - c)
    return [(_remote(srcs[t], lands[t], ssem.at[t], rsem.at[t], sib),
             _remote(lands[t], lands[t], ssem.at[t], rsem.at[t], sib)) for t in range(len(srcs))]


def _peer_copies(srcs, lands, ssem, rsem):
    x, y, c, _ = _place()
    me = 4 * x + 2 * y + c
    out = []
    for kk in range(1, 8):
        px, py, pc = x ^ (kk >> 2), y ^ ((kk >> 1) & 1), c ^ (kk & 1)
        send = _remote(srcs[0], lands[0].at[me], ssem.at[kk - 1], rsem.at[kk - 1], (px, py, pc))
        slot = lands[0].at[4 * px + 2 * py + pc]
        out.append((send, _remote(slot, slot, ssem.at[kk - 1], rsem.at[kk - 1], (px, py, pc))))
    return out


def _forward_copies(srcs, lands, ssem, rsem):
    x, y, c, chips = _place()
    sib = (x, y, 1 - c)
    out = []
    for t in range(len(lands)):
        half = lands[t].shape[1] // 2
        mine = pl.ds(pl.multiple_of(c * half, 16), half)
        other = pl.ds(pl.multiple_of((1 - c) * half, 16), half)
        for j, (px, py) in enumerate(chips):
            k = 3 * t + j
            blk_m, blk_o = lands[t].at[2 * px + py, mine], lands[t].at[2 * px + py, other]
            out.append((_remote(blk_m, blk_m, ssem.at[k], rsem.at[k], sib),
                        _remote(blk_o, blk_o, ssem.at[k], rsem.at[k], sib)))
    return out


def _win_and_small_copies(srcs, lands, ssem, rsem):
    return (_scatter_copies(srcs[:1], lands[:1], ssem, rsem)
            + _peer_copies(srcs[1:], lands[1:], ssem.at[pl.ds(3, 7)], rsem.at[pl.ds(3, 7)]))


def _split_start(plan, ncopy, srcs, lands, after, name):
    ns, nbuf = len(srcs), len(srcs) + len(lands)
    extra = [] if after is None else [after]
    n_in = nbuf + len(extra)

    def body(*refs):
        ssem, rsem, token = refs[n_in], refs[n_in + 1], refs[-1]
        for send, _ in plan(refs[:ns], refs[ns:nbuf], ssem, rsem):
            send.start()
        token[...] = jnp.zeros_like(token)

    bufs = [pltpu.with_memory_space_constraint(a, pltpu.HBM) for a in list(srcs) + list(lands)]
    outs = pl.pallas_call(
        body, name=name, in_specs=[HBM_SPEC] * nbuf + [ANY] * len(extra),
        out_specs=[SEM_SPEC, SEM_SPEC] + [HBM_SPEC] * nbuf + [pl.BlockSpec(memory_space=pltpu.VMEM)],
        out_shape=[pltpu.SemaphoreType.DMA((ncopy,)), pltpu.SemaphoreType.DMA((ncopy,))]
        + [pltpu.HBM(a.shape, a.dtype) for a in bufs] + [jax.ShapeDtypeStruct((8, 128), F32)],
        input_output_aliases={i: 2 + i for i in range(nbuf)},
        compiler_params=pltpu.CompilerParams(has_side_effects=DATAFLOW))(*bufs, *extra)
    return outs[0], outs[1], outs[2:2 + ns], outs[2 + ns:2 + nbuf], outs[-1]


def _split_wait(plan, ssem, rsem, srcs, lands, after, name, only=None):
    ns, nbuf = len(srcs), len(srcs) + len(lands)

    def body(*refs):
        s_ref, r_ref = refs[nbuf], refs[nbuf + 1]
        for k, (send, recv) in enumerate(plan(refs[:ns], refs[ns:nbuf], s_ref, r_ref)):
            if only is None or k in only:
                send.wait_send()
                recv.wait_recv()

    outs = pl.pallas_call(
        body, name=name, in_specs=[HBM_SPEC] * nbuf + [SEM_SPEC, SEM_SPEC] + [ANY] * len(after),
        out_specs=[HBM_SPEC] * nbuf,
        out_shape=[pltpu.HBM(a.shape, a.dtype) for a in list(srcs) + list(lands)],
        input_output_aliases={i: i for i in range(nbuf)},
        compiler_params=pltpu.CompilerParams(has_side_effects=DATAFLOW))(*srcs, *lands, ssem, rsem, *after)
    return outs


def _gather_finish(lands, tag):
    nt = len(lands)

    def body(*refs):
        outs = refs[nt:2 * nt]
        dsend, drecv = refs[2 * nt:]
        x, y, c, chips = _place()
        sib = (x, y, 1 - c)
        sends = []
        for t in range(nt):
            half = outs[t].shape[1] // 2
            mine = pl.ds(pl.multiple_of(c * half, 16), half)
            for j, (px, py) in enumerate(chips):
                blk = outs[t].at[2 * px + py, mine]
                cp = _remote(blk, blk, dsend.at[t, j], drecv.at[t, j], sib)
                cp.start()
                sends.append(cp)
        for t in range(nt):
            half = outs[t].shape[1] // 2
            other = pl.ds(pl.multiple_of((1 - c) * half, 16), half)
            for j, (px, py) in enumerate(chips):
                blk = outs[t].at[2 * px + py, other]
                _remote(blk, blk, dsend.at[t, j], drecv.at[t, j], sib).wait_recv()
        for cp in sends:
            cp.wait_send()

    return pl.pallas_call(
        body, name="gather_finish_" + tag, in_specs=[ANY] * nt, out_specs=[ANY] * nt,
        out_shape=[jax.ShapeDtypeStruct(a.shape, a.dtype) for a in lands],
        input_output_aliases={t: t for t in range(nt)},
        scratch_shapes=[pltpu.SemaphoreType.DMA((nt, 3)), pltpu.SemaphoreType.DMA((nt, 3))],
        compiler_params=_cp())(*lands)


def _chip_partial(owns, recv, chip_arr, tag):
    nt = len(owns)

    def body(chip_ref, *refs):
        k = pl.program_id(0)
        for t in range(nt):
            p = refs[t][...] + refs[nt + t][...].astype(F32)
            refs[2 * nt + t][...] = p.astype(BF16)

            @pl.when(k == chip_ref[0])
            def _():
                refs[3 * nt + t][...] = p

    blk_specs = [pl.BlockSpec((None,) + a.shape[1:], lambda k, chip_ref: (k, 0, 0)) for a in owns]
    own_specs = [pl.BlockSpec(a.shape[1:], lambda k, chip_ref: (0, 0)) for a in owns]
    return pl.pallas_call(
        body, name="rs_chip_partial_" + tag,
        grid_spec=pltpu.PrefetchScalarGridSpec(num_scalar_prefetch=1, grid=(NSH,), in_specs=blk_specs * 2,
                                               out_specs=blk_specs + own_specs),
        out_shape=[jax.ShapeDtypeStruct(a.shape, BF16) for a in owns]
        + [jax.ShapeDtypeStruct(a.shape[1:], F32) for a in owns],
        compiler_params=_cp(1))(chip_arr, *owns, *recv)


def _sum_chips(own, recv, tag):
    nt = len(own)

    def body(*refs):
        outs = refs[2 * nt:]
        for t in range(nt):
            r = refs[nt + t]
            outs[t][...] = ((refs[t][...] + r[0].astype(F32)) + r[1].astype(F32)) + r[2].astype(F32)

    vm = pl.BlockSpec(memory_space=pltpu.VMEM)
    return pl.pallas_call(
        body, name="rs_sum_chips_" + tag, in_specs=[vm] * (2 * nt), out_specs=[vm] * nt,
        out_shape=[jax.ShapeDtypeStruct(a.shape, F32) for a in own],
        compiler_params=_cp())(*own, *recv)


def _sum_chips_shared(own, recv, tag):
    def body(own_ref, recv_ref, out_ref, sib_ref, ssem, rsem):
        out_ref[...] = ((own_ref[...] + recv_ref[0].astype(F32)) + recv_ref[1].astype(F32)) + recv_ref[2].astype(F32)
        x, y, c, _ = _place()
        cp = _remote(out_ref, sib_ref, ssem.at[0], rsem.at[0], (x, y, 1 - c))
        cp.start()
        cp.wait()

    vm = pl.BlockSpec(memory_space=pltpu.VMEM)
    return pl.pallas_call(
        body, name="rs_sum_share_" + tag, in_specs=[vm, vm], out_specs=[vm, ANY],
        out_shape=[jax.ShapeDtypeStruct(own.shape, F32)] * 2,
        scratch_shapes=[pltpu.SemaphoreType.DMA((1,)), pltpu.SemaphoreType.DMA((1,))],
        compiler_params=_cp())(own, recv)


def _adamw_math(w, g, m, v):
    m = ADAM_B1 * m + (1.0 - ADAM_B1) * g
    v = ADAM_B2 * v + (1.0 - ADAM_B2) * (g * g)
    m_hat = m / (1.0 - ADAM_B1 ** ADAM_STEP)
    v_hat = v / (1.0 - ADAM_B2 ** ADAM_STEP)
    delta = -ADAM_LR * (m_hat / (jnp.sqrt(v_hat) + ADAM_EPS) + ADAM_WD * w)
    return delta, m, v


ADAM_SPLIT = 4


def _adamw_shards(own, sib, ws, ms, vs, c_arr, tag):
    nt = len(own)

    def body(c_ref, *refs):
        hh = pl.program_id(0)
        mine = hh == c_ref[0]
        for t in range(nt):
            g = jnp.where(mine, refs[t][...], refs[nt + t][...])
            delta, m, v = _adamw_math(refs[2 * nt + t][...], g, refs[3 * nt + t][...], refs[4 * nt + t][...])
            refs[5 * nt + t][...] = g
            refs[6 * nt + t][...] = delta
            refs[7 * nt + t][...] = m
            refs[8 * nt + t][...] = v

    def tile(a):
        return (a.shape[0] // ADAM_SPLIT, a.shape[1])

    def half_index(used_when_mine):
        def index(hh, q, c_ref):
            mine = 1 - (hh - c_ref[0]) * (hh - c_ref[0])
            used = mine if used_when_mine else 1 - mine
            return (used * q + (1 - used) * hh * (ADAM_SPLIT - 1), 0)
        return index

    own_specs = [pl.BlockSpec(tile(a), half_index(True)) for a in own]
    sib_specs = [pl.BlockSpec(tile(a), half_index(False)) for a in own]
    full_specs = [pl.BlockSpec(tile(a), lambda hh, q, c_ref: (hh * ADAM_SPLIT + q, 0)) for a in own]
    return pl.pallas_call(
        body, name="adamw_shards_" + tag,
        grid_spec=pltpu.PrefetchScalarGridSpec(num_scalar_prefetch=1, grid=(2, ADAM_SPLIT),
                                               in_specs=own_specs + sib_specs + full_specs * 3,
                                               out_specs=full_specs * 4),
        out_shape=[jax.ShapeDtypeStruct(w.shape, F32) for w in ws] * 4,
        compiler_params=_cp(2))(c_arr, *own, *sib, *ws, *ms, *vs)


SMALL_WPOOL_ROW = 32
SMALL_SCALE_ROW = SMALL_WPOOL_ROW + 4 * GROUP
SMALL_BIAS_ROW = SMALL_SCALE_ROW + 8
SMALL_SINKS_ROW = SMALL_BIAS_ROW + NQ
SMALL_LOSS_ROW = SMALL_SINKS_ROW + 8


def _small_sum_adamw(gathered, wp, mp, vp):
    rows = wp.shape[0]

    def body(g_ref, w_ref, m_ref, v_ref, *rest):
        outs, res = rest[:-1], rest[-1]
        g = g_ref[0]
        for d in range(1, 8):
            g = g + g_ref[d]
        delta, m, v = _adamw_math(w_ref[...], g, m_ref[...], v_ref[...])
        for kind, val in enumerate((g, delta, m, v)):
            res[kind] = val
            gains = outs[8 * kind:8 * kind + 4]
            w_pool_o, scale_o, bias_o, sinks_o = outs[8 * kind + 4:8 * kind + 8]
            for t in range(4):
                for i in range(8):
                    gains[t][:, 128 * i:128 * (i + 1)] = res[kind, pl.ds(8 * t + i, 1), :]
            for grp in range(4):
                w_pool_o[grp] = res[kind, pl.ds(SMALL_WPOOL_ROW + GROUP * grp, GROUP), :]
            for i in range(4):
                scale_o[:, 128 * i:128 * (i + 1)] = res[kind, pl.ds(SMALL_SCALE_ROW + i, 1), :]
            bias_o[...] = res[kind, pl.ds(SMALL_BIAS_ROW, NQ), :][:, 0:NBUCKET]
            sinks_o[...] = res[kind, pl.ds(SMALL_SINKS_ROW, 1), :][:, 0:NQ]
        outs[-1][...] = res[0, pl.ds(SMALL_LOSS_ROW, 1), :]

    vm = pl.BlockSpec(memory_space=pltpu.VMEM)
    one_kind = [jax.ShapeDtypeStruct((1, D), F32)] * 4 + [
        jax.ShapeDtypeStruct((4, GROUP, GROUP), F32), jax.ShapeDtypeStruct((1, POOL_W), F32),
        jax.ShapeDtypeStruct((NQ, NBUCKET), F32), jax.ShapeDtypeStruct((1, NQ), F32)]
    return pl.pallas_call(
        body, name="small_sum_adamw", in_specs=[vm] * 4, out_specs=[vm] * 33,
        out_shape=one_kind * 4 + [jax.ShapeDtypeStruct((1, 128), F32)],
        scratch_shapes=[pltpu.VMEM((4, rows, 128), F32)],
        compiler_params=_cp())(gathered, wp, mp, vp)


def _pack_small(gains4, w_pool, pool_scale, rel_bias_t, sinks, loss):
    bias_rows = jnp.pad(rel_bias_t, ((0, 0), (0, 128 - rel_bias_t.shape[1])))
    parts = list(gains4) + [w_pool, pool_scale, bias_rows, sinks, loss]
    rows = []
    for a in parts:
        flat = a.reshape(-1)
        n = flat.shape[0]
        padded = -(-n // 1024) * 1024
        rows.append(jnp.pad(flat, (0, padded - n)).reshape(-1, 128))
    return jnp.concatenate(rows, axis=0)


def kernel(x, g_pre_mix, w_in, w_pool, pool_scale, rel_bias, sinks, w_out, g_post_mix, g_pre_ffn, w_gate, w_up, w_down, g_post_ffn, loss_target, m_g_pre_mix, m_w_in, m_w_pool, m_pool_scale, m_rel_bias, m_sinks, m_w_out, m_g_post_mix, m_g_pre_ffn, m_w_gate, m_w_up, m_w_down, m_g_post_ffn, v_g_pre_mix, v_w_in, v_w_pool, v_pool_scale, v_rel_bias, v_sinks, v_w_out, v_g_post_mix, v_g_pre_ffn, v_w_gate, v_w_up, v_w_down, v_g_post_ffn):
    c = lax.axis_index("c")
    chip = 2 * lax.axis_index("x") + lax.axis_index("y")

    def shards(a_in, a_out, a_gate, a_up, a_down):
        return (a_in[0].T, a_out[0], a_gate[0].T, a_up[0].T, a_down[0])

    c_arr = c.reshape(1).astype(jnp.int32)
    chip_arr = chip.reshape(1).astype(jnp.int32)

    big_w = shards(w_in, w_out, w_gate, w_up, w_down)
    g_ssem, g_rsem, _, g_lands, g_token = _split_start(
        _gather_copies, 15, [], _cast_into_slots(big_w, chip_arr), None, "gather_start")
    fw = {}

    def w_in_ready(*after):
        fw["first"] = _split_wait(_gather_copies, g_ssem, g_rsem, [], g_lands, after, "gather_win_wait",
                                  only=(0, 1, 2))
        (fw["win"],) = _gather_finish([fw["first"][0]], "win")
        return fw["win"].reshape(IN_W, D)

    def w_out_ready(*after):
        fw["second"] = _split_wait(_gather_copies, g_ssem, g_rsem, [], [fw["win"]] + list(fw["first"][1:]), after,
                                   "gather_wout_wait", only=(3, 4, 5))
        (fw["wout"],) = _gather_finish([fw["second"][1]], "wout")
        return fw["wout"].reshape(D, D), None

    def ffn_forward_start(after):
        second = fw["second"]
        outs = _split_wait(_gather_copies, g_ssem, g_rsem, [], [second[0], fw["wout"]] + list(second[2:]),
                           [after], "gather_ffn_wait", only=tuple(range(6, 15)))
        fw["f"] = _split_start(_forward_copies, 9, [], list(outs[2:]), None, "gather_forward_start")
        return fw["f"][4]

    def ffn_weights(after):
        ssem, rsem, _, lands, _ = fw["f"]
        return _split_wait(_forward_copies, ssem, rsem, [], lands, [after], "gather_forward_wait")

    rs = {}

    def on_ffn_grads(ffn_g):
        oths = ffn_g[1::2]
        rs["ffn_own"] = ffn_g[0::2]
        rs["x"] = _split_start(_sibling_copies, 3, oths, [lax.empty(a.shape, BF16) for a in oths], ffn_g[0],
                               "rs_exchange_ffn_start")
        return rs["x"][4]

    def on_wout_grads(own, oth, after):
        ssem, rsem, srcs, lands, _ = rs["x"]
        recv_ffn = _split_wait(_sibling_copies, ssem, rsem, srcs, lands, [after], "rs_exchange_ffn_wait")[3:]
        recv_wout = _to_sibling([oth], "rs_exchange_wout")
        outs = _chip_partial([own] + list(rs["ffn_own"]), list(recv_wout) + list(recv_ffn), chip_arr, "early")
        rs["early_own"] = outs[4:]
        rs["s"] = _split_start(_scatter_copies, 12, outs[:4],
                               [lax.empty((3,) + a.shape[1:], BF16) for a in outs[:4]], outs[4], "scatter_early_start")
        return rs["s"][4]

    small = (g_pre_mix + g_token[:1, :1], g_post_mix, g_pre_ffn, g_post_ffn, w_pool[0], pool_scale, rel_bias, sinks)
    loss, gx, small_grads, ((win_own, win_oth), _, _) = _local_step(
        x[0], loss_target[0], small, w_in_ready, w_out_ready, ffn_weights, c_arr, on_ffn_grads, on_wout_grads,
        ffn_forward_start)

    ssem, rsem, srcs, lands, _ = rs["s"]
    recv_early = _split_wait(_scatter_copies, ssem, rsem, srcs, lands, [gx], "scatter_early_wait")[4:]
    finals = _sum_chips(list(rs["early_own"]), list(recv_early), "early")
    to_sib = [win_oth] + list(finals)
    sh_ssem, sh_rsem, sh_srcs, sh_lands, _ = _split_start(
        _sibling_copies, 5, to_sib, [lax.empty(a.shape, a.dtype) for a in to_sib], None, "rs_share_start")

    unused = jnp.zeros((1, 1), F32)
    gp = _pack_small(small_grads[:4], *small_grads[4:], loss)
    wp = _pack_small((g_pre_mix, g_post_mix, g_pre_ffn, g_post_ffn), w_pool, pool_scale, rel_bias.T, sinks, unused)
    mp = _pack_small((m_g_pre_mix, m_g_post_mix, m_g_pre_ffn, m_g_post_ffn), m_w_pool, m_pool_scale, m_rel_bias.T,
                     m_sinks, unused)
    vp = _pack_small((v_g_pre_mix, v_g_post_mix, v_g_pre_ffn, v_g_post_ffn), v_w_pool, v_pool_scale, v_rel_bias.T,
                     v_sinks, unused)

    moments = (shards(m_w_in, m_w_out, m_w_gate, m_w_up, m_w_down),
               shards(v_w_in, v_w_out, v_w_gate, v_w_up, v_w_down))
    sh_first = _split_wait(_sibling_copies, sh_ssem, sh_rsem, sh_srcs, sh_lands, [], "rs_share_win_wait", only=(0,))
    outs = _chip_partial([win_own], [sh_first[5]], chip_arr, "win")
    slots = lax.dynamic_update_slice(lax.empty((8,) + gp.shape, F32), gp[None], (2 * chip + c, 0, 0))
    w_ssem, w_rsem, w_srcs, w_lands, w_token = _split_start(
        _win_and_small_copies, 10, [outs[0], gp], [lax.empty((3,) + outs[0].shape[1:], BF16), slots], gx,
        "scatter_win_start")
    shared = _split_wait(_sibling_copies, sh_ssem, sh_rsem, sh_first[:5], sh_first[5:], [w_token],
                         "rs_share_early_wait", only=(1, 2, 3, 4))
    adam_e = _adamw_shards(shared[1:5], shared[6:], big_w[1:], moments[0][1:], moments[1][1:], c_arr, "early")
    w_first = _split_wait(_win_and_small_copies, w_ssem, w_rsem, w_srcs, w_lands, [adam_e[0]], "scatter_win_wait",
                          only=(0, 1, 2))
    win_final, win_sib = _sum_chips_shared(outs[1], w_first[2], "win")
    adam_w = _adamw_shards([win_final], [win_sib], big_w[:1], moments[0][:1], moments[1][:1], c_arr, "win")
    big_g, big_d, big_m, big_v = [[adam_w[i]] + list(adam_e[4 * i:4 * i + 4]) for i in range(4)]

    gathered = _split_wait(_win_and_small_copies, w_ssem, w_rsem, w_first[:2], w_first[2:], [adam_w[0]],
                           "small_allgather_wait", only=tuple(range(3, 10)))[3]
    flat = _small_sum_adamw(gathered, wp, mp, vp)
    small_out = [flat[8 * kind:8 * kind + 8] for kind in range(4)]
    total_loss = flat[-1][0, 0]

    def ordered(small8, big4):
        sg1, sg2, sg3, sg4, swp, ssc, srb, ssk = small8
        swp, srb = swp[None], srb.T
        bwin, bwout, bwg, bwu, bwd = big4[0].T[None], big4[1][None], big4[2].T[None], big4[3].T[None], big4[4][None]
        return [sg1, bwin, swp, ssc, srb, ssk, bwout, sg2, sg3, bwg, bwu, bwd, sg4]

    res = [total_loss, gx[None]]
    for sm, bg in zip(small_out, (big_g, big_d, big_m, big_v)):
        res += ordered(sm, bg)
    return tuple(res)
```

```python
import functools

import numpy as np
import jax
import jax.numpy as jnp
from jax import lax
from jax.experimental import pallas as pl
from jax.experimental.pallas import tpu as pltpu

F32 = jnp.float32
BF16 = jnp.bfloat16

D = 1024
POOL_W = 512
GROUP = 128
WINDOWS = (2, 4, 8, 16)
HALO = 128
NQ = 8
BLK = 128
NBUCKET = 32
IN_W = 1280
DFF = 2816
NSH = 4
FS = DFF // NSH
WIN_S = IN_W // NSH
WOUT_S = D // NSH
EPS = 1e-6
NEG = -1e30
SCALE = 0.125

ADAM_LR = 0.001
ADAM_B1 = 0.9
ADAM_B2 = 0.999
ADAM_EPS = 1e-08
ADAM_WD = 0.01
ADAM_STEP = 10

TM = 512
TM_IN = 1024
TM_POOL = 512
TM_FFN = 512
TM_FFN_FWD = 1024
FFN_ROWS = 256
VMEM_LIMIT = 60 * 1024 * 1024

MESH_T = pl.DeviceIdType.MESH
ANY = pl.BlockSpec(memory_space=pl.ANY)


def _cp(n_grid=0, **kw):
    sem = ("arbitrary",) * n_grid if n_grid else None
    return pltpu.CompilerParams(dimension_semantics=sem, vmem_limit_bytes=VMEM_LIMIT, **kw)


def _dot(a, b):
    return jnp.dot(a, b, preferred_element_type=F32)


def _dot_nt(a, b):
    return lax.dot_general(a, b, (((1,), (1,)), ((), ())), preferred_element_type=F32)


def _dot_tn(a, b):
    return lax.dot_general(a, b, (((0,), (0,)), ((), ())), preferred_element_type=F32)


def _rms(x):
    r = lax.rsqrt(jnp.mean(x * x, axis=-1, keepdims=True) + EPS)
    return r, x * r


def _rms_bwd(r, n, g, dout):
    dn = dout * g
    dx = r * (dn - n * jnp.mean(dn * n, axis=-1, keepdims=True))
    dg = jnp.sum(dout * n, axis=0, keepdims=True)
    return dx, dg


def _split(x, n):
    parts = []
    r = x
    for _ in range(n):
        p = r.astype(BF16)
        parts.append(p)
        r = r - p.astype(F32)
    return parts


def _band_dot(a, x, n):
    acc = None
    for p in _split(x, n):
        t = _dot(a, p)
        acc = t if acc is None else acc + t
    return acc


def _bucket_table():
    qi = np.arange(BLK)[None, :]
    kj = np.arange(2 * BLK)[:, None]
    dist = qi + BLK - kj
    n = np.maximum(dist, 0)
    nf = np.maximum(n, 1).astype(np.float32)
    large = 16 + (np.log(nf / np.float32(16)) / np.float32(np.log(128 / 16)) * np.float32(16)).astype(np.int32)
    large = np.minimum(large, NBUCKET - 1)
    return np.where(n < 16, n, large).astype(np.int32)


def _prenorm(x, g1, dep=None):
    s = x.shape[0]
    tm = min(TM_IN, s)

    def body(x_ref, g_ref, *rest):
        _, n = _rms(x_ref[...])
        rest[-1][...] = (n * g_ref[...]).astype(BF16)

    row = pl.BlockSpec((tm, D), lambda i: (i, 0))
    return pl.pallas_call(
        body, name="prenorm", grid=(s // tm,),
        in_specs=[row, pl.BlockSpec((1, D), lambda i: (0, 0))] + ([] if dep is None else [ANY]), out_specs=row,
        out_shape=jax.ShapeDtypeStruct((s, D), BF16),
        compiler_params=_cp(1))(x, g1, *([] if dep is None else [dep]))


def _inproj_fwd(h1, w_in):
    s = h1.shape[0]
    tm = min(TM_IN, s)

    def body(h_ref, w_ref, u_ref, q_ref, k_ref, v_ref):
        proj = _dot_nt(h_ref[...], w_ref[...])
        u_ref[...] = proj[:, :512]
        q_ref[...] = proj[:, 512:1024].astype(BF16)
        k_ref[...] = proj[:, 1024:1152].astype(BF16)
        v_ref[...] = proj[:, 1152:1280].astype(BF16)

    row = lambda w: pl.BlockSpec((tm, w), lambda i: (i, 0))
    return pl.pallas_call(
        body, name="inproj_fwd", grid=(s // tm,),
        in_specs=[row(D), pl.BlockSpec((IN_W, D), lambda i: (0, 0))],
        out_specs=[row(512), row(512), row(128), row(128)],
        out_shape=[jax.ShapeDtypeStruct((s, 512), F32), jax.ShapeDtypeStruct((s, 512), BF16),
                   jax.ShapeDtypeStruct((s, 128), BF16), jax.ShapeDtypeStruct((s, 128), BF16)],
        compiler_params=_cp(1))(h1, w_in)


def _window_sums(ext, w, lag_sign, tm, terms):
    rows = lax.broadcasted_iota(jnp.int32, (HALO, 2 * HALO), 0)
    cols = lax.broadcasted_iota(jnp.int32, (HALO, 2 * HALO), 1)
    d = rows + HALO - cols if lag_sign > 0 else cols - rows
    band = jnp.where((d >= 0) & (d < w), 1.0, 0.0).astype(BF16)
    return jnp.concatenate([_band_dot(band, ext[r:r + 2 * HALO], terms) for r in range(0, tm, HALO)], axis=0)


def _pooled(ext, cur, t0, tm):
    t = t0 + lax.broadcasted_iota(jnp.int32, (tm, GROUP), 0)
    out = []
    for g, w in enumerate(WINDOWS):
        sl = slice(g * GROUP, (g + 1) * GROUP)
        cnt = jnp.minimum(t + 1, w).astype(F32)
        out.append(_window_sums(ext[:, sl], w, 1, tm, 2) / cnt - cur[:, sl])
    return out


def _pool_fwd(u, w_pool, pool_scale):
    s = u.shape[0]
    tm = min(TM_POOL, s)
    hb = tm // HALO

    def body(uc_ref, uh_ref, wp_ref, sc_ref, o_ref, pooled_ref):
        i = pl.program_id(0)
        cur = uc_ref[...]
        halo = jnp.where(i > 0, uh_ref[...], 0.0)
        ext = jnp.concatenate([halo, cur], axis=0)
        pooled = _pooled(ext, cur, i * tm, tm)
        for g in range(4):
            sl = slice(g * GROUP, (g + 1) * GROUP)
            pb = pooled[g].astype(BF16)
            pooled_ref[:, sl] = pb
            o_ref[:, sl] = (_dot(pb, wp_ref[g]) * sc_ref[:, sl]).astype(BF16)

    row = pl.BlockSpec((tm, 512), lambda i: (i, 0))
    return pl.pallas_call(
        body, name="pool_fwd", grid=(s // tm,),
        in_specs=[row, pl.BlockSpec((HALO, 512), lambda i: (jnp.maximum(i * hb - 1, 0), 0)),
                  pl.BlockSpec((4, GROUP, GROUP), lambda i: (0, 0, 0)),
                  pl.BlockSpec((1, 512), lambda i: (0, 0))],
        out_specs=[row, row],
        out_shape=[jax.ShapeDtypeStruct((s, 512), BF16)] * 2,
        compiler_params=_cp(1))(u, u, w_pool, pool_scale)


def _bias_table(bucket, rel_bias):
    def body(b_ref, rb_ref, o_ref):
        bucket_v = b_ref[...]
        key = lax.broadcasted_iota(jnp.int32, (2 * BLK, BLK), 0)
        dist = lax.broadcasted_iota(jnp.int32, (2 * BLK, BLK), 1) + BLK - key
        inwin = (dist >= 0) & (dist < BLK)
        for h in range(NQ):
            acc = jnp.zeros((2 * BLK, BLK), F32)
            for b in range(NBUCKET):
                acc = jnp.where(bucket_v == b, rb_ref[b, h], acc)
            rest = jnp.where(inwin, acc, NEG)
            o_ref[1, h] = rest
            o_ref[0, h] = jnp.where(key >= BLK, rest, NEG)

    return pl.pallas_call(
        body, name="bias_table",
        in_specs=[pl.BlockSpec(memory_space=pltpu.VMEM), pl.BlockSpec(memory_space=pltpu.SMEM)],
        out_specs=pl.BlockSpec(memory_space=pltpu.VMEM),
        out_shape=jax.ShapeDtypeStruct((2, NQ, 2 * BLK, BLK), F32),
        compiler_params=_cp())(bucket, rel_bias)


HD = 64


def _kv_band(ref, n, transposed):
    pstart = pl.multiple_of(jnp.maximum(n - 1, 0) * BLK, BLK)
    cstart = pl.multiple_of(n * BLK, BLK)
    lo = lax.broadcasted_iota(jnp.int32, (2 * BLK, 128), 1) < HD
    band = jnp.concatenate([ref[pl.ds(pstart, BLK), :], ref[pl.ds(cstart, BLK), :]], axis=0).astype(F32)
    rot = pltpu.roll(band, HD, axis=1)
    halves = ((jnp.where(lo, band, 0.0), jnp.where(lo, 0.0, rot)), (jnp.where(lo, rot, 0.0), jnp.where(lo, 0.0, band)))
    if transposed:
        out = [jnp.concatenate([a.T, b.T], axis=1).astype(BF16) for a, b in halves]
    else:
        out = [jnp.concatenate([a, b], axis=0).astype(BF16) for a, b in halves]
    return out, (lo, pstart, cstart)


def _pair_probs(qt, kk, bias, sk_ref, p):
    sink = jnp.concatenate([jnp.full((1, 1, BLK), sk_ref[0, 2 * p + e], F32) for e in range(2)], axis=0)
    s = _dot_nt(kk, qt).reshape(2, 2 * BLK, BLK) + bias
    m = jnp.maximum(jnp.max(s, axis=1, keepdims=True), sink)
    pr = jnp.exp(s - m)
    es = jnp.exp(sink - m)
    inv = 1.0 / (jnp.sum(pr, axis=1, keepdims=True) + es)
    return pr * inv, es * inv


def _attn_fwd(q, k, v, bias, sinks):
    s = q.shape[0]
    nblk = s // BLK

    def body(q_ref, k_ref, v_ref, b_ref, sk_ref, o_ref, prob_ref, ps_ref):
        n = pl.program_id(0)
        kk, _ = _kv_band(k_ref, n, False)
        vt, _ = _kv_band(v_ref, n, True)
        bidx = jnp.minimum(n, 1)
        for p in range(4):
            h = p // 2
            qt = (q_ref[:, p * 128:(p + 1) * 128].astype(F32) * SCALE).astype(BF16)
            prob, ps = _pair_probs(qt, kk[h], b_ref[bidx, pl.ds(2 * p, 2)], sk_ref, p)
            pb = prob.astype(BF16)
            prob_ref[pl.ds(2 * p, 2)] = pb
            ps_ref[pl.ds(2 * p, 2), :] = ps.reshape(2, BLK)
            acc_t = _dot(vt[h], pb.reshape(4 * BLK, BLK))
            o_ref[:, p * 128:(p + 1) * 128] = acc_t.T.astype(BF16)

    return pl.pallas_call(
        body, name="attn_fwd", grid=(nblk,),
        in_specs=[pl.BlockSpec((BLK, 512), lambda i: (i, 0)),
                  pl.BlockSpec((s, 128), lambda i: (0, 0)),
                  pl.BlockSpec((s, 128), lambda i: (0, 0)),
                  pl.BlockSpec((2, NQ, 2 * BLK, BLK), lambda i: (0, 0, 0, 0)),
                  pl.BlockSpec(memory_space=pltpu.SMEM)],
        out_specs=[pl.BlockSpec((BLK, 512), lambda i: (i, 0)),
                   pl.BlockSpec((None, NQ, 2 * BLK, BLK), lambda i: (i, 0, 0, 0)),
                   pl.BlockSpec((None, NQ, BLK), lambda i: (i, 0, 0))],
        out_shape=[jax.ShapeDtypeStruct((s, 512), BF16), jax.ShapeDtypeStruct((nblk, NQ, 2 * BLK, BLK), BF16),
                   jax.ShapeDtypeStruct((nblk, NQ, BLK), F32)],
        compiler_params=_cp(1))(q, k, v, bias, sinks)


def _outproj_fwd(pool_o, attn_o, w_out, x, g2, g3, between=None):
    s = x.shape[0]
    tm = min(TM, s)
    nt = s // tm

    def part(name, tile0, tiles, filled, dep):
        def body(p_ref, a_ref, w_ref, x_ref, g2_ref, g3_ref, *rest):
            mix_ref, x1_ref, h2_ref = rest[-3:]
            mix = _dot(p_ref[...], w_ref[0:512, :]) + _dot(a_ref[...], w_ref[512:1024, :])
            mix_ref[...] = mix
            _, n2 = _rms(mix)
            x1 = x_ref[...] + n2 * g2_ref[...]
            x1_ref[...] = x1
            _, n3 = _rms(x1)
            h2_ref[...] = (n3 * g3_ref[...]).astype(BF16)

        row = lambda w: pl.BlockSpec((tm, w), lambda i: (i + tile0, 0))
        vec = pl.BlockSpec((1, D), lambda i: (0, 0))
        extra = list(filled) + ([] if dep is None else [dep])
        return pl.pallas_call(
            body, name=name, grid=(tiles,),
            in_specs=[row(512), row(512), pl.BlockSpec((D, D), lambda i: (0, 0)), row(D), vec, vec]
            + [ANY] * len(extra),
            out_specs=[row(D), row(D), row(D)],
            out_shape=[jax.ShapeDtypeStruct((s, D), F32), jax.ShapeDtypeStruct((s, D), F32),
                       jax.ShapeDtypeStruct((s, D), BF16)],
            input_output_aliases={6 + t: t for t in range(len(filled))},
            compiler_params=_cp(1))(pool_o, attn_o, w_out, x, g2, g3, *extra)

    if between is None or nt < 2:
        return part("outproj_fwd", 0, nt, (), None)
    first = part("outproj_fwd_a", 0, nt // 2, (), None)
    return part("outproj_fwd_b", nt // 2, nt - nt // 2, first, between(first[2]))


def _silu_parts(g):
    sg = jax.nn.sigmoid(g)
    return sg, g * sg


def _ffn_fwd(h2, wg, wu, wd, x1, target, g4):
    s = h2.shape[0]
    tm = min(TM_FFN_FWD, s)

    def body(h_ref, wg_ref, wu_ref, wd_ref, x1_ref, t_ref, g4_ref,
             gate_ref, up_ref, a_ref, df_ref, dy_ref, loss_ref, gg4_ref, f_acc):
        i = pl.program_id(0)
        j = pl.program_id(1)
        first = j == 0
        chunks = [pl.ds(r, min(FFN_ROWS, tm)) for r in range(0, tm, FFN_ROWS)]
        project = lambda rows: (_dot_nt(h_ref[rows, :], wg_ref[...]), _dot_nt(h_ref[rows, :], wu_ref[...]))
        ahead = [project(chunks[0])]
        for k, rows in enumerate(chunks):
            if k + 1 < len(chunks):
                ahead.append(project(chunks[k + 1]))
            gate, up = ahead[k]
            gate_ref[rows, :] = gate.astype(BF16)
            up_ref[rows, :] = up.astype(BF16)
            _, sl = _silu_parts(gate)
            a = (sl * up).astype(BF16)
            a_ref[rows, :] = a
            contrib = _dot(a, wd_ref[...])
            f_acc[rows, :] = jnp.where(first, contrib, f_acc[rows, :] + contrib)

        @pl.when((i == 0) & (j == 0))
        def _():
            loss_ref[...] = jnp.zeros_like(loss_ref)
            gg4_ref[...] = jnp.zeros_like(gg4_ref)

        @pl.when(j == NSH - 1)
        def _():
            r4, n4 = _rms(f_acc[...])
            g4v = g4_ref[...]
            err = x1_ref[...] + n4 * g4v - t_ref[...]
            loss_ref[...] += (0.5 / D) * jnp.sum(err * err).reshape(1, 1)
            dy = err * (1.0 / D)
            dy_ref[...] = dy
            df, dg = _rms_bwd(r4, n4, g4v, dy)
            gg4_ref[...] += dg
            df_ref[...] = df.astype(BF16)

    row = lambda w: pl.BlockSpec((tm, w), lambda i, j: (i, 0))
    sh = lambda r, c: pl.BlockSpec((None, r, c), lambda i, j: (j, 0, 0))
    act = pl.BlockSpec((None, tm, FS), lambda i, j: (j, i, 0))
    vec = pl.BlockSpec((1, D), lambda i, j: (0, 0))
    return pl.pallas_call(
        body, name="ffn_fwd", grid=(s // tm, NSH),
        in_specs=[row(D), sh(FS, D), sh(FS, D), sh(FS, D), row(D), row(D), vec],
        out_specs=[act, act, act, row(D), row(D), pl.BlockSpec((1, 1), lambda i, j: (0, 0)), vec],
        out_shape=[jax.ShapeDtypeStruct((NSH, s, FS), BF16)] * 3
        + [jax.ShapeDtypeStruct((s, D), BF16), jax.ShapeDtypeStruct((s, D), F32),
           jax.ShapeDtypeStruct((1, 1), F32), jax.ShapeDtypeStruct((1, D), F32)],
        scratch_shapes=[pltpu.VMEM((tm, D), F32)],
        compiler_params=_cp(2))(h2, wg, wu, wd, x1, target, g4)


def _ffn_bwd_act(df, gate, up, wg, wu, wd, dy, x1, mix, g3, g2):
    s = df.shape[0]
    tm = min(TM_FFN, s)

    def body(df_ref, gate_ref, up_ref, wg_ref, wu_ref, wd_ref, dy_ref, x1_ref, mix_ref, g3_ref, g2_ref,
             dgate_ref, dup_ref, dx1_ref, dmix_ref, gg3_ref, gg2_ref, acc):
        j = pl.program_id(0)
        i = pl.program_id(1)
        first = j == 0
        tile = pl.multiple_of(i * tm, tm)
        chunks = [pl.ds(r, min(FFN_ROWS, tm)) for r in range(0, tm, FFN_ROWS)]

        @pl.when((i == 0) & (j == 0))
        def _():
            gg3_ref[...] = jnp.zeros_like(gg3_ref)
            gg2_ref[...] = jnp.zeros_like(gg2_ref)

        def norms_backward(rows, dh2):
            r3, n3 = _rms(x1_ref[rows, :])
            dx1n, dg3 = _rms_bwd(r3, n3, g3_ref[...], dh2)
            dx1 = dy_ref[rows, :] + dx1n
            dx1_ref[rows, :] = dx1
            gg3_ref[...] += dg3
            r2, n2 = _rms(mix_ref[rows, :])
            dmix, dg2 = _rms_bwd(r2, n2, g2_ref[...], dx1)
            gg2_ref[...] += dg2
            dmix_ref[rows, :] = dmix.astype(BF16)

        def shard_pass(last):
            das = [_dot_nt(df_ref[chunks[0], :], wd_ref[...])]
            for k, rows in enumerate(chunks):
                if k + 1 < len(chunks):
                    das.append(_dot_nt(df_ref[chunks[k + 1], :], wd_ref[...]))
                da = das[k]
                g = gate_ref[rows, :].astype(F32)
                u = up_ref[rows, :].astype(F32)
                sg, sl = _silu_parts(g)
                dgate = (da * u * (sg * (1.0 + g * (1.0 - sg)))).astype(BF16)
                dup = (da * sl).astype(BF16)
                dgate_ref[rows, :] = dgate
                dup_ref[rows, :] = dup
                contrib = _dot(dgate, wg_ref[...]) + _dot(dup, wu_ref[...])
                acc_rows = pl.ds(tile + k * FFN_ROWS, rows.size)
                if last:
                    norms_backward(rows, acc[acc_rows, :] + contrib)
                else:
                    acc[acc_rows, :] = jnp.where(first, contrib, acc[acc_rows, :] + contrib)

        pl.when(j < NSH - 1)(functools.partial(shard_pass, False))
        pl.when(j == NSH - 1)(functools.partial(shard_pass, True))

    row = pl.BlockSpec((tm, D), lambda j, i: (i, 0))
    last = pl.BlockSpec((tm, D), lambda j, i: (i * (j // (NSH - 1)), 0))
    sh = pl.BlockSpec((None, FS, D), lambda j, i: (j, 0, 0))
    act = pl.BlockSpec((None, tm, FS), lambda j, i: (j, i, 0))
    vec = pl.BlockSpec((1, D), lambda j, i: (0, 0))
    return pl.pallas_call(
        body, name="ffn_bwd_act", grid=(NSH, s // tm),
        in_specs=[row, act, act, sh, sh, sh, last, last, last, vec, vec],
        out_specs=[act, act, last, last, vec, vec],
        out_shape=[jax.ShapeDtypeStruct((NSH, s, FS), BF16), jax.ShapeDtypeStruct((NSH, s, FS), BF16),
                   jax.ShapeDtypeStruct((s, D), F32), jax.ShapeDtypeStruct((s, D), BF16),
                   jax.ShapeDtypeStruct((1, D), F32), jax.ShapeDtypeStruct((1, D), F32)],
        scratch_shapes=[pltpu.VMEM((s, D), F32)],
        compiler_params=_cp(2))(df, gate, up, wg, wu, wd, dy, x1, mix, g3, g2)


SMEM_SPEC = pl.BlockSpec(memory_space=pltpu.SMEM)


def _emit_halves(acc_ref, row0, rows, c, own_ref, oth_ref):
    half = rows // 2
    own_ref[...] = acc_ref[pl.ds(row0 + pl.multiple_of(c * half, 8), half), :]
    oth_ref[...] = acc_ref[pl.ds(row0 + pl.multiple_of((1 - c) * half, 8), half), :].astype(BF16)


def _half_shapes(rows):
    return [jax.ShapeDtypeStruct((NSH, rows // 2, D), F32), jax.ShapeDtypeStruct((NSH, rows // 2, D), BF16)]


def _ffn_bwd_w(h2, act_a, dgate, dup, df, c_arr):
    s = h2.shape[0]
    tm = min(TM_FFN_FWD, s)
    nt = s // tm

    def body(c_ref, h_ref, a_ref, dgate_ref, dup_ref, df_ref, *rest):
        outs, accs = rest[:6], rest[6:]
        i = pl.program_id(1)

        @pl.when(i == 0)
        def _():
            for acc in accs:
                acc[...] = jnp.zeros_like(acc)

        h = h_ref[...]
        accs[0][...] += _dot_tn(dgate_ref[...], h)
        accs[1][...] += _dot_tn(dup_ref[...], h)
        accs[2][...] += _dot_tn(a_ref[...], df_ref[...])

        @pl.when(i == nt - 1)
        def _():
            for t, acc in enumerate(accs):
                _emit_halves(acc, 0, FS, c_ref[0], outs[2 * t], outs[2 * t + 1])

    row = lambda w: pl.BlockSpec((tm, w), lambda j, i: (i, 0))
    act = pl.BlockSpec((None, tm, FS), lambda j, i: (j, i, 0))
    half = pl.BlockSpec((None, FS // 2, D), lambda j, i: (j, 0, 0))
    return pl.pallas_call(
        body, name="ffn_bwd_w", grid=(NSH, nt),
        in_specs=[SMEM_SPEC, row(D), act, act, act, row(D)],
        out_specs=[half] * 6,
        out_shape=_half_shapes(FS) * 3,
        scratch_shapes=[pltpu.VMEM((FS, D), F32)] * 3,
        compiler_params=_cp(2))(c_arr, h2, act_a, dgate, dup, df)


def _outproj_bwd(dmix, w_out, pool_o, attn_o, c_arr, dep=None):
    s = dmix.shape[0]
    tm = min(TM, s)
    nt = s // tm

    def body(c_ref, dm_ref, w_ref, p_ref, a_ref, *rest):
        dpool_ref, dattn_ref, own_ref, oth_ref, gw_ref = rest[-5:]
        i = pl.program_id(0)

        @pl.when(i == 0)
        def _():
            gw_ref[...] = jnp.zeros_like(gw_ref)

        dm = dm_ref[...]
        dpool_ref[...] = _dot_nt(dm, w_ref[0:512, :])
        dattn_ref[...] = _dot_nt(dm, w_ref[512:1024, :]).astype(BF16)
        gw_ref[0:512, :] += _dot_tn(p_ref[...], dm)
        gw_ref[512:1024, :] += _dot_tn(a_ref[...], dm)

        @pl.when(i == nt - 1)
        def _():
            for k in range(NSH):
                _emit_halves(gw_ref, k * WOUT_S, WOUT_S, c_ref[0], own_ref.at[k], oth_ref.at[k])

    row = lambda w: pl.BlockSpec((tm, w), lambda i: (i, 0))
    full = pl.BlockSpec((D, D), lambda i: (0, 0))
    half = pl.BlockSpec((NSH, WOUT_S // 2, D), lambda i: (0, 0, 0))
    return pl.pallas_call(
        body, name="outproj_bwd", grid=(nt,),
        in_specs=[SMEM_SPEC, row(D), full, row(512), row(512)] + ([] if dep is None else [ANY]),
        out_specs=[row(512), row(512), half, half],
        out_shape=[jax.ShapeDtypeStruct((s, 512), F32), jax.ShapeDtypeStruct((s, 512), BF16)] + _half_shapes(WOUT_S),
        scratch_shapes=[pltpu.VMEM((D, D), F32)],
        compiler_params=_cp(1))(c_arr, dmix, w_out, pool_o, attn_o, *([] if dep is None else [dep]))


def _pool_bwd(pooled, dpool, w_pool, pool_scale, dep=None):
    s = pooled.shape[0]
    tm = min(TM_POOL, s)
    hb = tm // HALO
    nt = s // tm
    last_halo = s // HALO - 1

    def body(p_ref, dc_ref, dn_ref, wp_ref, sc_ref, *rest):
        du_ref, gwp_ref, gsc_ref = rest[-3:]
        i = pl.program_id(0)

        @pl.when(i == 0)
        def _():
            gwp_ref[...] = jnp.zeros_like(gwp_ref)
            gsc_ref[...] = jnp.zeros_like(gsc_ref)

        dcur = dc_ref[...]
        dnext = jnp.where(i < nt - 1, dn_ref[...], 0.0)
        dext = jnp.concatenate([dcur, dnext], axis=0)
        t_ext = i * tm + lax.broadcasted_iota(jnp.int32, (tm + HALO, GROUP), 0)
        for g, w in enumerate(WINDOWS):
            sl = slice(g * GROUP, (g + 1) * GROUP)
            pb = p_ref[:, sl]
            wp = wp_ref[g]
            mixed = _dot(pb, wp)
            gsc_ref[:, sl] += jnp.sum(dcur[:, sl] * mixed, axis=0, keepdims=True)
            dmixed = (dext[:, sl] * sc_ref[:, sl]).astype(BF16)
            gwp_ref[g] += _dot_tn(pb, dmixed[0:tm])
            dpooled = _dot_nt(dmixed, wp)
            z = dpooled / jnp.minimum(t_ext + 1, w).astype(F32)
            du_ref[:, sl] = (_window_sums(z, w, -1, tm, 2) - dpooled[0:tm]).astype(BF16)

    return pl.pallas_call(
        body, name="pool_bwd", grid=(nt,),
        in_specs=[pl.BlockSpec((tm, 512), lambda i: (i, 0)),
                  pl.BlockSpec((tm, 512), lambda i: (i, 0)),
                  pl.BlockSpec((HALO, 512), lambda i: (jnp.minimum((i + 1) * hb, last_halo), 0)),
                  pl.BlockSpec((4, GROUP, GROUP), lambda i: (0, 0, 0)),
                  pl.BlockSpec((1, 512), lambda i: (0, 0))] + ([] if dep is None else [ANY]),
        out_specs=[pl.BlockSpec((tm, 512), lambda i: (i, 0)),
                   pl.BlockSpec((4, GROUP, GROUP), lambda i: (0, 0, 0)),
                   pl.BlockSpec((1, 512), lambda i: (0, 0))],
        out_shape=[jax.ShapeDtypeStruct((s, 512), BF16), jax.ShapeDtypeStruct((4, GROUP, GROUP), F32),
                   jax.ShapeDtypeStruct((1, 512), F32)],
        compiler_params=_cp(1))(pooled, dpool, dpool, w_pool, pool_scale, *([] if dep is None else [dep]))


def _attn_bwd(q, k, v, do, probs, sink_share, bucket, dep=None):
    s = q.shape[0]
    nblk = s // BLK

    def body(q_ref, do_ref, k_ref, v_ref, prob_ref, ps_ref, bk_ref, *rest):
        dq_ref, dk_ref, dv_ref, grb_ref, gsk_ref, dss_ref = rest[-6:]
        n = pl.program_id(0)

        @pl.when(n == 0)
        def _():
            dk_ref[...] = jnp.zeros_like(dk_ref)
            dv_ref[...] = jnp.zeros_like(dv_ref)
            dss_ref[...] = jnp.zeros_like(dss_ref)
            gsk_ref[...] = jnp.zeros_like(gsk_ref)

        kt, (lo, pstart, cstart) = _kv_band(k_ref, n, True)
        vv, _ = _kv_band(v_ref, n, False)
        lo_q = lax.broadcasted_iota(jnp.int32, (BLK, 128), 1) < HD
        dkk = [jnp.zeros((2 * BLK, 128), F32), jnp.zeros((2 * BLK, 128), F32)]
        dvv = [jnp.zeros((2 * BLK, 128), F32), jnp.zeros((2 * BLK, 128), F32)]

        def by_head(t):
            return jnp.concatenate([jnp.where(lo_q, t, 0.0), jnp.where(lo_q, 0.0, t)], axis=0).astype(BF16)

        for p in range(4):
            h = p // 2
            qt = q_ref[:, p * 128:(p + 1) * 128].astype(F32) * SCALE
            dot = do_ref[:, p * 128:(p + 1) * 128]
            pb = prob_ref[pl.ds(2 * p, 2)]
            prob = pb.astype(F32)
            ps = ps_ref[pl.ds(2 * p, 2), :].reshape(2, 1, BLK)
            dp = _dot_nt(vv[h], dot).reshape(2, 2 * BLK, BLK)
            delta = jnp.sum(prob * dp, axis=1, keepdims=True)
            ds = prob * (dp - delta)
            sink_part = ps * delta
            for e in range(2):
                gs = -jnp.sum(sink_part[e]).reshape(1, 1)
                gsk_ref[pl.ds(2 * p + e, 1), :] += jnp.broadcast_to(gs, (1, 128))
            dss_ref[pl.ds(2 * p, 2)] += ds
            dsb = ds.astype(BF16)
            dq_t = _dot(kt[h], dsb.reshape(4 * BLK, BLK))
            dq_ref[:, p * 128:(p + 1) * 128] = (dq_t.T * SCALE).astype(BF16)
            dkk[h] = dkk[h] + _dot(jnp.concatenate([dsb[0], dsb[1]], axis=1), by_head(qt))
            dvv[h] = dvv[h] + _dot(jnp.concatenate([pb[0], pb[1]], axis=1), by_head(dot.astype(F32)))
        for acc, ref in ((dkk, dk_ref), (dvv, dv_ref)):
            f0 = acc[0] + pltpu.roll(acc[0], HD, axis=1)
            f1 = acc[1] + pltpu.roll(acc[1], HD, axis=1)
            band = jnp.where(lo, f0, f1)
            ref[pl.ds(pstart, BLK), :] += band[0:BLK]
            ref[pl.ds(cstart, BLK), :] += band[BLK:2 * BLK]

        @pl.when(n == nblk - 1)
        def _():
            _relbias_grad(dss_ref, bk_ref[...], grb_ref)

    blk = pl.BlockSpec((BLK, 512), lambda i: (i, 0))
    kv = pl.BlockSpec((s, 128), lambda i: (0, 0))
    row8 = pl.BlockSpec((NQ, 128), lambda i: (0, 0))
    return pl.pallas_call(
        body, name="attn_bwd", grid=(nblk,),
        in_specs=[blk, blk, kv, kv, pl.BlockSpec((None, NQ, 2 * BLK, BLK), lambda i: (i, 0, 0, 0)),
                  pl.BlockSpec((None, NQ, BLK), lambda i: (i, 0, 0)), pl.BlockSpec((2 * BLK, BLK), lambda i: (0, 0))]
        + ([] if dep is None else [ANY]),
        out_specs=[blk, kv, kv, row8, row8],
        out_shape=[jax.ShapeDtypeStruct((s, 512), BF16), jax.ShapeDtypeStruct((s, 128), F32),
                   jax.ShapeDtypeStruct((s, 128), F32), jax.ShapeDtypeStruct((NQ, 128), F32),
                   jax.ShapeDtypeStruct((NQ, 128), F32)],
        scratch_shapes=[pltpu.VMEM((NQ, 2 * BLK, BLK), F32)],
        compiler_params=_cp(1))(q, do, k, v, probs, sink_share, bucket, *([] if dep is None else [dep]))


def _relbias_grad(ds_ref, bucket_v, o_ref):
    lane = lax.broadcasted_iota(jnp.int32, (NQ, 128), 1)
    head_row = lax.broadcasted_iota(jnp.int32, (NQ, BLK), 0)
    acc = jnp.zeros((NQ, 128), F32)
    for b in range(NBUCKET):
        sel = bucket_v == b
        stack = jnp.zeros((NQ, BLK), F32)
        for h in range(NQ):
            col_sums = jnp.sum(jnp.where(sel, ds_ref[h], 0.0), axis=0, keepdims=True)
            stack = jnp.where(head_row == h, col_sums, stack)
        acc = acc + jnp.where(lane == b, jnp.sum(stack, axis=1, keepdims=True), 0.0)
    o_ref[...] = acc


def _inproj_bwd(x, g1, du, dq, dk, dv, w_in, dx1, c_arr):
    s = x.shape[0]
    tm = min(TM, s)
    nt = s // tm
    cols = ((0, 512), (512, 1024), (1024, 1152), (1152, 1280))

    def body(c_ref, x_ref, g_ref, du_ref, dq_ref, dk_ref, dv_ref, w_ref, dx1_ref,
             gx_ref, own_ref, oth_ref, gg_ref, gw_ref):
        i = pl.program_id(0)

        @pl.when(i == 0)
        def _():
            gw_ref[...] = jnp.zeros_like(gw_ref)
            gg_ref[...] = jnp.zeros_like(gg_ref)

        parts = (du_ref[...], dq_ref[...], dk_ref[...].astype(BF16), dv_ref[...].astype(BF16))
        dh = None
        for (a, b), dpart in zip(cols, parts):
            t = _dot(dpart, w_ref[a:b, :])
            dh = t if dh is None else dh + t
        gv = g_ref[...]
        r1, n1 = _rms(x_ref[...])
        h = (n1 * gv).astype(BF16)
        for (a, b), dpart in zip(cols, parts):
            gw_ref[a:b, :] += _dot_tn(dpart, h)
        dx, dg = _rms_bwd(r1, n1, gv, dh)
        gg_ref[...] += dg
        gx_ref[...] = dx1_ref[...] + dx

        @pl.when(i == nt - 1)
        def _():
            for k in range(NSH):
                _emit_halves(gw_ref, k * WIN_S, WIN_S, c_ref[0], own_ref.at[k], oth_ref.at[k])

    row = lambda w: pl.BlockSpec((tm, w), lambda i: (i, 0))
    vec = pl.BlockSpec((1, D), lambda i: (0, 0))
    full = pl.BlockSpec((IN_W, D), lambda i: (0, 0))
    half = pl.BlockSpec((NSH, WIN_S // 2, D), lambda i: (0, 0, 0))
    return pl.pallas_call(
        body, name="inproj_bwd", grid=(nt,),
        in_specs=[SMEM_SPEC, row(D), vec, row(512), row(512), row(128), row(128), full, row(D)],
        out_specs=[row(D), half, half, vec],
        out_shape=[jax.ShapeDtypeStruct((s, D), F32)] + _half_shapes(WIN_S) + [jax.ShapeDtypeStruct((1, D), F32)],
        scratch_shapes=[pltpu.VMEM((IN_W, D), F32)],
        compiler_params=_cp(1))(c_arr, x, g1, du, dq, dk, dv, w_in, dx1)


def _local_step(x, target, small, w_in, w_out, ffn_weights, c_arr, on_ffn_grads=None, on_wout_grads=None,
                mid_outproj=None, start_dep=None):
    g1, g2, g3, g4, w_pool, pool_scale, rel_bias, sinks = small
    bucket = jnp.asarray(_bucket_table())
    wp_bf = w_pool.astype(BF16)
    bias = _bias_table(bucket, rel_bias)
    h1 = _prenorm(x, g1, start_dep)
    if callable(w_in):
        w_in = w_in(bias, h1)
    u, q, k, v = _inproj_fwd(h1, w_in)
    pool_o, pooled = _pool_fwd(u, wp_bf, pool_scale)
    attn_o, probs, sink_share = _attn_fwd(q, k, v, bias, sinks)
    g2_fwd = g2
    if callable(w_out):
        w_out, after = w_out(pool_o, attn_o)
        if after is not None:
            g2_fwd = g2 + after[:1, :1]
    mix, x1, h2 = _outproj_fwd(pool_o, attn_o, w_out, x, g2_fwd, g3, mid_outproj)
    wg, wu, wd = ffn_weights(h2) if callable(ffn_weights) else ffn_weights
    gate, up, act_a, df, dy, loss, gg4 = _ffn_fwd(h2, wg, wu, wd, x1, target, g4)
    dgate, dup, dx1, dmix, gg3, gg2 = _ffn_bwd_act(df, gate, up, wg, wu, wd, dy, x1, mix, g3, g2)
    ffn_g = _ffn_bwd_w(h2, act_a, dgate, dup, df, c_arr)
    dep = None if on_ffn_grads is None else on_ffn_grads(ffn_g)
    dpool, dattn, wout_own, wout_oth = _outproj_bwd(dmix, w_out, pool_o, attn_o, c_arr, dep)
    dep = None if on_wout_grads is None else on_wout_grads(wout_own, wout_oth, dpool)
    du, gwp, gsc = _pool_bwd(pooled, dpool, wp_bf, pool_scale, dep)
    dq, dk, dv, grb, gsk = _attn_bwd(q, k, v, dattn, probs, sink_share, bucket, dep)
    gx, win_own, win_oth, gg1 = _inproj_bwd(x, g1, du, dq, dk, dv, w_in, dx1, c_arr)
    small_grads = (gg1, gg2, gg3, gg4, gwp, gsc, grb, gsk[:, 0].reshape(1, NQ))
    return loss, gx, small_grads, ((win_own, win_oth), (wout_own, wout_oth), ffn_g)


def _place():
    x, y, c = lax.axis_index("x"), lax.axis_index("y"), lax.axis_index("c")
    chips = ((1 - x, y), (x, 1 - y), (1 - x, 1 - y))
    return x, y, c, chips


def _remote(src, dst, ssem, rsem, dev):
    return pltpu.make_async_remote_copy(src_ref=src, dst_ref=dst, send_sem=ssem, recv_sem=rsem,
                                        device_id=dev, device_id_type=MESH_T)


def _to_sibling(arrs, name, after=()):
    nt, na = len(arrs), len(after)

    def body(*refs):
        srcs, dsts = refs[:nt], refs[nt + na:2 * nt + na]
        ssem, rsem = refs[2 * nt + na:]
        x, y, c, _ = _place()
        cps = [_remote(srcs[t], dsts[t], ssem.at[t], rsem.at[t], (x, y, 1 - c)) for t in range(nt)]
        for cp in cps:
            cp.start()
        for cp in cps:
            cp.wait()

    return pl.pallas_call(
        body, name=name, in_specs=[ANY] * (nt + na), out_specs=[ANY] * nt,
        out_shape=[jax.ShapeDtypeStruct(a.shape, a.dtype) for a in arrs],
        scratch_shapes=[pltpu.SemaphoreType.DMA((nt,)), pltpu.SemaphoreType.DMA((nt,))],
        compiler_params=_cp())(*arrs, *after)


HBM_SPEC = pl.BlockSpec(memory_space=pltpu.HBM)
SEM_SPEC = pl.BlockSpec(memory_space=pltpu.SEMAPHORE)
DATAFLOW = pltpu.SideEffectType.DATAFLOW_SIDE_EFFECTING


def _cast_into_slots(ws, chip_arr):
    nt, tiles = len(ws), 4

    def body(chip_ref, *refs):
        for t in range(nt):
            refs[nt + t][...] = refs[t][...].astype(BF16)

    return pl.pallas_call(
        body, name="cast_weights",
        grid_spec=pltpu.PrefetchScalarGridSpec(
            num_scalar_prefetch=1, grid=(tiles,),
            in_specs=[pl.BlockSpec((w.shape[0] // tiles, w.shape[1]), lambda i, chip_ref: (i, 0)) for w in ws],
            out_specs=[pl.BlockSpec((None, w.shape[0] // tiles, w.shape[1]), lambda i, chip_ref: (chip_ref[0], i, 0))
                       for w in ws]),
        out_shape=[jax.ShapeDtypeStruct((NSH,) + w.shape, BF16) for w in ws],
        compiler_params=_cp(1))(chip_arr, *ws)


def _gather_copies(srcs, lands, ssem, rsem, first=0):
    x, y, c, chips = _place()
    me = 2 * x + y
    out = []
    for t in range(len(lands)):
        half = lands[t].shape[1] // 2
        mine = pl.ds(pl.multiple_of(c * half, 16), half)
        for j, (px, py) in enumerate(chips):
            k = 3 * (first + t) + j
            send = _remote(lands[t].at[me, mine], lands[t].at[me, mine], ssem.at[k], rsem.at[k], (px, py, c))
            blk = lands[t].at[2 * px + py, mine]
            out.append((send, _remote(blk, blk, ssem.at[k], rsem.at[k], (px, py, c))))
    return out


def _scatter_copies(srcs, lands, ssem, rsem):
    x, y, c, chips = _place()
    out = []
    for t in range(len(srcs)):
        for j, (px, py) in enumerate(chips):
            k = 3 * t + j
            send = _remote(srcs[t].at[2 * px + py], lands[t].at[j], ssem.at[k], rsem.at[k], (px, py, c))
            out.append((send, _remote(lands[t].at[j], lands[t].at[j], ssem.at[k], rsem.at[k], (px, py, c))))
    return out


def _sibling_copies(srcs, lands, ssem, rsem):
    x, y, c, _ = _place()
    sib = (x, y, 1 - c)
    return [(_remote(srcs[t], lands[t], ssem.at[t], rsem.at[t], sib),
             _remote(lands[t], lands[t], ssem.at[t], rsem.at[t], sib)) for t in range(len(srcs))]


def _peer_copies(srcs, lands, ssem, rsem):
    x, y, c, _ = _place()
    me = 4 * x + 2 * y + c
    out = []
    for kk in range(1, 8):
        px, py, pc = x ^ (kk >> 2), y ^ ((kk >> 1) & 1), c ^ (kk & 1)
        send = _remote(srcs[0], lands[0].at[me], ssem.at[kk - 1], rsem.at[kk - 1], (px, py, pc))
        slot = lands[0].at[4 * px + 2 * py + pc]
        out.append((send, _remote(slot, slot, ssem.at[kk - 1], rsem.at[kk - 1], (px, py, pc))))
    return out


def _forward_copies(srcs, lands, ssem, rsem):
    x, y, c, chips = _place()
    sib = (x, y, 1 - c)
    out = []
    for t in range(len(lands)):
        half = lands[t].shape[1] // 2
        mine = pl.ds(pl.multiple_of(c * half, 16), half)
        other = pl.ds(pl.multiple_of((1 - c) * half, 16), half)
        for j, (px, py) in enumerate(chips):
            k = 3 * t + j
            blk_m, blk_o = lands[t].at[2 * px + py, mine], lands[t].at[2 * px + py, other]
            out.append((_remote(blk_m, blk_m, ssem.at[k], rsem.at[k], sib),
                        _remote(blk_o, blk_o, ssem.at[k], rsem.at[k], sib)))
    return out


def _win_and_small_copies(srcs, lands, ssem, rsem):
    return (_scatter_copies(srcs[:1], lands[:1], ssem, rsem)
            + _peer_copies(srcs[1:], lands[1:], ssem.at[pl.ds(3, 7)], rsem.at[pl.ds(3, 7)]))


def _split_start(plan, ncopy, srcs, lands, after, name):
    ns, nbuf = len(srcs), len(srcs) + len(lands)
    extra = [] if after is None else [after]
    n_in = nbuf + len(extra)

    def body(*refs):
        ssem, rsem, token = refs[n_in], refs[n_in + 1], refs[-1]
        for send, _ in plan(refs[:ns], refs[ns:nbuf], ssem, rsem):
            send.start()
        token[...] = jnp.zeros_like(token)

    bufs = [pltpu.with_memory_space_constraint(a, pltpu.HBM) for a in list(srcs) + list(lands)]
    outs = pl.pallas_call(
        body, name=name, in_specs=[HBM_SPEC] * nbuf + [ANY] * len(extra),
        out_specs=[SEM_SPEC, SEM_SPEC] + [HBM_SPEC] * nbuf + [pl.BlockSpec(memory_space=pltpu.VMEM)],
        out_shape=[pltpu.SemaphoreType.DMA((ncopy,)), pltpu.SemaphoreType.DMA((ncopy,))]
        + [pltpu.HBM(a.shape, a.dtype) for a in bufs] + [jax.ShapeDtypeStruct((8, 128), F32)],
        input_output_aliases={i: 2 + i for i in range(nbuf)},
        compiler_params=pltpu.CompilerParams(has_side_effects=DATAFLOW))(*bufs, *extra)
    return outs[0], outs[1], outs[2:2 + ns], outs[2 + ns:2 + nbuf], outs[-1]


def _split_wait(plan, ssem, rsem, srcs, lands, after, name, only=None):
    ns, nbuf = len(srcs), len(srcs) + len(lands)

    def body(*refs):
        s_ref, r_ref = refs[nbuf], refs[nbuf + 1]
        for k, (send, recv) in enumerate(plan(refs[:ns], refs[ns:nbuf], s_ref, r_ref)):
            if only is None or k in only:
                send.wait_send()
                recv.wait_recv()

    outs = pl.pallas_call(
        body, name=name, in_specs=[HBM_SPEC] * nbuf + [SEM_SPEC, SEM_SPEC] + [ANY] * len(after),
        out_specs=[HBM_SPEC] * nbuf,
        out_shape=[pltpu.HBM(a.shape, a.dtype) for a in list(srcs) + list(lands)],
        input_output_aliases={i: i for i in range(nbuf)},
        compiler_params=pltpu.CompilerParams(has_side_effects=DATAFLOW))(*srcs, *lands, ssem, rsem, *after)
    return outs


def _gather_finish(lands, tag):
    nt = len(lands)

    def body(*refs):
        outs = refs[nt:2 * nt]
        dsend, drecv = refs[2 * nt:]
        x, y, c, chips = _place()
        sib = (x, y, 1 - c)
        sends = []
        for t in range(nt):
            half = outs[t].shape[1] // 2
            mine = pl.ds(pl.multiple_of(c * half, 16), half)
            for j, (px, py) in enumerate(chips):
                blk = outs[t].at[2 * px + py, mine]
                cp = _remote(blk, blk, dsend.at[t, j], drecv.at[t, j], sib)
                cp.start()
                sends.append(cp)
        for t in range(nt):
            half = outs[t].shape[1] // 2
            other = pl.ds(pl.multiple_of((1 - c) * half, 16), half)
            for j, (px, py) in enumerate(chips):
                blk = outs[t].at[2 * px + py, other]
                _remote(blk, blk, dsend.at[t, j], drecv.at[t, j], sib).wait_recv()
        for cp in sends:
            cp.wait_send()

    return pl.pallas_call(
        body, name="gather_finish_" + tag, in_specs=[ANY] * nt, out_specs=[ANY] * nt,
        out_shape=[jax.ShapeDtypeStruct(a.shape, a.dtype) for a in lands],
        input_output_aliases={t: t for t in range(nt)},
        scratch_shapes=[pltpu.SemaphoreType.DMA((nt, 3)), pltpu.SemaphoreType.DMA((nt, 3))],
        compiler_params=_cp())(*lands)


def _chip_partial(owns, recv, chip_arr, tag):
    nt = len(owns)

    def body(chip_ref, *refs):
        k = pl.program_id(0)
        for t in range(nt):
            p = refs[t][...] + refs[nt + t][...].astype(F32)
            refs[2 * nt + t][...] = p.astype(BF16)

            @pl.when(k == chip_ref[0])
            def _():
                refs[3 * nt + t][...] = p

    blk_specs = [pl.BlockSpec((None,) + a.shape[1:], lambda k, chip_ref: (k, 0, 0)) for a in owns]
    own_specs = [pl.BlockSpec(a.shape[1:], lambda k, chip_ref: (0, 0)) for a in owns]
    return pl.pallas_call(
        body, name="rs_chip_partial_" + tag,
        grid_spec=pltpu.PrefetchScalarGridSpec(num_scalar_prefetch=1, grid=(NSH,), in_specs=blk_specs * 2,
                                               out_specs=blk_specs + own_specs),
        out_shape=[jax.ShapeDtypeStruct(a.shape, BF16) for a in owns]
        + [jax.ShapeDtypeStruct(a.shape[1:], F32) for a in owns],
        compiler_params=_cp(1))(chip_arr, *owns, *recv)


def _sum_chips(own, recv, tag):
    nt = len(own)

    def body(*refs):
        outs = refs[2 * nt:]
        for t in range(nt):
            r = refs[nt + t]
            outs[t][...] = ((refs[t][...] + r[0].astype(F32)) + r[1].astype(F32)) + r[2].astype(F32)

    vm = pl.BlockSpec(memory_space=pltpu.VMEM)
    return pl.pallas_call(
        body, name="rs_sum_chips_" + tag, in_specs=[vm] * (2 * nt), out_specs=[vm] * nt,
        out_shape=[jax.ShapeDtypeStruct(a.shape, F32) for a in own],
        compiler_params=_cp())(*own, *recv)


def _sum_chips_shared(own, recv, tag):
    def body(own_ref, recv_ref, out_ref, sib_ref, ssem, rsem):
        out_ref[...] = ((own_ref[...] + recv_ref[0].astype(F32)) + recv_ref[1].astype(F32)) + recv_ref[2].astype(F32)
        x, y, c, _ = _place()
        cp = _remote(out_ref, sib_ref, ssem.at[0], rsem.at[0], (x, y, 1 - c))
        cp.start()
        cp.wait()

    vm = pl.BlockSpec(memory_space=pltpu.VMEM)
    return pl.pallas_call(
        body, name="rs_sum_share_" + tag, in_specs=[vm, vm], out_specs=[vm, ANY],
        out_shape=[jax.ShapeDtypeStruct(own.shape, F32)] * 2,
        scratch_shapes=[pltpu.SemaphoreType.DMA((1,)), pltpu.SemaphoreType.DMA((1,))],
        compiler_params=_cp())(own, recv)


def _adamw_math(w, g, m, v):
    m = ADAM_B1 * m + (1.0 - ADAM_B1) * g
    v = ADAM_B2 * v + (1.0 - ADAM_B2) * (g * g)
    m_hat = m / (1.0 - ADAM_B1 ** ADAM_STEP)
    v_hat = v / (1.0 - ADAM_B2 ** ADAM_STEP)
    delta = -ADAM_LR * (m_hat / (jnp.sqrt(v_hat) + ADAM_EPS) + ADAM_WD * w)
    return delta, m, v


ADAM_SPLIT = 4


def _adamw_shards(own, sib, ws, ms, vs, c_arr, tag):
    nt = len(own)

    def body(c_ref, *refs):
        hh = pl.program_id(0)
        mine = hh == c_ref[0]
        for t in range(nt):
            g = jnp.where(mine, refs[t][...], refs[nt + t][...])
            delta, m, v = _adamw_math(refs[2 * nt + t][...], g, refs[3 * nt + t][...], refs[4 * nt + t][...])
            refs[5 * nt + t][...] = g
            refs[6 * nt + t][...] = delta
            refs[7 * nt + t][...] = m
            refs[8 * nt + t][...] = v

    def tile(a):
        return (a.shape[0] // ADAM_SPLIT, a.shape[1])

    def half_index(used_when_mine):
        def index(hh, q, c_ref):
            mine = 1 - (hh - c_ref[0]) * (hh - c_ref[0])
            used = mine if used_when_mine else 1 - mine
            return (used * q + (1 - used) * hh * (ADAM_SPLIT - 1), 0)
        return index

    own_specs = [pl.BlockSpec(tile(a), half_index(True)) for a in own]
    sib_specs = [pl.BlockSpec(tile(a), half_index(False)) for a in own]
    full_specs = [pl.BlockSpec(tile(a), lambda hh, q, c_ref: (hh * ADAM_SPLIT + q, 0)) for a in own]
    return pl.pallas_call(
        body, name="adamw_shards_" + tag,
        grid_spec=pltpu.PrefetchScalarGridSpec(num_scalar_prefetch=1, grid=(2, ADAM_SPLIT),
                                               in_specs=own_specs + sib_specs + full_specs * 3,
                                               out_specs=full_specs * 4),
        out_shape=[jax.ShapeDtypeStruct(w.shape, F32) for w in ws] * 4,
        compiler_params=_cp(2))(c_arr, *own, *sib, *ws, *ms, *vs)


SMALL_WPOOL_ROW = 32
SMALL_SCALE_ROW = SMALL_WPOOL_ROW + 4 * GROUP
SMALL_BIAS_ROW = SMALL_SCALE_ROW + 8
SMALL_SINKS_ROW = SMALL_BIAS_ROW + NQ
SMALL_LOSS_ROW = SMALL_SINKS_ROW + 8


def _small_sum_adamw(gathered, wp, mp, vp):
    rows = wp.shape[0]

    def body(g_ref, w_ref, m_ref, v_ref, *rest):
        outs, res = rest[:-1], rest[-1]
        g = g_ref[0]
        for d in range(1, 8):
            g = g + g_ref[d]
        delta, m, v = _adamw_math(w_ref[...], g, m_ref[...], v_ref[...])
        for kind, val in enumerate((g, delta, m, v)):
            res[kind] = val
            gains = outs[8 * kind:8 * kind + 4]
            w_pool_o, scale_o, bias_o, sinks_o = outs[8 * kind + 4:8 * kind + 8]
            for t in range(4):
                for i in range(8):
                    gains[t][:, 128 * i:128 * (i + 1)] = res[kind, pl.ds(8 * t + i, 1), :]
            for grp in range(4):
                w_pool_o[grp] = res[kind, pl.ds(SMALL_WPOOL_ROW + GROUP * grp, GROUP), :]
            for i in range(4):
                scale_o[:, 128 * i:128 * (i + 1)] = res[kind, pl.ds(SMALL_SCALE_ROW + i, 1), :]
            bias_o[...] = res[kind, pl.ds(SMALL_BIAS_ROW, NQ), :][:, 0:NBUCKET]
            sinks_o[...] = res[kind, pl.ds(SMALL_SINKS_ROW, 1), :][:, 0:NQ]
        outs[-1][...] = res[0, pl.ds(SMALL_LOSS_ROW, 1), :]

    vm = pl.BlockSpec(memory_space=pltpu.VMEM)
    one_kind = [jax.ShapeDtypeStruct((1, D), F32)] * 4 + [
        jax.ShapeDtypeStruct((4, GROUP, GROUP), F32), jax.ShapeDtypeStruct((1, POOL_W), F32),
        jax.ShapeDtypeStruct((NQ, NBUCKET), F32), jax.ShapeDtypeStruct((1, NQ), F32)]
    return pl.pallas_call(
        body, name="small_sum_adamw", in_specs=[vm] * 4, out_specs=[vm] * 33,
        out_shape=one_kind * 4 + [jax.ShapeDtypeStruct((1, 128), F32)],
        scratch_shapes=[pltpu.VMEM((4, rows, 128), F32)],
        compiler_params=_cp())(gathered, wp, mp, vp)


def _pack_small(gains4, w_pool, pool_scale, rel_bias_t, sinks, loss):
    bias_rows = jnp.pad(rel_bias_t, ((0, 0), (0, 128 - rel_bias_t.shape[1])))
    parts = list(gains4) + [w_pool, pool_scale, bias_rows, sinks, loss]
    rows = []
    for a in parts:
        flat = a.reshape(-1)
        n = flat.shape[0]
        padded = -(-n // 1024) * 1024
        rows.append(jnp.pad(flat, (0, padded - n)).reshape(-1, 128))
    return jnp.concatenate(rows, axis=0)


def kernel(x, g_pre_mix, w_in, w_pool, pool_scale, rel_bias, sinks, w_out, g_post_mix, g_pre_ffn, w_gate, w_up, w_down, g_post_ffn, loss_target, m_g_pre_mix, m_w_in, m_w_pool, m_pool_scale, m_rel_bias, m_sinks, m_w_out, m_g_post_mix, m_g_pre_ffn, m_w_gate, m_w_up, m_w_down, m_g_post_ffn, v_g_pre_mix, v_w_in, v_w_pool, v_pool_scale, v_rel_bias, v_sinks, v_w_out, v_g_post_mix, v_g_pre_ffn, v_w_gate, v_w_up, v_w_down, v_g_post_ffn):
    c = lax.axis_index("c")
    chip = 2 * lax.axis_index("x") + lax.axis_index("y")

    def shards(a_in, a_out, a_gate, a_up, a_down):
        return (a_in[0].T, a_out[0], a_gate[0].T, a_up[0].T, a_down[0])

    c_arr = c.reshape(1).astype(jnp.int32)
    chip_arr = chip.reshape(1).astype(jnp.int32)

    big_w = shards(w_in, w_out, w_gate, w_up, w_down)
    g_ssem, g_rsem, _, g_lands, g_token = _split_start(
        _gather_copies, 15, [], _cast_into_slots(big_w, chip_arr), None, "gather_start")
    fw = {}

    def w_in_ready(*after):
        fw["first"] = _split_wait(_gather_copies, g_ssem, g_rsem, [], g_lands, after, "gather_win_wait",
                                  only=(0, 1, 2))
        (fw["win"],) = _gather_finish([fw["first"][0]], "win")
        return fw["win"].reshape(IN_W, D)

    def w_out_ready(*after):
        fw["second"] = _split_wait(functools.partial(_gather_copies, first=1), g_ssem, g_rsem, [],
                                   list(fw["first"][1:]), after, "gather_wout_wait", only=(0, 1, 2))
        (fw["wout"],) = _gather_finish([fw["second"][0]], "wout")
        return fw["wout"].reshape(D, D), None

    def ffn_forward_start(after):
        outs = _split_wait(functools.partial(_gather_copies, first=2), g_ssem, g_rsem, [], list(fw["second"][1:]),
                           [after], "gather_ffn_wait")
        fw["f"] = _split_start(_forward_copies, 9, [], list(outs), None, "gather_forward_start")
        return fw["f"][4]

    def ffn_weights(after):
        ssem, rsem, _, lands, _ = fw["f"]
        return _split_wait(_forward_copies, ssem, rsem, [], lands, [after], "gather_forward_wait")

    rs = {}

    def on_ffn_grads(ffn_g):
        oths = ffn_g[1::2]
        rs["ffn_own"] = ffn_g[0::2]
        rs["x"] = _split_start(_sibling_copies, 3, oths, [lax.empty(a.shape, BF16) for a in oths], ffn_g[0],
                               "rs_exchange_ffn_start")
        return rs["x"][4]

    def on_wout_grads(own, oth, after):
        ssem, rsem, srcs, lands, _ = rs["x"]
        recv_ffn = _split_wait(_sibling_copies, ssem, rsem, srcs, lands, [after], "rs_exchange_ffn_wait")[3:]
        recv_wout = _to_sibling([oth], "rs_exchange_wout")
        outs = _chip_partial([own] + list(rs["ffn_own"]), list(recv_wout) + list(recv_ffn), chip_arr, "early")
        rs["early_own"] = outs[4:]
        rs["s"] = _split_start(_scatter_copies, 12, outs[:4],
                               [lax.empty((3,) + a.shape[1:], BF16) for a in outs[:4]], outs[4], "scatter_early_start")
        return rs["s"][4]

    small = (g_pre_mix, g_post_mix, g_pre_ffn, g_post_ffn, w_pool[0], pool_scale, rel_bias, sinks)
    loss, gx, small_grads, ((win_own, win_oth), _, _) = _local_step(
        x[0], loss_target[0], small, w_in_ready, w_out_ready, ffn_weights, c_arr, on_ffn_grads, on_wout_grads,
        ffn_forward_start, g_token)

    ssem, rsem, srcs, lands, _ = rs["s"]
    recv_early = _split_wait(_scatter_copies, ssem, rsem, srcs, lands, [gx], "scatter_early_wait")[4:]
    finals = _sum_chips(list(rs["early_own"]), list(recv_early), "early")
    to_sib = [win_oth] + list(finals)
    sh_ssem, sh_rsem, sh_srcs, sh_lands, _ = _split_start(
        _sibling_copies, 5, to_sib, [lax.empty(a.shape, a.dtype) for a in to_sib], None, "rs_share_start")

    unused = jnp.zeros((1, 1), F32)
    gp = _pack_small(small_grads[:4], *small_grads[4:], loss)
    wp = _pack_small((g_pre_mix, g_post_mix, g_pre_ffn, g_post_ffn), w_pool, pool_scale, rel_bias.T, sinks, unused)
    mp = _pack_small((m_g_pre_mix, m_g_post_mix, m_g_pre_ffn, m_g_post_ffn), m_w_pool, m_pool_scale, m_rel_bias.T,
                     m_sinks, unused)
    vp = _pack_small((v_g_pre_mix, v_g_post_mix, v_g_pre_ffn, v_g_post_ffn), v_w_pool, v_pool_scale, v_rel_bias.T,
                     v_sinks, unused)

    moments = (shards(m_w_in, m_w_out, m_w_gate, m_w_up, m_w_down),
               shards(v_w_in, v_w_out, v_w_gate, v_w_up, v_w_down))
    sh_first = _split_wait(_sibling_copies, sh_ssem, sh_rsem, sh_srcs, sh_lands, [], "rs_share_win_wait", only=(0,))
    outs = _chip_partial([win_own], [sh_first[5]], chip_arr, "win")
    slots = lax.dynamic_update_slice(lax.empty((8,) + gp.shape, F32), gp[None], (2 * chip + c, 0, 0))
    w_ssem, w_rsem, w_srcs, w_lands, w_token = _split_start(
        _win_and_small_copies, 10, [outs[0], gp], [lax.empty((3,) + outs[0].shape[1:], BF16), slots], gx,
        "scatter_win_start")
    shared = _split_wait(_sibling_copies, sh_ssem, sh_rsem, sh_first[:5], sh_first[5:], [w_token],
                         "rs_share_early_wait", only=(1, 2, 3, 4))
    adam_e = _adamw_shards(shared[1:5], shared[6:], big_w[1:], moments[0][1:], moments[1][1:], c_arr, "early")
    w_first = _split_wait(_win_and_small_copies, w_ssem, w_rsem, w_srcs, w_lands, [adam_e[0]], "scatter_win_wait",
                          only=(0, 1, 2))
    win_final, win_sib = _sum_chips_shared(outs[1], w_first[2], "win")
    adam_w = _adamw_shards([win_final], [win_sib], big_w[:1], moments[0][:1], moments[1][:1], c_arr, "win")
    big_g, big_d, big_m, big_v = [[adam_w[i]] + list(adam_e[4 * i:4 * i + 4]) for i in range(4)]

    gathered = _split_wait(_win_and_small_copies, w_ssem, w_rsem, w_first[:2], w_first[2:], [adam_w[0]],
                           "small_allgather_wait", only=tuple(range(3, 10)))[3]
    flat = _small_sum_adamw(gathered, wp, mp, vp)
    small_out = [flat[8 * kind:8 * kind + 8] for kind in range(4)]
    total_loss = flat[-1][0, 0]

    def ordered(small8, big4):
        sg1, sg2, sg3, sg4, swp, ssc, srb, ssk = small8
        swp, srb = swp[None], srb.T
        bwin, bwout, bwg, bwu, bwd = big4[0].T[None], big4[1][None], big4[2].T[None], big4[3].T[None], big4[4][None]
        return [sg1, bwin, swp, ssc, srb, ssk, bwout, sg2, sg3, bwg, bwu, bwd, sg4]

    res = [total_loss, gx[None]]
    for sm, bg in zip(small_out, (big_g, big_d, big_m, big_v)):
        res += ordered(sm, bg)
    return tuple(res)
```

```python
import functools

import numpy as np
import jax
import jax.numpy as jnp
from jax import lax
from jax.experimental import pallas as pl
from jax.experimental.pallas import tpu as pltpu

F32 = jnp.float32
BF16 = jnp.bfloat16

D = 1024
POOL_W = 512
GROUP = 128
WINDOWS = (2, 4, 8, 16)
HALO = 128
NQ = 8
BLK = 128
NBUCKET = 32
IN_W = 1280
DFF = 2816
NSH = 4
FS = DFF // NSH
WIN_S = IN_W // NSH
WOUT_S = D // NSH
EPS = 1e-6
NEG = -1e30
SCALE = 0.125

ADAM_LR = 0.001
ADAM_B1 = 0.9
ADAM_B2 = 0.999
ADAM_EPS = 1e-08
ADAM_WD = 0.01
ADAM_STEP = 10

TM = 512
TM_IN = 1024
TM_POOL = 512
TM_FFN = 512
TM_FFN_FWD = 1024
FFN_ROWS = 256
VMEM_LIMIT = 60 * 1024 * 1024

MESH_T = pl.DeviceIdType.MESH
ANY = pl.BlockSpec(memory_space=pl.ANY)


def _cp(n_grid=0, **kw):
    sem = ("arbitrary",) * n_grid if n_grid else None
    return pltpu.CompilerParams(dimension_semantics=sem, vmem_limit_bytes=VMEM_LIMIT, **kw)


def _dot(a, b):
    return jnp.dot(a, b, preferred_element_type=F32)


def _dot_nt(a, b):
    return lax.dot_general(a, b, (((1,), (1,)), ((), ())), preferred_element_type=F32)


def _dot_tn(a, b):
    return lax.dot_general(a, b, (((0,), (0,)), ((), ())), preferred_element_type=F32)


def _rms(x):
    r = lax.rsqrt(jnp.mean(x * x, axis=-1, keepdims=True) + EPS)
    return r, x * r


def _rms_bwd(r, n, g, dout):
    dn = dout * g
    dx = r * (dn - n * jnp.mean(dn * n, axis=-1, keepdims=True))
    dg = jnp.sum(dout * n, axis=0, keepdims=True)
    return dx, dg


def _split(x, n):
    parts = []
    r = x
    for _ in range(n):
        p = r.astype(BF16)
        parts.append(p)
        r = r - p.astype(F32)
    return parts


def _band_dot(a, x, n):
    acc = None
    for p in _split(x, n):
        t = _dot(a, p)
        acc = t if acc is None else acc + t
    return acc


def _bucket_table():
    qi = np.arange(BLK)[None, :]
    kj = np.arange(2 * BLK)[:, None]
    dist = qi + BLK - kj
    n = np.maximum(dist, 0)
    nf = np.maximum(n, 1).astype(np.float32)
    large = 16 + (np.log(nf / np.float32(16)) / np.float32(np.log(128 / 16)) * np.float32(16)).astype(np.int32)
    large = np.minimum(large, NBUCKET - 1)
    return np.where(n < 16, n, large).astype(np.int32)


def _prenorm(x, g1, dep=None):
    s = x.shape[0]
    tm = min(TM_IN, s)

    def body(x_ref, g_ref, *rest):
        _, n = _rms(x_ref[...])
        rest[-1][...] = (n * g_ref[...]).astype(BF16)

    row = pl.BlockSpec((tm, D), lambda i: (i, 0))
    return pl.pallas_call(
        body, name="prenorm", grid=(s // tm,),
        in_specs=[row, pl.BlockSpec((1, D), lambda i: (0, 0))] + ([] if dep is None else [ANY]), out_specs=row,
        out_shape=jax.ShapeDtypeStruct((s, D), BF16),
        compiler_params=_cp(1))(x, g1, *([] if dep is None else [dep]))


def _inproj_fwd(h1, w_in):
    s = h1.shape[0]
    tm = min(TM_IN, s)

    def body(h_ref, w_ref, u_ref, q_ref, k_ref, v_ref):
        proj = _dot_nt(h_ref[...], w_ref[...])
        u_ref[...] = proj[:, :512]
        q_ref[...] = proj[:, 512:1024].astype(BF16)
        k_ref[...] = proj[:, 1024:1152].astype(BF16)
        v_ref[...] = proj[:, 1152:1280].astype(BF16)

    row = lambda w: pl.BlockSpec((tm, w), lambda i: (i, 0))
    return pl.pallas_call(
        body, name="inproj_fwd", grid=(s // tm,),
        in_specs=[row(D), pl.BlockSpec((IN_W, D), lambda i: (0, 0))],
        out_specs=[row(512), row(512), row(128), row(128)],
        out_shape=[jax.ShapeDtypeStruct((s, 512), F32), jax.ShapeDtypeStruct((s, 512), BF16),
                   jax.ShapeDtypeStruct((s, 128), BF16), jax.ShapeDtypeStruct((s, 128), BF16)],
        compiler_params=_cp(1))(h1, w_in)


def _window_sums(ext, w, lag_sign, tm, terms):
    rows = lax.broadcasted_iota(jnp.int32, (HALO, 2 * HALO), 0)
    cols = lax.broadcasted_iota(jnp.int32, (HALO, 2 * HALO), 1)
    d = rows + HALO - cols if lag_sign > 0 else cols - rows
    band = jnp.where((d >= 0) & (d < w), 1.0, 0.0).astype(BF16)
    return jnp.concatenate([_band_dot(band, ext[r:r + 2 * HALO], terms) for r in range(0, tm, HALO)], axis=0)


def _pooled(ext, cur, t0, tm):
    t = t0 + lax.broadcasted_iota(jnp.int32, (tm, GROUP), 0)
    out = []
    for g, w in enumerate(WINDOWS):
        sl = slice(g * GROUP, (g + 1) * GROUP)
        cnt = jnp.minimum(t + 1, w).astype(F32)
        out.append(_window_sums(ext[:, sl], w, 1, tm, 2) / cnt - cur[:, sl])
    return out


def _pool_fwd(u, w_pool, pool_scale):
    s = u.shape[0]
    tm = min(TM_POOL, s)
    hb = tm // HALO

    def body(uc_ref, uh_ref, wp_ref, sc_ref, o_ref, pooled_ref):
        i = pl.program_id(0)
        cur = uc_ref[...]
        halo = jnp.where(i > 0, uh_ref[...], 0.0)
        ext = jnp.concatenate([halo, cur], axis=0)
        pooled = _pooled(ext, cur, i * tm, tm)
        for g in range(4):
            sl = slice(g * GROUP, (g + 1) * GROUP)
            pb = pooled[g].astype(BF16)
            pooled_ref[:, sl] = pb
            o_ref[:, sl] = (_dot(pb, wp_ref[g]) * sc_ref[:, sl]).astype(BF16)

    row = pl.BlockSpec((tm, 512), lambda i: (i, 0))
    return pl.pallas_call(
        body, name="pool_fwd", grid=(s // tm,),
        in_specs=[row, pl.BlockSpec((HALO, 512), lambda i: (jnp.maximum(i * hb - 1, 0), 0)),
                  pl.BlockSpec((4, GROUP, GROUP), lambda i: (0, 0, 0)),
                  pl.BlockSpec((1, 512), lambda i: (0, 0))],
        out_specs=[row, row],
        out_shape=[jax.ShapeDtypeStruct((s, 512), BF16)] * 2,
        compiler_params=_cp(1))(u, u, w_pool, pool_scale)


def _bias_table(bucket, rel_bias):
    def body(b_ref, rb_ref, o_ref):
        bucket_v = b_ref[...]
        key = lax.broadcasted_iota(jnp.int32, (2 * BLK, BLK), 0)
        dist = lax.broadcasted_iota(jnp.int32, (2 * BLK, BLK), 1) + BLK - key
        inwin = (dist >= 0) & (dist < BLK)
        for h in range(NQ):
            acc = jnp.zeros((2 * BLK, BLK), F32)
            for b in range(NBUCKET):
                acc = jnp.where(bucket_v == b, rb_ref[b, h], acc)
            rest = jnp.where(inwin, acc, NEG)
            o_ref[1, h] = rest
            o_ref[0, h] = jnp.where(key >= BLK, rest, NEG)

    return pl.pallas_call(
        body, name="bias_table",
        in_specs=[pl.BlockSpec(memory_space=pltpu.VMEM), pl.BlockSpec(memory_space=pltpu.SMEM)],
        out_specs=pl.BlockSpec(memory_space=pltpu.VMEM),
        out_shape=jax.ShapeDtypeStruct((2, NQ, 2 * BLK, BLK), F32),
        compiler_params=_cp())(bucket, rel_bias)


HD = 64


def _kv_band(ref, n, transposed):
    pstart = pl.multiple_of(jnp.maximum(n - 1, 0) * BLK, BLK)
    cstart = pl.multiple_of(n * BLK, BLK)
    lo = lax.broadcasted_iota(jnp.int32, (2 * BLK, 128), 1) < HD
    band = jnp.concatenate([ref[pl.ds(pstart, BLK), :], ref[pl.ds(cstart, BLK), :]], axis=0).astype(F32)
    rot = pltpu.roll(band, HD, axis=1)
    halves = ((jnp.where(lo, band, 0.0), jnp.where(lo, 0.0, rot)), (jnp.where(lo, rot, 0.0), jnp.where(lo, 0.0, band)))
    if transposed:
        out = [jnp.concatenate([a.T, b.T], axis=1).astype(BF16) for a, b in halves]
    else:
        out = [jnp.concatenate([a, b], axis=0).astype(BF16) for a, b in halves]
    return out, (lo, pstart, cstart)


def _pair_probs(qt, kk, bias, sk_ref, p):
    sink = jnp.concatenate([jnp.full((1, 1, BLK), sk_ref[0, 2 * p + e], F32) for e in range(2)], axis=0)
    s = _dot_nt(kk, qt).reshape(2, 2 * BLK, BLK) + bias
    m = jnp.maximum(jnp.max(s, axis=1, keepdims=True), sink)
    pr = jnp.exp(s - m)
    es = jnp.exp(sink - m)
    inv = 1.0 / (jnp.sum(pr, axis=1, keepdims=True) + es)
    return pr * inv, es * inv


def _attn_fwd(q, k, v, bias, sinks):
    s = q.shape[0]
    nblk = s // BLK

    def body(q_ref, k_ref, v_ref, b_ref, sk_ref, o_ref, prob_ref, ps_ref):
        n = pl.program_id(0)
        kk, _ = _kv_band(k_ref, n, False)
        vt, _ = _kv_band(v_ref, n, True)
        bidx = jnp.minimum(n, 1)
        for p in range(4):
            h = p // 2
            qt = (q_ref[:, p * 128:(p + 1) * 128].astype(F32) * SCALE).astype(BF16)
            prob, ps = _pair_probs(qt, kk[h], b_ref[bidx, pl.ds(2 * p, 2)], sk_ref, p)
            pb = prob.astype(BF16)
            prob_ref[pl.ds(2 * p, 2)] = pb
            ps_ref[pl.ds(2 * p, 2), :] = ps.reshape(2, BLK)
            acc_t = _dot(vt[h], pb.reshape(4 * BLK, BLK))
            o_ref[:, p * 128:(p + 1) * 128] = acc_t.T.astype(BF16)

    return pl.pallas_call(
        body, name="attn_fwd", grid=(nblk,),
        in_specs=[pl.BlockSpec((BLK, 512), lambda i: (i, 0)),
                  pl.BlockSpec((s, 128), lambda i: (0, 0)),
                  pl.BlockSpec((s, 128), lambda i: (0, 0)),
                  pl.BlockSpec((2, NQ, 2 * BLK, BLK), lambda i: (0, 0, 0, 0)),
                  pl.BlockSpec(memory_space=pltpu.SMEM)],
        out_specs=[pl.BlockSpec((BLK, 512), lambda i: (i, 0)),
                   pl.BlockSpec((None, NQ, 2 * BLK, BLK), lambda i: (i, 0, 0, 0)),
                   pl.BlockSpec((None, NQ, BLK), lambda i: (i, 0, 0))],
        out_shape=[jax.ShapeDtypeStruct((s, 512), BF16), jax.ShapeDtypeStruct((nblk, NQ, 2 * BLK, BLK), BF16),
                   jax.ShapeDtypeStruct((nblk, NQ, BLK), F32)],
        compiler_params=_cp(1))(q, k, v, bias, sinks)


def _outproj_fwd(pool_o, attn_o, w_out, x, g2, g3, between=None):
    s = x.shape[0]
    tm = min(TM, s)
    nt = s // tm

    def part(name, tile0, tiles, filled, dep):
        def body(p_ref, a_ref, w_ref, x_ref, g2_ref, g3_ref, *rest):
            mix_ref, x1_ref, h2_ref = rest[-3:]
            mix = _dot(p_ref[...], w_ref[0:512, :]) + _dot(a_ref[...], w_ref[512:1024, :])
            mix_ref[...] = mix
            _, n2 = _rms(mix)
            x1 = x_ref[...] + n2 * g2_ref[...]
            x1_ref[...] = x1
            _, n3 = _rms(x1)
            h2_ref[...] = (n3 * g3_ref[...]).astype(BF16)

        row = lambda w: pl.BlockSpec((tm, w), lambda i: (i + tile0, 0))
        vec = pl.BlockSpec((1, D), lambda i: (0, 0))
        extra = list(filled) + ([] if dep is None else [dep])
        return pl.pallas_call(
            body, name=name, grid=(tiles,),
            in_specs=[row(512), row(512), pl.BlockSpec((D, D), lambda i: (0, 0)), row(D), vec, vec]
            + [ANY] * len(extra),
            out_specs=[row(D), row(D), row(D)],
            out_shape=[jax.ShapeDtypeStruct((s, D), F32), jax.ShapeDtypeStruct((s, D), F32),
                       jax.ShapeDtypeStruct((s, D), BF16)],
            input_output_aliases={6 + t: t for t in range(len(filled))},
            compiler_params=_cp(1))(pool_o, attn_o, w_out, x, g2, g3, *extra)

    if between is None or nt < 2:
        return part("outproj_fwd", 0, nt, (), None)
    head = max(1, 5 * nt // 8)
    first = part("outproj_fwd_a", 0, head, (), None)
    return part("outproj_fwd_b", head, nt - head, first, between(first[2]))


def _silu_parts(g):
    sg = jax.nn.sigmoid(g)
    return sg, g * sg


def _ffn_fwd(h2, wg, wu, wd, x1, target, g4):
    s = h2.shape[0]
    tm = min(TM_FFN_FWD, s)

    def body(h_ref, wg_ref, wu_ref, wd_ref, x1_ref, t_ref, g4_ref,
             gate_ref, up_ref, a_ref, df_ref, dy_ref, loss_ref, gg4_ref, f_acc):
        i = pl.program_id(0)
        j = pl.program_id(1)
        first = j == 0
        chunks = [pl.ds(r, min(FFN_ROWS, tm)) for r in range(0, tm, FFN_ROWS)]
        project = lambda rows: (_dot_nt(h_ref[rows, :], wg_ref[...]), _dot_nt(h_ref[rows, :], wu_ref[...]))
        ahead = [project(chunks[0])]
        for k, rows in enumerate(chunks):
            if k + 1 < len(chunks):
                ahead.append(project(chunks[k + 1]))
            gate, up = ahead[k]
            gate_ref[rows, :] = gate.astype(BF16)
            up_ref[rows, :] = up.astype(BF16)
            _, sl = _silu_parts(gate)
            a = (sl * up).astype(BF16)
            a_ref[rows, :] = a
            contrib = _dot(a, wd_ref[...])
            f_acc[rows, :] = jnp.where(first, contrib, f_acc[rows, :] + contrib)

        @pl.when((i == 0) & (j == 0))
        def _():
            loss_ref[...] = jnp.zeros_like(loss_ref)
            gg4_ref[...] = jnp.zeros_like(gg4_ref)

        @pl.when(j == NSH - 1)
        def _():
            r4, n4 = _rms(f_acc[...])
            g4v = g4_ref[...]
            err = x1_ref[...] + n4 * g4v - t_ref[...]
            loss_ref[...] += (0.5 / D) * jnp.sum(err * err).reshape(1, 1)
            dy = err * (1.0 / D)
            dy_ref[...] = dy
            df, dg = _rms_bwd(r4, n4, g4v, dy)
            gg4_ref[...] += dg
            df_ref[...] = df.astype(BF16)

    row = lambda w: pl.BlockSpec((tm, w), lambda i, j: (i, 0))
    sh = lambda r, c: pl.BlockSpec((None, r, c), lambda i, j: (j, 0, 0))
    act = pl.BlockSpec((None, tm, FS), lambda i, j: (j, i, 0))
    vec = pl.BlockSpec((1, D), lambda i, j: (0, 0))
    return pl.pallas_call(
        body, name="ffn_fwd", grid=(s // tm, NSH),
        in_specs=[row(D), sh(FS, D), sh(FS, D), sh(FS, D), row(D), row(D), vec],
        out_specs=[act, act, act, row(D), row(D), pl.BlockSpec((1, 1), lambda i, j: (0, 0)), vec],
        out_shape=[jax.ShapeDtypeStruct((NSH, s, FS), BF16)] * 3
        + [jax.ShapeDtypeStruct((s, D), BF16), jax.ShapeDtypeStruct((s, D), F32),
           jax.ShapeDtypeStruct((1, 1), F32), jax.ShapeDtypeStruct((1, D), F32)],
        scratch_shapes=[pltpu.VMEM((tm, D), F32)],
        compiler_params=_cp(2))(h2, wg, wu, wd, x1, target, g4)


def _ffn_bwd_act(df, gate, up, wg, wu, wd, dy, x1, mix, g3, g2):
    s = df.shape[0]
    tm = min(TM_FFN, s)

    def body(df_ref, gate_ref, up_ref, wg_ref, wu_ref, wd_ref, dy_ref, x1_ref, mix_ref, g3_ref, g2_ref,
             dgate_ref, dup_ref, dx1_ref, dmix_ref, gg3_ref, gg2_ref, acc):
        j = pl.program_id(0)
        i = pl.program_id(1)
        first = j == 0
        tile = pl.multiple_of(i * tm, tm)
        chunks = [pl.ds(r, min(FFN_ROWS, tm)) for r in range(0, tm, FFN_ROWS)]

        @pl.when((i == 0) & (j == 0))
        def _():
            gg3_ref[...] = jnp.zeros_like(gg3_ref)
            gg2_ref[...] = jnp.zeros_like(gg2_ref)

        def norms_backward(rows, dh2):
            r3, n3 = _rms(x1_ref[rows, :])
            dx1n, dg3 = _rms_bwd(r3, n3, g3_ref[...], dh2)
            dx1 = dy_ref[rows, :] + dx1n
            dx1_ref[rows, :] = dx1
            gg3_ref[...] += dg3
            r2, n2 = _rms(mix_ref[rows, :])
            dmix, dg2 = _rms_bwd(r2, n2, g2_ref[...], dx1)
            gg2_ref[...] += dg2
            dmix_ref[rows, :] = dmix.astype(BF16)

        def shard_pass(last):
            das = [_dot_nt(df_ref[chunks[0], :], wd_ref[...])]
            for k, rows in enumerate(chunks):
                if k + 1 < len(chunks):
                    das.append(_dot_nt(df_ref[chunks[k + 1], :], wd_ref[...]))
                da = das[k]
                g = gate_ref[rows, :].astype(F32)
                u = up_ref[rows, :].astype(F32)
                sg, sl = _silu_parts(g)
                dgate = (da * u * (sg * (1.0 + g * (1.0 - sg)))).astype(BF16)
                dup = (da * sl).astype(BF16)
                dgate_ref[rows, :] = dgate
                dup_ref[rows, :] = dup
                contrib = _dot(dgate, wg_ref[...]) + _dot(dup, wu_ref[...])
                acc_rows = pl.ds(tile + k * FFN_ROWS, rows.size)
                if last:
                    norms_backward(rows, acc[acc_rows, :] + contrib)
                else:
                    acc[acc_rows, :] = jnp.where(first, contrib, acc[acc_rows, :] + contrib)

        pl.when(j < NSH - 1)(functools.partial(shard_pass, False))
        pl.when(j == NSH - 1)(functools.partial(shard_pass, True))

    row = pl.BlockSpec((tm, D), lambda j, i: (i, 0))
    last = pl.BlockSpec((tm, D), lambda j, i: (i * (j // (NSH - 1)), 0))
    sh = pl.BlockSpec((None, FS, D), lambda j, i: (j, 0, 0))
    act = pl.BlockSpec((None, tm, FS), lambda j, i: (j, i, 0))
    vec = pl.BlockSpec((1, D), lambda j, i: (0, 0))
    return pl.pallas_call(
        body, name="ffn_bwd_act", grid=(NSH, s // tm),
        in_specs=[row, act, act, sh, sh, sh, last, last, last, vec, vec],
        out_specs=[act, act, last, last, vec, vec],
        out_shape=[jax.ShapeDtypeStruct((NSH, s, FS), BF16), jax.ShapeDtypeStruct((NSH, s, FS), BF16),
                   jax.ShapeDtypeStruct((s, D), F32), jax.ShapeDtypeStruct((s, D), BF16),
                   jax.ShapeDtypeStruct((1, D), F32), jax.ShapeDtypeStruct((1, D), F32)],
        scratch_shapes=[pltpu.VMEM((s, D), F32)],
        compiler_params=_cp(2))(df, gate, up, wg, wu, wd, dy, x1, mix, g3, g2)


SMEM_SPEC = pl.BlockSpec(memory_space=pltpu.SMEM)


def _emit_halves(acc_ref, row0, rows, c, own_ref, oth_ref):
    half = rows // 2
    own_ref[...] = acc_ref[pl.ds(row0 + pl.multiple_of(c * half, 8), half), :]
    oth_ref[...] = acc_ref[pl.ds(row0 + pl.multiple_of((1 - c) * half, 8), half), :].astype(BF16)


def _half_shapes(rows):
    return [jax.ShapeDtypeStruct((NSH, rows // 2, D), F32), jax.ShapeDtypeStruct((NSH, rows // 2, D), BF16)]


def _ffn_bwd_w(h2, act_a, dgate, dup, df, c_arr):
    s = h2.shape[0]
    tm = min(TM_FFN_FWD, s)
    nt = s // tm

    def body(c_ref, h_ref, a_ref, dgate_ref, dup_ref, df_ref, *rest):
        outs, accs = rest[:6], rest[6:]
        i = pl.program_id(1)

        @pl.when(i == 0)
        def _():
            for acc in accs:
                acc[...] = jnp.zeros_like(acc)

        h = h_ref[...]
        accs[0][...] += _dot_tn(dgate_ref[...], h)
        accs[1][...] += _dot_tn(dup_ref[...], h)
        accs[2][...] += _dot_tn(a_ref[...], df_ref[...])

        @pl.when(i == nt - 1)
        def _():
            for t, acc in enumerate(accs):
                _emit_halves(acc, 0, FS, c_ref[0], outs[2 * t], outs[2 * t + 1])

    row = lambda w: pl.BlockSpec((tm, w), lambda j, i: (i, 0))
    act = pl.BlockSpec((None, tm, FS), lambda j, i: (j, i, 0))
    half = pl.BlockSpec((None, FS // 2, D), lambda j, i: (j, 0, 0))
    return pl.pallas_call(
        body, name="ffn_bwd_w", grid=(NSH, nt),
        in_specs=[SMEM_SPEC, row(D), act, act, act, row(D)],
        out_specs=[half] * 6,
        out_shape=_half_shapes(FS) * 3,
        scratch_shapes=[pltpu.VMEM((FS, D), F32)] * 3,
        compiler_params=_cp(2))(c_arr, h2, act_a, dgate, dup, df)


def _outproj_bwd(dmix, w_out, pool_o, attn_o, c_arr, dep=None):
    s = dmix.shape[0]
    tm = min(TM, s)
    nt = s // tm

    def body(c_ref, dm_ref, w_ref, p_ref, a_ref, *rest):
        dpool_ref, dattn_ref, own_ref, oth_ref, gw_ref = rest[-5:]
        i = pl.program_id(0)

        @pl.when(i == 0)
        def _():
            gw_ref[...] = jnp.zeros_like(gw_ref)

        dm = dm_ref[...]
        dpool_ref[...] = _dot_nt(dm, w_ref[0:512, :])
        dattn_ref[...] = _dot_nt(dm, w_ref[512:1024, :]).astype(BF16)
        gw_ref[0:512, :] += _dot_tn(p_ref[...], dm)
        gw_ref[512:1024, :] += _dot_tn(a_ref[...], dm)

        @pl.when(i == nt - 1)
        def _():
            for k in range(NSH):
                _emit_halves(gw_ref, k * WOUT_S, WOUT_S, c_ref[0], own_ref.at[k], oth_ref.at[k])

    row = lambda w: pl.BlockSpec((tm, w), lambda i: (i, 0))
    full = pl.BlockSpec((D, D), lambda i: (0, 0))
    half = pl.BlockSpec((NSH, WOUT_S // 2, D), lambda i: (0, 0, 0))
    return pl.pallas_call(
        body, name="outproj_bwd", grid=(nt,),
        in_specs=[SMEM_SPEC, row(D), full, row(512), row(512)] + ([] if dep is None else [ANY]),
        out_specs=[row(512), row(512), half, half],
        out_shape=[jax.ShapeDtypeStruct((s, 512), F32), jax.ShapeDtypeStruct((s, 512), BF16)] + _half_shapes(WOUT_S),
        scratch_shapes=[pltpu.VMEM((D, D), F32)],
        compiler_params=_cp(1))(c_arr, dmix, w_out, pool_o, attn_o, *([] if dep is None else [dep]))


def _pool_bwd(pooled, dpool, w_pool, pool_scale, dep=None):
    s = pooled.shape[0]
    tm = min(TM_POOL, s)
    hb = tm // HALO
    nt = s // tm
    last_halo = s // HALO - 1

    def body(p_ref, dc_ref, dn_ref, wp_ref, sc_ref, *rest):
        du_ref, gwp_ref, gsc_ref = rest[-3:]
        i = pl.program_id(0)

        @pl.when(i == 0)
        def _():
            gwp_ref[...] = jnp.zeros_like(gwp_ref)
            gsc_ref[...] = jnp.zeros_like(gsc_ref)

        dcur = dc_ref[...]
        dnext = jnp.where(i < nt - 1, dn_ref[...], 0.0)
        dext = jnp.concatenate([dcur, dnext], axis=0)
        t_ext = i * tm + lax.broadcasted_iota(jnp.int32, (tm + HALO, GROUP), 0)
        for g, w in enumerate(WINDOWS):
            sl = slice(g * GROUP, (g + 1) * GROUP)
            pb = p_ref[:, sl]
            wp = wp_ref[g]
            mixed = _dot(pb, wp)
            gsc_ref[:, sl] += jnp.sum(dcur[:, sl] * mixed, axis=0, keepdims=True)
            dmixed = (dext[:, sl] * sc_ref[:, sl]).astype(BF16)
            gwp_ref[g] += _dot_tn(pb, dmixed[0:tm])
            dpooled = _dot_nt(dmixed, wp)
            z = dpooled / jnp.minimum(t_ext + 1, w).astype(F32)
            du_ref[:, sl] = (_window_sums(z, w, -1, tm, 2) - dpooled[0:tm]).astype(BF16)

    return pl.pallas_call(
        body, name="pool_bwd", grid=(nt,),
        in_specs=[pl.BlockSpec((tm, 512), lambda i: (i, 0)),
                  pl.BlockSpec((tm, 512), lambda i: (i, 0)),
                  pl.BlockSpec((HALO, 512), lambda i: (jnp.minimum((i + 1) * hb, last_halo), 0)),
                  pl.BlockSpec((4, GROUP, GROUP), lambda i: (0, 0, 0)),
                  pl.BlockSpec((1, 512), lambda i: (0, 0))] + ([] if dep is None else [ANY]),
        out_specs=[pl.BlockSpec((tm, 512), lambda i: (i, 0)),
                   pl.BlockSpec((4, GROUP, GROUP), lambda i: (0, 0, 0)),
                   pl.BlockSpec((1, 512), lambda i: (0, 0))],
        out_shape=[jax.ShapeDtypeStruct((s, 512), BF16), jax.ShapeDtypeStruct((4, GROUP, GROUP), F32),
                   jax.ShapeDtypeStruct((1, 512), F32)],
        compiler_params=_cp(1))(pooled, dpool, dpool, w_pool, pool_scale, *([] if dep is None else [dep]))


def _attn_bwd(q, k, v, do, probs, sink_share, bucket, dep=None):
    s = q.shape[0]
    nblk = s // BLK

    def body(q_ref, do_ref, k_ref, v_ref, prob_ref, ps_ref, bk_ref, *rest):
        dq_ref, dk_ref, dv_ref, grb_ref, gsk_ref, dss_ref = rest[-6:]
        n = pl.program_id(0)

        @pl.when(n == 0)
        def _():
            dk_ref[...] = jnp.zeros_like(dk_ref)
            dv_ref[...] = jnp.zeros_like(dv_ref)
            dss_ref[...] = jnp.zeros_like(dss_ref)
            gsk_ref[...] = jnp.zeros_like(gsk_ref)

        kt, (lo, pstart, cstart) = _kv_band(k_ref, n, True)
        vv, _ = _kv_band(v_ref, n, False)
        lo_q = lax.broadcasted_iota(jnp.int32, (BLK, 128), 1) < HD
        dkk = [jnp.zeros((2 * BLK, 128), F32), jnp.zeros((2 * BLK, 128), F32)]
        dvv = [jnp.zeros((2 * BLK, 128), F32), jnp.zeros((2 * BLK, 128), F32)]

        def by_head(t):
            return jnp.concatenate([jnp.where(lo_q, t, 0.0), jnp.where(lo_q, 0.0, t)], axis=0).astype(BF16)

        for p in range(4):
            h = p // 2
            qt = q_ref[:, p * 128:(p + 1) * 128].astype(F32) * SCALE
            dot = do_ref[:, p * 128:(p + 1) * 128]
            pb = prob_ref[pl.ds(2 * p, 2)]
            prob = pb.astype(F32)
            ps = ps_ref[pl.ds(2 * p, 2), :].reshape(2, 1, BLK)
            dp = _dot_nt(vv[h], dot).reshape(2, 2 * BLK, BLK)
            delta = jnp.sum(prob * dp, axis=1, keepdims=True)
            ds = prob * (dp - delta)
            sink_part = ps * delta
            for e in range(2):
                gs = -jnp.sum(sink_part[e]).reshape(1, 1)
                gsk_ref[pl.ds(2 * p + e, 1), :] += jnp.broadcast_to(gs, (1, 128))
            dss_ref[pl.ds(2 * p, 2)] += ds
            dsb = ds.astype(BF16)
            dq_t = _dot(kt[h], dsb.reshape(4 * BLK, BLK))
            dq_ref[:, p * 128:(p + 1) * 128] = (dq_t.T * SCALE).astype(BF16)
            dkk[h] = dkk[h] + _dot(jnp.concatenate([dsb[0], dsb[1]], axis=1), by_head(qt))
            dvv[h] = dvv[h] + _dot(jnp.concatenate([pb[0], pb[1]], axis=1), by_head(dot.astype(F32)))
        for acc, ref in ((dkk, dk_ref), (dvv, dv_ref)):
            f0 = acc[0] + pltpu.roll(acc[0], HD, axis=1)
            f1 = acc[1] + pltpu.roll(acc[1], HD, axis=1)
            band = jnp.where(lo, f0, f1)
            ref[pl.ds(pstart, BLK), :] += band[0:BLK]
            ref[pl.ds(cstart, BLK), :] += band[BLK:2 * BLK]

        @pl.when(n == nblk - 1)
        def _():
            _relbias_grad(dss_ref, bk_ref[...], grb_ref)

    blk = pl.BlockSpec((BLK, 512), lambda i: (i, 0))
    kv = pl.BlockSpec((s, 128), lambda i: (0, 0))
    row8 = pl.BlockSpec((NQ, 128), lambda i: (0, 0))
    return pl.pallas_call(
        body, name="attn_bwd", grid=(nblk,),
        in_specs=[blk, blk, kv, kv, pl.BlockSpec((None, NQ, 2 * BLK, BLK), lambda i: (i, 0, 0, 0)),
                  pl.BlockSpec((None, NQ, BLK), lambda i: (i, 0, 0)), pl.BlockSpec((2 * BLK, BLK), lambda i: (0, 0))]
        + ([] if dep is None else [ANY]),
        out_specs=[blk, kv, kv, row8, row8],
        out_shape=[jax.ShapeDtypeStruct((s, 512), BF16), jax.ShapeDtypeStruct((s, 128), F32),
                   jax.ShapeDtypeStruct((s, 128), F32), jax.ShapeDtypeStruct((NQ, 128), F32),
                   jax.ShapeDtypeStruct((NQ, 128), F32)],
        scratch_shapes=[pltpu.VMEM((NQ, 2 * BLK, BLK), F32)],
        compiler_params=_cp(1))(q, do, k, v, probs, sink_share, bucket, *([] if dep is None else [dep]))


def _relbias_grad(ds_ref, bucket_v, o_ref):
    lane = lax.broadcasted_iota(jnp.int32, (NQ, 128), 1)
    head_row = lax.broadcasted_iota(jnp.int32, (NQ, BLK), 0)
    acc = jnp.zeros((NQ, 128), F32)
    for b in range(NBUCKET):
        sel = bucket_v == b
        stack = jnp.zeros((NQ, BLK), F32)
        for h in range(NQ):
            col_sums = jnp.sum(jnp.where(sel, ds_ref[h], 0.0), axis=0, keepdims=True)
            stack = jnp.where(head_row == h, col_sums, stack)
        acc = acc + jnp.where(lane == b, jnp.sum(stack, axis=1, keepdims=True), 0.0)
    o_ref[...] = acc


def _inproj_bwd(x, g1, du, dq, dk, dv, w_in, dx1, c_arr):
    s = x.shape[0]
    tm = min(TM, s)
    nt = s // tm
    cols = ((0, 512), (512, 1024), (1024, 1152), (1152, 1280))

    def body(c_ref, x_ref, g_ref, du_ref, dq_ref, dk_ref, dv_ref, w_ref, dx1_ref,
             gx_ref, own_ref, oth_ref, gg_ref, gw_ref):
        i = pl.program_id(0)

        @pl.when(i == 0)
        def _():
            gw_ref[...] = jnp.zeros_like(gw_ref)
            gg_ref[...] = jnp.zeros_like(gg_ref)

        parts = (du_ref[...], dq_ref[...], dk_ref[...].astype(BF16), dv_ref[...].astype(BF16))
        dh = None
        for (a, b), dpart in zip(cols, parts):
            t = _dot(dpart, w_ref[a:b, :])
            dh = t if dh is None else dh + t
        gv = g_ref[...]
        r1, n1 = _rms(x_ref[...])
        h = (n1 * gv).astype(BF16)
        for (a, b), dpart in zip(cols, parts):
            gw_ref[a:b, :] += _dot_tn(dpart, h)
        dx, dg = _rms_bwd(r1, n1, gv, dh)
        gg_ref[...] += dg
        gx_ref[...] = dx1_ref[...] + dx

        @pl.when(i == nt - 1)
        def _():
            for k in range(NSH):
                _emit_halves(gw_ref, k * WIN_S, WIN_S, c_ref[0], own_ref.at[k], oth_ref.at[k])

    row = lambda w: pl.BlockSpec((tm, w), lambda i: (i, 0))
    vec = pl.BlockSpec((1, D), lambda i: (0, 0))
    full = pl.BlockSpec((IN_W, D), lambda i: (0, 0))
    half = pl.BlockSpec((NSH, WIN_S // 2, D), lambda i: (0, 0, 0))
    return pl.pallas_call(
        body, name="inproj_bwd", grid=(nt,),
        in_specs=[SMEM_SPEC, row(D), vec, row(512), row(512), row(128), row(128), full, row(D)],
        out_specs=[row(D), half, half, vec],
        out_shape=[jax.ShapeDtypeStruct((s, D), F32)] + _half_shapes(WIN_S) + [jax.ShapeDtypeStruct((1, D), F32)],
        scratch_shapes=[pltpu.VMEM((IN_W, D), F32)],
        compiler_params=_cp(1))(c_arr, x, g1, du, dq, dk, dv, w_in, dx1)


def _local_step(x, target, small, w_in, w_out, ffn_weights, c_arr, on_ffn_grads=None, on_wout_grads=None,
                mid_outproj=None, start_dep=None):
    g1, g2, g3, g4, w_pool, pool_scale, rel_bias, sinks = small
    bucket = jnp.asarray(_bucket_table())
    wp_bf = w_pool.astype(BF16)
    bias = _bias_table(bucket, rel_bias)
    h1 = _prenorm(x, g1, start_dep)
    if callable(w_in):
        w_in = w_in(bias, h1)
    u, q, k, v = _inproj_fwd(h1, w_in)
    pool_o, pooled = _pool_fwd(u, wp_bf, pool_scale)
    attn_o, probs, sink_share = _attn_fwd(q, k, v, bias, sinks)
    g2_fwd = g2
    if callable(w_out):
        w_out, after = w_out(pool_o, attn_o)
        if after is not None:
            g2_fwd = g2 + after[:1, :1]
    mix, x1, h2 = _outproj_fwd(pool_o, attn_o, w_out, x, g2_fwd, g3, mid_outproj)
    wg, wu, wd = ffn_weights(h2) if callable(ffn_weights) else ffn_weights
    gate, up, act_a, df, dy, loss, gg4 = _ffn_fwd(h2, wg, wu, wd, x1, target, g4)
    dgate, dup, dx1, dmix, gg3, gg2 = _ffn_bwd_act(df, gate, up, wg, wu, wd, dy, x1, mix, g3, g2)
    ffn_g = _ffn_bwd_w(h2, act_a, dgate, dup, df, c_arr)
    dep = None if on_ffn_grads is None else on_ffn_grads(ffn_g)
    dpool, dattn, wout_own, wout_oth = _outproj_bwd(dmix, w_out, pool_o, attn_o, c_arr, dep)
    dep = None if on_wout_grads is None else on_wout_grads(wout_own, wout_oth, dpool)
    du, gwp, gsc = _pool_bwd(pooled, dpool, wp_bf, pool_scale, dep)
    dq, dk, dv, grb, gsk = _attn_bwd(q, k, v, dattn, probs, sink_share, bucket, dep)
    gx, win_own, win_oth, gg1 = _inproj_bwd(x, g1, du, dq, dk, dv, w_in, dx1, c_arr)
    small_grads = (gg1, gg2, gg3, gg4, gwp, gsc, grb, gsk[:, 0].reshape(1, NQ))
    return loss, gx, small_grads, ((win_own, win_oth), (wout_own, wout_oth), ffn_g)


def _place():
    x, y, c = lax.axis_index("x"), lax.axis_index("y"), lax.axis_index("c")
    chips = ((1 - x, y), (x, 1 - y), (1 - x, 1 - y))
    return x, y, c, chips


def _remote(src, dst, ssem, rsem, dev):
    return pltpu.make_async_remote_copy(src_ref=src, dst_ref=dst, send_sem=ssem, recv_sem=rsem,
                                        device_id=dev, device_id_type=MESH_T)


def _to_sibling(arrs, name, after=()):
    nt, na = len(arrs), len(after)

    def body(*refs):
        srcs, dsts = refs[:nt], refs[nt + na:2 * nt + na]
        ssem, rsem = refs[2 * nt + na:]
        x, y, c, _ = _place()
        cps = [_remote(srcs[t], dsts[t], ssem.at[t], rsem.at[t], (x, y, 1 - c)) for t in range(nt)]
        for cp in cps:
            cp.start()
        for cp in cps:
            cp.wait()

    return pl.pallas_call(
        body, name=name, in_specs=[ANY] * (nt + na), out_specs=[ANY] * nt,
        out_shape=[jax.ShapeDtypeStruct(a.shape, a.dtype) for a in arrs],
        scratch_shapes=[pltpu.SemaphoreType.DMA((nt,)), pltpu.SemaphoreType.DMA((nt,))],
        compiler_params=_cp())(*arrs, *after)


HBM_SPEC = pl.BlockSpec(memory_space=pltpu.HBM)
SEM_SPEC = pl.BlockSpec(memory_space=pltpu.SEMAPHORE)
DATAFLOW = pltpu.SideEffectType.DATAFLOW_SIDE_EFFECTING


def _cast_into_slots(ws, chip_arr):
    nt, tiles = len(ws), 4

    def body(chip_ref, *refs):
        for t in range(nt):
            refs[nt + t][...] = refs[t][...].astype(BF16)

    return pl.pallas_call(
        body, name="cast_weights",
        grid_spec=pltpu.PrefetchScalarGridSpec(
            num_scalar_prefetch=1, grid=(tiles,),
            in_specs=[pl.BlockSpec((w.shape[0] // tiles, w.shape[1]), lambda i, chip_ref: (i, 0)) for w in ws],
            out_specs=[pl.BlockSpec((None, w.shape[0] // tiles, w.shape[1]), lambda i, chip_ref: (chip_ref[0], i, 0))
                       for w in ws]),
        out_shape=[jax.ShapeDtypeStruct((NSH,) + w.shape, BF16) for w in ws],
        compiler_params=_cp(1))(chip_arr, *ws)


def _gather_copies(srcs, lands, ssem, rsem, first=0):
    x, y, c, chips = _place()
    me = 2 * x + y
    out = []
    for t in range(len(lands)):
        half = lands[t].shape[1] // 2
        mine = pl.ds(pl.multiple_of(c * half, 16), half)
        for j, (px, py) in enumerate(chips):
            k = 3 * (first + t) + j
            send = _remote(lands[t].at[me, mine], lands[t].at[me, mine], ssem.at[k], rsem.at[k], (px, py, c))
            blk = lands[t].at[2 * px + py, mine]
            out.append((send, _remote(blk, blk, ssem.at[k], rsem.at[k], (px, py, c))))
    return out


def _scatter_copies(srcs, lands, ssem, rsem):
    x, y, c, chips = _place()
    out = []
    for t in range(len(srcs)):
        for j, (px, py) in enumerate(chips):
            k = 3 * t + j
            send = _remote(srcs[t].at[2 * px + py], lands[t].at[j], ssem.at[k], rsem.at[k], (px, py, c))
            out.append((send, _remote(lands[t].at[j], lands[t].at[j], ssem.at[k], rsem.at[k], (px, py, c))))
    return out


def _sibling_copies(srcs, lands, ssem, rsem):
    x, y, c, _ = _place()
    sib = (x, y, 1 - c)
    return [(_remote(srcs[t], lands[t], ssem.at[t], rsem.at[t], sib),
             _remote(lands[t], lands[t], ssem.at[t], rsem.at[t], sib)) for t in range(len(srcs))]


def _peer_copies(srcs, lands, ssem, rsem):
    x, y, c, _ = _place()
    me = 4 * x + 2 * y + c
    out = []
    for kk in range(1, 8):
        px, py, pc = x ^ (kk >> 2), y ^ ((kk >> 1) & 1), c ^ (kk & 1)
        send = _remote(srcs[0], lands[0].at[me], ssem.at[kk - 1], rsem.at[kk - 1], (px, py, pc))
        slot = lands[0].at[4 * px + 2 * py + pc]
        out.append((send, _remote(slot, slot, ssem.at[kk - 1], rsem.at[kk - 1], (px, py, pc))))
    return out


def _forward_copies(srcs, lands, ssem, rsem):
    x, y, c, chips = _place()
    sib = (x, y, 1 - c)
    out = []
    for t in range(len(lands)):
        half = lands[t].shape[1] // 2
        mine = pl.ds(pl.multiple_of(c * half, 16), half)
        other = pl.ds(pl.multiple_of((1 - c) * half, 16), half)
        for j, (px, py) in enumerate(chips):
            k = 3 * t + j
            blk_m, blk_o = lands[t].at[2 * px + py, mine], lands[t].at[2 * px + py, other]
            out.append((_remote(blk_m, blk_m, ssem.at[k], rsem.at[k], sib),
                        _remote(blk_o, blk_o, ssem.at[k], rsem.at[k], sib)))
    return out


def _win_and_small_copies(srcs, lands, ssem, rsem):
    return (_scatter_copies(srcs[:1], lands[:1], ssem, rsem)
            + _peer_copies(srcs[1:], lands[1:], ssem.at[pl.ds(3, 7)], rsem.at[pl.ds(3, 7)]))


def _split_start(plan, ncopy, srcs, lands, after, name):
    ns, nbuf = len(srcs), len(srcs) + len(lands)
    extra = [] if after is None else [after]
    n_in = nbuf + len(extra)

    def body(*refs):
        ssem, rsem, token = refs[n_in], refs[n_in + 1], refs[-1]
        for send, _ in plan(refs[:ns], refs[ns:nbuf], ssem, rsem):
            send.start()
        token[...] = jnp.zeros_like(token)

    bufs = [pltpu.with_memory_space_constraint(a, pltpu.HBM) for a in list(srcs) + list(lands)]
    outs = pl.pallas_call(
        body, name=name, in_specs=[HBM_SPEC] * nbuf + [ANY] * len(extra),
        out_specs=[SEM_SPEC, SEM_SPEC] + [HBM_SPEC] * nbuf + [pl.BlockSpec(memory_space=pltpu.VMEM)],
        out_shape=[pltpu.SemaphoreType.DMA((ncopy,)), pltpu.SemaphoreType.DMA((ncopy,))]
        + [pltpu.HBM(a.shape, a.dtype) for a in bufs] + [jax.ShapeDtypeStruct((8, 128), F32)],
        input_output_aliases={i: 2 + i for i in range(nbuf)},
        compiler_params=pltpu.CompilerParams(has_side_effects=DATAFLOW))(*bufs, *extra)
    return outs[0], outs[1], outs[2:2 + ns], outs[2 + ns:2 + nbuf], outs[-1]


def _split_wait(plan, ssem, rsem, srcs, lands, after, name, only=None):
    ns, nbuf = len(srcs), len(srcs) + len(lands)

    def body(*refs):
        s_ref, r_ref = refs[nbuf], refs[nbuf + 1]
        for k, (send, recv) in enumerate(plan(refs[:ns], refs[ns:nbuf], s_ref, r_ref)):
            if only is None or k in only:
                send.wait_send()
                recv.wait_recv()

    outs = pl.pallas_call(
        body, name=name, in_specs=[HBM_SPEC] * nbuf + [SEM_SPEC, SEM_SPEC] + [ANY] * len(after),
        out_specs=[HBM_SPEC] * nbuf,
        out_shape=[pltpu.HBM(a.shape, a.dtype) for a in list(srcs) + list(lands)],
        input_output_aliases={i: i for i in range(nbuf)},
        compiler_params=pltpu.CompilerParams(has_side_effects=DATAFLOW))(*srcs, *lands, ssem, rsem, *after)
    return outs


def _gather_finish(lands, tag):
    nt = len(lands)

    def body(*refs):
        outs = refs[nt:2 * nt]
        dsend, drecv = refs[2 * nt:]
        x, y, c, chips = _place()
        sib = (x, y, 1 - c)
        sends = []
        for t in range(nt):
            half = outs[t].shape[1] // 2
            mine = pl.ds(pl.multiple_of(c * half, 16), half)
            for j, (px, py) in enumerate(chips):
                blk = outs[t].at[2 * px + py, mine]
                cp = _remote(blk, blk, dsend.at[t, j], drecv.at[t, j], sib)
                cp.start()
                sends.append(cp)
        for t in range(nt):
            half = outs[t].shape[1] // 2
            other = pl.ds(pl.multiple_of((1 - c) * half, 16), half)
            for j, (px, py) in enumerate(chips):
                blk = outs[t].at[2 * px + py, other]
                _remote(blk, blk, dsend.at[t, j], drecv.at[t, j], sib).wait_recv()
        for cp in sends:
            cp.wait_send()

    return pl.pallas_call(
        body, name="gather_finish_" + tag, in_specs=[ANY] * nt, out_specs=[ANY] * nt,
        out_shape=[jax.ShapeDtypeStruct(a.shape, a.dtype) for a in lands],
        input_output_aliases={t: t for t in range(nt)},
        scratch_shapes=[pltpu.SemaphoreType.DMA((nt, 3)), pltpu.SemaphoreType.DMA((nt, 3))],
        compiler_params=_cp())(*lands)


def _chip_partial(owns, recv, chip_arr, tag):
    nt = len(owns)

    def body(chip_ref, *refs):
        k = pl.program_id(0)
        for t in range(nt):
            p = refs[t][...] + refs[nt + t][...].astype(F32)
            refs[2 * nt + t][...] = p.astype(BF16)

            @pl.when(k == chip_ref[0])
            def _():
                refs[3 * nt + t][...] = p

    blk_specs = [pl.BlockSpec((None,) + a.shape[1:], lambda k, chip_ref: (k, 0, 0)) for a in owns]
    own_specs = [pl.BlockSpec(a.shape[1:], lambda k, chip_ref: (0, 0)) for a in owns]
    return pl.pallas_call(
        body, name="rs_chip_partial_" + tag,
        grid_spec=pltpu.PrefetchScalarGridSpec(num_scalar_prefetch=1, grid=(NSH,), in_specs=blk_specs * 2,
                                               out_specs=blk_specs + own_specs),
        out_shape=[jax.ShapeDtypeStruct(a.shape, BF16) for a in owns]
        + [jax.ShapeDtypeStruct(a.shape[1:], F32) for a in owns],
        compiler_params=_cp(1))(chip_arr, *owns, *recv)


def _sum_chips(own, recv, tag):
    nt = len(own)

    def body(*refs):
        outs = refs[2 * nt:]
        for t in range(nt):
            r = refs[nt + t]
            outs[t][...] = ((refs[t][...] + r[0].astype(F32)) + r[1].astype(F32)) + r[2].astype(F32)

    vm = pl.BlockSpec(memory_space=pltpu.VMEM)
    return pl.pallas_call(
        body, name="rs_sum_chips_" + tag, in_specs=[vm] * (2 * nt), out_specs=[vm] * nt,
        out_shape=[jax.ShapeDtypeStruct(a.shape, F32) for a in own],
        compiler_params=_cp())(*own, *recv)


def _sum_chips_shared(own, recv, tag):
    def body(own_ref, recv_ref, out_ref, sib_ref, ssem, rsem):
        out_ref[...] = ((own_ref[...] + recv_ref[0].astype(F32)) + recv_ref[1].astype(F32)) + recv_ref[2].astype(F32)
        x, y, c, _ = _place()
        cp = _remote(out_ref, sib_ref, ssem.at[0], rsem.at[0], (x, y, 1 - c))
        cp.start()
        cp.wait()

    vm = pl.BlockSpec(memory_space=pltpu.VMEM)
    return pl.pallas_call(
        body, name="rs_sum_share_" + tag, in_specs=[vm, vm], out_specs=[vm, ANY],
        out_shape=[jax.ShapeDtypeStruct(own.shape, F32)] * 2,
        scratch_shapes=[pltpu.SemaphoreType.DMA((1,)), pltpu.SemaphoreType.DMA((1,))],
        compiler_params=_cp())(own, recv)


def _adamw_math(w, g, m, v):
    m = ADAM_B1 * m + (1.0 - ADAM_B1) * g
    v = ADAM_B2 * v + (1.0 - ADAM_B2) * (g * g)
    m_hat = m / (1.0 - ADAM_B1 ** ADAM_STEP)
    v_hat = v / (1.0 - ADAM_B2 ** ADAM_STEP)
    delta = -ADAM_LR * (m_hat / (jnp.sqrt(v_hat) + ADAM_EPS) + ADAM_WD * w)
    return delta, m, v


ADAM_SPLIT = 4


def _adamw_shards(own, sib, ws, ms, vs, c_arr, tag):
    nt = len(own)

    def body(c_ref, *refs):
        hh = pl.program_id(0)
        mine = hh == c_ref[0]
        for t in range(nt):
            g = jnp.where(mine, refs[t][...], refs[nt + t][...])
            delta, m, v = _adamw_math(refs[2 * nt + t][...], g, refs[3 * nt + t][...], refs[4 * nt + t][...])
            refs[5 * nt + t][...] = g
            refs[6 * nt + t][...] = delta
            refs[7 * nt + t][...] = m
            refs[8 * nt + t][...] = v

    def tile(a):
        return (a.shape[0] // ADAM_SPLIT, a.shape[1])

    def half_index(used_when_mine):
        def index(hh, q, c_ref):
            mine = 1 - (hh - c_ref[0]) * (hh - c_ref[0])
            used = mine if used_when_mine else 1 - mine
            return (used * q + (1 - used) * hh * (ADAM_SPLIT - 1), 0)
        return index

    own_specs = [pl.BlockSpec(tile(a), half_index(True)) for a in own]
    sib_specs = [pl.BlockSpec(tile(a), half_index(False)) for a in own]
    full_specs = [pl.BlockSpec(tile(a), lambda hh, q, c_ref: (hh * ADAM_SPLIT + q, 0)) for a in own]
    return pl.pallas_call(
        body, name="adamw_shards_" + tag,
        grid_spec=pltpu.PrefetchScalarGridSpec(num_scalar_prefetch=1, grid=(2, ADAM_SPLIT),
                                               in_specs=own_specs + sib_specs + full_specs * 3,
                                               out_specs=full_specs * 4),
        out_shape=[jax.ShapeDtypeStruct(w.shape, F32) for w in ws] * 4,
        compiler_params=_cp(2))(c_arr, *own, *sib, *ws, *ms, *vs)


SMALL_WPOOL_ROW = 32
SMALL_SCALE_ROW = SMALL_WPOOL_ROW + 4 * GROUP
SMALL_BIAS_ROW = SMALL_SCALE_ROW + 8
SMALL_SINKS_ROW = SMALL_BIAS_ROW + NQ
SMALL_LOSS_ROW = SMALL_SINKS_ROW + 8


def _small_sum_adamw(gathered, wp, mp, vp):
    rows = wp.shape[0]

    def body(g_ref, w_ref, m_ref, v_ref, *rest):
        outs, res = rest[:-1], rest[-1]
        g = g_ref[0]
        for d in range(1, 8):
            g = g + g_ref[d]
        delta, m, v = _adamw_math(w_ref[...], g, m_ref[...], v_ref[...])
        for kind, val in enumerate((g, delta, m, v)):
            res[kind] = val
            gains = outs[8 * kind:8 * kind + 4]
            w_pool_o, scale_o, bias_o, sinks_o = outs[8 * kind + 4:8 * kind + 8]
            for t in range(4):
                for i in range(8):
                    gains[t][:, 128 * i:128 * (i + 1)] = res[kind, pl.ds(8 * t + i, 1), :]
            for grp in range(4):
                w_pool_o[grp] = res[kind, pl.ds(SMALL_WPOOL_ROW + GROUP * grp, GROUP), :]
            for i in range(4):
                scale_o[:, 128 * i:128 * (i + 1)] = res[kind, pl.ds(SMALL_SCALE_ROW + i, 1), :]
            bias_o[...] = res[kind, pl.ds(SMALL_BIAS_ROW, NQ), :][:, 0:NBUCKET]
            sinks_o[...] = res[kind, pl.ds(SMALL_SINKS_ROW, 1), :][:, 0:NQ]
        outs[-1][...] = res[0, pl.ds(SMALL_LOSS_ROW, 1), :]

    vm = pl.BlockSpec(memory_space=pltpu.VMEM)
    one_kind = [jax.ShapeDtypeStruct((1, D), F32)] * 4 + [
        jax.ShapeDtypeStruct((4, GROUP, GROUP), F32), jax.ShapeDtypeStruct((1, POOL_W), F32),
        jax.ShapeDtypeStruct((NQ, NBUCKET), F32), jax.ShapeDtypeStruct((1, NQ), F32)]
    return pl.pallas_call(
        body, name="small_sum_adamw", in_specs=[vm] * 4, out_specs=[vm] * 33,
        out_shape=one_kind * 4 + [jax.ShapeDtypeStruct((1, 128), F32)],
        scratch_shapes=[pltpu.VMEM((4, rows, 128), F32)],
        compiler_params=_cp())(gathered, wp, mp, vp)


def _pack_small(gains4, w_pool, pool_scale, rel_bias_t, sinks, loss):
    bias_rows = jnp.pad(rel_bias_t, ((0, 0), (0, 128 - rel_bias_t.shape[1])))
    parts = list(gains4) + [w_pool, pool_scale, bias_rows, sinks, loss]
    rows = []
    for a in parts:
        flat = a.reshape(-1)
        n = flat.shape[0]
        padded = -(-n // 1024) * 1024
        rows.append(jnp.pad(flat, (0, padded - n)).reshape(-1, 128))
    return jnp.concatenate(rows, axis=0)


def kernel(x, g_pre_mix, w_in, w_pool, pool_scale, rel_bias, sinks, w_out, g_post_mix, g_pre_ffn, w_gate, w_up, w_down, g_post_ffn, loss_target, m_g_pre_mix, m_w_in, m_w_pool, m_pool_scale, m_rel_bias, m_sinks, m_w_out, m_g_post_mix, m_g_pre_ffn, m_w_gate, m_w_up, m_w_down, m_g_post_ffn, v_g_pre_mix, v_w_in, v_w_pool, v_pool_scale, v_rel_bias, v_sinks, v_w_out, v_g_post_mix, v_g_pre_ffn, v_w_gate, v_w_up, v_w_down, v_g_post_ffn):
    c = lax.axis_index("c")
    chip = 2 * lax.axis_index("x") + lax.axis_index("y")

    def shards(a_in, a_out, a_gate, a_up, a_down):
        return (a_in[0].T, a_out[0], a_gate[0].T, a_up[0].T, a_down[0])

    c_arr = c.reshape(1).astype(jnp.int32)
    chip_arr = chip.reshape(1).astype(jnp.int32)

    big_w = shards(w_in, w_out, w_gate, w_up, w_down)
    g_ssem, g_rsem, _, g_lands, g_token = _split_start(
        _gather_copies, 15, [], _cast_into_slots(big_w, chip_arr), None, "gather_start")
    fw = {}

    def w_in_ready(*after):
        fw["first"] = _split_wait(_gather_copies, g_ssem, g_rsem, [], g_lands, after, "gather_win_wait",
                                  only=(0, 1, 2))
        (fw["win"],) = _gather_finish([fw["first"][0]], "win")
        return fw["win"].reshape(IN_W, D)

    def w_out_ready(*after):
        fw["second"] = _split_wait(functools.partial(_gather_copies, first=1), g_ssem, g_rsem, [],
                                   list(fw["first"][1:]), after, "gather_wout_wait", only=(0, 1, 2))
        (fw["wout"],) = _gather_finish([fw["second"][0]], "wout")
        return fw["wout"].reshape(D, D), None

    def ffn_forward_start(after):
        outs = _split_wait(functools.partial(_gather_copies, first=2), g_ssem, g_rsem, [], list(fw["second"][1:]),
                           [after], "gather_ffn_wait")
        fw["f"] = _split_start(_forward_copies, 9, [], list(outs), None, "gather_forward_start")
        return fw["f"][4]

    def ffn_weights(after):
        ssem, rsem, _, lands, _ = fw["f"]
        return _split_wait(_forward_copies, ssem, rsem, [], lands, [after], "gather_forward_wait")

    rs = {}

    def on_ffn_grads(ffn_g):
        oths = ffn_g[1::2]
        rs["ffn_own"] = ffn_g[0::2]
        rs["x"] = _split_start(_sibling_copies, 3, oths, [lax.empty(a.shape, BF16) for a in oths], ffn_g[0],
                               "rs_exchange_ffn_start")
        return rs["x"][4]

    def on_wout_grads(own, oth, after):
        ssem, rsem, srcs, lands, _ = rs["x"]
        recv_ffn = _split_wait(_sibling_copies, ssem, rsem, srcs, lands, [after], "rs_exchange_ffn_wait")[3:]
        recv_wout = _to_sibling([oth], "rs_exchange_wout")
        outs = _chip_partial([own] + list(rs["ffn_own"]), list(recv_wout) + list(recv_ffn), chip_arr, "early")
        rs["early_own"] = outs[4:]
        rs["s"] = _split_start(_scatter_copies, 12, outs[:4],
                               [lax.empty((3,) + a.shape[1:], BF16) for a in outs[:4]], outs[4], "scatter_early_start")
        return rs["s"][4]

    small = (g_pre_mix, g_post_mix, g_pre_ffn, g_post_ffn, w_pool[0], pool_scale, rel_bias, sinks)
    loss, gx, small_grads, ((win_own, win_oth), _, _) = _local_step(
        x[0], loss_target[0], small, w_in_ready, w_out_ready, ffn_weights, c_arr, on_ffn_grads, on_wout_grads,
        ffn_forward_start, g_token)

    ssem, rsem, srcs, lands, _ = rs["s"]
    recv_early = _split_wait(_scatter_copies, ssem, rsem, srcs, lands, [gx], "scatter_early_wait")[4:]
    finals = _sum_chips(list(rs["early_own"]), list(recv_early), "early")
    to_sib = [win_oth] + list(finals)
    sh_ssem, sh_rsem, sh_srcs, sh_lands, _ = _split_start(
        _sibling_copies, 5, to_sib, [lax.empty(a.shape, a.dtype) for a in to_sib], None, "rs_share_start")

    unused = jnp.zeros((1, 1), F32)
    gp = _pack_small(small_grads[:4], *small_grads[4:], loss)
    wp = _pack_small((g_pre_mix, g_post_mix, g_pre_ffn, g_post_ffn), w_pool, pool_scale, rel_bias.T, sinks, unused)
    mp = _pack_small((m_g_pre_mix, m_g_post_mix, m_g_pre_ffn, m_g_post_ffn), m_w_pool, m_pool_scale, m_rel_bias.T,
                     m_sinks, unused)
    vp = _pack_small((v_g_pre_mix, v_g_post_mix, v_g_pre_ffn, v_g_post_ffn), v_w_pool, v_pool_scale, v_rel_bias.T,
                     v_sinks, unused)

    moments = (shards(m_w_in, m_w_out, m_w_gate, m_w_up, m_w_down),
               shards(v_w_in, v_w_out, v_w_gate, v_w_up, v_w_down))
    sh_first = _split_wait(_sibling_copies, sh_ssem, sh_rsem, sh_srcs, sh_lands, [], "rs_share_win_wait", only=(0,))
    outs = _chip_partial([win_own], [sh_first[5]], chip_arr, "win")
    slots = lax.dynamic_update_slice(lax.empty((8,) + gp.shape, F32), gp[None], (2 * chip + c, 0, 0))
    w_ssem, w_rsem, w_srcs, w_lands, w_token = _split_start(
        _win_and_small_copies, 10, [outs[0], gp], [lax.empty((3,) + outs[0].shape[1:], BF16), slots], gx,
        "scatter_win_start")
    shared = _split_wait(_sibling_copies, sh_ssem, sh_rsem, sh_first[:5], sh_first[5:], [w_token],
                         "rs_share_early_wait", only=(1, 2, 3, 4))
    adam_e = _adamw_shards(shared[1:5], shared[6:], big_w[1:], moments[0][1:], moments[1][1:], c_arr, "early")
    w_first = _split_wait(_win_and_small_copies, w_ssem, w_rsem, w_srcs, w_lands, [adam_e[0]], "scatter_win_wait",
                          only=(0, 1, 2))
    win_final, win_sib = _sum_chips_shared(outs[1], w_first[2], "win")
    adam_w = _adamw_shards([win_final], [win_sib], big_w[:1], moments[0][:1], moments[1][:1], c_arr, "win")
    big_g, big_d, big_m, big_v = [[adam_w[i]] + list(adam_e[4 * i:4 * i + 4]) for i in range(4)]

    gathered = _split_wait(_win_and_small_copies, w_ssem, w_rsem, w_first[:2], w_first[2:], [adam_w[0]],
                           "small_allgather_wait", only=tuple(range(3, 10)))[3]
    flat = _small_sum_adamw(gathered, wp, mp, vp)
    small_out = [flat[8 * kind:8 * kind + 8] for kind in range(4)]
    total_loss = flat[-1][0, 0]

    def ordered(small8, big4):
        sg1, sg2, sg3, sg4, swp, ssc, srb, ssk = small8
        swp, srb = swp[None], srb.T
        bwin, bwout, bwg, bwu, bwd = big4[0].T[None], big4[1][None], big4[2].T[None], big4[3].T[None], big4[4][None]
        return [sg1, bwin, swp, ssc, srb, ssk, bwout, sg2, sg3, bwg, bwu, bwd, sg4]

    res = [total_loss, gx[None]]
    for sm, bg in zip(small_out, (big_g, big_d, big_m, big_v)):
        res += ordered(sm, bg)
    return tuple(res)
```

```python
import functools

import numpy as np
import jax
import jax.numpy as jnp
from jax import lax
from jax.experimental import pallas as pl
from jax.experimental.pallas import tpu as pltpu

F32 = jnp.float32
BF16 = jnp.bfloat16

D = 1024
POOL_W = 512
GROUP = 128
WINDOWS = (2, 4, 8, 16)
HALO = 128
NQ = 8
BLK = 128
NBUCKET = 32
IN_W = 1280
DFF = 2816
NSH = 4
FS = DFF // NSH
WIN_S = IN_W // NSH
WOUT_S = D // NSH
EPS = 1e-6
NEG = -1e30
SCALE = 0.125

ADAM_LR = 0.001
ADAM_B1 = 0.9
ADAM_B2 = 0.999
ADAM_EPS = 1e-08
ADAM_WD = 0.01
ADAM_STEP = 10

TM = 512
TM_IN = 1024
TM_POOL = 512
TM_FFN = 512
TM_FFN_FWD = 1024
FFN_ROWS = 256
VMEM_LIMIT = 60 * 1024 * 1024

MESH_T = pl.DeviceIdType.MESH
ANY = pl.BlockSpec(memory_space=pl.ANY)


def _cp(n_grid=0, **kw):
    sem = ("arbitrary",) * n_grid if n_grid else None
    return pltpu.CompilerParams(dimension_semantics=sem, vmem_limit_bytes=VMEM_LIMIT, **kw)


def _dot(a, b):
    return jnp.dot(a, b, preferred_element_type=F32)


def _dot_nt(a, b):
    return lax.dot_general(a, b, (((1,), (1,)), ((), ())), preferred_element_type=F32)


def _dot_tn(a, b):
    return lax.dot_general(a, b, (((0,), (0,)), ((), ())), preferred_element_type=F32)


def _rms(x):
    r = lax.rsqrt(jnp.mean(x * x, axis=-1, keepdims=True) + EPS)
    return r, x * r


def _rms_bwd(r, n, g, dout):
    dn = dout * g
    dx = r * (dn - n * jnp.mean(dn * n, axis=-1, keepdims=True))
    dg = jnp.sum(dout * n, axis=0, keepdims=True)
    return dx, dg


def _split(x, n):
    parts = []
    r = x
    for _ in range(n):
        p = r.astype(BF16)
        parts.append(p)
        r = r - p.astype(F32)
    return parts


def _band_dot(a, x, n):
    acc = None
    for p in _split(x, n):
        t = _dot(a, p)
        acc = t if acc is None else acc + t
    return acc


def _bucket_table():
    qi = np.arange(BLK)[None, :]
    kj = np.arange(2 * BLK)[:, None]
    dist = qi + BLK - kj
    n = np.maximum(dist, 0)
    nf = np.maximum(n, 1).astype(np.float32)
    large = 16 + (np.log(nf / np.float32(16)) / np.float32(np.log(128 / 16)) * np.float32(16)).astype(np.int32)
    large = np.minimum(large, NBUCKET - 1)
    return np.where(n < 16, n, large).astype(np.int32)


def _prenorm(x, g1, dep=None):
    s = x.shape[0]
    tm = min(TM_IN, s)

    def body(x_ref, g_ref, *rest):
        _, n = _rms(x_ref[...])
        rest[-1][...] = (n * g_ref[...]).astype(BF16)

    row = pl.BlockSpec((tm, D), lambda i: (i, 0))
    return pl.pallas_call(
        body, name="prenorm", grid=(s // tm,),
        in_specs=[row, pl.BlockSpec((1, D), lambda i: (0, 0))] + ([] if dep is None else [ANY]), out_specs=row,
        out_shape=jax.ShapeDtypeStruct((s, D), BF16),
        compiler_params=_cp(1))(x, g1, *([] if dep is None else [dep]))


def _inproj_fwd(h1, w_in):
    s = h1.shape[0]
    tm = min(TM_IN, s)

    def body(h_ref, w_ref, u_ref, q_ref, k_ref, v_ref):
        proj = _dot_nt(h_ref[...], w_ref[...])
        u_ref[...] = proj[:, :512]
        q_ref[...] = proj[:, 512:1024].astype(BF16)
        k_ref[...] = proj[:, 1024:1152].astype(BF16)
        v_ref[...] = proj[:, 1152:1280].astype(BF16)

    row = lambda w: pl.BlockSpec((tm, w), lambda i: (i, 0))
    return pl.pallas_call(
        body, name="inproj_fwd", grid=(s // tm,),
        in_specs=[row(D), pl.BlockSpec((IN_W, D), lambda i: (0, 0))],
        out_specs=[row(512), row(512), row(128), row(128)],
        out_shape=[jax.ShapeDtypeStruct((s, 512), F32), jax.ShapeDtypeStruct((s, 512), BF16),
                   jax.ShapeDtypeStruct((s, 128), BF16), jax.ShapeDtypeStruct((s, 128), BF16)],
        compiler_params=_cp(1))(h1, w_in)


def _window_sums(ext, w, lag_sign, tm, terms):
    rows = lax.broadcasted_iota(jnp.int32, (HALO, 2 * HALO), 0)
    cols = lax.broadcasted_iota(jnp.int32, (HALO, 2 * HALO), 1)
    d = rows + HALO - cols if lag_sign > 0 else cols - rows
    band = jnp.where((d >= 0) & (d < w), 1.0, 0.0).astype(BF16)
    return jnp.concatenate([_band_dot(band, ext[r:r + 2 * HALO], terms) for r in range(0, tm, HALO)], axis=0)


def _pooled(ext, cur, t0, tm):
    t = t0 + lax.broadcasted_iota(jnp.int32, (tm, GROUP), 0)
    out = []
    for g, w in enumerate(WINDOWS):
        sl = slice(g * GROUP, (g + 1) * GROUP)
        cnt = jnp.minimum(t + 1, w).astype(F32)
        out.append(_window_sums(ext[:, sl], w, 1, tm, 2) / cnt - cur[:, sl])
    return out


def _pool_fwd(u, w_pool, pool_scale):
    s = u.shape[0]
    tm = min(TM_POOL, s)
    hb = tm // HALO

    def body(uc_ref, uh_ref, wp_ref, sc_ref, o_ref, pooled_ref):
        i = pl.program_id(0)
        cur = uc_ref[...]
        halo = jnp.where(i > 0, uh_ref[...], 0.0)
        ext = jnp.concatenate([halo, cur], axis=0)
        pooled = _pooled(ext, cur, i * tm, tm)
        for g in range(4):
            sl = slice(g * GROUP, (g + 1) * GROUP)
            pb = pooled[g].astype(BF16)
            pooled_ref[:, sl] = pb
            o_ref[:, sl] = (_dot(pb, wp_ref[g]) * sc_ref[:, sl]).astype(BF16)

    row = pl.BlockSpec((tm, 512), lambda i: (i, 0))
    return pl.pallas_call(
        body, name="pool_fwd", grid=(s // tm,),
        in_specs=[row, pl.BlockSpec((HALO, 512), lambda i: (jnp.maximum(i * hb - 1, 0), 0)),
                  pl.BlockSpec((4, GROUP, GROUP), lambda i: (0, 0, 0)),
                  pl.BlockSpec((1, 512), lambda i: (0, 0))],
        out_specs=[row, row],
        out_shape=[jax.ShapeDtypeStruct((s, 512), BF16)] * 2,
        compiler_params=_cp(1))(u, u, w_pool, pool_scale)


def _bias_table(bucket, rel_bias):
    def body(b_ref, rb_ref, o_ref):
        bucket_v = b_ref[...]
        key = lax.broadcasted_iota(jnp.int32, (2 * BLK, BLK), 0)
        dist = lax.broadcasted_iota(jnp.int32, (2 * BLK, BLK), 1) + BLK - key
        inwin = (dist >= 0) & (dist < BLK)
        for h in range(NQ):
            acc = jnp.zeros((2 * BLK, BLK), F32)
            for b in range(NBUCKET):
                acc = jnp.where(bucket_v == b, rb_ref[b, h], acc)
            rest = jnp.where(inwin, acc, NEG)
            o_ref[1, h] = rest
            o_ref[0, h] = jnp.where(key >= BLK, rest, NEG)

    return pl.pallas_call(
        body, name="bias_table",
        in_specs=[pl.BlockSpec(memory_space=pltpu.VMEM), pl.BlockSpec(memory_space=pltpu.SMEM)],
        out_specs=pl.BlockSpec(memory_space=pltpu.VMEM),
        out_shape=jax.ShapeDtypeStruct((2, NQ, 2 * BLK, BLK), F32),
        compiler_params=_cp())(bucket, rel_bias)


HD = 64


def _kv_band(ref, n, transposed):
    pstart = pl.multiple_of(jnp.maximum(n - 1, 0) * BLK, BLK)
    cstart = pl.multiple_of(n * BLK, BLK)
    lo = lax.broadcasted_iota(jnp.int32, (2 * BLK, 128), 1) < HD
    band = jnp.concatenate([ref[pl.ds(pstart, BLK), :], ref[pl.ds(cstart, BLK), :]], axis=0).astype(F32)
    rot = pltpu.roll(band, HD, axis=1)
    halves = ((jnp.where(lo, band, 0.0), jnp.where(lo, 0.0, rot)), (jnp.where(lo, rot, 0.0), jnp.where(lo, 0.0, band)))
    if transposed:
        out = [jnp.concatenate([a.T, b.T], axis=1).astype(BF16) for a, b in halves]
    else:
        out = [jnp.concatenate([a, b], axis=0).astype(BF16) for a, b in halves]
    return out, (lo, pstart, cstart)


def _pair_probs(qt, kk, bias, sk_ref, p):
    sink = jnp.concatenate([jnp.full((1, 1, BLK), sk_ref[0, 2 * p + e], F32) for e in range(2)], axis=0)
    s = _dot_nt(kk, qt).reshape(2, 2 * BLK, BLK) + bias
    m = jnp.maximum(jnp.max(s, axis=1, keepdims=True), sink)
    pr = jnp.exp(s - m)
    es = jnp.exp(sink - m)
    inv = 1.0 / (jnp.sum(pr, axis=1, keepdims=True) + es)
    return pr * inv, es * inv


def _attn_fwd(q, k, v, bias, sinks):
    s = q.shape[0]
    nblk = s // BLK

    def body(q_ref, k_ref, v_ref, b_ref, sk_ref, o_ref, prob_ref, ps_ref):
        n = pl.program_id(0)
        kk, _ = _kv_band(k_ref, n, False)
        vt, _ = _kv_band(v_ref, n, True)
        bidx = jnp.minimum(n, 1)
        for p in range(4):
            h = p // 2
            qt = (q_ref[:, p * 128:(p + 1) * 128].astype(F32) * SCALE).astype(BF16)
            prob, ps = _pair_probs(qt, kk[h], b_ref[bidx, pl.ds(2 * p, 2)], sk_ref, p)
            pb = prob.astype(BF16)
            prob_ref[pl.ds(2 * p, 2)] = pb
            ps_ref[pl.ds(2 * p, 2), :] = ps.reshape(2, BLK)
            acc_t = _dot(vt[h], pb.reshape(4 * BLK, BLK))
            o_ref[:, p * 128:(p + 1) * 128] = acc_t.T.astype(BF16)

    return pl.pallas_call(
        body, name="attn_fwd", grid=(nblk,),
        in_specs=[pl.BlockSpec((BLK, 512), lambda i: (i, 0)),
                  pl.BlockSpec((s, 128), lambda i: (0, 0)),
                  pl.BlockSpec((s, 128), lambda i: (0, 0)),
                  pl.BlockSpec((2, NQ, 2 * BLK, BLK), lambda i: (0, 0, 0, 0)),
                  pl.BlockSpec(memory_space=pltpu.SMEM)],
        out_specs=[pl.BlockSpec((BLK, 512), lambda i: (i, 0)),
                   pl.BlockSpec((None, NQ, 2 * BLK, BLK), lambda i: (i, 0, 0, 0)),
                   pl.BlockSpec((None, NQ, BLK), lambda i: (i, 0, 0))],
        out_shape=[jax.ShapeDtypeStruct((s, 512), BF16), jax.ShapeDtypeStruct((nblk, NQ, 2 * BLK, BLK), BF16),
                   jax.ShapeDtypeStruct((nblk, NQ, BLK), F32)],
        compiler_params=_cp(1))(q, k, v, bias, sinks)


def _outproj_fwd(pool_o, attn_o, w_out, x, g2, g3, between=None):
    s = x.shape[0]
    tm = min(TM, s)
    nt = s // tm

    def part(name, tile0, tiles, filled, dep):
        def body(p_ref, a_ref, w_ref, x_ref, g2_ref, g3_ref, *rest):
            mix_ref, x1_ref, h2_ref = rest[-3:]
            mix = _dot(p_ref[...], w_ref[0:512, :]) + _dot(a_ref[...], w_ref[512:1024, :])
            mix_ref[...] = mix
            _, n2 = _rms(mix)
            x1 = x_ref[...] + n2 * g2_ref[...]
            x1_ref[...] = x1
            _, n3 = _rms(x1)
            h2_ref[...] = (n3 * g3_ref[...]).astype(BF16)

        row = lambda w: pl.BlockSpec((tm, w), lambda i: (i + tile0, 0))
        vec = pl.BlockSpec((1, D), lambda i: (0, 0))
        extra = list(filled) + ([] if dep is None else [dep])
        return pl.pallas_call(
            body, name=name, grid=(tiles,),
            in_specs=[row(512), row(512), pl.BlockSpec((D, D), lambda i: (0, 0)), row(D), vec, vec]
            + [ANY] * len(extra),
            out_specs=[row(D), row(D), row(D)],
            out_shape=[jax.ShapeDtypeStruct((s, D), F32), jax.ShapeDtypeStruct((s, D), F32),
                       jax.ShapeDtypeStruct((s, D), BF16)],
            input_output_aliases={6 + t: t for t in range(len(filled))},
            compiler_params=_cp(1))(pool_o, attn_o, w_out, x, g2, g3, *extra)

    if between is None or nt < 2:
        return part("outproj_fwd", 0, nt, (), None)
    head = max(1, 5 * nt // 8)
    first = part("outproj_fwd_a", 0, head, (), None)
    return part("outproj_fwd_b", head, nt - head, first, between(first[2]))


def _silu_parts(g):
    sg = jax.nn.sigmoid(g)
    return sg, g * sg


def _ffn_fwd(h2, wg, wu, wd, x1, target, g4):
    s = h2.shape[0]
    tm = min(TM_FFN_FWD, s)

    def body(h_ref, wg_ref, wu_ref, wd_ref, x1_ref, t_ref, g4_ref,
             gate_ref, up_ref, a_ref, df_ref, dy_ref, loss_ref, gg4_ref, f_acc):
        i = pl.program_id(0)
        j = pl.program_id(1)
        first = j == 0
        chunks = [pl.ds(r, min(FFN_ROWS, tm)) for r in range(0, tm, FFN_ROWS)]
        project = lambda rows: (_dot_nt(h_ref[rows, :], wg_ref[...]), _dot_nt(h_ref[rows, :], wu_ref[...]))
        ahead = [project(chunks[0])]
        for k, rows in enumerate(chunks):
            if k + 1 < len(chunks):
                ahead.append(project(chunks[k + 1]))
            gate, up = ahead[k]
            gate_ref[rows, :] = gate.astype(BF16)
            up_ref[rows, :] = up.astype(BF16)
            _, sl = _silu_parts(gate)
            a = (sl * up).astype(BF16)
            a_ref[rows, :] = a
            contrib = _dot(a, wd_ref[...])
            f_acc[rows, :] = jnp.where(first, contrib, f_acc[rows, :] + contrib)

        @pl.when((i == 0) & (j == 0))
        def _():
            loss_ref[...] = jnp.zeros_like(loss_ref)
            gg4_ref[...] = jnp.zeros_like(gg4_ref)

        @pl.when(j == NSH - 1)
        def _():
            r4, n4 = _rms(f_acc[...])
            g4v = g4_ref[...]
            err = x1_ref[...] + n4 * g4v - t_ref[...]
            loss_ref[...] += (0.5 / D) * jnp.sum(err * err).reshape(1, 1)
            dy = err * (1.0 / D)
            dy_ref[...] = dy
            df, dg = _rms_bwd(r4, n4, g4v, dy)
            gg4_ref[...] += dg
            df_ref[...] = df.astype(BF16)

    row = lambda w: pl.BlockSpec((tm, w), lambda i, j: (i, 0))
    sh = lambda r, c: pl.BlockSpec((None, r, c), lambda i, j: (j, 0, 0))
    act = pl.BlockSpec((None, tm, FS), lambda i, j: (j, i, 0))
    vec = pl.BlockSpec((1, D), lambda i, j: (0, 0))
    return pl.pallas_call(
        body, name="ffn_fwd", grid=(s // tm, NSH),
        in_specs=[row(D), sh(FS, D), sh(FS, D), sh(FS, D), row(D), row(D), vec],
        out_specs=[act, act, act, row(D), row(D), pl.BlockSpec((1, 1), lambda i, j: (0, 0)), vec],
        out_shape=[jax.ShapeDtypeStruct((NSH, s, FS), BF16)] * 3
        + [jax.ShapeDtypeStruct((s, D), BF16), jax.ShapeDtypeStruct((s, D), F32),
           jax.ShapeDtypeStruct((1, 1), F32), jax.ShapeDtypeStruct((1, D), F32)],
        scratch_shapes=[pltpu.VMEM((tm, D), F32)],
        compiler_params=_cp(2))(h2, wg, wu, wd, x1, target, g4)


def _ffn_bwd_act(df, gate, up, wg, wu, wd, dy, x1, mix, g3, g2):
    s = df.shape[0]
    tm = min(TM_FFN, s)

    def body(df_ref, gate_ref, up_ref, wg_ref, wu_ref, wd_ref, dy_ref, x1_ref, mix_ref, g3_ref, g2_ref,
             dgate_ref, dup_ref, dx1_ref, dmix_ref, gg3_ref, gg2_ref, acc):
        j = pl.program_id(0)
        i = pl.program_id(1)
        first = j == 0
        tile = pl.multiple_of(i * tm, tm)
        chunks = [pl.ds(r, min(FFN_ROWS, tm)) for r in range(0, tm, FFN_ROWS)]

        @pl.when((i == 0) & (j == 0))
        def _():
            gg3_ref[...] = jnp.zeros_like(gg3_ref)
            gg2_ref[...] = jnp.zeros_like(gg2_ref)

        def norms_backward(rows, dh2):
            r3, n3 = _rms(x1_ref[rows, :])
            dx1n, dg3 = _rms_bwd(r3, n3, g3_ref[...], dh2)
            dx1 = dy_ref[rows, :] + dx1n
            dx1_ref[rows, :] = dx1
            gg3_ref[...] += dg3
            r2, n2 = _rms(mix_ref[rows, :])
            dmix, dg2 = _rms_bwd(r2, n2, g2_ref[...], dx1)
            gg2_ref[...] += dg2
            dmix_ref[rows, :] = dmix.astype(BF16)

        def shard_pass(last):
            das = [_dot_nt(df_ref[chunks[0], :], wd_ref[...])]
            for k, rows in enumerate(chunks):
                if k + 1 < len(chunks):
                    das.append(_dot_nt(df_ref[chunks[k + 1], :], wd_ref[...]))
                da = das[k]
                g = gate_ref[rows, :].astype(F32)
                u = up_ref[rows, :].astype(F32)
                sg, sl = _silu_parts(g)
                dgate = (da * u * (sg * (1.0 + g * (1.0 - sg)))).astype(BF16)
                dup = (da * sl).astype(BF16)
                dgate_ref[rows, :] = dgate
                dup_ref[rows, :] = dup
                contrib = _dot(dgate, wg_ref[...]) + _dot(dup, wu_ref[...])
                acc_rows = pl.ds(tile + k * FFN_ROWS, rows.size)
                if last:
                    norms_backward(rows, acc[acc_rows, :] + contrib)
                else:
                    acc[acc_rows, :] = jnp.where(first, contrib, acc[acc_rows, :] + contrib)

        pl.when(j < NSH - 1)(functools.partial(shard_pass, False))
        pl.when(j == NSH - 1)(functools.partial(shard_pass, True))

    row = pl.BlockSpec((tm, D), lambda j, i: (i, 0))
    last = pl.BlockSpec((tm, D), lambda j, i: (i * (j // (NSH - 1)), 0))
    sh = pl.BlockSpec((None, FS, D), lambda j, i: (j, 0, 0))
    act = pl.BlockSpec((None, tm, FS), lambda j, i: (j, i, 0))
    vec = pl.BlockSpec((1, D), lambda j, i: (0, 0))
    return pl.pallas_call(
        body, name="ffn_bwd_act", grid=(NSH, s // tm),
        in_specs=[row, act, act, sh, sh, sh, last, last, last, vec, vec],
        out_specs=[act, act, last, last, vec, vec],
        out_shape=[jax.ShapeDtypeStruct((NSH, s, FS), BF16), jax.ShapeDtypeStruct((NSH, s, FS), BF16),
                   jax.ShapeDtypeStruct((s, D), F32), jax.ShapeDtypeStruct((s, D), BF16),
                   jax.ShapeDtypeStruct((1, D), F32), jax.ShapeDtypeStruct((1, D), F32)],
        scratch_shapes=[pltpu.VMEM((s, D), F32)],
        compiler_params=_cp(2))(df, gate, up, wg, wu, wd, dy, x1, mix, g3, g2)


SMEM_SPEC = pl.BlockSpec(memory_space=pltpu.SMEM)


def _emit_halves(acc_ref, row0, rows, c, own_ref, oth_ref):
    half = rows // 2
    own_ref[...] = acc_ref[pl.ds(row0 + pl.multiple_of(c * half, 8), half), :]
    oth_ref[...] = acc_ref[pl.ds(row0 + pl.multiple_of((1 - c) * half, 8), half), :].astype(BF16)


def _half_shapes(rows):
    return [jax.ShapeDtypeStruct((NSH, rows // 2, D), F32), jax.ShapeDtypeStruct((NSH, rows // 2, D), BF16)]


def _ffn_bwd_w(h2, act_a, dgate, dup, df, c_arr):
    s = h2.shape[0]
    tm = min(TM_FFN_FWD, s)
    nt = s // tm

    def body(c_ref, h_ref, a_ref, dgate_ref, dup_ref, df_ref, *rest):
        outs, accs = rest[:6], rest[6:]
        i = pl.program_id(1)

        @pl.when(i == 0)
        def _():
            for acc in accs:
                acc[...] = jnp.zeros_like(acc)

        h = h_ref[...]
        accs[0][...] += _dot_tn(dgate_ref[...], h)
        accs[1][...] += _dot_tn(dup_ref[...], h)
        accs[2][...] += _dot_tn(a_ref[...], df_ref[...])

        @pl.when(i == nt - 1)
        def _():
            for t, acc in enumerate(accs):
                _emit_halves(acc, 0, FS, c_ref[0], outs[2 * t], outs[2 * t + 1])

    row = lambda w: pl.BlockSpec((tm, w), lambda j, i: (i, 0))
    act = pl.BlockSpec((None, tm, FS), lambda j, i: (j, i, 0))
    half = pl.BlockSpec((None, FS // 2, D), lambda j, i: (j, 0, 0))
    return pl.pallas_call(
        body, name="ffn_bwd_w", grid=(NSH, nt),
        in_specs=[SMEM_SPEC, row(D), act, act, act, row(D)],
        out_specs=[half] * 6,
        out_shape=_half_shapes(FS) * 3,
        scratch_shapes=[pltpu.VMEM((FS, D), F32)] * 3,
        compiler_params=_cp(2))(c_arr, h2, act_a, dgate, dup, df)


def _outproj_bwd(dmix, w_out, pool_o, attn_o, c_arr, dep=None):
    s = dmix.shape[0]
    tm = min(TM, s)
    nt = s // tm

    def body(c_ref, dm_ref, w_ref, p_ref, a_ref, *rest):
        dpool_ref, dattn_ref, own_ref, oth_ref, gw_ref = rest[-5:]
        i = pl.program_id(0)

        @pl.when(i == 0)
        def _():
            gw_ref[...] = jnp.zeros_like(gw_ref)

        dm = dm_ref[...]
        dpool_ref[...] = _dot_nt(dm, w_ref[0:512, :])
        dattn_ref[...] = _dot_nt(dm, w_ref[512:1024, :]).astype(BF16)
        gw_ref[0:512, :] += _dot_tn(p_ref[...], dm)
        gw_ref[512:1024, :] += _dot_tn(a_ref[...], dm)

        @pl.when(i == nt - 1)
        def _():
            for k in range(NSH):
                _emit_halves(gw_ref, k * WOUT_S, WOUT_S, c_ref[0], own_ref.at[k], oth_ref.at[k])

    row = lambda w: pl.BlockSpec((tm, w), lambda i: (i, 0))
    full = pl.BlockSpec((D, D), lambda i: (0, 0))
    half = pl.BlockSpec((NSH, WOUT_S // 2, D), lambda i: (0, 0, 0))
    return pl.pallas_call(
        body, name="outproj_bwd", grid=(nt,),
        in_specs=[SMEM_SPEC, row(D), full, row(512), row(512)] + ([] if dep is None else [ANY]),
        out_specs=[row(512), row(512), half, half],
        out_shape=[jax.ShapeDtypeStruct((s, 512), F32), jax.ShapeDtypeStruct((s, 512), BF16)] + _half_shapes(WOUT_S),
        scratch_shapes=[pltpu.VMEM((D, D), F32)],
        compiler_params=_cp(1))(c_arr, dmix, w_out, pool_o, attn_o, *([] if dep is None else [dep]))


def _pool_bwd(pooled, dpool, w_pool, pool_scale, dep=None):
    s = pooled.shape[0]
    tm = min(TM_POOL, s)
    hb = tm // HALO
    nt = s // tm
    last_halo = s // HALO - 1

    def body(p_ref, dc_ref, dn_ref, wp_ref, sc_ref, *rest):
        du_ref, gwp_ref, gsc_ref = rest[-3:]
        i = pl.program_id(0)

        @pl.when(i == 0)
        def _():
            gwp_ref[...] = jnp.zeros_like(gwp_ref)
            gsc_ref[...] = jnp.zeros_like(gsc_ref)

        dcur = dc_ref[...]
        dnext = jnp.where(i < nt - 1, dn_ref[...], 0.0)
        dext = jnp.concatenate([dcur, dnext], axis=0)
        t_ext = i * tm + lax.broadcasted_iota(jnp.int32, (tm + HALO, GROUP), 0)
        for g, w in enumerate(WINDOWS):
            sl = slice(g * GROUP, (g + 1) * GROUP)
            pb = p_ref[:, sl]
            wp = wp_ref[g]
            mixed = _dot(pb, wp)
            gsc_ref[:, sl] += jnp.sum(dcur[:, sl] * mixed, axis=0, keepdims=True)
            dmixed = (dext[:, sl] * sc_ref[:, sl]).astype(BF16)
            gwp_ref[g] += _dot_tn(pb, dmixed[0:tm])
            dpooled = _dot_nt(dmixed, wp)
            z = dpooled / jnp.minimum(t_ext + 1, w).astype(F32)
            du_ref[:, sl] = (_window_sums(z, w, -1, tm, 2) - dpooled[0:tm]).astype(BF16)

    return pl.pallas_call(
        body, name="pool_bwd", grid=(nt,),
        in_specs=[pl.BlockSpec((tm, 512), lambda i: (i, 0)),
                  pl.BlockSpec((tm, 512), lambda i: (i, 0)),
                  pl.BlockSpec((HALO, 512), lambda i: (jnp.minimum((i + 1) * hb, last_halo), 0)),
                  pl.BlockSpec((4, GROUP, GROUP), lambda i: (0, 0, 0)),
                  pl.BlockSpec((1, 512), lambda i: (0, 0))] + ([] if dep is None else [ANY]),
        out_specs=[pl.BlockSpec((tm, 512), lambda i: (i, 0)),
                   pl.BlockSpec((4, GROUP, GROUP), lambda i: (0, 0, 0)),
                   pl.BlockSpec((1, 512), lambda i: (0, 0))],
        out_shape=[jax.ShapeDtypeStruct((s, 512), BF16), jax.ShapeDtypeStruct((4, GROUP, GROUP), F32),
                   jax.ShapeDtypeStruct((1, 512), F32)],
        compiler_params=_cp(1))(pooled, dpool, dpool, w_pool, pool_scale, *([] if dep is None else [dep]))


def _attn_bwd(q, k, v, do, probs, sink_share, bucket, dep=None):
    s = q.shape[0]
    nblk = s // BLK

    def body(q_ref, do_ref, k_ref, v_ref, prob_ref, ps_ref, bk_ref, *rest):
        dq_ref, dk_ref, dv_ref, grb_ref, gsk_ref, dss_ref = rest[-6:]
        n = pl.program_id(0)

        @pl.when(n == 0)
        def _():
            dk_ref[...] = jnp.zeros_like(dk_ref)
            dv_ref[...] = jnp.zeros_like(dv_ref)
            dss_ref[...] = jnp.zeros_like(dss_ref)
            gsk_ref[...] = jnp.zeros_like(gsk_ref)

        kt, (lo, pstart, cstart) = _kv_band(k_ref, n, True)
        vv, _ = _kv_band(v_ref, n, False)
        lo_q = lax.broadcasted_iota(jnp.int32, (BLK, 128), 1) < HD
        dkk = [jnp.zeros((2 * BLK, 128), F32), jnp.zeros((2 * BLK, 128), F32)]
        dvv = [jnp.zeros((2 * BLK, 128), F32), jnp.zeros((2 * BLK, 128), F32)]

        def by_head(t):
            return jnp.concatenate([jnp.where(lo_q, t, 0.0), jnp.where(lo_q, 0.0, t)], axis=0).astype(BF16)

        for p in range(4):
            h = p // 2
            qt = q_ref[:, p * 128:(p + 1) * 128].astype(F32) * SCALE
            dot = do_ref[:, p * 128:(p + 1) * 128]
            pb = prob_ref[pl.ds(2 * p, 2)]
            prob = pb.astype(F32)
            ps = ps_ref[pl.ds(2 * p, 2), :].reshape(2, 1, BLK)
            dp = _dot_nt(vv[h], dot).reshape(2, 2 * BLK, BLK)
            delta = jnp.sum(prob * dp, axis=1, keepdims=True)
            ds = prob * (dp - delta)
            sink_part = ps * delta
            for e in range(2):
                gs = -jnp.sum(sink_part[e]).reshape(1, 1)
                gsk_ref[pl.ds(2 * p + e, 1), :] += jnp.broadcast_to(gs, (1, 128))
            dss_ref[pl.ds(2 * p, 2)] += ds
            dsb = ds.astype(BF16)
            dq_t = _dot(kt[h], dsb.reshape(4 * BLK, BLK))
            dq_ref[:, p * 128:(p + 1) * 128] = (dq_t.T * SCALE).astype(BF16)
            dkk[h] = dkk[h] + _dot(jnp.concatenate([dsb[0], dsb[1]], axis=1), by_head(qt))
            dvv[h] = dvv[h] + _dot(jnp.concatenate([pb[0], pb[1]], axis=1), by_head(dot.astype(F32)))
        for acc, ref in ((dkk, dk_ref), (dvv, dv_ref)):
            f0 = acc[0] + pltpu.roll(acc[0], HD, axis=1)
            f1 = acc[1] + pltpu.roll(acc[1], HD, axis=1)
            band = jnp.where(lo, f0, f1)
            ref[pl.ds(pstart, BLK), :] += band[0:BLK]
            ref[pl.ds(cstart, BLK), :] += band[BLK:2 * BLK]

        @pl.when(n == nblk - 1)
        def _():
            _relbias_grad(dss_ref, bk_ref[...], grb_ref)

    blk = pl.BlockSpec((BLK, 512), lambda i: (i, 0))
    kv = pl.BlockSpec((s, 128), lambda i: (0, 0))
    row8 = pl.BlockSpec((NQ, 128), lambda i: (0, 0))
    return pl.pallas_call(
        body, name="attn_bwd", grid=(nblk,),
        in_specs=[blk, blk, kv, kv, pl.BlockSpec((None, NQ, 2 * BLK, BLK), lambda i: (i, 0, 0, 0)),
                  pl.BlockSpec((None, NQ, BLK), lambda i: (i, 0, 0)), pl.BlockSpec((2 * BLK, BLK), lambda i: (0, 0))]
        + ([] if dep is None else [ANY]),
        out_specs=[blk, kv, kv, row8, row8],
        out_shape=[jax.ShapeDtypeStruct((s, 512), BF16), jax.ShapeDtypeStruct((s, 128), F32),
                   jax.ShapeDtypeStruct((s, 128), F32), jax.ShapeDtypeStruct((NQ, 128), F32),
                   jax.ShapeDtypeStruct((NQ, 128), F32)],
        scratch_shapes=[pltpu.VMEM((NQ, 2 * BLK, BLK), F32)],
        compiler_params=_cp(1))(q, do, k, v, probs, sink_share, bucket, *([] if dep is None else [dep]))


def _relbias_grad(ds_ref, bucket_v, o_ref):
    lane = lax.broadcasted_iota(jnp.int32, (NQ, 128), 1)
    head_row = lax.broadcasted_iota(jnp.int32, (NQ, BLK), 0)
    acc = jnp.zeros((NQ, 128), F32)
    for b in range(NBUCKET):
        sel = bucket_v == b
        stack = jnp.zeros((NQ, BLK), F32)
        for h in range(NQ):
            col_sums = jnp.sum(jnp.where(sel, ds_ref[h], 0.0), axis=0, keepdims=True)
            stack = jnp.where(head_row == h, col_sums, stack)
        acc = acc + jnp.where(lane == b, jnp.sum(stack, axis=1, keepdims=True), 0.0)
    o_ref[...] = acc


def _inproj_bwd(x, g1, du, dq, dk, dv, w_in, dx1, c_arr):
    s = x.shape[0]
    tm = min(TM, s)
    nt = s // tm
    cols = ((0, 512), (512, 1024), (1024, 1152), (1152, 1280))

    def body(c_ref, x_ref, g_ref, du_ref, dq_ref, dk_ref, dv_ref, w_ref, dx1_ref,
             gx_ref, own_ref, oth_ref, gg_ref, gw_ref):
        i = pl.program_id(0)

        @pl.when(i == 0)
        def _():
            gw_ref[...] = jnp.zeros_like(gw_ref)
            gg_ref[...] = jnp.zeros_like(gg_ref)

        parts = (du_ref[...], dq_ref[...], dk_ref[...].astype(BF16), dv_ref[...].astype(BF16))
        dh = None
        for (a, b), dpart in zip(cols, parts):
            t = _dot(dpart, w_ref[a:b, :])
            dh = t if dh is None else dh + t
        gv = g_ref[...]
        r1, n1 = _rms(x_ref[...])
        h = (n1 * gv).astype(BF16)
        for (a, b), dpart in zip(cols, parts):
            gw_ref[a:b, :] += _dot_tn(dpart, h)
        dx, dg = _rms_bwd(r1, n1, gv, dh)
        gg_ref[...] += dg
        gx_ref[...] = dx1_ref[...] + dx

        @pl.when(i == nt - 1)
        def _():
            for k in range(NSH):
                _emit_halves(gw_ref, k * WIN_S, WIN_S, c_ref[0], own_ref.at[k], oth_ref.at[k])

    row = lambda w: pl.BlockSpec((tm, w), lambda i: (i, 0))
    vec = pl.BlockSpec((1, D), lambda i: (0, 0))
    full = pl.BlockSpec((IN_W, D), lambda i: (0, 0))
    half = pl.BlockSpec((NSH, WIN_S // 2, D), lambda i: (0, 0, 0))
    return pl.pallas_call(
        body, name="inproj_bwd", grid=(nt,),
        in_specs=[SMEM_SPEC, row(D), vec, row(512), row(512), row(128), row(128), full, row(D)],
        out_specs=[row(D), half, half, vec],
        out_shape=[jax.ShapeDtypeStruct((s, D), F32)] + _half_shapes(WIN_S) + [jax.ShapeDtypeStruct((1, D), F32)],
        scratch_shapes=[pltpu.VMEM((IN_W, D), F32)],
        compiler_params=_cp(1))(c_arr, x, g1, du, dq, dk, dv, w_in, dx1)


def _local_step(x, target, small, w_in, w_out, ffn_weights, c_arr, on_ffn_grads=None, on_wout_grads=None,
                mid_outproj=None, start_dep=None):
    g1, g2, g3, g4, w_pool, pool_scale, rel_bias, sinks = small
    bucket = jnp.asarray(_bucket_table())
    wp_bf = w_pool.astype(BF16)
    bias = _bias_table(bucket, rel_bias)
    h1 = _prenorm(x, g1, start_dep)
    if callable(w_in):
        w_in = w_in(bias, h1)
    u, q, k, v = _inproj_fwd(h1, w_in)
    pool_o, pooled = _pool_fwd(u, wp_bf, pool_scale)
    attn_o, probs, sink_share = _attn_fwd(q, k, v, bias, sinks)
    g2_fwd = g2
    if callable(w_out):
        w_out, after = w_out(pool_o, attn_o)
        if after is not None:
            g2_fwd = g2 + after[:1, :1]
    mix, x1, h2 = _outproj_fwd(pool_o, attn_o, w_out, x, g2_fwd, g3, mid_outproj)
    wg, wu, wd = ffn_weights(h2) if callable(ffn_weights) else ffn_weights
    gate, up, act_a, df, dy, loss, gg4 = _ffn_fwd(h2, wg, wu, wd, x1, target, g4)
    dgate, dup, dx1, dmix, gg3, gg2 = _ffn_bwd_act(df, gate, up, wg, wu, wd, dy, x1, mix, g3, g2)
    ffn_g = _ffn_bwd_w(h2, act_a, dgate, dup, df, c_arr)
    dep = None if on_ffn_grads is None else on_ffn_grads(ffn_g)
    dpool, dattn, wout_own, wout_oth = _outproj_bwd(dmix, w_out, pool_o, attn_o, c_arr, dep)
    dep = None if on_wout_grads is None else on_wout_grads(wout_own, wout_oth, dpool)
    du, gwp, gsc = _pool_bwd(pooled, dpool, wp_bf, pool_scale, dep)
    dq, dk, dv, grb, gsk = _attn_bwd(q, k, v, dattn, probs, sink_share, bucket, dep)
    gx, win_own, win_oth, gg1 = _inproj_bwd(x, g1, du, dq, dk, dv, w_in, dx1, c_arr)
    small_grads = (gg1, gg2, gg3, gg4, gwp, gsc, grb, gsk[:, 0].reshape(1, NQ))
    return loss, gx, small_grads, ((win_own, win_oth), (wout_own, wout_oth), ffn_g)


def _place():
    x, y, c = lax.axis_index("x"), lax.axis_index("y"), lax.axis_index("c")
    chips = ((1 - x, y), (x, 1 - y), (1 - x, 1 - y))
    return x, y, c, chips


def _remote(src, dst, ssem, rsem, dev):
    return pltpu.make_async_remote_copy(src_ref=src, dst_ref=dst, send_sem=ssem, recv_sem=rsem,
                                        device_id=dev, device_id_type=MESH_T)


def _to_sibling(arrs, name, after=()):
    nt, na = len(arrs), len(after)

    def body(*refs):
        srcs, dsts = refs[:nt], refs[nt + na:2 * nt + na]
        ssem, rsem = refs[2 * nt + na:]
        x, y, c, _ = _place()
        cps = [_remote(srcs[t], dsts[t], ssem.at[t], rsem.at[t], (x, y, 1 - c)) for t in range(nt)]
        for cp in cps:
            cp.start()
        for cp in cps:
            cp.wait()

    return pl.pallas_call(
        body, name=name, in_specs=[ANY] * (nt + na), out_specs=[ANY] * nt,
        out_shape=[jax.ShapeDtypeStruct(a.shape, a.dtype) for a in arrs],
        scratch_shapes=[pltpu.SemaphoreType.DMA((nt,)), pltpu.SemaphoreType.DMA((nt,))],
        compiler_params=_cp())(*arrs, *after)


HBM_SPEC = pl.BlockSpec(memory_space=pltpu.HBM)
SEM_SPEC = pl.BlockSpec(memory_space=pltpu.SEMAPHORE)
DATAFLOW = pltpu.SideEffectType.DATAFLOW_SIDE_EFFECTING


def _cast_into_slots(ws, chip_arr):
    nt, tiles = len(ws), 4

    def body(chip_ref, *refs):
        for t in range(nt):
            refs[nt + t][...] = refs[t][...].astype(BF16)

    return pl.pallas_call(
        body, name="cast_weights",
        grid_spec=pltpu.PrefetchScalarGridSpec(
            num_scalar_prefetch=1, grid=(tiles,),
            in_specs=[pl.BlockSpec((w.shape[0] // tiles, w.shape[1]), lambda i, chip_ref: (i, 0)) for w in ws],
            out_specs=[pl.BlockSpec((None, w.shape[0] // tiles, w.shape[1]), lambda i, chip_ref: (chip_ref[0], i, 0))
                       for w in ws]),
        out_shape=[jax.ShapeDtypeStruct((NSH,) + w.shape, BF16) for w in ws],
        compiler_params=_cp(1))(chip_arr, *ws)


def _gather_copies(srcs, lands, ssem, rsem, first=0):
    x, y, c, chips = _place()
    me = 2 * x + y
    out = []
    for t in range(len(lands)):
        half = lands[t].shape[1] // 2
        mine = pl.ds(pl.multiple_of(c * half, 16), half)
        for j, (px, py) in enumerate(chips):
            k = 3 * (first + t) + j
            send = _remote(lands[t].at[me, mine], lands[t].at[me, mine], ssem.at[k], rsem.at[k], (px, py, c))
            blk = lands[t].at[2 * px + py, mine]
            out.append((send, _remote(blk, blk, ssem.at[k], rsem.at[k], (px, py, c))))
    return out


def _scatter_copies(srcs, lands, ssem, rsem):
    x, y, c, chips = _place()
    out = []
    for t in range(len(srcs)):
        for j, (px, py) in enumerate(chips):
            k = 3 * t + j
            send = _remote(srcs[t].at[2 * px + py], lands[t].at[j], ssem.at[k], rsem.at[k], (px, py, c))
            out.append((send, _remote(lands[t].at[j], lands[t].at[j], ssem.at[k], rsem.at[k], (px, py, c))))
    return out


def _sibling_copies(srcs, lands, ssem, rsem):
    x, y, c, _ = _place()
    sib = (x, y, 1 - c)
    return [(_remote(srcs[t], lands[t], ssem.at[t], rsem.at[t], sib),
             _remote(lands[t], lands[t], ssem.at[t], rsem.at[t], sib)) for t in range(len(srcs))]


def _peer_copies(srcs, lands, ssem, rsem):
    x, y, c, _ = _place()
    me = 4 * x + 2 * y + c
    out = []
    for kk in range(1, 8):
        px, py, pc = x ^ (kk >> 2), y ^ ((kk >> 1) & 1), c ^ (kk & 1)
        send = _remote(srcs[0], lands[0].at[me], ssem.at[kk - 1], rsem.at[kk - 1], (px, py, pc))
        slot = lands[0].at[4 * px + 2 * py + pc]
        out.append((send, _remote(slot, slot, ssem.at[kk - 1], rsem.at[kk - 1], (px, py, pc))))
    return out


def _forward_copies(srcs, lands, ssem, rsem):
    x, y, c, chips = _place()
    sib = (x, y, 1 - c)
    out = []
    for t in range(len(lands)):
        half = lands[t].shape[1] // 2
        mine = pl.ds(pl.multiple_of(c * half, 16), half)
        other = pl.ds(pl.multiple_of((1 - c) * half, 16), half)
        for j, (px, py) in enumerate(chips):
            k = 3 * t + j
            blk_m, blk_o = lands[t].at[2 * px + py, mine], lands[t].at[2 * px + py, other]
            out.append((_remote(blk_m, blk_m, ssem.at[k], rsem.at[k], sib),
                        _remote(blk_o, blk_o, ssem.at[k], rsem.at[k], sib)))
    return out


def _win_and_small_copies(srcs, lands, ssem, rsem):
    return (_scatter_copies(srcs[:1], lands[:1], ssem, rsem)
            + _peer_copies(srcs[1:], lands[1:], ssem.at[pl.ds(3, 7)], rsem.at[pl.ds(3, 7)]))


def _split_start(plan, ncopy, srcs, lands, after, name):
    ns, nbuf = len(srcs), len(srcs) + len(lands)
    extra = [] if after is None else [after]
    n_in = nbuf + len(extra)

    def body(*refs):
        ssem, rsem, token = refs[n_in], refs[n_in + 1], refs[-1]
        for send, _ in plan(refs[:ns], refs[ns:nbuf], ssem, rsem):
            send.start()
        token[...] = jnp.zeros_like(token)

    bufs = [pltpu.with_memory_space_constraint(a, pltpu.HBM) for a in list(srcs) + list(lands)]
    outs = pl.pallas_call(
        body, name=name, in_specs=[HBM_SPEC] * nbuf + [ANY] * len(extra),
        out_specs=[SEM_SPEC, SEM_SPEC] + [HBM_SPEC] * nbuf + [pl.BlockSpec(memory_space=pltpu.VMEM)],
        out_shape=[pltpu.SemaphoreType.DMA((ncopy,)), pltpu.SemaphoreType.DMA((ncopy,))]
        + [pltpu.HBM(a.shape, a.dtype) for a in bufs] + [jax.ShapeDtypeStruct((8, 128), F32)],
        input_output_aliases={i: 2 + i for i in range(nbuf)},
        compiler_params=pltpu.CompilerParams(has_side_effects=DATAFLOW))(*bufs, *extra)
    return outs[0], outs[1], outs[2:2 + ns], outs[2 + ns:2 + nbuf], outs[-1]


def _split_wait(plan, ssem, rsem, srcs, lands, after, name, only=None):
    ns, nbuf = len(srcs), len(srcs) + len(lands)

    def body(*refs):
        s_ref, r_ref = refs[nbuf], refs[nbuf + 1]
        for k, (send, recv) in enumerate(plan(refs[:ns], refs[ns:nbuf], s_ref, r_ref)):
            if only is None or k in only:
                send.wait_send()
                recv.wait_recv()

    outs = pl.pallas_call(
        body, name=name, in_specs=[HBM_SPEC] * nbuf + [SEM_SPEC, SEM_SPEC] + [ANY] * len(after),
        out_specs=[HBM_SPEC] * nbuf,
        out_shape=[pltpu.HBM(a.shape, a.dtype) for a in list(srcs) + list(lands)],
        input_output_aliases={i: i for i in range(nbuf)},
        compiler_params=pltpu.CompilerParams(has_side_effects=DATAFLOW))(*srcs, *lands, ssem, rsem, *after)
    return outs


def _gather_finish(lands, tag):
    nt = len(lands)

    def body(*refs):
        outs = refs[nt:2 * nt]
        dsend, drecv = refs[2 * nt:]
        x, y, c, chips = _place()
        sib = (x, y, 1 - c)
        sends = []
        for t in range(nt):
            half = outs[t].shape[1] // 2
            mine = pl.ds(pl.multiple_of(c * half, 16), half)
            for j, (px, py) in enumerate(chips):
                blk = outs[t].at[2 * px + py, mine]
                cp = _remote(blk, blk, dsend.at[t, j], drecv.at[t, j], sib)
                cp.start()
                sends.append(cp)
        for t in range(nt):
            half = outs[t].shape[1] // 2
            other = pl.ds(pl.multiple_of((1 - c) * half, 16), half)
            for j, (px, py) in enumerate(chips):
                blk = outs[t].at[2 * px + py, other]
                _remote(blk, blk, dsend.at[t, j], drecv.at[t, j], sib).wait_recv()
        for cp in sends:
            cp.wait_send()

    return pl.pallas_call(
        body, name="gather_finish_" + tag, in_specs=[ANY] * nt, out_specs=[ANY] * nt,
        out_shape=[jax.ShapeDtypeStruct(a.shape, a.dtype) for a in lands],
        input_output_aliases={t: t for t in range(nt)},
        scratch_shapes=[pltpu.SemaphoreType.DMA((nt, 3)), pltpu.SemaphoreType.DMA((nt, 3))],
        compiler_params=_cp())(*lands)


def _chip_partial(owns, recv, chip_arr, tag, whole=False):
    nt = len(owns)

    if whole:
        def body_whole(chip_ref, *refs):
            for t in range(nt):
                refs[2 * nt + t][...] = (refs[t][...] + refs[nt + t][...].astype(F32)).astype(BF16)
                mine = chip_ref[0]
                refs[3 * nt + t][...] = refs[t][mine] + refs[nt + t][mine].astype(F32)

        vm = pl.BlockSpec(memory_space=pltpu.VMEM)
        return pl.pallas_call(
            body_whole, name="rs_chip_partial_" + tag, in_specs=[SMEM_SPEC] + [vm] * (2 * nt), out_specs=[vm] * (2 * nt),
            out_shape=[jax.ShapeDtypeStruct(a.shape, BF16) for a in owns]
            + [jax.ShapeDtypeStruct(a.shape[1:], F32) for a in owns],
            compiler_params=_cp())(chip_arr, *owns, *recv)

    def body(chip_ref, *refs):
        k = pl.program_id(0)
        for t in range(nt):
            p = refs[t][...] + refs[nt + t][...].astype(F32)
            refs[2 * nt + t][...] = p.astype(BF16)

            @pl.when(k == chip_ref[0])
            def _():
                refs[3 * nt + t][...] = p

    blk_specs = [pl.BlockSpec((None,) + a.shape[1:], lambda k, chip_ref: (k, 0, 0)) for a in owns]
    own_specs = [pl.BlockSpec(a.shape[1:], lambda k, chip_ref: (0, 0)) for a in owns]
    return pl.pallas_call(
        body, name="rs_chip_partial_" + tag,
        grid_spec=pltpu.PrefetchScalarGridSpec(num_scalar_prefetch=1, grid=(NSH,), in_specs=blk_specs * 2,
                                               out_specs=blk_specs + own_specs),
        out_shape=[jax.ShapeDtypeStruct(a.shape, BF16) for a in owns]
        + [jax.ShapeDtypeStruct(a.shape[1:], F32) for a in owns],
        compiler_params=_cp(1))(chip_arr, *owns, *recv)


def _sum_chips(own, recv, tag):
    nt = len(own)

    def body(*refs):
        outs = refs[2 * nt:]
        for t in range(nt):
            r = refs[nt + t]
            outs[t][...] = ((refs[t][...] + r[0].astype(F32)) + r[1].astype(F32)) + r[2].astype(F32)

    vm = pl.BlockSpec(memory_space=pltpu.VMEM)
    return pl.pallas_call(
        body, name="rs_sum_chips_" + tag, in_specs=[vm] * (2 * nt), out_specs=[vm] * nt,
        out_shape=[jax.ShapeDtypeStruct(a.shape, F32) for a in own],
        compiler_params=_cp())(*own, *recv)


def _sum_chips_shared(own, recv, tag):
    def body(own_ref, recv_ref, out_ref, sib_ref, ssem, rsem):
        out_ref[...] = ((own_ref[...] + recv_ref[0].astype(F32)) + recv_ref[1].astype(F32)) + recv_ref[2].astype(F32)
        x, y, c, _ = _place()
        cp = _remote(out_ref, sib_ref, ssem.at[0], rsem.at[0], (x, y, 1 - c))
        cp.start()
        cp.wait()

    vm = pl.BlockSpec(memory_space=pltpu.VMEM)
    return pl.pallas_call(
        body, name="rs_sum_share_" + tag, in_specs=[vm, vm], out_specs=[vm, ANY],
        out_shape=[jax.ShapeDtypeStruct(own.shape, F32)] * 2,
        scratch_shapes=[pltpu.SemaphoreType.DMA((1,)), pltpu.SemaphoreType.DMA((1,))],
        compiler_params=_cp())(own, recv)


def _adamw_math(w, g, m, v):
    m = ADAM_B1 * m + (1.0 - ADAM_B1) * g
    v = ADAM_B2 * v + (1.0 - ADAM_B2) * (g * g)
    m_hat = m / (1.0 - ADAM_B1 ** ADAM_STEP)
    v_hat = v / (1.0 - ADAM_B2 ** ADAM_STEP)
    delta = -ADAM_LR * (m_hat / (jnp.sqrt(v_hat) + ADAM_EPS) + ADAM_WD * w)
    return delta, m, v


ADAM_SPLIT = 4


def _adamw_shards(own, sib, ws, ms, vs, c_arr, tag):
    nt = len(own)

    def body(c_ref, *refs):
        hh = pl.program_id(0)
        mine = hh == c_ref[0]
        for t in range(nt):
            g = jnp.where(mine, refs[t][...], refs[nt + t][...])
            delta, m, v = _adamw_math(refs[2 * nt + t][...], g, refs[3 * nt + t][...], refs[4 * nt + t][...])
            refs[5 * nt + t][...] = g
            refs[6 * nt + t][...] = delta
            refs[7 * nt + t][...] = m
            refs[8 * nt + t][...] = v

    def tile(a):
        return (a.shape[0] // ADAM_SPLIT, a.shape[1])

    def half_index(used_when_mine):
        def index(hh, q, c_ref):
            mine = 1 - (hh - c_ref[0]) * (hh - c_ref[0])
            used = mine if used_when_mine else 1 - mine
            return (used * q + (1 - used) * hh * (ADAM_SPLIT - 1), 0)
        return index

    own_specs = [pl.BlockSpec(tile(a), half_index(True)) for a in own]
    sib_specs = [pl.BlockSpec(tile(a), half_index(False)) for a in own]
    full_specs = [pl.BlockSpec(tile(a), lambda hh, q, c_ref: (hh * ADAM_SPLIT + q, 0)) for a in own]
    return pl.pallas_call(
        body, name="adamw_shards_" + tag,
        grid_spec=pltpu.PrefetchScalarGridSpec(num_scalar_prefetch=1, grid=(2, ADAM_SPLIT),
                                               in_specs=own_specs + sib_specs + full_specs * 3,
                                               out_specs=full_specs * 4),
        out_shape=[jax.ShapeDtypeStruct(w.shape, F32) for w in ws] * 4,
        compiler_params=_cp(2))(c_arr, *own, *sib, *ws, *ms, *vs)


SMALL_WPOOL_ROW = 32
SMALL_SCALE_ROW = SMALL_WPOOL_ROW + 4 * GROUP
SMALL_BIAS_ROW = SMALL_SCALE_ROW + 8
SMALL_SINKS_ROW = SMALL_BIAS_ROW + NQ
SMALL_LOSS_ROW = SMALL_SINKS_ROW + 8


def _small_sum_adamw(gathered, wp, mp, vp):
    rows = wp.shape[0]

    def body(g_ref, w_ref, m_ref, v_ref, *rest):
        outs, res = rest[:-1], rest[-1]
        g = g_ref[0]
        for d in range(1, 8):
            g = g + g_ref[d]
        delta, m, v = _adamw_math(w_ref[...], g, m_ref[...], v_ref[...])
        for kind, val in enumerate((g, delta, m, v)):
            res[kind] = val
            gains = outs[8 * kind:8 * kind + 4]
            w_pool_o, scale_o, bias_o, sinks_o = outs[8 * kind + 4:8 * kind + 8]
            for t in range(4):
                for i in range(8):
                    gains[t][:, 128 * i:128 * (i + 1)] = res[kind, pl.ds(8 * t + i, 1), :]
            for grp in range(4):
                w_pool_o[grp] = res[kind, pl.ds(SMALL_WPOOL_ROW + GROUP * grp, GROUP), :]
            for i in range(4):
                scale_o[:, 128 * i:128 * (i + 1)] = res[kind, pl.ds(SMALL_SCALE_ROW + i, 1), :]
            bias_o[...] = res[kind, pl.ds(SMALL_BIAS_ROW, NQ), :][:, 0:NBUCKET]
            sinks_o[...] = res[kind, pl.ds(SMALL_SINKS_ROW, 1), :][:, 0:NQ]
        outs[-1][...] = res[0, pl.ds(SMALL_LOSS_ROW, 1), :]

    vm = pl.BlockSpec(memory_space=pltpu.VMEM)
    one_kind = [jax.ShapeDtypeStruct((1, D), F32)] * 4 + [
        jax.ShapeDtypeStruct((4, GROUP, GROUP), F32), jax.ShapeDtypeStruct((1, POOL_W), F32),
        jax.ShapeDtypeStruct((NQ, NBUCKET), F32), jax.ShapeDtypeStruct((1, NQ), F32)]
    return pl.pallas_call(
        body, name="small_sum_adamw", in_specs=[vm] * 4, out_specs=[vm] * 33,
        out_shape=one_kind * 4 + [jax.ShapeDtypeStruct((1, 128), F32)],
        scratch_shapes=[pltpu.VMEM((4, rows, 128), F32)],
        compiler_params=_cp())(gathered, wp, mp, vp)


def _pack_small(gains4, w_pool, pool_scale, rel_bias_t, sinks, loss):
    bias_rows = jnp.pad(rel_bias_t, ((0, 0), (0, 128 - rel_bias_t.shape[1])))
    parts = list(gains4) + [w_pool, pool_scale, bias_rows, sinks, loss]
    rows = []
    for a in parts:
        flat = a.reshape(-1)
        n = flat.shape[0]
        padded = -(-n // 1024) * 1024
        rows.append(jnp.pad(flat, (0, padded - n)).reshape(-1, 128))
    return jnp.concatenate(rows, axis=0)


def kernel(x, g_pre_mix, w_in, w_pool, pool_scale, rel_bias, sinks, w_out, g_post_mix, g_pre_ffn, w_gate, w_up, w_down, g_post_ffn, loss_target, m_g_pre_mix, m_w_in, m_w_pool, m_pool_scale, m_rel_bias, m_sinks, m_w_out, m_g_post_mix, m_g_pre_ffn, m_w_gate, m_w_up, m_w_down, m_g_post_ffn, v_g_pre_mix, v_w_in, v_w_pool, v_pool_scale, v_rel_bias, v_sinks, v_w_out, v_g_post_mix, v_g_pre_ffn, v_w_gate, v_w_up, v_w_down, v_g_post_ffn):
    c = lax.axis_index("c")
    chip = 2 * lax.axis_index("x") + lax.axis_index("y")

    def shards(a_in, a_out, a_gate, a_up, a_down):
        return (a_in[0].T, a_out[0], a_gate[0].T, a_up[0].T, a_down[0])

    c_arr = c.reshape(1).astype(jnp.int32)
    chip_arr = chip.reshape(1).astype(jnp.int32)

    big_w = shards(w_in, w_out, w_gate, w_up, w_down)
    g_ssem, g_rsem, _, g_lands, g_token = _split_start(
        _gather_copies, 15, [], _cast_into_slots(big_w, chip_arr), None, "gather_start")
    fw = {}

    def w_in_ready(*after):
        fw["first"] = _split_wait(_gather_copies, g_ssem, g_rsem, [], g_lands, after, "gather_win_wait",
                                  only=(0, 1, 2))
        (fw["win"],) = _gather_finish([fw["first"][0]], "win")
        return fw["win"].reshape(IN_W, D)

    def w_out_ready(*after):
        fw["second"] = _split_wait(functools.partial(_gather_copies, first=1), g_ssem, g_rsem, [],
                                   list(fw["first"][1:]), after, "gather_wout_wait", only=(0, 1, 2))
        (fw["wout"],) = _gather_finish([fw["second"][0]], "wout")
        return fw["wout"].reshape(D, D), None

    def ffn_forward_start(after):
        outs = _split_wait(functools.partial(_gather_copies, first=2), g_ssem, g_rsem, [], list(fw["second"][1:]),
                           [after], "gather_ffn_wait")
        fw["f"] = _split_start(_forward_copies, 9, [], list(outs), None, "gather_forward_start")
        return fw["f"][4]

    def ffn_weights(after):
        ssem, rsem, _, lands, _ = fw["f"]
        return _split_wait(_forward_copies, ssem, rsem, [], lands, [after], "gather_forward_wait")

    rs = {}

    def on_ffn_grads(ffn_g):
        oths = ffn_g[1::2]
        rs["ffn_own"] = ffn_g[0::2]
        rs["x"] = _split_start(_sibling_copies, 3, oths, [lax.empty(a.shape, BF16) for a in oths], ffn_g[0],
                               "rs_exchange_ffn_start")
        return rs["x"][4]

    def on_wout_grads(own, oth, after):
        ssem, rsem, srcs, lands, _ = rs["x"]
        recv_ffn = _split_wait(_sibling_copies, ssem, rsem, srcs, lands, [after], "rs_exchange_ffn_wait")[3:]
        recv_wout = _to_sibling([oth], "rs_exchange_wout")
        outs = _chip_partial([own] + list(rs["ffn_own"]), list(recv_wout) + list(recv_ffn), chip_arr, "early")
        rs["early_own"] = outs[4:]
        rs["s"] = _split_start(_scatter_copies, 12, outs[:4],
                               [lax.empty((3,) + a.shape[1:], BF16) for a in outs[:4]], outs[4], "scatter_early_start")
        return rs["s"][4]

    small = (g_pre_mix, g_post_mix, g_pre_ffn, g_post_ffn, w_pool[0], pool_scale, rel_bias, sinks)
    loss, gx, small_grads, ((win_own, win_oth), _, _) = _local_step(
        x[0], loss_target[0], small, w_in_ready, w_out_ready, ffn_weights, c_arr, on_ffn_grads, on_wout_grads,
        ffn_forward_start, g_token)

    ssem, rsem, srcs, lands, _ = rs["s"]
    recv_early = _split_wait(_scatter_copies, ssem, rsem, srcs, lands, [gx], "scatter_early_wait")[4:]
    finals = _sum_chips(list(rs["early_own"]), list(recv_early), "early")
    to_sib = [win_oth] + list(finals)
    sh_ssem, sh_rsem, sh_srcs, sh_lands, _ = _split_start(
        _sibling_copies, 5, to_sib, [lax.empty(a.shape, a.dtype) for a in to_sib], None, "rs_share_start")

    unused = jnp.zeros((1, 1), F32)
    gp = _pack_small(small_grads[:4], *small_grads[4:], loss)
    wp = _pack_small((g_pre_mix, g_post_mix, g_pre_ffn, g_post_ffn), w_pool, pool_scale, rel_bias.T, sinks, unused)
    mp = _pack_small((m_g_pre_mix, m_g_post_mix, m_g_pre_ffn, m_g_post_ffn), m_w_pool, m_pool_scale, m_rel_bias.T,
                     m_sinks, unused)
    vp = _pack_small((v_g_pre_mix, v_g_post_mix, v_g_pre_ffn, v_g_post_ffn), v_w_pool, v_pool_scale, v_rel_bias.T,
                     v_sinks, unused)

    moments = (shards(m_w_in, m_w_out, m_w_gate, m_w_up, m_w_down),
               shards(v_w_in, v_w_out, v_w_gate, v_w_up, v_w_down))
    sh_first = _split_wait(_sibling_copies, sh_ssem, sh_rsem, sh_srcs, sh_lands, [], "rs_share_win_wait", only=(0,))
    outs = _chip_partial([win_own], [sh_first[5]], chip_arr, "win", whole=True)
    slots = lax.dynamic_update_slice(lax.empty((8,) + gp.shape, F32), gp[None], (2 * chip + c, 0, 0))
    w_ssem, w_rsem, w_srcs, w_lands, w_token = _split_start(
        _win_and_small_copies, 10, [outs[0], gp], [lax.empty((3,) + outs[0].shape[1:], BF16), slots], gx,
        "scatter_win_start")
    shared = _split_wait(_sibling_copies, sh_ssem, sh_rsem, sh_first[:5], sh_first[5:], [w_token],
                         "rs_share_early_wait", only=(1, 2, 3, 4))
    adam_e = _adamw_shards(shared[1:5], shared[6:], big_w[1:], moments[0][1:], moments[1][1:], c_arr, "early")
    w_first = _split_wait(_win_and_small_copies, w_ssem, w_rsem, w_srcs, w_lands, [adam_e[0]], "scatter_win_wait",
                          only=(0, 1, 2))
    win_final, win_sib = _sum_chips_shared(outs[1], w_first[2], "win")
    adam_w = _adamw_shards([win_final], [win_sib], big_w[:1], moments[0][:1], moments[1][:1], c_arr, "win")
    big_g, big_d, big_m, big_v = [[adam_w[i]] + list(adam_e[4 * i:4 * i + 4]) for i in range(4)]

    gathered = _split_wait(_win_and_small_copies, w_ssem, w_rsem, w_first[:2], w_first[2:], [adam_w[0]],
                           "small_allgather_wait", only=tuple(range(3, 10)))[3]
    flat = _small_sum_adamw(gathered, wp, mp, vp)
    small_out = [flat[8 * kind:8 * kind + 8] for kind in range(4)]
    total_loss = flat[-1][0, 0]

    def ordered(small8, big4):
        sg1, sg2, sg3, sg4, swp, ssc, srb, ssk = small8
        swp, srb = swp[None], srb.T
        bwin, bwout, bwg, bwu, bwd = big4[0].T[None], big4[1][None], big4[2].T[None], big4[3].T[None], big4[4][None]
        return [sg1, bwin, swp, ssc, srb, ssk, bwout, sg2, sg3, bwg, bwu, bwd, sg4]

    res = [total_loss, gx[None]]
    for sm, bg in zip(small_out, (big_g, big_d, big_m, big_v)):
        res += ordered(sm, bg)
    return tuple(res)
```

```python
import functools

import numpy as np
import jax
import jax.numpy as jnp
from jax import lax
from jax.experimental import pallas as pl
from jax.experimental.pallas import tpu as pltpu

F32 = jnp.float32
BF16 = jnp.bfloat16

D = 1024
POOL_W = 512
GROUP = 128
WINDOWS = (2, 4, 8, 16)
HALO = 128
NQ = 8
BLK = 128
NBUCKET = 32
IN_W = 1280
DFF = 2816
NSH = 4
FS = DFF // NSH
WIN_S = IN_W // NSH
WOUT_S = D // NSH
EPS = 1e-6
NEG = -1e30
SCALE = 0.125

ADAM_LR = 0.001
ADAM_B1 = 0.9
ADAM_B2 = 0.999
ADAM_EPS = 1e-08
ADAM_WD = 0.01
ADAM_STEP = 10

TM = 512
TM_IN = 1024
TM_POOL = 1024
TM_FFN = 512
TM_FFN_FWD = 1024
FFN_ROWS = 256
VMEM_LIMIT = 60 * 1024 * 1024

MESH_T = pl.DeviceIdType.MESH
ANY = pl.BlockSpec(memory_space=pl.ANY)


def _cp(n_grid=0, **kw):
    sem = ("arbitrary",) * n_grid if n_grid else None
    return pltpu.CompilerParams(dimension_semantics=sem, vmem_limit_bytes=VMEM_LIMIT, **kw)


def _dot(a, b):
    return jnp.dot(a, b, preferred_element_type=F32)


def _dot_nt(a, b):
    return lax.dot_general(a, b, (((1,), (1,)), ((), ())), preferred_element_type=F32)


def _dot_tn(a, b):
    return lax.dot_general(a, b, (((0,), (0,)), ((), ())), preferred_element_type=F32)


def _rms(x):
    r = lax.rsqrt(jnp.mean(x * x, axis=-1, keepdims=True) + EPS)
    return r, x * r


def _rms_bwd(r, n, g, dout):
    dn = dout * g
    dx = r * (dn - n * jnp.mean(dn * n, axis=-1, keepdims=True))
    dg = jnp.sum(dout * n, axis=0, keepdims=True)
    return dx, dg


def _split(x, n):
    parts = []
    r = x
    for _ in range(n):
        p = r.astype(BF16)
        parts.append(p)
        r = r - p.astype(F32)
    return parts


def _band_dot(a, x, n):
    acc = None
    for p in _split(x, n):
        t = _dot(a, p)
        acc = t if acc is None else acc + t
    return acc


def _bucket_table():
    qi = np.arange(BLK)[None, :]
    kj = np.arange(2 * BLK)[:, None]
    dist = qi + BLK - kj
    n = np.maximum(dist, 0)
    nf = np.maximum(n, 1).astype(np.float32)
    large = 16 + (np.log(nf / np.float32(16)) / np.float32(np.log(128 / 16)) * np.float32(16)).astype(np.int32)
    large = np.minimum(large, NBUCKET - 1)
    return np.where(n < 16, n, large).astype(np.int32)


def _prenorm(x, g1, dep=None):
    s = x.shape[0]
    tm = min(TM_IN, s)

    def body(x_ref, g_ref, *rest):
        _, n = _rms(x_ref[...])
        rest[-1][...] = (n * g_ref[...]).astype(BF16)

    row = pl.BlockSpec((tm, D), lambda i: (i, 0))
    return pl.pallas_call(
        body, name="prenorm", grid=(s // tm,),
        in_specs=[row, pl.BlockSpec((1, D), lambda i: (0, 0))] + ([] if dep is None else [ANY]), out_specs=row,
        out_shape=jax.ShapeDtypeStruct((s, D), BF16),
        compiler_params=_cp(1))(x, g1, *([] if dep is None else [dep]))


def _inproj_fwd(h1, w_in):
    s = h1.shape[0]
    tm = min(TM_IN, s)

    def body(h_ref, w_ref, u_ref, q_ref, k_ref, v_ref):
        proj = _dot_nt(h_ref[...], w_ref[...])
        u_ref[...] = proj[:, :512]
        q_ref[...] = proj[:, 512:1024].astype(BF16)
        k_ref[...] = proj[:, 1024:1152].astype(BF16)
        v_ref[...] = proj[:, 1152:1280].astype(BF16)

    row = lambda w: pl.BlockSpec((tm, w), lambda i: (i, 0))
    return pl.pallas_call(
        body, name="inproj_fwd", grid=(s // tm,),
        in_specs=[row(D), pl.BlockSpec((IN_W, D), lambda i: (0, 0))],
        out_specs=[row(512), row(512), row(128), row(128)],
        out_shape=[jax.ShapeDtypeStruct((s, 512), F32), jax.ShapeDtypeStruct((s, 512), BF16),
                   jax.ShapeDtypeStruct((s, 128), BF16), jax.ShapeDtypeStruct((s, 128), BF16)],
        compiler_params=_cp(1))(h1, w_in)


def _window_sums(ext, w, lag_sign, tm, terms):
    rows = lax.broadcasted_iota(jnp.int32, (HALO, 2 * HALO), 0)
    cols = lax.broadcasted_iota(jnp.int32, (HALO, 2 * HALO), 1)
    d = rows + HALO - cols if lag_sign > 0 else cols - rows
    band = jnp.where((d >= 0) & (d < w), 1.0, 0.0).astype(BF16)
    return jnp.concatenate([_band_dot(band, ext[r:r + 2 * HALO], terms) for r in range(0, tm, HALO)], axis=0)


def _pooled(ext, cur, t0, tm):
    t = t0 + lax.broadcasted_iota(jnp.int32, (tm, GROUP), 0)
    out = []
    for g, w in enumerate(WINDOWS):
        sl = slice(g * GROUP, (g + 1) * GROUP)
        cnt = jnp.minimum(t + 1, w).astype(F32)
        out.append(_window_sums(ext[:, sl], w, 1, tm, 2) / cnt - cur[:, sl])
    return out


def _pool_fwd(u, w_pool, pool_scale):
    s = u.shape[0]
    tm = min(TM_POOL, s)
    hb = tm // HALO

    def body(uc_ref, uh_ref, wp_ref, sc_ref, o_ref, pooled_ref):
        i = pl.program_id(0)
        cur = uc_ref[...]
        halo = jnp.where(i > 0, uh_ref[...], 0.0)
        ext = jnp.concatenate([halo, cur], axis=0)
        pooled = _pooled(ext, cur, i * tm, tm)
        for g in range(4):
            sl = slice(g * GROUP, (g + 1) * GROUP)
            pb = pooled[g].astype(BF16)
            pooled_ref[:, sl] = pb
            o_ref[:, sl] = (_dot(pb, wp_ref[g]) * sc_ref[:, sl]).astype(BF16)

    row = pl.BlockSpec((tm, 512), lambda i: (i, 0))
    return pl.pallas_call(
        body, name="pool_fwd", grid=(s // tm,),
        in_specs=[row, pl.BlockSpec((HALO, 512), lambda i: (jnp.maximum(i * hb - 1, 0), 0)),
                  pl.BlockSpec((4, GROUP, GROUP), lambda i: (0, 0, 0)),
                  pl.BlockSpec((1, 512), lambda i: (0, 0))],
        out_specs=[row, row],
        out_shape=[jax.ShapeDtypeStruct((s, 512), BF16)] * 2,
        compiler_params=_cp(1))(u, u, w_pool, pool_scale)


def _bias_table(bucket, rel_bias):
    def body(b_ref, rb_ref, o_ref):
        bucket_v = b_ref[...]
        key = lax.broadcasted_iota(jnp.int32, (2 * BLK, BLK), 0)
        dist = lax.broadcasted_iota(jnp.int32, (2 * BLK, BLK), 1) + BLK - key
        inwin = (dist >= 0) & (dist < BLK)
        for h in range(NQ):
            acc = jnp.zeros((2 * BLK, BLK), F32)
            for b in range(NBUCKET):
                acc = jnp.where(bucket_v == b, rb_ref[b, h], acc)
            rest = jnp.where(inwin, acc, NEG)
            o_ref[1, h] = rest
            o_ref[0, h] = jnp.where(key >= BLK, rest, NEG)

    return pl.pallas_call(
        body, name="bias_table",
        in_specs=[pl.BlockSpec(memory_space=pltpu.VMEM), pl.BlockSpec(memory_space=pltpu.SMEM)],
        out_specs=pl.BlockSpec(memory_space=pltpu.VMEM),
        out_shape=jax.ShapeDtypeStruct((2, NQ, 2 * BLK, BLK), F32),
        compiler_params=_cp())(bucket, rel_bias)


HD = 64


def _kv_band(ref, n, transposed):
    pstart = pl.multiple_of(jnp.maximum(n - 1, 0) * BLK, BLK)
    cstart = pl.multiple_of(n * BLK, BLK)
    lo = lax.broadcasted_iota(jnp.int32, (2 * BLK, 128), 1) < HD
    band = jnp.concatenate([ref[pl.ds(pstart, BLK), :], ref[pl.ds(cstart, BLK), :]], axis=0).astype(F32)
    rot = pltpu.roll(band, HD, axis=1)
    halves = ((jnp.where(lo, band, 0.0), jnp.where(lo, 0.0, rot)), (jnp.where(lo, rot, 0.0), jnp.where(lo, 0.0, band)))
    if transposed:
        out = [jnp.concatenate([a.T, b.T], axis=1).astype(BF16) for a, b in halves]
    else:
        out = [jnp.concatenate([a, b], axis=0).astype(BF16) for a, b in halves]
    return out, (lo, pstart, cstart)


def _pair_probs(qt, kk, bias, sk_ref, p):
    sink = jnp.concatenate([jnp.full((1, 1, BLK), sk_ref[0, 2 * p + e], F32) for e in range(2)], axis=0)
    s = _dot_nt(kk, qt).reshape(2, 2 * BLK, BLK) + bias
    m = jnp.maximum(jnp.max(s, axis=1, keepdims=True), sink)
    pr = jnp.exp(s - m)
    es = jnp.exp(sink - m)
    inv = 1.0 / (jnp.sum(pr, axis=1, keepdims=True) + es)
    return pr * inv, es * inv


def _attn_fwd(q, k, v, bias, sinks):
    s = q.shape[0]
    nblk = s // BLK

    def body(q_ref, k_ref, v_ref, b_ref, sk_ref, o_ref, prob_ref, ps_ref):
        n = pl.program_id(0)
        kk, _ = _kv_band(k_ref, n, False)
        vt, _ = _kv_band(v_ref, n, True)
        bidx = jnp.minimum(n, 1)
        for p in range(4):
            h = p // 2
            qt = (q_ref[:, p * 128:(p + 1) * 128].astype(F32) * SCALE).astype(BF16)
            prob, ps = _pair_probs(qt, kk[h], b_ref[bidx, pl.ds(2 * p, 2)], sk_ref, p)
            pb = prob.astype(BF16)
            prob_ref[pl.ds(2 * p, 2)] = pb
            ps_ref[pl.ds(2 * p, 2), :] = ps.reshape(2, BLK)
            acc_t = _dot(vt[h], pb.reshape(4 * BLK, BLK))
            o_ref[:, p * 128:(p + 1) * 128] = acc_t.T.astype(BF16)

    return pl.pallas_call(
        body, name="attn_fwd", grid=(nblk,),
        in_specs=[pl.BlockSpec((BLK, 512), lambda i: (i, 0)),
                  pl.BlockSpec((s, 128), lambda i: (0, 0)),
                  pl.BlockSpec((s, 128), lambda i: (0, 0)),
                  pl.BlockSpec((2, NQ, 2 * BLK, BLK), lambda i: (0, 0, 0, 0)),
                  pl.BlockSpec(memory_space=pltpu.SMEM)],
        out_specs=[pl.BlockSpec((BLK, 512), lambda i: (i, 0)),
                   pl.BlockSpec((None, NQ, 2 * BLK, BLK), lambda i: (i, 0, 0, 0)),
                   pl.BlockSpec((None, NQ, BLK), lambda i: (i, 0, 0))],
        out_shape=[jax.ShapeDtypeStruct((s, 512), BF16), jax.ShapeDtypeStruct((nblk, NQ, 2 * BLK, BLK), BF16),
                   jax.ShapeDtypeStruct((nblk, NQ, BLK), F32)],
        compiler_params=_cp(1))(q, k, v, bias, sinks)


def _outproj_fwd(pool_o, attn_o, w_out, x, g2, g3, between=None):
    s = x.shape[0]
    tm = min(TM, s)
    nt = s // tm

    def part(name, tile0, tiles, filled, dep):
        def body(p_ref, a_ref, w_ref, x_ref, g2_ref, g3_ref, *rest):
            mix_ref, x1_ref, h2_ref = rest[-3:]
            mix = _dot(p_ref[...], w_ref[0:512, :]) + _dot(a_ref[...], w_ref[512:1024, :])
            mix_ref[...] = mix
            _, n2 = _rms(mix)
            x1 = x_ref[...] + n2 * g2_ref[...]
            x1_ref[...] = x1
            _, n3 = _rms(x1)
            h2_ref[...] = (n3 * g3_ref[...]).astype(BF16)

        row = lambda w: pl.BlockSpec((tm, w), lambda i: (i + tile0, 0))
        vec = pl.BlockSpec((1, D), lambda i: (0, 0))
        extra = list(filled) + ([] if dep is None else [dep])
        return pl.pallas_call(
            body, name=name, grid=(tiles,),
            in_specs=[row(512), row(512), pl.BlockSpec((D, D), lambda i: (0, 0)), row(D), vec, vec]
            + [ANY] * len(extra),
            out_specs=[row(D), row(D), row(D)],
            out_shape=[jax.ShapeDtypeStruct((s, D), F32), jax.ShapeDtypeStruct((s, D), F32),
                       jax.ShapeDtypeStruct((s, D), BF16)],
            input_output_aliases={6 + t: t for t in range(len(filled))},
            compiler_params=_cp(1))(pool_o, attn_o, w_out, x, g2, g3, *extra)

    if between is None or nt < 2:
        return part("outproj_fwd", 0, nt, (), None)
    head = max(1, 5 * nt // 8)
    first = part("outproj_fwd_a", 0, head, (), None)
    return part("outproj_fwd_b", head, nt - head, first, between(first[2]))


def _silu_parts(g):
    sg = jax.nn.sigmoid(g)
    return sg, g * sg


def _ffn_fwd(h2, wg, wu, wd, x1, target, g4):
    s = h2.shape[0]
    tm = min(TM_FFN_FWD, s)

    def body(h_ref, wg_ref, wu_ref, wd_ref, x1_ref, t_ref, g4_ref,
             gate_ref, up_ref, a_ref, df_ref, dy_ref, loss_ref, gg4_ref, f_acc):
        i = pl.program_id(0)
        j = pl.program_id(1)
        first = j == 0
        chunks = [pl.ds(r, min(FFN_ROWS, tm)) for r in range(0, tm, FFN_ROWS)]
        project = lambda rows: (_dot_nt(h_ref[rows, :], wg_ref[...]), _dot_nt(h_ref[rows, :], wu_ref[...]))
        ahead = [project(chunks[0])]
        for k, rows in enumerate(chunks):
            if k + 1 < len(chunks):
                ahead.append(project(chunks[k + 1]))
            gate, up = ahead[k]
            gate_ref[rows, :] = gate.astype(BF16)
            up_ref[rows, :] = up.astype(BF16)
            _, sl = _silu_parts(gate)
            a = (sl * up).astype(BF16)
            a_ref[rows, :] = a
            contrib = _dot(a, wd_ref[...])
            f_acc[rows, :] = jnp.where(first, contrib, f_acc[rows, :] + contrib)

        @pl.when((i == 0) & (j == 0))
        def _():
            loss_ref[...] = jnp.zeros_like(loss_ref)
            gg4_ref[...] = jnp.zeros_like(gg4_ref)

        @pl.when(j == NSH - 1)
        def _():
            r4, n4 = _rms(f_acc[...])
            g4v = g4_ref[...]
            err = x1_ref[...] + n4 * g4v - t_ref[...]
            loss_ref[...] += (0.5 / D) * jnp.sum(err * err).reshape(1, 1)
            dy = err * (1.0 / D)
            dy_ref[...] = dy
            df, dg = _rms_bwd(r4, n4, g4v, dy)
            gg4_ref[...] += dg
            df_ref[...] = df.astype(BF16)

    row = lambda w: pl.BlockSpec((tm, w), lambda i, j: (i, 0))
    sh = lambda r, c: pl.BlockSpec((None, r, c), lambda i, j: (j, 0, 0))
    act = pl.BlockSpec((None, tm, FS), lambda i, j: (j, i, 0))
    vec = pl.BlockSpec((1, D), lambda i, j: (0, 0))
    return pl.pallas_call(
        body, name="ffn_fwd", grid=(s // tm, NSH),
        in_specs=[row(D), sh(FS, D), sh(FS, D), sh(FS, D), row(D), row(D), vec],
        out_specs=[act, act, act, row(D), row(D), pl.BlockSpec((1, 1), lambda i, j: (0, 0)), vec],
        out_shape=[jax.ShapeDtypeStruct((NSH, s, FS), BF16)] * 3
        + [jax.ShapeDtypeStruct((s, D), BF16), jax.ShapeDtypeStruct((s, D), F32),
           jax.ShapeDtypeStruct((1, 1), F32), jax.ShapeDtypeStruct((1, D), F32)],
        scratch_shapes=[pltpu.VMEM((tm, D), F32)],
        compiler_params=_cp(2))(h2, wg, wu, wd, x1, target, g4)


def _ffn_bwd_act(df, gate, up, wg, wu, wd, dy, x1, mix, g3, g2):
    s = df.shape[0]
    tm = min(TM_FFN, s)

    def body(df_ref, gate_ref, up_ref, wg_ref, wu_ref, wd_ref, dy_ref, x1_ref, mix_ref, g3_ref, g2_ref,
             dgate_ref, dup_ref, dx1_ref, dmix_ref, gg3_ref, gg2_ref, acc):
        j = pl.program_id(0)
        i = pl.program_id(1)
        first = j == 0
        tile = pl.multiple_of(i * tm, tm)
        chunks = [pl.ds(r, min(FFN_ROWS, tm)) for r in range(0, tm, FFN_ROWS)]

        @pl.when((i == 0) & (j == 0))
        def _():
            gg3_ref[...] = jnp.zeros_like(gg3_ref)
            gg2_ref[...] = jnp.zeros_like(gg2_ref)

        def norms_backward(rows, dh2):
            r3, n3 = _rms(x1_ref[rows, :])
            dx1n, dg3 = _rms_bwd(r3, n3, g3_ref[...], dh2)
            dx1 = dy_ref[rows, :] + dx1n
            dx1_ref[rows, :] = dx1
            gg3_ref[...] += dg3
            r2, n2 = _rms(mix_ref[rows, :])
            dmix, dg2 = _rms_bwd(r2, n2, g2_ref[...], dx1)
            gg2_ref[...] += dg2
            dmix_ref[rows, :] = dmix.astype(BF16)

        def shard_pass(last):
            das = [_dot_nt(df_ref[chunks[0], :], wd_ref[...])]
            for k, rows in enumerate(chunks):
                if k + 1 < len(chunks):
                    das.append(_dot_nt(df_ref[chunks[k + 1], :], wd_ref[...]))
                da = das[k]
                g = gate_ref[rows, :].astype(F32)
                u = up_ref[rows, :].astype(F32)
                sg, sl = _silu_parts(g)
                dgate = (da * u * (sg * (1.0 + g * (1.0 - sg)))).astype(BF16)
                dup = (da * sl).astype(BF16)
                dgate_ref[rows, :] = dgate
                dup_ref[rows, :] = dup
                contrib = _dot(dgate, wg_ref[...]) + _dot(dup, wu_ref[...])
                acc_rows = pl.ds(tile + k * FFN_ROWS, rows.size)
                if last:
                    norms_backward(rows, acc[acc_rows, :] + contrib)
                else:
                    acc[acc_rows, :] = jnp.where(first, contrib, acc[acc_rows, :] + contrib)

        pl.when(j < NSH - 1)(functools.partial(shard_pass, False))
        pl.when(j == NSH - 1)(functools.partial(shard_pass, True))

    row = pl.BlockSpec((tm, D), lambda j, i: (i, 0))
    last = pl.BlockSpec((tm, D), lambda j, i: (i * (j // (NSH - 1)), 0))
    sh = pl.BlockSpec((None, FS, D), lambda j, i: (j, 0, 0))
    act = pl.BlockSpec((None, tm, FS), lambda j, i: (j, i, 0))
    vec = pl.BlockSpec((1, D), lambda j, i: (0, 0))
    return pl.pallas_call(
        body, name="ffn_bwd_act", grid=(NSH, s // tm),
        in_specs=[row, act, act, sh, sh, sh, last, last, last, vec, vec],
        out_specs=[act, act, last, last, vec, vec],
        out_shape=[jax.ShapeDtypeStruct((NSH, s, FS), BF16), jax.ShapeDtypeStruct((NSH, s, FS), BF16),
                   jax.ShapeDtypeStruct((s, D), F32), jax.ShapeDtypeStruct((s, D), BF16),
                   jax.ShapeDtypeStruct((1, D), F32), jax.ShapeDtypeStruct((1, D), F32)],
        scratch_shapes=[pltpu.VMEM((s, D), F32)],
        compiler_params=_cp(2))(df, gate, up, wg, wu, wd, dy, x1, mix, g3, g2)


SMEM_SPEC = pl.BlockSpec(memory_space=pltpu.SMEM)


def _emit_halves(acc_ref, row0, rows, c, own_ref, oth_ref):
    half = rows // 2
    own_ref[...] = acc_ref[pl.ds(row0 + pl.multiple_of(c * half, 8), half), :]
    oth_ref[...] = acc_ref[pl.ds(row0 + pl.multiple_of((1 - c) * half, 8), half), :].astype(BF16)


def _half_shapes(rows):
    return [jax.ShapeDtypeStruct((NSH, rows // 2, D), F32), jax.ShapeDtypeStruct((NSH, rows // 2, D), BF16)]


def _ffn_bwd_w(h2, act_a, dgate, dup, df, c_arr):
    s = h2.shape[0]
    tm = min(TM_FFN_FWD, s)
    nt = s // tm

    def body(c_ref, h_ref, a_ref, dgate_ref, dup_ref, df_ref, *rest):
        outs, accs = rest[:6], rest[6:]
        i = pl.program_id(1)

        @pl.when(i == 0)
        def _():
            for acc in accs:
                acc[...] = jnp.zeros_like(acc)

        h = h_ref[...]
        accs[0][...] += _dot_tn(dgate_ref[...], h)
        accs[1][...] += _dot_tn(dup_ref[...], h)
        accs[2][...] += _dot_tn(a_ref[...], df_ref[...])

        @pl.when(i == nt - 1)
        def _():
            for t, acc in enumerate(accs):
                _emit_halves(acc, 0, FS, c_ref[0], outs[2 * t], outs[2 * t + 1])

    row = lambda w: pl.BlockSpec((tm, w), lambda j, i: (i, 0))
    act = pl.BlockSpec((None, tm, FS), lambda j, i: (j, i, 0))
    half = pl.BlockSpec((None, FS // 2, D), lambda j, i: (j, 0, 0))
    return pl.pallas_call(
        body, name="ffn_bwd_w", grid=(NSH, nt),
        in_specs=[SMEM_SPEC, row(D), act, act, act, row(D)],
        out_specs=[half] * 6,
        out_shape=_half_shapes(FS) * 3,
        scratch_shapes=[pltpu.VMEM((FS, D), F32)] * 3,
        compiler_params=_cp(2))(c_arr, h2, act_a, dgate, dup, df)


def _outproj_bwd(dmix, w_out, pool_o, attn_o, c_arr, dep=None):
    s = dmix.shape[0]
    tm = min(TM, s)
    nt = s // tm

    def body(c_ref, dm_ref, w_ref, p_ref, a_ref, *rest):
        dpool_ref, dattn_ref, own_ref, oth_ref, gw_ref = rest[-5:]
        i = pl.program_id(0)

        @pl.when(i == 0)
        def _():
            gw_ref[...] = jnp.zeros_like(gw_ref)

        dm = dm_ref[...]
        dpool_ref[...] = _dot_nt(dm, w_ref[0:512, :])
        dattn_ref[...] = _dot_nt(dm, w_ref[512:1024, :]).astype(BF16)
        gw_ref[0:512, :] += _dot_tn(p_ref[...], dm)
        gw_ref[512:1024, :] += _dot_tn(a_ref[...], dm)

        @pl.when(i == nt - 1)
        def _():
            for k in range(NSH):
                _emit_halves(gw_ref, k * WOUT_S, WOUT_S, c_ref[0], own_ref.at[k], oth_ref.at[k])

    row = lambda w: pl.BlockSpec((tm, w), lambda i: (i, 0))
    full = pl.BlockSpec((D, D), lambda i: (0, 0))
    half = pl.BlockSpec((NSH, WOUT_S // 2, D), lambda i: (0, 0, 0))
    return pl.pallas_call(
        body, name="outproj_bwd", grid=(nt,),
        in_specs=[SMEM_SPEC, row(D), full, row(512), row(512)] + ([] if dep is None else [ANY]),
        out_specs=[row(512), row(512), half, half],
        out_shape=[jax.ShapeDtypeStruct((s, 512), F32), jax.ShapeDtypeStruct((s, 512), BF16)] + _half_shapes(WOUT_S),
        scratch_shapes=[pltpu.VMEM((D, D), F32)],
        compiler_params=_cp(1))(c_arr, dmix, w_out, pool_o, attn_o, *([] if dep is None else [dep]))


def _pool_bwd(pooled, dpool, w_pool, pool_scale, dep=None):
    s = pooled.shape[0]
    tm = min(TM_POOL, s)
    hb = tm // HALO
    nt = s // tm
    last_halo = s // HALO - 1

    def body(p_ref, dc_ref, dn_ref, wp_ref, sc_ref, *rest):
        du_ref, gwp_ref, gsc_ref = rest[-3:]
        i = pl.program_id(0)

        @pl.when(i == 0)
        def _():
            gwp_ref[...] = jnp.zeros_like(gwp_ref)
            gsc_ref[...] = jnp.zeros_like(gsc_ref)

        dcur = dc_ref[...]
        dnext = jnp.where(i < nt - 1, dn_ref[...], 0.0)
        dext = jnp.concatenate([dcur, dnext], axis=0)
        t_ext = i * tm + lax.broadcasted_iota(jnp.int32, (tm + HALO, GROUP), 0)
        for g, w in enumerate(WINDOWS):
            sl = slice(g * GROUP, (g + 1) * GROUP)
            pb = p_ref[:, sl]
            wp = wp_ref[g]
            mixed = _dot(pb, wp)
            gsc_ref[:, sl] += jnp.sum(dcur[:, sl] * mixed, axis=0, keepdims=True)
            dmixed = (dext[:, sl] * sc_ref[:, sl]).astype(BF16)
            gwp_ref[g] += _dot_tn(pb, dmixed[0:tm])
            dpooled = _dot_nt(dmixed, wp)
            z = dpooled / jnp.minimum(t_ext + 1, w).astype(F32)
            du_ref[:, sl] = (_window_sums(z, w, -1, tm, 2) - dpooled[0:tm]).astype(BF16)

    return pl.pallas_call(
        body, name="pool_bwd", grid=(nt,),
        in_specs=[pl.BlockSpec((tm, 512), lambda i: (i, 0)),
                  pl.BlockSpec((tm, 512), lambda i: (i, 0)),
                  pl.BlockSpec((HALO, 512), lambda i: (jnp.minimum((i + 1) * hb, last_halo), 0)),
                  pl.BlockSpec((4, GROUP, GROUP), lambda i: (0, 0, 0)),
                  pl.BlockSpec((1, 512), lambda i: (0, 0))] + ([] if dep is None else [ANY]),
        out_specs=[pl.BlockSpec((tm, 512), lambda i: (i, 0)),
                   pl.BlockSpec((4, GROUP, GROUP), lambda i: (0, 0, 0)),
                   pl.BlockSpec((1, 512), lambda i: (0, 0))],
        out_shape=[jax.ShapeDtypeStruct((s, 512), BF16), jax.ShapeDtypeStruct((4, GROUP, GROUP), F32),
                   jax.ShapeDtypeStruct((1, 512), F32)],
        compiler_params=_cp(1))(pooled, dpool, dpool, w_pool, pool_scale, *([] if dep is None else [dep]))


def _attn_bwd(q, k, v, do, probs, sink_share, bucket, dep=None):
    s = q.shape[0]
    nblk = s // BLK

    def body(q_ref, do_ref, k_ref, v_ref, prob_ref, ps_ref, bk_ref, *rest):
        dq_ref, dk_ref, dv_ref, grb_ref, gsk_ref, dss_ref = rest[-6:]
        n = pl.program_id(0)

        @pl.when(n == 0)
        def _():
            dk_ref[...] = jnp.zeros_like(dk_ref)
            dv_ref[...] = jnp.zeros_like(dv_ref)
            dss_ref[...] = jnp.zeros_like(dss_ref)
            gsk_ref[...] = jnp.zeros_like(gsk_ref)

        kt, (lo, pstart, cstart) = _kv_band(k_ref, n, True)
        vv, _ = _kv_band(v_ref, n, False)
        lo_q = lax.broadcasted_iota(jnp.int32, (BLK, 128), 1) < HD
        dkk = [jnp.zeros((2 * BLK, 128), F32), jnp.zeros((2 * BLK, 128), F32)]
        dvv = [jnp.zeros((2 * BLK, 128), F32), jnp.zeros((2 * BLK, 128), F32)]

        def by_head(t):
            return jnp.concatenate([jnp.where(lo_q, t, 0.0), jnp.where(lo_q, 0.0, t)], axis=0).astype(BF16)

        for p in range(4):
            h = p // 2
            qt = q_ref[:, p * 128:(p + 1) * 128].astype(F32) * SCALE
            dot = do_ref[:, p * 128:(p + 1) * 128]
            pb = prob_ref[pl.ds(2 * p, 2)]
            prob = pb.astype(F32)
            ps = ps_ref[pl.ds(2 * p, 2), :].reshape(2, 1, BLK)
            dp = _dot_nt(vv[h], dot).reshape(2, 2 * BLK, BLK)
            delta = jnp.sum(prob * dp, axis=1, keepdims=True)
            ds = prob * (dp - delta)
            sink_part = ps * delta
            for e in range(2):
                gs = -jnp.sum(sink_part[e]).reshape(1, 1)
                gsk_ref[pl.ds(2 * p + e, 1), :] += jnp.broadcast_to(gs, (1, 128))
            dss_ref[pl.ds(2 * p, 2)] += ds
            dsb = ds.astype(BF16)
            dq_t = _dot(kt[h], dsb.reshape(4 * BLK, BLK))
            dq_ref[:, p * 128:(p + 1) * 128] = (dq_t.T * SCALE).astype(BF16)
            dkk[h] = dkk[h] + _dot(jnp.concatenate([dsb[0], dsb[1]], axis=1), by_head(qt))
            dvv[h] = dvv[h] + _dot(jnp.concatenate([pb[0], pb[1]], axis=1), by_head(dot.astype(F32)))
        for acc, ref in ((dkk, dk_ref), (dvv, dv_ref)):
            f0 = acc[0] + pltpu.roll(acc[0], HD, axis=1)
            f1 = acc[1] + pltpu.roll(acc[1], HD, axis=1)
            band = jnp.where(lo, f0, f1)
            ref[pl.ds(pstart, BLK), :] += band[0:BLK]
            ref[pl.ds(cstart, BLK), :] += band[BLK:2 * BLK]

        @pl.when(n == nblk - 1)
        def _():
            _relbias_grad(dss_ref, bk_ref[...], grb_ref)

    blk = pl.BlockSpec((BLK, 512), lambda i: (i, 0))
    kv = pl.BlockSpec((s, 128), lambda i: (0, 0))
    row8 = pl.BlockSpec((NQ, 128), lambda i: (0, 0))
    return pl.pallas_call(
        body, name="attn_bwd", grid=(nblk,),
        in_specs=[blk, blk, kv, kv, pl.BlockSpec((None, NQ, 2 * BLK, BLK), lambda i: (i, 0, 0, 0)),
                  pl.BlockSpec((None, NQ, BLK), lambda i: (i, 0, 0)), pl.BlockSpec((2 * BLK, BLK), lambda i: (0, 0))]
        + ([] if dep is None else [ANY]),
        out_specs=[blk, kv, kv, row8, row8],
        out_shape=[jax.ShapeDtypeStruct((s, 512), BF16), jax.ShapeDtypeStruct((s, 128), F32),
                   jax.ShapeDtypeStruct((s, 128), F32), jax.ShapeDtypeStruct((NQ, 128), F32),
                   jax.ShapeDtypeStruct((NQ, 128), F32)],
        scratch_shapes=[pltpu.VMEM((NQ, 2 * BLK, BLK), F32)],
        compiler_params=_cp(1))(q, do, k, v, probs, sink_share, bucket, *([] if dep is None else [dep]))


def _relbias_grad(ds_ref, bucket_v, o_ref):
    lane = lax.broadcasted_iota(jnp.int32, (NQ, 128), 1)
    head_row = lax.broadcasted_iota(jnp.int32, (NQ, BLK), 0)
    acc = jnp.zeros((NQ, 128), F32)
    for b in range(NBUCKET):
        sel = bucket_v == b
        stack = jnp.zeros((NQ, BLK), F32)
        for h in range(NQ):
            col_sums = jnp.sum(jnp.where(sel, ds_ref[h], 0.0), axis=0, keepdims=True)
            stack = jnp.where(head_row == h, col_sums, stack)
        acc = acc + jnp.where(lane == b, jnp.sum(stack, axis=1, keepdims=True), 0.0)
    o_ref[...] = acc


def _inproj_bwd(x, g1, du, dq, dk, dv, w_in, dx1, c_arr):
    s = x.shape[0]
    tm = min(TM, s)
    nt = s // tm
    cols = ((0, 512), (512, 1024), (1024, 1152), (1152, 1280))

    def body(c_ref, x_ref, g_ref, du_ref, dq_ref, dk_ref, dv_ref, w_ref, dx1_ref,
             gx_ref, own_ref, oth_ref, gg_ref, gw_ref):
        i = pl.program_id(0)

        @pl.when(i == 0)
        def _():
            gw_ref[...] = jnp.zeros_like(gw_ref)
            gg_ref[...] = jnp.zeros_like(gg_ref)

        parts = (du_ref[...], dq_ref[...], dk_ref[...].astype(BF16), dv_ref[...].astype(BF16))
        dh = None
        for (a, b), dpart in zip(cols, parts):
            t = _dot(dpart, w_ref[a:b, :])
            dh = t if dh is None else dh + t
        gv = g_ref[...]
        r1, n1 = _rms(x_ref[...])
        h = (n1 * gv).astype(BF16)
        for (a, b), dpart in zip(cols, parts):
            gw_ref[a:b, :] += _dot_tn(dpart, h)
        dx, dg = _rms_bwd(r1, n1, gv, dh)
        gg_ref[...] += dg
        gx_ref[...] = dx1_ref[...] + dx

        @pl.when(i == nt - 1)
        def _():
            for k in range(NSH):
                _emit_halves(gw_ref, k * WIN_S, WIN_S, c_ref[0], own_ref.at[k], oth_ref.at[k])

    row = lambda w: pl.BlockSpec((tm, w), lambda i: (i, 0))
    vec = pl.BlockSpec((1, D), lambda i: (0, 0))
    full = pl.BlockSpec((IN_W, D), lambda i: (0, 0))
    half = pl.BlockSpec((NSH, WIN_S // 2, D), lambda i: (0, 0, 0))
    return pl.pallas_call(
        body, name="inproj_bwd", grid=(nt,),
        in_specs=[SMEM_SPEC, row(D), vec, row(512), row(512), row(128), row(128), full, row(D)],
        out_specs=[row(D), half, half, vec],
        out_shape=[jax.ShapeDtypeStruct((s, D), F32)] + _half_shapes(WIN_S) + [jax.ShapeDtypeStruct((1, D), F32)],
        scratch_shapes=[pltpu.VMEM((IN_W, D), F32)],
        compiler_params=_cp(1))(c_arr, x, g1, du, dq, dk, dv, w_in, dx1)


def _local_step(x, target, small, w_in, w_out, ffn_weights, c_arr, on_ffn_grads=None, on_wout_grads=None,
                mid_outproj=None, start_dep=None):
    g1, g2, g3, g4, w_pool, pool_scale, rel_bias, sinks = small
    bucket = jnp.asarray(_bucket_table())
    wp_bf = w_pool.astype(BF16)
    bias = _bias_table(bucket, rel_bias)
    h1 = _prenorm(x, g1, start_dep)
    if callable(w_in):
        w_in = w_in(bias, h1)
    u, q, k, v = _inproj_fwd(h1, w_in)
    pool_o, pooled = _pool_fwd(u, wp_bf, pool_scale)
    attn_o, probs, sink_share = _attn_fwd(q, k, v, bias, sinks)
    g2_fwd = g2
    if callable(w_out):
        w_out, after = w_out(pool_o, attn_o)
        if after is not None:
            g2_fwd = g2 + after[:1, :1]
    mix, x1, h2 = _outproj_fwd(pool_o, attn_o, w_out, x, g2_fwd, g3, mid_outproj)
    wg, wu, wd = ffn_weights(h2) if callable(ffn_weights) else ffn_weights
    gate, up, act_a, df, dy, loss, gg4 = _ffn_fwd(h2, wg, wu, wd, x1, target, g4)
    dgate, dup, dx1, dmix, gg3, gg2 = _ffn_bwd_act(df, gate, up, wg, wu, wd, dy, x1, mix, g3, g2)
    ffn_g = _ffn_bwd_w(h2, act_a, dgate, dup, df, c_arr)
    dep = None if on_ffn_grads is None else on_ffn_grads(ffn_g)
    dpool, dattn, wout_own, wout_oth = _outproj_bwd(dmix, w_out, pool_o, attn_o, c_arr, dep)
    dep = None if on_wout_grads is None else on_wout_grads(wout_own, wout_oth, dpool)
    du, gwp, gsc = _pool_bwd(pooled, dpool, wp_bf, pool_scale, dep)
    dq, dk, dv, grb, gsk = _attn_bwd(q, k, v, dattn, probs, sink_share, bucket, dep)
    gx, win_own, win_oth, gg1 = _inproj_bwd(x, g1, du, dq, dk, dv, w_in, dx1, c_arr)
    small_grads = (gg1, gg2, gg3, gg4, gwp, gsc, grb, gsk[:, 0].reshape(1, NQ))
    return loss, gx, small_grads, ((win_own, win_oth), (wout_own, wout_oth), ffn_g)


def _place():
    x, y, c = lax.axis_index("x"), lax.axis_index("y"), lax.axis_index("c")
    chips = ((1 - x, y), (x, 1 - y), (1 - x, 1 - y))
    return x, y, c, chips


def _remote(src, dst, ssem, rsem, dev):
    return pltpu.make_async_remote_copy(src_ref=src, dst_ref=dst, send_sem=ssem, recv_sem=rsem,
                                        device_id=dev, device_id_type=MESH_T)


def _to_sibling(arrs, name, after=()):
    nt, na = len(arrs), len(after)

    def body(*refs):
        srcs, dsts = refs[:nt], refs[nt + na:2 * nt + na]
        ssem, rsem = refs[2 * nt + na:]
        x, y, c, _ = _place()
        cps = [_remote(srcs[t], dsts[t], ssem.at[t], rsem.at[t], (x, y, 1 - c)) for t in range(nt)]
        for cp in cps:
            cp.start()
        for cp in cps:
            cp.wait()

    return pl.pallas_call(
        body, name=name, in_specs=[ANY] * (nt + na), out_specs=[ANY] * nt,
        out_shape=[jax.ShapeDtypeStruct(a.shape, a.dtype) for a in arrs],
        scratch_shapes=[pltpu.SemaphoreType.DMA((nt,)), pltpu.SemaphoreType.DMA((nt,))],
        compiler_params=_cp())(*arrs, *after)


HBM_SPEC = pl.BlockSpec(memory_space=pltpu.HBM)
SEM_SPEC = pl.BlockSpec(memory_space=pltpu.SEMAPHORE)
DATAFLOW = pltpu.SideEffectType.DATAFLOW_SIDE_EFFECTING


def _cast_into_slots(ws, chip_arr):
    nt, tiles = len(ws), 4

    def body(chip_ref, *refs):
        for t in range(nt):
            refs[nt + t][...] = refs[t][...].astype(BF16)

    return pl.pallas_call(
        body, name="cast_weights",
        grid_spec=pltpu.PrefetchScalarGridSpec(
            num_scalar_prefetch=1, grid=(tiles,),
            in_specs=[pl.BlockSpec((w.shape[0] // tiles, w.shape[1]), lambda i, chip_ref: (i, 0)) for w in ws],
            out_specs=[pl.BlockSpec((None, w.shape[0] // tiles, w.shape[1]), lambda i, chip_ref: (chip_ref[0], i, 0))
                       for w in ws]),
        out_shape=[jax.ShapeDtypeStruct((NSH,) + w.shape, BF16) for w in ws],
        compiler_params=_cp(1))(chip_arr, *ws)


def _gather_copies(srcs, lands, ssem, rsem, first=0):
    x, y, c, chips = _place()
    me = 2 * x + y
    out = []
    for t in range(len(lands)):
        half = lands[t].shape[1] // 2
        mine = pl.ds(pl.multiple_of(c * half, 16), half)
        for j, (px, py) in enumerate(chips):
            k = 3 * (first + t) + j
            send = _remote(lands[t].at[me, mine], lands[t].at[me, mine], ssem.at[k], rsem.at[k], (px, py, c))
            blk = lands[t].at[2 * px + py, mine]
            out.append((send, _remote(blk, blk, ssem.at[k], rsem.at[k], (px, py, c))))
    return out


def _scatter_copies(srcs, lands, ssem, rsem):
    x, y, c, chips = _place()
    out = []
    for t in range(len(srcs)):
        for j, (px, py) in enumerate(chips):
            k = 3 * t + j
            send = _remote(srcs[t].at[2 * px + py], lands[t].at[j], ssem.at[k], rsem.at[k], (px, py, c))
            out.append((send, _remote(lands[t].at[j], lands[t].at[j], ssem.at[k], rsem.at[k], (px, py, c))))
    return out


def _sibling_copies(srcs, lands, ssem, rsem):
    x, y, c, _ = _place()
    sib = (x, y, 1 - c)
    return [(_remote(srcs[t], lands[t], ssem.at[t], rsem.at[t], sib),
             _remote(lands[t], lands[t], ssem.at[t], rsem.at[t], sib)) for t in range(len(srcs))]


def _peer_copies(srcs, lands, ssem, rsem):
    x, y, c, _ = _place()
    me = 4 * x + 2 * y + c
    out = []
    for kk in range(1, 8):
        px, py, pc = x ^ (kk >> 2), y ^ ((kk >> 1) & 1), c ^ (kk & 1)
        send = _remote(srcs[0], lands[0].at[me], ssem.at[kk - 1], rsem.at[kk - 1], (px, py, pc))
        slot = lands[0].at[4 * px + 2 * py + pc]
        out.append((send, _remote(slot, slot, ssem.at[kk - 1], rsem.at[kk - 1], (px, py, pc))))
    return out


def _forward_copies(srcs, lands, ssem, rsem):
    x, y, c, chips = _place()
    sib = (x, y, 1 - c)
    out = []
    for t in range(len(lands)):
        half = lands[t].shape[1] // 2
        mine = pl.ds(pl.multiple_of(c * half, 16), half)
        other = pl.ds(pl.multiple_of((1 - c) * half, 16), half)
        for j, (px, py) in enumerate(chips):
            k = 3 * t + j
            blk_m, blk_o = lands[t].at[2 * px + py, mine], lands[t].at[2 * px + py, other]
            out.append((_remote(blk_m, blk_m, ssem.at[k], rsem.at[k], sib),
                        _remote(blk_o, blk_o, ssem.at[k], rsem.at[k], sib)))
    return out


def _win_and_small_copies(srcs, lands, ssem, rsem):
    return (_scatter_copies(srcs[:1], lands[:1], ssem, rsem)
            + _peer_copies(srcs[1:], lands[1:], ssem.at[pl.ds(3, 7)], rsem.at[pl.ds(3, 7)]))


def _split_start(plan, ncopy, srcs, lands, after, name):
    ns, nbuf = len(srcs), len(srcs) + len(lands)
    extra = [] if after is None else [after]
    n_in = nbuf + len(extra)

    def body(*refs):
        ssem, rsem, token = refs[n_in], refs[n_in + 1], refs[-1]
        for send, _ in plan(refs[:ns], refs[ns:nbuf], ssem, rsem):
            send.start()
        token[...] = jnp.zeros_like(token)

    bufs = [pltpu.with_memory_space_constraint(a, pltpu.HBM) for a in list(srcs) + list(lands)]
    outs = pl.pallas_call(
        body, name=name, in_specs=[HBM_SPEC] * nbuf + [ANY] * len(extra),
        out_specs=[SEM_SPEC, SEM_SPEC] + [HBM_SPEC] * nbuf + [pl.BlockSpec(memory_space=pltpu.VMEM)],
        out_shape=[pltpu.SemaphoreType.DMA((ncopy,)), pltpu.SemaphoreType.DMA((ncopy,))]
        + [pltpu.HBM(a.shape, a.dtype) for a in bufs] + [jax.ShapeDtypeStruct((8, 128), F32)],
        input_output_aliases={i: 2 + i for i in range(nbuf)},
        compiler_params=pltpu.CompilerParams(has_side_effects=DATAFLOW))(*bufs, *extra)
    return outs[0], outs[1], outs[2:2 + ns], outs[2 + ns:2 + nbuf], outs[-1]


def _split_wait(plan, ssem, rsem, srcs, lands, after, name, only=None):
    ns, nbuf = len(srcs), len(srcs) + len(lands)

    def body(*refs):
        s_ref, r_ref = refs[nbuf], refs[nbuf + 1]
        for k, (send, recv) in enumerate(plan(refs[:ns], refs[ns:nbuf], s_ref, r_ref)):
            if only is None or k in only:
                send.wait_send()
                recv.wait_recv()

    outs = pl.pallas_call(
        body, name=name, in_specs=[HBM_SPEC] * nbuf + [SEM_SPEC, SEM_SPEC] + [ANY] * len(after),
        out_specs=[HBM_SPEC] * nbuf,
        out_shape=[pltpu.HBM(a.shape, a.dtype) for a in list(srcs) + list(lands)],
        input_output_aliases={i: i for i in range(nbuf)},
        compiler_params=pltpu.CompilerParams(has_side_effects=DATAFLOW))(*srcs, *lands, ssem, rsem, *after)
    return outs


def _gather_finish(lands, tag):
    nt = len(lands)

    def body(*refs):
        outs = refs[nt:2 * nt]
        dsend, drecv = refs[2 * nt:]
        x, y, c, chips = _place()
        sib = (x, y, 1 - c)
        sends = []
        for t in range(nt):
            half = outs[t].shape[1] // 2
            mine = pl.ds(pl.multiple_of(c * half, 16), half)
            for j, (px, py) in enumerate(chips):
                blk = outs[t].at[2 * px + py, mine]
                cp = _remote(blk, blk, dsend.at[t, j], drecv.at[t, j], sib)
                cp.start()
                sends.append(cp)
        for t in range(nt):
            half = outs[t].shape[1] // 2
            other = pl.ds(pl.multiple_of((1 - c) * half, 16), half)
            for j, (px, py) in enumerate(chips):
                blk = outs[t].at[2 * px + py, other]
                _remote(blk, blk, dsend.at[t, j], drecv.at[t, j], sib).wait_recv()
        for cp in sends:
            cp.wait_send()

    return pl.pallas_call(
        body, name="gather_finish_" + tag, in_specs=[ANY] * nt, out_specs=[ANY] * nt,
        out_shape=[jax.ShapeDtypeStruct(a.shape, a.dtype) for a in lands],
        input_output_aliases={t: t for t in range(nt)},
        scratch_shapes=[pltpu.SemaphoreType.DMA((nt, 3)), pltpu.SemaphoreType.DMA((nt, 3))],
        compiler_params=_cp())(*lands)


def _chip_partial(owns, recv, chip_arr, tag, whole=False):
    nt = len(owns)

    if whole:
        def body_whole(chip_ref, *refs):
            for t in range(nt):
                refs[2 * nt + t][...] = (refs[t][...] + refs[nt + t][...].astype(F32)).astype(BF16)
                mine = chip_ref[0]
                refs[3 * nt + t][...] = refs[t][mine] + refs[nt + t][mine].astype(F32)

        vm = pl.BlockSpec(memory_space=pltpu.VMEM)
        return pl.pallas_call(
            body_whole, name="rs_chip_partial_" + tag, in_specs=[SMEM_SPEC] + [vm] * (2 * nt), out_specs=[vm] * (2 * nt),
            out_shape=[jax.ShapeDtypeStruct(a.shape, BF16) for a in owns]
            + [jax.ShapeDtypeStruct(a.shape[1:], F32) for a in owns],
            compiler_params=_cp())(chip_arr, *owns, *recv)

    def body(chip_ref, *refs):
        k = pl.program_id(0)
        for t in range(nt):
            p = refs[t][...] + refs[nt + t][...].astype(F32)
            refs[2 * nt + t][...] = p.astype(BF16)

            @pl.when(k == chip_ref[0])
            def _():
                refs[3 * nt + t][...] = p

    blk_specs = [pl.BlockSpec((None,) + a.shape[1:], lambda k, chip_ref: (k, 0, 0)) for a in owns]
    own_specs = [pl.BlockSpec(a.shape[1:], lambda k, chip_ref: (0, 0)) for a in owns]
    return pl.pallas_call(
        body, name="rs_chip_partial_" + tag,
        grid_spec=pltpu.PrefetchScalarGridSpec(num_scalar_prefetch=1, grid=(NSH,), in_specs=blk_specs * 2,
                                               out_specs=blk_specs + own_specs),
        out_shape=[jax.ShapeDtypeStruct(a.shape, BF16) for a in owns]
        + [jax.ShapeDtypeStruct(a.shape[1:], F32) for a in owns],
        compiler_params=_cp(1))(chip_arr, *owns, *recv)


def _sum_chips(own, recv, tag):
    nt = len(own)

    def body(*refs):
        outs = refs[2 * nt:]
        for t in range(nt):
            r = refs[nt + t]
            outs[t][...] = ((refs[t][...] + r[0].astype(F32)) + r[1].astype(F32)) + r[2].astype(F32)

    vm = pl.BlockSpec(memory_space=pltpu.VMEM)
    return pl.pallas_call(
        body, name="rs_sum_chips_" + tag, in_specs=[vm] * (2 * nt), out_specs=[vm] * nt,
        out_shape=[jax.ShapeDtypeStruct(a.shape, F32) for a in own],
        compiler_params=_cp())(*own, *recv)


def _sum_chips_shared(own, recv, tag):
    def body(own_ref, recv_ref, out_ref, sib_ref, ssem, rsem):
        out_ref[...] = ((own_ref[...] + recv_ref[0].astype(F32)) + recv_ref[1].astype(F32)) + recv_ref[2].astype(F32)
        x, y, c, _ = _place()
        cp = _remote(out_ref, sib_ref, ssem.at[0], rsem.at[0], (x, y, 1 - c))
        cp.start()
        cp.wait()

    vm = pl.BlockSpec(memory_space=pltpu.VMEM)
    return pl.pallas_call(
        body, name="rs_sum_share_" + tag, in_specs=[vm, vm], out_specs=[vm, ANY],
        out_shape=[jax.ShapeDtypeStruct(own.shape, F32)] * 2,
        scratch_shapes=[pltpu.SemaphoreType.DMA((1,)), pltpu.SemaphoreType.DMA((1,))],
        compiler_params=_cp())(own, recv)


def _adamw_math(w, g, m, v):
    m = ADAM_B1 * m + (1.0 - ADAM_B1) * g
    v = ADAM_B2 * v + (1.0 - ADAM_B2) * (g * g)
    m_hat = m / (1.0 - ADAM_B1 ** ADAM_STEP)
    v_hat = v / (1.0 - ADAM_B2 ** ADAM_STEP)
    delta = -ADAM_LR * (m_hat / (jnp.sqrt(v_hat) + ADAM_EPS) + ADAM_WD * w)
    return delta, m, v


ADAM_SPLIT = 4


def _adamw_shards(own, sib, ws, ms, vs, c_arr, tag):
    nt = len(own)

    def body(c_ref, *refs):
        hh = pl.program_id(0)
        mine = hh == c_ref[0]
        for t in range(nt):
            g = jnp.where(mine, refs[t][...], refs[nt + t][...])
            delta, m, v = _adamw_math(refs[2 * nt + t][...], g, refs[3 * nt + t][...], refs[4 * nt + t][...])
            refs[5 * nt + t][...] = g
            refs[6 * nt + t][...] = delta
            refs[7 * nt + t][...] = m
            refs[8 * nt + t][...] = v

    def tile(a):
        return (a.shape[0] // ADAM_SPLIT, a.shape[1])

    def half_index(used_when_mine):
        def index(hh, q, c_ref):
            mine = 1 - (hh - c_ref[0]) * (hh - c_ref[0])
            used = mine if used_when_mine else 1 - mine
            return (used * q + (1 - used) * hh * (ADAM_SPLIT - 1), 0)
        return index

    own_specs = [pl.BlockSpec(tile(a), half_index(True)) for a in own]
    sib_specs = [pl.BlockSpec(tile(a), half_index(False)) for a in own]
    full_specs = [pl.BlockSpec(tile(a), lambda hh, q, c_ref: (hh * ADAM_SPLIT + q, 0)) for a in own]
    return pl.pallas_call(
        body, name="adamw_shards_" + tag,
        grid_spec=pltpu.PrefetchScalarGridSpec(num_scalar_prefetch=1, grid=(2, ADAM_SPLIT),
                                               in_specs=own_specs + sib_specs + full_specs * 3,
                                               out_specs=full_specs * 4),
        out_shape=[jax.ShapeDtypeStruct(w.shape, F32) for w in ws] * 4,
        compiler_params=_cp(2))(c_arr, *own, *sib, *ws, *ms, *vs)


SMALL_WPOOL_ROW = 32
SMALL_SCALE_ROW = SMALL_WPOOL_ROW + 4 * GROUP
SMALL_BIAS_ROW = SMALL_SCALE_ROW + 8
SMALL_SINKS_ROW = SMALL_BIAS_ROW + NQ
SMALL_LOSS_ROW = SMALL_SINKS_ROW + 8


def _small_sum_adamw(gathered, wp, mp, vp):
    rows = wp.shape[0]

    def body(g_ref, w_ref, m_ref, v_ref, *rest):
        outs, res = rest[:-1], rest[-1]
        g = g_ref[0]
        for d in range(1, 8):
            g = g + g_ref[d]
        delta, m, v = _adamw_math(w_ref[...], g, m_ref[...], v_ref[...])
        for kind, val in enumerate((g, delta, m, v)):
            res[kind] = val
            gains = outs[8 * kind:8 * kind + 4]
            w_pool_o, scale_o, bias_o, sinks_o = outs[8 * kind + 4:8 * kind + 8]
            for t in range(4):
                for i in range(8):
                    gains[t][:, 128 * i:128 * (i + 1)] = res[kind, pl.ds(8 * t + i, 1), :]
            for grp in range(4):
                w_pool_o[grp] = res[kind, pl.ds(SMALL_WPOOL_ROW + GROUP * grp, GROUP), :]
            for i in range(4):
                scale_o[:, 128 * i:128 * (i + 1)] = res[kind, pl.ds(SMALL_SCALE_ROW + i, 1), :]
            bias_o[...] = res[kind, pl.ds(SMALL_BIAS_ROW, NQ), :][:, 0:NBUCKET]
            sinks_o[...] = res[kind, pl.ds(SMALL_SINKS_ROW, 1), :][:, 0:NQ]
        outs[-1][...] = res[0, pl.ds(SMALL_LOSS_ROW, 1), :]

    vm = pl.BlockSpec(memory_space=pltpu.VMEM)
    one_kind = [jax.ShapeDtypeStruct((1, D), F32)] * 4 + [
        jax.ShapeDtypeStruct((4, GROUP, GROUP), F32), jax.ShapeDtypeStruct((1, POOL_W), F32),
        jax.ShapeDtypeStruct((NQ, NBUCKET), F32), jax.ShapeDtypeStruct((1, NQ), F32)]
    return pl.pallas_call(
        body, name="small_sum_adamw", in_specs=[vm] * 4, out_specs=[vm] * 33,
        out_shape=one_kind * 4 + [jax.ShapeDtypeStruct((1, 128), F32)],
        scratch_shapes=[pltpu.VMEM((4, rows, 128), F32)],
        compiler_params=_cp())(gathered, wp, mp, vp)


def _pack_small(gains4, w_pool, pool_scale, rel_bias_t, sinks, loss):
    bias_rows = jnp.pad(rel_bias_t, ((0, 0), (0, 128 - rel_bias_t.shape[1])))
    parts = list(gains4) + [w_pool, pool_scale, bias_rows, sinks, loss]
    rows = []
    for a in parts:
        flat = a.reshape(-1)
        n = flat.shape[0]
        padded = -(-n // 1024) * 1024
        rows.append(jnp.pad(flat, (0, padded - n)).reshape(-1, 128))
    return jnp.concatenate(rows, axis=0)


def kernel(x, g_pre_mix, w_in, w_pool, pool_scale, rel_bias, sinks, w_out, g_post_mix, g_pre_ffn, w_gate, w_up, w_down, g_post_ffn, loss_target, m_g_pre_mix, m_w_in, m_w_pool, m_pool_scale, m_rel_bias, m_sinks, m_w_out, m_g_post_mix, m_g_pre_ffn, m_w_gate, m_w_up, m_w_down, m_g_post_ffn, v_g_pre_mix, v_w_in, v_w_pool, v_pool_scale, v_rel_bias, v_sinks, v_w_out, v_g_post_mix, v_g_pre_ffn, v_w_gate, v_w_up, v_w_down, v_g_post_ffn):
    c = lax.axis_index("c")
    chip = 2 * lax.axis_index("x") + lax.axis_index("y")

    def shards(a_in, a_out, a_gate, a_up, a_down):
        return (a_in[0].T, a_out[0], a_gate[0].T, a_up[0].T, a_down[0])

    c_arr = c.reshape(1).astype(jnp.int32)
    chip_arr = chip.reshape(1).astype(jnp.int32)

    big_w = shards(w_in, w_out, w_gate, w_up, w_down)
    g_ssem, g_rsem, _, g_lands, g_token = _split_start(
        _gather_copies, 15, [], _cast_into_slots(big_w, chip_arr), None, "gather_start")
    fw = {}

    def w_in_ready(*after):
        fw["first"] = _split_wait(_gather_copies, g_ssem, g_rsem, [], g_lands, after, "gather_win_wait",
                                  only=(0, 1, 2))
        (fw["win"],) = _gather_finish([fw["first"][0]], "win")
        return fw["win"].reshape(IN_W, D)

    def w_out_ready(*after):
        fw["second"] = _split_wait(functools.partial(_gather_copies, first=1), g_ssem, g_rsem, [],
                                   list(fw["first"][1:]), after, "gather_wout_wait", only=(0, 1, 2))
        (fw["wout"],) = _gather_finish([fw["second"][0]], "wout")
        return fw["wout"].reshape(D, D), None

    def ffn_forward_start(after):
        outs = _split_wait(functools.partial(_gather_copies, first=2), g_ssem, g_rsem, [], list(fw["second"][1:]),
                           [after], "gather_ffn_wait")
        fw["f"] = _split_start(_forward_copies, 9, [], list(outs), None, "gather_forward_start")
        return fw["f"][4]

    def ffn_weights(after):
        ssem, rsem, _, lands, _ = fw["f"]
        return _split_wait(_forward_copies, ssem, rsem, [], lands, [after], "gather_forward_wait")

    rs = {}

    def on_ffn_grads(ffn_g):
        oths = ffn_g[1::2]
        rs["ffn_own"] = ffn_g[0::2]
        rs["x"] = _split_start(_sibling_copies, 3, oths, [lax.empty(a.shape, BF16) for a in oths], ffn_g[0],
                               "rs_exchange_ffn_start")
        return rs["x"][4]

    def on_wout_grads(own, oth, after):
        ssem, rsem, srcs, lands, _ = rs["x"]
        recv_ffn = _split_wait(_sibling_copies, ssem, rsem, srcs, lands, [after], "rs_exchange_ffn_wait")[3:]
        recv_wout = _to_sibling([oth], "rs_exchange_wout")
        outs = _chip_partial([own] + list(rs["ffn_own"]), list(recv_wout) + list(recv_ffn), chip_arr, "early")
        rs["early_own"] = outs[4:]
        rs["s"] = _split_start(_scatter_copies, 12, outs[:4],
                               [lax.empty((3,) + a.shape[1:], BF16) for a in outs[:4]], outs[4], "scatter_early_start")
        return rs["s"][4]

    small = (g_pre_mix, g_post_mix, g_pre_ffn, g_post_ffn, w_pool[0], pool_scale, rel_bias, sinks)
    loss, gx, small_grads, ((win_own, win_oth), _, _) = _local_step(
        x[0], loss_target[0], small, w_in_ready, w_out_ready, ffn_weights, c_arr, on_ffn_grads, on_wout_grads,
        ffn_forward_start, g_token)

    ssem, rsem, srcs, lands, _ = rs["s"]
    recv_early = _split_wait(_scatter_copies, ssem, rsem, srcs, lands, [gx], "scatter_early_wait")[4:]
    finals = _sum_chips(list(rs["early_own"]), list(recv_early), "early")
    to_sib = [win_oth] + list(finals)
    sh_ssem, sh_rsem, sh_srcs, sh_lands, _ = _split_start(
        _sibling_copies, 5, to_sib, [lax.empty(a.shape, a.dtype) for a in to_sib], None, "rs_share_start")

    unused = jnp.zeros((1, 1), F32)
    gp = _pack_small(small_grads[:4], *small_grads[4:], loss)
    wp = _pack_small((g_pre_mix, g_post_mix, g_pre_ffn, g_post_ffn), w_pool, pool_scale, rel_bias.T, sinks, unused)
    mp = _pack_small((m_g_pre_mix, m_g_post_mix, m_g_pre_ffn, m_g_post_ffn), m_w_pool, m_pool_scale, m_rel_bias.T,
                     m_sinks, unused)
    vp = _pack_small((v_g_pre_mix, v_g_post_mix, v_g_pre_ffn, v_g_post_ffn), v_w_pool, v_pool_scale, v_rel_bias.T,
                     v_sinks, unused)

    moments = (shards(m_w_in, m_w_out, m_w_gate, m_w_up, m_w_down),
               shards(v_w_in, v_w_out, v_w_gate, v_w_up, v_w_down))
    sh_first = _split_wait(_sibling_copies, sh_ssem, sh_rsem, sh_srcs, sh_lands, [], "rs_share_win_wait", only=(0,))
    outs = _chip_partial([win_own], [sh_first[5]], chip_arr, "win", whole=True)
    slots = lax.dynamic_update_slice(lax.empty((8,) + gp.shape, F32), gp[None], (2 * chip + c, 0, 0))
    w_ssem, w_rsem, w_srcs, w_lands, w_token = _split_start(
        _win_and_small_copies, 10, [outs[0], gp], [lax.empty((3,) + outs[0].shape[1:], BF16), slots], gx,
        "scatter_win_start")
    shared = _split_wait(_sibling_copies, sh_ssem, sh_rsem, sh_first[:5], sh_first[5:], [w_token],
                         "rs_share_early_wait", only=(1, 2, 3, 4))
    adam_e = _adamw_shards(shared[1:5], shared[6:], big_w[1:], moments[0][1:], moments[1][1:], c_arr, "early")
    w_first = _split_wait(_win_and_small_copies, w_ssem, w_rsem, w_srcs, w_lands, [adam_e[0]], "scatter_win_wait",
                          only=(0, 1, 2))
    win_final, win_sib = _sum_chips_shared(outs[1], w_first[2], "win")
    adam_w = _adamw_shards([win_final], [win_sib], big_w[:1], moments[0][:1], moments[1][:1], c_arr, "win")
    big_g, big_d, big_m, big_v = [[adam_w[i]] + list(adam_e[4 * i:4 * i + 4]) for i in range(4)]

    gathered = _split_wait(_win_and_small_copies, w_ssem, w_rsem, w_first[:2], w_first[2:], [adam_w[0]],
                           "small_allgather_wait", only=tuple(range(3, 10)))[3]
    flat = _small_sum_adamw(gathered, wp, mp, vp)
    small_out = [flat[8 * kind:8 * kind + 8] for kind in range(4)]
    total_loss = flat[-1][0, 0]

    def ordered(small8, big4):
        sg1, sg2, sg3, sg4, swp, ssc, srb, ssk = small8
        swp, srb = swp[None], srb.T
        bwin, bwout, bwg, bwu, bwd = big4[0].T[None], big4[1][None], big4[2].T[None], big4[3].T[None], big4[4][None]
        return [sg1, bwin, swp, ssc, srb, ssk, bwout, sg2, sg3, bwg, bwu, bwd, sg4]

    res = [total_loss, gx[None]]
    for sm, bg in zip(small_out, (big_g, big_d, big_m, big_v)):
        res += ordered(sm, bg)
    return tuple(res)
```

```python
import functools

import numpy as np
import jax
import jax.numpy as jnp
from jax import lax
from jax.experimental import pallas as pl
from jax.experimental.pallas import tpu as pltpu

F32 = jnp.float32
BF16 = jnp.bfloat16

D = 1024
POOL_W = 512
GROUP = 128
WINDOWS = (2, 4, 8, 16)
HALO = 128
NQ = 8
BLK = 128
NBUCKET = 32
IN_W = 1280
DFF = 2816
NSH = 4
FS = DFF // NSH
WIN_S = IN_W // NSH
WOUT_S = D // NSH
EPS = 1e-6
NEG = -1e30
SCALE = 0.125

ADAM_LR = 0.001
ADAM_B1 = 0.9
ADAM_B2 = 0.999
ADAM_EPS = 1e-08
ADAM_WD = 0.01
ADAM_STEP = 10

TM = 512
TM_IN = 1024
TM_POOL = 1024
TM_FFN = 512
TM_FFN_FWD = 1024
FFN_ROWS = 256
VMEM_LIMIT = 60 * 1024 * 1024

MESH_T = pl.DeviceIdType.MESH
ANY = pl.BlockSpec(memory_space=pl.ANY)


def _cp(n_grid=0, **kw):
    sem = ("arbitrary",) * n_grid if n_grid else None
    return pltpu.CompilerParams(dimension_semantics=sem, vmem_limit_bytes=VMEM_LIMIT, **kw)


def _dot(a, b):
    return jnp.dot(a, b, preferred_element_type=F32)


def _dot_nt(a, b):
    return lax.dot_general(a, b, (((1,), (1,)), ((), ())), preferred_element_type=F32)


def _dot_tn(a, b):
    return lax.dot_general(a, b, (((0,), (0,)), ((), ())), preferred_element_type=F32)


def _rms(x):
    r = lax.rsqrt(jnp.mean(x * x, axis=-1, keepdims=True) + EPS)
    return r, x * r


def _rms_bwd(r, n, g, dout):
    dn = dout * g
    dx = r * (dn - n * jnp.mean(dn * n, axis=-1, keepdims=True))
    dg = jnp.sum(dout * n, axis=0, keepdims=True)
    return dx, dg


def _split(x, n):
    parts = []
    r = x
    for _ in range(n):
        p = r.astype(BF16)
        parts.append(p)
        r = r - p.astype(F32)
    return parts


def _band_dot(a, x, n):
    acc = None
    for p in _split(x, n):
        t = _dot(a, p)
        acc = t if acc is None else acc + t
    return acc


def _bucket_table():
    qi = np.arange(BLK)[None, :]
    kj = np.arange(2 * BLK)[:, None]
    dist = qi + BLK - kj
    n = np.maximum(dist, 0)
    nf = np.maximum(n, 1).astype(np.float32)
    large = 16 + (np.log(nf / np.float32(16)) / np.float32(np.log(128 / 16)) * np.float32(16)).astype(np.int32)
    large = np.minimum(large, NBUCKET - 1)
    return np.where(n < 16, n, large).astype(np.int32)


def _prenorm(x, g1, dep=None):
    s = x.shape[0]
    tm = min(TM_IN, s)

    def body(x_ref, g_ref, *rest):
        _, n = _rms(x_ref[...])
        rest[-1][...] = (n * g_ref[...]).astype(BF16)

    row = pl.BlockSpec((tm, D), lambda i: (i, 0))
    return pl.pallas_call(
        body, name="prenorm", grid=(s // tm,),
        in_specs=[row, pl.BlockSpec((1, D), lambda i: (0, 0))] + ([] if dep is None else [ANY]), out_specs=row,
        out_shape=jax.ShapeDtypeStruct((s, D), BF16),
        compiler_params=_cp(1))(x, g1, *([] if dep is None else [dep]))


def _inproj_fwd(h1, w_in):
    s = h1.shape[0]
    tm = min(TM_IN, s)

    def body(h_ref, w_ref, u_ref, q_ref, k_ref, v_ref):
        proj = _dot_nt(h_ref[...], w_ref[...])
        u_ref[...] = proj[:, :512]
        q_ref[...] = proj[:, 512:1024].astype(BF16)
        k_ref[...] = proj[:, 1024:1152].astype(BF16)
        v_ref[...] = proj[:, 1152:1280].astype(BF16)

    row = lambda w: pl.BlockSpec((tm, w), lambda i: (i, 0))
    return pl.pallas_call(
        body, name="inproj_fwd", grid=(s // tm,),
        in_specs=[row(D), pl.BlockSpec((IN_W, D), lambda i: (0, 0))],
        out_specs=[row(512), row(512), row(128), row(128)],
        out_shape=[jax.ShapeDtypeStruct((s, 512), F32), jax.ShapeDtypeStruct((s, 512), BF16),
                   jax.ShapeDtypeStruct((s, 128), BF16), jax.ShapeDtypeStruct((s, 128), BF16)],
        compiler_params=_cp(1))(h1, w_in)


def _window_sums(ext, w, lag_sign, tm, terms):
    rows = lax.broadcasted_iota(jnp.int32, (HALO, 2 * HALO), 0)
    cols = lax.broadcasted_iota(jnp.int32, (HALO, 2 * HALO), 1)
    d = rows + HALO - cols if lag_sign > 0 else cols - rows
    band = jnp.where((d >= 0) & (d < w), 1.0, 0.0).astype(BF16)
    return jnp.concatenate([_band_dot(band, ext[r:r + 2 * HALO], terms) for r in range(0, tm, HALO)], axis=0)


def _pooled(ext, cur, t0, tm):
    t = t0 + lax.broadcasted_iota(jnp.int32, (tm, GROUP), 0)
    out = []
    for g, w in enumerate(WINDOWS):
        sl = slice(g * GROUP, (g + 1) * GROUP)
        cnt = jnp.minimum(t + 1, w).astype(F32)
        out.append(_window_sums(ext[:, sl], w, 1, tm, 2) / cnt - cur[:, sl])
    return out


def _pool_fwd(u, w_pool, pool_scale):
    s = u.shape[0]
    tm = min(TM_POOL, s)
    hb = tm // HALO

    def body(uc_ref, uh_ref, wp_ref, sc_ref, o_ref, pooled_ref):
        i = pl.program_id(0)
        cur = uc_ref[...]
        halo = jnp.where(i > 0, uh_ref[...], 0.0)
        ext = jnp.concatenate([halo, cur], axis=0)
        pooled = _pooled(ext, cur, i * tm, tm)
        for g in range(4):
            sl = slice(g * GROUP, (g + 1) * GROUP)
            pb = pooled[g].astype(BF16)
            pooled_ref[:, sl] = pb
            o_ref[:, sl] = (_dot(pb, wp_ref[g]) * sc_ref[:, sl]).astype(BF16)

    row = pl.BlockSpec((tm, 512), lambda i: (i, 0))
    return pl.pallas_call(
        body, name="pool_fwd", grid=(s // tm,),
        in_specs=[row, pl.BlockSpec((HALO, 512), lambda i: (jnp.maximum(i * hb - 1, 0), 0)),
                  pl.BlockSpec((4, GROUP, GROUP), lambda i: (0, 0, 0)),
                  pl.BlockSpec((1, 512), lambda i: (0, 0))],
        out_specs=[row, row],
        out_shape=[jax.ShapeDtypeStruct((s, 512), BF16)] * 2,
        compiler_params=_cp(1))(u, u, w_pool, pool_scale)


def _bias_table(bucket, rel_bias):
    def body(b_ref, rb_ref, o_ref):
        bucket_v = b_ref[...]
        key = lax.broadcasted_iota(jnp.int32, (2 * BLK, BLK), 0)
        dist = lax.broadcasted_iota(jnp.int32, (2 * BLK, BLK), 1) + BLK - key
        inwin = (dist >= 0) & (dist < BLK)
        for h in range(NQ):
            acc = jnp.zeros((2 * BLK, BLK), F32)
            for b in range(NBUCKET):
                acc = jnp.where(bucket_v == b, rb_ref[b, h], acc)
            rest = jnp.where(inwin, acc, NEG)
            o_ref[1, h] = rest
            o_ref[0, h] = jnp.where(key >= BLK, rest, NEG)

    return pl.pallas_call(
        body, name="bias_table",
        in_specs=[pl.BlockSpec(memory_space=pltpu.VMEM), pl.BlockSpec(memory_space=pltpu.SMEM)],
        out_specs=pl.BlockSpec(memory_space=pltpu.VMEM),
        out_shape=jax.ShapeDtypeStruct((2, NQ, 2 * BLK, BLK), F32),
        compiler_params=_cp())(bucket, rel_bias)


HD = 64


def _kv_band(ref, n, transposed):
    pstart = pl.multiple_of(jnp.maximum(n - 1, 0) * BLK, BLK)
    cstart = pl.multiple_of(n * BLK, BLK)
    lo = lax.broadcasted_iota(jnp.int32, (2 * BLK, 128), 1) < HD
    band = jnp.concatenate([ref[pl.ds(pstart, BLK), :], ref[pl.ds(cstart, BLK), :]], axis=0).astype(F32)
    rot = pltpu.roll(band, HD, axis=1)
    halves = ((jnp.where(lo, band, 0.0), jnp.where(lo, 0.0, rot)), (jnp.where(lo, rot, 0.0), jnp.where(lo, 0.0, band)))
    if transposed:
        out = [jnp.concatenate([a.T, b.T], axis=1).astype(BF16) for a, b in halves]
    else:
        out = [jnp.concatenate([a, b], axis=0).astype(BF16) for a, b in halves]
    return out, (lo, pstart, cstart)


def _pair_probs(qt, kk, bias, sk_ref, p):
    sink = jnp.concatenate([jnp.full((1, 1, BLK), sk_ref[0, 2 * p + e], F32) for e in range(2)], axis=0)
    s = _dot_nt(kk, qt).reshape(2, 2 * BLK, BLK) + bias
    m = jnp.maximum(jnp.max(s, axis=1, keepdims=True), sink)
    pr = jnp.exp(s - m)
    es = jnp.exp(sink - m)
    inv = 1.0 / (jnp.sum(pr, axis=1, keepdims=True) + es)
    return pr * inv, es * inv


def _attn_fwd(q, k, v, bias, sinks):
    s = q.shape[0]
    nblk = s // BLK

    def body(q_ref, k_ref, v_ref, b_ref, sk_ref, o_ref, prob_ref, ps_ref):
        n = pl.program_id(0)
        kk, _ = _kv_band(k_ref, n, False)
        vt, _ = _kv_band(v_ref, n, True)
        bidx = jnp.minimum(n, 1)
        for p in range(4):
            h = p // 2
            qt = (q_ref[:, p * 128:(p + 1) * 128].astype(F32) * SCALE).astype(BF16)
            prob, ps = _pair_probs(qt, kk[h], b_ref[bidx, pl.ds(2 * p, 2)], sk_ref, p)
            pb = prob.astype(BF16)
            prob_ref[pl.ds(2 * p, 2)] = pb
            ps_ref[pl.ds(2 * p, 2), :] = ps.reshape(2, BLK)
            acc_t = _dot(vt[h], pb.reshape(4 * BLK, BLK))
            o_ref[:, p * 128:(p + 1) * 128] = acc_t.T.astype(BF16)

    return pl.pallas_call(
        body, name="attn_fwd", grid=(nblk,),
        in_specs=[pl.BlockSpec((BLK, 512), lambda i: (i, 0)),
                  pl.BlockSpec((s, 128), lambda i: (0, 0)),
                  pl.BlockSpec((s, 128), lambda i: (0, 0)),
                  pl.BlockSpec((2, NQ, 2 * BLK, BLK), lambda i: (0, 0, 0, 0)),
                  pl.BlockSpec(memory_space=pltpu.SMEM)],
        out_specs=[pl.BlockSpec((BLK, 512), lambda i: (i, 0)),
                   pl.BlockSpec((None, NQ, 2 * BLK, BLK), lambda i: (i, 0, 0, 0)),
                   pl.BlockSpec((None, NQ, BLK), lambda i: (i, 0, 0))],
        out_shape=[jax.ShapeDtypeStruct((s, 512), BF16), jax.ShapeDtypeStruct((nblk, NQ, 2 * BLK, BLK), BF16),
                   jax.ShapeDtypeStruct((nblk, NQ, BLK), F32)],
        compiler_params=_cp(1))(q, k, v, bias, sinks)


def _outproj_fwd(pool_o, attn_o, w_out, x, g2, g3, between=None):
    s = x.shape[0]
    tm = min(TM, s)
    nt = s // tm

    def part(name, tile0, tiles, filled, dep):
        def body(p_ref, a_ref, w_ref, x_ref, g2_ref, g3_ref, *rest):
            mix_ref, x1_ref, h2_ref = rest[-3:]
            mix = _dot(p_ref[...], w_ref[0:512, :]) + _dot(a_ref[...], w_ref[512:1024, :])
            mix_ref[...] = mix
            _, n2 = _rms(mix)
            x1 = x_ref[...] + n2 * g2_ref[...]
            x1_ref[...] = x1
            _, n3 = _rms(x1)
            h2_ref[...] = (n3 * g3_ref[...]).astype(BF16)

        row = lambda w: pl.BlockSpec((tm, w), lambda i: (i + tile0, 0))
        vec = pl.BlockSpec((1, D), lambda i: (0, 0))
        extra = list(filled) + ([] if dep is None else [dep])
        return pl.pallas_call(
            body, name=name, grid=(tiles,),
            in_specs=[row(512), row(512), pl.BlockSpec((D, D), lambda i: (0, 0)), row(D), vec, vec]
            + [ANY] * len(extra),
            out_specs=[row(D), row(D), row(D)],
            out_shape=[jax.ShapeDtypeStruct((s, D), F32), jax.ShapeDtypeStruct((s, D), F32),
                       jax.ShapeDtypeStruct((s, D), BF16)],
            input_output_aliases={6 + t: t for t in range(len(filled))},
            compiler_params=_cp(1))(pool_o, attn_o, w_out, x, g2, g3, *extra)

    if between is None or nt < 2:
        return part("outproj_fwd", 0, nt, (), None)
    head = max(1, 5 * nt // 8)
    first = part("outproj_fwd_a", 0, head, (), None)
    return part("outproj_fwd_b", head, nt - head, first, between(first[2]))


def _silu_parts(g):
    sg = jax.nn.sigmoid(g)
    return sg, g * sg


def _ffn_fwd(h2, wg, wu, wd, x1, target, g4):
    s = h2.shape[0]
    tm = min(TM_FFN_FWD, s)

    def body(h_ref, wg_ref, wu_ref, wd_ref, x1_ref, t_ref, g4_ref,
             gate_ref, up_ref, a_ref, df_ref, dy_ref, loss_ref, gg4_ref, f_acc):
        i = pl.program_id(0)
        j = pl.program_id(1)
        first = j == 0
        chunks = [pl.ds(r, min(FFN_ROWS, tm)) for r in range(0, tm, FFN_ROWS)]
        project = lambda rows: (_dot_nt(h_ref[rows, :], wg_ref[...]), _dot_nt(h_ref[rows, :], wu_ref[...]))
        ahead = [project(chunks[0])]
        for k, rows in enumerate(chunks):
            if k + 1 < len(chunks):
                ahead.append(project(chunks[k + 1]))
            gate, up = ahead[k]
            gate_ref[rows, :] = gate.astype(BF16)
            up_ref[rows, :] = up.astype(BF16)
            _, sl = _silu_parts(gate)
            a = (sl * up).astype(BF16)
            a_ref[rows, :] = a
            contrib = _dot(a, wd_ref[...])
            f_acc[rows, :] = jnp.where(first, contrib, f_acc[rows, :] + contrib)

        @pl.when((i == 0) & (j == 0))
        def _():
            loss_ref[...] = jnp.zeros_like(loss_ref)
            gg4_ref[...] = jnp.zeros_like(gg4_ref)

        @pl.when(j == NSH - 1)
        def _():
            r4, n4 = _rms(f_acc[...])
            g4v = g4_ref[...]
            err = x1_ref[...] + n4 * g4v - t_ref[...]
            loss_ref[...] += (0.5 / D) * jnp.sum(err * err).reshape(1, 1)
            dy = err * (1.0 / D)
            dy_ref[...] = dy
            df, dg = _rms_bwd(r4, n4, g4v, dy)
            gg4_ref[...] += dg
            df_ref[...] = df.astype(BF16)

    row = lambda w: pl.BlockSpec((tm, w), lambda i, j: (i, 0))
    sh = lambda r, c: pl.BlockSpec((None, r, c), lambda i, j: (j, 0, 0))
    act = pl.BlockSpec((None, tm, FS), lambda i, j: (j, i, 0))
    vec = pl.BlockSpec((1, D), lambda i, j: (0, 0))
    return pl.pallas_call(
        body, name="ffn_fwd", grid=(s // tm, NSH),
        in_specs=[row(D), sh(FS, D), sh(FS, D), sh(FS, D), row(D), row(D), vec],
        out_specs=[act, act, act, row(D), row(D), pl.BlockSpec((1, 1), lambda i, j: (0, 0)), vec],
        out_shape=[jax.ShapeDtypeStruct((NSH, s, FS), BF16)] * 3
        + [jax.ShapeDtypeStruct((s, D), BF16), jax.ShapeDtypeStruct((s, D), F32),
           jax.ShapeDtypeStruct((1, 1), F32), jax.ShapeDtypeStruct((1, D), F32)],
        scratch_shapes=[pltpu.VMEM((tm, D), F32)],
        compiler_params=_cp(2))(h2, wg, wu, wd, x1, target, g4)


def _ffn_bwd_act(df, gate, up, wg, wu, wd, dy, x1, mix, g3, g2):
    s = df.shape[0]
    tm = min(TM_FFN, s)

    def body(df_ref, gate_ref, up_ref, wg_ref, wu_ref, wd_ref, dy_ref, x1_ref, mix_ref, g3_ref, g2_ref,
             dgate_ref, dup_ref, dx1_ref, dmix_ref, gg3_ref, gg2_ref, acc):
        j = pl.program_id(0)
        i = pl.program_id(1)
        first = j == 0
        tile = pl.multiple_of(i * tm, tm)
        chunks = [pl.ds(r, min(FFN_ROWS, tm)) for r in range(0, tm, FFN_ROWS)]

        @pl.when((i == 0) & (j == 0))
        def _():
            gg3_ref[...] = jnp.zeros_like(gg3_ref)
            gg2_ref[...] = jnp.zeros_like(gg2_ref)

        def norms_backward(rows, dh2):
            r3, n3 = _rms(x1_ref[rows, :])
            dx1n, dg3 = _rms_bwd(r3, n3, g3_ref[...], dh2)
            dx1 = dy_ref[rows, :] + dx1n
            dx1_ref[rows, :] = dx1
            gg3_ref[...] += dg3
            r2, n2 = _rms(mix_ref[rows, :])
            dmix, dg2 = _rms_bwd(r2, n2, g2_ref[...], dx1)
            gg2_ref[...] += dg2
            dmix_ref[rows, :] = dmix.astype(BF16)

        def shard_pass(last):
            das = [_dot_nt(df_ref[chunks[0], :], wd_ref[...])]
            for k, rows in enumerate(chunks):
                if k + 1 < len(chunks):
                    das.append(_dot_nt(df_ref[chunks[k + 1], :], wd_ref[...]))
                da = das[k]
                g = gate_ref[rows, :].astype(F32)
                u = up_ref[rows, :].astype(F32)
                sg, sl = _silu_parts(g)
                dgate = (da * u * (sg * (1.0 + g * (1.0 - sg)))).astype(BF16)
                dup = (da * sl).astype(BF16)
                dgate_ref[rows, :] = dgate
                dup_ref[rows, :] = dup
                contrib = _dot(dgate, wg_ref[...]) + _dot(dup, wu_ref[...])
                acc_rows = pl.ds(tile + k * FFN_ROWS, rows.size)
                if last:
                    norms_backward(rows, acc[acc_rows, :] + contrib)
                else:
                    acc[acc_rows, :] = jnp.where(first, contrib, acc[acc_rows, :] + contrib)

        pl.when(j < NSH - 1)(functools.partial(shard_pass, False))
        pl.when(j == NSH - 1)(functools.partial(shard_pass, True))

    row = pl.BlockSpec((tm, D), lambda j, i: (i, 0))
    last = pl.BlockSpec((tm, D), lambda j, i: (i * (j // (NSH - 1)), 0))
    sh = pl.BlockSpec((None, FS, D), lambda j, i: (j, 0, 0))
    act = pl.BlockSpec((None, tm, FS), lambda j, i: (j, i, 0))
    vec = pl.BlockSpec((1, D), lambda j, i: (0, 0))
    return pl.pallas_call(
        body, name="ffn_bwd_act", grid=(NSH, s // tm),
        in_specs=[row, act, act, sh, sh, sh, last, last, last, vec, vec],
        out_specs=[act, act, last, last, vec, vec],
        out_shape=[jax.ShapeDtypeStruct((NSH, s, FS), BF16), jax.ShapeDtypeStruct((NSH, s, FS), BF16),
                   jax.ShapeDtypeStruct((s, D), F32), jax.ShapeDtypeStruct((s, D), BF16),
                   jax.ShapeDtypeStruct((1, D), F32), jax.ShapeDtypeStruct((1, D), F32)],
        scratch_shapes=[pltpu.VMEM((s, D), F32)],
        compiler_params=_cp(2))(df, gate, up, wg, wu, wd, dy, x1, mix, g3, g2)


SMEM_SPEC = pl.BlockSpec(memory_space=pltpu.SMEM)


def _emit_halves(acc_ref, row0, rows, c, own_ref, oth_ref):
    half = rows // 2
    own_ref[...] = acc_ref[pl.ds(row0 + pl.multiple_of(c * half, 8), half), :]
    oth_ref[...] = acc_ref[pl.ds(row0 + pl.multiple_of((1 - c) * half, 8), half), :].astype(BF16)


def _half_shapes(rows):
    return [jax.ShapeDtypeStruct((NSH, rows // 2, D), F32), jax.ShapeDtypeStruct((NSH, rows // 2, D), BF16)]


def _ffn_bwd_w(h2, act_a, dgate, dup, df, c_arr):
    s = h2.shape[0]
    tm = min(2 * TM_FFN_FWD, s)
    nt = s // tm

    def body(c_ref, h_ref, a_ref, dgate_ref, dup_ref, df_ref, *rest):
        outs, accs = rest[:6], rest[6:]
        i = pl.program_id(1)

        @pl.when(i == 0)
        def _():
            for acc in accs:
                acc[...] = jnp.zeros_like(acc)

        h = h_ref[...]
        accs[0][...] += _dot_tn(dgate_ref[...], h)
        accs[1][...] += _dot_tn(dup_ref[...], h)
        accs[2][...] += _dot_tn(a_ref[...], df_ref[...])

        @pl.when(i == nt - 1)
        def _():
            for t, acc in enumerate(accs):
                _emit_halves(acc, 0, FS, c_ref[0], outs[2 * t], outs[2 * t + 1])

    row = lambda w: pl.BlockSpec((tm, w), lambda j, i: (i, 0))
    act = pl.BlockSpec((None, tm, FS), lambda j, i: (j, i, 0))
    half = pl.BlockSpec((None, FS // 2, D), lambda j, i: (j, 0, 0))
    return pl.pallas_call(
        body, name="ffn_bwd_w", grid=(NSH, nt),
        in_specs=[SMEM_SPEC, row(D), act, act, act, row(D)],
        out_specs=[half] * 6,
        out_shape=_half_shapes(FS) * 3,
        scratch_shapes=[pltpu.VMEM((FS, D), F32)] * 3,
        compiler_params=_cp(2))(c_arr, h2, act_a, dgate, dup, df)


def _outproj_bwd(dmix, w_out, pool_o, attn_o, c_arr, dep=None):
    s = dmix.shape[0]
    tm = min(TM, s)
    nt = s // tm

    def body(c_ref, dm_ref, w_ref, p_ref, a_ref, *rest):
        dpool_ref, dattn_ref, own_ref, oth_ref, gw_ref = rest[-5:]
        i = pl.program_id(0)

        @pl.when(i == 0)
        def _():
            gw_ref[...] = jnp.zeros_like(gw_ref)

        dm = dm_ref[...]
        dpool_ref[...] = _dot_nt(dm, w_ref[0:512, :])
        dattn_ref[...] = _dot_nt(dm, w_ref[512:1024, :]).astype(BF16)
        gw_ref[0:512, :] += _dot_tn(p_ref[...], dm)
        gw_ref[512:1024, :] += _dot_tn(a_ref[...], dm)

        @pl.when(i == nt - 1)
        def _():
            for k in range(NSH):
                _emit_halves(gw_ref, k * WOUT_S, WOUT_S, c_ref[0], own_ref.at[k], oth_ref.at[k])

    row = lambda w: pl.BlockSpec((tm, w), lambda i: (i, 0))
    full = pl.BlockSpec((D, D), lambda i: (0, 0))
    half = pl.BlockSpec((NSH, WOUT_S // 2, D), lambda i: (0, 0, 0))
    return pl.pallas_call(
        body, name="outproj_bwd", grid=(nt,),
        in_specs=[SMEM_SPEC, row(D), full, row(512), row(512)] + ([] if dep is None else [ANY]),
        out_specs=[row(512), row(512), half, half],
        out_shape=[jax.ShapeDtypeStruct((s, 512), F32), jax.ShapeDtypeStruct((s, 512), BF16)] + _half_shapes(WOUT_S),
        scratch_shapes=[pltpu.VMEM((D, D), F32)],
        compiler_params=_cp(1))(c_arr, dmix, w_out, pool_o, attn_o, *([] if dep is None else [dep]))


def _pool_bwd(pooled, dpool, w_pool, pool_scale, dep=None):
    s = pooled.shape[0]
    tm = min(TM_POOL, s)
    hb = tm // HALO
    nt = s // tm
    last_halo = s // HALO - 1

    def body(p_ref, dc_ref, dn_ref, wp_ref, sc_ref, *rest):
        du_ref, gwp_ref, gsc_ref = rest[-3:]
        i = pl.program_id(0)

        @pl.when(i == 0)
        def _():
            gwp_ref[...] = jnp.zeros_like(gwp_ref)
            gsc_ref[...] = jnp.zeros_like(gsc_ref)

        dcur = dc_ref[...]
        dnext = jnp.where(i < nt - 1, dn_ref[...], 0.0)
        dext = jnp.concatenate([dcur, dnext], axis=0)
        t_ext = i * tm + lax.broadcasted_iota(jnp.int32, (tm + HALO, GROUP), 0)
        for g, w in enumerate(WINDOWS):
            sl = slice(g * GROUP, (g + 1) * GROUP)
            pb = p_ref[:, sl]
            wp = wp_ref[g]
            mixed = _dot(pb, wp)
            gsc_ref[:, sl] += jnp.sum(dcur[:, sl] * mixed, axis=0, keepdims=True)
            dmixed = (dext[:, sl] * sc_ref[:, sl]).astype(BF16)
            gwp_ref[g] += _dot_tn(pb, dmixed[0:tm])
            dpooled = _dot_nt(dmixed, wp)
            z = dpooled / jnp.minimum(t_ext + 1, w).astype(F32)
            du_ref[:, sl] = (_window_sums(z, w, -1, tm, 2) - dpooled[0:tm]).astype(BF16)

    return pl.pallas_call(
        body, name="pool_bwd", grid=(nt,),
        in_specs=[pl.BlockSpec((tm, 512), lambda i: (i, 0)),
                  pl.BlockSpec((tm, 512), lambda i: (i, 0)),
                  pl.BlockSpec((HALO, 512), lambda i: (jnp.minimum((i + 1) * hb, last_halo), 0)),
                  pl.BlockSpec((4, GROUP, GROUP), lambda i: (0, 0, 0)),
                  pl.BlockSpec((1, 512), lambda i: (0, 0))] + ([] if dep is None else [ANY]),
        out_specs=[pl.BlockSpec((tm, 512), lambda i: (i, 0)),
                   pl.BlockSpec((4, GROUP, GROUP), lambda i: (0, 0, 0)),
                   pl.BlockSpec((1, 512), lambda i: (0, 0))],
        out_shape=[jax.ShapeDtypeStruct((s, 512), BF16), jax.ShapeDtypeStruct((4, GROUP, GROUP), F32),
                   jax.ShapeDtypeStruct((1, 512), F32)],
        compiler_params=_cp(1))(pooled, dpool, dpool, w_pool, pool_scale, *([] if dep is None else [dep]))


def _attn_bwd(q, k, v, do, probs, sink_share, bucket, dep=None):
    s = q.shape[0]
    nblk = s // BLK

    def body(q_ref, do_ref, k_ref, v_ref, prob_ref, ps_ref, bk_ref, *rest):
        dq_ref, dk_ref, dv_ref, grb_ref, gsk_ref, dss_ref = rest[-6:]
        n = pl.program_id(0)

        @pl.when(n == 0)
        def _():
            dk_ref[...] = jnp.zeros_like(dk_ref)
            dv_ref[...] = jnp.zeros_like(dv_ref)
            dss_ref[...] = jnp.zeros_like(dss_ref)
            gsk_ref[...] = jnp.zeros_like(gsk_ref)

        kt, (lo, pstart, cstart) = _kv_band(k_ref, n, True)
        vv, _ = _kv_band(v_ref, n, False)
        lo_q = lax.broadcasted_iota(jnp.int32, (BLK, 128), 1) < HD
        dkk = [jnp.zeros((2 * BLK, 128), F32), jnp.zeros((2 * BLK, 128), F32)]
        dvv = [jnp.zeros((2 * BLK, 128), F32), jnp.zeros((2 * BLK, 128), F32)]

        def by_head(t):
            return jnp.concatenate([jnp.where(lo_q, t, 0.0), jnp.where(lo_q, 0.0, t)], axis=0).astype(BF16)

        for p in range(4):
            h = p // 2
            qt = q_ref[:, p * 128:(p + 1) * 128].astype(F32) * SCALE
            dot = do_ref[:, p * 128:(p + 1) * 128]
            pb = prob_ref[pl.ds(2 * p, 2)]
            prob = pb.astype(F32)
            ps = ps_ref[pl.ds(2 * p, 2), :].reshape(2, 1, BLK)
            dp = _dot_nt(vv[h], dot).reshape(2, 2 * BLK, BLK)
            delta = jnp.sum(prob * dp, axis=1, keepdims=True)
            ds = prob * (dp - delta)
            sink_part = ps * delta
            for e in range(2):
                gs = -jnp.sum(sink_part[e]).reshape(1, 1)
                gsk_ref[pl.ds(2 * p + e, 1), :] += jnp.broadcast_to(gs, (1, 128))
            dss_ref[pl.ds(2 * p, 2)] += ds
            dsb = ds.astype(BF16)
            dq_t = _dot(kt[h], dsb.reshape(4 * BLK, BLK))
            dq_ref[:, p * 128:(p + 1) * 128] = (dq_t.T * SCALE).astype(BF16)
            dkk[h] = dkk[h] + _dot(jnp.concatenate([dsb[0], dsb[1]], axis=1), by_head(qt))
            dvv[h] = dvv[h] + _dot(jnp.concatenate([pb[0], pb[1]], axis=1), by_head(dot.astype(F32)))
        for acc, ref in ((dkk, dk_ref), (dvv, dv_ref)):
            f0 = acc[0] + pltpu.roll(acc[0], HD, axis=1)
            f1 = acc[1] + pltpu.roll(acc[1], HD, axis=1)
            band = jnp.where(lo, f0, f1)
            ref[pl.ds(pstart, BLK), :] += band[0:BLK]
            ref[pl.ds(cstart, BLK), :] += band[BLK:2 * BLK]

        @pl.when(n == nblk - 1)
        def _():
            _relbias_grad(dss_ref, bk_ref[...], grb_ref)

    blk = pl.BlockSpec((BLK, 512), lambda i: (i, 0))
    kv = pl.BlockSpec((s, 128), lambda i: (0, 0))
    row8 = pl.BlockSpec((NQ, 128), lambda i: (0, 0))
    return pl.pallas_call(
        body, name="attn_bwd", grid=(nblk,),
        in_specs=[blk, blk, kv, kv, pl.BlockSpec((None, NQ, 2 * BLK, BLK), lambda i: (i, 0, 0, 0)),
                  pl.BlockSpec((None, NQ, BLK), lambda i: (i, 0, 0)), pl.BlockSpec((2 * BLK, BLK), lambda i: (0, 0))]
        + ([] if dep is None else [ANY]),
        out_specs=[blk, kv, kv, row8, row8],
        out_shape=[jax.ShapeDtypeStruct((s, 512), BF16), jax.ShapeDtypeStruct((s, 128), F32),
                   jax.ShapeDtypeStruct((s, 128), F32), jax.ShapeDtypeStruct((NQ, 128), F32),
                   jax.ShapeDtypeStruct((NQ, 128), F32)],
        scratch_shapes=[pltpu.VMEM((NQ, 2 * BLK, BLK), F32)],
        compiler_params=_cp(1))(q, do, k, v, probs, sink_share, bucket, *([] if dep is None else [dep]))


def _relbias_grad(ds_ref, bucket_v, o_ref):
    lane = lax.broadcasted_iota(jnp.int32, (NQ, 128), 1)
    head_row = lax.broadcasted_iota(jnp.int32, (NQ, BLK), 0)
    acc = jnp.zeros((NQ, 128), F32)
    for b in range(NBUCKET):
        sel = bucket_v == b
        stack = jnp.zeros((NQ, BLK), F32)
        for h in range(NQ):
            col_sums = jnp.sum(jnp.where(sel, ds_ref[h], 0.0), axis=0, keepdims=True)
            stack = jnp.where(head_row == h, col_sums, stack)
        acc = acc + jnp.where(lane == b, jnp.sum(stack, axis=1, keepdims=True), 0.0)
    o_ref[...] = acc


def _inproj_bwd(x, g1, du, dq, dk, dv, w_in, dx1, c_arr):
    s = x.shape[0]
    tm = min(TM, s)
    nt = s // tm
    cols = ((0, 512), (512, 1024), (1024, 1152), (1152, 1280))

    def body(c_ref, x_ref, g_ref, du_ref, dq_ref, dk_ref, dv_ref, w_ref, dx1_ref,
             gx_ref, own_ref, oth_ref, gg_ref, gw_ref):
        i = pl.program_id(0)

        @pl.when(i == 0)
        def _():
            gw_ref[...] = jnp.zeros_like(gw_ref)
            gg_ref[...] = jnp.zeros_like(gg_ref)

        parts = (du_ref[...], dq_ref[...], dk_ref[...].astype(BF16), dv_ref[...].astype(BF16))
        dh = None
        for (a, b), dpart in zip(cols, parts):
            t = _dot(dpart, w_ref[a:b, :])
            dh = t if dh is None else dh + t
        gv = g_ref[...]
        r1, n1 = _rms(x_ref[...])
        h = (n1 * gv).astype(BF16)
        for (a, b), dpart in zip(cols, parts):
            gw_ref[a:b, :] += _dot_tn(dpart, h)
        dx, dg = _rms_bwd(r1, n1, gv, dh)
        gg_ref[...] += dg
        gx_ref[...] = dx1_ref[...] + dx

        @pl.when(i == nt - 1)
        def _():
            for k in range(NSH):
                _emit_halves(gw_ref, k * WIN_S, WIN_S, c_ref[0], own_ref.at[k], oth_ref.at[k])

    row = lambda w: pl.BlockSpec((tm, w), lambda i: (i, 0))
    vec = pl.BlockSpec((1, D), lambda i: (0, 0))
    full = pl.BlockSpec((IN_W, D), lambda i: (0, 0))
    half = pl.BlockSpec((NSH, WIN_S // 2, D), lambda i: (0, 0, 0))
    return pl.pallas_call(
        body, name="inproj_bwd", grid=(nt,),
        in_specs=[SMEM_SPEC, row(D), vec, row(512), row(512), row(128), row(128), full, row(D)],
        out_specs=[row(D), half, half, vec],
        out_shape=[jax.ShapeDtypeStruct((s, D), F32)] + _half_shapes(WIN_S) + [jax.ShapeDtypeStruct((1, D), F32)],
        scratch_shapes=[pltpu.VMEM((IN_W, D), F32)],
        compiler_params=_cp(1))(c_arr, x, g1, du, dq, dk, dv, w_in, dx1)


def _local_step(x, target, small, w_in, w_out, ffn_weights, c_arr, on_ffn_grads=None, on_wout_grads=None,
                mid_outproj=None, start_dep=None):
    g1, g2, g3, g4, w_pool, pool_scale, rel_bias, sinks = small
    bucket = jnp.asarray(_bucket_table())
    wp_bf = w_pool.astype(BF16)
    bias = _bias_table(bucket, rel_bias)
    h1 = _prenorm(x, g1, start_dep)
    if callable(w_in):
        w_in = w_in(bias, h1)
    u, q, k, v = _inproj_fwd(h1, w_in)
    pool_o, pooled = _pool_fwd(u, wp_bf, pool_scale)
    attn_o, probs, sink_share = _attn_fwd(q, k, v, bias, sinks)
    g2_fwd = g2
    if callable(w_out):
        w_out, after = w_out(pool_o, attn_o)
        if after is not None:
            g2_fwd = g2 + after[:1, :1]
    mix, x1, h2 = _outproj_fwd(pool_o, attn_o, w_out, x, g2_fwd, g3, mid_outproj)
    wg, wu, wd = ffn_weights(h2) if callable(ffn_weights) else ffn_weights
    gate, up, act_a, df, dy, loss, gg4 = _ffn_fwd(h2, wg, wu, wd, x1, target, g4)
    dgate, dup, dx1, dmix, gg3, gg2 = _ffn_bwd_act(df, gate, up, wg, wu, wd, dy, x1, mix, g3, g2)
    ffn_g = _ffn_bwd_w(h2, act_a, dgate, dup, df, c_arr)
    dep = None if on_ffn_grads is None else on_ffn_grads(ffn_g)
    dpool, dattn, wout_own, wout_oth = _outproj_bwd(dmix, w_out, pool_o, attn_o, c_arr, dep)
    dep = None if on_wout_grads is None else on_wout_grads(wout_own, wout_oth, dpool)
    du, gwp, gsc = _pool_bwd(pooled, dpool, wp_bf, pool_scale, dep)
    dq, dk, dv, grb, gsk = _attn_bwd(q, k, v, dattn, probs, sink_share, bucket, dep)
    gx, win_own, win_oth, gg1 = _inproj_bwd(x, g1, du, dq, dk, dv, w_in, dx1, c_arr)
    small_grads = (gg1, gg2, gg3, gg4, gwp, gsc, grb, gsk[:, 0].reshape(1, NQ))
    return loss, gx, small_grads, ((win_own, win_oth), (wout_own, wout_oth), ffn_g)


def _place():
    x, y, c = lax.axis_index("x"), lax.axis_index("y"), lax.axis_index("c")
    chips = ((1 - x, y), (x, 1 - y), (1 - x, 1 - y))
    return x, y, c, chips


def _remote(src, dst, ssem, rsem, dev):
    return pltpu.make_async_remote_copy(src_ref=src, dst_ref=dst, send_sem=ssem, recv_sem=rsem,
                                        device_id=dev, device_id_type=MESH_T)


def _to_sibling(arrs, name, after=()):
    nt, na = len(arrs), len(after)

    def body(*refs):
        srcs, dsts = refs[:nt], refs[nt + na:2 * nt + na]
        ssem, rsem = refs[2 * nt + na:]
        x, y, c, _ = _place()
        cps = [_remote(srcs[t], dsts[t], ssem.at[t], rsem.at[t], (x, y, 1 - c)) for t in range(nt)]
        for cp in cps:
            cp.start()
        for cp in cps:
            cp.wait()

    return pl.pallas_call(
        body, name=name, in_specs=[ANY] * (nt + na), out_specs=[ANY] * nt,
        out_shape=[jax.ShapeDtypeStruct(a.shape, a.dtype) for a in arrs],
        scratch_shapes=[pltpu.SemaphoreType.DMA((nt,)), pltpu.SemaphoreType.DMA((nt,))],
        compiler_params=_cp())(*arrs, *after)


HBM_SPEC = pl.BlockSpec(memory_space=pltpu.HBM)
SEM_SPEC = pl.BlockSpec(memory_space=pltpu.SEMAPHORE)
DATAFLOW = pltpu.SideEffectType.DATAFLOW_SIDE_EFFECTING


def _cast_into_slots(ws, chip_arr):
    nt, tiles = len(ws), 4

    def body(chip_ref, *refs):
        for t in range(nt):
            refs[nt + t][...] = refs[t][...].astype(BF16)

    return pl.pallas_call(
        body, name="cast_weights",
        grid_spec=pltpu.PrefetchScalarGridSpec(
            num_scalar_prefetch=1, grid=(tiles,),
            in_specs=[pl.BlockSpec((w.shape[0] // tiles, w.shape[1]), lambda i, chip_ref: (i, 0)) for w in ws],
            out_specs=[pl.BlockSpec((None, w.shape[0] // tiles, w.shape[1]), lambda i, chip_ref: (chip_ref[0], i, 0))
                       for w in ws]),
        out_shape=[jax.ShapeDtypeStruct((NSH,) + w.shape, BF16) for w in ws],
        compiler_params=_cp(1))(chip_arr, *ws)


def _gather_copies(srcs, lands, ssem, rsem, first=0):
    x, y, c, chips = _place()
    me = 2 * x + y
    out = []
    for t in range(len(lands)):
        half = lands[t].shape[1] // 2
        mine = pl.ds(pl.multiple_of(c * half, 16), half)
        for j, (px, py) in enumerate(chips):
            k = 3 * (first + t) + j
            send = _remote(lands[t].at[me, mine], lands[t].at[me, mine], ssem.at[k], rsem.at[k], (px, py, c))
            blk = lands[t].at[2 * px + py, mine]
            out.append((send, _remote(blk, blk, ssem.at[k], rsem.at[k], (px, py, c))))
    return out


def _scatter_copies(srcs, lands, ssem, rsem):
    x, y, c, chips = _place()
    out = []
    for t in range(len(srcs)):
        for j, (px, py) in enumerate(chips):
            k = 3 * t + j
            send = _remote(srcs[t].at[2 * px + py], lands[t].at[j], ssem.at[k], rsem.at[k], (px, py, c))
            out.append((send, _remote(lands[t].at[j], lands[t].at[j], ssem.at[k], rsem.at[k], (px, py, c))))
    return out


def _sibling_copies(srcs, lands, ssem, rsem):
    x, y, c, _ = _place()
    sib = (x, y, 1 - c)
    return [(_remote(srcs[t], lands[t], ssem.at[t], rsem.at[t], sib),
             _remote(lands[t], lands[t], ssem.at[t], rsem.at[t], sib)) for t in range(len(srcs))]


def _peer_copies(srcs, lands, ssem, rsem):
    x, y, c, _ = _place()
    me = 4 * x + 2 * y + c
    out = []
    for kk in range(1, 8):
        px, py, pc = x ^ (kk >> 2), y ^ ((kk >> 1) & 1), c ^ (kk & 1)
        send = _remote(srcs[0], lands[0].at[me], ssem.at[kk - 1], rsem.at[kk - 1], (px, py, pc))
        slot = lands[0].at[4 * px + 2 * py + pc]
        out.append((send, _remote(slot, slot, ssem.at[kk - 1], rsem.at[kk - 1], (px, py, pc))))
    return out


def _forward_copies(srcs, lands, ssem, rsem):
    x, y, c, chips = _place()
    sib = (x, y, 1 - c)
    out = []
    for t in range(len(lands)):
        half = lands[t].shape[1] // 2
        mine = pl.ds(pl.multiple_of(c * half, 16), half)
        other = pl.ds(pl.multiple_of((1 - c) * half, 16), half)
        for j, (px, py) in enumerate(chips):
            k = 3 * t + j
            blk_m, blk_o = lands[t].at[2 * px + py, mine], lands[t].at[2 * px + py, other]
            out.append((_remote(blk_m, blk_m, ssem.at[k], rsem.at[k], sib),
                        _remote(blk_o, blk_o, ssem.at[k], rsem.at[k], sib)))
    return out


def _win_and_small_copies(srcs, lands, ssem, rsem):
    return (_scatter_copies(srcs[:1], lands[:1], ssem, rsem)
            + _peer_copies(srcs[1:], lands[1:], ssem.at[pl.ds(3, 7)], rsem.at[pl.ds(3, 7)]))


def _split_start(plan, ncopy, srcs, lands, after, name):
    ns, nbuf = len(srcs), len(srcs) + len(lands)
    extra = [] if after is None else [after]
    n_in = nbuf + len(extra)

    def body(*refs):
        ssem, rsem, token = refs[n_in], refs[n_in + 1], refs[-1]
        for send, _ in plan(refs[:ns], refs[ns:nbuf], ssem, rsem):
            send.start()
        token[...] = jnp.zeros_like(token)

    bufs = [pltpu.with_memory_space_constraint(a, pltpu.HBM) for a in list(srcs) + list(lands)]
    outs = pl.pallas_call(
        body, name=name, in_specs=[HBM_SPEC] * nbuf + [ANY] * len(extra),
        out_specs=[SEM_SPEC, SEM_SPEC] + [HBM_SPEC] * nbuf + [pl.BlockSpec(memory_space=pltpu.VMEM)],
        out_shape=[pltpu.SemaphoreType.DMA((ncopy,)), pltpu.SemaphoreType.DMA((ncopy,))]
        + [pltpu.HBM(a.shape, a.dtype) for a in bufs] + [jax.ShapeDtypeStruct((8, 128), F32)],
        input_output_aliases={i: 2 + i for i in range(nbuf)},
        compiler_params=pltpu.CompilerParams(has_side_effects=DATAFLOW))(*bufs, *extra)
    return outs[0], outs[1], outs[2:2 + ns], outs[2 + ns:2 + nbuf], outs[-1]


def _split_wait(plan, ssem, rsem, srcs, lands, after, name, only=None):
    ns, nbuf = len(srcs), len(srcs) + len(lands)

    def body(*refs):
        s_ref, r_ref = refs[nbuf], refs[nbuf + 1]
        for k, (send, recv) in enumerate(plan(refs[:ns], refs[ns:nbuf], s_ref, r_ref)):
            if only is None or k in only:
                send.wait_send()
                recv.wait_recv()

    outs = pl.pallas_call(
        body, name=name, in_specs=[HBM_SPEC] * nbuf + [SEM_SPEC, SEM_SPEC] + [ANY] * len(after),
        out_specs=[HBM_SPEC] * nbuf,
        out_shape=[pltpu.HBM(a.shape, a.dtype) for a in list(srcs) + list(lands)],
        input_output_aliases={i: i for i in range(nbuf)},
        compiler_params=pltpu.CompilerParams(has_side_effects=DATAFLOW))(*srcs, *lands, ssem, rsem, *after)
    return outs


def _gather_finish(lands, tag):
    nt = len(lands)

    def body(*refs):
        outs = refs[nt:2 * nt]
        dsend, drecv = refs[2 * nt:]
        x, y, c, chips = _place()
        sib = (x, y, 1 - c)
        sends = []
        for t in range(nt):
            half = outs[t].shape[1] // 2
            mine = pl.ds(pl.multiple_of(c * half, 16), half)
            for j, (px, py) in enumerate(chips):
                blk = outs[t].at[2 * px + py, mine]
                cp = _remote(blk, blk, dsend.at[t, j], drecv.at[t, j], sib)
                cp.start()
                sends.append(cp)
        for t in range(nt):
            half = outs[t].shape[1] // 2
            other = pl.ds(pl.multiple_of((1 - c) * half, 16), half)
            for j, (px, py) in enumerate(chips):
                blk = outs[t].at[2 * px + py, other]
                _remote(blk, blk, dsend.at[t, j], drecv.at[t, j], sib).wait_recv()
        for cp in sends:
            cp.wait_send()

    return pl.pallas_call(
        body, name="gather_finish_" + tag, in_specs=[ANY] * nt, out_specs=[ANY] * nt,
        out_shape=[jax.ShapeDtypeStruct(a.shape, a.dtype) for a in lands],
        input_output_aliases={t: t for t in range(nt)},
        scratch_shapes=[pltpu.SemaphoreType.DMA((nt, 3)), pltpu.SemaphoreType.DMA((nt, 3))],
        compiler_params=_cp())(*lands)


def _chip_partial(owns, recv, chip_arr, tag, whole=False):
    nt = len(owns)

    if whole:
        def body_whole(chip_ref, *refs):
            for t in range(nt):
                refs[2 * nt + t][...] = (refs[t][...] + refs[nt + t][...].astype(F32)).astype(BF16)
                mine = chip_ref[0]
                refs[3 * nt + t][...] = refs[t][mine] + refs[nt + t][mine].astype(F32)

        vm = pl.BlockSpec(memory_space=pltpu.VMEM)
        return pl.pallas_call(
            body_whole, name="rs_chip_partial_" + tag, in_specs=[SMEM_SPEC] + [vm] * (2 * nt), out_specs=[vm] * (2 * nt),
            out_shape=[jax.ShapeDtypeStruct(a.shape, BF16) for a in owns]
            + [jax.ShapeDtypeStruct(a.shape[1:], F32) for a in owns],
            compiler_params=_cp())(chip_arr, *owns, *recv)

    def body(chip_ref, *refs):
        k = pl.program_id(0)
        for t in range(nt):
            p = refs[t][...] + refs[nt + t][...].astype(F32)
            refs[2 * nt + t][...] = p.astype(BF16)

            @pl.when(k == chip_ref[0])
            def _():
                refs[3 * nt + t][...] = p

    blk_specs = [pl.BlockSpec((None,) + a.shape[1:], lambda k, chip_ref: (k, 0, 0)) for a in owns]
    own_specs = [pl.BlockSpec(a.shape[1:], lambda k, chip_ref: (0, 0)) for a in owns]
    return pl.pallas_call(
        body, name="rs_chip_partial_" + tag,
        grid_spec=pltpu.PrefetchScalarGridSpec(num_scalar_prefetch=1, grid=(NSH,), in_specs=blk_specs * 2,
                                               out_specs=blk_specs + own_specs),
        out_shape=[jax.ShapeDtypeStruct(a.shape, BF16) for a in owns]
        + [jax.ShapeDtypeStruct(a.shape[1:], F32) for a in owns],
        compiler_params=_cp(1))(chip_arr, *owns, *recv)


def _sum_chips(own, recv, tag):
    nt = len(own)

    def body(*refs):
        outs = refs[2 * nt:]
        for t in range(nt):
            r = refs[nt + t]
            outs[t][...] = ((refs[t][...] + r[0].astype(F32)) + r[1].astype(F32)) + r[2].astype(F32)

    vm = pl.BlockSpec(memory_space=pltpu.VMEM)
    return pl.pallas_call(
        body, name="rs_sum_chips_" + tag, in_specs=[vm] * (2 * nt), out_specs=[vm] * nt,
        out_shape=[jax.ShapeDtypeStruct(a.shape, F32) for a in own],
        compiler_params=_cp())(*own, *recv)


def _sum_chips_shared(own, recv, tag):
    def body(own_ref, recv_ref, out_ref, sib_ref, ssem, rsem):
        out_ref[...] = ((own_ref[...] + recv_ref[0].astype(F32)) + recv_ref[1].astype(F32)) + recv_ref[2].astype(F32)
        x, y, c, _ = _place()
        cp = _remote(out_ref, sib_ref, ssem.at[0], rsem.at[0], (x, y, 1 - c))
        cp.start()
        cp.wait()

    vm = pl.BlockSpec(memory_space=pltpu.VMEM)
    return pl.pallas_call(
        body, name="rs_sum_share_" + tag, in_specs=[vm, vm], out_specs=[vm, ANY],
        out_shape=[jax.ShapeDtypeStruct(own.shape, F32)] * 2,
        scratch_shapes=[pltpu.SemaphoreType.DMA((1,)), pltpu.SemaphoreType.DMA((1,))],
        compiler_params=_cp())(own, recv)


def _adamw_math(w, g, m, v):
    m = ADAM_B1 * m + (1.0 - ADAM_B1) * g
    v = ADAM_B2 * v + (1.0 - ADAM_B2) * (g * g)
    m_hat = m / (1.0 - ADAM_B1 ** ADAM_STEP)
    v_hat = v / (1.0 - ADAM_B2 ** ADAM_STEP)
    delta = -ADAM_LR * (m_hat / (jnp.sqrt(v_hat) + ADAM_EPS) + ADAM_WD * w)
    return delta, m, v


ADAM_SPLIT = 4


def _adamw_shards(own, sib, ws, ms, vs, c_arr, tag):
    nt = len(own)

    def body(c_ref, *refs):
        hh = pl.program_id(0)
        mine = hh == c_ref[0]
        for t in range(nt):
            g = jnp.where(mine, refs[t][...], refs[nt + t][...])
            delta, m, v = _adamw_math(refs[2 * nt + t][...], g, refs[3 * nt + t][...], refs[4 * nt + t][...])
            refs[5 * nt + t][...] = g
            refs[6 * nt + t][...] = delta
            refs[7 * nt + t][...] = m
            refs[8 * nt + t][...] = v

    def tile(a):
        return (a.shape[0] // ADAM_SPLIT, a.shape[1])

    def half_index(used_when_mine):
        def index(hh, q, c_ref):
            mine = 1 - (hh - c_ref[0]) * (hh - c_ref[0])
            used = mine if used_when_mine else 1 - mine
            return (used * q + (1 - used) * hh * (ADAM_SPLIT - 1), 0)
        return index

    own_specs = [pl.BlockSpec(tile(a), half_index(True)) for a in own]
    sib_specs = [pl.BlockSpec(tile(a), half_index(False)) for a in own]
    full_specs = [pl.BlockSpec(tile(a), lambda hh, q, c_ref: (hh * ADAM_SPLIT + q, 0)) for a in own]
    return pl.pallas_call(
        body, name="adamw_shards_" + tag,
        grid_spec=pltpu.PrefetchScalarGridSpec(num_scalar_prefetch=1, grid=(2, ADAM_SPLIT),
                                               in_specs=own_specs + sib_specs + full_specs * 3,
                                               out_specs=full_specs * 4),
        out_shape=[jax.ShapeDtypeStruct(w.shape, F32) for w in ws] * 4,
        compiler_params=_cp(2))(c_arr, *own, *sib, *ws, *ms, *vs)


SMALL_WPOOL_ROW = 32
SMALL_SCALE_ROW = SMALL_WPOOL_ROW + 4 * GROUP
SMALL_BIAS_ROW = SMALL_SCALE_ROW + 8
SMALL_SINKS_ROW = SMALL_BIAS_ROW + NQ
SMALL_LOSS_ROW = SMALL_SINKS_ROW + 8


def _small_sum_adamw(gathered, wp, mp, vp):
    rows = wp.shape[0]

    def body(g_ref, w_ref, m_ref, v_ref, *rest):
        outs, res = rest[:-1], rest[-1]
        g = g_ref[0]
        for d in range(1, 8):
            g = g + g_ref[d]
        delta, m, v = _adamw_math(w_ref[...], g, m_ref[...], v_ref[...])
        for kind, val in enumerate((g, delta, m, v)):
            res[kind] = val
            gains = outs[8 * kind:8 * kind + 4]
            w_pool_o, scale_o, bias_o, sinks_o = outs[8 * kind + 4:8 * kind + 8]
            for t in range(4):
                for i in range(8):
                    gains[t][:, 128 * i:128 * (i + 1)] = res[kind, pl.ds(8 * t + i, 1), :]
            for grp in range(4):
                w_pool_o[grp] = res[kind, pl.ds(SMALL_WPOOL_ROW + GROUP * grp, GROUP), :]
            for i in range(4):
                scale_o[:, 128 * i:128 * (i + 1)] = res[kind, pl.ds(SMALL_SCALE_ROW + i, 1), :]
            bias_o[...] = res[kind, pl.ds(SMALL_BIAS_ROW, NQ), :][:, 0:NBUCKET]
            sinks_o[...] = res[kind, pl.ds(SMALL_SINKS_ROW, 1), :][:, 0:NQ]
        outs[-1][...] = res[0, pl.ds(SMALL_LOSS_ROW, 1), :]

    vm = pl.BlockSpec(memory_space=pltpu.VMEM)
    one_kind = [jax.ShapeDtypeStruct((1, D), F32)] * 4 + [
        jax.ShapeDtypeStruct((4, GROUP, GROUP), F32), jax.ShapeDtypeStruct((1, POOL_W), F32),
        jax.ShapeDtypeStruct((NQ, NBUCKET), F32), jax.ShapeDtypeStruct((1, NQ), F32)]
    return pl.pallas_call(
        body, name="small_sum_adamw", in_specs=[vm] * 4, out_specs=[vm] * 33,
        out_shape=one_kind * 4 + [jax.ShapeDtypeStruct((1, 128), F32)],
        scratch_shapes=[pltpu.VMEM((4, rows, 128), F32)],
        compiler_params=_cp())(gathered, wp, mp, vp)


def _pack_small(gains4, w_pool, pool_scale, rel_bias_t, sinks, loss):
    bias_rows = jnp.pad(rel_bias_t, ((0, 0), (0, 128 - rel_bias_t.shape[1])))
    parts = list(gains4) + [w_pool, pool_scale, bias_rows, sinks, loss]
    rows = []
    for a in parts:
        flat = a.reshape(-1)
        n = flat.shape[0]
        padded = -(-n // 1024) * 1024
        rows.append(jnp.pad(flat, (0, padded - n)).reshape(-1, 128))
    return jnp.concatenate(rows, axis=0)


def kernel(x, g_pre_mix, w_in, w_pool, pool_scale, rel_bias, sinks, w_out, g_post_mix, g_pre_ffn, w_gate, w_up, w_down, g_post_ffn, loss_target, m_g_pre_mix, m_w_in, m_w_pool, m_pool_scale, m_rel_bias, m_sinks, m_w_out, m_g_post_mix, m_g_pre_ffn, m_w_gate, m_w_up, m_w_down, m_g_post_ffn, v_g_pre_mix, v_w_in, v_w_pool, v_pool_scale, v_rel_bias, v_sinks, v_w_out, v_g_post_mix, v_g_pre_ffn, v_w_gate, v_w_up, v_w_down, v_g_post_ffn):
    c = lax.axis_index("c")
    chip = 2 * lax.axis_index("x") + lax.axis_index("y")

    def shards(a_in, a_out, a_gate, a_up, a_down):
        return (a_in[0].T, a_out[0], a_gate[0].T, a_up[0].T, a_down[0])

    c_arr = c.reshape(1).astype(jnp.int32)
    chip_arr = chip.reshape(1).astype(jnp.int32)

    big_w = shards(w_in, w_out, w_gate, w_up, w_down)
    g_ssem, g_rsem, _, g_lands, g_token = _split_start(
        _gather_copies, 15, [], _cast_into_slots(big_w, chip_arr), None, "gather_start")
    fw = {}

    def w_in_ready(*after):
        fw["first"] = _split_wait(_gather_copies, g_ssem, g_rsem, [], g_lands, after, "gather_win_wait",
                                  only=(0, 1, 2))
        (fw["win"],) = _gather_finish([fw["first"][0]], "win")
        return fw["win"].reshape(IN_W, D)

    def w_out_ready(*after):
        fw["second"] = _split_wait(functools.partial(_gather_copies, first=1), g_ssem, g_rsem, [],
                                   list(fw["first"][1:]), after, "gather_wout_wait", only=(0, 1, 2))
        (fw["wout"],) = _gather_finish([fw["second"][0]], "wout")
        return fw["wout"].reshape(D, D), None

    def ffn_forward_start(after):
        outs = _split_wait(functools.partial(_gather_copies, first=2), g_ssem, g_rsem, [], list(fw["second"][1:]),
                           [after], "gather_ffn_wait")
        fw["f"] = _split_start(_forward_copies, 9, [], list(outs), None, "gather_forward_start")
        return fw["f"][4]

    def ffn_weights(after):
        ssem, rsem, _, lands, _ = fw["f"]
        return _split_wait(_forward_copies, ssem, rsem, [], lands, [after], "gather_forward_wait")

    rs = {}

    def on_ffn_grads(ffn_g):
        oths = ffn_g[1::2]
        rs["ffn_own"] = ffn_g[0::2]
        rs["x"] = _split_start(_sibling_copies, 3, oths, [lax.empty(a.shape, BF16) for a in oths], ffn_g[0],
                               "rs_exchange_ffn_start")
        return rs["x"][4]

    def on_wout_grads(own, oth, after):
        ssem, rsem, srcs, lands, _ = rs["x"]
        recv_ffn = _split_wait(_sibling_copies, ssem, rsem, srcs, lands, [after], "rs_exchange_ffn_wait")[3:]
        recv_wout = _to_sibling([oth], "rs_exchange_wout")
        outs = _chip_partial([own] + list(rs["ffn_own"]), list(recv_wout) + list(recv_ffn), chip_arr, "early")
        rs["early_own"] = outs[4:]
        rs["s"] = _split_start(_scatter_copies, 12, outs[:4],
                               [lax.empty((3,) + a.shape[1:], BF16) for a in outs[:4]], outs[4], "scatter_early_start")
        return rs["s"][4]

    small = (g_pre_mix, g_post_mix, g_pre_ffn, g_post_ffn, w_pool[0], pool_scale, rel_bias, sinks)
    loss, gx, small_grads, ((win_own, win_oth), _, _) = _local_step(
        x[0], loss_target[0], small, w_in_ready, w_out_ready, ffn_weights, c_arr, on_ffn_grads, on_wout_grads,
        ffn_forward_start, g_token)

    ssem, rsem, srcs, lands, _ = rs["s"]
    recv_early = _split_wait(_scatter_copies, ssem, rsem, srcs, lands, [gx], "scatter_early_wait")[4:]
    finals = _sum_chips(list(rs["early_own"]), list(recv_early), "early")
    to_sib = [win_oth] + list(finals)
    sh_ssem, sh_rsem, sh_srcs, sh_lands, _ = _split_start(
        _sibling_copies, 5, to_sib, [lax.empty(a.shape, a.dtype) for a in to_sib], None, "rs_share_start")

    unused = jnp.zeros((1, 1), F32)
    gp = _pack_small(small_grads[:4], *small_grads[4:], loss)
    wp = _pack_small((g_pre_mix, g_post_mix, g_pre_ffn, g_post_ffn), w_pool, pool_scale, rel_bias.T, sinks, unused)
    mp = _pack_small((m_g_pre_mix, m_g_post_mix, m_g_pre_ffn, m_g_post_ffn), m_w_pool, m_pool_scale, m_rel_bias.T,
                     m_sinks, unused)
    vp = _pack_small((v_g_pre_mix, v_g_post_mix, v_g_pre_ffn, v_g_post_ffn), v_w_pool, v_pool_scale, v_rel_bias.T,
                     v_sinks, unused)

    moments = (shards(m_w_in, m_w_out, m_w_gate, m_w_up, m_w_down),
               shards(v_w_in, v_w_out, v_w_gate, v_w_up, v_w_down))
    sh_first = _split_wait(_sibling_copies, sh_ssem, sh_rsem, sh_srcs, sh_lands, [], "rs_share_win_wait", only=(0,))
    outs = _chip_partial([win_own], [sh_first[5]], chip_arr, "win", whole=True)
    slots = lax.dynamic_update_slice(lax.empty((8,) + gp.shape, F32), gp[None], (2 * chip + c, 0, 0))
    w_ssem, w_rsem, w_srcs, w_lands, w_token = _split_start(
        _win_and_small_copies, 10, [outs[0], gp], [lax.empty((3,) + outs[0].shape[1:], BF16), slots], gx,
        "scatter_win_start")
    shared = _split_wait(_sibling_copies, sh_ssem, sh_rsem, sh_first[:5], sh_first[5:], [w_token],
                         "rs_share_early_wait", only=(1, 2, 3, 4))
    adam_e = _adamw_shards(shared[1:5], shared[6:], big_w[1:], moments[0][1:], moments[1][1:], c_arr, "early")
    w_first = _split_wait(_win_and_small_copies, w_ssem, w_rsem, w_srcs, w_lands, [adam_e[0]], "scatter_win_wait",
                          only=(0, 1, 2))
    win_final, win_sib = _sum_chips_shared(outs[1], w_first[2], "win")
    adam_w = _adamw_shards([win_final], [win_sib], big_w[:1], moments[0][:1], moments[1][:1], c_arr, "win")
    big_g, big_d, big_m, big_v = [[adam_w[i]] + list(adam_e[4 * i:4 * i + 4]) for i in range(4)]

    gathered = _split_wait(_win_and_small_copies, w_ssem, w_rsem, w_first[:2], w_first[2:], [adam_w[0]],
                           "small_allgather_wait", only=tuple(range(3, 10)))[3]
    flat = _small_sum_adamw(gathered, wp, mp, vp)
    small_out = [flat[8 * kind:8 * kind + 8] for kind in range(4)]
    total_loss = flat[-1][0, 0]

    def ordered(small8, big4):
        sg1, sg2, sg3, sg4, swp, ssc, srb, ssk = small8
        swp, srb = swp[None], srb.T
        bwin, bwout, bwg, bwu, bwd = big4[0].T[None], big4[1][None], big4[2].T[None], big4[3].T[None], big4[4][None]
        return [sg1, bwin, swp, ssc, srb, ssk, bwout, sg2, sg3, bwg, bwu, bwd, sg4]

    res = [total_loss, gx[None]]
    for sm, bg in zip(small_out, (big_g, big_d, big_m, big_v)):
        res += ordered(sm, bg)
    return tuple(res)
```

```python
import functools

import numpy as np
import jax
import jax.numpy as jnp
from jax import lax
from jax.experimental import pallas as pl
from jax.experimental.pallas import tpu as pltpu

F32 = jnp.float32
BF16 = jnp.bfloat16

D = 1024
POOL_W = 512
GROUP = 128
WINDOWS = (2, 4, 8, 16)
HALO = 128
NQ = 8
BLK = 128
NBUCKET = 32
IN_W = 1280
DFF = 2816
NSH = 4
FS = DFF // NSH
WIN_S = IN_W // NSH
WOUT_S = D // NSH
EPS = 1e-6
NEG = -1e30
SCALE = 0.125

ADAM_LR = 0.001
ADAM_B1 = 0.9
ADAM_B2 = 0.999
ADAM_EPS = 1e-08
ADAM_WD = 0.01
ADAM_STEP = 10

TM = 512
TM_IN = 1024
TM_POOL = 1024
TM_FFN = 512
TM_FFN_FWD = 1024
TM_FFN_W = 2048
FFN_ROWS = 256
VMEM_LIMIT = 60 * 1024 * 1024

MESH_T = pl.DeviceIdType.MESH
ANY = pl.BlockSpec(memory_space=pl.ANY)


def _cp(n_grid=0, **kw):
    sem = ("arbitrary",) * n_grid if n_grid else None
    return pltpu.CompilerParams(dimension_semantics=sem, vmem_limit_bytes=VMEM_LIMIT, **kw)


def _dot(a, b):
    return jnp.dot(a, b, preferred_element_type=F32)


def _dot_nt(a, b):
    return lax.dot_general(a, b, (((1,), (1,)), ((), ())), preferred_element_type=F32)


def _dot_tn(a, b):
    return lax.dot_general(a, b, (((0,), (0,)), ((), ())), preferred_element_type=F32)


def _rms(x):
    r = lax.rsqrt(jnp.mean(x * x, axis=-1, keepdims=True) + EPS)
    return r, x * r


def _rms_bwd(r, n, g, dout):
    dn = dout * g
    dx = r * (dn - n * jnp.mean(dn * n, axis=-1, keepdims=True))
    dg = jnp.sum(dout * n, axis=0, keepdims=True)
    return dx, dg


def _split(x, n):
    parts = []
    r = x
    for _ in range(n):
        p = r.astype(BF16)
        parts.append(p)
        r = r - p.astype(F32)
    return parts


def _band_dot(a, x, n):
    acc = None
    for p in _split(x, n):
        t = _dot(a, p)
        acc = t if acc is None else acc + t
    return acc


def _bucket_table():
    qi = np.arange(BLK)[None, :]
    kj = np.arange(2 * BLK)[:, None]
    dist = qi + BLK - kj
    n = np.maximum(dist, 0)
    nf = np.maximum(n, 1).astype(np.float32)
    large = 16 + (np.log(nf / np.float32(16)) / np.float32(np.log(128 / 16)) * np.float32(16)).astype(np.int32)
    large = np.minimum(large, NBUCKET - 1)
    return np.where(n < 16, n, large).astype(np.int32)


def _prenorm(x, g1, dep=None):
    s = x.shape[0]
    tm = min(TM_IN, s)

    def body(x_ref, g_ref, *rest):
        _, n = _rms(x_ref[...])
        rest[-1][...] = (n * g_ref[...]).astype(BF16)

    row = pl.BlockSpec((tm, D), lambda i: (i, 0))
    return pl.pallas_call(
        body, name="prenorm", grid=(s // tm,),
        in_specs=[row, pl.BlockSpec((1, D), lambda i: (0, 0))] + ([] if dep is None else [ANY]), out_specs=row,
        out_shape=jax.ShapeDtypeStruct((s, D), BF16),
        compiler_params=_cp(1))(x, g1, *([] if dep is None else [dep]))


def _inproj_fwd(h1, w_in):
    s = h1.shape[0]
    tm = min(TM_IN, s)

    def body(h_ref, w_ref, u_ref, q_ref, k_ref, v_ref):
        proj = _dot_nt(h_ref[...], w_ref[...])
        u_ref[...] = proj[:, :512]
        q_ref[...] = proj[:, 512:1024].astype(BF16)
        k_ref[...] = proj[:, 1024:1152].astype(BF16)
        v_ref[...] = proj[:, 1152:1280].astype(BF16)

    row = lambda w: pl.BlockSpec((tm, w), lambda i: (i, 0))
    return pl.pallas_call(
        body, name="inproj_fwd", grid=(s // tm,),
        in_specs=[row(D), pl.BlockSpec((IN_W, D), lambda i: (0, 0))],
        out_specs=[row(512), row(512), row(128), row(128)],
        out_shape=[jax.ShapeDtypeStruct((s, 512), F32), jax.ShapeDtypeStruct((s, 512), BF16),
                   jax.ShapeDtypeStruct((s, 128), BF16), jax.ShapeDtypeStruct((s, 128), BF16)],
        compiler_params=_cp(1))(h1, w_in)


def _window_sums(ext, w, lag_sign, tm, terms):
    rows = lax.broadcasted_iota(jnp.int32, (HALO, 2 * HALO), 0)
    cols = lax.broadcasted_iota(jnp.int32, (HALO, 2 * HALO), 1)
    d = rows + HALO - cols if lag_sign > 0 else cols - rows
    band = jnp.where((d >= 0) & (d < w), 1.0, 0.0).astype(BF16)
    return jnp.concatenate([_band_dot(band, ext[r:r + 2 * HALO], terms) for r in range(0, tm, HALO)], axis=0)


def _pooled(ext, cur, t0, tm):
    t = t0 + lax.broadcasted_iota(jnp.int32, (tm, GROUP), 0)
    out = []
    for g, w in enumerate(WINDOWS):
        sl = slice(g * GROUP, (g + 1) * GROUP)
        cnt = jnp.minimum(t + 1, w).astype(F32)
        out.append(_window_sums(ext[:, sl], w, 1, tm, 2) / cnt - cur[:, sl])
    return out


def _pool_fwd(u, w_pool, pool_scale):
    s = u.shape[0]
    tm = min(TM_POOL, s)
    hb = tm // HALO

    def body(uc_ref, uh_ref, wp_ref, sc_ref, o_ref, pooled_ref):
        i = pl.program_id(0)
        cur = uc_ref[...]
        halo = jnp.where(i > 0, uh_ref[...], 0.0)
        ext = jnp.concatenate([halo, cur], axis=0)
        pooled = _pooled(ext, cur, i * tm, tm)
        for g in range(4):
            sl = slice(g * GROUP, (g + 1) * GROUP)
            pb = pooled[g].astype(BF16)
            pooled_ref[:, sl] = pb
            o_ref[:, sl] = (_dot(pb, wp_ref[g]) * sc_ref[:, sl]).astype(BF16)

    row = pl.BlockSpec((tm, 512), lambda i: (i, 0))
    return pl.pallas_call(
        body, name="pool_fwd", grid=(s // tm,),
        in_specs=[row, pl.BlockSpec((HALO, 512), lambda i: (jnp.maximum(i * hb - 1, 0), 0)),
                  pl.BlockSpec((4, GROUP, GROUP), lambda i: (0, 0, 0)),
                  pl.BlockSpec((1, 512), lambda i: (0, 0))],
        out_specs=[row, row],
        out_shape=[jax.ShapeDtypeStruct((s, 512), BF16)] * 2,
        compiler_params=_cp(1))(u, u, w_pool, pool_scale)


def _bias_table(bucket, rel_bias):
    def body(b_ref, rb_ref, o_ref):
        bucket_v = b_ref[...]
        key = lax.broadcasted_iota(jnp.int32, (2 * BLK, BLK), 0)
        dist = lax.broadcasted_iota(jnp.int32, (2 * BLK, BLK), 1) + BLK - key
        inwin = (dist >= 0) & (dist < BLK)
        for h in range(NQ):
            acc = jnp.zeros((2 * BLK, BLK), F32)
            for b in range(NBUCKET):
                acc = jnp.where(bucket_v == b, rb_ref[b, h], acc)
            rest = jnp.where(inwin, acc, NEG)
            o_ref[1, h] = rest
            o_ref[0, h] = jnp.where(key >= BLK, rest, NEG)

    return pl.pallas_call(
        body, name="bias_table",
        in_specs=[pl.BlockSpec(memory_space=pltpu.VMEM), pl.BlockSpec(memory_space=pltpu.SMEM)],
        out_specs=pl.BlockSpec(memory_space=pltpu.VMEM),
        out_shape=jax.ShapeDtypeStruct((2, NQ, 2 * BLK, BLK), F32),
        compiler_params=_cp())(bucket, rel_bias)


HD = 64


def _kv_band(ref, n, transposed):
    pstart = pl.multiple_of(jnp.maximum(n - 1, 0) * BLK, BLK)
    cstart = pl.multiple_of(n * BLK, BLK)
    lo = lax.broadcasted_iota(jnp.int32, (2 * BLK, 128), 1) < HD
    band = jnp.concatenate([ref[pl.ds(pstart, BLK), :], ref[pl.ds(cstart, BLK), :]], axis=0).astype(F32)
    rot = pltpu.roll(band, HD, axis=1)
    halves = ((jnp.where(lo, band, 0.0), jnp.where(lo, 0.0, rot)), (jnp.where(lo, rot, 0.0), jnp.where(lo, 0.0, band)))
    if transposed:
        out = [jnp.concatenate([a.T, b.T], axis=1).astype(BF16) for a, b in halves]
    else:
        out = [jnp.concatenate([a, b], axis=0).astype(BF16) for a, b in halves]
    return out, (lo, pstart, cstart)


def _pair_probs(qt, kk, bias, sk_ref, p):
    sink = jnp.concatenate([jnp.full((1, 1, BLK), sk_ref[0, 2 * p + e], F32) for e in range(2)], axis=0)
    s = _dot_nt(kk, qt).reshape(2, 2 * BLK, BLK) + bias
    m = jnp.maximum(jnp.max(s, axis=1, keepdims=True), sink)
    pr = jnp.exp(s - m)
    es = jnp.exp(sink - m)
    inv = 1.0 / (jnp.sum(pr, axis=1, keepdims=True) + es)
    return pr * inv, es * inv


def _attn_fwd(q, k, v, bias, sinks):
    s = q.shape[0]
    nblk = s // BLK

    def body(q_ref, k_ref, v_ref, b_ref, sk_ref, o_ref, prob_ref, ps_ref):
        n = pl.program_id(0)
        kk, _ = _kv_band(k_ref, n, False)
        vt, _ = _kv_band(v_ref, n, True)
        bidx = jnp.minimum(n, 1)
        for p in range(4):
            h = p // 2
            qt = (q_ref[:, p * 128:(p + 1) * 128].astype(F32) * SCALE).astype(BF16)
            prob, ps = _pair_probs(qt, kk[h], b_ref[bidx, pl.ds(2 * p, 2)], sk_ref, p)
            pb = prob.astype(BF16)
            prob_ref[pl.ds(2 * p, 2)] = pb
            ps_ref[pl.ds(2 * p, 2), :] = ps.reshape(2, BLK)
            acc_t = _dot(vt[h], pb.reshape(4 * BLK, BLK))
            o_ref[:, p * 128:(p + 1) * 128] = acc_t.T.astype(BF16)

    return pl.pallas_call(
        body, name="attn_fwd", grid=(nblk,),
        in_specs=[pl.BlockSpec((BLK, 512), lambda i: (i, 0)),
                  pl.BlockSpec((s, 128), lambda i: (0, 0)),
                  pl.BlockSpec((s, 128), lambda i: (0, 0)),
                  pl.BlockSpec((2, NQ, 2 * BLK, BLK), lambda i: (0, 0, 0, 0)),
                  pl.BlockSpec(memory_space=pltpu.SMEM)],
        out_specs=[pl.BlockSpec((BLK, 512), lambda i: (i, 0)),
                   pl.BlockSpec((None, NQ, 2 * BLK, BLK), lambda i: (i, 0, 0, 0)),
                   pl.BlockSpec((None, NQ, BLK), lambda i: (i, 0, 0))],
        out_shape=[jax.ShapeDtypeStruct((s, 512), BF16), jax.ShapeDtypeStruct((nblk, NQ, 2 * BLK, BLK), BF16),
                   jax.ShapeDtypeStruct((nblk, NQ, BLK), F32)],
        compiler_params=_cp(1))(q, k, v, bias, sinks)


def _outproj_fwd(pool_o, attn_o, w_out, x, g2, g3, between=None):
    s = x.shape[0]
    tm = min(TM, s)
    nt = s // tm

    def part(name, tile0, tiles, filled, dep):
        def body(p_ref, a_ref, w_ref, x_ref, g2_ref, g3_ref, *rest):
            mix_ref, x1_ref, h2_ref = rest[-3:]
            mix = _dot(p_ref[...], w_ref[0:512, :]) + _dot(a_ref[...], w_ref[512:1024, :])
            mix_ref[...] = mix
            _, n2 = _rms(mix)
            x1 = x_ref[...] + n2 * g2_ref[...]
            x1_ref[...] = x1
            _, n3 = _rms(x1)
            h2_ref[...] = (n3 * g3_ref[...]).astype(BF16)

        row = lambda w: pl.BlockSpec((tm, w), lambda i: (i + tile0, 0))
        vec = pl.BlockSpec((1, D), lambda i: (0, 0))
        extra = list(filled) + ([] if dep is None else [dep])
        return pl.pallas_call(
            body, name=name, grid=(tiles,),
            in_specs=[row(512), row(512), pl.BlockSpec((D, D), lambda i: (0, 0)), row(D), vec, vec]
            + [ANY] * len(extra),
            out_specs=[row(D), row(D), row(D)],
            out_shape=[jax.ShapeDtypeStruct((s, D), F32), jax.ShapeDtypeStruct((s, D), F32),
                       jax.ShapeDtypeStruct((s, D), BF16)],
            input_output_aliases={6 + t: t for t in range(len(filled))},
            compiler_params=_cp(1))(pool_o, attn_o, w_out, x, g2, g3, *extra)

    if between is None or nt < 2:
        return part("outproj_fwd", 0, nt, (), None)
    head = max(1, 5 * nt // 8)
    first = part("outproj_fwd_a", 0, head, (), None)
    return part("outproj_fwd_b", head, nt - head, first, between(first[2]))


def _silu_parts(g):
    sg = jax.nn.sigmoid(g)
    return sg, g * sg


def _ffn_fwd(h2, wg, wu, wd, x1, target, g4):
    s = h2.shape[0]
    tm = min(TM_FFN_FWD, s)

    def body(h_ref, wg_ref, wu_ref, wd_ref, x1_hbm, t_hbm, g4_ref,
             gate_ref, up_ref, a_ref, df_ref, dy_ref, loss_ref, gg4_ref, f_acc, x1_ref, t_ref, sem):
        i = pl.program_id(0)
        j = pl.program_id(1)
        first = j == 0
        tile = pl.ds(pl.multiple_of(i * tm, tm), tm)
        fetches = (pltpu.make_async_copy(x1_hbm.at[tile], x1_ref, sem.at[0]),
                   pltpu.make_async_copy(t_hbm.at[tile], t_ref, sem.at[1]))

        @pl.when(first)
        def _():
            for fetch in fetches:
                fetch.start()

        chunks = [pl.ds(r, min(FFN_ROWS, tm)) for r in range(0, tm, FFN_ROWS)]
        project = lambda rows: (_dot_nt(h_ref[rows, :], wg_ref[...]), _dot_nt(h_ref[rows, :], wu_ref[...]))
        ahead = [project(chunks[0])]
        for k, rows in enumerate(chunks):
            if k + 1 < len(chunks):
                ahead.append(project(chunks[k + 1]))
            gate, up = ahead[k]
            gate_ref[rows, :] = gate.astype(BF16)
            up_ref[rows, :] = up.astype(BF16)
            _, sl = _silu_parts(gate)
            a = (sl * up).astype(BF16)
            a_ref[rows, :] = a
            contrib = _dot(a, wd_ref[...])
            f_acc[rows, :] = jnp.where(first, contrib, f_acc[rows, :] + contrib)

        @pl.when((i == 0) & (j == 0))
        def _():
            loss_ref[...] = jnp.zeros_like(loss_ref)
            gg4_ref[...] = jnp.zeros_like(gg4_ref)

        @pl.when(j == NSH - 1)
        def _():
            for fetch in fetches:
                fetch.wait()
            r4, n4 = _rms(f_acc[...])
            g4v = g4_ref[...]
            err = x1_ref[...] + n4 * g4v - t_ref[...]
            loss_ref[...] += (0.5 / D) * jnp.sum(err * err).reshape(1, 1)
            dy = err * (1.0 / D)
            dy_ref[...] = dy
            df, dg = _rms_bwd(r4, n4, g4v, dy)
            gg4_ref[...] += dg
            df_ref[...] = df.astype(BF16)

    row = lambda w: pl.BlockSpec((tm, w), lambda i, j: (i, 0))
    sh = lambda r, c: pl.BlockSpec((None, r, c), lambda i, j: (j, 0, 0))
    act = pl.BlockSpec((None, tm, FS), lambda i, j: (j, i, 0))
    vec = pl.BlockSpec((1, D), lambda i, j: (0, 0))
    return pl.pallas_call(
        body, name="ffn_fwd", grid=(s // tm, NSH),
        in_specs=[row(D), sh(FS, D), sh(FS, D), sh(FS, D), ANY, ANY, vec],
        out_specs=[act, act, act, row(D), row(D), pl.BlockSpec((1, 1), lambda i, j: (0, 0)), vec],
        out_shape=[jax.ShapeDtypeStruct((NSH, s, FS), BF16)] * 3
        + [jax.ShapeDtypeStruct((s, D), BF16), jax.ShapeDtypeStruct((s, D), F32),
           jax.ShapeDtypeStruct((1, 1), F32), jax.ShapeDtypeStruct((1, D), F32)],
        scratch_shapes=[pltpu.VMEM((tm, D), F32)] * 3 + [pltpu.SemaphoreType.DMA((2,))],
        compiler_params=_cp(2))(h2, wg, wu, wd, x1, target, g4)


def _ffn_bwd_act(df, gate, up, wg, wu, wd, dy, x1, mix, g3, g2):
    s = df.shape[0]
    tm = min(TM_FFN, s)

    def body(df_ref, gate_ref, up_ref, wg_ref, wu_ref, wd_ref, dy_ref, x1_ref, mix_ref, g3_ref, g2_ref,
             dgate_ref, dup_ref, dx1_ref, dmix_ref, gg3_ref, gg2_ref, acc):
        j = pl.program_id(0)
        i = pl.program_id(1)
        first = j == 0
        tile = pl.multiple_of(i * tm, tm)
        chunks = [pl.ds(r, min(FFN_ROWS, tm)) for r in range(0, tm, FFN_ROWS)]

        @pl.when((i == 0) & (j == 0))
        def _():
            gg3_ref[...] = jnp.zeros_like(gg3_ref)
            gg2_ref[...] = jnp.zeros_like(gg2_ref)

        def norms_backward(rows, dh2):
            r3, n3 = _rms(x1_ref[rows, :])
            dx1n, dg3 = _rms_bwd(r3, n3, g3_ref[...], dh2)
            dx1 = dy_ref[rows, :] + dx1n
            dx1_ref[rows, :] = dx1
            gg3_ref[...] += dg3
            r2, n2 = _rms(mix_ref[rows, :])
            dmix, dg2 = _rms_bwd(r2, n2, g2_ref[...], dx1)
            gg2_ref[...] += dg2
            dmix_ref[rows, :] = dmix.astype(BF16)

        def shard_pass(last):
            das = [_dot_nt(df_ref[chunks[0], :], wd_ref[...])]
            for k, rows in enumerate(chunks):
                if k + 1 < len(chunks):
                    das.append(_dot_nt(df_ref[chunks[k + 1], :], wd_ref[...]))
                da = das[k]
                g = gate_ref[rows, :].astype(F32)
                u = up_ref[rows, :].astype(F32)
                sg, sl = _silu_parts(g)
                dgate = (da * u * (sg * (1.0 + g * (1.0 - sg)))).astype(BF16)
                dup = (da * sl).astype(BF16)
                dgate_ref[rows, :] = dgate
                dup_ref[rows, :] = dup
                contrib = _dot(dgate, wg_ref[...]) + _dot(dup, wu_ref[...])
                acc_rows = pl.ds(tile + k * FFN_ROWS, rows.size)
                if last:
                    norms_backward(rows, acc[acc_rows, :] + contrib)
                else:
                    acc[acc_rows, :] = jnp.where(first, contrib, acc[acc_rows, :] + contrib)

        pl.when(j < NSH - 1)(functools.partial(shard_pass, False))
        pl.when(j == NSH - 1)(functools.partial(shard_pass, True))

    row = pl.BlockSpec((tm, D), lambda j, i: (i, 0))
    last = pl.BlockSpec((tm, D), lambda j, i: (i * (j // (NSH - 1)), 0))
    sh = pl.BlockSpec((None, FS, D), lambda j, i: (j, 0, 0))
    act = pl.BlockSpec((None, tm, FS), lambda j, i: (j, i, 0))
    vec = pl.BlockSpec((1, D), lambda j, i: (0, 0))
    return pl.pallas_call(
        body, name="ffn_bwd_act", grid=(NSH, s // tm),
        in_specs=[row, act, act, sh, sh, sh, last, last, last, vec, vec],
        out_specs=[act, act, last, last, vec, vec],
        out_shape=[jax.ShapeDtypeStruct((NSH, s, FS), BF16), jax.ShapeDtypeStruct((NSH, s, FS), BF16),
                   jax.ShapeDtypeStruct((s, D), F32), jax.ShapeDtypeStruct((s, D), BF16),
                   jax.ShapeDtypeStruct((1, D), F32), jax.ShapeDtypeStruct((1, D), F32)],
        scratch_shapes=[pltpu.VMEM((s, D), F32)],
        compiler_params=_cp(2))(df, gate, up, wg, wu, wd, dy, x1, mix, g3, g2)


SMEM_SPEC = pl.BlockSpec(memory_space=pltpu.SMEM)


def _emit_halves(acc_ref, row0, rows, c, own_ref, oth_ref):
    half = rows // 2
    own_ref[...] = acc_ref[pl.ds(row0 + pl.multiple_of(c * half, 8), half), :]
    oth_ref[...] = acc_ref[pl.ds(row0 + pl.multiple_of((1 - c) * half, 8), half), :].astype(BF16)


def _half_shapes(rows):
    return [jax.ShapeDtypeStruct((NSH, rows // 2, D), F32), jax.ShapeDtypeStruct((NSH, rows // 2, D), BF16)]


def _ffn_bwd_w(h2, act_a, dgate, dup, df, c_arr):
    s = h2.shape[0]
    tm = min(TM_FFN_W, s)
    nt = s // tm

    def body(c_ref, h_ref, a_ref, dgate_ref, dup_ref, df_ref, *rest):
        outs, accs = rest[:6], rest[6:]
        i = pl.program_id(1)

        @pl.when(i == 0)
        def _():
            for acc in accs:
                acc[...] = jnp.zeros_like(acc)

        h = h_ref[...]
        accs[0][...] += _dot_tn(dgate_ref[...], h)
        accs[1][...] += _dot_tn(dup_ref[...], h)
        accs[2][...] += _dot_tn(a_ref[...], df_ref[...])

        @pl.when(i == nt - 1)
        def _():
            for t, acc in enumerate(accs):
                _emit_halves(acc, 0, FS, c_ref[0], outs[2 * t], outs[2 * t + 1])

    row = lambda w: pl.BlockSpec((tm, w), lambda j, i: (i, 0))
    act = pl.BlockSpec((None, tm, FS), lambda j, i: (j, i, 0))
    half = pl.BlockSpec((None, FS // 2, D), lambda j, i: (j, 0, 0))
    return pl.pallas_call(
        body, name="ffn_bwd_w", grid=(NSH, nt),
        in_specs=[SMEM_SPEC, row(D), act, act, act, row(D)],
        out_specs=[half] * 6,
        out_shape=_half_shapes(FS) * 3,
        scratch_shapes=[pltpu.VMEM((FS, D), F32)] * 3,
        compiler_params=_cp(2))(c_arr, h2, act_a, dgate, dup, df)


def _outproj_bwd(dmix, w_out, pool_o, attn_o, c_arr, dep=None):
    s = dmix.shape[0]
    tm = min(TM, s)
    nt = s // tm

    def body(c_ref, dm_ref, w_ref, p_ref, a_ref, *rest):
        dpool_ref, dattn_ref, own_ref, oth_ref, gw_ref = rest[-5:]
        i = pl.program_id(0)

        @pl.when(i == 0)
        def _():
            gw_ref[...] = jnp.zeros_like(gw_ref)

        dm = dm_ref[...]
        dpool_ref[...] = _dot_nt(dm, w_ref[0:512, :])
        dattn_ref[...] = _dot_nt(dm, w_ref[512:1024, :]).astype(BF16)
        gw_ref[0:512, :] += _dot_tn(p_ref[...], dm)
        gw_ref[512:1024, :] += _dot_tn(a_ref[...], dm)

        @pl.when(i == nt - 1)
        def _():
            for k in range(NSH):
                _emit_halves(gw_ref, k * WOUT_S, WOUT_S, c_ref[0], own_ref.at[k], oth_ref.at[k])

    row = lambda w: pl.BlockSpec((tm, w), lambda i: (i, 0))
    full = pl.BlockSpec((D, D), lambda i: (0, 0))
    half = pl.BlockSpec((NSH, WOUT_S // 2, D), lambda i: (0, 0, 0))
    return pl.pallas_call(
        body, name="outproj_bwd", grid=(nt,),
        in_specs=[SMEM_SPEC, row(D), full, row(512), row(512)] + ([] if dep is None else [ANY]),
        out_specs=[row(512), row(512), half, half],
        out_shape=[jax.ShapeDtypeStruct((s, 512), F32), jax.ShapeDtypeStruct((s, 512), BF16)] + _half_shapes(WOUT_S),
        scratch_shapes=[pltpu.VMEM((D, D), F32)],
        compiler_params=_cp(1))(c_arr, dmix, w_out, pool_o, attn_o, *([] if dep is None else [dep]))


def _pool_bwd(pooled, dpool, w_pool, pool_scale, dep=None):
    s = pooled.shape[0]
    tm = min(TM_POOL, s)
    hb = tm // HALO
    nt = s // tm
    last_halo = s // HALO - 1

    def body(p_ref, dc_ref, dn_ref, wp_ref, sc_ref, *rest):
        du_ref, gwp_ref, gsc_ref = rest[-3:]
        i = pl.program_id(0)

        @pl.when(i == 0)
        def _():
            gwp_ref[...] = jnp.zeros_like(gwp_ref)
            gsc_ref[...] = jnp.zeros_like(gsc_ref)

        dcur = dc_ref[...]
        dnext = jnp.where(i < nt - 1, dn_ref[...], 0.0)
        dext = jnp.concatenate([dcur, dnext], axis=0)
        t_ext = i * tm + lax.broadcasted_iota(jnp.int32, (tm + HALO, GROUP), 0)
        for g, w in enumerate(WINDOWS):
            sl = slice(g * GROUP, (g + 1) * GROUP)
            pb = p_ref[:, sl]
            wp = wp_ref[g]
            mixed = _dot(pb, wp)
            gsc_ref[:, sl] += jnp.sum(dcur[:, sl] * mixed, axis=0, keepdims=True)
            dmixed = (dext[:, sl] * sc_ref[:, sl]).astype(BF16)
            gwp_ref[g] += _dot_tn(pb, dmixed[0:tm])
            dpooled = _dot_nt(dmixed, wp)
            z = dpooled / jnp.minimum(t_ext + 1, w).astype(F32)
            du_ref[:, sl] = (_window_sums(z, w, -1, tm, 2) - dpooled[0:tm]).astype(BF16)

    return pl.pallas_call(
        body, name="pool_bwd", grid=(nt,),
        in_specs=[pl.BlockSpec((tm, 512), lambda i: (i, 0)),
                  pl.BlockSpec((tm, 512), lambda i: (i, 0)),
                  pl.BlockSpec((HALO, 512), lambda i: (jnp.minimum((i + 1) * hb, last_halo), 0)),
                  pl.BlockSpec((4, GROUP, GROUP), lambda i: (0, 0, 0)),
                  pl.BlockSpec((1, 512), lambda i: (0, 0))] + ([] if dep is None else [ANY]),
        out_specs=[pl.BlockSpec((tm, 512), lambda i: (i, 0)),
                   pl.BlockSpec((4, GROUP, GROUP), lambda i: (0, 0, 0)),
                   pl.BlockSpec((1, 512), lambda i: (0, 0))],
        out_shape=[jax.ShapeDtypeStruct((s, 512), BF16), jax.ShapeDtypeStruct((4, GROUP, GROUP), F32),
                   jax.ShapeDtypeStruct((1, 512), F32)],
        compiler_params=_cp(1))(pooled, dpool, dpool, w_pool, pool_scale, *([] if dep is None else [dep]))


def _attn_bwd(q, k, v, do, probs, sink_share, bucket, dep=None):
    s = q.shape[0]
    nblk = s // BLK

    def body(q_ref, do_ref, k_ref, v_ref, prob_ref, ps_ref, bk_ref, *rest):
        dq_ref, dk_ref, dv_ref, grb_ref, gsk_ref, dss_ref = rest[-6:]
        n = pl.program_id(0)

        @pl.when(n == 0)
        def _():
            dk_ref[...] = jnp.zeros_like(dk_ref)
            dv_ref[...] = jnp.zeros_like(dv_ref)
            dss_ref[...] = jnp.zeros_like(dss_ref)
            gsk_ref[...] = jnp.zeros_like(gsk_ref)

        kt, (lo, pstart, cstart) = _kv_band(k_ref, n, True)
        vv, _ = _kv_band(v_ref, n, False)
        lo_q = lax.broadcasted_iota(jnp.int32, (BLK, 128), 1) < HD
        dkk = [jnp.zeros((2 * BLK, 128), F32), jnp.zeros((2 * BLK, 128), F32)]
        dvv = [jnp.zeros((2 * BLK, 128), F32), jnp.zeros((2 * BLK, 128), F32)]

        def by_head(t):
            return jnp.concatenate([jnp.where(lo_q, t, 0.0), jnp.where(lo_q, 0.0, t)], axis=0).astype(BF16)

        for p in range(4):
            h = p // 2
            qt = q_ref[:, p * 128:(p + 1) * 128].astype(F32) * SCALE
            dot = do_ref[:, p * 128:(p + 1) * 128]
            pb = prob_ref[pl.ds(2 * p, 2)]
            prob = pb.astype(F32)
            ps = ps_ref[pl.ds(2 * p, 2), :].reshape(2, 1, BLK)
            dp = _dot_nt(vv[h], dot).reshape(2, 2 * BLK, BLK)
            delta = jnp.sum(prob * dp, axis=1, keepdims=True)
            ds = prob * (dp - delta)
            sink_part = ps * delta
            for e in range(2):
                gs = -jnp.sum(sink_part[e]).reshape(1, 1)
                gsk_ref[pl.ds(2 * p + e, 1), :] += jnp.broadcast_to(gs, (1, 128))
            dss_ref[pl.ds(2 * p, 2)] += ds
            dsb = ds.astype(BF16)
            dq_t = _dot(kt[h], dsb.reshape(4 * BLK, BLK))
            dq_ref[:, p * 128:(p + 1) * 128] = (dq_t.T * SCALE).astype(BF16)
            dkk[h] = dkk[h] + _dot(jnp.concatenate([dsb[0], dsb[1]], axis=1), by_head(qt))
            dvv[h] = dvv[h] + _dot(jnp.concatenate([pb[0], pb[1]], axis=1), by_head(dot.astype(F32)))
        for acc, ref in ((dkk, dk_ref), (dvv, dv_ref)):
            f0 = acc[0] + pltpu.roll(acc[0], HD, axis=1)
            f1 = acc[1] + pltpu.roll(acc[1], HD, axis=1)
            band = jnp.where(lo, f0, f1)
            ref[pl.ds(pstart, BLK), :] += band[0:BLK]
            ref[pl.ds(cstart, BLK), :] += band[BLK:2 * BLK]

        @pl.when(n == nblk - 1)
        def _():
            _relbias_grad(dss_ref, bk_ref[...], grb_ref)

    blk = pl.BlockSpec((BLK, 512), lambda i: (i, 0))
    kv = pl.BlockSpec((s, 128), lambda i: (0, 0))
    row8 = pl.BlockSpec((NQ, 128), lambda i: (0, 0))
    return pl.pallas_call(
        body, name="attn_bwd", grid=(nblk,),
        in_specs=[blk, blk, kv, kv, pl.BlockSpec((None, NQ, 2 * BLK, BLK), lambda i: (i, 0, 0, 0)),
                  pl.BlockSpec((None, NQ, BLK), lambda i: (i, 0, 0)), pl.BlockSpec((2 * BLK, BLK), lambda i: (0, 0))]
        + ([] if dep is None else [ANY]),
        out_specs=[blk, kv, kv, row8, row8],
        out_shape=[jax.ShapeDtypeStruct((s, 512), BF16), jax.ShapeDtypeStruct((s, 128), F32),
                   jax.ShapeDtypeStruct((s, 128), F32), jax.ShapeDtypeStruct((NQ, 128), F32),
                   jax.ShapeDtypeStruct((NQ, 128), F32)],
        scratch_shapes=[pltpu.VMEM((NQ, 2 * BLK, BLK), F32)],
        compiler_params=_cp(1))(q, do, k, v, probs, sink_share, bucket, *([] if dep is None else [dep]))


def _relbias_grad(ds_ref, bucket_v, o_ref):
    lane = lax.broadcasted_iota(jnp.int32, (NQ, 128), 1)
    head_row = lax.broadcasted_iota(jnp.int32, (NQ, BLK), 0)
    acc = jnp.zeros((NQ, 128), F32)
    for b in range(NBUCKET):
        sel = bucket_v == b
        stack = jnp.zeros((NQ, BLK), F32)
        for h in range(NQ):
            col_sums = jnp.sum(jnp.where(sel, ds_ref[h], 0.0), axis=0, keepdims=True)
            stack = jnp.where(head_row == h, col_sums, stack)
        acc = acc + jnp.where(lane == b, jnp.sum(stack, axis=1, keepdims=True), 0.0)
    o_ref[...] = acc


def _inproj_bwd(x, g1, du, dq, dk, dv, w_in, dx1, c_arr):
    s = x.shape[0]
    tm = min(TM, s)
    nt = s // tm
    cols = ((0, 512), (512, 1024), (1024, 1152), (1152, 1280))

    def body(c_ref, x_ref, g_ref, du_ref, dq_ref, dk_ref, dv_ref, w_ref, dx1_ref,
             gx_ref, own_ref, oth_ref, gg_ref, gw_ref):
        i = pl.program_id(0)

        @pl.when(i == 0)
        def _():
            gw_ref[...] = jnp.zeros_like(gw_ref)
            gg_ref[...] = jnp.zeros_like(gg_ref)

        parts = (du_ref[...], dq_ref[...], dk_ref[...].astype(BF16), dv_ref[...].astype(BF16))
        dh = None
        for (a, b), dpart in zip(cols, parts):
            t = _dot(dpart, w_ref[a:b, :])
            dh = t if dh is None else dh + t
        gv = g_ref[...]
        r1, n1 = _rms(x_ref[...])
        h = (n1 * gv).astype(BF16)
        for (a, b), dpart in zip(cols, parts):
            gw_ref[a:b, :] += _dot_tn(dpart, h)
        dx, dg = _rms_bwd(r1, n1, gv, dh)
        gg_ref[...] += dg
        gx_ref[...] = dx1_ref[...] + dx

        @pl.when(i == nt - 1)
        def _():
            for k in range(NSH):
                _emit_halves(gw_ref, k * WIN_S, WIN_S, c_ref[0], own_ref.at[k], oth_ref.at[k])

    row = lambda w: pl.BlockSpec((tm, w), lambda i: (i, 0))
    vec = pl.BlockSpec((1, D), lambda i: (0, 0))
    full = pl.BlockSpec((IN_W, D), lambda i: (0, 0))
    half = pl.BlockSpec((NSH, WIN_S // 2, D), lambda i: (0, 0, 0))
    return pl.pallas_call(
        body, name="inproj_bwd", grid=(nt,),
        in_specs=[SMEM_SPEC, row(D), vec, row(512), row(512), row(128), row(128), full, row(D)],
        out_specs=[row(D), half, half, vec],
        out_shape=[jax.ShapeDtypeStruct((s, D), F32)] + _half_shapes(WIN_S) + [jax.ShapeDtypeStruct((1, D), F32)],
        scratch_shapes=[pltpu.VMEM((IN_W, D), F32)],
        compiler_params=_cp(1))(c_arr, x, g1, du, dq, dk, dv, w_in, dx1)


def _local_step(x, target, small, w_in, w_out, ffn_weights, c_arr, on_ffn_grads=None, on_wout_grads=None,
                mid_outproj=None, start_dep=None):
    g1, g2, g3, g4, w_pool, pool_scale, rel_bias, sinks = small
    bucket = jnp.asarray(_bucket_table())
    wp_bf = w_pool.astype(BF16)
    bias = _bias_table(bucket, rel_bias)
    h1 = _prenorm(x, g1, start_dep)
    if callable(w_in):
        w_in = w_in(bias, h1)
    u, q, k, v = _inproj_fwd(h1, w_in)
    pool_o, pooled = _pool_fwd(u, wp_bf, pool_scale)
    attn_o, probs, sink_share = _attn_fwd(q, k, v, bias, sinks)
    g2_fwd = g2
    if callable(w_out):
        w_out, after = w_out(pool_o, attn_o)
        if after is not None:
            g2_fwd = g2 + after[:1, :1]
    mix, x1, h2 = _outproj_fwd(pool_o, attn_o, w_out, x, g2_fwd, g3, mid_outproj)
    wg, wu, wd = ffn_weights(h2) if callable(ffn_weights) else ffn_weights
    gate, up, act_a, df, dy, loss, gg4 = _ffn_fwd(h2, wg, wu, wd, x1, target, g4)
    dgate, dup, dx1, dmix, gg3, gg2 = _ffn_bwd_act(df, gate, up, wg, wu, wd, dy, x1, mix, g3, g2)
    ffn_g = _ffn_bwd_w(h2, act_a, dgate, dup, df, c_arr)
    dep = None if on_ffn_grads is None else on_ffn_grads(ffn_g)
    dpool, dattn, wout_own, wout_oth = _outproj_bwd(dmix, w_out, pool_o, attn_o, c_arr, dep)
    dep = None if on_wout_grads is None else on_wout_grads(wout_own, wout_oth, dpool)
    du, gwp, gsc = _pool_bwd(pooled, dpool, wp_bf, pool_scale, dep)
    dq, dk, dv, grb, gsk = _attn_bwd(q, k, v, dattn, probs, sink_share, bucket, dep)
    gx, win_own, win_oth, gg1 = _inproj_bwd(x, g1, du, dq, dk, dv, w_in, dx1, c_arr)
    small_grads = (gg1, gg2, gg3, gg4, gwp, gsc, grb, gsk[:, 0].reshape(1, NQ))
    return loss, gx, small_grads, ((win_own, win_oth), (wout_own, wout_oth), ffn_g)


def _place():
    x, y, c = lax.axis_index("x"), lax.axis_index("y"), lax.axis_index("c")
    chips = ((1 - x, y), (x, 1 - y), (1 - x, 1 - y))
    return x, y, c, chips


def _remote(src, dst, ssem, rsem, dev):
    return pltpu.make_async_remote_copy(src_ref=src, dst_ref=dst, send_sem=ssem, recv_sem=rsem,
                                        device_id=dev, device_id_type=MESH_T)


def _to_sibling(arrs, name, after=()):
    nt, na = len(arrs), len(after)

    def body(*refs):
        srcs, dsts = refs[:nt], refs[nt + na:2 * nt + na]
        ssem, rsem = refs[2 * nt + na:]
        x, y, c, _ = _place()
        cps = [_remote(srcs[t], dsts[t], ssem.at[t], rsem.at[t], (x, y, 1 - c)) for t in range(nt)]
        for cp in cps:
            cp.start()
        for cp in cps:
            cp.wait()

    return pl.pallas_call(
        body, name=name, in_specs=[ANY] * (nt + na), out_specs=[ANY] * nt,
        out_shape=[jax.ShapeDtypeStruct(a.shape, a.dtype) for a in arrs],
        scratch_shapes=[pltpu.SemaphoreType.DMA((nt,)), pltpu.SemaphoreType.DMA((nt,))],
        compiler_params=_cp())(*arrs, *after)


HBM_SPEC = pl.BlockSpec(memory_space=pltpu.HBM)
SEM_SPEC = pl.BlockSpec(memory_space=pltpu.SEMAPHORE)
DATAFLOW = pltpu.SideEffectType.DATAFLOW_SIDE_EFFECTING


def _cast_into_slots(ws, chip_arr):
    nt, tiles = len(ws), 4

    def body(chip_ref, *refs):
        for t in range(nt):
            refs[nt + t][...] = refs[t][...].astype(BF16)

    return pl.pallas_call(
        body, name="cast_weights",
        grid_spec=pltpu.PrefetchScalarGridSpec(
            num_scalar_prefetch=1, grid=(tiles,),
            in_specs=[pl.BlockSpec((w.shape[0] // tiles, w.shape[1]), lambda i, chip_ref: (i, 0)) for w in ws],
            out_specs=[pl.BlockSpec((None, w.shape[0] // tiles, w.shape[1]), lambda i, chip_ref: (chip_ref[0], i, 0))
                       for w in ws]),
        out_shape=[jax.ShapeDtypeStruct((NSH,) + w.shape, BF16) for w in ws],
        compiler_params=_cp(1))(chip_arr, *ws)


def _gather_copies(srcs, lands, ssem, rsem, first=0):
    x, y, c, chips = _place()
    me = 2 * x + y
    out = []
    for t in range(len(lands)):
        half = lands[t].shape[1] // 2
        mine = pl.ds(pl.multiple_of(c * half, 16), half)
        for j, (px, py) in enumerate(chips):
            k = 3 * (first + t) + j
            send = _remote(lands[t].at[me, mine], lands[t].at[me, mine], ssem.at[k], rsem.at[k], (px, py, c))
            blk = lands[t].at[2 * px + py, mine]
            out.append((send, _remote(blk, blk, ssem.at[k], rsem.at[k], (px, py, c))))
    return out


def _scatter_copies(srcs, lands, ssem, rsem):
    x, y, c, chips = _place()
    out = []
    for t in range(len(srcs)):
        for j, (px, py) in enumerate(chips):
            k = 3 * t + j
            send = _remote(srcs[t].at[2 * px + py], lands[t].at[j], ssem.at[k], rsem.at[k], (px, py, c))
            out.append((send, _remote(lands[t].at[j], lands[t].at[j], ssem.at[k], rsem.at[k], (px, py, c))))
    return out


def _sibling_copies(srcs, lands, ssem, rsem):
    x, y, c, _ = _place()
    sib = (x, y, 1 - c)
    return [(_remote(srcs[t], lands[t], ssem.at[t], rsem.at[t], sib),
             _remote(lands[t], lands[t], ssem.at[t], rsem.at[t], sib)) for t in range(len(srcs))]


def _peer_copies(srcs, lands, ssem, rsem):
    x, y, c, _ = _place()
    me = 4 * x + 2 * y + c
    out = []
    for kk in range(1, 8):
        px, py, pc = x ^ (kk >> 2), y ^ ((kk >> 1) & 1), c ^ (kk & 1)
        send = _remote(srcs[0], lands[0].at[me], ssem.at[kk - 1], rsem.at[kk - 1], (px, py, pc))
        slot = lands[0].at[4 * px + 2 * py + pc]
        out.append((send, _remote(slot, slot, ssem.at[kk - 1], rsem.at[kk - 1], (px, py, pc))))
    return out


def _forward_copies(srcs, lands, ssem, rsem):
    x, y, c, chips = _place()
    sib = (x, y, 1 - c)
    out = []
    for t in range(len(lands)):
        half = lands[t].shape[1] // 2
        mine = pl.ds(pl.multiple_of(c * half, 16), half)
        other = pl.ds(pl.multiple_of((1 - c) * half, 16), half)
        for j, (px, py) in enumerate(chips):
            k = 3 * t + j
            blk_m, blk_o = lands[t].at[2 * px + py, mine], lands[t].at[2 * px + py, other]
            out.append((_remote(blk_m, blk_m, ssem.at[k], rsem.at[k], sib),
                        _remote(blk_o, blk_o, ssem.at[k], rsem.at[k], sib)))
    return out


def _win_and_small_copies(srcs, lands, ssem, rsem):
    return (_scatter_copies(srcs[:1], lands[:1], ssem, rsem)
            + _peer_copies(srcs[1:], lands[1:], ssem.at[pl.ds(3, 7)], rsem.at[pl.ds(3, 7)]))


def _split_start(plan, ncopy, srcs, lands, after, name):
    ns, nbuf = len(srcs), len(srcs) + len(lands)
    extra = [] if after is None else [after]
    n_in = nbuf + len(extra)

    def body(*refs):
        ssem, rsem, token = refs[n_in], refs[n_in + 1], refs[-1]
        for send, _ in plan(refs[:ns], refs[ns:nbuf], ssem, rsem):
            send.start()
        token[...] = jnp.zeros_like(token)

    bufs = [pltpu.with_memory_space_constraint(a, pltpu.HBM) for a in list(srcs) + list(lands)]
    outs = pl.pallas_call(
        body, name=name, in_specs=[HBM_SPEC] * nbuf + [ANY] * len(extra),
        out_specs=[SEM_SPEC, SEM_SPEC] + [HBM_SPEC] * nbuf + [pl.BlockSpec(memory_space=pltpu.VMEM)],
        out_shape=[pltpu.SemaphoreType.DMA((ncopy,)), pltpu.SemaphoreType.DMA((ncopy,))]
        + [pltpu.HBM(a.shape, a.dtype) for a in bufs] + [jax.ShapeDtypeStruct((8, 128), F32)],
        input_output_aliases={i: 2 + i for i in range(nbuf)},
        compiler_params=pltpu.CompilerParams(has_side_effects=DATAFLOW))(*bufs, *extra)
    return outs[0], outs[1], outs[2:2 + ns], outs[2 + ns:2 + nbuf], outs[-1]


def _split_wait(plan, ssem, rsem, srcs, lands, after, name, only=None):
    ns, nbuf = len(srcs), len(srcs) + len(lands)

    def body(*refs):
        s_ref, r_ref = refs[nbuf], refs[nbuf + 1]
        for k, (send, recv) in enumerate(plan(refs[:ns], refs[ns:nbuf], s_ref, r_ref)):
            if only is None or k in only:
                send.wait_send()
                recv.wait_recv()

    outs = pl.pallas_call(
        body, name=name, in_specs=[HBM_SPEC] * nbuf + [SEM_SPEC, SEM_SPEC] + [ANY] * len(after),
        out_specs=[HBM_SPEC] * nbuf,
        out_shape=[pltpu.HBM(a.shape, a.dtype) for a in list(srcs) + list(lands)],
        input_output_aliases={i: i for i in range(nbuf)},
        compiler_params=pltpu.CompilerParams(has_side_effects=DATAFLOW))(*srcs, *lands, ssem, rsem, *after)
    return outs


def _gather_finish(lands, tag):
    nt = len(lands)

    def body(*refs):
        outs = refs[nt:2 * nt]
        dsend, drecv = refs[2 * nt:]
        x, y, c, chips = _place()
        sib = (x, y, 1 - c)
        sends = []
        for t in range(nt):
            half = outs[t].shape[1] // 2
            mine = pl.ds(pl.multiple_of(c * half, 16), half)
            for j, (px, py) in enumerate(chips):
                blk = outs[t].at[2 * px + py, mine]
                cp = _remote(blk, blk, dsend.at[t, j], drecv.at[t, j], sib)
                cp.start()
                sends.append(cp)
        for t in range(nt):
            half = outs[t].shape[1] // 2
            other = pl.ds(pl.multiple_of((1 - c) * half, 16), half)
            for j, (px, py) in enumerate(chips):
                blk = outs[t].at[2 * px + py, other]
                _remote(blk, blk, dsend.at[t, j], drecv.at[t, j], sib).wait_recv()
        for cp in sends:
            cp.wait_send()

    return pl.pallas_call(
        body, name="gather_finish_" + tag, in_specs=[ANY] * nt, out_specs=[ANY] * nt,
        out_shape=[jax.ShapeDtypeStruct(a.shape, a.dtype) for a in lands],
        input_output_aliases={t: t for t in range(nt)},
        scratch_shapes=[pltpu.SemaphoreType.DMA((nt, 3)), pltpu.SemaphoreType.DMA((nt, 3))],
        compiler_params=_cp())(*lands)


def _chip_partial(owns, recv, chip_arr, tag, whole=False):
    nt = len(owns)

    if whole:
        def body_whole(chip_ref, *refs):
            for t in range(nt):
                refs[2 * nt + t][...] = (refs[t][...] + refs[nt + t][...].astype(F32)).astype(BF16)
                mine = chip_ref[0]
                refs[3 * nt + t][...] = refs[t][mine] + refs[nt + t][mine].astype(F32)

        vm = pl.BlockSpec(memory_space=pltpu.VMEM)
        return pl.pallas_call(
            body_whole, name="rs_chip_partial_" + tag, in_specs=[SMEM_SPEC] + [vm] * (2 * nt), out_specs=[vm] * (2 * nt),
            out_shape=[jax.ShapeDtypeStruct(a.shape, BF16) for a in owns]
            + [jax.ShapeDtypeStruct(a.shape[1:], F32) for a in owns],
            compiler_params=_cp())(chip_arr, *owns, *recv)

    def body(chip_ref, *refs):
        k = pl.program_id(0)
        for t in range(nt):
            p = refs[t][...] + refs[nt + t][...].astype(F32)
            refs[2 * nt + t][...] = p.astype(BF16)

            @pl.when(k == chip_ref[0])
            def _():
                refs[3 * nt + t][...] = p

    blk_specs = [pl.BlockSpec((None,) + a.shape[1:], lambda k, chip_ref: (k, 0, 0)) for a in owns]
    own_specs = [pl.BlockSpec(a.shape[1:], lambda k, chip_ref: (0, 0)) for a in owns]
    return pl.pallas_call(
        body, name="rs_chip_partial_" + tag,
        grid_spec=pltpu.PrefetchScalarGridSpec(num_scalar_prefetch=1, grid=(NSH,), in_specs=blk_specs * 2,
                                               out_specs=blk_specs + own_specs),
        out_shape=[jax.ShapeDtypeStruct(a.shape, BF16) for a in owns]
        + [jax.ShapeDtypeStruct(a.shape[1:], F32) for a in owns],
        compiler_params=_cp(1))(chip_arr, *owns, *recv)


def _sum_chips(own, recv, tag):
    nt = len(own)

    def body(*refs):
        outs = refs[2 * nt:]
        for t in range(nt):
            r = refs[nt + t]
            outs[t][...] = ((refs[t][...] + r[0].astype(F32)) + r[1].astype(F32)) + r[2].astype(F32)

    vm = pl.BlockSpec(memory_space=pltpu.VMEM)
    return pl.pallas_call(
        body, name="rs_sum_chips_" + tag, in_specs=[vm] * (2 * nt), out_specs=[vm] * nt,
        out_shape=[jax.ShapeDtypeStruct(a.shape, F32) for a in own],
        compiler_params=_cp())(*own, *recv)


def _sum_chips_shared(own, recv, tag):
    def body(own_ref, recv_ref, out_ref, sib_ref, ssem, rsem):
        out_ref[...] = ((own_ref[...] + recv_ref[0].astype(F32)) + recv_ref[1].astype(F32)) + recv_ref[2].astype(F32)
        x, y, c, _ = _place()
        cp = _remote(out_ref, sib_ref, ssem.at[0], rsem.at[0], (x, y, 1 - c))
        cp.start()
        cp.wait()

    vm = pl.BlockSpec(memory_space=pltpu.VMEM)
    return pl.pallas_call(
        body, name="rs_sum_share_" + tag, in_specs=[vm, vm], out_specs=[vm, ANY],
        out_shape=[jax.ShapeDtypeStruct(own.shape, F32)] * 2,
        scratch_shapes=[pltpu.SemaphoreType.DMA((1,)), pltpu.SemaphoreType.DMA((1,))],
        compiler_params=_cp())(own, recv)


def _adamw_math(w, g, m, v):
    m = ADAM_B1 * m + (1.0 - ADAM_B1) * g
    v = ADAM_B2 * v + (1.0 - ADAM_B2) * (g * g)
    m_hat = m / (1.0 - ADAM_B1 ** ADAM_STEP)
    v_hat = v / (1.0 - ADAM_B2 ** ADAM_STEP)
    delta = -ADAM_LR * (m_hat / (jnp.sqrt(v_hat) + ADAM_EPS) + ADAM_WD * w)
    return delta, m, v


ADAM_SPLIT = 4


def _adamw_shards(own, sib, ws, ms, vs, c_arr, tag):
    nt = len(own)

    def body(c_ref, *refs):
        hh = pl.program_id(0)
        mine = hh == c_ref[0]
        for t in range(nt):
            g = jnp.where(mine, refs[t][...], refs[nt + t][...])
            delta, m, v = _adamw_math(refs[2 * nt + t][...], g, refs[3 * nt + t][...], refs[4 * nt + t][...])
            refs[5 * nt + t][...] = g
            refs[6 * nt + t][...] = delta
            refs[7 * nt + t][...] = m
            refs[8 * nt + t][...] = v

    def tile(a):
        return (a.shape[0] // ADAM_SPLIT, a.shape[1])

    def half_index(used_when_mine):
        def index(hh, q, c_ref):
            mine = 1 - (hh - c_ref[0]) * (hh - c_ref[0])
            used = mine if used_when_mine else 1 - mine
            return (used * q + (1 - used) * hh * (ADAM_SPLIT - 1), 0)
        return index

    own_specs = [pl.BlockSpec(tile(a), half_index(True)) for a in own]
    sib_specs = [pl.BlockSpec(tile(a), half_index(False)) for a in own]
    full_specs = [pl.BlockSpec(tile(a), lambda hh, q, c_ref: (hh * ADAM_SPLIT + q, 0)) for a in own]
    return pl.pallas_call(
        body, name="adamw_shards_" + tag,
        grid_spec=pltpu.PrefetchScalarGridSpec(num_scalar_prefetch=1, grid=(2, ADAM_SPLIT),
                                               in_specs=own_specs + sib_specs + full_specs * 3,
                                               out_specs=full_specs * 4),
        out_shape=[jax.ShapeDtypeStruct(w.shape, F32) for w in ws] * 4,
        compiler_params=_cp(2))(c_arr, *own, *sib, *ws, *ms, *vs)


SMALL_WPOOL_ROW = 32
SMALL_SCALE_ROW = SMALL_WPOOL_ROW + 4 * GROUP
SMALL_BIAS_ROW = SMALL_SCALE_ROW + 8
SMALL_SINKS_ROW = SMALL_BIAS_ROW + NQ
SMALL_LOSS_ROW = SMALL_SINKS_ROW + 8


def _small_sum_adamw(gathered, wp, mp, vp):
    rows = wp.shape[0]

    def body(g_ref, w_ref, m_ref, v_ref, *rest):
        outs, res = rest[:-1], rest[-1]
        g = g_ref[0]
        for d in range(1, 8):
            g = g + g_ref[d]
        delta, m, v = _adamw_math(w_ref[...], g, m_ref[...], v_ref[...])
        for kind, val in enumerate((g, delta, m, v)):
            res[kind] = val
            gains = outs[8 * kind:8 * kind + 4]
            w_pool_o, scale_o, bias_o, sinks_o = outs[8 * kind + 4:8 * kind + 8]
            for t in range(4):
                for i in range(8):
                    gains[t][:, 128 * i:128 * (i + 1)] = res[kind, pl.ds(8 * t + i, 1), :]
            for grp in range(4):
                w_pool_o[grp] = res[kind, pl.ds(SMALL_WPOOL_ROW + GROUP * grp, GROUP), :]
            for i in range(4):
                scale_o[:, 128 * i:128 * (i + 1)] = res[kind, pl.ds(SMALL_SCALE_ROW + i, 1), :]
            bias_o[...] = res[kind, pl.ds(SMALL_BIAS_ROW, NQ), :][:, 0:NBUCKET]
            sinks_o[...] = res[kind, pl.ds(SMALL_SINKS_ROW, 1), :][:, 0:NQ]
        outs[-1][...] = res[0, pl.ds(SMALL_LOSS_ROW, 1), :]

    vm = pl.BlockSpec(memory_space=pltpu.VMEM)
    one_kind = [jax.ShapeDtypeStruct((1, D), F32)] * 4 + [
        jax.ShapeDtypeStruct((4, GROUP, GROUP), F32), jax.ShapeDtypeStruct((1, POOL_W), F32),
        jax.ShapeDtypeStruct((NQ, NBUCKET), F32), jax.ShapeDtypeStruct((1, NQ), F32)]
    return pl.pallas_call(
        body, name="small_sum_adamw", in_specs=[vm] * 4, out_specs=[vm] * 33,
        out_shape=one_kind * 4 + [jax.ShapeDtypeStruct((1, 128), F32)],
        scratch_shapes=[pltpu.VMEM((4, rows, 128), F32)],
        compiler_params=_cp())(gathered, wp, mp, vp)


def _pack_small(gains4, w_pool, pool_scale, rel_bias_t, sinks, loss):
    bias_rows = jnp.pad(rel_bias_t, ((0, 0), (0, 128 - rel_bias_t.shape[1])))
    parts = list(gains4) + [w_pool, pool_scale, bias_rows, sinks, loss]
    rows = []
    for a in parts:
        flat = a.reshape(-1)
        n = flat.shape[0]
        padded = -(-n // 1024) * 1024
        rows.append(jnp.pad(flat, (0, padded - n)).reshape(-1, 128))
    return jnp.concatenate(rows, axis=0)


def kernel(x, g_pre_mix, w_in, w_pool, pool_scale, rel_bias, sinks, w_out, g_post_mix, g_pre_ffn, w_gate, w_up, w_down, g_post_ffn, loss_target, m_g_pre_mix, m_w_in, m_w_pool, m_pool_scale, m_rel_bias, m_sinks, m_w_out, m_g_post_mix, m_g_pre_ffn, m_w_gate, m_w_up, m_w_down, m_g_post_ffn, v_g_pre_mix, v_w_in, v_w_pool, v_pool_scale, v_rel_bias, v_sinks, v_w_out, v_g_post_mix, v_g_pre_ffn, v_w_gate, v_w_up, v_w_down, v_g_post_ffn):
    c = lax.axis_index("c")
    chip = 2 * lax.axis_index("x") + lax.axis_index("y")

    def shards(a_in, a_out, a_gate, a_up, a_down):
        return (a_in[0].T, a_out[0], a_gate[0].T, a_up[0].T, a_down[0])

    c_arr = c.reshape(1).astype(jnp.int32)
    chip_arr = chip.reshape(1).astype(jnp.int32)

    big_w = shards(w_in, w_out, w_gate, w_up, w_down)
    g_ssem, g_rsem, _, g_lands, g_token = _split_start(
        _gather_copies, 15, [], _cast_into_slots(big_w, chip_arr), None, "gather_start")
    fw = {}

    def w_in_ready(*after):
        fw["first"] = _split_wait(_gather_copies, g_ssem, g_rsem, [], g_lands, after, "gather_win_wait",
                                  only=(0, 1, 2))
        (fw["win"],) = _gather_finish([fw["first"][0]], "win")
        return fw["win"].reshape(IN_W, D)

    def w_out_ready(*after):
        fw["second"] = _split_wait(functools.partial(_gather_copies, first=1), g_ssem, g_rsem, [],
                                   list(fw["first"][1:]), after, "gather_wout_wait", only=(0, 1, 2))
        (fw["wout"],) = _gather_finish([fw["second"][0]], "wout")
        return fw["wout"].reshape(D, D), None

    def ffn_forward_start(after):
        outs = _split_wait(functools.partial(_gather_copies, first=2), g_ssem, g_rsem, [], list(fw["second"][1:]),
                           [after], "gather_ffn_wait")
        fw["f"] = _split_start(_forward_copies, 9, [], list(outs), None, "gather_forward_start")
        return fw["f"][4]

    def ffn_weights(after):
        ssem, rsem, _, lands, _ = fw["f"]
        return _split_wait(_forward_copies, ssem, rsem, [], lands, [after], "gather_forward_wait")

    rs = {}

    def on_ffn_grads(ffn_g):
        oths = ffn_g[1::2]
        rs["ffn_own"] = ffn_g[0::2]
        rs["x"] = _split_start(_sibling_copies, 3, oths, [lax.empty(a.shape, BF16) for a in oths], ffn_g[0],
                               "rs_exchange_ffn_start")
        return rs["x"][4]

    def on_wout_grads(own, oth, after):
        ssem, rsem, srcs, lands, _ = rs["x"]
        recv_ffn = _split_wait(_sibling_copies, ssem, rsem, srcs, lands, [after], "rs_exchange_ffn_wait")[3:]
        recv_wout = _to_sibling([oth], "rs_exchange_wout")
        outs = _chip_partial([own] + list(rs["ffn_own"]), list(recv_wout) + list(recv_ffn), chip_arr, "early")
        rs["early_own"] = outs[4:]
        rs["s"] = _split_start(_scatter_copies, 12, outs[:4],
                               [lax.empty((3,) + a.shape[1:], BF16) for a in outs[:4]], outs[4], "scatter_early_start")
        return rs["s"][4]

    small = (g_pre_mix, g_post_mix, g_pre_ffn, g_post_ffn, w_pool[0], pool_scale, rel_bias, sinks)
    loss, gx, small_grads, ((win_own, win_oth), _, _) = _local_step(
        x[0], loss_target[0], small, w_in_ready, w_out_ready, ffn_weights, c_arr, on_ffn_grads, on_wout_grads,
        ffn_forward_start, g_token)

    ssem, rsem, srcs, lands, _ = rs["s"]
    recv_early = _split_wait(_scatter_copies, ssem, rsem, srcs, lands, [gx], "scatter_early_wait")[4:]
    finals = _sum_chips(list(rs["early_own"]), list(recv_early), "early")
    to_sib = [win_oth] + list(finals)
    sh_ssem, sh_rsem, sh_srcs, sh_lands, _ = _split_start(
        _sibling_copies, 5, to_sib, [lax.empty(a.shape, a.dtype) for a in to_sib], None, "rs_share_start")

    unused = jnp.zeros((1, 1), F32)
    gp = _pack_small(small_grads[:4], *small_grads[4:], loss)
    wp = _pack_small((g_pre_mix, g_post_mix, g_pre_ffn, g_post_ffn), w_pool, pool_scale, rel_bias.T, sinks, unused)
    mp = _pack_small((m_g_pre_mix, m_g_post_mix, m_g_pre_ffn, m_g_post_ffn), m_w_pool, m_pool_scale, m_rel_bias.T,
                     m_sinks, unused)
    vp = _pack_small((v_g_pre_mix, v_g_post_mix, v_g_pre_ffn, v_g_post_ffn), v_w_pool, v_pool_scale, v_rel_bias.T,
                     v_sinks, unused)

    moments = (shards(m_w_in, m_w_out, m_w_gate, m_w_up, m_w_down),
               shards(v_w_in, v_w_out, v_w_gate, v_w_up, v_w_down))
    sh_first = _split_wait(_sibling_copies, sh_ssem, sh_rsem, sh_srcs, sh_lands, [], "rs_share_win_wait", only=(0,))
    outs = _chip_partial([win_own], [sh_first[5]], chip_arr, "win", whole=True)
    slots = lax.dynamic_update_slice(lax.empty((8,) + gp.shape, F32), gp[None], (2 * chip + c, 0, 0))
    w_ssem, w_rsem, w_srcs, w_lands, w_token = _split_start(
        _win_and_small_copies, 10, [outs[0], gp], [lax.empty((3,) + outs[0].shape[1:], BF16), slots], gx,
        "scatter_win_start")
    shared = _split_wait(_sibling_copies, sh_ssem, sh_rsem, sh_first[:5], sh_first[5:], [w_token],
                         "rs_share_early_wait", only=(1, 2, 3, 4))
    adam_e = _adamw_shards(shared[1:5], shared[6:], big_w[1:], moments[0][1:], moments[1][1:], c_arr, "early")
    w_first = _split_wait(_win_and_small_copies, w_ssem, w_rsem, w_srcs, w_lands, [adam_e[0]], "scatter_win_wait",
                          only=(0, 1, 2))
    win_final, win_sib = _sum_chips_shared(outs[1], w_first[2], "win")
    adam_w = _adamw_shards([win_final], [win_sib], big_w[:1], moments[0][:1], moments[1][:1], c_arr, "win")
    big_g, big_d, big_m, big_v = [[adam_w[i]] + list(adam_e[4 * i:4 * i + 4]) for i in range(4)]

    gathered = _split_wait(_win_and_small_copies, w_ssem, w_rsem, w_first[:2], w_first[2:], [adam_w[0]],
                           "small_allgather_wait", only=tuple(range(3, 10)))[3]
    flat = _small_sum_adamw(gathered, wp, mp, vp)
    small_out = [flat[8 * kind:8 * kind + 8] for kind in range(4)]
    total_loss = flat[-1][0, 0]

    def ordered(small8, big4):
        sg1, sg2, sg3, sg4, swp, ssc, srb, ssk = small8
        swp, srb = swp[None], srb.T
        bwin, bwout, bwg, bwu, bwd = big4[0].T[None], big4[1][None], big4[2].T[None], big4[3].T[None], big4[4][None]
        return [sg1, bwin, swp, ssc, srb, ssk, bwout, sg2, sg3, bwg, bwu, bwd, sg4]

    res = [total_loss, gx[None]]
    for sm, bg in zip(small_out, (big_g, big_d, big_m, big_v)):
        res += ordered(sm, bg)
    return tuple(res)
```

```python
import functools

import numpy as np
import jax
import jax.numpy as jnp
from jax import lax
from jax.experimental import pallas as pl
from jax.experimental.pallas import tpu as pltpu

F32 = jnp.float32
BF16 = jnp.bfloat16

D = 1024
POOL_W = 512
GROUP = 128
WINDOWS = (2, 4, 8, 16)
HALO = 128
NQ = 8
BLK = 128
NBUCKET = 32
IN_W = 1280
DFF = 2816
NSH = 4
FS = DFF // NSH
WIN_S = IN_W // NSH
WOUT_S = D // NSH
EPS = 1e-6
NEG = -1e30
SCALE = 0.125

ADAM_LR = 0.001
ADAM_B1 = 0.9
ADAM_B2 = 0.999
ADAM_EPS = 1e-08
ADAM_WD = 0.01
ADAM_STEP = 10

TM = 512
TM_IN = 1024
TM_POOL = 1024
TM_FFN = 512
TM_FFN_FWD = 1024
TM_FFN_W = 2048
FFN_ROWS = 256
VMEM_LIMIT = 60 * 1024 * 1024

MESH_T = pl.DeviceIdType.MESH
ANY = pl.BlockSpec(memory_space=pl.ANY)


def _cp(n_grid=0, **kw):
    sem = ("arbitrary",) * n_grid if n_grid else None
    return pltpu.CompilerParams(dimension_semantics=sem, vmem_limit_bytes=VMEM_LIMIT, **kw)


def _dot(a, b):
    return jnp.dot(a, b, preferred_element_type=F32)


def _dot_nt(a, b):
    return lax.dot_general(a, b, (((1,), (1,)), ((), ())), preferred_element_type=F32)


def _dot_tn(a, b):
    return lax.dot_general(a, b, (((0,), (0,)), ((), ())), preferred_element_type=F32)


def _rms(x):
    r = lax.rsqrt(jnp.mean(x * x, axis=-1, keepdims=True) + EPS)
    return r, x * r


def _rms_bwd(r, n, g, dout):
    dn = dout * g
    dx = r * (dn - n * jnp.mean(dn * n, axis=-1, keepdims=True))
    dg = jnp.sum(dout * n, axis=0, keepdims=True)
    return dx, dg


def _split(x, n):
    parts = []
    r = x
    for _ in range(n):
        p = r.astype(BF16)
        parts.append(p)
        r = r - p.astype(F32)
    return parts


def _band_dot(a, x, n):
    acc = None
    for p in _split(x, n):
        t = _dot(a, p)
        acc = t if acc is None else acc + t
    return acc


def _bucket_table():
    qi = np.arange(BLK)[None, :]
    kj = np.arange(2 * BLK)[:, None]
    dist = qi + BLK - kj
    n = np.maximum(dist, 0)
    nf = np.maximum(n, 1).astype(np.float32)
    large = 16 + (np.log(nf / np.float32(16)) / np.float32(np.log(128 / 16)) * np.float32(16)).astype(np.int32)
    large = np.minimum(large, NBUCKET - 1)
    return np.where(n < 16, n, large).astype(np.int32)


def _prenorm(x, g1, dep=None):
    s = x.shape[0]
    tm = min(TM_IN, s)

    def body(x_ref, g_ref, *rest):
        _, n = _rms(x_ref[...])
        rest[-1][...] = (n * g_ref[...]).astype(BF16)

    row = pl.BlockSpec((tm, D), lambda i: (i, 0))
    return pl.pallas_call(
        body, name="prenorm", grid=(s // tm,),
        in_specs=[row, pl.BlockSpec((1, D), lambda i: (0, 0))] + ([] if dep is None else [ANY]), out_specs=row,
        out_shape=jax.ShapeDtypeStruct((s, D), BF16),
        compiler_params=_cp(1))(x, g1, *([] if dep is None else [dep]))


def _inproj_fwd(h1, w_in):
    s = h1.shape[0]
    tm = min(TM_IN, s)

    def body(h_ref, w_ref, u_ref, q_ref, k_ref, v_ref):
        proj = _dot_nt(h_ref[...], w_ref[...])
        u_ref[...] = proj[:, :512]
        q_ref[...] = proj[:, 512:1024].astype(BF16)
        k_ref[...] = proj[:, 1024:1152].astype(BF16)
        v_ref[...] = proj[:, 1152:1280].astype(BF16)

    row = lambda w: pl.BlockSpec((tm, w), lambda i: (i, 0))
    return pl.pallas_call(
        body, name="inproj_fwd", grid=(s // tm,),
        in_specs=[row(D), pl.BlockSpec((IN_W, D), lambda i: (0, 0))],
        out_specs=[row(512), row(512), row(128), row(128)],
        out_shape=[jax.ShapeDtypeStruct((s, 512), F32), jax.ShapeDtypeStruct((s, 512), BF16),
                   jax.ShapeDtypeStruct((s, 128), BF16), jax.ShapeDtypeStruct((s, 128), BF16)],
        compiler_params=_cp(1))(h1, w_in)


def _window_sums(ext, w, lag_sign, tm, terms):
    rows = lax.broadcasted_iota(jnp.int32, (HALO, 2 * HALO), 0)
    cols = lax.broadcasted_iota(jnp.int32, (HALO, 2 * HALO), 1)
    d = rows + HALO - cols if lag_sign > 0 else cols - rows
    band = jnp.where((d >= 0) & (d < w), 1.0, 0.0).astype(BF16)
    return jnp.concatenate([_band_dot(band, ext[r:r + 2 * HALO], terms) for r in range(0, tm, HALO)], axis=0)


def _pooled(ext, cur, t0, tm):
    t = t0 + lax.broadcasted_iota(jnp.int32, (tm, GROUP), 0)
    out = []
    for g, w in enumerate(WINDOWS):
        sl = slice(g * GROUP, (g + 1) * GROUP)
        cnt = jnp.minimum(t + 1, w).astype(F32)
        out.append(_window_sums(ext[:, sl], w, 1, tm, 2) / cnt - cur[:, sl])
    return out


def _pool_fwd(u, w_pool, pool_scale):
    s = u.shape[0]
    tm = min(TM_POOL, s)
    hb = tm // HALO

    def body(uc_ref, uh_ref, wp_ref, sc_ref, o_ref, pooled_ref):
        i = pl.program_id(0)
        cur = uc_ref[...]
        halo = jnp.where(i > 0, uh_ref[...], 0.0)
        ext = jnp.concatenate([halo, cur], axis=0)
        pooled = _pooled(ext, cur, i * tm, tm)
        for g in range(4):
            sl = slice(g * GROUP, (g + 1) * GROUP)
            pb = pooled[g].astype(BF16)
            pooled_ref[:, sl] = pb
            o_ref[:, sl] = (_dot(pb, wp_ref[g]) * sc_ref[:, sl]).astype(BF16)

    row = pl.BlockSpec((tm, 512), lambda i: (i, 0))
    return pl.pallas_call(
        body, name="pool_fwd", grid=(s // tm,),
        in_specs=[row, pl.BlockSpec((HALO, 512), lambda i: (jnp.maximum(i * hb - 1, 0), 0)),
                  pl.BlockSpec((4, GROUP, GROUP), lambda i: (0, 0, 0)),
                  pl.BlockSpec((1, 512), lambda i: (0, 0))],
        out_specs=[row, row],
        out_shape=[jax.ShapeDtypeStruct((s, 512), BF16)] * 2,
        compiler_params=_cp(1))(u, u, w_pool, pool_scale)


def _bias_table(bucket, rel_bias):
    def body(b_ref, rb_ref, o_ref):
        bucket_v = b_ref[...]
        key = lax.broadcasted_iota(jnp.int32, (2 * BLK, BLK), 0)
        dist = lax.broadcasted_iota(jnp.int32, (2 * BLK, BLK), 1) + BLK - key
        inwin = (dist >= 0) & (dist < BLK)
        for h in range(NQ):
            acc = jnp.zeros((2 * BLK, BLK), F32)
            for b in range(NBUCKET):
                acc = jnp.where(bucket_v == b, rb_ref[b, h], acc)
            rest = jnp.where(inwin, acc, NEG)
            o_ref[1, h] = rest
            o_ref[0, h] = jnp.where(key >= BLK, rest, NEG)

    return pl.pallas_call(
        body, name="bias_table",
        in_specs=[pl.BlockSpec(memory_space=pltpu.VMEM), pl.BlockSpec(memory_space=pltpu.SMEM)],
        out_specs=pl.BlockSpec(memory_space=pltpu.VMEM),
        out_shape=jax.ShapeDtypeStruct((2, NQ, 2 * BLK, BLK), F32),
        compiler_params=_cp())(bucket, rel_bias)


HD = 64


def _kv_band(ref, n, transposed):
    pstart = pl.multiple_of(jnp.maximum(n - 1, 0) * BLK, BLK)
    cstart = pl.multiple_of(n * BLK, BLK)
    lo = lax.broadcasted_iota(jnp.int32, (2 * BLK, 128), 1) < HD
    band = jnp.concatenate([ref[pl.ds(pstart, BLK), :], ref[pl.ds(cstart, BLK), :]], axis=0).astype(F32)
    rot = pltpu.roll(band, HD, axis=1)
    halves = ((jnp.where(lo, band, 0.0), jnp.where(lo, 0.0, rot)), (jnp.where(lo, rot, 0.0), jnp.where(lo, 0.0, band)))
    if transposed:
        out = [jnp.concatenate([a.T, b.T], axis=1).astype(BF16) for a, b in halves]
    else:
        out = [jnp.concatenate([a, b], axis=0).astype(BF16) for a, b in halves]
    return out, (lo, pstart, cstart)


def _pair_probs(qt, kk, bias, sk_ref, p):
    sink = jnp.concatenate([jnp.full((1, 1, BLK), sk_ref[0, 2 * p + e], F32) for e in range(2)], axis=0)
    s = _dot_nt(kk, qt).reshape(2, 2 * BLK, BLK) + bias
    m = jnp.maximum(jnp.max(s, axis=1, keepdims=True), sink)
    pr = jnp.exp(s - m)
    es = jnp.exp(sink - m)
    inv = 1.0 / (jnp.sum(pr, axis=1, keepdims=True) + es)
    return pr * inv, es * inv


def _attn_fwd(q, k, v, bias, sinks):
    s = q.shape[0]
    nblk = s // BLK

    def body(q_ref, k_ref, v_ref, b_ref, sk_ref, o_ref, prob_ref, ps_ref):
        n = pl.program_id(0)
        kk, _ = _kv_band(k_ref, n, False)
        vt, _ = _kv_band(v_ref, n, True)
        bidx = jnp.minimum(n, 1)
        for p in range(4):
            h = p // 2
            qt = (q_ref[:, p * 128:(p + 1) * 128].astype(F32) * SCALE).astype(BF16)
            prob, ps = _pair_probs(qt, kk[h], b_ref[bidx, pl.ds(2 * p, 2)], sk_ref, p)
            pb = prob.astype(BF16)
            prob_ref[pl.ds(2 * p, 2)] = pb
            ps_ref[pl.ds(2 * p, 2), :] = ps.reshape(2, BLK)
            acc_t = _dot(vt[h], pb.reshape(4 * BLK, BLK))
            o_ref[:, p * 128:(p + 1) * 128] = acc_t.T.astype(BF16)

    return pl.pallas_call(
        body, name="attn_fwd", grid=(nblk,),
        in_specs=[pl.BlockSpec((BLK, 512), lambda i: (i, 0)),
                  pl.BlockSpec((s, 128), lambda i: (0, 0)),
                  pl.BlockSpec((s, 128), lambda i: (0, 0)),
                  pl.BlockSpec((2, NQ, 2 * BLK, BLK), lambda i: (0, 0, 0, 0)),
                  pl.BlockSpec(memory_space=pltpu.SMEM)],
        out_specs=[pl.BlockSpec((BLK, 512), lambda i: (i, 0)),
                   pl.BlockSpec((None, NQ, 2 * BLK, BLK), lambda i: (i, 0, 0, 0)),
                   pl.BlockSpec((None, NQ, BLK), lambda i: (i, 0, 0))],
        out_shape=[jax.ShapeDtypeStruct((s, 512), BF16), jax.ShapeDtypeStruct((nblk, NQ, 2 * BLK, BLK), BF16),
                   jax.ShapeDtypeStruct((nblk, NQ, BLK), F32)],
        compiler_params=_cp(1))(q, k, v, bias, sinks)


def _outproj_fwd(pool_o, attn_o, w_out, x, g2, g3, between=None):
    s = x.shape[0]
    tm = min(TM, s)
    nt = s // tm

    def part(name, tile0, tiles, filled, dep):
        def body(p_ref, a_ref, w_ref, x_ref, g2_ref, g3_ref, *rest):
            mix_ref, x1_ref, h2_ref = rest[-3:]
            mix = _dot(p_ref[...], w_ref[0:512, :]) + _dot(a_ref[...], w_ref[512:1024, :])
            mix_ref[...] = mix
            _, n2 = _rms(mix)
            x1 = x_ref[...] + n2 * g2_ref[...]
            x1_ref[...] = x1
            _, n3 = _rms(x1)
            h2_ref[...] = (n3 * g3_ref[...]).astype(BF16)

        row = lambda w: pl.BlockSpec((tm, w), lambda i: (i + tile0, 0))
        vec = pl.BlockSpec((1, D), lambda i: (0, 0))
        extra = list(filled) + ([] if dep is None else [dep])
        return pl.pallas_call(
            body, name=name, grid=(tiles,),
            in_specs=[row(512), row(512), pl.BlockSpec((D, D), lambda i: (0, 0)), row(D), vec, vec]
            + [ANY] * len(extra),
            out_specs=[row(D), row(D), row(D)],
            out_shape=[jax.ShapeDtypeStruct((s, D), F32), jax.ShapeDtypeStruct((s, D), F32),
                       jax.ShapeDtypeStruct((s, D), BF16)],
            input_output_aliases={6 + t: t for t in range(len(filled))},
            compiler_params=_cp(1))(pool_o, attn_o, w_out, x, g2, g3, *extra)

    if between is None or nt < 2:
        return part("outproj_fwd", 0, nt, (), None)
    head = max(1, 3 * nt // 4)
    first = part("outproj_fwd_a", 0, head, (), None)
    return part("outproj_fwd_b", head, nt - head, first, between(first[2]))


def _silu_parts(g):
    sg = jax.nn.sigmoid(g)
    return sg, g * sg


def _ffn_fwd(h2, wg, wu, wd, x1, target, g4):
    s = h2.shape[0]
    tm = min(TM_FFN_FWD, s)

    def body(h_ref, wg_ref, wu_ref, wd_ref, x1_hbm, t_hbm, g4_ref,
             gate_ref, up_ref, a_ref, df_ref, dy_ref, loss_ref, gg4_ref, f_acc, x1_ref, t_ref, sem):
        i = pl.program_id(0)
        j = pl.program_id(1)
        first = j == 0
        tile = pl.ds(pl.multiple_of(i * tm, tm), tm)
        fetches = (pltpu.make_async_copy(x1_hbm.at[tile], x1_ref, sem.at[0]),
                   pltpu.make_async_copy(t_hbm.at[tile], t_ref, sem.at[1]))

        @pl.when(first)
        def _():
            for fetch in fetches:
                fetch.start()

        chunks = [pl.ds(r, min(FFN_ROWS, tm)) for r in range(0, tm, FFN_ROWS)]
        project = lambda rows: (_dot_nt(h_ref[rows, :], wg_ref[...]), _dot_nt(h_ref[rows, :], wu_ref[...]))
        ahead = [project(chunks[0])]
        for k, rows in enumerate(chunks):
            if k + 1 < len(chunks):
                ahead.append(project(chunks[k + 1]))
            gate, up = ahead[k]
            gate_ref[rows, :] = gate.astype(BF16)
            up_ref[rows, :] = up.astype(BF16)
            _, sl = _silu_parts(gate)
            a = (sl * up).astype(BF16)
            a_ref[rows, :] = a
            contrib = _dot(a, wd_ref[...])
            f_acc[rows, :] = jnp.where(first, contrib, f_acc[rows, :] + contrib)

        @pl.when((i == 0) & (j == 0))
        def _():
            loss_ref[...] = jnp.zeros_like(loss_ref)
            gg4_ref[...] = jnp.zeros_like(gg4_ref)

        @pl.when(j == NSH - 1)
        def _():
            for fetch in fetches:
                fetch.wait()
            r4, n4 = _rms(f_acc[...])
            g4v = g4_ref[...]
            err = x1_ref[...] + n4 * g4v - t_ref[...]
            loss_ref[...] += (0.5 / D) * jnp.sum(err * err).reshape(1, 1)
            dy = err * (1.0 / D)
            dy_ref[...] = dy
            df, dg = _rms_bwd(r4, n4, g4v, dy)
            gg4_ref[...] += dg
            df_ref[...] = df.astype(BF16)

    row = lambda w: pl.BlockSpec((tm, w), lambda i, j: (i, 0))
    sh = lambda r, c: pl.BlockSpec((None, r, c), lambda i, j: (j, 0, 0))
    act = pl.BlockSpec((None, tm, FS), lambda i, j: (j, i, 0))
    vec = pl.BlockSpec((1, D), lambda i, j: (0, 0))
    return pl.pallas_call(
        body, name="ffn_fwd", grid=(s // tm, NSH),
        in_specs=[row(D), sh(FS, D), sh(FS, D), sh(FS, D), ANY, ANY, vec],
        out_specs=[act, act, act, row(D), row(D), pl.BlockSpec((1, 1), lambda i, j: (0, 0)), vec],
        out_shape=[jax.ShapeDtypeStruct((NSH, s, FS), BF16)] * 3
        + [jax.ShapeDtypeStruct((s, D), BF16), jax.ShapeDtypeStruct((s, D), F32),
           jax.ShapeDtypeStruct((1, 1), F32), jax.ShapeDtypeStruct((1, D), F32)],
        scratch_shapes=[pltpu.VMEM((tm, D), F32)] * 3 + [pltpu.SemaphoreType.DMA((2,))],
        compiler_params=_cp(2))(h2, wg, wu, wd, x1, target, g4)


def _ffn_bwd_act(df, gate, up, wg, wu, wd, dy, x1, mix, g3, g2):
    s = df.shape[0]
    tm = min(TM_FFN, s)

    def body(df_ref, gate_ref, up_ref, wg_ref, wu_ref, wd_ref, dy_ref, x1_ref, mix_ref, g3_ref, g2_ref,
             dgate_ref, dup_ref, dx1_ref, dmix_ref, gg3_ref, gg2_ref, acc):
        j = pl.program_id(0)
        i = pl.program_id(1)
        first = j == 0
        tile = pl.multiple_of(i * tm, tm)
        chunks = [pl.ds(r, min(FFN_ROWS, tm)) for r in range(0, tm, FFN_ROWS)]

        @pl.when((i == 0) & (j == 0))
        def _():
            gg3_ref[...] = jnp.zeros_like(gg3_ref)
            gg2_ref[...] = jnp.zeros_like(gg2_ref)

        def norms_backward(rows, dh2):
            r3, n3 = _rms(x1_ref[rows, :])
            dx1n, dg3 = _rms_bwd(r3, n3, g3_ref[...], dh2)
            dx1 = dy_ref[rows, :] + dx1n
            dx1_ref[rows, :] = dx1
            gg3_ref[...] += dg3
            r2, n2 = _rms(mix_ref[rows, :])
            dmix, dg2 = _rms_bwd(r2, n2, g2_ref[...], dx1)
            gg2_ref[...] += dg2
            dmix_ref[rows, :] = dmix.astype(BF16)

        def shard_pass(last):
            das = [_dot_nt(df_ref[chunks[0], :], wd_ref[...])]
            for k, rows in enumerate(chunks):
                if k + 1 < len(chunks):
                    das.append(_dot_nt(df_ref[chunks[k + 1], :], wd_ref[...]))
                da = das[k]
                g = gate_ref[rows, :].astype(F32)
                u = up_ref[rows, :].astype(F32)
                sg, sl = _silu_parts(g)
                dgate = (da * u * (sg * (1.0 + g * (1.0 - sg)))).astype(BF16)
                dup = (da * sl).astype(BF16)
                dgate_ref[rows, :] = dgate
                dup_ref[rows, :] = dup
                contrib = _dot(dgate, wg_ref[...]) + _dot(dup, wu_ref[...])
                acc_rows = pl.ds(tile + k * FFN_ROWS, rows.size)
                if last:
                    norms_backward(rows, acc[acc_rows, :] + contrib)
                else:
                    acc[acc_rows, :] = jnp.where(first, contrib, acc[acc_rows, :] + contrib)

        pl.when(j < NSH - 1)(functools.partial(shard_pass, False))
        pl.when(j == NSH - 1)(functools.partial(shard_pass, True))

    row = pl.BlockSpec((tm, D), lambda j, i: (i, 0))
    last = pl.BlockSpec((tm, D), lambda j, i: (i * (j // (NSH - 1)), 0))
    sh = pl.BlockSpec((None, FS, D), lambda j, i: (j, 0, 0))
    act = pl.BlockSpec((None, tm, FS), lambda j, i: (j, i, 0))
    vec = pl.BlockSpec((1, D), lambda j, i: (0, 0))
    return pl.pallas_call(
        body, name="ffn_bwd_act", grid=(NSH, s // tm),
        in_specs=[row, act, act, sh, sh, sh, last, last, last, vec, vec],
        out_specs=[act, act, last, last, vec, vec],
        out_shape=[jax.ShapeDtypeStruct((NSH, s, FS), BF16), jax.ShapeDtypeStruct((NSH, s, FS), BF16),
                   jax.ShapeDtypeStruct((s, D), F32), jax.ShapeDtypeStruct((s, D), BF16),
                   jax.ShapeDtypeStruct((1, D), F32), jax.ShapeDtypeStruct((1, D), F32)],
        scratch_shapes=[pltpu.VMEM((s, D), F32)],
        compiler_params=_cp(2))(df, gate, up, wg, wu, wd, dy, x1, mix, g3, g2)


SMEM_SPEC = pl.BlockSpec(memory_space=pltpu.SMEM)


def _emit_halves(acc_ref, row0, rows, c, own_ref, oth_ref):
    half = rows // 2
    own_ref[...] = acc_ref[pl.ds(row0 + pl.multiple_of(c * half, 8), half), :]
    oth_ref[...] = acc_ref[pl.ds(row0 + pl.multiple_of((1 - c) * half, 8), half), :].astype(BF16)


def _half_shapes(rows):
    return [jax.ShapeDtypeStruct((NSH, rows // 2, D), F32), jax.ShapeDtypeStruct((NSH, rows // 2, D), BF16)]


def _ffn_bwd_w(h2, act_a, dgate, dup, df, c_arr):
    s = h2.shape[0]
    tm = min(TM_FFN_W, s)
    nt = s // tm

    def body(c_ref, h_ref, a_ref, dgate_ref, dup_ref, df_ref, *rest):
        outs, accs = rest[:6], rest[6:]
        i = pl.program_id(1)

        @pl.when(i == 0)
        def _():
            for acc in accs:
                acc[...] = jnp.zeros_like(acc)

        h = h_ref[...]
        accs[0][...] += _dot_tn(dgate_ref[...], h)
        accs[1][...] += _dot_tn(dup_ref[...], h)
        accs[2][...] += _dot_tn(a_ref[...], df_ref[...])

        @pl.when(i == nt - 1)
        def _():
            for t, acc in enumerate(accs):
                _emit_halves(acc, 0, FS, c_ref[0], outs[2 * t], outs[2 * t + 1])

    row = lambda w: pl.BlockSpec((tm, w), lambda j, i: (i, 0))
    act = pl.BlockSpec((None, tm, FS), lambda j, i: (j, i, 0))
    half = pl.BlockSpec((None, FS // 2, D), lambda j, i: (j, 0, 0))
    return pl.pallas_call(
        body, name="ffn_bwd_w", grid=(NSH, nt),
        in_specs=[SMEM_SPEC, row(D), act, act, act, row(D)],
        out_specs=[half] * 6,
        out_shape=_half_shapes(FS) * 3,
        scratch_shapes=[pltpu.VMEM((FS, D), F32)] * 3,
        compiler_params=_cp(2))(c_arr, h2, act_a, dgate, dup, df)


def _outproj_bwd(dmix, w_out, pool_o, attn_o, c_arr, dep=None):
    s = dmix.shape[0]
    tm = min(TM, s)
    nt = s // tm

    def body(c_ref, dm_ref, w_ref, p_ref, a_ref, *rest):
        dpool_ref, dattn_ref, own_ref, oth_ref, gw_ref = rest[-5:]
        i = pl.program_id(0)

        @pl.when(i == 0)
        def _():
            gw_ref[...] = jnp.zeros_like(gw_ref)

        dm = dm_ref[...]
        dpool_ref[...] = _dot_nt(dm, w_ref[0:512, :])
        dattn_ref[...] = _dot_nt(dm, w_ref[512:1024, :]).astype(BF16)
        gw_ref[0:512, :] += _dot_tn(p_ref[...], dm)
        gw_ref[512:1024, :] += _dot_tn(a_ref[...], dm)

        @pl.when(i == nt - 1)
        def _():
            for k in range(NSH):
                _emit_halves(gw_ref, k * WOUT_S, WOUT_S, c_ref[0], own_ref.at[k], oth_ref.at[k])

    row = lambda w: pl.BlockSpec((tm, w), lambda i: (i, 0))
    full = pl.BlockSpec((D, D), lambda i: (0, 0))
    half = pl.BlockSpec((NSH, WOUT_S // 2, D), lambda i: (0, 0, 0))
    return pl.pallas_call(
        body, name="outproj_bwd", grid=(nt,),
        in_specs=[SMEM_SPEC, row(D), full, row(512), row(512)] + ([] if dep is None else [ANY]),
        out_specs=[row(512), row(512), half, half],
        out_shape=[jax.ShapeDtypeStruct((s, 512), F32), jax.ShapeDtypeStruct((s, 512), BF16)] + _half_shapes(WOUT_S),
        scratch_shapes=[pltpu.VMEM((D, D), F32)],
        compiler_params=_cp(1))(c_arr, dmix, w_out, pool_o, attn_o, *([] if dep is None else [dep]))


def _pool_bwd(pooled, dpool, w_pool, pool_scale, dep=None):
    s = pooled.shape[0]
    tm = min(TM_POOL, s)
    hb = tm // HALO
    nt = s // tm
    last_halo = s // HALO - 1

    def body(p_ref, dc_ref, dn_ref, wp_ref, sc_ref, *rest):
        du_ref, gwp_ref, gsc_ref = rest[-3:]
        i = pl.program_id(0)

        @pl.when(i == 0)
        def _():
            gwp_ref[...] = jnp.zeros_like(gwp_ref)
            gsc_ref[...] = jnp.zeros_like(gsc_ref)

        dcur = dc_ref[...]
        dnext = jnp.where(i < nt - 1, dn_ref[...], 0.0)
        dext = jnp.concatenate([dcur, dnext], axis=0)
        t_ext = i * tm + lax.broadcasted_iota(jnp.int32, (tm + HALO, GROUP), 0)
        for g, w in enumerate(WINDOWS):
            sl = slice(g * GROUP, (g + 1) * GROUP)
            pb = p_ref[:, sl]
            wp = wp_ref[g]
            mixed = _dot(pb, wp)
            gsc_ref[:, sl] += jnp.sum(dcur[:, sl] * mixed, axis=0, keepdims=True)
            dmixed = (dext[:, sl] * sc_ref[:, sl]).astype(BF16)
            gwp_ref[g] += _dot_tn(pb, dmixed[0:tm])
            dpooled = _dot_nt(dmixed, wp)
            z = dpooled / jnp.minimum(t_ext + 1, w).astype(F32)
            du_ref[:, sl] = (_window_sums(z, w, -1, tm, 2) - dpooled[0:tm]).astype(BF16)

    return pl.pallas_call(
        body, name="pool_bwd", grid=(nt,),
        in_specs=[pl.BlockSpec((tm, 512), lambda i: (i, 0)),
                  pl.BlockSpec((tm, 512), lambda i: (i, 0)),
                  pl.BlockSpec((HALO, 512), lambda i: (jnp.minimum((i + 1) * hb, last_halo), 0)),
                  pl.BlockSpec((4, GROUP, GROUP), lambda i: (0, 0, 0)),
                  pl.BlockSpec((1, 512), lambda i: (0, 0))] + ([] if dep is None else [ANY]),
        out_specs=[pl.BlockSpec((tm, 512), lambda i: (i, 0)),
                   pl.BlockSpec((4, GROUP, GROUP), lambda i: (0, 0, 0)),
                   pl.BlockSpec((1, 512), lambda i: (0, 0))],
        out_shape=[jax.ShapeDtypeStruct((s, 512), BF16), jax.ShapeDtypeStruct((4, GROUP, GROUP), F32),
                   jax.ShapeDtypeStruct((1, 512), F32)],
        compiler_params=_cp(1))(pooled, dpool, dpool, w_pool, pool_scale, *([] if dep is None else [dep]))


def _attn_bwd(q, k, v, do, probs, sink_share, bucket, dep=None):
    s = q.shape[0]
    nblk = s // BLK

    def body(q_ref, do_ref, k_ref, v_ref, prob_ref, ps_ref, bk_ref, *rest):
        dq_ref, dk_ref, dv_ref, grb_ref, gsk_ref, dss_ref = rest[-6:]
        n = pl.program_id(0)

        @pl.when(n == 0)
        def _():
            dk_ref[...] = jnp.zeros_like(dk_ref)
            dv_ref[...] = jnp.zeros_like(dv_ref)
            dss_ref[...] = jnp.zeros_like(dss_ref)
            gsk_ref[...] = jnp.zeros_like(gsk_ref)

        kt, (lo, pstart, cstart) = _kv_band(k_ref, n, True)
        vv, _ = _kv_band(v_ref, n, False)
        lo_q = lax.broadcasted_iota(jnp.int32, (BLK, 128), 1) < HD
        dkk = [jnp.zeros((2 * BLK, 128), F32), jnp.zeros((2 * BLK, 128), F32)]
        dvv = [jnp.zeros((2 * BLK, 128), F32), jnp.zeros((2 * BLK, 128), F32)]

        def by_head(t):
            return jnp.concatenate([jnp.where(lo_q, t, 0.0), jnp.where(lo_q, 0.0, t)], axis=0).astype(BF16)

        for p in range(4):
            h = p // 2
            qt = q_ref[:, p * 128:(p + 1) * 128].astype(F32) * SCALE
            dot = do_ref[:, p * 128:(p + 1) * 128]
            pb = prob_ref[pl.ds(2 * p, 2)]
            prob = pb.astype(F32)
            ps = ps_ref[pl.ds(2 * p, 2), :].reshape(2, 1, BLK)
            dp = _dot_nt(vv[h], dot).reshape(2, 2 * BLK, BLK)
            delta = jnp.sum(prob * dp, axis=1, keepdims=True)
            ds = prob * (dp - delta)
            sink_part = ps * delta
            for e in range(2):
                gs = -jnp.sum(sink_part[e]).reshape(1, 1)
                gsk_ref[pl.ds(2 * p + e, 1), :] += jnp.broadcast_to(gs, (1, 128))
            dss_ref[pl.ds(2 * p, 2)] += ds
            dsb = ds.astype(BF16)
            dq_t = _dot(kt[h], dsb.reshape(4 * BLK, BLK))
            dq_ref[:, p * 128:(p + 1) * 128] = (dq_t.T * SCALE).astype(BF16)
            dkk[h] = dkk[h] + _dot(jnp.concatenate([dsb[0], dsb[1]], axis=1), by_head(qt))
            dvv[h] = dvv[h] + _dot(jnp.concatenate([pb[0], pb[1]], axis=1), by_head(dot.astype(F32)))
        for acc, ref in ((dkk, dk_ref), (dvv, dv_ref)):
            f0 = acc[0] + pltpu.roll(acc[0], HD, axis=1)
            f1 = acc[1] + pltpu.roll(acc[1], HD, axis=1)
            band = jnp.where(lo, f0, f1)
            ref[pl.ds(pstart, BLK), :] += band[0:BLK]
            ref[pl.ds(cstart, BLK), :] += band[BLK:2 * BLK]

        @pl.when(n == nblk - 1)
        def _():
            _relbias_grad(dss_ref, bk_ref[...], grb_ref)

    blk = pl.BlockSpec((BLK, 512), lambda i: (i, 0))
    kv = pl.BlockSpec((s, 128), lambda i: (0, 0))
    row8 = pl.BlockSpec((NQ, 128), lambda i: (0, 0))
    return pl.pallas_call(
        body, name="attn_bwd", grid=(nblk,),
        in_specs=[blk, blk, kv, kv, pl.BlockSpec((None, NQ, 2 * BLK, BLK), lambda i: (i, 0, 0, 0)),
                  pl.BlockSpec((None, NQ, BLK), lambda i: (i, 0, 0)), pl.BlockSpec((2 * BLK, BLK), lambda i: (0, 0))]
        + ([] if dep is None else [ANY]),
        out_specs=[blk, kv, kv, row8, row8],
        out_shape=[jax.ShapeDtypeStruct((s, 512), BF16), jax.ShapeDtypeStruct((s, 128), F32),
                   jax.ShapeDtypeStruct((s, 128), F32), jax.ShapeDtypeStruct((NQ, 128), F32),
                   jax.ShapeDtypeStruct((NQ, 128), F32)],
        scratch_shapes=[pltpu.VMEM((NQ, 2 * BLK, BLK), F32)],
        compiler_params=_cp(1))(q, do, k, v, probs, sink_share, bucket, *([] if dep is None else [dep]))


def _relbias_grad(ds_ref, bucket_v, o_ref):
    lane = lax.broadcasted_iota(jnp.int32, (NQ, 128), 1)
    head_row = lax.broadcasted_iota(jnp.int32, (NQ, BLK), 0)
    acc = jnp.zeros((NQ, 128), F32)
    for b in range(NBUCKET):
        sel = bucket_v == b
        stack = jnp.zeros((NQ, BLK), F32)
        for h in range(NQ):
            col_sums = jnp.sum(jnp.where(sel, ds_ref[h], 0.0), axis=0, keepdims=True)
            stack = jnp.where(head_row == h, col_sums, stack)
        acc = acc + jnp.where(lane == b, jnp.sum(stack, axis=1, keepdims=True), 0.0)
    o_ref[...] = acc


def _inproj_bwd(x, g1, du, dq, dk, dv, w_in, dx1, c_arr):
    s = x.shape[0]
    tm = min(TM, s)
    nt = s // tm
    cols = ((0, 512), (512, 1024), (1024, 1152), (1152, 1280))

    def body(c_ref, x_ref, g_ref, du_ref, dq_ref, dk_ref, dv_ref, w_ref, dx1_ref,
             gx_ref, own_ref, oth_ref, gg_ref, gw_ref):
        i = pl.program_id(0)

        @pl.when(i == 0)
        def _():
            gw_ref[...] = jnp.zeros_like(gw_ref)
            gg_ref[...] = jnp.zeros_like(gg_ref)

        parts = (du_ref[...], dq_ref[...], dk_ref[...].astype(BF16), dv_ref[...].astype(BF16))
        dh = None
        for (a, b), dpart in zip(cols, parts):
            t = _dot(dpart, w_ref[a:b, :])
            dh = t if dh is None else dh + t
        gv = g_ref[...]
        r1, n1 = _rms(x_ref[...])
        h = (n1 * gv).astype(BF16)
        for (a, b), dpart in zip(cols, parts):
            gw_ref[a:b, :] += _dot_tn(dpart, h)
        dx, dg = _rms_bwd(r1, n1, gv, dh)
        gg_ref[...] += dg
        gx_ref[...] = dx1_ref[...] + dx

        @pl.when(i == nt - 1)
        def _():
            for k in range(NSH):
                _emit_halves(gw_ref, k * WIN_S, WIN_S, c_ref[0], own_ref.at[k], oth_ref.at[k])

    row = lambda w: pl.BlockSpec((tm, w), lambda i: (i, 0))
    vec = pl.BlockSpec((1, D), lambda i: (0, 0))
    full = pl.BlockSpec((IN_W, D), lambda i: (0, 0))
    half = pl.BlockSpec((NSH, WIN_S // 2, D), lambda i: (0, 0, 0))
    return pl.pallas_call(
        body, name="inproj_bwd", grid=(nt,),
        in_specs=[SMEM_SPEC, row(D), vec, row(512), row(512), row(128), row(128), full, row(D)],
        out_specs=[row(D), half, half, vec],
        out_shape=[jax.ShapeDtypeStruct((s, D), F32)] + _half_shapes(WIN_S) + [jax.ShapeDtypeStruct((1, D), F32)],
        scratch_shapes=[pltpu.VMEM((IN_W, D), F32)],
        compiler_params=_cp(1))(c_arr, x, g1, du, dq, dk, dv, w_in, dx1)


def _local_step(x, target, small, w_in, w_out, ffn_weights, c_arr, on_ffn_grads=None, on_wout_grads=None,
                mid_outproj=None, start_dep=None):
    g1, g2, g3, g4, w_pool, pool_scale, rel_bias, sinks = small
    bucket = jnp.asarray(_bucket_table())
    wp_bf = w_pool.astype(BF16)
    bias = _bias_table(bucket, rel_bias)
    h1 = _prenorm(x, g1, start_dep)
    if callable(w_in):
        w_in = w_in(bias, h1)
    u, q, k, v = _inproj_fwd(h1, w_in)
    pool_o, pooled = _pool_fwd(u, wp_bf, pool_scale)
    attn_o, probs, sink_share = _attn_fwd(q, k, v, bias, sinks)
    g2_fwd = g2
    if callable(w_out):
        w_out, after = w_out(pool_o, attn_o)
        if after is not None:
            g2_fwd = g2 + after[:1, :1]
    mix, x1, h2 = _outproj_fwd(pool_o, attn_o, w_out, x, g2_fwd, g3, mid_outproj)
    wg, wu, wd = ffn_weights(h2) if callable(ffn_weights) else ffn_weights
    gate, up, act_a, df, dy, loss, gg4 = _ffn_fwd(h2, wg, wu, wd, x1, target, g4)
    dgate, dup, dx1, dmix, gg3, gg2 = _ffn_bwd_act(df, gate, up, wg, wu, wd, dy, x1, mix, g3, g2)
    ffn_g = _ffn_bwd_w(h2, act_a, dgate, dup, df, c_arr)
    dep = None if on_ffn_grads is None else on_ffn_grads(ffn_g)
    dpool, dattn, wout_own, wout_oth = _outproj_bwd(dmix, w_out, pool_o, attn_o, c_arr, dep)
    dep = None if on_wout_grads is None else on_wout_grads(wout_own, wout_oth, dpool)
    du, gwp, gsc = _pool_bwd(pooled, dpool, wp_bf, pool_scale, dep)
    dq, dk, dv, grb, gsk = _attn_bwd(q, k, v, dattn, probs, sink_share, bucket, dep)
    gx, win_own, win_oth, gg1 = _inproj_bwd(x, g1, du, dq, dk, dv, w_in, dx1, c_arr)
    small_grads = (gg1, gg2, gg3, gg4, gwp, gsc, grb, gsk[:, 0].reshape(1, NQ))
    return loss, gx, small_grads, ((win_own, win_oth), (wout_own, wout_oth), ffn_g)


def _place():
    x, y, c = lax.axis_index("x"), lax.axis_index("y"), lax.axis_index("c")
    chips = ((1 - x, y), (x, 1 - y), (1 - x, 1 - y))
    return x, y, c, chips


def _remote(src, dst, ssem, rsem, dev):
    return pltpu.make_async_remote_copy(src_ref=src, dst_ref=dst, send_sem=ssem, recv_sem=rsem,
                                        device_id=dev, device_id_type=MESH_T)


def _to_sibling(arrs, name, after=()):
    nt, na = len(arrs), len(after)

    def body(*refs):
        srcs, dsts = refs[:nt], refs[nt + na:2 * nt + na]
        ssem, rsem = refs[2 * nt + na:]
        x, y, c, _ = _place()
        cps = [_remote(srcs[t], dsts[t], ssem.at[t], rsem.at[t], (x, y, 1 - c)) for t in range(nt)]
        for cp in cps:
            cp.start()
        for cp in cps:
            cp.wait()

    return pl.pallas_call(
        body, name=name, in_specs=[ANY] * (nt + na), out_specs=[ANY] * nt,
        out_shape=[jax.ShapeDtypeStruct(a.shape, a.dtype) for a in arrs],
        scratch_shapes=[pltpu.SemaphoreType.DMA((nt,)), pltpu.SemaphoreType.DMA((nt,))],
        compiler_params=_cp())(*arrs, *after)


HBM_SPEC = pl.BlockSpec(memory_space=pltpu.HBM)
SEM_SPEC = pl.BlockSpec(memory_space=pltpu.SEMAPHORE)
DATAFLOW = pltpu.SideEffectType.DATAFLOW_SIDE_EFFECTING


def _cast_into_slots(ws, chip_arr):
    nt, tiles = len(ws), 4

    def body(chip_ref, *refs):
        for t in range(nt):
            refs[nt + t][...] = refs[t][...].astype(BF16)

    return pl.pallas_call(
        body, name="cast_weights",
        grid_spec=pltpu.PrefetchScalarGridSpec(
            num_scalar_prefetch=1, grid=(tiles,),
            in_specs=[pl.BlockSpec((w.shape[0] // tiles, w.shape[1]), lambda i, chip_ref: (i, 0)) for w in ws],
            out_specs=[pl.BlockSpec((None, w.shape[0] // tiles, w.shape[1]), lambda i, chip_ref: (chip_ref[0], i, 0))
                       for w in ws]),
        out_shape=[jax.ShapeDtypeStruct((NSH,) + w.shape, BF16) for w in ws],
        compiler_params=_cp(1))(chip_arr, *ws)


def _gather_copies(srcs, lands, ssem, rsem, first=0):
    x, y, c, chips = _place()
    me = 2 * x + y
    out = []
    for t in range(len(lands)):
        half = lands[t].shape[1] // 2
        mine = pl.ds(pl.multiple_of(c * half, 16), half)
        for j, (px, py) in enumerate(chips):
            k = 3 * (first + t) + j
            send = _remote(lands[t].at[me, mine], lands[t].at[me, mine], ssem.at[k], rsem.at[k], (px, py, c))
            blk = lands[t].at[2 * px + py, mine]
            out.append((send, _remote(blk, blk, ssem.at[k], rsem.at[k], (px, py, c))))
    return out


def _scatter_copies(srcs, lands, ssem, rsem):
    x, y, c, chips = _place()
    out = []
    for t in range(len(srcs)):
        for j, (px, py) in enumerate(chips):
            k = 3 * t + j
            send = _remote(srcs[t].at[2 * px + py], lands[t].at[j], ssem.at[k], rsem.at[k], (px, py, c))
            out.append((send, _remote(lands[t].at[j], lands[t].at[j], ssem.at[k], rsem.at[k], (px, py, c))))
    return out


def _sibling_copies(srcs, lands, ssem, rsem):
    x, y, c, _ = _place()
    sib = (x, y, 1 - c)
    return [(_remote(srcs[t], lands[t], ssem.at[t], rsem.at[t], sib),
             _remote(lands[t], lands[t], ssem.at[t], rsem.at[t], sib)) for t in range(len(srcs))]


def _peer_copies(srcs, lands, ssem, rsem):
    x, y, c, _ = _place()
    me = 4 * x + 2 * y + c
    out = []
    for kk in range(1, 8):
        px, py, pc = x ^ (kk >> 2), y ^ ((kk >> 1) & 1), c ^ (kk & 1)
        send = _remote(srcs[0], lands[0].at[me], ssem.at[kk - 1], rsem.at[kk - 1], (px, py, pc))
        slot = lands[0].at[4 * px + 2 * py + pc]
        out.append((send, _remote(slot, slot, ssem.at[kk - 1], rsem.at[kk - 1], (px, py, pc))))
    return out


def _forward_copies(srcs, lands, ssem, rsem):
    x, y, c, chips = _place()
    sib = (x, y, 1 - c)
    out = []
    for t in range(len(lands)):
        half = lands[t].shape[1] // 2
        mine = pl.ds(pl.multiple_of(c * half, 16), half)
        other = pl.ds(pl.multiple_of((1 - c) * half, 16), half)
        for j, (px, py) in enumerate(chips):
            k = 3 * t + j
            blk_m, blk_o = lands[t].at[2 * px + py, mine], lands[t].at[2 * px + py, other]
            out.append((_remote(blk_m, blk_m, ssem.at[k], rsem.at[k], sib),
                        _remote(blk_o, blk_o, ssem.at[k], rsem.at[k], sib)))
    return out


def _win_and_small_copies(srcs, lands, ssem, rsem):
    return (_scatter_copies(srcs[:1], lands[:1], ssem, rsem)
            + _peer_copies(srcs[1:], lands[1:], ssem.at[pl.ds(3, 7)], rsem.at[pl.ds(3, 7)]))


def _split_start(plan, ncopy, srcs, lands, after, name):
    ns, nbuf = len(srcs), len(srcs) + len(lands)
    extra = [] if after is None else [after]
    n_in = nbuf + len(extra)

    def body(*refs):
        ssem, rsem, token = refs[n_in], refs[n_in + 1], refs[-1]
        for send, _ in plan(refs[:ns], refs[ns:nbuf], ssem, rsem):
            send.start()
        token[...] = jnp.zeros_like(token)

    bufs = [pltpu.with_memory_space_constraint(a, pltpu.HBM) for a in list(srcs) + list(lands)]
    outs = pl.pallas_call(
        body, name=name, in_specs=[HBM_SPEC] * nbuf + [ANY] * len(extra),
        out_specs=[SEM_SPEC, SEM_SPEC] + [HBM_SPEC] * nbuf + [pl.BlockSpec(memory_space=pltpu.VMEM)],
        out_shape=[pltpu.SemaphoreType.DMA((ncopy,)), pltpu.SemaphoreType.DMA((ncopy,))]
        + [pltpu.HBM(a.shape, a.dtype) for a in bufs] + [jax.ShapeDtypeStruct((8, 128), F32)],
        input_output_aliases={i: 2 + i for i in range(nbuf)},
        compiler_params=pltpu.CompilerParams(has_side_effects=DATAFLOW))(*bufs, *extra)
    return outs[0], outs[1], outs[2:2 + ns], outs[2 + ns:2 + nbuf], outs[-1]


def _split_wait(plan, ssem, rsem, srcs, lands, after, name, only=None):
    ns, nbuf = len(srcs), len(srcs) + len(lands)

    def body(*refs):
        s_ref, r_ref = refs[nbuf], refs[nbuf + 1]
        for k, (send, recv) in enumerate(plan(refs[:ns], refs[ns:nbuf], s_ref, r_ref)):
            if only is None or k in only:
                send.wait_send()
                recv.wait_recv()

    outs = pl.pallas_call(
        body, name=name, in_specs=[HBM_SPEC] * nbuf + [SEM_SPEC, SEM_SPEC] + [ANY] * len(after),
        out_specs=[HBM_SPEC] * nbuf,
        out_shape=[pltpu.HBM(a.shape, a.dtype) for a in list(srcs) + list(lands)],
        input_output_aliases={i: i for i in range(nbuf)},
        compiler_params=pltpu.CompilerParams(has_side_effects=DATAFLOW))(*srcs, *lands, ssem, rsem, *after)
    return outs


def _gather_finish(lands, tag):
    nt = len(lands)

    def body(*refs):
        outs = refs[nt:2 * nt]
        dsend, drecv = refs[2 * nt:]
        x, y, c, chips = _place()
        sib = (x, y, 1 - c)
        sends = []
        for t in range(nt):
            half = outs[t].shape[1] // 2
            mine = pl.ds(pl.multiple_of(c * half, 16), half)
            for j, (px, py) in enumerate(chips):
                blk = outs[t].at[2 * px + py, mine]
                cp = _remote(blk, blk, dsend.at[t, j], drecv.at[t, j], sib)
                cp.start()
                sends.append(cp)
        for t in range(nt):
            half = outs[t].shape[1] // 2
            other = pl.ds(pl.multiple_of((1 - c) * half, 16), half)
            for j, (px, py) in enumerate(chips):
                blk = outs[t].at[2 * px + py, other]
                _remote(blk, blk, dsend.at[t, j], drecv.at[t, j], sib).wait_recv()
        for cp in sends:
            cp.wait_send()

    return pl.pallas_call(
        body, name="gather_finish_" + tag, in_specs=[ANY] * nt, out_specs=[ANY] * nt,
        out_shape=[jax.ShapeDtypeStruct(a.shape, a.dtype) for a in lands],
        input_output_aliases={t: t for t in range(nt)},
        scratch_shapes=[pltpu.SemaphoreType.DMA((nt, 3)), pltpu.SemaphoreType.DMA((nt, 3))],
        compiler_params=_cp())(*lands)


def _chip_partial(owns, recv, chip_arr, tag, whole=False):
    nt = len(owns)

    if whole:
        def body_whole(chip_ref, *refs):
            for t in range(nt):
                refs[2 * nt + t][...] = (refs[t][...] + refs[nt + t][...].astype(F32)).astype(BF16)
                mine = chip_ref[0]
                refs[3 * nt + t][...] = refs[t][mine] + refs[nt + t][mine].astype(F32)

        vm = pl.BlockSpec(memory_space=pltpu.VMEM)
        return pl.pallas_call(
            body_whole, name="rs_chip_partial_" + tag, in_specs=[SMEM_SPEC] + [vm] * (2 * nt), out_specs=[vm] * (2 * nt),
            out_shape=[jax.ShapeDtypeStruct(a.shape, BF16) for a in owns]
            + [jax.ShapeDtypeStruct(a.shape[1:], F32) for a in owns],
            compiler_params=_cp())(chip_arr, *owns, *recv)

    def body(chip_ref, *refs):
        k = pl.program_id(0)
        for t in range(nt):
            p = refs[t][...] + refs[nt + t][...].astype(F32)
            refs[2 * nt + t][...] = p.astype(BF16)

            @pl.when(k == chip_ref[0])
            def _():
                refs[3 * nt + t][...] = p

    blk_specs = [pl.BlockSpec((None,) + a.shape[1:], lambda k, chip_ref: (k, 0, 0)) for a in owns]
    own_specs = [pl.BlockSpec(a.shape[1:], lambda k, chip_ref: (0, 0)) for a in owns]
    return pl.pallas_call(
        body, name="rs_chip_partial_" + tag,
        grid_spec=pltpu.PrefetchScalarGridSpec(num_scalar_prefetch=1, grid=(NSH,), in_specs=blk_specs * 2,
                                               out_specs=blk_specs + own_specs),
        out_shape=[jax.ShapeDtypeStruct(a.shape, BF16) for a in owns]
        + [jax.ShapeDtypeStruct(a.shape[1:], F32) for a in owns],
        compiler_params=_cp(1))(chip_arr, *owns, *recv)


def _sum_chips(own, recv, tag):
    nt = len(own)

    def body(*refs):
        outs = refs[2 * nt:]
        for t in range(nt):
            r = refs[nt + t]
            outs[t][...] = ((refs[t][...] + r[0].astype(F32)) + r[1].astype(F32)) + r[2].astype(F32)

    vm = pl.BlockSpec(memory_space=pltpu.VMEM)
    return pl.pallas_call(
        body, name="rs_sum_chips_" + tag, in_specs=[vm] * (2 * nt), out_specs=[vm] * nt,
        out_shape=[jax.ShapeDtypeStruct(a.shape, F32) for a in own],
        compiler_params=_cp())(*own, *recv)


def _sum_chips_shared(own, recv, tag):
    def body(own_ref, recv_ref, out_ref, sib_ref, ssem, rsem):
        out_ref[...] = ((own_ref[...] + recv_ref[0].astype(F32)) + recv_ref[1].astype(F32)) + recv_ref[2].astype(F32)
        x, y, c, _ = _place()
        cp = _remote(out_ref, sib_ref, ssem.at[0], rsem.at[0], (x, y, 1 - c))
        cp.start()
        cp.wait()

    vm = pl.BlockSpec(memory_space=pltpu.VMEM)
    return pl.pallas_call(
        body, name="rs_sum_share_" + tag, in_specs=[vm, vm], out_specs=[vm, ANY],
        out_shape=[jax.ShapeDtypeStruct(own.shape, F32)] * 2,
        scratch_shapes=[pltpu.SemaphoreType.DMA((1,)), pltpu.SemaphoreType.DMA((1,))],
        compiler_params=_cp())(own, recv)


def _adamw_math(w, g, m, v):
    m = ADAM_B1 * m + (1.0 - ADAM_B1) * g
    v = ADAM_B2 * v + (1.0 - ADAM_B2) * (g * g)
    m_hat = m / (1.0 - ADAM_B1 ** ADAM_STEP)
    v_hat = v / (1.0 - ADAM_B2 ** ADAM_STEP)
    delta = -ADAM_LR * (m_hat / (jnp.sqrt(v_hat) + ADAM_EPS) + ADAM_WD * w)
    return delta, m, v


ADAM_SPLIT = 4


def _adamw_shards(own, sib, ws, ms, vs, c_arr, tag):
    nt = len(own)

    def body(c_ref, *refs):
        hh = pl.program_id(0)
        mine = hh == c_ref[0]
        for t in range(nt):
            g = jnp.where(mine, refs[t][...], refs[nt + t][...])
            delta, m, v = _adamw_math(refs[2 * nt + t][...], g, refs[3 * nt + t][...], refs[4 * nt + t][...])
            refs[5 * nt + t][...] = g
            refs[6 * nt + t][...] = delta
            refs[7 * nt + t][...] = m
            refs[8 * nt + t][...] = v

    def tile(a):
        return (a.shape[0] // ADAM_SPLIT, a.shape[1])

    def half_index(used_when_mine):
        def index(hh, q, c_ref):
            mine = 1 - (hh - c_ref[0]) * (hh - c_ref[0])
            used = mine if used_when_mine else 1 - mine
            return (used * q + (1 - used) * hh * (ADAM_SPLIT - 1), 0)
        return index

    own_specs = [pl.BlockSpec(tile(a), half_index(True)) for a in own]
    sib_specs = [pl.BlockSpec(tile(a), half_index(False)) for a in own]
    full_specs = [pl.BlockSpec(tile(a), lambda hh, q, c_ref: (hh * ADAM_SPLIT + q, 0)) for a in own]
    return pl.pallas_call(
        body, name="adamw_shards_" + tag,
        grid_spec=pltpu.PrefetchScalarGridSpec(num_scalar_prefetch=1, grid=(2, ADAM_SPLIT),
                                               in_specs=own_specs + sib_specs + full_specs * 3,
                                               out_specs=full_specs * 4),
        out_shape=[jax.ShapeDtypeStruct(w.shape, F32) for w in ws] * 4,
        compiler_params=_cp(2))(c_arr, *own, *sib, *ws, *ms, *vs)


SMALL_WPOOL_ROW = 32
SMALL_SCALE_ROW = SMALL_WPOOL_ROW + 4 * GROUP
SMALL_BIAS_ROW = SMALL_SCALE_ROW + 8
SMALL_SINKS_ROW = SMALL_BIAS_ROW + NQ
SMALL_LOSS_ROW = SMALL_SINKS_ROW + 8


def _small_sum_adamw(gathered, wp, mp, vp):
    rows = wp.shape[0]

    def body(g_ref, w_ref, m_ref, v_ref, *rest):
        outs, res = rest[:-1], rest[-1]
        g = g_ref[0]
        for d in range(1, 8):
            g = g + g_ref[d]
        delta, m, v = _adamw_math(w_ref[...], g, m_ref[...], v_ref[...])
        for kind, val in enumerate((g, delta, m, v)):
            res[kind] = val
            gains = outs[8 * kind:8 * kind + 4]
            w_pool_o, scale_o, bias_o, sinks_o = outs[8 * kind + 4:8 * kind + 8]
            for t in range(4):
                for i in range(8):
                    gains[t][:, 128 * i:128 * (i + 1)] = res[kind, pl.ds(8 * t + i, 1), :]
            for grp in range(4):
                w_pool_o[grp] = res[kind, pl.ds(SMALL_WPOOL_ROW + GROUP * grp, GROUP), :]
            for i in range(4):
                scale_o[:, 128 * i:128 * (i + 1)] = res[kind, pl.ds(SMALL_SCALE_ROW + i, 1), :]
            bias_o[...] = res[kind, pl.ds(SMALL_BIAS_ROW, NQ), :][:, 0:NBUCKET]
            sinks_o[...] = res[kind, pl.ds(SMALL_SINKS_ROW, 1), :][:, 0:NQ]
        outs[-1][...] = res[0, pl.ds(SMALL_LOSS_ROW, 1), :]

    vm = pl.BlockSpec(memory_space=pltpu.VMEM)
    one_kind = [jax.ShapeDtypeStruct((1, D), F32)] * 4 + [
        jax.ShapeDtypeStruct((4, GROUP, GROUP), F32), jax.ShapeDtypeStruct((1, POOL_W), F32),
        jax.ShapeDtypeStruct((NQ, NBUCKET), F32), jax.ShapeDtypeStruct((1, NQ), F32)]
    return pl.pallas_call(
        body, name="small_sum_adamw", in_specs=[vm] * 4, out_specs=[vm] * 33,
        out_shape=one_kind * 4 + [jax.ShapeDtypeStruct((1, 128), F32)],
        scratch_shapes=[pltpu.VMEM((4, rows, 128), F32)],
        compiler_params=_cp())(gathered, wp, mp, vp)


def _pack_small(gains4, w_pool, pool_scale, rel_bias_t, sinks, loss):
    bias_rows = jnp.pad(rel_bias_t, ((0, 0), (0, 128 - rel_bias_t.shape[1])))
    parts = list(gains4) + [w_pool, pool_scale, bias_rows, sinks, loss]
    rows = []
    for a in parts:
        flat = a.reshape(-1)
        n = flat.shape[0]
        padded = -(-n // 1024) * 1024
        rows.append(jnp.pad(flat, (0, padded - n)).reshape(-1, 128))
    return jnp.concatenate(rows, axis=0)


def kernel(x, g_pre_mix, w_in, w_pool, pool_scale, rel_bias, sinks, w_out, g_post_mix, g_pre_ffn, w_gate, w_up, w_down, g_post_ffn, loss_target, m_g_pre_mix, m_w_in, m_w_pool, m_pool_scale, m_rel_bias, m_sinks, m_w_out, m_g_post_mix, m_g_pre_ffn, m_w_gate, m_w_up, m_w_down, m_g_post_ffn, v_g_pre_mix, v_w_in, v_w_pool, v_pool_scale, v_rel_bias, v_sinks, v_w_out, v_g_post_mix, v_g_pre_ffn, v_w_gate, v_w_up, v_w_down, v_g_post_ffn):
    c = lax.axis_index("c")
    chip = 2 * lax.axis_index("x") + lax.axis_index("y")

    def shards(a_in, a_out, a_gate, a_up, a_down):
        return (a_in[0].T, a_out[0], a_gate[0].T, a_up[0].T, a_down[0])

    c_arr = c.reshape(1).astype(jnp.int32)
    chip_arr = chip.reshape(1).astype(jnp.int32)

    big_w = shards(w_in, w_out, w_gate, w_up, w_down)
    g_ssem, g_rsem, _, g_lands, g_token = _split_start(
        _gather_copies, 15, [], _cast_into_slots(big_w, chip_arr), None, "gather_start")
    fw = {}

    def w_in_ready(*after):
        fw["first"] = _split_wait(_gather_copies, g_ssem, g_rsem, [], g_lands, after, "gather_win_wait",
                                  only=(0, 1, 2))
        (fw["win"],) = _gather_finish([fw["first"][0]], "win")
        return fw["win"].reshape(IN_W, D)

    def w_out_ready(*after):
        fw["second"] = _split_wait(functools.partial(_gather_copies, first=1), g_ssem, g_rsem, [],
                                   list(fw["first"][1:]), after, "gather_wout_wait", only=(0, 1, 2))
        (fw["wout"],) = _gather_finish([fw["second"][0]], "wout")
        return fw["wout"].reshape(D, D), None

    def ffn_forward_start(after):
        outs = _split_wait(functools.partial(_gather_copies, first=2), g_ssem, g_rsem, [], list(fw["second"][1:]),
                           [after], "gather_ffn_wait")
        fw["f"] = _split_start(_forward_copies, 9, [], list(outs), None, "gather_forward_start")
        return fw["f"][4]

    def ffn_weights(after):
        ssem, rsem, _, lands, _ = fw["f"]
        return _split_wait(_forward_copies, ssem, rsem, [], lands, [after], "gather_forward_wait")

    rs = {}

    def on_ffn_grads(ffn_g):
        oths = ffn_g[1::2]
        rs["ffn_own"] = ffn_g[0::2]
        rs["x"] = _split_start(_sibling_copies, 3, oths, [lax.empty(a.shape, BF16) for a in oths], ffn_g[0],
                               "rs_exchange_ffn_start")
        return rs["x"][4]

    def on_wout_grads(own, oth, after):
        ssem, rsem, srcs, lands, _ = rs["x"]
        recv_ffn = _split_wait(_sibling_copies, ssem, rsem, srcs, lands, [after], "rs_exchange_ffn_wait")[3:]
        recv_wout = _to_sibling([oth], "rs_exchange_wout")
        outs = _chip_partial([own] + list(rs["ffn_own"]), list(recv_wout) + list(recv_ffn), chip_arr, "early")
        rs["early_own"] = outs[4:]
        rs["s"] = _split_start(_scatter_copies, 12, outs[:4],
                               [lax.empty((3,) + a.shape[1:], BF16) for a in outs[:4]], outs[4], "scatter_early_start")
        return rs["s"][4]

    small = (g_pre_mix, g_post_mix, g_pre_ffn, g_post_ffn, w_pool[0], pool_scale, rel_bias, sinks)
    loss, gx, small_grads, ((win_own, win_oth), _, _) = _local_step(
        x[0], loss_target[0], small, w_in_ready, w_out_ready, ffn_weights, c_arr, on_ffn_grads, on_wout_grads,
        ffn_forward_start, g_token)

    ssem, rsem, srcs, lands, _ = rs["s"]
    recv_early = _split_wait(_scatter_copies, ssem, rsem, srcs, lands, [gx], "scatter_early_wait")[4:]
    finals = _sum_chips(list(rs["early_own"]), list(recv_early), "early")
    to_sib = [win_oth] + list(finals)
    sh_ssem, sh_rsem, sh_srcs, sh_lands, _ = _split_start(
        _sibling_copies, 5, to_sib, [lax.empty(a.shape, a.dtype) for a in to_sib], None, "rs_share_start")

    unused = jnp.zeros((1, 1), F32)
    gp = _pack_small(small_grads[:4], *small_grads[4:], loss)
    wp = _pack_small((g_pre_mix, g_post_mix, g_pre_ffn, g_post_ffn), w_pool, pool_scale, rel_bias.T, sinks, unused)
    mp = _pack_small((m_g_pre_mix, m_g_post_mix, m_g_pre_ffn, m_g_post_ffn), m_w_pool, m_pool_scale, m_rel_bias.T,
                     m_sinks, unused)
    vp = _pack_small((v_g_pre_mix, v_g_post_mix, v_g_pre_ffn, v_g_post_ffn), v_w_pool, v_pool_scale, v_rel_bias.T,
                     v_sinks, unused)

    moments = (shards(m_w_in, m_w_out, m_w_gate, m_w_up, m_w_down),
               shards(v_w_in, v_w_out, v_w_gate, v_w_up, v_w_down))
    sh_first = _split_wait(_sibling_copies, sh_ssem, sh_rsem, sh_srcs, sh_lands, [], "rs_share_win_wait", only=(0,))
    outs = _chip_partial([win_own], [sh_first[5]], chip_arr, "win", whole=True)
    slots = lax.dynamic_update_slice(lax.empty((8,) + gp.shape, F32), gp[None], (2 * chip + c, 0, 0))
    w_ssem, w_rsem, w_srcs, w_lands, w_token = _split_start(
        _win_and_small_copies, 10, [outs[0], gp], [lax.empty((3,) + outs[0].shape[1:], BF16), slots], gx,
        "scatter_win_start")
    shared = _split_wait(_sibling_copies, sh_ssem, sh_rsem, sh_first[:5], sh_first[5:], [w_token],
                         "rs_share_early_wait", only=(1, 2, 3, 4))
    adam_e = _adamw_shards(shared[1:5], shared[6:], big_w[1:], moments[0][1:], moments[1][1:], c_arr, "early")
    w_first = _split_wait(_win_and_small_copies, w_ssem, w_rsem, w_srcs, w_lands, [adam_e[0]], "scatter_win_wait",
                          only=(0, 1, 2))
    win_final, win_sib = _sum_chips_shared(outs[1], w_first[2], "win")
    adam_w = _adamw_shards([win_final], [win_sib], big_w[:1], moments[0][:1], moments[1][:1], c_arr, "win")
    big_g, big_d, big_m, big_v = [[adam_w[i]] + list(adam_e[4 * i:4 * i + 4]) for i in range(4)]

    gathered = _split_wait(_win_and_small_copies, w_ssem, w_rsem, w_first[:2], w_first[2:], [adam_w[0]],
                           "small_allgather_wait", only=tuple(range(3, 10)))[3]
    flat = _small_sum_adamw(gathered, wp, mp, vp)
    small_out = [flat[8 * kind:8 * kind + 8] for kind in range(4)]
    total_loss = flat[-1][0, 0]

    def ordered(small8, big4):
        sg1, sg2, sg3, sg4, swp, ssc, srb, ssk = small8
        swp, srb = swp[None], srb.T
        bwin, bwout, bwg, bwu, bwd = big4[0].T[None], big4[1][None], big4[2].T[None], big4[3].T[None], big4[4][None]
        return [sg1, bwin, swp, ssc, srb, ssk, bwout, sg2, sg3, bwg, bwu, bwd, sg4]

    res = [total_loss, gx[None]]
    for sm, bg in zip(small_out, (big_g, big_d, big_m, big_v)):
        res += ordered(sm, bg)
    return tuple(res)
```

```python
import functools

import numpy as np
import jax
import jax.numpy as jnp
from jax import lax
from jax.experimental import pallas as pl
from jax.experimental.pallas import tpu as pltpu

F32 = jnp.float32
BF16 = jnp.bfloat16

D = 1024
POOL_W = 512
GROUP = 128
WINDOWS = (2, 4, 8, 16)
HALO = 128
NQ = 8
BLK = 128
NBUCKET = 32
IN_W = 1280
DFF = 2816
NSH = 4
FS = DFF // NSH
WIN_S = IN_W // NSH
WOUT_S = D // NSH
EPS = 1e-6
NEG = -1e30
SCALE = 0.125

ADAM_LR = 0.001
ADAM_B1 = 0.9
ADAM_B2 = 0.999
ADAM_EPS = 1e-08
ADAM_WD = 0.01
ADAM_STEP = 10

TM = 512
TM_IN = 1024
TM_POOL = 1024
TM_FFN = 512
TM_FFN_FWD = 1024
TM_FFN_W = 2048
FFN_ROWS = 256
VMEM_LIMIT = 60 * 1024 * 1024

MESH_T = pl.DeviceIdType.MESH
ANY = pl.BlockSpec(memory_space=pl.ANY)


def _cp(n_grid=0, **kw):
    sem = ("arbitrary",) * n_grid if n_grid else None
    return pltpu.CompilerParams(dimension_semantics=sem, vmem_limit_bytes=VMEM_LIMIT, **kw)


def _dot(a, b):
    return jnp.dot(a, b, preferred_element_type=F32)


def _dot_nt(a, b):
    return lax.dot_general(a, b, (((1,), (1,)), ((), ())), preferred_element_type=F32)


def _dot_tn(a, b):
    return lax.dot_general(a, b, (((0,), (0,)), ((), ())), preferred_element_type=F32)


def _rms(x):
    r = lax.rsqrt(jnp.mean(x * x, axis=-1, keepdims=True) + EPS)
    return r, x * r


def _rms_bwd(r, n, g, dout):
    dn = dout * g
    dx = r * (dn - n * jnp.mean(dn * n, axis=-1, keepdims=True))
    dg = jnp.sum(dout * n, axis=0, keepdims=True)
    return dx, dg


def _split(x, n):
    parts = []
    r = x
    for _ in range(n):
        p = r.astype(BF16)
        parts.append(p)
        r = r - p.astype(F32)
    return parts


def _band_dot(a, x, n):
    acc = None
    for p in _split(x, n):
        t = _dot(a, p)
        acc = t if acc is None else acc + t
    return acc


def _bucket_table():
    qi = np.arange(BLK)[None, :]
    kj = np.arange(2 * BLK)[:, None]
    dist = qi + BLK - kj
    n = np.maximum(dist, 0)
    nf = np.maximum(n, 1).astype(np.float32)
    large = 16 + (np.log(nf / np.float32(16)) / np.float32(np.log(128 / 16)) * np.float32(16)).astype(np.int32)
    large = np.minimum(large, NBUCKET - 1)
    return np.where(n < 16, n, large).astype(np.int32)


def _prenorm(x, g1, dep=None):
    s = x.shape[0]
    tm = min(TM_IN, s)

    def body(x_ref, g_ref, *rest):
        _, n = _rms(x_ref[...])
        rest[-1][...] = (n * g_ref[...]).astype(BF16)

    row = pl.BlockSpec((tm, D), lambda i: (i, 0))
    return pl.pallas_call(
        body, name="prenorm", grid=(s // tm,),
        in_specs=[row, pl.BlockSpec((1, D), lambda i: (0, 0))] + ([] if dep is None else [ANY]), out_specs=row,
        out_shape=jax.ShapeDtypeStruct((s, D), BF16),
        compiler_params=_cp(1))(x, g1, *([] if dep is None else [dep]))


def _inproj_fwd(h1, w_in):
    s = h1.shape[0]
    tm = min(TM_IN, s)

    def body(h_ref, w_ref, u_ref, q_ref, k_ref, v_ref):
        proj = _dot_nt(h_ref[...], w_ref[...])
        u_ref[...] = proj[:, :512]
        q_ref[...] = proj[:, 512:1024].astype(BF16)
        k_ref[...] = proj[:, 1024:1152].astype(BF16)
        v_ref[...] = proj[:, 1152:1280].astype(BF16)

    row = lambda w: pl.BlockSpec((tm, w), lambda i: (i, 0))
    return pl.pallas_call(
        body, name="inproj_fwd", grid=(s // tm,),
        in_specs=[row(D), pl.BlockSpec((IN_W, D), lambda i: (0, 0))],
        out_specs=[row(512), row(512), row(128), row(128)],
        out_shape=[jax.ShapeDtypeStruct((s, 512), F32), jax.ShapeDtypeStruct((s, 512), BF16),
                   jax.ShapeDtypeStruct((s, 128), BF16), jax.ShapeDtypeStruct((s, 128), BF16)],
        compiler_params=_cp(1))(h1, w_in)


def _window_sums(ext, w, lag_sign, tm, terms):
    rows = lax.broadcasted_iota(jnp.int32, (HALO, 2 * HALO), 0)
    cols = lax.broadcasted_iota(jnp.int32, (HALO, 2 * HALO), 1)
    d = rows + HALO - cols if lag_sign > 0 else cols - rows
    band = jnp.where((d >= 0) & (d < w), 1.0, 0.0).astype(BF16)
    return jnp.concatenate([_band_dot(band, ext[r:r + 2 * HALO], terms) for r in range(0, tm, HALO)], axis=0)


def _pooled(ext, cur, t0, tm):
    t = t0 + lax.broadcasted_iota(jnp.int32, (tm, GROUP), 0)
    out = []
    for g, w in enumerate(WINDOWS):
        sl = slice(g * GROUP, (g + 1) * GROUP)
        cnt = jnp.minimum(t + 1, w).astype(F32)
        out.append(_window_sums(ext[:, sl], w, 1, tm, 2) / cnt - cur[:, sl])
    return out


def _pool_fwd(u, w_pool, pool_scale):
    s = u.shape[0]
    tm = min(TM_POOL, s)
    hb = tm // HALO

    def body(uc_ref, uh_ref, wp_ref, sc_ref, o_ref, pooled_ref):
        i = pl.program_id(0)
        cur = uc_ref[...]
        halo = jnp.where(i > 0, uh_ref[...], 0.0)
        ext = jnp.concatenate([halo, cur], axis=0)
        pooled = _pooled(ext, cur, i * tm, tm)
        for g in range(4):
            sl = slice(g * GROUP, (g + 1) * GROUP)
            pb = pooled[g].astype(BF16)
            pooled_ref[:, sl] = pb
            o_ref[:, sl] = (_dot(pb, wp_ref[g]) * sc_ref[:, sl]).astype(BF16)

    row = pl.BlockSpec((tm, 512), lambda i: (i, 0))
    return pl.pallas_call(
        body, name="pool_fwd", grid=(s // tm,),
        in_specs=[row, pl.BlockSpec((HALO, 512), lambda i: (jnp.maximum(i * hb - 1, 0), 0)),
                  pl.BlockSpec((4, GROUP, GROUP), lambda i: (0, 0, 0)),
                  pl.BlockSpec((1, 512), lambda i: (0, 0))],
        out_specs=[row, row],
        out_shape=[jax.ShapeDtypeStruct((s, 512), BF16)] * 2,
        compiler_params=_cp(1))(u, u, w_pool, pool_scale)


def _bias_table(bucket, rel_bias):
    def body(b_ref, rb_ref, o_ref):
        bucket_v = b_ref[...]
        key = lax.broadcasted_iota(jnp.int32, (2 * BLK, BLK), 0)
        dist = lax.broadcasted_iota(jnp.int32, (2 * BLK, BLK), 1) + BLK - key
        inwin = (dist >= 0) & (dist < BLK)
        for h in range(NQ):
            acc = jnp.zeros((2 * BLK, BLK), F32)
            for b in range(NBUCKET):
                acc = jnp.where(bucket_v == b, rb_ref[b, h], acc)
            rest = jnp.where(inwin, acc, NEG)
            o_ref[1, h] = rest
            o_ref[0, h] = jnp.where(key >= BLK, rest, NEG)

    return pl.pallas_call(
        body, name="bias_table",
        in_specs=[pl.BlockSpec(memory_space=pltpu.VMEM), pl.BlockSpec(memory_space=pltpu.SMEM)],
        out_specs=pl.BlockSpec(memory_space=pltpu.VMEM),
        out_shape=jax.ShapeDtypeStruct((2, NQ, 2 * BLK, BLK), F32),
        compiler_params=_cp())(bucket, rel_bias)


HD = 64


def _kv_band(ref, n, transposed):
    pstart = pl.multiple_of(jnp.maximum(n - 1, 0) * BLK, BLK)
    cstart = pl.multiple_of(n * BLK, BLK)
    lo = lax.broadcasted_iota(jnp.int32, (2 * BLK, 128), 1) < HD
    band = jnp.concatenate([ref[pl.ds(pstart, BLK), :], ref[pl.ds(cstart, BLK), :]], axis=0).astype(F32)
    rot = pltpu.roll(band, HD, axis=1)
    halves = ((jnp.where(lo, band, 0.0), jnp.where(lo, 0.0, rot)), (jnp.where(lo, rot, 0.0), jnp.where(lo, 0.0, band)))
    if transposed:
        out = [jnp.concatenate([a.T, b.T], axis=1).astype(BF16) for a, b in halves]
    else:
        out = [jnp.concatenate([a, b], axis=0).astype(BF16) for a, b in halves]
    return out, (lo, pstart, cstart)


def _pair_probs(qt, kk, bias, sk_ref, p):
    sink = jnp.concatenate([jnp.full((1, 1, BLK), sk_ref[0, 2 * p + e], F32) for e in range(2)], axis=0)
    s = _dot_nt(kk, qt).reshape(2, 2 * BLK, BLK) + bias
    m = jnp.maximum(jnp.max(s, axis=1, keepdims=True), sink)
    pr = jnp.exp(s - m)
    es = jnp.exp(sink - m)
    inv = 1.0 / (jnp.sum(pr, axis=1, keepdims=True) + es)
    return pr * inv, es * inv


def _attn_fwd(q, k, v, bias, sinks):
    s = q.shape[0]
    nblk = s // BLK

    def body(q_ref, k_ref, v_ref, b_ref, sk_ref, o_ref, prob_ref, ps_ref):
        n = pl.program_id(0)
        kk, _ = _kv_band(k_ref, n, False)
        vt, _ = _kv_band(v_ref, n, True)
        bidx = jnp.minimum(n, 1)
        for p in range(4):
            h = p // 2
            qt = (q_ref[:, p * 128:(p + 1) * 128].astype(F32) * SCALE).astype(BF16)
            prob, ps = _pair_probs(qt, kk[h], b_ref[bidx, pl.ds(2 * p, 2)], sk_ref, p)
            pb = prob.astype(BF16)
            prob_ref[pl.ds(2 * p, 2)] = pb
            ps_ref[pl.ds(2 * p, 2), :] = ps.reshape(2, BLK)
            acc_t = _dot(vt[h], pb.reshape(4 * BLK, BLK))
            o_ref[:, p * 128:(p + 1) * 128] = acc_t.T.astype(BF16)

    return pl.pallas_call(
        body, name="attn_fwd", grid=(nblk,),
        in_specs=[pl.BlockSpec((BLK, 512), lambda i: (i, 0)),
                  pl.BlockSpec((s, 128), lambda i: (0, 0)),
                  pl.BlockSpec((s, 128), lambda i: (0, 0)),
                  pl.BlockSpec((2, NQ, 2 * BLK, BLK), lambda i: (0, 0, 0, 0)),
                  pl.BlockSpec(memory_space=pltpu.SMEM)],
        out_specs=[pl.BlockSpec((BLK, 512), lambda i: (i, 0)),
                   pl.BlockSpec((None, NQ, 2 * BLK, BLK), lambda i: (i, 0, 0, 0)),
                   pl.BlockSpec((None, NQ, BLK), lambda i: (i, 0, 0))],
        out_shape=[jax.ShapeDtypeStruct((s, 512), BF16), jax.ShapeDtypeStruct((nblk, NQ, 2 * BLK, BLK), BF16),
                   jax.ShapeDtypeStruct((nblk, NQ, BLK), F32)],
        compiler_params=_cp(1))(q, k, v, bias, sinks)


def _outproj_fwd(pool_o, attn_o, w_out, x, g2, g3, between=None):
    s = x.shape[0]
    tm = min(TM, s)
    nt = s // tm

    def part(name, tile0, tiles, filled, dep):
        def body(p_ref, a_ref, w_ref, x_ref, g2_ref, g3_ref, *rest):
            mix_ref, x1_ref, h2_ref = rest[-3:]
            mix = _dot(p_ref[...], w_ref[0:512, :]) + _dot(a_ref[...], w_ref[512:1024, :])
            mix_ref[...] = mix
            _, n2 = _rms(mix)
            x1 = x_ref[...] + n2 * g2_ref[...]
            x1_ref[...] = x1
            _, n3 = _rms(x1)
            h2_ref[...] = (n3 * g3_ref[...]).astype(BF16)

        row = lambda w: pl.BlockSpec((tm, w), lambda i: (i + tile0, 0))
        vec = pl.BlockSpec((1, D), lambda i: (0, 0))
        extra = list(filled) + ([] if dep is None else [dep])
        return pl.pallas_call(
            body, name=name, grid=(tiles,),
            in_specs=[row(512), row(512), pl.BlockSpec((D, D), lambda i: (0, 0)), row(D), vec, vec]
            + [ANY] * len(extra),
            out_specs=[row(D), row(D), row(D)],
            out_shape=[jax.ShapeDtypeStruct((s, D), F32), jax.ShapeDtypeStruct((s, D), F32),
                       jax.ShapeDtypeStruct((s, D), BF16)],
            input_output_aliases={6 + t: t for t in range(len(filled))},
            compiler_params=_cp(1))(pool_o, attn_o, w_out, x, g2, g3, *extra)

    if between is None or nt < 2:
        return part("outproj_fwd", 0, nt, (), None)
    head = max(1, 5 * nt // 8)
    first = part("outproj_fwd_a", 0, head, (), None)
    return part("outproj_fwd_b", head, nt - head, first, between(first[2]))


def _silu_parts(g):
    sg = jax.nn.sigmoid(g)
    return sg, g * sg


def _ffn_fwd(h2, wg, wu, wd, x1, target, g4):
    s = h2.shape[0]
    tm = min(TM_FFN_FWD, s)

    def body(h_hbm, wg_ref, wu_ref, wd_ref, x1_hbm, t_hbm, g4_ref,
             gate_ref, up_ref, a_ref, df_ref, dy_ref, loss_ref, gg4_ref, f_acc, x1_ref, t_ref, sem, h_buf, h_sem):
        i = pl.program_id(0)
        j = pl.program_id(1)
        first = j == 0
        slot = i % 2

        def h_fetch(start, into):
            return pltpu.make_async_copy(h_hbm.at[pl.ds(start, tm)], h_buf.at[into], h_sem.at[into])

        @pl.when((i == 0) & first)
        def _():
            h_fetch(0, 0).start()

        @pl.when(first)
        def _():
            h_fetch(pl.multiple_of(i * tm, tm), slot).wait()

        @pl.when((j == 1) & (i + 1 < s // tm))
        def _():
            h_fetch(pl.multiple_of((i + 1) * tm, tm), 1 - slot).start()

        h_ref = h_buf.at[slot]
        tile = pl.ds(pl.multiple_of(i * tm, tm), tm)
        fetches = (pltpu.make_async_copy(x1_hbm.at[tile], x1_ref, sem.at[0]),
                   pltpu.make_async_copy(t_hbm.at[tile], t_ref, sem.at[1]))

        @pl.when(first)
        def _():
            for fetch in fetches:
                fetch.start()

        chunks = [pl.ds(r, min(FFN_ROWS, tm)) for r in range(0, tm, FFN_ROWS)]
        project = lambda rows: (_dot_nt(h_ref[rows, :], wg_ref[...]), _dot_nt(h_ref[rows, :], wu_ref[...]))
        ahead = [project(chunks[0])]
        for k, rows in enumerate(chunks):
            if k + 1 < len(chunks):
                ahead.append(project(chunks[k + 1]))
            gate, up = ahead[k]
            gate_ref[rows, :] = gate.astype(BF16)
            up_ref[rows, :] = up.astype(BF16)
            _, sl = _silu_parts(gate)
            a = (sl * up).astype(BF16)
            a_ref[rows, :] = a
            contrib = _dot(a, wd_ref[...])
            f_acc[rows, :] = jnp.where(first, contrib, f_acc[rows, :] + contrib)

        @pl.when((i == 0) & (j == 0))
        def _():
            loss_ref[...] = jnp.zeros_like(loss_ref)
            gg4_ref[...] = jnp.zeros_like(gg4_ref)

        @pl.when(j == NSH - 1)
        def _():
            for fetch in fetches:
                fetch.wait()
            r4, n4 = _rms(f_acc[...])
            g4v = g4_ref[...]
            err = x1_ref[...] + n4 * g4v - t_ref[...]
            loss_ref[...] += (0.5 / D) * jnp.sum(err * err).reshape(1, 1)
            dy = err * (1.0 / D)
            dy_ref[...] = dy
            df, dg = _rms_bwd(r4, n4, g4v, dy)
            gg4_ref[...] += dg
            df_ref[...] = df.astype(BF16)

    row = lambda w: pl.BlockSpec((tm, w), lambda i, j: (i, 0))
    sh = lambda r, c: pl.BlockSpec((None, r, c), lambda i, j: (j, 0, 0))
    act = pl.BlockSpec((None, tm, FS), lambda i, j: (j, i, 0))
    vec = pl.BlockSpec((1, D), lambda i, j: (0, 0))
    return pl.pallas_call(
        body, name="ffn_fwd", grid=(s // tm, NSH),
        in_specs=[ANY, sh(FS, D), sh(FS, D), sh(FS, D), ANY, ANY, vec],
        out_specs=[act, act, act, row(D), row(D), pl.BlockSpec((1, 1), lambda i, j: (0, 0)), vec],
        out_shape=[jax.ShapeDtypeStruct((NSH, s, FS), BF16)] * 3
        + [jax.ShapeDtypeStruct((s, D), BF16), jax.ShapeDtypeStruct((s, D), F32),
           jax.ShapeDtypeStruct((1, 1), F32), jax.ShapeDtypeStruct((1, D), F32)],
        scratch_shapes=[pltpu.VMEM((tm, D), F32)] * 3 + [pltpu.SemaphoreType.DMA((2,))]
        + [pltpu.VMEM((2, tm, D), BF16), pltpu.SemaphoreType.DMA((2,))],
        compiler_params=_cp(2))(h2, wg, wu, wd, x1, target, g4)


def _ffn_bwd_act(df, gate, up, wg, wu, wd, dy, x1, mix, g3, g2):
    s = df.shape[0]
    tm = min(TM_FFN, s)

    def body(df_ref, gate_ref, up_ref, wg_ref, wu_ref, wd_ref, dy_ref, x1_ref, mix_ref, g3_ref, g2_ref,
             dgate_ref, dup_ref, dx1_ref, dmix_ref, gg3_ref, gg2_ref, acc):
        j = pl.program_id(0)
        i = pl.program_id(1)
        first = j == 0
        tile = pl.multiple_of(i * tm, tm)
        chunks = [pl.ds(r, min(FFN_ROWS, tm)) for r in range(0, tm, FFN_ROWS)]

        @pl.when((i == 0) & (j == 0))
        def _():
            gg3_ref[...] = jnp.zeros_like(gg3_ref)
            gg2_ref[...] = jnp.zeros_like(gg2_ref)

        def norms_backward(rows, dh2):
            r3, n3 = _rms(x1_ref[rows, :])
            dx1n, dg3 = _rms_bwd(r3, n3, g3_ref[...], dh2)
            dx1 = dy_ref[rows, :] + dx1n
            dx1_ref[rows, :] = dx1
            gg3_ref[...] += dg3
            r2, n2 = _rms(mix_ref[rows, :])
            dmix, dg2 = _rms_bwd(r2, n2, g2_ref[...], dx1)
            gg2_ref[...] += dg2
            dmix_ref[rows, :] = dmix.astype(BF16)

        def shard_pass(last):
            das = [_dot_nt(df_ref[chunks[0], :], wd_ref[...])]
            for k, rows in enumerate(chunks):
                if k + 1 < len(chunks):
                    das.append(_dot_nt(df_ref[chunks[k + 1], :], wd_ref[...]))
                da = das[k]
                g = gate_ref[rows, :].astype(F32)
                u = up_ref[rows, :].astype(F32)
                sg, sl = _silu_parts(g)
                dgate = (da * u * (sg * (1.0 + g * (1.0 - sg)))).astype(BF16)
                dup = (da * sl).astype(BF16)
                dgate_ref[rows, :] = dgate
                dup_ref[rows, :] = dup
                contrib = _dot(dgate, wg_ref[...]) + _dot(dup, wu_ref[...])
                acc_rows = pl.ds(tile + k * FFN_ROWS, rows.size)
                if last:
                    norms_backward(rows, acc[acc_rows, :] + contrib)
                else:
                    acc[acc_rows, :] = jnp.where(first, contrib, acc[acc_rows, :] + contrib)

        pl.when(j < NSH - 1)(functools.partial(shard_pass, False))
        pl.when(j == NSH - 1)(functools.partial(shard_pass, True))

    row = pl.BlockSpec((tm, D), lambda j, i: (i, 0))
    last = pl.BlockSpec((tm, D), lambda j, i: (i * (j // (NSH - 1)), 0))
    sh = pl.BlockSpec((None, FS, D), lambda j, i: (j, 0, 0))
    act = pl.BlockSpec((None, tm, FS), lambda j, i: (j, i, 0))
    vec = pl.BlockSpec((1, D), lambda j, i: (0, 0))
    return pl.pallas_call(
        body, name="ffn_bwd_act", grid=(NSH, s // tm),
        in_specs=[row, act, act, sh, sh, sh, last, last, last, vec, vec],
        out_specs=[act, act, last, last, vec, vec],
        out_shape=[jax.ShapeDtypeStruct((NSH, s, FS), BF16), jax.ShapeDtypeStruct((NSH, s, FS), BF16),
                   jax.ShapeDtypeStruct((s, D), F32), jax.ShapeDtypeStruct((s, D), BF16),
                   jax.ShapeDtypeStruct((1, D), F32), jax.ShapeDtypeStruct((1, D), F32)],
        scratch_shapes=[pltpu.VMEM((s, D), F32)],
        compiler_params=_cp(2))(df, gate, up, wg, wu, wd, dy, x1, mix, g3, g2)


SMEM_SPEC = pl.BlockSpec(memory_space=pltpu.SMEM)


def _emit_halves(acc_ref, row0, rows, c, own_ref, oth_ref):
    half = rows // 2
    own_ref[...] = acc_ref[pl.ds(row0 + pl.multiple_of(c * half, 8), half), :]
    oth_ref[...] = acc_ref[pl.ds(row0 + pl.multiple_of((1 - c) * half, 8), half), :].astype(BF16)


def _half_shapes(rows):
    return [jax.ShapeDtypeStruct((NSH, rows // 2, D), F32), jax.ShapeDtypeStruct((NSH, rows // 2, D), BF16)]


def _ffn_bwd_w(h2, act_a, dgate, dup, df, c_arr):
    s = h2.shape[0]
    tm = min(TM_FFN_W, s)
    nt = s // tm

    def body(c_ref, h_ref, a_ref, dgate_ref, dup_ref, df_ref, *rest):
        outs, accs = rest[:6], rest[6:]
        i = pl.program_id(1)

        @pl.when(i == 0)
        def _():
            for acc in accs:
                acc[...] = jnp.zeros_like(acc)

        h = h_ref[...]
        accs[0][...] += _dot_tn(dgate_ref[...], h)
        accs[1][...] += _dot_tn(dup_ref[...], h)
        accs[2][...] += _dot_tn(a_ref[...], df_ref[...])

        @pl.when(i == nt - 1)
        def _():
            for t, acc in enumerate(accs):
                _emit_halves(acc, 0, FS, c_ref[0], outs[2 * t], outs[2 * t + 1])

    row = lambda w: pl.BlockSpec((tm, w), lambda j, i: (i, 0))
    act = pl.BlockSpec((None, tm, FS), lambda j, i: (j, i, 0))
    half = pl.BlockSpec((None, FS // 2, D), lambda j, i: (j, 0, 0))
    return pl.pallas_call(
        body, name="ffn_bwd_w", grid=(NSH, nt),
        in_specs=[SMEM_SPEC, row(D), act, act, act, row(D)],
        out_specs=[half] * 6,
        out_shape=_half_shapes(FS) * 3,
        scratch_shapes=[pltpu.VMEM((FS, D), F32)] * 3,
        compiler_params=_cp(2))(c_arr, h2, act_a, dgate, dup, df)


def _outproj_bwd(dmix, w_out, pool_o, attn_o, c_arr, dep=None):
    s = dmix.shape[0]
    tm = min(TM, s)
    nt = s // tm

    def body(c_ref, dm_ref, w_ref, p_ref, a_ref, *rest):
        dpool_ref, dattn_ref, own_ref, oth_ref, gw_ref = rest[-5:]
        i = pl.program_id(0)

        @pl.when(i == 0)
        def _():
            gw_ref[...] = jnp.zeros_like(gw_ref)

        dm = dm_ref[...]
        dpool_ref[...] = _dot_nt(dm, w_ref[0:512, :])
        dattn_ref[...] = _dot_nt(dm, w_ref[512:1024, :]).astype(BF16)
        gw_ref[0:512, :] += _dot_tn(p_ref[...], dm)
        gw_ref[512:1024, :] += _dot_tn(a_ref[...], dm)

        @pl.when(i == nt - 1)
        def _():
            for k in range(NSH):
                _emit_halves(gw_ref, k * WOUT_S, WOUT_S, c_ref[0], own_ref.at[k], oth_ref.at[k])

    row = lambda w: pl.BlockSpec((tm, w), lambda i: (i, 0))
    full = pl.BlockSpec((D, D), lambda i: (0, 0))
    half = pl.BlockSpec((NSH, WOUT_S // 2, D), lambda i: (0, 0, 0))
    return pl.pallas_call(
        body, name="outproj_bwd", grid=(nt,),
        in_specs=[SMEM_SPEC, row(D), full, row(512), row(512)] + ([] if dep is None else [ANY]),
        out_specs=[row(512), row(512), half, half],
        out_shape=[jax.ShapeDtypeStruct((s, 512), F32), jax.ShapeDtypeStruct((s, 512), BF16)] + _half_shapes(WOUT_S),
        scratch_shapes=[pltpu.VMEM((D, D), F32)],
        compiler_params=_cp(1))(c_arr, dmix, w_out, pool_o, attn_o, *([] if dep is None else [dep]))


def _pool_bwd(pooled, dpool, w_pool, pool_scale, dep=None):
    s = pooled.shape[0]
    tm = min(TM_POOL, s)
    hb = tm // HALO
    nt = s // tm
    last_halo = s // HALO - 1

    def body(p_ref, dc_ref, dn_ref, wp_ref, sc_ref, *rest):
        du_ref, gwp_ref, gsc_ref = rest[-3:]
        i = pl.program_id(0)

        @pl.when(i == 0)
        def _():
            gwp_ref[...] = jnp.zeros_like(gwp_ref)
            gsc_ref[...] = jnp.zeros_like(gsc_ref)

        dcur = dc_ref[...]
        dnext = jnp.where(i < nt - 1, dn_ref[...], 0.0)
        dext = jnp.concatenate([dcur, dnext], axis=0)
        t_ext = i * tm + lax.broadcasted_iota(jnp.int32, (tm + HALO, GROUP), 0)
        for g, w in enumerate(WINDOWS):
            sl = slice(g * GROUP, (g + 1) * GROUP)
            pb = p_ref[:, sl]
            wp = wp_ref[g]
            mixed = _dot(pb, wp)
            gsc_ref[:, sl] += jnp.sum(dcur[:, sl] * mixed, axis=0, keepdims=True)
            dmixed = (dext[:, sl] * sc_ref[:, sl]).astype(BF16)
            gwp_ref[g] += _dot_tn(pb, dmixed[0:tm])
            dpooled = _dot_nt(dmixed, wp)
            z = dpooled / jnp.minimum(t_ext + 1, w).astype(F32)
            du_ref[:, sl] = (_window_sums(z, w, -1, tm, 2) - dpooled[0:tm]).astype(BF16)

    return pl.pallas_call(
        body, name="pool_bwd", grid=(nt,),
        in_specs=[pl.BlockSpec((tm, 512), lambda i: (i, 0)),
                  pl.BlockSpec((tm, 512), lambda i: (i, 0)),
                  pl.BlockSpec((HALO, 512), lambda i: (jnp.minimum((i + 1) * hb, last_halo), 0)),
                  pl.BlockSpec((4, GROUP, GROUP), lambda i: (0, 0, 0)),
                  pl.BlockSpec((1, 512), lambda i: (0, 0))] + ([] if dep is None else [ANY]),
        out_specs=[pl.BlockSpec((tm, 512), lambda i: (i, 0)),
                   pl.BlockSpec((4, GROUP, GROUP), lambda i: (0, 0, 0)),
                   pl.BlockSpec((1, 512), lambda i: (0, 0))],
        out_shape=[jax.ShapeDtypeStruct((s, 512), BF16), jax.ShapeDtypeStruct((4, GROUP, GROUP), F32),
                   jax.ShapeDtypeStruct((1, 512), F32)],
        compiler_params=_cp(1))(pooled, dpool, dpool, w_pool, pool_scale, *([] if dep is None else [dep]))


def _attn_bwd(q, k, v, do, probs, sink_share, bucket, dep=None):
    s = q.shape[0]
    nblk = s // BLK

    def body(q_ref, do_ref, k_ref, v_ref, prob_ref, ps_ref, bk_ref, *rest):
        dq_ref, dk_ref, dv_ref, grb_ref, gsk_ref, dss_ref = rest[-6:]
        n = pl.program_id(0)

        @pl.when(n == 0)
        def _():
            dk_ref[...] = jnp.zeros_like(dk_ref)
            dv_ref[...] = jnp.zeros_like(dv_ref)
            dss_ref[...] = jnp.zeros_like(dss_ref)
            gsk_ref[...] = jnp.zeros_like(gsk_ref)

        kt, (lo, pstart, cstart) = _kv_band(k_ref, n, True)
        vv, _ = _kv_band(v_ref, n, False)
        lo_q = lax.broadcasted_iota(jnp.int32, (BLK, 128), 1) < HD
        dkk = [jnp.zeros((2 * BLK, 128), F32), jnp.zeros((2 * BLK, 128), F32)]
        dvv = [jnp.zeros((2 * BLK, 128), F32), jnp.zeros((2 * BLK, 128), F32)]

        def by_head(t):
            return jnp.concatenate([jnp.where(lo_q, t, 0.0), jnp.where(lo_q, 0.0, t)], axis=0).astype(BF16)

        for p in range(4):
            h = p // 2
            qt = q_ref[:, p * 128:(p + 1) * 128].astype(F32) * SCALE
            dot = do_ref[:, p * 128:(p + 1) * 128]
            pb = prob_ref[pl.ds(2 * p, 2)]
            prob = pb.astype(F32)
            ps = ps_ref[pl.ds(2 * p, 2), :].reshape(2, 1, BLK)
            dp = _dot_nt(vv[h], dot).reshape(2, 2 * BLK, BLK)
            delta = jnp.sum(prob * dp, axis=1, keepdims=True)
            ds = prob * (dp - delta)
            sink_part = ps * delta
            for e in range(2):
                gs = -jnp.sum(sink_part[e]).reshape(1, 1)
                gsk_ref[pl.ds(2 * p + e, 1), :] += jnp.broadcast_to(gs, (1, 128))
            dss_ref[pl.ds(2 * p, 2)] += ds
            dsb = ds.astype(BF16)
            dq_t = _dot(kt[h], dsb.reshape(4 * BLK, BLK))
            dq_ref[:, p * 128:(p + 1) * 128] = (dq_t.T * SCALE).astype(BF16)
            dkk[h] = dkk[h] + _dot(jnp.concatenate([dsb[0], dsb[1]], axis=1), by_head(qt))
            dvv[h] = dvv[h] + _dot(jnp.concatenate([pb[0], pb[1]], axis=1), by_head(dot.astype(F32)))
        for acc, ref in ((dkk, dk_ref), (dvv, dv_ref)):
            f0 = acc[0] + pltpu.roll(acc[0], HD, axis=1)
            f1 = acc[1] + pltpu.roll(acc[1], HD, axis=1)
            band = jnp.where(lo, f0, f1)
            ref[pl.ds(pstart, BLK), :] += band[0:BLK]
            ref[pl.ds(cstart, BLK), :] += band[BLK:2 * BLK]

        @pl.when(n == nblk - 1)
        def _():
            _relbias_grad(dss_ref, bk_ref[...], grb_ref)

    blk = pl.BlockSpec((BLK, 512), lambda i: (i, 0))
    kv = pl.BlockSpec((s, 128), lambda i: (0, 0))
    row8 = pl.BlockSpec((NQ, 128), lambda i: (0, 0))
    return pl.pallas_call(
        body, name="attn_bwd", grid=(nblk,),
        in_specs=[blk, blk, kv, kv, pl.BlockSpec((None, NQ, 2 * BLK, BLK), lambda i: (i, 0, 0, 0)),
                  pl.BlockSpec((None, NQ, BLK), lambda i: (i, 0, 0)), pl.BlockSpec((2 * BLK, BLK), lambda i: (0, 0))]
        + ([] if dep is None else [ANY]),
        out_specs=[blk, kv, kv, row8, row8],
        out_shape=[jax.ShapeDtypeStruct((s, 512), BF16), jax.ShapeDtypeStruct((s, 128), F32),
                   jax.ShapeDtypeStruct((s, 128), F32), jax.ShapeDtypeStruct((NQ, 128), F32),
                   jax.ShapeDtypeStruct((NQ, 128), F32)],
        scratch_shapes=[pltpu.VMEM((NQ, 2 * BLK, BLK), F32)],
        compiler_params=_cp(1))(q, do, k, v, probs, sink_share, bucket, *([] if dep is None else [dep]))


def _relbias_grad(ds_ref, bucket_v, o_ref):
    lane = lax.broadcasted_iota(jnp.int32, (NQ, 128), 1)
    head_row = lax.broadcasted_iota(jnp.int32, (NQ, BLK), 0)
    acc = jnp.zeros((NQ, 128), F32)
    for b in range(NBUCKET):
        sel = bucket_v == b
        stack = jnp.zeros((NQ, BLK), F32)
        for h in range(NQ):
            col_sums = jnp.sum(jnp.where(sel, ds_ref[h], 0.0), axis=0, keepdims=True)
            stack = jnp.where(head_row == h, col_sums, stack)
        acc = acc + jnp.where(lane == b, jnp.sum(stack, axis=1, keepdims=True), 0.0)
    o_ref[...] = acc


def _inproj_bwd(x, g1, du, dq, dk, dv, w_in, dx1, c_arr):
    s = x.shape[0]
    tm = min(TM, s)
    nt = s // tm
    cols = ((0, 512), (512, 1024), (1024, 1152), (1152, 1280))

    def body(c_ref, x_ref, g_ref, du_ref, dq_ref, dk_ref, dv_ref, w_ref, dx1_ref,
             gx_ref, own_ref, oth_ref, gg_ref, gw_ref):
        i = pl.program_id(0)

        @pl.when(i == 0)
        def _():
            gw_ref[...] = jnp.zeros_like(gw_ref)
            gg_ref[...] = jnp.zeros_like(gg_ref)

        parts = (du_ref[...], dq_ref[...], dk_ref[...].astype(BF16), dv_ref[...].astype(BF16))
        dh = None
        for (a, b), dpart in zip(cols, parts):
            t = _dot(dpart, w_ref[a:b, :])
            dh = t if dh is None else dh + t
        gv = g_ref[...]
        r1, n1 = _rms(x_ref[...])
        h = (n1 * gv).astype(BF16)
        for (a, b), dpart in zip(cols, parts):
            gw_ref[a:b, :] += _dot_tn(dpart, h)
        dx, dg = _rms_bwd(r1, n1, gv, dh)
        gg_ref[...] += dg
        gx_ref[...] = dx1_ref[...] + dx

        @pl.when(i == nt - 1)
        def _():
            for k in range(NSH):
                _emit_halves(gw_ref, k * WIN_S, WIN_S, c_ref[0], own_ref.at[k], oth_ref.at[k])

    row = lambda w: pl.BlockSpec((tm, w), lambda i: (i, 0))
    vec = pl.BlockSpec((1, D), lambda i: (0, 0))
    full = pl.BlockSpec((IN_W, D), lambda i: (0, 0))
    half = pl.BlockSpec((NSH, WIN_S // 2, D), lambda i: (0, 0, 0))
    return pl.pallas_call(
        body, name="inproj_bwd", grid=(nt,),
        in_specs=[SMEM_SPEC, row(D), vec, row(512), row(512), row(128), row(128), full, row(D)],
        out_specs=[row(D), half, half, vec],
        out_shape=[jax.ShapeDtypeStruct((s, D), F32)] + _half_shapes(WIN_S) + [jax.ShapeDtypeStruct((1, D), F32)],
        scratch_shapes=[pltpu.VMEM((IN_W, D), F32)],
        compiler_params=_cp(1))(c_arr, x, g1, du, dq, dk, dv, w_in, dx1)


def _local_step(x, target, small, w_in, w_out, ffn_weights, c_arr, on_ffn_grads=None, on_wout_grads=None,
                mid_outproj=None, start_dep=None):
    g1, g2, g3, g4, w_pool, pool_scale, rel_bias, sinks = small
    bucket = jnp.asarray(_bucket_table())
    wp_bf = w_pool.astype(BF16)
    bias = _bias_table(bucket, rel_bias)
    h1 = _prenorm(x, g1, start_dep)
    if callable(w_in):
        w_in = w_in(bias, h1)
    u, q, k, v = _inproj_fwd(h1, w_in)
    pool_o, pooled = _pool_fwd(u, wp_bf, pool_scale)
    attn_o, probs, sink_share = _attn_fwd(q, k, v, bias, sinks)
    g2_fwd = g2
    if callable(w_out):
        w_out, after = w_out(pool_o, attn_o)
        if after is not None:
            g2_fwd = g2 + after[:1, :1]
    mix, x1, h2 = _outproj_fwd(pool_o, attn_o, w_out, x, g2_fwd, g3, mid_outproj)
    wg, wu, wd = ffn_weights(h2) if callable(ffn_weights) else ffn_weights
    gate, up, act_a, df, dy, loss, gg4 = _ffn_fwd(h2, wg, wu, wd, x1, target, g4)
    dgate, dup, dx1, dmix, gg3, gg2 = _ffn_bwd_act(df, gate, up, wg, wu, wd, dy, x1, mix, g3, g2)
    ffn_g = _ffn_bwd_w(h2, act_a, dgate, dup, df, c_arr)
    dep = None if on_ffn_grads is None else on_ffn_grads(ffn_g)
    dpool, dattn, wout_own, wout_oth = _outproj_bwd(dmix, w_out, pool_o, attn_o, c_arr, dep)
    dep = None if on_wout_grads is None else on_wout_grads(wout_own, wout_oth, dpool)
    du, gwp, gsc = _pool_bwd(pooled, dpool, wp_bf, pool_scale, dep)
    dq, dk, dv, grb, gsk = _attn_bwd(q, k, v, dattn, probs, sink_share, bucket, dep)
    gx, win_own, win_oth, gg1 = _inproj_bwd(x, g1, du, dq, dk, dv, w_in, dx1, c_arr)
    small_grads = (gg1, gg2, gg3, gg4, gwp, gsc, grb, gsk[:, 0].reshape(1, NQ))
    return loss, gx, small_grads, ((win_own, win_oth), (wout_own, wout_oth), ffn_g)


def _place():
    x, y, c = lax.axis_index("x"), lax.axis_index("y"), lax.axis_index("c")
    chips = ((1 - x, y), (x, 1 - y), (1 - x, 1 - y))
    return x, y, c, chips


def _remote(src, dst, ssem, rsem, dev):
    return pltpu.make_async_remote_copy(src_ref=src, dst_ref=dst, send_sem=ssem, recv_sem=rsem,
                                        device_id=dev, device_id_type=MESH_T)


def _to_sibling(arrs, name, after=()):
    nt, na = len(arrs), len(after)

    def body(*refs):
        srcs, dsts = refs[:nt], refs[nt + na:2 * nt + na]
        ssem, rsem = refs[2 * nt + na:]
        x, y, c, _ = _place()
        cps = [_remote(srcs[t], dsts[t], ssem.at[t], rsem.at[t], (x, y, 1 - c)) for t in range(nt)]
        for cp in cps:
            cp.start()
        for cp in cps:
            cp.wait()

    return pl.pallas_call(
        body, name=name, in_specs=[ANY] * (nt + na), out_specs=[ANY] * nt,
        out_shape=[jax.ShapeDtypeStruct(a.shape, a.dtype) for a in arrs],
        scratch_shapes=[pltpu.SemaphoreType.DMA((nt,)), pltpu.SemaphoreType.DMA((nt,))],
        compiler_params=_cp())(*arrs, *after)


HBM_SPEC = pl.BlockSpec(memory_space=pltpu.HBM)
SEM_SPEC = pl.BlockSpec(memory_space=pltpu.SEMAPHORE)
DATAFLOW = pltpu.SideEffectType.DATAFLOW_SIDE_EFFECTING


def _cast_into_slots(ws, chip_arr):
    nt, tiles = len(ws), 4

    def body(chip_ref, *refs):
        for t in range(nt):
            refs[nt + t][...] = refs[t][...].astype(BF16)

    return pl.pallas_call(
        body, name="cast_weights",
        grid_spec=pltpu.PrefetchScalarGridSpec(
            num_scalar_prefetch=1, grid=(tiles,),
            in_specs=[pl.BlockSpec((w.shape[0] // tiles, w.shape[1]), lambda i, chip_ref: (i, 0)) for w in ws],
            out_specs=[pl.BlockSpec((None, w.shape[0] // tiles, w.shape[1]), lambda i, chip_ref: (chip_ref[0], i, 0))
                       for w in ws]),
        out_shape=[jax.ShapeDtypeStruct((NSH,) + w.shape, BF16) for w in ws],
        compiler_params=_cp(1))(chip_arr, *ws)


def _gather_copies(srcs, lands, ssem, rsem, first=0):
    x, y, c, chips = _place()
    me = 2 * x + y
    out = []
    for t in range(len(lands)):
        half = lands[t].shape[1] // 2
        mine = pl.ds(pl.multiple_of(c * half, 16), half)
        for j, (px, py) in enumerate(chips):
            k = 3 * (first + t) + j
            send = _remote(lands[t].at[me, mine], lands[t].at[me, mine], ssem.at[k], rsem.at[k], (px, py, c))
            blk = lands[t].at[2 * px + py, mine]
            out.append((send, _remote(blk, blk, ssem.at[k], rsem.at[k], (px, py, c))))
    return out


def _scatter_copies(srcs, lands, ssem, rsem):
    x, y, c, chips = _place()
    out = []
    for t in range(len(srcs)):
        for j, (px, py) in enumerate(chips):
            k = 3 * t + j
            send = _remote(srcs[t].at[2 * px + py], lands[t].at[j], ssem.at[k], rsem.at[k], (px, py, c))
            out.append((send, _remote(lands[t].at[j], lands[t].at[j], ssem.at[k], rsem.at[k], (px, py, c))))
    return out


def _sibling_copies(srcs, lands, ssem, rsem):
    x, y, c, _ = _place()
    sib = (x, y, 1 - c)
    return [(_remote(srcs[t], lands[t], ssem.at[t], rsem.at[t], sib),
             _remote(lands[t], lands[t], ssem.at[t], rsem.at[t], sib)) for t in range(len(srcs))]


def _peer_copies(srcs, lands, ssem, rsem):
    x, y, c, _ = _place()
    me = 4 * x + 2 * y + c
    out = []
    for kk in range(1, 8):
        px, py, pc = x ^ (kk >> 2), y ^ ((kk >> 1) & 1), c ^ (kk & 1)
        send = _remote(srcs[0], lands[0].at[me], ssem.at[kk - 1], rsem.at[kk - 1], (px, py, pc))
        slot = lands[0].at[4 * px + 2 * py + pc]
        out.append((send, _remote(slot, slot, ssem.at[kk - 1], rsem.at[kk - 1], (px, py, pc))))
    return out


def _forward_copies(srcs, lands, ssem, rsem):
    x, y, c, chips = _place()
    sib = (x, y, 1 - c)
    out = []
    for t in range(len(lands)):
        half = lands[t].shape[1] // 2
        mine = pl.ds(pl.multiple_of(c * half, 16), half)
        other = pl.ds(pl.multiple_of((1 - c) * half, 16), half)
        for j, (px, py) in enumerate(chips):
            k = 3 * t + j
            blk_m, blk_o = lands[t].at[2 * px + py, mine], lands[t].at[2 * px + py, other]
            out.append((_remote(blk_m, blk_m, ssem.at[k], rsem.at[k], sib),
                        _remote(blk_o, blk_o, ssem.at[k], rsem.at[k], sib)))
    return out


def _win_and_small_copies(srcs, lands, ssem, rsem):
    return (_scatter_copies(srcs[:1], lands[:1], ssem, rsem)
            + _peer_copies(srcs[1:], lands[1:], ssem.at[pl.ds(3, 7)], rsem.at[pl.ds(3, 7)]))


def _split_start(plan, ncopy, srcs, lands, after, name):
    ns, nbuf = len(srcs), len(srcs) + len(lands)
    extra = [] if after is None else [after]
    n_in = nbuf + len(extra)

    def body(*refs):
        ssem, rsem, token = refs[n_in], refs[n_in + 1], refs[-1]
        for send, _ in plan(refs[:ns], refs[ns:nbuf], ssem, rsem):
            send.start()
        token[...] = jnp.zeros_like(token)

    bufs = [pltpu.with_memory_space_constraint(a, pltpu.HBM) for a in list(srcs) + list(lands)]
    outs = pl.pallas_call(
        body, name=name, in_specs=[HBM_SPEC] * nbuf + [ANY] * len(extra),
        out_specs=[SEM_SPEC, SEM_SPEC] + [HBM_SPEC] * nbuf + [pl.BlockSpec(memory_space=pltpu.VMEM)],
        out_shape=[pltpu.SemaphoreType.DMA((ncopy,)), pltpu.SemaphoreType.DMA((ncopy,))]
        + [pltpu.HBM(a.shape, a.dtype) for a in bufs] + [jax.ShapeDtypeStruct((8, 128), F32)],
        input_output_aliases={i: 2 + i for i in range(nbuf)},
        compiler_params=pltpu.CompilerParams(has_side_effects=DATAFLOW))(*bufs, *extra)
    return outs[0], outs[1], outs[2:2 + ns], outs[2 + ns:2 + nbuf], outs[-1]


def _split_wait(plan, ssem, rsem, srcs, lands, after, name, only=None):
    ns, nbuf = len(srcs), len(srcs) + len(lands)

    def body(*refs):
        s_ref, r_ref = refs[nbuf], refs[nbuf + 1]
        for k, (send, recv) in enumerate(plan(refs[:ns], refs[ns:nbuf], s_ref, r_ref)):
            if only is None or k in only:
                send.wait_send()
                recv.wait_recv()

    outs = pl.pallas_call(
        body, name=name, in_specs=[HBM_SPEC] * nbuf + [SEM_SPEC, SEM_SPEC] + [ANY] * len(after),
        out_specs=[HBM_SPEC] * nbuf,
        out_shape=[pltpu.HBM(a.shape, a.dtype) for a in list(srcs) + list(lands)],
        input_output_aliases={i: i for i in range(nbuf)},
        compiler_params=pltpu.CompilerParams(has_side_effects=DATAFLOW))(*srcs, *lands, ssem, rsem, *after)
    return outs


def _gather_finish(lands, tag):
    nt = len(lands)

    def body(*refs):
        outs = refs[nt:2 * nt]
        dsend, drecv = refs[2 * nt:]
        x, y, c, chips = _place()
        sib = (x, y, 1 - c)
        sends = []
        for t in range(nt):
            half = outs[t].shape[1] // 2
            mine = pl.ds(pl.multiple_of(c * half, 16), half)
            for j, (px, py) in enumerate(chips):
                blk = outs[t].at[2 * px + py, mine]
                cp = _remote(blk, blk, dsend.at[t, j], drecv.at[t, j], sib)
                cp.start()
                sends.append(cp)
        for t in range(nt):
            half = outs[t].shape[1] // 2
            other = pl.ds(pl.multiple_of((1 - c) * half, 16), half)
            for j, (px, py) in enumerate(chips):
                blk = outs[t].at[2 * px + py, other]
                _remote(blk, blk, dsend.at[t, j], drecv.at[t, j], sib).wait_recv()
        for cp in sends:
            cp.wait_send()

    return pl.pallas_call(
        body, name="gather_finish_" + tag, in_specs=[ANY] * nt, out_specs=[ANY] * nt,
        out_shape=[jax.ShapeDtypeStruct(a.shape, a.dtype) for a in lands],
        input_output_aliases={t: t for t in range(nt)},
        scratch_shapes=[pltpu.SemaphoreType.DMA((nt, 3)), pltpu.SemaphoreType.DMA((nt, 3))],
        compiler_params=_cp())(*lands)


def _chip_partial(owns, recv, chip_arr, tag, whole=False):
    nt = len(owns)

    if whole:
        def body_whole(chip_ref, *refs):
            for t in range(nt):
                refs[2 * nt + t][...] = (refs[t][...] + refs[nt + t][...].astype(F32)).astype(BF16)
                mine = chip_ref[0]
                refs[3 * nt + t][...] = refs[t][mine] + refs[nt + t][mine].astype(F32)

        vm = pl.BlockSpec(memory_space=pltpu.VMEM)
        return pl.pallas_call(
            body_whole, name="rs_chip_partial_" + tag, in_specs=[SMEM_SPEC] + [vm] * (2 * nt), out_specs=[vm] * (2 * nt),
            out_shape=[jax.ShapeDtypeStruct(a.shape, BF16) for a in owns]
            + [jax.ShapeDtypeStruct(a.shape[1:], F32) for a in owns],
            compiler_params=_cp())(chip_arr, *owns, *recv)

    def body(chip_ref, *refs):
        k = pl.program_id(0)
        for t in range(nt):
            p = refs[t][...] + refs[nt + t][...].astype(F32)
            refs[2 * nt + t][...] = p.astype(BF16)

            @pl.when(k == chip_ref[0])
            def _():
                refs[3 * nt + t][...] = p

    blk_specs = [pl.BlockSpec((None,) + a.shape[1:], lambda k, chip_ref: (k, 0, 0)) for a in owns]
    own_specs = [pl.BlockSpec(a.shape[1:], lambda k, chip_ref: (0, 0)) for a in owns]
    return pl.pallas_call(
        body, name="rs_chip_partial_" + tag,
        grid_spec=pltpu.PrefetchScalarGridSpec(num_scalar_prefetch=1, grid=(NSH,), in_specs=blk_specs * 2,
                                               out_specs=blk_specs + own_specs),
        out_shape=[jax.ShapeDtypeStruct(a.shape, BF16) for a in owns]
        + [jax.ShapeDtypeStruct(a.shape[1:], F32) for a in owns],
        compiler_params=_cp(1))(chip_arr, *owns, *recv)


def _sum_chips(own, recv, tag):
    nt = len(own)

    def body(*refs):
        outs = refs[2 * nt:]
        for t in range(nt):
            r = refs[nt + t]
            outs[t][...] = ((refs[t][...] + r[0].astype(F32)) + r[1].astype(F32)) + r[2].astype(F32)

    vm = pl.BlockSpec(memory_space=pltpu.VMEM)
    return pl.pallas_call(
        body, name="rs_sum_chips_" + tag, in_specs=[vm] * (2 * nt), out_specs=[vm] * nt,
        out_shape=[jax.ShapeDtypeStruct(a.shape, F32) for a in own],
        compiler_params=_cp())(*own, *recv)


def _sum_chips_shared(own, recv, tag):
    def body(own_ref, recv_ref, out_ref, sib_ref, ssem, rsem):
        out_ref[...] = ((own_ref[...] + recv_ref[0].astype(F32)) + recv_ref[1].astype(F32)) + recv_ref[2].astype(F32)
        x, y, c, _ = _place()
        cp = _remote(out_ref, sib_ref, ssem.at[0], rsem.at[0], (x, y, 1 - c))
        cp.start()
        cp.wait()

    vm = pl.BlockSpec(memory_space=pltpu.VMEM)
    return pl.pallas_call(
        body, name="rs_sum_share_" + tag, in_specs=[vm, vm], out_specs=[vm, ANY],
        out_shape=[jax.ShapeDtypeStruct(own.shape, F32)] * 2,
        scratch_shapes=[pltpu.SemaphoreType.DMA((1,)), pltpu.SemaphoreType.DMA((1,))],
        compiler_params=_cp())(own, recv)


def _adamw_math(w, g, m, v):
    m = ADAM_B1 * m + (1.0 - ADAM_B1) * g
    v = ADAM_B2 * v + (1.0 - ADAM_B2) * (g * g)
    m_hat = m / (1.0 - ADAM_B1 ** ADAM_STEP)
    v_hat = v / (1.0 - ADAM_B2 ** ADAM_STEP)
    delta = -ADAM_LR * (m_hat / (jnp.sqrt(v_hat) + ADAM_EPS) + ADAM_WD * w)
    return delta, m, v


ADAM_SPLIT = 4


def _adamw_shards(own, sib, ws, ms, vs, c_arr, tag):
    nt = len(own)

    def body(c_ref, *refs):
        hh = pl.program_id(0)
        mine = hh == c_ref[0]
        for t in range(nt):
            g = jnp.where(mine, refs[t][...], refs[nt + t][...])
            delta, m, v = _adamw_math(refs[2 * nt + t][...], g, refs[3 * nt + t][...], refs[4 * nt + t][...])
            refs[5 * nt + t][...] = g
            refs[6 * nt + t][...] = delta
            refs[7 * nt + t][...] = m
            refs[8 * nt + t][...] = v

    def tile(a):
        return (a.shape[0] // ADAM_SPLIT, a.shape[1])

    def half_index(used_when_mine):
        def index(hh, q, c_ref):
            mine = 1 - (hh - c_ref[0]) * (hh - c_ref[0])
            used = mine if used_when_mine else 1 - mine
            return (used * q + (1 - used) * hh * (ADAM_SPLIT - 1), 0)
        return index

    own_specs = [pl.BlockSpec(tile(a), half_index(True)) for a in own]
    sib_specs = [pl.BlockSpec(tile(a), half_index(False)) for a in own]
    full_specs = [pl.BlockSpec(tile(a), lambda hh, q, c_ref: (hh * ADAM_SPLIT + q, 0)) for a in own]
    return pl.pallas_call(
        body, name="adamw_shards_" + tag,
        grid_spec=pltpu.PrefetchScalarGridSpec(num_scalar_prefetch=1, grid=(2, ADAM_SPLIT),
                                               in_specs=own_specs + sib_specs + full_specs * 3,
                                               out_specs=full_specs * 4),
        out_shape=[jax.ShapeDtypeStruct(w.shape, F32) for w in ws] * 4,
        compiler_params=_cp(2))(c_arr, *own, *sib, *ws, *ms, *vs)


SMALL_WPOOL_ROW = 32
SMALL_SCALE_ROW = SMALL_WPOOL_ROW + 4 * GROUP
SMALL_BIAS_ROW = SMALL_SCALE_ROW + 8
SMALL_SINKS_ROW = SMALL_BIAS_ROW + NQ
SMALL_LOSS_ROW = SMALL_SINKS_ROW + 8


def _small_sum_adamw(gathered, wp, mp, vp):
    rows = wp.shape[0]

    def body(g_ref, w_ref, m_ref, v_ref, *rest):
        outs, res = rest[:-1], rest[-1]
        g = g_ref[0]
        for d in range(1, 8):
            g = g + g_ref[d]
        delta, m, v = _adamw_math(w_ref[...], g, m_ref[...], v_ref[...])
        for kind, val in enumerate((g, delta, m, v)):
            res[kind] = val
            gains = outs[8 * kind:8 * kind + 4]
            w_pool_o, scale_o, bias_o, sinks_o = outs[8 * kind + 4:8 * kind + 8]
            for t in range(4):
                for i in range(8):
                    gains[t][:, 128 * i:128 * (i + 1)] = res[kind, pl.ds(8 * t + i, 1), :]
            for grp in range(4):
                w_pool_o[grp] = res[kind, pl.ds(SMALL_WPOOL_ROW + GROUP * grp, GROUP), :]
            for i in range(4):
                scale_o[:, 128 * i:128 * (i + 1)] = res[kind, pl.ds(SMALL_SCALE_ROW + i, 1), :]
            bias_o[...] = res[kind, pl.ds(SMALL_BIAS_ROW, NQ), :][:, 0:NBUCKET]
            sinks_o[...] = res[kind, pl.ds(SMALL_SINKS_ROW, 1), :][:, 0:NQ]
        outs[-1][...] = res[0, pl.ds(SMALL_LOSS_ROW, 1), :]

    vm = pl.BlockSpec(memory_space=pltpu.VMEM)
    one_kind = [jax.ShapeDtypeStruct((1, D), F32)] * 4 + [
        jax.ShapeDtypeStruct((4, GROUP, GROUP), F32), jax.ShapeDtypeStruct((1, POOL_W), F32),
        jax.ShapeDtypeStruct((NQ, NBUCKET), F32), jax.ShapeDtypeStruct((1, NQ), F32)]
    return pl.pallas_call(
        body, name="small_sum_adamw", in_specs=[vm] * 4, out_specs=[vm] * 33,
        out_shape=one_kind * 4 + [jax.ShapeDtypeStruct((1, 128), F32)],
        scratch_shapes=[pltpu.VMEM((4, rows, 128), F32)],
        compiler_params=_cp())(gathered, wp, mp, vp)


def _pack_small(gains4, w_pool, pool_scale, rel_bias_t, sinks, loss):
    bias_rows = jnp.pad(rel_bias_t, ((0, 0), (0, 128 - rel_bias_t.shape[1])))
    parts = list(gains4) + [w_pool, pool_scale, bias_rows, sinks, loss]
    rows = []
    for a in parts:
        flat = a.reshape(-1)
        n = flat.shape[0]
        padded = -(-n // 1024) * 1024
        rows.append(jnp.pad(flat, (0, padded - n)).reshape(-1, 128))
    return jnp.concatenate(rows, axis=0)


def kernel(x, g_pre_mix, w_in, w_pool, pool_scale, rel_bias, sinks, w_out, g_post_mix, g_pre_ffn, w_gate, w_up, w_down, g_post_ffn, loss_target, m_g_pre_mix, m_w_in, m_w_pool, m_pool_scale, m_rel_bias, m_sinks, m_w_out, m_g_post_mix, m_g_pre_ffn, m_w_gate, m_w_up, m_w_down, m_g_post_ffn, v_g_pre_mix, v_w_in, v_w_pool, v_pool_scale, v_rel_bias, v_sinks, v_w_out, v_g_post_mix, v_g_pre_ffn, v_w_gate, v_w_up, v_w_down, v_g_post_ffn):
    c = lax.axis_index("c")
    chip = 2 * lax.axis_index("x") + lax.axis_index("y")

    def shards(a_in, a_out, a_gate, a_up, a_down):
        return (a_in[0].T, a_out[0], a_gate[0].T, a_up[0].T, a_down[0])

    c_arr = c.reshape(1).astype(jnp.int32)
    chip_arr = chip.reshape(1).astype(jnp.int32)

    big_w = shards(w_in, w_out, w_gate, w_up, w_down)
    g_ssem, g_rsem, _, g_lands, g_token = _split_start(
        _gather_copies, 15, [], _cast_into_slots(big_w, chip_arr), None, "gather_start")
    fw = {}

    def w_in_ready(*after):
        fw["first"] = _split_wait(_gather_copies, g_ssem, g_rsem, [], g_lands, after, "gather_win_wait",
                                  only=(0, 1, 2))
        (fw["win"],) = _gather_finish([fw["first"][0]], "win")
        return fw["win"].reshape(IN_W, D)

    def w_out_ready(*after):
        fw["second"] = _split_wait(functools.partial(_gather_copies, first=1), g_ssem, g_rsem, [],
                                   list(fw["first"][1:]), after, "gather_wout_wait", only=(0, 1, 2))
        (fw["wout"],) = _gather_finish([fw["second"][0]], "wout")
        return fw["wout"].reshape(D, D), None

    def ffn_forward_start(after):
        outs = _split_wait(functools.partial(_gather_copies, first=2), g_ssem, g_rsem, [], list(fw["second"][1:]),
                           [after], "gather_ffn_wait")
        fw["f"] = _split_start(_forward_copies, 9, [], list(outs), None, "gather_forward_start")
        return fw["f"][4]

    def ffn_weights(after):
        ssem, rsem, _, lands, _ = fw["f"]
        return _split_wait(_forward_copies, ssem, rsem, [], lands, [after], "gather_forward_wait")

    rs = {}

    def on_ffn_grads(ffn_g):
        oths = ffn_g[1::2]
        rs["ffn_own"] = ffn_g[0::2]
        rs["x"] = _split_start(_sibling_copies, 3, oths, [lax.empty(a.shape, BF16) for a in oths], ffn_g[0],
                               "rs_exchange_ffn_start")
        return rs["x"][4]

    def on_wout_grads(own, oth, after):
        ssem, rsem, srcs, lands, _ = rs["x"]
        recv_ffn = _split_wait(_sibling_copies, ssem, rsem, srcs, lands, [after], "rs_exchange_ffn_wait")[3:]
        recv_wout = _to_sibling([oth], "rs_exchange_wout")
        outs = _chip_partial([own] + list(rs["ffn_own"]), list(recv_wout) + list(recv_ffn), chip_arr, "early")
        rs["early_own"] = outs[4:]
        rs["s"] = _split_start(_scatter_copies, 12, outs[:4],
                               [lax.empty((3,) + a.shape[1:], BF16) for a in outs[:4]], outs[4], "scatter_early_start")
        return rs["s"][4]

    small = (g_pre_mix, g_post_mix, g_pre_ffn, g_post_ffn, w_pool[0], pool_scale, rel_bias, sinks)
    loss, gx, small_grads, ((win_own, win_oth), _, _) = _local_step(
        x[0], loss_target[0], small, w_in_ready, w_out_ready, ffn_weights, c_arr, on_ffn_grads, on_wout_grads,
        ffn_forward_start, g_token)

    ssem, rsem, srcs, lands, _ = rs["s"]
    recv_early = _split_wait(_scatter_copies, ssem, rsem, srcs, lands, [gx], "scatter_early_wait")[4:]
    finals = _sum_chips(list(rs["early_own"]), list(recv_early), "early")
    to_sib = [win_oth] + list(finals)
    sh_ssem, sh_rsem, sh_srcs, sh_lands, _ = _split_start(
        _sibling_copies, 5, to_sib, [lax.empty(a.shape, a.dtype) for a in to_sib], None, "rs_share_start")

    unused = jnp.zeros((1, 1), F32)
    gp = _pack_small(small_grads[:4], *small_grads[4:], loss)
    wp = _pack_small((g_pre_mix, g_post_mix, g_pre_ffn, g_post_ffn), w_pool, pool_scale, rel_bias.T, sinks, unused)
    mp = _pack_small((m_g_pre_mix, m_g_post_mix, m_g_pre_ffn, m_g_post_ffn), m_w_pool, m_pool_scale, m_rel_bias.T,
                     m_sinks, unused)
    vp = _pack_small((v_g_pre_mix, v_g_post_mix, v_g_pre_ffn, v_g_post_ffn), v_w_pool, v_pool_scale, v_rel_bias.T,
                     v_sinks, unused)

    moments = (shards(m_w_in, m_w_out, m_w_gate, m_w_up, m_w_down),
               shards(v_w_in, v_w_out, v_w_gate, v_w_up, v_w_down))
    sh_first = _split_wait(_sibling_copies, sh_ssem, sh_rsem, sh_srcs, sh_lands, [], "rs_share_win_wait", only=(0,))
    outs = _chip_partial([win_own], [sh_first[5]], chip_arr, "win", whole=True)
    slots = lax.dynamic_update_slice(lax.empty((8,) + gp.shape, F32), gp[None], (2 * chip + c, 0, 0))
    w_ssem, w_rsem, w_srcs, w_lands, w_token = _split_start(
        _win_and_small_copies, 10, [outs[0], gp], [lax.empty((3,) + outs[0].shape[1:], BF16), slots], gx,
        "scatter_win_start")
    shared = _split_wait(_sibling_copies, sh_ssem, sh_rsem, sh_first[:5], sh_first[5:], [w_token],
                         "rs_share_early_wait", only=(1, 2, 3, 4))
    adam_e = _adamw_shards(shared[1:5], shared[6:], big_w[1:], moments[0][1:], moments[1][1:], c_arr, "early")
    w_first = _split_wait(_win_and_small_copies, w_ssem, w_rsem, w_srcs, w_lands, [adam_e[0]], "scatter_win_wait",
                          only=(0, 1, 2))
    win_final, win_sib = _sum_chips_shared(outs[1], w_first[2], "win")
    adam_w = _adamw_shards([win_final], [win_sib], big_w[:1], moments[0][:1], moments[1][:1], c_arr, "win")
    big_g, big_d, big_m, big_v = [[adam_w[i]] + list(adam_e[4 * i:4 * i + 4]) for i in range(4)]

    gathered = _split_wait(_win_and_small_copies, w_ssem, w_rsem, w_first[:2], w_first[2:], [adam_w[0]],
                           "small_allgather_wait", only=tuple(range(3, 10)))[3]
    flat = _small_sum_adamw(gathered, wp, mp, vp)
    small_out = [flat[8 * kind:8 * kind + 8] for kind in range(4)]
    total_loss = flat[-1][0, 0]

    def ordered(small8, big4):
        sg1, sg2, sg3, sg4, swp, ssc, srb, ssk = small8
        swp, srb = swp[None], srb.T
        bwin, bwout, bwg, bwu, bwd = big4[0].T[None], big4[1][None], big4[2].T[None], big4[3].T[None], big4[4][None]
        return [sg1, bwin, swp, ssc, srb, ssk, bwout, sg2, sg3, bwg, bwu, bwd, sg4]

    res = [total_loss, gx[None]]
    for sm, bg in zip(small_out, (big_g, big_d, big_m, big_v)):
        res += ordered(sm, bg)
    return tuple(res)
```
